```python
import math
import jax, jax.numpy as jnp
from jax import lax
import numpy as np

D_MODEL = 1024
BATCH = 8
SEQ = 4096
DEPTH = 1

ATTN_WINDOWS = (128, 512, 2048)
ATTN_DILATIONS = (1, 4, 16)
N_GROUPS = len(ATTN_WINDOWS)
ATTN_HEAD_DIM = 64
ATTN_HEADS_PER_GROUP = D_MODEL // 128
N_ATTN_HEADS = N_GROUPS * ATTN_HEADS_PER_GROUP
ATTN_OUT_WIDTH = ATTN_HEADS_PER_GROUP * ATTN_HEAD_DIM
ATTN_QKV_WIDTH = N_GROUPS * 3 * ATTN_HEADS_PER_GROUP * ATTN_HEAD_DIM
ATTN_BLOCK = 128
NEG_INF = -1e30
NUM_BUCKETS = 32
MAX_EXACT = NUM_BUCKETS // 2
MAX_DISTANCE = 2048
HGRN_HEADS = D_MODEL // 256
HGRN_DK = 128
HGRN_DV = 128
HGRN_WIDTH = HGRN_HEADS * HGRN_DK
HGRN_CHUNK = 32
GATE_WIDTH = 2 * D_MODEL
IN_WIDTH = ATTN_QKV_WIDTH + 4 * HGRN_WIDTH + GATE_WIDTH
D_FF = ((8 * D_MODEL // 3 + 127) // 128) * 128
CONV_WIDTH = 3
NORM_EPS = 1e-6

kernel_name = "hybrid_dilated_attn_hgrn2_convffn"


def rms_norm(x, w):
    xf = x.astype(jnp.float32)
    y = xf * lax.rsqrt(jnp.mean(xf * xf, axis=-1, keepdims=True) + NORM_EPS)
    return (y * w.astype(jnp.float32)).astype(x.dtype)


def t5_bucket(dist):
    n = jnp.maximum(dist, 0)
    nf = jnp.maximum(n, 1).astype(jnp.float32)
    large = MAX_EXACT + (jnp.log(nf / MAX_EXACT) / math.log(MAX_DISTANCE / MAX_EXACT)
                         * (NUM_BUCKETS - MAX_EXACT)).astype(jnp.int32)
    large = jnp.minimum(large, NUM_BUCKETS - 1)
    return jnp.where(n < MAX_EXACT, n, large)


def dilated_window_attention(q, k, v, bias_table, dilation, window):
    Bn, S, H, hd = q.shape
    blk = ATTN_BLOCK
    span = dilation * blk
    Sp = -(-S // span) * span
    pad = ((0, 0), (0, Sp - S), (0, 0), (0, 0))
    q, k, v = jnp.pad(q, pad), jnp.pad(k, pad), jnp.pad(v, pad)
    U = Sp // dilation
    NB = U // blk

    def to_sub(t):
        return t.reshape(Bn, U, dilation, H, hd).transpose(0, 2, 3, 1, 4)

    def band(t):
        ts = to_sub(t)
        prev = jnp.pad(ts, ((0, 0), (0, 0), (0, 0), (blk, 0), (0, 0)))[:, :, :, :U]
        return jnp.concatenate([prev.reshape(Bn, dilation, H, NB, blk, hd),
                                ts.reshape(Bn, dilation, H, NB, blk, hd)], axis=-2)

    qs = to_sub(q).reshape(Bn, dilation, H, NB, blk, hd)
    kb, vb = band(k), band(v)
    s = jnp.einsum('bdhnqe,bdhnke->bdhnqk', qs, kb).astype(jnp.float32) * (hd ** -0.5)

    rel = jnp.arange(blk)[:, None] + blk - jnp.arange(2 * blk)[None, :]
    in_win = (rel >= 0) & (rel <= window // dilation)
    key_ok = (jnp.arange(NB)[:, None] * blk - blk + jnp.arange(2 * blk)[None, :]) >= 0
    mask = in_win[None] & key_ok[:, None, :]
    bias = bias_table[t5_bucket(rel * dilation)].astype(jnp.float32)
    s = s + bias.transpose(2, 0, 1)[:, None]
    s = jnp.where(mask, s, NEG_INF)

    m = jnp.max(s, axis=-1, keepdims=True)
    p = jnp.exp(s - m)
    l = jnp.sum(p, axis=-1, keepdims=True)
    o = jnp.einsum('bdhnqk,bdhnke->bdhnqe', p.astype(vb.dtype), vb).astype(jnp.float32) / l
    lse = (m + jnp.log(l))[..., 0]

    o = o.reshape(Bn, dilation, H, U, hd).transpose(0, 3, 1, 2, 4).reshape(Bn, Sp, H, hd)[:, :S]
    lse = lse.reshape(Bn, dilation, H, U).transpose(0, 3, 1, 2).reshape(Bn, Sp, H)[:, :S]
    return o, lse


def attention_branch(attn_cols, rel_bias):
    Bn, S, _ = attn_cols.shape
    qkv = attn_cols.reshape(Bn, S, N_GROUPS, 3, ATTN_HEADS_PER_GROUP, ATTN_HEAD_DIM)
    outs, lses = [], []
    for g in range(N_GROUPS):
        bias_g = rel_bias[:, g * ATTN_HEADS_PER_GROUP:(g + 1) * ATTN_HEADS_PER_GROUP]
        o, lse = dilated_window_attention(qkv[:, :, g, 0], qkv[:, :, g, 1], qkv[:, :, g, 2],
                                          bias_g, ATTN_DILATIONS[g], ATTN_WINDOWS[g])
        outs.append(o)
        lses.append(lse)
    w = jax.nn.softmax(jnp.stack(lses, axis=0), axis=0)[..., None]
    y = jnp.sum(w * jnp.stack(outs, axis=0), axis=0)
    return y.reshape(Bn, S, ATTN_OUT_WIDTH).astype(attn_cols.dtype)


def hgrn2_chunked(q, k, g, v):
    Bn, H, S, dk = q.shape
    dv = v.shape[-1]
    C = HGRN_CHUNK
    NC = S // C
    q, k, g = (t.reshape(Bn, H, NC, C, dk) for t in (q, k, g))
    v = v.reshape(Bn, H, NC, C, dv)
    G = jnp.cumsum(g, axis=3)
    G_last = G[:, :, :, -1:, :]
    q_t = q * jnp.exp(G)
    k_t = k * jnp.exp(-G)
    k_dec = k * jnp.exp(G_last - G)
    tril = jnp.tril(jnp.ones((C, C), dtype=bool))
    A = jnp.where(tril, jnp.einsum('bhnik,bhnjk->bhnij', q_t, k_t), 0.0)
    o_intra = jnp.einsum('bhnij,bhnjv->bhniv', A, v)

    def step(state, xs):
        qc, kc, glc, vc = xs
        o = jnp.einsum('bhik,bhkv->bhiv', qc, state)
        state = jnp.exp(glc)[:, :, 0, :, None] * state + jnp.einsum('bhjk,bhjv->bhkv', kc, vc)
        return state, o

    xs = tuple(jnp.moveaxis(t, 2, 0) for t in (q_t, k_dec, G_last, v))
    state0 = jnp.zeros((Bn, H, dk, dv), jnp.float32)
    _, o_inter = lax.scan(step, state0, xs)
    o = o_intra + jnp.moveaxis(o_inter, 0, 2)
    return o.reshape(Bn, H, S, dv)


def hgrn2_branch(hg_cols, lb, norm_w):
    Bn, S, _ = hg_cols.shape
    q, f, i, og = jnp.split(hg_cols, 4, axis=-1)

    def heads(t):
        return t.reshape(Bn, S, HGRN_HEADS, -1).transpose(0, 2, 1, 3).astype(jnp.float32)

    lb_h = lb.astype(jnp.float32).reshape(HGRN_HEADS, 1, HGRN_DK)
    q = jax.nn.silu(heads(q))
    f = lb_h + (1.0 - lb_h) * jax.nn.sigmoid(heads(f))
    o = hgrn2_chunked(q, 1.0 - f, jnp.log(f), heads(i))
    o = o * lax.rsqrt(jnp.mean(o * o, axis=-1, keepdims=True) + NORM_EPS) * norm_w.astype(jnp.float32)
    o = o.transpose(0, 2, 1, 3).reshape(Bn, S, HGRN_WIDTH)
    return (o * jax.nn.silu(og.astype(jnp.float32))).astype(hg_cols.dtype)


def conv_ffn(h, w_up, conv_w, conv_b, w_down):
    S = h.shape[1]
    u = h @ w_up
    up = jnp.pad(u, ((0, 0), (CONV_WIDTH - 1, 0), (0, 0)))
    c = conv_b
    for j in range(CONV_WIDTH):
        c = c + conv_w[j] * up[:, j:j + S]
    gate, val = jnp.split(c, 2, axis=-1)
    return (jax.nn.gelu(gate, approximate=False) * val) @ w_down


def _fwd_setup_inputs(seed: int = 0) -> dict:
    key = jax.random.key(seed)
    ks = jax.random.split(key, 20)
    f32 = jnp.float32

    def nrm(k, shape, scale):
        return jax.random.normal(k, shape, f32) * scale

    def gain(k, shape):
        return 1.0 + 0.05 * jax.random.normal(k, shape, f32)

    return {
        "x": nrm(ks[0], (BATCH, SEQ, D_MODEL), 1.0),
        "pre_mix_norm": gain(ks[1], (DEPTH, D_MODEL)),
        "w_in": nrm(ks[2], (DEPTH, D_MODEL, IN_WIDTH), D_MODEL ** -0.5),
        "rel_bias": nrm(ks[3], (NUM_BUCKETS, N_ATTN_HEADS), 0.5),
        "hgrn_lb_raw": nrm(ks[4], (DEPTH + 1, HGRN_WIDTH), 0.1),
        "hgrn_norm": gain(ks[5], (DEPTH, HGRN_DV)),
        "w_branch_attn": nrm(ks[6], (DEPTH, ATTN_OUT_WIDTH, D_MODEL), ATTN_OUT_WIDTH ** -0.5),
        "w_branch_hgrn": nrm(ks[7], (DEPTH, HGRN_WIDTH, D_MODEL), HGRN_WIDTH ** -0.5),
        "w_out": nrm(ks[8], (DEPTH, D_MODEL, D_MODEL), D_MODEL ** -0.5),
        "post_mix_norm": gain(ks[9], (DEPTH, D_MODEL)),
        "pre_ffn_norm": gain(ks[10], (DEPTH, D_MODEL)),
        "w_up": nrm(ks[11], (DEPTH, D_MODEL, 2 * D_FF), D_MODEL ** -0.5),
        "conv_w": nrm(ks[12], (DEPTH, CONV_WIDTH, 2 * D_FF), CONV_WIDTH ** -0.5),
        "conv_b": nrm(ks[13], (DEPTH, 2 * D_FF), 0.02),
        "w_down": nrm(ks[14], (DEPTH, D_FF, D_MODEL), D_FF ** -0.5),
        "post_ffn_norm": gain(ks[15], (DEPTH, D_MODEL)),
    }


def _fwd_reference(x, pre_mix_norm, w_in, rel_bias, hgrn_lb_raw, hgrn_norm, w_branch_attn,
              w_branch_hgrn, w_out, post_mix_norm, pre_ffn_norm, w_up, conv_w, conv_b,
              w_down, post_ffn_norm):
    Bn, S, _ = x.shape
    lbs = jnp.cumsum(jax.nn.softmax(hgrn_lb_raw.astype(jnp.float32), axis=0), axis=0)
    split_at = [ATTN_QKV_WIDTH, ATTN_QKV_WIDTH + 4 * HGRN_WIDTH]
    for l in range(DEPTH):
        h = rms_norm(x, pre_mix_norm[l])
        proj = h @ w_in[l]
        attn_cols, hg_cols, gate_cols = jnp.split(proj, split_at, axis=-1)
        y_attn = attention_branch(attn_cols, rel_bias)
        y_hgrn = hgrn2_branch(hg_cols, lbs[l], hgrn_norm[l])
        gates = jax.nn.sigmoid(gate_cols.astype(jnp.float32)).reshape(Bn, S, 2, D_MODEL)
        merged = (gates[:, :, 0] * (y_attn @ w_branch_attn[l])
                  + gates[:, :, 1] * (y_hgrn @ w_branch_hgrn[l])).astype(x.dtype)
        x = x + rms_norm(merged @ w_out[l], post_mix_norm[l])
        h = rms_norm(x, pre_ffn_norm[l])
        x = x + rms_norm(conv_ffn(h, w_up[l], conv_w[l], conv_b[l], w_down[l]), post_ffn_norm[l])
    return x


import jax as _jax
import jax.numpy as _jnp

TWIN_FORMAT = 'train_step'
FWD_PARAMS = ['x', 'pre_mix_norm', 'w_in', 'rel_bias', 'hgrn_lb_raw', 'hgrn_norm', 'w_branch_attn', 'w_branch_hgrn', 'w_out', 'post_mix_norm', 'pre_ffn_norm', 'w_up', 'conv_w', 'conv_b', 'w_down', 'post_ffn_norm']
TWIN_WEIGHTS = ['pre_mix_norm', 'w_in', 'rel_bias', 'hgrn_lb_raw', 'hgrn_norm', 'w_branch_attn', 'w_branch_hgrn', 'w_out', 'post_mix_norm', 'pre_ffn_norm', 'w_up', 'conv_w', 'conv_b', 'w_down', 'post_ffn_norm']
TWIN_DIFF_INPUT = 'x'
TWIN_INPUTS = ['x', 'pre_mix_norm', 'w_in', 'rel_bias', 'hgrn_lb_raw', 'hgrn_norm', 'w_branch_attn', 'w_branch_hgrn', 'w_out', 'post_mix_norm', 'pre_ffn_norm', 'w_up', 'conv_w', 'conv_b', 'w_down', 'post_ffn_norm', 'loss_target', 'm_pre_mix_norm', 'm_w_in', 'm_rel_bias', 'm_hgrn_lb_raw', 'm_hgrn_norm', 'm_w_branch_attn', 'm_w_branch_hgrn', 'm_w_out', 'm_post_mix_norm', 'm_pre_ffn_norm', 'm_w_up', 'm_conv_w', 'm_conv_b', 'm_w_down', 'm_post_ffn_norm', 'v_pre_mix_norm', 'v_w_in', 'v_rel_bias', 'v_hgrn_lb_raw', 'v_hgrn_norm', 'v_w_branch_attn', 'v_w_branch_hgrn', 'v_w_out', 'v_post_mix_norm', 'v_pre_ffn_norm', 'v_w_up', 'v_conv_w', 'v_conv_b', 'v_w_down', 'v_post_ffn_norm']
TWIN_OUTPUTS = ['loss', 'grad_x', 'grad_pre_mix_norm', 'grad_w_in', 'grad_rel_bias', 'grad_hgrn_lb_raw', 'grad_hgrn_norm', 'grad_w_branch_attn', 'grad_w_branch_hgrn', 'grad_w_out', 'grad_post_mix_norm', 'grad_pre_ffn_norm', 'grad_w_up', 'grad_conv_w', 'grad_conv_b', 'grad_w_down', 'grad_post_ffn_norm', 'delta_pre_mix_norm', 'delta_w_in', 'delta_rel_bias', 'delta_hgrn_lb_raw', 'delta_hgrn_norm', 'delta_w_branch_attn', 'delta_w_branch_hgrn', 'delta_w_out', 'delta_post_mix_norm', 'delta_pre_ffn_norm', 'delta_w_up', 'delta_conv_w', 'delta_conv_b', 'delta_w_down', 'delta_post_ffn_norm', 'new_m_pre_mix_norm', 'new_m_w_in', 'new_m_rel_bias', 'new_m_hgrn_lb_raw', 'new_m_hgrn_norm', 'new_m_w_branch_attn', 'new_m_w_branch_hgrn', 'new_m_w_out', 'new_m_post_mix_norm', 'new_m_pre_ffn_norm', 'new_m_w_up', 'new_m_conv_w', 'new_m_conv_b', 'new_m_w_down', 'new_m_post_ffn_norm', 'new_v_pre_mix_norm', 'new_v_w_in', 'new_v_rel_bias', 'new_v_hgrn_lb_raw', 'new_v_hgrn_norm', 'new_v_w_branch_attn', 'new_v_w_branch_hgrn', 'new_v_w_out', 'new_v_post_mix_norm', 'new_v_pre_ffn_norm', 'new_v_w_up', 'new_v_conv_w', 'new_v_conv_b', 'new_v_w_down', 'new_v_post_ffn_norm']
TWIN_LEAF_KINDS = {'loss': 'loss', 'grad_x': 'grad_x', 'grad_pre_mix_norm': 'grad_w', 'grad_w_in': 'grad_w', 'grad_rel_bias': 'grad_w', 'grad_hgrn_lb_raw': 'grad_w', 'grad_hgrn_norm': 'grad_w', 'grad_w_branch_attn': 'grad_w', 'grad_w_branch_hgrn': 'grad_w', 'grad_w_out': 'grad_w', 'grad_post_mix_norm': 'grad_w', 'grad_pre_ffn_norm': 'grad_w', 'grad_w_up': 'grad_w', 'grad_conv_w': 'grad_w', 'grad_conv_b': 'grad_w', 'grad_w_down': 'grad_w', 'grad_post_ffn_norm': 'grad_w', 'delta_pre_mix_norm': 'delta_w', 'delta_w_in': 'delta_w', 'delta_rel_bias': 'delta_w', 'delta_hgrn_lb_raw': 'delta_w', 'delta_hgrn_norm': 'delta_w', 'delta_w_branch_attn': 'delta_w', 'delta_w_branch_hgrn': 'delta_w', 'delta_w_out': 'delta_w', 'delta_post_mix_norm': 'delta_w', 'delta_pre_ffn_norm': 'delta_w', 'delta_w_up': 'delta_w', 'delta_conv_w': 'delta_w', 'delta_conv_b': 'delta_w', 'delta_w_down': 'delta_w', 'delta_post_ffn_norm': 'delta_w', 'new_m_pre_mix_norm': 'new_m', 'new_m_w_in': 'new_m', 'new_m_rel_bias': 'new_m', 'new_m_hgrn_lb_raw': 'new_m', 'new_m_hgrn_norm': 'new_m', 'new_m_w_branch_attn': 'new_m', 'new_m_w_branch_hgrn': 'new_m', 'new_m_w_out': 'new_m', 'new_m_post_mix_norm': 'new_m', 'new_m_pre_ffn_norm': 'new_m', 'new_m_w_up': 'new_m', 'new_m_conv_w': 'new_m', 'new_m_conv_b': 'new_m', 'new_m_w_down': 'new_m', 'new_m_post_ffn_norm': 'new_m', 'new_v_pre_mix_norm': 'new_v', 'new_v_w_in': 'new_v', 'new_v_rel_bias': 'new_v', 'new_v_hgrn_lb_raw': 'new_v', 'new_v_hgrn_norm': 'new_v', 'new_v_w_branch_attn': 'new_v', 'new_v_w_branch_hgrn': 'new_v', 'new_v_w_out': 'new_v', 'new_v_post_mix_norm': 'new_v', 'new_v_pre_ffn_norm': 'new_v', 'new_v_w_up': 'new_v', 'new_v_conv_w': 'new_v', 'new_v_conv_b': 'new_v', 'new_v_w_down': 'new_v', 'new_v_post_ffn_norm': 'new_v'}


def _forward(args):
    return _fwd_reference(*[args[k] for k in FWD_PARAMS])


def _output_shape():
    out = _jax.eval_shape(lambda: _forward(_fwd_setup_inputs(0)))
    return out.shape, out.dtype

N_MICROBATCH = 1
ADAM_LR = 0.001
ADAM_B1 = 0.9
ADAM_B2 = 0.999
ADAM_EPS = 1e-08
ADAM_WD = 0.01
ADAM_STEP = 10
PER_EXAMPLE_BATCH_AXIS = {'x': 0, 'loss_target': 0}
SHARED_INPUTS = []
_WEIGHT_DTYPES = {'pre_mix_norm': _jnp.float32, 'w_in': _jnp.float32, 'rel_bias': _jnp.float32, 'hgrn_lb_raw': _jnp.float32, 'hgrn_norm': _jnp.float32, 'w_branch_attn': _jnp.float32, 'w_branch_hgrn': _jnp.float32, 'w_out': _jnp.float32, 'post_mix_norm': _jnp.float32, 'pre_ffn_norm': _jnp.float32, 'w_up': _jnp.float32, 'conv_w': _jnp.float32, 'conv_b': _jnp.float32, 'w_down': _jnp.float32, 'post_ffn_norm': _jnp.float32}
MOMENT_SCALE = {'pre_mix_norm': 6.884565e-01, 'w_in': 2.307790e-01, 'rel_bias': 1.261086e-01, 'hgrn_lb_raw': 5.469111e-02, 'hgrn_norm': 1.557726e+00, 'w_branch_attn': 1.363543e-01, 'w_branch_hgrn': 4.977456e-01, 'w_out': 5.127273e-01, 'post_mix_norm': 3.212569e+01, 'pre_ffn_norm': 5.411607e-01, 'w_up': 2.188680e-01, 'conv_w': 2.326930e-01, 'conv_b': 3.630540e-01, 'w_down': 4.099037e-01, 'post_ffn_norm': 3.198956e+01}


def _to_microbatches(a, axis):
    t = _jnp.moveaxis(a, axis, 0)
    t = t.reshape((N_MICROBATCH, t.shape[0] // N_MICROBATCH) + t.shape[1:])
    return _jnp.moveaxis(t, 1, axis + 1)


def setup_inputs(seed: int = 0) -> dict:
    inp = _fwd_setup_inputs(seed)
    key = _jax.random.fold_in(_jax.random.key(seed), 7919)
    shape, _ = _output_shape()
    out = dict(inp)
    out["loss_target"] = _jax.random.normal(_jax.random.fold_in(key, 0), shape, _jnp.float32)
    for i, name in enumerate(TWIN_WEIGHTS):
        w = inp[name].astype(_jnp.float32)
        if MOMENT_SCALE is None:
            s = _jnp.sqrt(_jnp.mean(_jnp.square(w)) + 1e-30)
        else:
            s = MOMENT_SCALE[name]
        km, kv = _jax.random.split(_jax.random.fold_in(key, i + 1))
        out[name] = w
        out["m_" + name] = s * _jax.random.normal(km, w.shape, _jnp.float32)
        out["v_" + name] = (s * s) * _jax.random.uniform(kv, w.shape, _jnp.float32, 0.5, 1.5)
    if N_MICROBATCH > 1:
        for name, axis in PER_EXAMPLE_BATCH_AXIS.items():
            out[name] = _to_microbatches(out[name], axis)
    return {'x': out['x'], 'pre_mix_norm': out['pre_mix_norm'], 'w_in': out['w_in'], 'rel_bias': out['rel_bias'], 'hgrn_lb_raw': out['hgrn_lb_raw'], 'hgrn_norm': out['hgrn_norm'], 'w_branch_attn': out['w_branch_attn'], 'w_branch_hgrn': out['w_branch_hgrn'], 'w_out': out['w_out'], 'post_mix_norm': out['post_mix_norm'], 'pre_ffn_norm': out['pre_ffn_norm'], 'w_up': out['w_up'], 'conv_w': out['conv_w'], 'conv_b': out['conv_b'], 'w_down': out['w_down'], 'post_ffn_norm': out['post_ffn_norm'], 'loss_target': out['loss_target'], 'm_pre_mix_norm': out['m_pre_mix_norm'], 'm_w_in': out['m_w_in'], 'm_rel_bias': out['m_rel_bias'], 'm_hgrn_lb_raw': out['m_hgrn_lb_raw'], 'm_hgrn_norm': out['m_hgrn_norm'], 'm_w_branch_attn': out['m_w_branch_attn'], 'm_w_branch_hgrn': out['m_w_branch_hgrn'], 'm_w_out': out['m_w_out'], 'm_post_mix_norm': out['m_post_mix_norm'], 'm_pre_ffn_norm': out['m_pre_ffn_norm'], 'm_w_up': out['m_w_up'], 'm_conv_w': out['m_conv_w'], 'm_conv_b': out['m_conv_b'], 'm_w_down': out['m_w_down'], 'm_post_ffn_norm': out['m_post_ffn_norm'], 'v_pre_mix_norm': out['v_pre_mix_norm'], 'v_w_in': out['v_w_in'], 'v_rel_bias': out['v_rel_bias'], 'v_hgrn_lb_raw': out['v_hgrn_lb_raw'], 'v_hgrn_norm': out['v_hgrn_norm'], 'v_w_branch_attn': out['v_w_branch_attn'], 'v_w_branch_hgrn': out['v_w_branch_hgrn'], 'v_w_out': out['v_w_out'], 'v_post_mix_norm': out['v_post_mix_norm'], 'v_pre_ffn_norm': out['v_pre_ffn_norm'], 'v_w_up': out['v_w_up'], 'v_conv_w': out['v_conv_w'], 'v_conv_b': out['v_conv_b'], 'v_w_down': out['v_w_down'], 'v_post_ffn_norm': out['v_post_ffn_norm']}


def _loss(weights, diff, rest, loss_target):
    with _jax.named_scope("forward"):
        args = {**rest, TWIN_DIFF_INPUT: diff, **{k: w.astype(_WEIGHT_DTYPES[k]) for k, w in weights.items()}}
        y = _forward(args)
    with _jax.named_scope("loss_head"):
        err = _jnp.square(y.astype(_jnp.float32) - loss_target)
        return 0.5 * _jnp.sum(_jnp.mean(err, axis=-1)) if err.ndim else 0.5 * err


def _adamw(w, g, m, v):
    m = ADAM_B1 * m + (1.0 - ADAM_B1) * g
    v = ADAM_B2 * v + (1.0 - ADAM_B2) * _jnp.square(g)
    m_hat = m / (1.0 - ADAM_B1 ** ADAM_STEP)
    v_hat = v / (1.0 - ADAM_B2 ** ADAM_STEP)
    delta = -ADAM_LR * (m_hat / (_jnp.sqrt(v_hat) + ADAM_EPS) + ADAM_WD * w)
    return delta, m, v


def reference(x, pre_mix_norm, w_in, rel_bias, hgrn_lb_raw, hgrn_norm, w_branch_attn, w_branch_hgrn, w_out, post_mix_norm, pre_ffn_norm, w_up, conv_w, conv_b, w_down, post_ffn_norm, loss_target, m_pre_mix_norm, m_w_in, m_rel_bias, m_hgrn_lb_raw, m_hgrn_norm, m_w_branch_attn, m_w_branch_hgrn, m_w_out, m_post_mix_norm, m_pre_ffn_norm, m_w_up, m_conv_w, m_conv_b, m_w_down, m_post_ffn_norm, v_pre_mix_norm, v_w_in, v_rel_bias, v_hgrn_lb_raw, v_hgrn_norm, v_w_branch_attn, v_w_branch_hgrn, v_w_out, v_post_mix_norm, v_pre_ffn_norm, v_w_up, v_conv_w, v_conv_b, v_w_down, v_post_ffn_norm):
    given = dict(x=x, pre_mix_norm=pre_mix_norm, w_in=w_in, rel_bias=rel_bias, hgrn_lb_raw=hgrn_lb_raw, hgrn_norm=hgrn_norm, w_branch_attn=w_branch_attn, w_branch_hgrn=w_branch_hgrn, w_out=w_out, post_mix_norm=post_mix_norm, pre_ffn_norm=pre_ffn_norm, w_up=w_up, conv_w=conv_w, conv_b=conv_b, w_down=w_down, post_ffn_norm=post_ffn_norm, loss_target=loss_target, m_pre_mix_norm=m_pre_mix_norm, m_w_in=m_w_in, m_rel_bias=m_rel_bias, m_hgrn_lb_raw=m_hgrn_lb_raw, m_hgrn_norm=m_hgrn_norm, m_w_branch_attn=m_w_branch_attn, m_w_branch_hgrn=m_w_branch_hgrn, m_w_out=m_w_out, m_post_mix_norm=m_post_mix_norm, m_pre_ffn_norm=m_pre_ffn_norm, m_w_up=m_w_up, m_conv_w=m_conv_w, m_conv_b=m_conv_b, m_w_down=m_w_down, m_post_ffn_norm=m_post_ffn_norm, v_pre_mix_norm=v_pre_mix_norm, v_w_in=v_w_in, v_rel_bias=v_rel_bias, v_hgrn_lb_raw=v_hgrn_lb_raw, v_hgrn_norm=v_hgrn_norm, v_w_branch_attn=v_w_branch_attn, v_w_branch_hgrn=v_w_branch_hgrn, v_w_out=v_w_out, v_post_mix_norm=v_post_mix_norm, v_pre_ffn_norm=v_pre_ffn_norm, v_w_up=v_w_up, v_conv_w=v_conv_w, v_conv_b=v_conv_b, v_w_down=v_w_down, v_post_ffn_norm=v_post_ffn_norm)
    weights = {n: given[n] for n in TWIN_WEIGHTS}
    shared = {n: given[n] for n in SHARED_INPUTS}
    per_example = {n: given[n] for n in ['x']}
    grad_fn = _jax.value_and_grad(_loss, argnums=(0, 1))

    def one_microbatch(ex, loss_target):
        ex = dict(ex)
        diff = ex.pop(TWIN_DIFF_INPUT)
        return grad_fn(weights, diff, {**shared, **ex}, loss_target)

    if N_MICROBATCH == 1:
        loss, (grad_w, grad_x) = one_microbatch(per_example, given["loss_target"])
    else:
        def body(carry, xs):
            loss_sum, grad_sum = carry
            l_k, (gw_k, gx_k) = one_microbatch(xs[0], xs[1])
            with _jax.named_scope("update"):
                return (loss_sum + l_k, _jax.tree.map(_jnp.add, grad_sum, gw_k)), gx_k

        init = (_jnp.zeros((), _jnp.float32), _jax.tree.map(_jnp.zeros_like, weights))
        (loss, grad_w), grad_x = _jax.lax.scan(body, init, (per_example, given["loss_target"]))
    with _jax.named_scope("update"):
        delta_w, new_m, new_v = {}, {}, {}
        for n in TWIN_WEIGHTS:
            delta_w[n], new_m[n], new_v[n] = _adamw(weights[n], grad_w[n], given["m_" + n], given["v_" + n])
    return (loss, grad_x, *[grad_w[n] for n in TWIN_WEIGHTS], *[delta_w[n] for n in TWIN_WEIGHTS],
            *[new_m[n] for n in TWIN_WEIGHTS], *[new_v[n] for n in TWIN_WEIGHTS])
```

```python
import functools
import math

import jax
import jax.numpy as jnp
from jax import lax
from jax.experimental import pallas as pl
from jax.experimental.pallas import tpu as pltpu

F32 = jnp.float32
BF16 = jnp.bfloat16
MESH = pl.DeviceIdType.MESH

D_MODEL = 1024
N_GROUPS = 3
DILATIONS = (1, 4, 16)
HEADS = 8
HEAD_DIM = 64
GROUP_W = HEADS * HEAD_DIM
QKV_W = N_GROUPS * 3 * GROUP_W
BLK = 128
NEG_INF = -1e30
NUM_BUCKETS = 32
MAX_EXACT = 16
MAX_DISTANCE = 2048
HG_HEADS = 4
HG_DK = 128
HG_W = HG_HEADS * HG_DK
HG_CHUNK = 32
HG_TILE = 256
IN_W = QKV_W + 4 * HG_W + 2 * D_MODEL
D_FF = 2816
EPS = 1e-6
N_CHIPS = 4
N_DEV = 8
LANES = 128

ADAM_LR, ADAM_B1, ADAM_B2, ADAM_EPS, ADAM_WD, ADAM_STEP = 0.001, 0.9, 0.999, 1e-08, 0.01, 10

VMEM_LIMIT = 56 * 1024 * 1024


def _cp(n_axes):
    return pltpu.CompilerParams(dimension_semantics=("arbitrary",) * n_axes, vmem_limit_bytes=VMEM_LIMIT)


def _sds(shape, dtype):
    return jax.ShapeDtypeStruct(tuple(shape), dtype)


def _sigmoid(v):
    return 1.0 / (1.0 + jnp.exp(-v))


def _bf(v):
    return v.astype(BF16)


def _dot(a, b, dims):
    return lax.dot_general(a, b, (dims, ((), ())), preferred_element_type=F32)


NN = ((1,), (0,))
NT = ((1,), (1,))
TN = ((0,), (0,))


def _mm_nn_blk(a, wg, name, tm=512):
    M, K = a.shape
    nb, _, Nb = wg.shape

    def body(a_ref, w_ref, o_ref):
        o_ref[...] = _dot(_bf(a_ref[...]), w_ref[...], NN)

    return pl.pallas_call(
        body, name=name, grid=(nb, M // tm),
        in_specs=[pl.BlockSpec((tm, K), lambda j, i: (i, 0)), pl.BlockSpec((None, K, Nb), lambda j, i: (j, 0, 0))],
        out_specs=pl.BlockSpec((tm, Nb), lambda j, i: (i, j)),
        out_shape=_sds((M, nb * Nb), F32), compiler_params=_cp(2))(a, wg)


def _mm_nt_blk(dy, wg, name, tm=1024):
    M = dy.shape[0]
    nb, K, Nb = wg.shape

    def body(dy_ref, w_ref, o_ref):
        j = pl.program_id(1)
        r = _dot(_bf(dy_ref[...]), w_ref[...], NT)

        @pl.when(j == 0)
        def _():
            o_ref[...] = r

        @pl.when(j > 0)
        def _():
            o_ref[...] += r

    return pl.pallas_call(
        body, name=name, grid=(M // tm, nb),
        in_specs=[pl.BlockSpec((tm, Nb), lambda i, j: (i, j)), pl.BlockSpec((None, K, Nb), lambda i, j: (j, 0, 0))],
        out_specs=pl.BlockSpec((tm, K), lambda i, j: (i, 0)),
        out_shape=_sds((M, K), F32), compiler_params=_cp(2))(dy, wg)


def _mm_tn_blk(x, dy, nb, name, tk=512):
    T, Mx = x.shape
    Nb = dy.shape[1] // nb

    def body(x_ref, dy_ref, o_ref):
        t = pl.program_id(1)
        r = _dot(_bf(x_ref[...]), _bf(dy_ref[...]), TN)

        @pl.when(t == 0)
        def _():
            o_ref[...] = r

        @pl.when(t > 0)
        def _():
            o_ref[...] += r

    return pl.pallas_call(
        body, name=name, grid=(nb, T // tk),
        in_specs=[pl.BlockSpec((tk, Mx), lambda j, t: (t, 0)), pl.BlockSpec((tk, Nb), lambda j, t: (t, j))],
        out_specs=pl.BlockSpec((None, Mx, Nb), lambda j, t: (j, 0, 0)),
        out_shape=_sds((nb, Mx, Nb), F32), compiler_params=_cp(2))(x, dy)


def _mm_nn(a, w, name, tm=512):
    M, K = a.shape
    N = w.shape[1]

    def body(a_ref, w_ref, o_ref):
        o_ref[...] = _dot(_bf(a_ref[...]), w_ref[...], NN)

    return pl.pallas_call(
        body, name=name, grid=(M // tm,),
        in_specs=[pl.BlockSpec((tm, K), lambda i: (i, 0)), pl.BlockSpec((K, N), lambda i: (0, 0))],
        out_specs=pl.BlockSpec((tm, N), lambda i: (i, 0)),
        out_shape=_sds((M, N), F32), compiler_params=_cp(1))(a, w)


def _mm_nt(dy, w, name, tm=512):
    M, N = dy.shape
    K = w.shape[0]

    def body(dy_ref, w_ref, o_ref):
        o_ref[...] = _dot(_bf(dy_ref[...]), w_ref[...], NT)

    return pl.pallas_call(
        body, name=name, grid=(M // tm,),
        in_specs=[pl.BlockSpec((tm, N), lambda i: (i, 0)), pl.BlockSpec((K, N), lambda i: (0, 0))],
        out_specs=pl.BlockSpec((tm, K), lambda i: (i, 0)),
        out_shape=_sds((M, K), F32), compiler_params=_cp(1))(dy, w)


def _mm_tn(x, dy, name, tk=512):
    T, Mx = x.shape
    N = dy.shape[1]

    def body(x_ref, dy_ref, o_ref):
        t = pl.program_id(0)
        r = _dot(_bf(x_ref[...]), _bf(dy_ref[...]), TN)

        @pl.when(t == 0)
        def _():
            o_ref[...] = r

        @pl.when(t > 0)
        def _():
            o_ref[...] += r

    return pl.pallas_call(
        body, name=name, grid=(T // tk,),
        in_specs=[pl.BlockSpec((tk, Mx), lambda t: (t, 0)), pl.BlockSpec((tk, N), lambda t: (t, 0))],
        out_specs=pl.BlockSpec((Mx, N), lambda t: (0, 0)),
        out_shape=_sds((Mx, N), F32), compiler_params=_cp(1))(x, dy)


def _tile(arr, bw, col=lambda c: 0):
    return ("tile", arr, bw, col)


def _full(arr):
    return ("full", arr)


def _out_tile(width, dtype, bw, col=lambda c: 0):
    return ("tile", width, dtype, bw, col)


def _out_acc(rows, width, bw, col=lambda c: 0):
    return ("acc", rows, width, bw, col)


def _rows_call(name, body, n_rows, tm, ncol, ins, outs):
    in_specs, args = [], []
    for e in ins:
        if e[0] == "tile":
            _, arr, bw, col = e
            in_specs.append(pl.BlockSpec((tm, bw), functools.partial(lambda c, i, col: (i, col(c)), col=col)))
        else:
            arr = e[1]
            in_specs.append(pl.BlockSpec(arr.shape, functools.partial(lambda c, i, nd: (0,) * nd, nd=arr.ndim)))
        args.append(arr)
    out_specs, out_shape = [], []
    for e in outs:
        if e[0] == "tile":
            _, width, dtype, bw, col = e
            out_specs.append(pl.BlockSpec((tm, bw), functools.partial(lambda c, i, col: (i, col(c)), col=col)))
            out_shape.append(_sds((n_rows, width), dtype))
        else:
            _, rows, width, bw, col = e
            out_specs.append(pl.BlockSpec((rows, bw), functools.partial(lambda c, i, col: (0, col(c)), col=col)))
            out_shape.append(_sds((rows, width), F32))
    return pl.pallas_call(
        body, name=name, grid=(ncol, n_rows // tm), in_specs=in_specs, out_specs=out_specs,
        out_shape=out_shape, compiler_params=_cp(2))(*args)


def _acc(ref, val):
    i = pl.program_id(1)

    @pl.when(i == 0)
    def _():
        ref[...] = val

    @pl.when(i > 0)
    def _():
        ref[...] += val


def _rinv(z):
    return lax.rsqrt(jnp.mean(z * z, axis=-1, keepdims=True) + EPS)


def _norm_bwd(dy, zhat, r, w):
    dyw = dy * w
    return r * (dyw - zhat * jnp.mean(dyw * zhat, axis=-1, keepdims=True))


def _norm_fwd(x, w, name):
    def body(x_ref, w_ref, h_ref):
        xv = x_ref[...]
        h_ref[...] = _bf(xv * _rinv(xv) * w_ref[...])

    return _rows_call(name, body, x.shape[0], 512, 1, [_tile(x, D_MODEL), _full(w)],
                      [_out_tile(D_MODEL, BF16, D_MODEL)])[0]


def _prenorm_bwd(dh, xin, w, dres, name):
    def body(dh_ref, x_ref, w_ref, dres_ref, dx_ref, dw_ref):
        xv = x_ref[...]
        r = _rinv(xv)
        xhat = xv * r
        dhv = dh_ref[...]
        dx_ref[...] = dres_ref[...] + _norm_bwd(dhv, xhat, r, w_ref[...])
        _acc(dw_ref, jnp.sum(dhv * xhat, axis=0, keepdims=True))

    return _rows_call(name, body, xin.shape[0], 512, 1,
                      [_tile(dh, D_MODEL), _tile(xin, D_MODEL), _full(w), _tile(dres, D_MODEL)],
                      [_out_tile(D_MODEL, F32, D_MODEL), _out_acc(1, D_MODEL, D_MODEL)])


def _postnorm_bwd(dout, z, w, name):
    def body(do_ref, z_ref, w_ref, dz_ref, dw_ref):
        zv = z_ref[...]
        r = _rinv(zv)
        zhat = zv * r
        dov = do_ref[...]
        dz_ref[...] = _bf(_norm_bwd(dov, zhat, r, w_ref[...]))
        _acc(dw_ref, jnp.sum(dov * zhat, axis=0, keepdims=True))

    return _rows_call(name, body, z.shape[0], 512, 1, [_tile(dout, D_MODEL), _tile(z, D_MODEL), _full(w)],
                      [_out_tile(D_MODEL, BF16, D_MODEL), _out_acc(1, D_MODEL, D_MODEL)])


def _t5_bucket(dist):
    n = jnp.maximum(dist, 0)
    nf = jnp.maximum(n, 1).astype(F32)
    large = MAX_EXACT + (jnp.log(nf / MAX_EXACT) / math.log(MAX_DISTANCE / MAX_EXACT)
                         * (NUM_BUCKETS - MAX_EXACT)).astype(jnp.int32)
    large = jnp.minimum(large, NUM_BUCKETS - 1)
    return jnp.where(n < MAX_EXACT, n, large)


def _band_rel():
    return jnp.arange(BLK)[:, None] + BLK - jnp.arange(2 * BLK)[None, :]


def _band_mask(n):
    row = lax.broadcasted_iota(jnp.int32, (BLK, 2 * BLK), 0)
    col = lax.broadcasted_iota(jnp.int32, (BLK, 2 * BLK), 1)
    rel = row + BLK - col
    return (rel >= 0) & (rel <= BLK) & ((col >= BLK) | (n > 0))


def _attn_fwd(proj, bias, g, name):
    S = proj.shape[0]
    d = DILATIONS[g]
    U = S // d
    NB = U // BLK
    ncb = IN_W // GROUP_W
    pv = proj.reshape(U, d * IN_W)

    def body(q_ref, kp_ref, kc_ref, vp_ref, vc_ref, b_ref, o_ref, lse_ref):
        mask = _band_mask(pl.program_id(1))
        for h in range(HEADS):
            hs = slice(h * HEAD_DIM, (h + 1) * HEAD_DIM)
            q = _bf(q_ref[:, hs])
            kb = _bf(jnp.concatenate([kp_ref[:, hs], kc_ref[:, hs]], axis=0))
            vb = _bf(jnp.concatenate([vp_ref[:, hs], vc_ref[:, hs]], axis=0))
            s = _dot(q, kb, NT) * (HEAD_DIM ** -0.5) + b_ref[h]
            s = jnp.where(mask, s, NEG_INF)
            m = jnp.max(s, axis=-1, keepdims=True)
            p = jnp.exp(s - m)
            l = jnp.sum(p, axis=-1, keepdims=True)
            o_ref[:, hs] = _dot(_bf(p), vb, NN) / l
            lse_ref[:, hs] = jnp.broadcast_to(m + jnp.log(l), (BLK, HEAD_DIM))

    def cur(part):
        return pl.BlockSpec((BLK, GROUP_W), lambda r, n: (n, r * ncb + g * 3 + part))

    def prev(part):
        return pl.BlockSpec((BLK, GROUP_W), lambda r, n: (jnp.maximum(n - 1, 0), r * ncb + g * 3 + part))

    o, lse = pl.pallas_call(
        body, name=name, grid=(d, NB),
        in_specs=[cur(0), prev(1), cur(1), prev(2), cur(2),
                  pl.BlockSpec((HEADS, BLK, 2 * BLK), lambda r, n: (0, 0, 0))],
        out_specs=[pl.BlockSpec((BLK, GROUP_W), lambda r, n: (n, r))] * 2,
        out_shape=[_sds((U, d * GROUP_W), F32)] * 2, compiler_params=_cp(2))(pv, pv, pv, pv, pv, bias)
    return o.reshape(S, GROUP_W), lse.reshape(S, GROUP_W)


def _attn_merge(os_, lses, name):
    def body(o0, o1, o2, l0, l1, l2, y_ref, lse_ref):
        a, b, c = l0[...], l1[...], l2[...]
        m = jnp.maximum(jnp.maximum(a, b), c)
        ea, eb, ec = jnp.exp(a - m), jnp.exp(b - m), jnp.exp(c - m)
        den = ea + eb + ec
        y_ref[...] = (ea * o0[...] + eb * o1[...] + ec * o2[...]) / den
        lse_ref[...] = m + jnp.log(den)

    S = os_[0].shape[0]
    return _rows_call(name, body, S, 512, 1, [_tile(t, GROUP_W) for t in (*os_, *lses)],
                      [_out_tile(GROUP_W, F32, GROUP_W)] * 2)


def _attn_bwd(proj, bias, lse, y, dy, g, name):
    S = proj.shape[0]
    d = DILATIONS[g]
    U = S // d
    NB = U // BLK
    ncb = IN_W // GROUP_W
    pv = proj.reshape(U, d * IN_W)
    lv, yv, dyv = (t.reshape(U, d * GROUP_W) for t in (lse, y, dy))

    def body(q_ref, kp_ref, kc_ref, vp_ref, vc_ref, b_ref, l_ref, y_ref, dy_ref,
             dq_ref, dk_ref, dv_ref, db_ref, ck_ref, cv_ref):
        r, n = pl.program_id(0), pl.program_id(1)

        @pl.when((r == 0) & (n == 0))
        def _():
            db_ref[...] = jnp.zeros_like(db_ref)

        @pl.when(n == 0)
        def _():
            ck_ref[...] = jnp.zeros_like(ck_ref)
            cv_ref[...] = jnp.zeros_like(cv_ref)

        @pl.when(n < NB)
        def _():
            mask = _band_mask(n)
            for h in range(HEADS):
                hs = slice(h * HEAD_DIM, (h + 1) * HEAD_DIM)
                q = _bf(q_ref[:, hs])
                kb = _bf(jnp.concatenate([kp_ref[:, hs], kc_ref[:, hs]], axis=0))
                vb = _bf(jnp.concatenate([vp_ref[:, hs], vc_ref[:, hs]], axis=0))
                s = _dot(q, kb, NT) * (HEAD_DIM ** -0.5) + b_ref[h]
                s = jnp.where(mask, s, NEG_INF)
                p = jnp.exp(s - l_ref[:, h * HEAD_DIM:h * HEAD_DIM + 1])
                dyh = dy_ref[:, hs]
                delta = jnp.sum(dyh * y_ref[:, hs], axis=-1, keepdims=True)
                dyb = _bf(dyh)
                ds = p * (_dot(dyb, vb, NT) - delta)
                db_ref[h] += ds
                dsb = _bf(ds * (HEAD_DIM ** -0.5))
                dq_ref[:, hs] = _bf(_dot(dsb, kb, NN))
                dkb = _dot(dsb, q, TN)
                dvb = _dot(_bf(p), dyb, TN)
                dk_ref[:, hs] = _bf(ck_ref[:, hs] + dkb[:BLK])
                dv_ref[:, hs] = _bf(cv_ref[:, hs] + dvb[:BLK])
                ck_ref[:, hs] = dkb[BLK:]
                cv_ref[:, hs] = dvb[BLK:]

        @pl.when(n == NB)
        def _():
            dk_ref[...] = _bf(ck_ref[...])
            dv_ref[...] = _bf(cv_ref[...])

    def qn(n):
        return jnp.minimum(n, NB - 1)

    def cur(part):
        return pl.BlockSpec((BLK, GROUP_W), lambda r, n: (qn(n), r * ncb + g * 3 + part))

    def prev(part):
        return pl.BlockSpec((BLK, GROUP_W), lambda r, n: (jnp.maximum(qn(n) - 1, 0), r * ncb + g * 3 + part))

    row = pl.BlockSpec((BLK, GROUP_W), lambda r, n: (qn(n), r))
    done = pl.BlockSpec((BLK, GROUP_W), lambda r, n: (jnp.maximum(n - 1, 0), r))
    dq, dk, dv, db = pl.pallas_call(
        body, name=name, grid=(d, NB + 1),
        in_specs=[cur(0), prev(1), cur(1), prev(2), cur(2),
                  pl.BlockSpec((HEADS, BLK, 2 * BLK), lambda r, n: (0, 0, 0)), row, row, row],
        out_specs=[row, done, done, pl.BlockSpec((HEADS, BLK, 2 * BLK), lambda r, n: (0, 0, 0))],
        out_shape=[_sds((U, d * GROUP_W), BF16)] * 3 + [_sds((HEADS, BLK, 2 * BLK), F32)],
        scratch_shapes=[pltpu.VMEM((BLK, GROUP_W), F32)] * 2,
        compiler_params=_cp(2))(pv, pv, pv, pv, pv, bias, lv, yv, dyv)
    return [t.reshape(S, GROUP_W) for t in (dq, dk, dv)], db


def _relbias_bwd(dbs, name):
    band = BLK * 2 * BLK
    buckets = jnp.stack([_t5_bucket(_band_rel() * d) for d in DILATIONS]).reshape(N_GROUPS, 1, band)
    onehot = (buckets == jnp.arange(NUM_BUCKETS).reshape(1, NUM_BUCKETS, 1)).astype(F32)
    dbf = jnp.stack([db.reshape(HEADS, band) for db in dbs])

    def body(oh_ref, db_ref, o_ref):
        o_ref[...] = lax.dot_general(oh_ref[...], db_ref[...], (NT, ((), ())), preferred_element_type=F32,
                                     precision=lax.Precision.HIGHEST)

    out = pl.pallas_call(
        body, name=name, grid=(N_GROUPS,),
        in_specs=[pl.BlockSpec((None, NUM_BUCKETS, band), lambda g: (g, 0, 0)),
                  pl.BlockSpec((None, HEADS, band), lambda g: (g, 0, 0))],
        out_specs=pl.BlockSpec((None, NUM_BUCKETS, HEADS), lambda g: (g, 0, 0)),
        out_shape=_sds((N_GROUPS, NUM_BUCKETS, HEADS), F32), compiler_params=_cp(1))(onehot, dbf)
    return out.transpose(1, 0, 2).reshape(NUM_BUCKETS, N_GROUPS * HEADS)


def _chunk_pos(shape):
    return lax.broadcasted_iota(jnp.int32, shape, 0) % HG_CHUNK


def _chunk_cumsum(v):
    pos = _chunk_pos(v.shape)
    s = 1
    while s < HG_CHUNK:
        v = v + jnp.where(pos >= s, pltpu.roll(v, s, 0), 0.0)
        s *= 2
    return v


def _chunk_rev_cumsum(v):
    pos = _chunk_pos(v.shape)
    n = v.shape[0]
    s = 1
    while s < HG_CHUNK:
        v = v + jnp.where(pos < HG_CHUNK - s, pltpu.roll(v, n - s, 0), 0.0)
        s *= 2
    return v


def _lower_bound(raw):
    a0, a1 = raw[0:1], raw[1:2]
    m = jnp.maximum(a0, a1)
    e0, e1 = jnp.exp(a0 - m), jnp.exp(a1 - m)
    return e0 / (e0 + e1)


def _hg_gates(qr, fr, lb):
    sf = _sigmoid(fr)
    f = lb + (1.0 - lb) * sf
    sq = _sigmoid(qr)
    return qr * sq, sq, f, sf


HG_COL0 = QKV_W // HG_W


def _hgrn_fwd(proj, lb_raw, nw, name):
    S = proj.shape[0]
    ncs = HG_TILE // HG_CHUNK
    tril = jnp.tril(jnp.ones((HG_CHUNK, HG_CHUNK), dtype=bool))

    def body(q_ref, f_ref, i_ref, og_ref, lb_ref, nw_ref, y_ref, o_ref, st_ref, state):
        @pl.when(pl.program_id(0) == 0)
        def _():
            state[...] = jnp.zeros_like(state)

        lb = _lower_bound(lb_ref[...])
        q, _, f, _ = _hg_gates(q_ref[...], f_ref[...], lb)
        k = 1.0 - f
        G = _chunk_cumsum(jnp.log(f))
        row = lax.broadcasted_iota(jnp.int32, (HG_CHUNK, HG_CHUNK), 0)
        col = lax.broadcasted_iota(jnp.int32, (HG_CHUNK, HG_CHUNK), 1)
        for h in range(HG_HEADS):
            hs = slice(h * HG_DK, (h + 1) * HG_DK)
            st = state[h]
            for c in range(ncs):
                cs = slice(c * HG_CHUNK, (c + 1) * HG_CHUNK)
                Gc = G[cs, hs]
                gl = Gc[HG_CHUNK - 1:HG_CHUNK]
                qt = _bf(q[cs, hs] * jnp.exp(Gc))
                kt = _bf(k[cs, hs] * jnp.exp(-Gc))
                kd = _bf(k[cs, hs] * jnp.exp(gl - Gc))
                v = _bf(i_ref[cs, hs])
                A = jnp.where(row >= col, _dot(qt, kt, NT), 0.0)
                o_ref[cs, hs] = _dot(_bf(A), v, NN) + _dot(qt, _bf(st), NT)
                st_ref[c, h] = st
                st = st * jnp.exp(gl) + _dot(v, kd, TN)
            state[h] = st
            oh = o_ref[:, hs]
            og = og_ref[:, hs]
            y_ref[:, hs] = oh * _rinv(oh) * nw_ref[...] * (og * _sigmoid(og))

    def colspec(j):
        return pl.BlockSpec((HG_TILE, HG_W), lambda i: (i, HG_COL0 + j))

    return pl.pallas_call(
        body, name=name, grid=(S // HG_TILE,),
        in_specs=[colspec(0), colspec(1), colspec(2), colspec(3),
                  pl.BlockSpec((2, HG_W), lambda i: (0, 0)), pl.BlockSpec((1, HG_DK), lambda i: (0, 0))],
        out_specs=[pl.BlockSpec((HG_TILE, HG_W), lambda i: (i, 0))] * 2
        + [pl.BlockSpec((ncs, HG_HEADS, HG_DK, HG_DK), lambda i: (i, 0, 0, 0))],
        out_shape=[_sds((S, HG_W), F32)] * 2 + [_sds((S // HG_CHUNK, HG_HEADS, HG_DK, HG_DK), F32)],
        scratch_shapes=[pltpu.VMEM((HG_HEADS, HG_DK, HG_DK), F32)],
        compiler_params=_cp(1))(proj, proj, proj, proj, lb_raw, nw)


def _hgrn_bwd(proj, lb_raw, nw, o, states, dy, name):
    S = proj.shape[0]
    ncs = HG_TILE // HG_CHUNK
    nt = S // HG_TILE

    def body(q_ref, f_ref, i_ref, og_ref, lb_ref, nw_ref, o_ref, st_ref, dy_ref,
             dq_ref, df_ref, di_ref, dog_ref, dlb_ref, dnw_ref, dstate, do_s, dG_s, dgl_s, dk_s, dlb_s):
        step = pl.program_id(0)

        @pl.when(step == 0)
        def _():
            dstate[...] = jnp.zeros_like(dstate)
            dlb_s[...] = jnp.zeros_like(dlb_s)
            dnw_ref[...] = jnp.zeros_like(dnw_ref)

        lb = _lower_bound(lb_ref[...])
        qr = q_ref[...]
        q, sq, f, sf = _hg_gates(qr, f_ref[...], lb)
        k = 1.0 - f
        G = _chunk_cumsum(jnp.log(f))
        nwv = nw_ref[...]
        row = lax.broadcasted_iota(jnp.int32, (HG_CHUNK, HG_CHUNK), 0)
        col = lax.broadcasted_iota(jnp.int32, (HG_CHUNK, HG_CHUNK), 1)
        for h in range(HG_HEADS):
            hs = slice(h * HG_DK, (h + 1) * HG_DK)
            oh = o_ref[:, hs]
            r = _rinv(oh)
            ohat = oh * r
            og = og_ref[:, hs]
            sg = _sigmoid(og)
            dyh = dy_ref[:, hs]
            don = dyh * (og * sg)
            dog_ref[:, hs] = _bf(dyh * (ohat * nwv) * (sg * (1.0 + og * (1.0 - sg))))
            dnw_ref[...] += jnp.sum(don * ohat, axis=0, keepdims=True)
            do_s[:, hs] = _norm_bwd(don, ohat, r, nwv)
            dst = dstate[h]
            for c in reversed(range(ncs)):
                cs = slice(c * HG_CHUNK, (c + 1) * HG_CHUNK)
                Gc = G[cs, hs]
                gl = Gc[HG_CHUNK - 1:HG_CHUNK]
                eG, enG, edG, egl = jnp.exp(Gc), jnp.exp(-Gc), jnp.exp(gl - Gc), jnp.exp(gl)
                qt, kt, kd = q[cs, hs] * eG, k[cs, hs] * enG, k[cs, hs] * edG
                qtb, ktb, kdb = _bf(qt), _bf(kt), _bf(kd)
                v = _bf(i_ref[cs, hs])
                do = _bf(do_s[cs, hs])
                st = st_ref[c, h]
                dstb = _bf(dst)
                A = jnp.where(row >= col, _dot(qtb, ktb, NT), 0.0)
                dA = _bf(jnp.where(row >= col, _dot(do, v, NT), 0.0))
                di_ref[cs, hs] = _bf(_dot(_bf(A), do, TN) + _dot(kdb, dstb, NT))
                dqt = _dot(dA, ktb, NN) + _dot(do, _bf(st), NN)
                dkt = _dot(dA, qtb, TN)
                dkd = _dot(v, dstb, NN)
                dgl = egl * jnp.sum(st * dst, axis=0, keepdims=True) + jnp.sum(dkd * kd, axis=0, keepdims=True)
                dst = dst * egl + _dot(do, qtb, TN)
                dq_ref[cs, hs] = _bf(dqt * eG * (sq[cs, hs] * (1.0 + qr[cs, hs] * (1.0 - sq[cs, hs]))))
                dk_s[cs, hs] = dkt * enG + dkd * edG
                dG_s[cs, hs] = dqt * qt - dkt * kt - dkd * kd
                dgl_s[cs, hs] = jnp.broadcast_to(dgl, (HG_CHUNK, HG_DK))
            dstate[h] = dst
        dg = _chunk_rev_cumsum(dG_s[...]) + dgl_s[...]
        dfv = dg / f - dk_s[...]
        df_ref[...] = _bf(dfv * (1.0 - lb) * sf * (1.0 - sf))
        dlb_s[...] += jnp.sum(dfv * (1.0 - sf), axis=0, keepdims=True)

        @pl.when(step == nt - 1)
        def _():
            t = dlb_s[...] * lb * (1.0 - lb)
            dlb_ref[...] = jnp.concatenate([t, -t], axis=0)

    def colspec(j):
        return pl.BlockSpec((HG_TILE, HG_W), lambda i: (nt - 1 - i, HG_COL0 + j))

    tile = pl.BlockSpec((HG_TILE, HG_W), lambda i: (nt - 1 - i, 0))
    outs = pl.pallas_call(
        body, name=name, grid=(nt,),
        in_specs=[colspec(0), colspec(1), colspec(2), colspec(3),
                  pl.BlockSpec((2, HG_W), lambda i: (0, 0)), pl.BlockSpec((1, HG_DK), lambda i: (0, 0)),
                  tile, pl.BlockSpec((ncs, HG_HEADS, HG_DK, HG_DK), lambda i: (nt - 1 - i, 0, 0, 0)), tile],
        out_specs=[tile] * 4 + [pl.BlockSpec((2, HG_W), lambda i: (0, 0)), pl.BlockSpec((1, HG_DK), lambda i: (0, 0))],
        out_shape=[_sds((S, HG_W), BF16)] * 4 + [_sds((2, HG_W), F32), _sds((1, HG_DK), F32)],
        scratch_shapes=[pltpu.VMEM((HG_HEADS, HG_DK, HG_DK), F32)] + [pltpu.VMEM((HG_TILE, HG_W), F32)] * 4
        + [pltpu.VMEM((1, HG_W), F32)],
        compiler_params=_cp(1))(proj, proj, proj, proj, lb_raw, nw, o, states, dy)
    return outs[:4], outs[4], outs[5]


GATE_COL0 = (QKV_W + 4 * HG_W) // GROUP_W
HALF_D = D_MODEL // 2


def _gate_tiles(proj):
    return [_tile(proj, HALF_D, lambda c: GATE_COL0 + c), _tile(proj, HALF_D, lambda c: GATE_COL0 + 2 + c)]


def _merge_fwd(za, zh, proj, name):
    def body(za_ref, zh_ref, g0_ref, g1_ref, m_ref):
        m_ref[...] = _bf(_sigmoid(g0_ref[...]) * za_ref[...] + _sigmoid(g1_ref[...]) * zh_ref[...])

    col = lambda c: c
    return _rows_call(name, body, za.shape[0], 512, 2,
                      [_tile(za, HALF_D, col), _tile(zh, HALF_D, col), *_gate_tiles(proj)],
                      [_out_tile(D_MODEL, BF16, HALF_D, col)])[0]


def _merge_bwd(dm, za, zh, proj, name):
    def body(dm_ref, za_ref, zh_ref, g0_ref, g1_ref, dza_ref, dzh_ref, dg0_ref, dg1_ref):
        dmv = dm_ref[...]
        s0, s1 = _sigmoid(g0_ref[...]), _sigmoid(g1_ref[...])
        dza_ref[...] = _bf(dmv * s0)
        dzh_ref[...] = _bf(dmv * s1)
        dg0_ref[...] = _bf(dmv * za_ref[...] * s0 * (1.0 - s0))
        dg1_ref[...] = _bf(dmv * zh_ref[...] * s1 * (1.0 - s1))

    col = lambda c: c
    return _rows_call(name, body, za.shape[0], 512, 2,
                      [_tile(dm, HALF_D, col), _tile(za, HALF_D, col), _tile(zh, HALF_D, col), *_gate_tiles(proj)],
                      [_out_tile(D_MODEL, BF16, HALF_D, col)] * 4)


def _mix_out(mo, x, w_post, w_pre, name):
    def body(mo_ref, x_ref, wp_ref, wf_ref, x1_ref, h2_ref):
        z = mo_ref[...]
        x1 = x_ref[...] + z * _rinv(z) * wp_ref[...]
        x1_ref[...] = x1
        h2_ref[...] = _bf(x1 * _rinv(x1) * wf_ref[...])

    return _rows_call(name, body, x.shape[0], 512, 1,
                      [_tile(mo, D_MODEL), _tile(x, D_MODEL), _full(w_post), _full(w_pre)],
                      [_out_tile(D_MODEL, F32, D_MODEL), _out_tile(D_MODEL, BF16, D_MODEL)])


def _loss_head(ffo, x1, tgt, w, name):
    def body(f_ref, x1_ref, t_ref, w_ref, dx_ref, df_ref, dw_ref, loss_ref):
        z = f_ref[...]
        r = _rinv(z)
        zhat = z * r
        wv = w_ref[...]
        e = x1_ref[...] + zhat * wv - t_ref[...]
        dx = e * (1.0 / D_MODEL)
        dx_ref[...] = dx
        df_ref[...] = _bf(_norm_bwd(dx, zhat, r, wv))
        _acc(dw_ref, jnp.sum(dx * zhat, axis=0, keepdims=True))
        part = 0.5 * jnp.sum(jnp.sum(e * e, axis=1, keepdims=True), axis=0, keepdims=True) * (1.0 / D_MODEL)
        _acc(loss_ref, jnp.broadcast_to(part, (1, LANES)))

    return _rows_call(name, body, x1.shape[0], 512, 1,
                      [_tile(ffo, D_MODEL), _tile(x1, D_MODEL), _tile(tgt, D_MODEL), _full(w)],
                      [_out_tile(D_MODEL, F32, D_MODEL), _out_tile(D_MODEL, BF16, D_MODEL),
                       _out_acc(1, D_MODEL, D_MODEL), _out_acc(1, LANES, LANES)])


CONV_CB = D_FF // 2
CONV_TM = 512
HALO = 8
SQRT_HALF = 0.7071067811865476
INV_SQRT_2PI = 0.3989422804014327


def _conv_taps(u_ref, halo_ref, first):
    u = u_ref[...]
    row = lax.broadcasted_iota(jnp.int32, u.shape, 0)
    p1 = jnp.where(first, 0.0, halo_ref[HALO - 1:HALO, :])
    p2 = jnp.where(first, 0.0, halo_ref[HALO - 2:HALO - 1, :])
    u1 = jnp.where(row == 0, p1, pltpu.roll(u, 1, 0))
    u2 = jnp.where(row == 0, p2, jnp.where(row == 1, p1, pltpu.roll(u, 2, 0)))
    return u2, u1, u


def _conv(taps, w_ref, b_ref):
    return b_ref[...] + w_ref[0:1, :] * taps[0] + w_ref[1:2, :] * taps[1] + w_ref[2:3, :] * taps[2]


def _conv_specs(tm):
    nh = tm // HALO
    nc = D_FF // CONV_CB

    def tile(off):
        return pl.BlockSpec((tm, CONV_CB), lambda c, i: (i, off + c))

    def halo(off):
        return pl.BlockSpec((HALO, CONV_CB), lambda c, i: (jnp.maximum(i * nh - 1, 0), off + c))

    def small(rows, off):
        return pl.BlockSpec((rows, CONV_CB), lambda c, i: (0, off + c))

    return nc, tile, halo, small


def _conv_gelu_fwd(u, cw, cb, name):
    S = u.shape[0]
    tm = CONV_TM
    nc, tile, halo, small = _conv_specs(tm)

    def body(ug, hg, uv, hv, wg, wv, bg, bv, a_ref):
        first = pl.program_id(1) == 0
        cg = _conv(_conv_taps(ug, hg, first), wg, bg)
        cv = _conv(_conv_taps(uv, hv, first), wv, bv)
        a_ref[...] = _bf(0.5 * cg * (1.0 + lax.erf(cg * SQRT_HALF)) * cv)

    return pl.pallas_call(
        body, name=name, grid=(nc, S // tm),
        in_specs=[tile(0), halo(0), tile(nc), halo(nc), small(3, 0), small(3, nc), small(1, 0), small(1, nc)],
        out_specs=tile(0), out_shape=_sds((S, D_FF), BF16), compiler_params=_cp(2))(u, u, u, u, cw, cw, cb, cb)


def _conv_gelu_bwd(u, da, cw, cb, name):
    S = u.shape[0]
    tm = CONV_TM
    nc, tile, halo, small = _conv_specs(tm)

    def body(ug, hg, uv, hv, wg, wv, bg, bv, da_ref, dcg_ref, dcv_ref, dwg_ref, dwv_ref, dbg_ref, dbv_ref):
        first = pl.program_id(1) == 0
        tg = _conv_taps(ug, hg, first)
        tv = _conv_taps(uv, hv, first)
        cg = _conv(tg, wg, bg)
        cv = _conv(tv, wv, bv)
        phi = 0.5 * (1.0 + lax.erf(cg * SQRT_HALF))
        dav = da_ref[...]
        dcg = dav * cv * (phi + cg * jnp.exp(-0.5 * cg * cg) * INV_SQRT_2PI)
        dcv = dav * (cg * phi)
        dcg_ref[...] = dcg
        dcv_ref[...] = dcv
        for dc, taps, dw_ref, db_ref in ((dcg, tg, dwg_ref, dbg_ref), (dcv, tv, dwv_ref, dbv_ref)):
            _acc(db_ref, jnp.sum(dc, axis=0, keepdims=True))
            for j in range(3):
                _acc(dw_ref.at[j:j + 1, :], jnp.sum(dc * taps[j], axis=0, keepdims=True))

    return pl.pallas_call(
        body, name=name, grid=(nc, S // tm),
        in_specs=[tile(0), halo(0), tile(nc), halo(nc), small(3, 0), small(3, nc), small(1, 0), small(1, nc), tile(0)],
        out_specs=[tile(0), tile(0), small(3, 0), small(3, 0), small(1, 0), small(1, 0)],
        out_shape=[_sds((S, D_FF), F32)] * 2 + [_sds((3, D_FF), F32)] * 2 + [_sds((1, D_FF), F32)] * 2,
        compiler_params=_cp(2))(u, u, u, u, cw, cw, cb, cb, da)


def _conv_input_bwd(dcg, dcv, cw, name):
    S = dcg.shape[0]
    tm = CONV_TM
    nc, tile, _, small = _conv_specs(tm)
    nh = tm // HALO
    nt = S // tm

    def nxt(off):
        return pl.BlockSpec((HALO, CONV_CB), lambda c, i: (jnp.minimum((i + 1) * nh, S // HALO - 1), off + c))

    def body(g_ref, ng_ref, v_ref, nv_ref, wg, wv, dug_ref, duv_ref):
        last = pl.program_id(1) == nt - 1
        for dc_ref, n_ref, w_ref, du_ref in ((g_ref, ng_ref, wg, dug_ref), (v_ref, nv_ref, wv, duv_ref)):
            dc = dc_ref[...]
            row = lax.broadcasted_iota(jnp.int32, dc.shape, 0)
            n1 = jnp.where(last, 0.0, n_ref[0:1, :])
            n2 = jnp.where(last, 0.0, n_ref[1:2, :])
            d1 = jnp.where(row == tm - 1, n1, pltpu.roll(dc, tm - 1, 0))
            d2 = jnp.where(row == tm - 1, n2, jnp.where(row == tm - 2, n1, pltpu.roll(dc, tm - 2, 0)))
            du_ref[...] = _bf(w_ref[2:3, :] * dc + w_ref[1:2, :] * d1 + w_ref[0:1, :] * d2)

    return pl.pallas_call(
        body, name=name, grid=(nc, nt),
        in_specs=[tile(0), nxt(0), tile(0), nxt(0), small(3, 0), small(3, nc)],
        out_specs=[tile(0), tile(0)], out_shape=[_sds((S, D_FF), BF16)] * 2,
        compiler_params=_cp(2))(dcg, dcg, dcv, dcv, cw, cw)


def _row_tile(n, cap):
    best = n
    for t in range(16, cap + 1, 16):
        if n % t == 0:
            best = t
    return best if best <= cap else n


def _adamw(w, g, m, v, name):
    R, C = w.shape
    tr = _row_tile(R, max(16, (512 * 1024) // (4 * C) // 16 * 16))

    def body(w_ref, g_ref, m_ref, v_ref, d_ref, nm_ref, nv_ref):
        gv = g_ref[...]
        nm = ADAM_B1 * m_ref[...] + (1.0 - ADAM_B1) * gv
        nv = ADAM_B2 * v_ref[...] + (1.0 - ADAM_B2) * (gv * gv)
        m_hat = nm / (1.0 - ADAM_B1 ** ADAM_STEP)
        v_hat = nv / (1.0 - ADAM_B2 ** ADAM_STEP)
        d_ref[...] = -ADAM_LR * (m_hat / (jnp.sqrt(v_hat) + ADAM_EPS) + ADAM_WD * w_ref[...])
        nm_ref[...] = nm
        nv_ref[...] = nv

    spec = pl.BlockSpec((tr, C), lambda i: (i, 0))
    return pl.pallas_call(body, name=name, grid=(R // tr,), in_specs=[spec] * 4, out_specs=[spec] * 3,
                          out_shape=[_sds((R, C), F32)] * 3, compiler_params=_cp(1))(w, g, m, v)


def _pair_sum(gfull, rcv, c_idx, name):
    nb, R, C = gfull.shape
    half = R // 2
    tr = _row_tile(half, 256)
    nt = half // tr

    def body(c_ref, g_ref, r_ref, o_ref):
        o_ref[...] = _bf(g_ref[...] + r_ref[...])

    return pl.pallas_call(
        body, name=name,
        grid_spec=pltpu.PrefetchScalarGridSpec(
            num_scalar_prefetch=1, grid=(nb, nt),
            in_specs=[pl.BlockSpec((None, tr, C), lambda j, i, c_ref: (j, c_ref[0] * nt + i, 0)),
                      pl.BlockSpec((None, tr, C), lambda j, i, c_ref: (j, i, 0))],
            out_specs=pl.BlockSpec((None, tr, C), lambda j, i, c_ref: (j, i, 0))),
        out_shape=_sds((nb, half, C), BF16), compiler_params=_cp(2))(c_idx, gfull, rcv)


def _chip_sum(parts, name):
    nb, H, C = parts.shape
    tr = _row_tile(H, 256)

    def body(p_ref, o_ref):
        acc = p_ref[0].astype(F32)
        for k in range(1, nb):
            acc = acc + p_ref[k].astype(F32)
        o_ref[...] = acc

    return pl.pallas_call(
        body, name=name, grid=(H // tr,),
        in_specs=[pl.BlockSpec((nb, tr, C), lambda i: (0, i, 0))], out_specs=pl.BlockSpec((tr, C), lambda i: (i, 0)),
        out_shape=_sds((H, C), F32), compiler_params=_cp(1))(parts)


ANY = pl.BlockSpec(memory_space=pl.ANY)


def _place():
    x, y, c = lax.axis_index("x"), lax.axis_index("y"), lax.axis_index("c")
    chips = [(1 - x, y), (x, 1 - y), (1 - x, 1 - y)]
    return x, y, c, chips


def _chip_id(px, py):
    return 2 * px + py


def _remote(src, dst, send_sems, recv_sems, k, to):
    return pltpu.make_async_remote_copy(src_ref=src, dst_ref=dst, send_sem=send_sems.at[k], recv_sem=recv_sems.at[k],
                                        device_id=to, device_id_type=MESH)


def _gather_weights(shards, name):
    n = len(shards)

    def body(*refs):
        ins, outs = refs[:n], refs[n:2 * n]
        send_sems, recv_sems, local_sems = refs[2 * n:]
        x, y, c, chips = _place()
        me = _chip_id(x, y)
        sib = (x, y, 1 - c)
        local = [pltpu.make_async_copy(ins[a], outs[a].at[me], local_sems.at[a]) for a in range(n)]
        for cp in local:
            cp.start()
        sent = []
        for a in range(n):
            R = ins[a].shape[0]
            rows = pl.ds(c * (R // 2), R // 2) if R % 2 == 0 else pl.ds(0, R)
            for j, chip in enumerate(chips):
                cp = _remote(ins[a].at[rows], outs[a].at[me, rows], send_sems, recv_sems, 6 * a + j, (*chip, c))
                cp.start()
                sent.append(cp)
        for a in range(n):
            R = ins[a].shape[0]
            split = R % 2 == 0
            rows = pl.ds(c * (R // 2), R // 2) if split else pl.ds(0, R)
            for j, chip in enumerate(chips):
                landed = outs[a].at[_chip_id(*chip), rows]
                _remote(landed, landed, send_sems, recv_sems, 6 * a + j, (*chip, c)).wait_recv()
                if split:
                    cp = _remote(landed, landed, send_sems, recv_sems, 6 * a + 3 + j, sib)
                    cp.start()
                    sent.append(cp)
        for a in range(n):
            R = ins[a].shape[0]
            if R % 2 == 0:
                other = pl.ds((1 - c) * (R // 2), R // 2)
                for j, chip in enumerate(chips):
                    passed = outs[a].at[_chip_id(*chip), other]
                    _remote(passed, passed, send_sems, recv_sems, 6 * a + 3 + j, sib).wait_recv()
        for cp in sent:
            cp.wait_send()
        for cp in local:
            cp.wait()

    return pl.pallas_call(
        body, name=name, in_specs=[ANY] * n, out_specs=[ANY] * n,
        out_shape=[_sds((N_CHIPS, *s.shape), s.dtype) for s in shards],
        scratch_shapes=[pltpu.SemaphoreType.DMA((6 * n,)), pltpu.SemaphoreType.DMA((6 * n,)),
                        pltpu.SemaphoreType.DMA((n,))])(*shards)


def _pair_exchange(grads, name):
    n = len(grads)

    def body(*refs):
        ins, outs = refs[:n], refs[n:2 * n]
        send_sems, recv_sems = refs[2 * n:]
        x, y, c, _ = _place()
        cps = []
        for a in range(n):
            half = ins[a].shape[1] // 2
            cp = _remote(ins[a].at[:, pl.ds((1 - c) * half, half), :], outs[a], send_sems, recv_sems, a, (x, y, 1 - c))
            cp.start()
            cps.append(cp)
        for cp in cps:
            cp.wait()

    return pl.pallas_call(
        body, name=name, in_specs=[ANY] * n, out_specs=[ANY] * n,
        out_shape=[_sds((g.shape[0], g.shape[1] // 2, g.shape[2]), g.dtype) for g in grads],
        scratch_shapes=[pltpu.SemaphoreType.DMA((n,)), pltpu.SemaphoreType.DMA((n,))])(*grads)


def _chip_exchange(parts, name):
    n = len(parts)

    def body(*refs):
        ins, outs = refs[:n], refs[n:2 * n]
        send_sems, recv_sems, local_sems = refs[2 * n:]
        x, y, c, chips = _place()
        me = _chip_id(x, y)
        local = [pltpu.make_async_copy(ins[a].at[me], outs[a].at[me], local_sems.at[a]) for a in range(n)]
        for cp in local:
            cp.start()
        cps = []
        for a in range(n):
            for j, chip in enumerate(chips):
                cp = _remote(ins[a].at[_chip_id(*chip)], outs[a].at[me], send_sems, recv_sems, 3 * a + j, (*chip, c))
                cp.start()
                cps.append(cp)
        for a in range(n):
            for j, chip in enumerate(chips):
                landed = outs[a].at[_chip_id(*chip)]
                _remote(landed, landed, send_sems, recv_sems, 3 * a + j, (*chip, c)).wait_recv()
        for cp in cps:
            cp.wait_send()
        for cp in local:
            cp.wait()

    return pl.pallas_call(
        body, name=name, in_specs=[ANY] * n, out_specs=[ANY] * n,
        out_shape=[_sds(p.shape, p.dtype) for p in parts],
        scratch_shapes=[pltpu.SemaphoreType.DMA((3 * n,)), pltpu.SemaphoreType.DMA((3 * n,)),
                        pltpu.SemaphoreType.DMA((n,))])(*parts)


def _pair_concat(halves, name):
    n = len(halves)

    def body(*refs):
        ins, outs = refs[:n], refs[n:2 * n]
        send_sems, recv_sems, local_sems = refs[2 * n:]
        x, y, c, _ = _place()
        cps = []
        for a in range(n):
            H = ins[a].shape[0]
            mine = outs[a].at[pl.ds(c * H, H)]
            lc = pltpu.make_async_copy(ins[a], mine, local_sems.at[a])
            lc.start()
            cp = _remote(ins[a], mine, send_sems, recv_sems, a, (x, y, 1 - c))
            cp.start()
            cps.append((lc, cp))
        for a, (lc, cp) in enumerate(cps):
            H = ins[a].shape[0]
            other = outs[a].at[pl.ds((1 - c) * H, H)]
            _remote(other, other, send_sems, recv_sems, a, (x, y, 1 - c)).wait_recv()
            cp.wait_send()
            lc.wait()

    return pl.pallas_call(
        body, name=name, in_specs=[ANY] * n, out_specs=[ANY] * n,
        out_shape=[_sds((2 * h.shape[0], h.shape[1]), h.dtype) for h in halves],
        scratch_shapes=[pltpu.SemaphoreType.DMA((n,)), pltpu.SemaphoreType.DMA((n,)),
                        pltpu.SemaphoreType.DMA((n,))])(*halves)


def _all_sum(pack, name):
    R, C = pack.shape

    def body(p_ref, o_ref, buf, send_sems, recv_sems):
        x, y, c, _ = _place()
        me = 4 * x + 2 * y + c
        buf[me] = p_ref[...]
        cps = []
        for k in range(1, N_DEV):
            to = (x ^ (k >> 2), y ^ ((k >> 1) & 1), c ^ (k & 1))
            cp = _remote(p_ref, buf.at[me], send_sems, recv_sems, k - 1, to)
            cp.start()
            cps.append(cp)
        for k in range(1, N_DEV):
            frm = (x ^ (k >> 2), y ^ ((k >> 1) & 1), c ^ (k & 1))
            slot = buf.at[4 * frm[0] + 2 * frm[1] + frm[2]]
            _remote(slot, slot, send_sems, recv_sems, k - 1, frm).wait_recv()
        acc = buf[0]
        for k in range(1, N_DEV):
            acc = acc + buf[k]
        o_ref[...] = acc
        for cp in cps:
            cp.wait_send()

    vm = pl.BlockSpec(memory_space=pltpu.VMEM)
    return pl.pallas_call(
        body, name=name, in_specs=[vm], out_specs=vm, out_shape=_sds((R, C), F32),
        scratch_shapes=[pltpu.VMEM((N_DEV, R, C), F32), pltpu.SemaphoreType.DMA((N_DEV - 1,)),
                        pltpu.SemaphoreType.DMA((N_DEV - 1,))])(pack)


def _local_step(xs, tgt, pre_mix_norm, W_in, rel_bias, hgrn_lb_raw, hgrn_norm, W_a, W_h, W_out, post_mix_norm,
                pre_ffn_norm, W_up, conv_w, conv_b, W_down, post_ffn_norm):
    h1 = _norm_fwd(xs, pre_mix_norm, "pre_mix_norm")
    proj = _mm_nn_blk(h1, W_in, "proj_in")
    rel = _band_rel()
    biases = [rel_bias[_t5_bucket(rel * d)][:, :, g * HEADS:(g + 1) * HEADS].transpose(2, 0, 1)
              for g, d in enumerate(DILATIONS)]
    fw = [_attn_fwd(proj, biases[g], g, f"attn_fwd{g}") for g in range(N_GROUPS)]
    y, lse = _attn_merge([t[0] for t in fw], [t[1] for t in fw], "attn_merge")
    yh, o_h, states = _hgrn_fwd(proj, hgrn_lb_raw, hgrn_norm, "hgrn_fwd")
    za = _mm_nn_blk(y, W_a, "branch_attn")
    zh = _mm_nn_blk(yh, W_h, "branch_hgrn")
    merged = _merge_fwd(za, zh, proj, "merge_fwd")
    mo = _mm_nn(merged, W_out, "mix_out")
    x1, h2 = _mix_out(mo, xs, post_mix_norm, pre_ffn_norm, "mix_residual")
    u = _mm_nn_blk(h2, W_up, "ffn_up")
    a = _conv_gelu_fwd(u, conv_w, conv_b, "conv_gelu_fwd")
    ffo = _mm_nn(a, W_down, "ffn_down")
    dx2, dff, g_post_ffn, loss = _loss_head(ffo, x1, tgt, post_ffn_norm, "loss_head")

    da = _mm_nt(dff, W_down, "d_ffn_act")
    g_down = _mm_tn(a, dff, "g_w_down")
    dcg, dcv, gwg, gwv, gbg, gbv = _conv_gelu_bwd(u, da, conv_w, conv_b, "conv_gelu_bwd")
    g_conv_w = jnp.concatenate([gwg, gwv], axis=1)
    g_conv_b = jnp.concatenate([gbg, gbv], axis=1)
    du = jnp.concatenate(_conv_input_bwd(dcg, dcv, conv_w, "conv_input_bwd"), axis=1)
    dh2 = _mm_nt_blk(du, W_up, "d_ffn_in")
    g_up = _mm_tn_blk(h2, du, N_CHIPS, "g_w_up")
    dx1, g_pre_ffn = _prenorm_bwd(dh2, x1, pre_ffn_norm, dx2, "pre_ffn_norm_bwd")
    dmo, g_post_mix = _postnorm_bwd(dx1, mo, post_mix_norm, "post_mix_norm_bwd")
    dmerged = _mm_nt(dmo, W_out, "d_merged")
    g_out = _mm_tn(merged, dmo, "g_w_out")
    dza, dzh, dg0, dg1 = _merge_bwd(dmerged, za, zh, proj, "merge_bwd")
    dy = _mm_nt_blk(dza, W_a, "d_attn_out")
    g_a = _mm_tn_blk(y, dza, N_CHIPS, "g_w_branch_attn")
    dyh = _mm_nt_blk(dzh, W_h, "d_hgrn_out")
    g_h = _mm_tn_blk(yh, dzh, N_CHIPS, "g_w_branch_hgrn")
    dqkv, dbs = [], []
    for g in range(N_GROUPS):
        parts, db = _attn_bwd(proj, biases[g], lse, y, dy, g, f"attn_bwd{g}")
        dqkv += parts
        dbs.append(db)
    g_rel_bias = _relbias_bwd(dbs, "rel_bias_bwd")
    dhg, g_lb_raw, g_hgrn_norm = _hgrn_bwd(proj, hgrn_lb_raw, hgrn_norm, o_h, states, dyh, "hgrn_bwd")
    dproj = jnp.concatenate([*dqkv, *dhg, dg0, dg1], axis=1)
    dh1 = _mm_nt_blk(dproj, W_in, "d_proj_in")
    g_in = _mm_tn_blk(h1, dproj, N_CHIPS, "g_w_in")
    grad_x, g_pre_mix = _prenorm_bwd(dh1, xs, pre_mix_norm, dx1, "pre_mix_norm_bwd")
    small = dict(pre_mix_norm=g_pre_mix, rel_bias=g_rel_bias, hgrn_lb_raw=g_lb_raw, hgrn_norm=g_hgrn_norm,
                 post_mix_norm=g_post_mix, pre_ffn_norm=g_pre_ffn, conv_w=g_conv_w, conv_b=g_conv_b,
                 post_ffn_norm=g_post_ffn)
    big = dict(w_in=g_in, w_up=g_up, w_down=g_down.reshape(N_CHIPS, D_FF // N_CHIPS, D_MODEL),
               w_out=g_out.reshape(N_CHIPS, D_MODEL // N_CHIPS, D_MODEL), w_branch_attn=g_a, w_branch_hgrn=g_h)
    return loss, grad_x, small, big


SMALL = ("pre_mix_norm", "rel_bias", "hgrn_lb_raw", "hgrn_norm", "post_mix_norm", "pre_ffn_norm", "conv_w", "conv_b",
         "post_ffn_norm")
BIG = ("w_in", "w_up", "w_down", "w_out", "w_branch_attn", "w_branch_hgrn")
WEIGHTS = ("pre_mix_norm", "w_in", "rel_bias", "hgrn_lb_raw", "hgrn_norm", "w_branch_attn", "w_branch_hgrn", "w_out",
           "post_mix_norm", "pre_ffn_norm", "w_up", "conv_w", "conv_b", "w_down", "post_ffn_norm")


def kernel(x, pre_mix_norm, w_in, rel_bias, hgrn_lb_raw, hgrn_norm, w_branch_attn, w_branch_hgrn, w_out, post_mix_norm, pre_ffn_norm, w_up, conv_w, conv_b, w_down, post_ffn_norm, loss_target, m_pre_mix_norm, m_w_in, m_rel_bias, m_hgrn_lb_raw, m_hgrn_norm, m_w_branch_attn, m_w_branch_hgrn, m_w_out, m_post_mix_norm, m_pre_ffn_norm, m_w_up, m_conv_w, m_conv_b, m_w_down, m_post_ffn_norm, v_pre_mix_norm, v_w_in, v_rel_bias, v_hgrn_lb_raw, v_hgrn_norm, v_w_branch_attn, v_w_branch_hgrn, v_w_out, v_post_mix_norm, v_pre_ffn_norm, v_w_up, v_conv_w, v_conv_b, v_w_down, v_post_ffn_norm):
    w = dict(pre_mix_norm=pre_mix_norm, w_in=w_in, rel_bias=rel_bias, hgrn_lb_raw=hgrn_lb_raw, hgrn_norm=hgrn_norm,
             w_branch_attn=w_branch_attn, w_branch_hgrn=w_branch_hgrn, w_out=w_out, post_mix_norm=post_mix_norm,
             pre_ffn_norm=pre_ffn_norm, w_up=w_up, conv_w=conv_w, conv_b=conv_b, w_down=w_down,
             post_ffn_norm=post_ffn_norm)
    m = dict(pre_mix_norm=m_pre_mix_norm, w_in=m_w_in, rel_bias=m_rel_bias, hgrn_lb_raw=m_hgrn_lb_raw,
             hgrn_norm=m_hgrn_norm, w_branch_attn=m_w_branch_attn, w_branch_hgrn=m_w_branch_hgrn, w_out=m_w_out,
             post_mix_norm=m_post_mix_norm, pre_ffn_norm=m_pre_ffn_norm, w_up=m_w_up, conv_w=m_conv_w,
             conv_b=m_conv_b, w_down=m_w_down, post_ffn_norm=m_post_ffn_norm)
    v = dict(pre_mix_norm=v_pre_mix_norm, w_in=v_w_in, rel_bias=v_rel_bias, hgrn_lb_raw=v_hgrn_lb_raw,
             hgrn_norm=v_hgrn_norm, w_branch_attn=v_w_branch_attn, w_branch_hgrn=v_w_branch_hgrn, w_out=v_w_out,
             post_mix_norm=v_post_mix_norm, pre_ffn_norm=v_pre_ffn_norm, w_up=v_w_up, conv_w=v_conv_w,
             conv_b=v_conv_b, w_down=v_w_down, post_ffn_norm=v_post_ffn_norm)
    shard2d = {n: (w[n][0] if w[n].ndim == 3 else w[n]) for n in WEIGHTS}
    chip = 2 * lax.axis_index("x") + lax.axis_index("y")
    core = lax.axis_index("c")

    gathered = _gather_weights([_bf(shard2d[n]) for n in BIG] + [shard2d["conv_w"]], "gather_weights")
    W = dict(zip(BIG, gathered[:-1]))
    conv_w_full = gathered[-1].transpose(1, 0, 2).reshape(3, 2 * D_FF)
    loss, grad_x, small, big = _local_step(
        x[0], loss_target[0], pre_mix_norm, W["w_in"], rel_bias, hgrn_lb_raw, hgrn_norm, W["w_branch_attn"],
        W["w_branch_hgrn"], W["w_out"].reshape(D_MODEL, D_MODEL), post_mix_norm, pre_ffn_norm, W["w_up"],
        conv_w_full, conv_b, W["w_down"].reshape(D_FF, D_MODEL), post_ffn_norm)

    flat = [small[n].reshape(-1) for n in SMALL] + [loss.reshape(-1)]
    sizes = [t.shape[0] for t in flat]
    summed = _all_sum(jnp.concatenate(flat).reshape(-1, LANES), "sum_small").reshape(-1)
    offs = [sum(sizes[:i]) for i in range(len(sizes))]
    grads = {}
    for n, o, sz in zip(SMALL, offs, sizes):
        grads[n] = summed[o:o + sz].reshape(small[n].shape)
    loss_total = summed[offs[-1]]
    cw = 2 * D_FF // N_CHIPS
    grads["conv_w"] = lax.dynamic_slice(grads["conv_w"], (0, chip * cw), (3, cw))

    gl = [big[n] for n in BIG]
    from_sibling = _pair_exchange(gl, "pair_exchange")
    c_idx = core.reshape(1).astype(jnp.int32)
    partial = [_pair_sum(g, r, c_idx, f"pair_sum_{n}") for n, g, r in zip(BIG, gl, from_sibling)]
    arrived = _chip_exchange(partial, "chip_exchange")
    halves = [_chip_sum(p, f"chip_sum_{n}") for n, p in zip(BIG, arrived)]
    for n, g in zip(BIG, _pair_concat(halves, "pair_concat")):
        grads[n] = g

    out_g, out_d, out_m, out_v = [], [], [], []
    for n in WEIGHTS:
        d2, m2, v2 = _adamw(shard2d[n], grads[n], m[n].reshape(shard2d[n].shape), v[n].reshape(shard2d[n].shape),
                            f"adamw_{n}")
        shape = w[n].shape
        out_g.append(grads[n].reshape(shape))
        out_d.append(d2.reshape(shape))
        out_m.append(m2.reshape(shape))
        out_v.append(v2.reshape(shape))
    return (loss_total, grad_x[None], *out_g, *out_d, *out_m, *out_v)
```

```python
import functools
import math

import jax
import jax.numpy as jnp
from jax import lax
from jax.experimental import pallas as pl
from jax.experimental.pallas import tpu as pltpu

F32 = jnp.float32
BF16 = jnp.bfloat16
MESH = pl.DeviceIdType.MESH

D_MODEL = 1024
N_GROUPS = 3
DILATIONS = (1, 4, 16)
HEADS = 8
HEAD_DIM = 64
GROUP_W = HEADS * HEAD_DIM
QKV_W = N_GROUPS * 3 * GROUP_W
BLK = 128
NEG_INF = -1e30
NUM_BUCKETS = 32
MAX_EXACT = 16
MAX_DISTANCE = 2048
HG_HEADS = 4
HG_DK = 128
HG_W = HG_HEADS * HG_DK
HG_CHUNK = 32
HG_TILE = 256
IN_W = QKV_W + 4 * HG_W + 2 * D_MODEL
D_FF = 2816
EPS = 1e-6
N_CHIPS = 4
N_DEV = 8
LANES = 128

ADAM_LR, ADAM_B1, ADAM_B2, ADAM_EPS, ADAM_WD, ADAM_STEP = 0.001, 0.9, 0.999, 1e-08, 0.01, 10

VMEM_LIMIT = 56 * 1024 * 1024


def _cp(n_axes):
    return pltpu.CompilerParams(dimension_semantics=("arbitrary",) * n_axes, vmem_limit_bytes=VMEM_LIMIT)


def _sds(shape, dtype):
    return jax.ShapeDtypeStruct(tuple(shape), dtype)


def _sigmoid(v):
    return 1.0 / (1.0 + jnp.exp(-v))


def _bf(v):
    return v.astype(BF16)


def _dot(a, b, dims):
    return lax.dot_general(a, b, (dims, ((), ())), preferred_element_type=F32)


NN = ((1,), (0,))
NT = ((1,), (1,))
TN = ((0,), (0,))


def _mm_nn_blk(a, wg, name, tm=512):
    M, K = a.shape
    nb, _, Nb = wg.shape

    def body(a_ref, w_ref, o_ref):
        o_ref[...] = _dot(_bf(a_ref[...]), w_ref[...], NN)

    return pl.pallas_call(
        body, name=name, grid=(nb, M // tm),
        in_specs=[pl.BlockSpec((tm, K), lambda j, i: (i, 0)), pl.BlockSpec((None, K, Nb), lambda j, i: (j, 0, 0))],
        out_specs=pl.BlockSpec((tm, Nb), lambda j, i: (i, j)),
        out_shape=_sds((M, nb * Nb), F32), compiler_params=_cp(2))(a, wg)


def _mm_nt_blk(dy, wg, name, tm=1024):
    M = dy.shape[0]
    nb, K, Nb = wg.shape

    def body(dy_ref, w_ref, o_ref):
        j = pl.program_id(1)
        r = _dot(_bf(dy_ref[...]), w_ref[...], NT)

        @pl.when(j == 0)
        def _():
            o_ref[...] = r

        @pl.when(j > 0)
        def _():
            o_ref[...] += r

    return pl.pallas_call(
        body, name=name, grid=(M // tm, nb),
        in_specs=[pl.BlockSpec((tm, Nb), lambda i, j: (i, j)), pl.BlockSpec((None, K, Nb), lambda i, j: (j, 0, 0))],
        out_specs=pl.BlockSpec((tm, K), lambda i, j: (i, 0)),
        out_shape=_sds((M, K), F32), compiler_params=_cp(2))(dy, wg)


def _mm_tn_blk(x, dy, nb, name, tk=512):
    T, Mx = x.shape
    Nb = dy.shape[1] // nb

    def body(x_ref, dy_ref, o_ref):
        t = pl.program_id(1)
        r = _dot(_bf(x_ref[...]), _bf(dy_ref[...]), TN)

        @pl.when(t == 0)
        def _():
            o_ref[...] = r

        @pl.when(t > 0)
        def _():
            o_ref[...] += r

    return pl.pallas_call(
        body, name=name, grid=(nb, T // tk),
        in_specs=[pl.BlockSpec((tk, Mx), lambda j, t: (t, 0)), pl.BlockSpec((tk, Nb), lambda j, t: (t, j))],
        out_specs=pl.BlockSpec((None, Mx, Nb), lambda j, t: (j, 0, 0)),
        out_shape=_sds((nb, Mx, Nb), F32), compiler_params=_cp(2))(x, dy)


def _mm_nn(a, w, name, tm=512):
    M, K = a.shape
    N = w.shape[1]

    def body(a_ref, w_ref, o_ref):
        o_ref[...] = _dot(_bf(a_ref[...]), w_ref[...], NN)

    return pl.pallas_call(
        body, name=name, grid=(M // tm,),
        in_specs=[pl.BlockSpec((tm, K), lambda i: (i, 0)), pl.BlockSpec((K, N), lambda i: (0, 0))],
        out_specs=pl.BlockSpec((tm, N), lambda i: (i, 0)),
        out_shape=_sds((M, N), F32), compiler_params=_cp(1))(a, w)


def _mm_nt(dy, w, name, tm=512):
    M, N = dy.shape
    K = w.shape[0]

    def body(dy_ref, w_ref, o_ref):
        o_ref[...] = _dot(_bf(dy_ref[...]), w_ref[...], NT)

    return pl.pallas_call(
        body, name=name, grid=(M // tm,),
        in_specs=[pl.BlockSpec((tm, N), lambda i: (i, 0)), pl.BlockSpec((K, N), lambda i: (0, 0))],
        out_specs=pl.BlockSpec((tm, K), lambda i: (i, 0)),
        out_shape=_sds((M, K), F32), compiler_params=_cp(1))(dy, w)


def _mm_tn(x, dy, name, tk=512):
    T, Mx = x.shape
    N = dy.shape[1]

    def body(x_ref, dy_ref, o_ref):
        t = pl.program_id(0)
        r = _dot(_bf(x_ref[...]), _bf(dy_ref[...]), TN)

        @pl.when(t == 0)
        def _():
            o_ref[...] = r

        @pl.when(t > 0)
        def _():
            o_ref[...] += r

    return pl.pallas_call(
        body, name=name, grid=(T // tk,),
        in_specs=[pl.BlockSpec((tk, Mx), lambda t: (t, 0)), pl.BlockSpec((tk, N), lambda t: (t, 0))],
        out_specs=pl.BlockSpec((Mx, N), lambda t: (0, 0)),
        out_shape=_sds((Mx, N), F32), compiler_params=_cp(1))(x, dy)


def _tile(arr, bw, col=lambda c: 0):
    return ("tile", arr, bw, col)


def _full(arr):
    return ("full", arr)


def _out_tile(width, dtype, bw, col=lambda c: 0):
    return ("tile", width, dtype, bw, col)


def _out_acc(rows, width, bw, col=lambda c: 0):
    return ("acc", rows, width, bw, col)


def _rows_call(name, body, n_rows, tm, ncol, ins, outs):
    in_specs, args = [], []
    for e in ins:
        if e[0] == "tile":
            _, arr, bw, col = e
            in_specs.append(pl.BlockSpec((tm, bw), functools.partial(lambda c, i, col: (i, col(c)), col=col)))
        else:
            arr = e[1]
            in_specs.append(pl.BlockSpec(arr.shape, functools.partial(lambda c, i, nd: (0,) * nd, nd=arr.ndim)))
        args.append(arr)
    out_specs, out_shape = [], []
    for e in outs:
        if e[0] == "tile":
            _, width, dtype, bw, col = e
            out_specs.append(pl.BlockSpec((tm, bw), functools.partial(lambda c, i, col: (i, col(c)), col=col)))
            out_shape.append(_sds((n_rows, width), dtype))
        else:
            _, rows, width, bw, col = e
            out_specs.append(pl.BlockSpec((rows, bw), functools.partial(lambda c, i, col: (0, col(c)), col=col)))
            out_shape.append(_sds((rows, width), F32))
    return pl.pallas_call(
        body, name=name, grid=(ncol, n_rows // tm), in_specs=in_specs, out_specs=out_specs,
        out_shape=out_shape, compiler_params=_cp(2))(*args)


def _acc(ref, val):
    i = pl.program_id(1)

    @pl.when(i == 0)
    def _():
        ref[...] = val

    @pl.when(i > 0)
    def _():
        ref[...] += val


def _rinv(z):
    return lax.rsqrt(jnp.mean(z * z, axis=-1, keepdims=True) + EPS)


def _norm_bwd(dy, zhat, r, w):
    dyw = dy * w
    return r * (dyw - zhat * jnp.mean(dyw * zhat, axis=-1, keepdims=True))


def _norm_fwd(x, w, name):
    def body(x_ref, w_ref, h_ref):
        xv = x_ref[...]
        h_ref[...] = _bf(xv * _rinv(xv) * w_ref[...])

    return _rows_call(name, body, x.shape[0], 512, 1, [_tile(x, D_MODEL), _full(w)],
                      [_out_tile(D_MODEL, BF16, D_MODEL)])[0]


def _prenorm_bwd(dh, xin, w, dres, name):
    def body(dh_ref, x_ref, w_ref, dres_ref, dx_ref, dw_ref):
        xv = x_ref[...]
        r = _rinv(xv)
        xhat = xv * r
        dhv = dh_ref[...]
        dx_ref[...] = dres_ref[...] + _norm_bwd(dhv, xhat, r, w_ref[...])
        _acc(dw_ref, jnp.sum(dhv * xhat, axis=0, keepdims=True))

    return _rows_call(name, body, xin.shape[0], 512, 1,
                      [_tile(dh, D_MODEL), _tile(xin, D_MODEL), _full(w), _tile(dres, D_MODEL)],
                      [_out_tile(D_MODEL, F32, D_MODEL), _out_acc(1, D_MODEL, D_MODEL)])


def _postnorm_bwd(dout, z, w, name):
    def body(do_ref, z_ref, w_ref, dz_ref, dw_ref):
        zv = z_ref[...]
        r = _rinv(zv)
        zhat = zv * r
        dov = do_ref[...]
        dz_ref[...] = _bf(_norm_bwd(dov, zhat, r, w_ref[...]))
        _acc(dw_ref, jnp.sum(dov * zhat, axis=0, keepdims=True))

    return _rows_call(name, body, z.shape[0], 512, 1, [_tile(dout, D_MODEL), _tile(z, D_MODEL), _full(w)],
                      [_out_tile(D_MODEL, BF16, D_MODEL), _out_acc(1, D_MODEL, D_MODEL)])


def _t5_bucket(dist):
    n = jnp.maximum(dist, 0)
    nf = jnp.maximum(n, 1).astype(F32)
    large = MAX_EXACT + (jnp.log(nf / MAX_EXACT) / math.log(MAX_DISTANCE / MAX_EXACT)
                         * (NUM_BUCKETS - MAX_EXACT)).astype(jnp.int32)
    large = jnp.minimum(large, NUM_BUCKETS - 1)
    return jnp.where(n < MAX_EXACT, n, large)


def _band_rel():
    return jnp.arange(BLK)[:, None] + BLK - jnp.arange(2 * BLK)[None, :]


def _band_mask(n):
    row = lax.broadcasted_iota(jnp.int32, (BLK, 2 * BLK), 0)
    col = lax.broadcasted_iota(jnp.int32, (BLK, 2 * BLK), 1)
    rel = row + BLK - col
    return (rel >= 0) & (rel <= BLK) & ((col >= BLK) | (n > 0))


PAIR = 2
PAIR_W = PAIR * HEAD_DIM


def _sub_rows(r, d):
    return pl.ds(r, BLK, stride=d) if d > 1 else pl.ds(0, BLK)


def _for_residues(d, fn):
    if d == 1:
        fn(0)
    else:
        lax.fori_loop(0, d, lambda r, carry: (fn(r), carry)[1], 0)


def _attn_specs(d, g, qblock):
    def col(part, hp):
        return (g * 3 + part) * (GROUP_W // PAIR_W) + hp

    def cur(part):
        return pl.BlockSpec((d * BLK, PAIR_W), lambda hp, n: (qblock(n), col(part, hp)))

    def prev(part):
        return pl.BlockSpec((d * BLK, PAIR_W), lambda hp, n: (jnp.maximum(qblock(n) - 1, 0), col(part, hp)))

    return cur, prev


def _attn_fwd(proj, bias, g, name):
    S = proj.shape[0]
    d = DILATIONS[g]
    NB = S // (d * BLK)

    def body(q_ref, kp_ref, kc_ref, vp_ref, vc_ref, b_ref, o_ref, lse_ref):
        hp = pl.program_id(0)
        mask = _band_mask(pl.program_id(1))

        def residue(r):
            rows = _sub_rows(r, d)
            q2 = q_ref[rows, :]
            k2 = jnp.concatenate([kp_ref[rows, :], kc_ref[rows, :]], axis=0)
            v2 = jnp.concatenate([vp_ref[rows, :], vc_ref[rows, :]], axis=0)
            outs, lses = [], []
            for hh in range(PAIR):
                hs = slice(hh * HEAD_DIM, (hh + 1) * HEAD_DIM)
                s = _dot(_bf(q2[:, hs]), _bf(k2[:, hs]), NT) * (HEAD_DIM ** -0.5) + b_ref[hp * PAIR + hh]
                s = jnp.where(mask, s, NEG_INF)
                m = jnp.max(s, axis=-1, keepdims=True)
                p = jnp.exp(s - m)
                l = jnp.sum(p, axis=-1, keepdims=True)
                outs.append(_dot(_bf(p), _bf(v2[:, hs]), NN) / l)
                lses.append(jnp.broadcast_to(m + jnp.log(l), (BLK, HEAD_DIM)))
            o_ref[rows, :] = jnp.concatenate(outs, axis=1)
            lse_ref[rows, :] = jnp.concatenate(lses, axis=1)

        _for_residues(d, residue)

    cur, prev = _attn_specs(d, g, lambda n: n)
    out = pl.BlockSpec((d * BLK, PAIR_W), lambda hp, n: (n, hp))
    return pl.pallas_call(
        body, name=name, grid=(HEADS // PAIR, NB),
        in_specs=[cur(0), prev(1), cur(1), prev(2), cur(2),
                  pl.BlockSpec((HEADS, BLK, 2 * BLK), lambda hp, n: (0, 0, 0))],
        out_specs=[out, out], out_shape=[_sds((S, GROUP_W), F32)] * 2,
        compiler_params=_cp(2))(proj, proj, proj, proj, proj, bias)


def _attn_merge(os_, lses, name):
    def body(o0, o1, o2, l0, l1, l2, y_ref, lse_ref):
        a, b, c = l0[...], l1[...], l2[...]
        m = jnp.maximum(jnp.maximum(a, b), c)
        ea, eb, ec = jnp.exp(a - m), jnp.exp(b - m), jnp.exp(c - m)
        den = ea + eb + ec
        y_ref[...] = (ea * o0[...] + eb * o1[...] + ec * o2[...]) / den
        lse_ref[...] = m + jnp.log(den)

    S = os_[0].shape[0]
    return _rows_call(name, body, S, 512, 1, [_tile(t, GROUP_W) for t in (*os_, *lses)],
                      [_out_tile(GROUP_W, F32, GROUP_W)] * 2)


def _attn_bwd(proj, bias, lse, y, dy, g, name):
    S = proj.shape[0]
    d = DILATIONS[g]
    NB = S // (d * BLK)

    def body(q_ref, kp_ref, kc_ref, vp_ref, vc_ref, b_ref, l_ref, y_ref, dy_ref,
             dq_ref, dk_ref, dv_ref, db_ref, ck_ref, cv_ref):
        hp, n = pl.program_id(0), pl.program_id(1)

        @pl.when((hp == 0) & (n == 0))
        def _():
            db_ref[...] = jnp.zeros_like(db_ref)

        @pl.when(n == 0)
        def _():
            ck_ref[...] = jnp.zeros_like(ck_ref)
            cv_ref[...] = jnp.zeros_like(cv_ref)

        @pl.when(n < NB)
        def _():
            mask = _band_mask(n)

            def residue(r):
                rows = _sub_rows(r, d)
                q2 = q_ref[rows, :]
                k2 = jnp.concatenate([kp_ref[rows, :], kc_ref[rows, :]], axis=0)
                v2 = jnp.concatenate([vp_ref[rows, :], vc_ref[rows, :]], axis=0)
                l2, y2, dy2 = l_ref[rows, :], y_ref[rows, :], dy_ref[rows, :]
                dqs, dks, dvs = [], [], []
                for hh in range(PAIR):
                    hs = slice(hh * HEAD_DIM, (hh + 1) * HEAD_DIM)
                    q, kb, vb = _bf(q2[:, hs]), _bf(k2[:, hs]), _bf(v2[:, hs])
                    s = _dot(q, kb, NT) * (HEAD_DIM ** -0.5) + b_ref[hp * PAIR + hh]
                    s = jnp.where(mask, s, NEG_INF)
                    p = jnp.exp(s - l2[:, hh * HEAD_DIM:hh * HEAD_DIM + 1])
                    dyh = dy2[:, hs]
                    delta = jnp.sum(dyh * y2[:, hs], axis=-1, keepdims=True)
                    dyb = _bf(dyh)
                    ds = p * (_dot(dyb, vb, NT) - delta)
                    db_ref[hp * PAIR + hh] += ds
                    dsb = _bf(ds * (HEAD_DIM ** -0.5))
                    dqs.append(_dot(dsb, kb, NN))
                    dks.append(_dot(dsb, q, TN))
                    dvs.append(_dot(_bf(p), dyb, TN))
                dkb = jnp.concatenate(dks, axis=1)
                dvb = jnp.concatenate(dvs, axis=1)
                dq_ref[rows, :] = jnp.concatenate(dqs, axis=1)
                dk_ref[rows, :] = ck_ref[rows, :] + dkb[:BLK]
                dv_ref[rows, :] = cv_ref[rows, :] + dvb[:BLK]
                ck_ref[rows, :] = dkb[BLK:]
                cv_ref[rows, :] = dvb[BLK:]

            _for_residues(d, residue)

        @pl.when(n == NB)
        def _():
            dk_ref[...] = ck_ref[...]
            dv_ref[...] = cv_ref[...]

    def qn(n):
        return jnp.minimum(n, NB - 1)

    cur, prev = _attn_specs(d, g, qn)
    row = pl.BlockSpec((d * BLK, PAIR_W), lambda hp, n: (qn(n), hp))
    done = pl.BlockSpec((d * BLK, PAIR_W), lambda hp, n: (jnp.maximum(n - 1, 0), hp))
    dq, dk, dv, db = pl.pallas_call(
        body, name=name, grid=(HEADS // PAIR, NB + 1),
        in_specs=[cur(0), prev(1), cur(1), prev(2), cur(2),
                  pl.BlockSpec((HEADS, BLK, 2 * BLK), lambda hp, n: (0, 0, 0)), row, row, row],
        out_specs=[row, done, done, pl.BlockSpec((HEADS, BLK, 2 * BLK), lambda hp, n: (0, 0, 0))],
        out_shape=[_sds((S, GROUP_W), F32)] * 3 + [_sds((HEADS, BLK, 2 * BLK), F32)],
        scratch_shapes=[pltpu.VMEM((d * BLK, PAIR_W), F32)] * 2,
        compiler_params=_cp(2))(proj, proj, proj, proj, proj, bias, lse, y, dy)
    return [dq, dk, dv], db


BAND = BLK * 2 * BLK


def _bucket_onehot():
    buckets = jnp.stack([_t5_bucket(_band_rel() * d) for d in DILATIONS]).reshape(N_GROUPS, 1, BAND)
    return (buckets == jnp.arange(NUM_BUCKETS).reshape(1, NUM_BUCKETS, 1)).astype(F32)


def _relbias_fwd(rel_bias, name):
    table = rel_bias.reshape(NUM_BUCKETS, N_GROUPS, HEADS).transpose(1, 0, 2)

    def body(t_ref, oh_ref, o_ref):
        o_ref[...] = lax.dot_general(t_ref[...], oh_ref[...], (TN, ((), ())), preferred_element_type=F32,
                                     precision=lax.Precision.HIGHEST)

    out = pl.pallas_call(
        body, name=name, grid=(N_GROUPS,),
        in_specs=[pl.BlockSpec((None, NUM_BUCKETS, HEADS), lambda g: (g, 0, 0)),
                  pl.BlockSpec((None, NUM_BUCKETS, BAND), lambda g: (g, 0, 0))],
        out_specs=pl.BlockSpec((None, HEADS, BAND), lambda g: (g, 0, 0)),
        out_shape=_sds((N_GROUPS, HEADS, BAND), F32), compiler_params=_cp(1))(table, _bucket_onehot())
    return out.reshape(N_GROUPS, HEADS, BLK, 2 * BLK)


def _relbias_bwd(dbs, name):
    band = BAND
    onehot = _bucket_onehot()
    dbf = jnp.stack([db.reshape(HEADS, band) for db in dbs])

    def body(oh_ref, db_ref, o_ref):
        o_ref[...] = lax.dot_general(oh_ref[...], db_ref[...], (NT, ((), ())), preferred_element_type=F32,
                                     precision=lax.Precision.HIGHEST)

    out = pl.pallas_call(
        body, name=name, grid=(N_GROUPS,),
        in_specs=[pl.BlockSpec((None, NUM_BUCKETS, band), lambda g: (g, 0, 0)),
                  pl.BlockSpec((None, HEADS, band), lambda g: (g, 0, 0))],
        out_specs=pl.BlockSpec((None, NUM_BUCKETS, HEADS), lambda g: (g, 0, 0)),
        out_shape=_sds((N_GROUPS, NUM_BUCKETS, HEADS), F32), compiler_params=_cp(1))(onehot, dbf)
    return out.transpose(1, 0, 2).reshape(NUM_BUCKETS, N_GROUPS * HEADS)


def _chunk_pos(shape):
    return lax.broadcasted_iota(jnp.int32, shape, 0) % HG_CHUNK


def _chunk_cumsum(v):
    pos = _chunk_pos(v.shape)
    s = 1
    while s < HG_CHUNK:
        v = v + jnp.where(pos >= s, pltpu.roll(v, s, 0), 0.0)
        s *= 2
    return v


def _chunk_rev_cumsum(v):
    pos = _chunk_pos(v.shape)
    n = v.shape[0]
    s = 1
    while s < HG_CHUNK:
        v = v + jnp.where(pos < HG_CHUNK - s, pltpu.roll(v, n - s, 0), 0.0)
        s *= 2
    return v


def _lower_bound(raw):
    a0, a1 = raw[0:1], raw[1:2]
    m = jnp.maximum(a0, a1)
    e0, e1 = jnp.exp(a0 - m), jnp.exp(a1 - m)
    return e0 / (e0 + e1)


def _hg_gates(qr, fr, lb):
    sf = _sigmoid(fr)
    f = lb + (1.0 - lb) * sf
    sq = _sigmoid(qr)
    return qr * sq, sq, f, sf


HG_COL0 = QKV_W // HG_W


def _hgrn_fwd(proj, lb_raw, nw, name):
    S = proj.shape[0]
    ncs = HG_TILE // HG_CHUNK
    tril = jnp.tril(jnp.ones((HG_CHUNK, HG_CHUNK), dtype=bool))

    def body(q_ref, f_ref, i_ref, og_ref, lb_ref, nw_ref, y_ref, o_ref, st_ref, state):
        @pl.when(pl.program_id(0) == 0)
        def _():
            state[...] = jnp.zeros_like(state)

        lb = _lower_bound(lb_ref[...])
        q, _, f, _ = _hg_gates(q_ref[...], f_ref[...], lb)
        k = 1.0 - f
        G = _chunk_cumsum(jnp.log(f))
        row = lax.broadcasted_iota(jnp.int32, (HG_CHUNK, HG_CHUNK), 0)
        col = lax.broadcasted_iota(jnp.int32, (HG_CHUNK, HG_CHUNK), 1)
        for h in range(HG_HEADS):
            hs = slice(h * HG_DK, (h + 1) * HG_DK)
            st = state[h]
            for c in range(ncs):
                cs = slice(c * HG_CHUNK, (c + 1) * HG_CHUNK)
                Gc = G[cs, hs]
                gl = Gc[HG_CHUNK - 1:HG_CHUNK]
                qt = _bf(q[cs, hs] * jnp.exp(Gc))
                kt = _bf(k[cs, hs] * jnp.exp(-Gc))
                kd = _bf(k[cs, hs] * jnp.exp(gl - Gc))
                v = _bf(i_ref[cs, hs])
                A = jnp.where(row >= col, _dot(qt, kt, NT), 0.0)
                o_ref[cs, hs] = _dot(_bf(A), v, NN) + _dot(qt, _bf(st), NT)
                st_ref[c, h] = st
                st = st * jnp.exp(gl) + _dot(v, kd, TN)
            state[h] = st
            oh = o_ref[:, hs]
            og = og_ref[:, hs]
            y_ref[:, hs] = oh * _rinv(oh) * nw_ref[...] * (og * _sigmoid(og))

    def colspec(j):
        return pl.BlockSpec((HG_TILE, HG_W), lambda i: (i, HG_COL0 + j))

    return pl.pallas_call(
        body, name=name, grid=(S // HG_TILE,),
        in_specs=[colspec(0), colspec(1), colspec(2), colspec(3),
                  pl.BlockSpec((2, HG_W), lambda i: (0, 0)), pl.BlockSpec((1, HG_DK), lambda i: (0, 0))],
        out_specs=[pl.BlockSpec((HG_TILE, HG_W), lambda i: (i, 0))] * 2
        + [pl.BlockSpec((ncs, HG_HEADS, HG_DK, HG_DK), lambda i: (i, 0, 0, 0))],
        out_shape=[_sds((S, HG_W), F32)] * 2 + [_sds((S // HG_CHUNK, HG_HEADS, HG_DK, HG_DK), F32)],
        scratch_shapes=[pltpu.VMEM((HG_HEADS, HG_DK, HG_DK), F32)],
        compiler_params=_cp(1))(proj, proj, proj, proj, lb_raw, nw)


def _hgrn_bwd(proj, lb_raw, nw, o, states, dy, name):
    S = proj.shape[0]
    ncs = HG_TILE // HG_CHUNK
    nt = S // HG_TILE

    def body(q_ref, f_ref, i_ref, og_ref, lb_ref, nw_ref, o_ref, st_ref, dy_ref,
             dq_ref, df_ref, di_ref, dog_ref, dlb_ref, dnw_ref, dstate, do_s, dG_s, dgl_s, dk_s, dlb_s):
        step = pl.program_id(0)

        @pl.when(step == 0)
        def _():
            dstate[...] = jnp.zeros_like(dstate)
            dlb_s[...] = jnp.zeros_like(dlb_s)
            dnw_ref[...] = jnp.zeros_like(dnw_ref)

        lb = _lower_bound(lb_ref[...])
        qr = q_ref[...]
        q, sq, f, sf = _hg_gates(qr, f_ref[...], lb)
        k = 1.0 - f
        G = _chunk_cumsum(jnp.log(f))
        nwv = nw_ref[...]
        row = lax.broadcasted_iota(jnp.int32, (HG_CHUNK, HG_CHUNK), 0)
        col = lax.broadcasted_iota(jnp.int32, (HG_CHUNK, HG_CHUNK), 1)
        for h in range(HG_HEADS):
            hs = slice(h * HG_DK, (h + 1) * HG_DK)
            oh = o_ref[:, hs]
            r = _rinv(oh)
            ohat = oh * r
            og = og_ref[:, hs]
            sg = _sigmoid(og)
            dyh = dy_ref[:, hs]
            don = dyh * (og * sg)
            dog_ref[:, hs] = _bf(dyh * (ohat * nwv) * (sg * (1.0 + og * (1.0 - sg))))
            dnw_ref[...] += jnp.sum(don * ohat, axis=0, keepdims=True)
            do_s[:, hs] = _norm_bwd(don, ohat, r, nwv)
            dst = dstate[h]
            for c in reversed(range(ncs)):
                cs = slice(c * HG_CHUNK, (c + 1) * HG_CHUNK)
                Gc = G[cs, hs]
                gl = Gc[HG_CHUNK - 1:HG_CHUNK]
                eG, enG, edG, egl = jnp.exp(Gc), jnp.exp(-Gc), jnp.exp(gl - Gc), jnp.exp(gl)
                qt, kt, kd = q[cs, hs] * eG, k[cs, hs] * enG, k[cs, hs] * edG
                qtb, ktb, kdb = _bf(qt), _bf(kt), _bf(kd)
                v = _bf(i_ref[cs, hs])
                do = _bf(do_s[cs, hs])
                st = st_ref[c, h]
                dstb = _bf(dst)
                A = jnp.where(row >= col, _dot(qtb, ktb, NT), 0.0)
                dA = _bf(jnp.where(row >= col, _dot(do, v, NT), 0.0))
                di_ref[cs, hs] = _bf(_dot(_bf(A), do, TN) + _dot(kdb, dstb, NT))
                dqt = _dot(dA, ktb, NN) + _dot(do, _bf(st), NN)
                dkt = _dot(dA, qtb, TN)
                dkd = _dot(v, dstb, NN)
                dgl = egl * jnp.sum(st * dst, axis=0, keepdims=True) + jnp.sum(dkd * kd, axis=0, keepdims=True)
                dst = dst * egl + _dot(do, qtb, TN)
                dq_ref[cs, hs] = _bf(dqt * eG * (sq[cs, hs] * (1.0 + qr[cs, hs] * (1.0 - sq[cs, hs]))))
                dk_s[cs, hs] = dkt * enG + dkd * edG
                dG_s[cs, hs] = dqt * qt - dkt * kt - dkd * kd
                dgl_s[cs, hs] = jnp.broadcast_to(dgl, (HG_CHUNK, HG_DK))
            dstate[h] = dst
        dg = _chunk_rev_cumsum(dG_s[...]) + dgl_s[...]
        dfv = dg / f - dk_s[...]
        df_ref[...] = _bf(dfv * (1.0 - lb) * sf * (1.0 - sf))
        dlb_s[...] += jnp.sum(dfv * (1.0 - sf), axis=0, keepdims=True)

        @pl.when(step == nt - 1)
        def _():
            t = dlb_s[...] * lb * (1.0 - lb)
            dlb_ref[...] = jnp.concatenate([t, -t], axis=0)

    def colspec(j):
        return pl.BlockSpec((HG_TILE, HG_W), lambda i: (nt - 1 - i, HG_COL0 + j))

    tile = pl.BlockSpec((HG_TILE, HG_W), lambda i: (nt - 1 - i, 0))
    outs = pl.pallas_call(
        body, name=name, grid=(nt,),
        in_specs=[colspec(0), colspec(1), colspec(2), colspec(3),
                  pl.BlockSpec((2, HG_W), lambda i: (0, 0)), pl.BlockSpec((1, HG_DK), lambda i: (0, 0)),
                  tile, pl.BlockSpec((ncs, HG_HEADS, HG_DK, HG_DK), lambda i: (nt - 1 - i, 0, 0, 0)), tile],
        out_specs=[tile] * 4 + [pl.BlockSpec((2, HG_W), lambda i: (0, 0)), pl.BlockSpec((1, HG_DK), lambda i: (0, 0))],
        out_shape=[_sds((S, HG_W), BF16)] * 4 + [_sds((2, HG_W), F32), _sds((1, HG_DK), F32)],
        scratch_shapes=[pltpu.VMEM((HG_HEADS, HG_DK, HG_DK), F32)] + [pltpu.VMEM((HG_TILE, HG_W), F32)] * 4
        + [pltpu.VMEM((1, HG_W), F32)],
        compiler_params=_cp(1))(proj, proj, proj, proj, lb_raw, nw, o, states, dy)
    return outs[:4], outs[4], outs[5]


GATE_COL0 = (QKV_W + 4 * HG_W) // GROUP_W
HALF_D = D_MODEL // 2


def _gate_tiles(proj):
    return [_tile(proj, HALF_D, lambda c: GATE_COL0 + c), _tile(proj, HALF_D, lambda c: GATE_COL0 + 2 + c)]


def _merge_fwd(za, zh, proj, name):
    def body(za_ref, zh_ref, g0_ref, g1_ref, m_ref):
        m_ref[...] = _bf(_sigmoid(g0_ref[...]) * za_ref[...] + _sigmoid(g1_ref[...]) * zh_ref[...])

    col = lambda c: c
    return _rows_call(name, body, za.shape[0], 512, 2,
                      [_tile(za, HALF_D, col), _tile(zh, HALF_D, col), *_gate_tiles(proj)],
                      [_out_tile(D_MODEL, BF16, HALF_D, col)])[0]


def _merge_bwd(dm, za, zh, proj, name):
    def body(dm_ref, za_ref, zh_ref, g0_ref, g1_ref, dza_ref, dzh_ref, dg0_ref, dg1_ref):
        dmv = dm_ref[...]
        s0, s1 = _sigmoid(g0_ref[...]), _sigmoid(g1_ref[...])
        dza_ref[...] = _bf(dmv * s0)
        dzh_ref[...] = _bf(dmv * s1)
        dg0_ref[...] = _bf(dmv * za_ref[...] * s0 * (1.0 - s0))
        dg1_ref[...] = _bf(dmv * zh_ref[...] * s1 * (1.0 - s1))

    col = lambda c: c
    return _rows_call(name, body, za.shape[0], 512, 2,
                      [_tile(dm, HALF_D, col), _tile(za, HALF_D, col), _tile(zh, HALF_D, col), *_gate_tiles(proj)],
                      [_out_tile(D_MODEL, BF16, HALF_D, col)] * 4)


def _mix_out(mo, x, w_post, w_pre, name):
    def body(mo_ref, x_ref, wp_ref, wf_ref, x1_ref, h2_ref):
        z = mo_ref[...]
        x1 = x_ref[...] + z * _rinv(z) * wp_ref[...]
        x1_ref[...] = x1
        h2_ref[...] = _bf(x1 * _rinv(x1) * wf_ref[...])

    return _rows_call(name, body, x.shape[0], 512, 1,
                      [_tile(mo, D_MODEL), _tile(x, D_MODEL), _full(w_post), _full(w_pre)],
                      [_out_tile(D_MODEL, F32, D_MODEL), _out_tile(D_MODEL, BF16, D_MODEL)])


def _loss_head(ffo, x1, tgt, w, name):
    def body(f_ref, x1_ref, t_ref, w_ref, dx_ref, df_ref, dw_ref, loss_ref):
        z = f_ref[...]
        r = _rinv(z)
        zhat = z * r
        wv = w_ref[...]
        e = x1_ref[...] + zhat * wv - t_ref[...]
        dx = e * (1.0 / D_MODEL)
        dx_ref[...] = dx
        df_ref[...] = _bf(_norm_bwd(dx, zhat, r, wv))
        _acc(dw_ref, jnp.sum(dx * zhat, axis=0, keepdims=True))
        part = 0.5 * jnp.sum(jnp.sum(e * e, axis=1, keepdims=True), axis=0, keepdims=True) * (1.0 / D_MODEL)
        _acc(loss_ref, jnp.broadcast_to(part, (1, LANES)))

    return _rows_call(name, body, x1.shape[0], 512, 1,
                      [_tile(ffo, D_MODEL), _tile(x1, D_MODEL), _tile(tgt, D_MODEL), _full(w)],
                      [_out_tile(D_MODEL, F32, D_MODEL), _out_tile(D_MODEL, BF16, D_MODEL),
                       _out_acc(1, D_MODEL, D_MODEL), _out_acc(1, LANES, LANES)])


CONV_CB = D_FF // 2
CONV_TM = 512
HALO = 8
SQRT_HALF = 0.7071067811865476
INV_SQRT_2PI = 0.3989422804014327


def _conv_taps(u_ref, halo_ref, first):
    u = u_ref[...]
    row = lax.broadcasted_iota(jnp.int32, u.shape, 0)
    p1 = jnp.where(first, 0.0, halo_ref[HALO - 1:HALO, :])
    p2 = jnp.where(first, 0.0, halo_ref[HALO - 2:HALO - 1, :])
    u1 = jnp.where(row == 0, p1, pltpu.roll(u, 1, 0))
    u2 = jnp.where(row == 0, p2, jnp.where(row == 1, p1, pltpu.roll(u, 2, 0)))
    return u2, u1, u


def _conv(taps, w_ref, b_ref):
    return b_ref[...] + w_ref[0:1, :] * taps[0] + w_ref[1:2, :] * taps[1] + w_ref[2:3, :] * taps[2]


def _conv_specs(tm):
    nh = tm // HALO
    nc = D_FF // CONV_CB

    def tile(off):
        return pl.BlockSpec((tm, CONV_CB), lambda c, i: (i, off + c))

    def halo(off):
        return pl.BlockSpec((HALO, CONV_CB), lambda c, i: (jnp.maximum(i * nh - 1, 0), off + c))

    def small(rows, off):
        return pl.BlockSpec((rows, CONV_CB), lambda c, i: (0, off + c))

    return nc, tile, halo, small


def _conv_gelu_fwd(u, cw, cb, name):
    S = u.shape[0]
    tm = CONV_TM
    nc, tile, halo, small = _conv_specs(tm)

    def body(ug, hg, uv, hv, wg, wv, bg, bv, a_ref):
        first = pl.program_id(1) == 0
        cg = _conv(_conv_taps(ug, hg, first), wg, bg)
        cv = _conv(_conv_taps(uv, hv, first), wv, bv)
        a_ref[...] = _bf(0.5 * cg * (1.0 + lax.erf(cg * SQRT_HALF)) * cv)

    return pl.pallas_call(
        body, name=name, grid=(nc, S // tm),
        in_specs=[tile(0), halo(0), tile(nc), halo(nc), small(3, 0), small(3, nc), small(1, 0), small(1, nc)],
        out_specs=tile(0), out_shape=_sds((S, D_FF), BF16), compiler_params=_cp(2))(u, u, u, u, cw, cw, cb, cb)


def _conv_gelu_bwd(u, da, cw, cb, name):
    S = u.shape[0]
    tm = CONV_TM
    nc, tile, halo, small = _conv_specs(tm)

    def body(ug, hg, uv, hv, wg, wv, bg, bv, da_ref, dcg_ref, dcv_ref, dwg_ref, dwv_ref, dbg_ref, dbv_ref):
        first = pl.program_id(1) == 0
        tg = _conv_taps(ug, hg, first)
        tv = _conv_taps(uv, hv, first)
        cg = _conv(tg, wg, bg)
        cv = _conv(tv, wv, bv)
        phi = 0.5 * (1.0 + lax.erf(cg * SQRT_HALF))
        dav = da_ref[...]
        dcg = dav * cv * (phi + cg * jnp.exp(-0.5 * cg * cg) * INV_SQRT_2PI)
        dcv = dav * (cg * phi)
        dcg_ref[...] = dcg
        dcv_ref[...] = dcv
        for dc, taps, dw_ref, db_ref in ((dcg, tg, dwg_ref, dbg_ref), (dcv, tv, dwv_ref, dbv_ref)):
            _acc(db_ref, jnp.sum(dc, axis=0, keepdims=True))
            for j in range(3):
                _acc(dw_ref.at[j:j + 1, :], jnp.sum(dc * taps[j], axis=0, keepdims=True))

    return pl.pallas_call(
        body, name=name, grid=(nc, S // tm),
        in_specs=[tile(0), halo(0), tile(nc), halo(nc), small(3, 0), small(3, nc), small(1, 0), small(1, nc), tile(0)],
        out_specs=[tile(0), tile(0), small(3, 0), small(3, 0), small(1, 0), small(1, 0)],
        out_shape=[_sds((S, D_FF), F32)] * 2 + [_sds((3, D_FF), F32)] * 2 + [_sds((1, D_FF), F32)] * 2,
        compiler_params=_cp(2))(u, u, u, u, cw, cw, cb, cb, da)


def _conv_input_bwd(dcg, dcv, cw, name):
    S = dcg.shape[0]
    tm = CONV_TM
    nc, tile, _, small = _conv_specs(tm)
    nh = tm // HALO
    nt = S // tm

    def nxt(off):
        return pl.BlockSpec((HALO, CONV_CB), lambda c, i: (jnp.minimum((i + 1) * nh, S // HALO - 1), off + c))

    def body(g_ref, ng_ref, v_ref, nv_ref, wg, wv, dug_ref, duv_ref):
        last = pl.program_id(1) == nt - 1
        for dc_ref, n_ref, w_ref, du_ref in ((g_ref, ng_ref, wg, dug_ref), (v_ref, nv_ref, wv, duv_ref)):
            dc = dc_ref[...]
            row = lax.broadcasted_iota(jnp.int32, dc.shape, 0)
            n1 = jnp.where(last, 0.0, n_ref[0:1, :])
            n2 = jnp.where(last, 0.0, n_ref[1:2, :])
            d1 = jnp.where(row == tm - 1, n1, pltpu.roll(dc, tm - 1, 0))
            d2 = jnp.where(row == tm - 1, n2, jnp.where(row == tm - 2, n1, pltpu.roll(dc, tm - 2, 0)))
            du_ref[...] = _bf(w_ref[2:3, :] * dc + w_ref[1:2, :] * d1 + w_ref[0:1, :] * d2)

    return pl.pallas_call(
        body, name=name, grid=(nc, nt),
        in_specs=[tile(0), nxt(0), tile(0), nxt(0), small(3, 0), small(3, nc)],
        out_specs=[tile(0), tile(0)], out_shape=[_sds((S, D_FF), BF16)] * 2,
        compiler_params=_cp(2))(dcg, dcg, dcv, dcv, cw, cw)


def _row_tile(n, cap):
    best = n
    for t in range(16, cap + 1, 16):
        if n % t == 0:
            best = t
    return best if best <= cap else n


def _adamw(w, g, m, v, name):
    R, C = w.shape
    tr = _row_tile(R, max(16, (512 * 1024) // (4 * C) // 16 * 16))

    def body(w_ref, g_ref, m_ref, v_ref, d_ref, nm_ref, nv_ref):
        gv = g_ref[...]
        nm = ADAM_B1 * m_ref[...] + (1.0 - ADAM_B1) * gv
        nv = ADAM_B2 * v_ref[...] + (1.0 - ADAM_B2) * (gv * gv)
        m_hat = nm / (1.0 - ADAM_B1 ** ADAM_STEP)
        v_hat = nv / (1.0 - ADAM_B2 ** ADAM_STEP)
        d_ref[...] = -ADAM_LR * (m_hat / (jnp.sqrt(v_hat) + ADAM_EPS) + ADAM_WD * w_ref[...])
        nm_ref[...] = nm
        nv_ref[...] = nv

    spec = pl.BlockSpec((tr, C), lambda i: (i, 0))
    return pl.pallas_call(body, name=name, grid=(R // tr,), in_specs=[spec] * 4, out_specs=[spec] * 3,
                          out_shape=[_sds((R, C), F32)] * 3, compiler_params=_cp(1))(w, g, m, v)


def _pair_sum(gfull, rcv, c_idx, name):
    nb, R, C = gfull.shape
    half = R // 2
    tr = _row_tile(half, 256)
    nt = half // tr

    def body(c_ref, g_ref, r_ref, o_ref):
        o_ref[...] = _bf(g_ref[...] + r_ref[...])

    return pl.pallas_call(
        body, name=name,
        grid_spec=pltpu.PrefetchScalarGridSpec(
            num_scalar_prefetch=1, grid=(nb, nt),
            in_specs=[pl.BlockSpec((None, tr, C), lambda j, i, c_ref: (j, c_ref[0] * nt + i, 0)),
                      pl.BlockSpec((None, tr, C), lambda j, i, c_ref: (j, i, 0))],
            out_specs=pl.BlockSpec((None, tr, C), lambda j, i, c_ref: (j, i, 0))),
        out_shape=_sds((nb, half, C), BF16), compiler_params=_cp(2))(c_idx, gfull, rcv)


def _chip_sum(arrived, own, place, name):
    nb, H, C = arrived.shape
    tr = _row_tile(H, 256)
    nt = H // tr

    def body(pl_ref, *refs):
        o_ref = refs[nb + 1]
        me = pl_ref[0]
        acc = None
        for k in range(nb):
            term = jnp.where(me == k, refs[nb][...], refs[k][...]).astype(F32)
            acc = term if acc is None else acc + term
        o_ref[...] = acc

    def other(k):
        return pl.BlockSpec((None, tr, C), lambda i, p: (jnp.where(p[0] == k, (k + 1) % nb, k), i, 0))

    return pl.pallas_call(
        body, name=name,
        grid_spec=pltpu.PrefetchScalarGridSpec(
            num_scalar_prefetch=1, grid=(nt,),
            in_specs=[other(k) for k in range(nb)] + [pl.BlockSpec((None, tr, C), lambda i, p: (p[0], i, 0))],
            out_specs=pl.BlockSpec((tr, C), lambda i, p: (p[1] * nt + i, 0))),
        out_shape=_sds((2 * H, C), F32), compiler_params=_cp(1))(place, *([arrived] * nb), own)


def _cast_into_slot(shard, place, name):
    R, C = shard.shape
    tr = _row_tile(R, 256)

    def body(pl_ref, s_ref, o_ref):
        o_ref[...] = _bf(s_ref[...])

    return pl.pallas_call(
        body, name=name,
        grid_spec=pltpu.PrefetchScalarGridSpec(
            num_scalar_prefetch=1, grid=(R // tr,),
            in_specs=[pl.BlockSpec((tr, C), lambda i, p: (i, 0))],
            out_specs=pl.BlockSpec((None, tr, C), lambda i, p: (p[0], i, 0))),
        out_shape=_sds((N_CHIPS, R, C), BF16), compiler_params=_cp(1))(place, shard)


ANY = pl.BlockSpec(memory_space=pl.ANY)


def _place():
    x, y, c = lax.axis_index("x"), lax.axis_index("y"), lax.axis_index("c")
    chips = [(1 - x, y), (x, 1 - y), (1 - x, 1 - y)]
    return x, y, c, chips


def _chip_id(px, py):
    return 2 * px + py


def _remote(src, dst, send_sems, recv_sems, k, to):
    return pltpu.make_async_remote_copy(src_ref=src, dst_ref=dst, send_sem=send_sems.at[k], recv_sem=recv_sems.at[k],
                                        device_id=to, device_id_type=MESH)


def _gather_weights(slots, wholes, name):
    ns, nw = len(slots), len(wholes)
    n = ns + nw

    def body(*refs):
        ins = refs[ns:n]
        outs = refs[n:2 * n]
        send_sems, recv_sems, local_sems = refs[2 * n:]
        x, y, c, chips = _place()
        me = _chip_id(x, y)
        sib = (x, y, 1 - c)
        local = [pltpu.make_async_copy(ins[b], outs[ns + b].at[me], local_sems.at[b]) for b in range(nw)]
        for cp in local:
            cp.start()
        sent = []
        for a in range(n):
            R = outs[a].shape[1]
            rows = pl.ds(c * (R // 2), R // 2) if a < ns else pl.ds(0, R)
            src = outs[a].at[me, rows] if a < ns else ins[a - ns]
            for j, chip in enumerate(chips):
                cp = _remote(src, outs[a].at[me, rows], send_sems, recv_sems, 6 * a + j, (*chip, c))
                cp.start()
                sent.append(cp)
        for a in range(n):
            R = outs[a].shape[1]
            rows = pl.ds(c * (R // 2), R // 2) if a < ns else pl.ds(0, R)
            for j, chip in enumerate(chips):
                landed = outs[a].at[_chip_id(*chip), rows]
                _remote(landed, landed, send_sems, recv_sems, 6 * a + j, (*chip, c)).wait_recv()
                if a < ns:
                    cp = _remote(landed, landed, send_sems, recv_sems, 6 * a + 3 + j, sib)
                    cp.start()
                    sent.append(cp)
        for a in range(ns):
            R = outs[a].shape[1]
            other = pl.ds((1 - c) * (R // 2), R // 2)
            for j, chip in enumerate(chips):
                passed = outs[a].at[_chip_id(*chip), other]
                _remote(passed, passed, send_sems, recv_sems, 6 * a + 3 + j, sib).wait_recv()
        for cp in sent:
            cp.wait_send()
        for cp in local:
            cp.wait()

    return pl.pallas_call(
        body, name=name, in_specs=[ANY] * n, out_specs=[ANY] * n,
        out_shape=[_sds(s.shape, s.dtype) for s in slots] + [_sds((N_CHIPS, *s.shape), s.dtype) for s in wholes],
        input_output_aliases={a: a for a in range(ns)},
        scratch_shapes=[pltpu.SemaphoreType.DMA((6 * n,)), pltpu.SemaphoreType.DMA((6 * n,)),
                        pltpu.SemaphoreType.DMA((max(nw, 1),))])(*slots, *wholes)


def _pair_exchange(grads, name):
    n = len(grads)

    def body(*refs):
        ins, outs = refs[:n], refs[n:2 * n]
        send_sems, recv_sems = refs[2 * n:]
        x, y, c, _ = _place()
        cps = []
        for a in range(n):
            half = ins[a].shape[1] // 2
            cp = _remote(ins[a].at[:, pl.ds((1 - c) * half, half), :], outs[a], send_sems, recv_sems, a, (x, y, 1 - c))
            cp.start()
            cps.append(cp)
        for cp in cps:
            cp.wait()

    return pl.pallas_call(
        body, name=name, in_specs=[ANY] * n, out_specs=[ANY] * n,
        out_shape=[_sds((g.shape[0], g.shape[1] // 2, g.shape[2]), g.dtype) for g in grads],
        scratch_shapes=[pltpu.SemaphoreType.DMA((n,)), pltpu.SemaphoreType.DMA((n,))])(*grads)


def _chip_exchange(parts, name):
    n = len(parts)

    def body(*refs):
        ins, outs = refs[:n], refs[n:2 * n]
        send_sems, recv_sems = refs[2 * n:]
        x, y, c, chips = _place()
        me = _chip_id(x, y)
        cps = []
        for a in range(n):
            for j, chip in enumerate(chips):
                cp = _remote(ins[a].at[_chip_id(*chip)], outs[a].at[me], send_sems, recv_sems, 3 * a + j, (*chip, c))
                cp.start()
                cps.append(cp)
        for a in range(n):
            for j, chip in enumerate(chips):
                landed = outs[a].at[_chip_id(*chip)]
                _remote(landed, landed, send_sems, recv_sems, 3 * a + j, (*chip, c)).wait_recv()
        for cp in cps:
            cp.wait_send()

    return pl.pallas_call(
        body, name=name, in_specs=[ANY] * n, out_specs=[ANY] * n,
        out_shape=[_sds(p.shape, p.dtype) for p in parts],
        scratch_shapes=[pltpu.SemaphoreType.DMA((3 * n,)), pltpu.SemaphoreType.DMA((3 * n,))])(*parts)


def _pair_concat(fulls, name):
    n = len(fulls)

    def body(*refs):
        outs = refs[n:2 * n]
        send_sems, recv_sems = refs[2 * n:]
        x, y, c, _ = _place()
        cps = []
        for a in range(n):
            H = outs[a].shape[0] // 2
            mine = outs[a].at[pl.ds(c * H, H)]
            cp = _remote(mine, mine, send_sems, recv_sems, a, (x, y, 1 - c))
            cp.start()
            cps.append(cp)
        for a, cp in enumerate(cps):
            H = outs[a].shape[0] // 2
            other = outs[a].at[pl.ds((1 - c) * H, H)]
            _remote(other, other, send_sems, recv_sems, a, (x, y, 1 - c)).wait_recv()
            cp.wait_send()

    return pl.pallas_call(
        body, name=name, in_specs=[ANY] * n, out_specs=[ANY] * n,
        out_shape=[_sds(f.shape, f.dtype) for f in fulls], input_output_aliases={a: a for a in range(n)},
        scratch_shapes=[pltpu.SemaphoreType.DMA((n,)), pltpu.SemaphoreType.DMA((n,))])(*fulls)


def _all_sum(pack, name):
    R, C = pack.shape

    def body(p_ref, o_ref, buf, send_sems, recv_sems):
        x, y, c, _ = _place()
        me = 4 * x + 2 * y + c
        buf[me] = p_ref[...]
        cps = []
        for k in range(1, N_DEV):
            to = (x ^ (k >> 2), y ^ ((k >> 1) & 1), c ^ (k & 1))
            cp = _remote(p_ref, buf.at[me], send_sems, recv_sems, k - 1, to)
            cp.start()
            cps.append(cp)
        for k in range(1, N_DEV):
            frm = (x ^ (k >> 2), y ^ ((k >> 1) & 1), c ^ (k & 1))
            slot = buf.at[4 * frm[0] + 2 * frm[1] + frm[2]]
            _remote(slot, slot, send_sems, recv_sems, k - 1, frm).wait_recv()
        acc = buf[0]
        for k in range(1, N_DEV):
            acc = acc + buf[k]
        o_ref[...] = acc
        for cp in cps:
            cp.wait_send()

    vm = pl.BlockSpec(memory_space=pltpu.VMEM)
    return pl.pallas_call(
        body, name=name, in_specs=[vm], out_specs=vm, out_shape=_sds((R, C), F32),
        scratch_shapes=[pltpu.VMEM((N_DEV, R, C), F32), pltpu.SemaphoreType.DMA((N_DEV - 1,)),
                        pltpu.SemaphoreType.DMA((N_DEV - 1,))])(pack)


def _local_step(xs, tgt, pre_mix_norm, W_in, rel_bias, hgrn_lb_raw, hgrn_norm, W_a, W_h, W_out, post_mix_norm,
                pre_ffn_norm, W_up, conv_w, conv_b, W_down, post_ffn_norm):
    h1 = _norm_fwd(xs, pre_mix_norm, "pre_mix_norm")
    proj = _mm_nn_blk(h1, W_in, "proj_in")
    biases = _relbias_fwd(rel_bias, "rel_bias_fwd")
    fw = [_attn_fwd(proj, biases[g], g, f"attn_fwd{g}") for g in range(N_GROUPS)]
    y, lse = _attn_merge([t[0] for t in fw], [t[1] for t in fw], "attn_merge")
    yh, o_h, states = _hgrn_fwd(proj, hgrn_lb_raw, hgrn_norm, "hgrn_fwd")
    za = _mm_nn_blk(y, W_a, "branch_attn")
    zh = _mm_nn_blk(yh, W_h, "branch_hgrn")
    merged = _merge_fwd(za, zh, proj, "merge_fwd")
    mo = _mm_nn(merged, W_out, "mix_out")
    x1, h2 = _mix_out(mo, xs, post_mix_norm, pre_ffn_norm, "mix_residual")
    u = _mm_nn_blk(h2, W_up, "ffn_up")
    a = _conv_gelu_fwd(u, conv_w, conv_b, "conv_gelu_fwd")
    ffo = _mm_nn(a, W_down, "ffn_down")
    dx2, dff, g_post_ffn, loss = _loss_head(ffo, x1, tgt, post_ffn_norm, "loss_head")

    da = _mm_nt(dff, W_down, "d_ffn_act")
    g_down = _mm_tn(a, dff, "g_w_down")
    dcg, dcv, gwg, gwv, gbg, gbv = _conv_gelu_bwd(u, da, conv_w, conv_b, "conv_gelu_bwd")
    g_conv_w = jnp.concatenate([gwg, gwv], axis=1)
    g_conv_b = jnp.concatenate([gbg, gbv], axis=1)
    du = jnp.concatenate(_conv_input_bwd(dcg, dcv, conv_w, "conv_input_bwd"), axis=1)
    dh2 = _mm_nt_blk(du, W_up, "d_ffn_in")
    g_up = _mm_tn_blk(h2, du, N_CHIPS, "g_w_up")
    dx1, g_pre_ffn = _prenorm_bwd(dh2, x1, pre_ffn_norm, dx2, "pre_ffn_norm_bwd")
    dmo, g_post_mix = _postnorm_bwd(dx1, mo, post_mix_norm, "post_mix_norm_bwd")
    dmerged = _mm_nt(dmo, W_out, "d_merged")
    g_out = _mm_tn(merged, dmo, "g_w_out")
    dza, dzh, dg0, dg1 = _merge_bwd(dmerged, za, zh, proj, "merge_bwd")
    dy = _mm_nt_blk(dza, W_a, "d_attn_out")
    g_a = _mm_tn_blk(y, dza, N_CHIPS, "g_w_branch_attn")
    dyh = _mm_nt_blk(dzh, W_h, "d_hgrn_out")
    g_h = _mm_tn_blk(yh, dzh, N_CHIPS, "g_w_branch_hgrn")
    dqkv, dbs = [], []
    for g in range(N_GROUPS):
        parts, db = _attn_bwd(proj, biases[g], lse, y, dy, g, f"attn_bwd{g}")
        dqkv += parts
        dbs.append(db)
    g_rel_bias = _relbias_bwd(dbs, "rel_bias_bwd")
    dhg, g_lb_raw, g_hgrn_norm = _hgrn_bwd(proj, hgrn_lb_raw, hgrn_norm, o_h, states, dyh, "hgrn_bwd")
    dproj = jnp.concatenate([*[_bf(t) for t in dqkv], *dhg, dg0, dg1], axis=1)
    dh1 = _mm_nt_blk(dproj, W_in, "d_proj_in")
    g_in = _mm_tn_blk(h1, dproj, N_CHIPS, "g_w_in")
    grad_x, g_pre_mix = _prenorm_bwd(dh1, xs, pre_mix_norm, dx1, "pre_mix_norm_bwd")
    small = dict(pre_mix_norm=g_pre_mix, rel_bias=g_rel_bias, hgrn_lb_raw=g_lb_raw, hgrn_norm=g_hgrn_norm,
                 post_mix_norm=g_post_mix, pre_ffn_norm=g_pre_ffn, conv_w=g_conv_w, conv_b=g_conv_b,
                 post_ffn_norm=g_post_ffn)
    big = dict(w_in=g_in, w_up=g_up, w_down=g_down.reshape(N_CHIPS, D_FF // N_CHIPS, D_MODEL),
               w_out=g_out.reshape(N_CHIPS, D_MODEL // N_CHIPS, D_MODEL), w_branch_attn=g_a, w_branch_hgrn=g_h)
    return loss, grad_x, small, big


SMALL = ("pre_mix_norm", "rel_bias", "hgrn_lb_raw", "hgrn_norm", "post_mix_norm", "pre_ffn_norm", "conv_w", "conv_b",
         "post_ffn_norm")
BIG = ("w_in", "w_up", "w_down", "w_out", "w_branch_attn", "w_branch_hgrn")
WEIGHTS = ("pre_mix_norm", "w_in", "rel_bias", "hgrn_lb_raw", "hgrn_norm", "w_branch_attn", "w_branch_hgrn", "w_out",
           "post_mix_norm", "pre_ffn_norm", "w_up", "conv_w", "conv_b", "w_down", "post_ffn_norm")


def kernel(x, pre_mix_norm, w_in, rel_bias, hgrn_lb_raw, hgrn_norm, w_branch_attn, w_branch_hgrn, w_out, post_mix_norm, pre_ffn_norm, w_up, conv_w, conv_b, w_down, post_ffn_norm, loss_target, m_pre_mix_norm, m_w_in, m_rel_bias, m_hgrn_lb_raw, m_hgrn_norm, m_w_branch_attn, m_w_branch_hgrn, m_w_out, m_post_mix_norm, m_pre_ffn_norm, m_w_up, m_conv_w, m_conv_b, m_w_down, m_post_ffn_norm, v_pre_mix_norm, v_w_in, v_rel_bias, v_hgrn_lb_raw, v_hgrn_norm, v_w_branch_attn, v_w_branch_hgrn, v_w_out, v_post_mix_norm, v_pre_ffn_norm, v_w_up, v_conv_w, v_conv_b, v_w_down, v_post_ffn_norm):
    w = dict(pre_mix_norm=pre_mix_norm, w_in=w_in, rel_bias=rel_bias, hgrn_lb_raw=hgrn_lb_raw, hgrn_norm=hgrn_norm,
             w_branch_attn=w_branch_attn, w_branch_hgrn=w_branch_hgrn, w_out=w_out, post_mix_norm=post_mix_norm,
             pre_ffn_norm=pre_ffn_norm, w_up=w_up, conv_w=conv_w, conv_b=conv_b, w_down=w_down,
             post_ffn_norm=post_ffn_norm)
    m = dict(pre_mix_norm=m_pre_mix_norm, w_in=m_w_in, rel_bias=m_rel_bias, hgrn_lb_raw=m_hgrn_lb_raw,
             hgrn_norm=m_hgrn_norm, w_branch_attn=m_w_branch_attn, w_branch_hgrn=m_w_branch_hgrn, w_out=m_w_out,
             post_mix_norm=m_post_mix_norm, pre_ffn_norm=m_pre_ffn_norm, w_up=m_w_up, conv_w=m_conv_w,
             conv_b=m_conv_b, w_down=m_w_down, post_ffn_norm=m_post_ffn_norm)
    v = dict(pre_mix_norm=v_pre_mix_norm, w_in=v_w_in, rel_bias=v_rel_bias, hgrn_lb_raw=v_hgrn_lb_raw,
             hgrn_norm=v_hgrn_norm, w_branch_attn=v_w_branch_attn, w_branch_hgrn=v_w_branch_hgrn, w_out=v_w_out,
             post_mix_norm=v_post_mix_norm, pre_ffn_norm=v_pre_ffn_norm, w_up=v_w_up, conv_w=v_conv_w,
             conv_b=v_conv_b, w_down=v_w_down, post_ffn_norm=v_post_ffn_norm)
    shard2d = {n: (w[n][0] if w[n].ndim == 3 else w[n]) for n in WEIGHTS}
    chip = 2 * lax.axis_index("x") + lax.axis_index("y")
    core = lax.axis_index("c")

    place = jnp.stack([chip, core]).astype(jnp.int32)
    gathered = _gather_weights([_cast_into_slot(shard2d[n], place, f"cast_{n}") for n in BIG], [shard2d["conv_w"]],
                               "gather_weights")
    W = dict(zip(BIG, gathered[:-1]))
    conv_w_full = gathered[-1].transpose(1, 0, 2).reshape(3, 2 * D_FF)
    loss, grad_x, small, big = _local_step(
        x[0], loss_target[0], pre_mix_norm, W["w_in"], rel_bias, hgrn_lb_raw, hgrn_norm, W["w_branch_attn"],
        W["w_branch_hgrn"], W["w_out"].reshape(D_MODEL, D_MODEL), post_mix_norm, pre_ffn_norm, W["w_up"],
        conv_w_full, conv_b, W["w_down"].reshape(D_FF, D_MODEL), post_ffn_norm)

    flat = [small[n].reshape(-1) for n in SMALL] + [loss.reshape(-1)]
    sizes = [t.shape[0] for t in flat]
    summed = _all_sum(jnp.concatenate(flat).reshape(-1, LANES), "sum_small").reshape(-1)
    offs = [sum(sizes[:i]) for i in range(len(sizes))]
    grads = {}
    for n, o, sz in zip(SMALL, offs, sizes):
        grads[n] = summed[o:o + sz].reshape(small[n].shape)
    loss_total = summed[offs[-1]]
    cw = 2 * D_FF // N_CHIPS
    grads["conv_w"] = lax.dynamic_slice(grads["conv_w"], (0, chip * cw), (3, cw))

    gl = [big[n] for n in BIG]
    from_sibling = _pair_exchange(gl, "pair_exchange")
    c_idx = core.reshape(1).astype(jnp.int32)
    partial = [_pair_sum(g, r, c_idx, f"pair_sum_{n}") for n, g, r in zip(BIG, gl, from_sibling)]
    arrived = _chip_exchange(partial, "chip_exchange")
    halves = [_chip_sum(r, p, place, f"chip_sum_{n}") for n, r, p in zip(BIG, arrived, partial)]
    for n, g in zip(BIG, _pair_concat(halves, "pair_concat")):
        grads[n] = g

    out_g, out_d, out_m, out_v = [], [], [], []
    for n in WEIGHTS:
        d2, m2, v2 = _adamw(shard2d[n], grads[n], m[n].reshape(shard2d[n].shape), v[n].reshape(shard2d[n].shape),
                            f"adamw_{n}")
        shape = w[n].shape
        out_g.append(grads[n].reshape(shape))
        out_d.append(d2.reshape(shape))
        out_m.append(m2.reshape(shape))
        out_v.append(v2.reshape(shape))
    return (loss_total, grad_x[None], *out_g, *out_d, *out_m, *out_v)
```

```python
import functools
import math

import jax
import jax.numpy as jnp
from jax import lax
from jax.experimental import pallas as pl
from jax.experimental.pallas import tpu as pltpu

F32 = jnp.float32
BF16 = jnp.bfloat16
MESH = pl.DeviceIdType.MESH

D_MODEL = 1024
N_GROUPS = 3
DILATIONS = (1, 4, 16)
HEADS = 8
HEAD_DIM = 64
GROUP_W = HEADS * HEAD_DIM
QKV_W = N_GROUPS * 3 * GROUP_W
BLK = 128
NEG_INF = -1e30
NUM_BUCKETS = 32
MAX_EXACT = 16
MAX_DISTANCE = 2048
HG_HEADS = 4
HG_DK = 128
HG_W = HG_HEADS * HG_DK
HG_CHUNK = 32
HG_TILE = 256
IN_W = QKV_W + 4 * HG_W + 2 * D_MODEL
D_FF = 2816
EPS = 1e-6
N_CHIPS = 4
N_DEV = 8
LANES = 128

ADAM_LR, ADAM_B1, ADAM_B2, ADAM_EPS, ADAM_WD, ADAM_STEP = 0.001, 0.9, 0.999, 1e-08, 0.01, 10

VMEM_LIMIT = 56 * 1024 * 1024


def _cp(n_axes):
    return pltpu.CompilerParams(dimension_semantics=("arbitrary",) * n_axes, vmem_limit_bytes=VMEM_LIMIT)


def _sds(shape, dtype):
    return jax.ShapeDtypeStruct(tuple(shape), dtype)


def _sigmoid(v):
    return 1.0 / (1.0 + jnp.exp(-v))


def _bf(v):
    return v.astype(BF16)


def _dot(a, b, dims):
    return lax.dot_general(a, b, (dims, ((), ())), preferred_element_type=F32)


NN = ((1,), (0,))
NT = ((1,), (1,))
TN = ((0,), (0,))


def _mm_nn_blk(a, wg, name, tm=512):
    M, K = a.shape
    nb, _, Nb = wg.shape

    def body(a_ref, w_ref, o_ref):
        o_ref[...] = _dot(_bf(a_ref[...]), w_ref[...], NN)

    return pl.pallas_call(
        body, name=name, grid=(nb, M // tm),
        in_specs=[pl.BlockSpec((tm, K), lambda j, i: (i, 0)), pl.BlockSpec((None, K, Nb), lambda j, i: (j, 0, 0))],
        out_specs=pl.BlockSpec((tm, Nb), lambda j, i: (i, j)),
        out_shape=_sds((M, nb * Nb), F32), compiler_params=_cp(2))(a, wg)


def _mm_nt_blk(dy, wg, name, tm=1024):
    M = dy.shape[0]
    nb, K, Nb = wg.shape

    def body(dy_ref, w_ref, o_ref):
        j = pl.program_id(1)
        r = _dot(_bf(dy_ref[...]), w_ref[...], NT)

        @pl.when(j == 0)
        def _():
            o_ref[...] = r

        @pl.when(j > 0)
        def _():
            o_ref[...] += r

    return pl.pallas_call(
        body, name=name, grid=(M // tm, nb),
        in_specs=[pl.BlockSpec((tm, Nb), lambda i, j: (i, j)), pl.BlockSpec((None, K, Nb), lambda i, j: (j, 0, 0))],
        out_specs=pl.BlockSpec((tm, K), lambda i, j: (i, 0)),
        out_shape=_sds((M, K), F32), compiler_params=_cp(2))(dy, wg)


def _mm_tn_blk(x, dy, nb, name, tk=512):
    T, Mx = x.shape
    Nb = dy.shape[1] // nb

    def body(x_ref, dy_ref, o_ref):
        t = pl.program_id(1)
        r = _dot(_bf(x_ref[...]), _bf(dy_ref[...]), TN)

        @pl.when(t == 0)
        def _():
            o_ref[...] = r

        @pl.when(t > 0)
        def _():
            o_ref[...] += r

    return pl.pallas_call(
        body, name=name, grid=(nb, T // tk),
        in_specs=[pl.BlockSpec((tk, Mx), lambda j, t: (t, 0)), pl.BlockSpec((tk, Nb), lambda j, t: (t, j))],
        out_specs=pl.BlockSpec((None, Mx, Nb), lambda j, t: (j, 0, 0)),
        out_shape=_sds((nb, Mx, Nb), F32), compiler_params=_cp(2))(x, dy)


def _mm_nn(a, w, name, tm=512):
    M, K = a.shape
    N = w.shape[1]

    def body(a_ref, w_ref, o_ref):
        o_ref[...] = _dot(_bf(a_ref[...]), w_ref[...], NN)

    return pl.pallas_call(
        body, name=name, grid=(M // tm,),
        in_specs=[pl.BlockSpec((tm, K), lambda i: (i, 0)), pl.BlockSpec((K, N), lambda i: (0, 0))],
        out_specs=pl.BlockSpec((tm, N), lambda i: (i, 0)),
        out_shape=_sds((M, N), F32), compiler_params=_cp(1))(a, w)


def _mm_nt(dy, w, name, tm=512):
    M, N = dy.shape
    K = w.shape[0]

    def body(dy_ref, w_ref, o_ref):
        o_ref[...] = _dot(_bf(dy_ref[...]), w_ref[...], NT)

    return pl.pallas_call(
        body, name=name, grid=(M // tm,),
        in_specs=[pl.BlockSpec((tm, N), lambda i: (i, 0)), pl.BlockSpec((K, N), lambda i: (0, 0))],
        out_specs=pl.BlockSpec((tm, K), lambda i: (i, 0)),
        out_shape=_sds((M, K), F32), compiler_params=_cp(1))(dy, w)


def _mm_tn(x, dy, name, tk=512):
    T, Mx = x.shape
    N = dy.shape[1]

    def body(x_ref, dy_ref, o_ref):
        t = pl.program_id(0)
        r = _dot(_bf(x_ref[...]), _bf(dy_ref[...]), TN)

        @pl.when(t == 0)
        def _():
            o_ref[...] = r

        @pl.when(t > 0)
        def _():
            o_ref[...] += r

    return pl.pallas_call(
        body, name=name, grid=(T // tk,),
        in_specs=[pl.BlockSpec((tk, Mx), lambda t: (t, 0)), pl.BlockSpec((tk, N), lambda t: (t, 0))],
        out_specs=pl.BlockSpec((Mx, N), lambda t: (0, 0)),
        out_shape=_sds((Mx, N), F32), compiler_params=_cp(1))(x, dy)


def _tile(arr, bw, col=lambda c: 0):
    return ("tile", arr, bw, col)


def _full(arr):
    return ("full", arr)


def _out_tile(width, dtype, bw, col=lambda c: 0):
    return ("tile", width, dtype, bw, col)


def _out_acc(rows, width, bw, col=lambda c: 0):
    return ("acc", rows, width, bw, col)


def _rows_call(name, body, n_rows, tm, ncol, ins, outs):
    in_specs, args = [], []
    for e in ins:
        if e[0] == "tile":
            _, arr, bw, col = e
            in_specs.append(pl.BlockSpec((tm, bw), functools.partial(lambda c, i, col: (i, col(c)), col=col)))
        else:
            arr = e[1]
            in_specs.append(pl.BlockSpec(arr.shape, functools.partial(lambda c, i, nd: (0,) * nd, nd=arr.ndim)))
        args.append(arr)
    out_specs, out_shape = [], []
    for e in outs:
        if e[0] == "tile":
            _, width, dtype, bw, col = e
            out_specs.append(pl.BlockSpec((tm, bw), functools.partial(lambda c, i, col: (i, col(c)), col=col)))
            out_shape.append(_sds((n_rows, width), dtype))
        else:
            _, rows, width, bw, col = e
            out_specs.append(pl.BlockSpec((rows, bw), functools.partial(lambda c, i, col: (0, col(c)), col=col)))
            out_shape.append(_sds((rows, width), F32))
    return pl.pallas_call(
        body, name=name, grid=(ncol, n_rows // tm), in_specs=in_specs, out_specs=out_specs,
        out_shape=out_shape, compiler_params=_cp(2))(*args)


def _acc(ref, val):
    i = pl.program_id(1)

    @pl.when(i == 0)
    def _():
        ref[...] = val

    @pl.when(i > 0)
    def _():
        ref[...] += val


def _rinv(z):
    return lax.rsqrt(jnp.mean(z * z, axis=-1, keepdims=True) + EPS)


def _norm_bwd(dy, zhat, r, w):
    dyw = dy * w
    return r * (dyw - zhat * jnp.mean(dyw * zhat, axis=-1, keepdims=True))


def _norm_fwd(x, w, name):
    def body(x_ref, w_ref, h_ref):
        xv = x_ref[...]
        h_ref[...] = _bf(xv * _rinv(xv) * w_ref[...])

    return _rows_call(name, body, x.shape[0], 512, 1, [_tile(x, D_MODEL), _full(w)],
                      [_out_tile(D_MODEL, BF16, D_MODEL)])[0]


def _prenorm_bwd(dh, xin, w, dres, name):
    def body(dh_ref, x_ref, w_ref, dres_ref, dx_ref, dw_ref):
        xv = x_ref[...]
        r = _rinv(xv)
        xhat = xv * r
        dhv = dh_ref[...]
        dx_ref[...] = dres_ref[...] + _norm_bwd(dhv, xhat, r, w_ref[...])
        _acc(dw_ref, jnp.sum(dhv * xhat, axis=0, keepdims=True))

    return _rows_call(name, body, xin.shape[0], 512, 1,
                      [_tile(dh, D_MODEL), _tile(xin, D_MODEL), _full(w), _tile(dres, D_MODEL)],
                      [_out_tile(D_MODEL, F32, D_MODEL), _out_acc(1, D_MODEL, D_MODEL)])


def _postnorm_bwd(dout, z, w, name):
    def body(do_ref, z_ref, w_ref, dz_ref, dw_ref):
        zv = z_ref[...]
        r = _rinv(zv)
        zhat = zv * r
        dov = do_ref[...]
        dz_ref[...] = _bf(_norm_bwd(dov, zhat, r, w_ref[...]))
        _acc(dw_ref, jnp.sum(dov * zhat, axis=0, keepdims=True))

    return _rows_call(name, body, z.shape[0], 512, 1, [_tile(dout, D_MODEL), _tile(z, D_MODEL), _full(w)],
                      [_out_tile(D_MODEL, BF16, D_MODEL), _out_acc(1, D_MODEL, D_MODEL)])


def _t5_bucket(dist):
    n = jnp.maximum(dist, 0)
    nf = jnp.maximum(n, 1).astype(F32)
    large = MAX_EXACT + (jnp.log(nf / MAX_EXACT) / math.log(MAX_DISTANCE / MAX_EXACT)
                         * (NUM_BUCKETS - MAX_EXACT)).astype(jnp.int32)
    large = jnp.minimum(large, NUM_BUCKETS - 1)
    return jnp.where(n < MAX_EXACT, n, large)


def _band_rel():
    return jnp.arange(BLK)[:, None] + BLK - jnp.arange(2 * BLK)[None, :]


def _band_mask(n):
    row = lax.broadcasted_iota(jnp.int32, (BLK, 2 * BLK), 0)
    col = lax.broadcasted_iota(jnp.int32, (BLK, 2 * BLK), 1)
    rel = row + BLK - col
    return (rel >= 0) & (rel <= BLK) & ((col >= BLK) | (n > 0))


RES_UNROLL = 4


def _heads_per_step(d):
    return HEADS if d == 1 else LANES // HEAD_DIM


def _sub_rows(r, d):
    return pl.ds(r, BLK, stride=d) if d > 1 else pl.ds(0, BLK)


def _for_residues(d, fn):
    if d <= RES_UNROLL:
        for r in range(d):
            fn(r)
    else:
        def group(i, carry):
            for k in range(RES_UNROLL):
                fn(i * RES_UNROLL + k)
            return carry

        lax.fori_loop(0, d // RES_UNROLL, group, 0)


def _attn_specs(d, g, qblock):
    cw = _heads_per_step(d) * HEAD_DIM

    def col(part, hp):
        return (g * 3 + part) * (GROUP_W // cw) + hp

    def cur(part):
        return pl.BlockSpec((d * BLK, cw), lambda hp, n: (qblock(n), col(part, hp)))

    def prev(part):
        return pl.BlockSpec((d * BLK, cw), lambda hp, n: (jnp.maximum(qblock(n) - 1, 0), col(part, hp)))

    return cur, prev


def _attn_fwd(proj, bias, g, name):
    S = proj.shape[0]
    d = DILATIONS[g]
    NB = S // (d * BLK)
    hps = _heads_per_step(d)

    def body(q_ref, kp_ref, kc_ref, vp_ref, vc_ref, b_ref, o_ref, lse_ref):
        hp = pl.program_id(0)
        mask = _band_mask(pl.program_id(1))

        def residue(r):
            rows = _sub_rows(r, d)
            q2 = q_ref[rows, :]
            k2 = jnp.concatenate([kp_ref[rows, :], kc_ref[rows, :]], axis=0)
            v2 = jnp.concatenate([vp_ref[rows, :], vc_ref[rows, :]], axis=0)
            outs, lses = [], []
            for hh in range(hps):
                hs = slice(hh * HEAD_DIM, (hh + 1) * HEAD_DIM)
                s = _dot(_bf(q2[:, hs]), _bf(k2[:, hs]), NT) * (HEAD_DIM ** -0.5) + b_ref[hp * hps + hh]
                s = jnp.where(mask, s, NEG_INF)
                m = jnp.max(s, axis=-1, keepdims=True)
                p = jnp.exp(s - m)
                l = jnp.sum(p, axis=-1, keepdims=True)
                outs.append(_dot(_bf(p), _bf(v2[:, hs]), NN) / l)
                lses.append(jnp.broadcast_to(m + jnp.log(l), (BLK, HEAD_DIM)))
            o_ref[rows, :] = jnp.concatenate(outs, axis=1)
            lse_ref[rows, :] = jnp.concatenate(lses, axis=1)

        _for_residues(d, residue)

    cur, prev = _attn_specs(d, g, lambda n: n)
    out = pl.BlockSpec((d * BLK, hps * HEAD_DIM), lambda hp, n: (n, hp))
    return pl.pallas_call(
        body, name=name, grid=(HEADS // hps, NB),
        in_specs=[cur(0), prev(1), cur(1), prev(2), cur(2),
                  pl.BlockSpec((HEADS, BLK, 2 * BLK), lambda hp, n: (0, 0, 0))],
        out_specs=[out, out], out_shape=[_sds((S, GROUP_W), F32)] * 2,
        compiler_params=_cp(2))(proj, proj, proj, proj, proj, bias)


def _attn_merge(os_, lses, name):
    def body(o0, o1, o2, l0, l1, l2, y_ref, lse_ref):
        a, b, c = l0[...], l1[...], l2[...]
        m = jnp.maximum(jnp.maximum(a, b), c)
        ea, eb, ec = jnp.exp(a - m), jnp.exp(b - m), jnp.exp(c - m)
        den = ea + eb + ec
        y_ref[...] = (ea * o0[...] + eb * o1[...] + ec * o2[...]) / den
        lse_ref[...] = m + jnp.log(den)

    S = os_[0].shape[0]
    return _rows_call(name, body, S, 512, 1, [_tile(t, GROUP_W) for t in (*os_, *lses)],
                      [_out_tile(GROUP_W, F32, GROUP_W)] * 2)


def _attn_bwd(proj, bias, lse, y, dy, g, name):
    S = proj.shape[0]
    d = DILATIONS[g]
    NB = S // (d * BLK)
    hps = _heads_per_step(d)

    def body(q_ref, kp_ref, kc_ref, vp_ref, vc_ref, b_ref, l_ref, y_ref, dy_ref,
             dq_ref, dk_ref, dv_ref, db_ref, ck_ref, cv_ref):
        hp, n = pl.program_id(0), pl.program_id(1)

        @pl.when((hp == 0) & (n == 0))
        def _():
            db_ref[...] = jnp.zeros_like(db_ref)

        @pl.when(n == 0)
        def _():
            ck_ref[...] = jnp.zeros_like(ck_ref)
            cv_ref[...] = jnp.zeros_like(cv_ref)

        @pl.when(n < NB)
        def _():
            mask = _band_mask(n)

            def residue(r):
                rows = _sub_rows(r, d)
                q2 = q_ref[rows, :]
                k2 = jnp.concatenate([kp_ref[rows, :], kc_ref[rows, :]], axis=0)
                v2 = jnp.concatenate([vp_ref[rows, :], vc_ref[rows, :]], axis=0)
                l2, y2, dy2 = l_ref[rows, :], y_ref[rows, :], dy_ref[rows, :]
                dqs, dks, dvs = [], [], []
                for hh in range(hps):
                    hs = slice(hh * HEAD_DIM, (hh + 1) * HEAD_DIM)
                    q, kb, vb = _bf(q2[:, hs]), _bf(k2[:, hs]), _bf(v2[:, hs])
                    s = _dot(q, kb, NT) * (HEAD_DIM ** -0.5) + b_ref[hp * hps + hh]
                    s = jnp.where(mask, s, NEG_INF)
                    p = jnp.exp(s - l2[:, hh * HEAD_DIM:hh * HEAD_DIM + 1])
                    dyh = dy2[:, hs]
                    delta = jnp.sum(dyh * y2[:, hs], axis=-1, keepdims=True)
                    dyb = _bf(dyh)
                    ds = p * (_dot(dyb, vb, NT) - delta)
                    db_ref[hp * hps + hh] += ds
                    dsb = _bf(ds * (HEAD_DIM ** -0.5))
                    dqs.append(_dot(dsb, kb, NN))
                    dks.append(_dot(dsb, q, TN))
                    dvs.append(_dot(_bf(p), dyb, TN))
                dkb = jnp.concatenate(dks, axis=1)
                dvb = jnp.concatenate(dvs, axis=1)
                dq_ref[rows, :] = jnp.concatenate(dqs, axis=1)
                dk_ref[rows, :] = ck_ref[rows, :] + dkb[:BLK]
                dv_ref[rows, :] = cv_ref[rows, :] + dvb[:BLK]
                ck_ref[rows, :] = dkb[BLK:]
                cv_ref[rows, :] = dvb[BLK:]

            _for_residues(d, residue)

        @pl.when(n == NB)
        def _():
            dk_ref[...] = ck_ref[...]
            dv_ref[...] = cv_ref[...]

    def qn(n):
        return jnp.minimum(n, NB - 1)

    cur, prev = _attn_specs(d, g, qn)
    cw = hps * HEAD_DIM
    row = pl.BlockSpec((d * BLK, cw), lambda hp, n: (qn(n), hp))
    done = pl.BlockSpec((d * BLK, cw), lambda hp, n: (jnp.maximum(n - 1, 0), hp))
    dq, dk, dv, db = pl.pallas_call(
        body, name=name, grid=(HEADS // hps, NB + 1),
        in_specs=[cur(0), prev(1), cur(1), prev(2), cur(2),
                  pl.BlockSpec((HEADS, BLK, 2 * BLK), lambda hp, n: (0, 0, 0)), row, row, row],
        out_specs=[row, done, done, pl.BlockSpec((HEADS, BLK, 2 * BLK), lambda hp, n: (0, 0, 0))],
        out_shape=[_sds((S, GROUP_W), F32)] * 3 + [_sds((HEADS, BLK, 2 * BLK), F32)],
        scratch_shapes=[pltpu.VMEM((d * BLK, cw), F32)] * 2,
        compiler_params=_cp(2))(proj, proj, proj, proj, proj, bias, lse, y, dy)
    return [dq, dk, dv], db


BAND = BLK * 2 * BLK


def _bucket_onehot():
    buckets = jnp.stack([_t5_bucket(_band_rel() * d) for d in DILATIONS]).reshape(N_GROUPS, 1, BAND)
    return (buckets == jnp.arange(NUM_BUCKETS).reshape(1, NUM_BUCKETS, 1)).astype(F32)


def _relbias_fwd(rel_bias, name):
    table = rel_bias.reshape(NUM_BUCKETS, N_GROUPS, HEADS).transpose(1, 0, 2)

    def body(t_ref, oh_ref, o_ref):
        o_ref[...] = lax.dot_general(t_ref[...], oh_ref[...], (TN, ((), ())), preferred_element_type=F32,
                                     precision=lax.Precision.HIGHEST)

    out = pl.pallas_call(
        body, name=name, grid=(N_GROUPS,),
        in_specs=[pl.BlockSpec((None, NUM_BUCKETS, HEADS), lambda g: (g, 0, 0)),
                  pl.BlockSpec((None, NUM_BUCKETS, BAND), lambda g: (g, 0, 0))],
        out_specs=pl.BlockSpec((None, HEADS, BAND), lambda g: (g, 0, 0)),
        out_shape=_sds((N_GROUPS, HEADS, BAND), F32), compiler_params=_cp(1))(table, _bucket_onehot())
    return out.reshape(N_GROUPS, HEADS, BLK, 2 * BLK)


def _relbias_bwd(dbs, name):
    band = BAND
    onehot = _bucket_onehot()
    dbf = jnp.stack([db.reshape(HEADS, band) for db in dbs])

    def body(oh_ref, db_ref, o_ref):
        o_ref[...] = lax.dot_general(oh_ref[...], db_ref[...], (NT, ((), ())), preferred_element_type=F32,
                                     precision=lax.Precision.HIGHEST)

    out = pl.pallas_call(
        body, name=name, grid=(N_GROUPS,),
        in_specs=[pl.BlockSpec((None, NUM_BUCKETS, band), lambda g: (g, 0, 0)),
                  pl.BlockSpec((None, HEADS, band), lambda g: (g, 0, 0))],
        out_specs=pl.BlockSpec((None, NUM_BUCKETS, HEADS), lambda g: (g, 0, 0)),
        out_shape=_sds((N_GROUPS, NUM_BUCKETS, HEADS), F32), compiler_params=_cp(1))(onehot, dbf)
    return out.transpose(1, 0, 2).reshape(NUM_BUCKETS, N_GROUPS * HEADS)


def _chunk_pos(shape):
    return lax.broadcasted_iota(jnp.int32, shape, 0) % HG_CHUNK


def _chunk_cumsum(v):
    pos = _chunk_pos(v.shape)
    s = 1
    while s < HG_CHUNK:
        v = v + jnp.where(pos >= s, pltpu.roll(v, s, 0), 0.0)
        s *= 2
    return v


def _chunk_rev_cumsum(v):
    pos = _chunk_pos(v.shape)
    n = v.shape[0]
    s = 1
    while s < HG_CHUNK:
        v = v + jnp.where(pos < HG_CHUNK - s, pltpu.roll(v, n - s, 0), 0.0)
        s *= 2
    return v


def _lower_bound(raw):
    a0, a1 = raw[0:1], raw[1:2]
    m = jnp.maximum(a0, a1)
    e0, e1 = jnp.exp(a0 - m), jnp.exp(a1 - m)
    return e0 / (e0 + e1)


def _hg_gates(qr, fr, lb):
    sf = _sigmoid(fr)
    f = lb + (1.0 - lb) * sf
    sq = _sigmoid(qr)
    return qr * sq, sq, f, sf


HG_COL0 = QKV_W // HG_W


def _hgrn_fwd(proj, lb_raw, nw, name):
    S = proj.shape[0]
    ncs = HG_TILE // HG_CHUNK
    tril = jnp.tril(jnp.ones((HG_CHUNK, HG_CHUNK), dtype=bool))

    def body(q_ref, f_ref, i_ref, og_ref, lb_ref, nw_ref, y_ref, o_ref, st_ref, state):
        @pl.when(pl.program_id(0) == 0)
        def _():
            state[...] = jnp.zeros_like(state)

        lb = _lower_bound(lb_ref[...])
        q, _, f, _ = _hg_gates(q_ref[...], f_ref[...], lb)
        k = 1.0 - f
        G = _chunk_cumsum(jnp.log(f))
        row = lax.broadcasted_iota(jnp.int32, (HG_CHUNK, HG_CHUNK), 0)
        col = lax.broadcasted_iota(jnp.int32, (HG_CHUNK, HG_CHUNK), 1)
        for h in range(HG_HEADS):
            hs = slice(h * HG_DK, (h + 1) * HG_DK)
            st = state[h]
            for c in range(ncs):
                cs = slice(c * HG_CHUNK, (c + 1) * HG_CHUNK)
                Gc = G[cs, hs]
                gl = Gc[HG_CHUNK - 1:HG_CHUNK]
                qt = _bf(q[cs, hs] * jnp.exp(Gc))
                kt = _bf(k[cs, hs] * jnp.exp(-Gc))
                kd = _bf(k[cs, hs] * jnp.exp(gl - Gc))
                v = _bf(i_ref[cs, hs])
                A = jnp.where(row >= col, _dot(qt, kt, NT), 0.0)
                o_ref[cs, hs] = _dot(_bf(A), v, NN) + _dot(qt, _bf(st), NT)
                st_ref[c, h] = st
                st = st * jnp.exp(gl) + _dot(v, kd, TN)
            state[h] = st
            oh = o_ref[:, hs]
            og = og_ref[:, hs]
            y_ref[:, hs] = oh * _rinv(oh) * nw_ref[...] * (og * _sigmoid(og))

    def colspec(j):
        return pl.BlockSpec((HG_TILE, HG_W), lambda i: (i, HG_COL0 + j))

    return pl.pallas_call(
        body, name=name, grid=(S // HG_TILE,),
        in_specs=[colspec(0), colspec(1), colspec(2), colspec(3),
                  pl.BlockSpec((2, HG_W), lambda i: (0, 0)), pl.BlockSpec((1, HG_DK), lambda i: (0, 0))],
        out_specs=[pl.BlockSpec((HG_TILE, HG_W), lambda i: (i, 0))] * 2
        + [pl.BlockSpec((ncs, HG_HEADS, HG_DK, HG_DK), lambda i: (i, 0, 0, 0))],
        out_shape=[_sds((S, HG_W), F32)] * 2 + [_sds((S // HG_CHUNK, HG_HEADS, HG_DK, HG_DK), F32)],
        scratch_shapes=[pltpu.VMEM((HG_HEADS, HG_DK, HG_DK), F32)],
        compiler_params=_cp(1))(proj, proj, proj, proj, lb_raw, nw)


def _hgrn_bwd(proj, lb_raw, nw, o, states, dy, name):
    S = proj.shape[0]
    ncs = HG_TILE // HG_CHUNK
    nt = S // HG_TILE

    def body(q_ref, f_ref, i_ref, og_ref, lb_ref, nw_ref, o_ref, st_ref, dy_ref,
             dq_ref, df_ref, di_ref, dog_ref, dlb_ref, dnw_ref, dstate, do_s, dG_s, dgl_s, dk_s, dlb_s):
        step = pl.program_id(0)

        @pl.when(step == 0)
        def _():
            dstate[...] = jnp.zeros_like(dstate)
            dlb_s[...] = jnp.zeros_like(dlb_s)
            dnw_ref[...] = jnp.zeros_like(dnw_ref)

        lb = _lower_bound(lb_ref[...])
        qr = q_ref[...]
        q, sq, f, sf = _hg_gates(qr, f_ref[...], lb)
        k = 1.0 - f
        G = _chunk_cumsum(jnp.log(f))
        nwv = nw_ref[...]
        row = lax.broadcasted_iota(jnp.int32, (HG_CHUNK, HG_CHUNK), 0)
        col = lax.broadcasted_iota(jnp.int32, (HG_CHUNK, HG_CHUNK), 1)
        for h in range(HG_HEADS):
            hs = slice(h * HG_DK, (h + 1) * HG_DK)
            oh = o_ref[:, hs]
            r = _rinv(oh)
            ohat = oh * r
            og = og_ref[:, hs]
            sg = _sigmoid(og)
            dyh = dy_ref[:, hs]
            don = dyh * (og * sg)
            dog_ref[:, hs] = _bf(dyh * (ohat * nwv) * (sg * (1.0 + og * (1.0 - sg))))
            dnw_ref[...] += jnp.sum(don * ohat, axis=0, keepdims=True)
            do_s[:, hs] = _norm_bwd(don, ohat, r, nwv)
            dst = dstate[h]
            for c in reversed(range(ncs)):
                cs = slice(c * HG_CHUNK, (c + 1) * HG_CHUNK)
                Gc = G[cs, hs]
                gl = Gc[HG_CHUNK - 1:HG_CHUNK]
                eG, enG, edG, egl = jnp.exp(Gc), jnp.exp(-Gc), jnp.exp(gl - Gc), jnp.exp(gl)
                qt, kt, kd = q[cs, hs] * eG, k[cs, hs] * enG, k[cs, hs] * edG
                qtb, ktb, kdb = _bf(qt), _bf(kt), _bf(kd)
                v = _bf(i_ref[cs, hs])
                do = _bf(do_s[cs, hs])
                st = st_ref[c, h]
                dstb = _bf(dst)
                A = jnp.where(row >= col, _dot(qtb, ktb, NT), 0.0)
                dA = _bf(jnp.where(row >= col, _dot(do, v, NT), 0.0))
                di_ref[cs, hs] = _bf(_dot(_bf(A), do, TN) + _dot(kdb, dstb, NT))
                dqt = _dot(dA, ktb, NN) + _dot(do, _bf(st), NN)
                dkt = _dot(dA, qtb, TN)
                dkd = _dot(v, dstb, NN)
                dgl = egl * jnp.sum(st * dst, axis=0, keepdims=True) + jnp.sum(dkd * kd, axis=0, keepdims=True)
                dst = dst * egl + _dot(do, qtb, TN)
                dq_ref[cs, hs] = _bf(dqt * eG * (sq[cs, hs] * (1.0 + qr[cs, hs] * (1.0 - sq[cs, hs]))))
                dk_s[cs, hs] = dkt * enG + dkd * edG
                dG_s[cs, hs] = dqt * qt - dkt * kt - dkd * kd
                dgl_s[cs, hs] = jnp.broadcast_to(dgl, (HG_CHUNK, HG_DK))
            dstate[h] = dst
        dg = _chunk_rev_cumsum(dG_s[...]) + dgl_s[...]
        dfv = dg / f - dk_s[...]
        df_ref[...] = _bf(dfv * (1.0 - lb) * sf * (1.0 - sf))
        dlb_s[...] += jnp.sum(dfv * (1.0 - sf), axis=0, keepdims=True)

        @pl.when(step == nt - 1)
        def _():
            t = dlb_s[...] * lb * (1.0 - lb)
            dlb_ref[...] = jnp.concatenate([t, -t], axis=0)

    def colspec(j):
        return pl.BlockSpec((HG_TILE, HG_W), lambda i: (nt - 1 - i, HG_COL0 + j))

    tile = pl.BlockSpec((HG_TILE, HG_W), lambda i: (nt - 1 - i, 0))
    outs = pl.pallas_call(
        body, name=name, grid=(nt,),
        in_specs=[colspec(0), colspec(1), colspec(2), colspec(3),
                  pl.BlockSpec((2, HG_W), lambda i: (0, 0)), pl.BlockSpec((1, HG_DK), lambda i: (0, 0)),
                  tile, pl.BlockSpec((ncs, HG_HEADS, HG_DK, HG_DK), lambda i: (nt - 1 - i, 0, 0, 0)), tile],
        out_specs=[tile] * 4 + [pl.BlockSpec((2, HG_W), lambda i: (0, 0)), pl.BlockSpec((1, HG_DK), lambda i: (0, 0))],
        out_shape=[_sds((S, HG_W), BF16)] * 4 + [_sds((2, HG_W), F32), _sds((1, HG_DK), F32)],
        scratch_shapes=[pltpu.VMEM((HG_HEADS, HG_DK, HG_DK), F32)] + [pltpu.VMEM((HG_TILE, HG_W), F32)] * 4
        + [pltpu.VMEM((1, HG_W), F32)],
        compiler_params=_cp(1))(proj, proj, proj, proj, lb_raw, nw, o, states, dy)
    return outs[:4], outs[4], outs[5]


GATE_COL0 = (QKV_W + 4 * HG_W) // GROUP_W
HALF_D = D_MODEL // 2


def _gate_tiles(proj):
    return [_tile(proj, HALF_D, lambda c: GATE_COL0 + c), _tile(proj, HALF_D, lambda c: GATE_COL0 + 2 + c)]


def _merge_fwd(za, zh, proj, name):
    def body(za_ref, zh_ref, g0_ref, g1_ref, m_ref):
        m_ref[...] = _bf(_sigmoid(g0_ref[...]) * za_ref[...] + _sigmoid(g1_ref[...]) * zh_ref[...])

    col = lambda c: c
    return _rows_call(name, body, za.shape[0], 512, 2,
                      [_tile(za, HALF_D, col), _tile(zh, HALF_D, col), *_gate_tiles(proj)],
                      [_out_tile(D_MODEL, BF16, HALF_D, col)])[0]


def _merge_bwd(dm, za, zh, proj, name):
    def body(dm_ref, za_ref, zh_ref, g0_ref, g1_ref, dza_ref, dzh_ref, dg0_ref, dg1_ref):
        dmv = dm_ref[...]
        s0, s1 = _sigmoid(g0_ref[...]), _sigmoid(g1_ref[...])
        dza_ref[...] = _bf(dmv * s0)
        dzh_ref[...] = _bf(dmv * s1)
        dg0_ref[...] = _bf(dmv * za_ref[...] * s0 * (1.0 - s0))
        dg1_ref[...] = _bf(dmv * zh_ref[...] * s1 * (1.0 - s1))

    col = lambda c: c
    return _rows_call(name, body, za.shape[0], 512, 2,
                      [_tile(dm, HALF_D, col), _tile(za, HALF_D, col), _tile(zh, HALF_D, col), *_gate_tiles(proj)],
                      [_out_tile(D_MODEL, BF16, HALF_D, col)] * 4)


def _mix_out(mo, x, w_post, w_pre, name):
    def body(mo_ref, x_ref, wp_ref, wf_ref, x1_ref, h2_ref):
        z = mo_ref[...]
        x1 = x_ref[...] + z * _rinv(z) * wp_ref[...]
        x1_ref[...] = x1
        h2_ref[...] = _bf(x1 * _rinv(x1) * wf_ref[...])

    return _rows_call(name, body, x.shape[0], 512, 1,
                      [_tile(mo, D_MODEL), _tile(x, D_MODEL), _full(w_post), _full(w_pre)],
                      [_out_tile(D_MODEL, F32, D_MODEL), _out_tile(D_MODEL, BF16, D_MODEL)])


def _loss_head(ffo, x1, tgt, w, name):
    def body(f_ref, x1_ref, t_ref, w_ref, dx_ref, df_ref, dw_ref, loss_ref):
        z = f_ref[...]
        r = _rinv(z)
        zhat = z * r
        wv = w_ref[...]
        e = x1_ref[...] + zhat * wv - t_ref[...]
        dx = e * (1.0 / D_MODEL)
        dx_ref[...] = dx
        df_ref[...] = _bf(_norm_bwd(dx, zhat, r, wv))
        _acc(dw_ref, jnp.sum(dx * zhat, axis=0, keepdims=True))
        part = 0.5 * jnp.sum(jnp.sum(e * e, axis=1, keepdims=True), axis=0, keepdims=True) * (1.0 / D_MODEL)
        _acc(loss_ref, jnp.broadcast_to(part, (1, LANES)))

    return _rows_call(name, body, x1.shape[0], 512, 1,
                      [_tile(ffo, D_MODEL), _tile(x1, D_MODEL), _tile(tgt, D_MODEL), _full(w)],
                      [_out_tile(D_MODEL, F32, D_MODEL), _out_tile(D_MODEL, BF16, D_MODEL),
                       _out_acc(1, D_MODEL, D_MODEL), _out_acc(1, LANES, LANES)])


CONV_CB = D_FF // 2
CONV_TM = 512
HALO = 8
SQRT_HALF = 0.7071067811865476
INV_SQRT_2PI = 0.3989422804014327


def _conv_taps(u_ref, halo_ref, first):
    u = u_ref[...]
    row = lax.broadcasted_iota(jnp.int32, u.shape, 0)
    p1 = jnp.where(first, 0.0, halo_ref[HALO - 1:HALO, :])
    p2 = jnp.where(first, 0.0, halo_ref[HALO - 2:HALO - 1, :])
    u1 = jnp.where(row == 0, p1, pltpu.roll(u, 1, 0))
    u2 = jnp.where(row == 0, p2, jnp.where(row == 1, p1, pltpu.roll(u, 2, 0)))
    return u2, u1, u


def _conv(taps, w_ref, b_ref):
    return b_ref[...] + w_ref[0:1, :] * taps[0] + w_ref[1:2, :] * taps[1] + w_ref[2:3, :] * taps[2]


def _conv_specs(tm):
    nh = tm // HALO
    nc = D_FF // CONV_CB

    def tile(off):
        return pl.BlockSpec((tm, CONV_CB), lambda c, i: (i, off + c))

    def halo(off):
        return pl.BlockSpec((HALO, CONV_CB), lambda c, i: (jnp.maximum(i * nh - 1, 0), off + c))

    def small(rows, off):
        return pl.BlockSpec((rows, CONV_CB), lambda c, i: (0, off + c))

    return nc, tile, halo, small


def _conv_gelu_fwd(u, cw, cb, name):
    S = u.shape[0]
    tm = CONV_TM
    nc, tile, halo, small = _conv_specs(tm)

    def body(ug, hg, uv, hv, wg, wv, bg, bv, a_ref):
        first = pl.program_id(1) == 0
        cg = _conv(_conv_taps(ug, hg, first), wg, bg)
        cv = _conv(_conv_taps(uv, hv, first), wv, bv)
        a_ref[...] = _bf(0.5 * cg * (1.0 + lax.erf(cg * SQRT_HALF)) * cv)

    return pl.pallas_call(
        body, name=name, grid=(nc, S // tm),
        in_specs=[tile(0), halo(0), tile(nc), halo(nc), small(3, 0), small(3, nc), small(1, 0), small(1, nc)],
        out_specs=tile(0), out_shape=_sds((S, D_FF), BF16), compiler_params=_cp(2))(u, u, u, u, cw, cw, cb, cb)


def _conv_gelu_bwd(u, da, cw, cb, name):
    S = u.shape[0]
    tm = CONV_TM
    nc, tile, halo, small = _conv_specs(tm)

    def body(ug, hg, uv, hv, wg, wv, bg, bv, da_ref, dcg_ref, dcv_ref, dwg_ref, dwv_ref, dbg_ref, dbv_ref):
        first = pl.program_id(1) == 0
        tg = _conv_taps(ug, hg, first)
        tv = _conv_taps(uv, hv, first)
        cg = _conv(tg, wg, bg)
        cv = _conv(tv, wv, bv)
        phi = 0.5 * (1.0 + lax.erf(cg * SQRT_HALF))
        dav = da_ref[...]
        dcg = dav * cv * (phi + cg * jnp.exp(-0.5 * cg * cg) * INV_SQRT_2PI)
        dcv = dav * (cg * phi)
        dcg_ref[...] = dcg
        dcv_ref[...] = dcv
        for dc, taps, dw_ref, db_ref in ((dcg, tg, dwg_ref, dbg_ref), (dcv, tv, dwv_ref, dbv_ref)):
            _acc(db_ref, jnp.sum(dc, axis=0, keepdims=True))
            for j in range(3):
                _acc(dw_ref.at[j:j + 1, :], jnp.sum(dc * taps[j], axis=0, keepdims=True))

    return pl.pallas_call(
        body, name=name, grid=(nc, S // tm),
        in_specs=[tile(0), halo(0), tile(nc), halo(nc), small(3, 0), small(3, nc), small(1, 0), small(1, nc), tile(0)],
        out_specs=[tile(0), tile(0), small(3, 0), small(3, 0), small(1, 0), small(1, 0)],
        out_shape=[_sds((S, D_FF), F32)] * 2 + [_sds((3, D_FF), F32)] * 2 + [_sds((1, D_FF), F32)] * 2,
        compiler_params=_cp(2))(u, u, u, u, cw, cw, cb, cb, da)


def _conv_input_bwd(dcg, dcv, cw, name):
    S = dcg.shape[0]
    tm = CONV_TM
    nc, tile, _, small = _conv_specs(tm)
    nh = tm // HALO
    nt = S // tm

    def nxt(off):
        return pl.BlockSpec((HALO, CONV_CB), lambda c, i: (jnp.minimum((i + 1) * nh, S // HALO - 1), off + c))

    def body(g_ref, ng_ref, v_ref, nv_ref, wg, wv, dug_ref, duv_ref):
        last = pl.program_id(1) == nt - 1
        for dc_ref, n_ref, w_ref, du_ref in ((g_ref, ng_ref, wg, dug_ref), (v_ref, nv_ref, wv, duv_ref)):
            dc = dc_ref[...]
            row = lax.broadcasted_iota(jnp.int32, dc.shape, 0)
            n1 = jnp.where(last, 0.0, n_ref[0:1, :])
            n2 = jnp.where(last, 0.0, n_ref[1:2, :])
            d1 = jnp.where(row == tm - 1, n1, pltpu.roll(dc, tm - 1, 0))
            d2 = jnp.where(row == tm - 1, n2, jnp.where(row == tm - 2, n1, pltpu.roll(dc, tm - 2, 0)))
            du_ref[...] = _bf(w_ref[2:3, :] * dc + w_ref[1:2, :] * d1 + w_ref[0:1, :] * d2)

    return pl.pallas_call(
        body, name=name, grid=(nc, nt),
        in_specs=[tile(0), nxt(0), tile(0), nxt(0), small(3, 0), small(3, nc)],
        out_specs=[tile(0), tile(0)], out_shape=[_sds((S, D_FF), BF16)] * 2,
        compiler_params=_cp(2))(dcg, dcg, dcv, dcv, cw, cw)


def _row_tile(n, cap):
    best = n
    for t in range(16, cap + 1, 16):
        if n % t == 0:
            best = t
    return best if best <= cap else n


def _adamw(w, g, m, v, name):
    R, C = w.shape
    tr = _row_tile(R, max(16, (512 * 1024) // (4 * C) // 16 * 16))

    def body(w_ref, g_ref, m_ref, v_ref, d_ref, nm_ref, nv_ref):
        gv = g_ref[...]
        nm = ADAM_B1 * m_ref[...] + (1.0 - ADAM_B1) * gv
        nv = ADAM_B2 * v_ref[...] + (1.0 - ADAM_B2) * (gv * gv)
        m_hat = nm / (1.0 - ADAM_B1 ** ADAM_STEP)
        v_hat = nv / (1.0 - ADAM_B2 ** ADAM_STEP)
        d_ref[...] = -ADAM_LR * (m_hat / (jnp.sqrt(v_hat) + ADAM_EPS) + ADAM_WD * w_ref[...])
        nm_ref[...] = nm
        nv_ref[...] = nv

    spec = pl.BlockSpec((tr, C), lambda i: (i, 0))
    return pl.pallas_call(body, name=name, grid=(R // tr,), in_specs=[spec] * 4, out_specs=[spec] * 3,
                          out_shape=[_sds((R, C), F32)] * 3, compiler_params=_cp(1))(w, g, m, v)


def _pair_sum(gfull, rcv, c_idx, name):
    nb, R, C = gfull.shape
    half = R // 2
    tr = _row_tile(half, 256)
    nt = half // tr

    def body(c_ref, g_ref, r_ref, o_ref):
        o_ref[...] = _bf(g_ref[...] + r_ref[...])

    return pl.pallas_call(
        body, name=name,
        grid_spec=pltpu.PrefetchScalarGridSpec(
            num_scalar_prefetch=1, grid=(nb, nt),
            in_specs=[pl.BlockSpec((None, tr, C), lambda j, i, c_ref: (j, c_ref[0] * nt + i, 0)),
                      pl.BlockSpec((None, tr, C), lambda j, i, c_ref: (j, i, 0))],
            out_specs=pl.BlockSpec((None, tr, C), lambda j, i, c_ref: (j, i, 0))),
        out_shape=_sds((nb, half, C), BF16), compiler_params=_cp(2))(c_idx, gfull, rcv)


def _chip_sum(arrived, own, place, name):
    nb, H, C = arrived.shape
    tr = _row_tile(H, 256)
    nt = H // tr

    def body(pl_ref, *refs):
        o_ref = refs[nb + 1]
        me = pl_ref[0]
        acc = None
        for k in range(nb):
            term = jnp.where(me == k, refs[nb][...], refs[k][...]).astype(F32)
            acc = term if acc is None else acc + term
        o_ref[...] = acc

    def other(k):
        return pl.BlockSpec((None, tr, C), lambda i, p: (jnp.where(p[0] == k, (k + 1) % nb, k), i, 0))

    return pl.pallas_call(
        body, name=name,
        grid_spec=pltpu.PrefetchScalarGridSpec(
            num_scalar_prefetch=1, grid=(nt,),
            in_specs=[other(k) for k in range(nb)] + [pl.BlockSpec((None, tr, C), lambda i, p: (p[0], i, 0))],
            out_specs=pl.BlockSpec((tr, C), lambda i, p: (p[1] * nt + i, 0))),
        out_shape=_sds((2 * H, C), F32), compiler_params=_cp(1))(place, *([arrived] * nb), own)


def _cast_into_slot(shard, place, name):
    R, C = shard.shape
    tr = _row_tile(R, 256)

    def body(pl_ref, s_ref, o_ref):
        o_ref[...] = _bf(s_ref[...])

    return pl.pallas_call(
        body, name=name,
        grid_spec=pltpu.PrefetchScalarGridSpec(
            num_scalar_prefetch=1, grid=(R // tr,),
            in_specs=[pl.BlockSpec((tr, C), lambda i, p: (i, 0))],
            out_specs=pl.BlockSpec((None, tr, C), lambda i, p: (p[0], i, 0))),
        out_shape=_sds((N_CHIPS, R, C), BF16), compiler_params=_cp(1))(place, shard)


ANY = pl.BlockSpec(memory_space=pl.ANY)


def _place():
    x, y, c = lax.axis_index("x"), lax.axis_index("y"), lax.axis_index("c")
    chips = [(1 - x, y), (x, 1 - y), (1 - x, 1 - y)]
    return x, y, c, chips


def _chip_id(px, py):
    return 2 * px + py


def _remote(src, dst, send_sems, recv_sems, k, to):
    return pltpu.make_async_remote_copy(src_ref=src, dst_ref=dst, send_sem=send_sems.at[k], recv_sem=recv_sems.at[k],
                                        device_id=to, device_id_type=MESH)


def _gather_weights(slots, wholes, name):
    ns, nw = len(slots), len(wholes)
    n = ns + nw

    def body(*refs):
        ins = refs[ns:n]
        outs = refs[n:2 * n]
        send_sems, recv_sems, local_sems = refs[2 * n:]
        x, y, c, chips = _place()
        me = _chip_id(x, y)
        sib = (x, y, 1 - c)
        local = [pltpu.make_async_copy(ins[b], outs[ns + b].at[me], local_sems.at[b]) for b in range(nw)]
        for cp in local:
            cp.start()
        sent = []
        for a in range(n):
            R = outs[a].shape[1]
            rows = pl.ds(c * (R // 2), R // 2) if a < ns else pl.ds(0, R)
            src = outs[a].at[me, rows] if a < ns else ins[a - ns]
            for j, chip in enumerate(chips):
                cp = _remote(src, outs[a].at[me, rows], send_sems, recv_sems, 6 * a + j, (*chip, c))
                cp.start()
                sent.append(cp)
        for a in range(n):
            R = outs[a].shape[1]
            rows = pl.ds(c * (R // 2), R // 2) if a < ns else pl.ds(0, R)
            for j, chip in enumerate(chips):
                landed = outs[a].at[_chip_id(*chip), rows]
                _remote(landed, landed, send_sems, recv_sems, 6 * a + j, (*chip, c)).wait_recv()
                if a < ns:
                    cp = _remote(landed, landed, send_sems, recv_sems, 6 * a + 3 + j, sib)
                    cp.start()
                    sent.append(cp)
        for a in range(ns):
            R = outs[a].shape[1]
            other = pl.ds((1 - c) * (R // 2), R // 2)
            for j, chip in enumerate(chips):
                passed = outs[a].at[_chip_id(*chip), other]
                _remote(passed, passed, send_sems, recv_sems, 6 * a + 3 + j, sib).wait_recv()
        for cp in sent:
            cp.wait_send()
        for cp in local:
            cp.wait()

    return pl.pallas_call(
        body, name=name, in_specs=[ANY] * n, out_specs=[ANY] * n,
        out_shape=[_sds(s.shape, s.dtype) for s in slots] + [_sds((N_CHIPS, *s.shape), s.dtype) for s in wholes],
        input_output_aliases={a: a for a in range(ns)},
        scratch_shapes=[pltpu.SemaphoreType.DMA((6 * n,)), pltpu.SemaphoreType.DMA((6 * n,)),
                        pltpu.SemaphoreType.DMA((max(nw, 1),))])(*slots, *wholes)


def _pair_exchange(grads, name):
    n = len(grads)

    def body(*refs):
        ins, outs = refs[:n], refs[n:2 * n]
        send_sems, recv_sems = refs[2 * n:]
        x, y, c, _ = _place()
        cps = []
        for a in range(n):
            half = ins[a].shape[1] // 2
            cp = _remote(ins[a].at[:, pl.ds((1 - c) * half, half), :], outs[a], send_sems, recv_sems, a, (x, y, 1 - c))
            cp.start()
            cps.append(cp)
        for cp in cps:
            cp.wait()

    return pl.pallas_call(
        body, name=name, in_specs=[ANY] * n, out_specs=[ANY] * n,
        out_shape=[_sds((g.shape[0], g.shape[1] // 2, g.shape[2]), g.dtype) for g in grads],
        scratch_shapes=[pltpu.SemaphoreType.DMA((n,)), pltpu.SemaphoreType.DMA((n,))])(*grads)


def _chip_exchange(parts, name):
    n = len(parts)

    def body(*refs):
        ins, outs = refs[:n], refs[n:2 * n]
        send_sems, recv_sems = refs[2 * n:]
        x, y, c, chips = _place()
        me = _chip_id(x, y)
        cps = []
        for a in range(n):
            for j, chip in enumerate(chips):
                cp = _remote(ins[a].at[_chip_id(*chip)], outs[a].at[me], send_sems, recv_sems, 3 * a + j, (*chip, c))
                cp.start()
                cps.append(cp)
        for a in range(n):
            for j, chip in enumerate(chips):
                landed = outs[a].at[_chip_id(*chip)]
                _remote(landed, landed, send_sems, recv_sems, 3 * a + j, (*chip, c)).wait_recv()
        for cp in cps:
            cp.wait_send()

    return pl.pallas_call(
        body, name=name, in_specs=[ANY] * n, out_specs=[ANY] * n,
        out_shape=[_sds(p.shape, p.dtype) for p in parts],
        scratch_shapes=[pltpu.SemaphoreType.DMA((3 * n,)), pltpu.SemaphoreType.DMA((3 * n,))])(*parts)


def _pair_concat(fulls, name):
    n = len(fulls)

    def body(*refs):
        outs = refs[n:2 * n]
        send_sems, recv_sems = refs[2 * n:]
        x, y, c, _ = _place()
        cps = []
        for a in range(n):
            H = outs[a].shape[0] // 2
            mine = outs[a].at[pl.ds(c * H, H)]
            cp = _remote(mine, mine, send_sems, recv_sems, a, (x, y, 1 - c))
            cp.start()
            cps.append(cp)
        for a, cp in enumerate(cps):
            H = outs[a].shape[0] // 2
            other = outs[a].at[pl.ds((1 - c) * H, H)]
            _remote(other, other, send_sems, recv_sems, a, (x, y, 1 - c)).wait_recv()
            cp.wait_send()

    return pl.pallas_call(
        body, name=name, in_specs=[ANY] * n, out_specs=[ANY] * n,
        out_shape=[_sds(f.shape, f.dtype) for f in fulls], input_output_aliases={a: a for a in range(n)},
        scratch_shapes=[pltpu.SemaphoreType.DMA((n,)), pltpu.SemaphoreType.DMA((n,))])(*fulls)


def _all_sum(pack, name):
    R, C = pack.shape

    def body(p_ref, o_ref, buf, send_sems, recv_sems):
        x, y, c, _ = _place()
        me = 4 * x + 2 * y + c
        buf[me] = p_ref[...]
        cps = []
        for k in range(1, N_DEV):
            to = (x ^ (k >> 2), y ^ ((k >> 1) & 1), c ^ (k & 1))
            cp = _remote(p_ref, buf.at[me], send_sems, recv_sems, k - 1, to)
            cp.start()
            cps.append(cp)
        for k in range(1, N_DEV):
            frm = (x ^ (k >> 2), y ^ ((k >> 1) & 1), c ^ (k & 1))
            slot = buf.at[4 * frm[0] + 2 * frm[1] + frm[2]]
            _remote(slot, slot, send_sems, recv_sems, k - 1, frm).wait_recv()
        acc = buf[0]
        for k in range(1, N_DEV):
            acc = acc + buf[k]
        o_ref[...] = acc
        for cp in cps:
            cp.wait_send()

    vm = pl.BlockSpec(memory_space=pltpu.VMEM)
    return pl.pallas_call(
        body, name=name, in_specs=[vm], out_specs=vm, out_shape=_sds((R, C), F32),
        scratch_shapes=[pltpu.VMEM((N_DEV, R, C), F32), pltpu.SemaphoreType.DMA((N_DEV - 1,)),
                        pltpu.SemaphoreType.DMA((N_DEV - 1,))])(pack)


def _local_step(xs, tgt, pre_mix_norm, W_in, rel_bias, hgrn_lb_raw, hgrn_norm, W_a, W_h, W_out, post_mix_norm,
                pre_ffn_norm, W_up, conv_w, conv_b, W_down, post_ffn_norm):
    h1 = _norm_fwd(xs, pre_mix_norm, "pre_mix_norm")
    proj = _mm_nn_blk(h1, W_in, "proj_in")
    biases = _relbias_fwd(rel_bias, "rel_bias_fwd")
    fw = [_attn_fwd(proj, biases[g], g, f"attn_fwd{g}") for g in range(N_GROUPS)]
    y, lse = _attn_merge([t[0] for t in fw], [t[1] for t in fw], "attn_merge")
    yh, o_h, states = _hgrn_fwd(proj, hgrn_lb_raw, hgrn_norm, "hgrn_fwd")
    za = _mm_nn_blk(y, W_a, "branch_attn")
    zh = _mm_nn_blk(yh, W_h, "branch_hgrn")
    merged = _merge_fwd(za, zh, proj, "merge_fwd")
    mo = _mm_nn(merged, W_out, "mix_out")
    x1, h2 = _mix_out(mo, xs, post_mix_norm, pre_ffn_norm, "mix_residual")
    u = _mm_nn_blk(h2, W_up, "ffn_up")
    a = _conv_gelu_fwd(u, conv_w, conv_b, "conv_gelu_fwd")
    ffo = _mm_nn(a, W_down, "ffn_down")
    dx2, dff, g_post_ffn, loss = _loss_head(ffo, x1, tgt, post_ffn_norm, "loss_head")

    da = _mm_nt(dff, W_down, "d_ffn_act")
    g_down = _mm_tn(a, dff, "g_w_down")
    dcg, dcv, gwg, gwv, gbg, gbv = _conv_gelu_bwd(u, da, conv_w, conv_b, "conv_gelu_bwd")
    g_conv_w = jnp.concatenate([gwg, gwv], axis=1)
    g_conv_b = jnp.concatenate([gbg, gbv], axis=1)
    du = jnp.concatenate(_conv_input_bwd(dcg, dcv, conv_w, "conv_input_bwd"), axis=1)
    dh2 = _mm_nt_blk(du, W_up, "d_ffn_in")
    g_up = _mm_tn_blk(h2, du, N_CHIPS, "g_w_up")
    dx1, g_pre_ffn = _prenorm_bwd(dh2, x1, pre_ffn_norm, dx2, "pre_ffn_norm_bwd")
    dmo, g_post_mix = _postnorm_bwd(dx1, mo, post_mix_norm, "post_mix_norm_bwd")
    dmerged = _mm_nt(dmo, W_out, "d_merged")
    g_out = _mm_tn(merged, dmo, "g_w_out")
    dza, dzh, dg0, dg1 = _merge_bwd(dmerged, za, zh, proj, "merge_bwd")
    dy = _mm_nt_blk(dza, W_a, "d_attn_out")
    g_a = _mm_tn_blk(y, dza, N_CHIPS, "g_w_branch_attn")
    dyh = _mm_nt_blk(dzh, W_h, "d_hgrn_out")
    g_h = _mm_tn_blk(yh, dzh, N_CHIPS, "g_w_branch_hgrn")
    dqkv, dbs = [], []
    for g in range(N_GROUPS):
        parts, db = _attn_bwd(proj, biases[g], lse, y, dy, g, f"attn_bwd{g}")
        dqkv += parts
        dbs.append(db)
    g_rel_bias = _relbias_bwd(dbs, "rel_bias_bwd")
    dhg, g_lb_raw, g_hgrn_norm = _hgrn_bwd(proj, hgrn_lb_raw, hgrn_norm, o_h, states, dyh, "hgrn_bwd")
    dproj = jnp.concatenate([*[_bf(t) for t in dqkv], *dhg, dg0, dg1], axis=1)
    dh1 = _mm_nt_blk(dproj, W_in, "d_proj_in")
    g_in = _mm_tn_blk(h1, dproj, N_CHIPS, "g_w_in")
    grad_x, g_pre_mix = _prenorm_bwd(dh1, xs, pre_mix_norm, dx1, "pre_mix_norm_bwd")
    small = dict(pre_mix_norm=g_pre_mix, rel_bias=g_rel_bias, hgrn_lb_raw=g_lb_raw, hgrn_norm=g_hgrn_norm,
                 post_mix_norm=g_post_mix, pre_ffn_norm=g_pre_ffn, conv_w=g_conv_w, conv_b=g_conv_b,
                 post_ffn_norm=g_post_ffn)
    big = dict(w_in=g_in, w_up=g_up, w_down=g_down.reshape(N_CHIPS, D_FF // N_CHIPS, D_MODEL),
               w_out=g_out.reshape(N_CHIPS, D_MODEL // N_CHIPS, D_MODEL), w_branch_attn=g_a, w_branch_hgrn=g_h)
    return loss, grad_x, small, big


SMALL = ("pre_mix_norm", "rel_bias", "hgrn_lb_raw", "hgrn_norm", "post_mix_norm", "pre_ffn_norm", "conv_w", "conv_b",
         "post_ffn_norm")
BIG = ("w_in", "w_up", "w_down", "w_out", "w_branch_attn", "w_branch_hgrn")
WEIGHTS = ("pre_mix_norm", "w_in", "rel_bias", "hgrn_lb_raw", "hgrn_norm", "w_branch_attn", "w_branch_hgrn", "w_out",
           "post_mix_norm", "pre_ffn_norm", "w_up", "conv_w", "conv_b", "w_down", "post_ffn_norm")


def kernel(x, pre_mix_norm, w_in, rel_bias, hgrn_lb_raw, hgrn_norm, w_branch_attn, w_branch_hgrn, w_out, post_mix_norm, pre_ffn_norm, w_up, conv_w, conv_b, w_down, post_ffn_norm, loss_target, m_pre_mix_norm, m_w_in, m_rel_bias, m_hgrn_lb_raw, m_hgrn_norm, m_w_branch_attn, m_w_branch_hgrn, m_w_out, m_post_mix_norm, m_pre_ffn_norm, m_w_up, m_conv_w, m_conv_b, m_w_down, m_post_ffn_norm, v_pre_mix_norm, v_w_in, v_rel_bias, v_hgrn_lb_raw, v_hgrn_norm, v_w_branch_attn, v_w_branch_hgrn, v_w_out, v_post_mix_norm, v_pre_ffn_norm, v_w_up, v_conv_w, v_conv_b, v_w_down, v_post_ffn_norm):
    w = dict(pre_mix_norm=pre_mix_norm, w_in=w_in, rel_bias=rel_bias, hgrn_lb_raw=hgrn_lb_raw, hgrn_norm=hgrn_norm,
             w_branch_attn=w_branch_attn, w_branch_hgrn=w_branch_hgrn, w_out=w_out, post_mix_norm=post_mix_norm,
             pre_ffn_norm=pre_ffn_norm, w_up=w_up, conv_w=conv_w, conv_b=conv_b, w_down=w_down,
             post_ffn_norm=post_ffn_norm)
    m = dict(pre_mix_norm=m_pre_mix_norm, w_in=m_w_in, rel_bias=m_rel_bias, hgrn_lb_raw=m_hgrn_lb_raw,
             hgrn_norm=m_hgrn_norm, w_branch_attn=m_w_branch_attn, w_branch_hgrn=m_w_branch_hgrn, w_out=m_w_out,
             post_mix_norm=m_post_mix_norm, pre_ffn_norm=m_pre_ffn_norm, w_up=m_w_up, conv_w=m_conv_w,
             conv_b=m_conv_b, w_down=m_w_down, post_ffn_norm=m_post_ffn_norm)
    v = dict(pre_mix_norm=v_pre_mix_norm, w_in=v_w_in, rel_bias=v_rel_bias, hgrn_lb_raw=v_hgrn_lb_raw,
             hgrn_norm=v_hgrn_norm, w_branch_attn=v_w_branch_attn, w_branch_hgrn=v_w_branch_hgrn, w_out=v_w_out,
             post_mix_norm=v_post_mix_norm, pre_ffn_norm=v_pre_ffn_norm, w_up=v_w_up, conv_w=v_conv_w,
             conv_b=v_conv_b, w_down=v_w_down, post_ffn_norm=v_post_ffn_norm)
    shard2d = {n: (w[n][0] if w[n].ndim == 3 else w[n]) for n in WEIGHTS}
    chip = 2 * lax.axis_index("x") + lax.axis_index("y")
    core = lax.axis_index("c")

    place = jnp.stack([chip, core]).astype(jnp.int32)
    gathered = _gather_weights([_cast_into_slot(shard2d[n], place, f"cast_{n}") for n in BIG], [shard2d["conv_w"]],
                               "gather_weights")
    W = dict(zip(BIG, gathered[:-1]))
    conv_w_full = gathered[-1].transpose(1, 0, 2).reshape(3, 2 * D_FF)
    loss, grad_x, small, big = _local_step(
        x[0], loss_target[0], pre_mix_norm, W["w_in"], rel_bias, hgrn_lb_raw, hgrn_norm, W["w_branch_attn"],
        W["w_branch_hgrn"], W["w_out"].reshape(D_MODEL, D_MODEL), post_mix_norm, pre_ffn_norm, W["w_up"],
        conv_w_full, conv_b, W["w_down"].reshape(D_FF, D_MODEL), post_ffn_norm)

    flat = [small[n].reshape(-1) for n in SMALL] + [loss.reshape(-1)]
    sizes = [t.shape[0] for t in flat]
    summed = _all_sum(jnp.concatenate(flat).reshape(-1, LANES), "sum_small").reshape(-1)
    offs = [sum(sizes[:i]) for i in range(len(sizes))]
    grads = {}
    for n, o, sz in zip(SMALL, offs, sizes):
        grads[n] = summed[o:o + sz].reshape(small[n].shape)
    loss_total = summed[offs[-1]]
    cw = 2 * D_FF // N_CHIPS
    grads["conv_w"] = lax.dynamic_slice(grads["conv_w"], (0, chip * cw), (3, cw))

    gl = [big[n] for n in BIG]
    from_sibling = _pair_exchange(gl, "pair_exchange")
    c_idx = core.reshape(1).astype(jnp.int32)
    partial = [_pair_sum(g, r, c_idx, f"pair_sum_{n}") for n, g, r in zip(BIG, gl, from_sibling)]
    arrived = _chip_exchange(partial, "chip_exchange")
    halves = [_chip_sum(r, p, place, f"chip_sum_{n}") for n, r, p in zip(BIG, arrived, partial)]
    for n, g in zip(BIG, _pair_concat(halves, "pair_concat")):
        grads[n] = g

    out_g, out_d, out_m, out_v = [], [], [], []
    for n in WEIGHTS:
        d2, m2, v2 = _adamw(shard2d[n], grads[n], m[n].reshape(shard2d[n].shape), v[n].reshape(shard2d[n].shape),
                            f"adamw_{n}")
        shape = w[n].shape
        out_g.append(grads[n].reshape(shape))
        out_d.append(d2.reshape(shape))
        out_m.append(m2.reshape(shape))
        out_v.append(v2.reshape(shape))
    return (loss_total, grad_x[None], *out_g, *out_d, *out_m, *out_v)
```

```python
import functools
import math

import jax
import jax.numpy as jnp
from jax import lax
from jax.experimental import pallas as pl
from jax.experimental.pallas import tpu as pltpu

F32 = jnp.float32
BF16 = jnp.bfloat16
MESH = pl.DeviceIdType.MESH

D_MODEL = 1024
N_GROUPS = 3
DILATIONS = (1, 4, 16)
HEADS = 8
HEAD_DIM = 64
GROUP_W = HEADS * HEAD_DIM
QKV_W = N_GROUPS * 3 * GROUP_W
BLK = 128
NEG_INF = -1e30
NUM_BUCKETS = 32
MAX_EXACT = 16
MAX_DISTANCE = 2048
HG_HEADS = 4
HG_DK = 128
HG_W = HG_HEADS * HG_DK
HG_CHUNK = 32
HG_TILE = 256
IN_W = QKV_W + 4 * HG_W + 2 * D_MODEL
D_FF = 2816
EPS = 1e-6
N_CHIPS = 4
N_DEV = 8
LANES = 128

ADAM_LR, ADAM_B1, ADAM_B2, ADAM_EPS, ADAM_WD, ADAM_STEP = 0.001, 0.9, 0.999, 1e-08, 0.01, 10

VMEM_LIMIT = 56 * 1024 * 1024


def _cp(n_axes):
    return pltpu.CompilerParams(dimension_semantics=("arbitrary",) * n_axes, vmem_limit_bytes=VMEM_LIMIT)


def _sds(shape, dtype):
    return jax.ShapeDtypeStruct(tuple(shape), dtype)


def _sigmoid(v):
    return 1.0 / (1.0 + jnp.exp(-v))


def _bf(v):
    return v.astype(BF16)


def _dot(a, b, dims):
    return lax.dot_general(a, b, (dims, ((), ())), preferred_element_type=F32)


NN = ((1,), (0,))
NT = ((1,), (1,))
TN = ((0,), (0,))

ANY = pl.BlockSpec(memory_space=pl.ANY)


class _Plan:
    def __init__(self, copies, n_sems, ins=(), inouts=(), outs=()):
        self.copies, self.n_sems = copies, n_sems
        self.ins, self.inouts, self.outs = list(ins), list(inouts), list(outs)


def _call(body, plans=None, *, name, grid, in_specs, out_specs, out_shape, args, scratch_shapes=()):
    plans = list(plans or ())
    in_specs, out_specs, out_shape = list(in_specs), list(out_specs), list(out_shape)
    scratch_shapes = list(scratch_shapes)
    n_in, n_out, n_scr = len(in_specs), len(out_specs), len(scratch_shapes)
    x_in, x_out, aliases, spans = [], [], {}, []
    for p in plans:
        i0, o0 = len(x_in), len(x_out)
        x_in += p.ins
        for a in p.inouts:
            aliases[n_in + len(x_in)] = n_out + len(x_out)
            x_in.append(a)
            x_out.append(_sds(a.shape, a.dtype))
        x_out += p.outs
        spans.append((i0, len(p.ins), o0, len(p.inouts), len(p.outs)))
    sems = [pltpu.SemaphoreType.DMA((p.n_sems,)) for p in plans for _ in range(3)]

    def wrapped(*refs):
        xi = refs[n_in:n_in + len(x_in)]
        base = n_in + len(x_in)
        xo = refs[base + n_out:base + n_out + len(x_out)]
        sbase = base + n_out + len(x_out)
        xs = refs[sbase + n_scr:]
        ids = [pl.program_id(k) for k in range(len(grid))]
        first = functools.reduce(jnp.logical_and, [i == 0 for i in ids])
        last = functools.reduce(jnp.logical_and, [i == g - 1 for i, g in zip(ids, grid)])

        def descriptors(k):
            i0, ni, o0, nio, no = spans[k]
            return plans[k].copies(xi[i0:i0 + ni], xo[o0:o0 + nio], xo[o0 + nio:o0 + nio + no], *xs[3 * k:3 * k + 3])

        @pl.when(first)
        def _():
            for k in range(len(plans)):
                sends, _, local = descriptors(k)
                for cp in (*sends, *local):
                    cp.start()

        body(*refs[:n_in], *refs[base:base + n_out], *refs[sbase:sbase + n_scr])

        @pl.when(last)
        def _():
            for k in range(len(plans)):
                sends, recvs, local = descriptors(k)
                for cp in recvs:
                    cp.wait_recv()
                for cp in sends:
                    cp.wait_send()
                for cp in local:
                    cp.wait()

    res = pl.pallas_call(
        wrapped if plans else body, name=name, grid=grid, in_specs=in_specs + [ANY] * len(x_in),
        out_specs=out_specs + [ANY] * len(x_out), out_shape=out_shape + x_out, input_output_aliases=aliases,
        scratch_shapes=scratch_shapes + sems, compiler_params=_cp(len(grid)))(*args, *x_in)
    res = list(res)
    carried = [res[n_out + o0:n_out + o0 + nio + no] for (_, _, o0, nio, no) in spans]
    return res[:n_out], carried


def _mm_nn_blk(a, wg, name, tm=512, plans=None):
    M, K = a.shape
    nb, _, Nb = wg.shape

    def body(a_ref, w_ref, o_ref):
        o_ref[...] = _dot(_bf(a_ref[...]), w_ref[...], NN)

    (out,), carried = _call(
        body, plans, name=name, grid=(nb, M // tm),
        in_specs=[pl.BlockSpec((tm, K), lambda j, i: (i, 0)), pl.BlockSpec((None, K, Nb), lambda j, i: (j, 0, 0))],
        out_specs=[pl.BlockSpec((tm, Nb), lambda j, i: (i, j))],
        out_shape=[_sds((M, nb * Nb), F32)], args=(a, wg))
    return out if plans is None else (out, carried)


def _mm_nt_blk(dy, wg, name, tm=1024, plans=None):
    M = dy.shape[0]
    nb, K, Nb = wg.shape

    def body(dy_ref, w_ref, o_ref):
        j = pl.program_id(1)
        r = _dot(_bf(dy_ref[...]), w_ref[...], NT)

        @pl.when(j == 0)
        def _():
            o_ref[...] = r

        @pl.when(j > 0)
        def _():
            o_ref[...] += r

    (out,), carried = _call(
        body, plans, name=name, grid=(M // tm, nb),
        in_specs=[pl.BlockSpec((tm, Nb), lambda i, j: (i, j)), pl.BlockSpec((None, K, Nb), lambda i, j: (j, 0, 0))],
        out_specs=[pl.BlockSpec((tm, K), lambda i, j: (i, 0))],
        out_shape=[_sds((M, K), F32)], args=(dy, wg))
    return out if plans is None else (out, carried)


def _mm_tn_blk(x, dy, nb, name, tk=512):
    T, Mx = x.shape
    Nb = dy.shape[1] // nb

    def body(x_ref, dy_ref, o_ref):
        t = pl.program_id(1)
        r = _dot(_bf(x_ref[...]), _bf(dy_ref[...]), TN)

        @pl.when(t == 0)
        def _():
            o_ref[...] = r

        @pl.when(t > 0)
        def _():
            o_ref[...] += r

    return pl.pallas_call(
        body, name=name, grid=(nb, T // tk),
        in_specs=[pl.BlockSpec((tk, Mx), lambda j, t: (t, 0)), pl.BlockSpec((tk, Nb), lambda j, t: (t, j))],
        out_specs=pl.BlockSpec((None, Mx, Nb), lambda j, t: (j, 0, 0)),
        out_shape=_sds((nb, Mx, Nb), F32), compiler_params=_cp(2))(x, dy)


def _mm_nn(a, w, name, tm=512):
    M, K = a.shape
    N = w.shape[1]

    def body(a_ref, w_ref, o_ref):
        o_ref[...] = _dot(_bf(a_ref[...]), w_ref[...], NN)

    return pl.pallas_call(
        body, name=name, grid=(M // tm,),
        in_specs=[pl.BlockSpec((tm, K), lambda i: (i, 0)), pl.BlockSpec((K, N), lambda i: (0, 0))],
        out_specs=pl.BlockSpec((tm, N), lambda i: (i, 0)),
        out_shape=_sds((M, N), F32), compiler_params=_cp(1))(a, w)


def _mm_nt(dy, w, name, tm=512):
    M, N = dy.shape
    K = w.shape[0]

    def body(dy_ref, w_ref, o_ref):
        o_ref[...] = _dot(_bf(dy_ref[...]), w_ref[...], NT)

    return pl.pallas_call(
        body, name=name, grid=(M // tm,),
        in_specs=[pl.BlockSpec((tm, N), lambda i: (i, 0)), pl.BlockSpec((K, N), lambda i: (0, 0))],
        out_specs=pl.BlockSpec((tm, K), lambda i: (i, 0)),
        out_shape=_sds((M, K), F32), compiler_params=_cp(1))(dy, w)


def _mm_tn(x, dy, name, tk=512):
    T, Mx = x.shape
    N = dy.shape[1]

    def body(x_ref, dy_ref, o_ref):
        t = pl.program_id(0)
        r = _dot(_bf(x_ref[...]), _bf(dy_ref[...]), TN)

        @pl.when(t == 0)
        def _():
            o_ref[...] = r

        @pl.when(t > 0)
        def _():
            o_ref[...] += r

    return pl.pallas_call(
        body, name=name, grid=(T // tk,),
        in_specs=[pl.BlockSpec((tk, Mx), lambda t: (t, 0)), pl.BlockSpec((tk, N), lambda t: (t, 0))],
        out_specs=pl.BlockSpec((Mx, N), lambda t: (0, 0)),
        out_shape=_sds((Mx, N), F32), compiler_params=_cp(1))(x, dy)


def _tile(arr, bw, col=lambda c: 0):
    return ("tile", arr, bw, col)


def _full(arr):
    return ("full", arr)


def _out_tile(width, dtype, bw, col=lambda c: 0):
    return ("tile", width, dtype, bw, col)


def _out_acc(rows, width, bw, col=lambda c: 0):
    return ("acc", rows, width, bw, col)


def _rows_call(name, body, n_rows, tm, ncol, ins, outs, plans=None):
    in_specs, args = [], []
    for e in ins:
        if e[0] == "tile":
            _, arr, bw, col = e
            in_specs.append(pl.BlockSpec((tm, bw), functools.partial(lambda c, i, col: (i, col(c)), col=col)))
        else:
            arr = e[1]
            in_specs.append(pl.BlockSpec(arr.shape, functools.partial(lambda c, i, nd: (0,) * nd, nd=arr.ndim)))
        args.append(arr)
    out_specs, out_shape = [], []
    for e in outs:
        if e[0] == "tile":
            _, width, dtype, bw, col = e
            out_specs.append(pl.BlockSpec((tm, bw), functools.partial(lambda c, i, col: (i, col(c)), col=col)))
            out_shape.append(_sds((n_rows, width), dtype))
        else:
            _, rows, width, bw, col = e
            out_specs.append(pl.BlockSpec((rows, bw), functools.partial(lambda c, i, col: (0, col(c)), col=col)))
            out_shape.append(_sds((rows, width), F32))
    out, carried = _call(body, plans, name=name, grid=(ncol, n_rows // tm), in_specs=in_specs, out_specs=out_specs,
                         out_shape=out_shape, args=args)
    return out if plans is None else (out, carried)


def _acc(ref, val):
    i = pl.program_id(1)

    @pl.when(i == 0)
    def _():
        ref[...] = val

    @pl.when(i > 0)
    def _():
        ref[...] += val


def _rinv(z):
    return lax.rsqrt(jnp.mean(z * z, axis=-1, keepdims=True) + EPS)


def _norm_bwd(dy, zhat, r, w):
    dyw = dy * w
    return r * (dyw - zhat * jnp.mean(dyw * zhat, axis=-1, keepdims=True))


def _norm_fwd(x, w, name):
    def body(x_ref, w_ref, h_ref):
        xv = x_ref[...]
        h_ref[...] = _bf(xv * _rinv(xv) * w_ref[...])

    return _rows_call(name, body, x.shape[0], 512, 1, [_tile(x, D_MODEL), _full(w)],
                      [_out_tile(D_MODEL, BF16, D_MODEL)])[0]


def _prenorm_bwd(dh, xin, w, dres, name, plans=None):
    def body(dh_ref, x_ref, w_ref, dres_ref, dx_ref, dw_ref):
        xv = x_ref[...]
        r = _rinv(xv)
        xhat = xv * r
        dhv = dh_ref[...]
        dx_ref[...] = dres_ref[...] + _norm_bwd(dhv, xhat, r, w_ref[...])
        _acc(dw_ref, jnp.sum(dhv * xhat, axis=0, keepdims=True))

    return _rows_call(name, body, xin.shape[0], 512, 1,
                      [_tile(dh, D_MODEL), _tile(xin, D_MODEL), _full(w), _tile(dres, D_MODEL)],
                      [_out_tile(D_MODEL, F32, D_MODEL), _out_acc(1, D_MODEL, D_MODEL)], plans)


def _postnorm_bwd(dout, z, w, name):
    def body(do_ref, z_ref, w_ref, dz_ref, dw_ref):
        zv = z_ref[...]
        r = _rinv(zv)
        zhat = zv * r
        dov = do_ref[...]
        dz_ref[...] = _bf(_norm_bwd(dov, zhat, r, w_ref[...]))
        _acc(dw_ref, jnp.sum(dov * zhat, axis=0, keepdims=True))

    return _rows_call(name, body, z.shape[0], 512, 1, [_tile(dout, D_MODEL), _tile(z, D_MODEL), _full(w)],
                      [_out_tile(D_MODEL, BF16, D_MODEL), _out_acc(1, D_MODEL, D_MODEL)])


def _t5_bucket(dist):
    n = jnp.maximum(dist, 0)
    nf = jnp.maximum(n, 1).astype(F32)
    large = MAX_EXACT + (jnp.log(nf / MAX_EXACT) / math.log(MAX_DISTANCE / MAX_EXACT)
                         * (NUM_BUCKETS - MAX_EXACT)).astype(jnp.int32)
    large = jnp.minimum(large, NUM_BUCKETS - 1)
    return jnp.where(n < MAX_EXACT, n, large)


def _band_rel():
    return jnp.arange(BLK)[:, None] + BLK - jnp.arange(2 * BLK)[None, :]


def _band_mask(n):
    row = lax.broadcasted_iota(jnp.int32, (BLK, 2 * BLK), 0)
    col = lax.broadcasted_iota(jnp.int32, (BLK, 2 * BLK), 1)
    rel = row + BLK - col
    return (rel >= 0) & (rel <= BLK) & ((col >= BLK) | (n > 0))


RES_UNROLL = 4


def _heads_per_step(d):
    return HEADS if d == 1 else LANES // HEAD_DIM


def _sub_rows(r, d):
    return pl.ds(r, BLK, stride=d) if d > 1 else pl.ds(0, BLK)


def _for_residues(d, fn):
    if d <= RES_UNROLL:
        for r in range(d):
            fn(r)
    else:
        def group(i, carry):
            for k in range(RES_UNROLL):
                fn(i * RES_UNROLL + k)
            return carry

        lax.fori_loop(0, d // RES_UNROLL, group, 0)


def _attn_specs(d, g, qblock):
    cw = _heads_per_step(d) * HEAD_DIM

    def col(part, hp):
        return (g * 3 + part) * (GROUP_W // cw) + hp

    def cur(part):
        return pl.BlockSpec((d * BLK, cw), lambda hp, n: (qblock(n), col(part, hp)))

    def prev(part):
        return pl.BlockSpec((d * BLK, cw), lambda hp, n: (jnp.maximum(qblock(n) - 1, 0), col(part, hp)))

    return cur, prev


def _attn_fwd(proj, bias, g, name, plans=None):
    S = proj.shape[0]
    d = DILATIONS[g]
    NB = S // (d * BLK)
    hps = _heads_per_step(d)

    def body(q_ref, kp_ref, kc_ref, vp_ref, vc_ref, b_ref, o_ref, lse_ref):
        hp = pl.program_id(0)
        mask = _band_mask(pl.program_id(1))

        def residue(r):
            rows = _sub_rows(r, d)
            q2 = q_ref[rows, :]
            k2 = jnp.concatenate([kp_ref[rows, :], kc_ref[rows, :]], axis=0)
            v2 = jnp.concatenate([vp_ref[rows, :], vc_ref[rows, :]], axis=0)
            outs, lses = [], []
            for hh in range(hps):
                hs = slice(hh * HEAD_DIM, (hh + 1) * HEAD_DIM)
                s = _dot(_bf(q2[:, hs]), _bf(k2[:, hs]), NT) * (HEAD_DIM ** -0.5) + b_ref[hp * hps + hh]
                s = jnp.where(mask, s, NEG_INF)
                m = jnp.max(s, axis=-1, keepdims=True)
                p = jnp.exp(s - m)
                l = jnp.sum(p, axis=-1, keepdims=True)
                outs.append(_dot(_bf(p), _bf(v2[:, hs]), NN) / l)
                lses.append(jnp.broadcast_to(m + jnp.log(l), (BLK, HEAD_DIM)))
            o_ref[rows, :] = jnp.concatenate(outs, axis=1)
            lse_ref[rows, :] = jnp.concatenate(lses, axis=1)

        _for_residues(d, residue)

    cur, prev = _attn_specs(d, g, lambda n: n)
    out = pl.BlockSpec((d * BLK, hps * HEAD_DIM), lambda hp, n: (n, hp))
    res, carried = _call(
        body, plans, name=name, grid=(HEADS // hps, NB),
        in_specs=[cur(0), prev(1), cur(1), prev(2), cur(2),
                  pl.BlockSpec((HEADS, BLK, 2 * BLK), lambda hp, n: (0, 0, 0))],
        out_specs=[out, out], out_shape=[_sds((S, GROUP_W), F32)] * 2,
        args=(proj, proj, proj, proj, proj, bias))
    return res if plans is None else (res, carried)


def _attn_merge(os_, lses, name):
    def body(o0, o1, o2, l0, l1, l2, y_ref, lse_ref):
        a, b, c = l0[...], l1[...], l2[...]
        m = jnp.maximum(jnp.maximum(a, b), c)
        ea, eb, ec = jnp.exp(a - m), jnp.exp(b - m), jnp.exp(c - m)
        den = ea + eb + ec
        y_ref[...] = (ea * o0[...] + eb * o1[...] + ec * o2[...]) / den
        lse_ref[...] = m + jnp.log(den)

    S = os_[0].shape[0]
    return _rows_call(name, body, S, 512, 1, [_tile(t, GROUP_W) for t in (*os_, *lses)],
                      [_out_tile(GROUP_W, F32, GROUP_W)] * 2)


def _attn_bwd(proj, bias, lse, y, dy, g, name, plans=None):
    S = proj.shape[0]
    d = DILATIONS[g]
    NB = S // (d * BLK)
    hps = _heads_per_step(d)

    def body(q_ref, kp_ref, kc_ref, vp_ref, vc_ref, b_ref, l_ref, y_ref, dy_ref,
             dq_ref, dk_ref, dv_ref, db_ref, ck_ref, cv_ref):
        hp, n = pl.program_id(0), pl.program_id(1)

        @pl.when((hp == 0) & (n == 0))
        def _():
            db_ref[...] = jnp.zeros_like(db_ref)

        @pl.when(n == 0)
        def _():
            ck_ref[...] = jnp.zeros_like(ck_ref)
            cv_ref[...] = jnp.zeros_like(cv_ref)

        @pl.when(n < NB)
        def _():
            mask = _band_mask(n)

            def residue(r):
                rows = _sub_rows(r, d)
                q2 = q_ref[rows, :]
                k2 = jnp.concatenate([kp_ref[rows, :], kc_ref[rows, :]], axis=0)
                v2 = jnp.concatenate([vp_ref[rows, :], vc_ref[rows, :]], axis=0)
                l2, y2, dy2 = l_ref[rows, :], y_ref[rows, :], dy_ref[rows, :]
                dqs, dks, dvs = [], [], []
                for hh in range(hps):
                    hs = slice(hh * HEAD_DIM, (hh + 1) * HEAD_DIM)
                    q, kb, vb = _bf(q2[:, hs]), _bf(k2[:, hs]), _bf(v2[:, hs])
                    s = _dot(q, kb, NT) * (HEAD_DIM ** -0.5) + b_ref[hp * hps + hh]
                    s = jnp.where(mask, s, NEG_INF)
                    p = jnp.exp(s - l2[:, hh * HEAD_DIM:hh * HEAD_DIM + 1])
                    dyh = dy2[:, hs]
                    delta = jnp.sum(dyh * y2[:, hs], axis=-1, keepdims=True)
                    dyb = _bf(dyh)
                    ds = p * (_dot(dyb, vb, NT) - delta)
                    db_ref[hp * hps + hh] += ds
                    dsb = _bf(ds * (HEAD_DIM ** -0.5))
                    dqs.append(_dot(dsb, kb, NN))
                    dks.append(_dot(dsb, q, TN))
                    dvs.append(_dot(_bf(p), dyb, TN))
                dkb = jnp.concatenate(dks, axis=1)
                dvb = jnp.concatenate(dvs, axis=1)
                dq_ref[rows, :] = jnp.concatenate(dqs, axis=1)
                dk_ref[rows, :] = ck_ref[rows, :] + dkb[:BLK]
                dv_ref[rows, :] = cv_ref[rows, :] + dvb[:BLK]
                ck_ref[rows, :] = dkb[BLK:]
                cv_ref[rows, :] = dvb[BLK:]

            _for_residues(d, residue)

        @pl.when(n == NB)
        def _():
            dk_ref[...] = ck_ref[...]
            dv_ref[...] = cv_ref[...]

    def qn(n):
        return jnp.minimum(n, NB - 1)

    cur, prev = _attn_specs(d, g, qn)
    cw = hps * HEAD_DIM
    row = pl.BlockSpec((d * BLK, cw), lambda hp, n: (qn(n), hp))
    done = pl.BlockSpec((d * BLK, cw), lambda hp, n: (jnp.maximum(n - 1, 0), hp))
    (dq, dk, dv, db), carried = _call(
        body, plans, name=name, grid=(HEADS // hps, NB + 1),
        in_specs=[cur(0), prev(1), cur(1), prev(2), cur(2),
                  pl.BlockSpec((HEADS, BLK, 2 * BLK), lambda hp, n: (0, 0, 0)), row, row, row],
        out_specs=[row, done, done, pl.BlockSpec((HEADS, BLK, 2 * BLK), lambda hp, n: (0, 0, 0))],
        out_shape=[_sds((S, GROUP_W), F32)] * 3 + [_sds((HEADS, BLK, 2 * BLK), F32)],
        scratch_shapes=[pltpu.VMEM((d * BLK, cw), F32)] * 2,
        args=(proj, proj, proj, proj, proj, bias, lse, y, dy))
    return ([dq, dk, dv], db) if plans is None else ([dq, dk, dv], db, carried)


BAND = BLK * 2 * BLK


def _bucket_onehot():
    buckets = jnp.stack([_t5_bucket(_band_rel() * d) for d in DILATIONS]).reshape(N_GROUPS, 1, BAND)
    return (buckets == jnp.arange(NUM_BUCKETS).reshape(1, NUM_BUCKETS, 1)).astype(F32)


def _relbias_fwd(rel_bias, name):
    table = rel_bias.reshape(NUM_BUCKETS, N_GROUPS, HEADS).transpose(1, 0, 2)

    def body(t_ref, oh_ref, o_ref):
        o_ref[...] = lax.dot_general(t_ref[...], oh_ref[...], (TN, ((), ())), preferred_element_type=F32,
                                     precision=lax.Precision.HIGHEST)

    out = pl.pallas_call(
        body, name=name, grid=(N_GROUPS,),
        in_specs=[pl.BlockSpec((None, NUM_BUCKETS, HEADS), lambda g: (g, 0, 0)),
                  pl.BlockSpec((None, NUM_BUCKETS, BAND), lambda g: (g, 0, 0))],
        out_specs=pl.BlockSpec((None, HEADS, BAND), lambda g: (g, 0, 0)),
        out_shape=_sds((N_GROUPS, HEADS, BAND), F32), compiler_params=_cp(1))(table, _bucket_onehot())
    return out.reshape(N_GROUPS, HEADS, BLK, 2 * BLK)


def _relbias_bwd(dbs, name):
    band = BAND
    onehot = _bucket_onehot()
    dbf = jnp.stack([db.reshape(HEADS, band) for db in dbs])

    def body(oh_ref, db_ref, o_ref):
        o_ref[...] = lax.dot_general(oh_ref[...], db_ref[...], (NT, ((), ())), preferred_element_type=F32,
                                     precision=lax.Precision.HIGHEST)

    out = pl.pallas_call(
        body, name=name, grid=(N_GROUPS,),
        in_specs=[pl.BlockSpec((None, NUM_BUCKETS, band), lambda g: (g, 0, 0)),
                  pl.BlockSpec((None, HEADS, band), lambda g: (g, 0, 0))],
        out_specs=pl.BlockSpec((None, NUM_BUCKETS, HEADS), lambda g: (g, 0, 0)),
        out_shape=_sds((N_GROUPS, NUM_BUCKETS, HEADS), F32), compiler_params=_cp(1))(onehot, dbf)
    return out.transpose(1, 0, 2).reshape(NUM_BUCKETS, N_GROUPS * HEADS)


def _chunk_pos(shape):
    return lax.broadcasted_iota(jnp.int32, shape, 0) % HG_CHUNK


def _chunk_cumsum(v):
    pos = _chunk_pos(v.shape)
    s = 1
    while s < HG_CHUNK:
        v = v + jnp.where(pos >= s, pltpu.roll(v, s, 0), 0.0)
        s *= 2
    return v


def _chunk_rev_cumsum(v):
    pos = _chunk_pos(v.shape)
    n = v.shape[0]
    s = 1
    while s < HG_CHUNK:
        v = v + jnp.where(pos < HG_CHUNK - s, pltpu.roll(v, n - s, 0), 0.0)
        s *= 2
    return v


def _lower_bound(raw):
    a0, a1 = raw[0:1], raw[1:2]
    m = jnp.maximum(a0, a1)
    e0, e1 = jnp.exp(a0 - m), jnp.exp(a1 - m)
    return e0 / (e0 + e1)


def _hg_gates(qr, fr, lb):
    sf = _sigmoid(fr)
    f = lb + (1.0 - lb) * sf
    sq = _sigmoid(qr)
    return qr * sq, sq, f, sf


HG_COL0 = QKV_W // HG_W


def _hgrn_fwd(proj, lb_raw, nw, name):
    S = proj.shape[0]
    ncs = HG_TILE // HG_CHUNK
    tril = jnp.tril(jnp.ones((HG_CHUNK, HG_CHUNK), dtype=bool))

    def body(q_ref, f_ref, i_ref, og_ref, lb_ref, nw_ref, y_ref, o_ref, st_ref, state):
        @pl.when(pl.program_id(0) == 0)
        def _():
            state[...] = jnp.zeros_like(state)

        lb = _lower_bound(lb_ref[...])
        q, _, f, _ = _hg_gates(q_ref[...], f_ref[...], lb)
        k = 1.0 - f
        G = _chunk_cumsum(jnp.log(f))
        row = lax.broadcasted_iota(jnp.int32, (HG_CHUNK, HG_CHUNK), 0)
        col = lax.broadcasted_iota(jnp.int32, (HG_CHUNK, HG_CHUNK), 1)
        for h in range(HG_HEADS):
            hs = slice(h * HG_DK, (h + 1) * HG_DK)
            st = state[h]
            for c in range(ncs):
                cs = slice(c * HG_CHUNK, (c + 1) * HG_CHUNK)
                Gc = G[cs, hs]
                gl = Gc[HG_CHUNK - 1:HG_CHUNK]
                qt = _bf(q[cs, hs] * jnp.exp(Gc))
                kt = _bf(k[cs, hs] * jnp.exp(-Gc))
                kd = _bf(k[cs, hs] * jnp.exp(gl - Gc))
                v = _bf(i_ref[cs, hs])
                A = jnp.where(row >= col, _dot(qt, kt, NT), 0.0)
                o_ref[cs, hs] = _dot(_bf(A), v, NN) + _dot(qt, _bf(st), NT)
                st_ref[c, h] = st
                st = st * jnp.exp(gl) + _dot(v, kd, TN)
            state[h] = st
            oh = o_ref[:, hs]
            og = og_ref[:, hs]
            y_ref[:, hs] = oh * _rinv(oh) * nw_ref[...] * (og * _sigmoid(og))

    def colspec(j):
        return pl.BlockSpec((HG_TILE, HG_W), lambda i: (i, HG_COL0 + j))

    return pl.pallas_call(
        body, name=name, grid=(S // HG_TILE,),
        in_specs=[colspec(0), colspec(1), colspec(2), colspec(3),
                  pl.BlockSpec((2, HG_W), lambda i: (0, 0)), pl.BlockSpec((1, HG_DK), lambda i: (0, 0))],
        out_specs=[pl.BlockSpec((HG_TILE, HG_W), lambda i: (i, 0))] * 2
        + [pl.BlockSpec((ncs, HG_HEADS, HG_DK, HG_DK), lambda i: (i, 0, 0, 0))],
        out_shape=[_sds((S, HG_W), F32)] * 2 + [_sds((S // HG_CHUNK, HG_HEADS, HG_DK, HG_DK), F32)],
        scratch_shapes=[pltpu.VMEM((HG_HEADS, HG_DK, HG_DK), F32)],
        compiler_params=_cp(1))(proj, proj, proj, proj, lb_raw, nw)


def _hgrn_bwd(proj, lb_raw, nw, o, states, dy, name):
    S = proj.shape[0]
    ncs = HG_TILE // HG_CHUNK
    nt = S // HG_TILE

    def body(q_ref, f_ref, i_ref, og_ref, lb_ref, nw_ref, o_ref, st_ref, dy_ref,
             dq_ref, df_ref, di_ref, dog_ref, dlb_ref, dnw_ref, dstate, do_s, dG_s, dgl_s, dk_s, dlb_s):
        step = pl.program_id(0)

        @pl.when(step == 0)
        def _():
            dstate[...] = jnp.zeros_like(dstate)
            dlb_s[...] = jnp.zeros_like(dlb_s)
            dnw_ref[...] = jnp.zeros_like(dnw_ref)

        lb = _lower_bound(lb_ref[...])
        qr = q_ref[...]
        q, sq, f, sf = _hg_gates(qr, f_ref[...], lb)
        k = 1.0 - f
        G = _chunk_cumsum(jnp.log(f))
        nwv = nw_ref[...]
        row = lax.broadcasted_iota(jnp.int32, (HG_CHUNK, HG_CHUNK), 0)
        col = lax.broadcasted_iota(jnp.int32, (HG_CHUNK, HG_CHUNK), 1)
        for h in range(HG_HEADS):
            hs = slice(h * HG_DK, (h + 1) * HG_DK)
            oh = o_ref[:, hs]
            r = _rinv(oh)
            ohat = oh * r
            og = og_ref[:, hs]
            sg = _sigmoid(og)
            dyh = dy_ref[:, hs]
            don = dyh * (og * sg)
            dog_ref[:, hs] = _bf(dyh * (ohat * nwv) * (sg * (1.0 + og * (1.0 - sg))))
            dnw_ref[...] += jnp.sum(don * ohat, axis=0, keepdims=True)
            do_s[:, hs] = _norm_bwd(don, ohat, r, nwv)
            dst = dstate[h]
            for c in reversed(range(ncs)):
                cs = slice(c * HG_CHUNK, (c + 1) * HG_CHUNK)
                Gc = G[cs, hs]
                gl = Gc[HG_CHUNK - 1:HG_CHUNK]
                eG, enG, edG, egl = jnp.exp(Gc), jnp.exp(-Gc), jnp.exp(gl - Gc), jnp.exp(gl)
                qt, kt, kd = q[cs, hs] * eG, k[cs, hs] * enG, k[cs, hs] * edG
                qtb, ktb, kdb = _bf(qt), _bf(kt), _bf(kd)
                v = _bf(i_ref[cs, hs])
                do = _bf(do_s[cs, hs])
                st = st_ref[c, h]
                dstb = _bf(dst)
                A = jnp.where(row >= col, _dot(qtb, ktb, NT), 0.0)
                dA = _bf(jnp.where(row >= col, _dot(do, v, NT), 0.0))
                di_ref[cs, hs] = _bf(_dot(_bf(A), do, TN) + _dot(kdb, dstb, NT))
                dqt = _dot(dA, ktb, NN) + _dot(do, _bf(st), NN)
                dkt = _dot(dA, qtb, TN)
                dkd = _dot(v, dstb, NN)
                dgl = egl * jnp.sum(st * dst, axis=0, keepdims=True) + jnp.sum(dkd * kd, axis=0, keepdims=True)
                dst = dst * egl + _dot(do, qtb, TN)
                dq_ref[cs, hs] = _bf(dqt * eG * (sq[cs, hs] * (1.0 + qr[cs, hs] * (1.0 - sq[cs, hs]))))
                dk_s[cs, hs] = dkt * enG + dkd * edG
                dG_s[cs, hs] = dqt * qt - dkt * kt - dkd * kd
                dgl_s[cs, hs] = jnp.broadcast_to(dgl, (HG_CHUNK, HG_DK))
            dstate[h] = dst
        dg = _chunk_rev_cumsum(dG_s[...]) + dgl_s[...]
        dfv = dg / f - dk_s[...]
        df_ref[...] = _bf(dfv * (1.0 - lb) * sf * (1.0 - sf))
        dlb_s[...] += jnp.sum(dfv * (1.0 - sf), axis=0, keepdims=True)

        @pl.when(step == nt - 1)
        def _():
            t = dlb_s[...] * lb * (1.0 - lb)
            dlb_ref[...] = jnp.concatenate([t, -t], axis=0)

    def colspec(j):
        return pl.BlockSpec((HG_TILE, HG_W), lambda i: (nt - 1 - i, HG_COL0 + j))

    tile = pl.BlockSpec((HG_TILE, HG_W), lambda i: (nt - 1 - i, 0))
    outs = pl.pallas_call(
        body, name=name, grid=(nt,),
        in_specs=[colspec(0), colspec(1), colspec(2), colspec(3),
                  pl.BlockSpec((2, HG_W), lambda i: (0, 0)), pl.BlockSpec((1, HG_DK), lambda i: (0, 0)),
                  tile, pl.BlockSpec((ncs, HG_HEADS, HG_DK, HG_DK), lambda i: (nt - 1 - i, 0, 0, 0)), tile],
        out_specs=[tile] * 4 + [pl.BlockSpec((2, HG_W), lambda i: (0, 0)), pl.BlockSpec((1, HG_DK), lambda i: (0, 0))],
        out_shape=[_sds((S, HG_W), BF16)] * 4 + [_sds((2, HG_W), F32), _sds((1, HG_DK), F32)],
        scratch_shapes=[pltpu.VMEM((HG_HEADS, HG_DK, HG_DK), F32)] + [pltpu.VMEM((HG_TILE, HG_W), F32)] * 4
        + [pltpu.VMEM((1, HG_W), F32)],
        compiler_params=_cp(1))(proj, proj, proj, proj, lb_raw, nw, o, states, dy)
    return outs[:4], outs[4], outs[5]


GATE_COL0 = (QKV_W + 4 * HG_W) // GROUP_W
HALF_D = D_MODEL // 2


def _gate_tiles(proj):
    return [_tile(proj, HALF_D, lambda c: GATE_COL0 + c), _tile(proj, HALF_D, lambda c: GATE_COL0 + 2 + c)]


def _merge_fwd(za, zh, proj, name):
    def body(za_ref, zh_ref, g0_ref, g1_ref, m_ref):
        m_ref[...] = _bf(_sigmoid(g0_ref[...]) * za_ref[...] + _sigmoid(g1_ref[...]) * zh_ref[...])

    col = lambda c: c
    return _rows_call(name, body, za.shape[0], 512, 2,
                      [_tile(za, HALF_D, col), _tile(zh, HALF_D, col), *_gate_tiles(proj)],
                      [_out_tile(D_MODEL, BF16, HALF_D, col)])[0]


def _merge_bwd(dm, za, zh, proj, name):
    def body(dm_ref, za_ref, zh_ref, g0_ref, g1_ref, dza_ref, dzh_ref, dg0_ref, dg1_ref):
        dmv = dm_ref[...]
        s0, s1 = _sigmoid(g0_ref[...]), _sigmoid(g1_ref[...])
        dza_ref[...] = _bf(dmv * s0)
        dzh_ref[...] = _bf(dmv * s1)
        dg0_ref[...] = _bf(dmv * za_ref[...] * s0 * (1.0 - s0))
        dg1_ref[...] = _bf(dmv * zh_ref[...] * s1 * (1.0 - s1))

    col = lambda c: c
    return _rows_call(name, body, za.shape[0], 512, 2,
                      [_tile(dm, HALF_D, col), _tile(za, HALF_D, col), _tile(zh, HALF_D, col), *_gate_tiles(proj)],
                      [_out_tile(D_MODEL, BF16, HALF_D, col)] * 4)


def _mix_out(mo, x, w_post, w_pre, name):
    def body(mo_ref, x_ref, wp_ref, wf_ref, x1_ref, h2_ref):
        z = mo_ref[...]
        x1 = x_ref[...] + z * _rinv(z) * wp_ref[...]
        x1_ref[...] = x1
        h2_ref[...] = _bf(x1 * _rinv(x1) * wf_ref[...])

    return _rows_call(name, body, x.shape[0], 512, 1,
                      [_tile(mo, D_MODEL), _tile(x, D_MODEL), _full(w_post), _full(w_pre)],
                      [_out_tile(D_MODEL, F32, D_MODEL), _out_tile(D_MODEL, BF16, D_MODEL)])


def _loss_head(ffo, x1, tgt, w, name):
    def body(f_ref, x1_ref, t_ref, w_ref, dx_ref, df_ref, dw_ref, loss_ref):
        z = f_ref[...]
        r = _rinv(z)
        zhat = z * r
        wv = w_ref[...]
        e = x1_ref[...] + zhat * wv - t_ref[...]
        dx = e * (1.0 / D_MODEL)
        dx_ref[...] = dx
        df_ref[...] = _bf(_norm_bwd(dx, zhat, r, wv))
        _acc(dw_ref, jnp.sum(dx * zhat, axis=0, keepdims=True))
        part = 0.5 * jnp.sum(jnp.sum(e * e, axis=1, keepdims=True), axis=0, keepdims=True) * (1.0 / D_MODEL)
        _acc(loss_ref, jnp.broadcast_to(part, (1, LANES)))

    return _rows_call(name, body, x1.shape[0], 512, 1,
                      [_tile(ffo, D_MODEL), _tile(x1, D_MODEL), _tile(tgt, D_MODEL), _full(w)],
                      [_out_tile(D_MODEL, F32, D_MODEL), _out_tile(D_MODEL, BF16, D_MODEL),
                       _out_acc(1, D_MODEL, D_MODEL), _out_acc(1, LANES, LANES)])


CONV_CB = D_FF // 2
CONV_TM = 512
HALO = 8
SQRT_HALF = 0.7071067811865476
INV_SQRT_2PI = 0.3989422804014327


def _conv_taps(u_ref, halo_ref, first):
    u = u_ref[...]
    row = lax.broadcasted_iota(jnp.int32, u.shape, 0)
    p1 = jnp.where(first, 0.0, halo_ref[HALO - 1:HALO, :])
    p2 = jnp.where(first, 0.0, halo_ref[HALO - 2:HALO - 1, :])
    u1 = jnp.where(row == 0, p1, pltpu.roll(u, 1, 0))
    u2 = jnp.where(row == 0, p2, jnp.where(row == 1, p1, pltpu.roll(u, 2, 0)))
    return u2, u1, u


def _conv(taps, w_ref, b_ref):
    return b_ref[...] + w_ref[0:1, :] * taps[0] + w_ref[1:2, :] * taps[1] + w_ref[2:3, :] * taps[2]


def _conv_specs(tm):
    nh = tm // HALO
    nc = D_FF // CONV_CB

    def tile(off):
        return pl.BlockSpec((tm, CONV_CB), lambda c, i: (i, off + c))

    def halo(off):
        return pl.BlockSpec((HALO, CONV_CB), lambda c, i: (jnp.maximum(i * nh - 1, 0), off + c))

    def small(rows, off):
        return pl.BlockSpec((rows, CONV_CB), lambda c, i: (0, off + c))

    return nc, tile, halo, small


def _conv_gelu_fwd(u, cw, cb, name):
    S = u.shape[0]
    tm = CONV_TM
    nc, tile, halo, small = _conv_specs(tm)

    def body(ug, hg, uv, hv, wg, wv, bg, bv, a_ref):
        first = pl.program_id(1) == 0
        cg = _conv(_conv_taps(ug, hg, first), wg, bg)
        cv = _conv(_conv_taps(uv, hv, first), wv, bv)
        a_ref[...] = _bf(0.5 * cg * (1.0 + lax.erf(cg * SQRT_HALF)) * cv)

    return pl.pallas_call(
        body, name=name, grid=(nc, S // tm),
        in_specs=[tile(0), halo(0), tile(nc), halo(nc), small(3, 0), small(3, nc), small(1, 0), small(1, nc)],
        out_specs=tile(0), out_shape=_sds((S, D_FF), BF16), compiler_params=_cp(2))(u, u, u, u, cw, cw, cb, cb)


def _conv_gelu_bwd(u, da, cw, cb, name, plans=None):
    S = u.shape[0]
    tm = CONV_TM
    nc, tile, halo, small = _conv_specs(tm)

    def body(ug, hg, uv, hv, wg, wv, bg, bv, da_ref, dcg_ref, dcv_ref, dwg_ref, dwv_ref, dbg_ref, dbv_ref):
        first = pl.program_id(1) == 0
        tg = _conv_taps(ug, hg, first)
        tv = _conv_taps(uv, hv, first)
        cg = _conv(tg, wg, bg)
        cv = _conv(tv, wv, bv)
        phi = 0.5 * (1.0 + lax.erf(cg * SQRT_HALF))
        dav = da_ref[...]
        dcg = dav * cv * (phi + cg * jnp.exp(-0.5 * cg * cg) * INV_SQRT_2PI)
        dcv = dav * (cg * phi)
        dcg_ref[...] = dcg
        dcv_ref[...] = dcv
        for dc, taps, dw_ref, db_ref in ((dcg, tg, dwg_ref, dbg_ref), (dcv, tv, dwv_ref, dbv_ref)):
            _acc(db_ref, jnp.sum(dc, axis=0, keepdims=True))
            for j in range(3):
                _acc(dw_ref.at[j:j + 1, :], jnp.sum(dc * taps[j], axis=0, keepdims=True))

    res, carried = _call(
        body, plans, name=name, grid=(nc, S // tm),
        in_specs=[tile(0), halo(0), tile(nc), halo(nc), small(3, 0), small(3, nc), small(1, 0), small(1, nc), tile(0)],
        out_specs=[tile(0), tile(0), small(3, 0), small(3, 0), small(1, 0), small(1, 0)],
        out_shape=[_sds((S, D_FF), F32)] * 2 + [_sds((3, D_FF), F32)] * 2 + [_sds((1, D_FF), F32)] * 2,
        args=(u, u, u, u, cw, cw, cb, cb, da))
    return res if plans is None else (res, carried)


def _conv_input_bwd(dcg, dcv, cw, name, plans=None):
    S = dcg.shape[0]
    tm = CONV_TM
    nc, tile, _, small = _conv_specs(tm)
    nh = tm // HALO
    nt = S // tm

    def nxt(off):
        return pl.BlockSpec((HALO, CONV_CB), lambda c, i: (jnp.minimum((i + 1) * nh, S // HALO - 1), off + c))

    def body(g_ref, ng_ref, v_ref, nv_ref, wg, wv, dug_ref, duv_ref):
        last = pl.program_id(1) == nt - 1
        for dc_ref, n_ref, w_ref, du_ref in ((g_ref, ng_ref, wg, dug_ref), (v_ref, nv_ref, wv, duv_ref)):
            dc = dc_ref[...]
            row = lax.broadcasted_iota(jnp.int32, dc.shape, 0)
            n1 = jnp.where(last, 0.0, n_ref[0:1, :])
            n2 = jnp.where(last, 0.0, n_ref[1:2, :])
            d1 = jnp.where(row == tm - 1, n1, pltpu.roll(dc, tm - 1, 0))
            d2 = jnp.where(row == tm - 1, n2, jnp.where(row == tm - 2, n1, pltpu.roll(dc, tm - 2, 0)))
            du_ref[...] = _bf(w_ref[2:3, :] * dc + w_ref[1:2, :] * d1 + w_ref[0:1, :] * d2)

    res, carried = _call(
        body, plans, name=name, grid=(nc, nt),
        in_specs=[tile(0), nxt(0), tile(0), nxt(0), small(3, 0), small(3, nc)],
        out_specs=[tile(0), tile(0)], out_shape=[_sds((S, D_FF), BF16)] * 2,
        args=(dcg, dcg, dcv, dcv, cw, cw))
    return res if plans is None else (res, carried)


def _row_tile(n, cap):
    best = n
    for t in range(16, cap + 1, 16):
        if n % t == 0:
            best = t
    return best if best <= cap else n


def _adamw(w, g, m, v, name):
    R, C = w.shape
    tr = _row_tile(R, max(16, (512 * 1024) // (4 * C) // 16 * 16))

    def body(w_ref, g_ref, m_ref, v_ref, d_ref, nm_ref, nv_ref):
        gv = g_ref[...]
        nm = ADAM_B1 * m_ref[...] + (1.0 - ADAM_B1) * gv
        nv = ADAM_B2 * v_ref[...] + (1.0 - ADAM_B2) * (gv * gv)
        m_hat = nm / (1.0 - ADAM_B1 ** ADAM_STEP)
        v_hat = nv / (1.0 - ADAM_B2 ** ADAM_STEP)
        d_ref[...] = -ADAM_LR * (m_hat / (jnp.sqrt(v_hat) + ADAM_EPS) + ADAM_WD * w_ref[...])
        nm_ref[...] = nm
        nv_ref[...] = nv

    spec = pl.BlockSpec((tr, C), lambda i: (i, 0))
    return pl.pallas_call(body, name=name, grid=(R // tr,), in_specs=[spec] * 4, out_specs=[spec] * 3,
                          out_shape=[_sds((R, C), F32)] * 3, compiler_params=_cp(1))(w, g, m, v)


def _pair_sum(gfull, rcv, c_idx, name):
    nb, R, C = gfull.shape
    half = R // 2
    tr = _row_tile(half, 256)
    nt = half // tr

    def body(c_ref, g_ref, r_ref, o_ref):
        o_ref[...] = _bf(g_ref[...] + r_ref[...])

    return pl.pallas_call(
        body, name=name,
        grid_spec=pltpu.PrefetchScalarGridSpec(
            num_scalar_prefetch=1, grid=(nb, nt),
            in_specs=[pl.BlockSpec((None, tr, C), lambda j, i, c_ref: (j, c_ref[0] * nt + i, 0)),
                      pl.BlockSpec((None, tr, C), lambda j, i, c_ref: (j, i, 0))],
            out_specs=pl.BlockSpec((None, tr, C), lambda j, i, c_ref: (j, i, 0))),
        out_shape=_sds((nb, half, C), BF16), compiler_params=_cp(2))(c_idx, gfull, rcv)


def _chip_sum(arrived, own, place, name):
    nb, H, C = arrived.shape
    tr = _row_tile(H, 256)
    nt = H // tr

    def body(pl_ref, *refs):
        o_ref = refs[nb + 1]
        me = pl_ref[0]
        acc = None
        for k in range(nb):
            term = jnp.where(me == k, refs[nb][...], refs[k][...]).astype(F32)
            acc = term if acc is None else acc + term
        o_ref[...] = acc

    def other(k):
        return pl.BlockSpec((None, tr, C), lambda i, p: (jnp.where(p[0] == k, (k + 1) % nb, k), i, 0))

    return pl.pallas_call(
        body, name=name,
        grid_spec=pltpu.PrefetchScalarGridSpec(
            num_scalar_prefetch=1, grid=(nt,),
            in_specs=[other(k) for k in range(nb)] + [pl.BlockSpec((None, tr, C), lambda i, p: (p[0], i, 0))],
            out_specs=pl.BlockSpec((tr, C), lambda i, p: (p[1] * nt + i, 0))),
        out_shape=_sds((2 * H, C), F32), compiler_params=_cp(1))(place, *([arrived] * nb), own)


def _cast_into_slot(shard, place, name):
    R, C = shard.shape
    tr = _row_tile(R, 256)

    def body(pl_ref, s_ref, o_ref):
        o_ref[...] = _bf(s_ref[...])

    return pl.pallas_call(
        body, name=name,
        grid_spec=pltpu.PrefetchScalarGridSpec(
            num_scalar_prefetch=1, grid=(R // tr,),
            in_specs=[pl.BlockSpec((tr, C), lambda i, p: (i, 0))],
            out_specs=pl.BlockSpec((None, tr, C), lambda i, p: (p[0], i, 0))),
        out_shape=_sds((N_CHIPS, R, C), BF16), compiler_params=_cp(1))(place, shard)


def _place():
    x, y, c = lax.axis_index("x"), lax.axis_index("y"), lax.axis_index("c")
    chips = [(1 - x, y), (x, 1 - y), (1 - x, 1 - y)]
    return x, y, c, chips


def _chip_id(px, py):
    return 2 * px + py


def _remote(src, dst, send_sems, recv_sems, k, to):
    return pltpu.make_async_remote_copy(src_ref=src, dst_ref=dst, send_sem=send_sems.at[k], recv_sem=recv_sems.at[k],
                                        device_id=to, device_id_type=MESH)


def _gather_weights(slots, wholes, name):
    ns, nw = len(slots), len(wholes)
    n = ns + nw

    def body(*refs):
        ins = refs[ns:n]
        outs = refs[n:2 * n]
        send_sems, recv_sems, local_sems = refs[2 * n:]
        x, y, c, chips = _place()
        me = _chip_id(x, y)
        sib = (x, y, 1 - c)
        local = [pltpu.make_async_copy(ins[b], outs[ns + b].at[me], local_sems.at[b]) for b in range(nw)]
        for cp in local:
            cp.start()
        sent = []
        for a in range(n):
            R = outs[a].shape[1]
            rows = pl.ds(c * (R // 2), R // 2) if a < ns else pl.ds(0, R)
            src = outs[a].at[me, rows] if a < ns else ins[a - ns]
            for j, chip in enumerate(chips):
                cp = _remote(src, outs[a].at[me, rows], send_sems, recv_sems, 6 * a + j, (*chip, c))
                cp.start()
                sent.append(cp)
        for a in range(n):
            R = outs[a].shape[1]
            rows = pl.ds(c * (R // 2), R // 2) if a < ns else pl.ds(0, R)
            for j, chip in enumerate(chips):
                landed = outs[a].at[_chip_id(*chip), rows]
                _remote(landed, landed, send_sems, recv_sems, 6 * a + j, (*chip, c)).wait_recv()
                if a < ns:
                    cp = _remote(landed, landed, send_sems, recv_sems, 6 * a + 3 + j, sib)
                    cp.start()
                    sent.append(cp)
        for a in range(ns):
            R = outs[a].shape[1]
            other = pl.ds((1 - c) * (R // 2), R // 2)
            for j, chip in enumerate(chips):
                passed = outs[a].at[_chip_id(*chip), other]
                _remote(passed, passed, send_sems, recv_sems, 6 * a + 3 + j, sib).wait_recv()
        for cp in sent:
            cp.wait_send()
        for cp in local:
            cp.wait()

    return pl.pallas_call(
        body, name=name, in_specs=[ANY] * n, out_specs=[ANY] * n,
        out_shape=[_sds(s.shape, s.dtype) for s in slots] + [_sds((N_CHIPS, *s.shape), s.dtype) for s in wholes],
        input_output_aliases={a: a for a in range(ns)},
        scratch_shapes=[pltpu.SemaphoreType.DMA((6 * n,)), pltpu.SemaphoreType.DMA((6 * n,)),
                        pltpu.SemaphoreType.DMA((max(nw, 1),))])(*slots, *wholes)


def _gather_ici_plan(slots, wholes):
    ns, nw = len(slots), len(wholes)

    def copies(ins, ios, outs, send_sems, recv_sems, local_sems):
        x, y, c, chips = _place()
        me = _chip_id(x, y)
        sends, recvs = [], []
        for a in range(ns + nw):
            dst = ios[a] if a < ns else outs[a - ns]
            R = dst.shape[1]
            rows = pl.ds(c * (R // 2), R // 2) if a < ns else pl.ds(0, R)
            src = dst.at[me, rows] if a < ns else ins[a - ns]
            for j, chip in enumerate(chips):
                sends.append(_remote(src, dst.at[me, rows], send_sems, recv_sems, 3 * a + j, (*chip, c)))
                landed = dst.at[_chip_id(*chip), rows]
                recvs.append(_remote(landed, landed, send_sems, recv_sems, 3 * a + j, (*chip, c)))
        local = [pltpu.make_async_copy(ins[b], outs[b].at[me], local_sems.at[b]) for b in range(nw)]
        return sends, recvs, local

    return _Plan(copies, 3 * (ns + nw), ins=wholes, inouts=slots,
                 outs=[_sds((N_CHIPS, *s.shape), s.dtype) for s in wholes])


def _gather_pass_plan(slots):
    def copies(ins, ios, outs, send_sems, recv_sems, local_sems):
        x, y, c, chips = _place()
        sib = (x, y, 1 - c)
        sends, recvs = [], []
        for a, buf in enumerate(ios):
            half = buf.shape[1] // 2
            for j, chip in enumerate(chips):
                mine = buf.at[_chip_id(*chip), pl.ds(c * half, half)]
                other = buf.at[_chip_id(*chip), pl.ds((1 - c) * half, half)]
                sends.append(_remote(mine, mine, send_sems, recv_sems, 3 * a + j, sib))
                recvs.append(_remote(other, other, send_sems, recv_sems, 3 * a + j, sib))
        return sends, recvs, []

    return _Plan(copies, 3 * len(slots), inouts=slots)


def _pair_plan(grads):
    def copies(ins, ios, outs, send_sems, recv_sems, local_sems):
        x, y, c, _ = _place()
        sib = (x, y, 1 - c)
        sends, recvs = [], []
        for a, g in enumerate(ins):
            half = g.shape[1] // 2
            sends.append(_remote(g.at[:, pl.ds((1 - c) * half, half), :], outs[a], send_sems, recv_sems, a, sib))
            recvs.append(_remote(outs[a], outs[a], send_sems, recv_sems, a, sib))
        return sends, recvs, []

    return _Plan(copies, len(grads), ins=grads,
                 outs=[_sds((g.shape[0], g.shape[1] // 2, g.shape[2]), g.dtype) for g in grads])


def _chip_plan(parts):
    def copies(ins, ios, outs, send_sems, recv_sems, local_sems):
        x, y, c, chips = _place()
        me = _chip_id(x, y)
        sends, recvs = [], []
        for a, part in enumerate(ins):
            for j, chip in enumerate(chips):
                sends.append(_remote(part.at[_chip_id(*chip)], outs[a].at[me], send_sems, recv_sems, 3 * a + j, (*chip, c)))
                landed = outs[a].at[_chip_id(*chip)]
                recvs.append(_remote(landed, landed, send_sems, recv_sems, 3 * a + j, (*chip, c)))
        return sends, recvs, []

    return _Plan(copies, 3 * len(parts), ins=parts, outs=[_sds(p.shape, p.dtype) for p in parts])


def _pair_concat(fulls, name):
    n = len(fulls)

    def body(*refs):
        outs = refs[n:2 * n]
        send_sems, recv_sems = refs[2 * n:]
        x, y, c, _ = _place()
        cps = []
        for a in range(n):
            H = outs[a].shape[0] // 2
            mine = outs[a].at[pl.ds(c * H, H)]
            cp = _remote(mine, mine, send_sems, recv_sems, a, (x, y, 1 - c))
            cp.start()
            cps.append(cp)
        for a, cp in enumerate(cps):
            H = outs[a].shape[0] // 2
            other = outs[a].at[pl.ds((1 - c) * H, H)]
            _remote(other, other, send_sems, recv_sems, a, (x, y, 1 - c)).wait_recv()
            cp.wait_send()

    return pl.pallas_call(
        body, name=name, in_specs=[ANY] * n, out_specs=[ANY] * n,
        out_shape=[_sds(f.shape, f.dtype) for f in fulls], input_output_aliases={a: a for a in range(n)},
        scratch_shapes=[pltpu.SemaphoreType.DMA((n,)), pltpu.SemaphoreType.DMA((n,))])(*fulls)


def _all_sum(pack, name):
    R, C = pack.shape

    def body(p_ref, o_ref, buf, send_sems, recv_sems):
        x, y, c, _ = _place()
        me = 4 * x + 2 * y + c
        buf[me] = p_ref[...]
        cps = []
        for k in range(1, N_DEV):
            to = (x ^ (k >> 2), y ^ ((k >> 1) & 1), c ^ (k & 1))
            cp = _remote(p_ref, buf.at[me], send_sems, recv_sems, k - 1, to)
            cp.start()
            cps.append(cp)
        for k in range(1, N_DEV):
            frm = (x ^ (k >> 2), y ^ ((k >> 1) & 1), c ^ (k & 1))
            slot = buf.at[4 * frm[0] + 2 * frm[1] + frm[2]]
            _remote(slot, slot, send_sems, recv_sems, k - 1, frm).wait_recv()
        acc = buf[0]
        for k in range(1, N_DEV):
            acc = acc + buf[k]
        o_ref[...] = acc
        for cp in cps:
            cp.wait_send()

    vm = pl.BlockSpec(memory_space=pltpu.VMEM)
    return pl.pallas_call(
        body, name=name, in_specs=[vm], out_specs=vm, out_shape=_sds((R, C), F32),
        scratch_shapes=[pltpu.VMEM((N_DEV, R, C), F32), pltpu.SemaphoreType.DMA((N_DEV - 1,)),
                        pltpu.SemaphoreType.DMA((N_DEV - 1,))])(pack)


def _local_step(xs, tgt, p, ex):
    h1 = _norm_fwd(xs, p["pre_mix_norm"], "pre_mix_norm")
    proj, got = _mm_nn_blk(h1, ex.weight("w_in"), "proj_in", plans=ex.carry("proj_in"))
    ex.done("proj_in", got)
    biases = _relbias_fwd(p["rel_bias"], "rel_bias_fwd")
    fw0, got = _attn_fwd(proj, biases[0], 0, "attn_fwd0", plans=ex.carry("attn_fwd0"))
    ex.done("attn_fwd0", got)
    fw = [fw0] + [_attn_fwd(proj, biases[g], g, f"attn_fwd{g}") for g in range(1, N_GROUPS)]
    y, lse = _attn_merge([t[0] for t in fw], [t[1] for t in fw], "attn_merge")
    yh, o_h, states = _hgrn_fwd(proj, p["hgrn_lb_raw"], p["hgrn_norm"], "hgrn_fwd")
    W_a, W_h, W_out = ex.weight("w_branch_attn"), ex.weight("w_branch_hgrn"), ex.weight("w_out")
    W_up, W_down, conv_w = ex.weight("w_up"), ex.weight("w_down"), ex.weight("conv_w")
    za = _mm_nn_blk(y, W_a, "branch_attn")
    zh = _mm_nn_blk(yh, W_h, "branch_hgrn")
    merged = _merge_fwd(za, zh, proj, "merge_fwd")
    mo = _mm_nn(merged, W_out, "mix_out")
    x1, h2 = _mix_out(mo, xs, p["post_mix_norm"], p["pre_ffn_norm"], "mix_residual")
    u = _mm_nn_blk(h2, W_up, "ffn_up")
    a = _conv_gelu_fwd(u, conv_w, p["conv_b"], "conv_gelu_fwd")
    ffo = _mm_nn(a, W_down, "ffn_down")
    dx2, dff, g_post_ffn, loss = _loss_head(ffo, x1, tgt, p["post_ffn_norm"], "loss_head")

    da = _mm_nt(dff, W_down, "d_ffn_act")
    ex.grad("w_down", _mm_tn(a, dff, "g_w_down").reshape(N_CHIPS, D_FF // N_CHIPS, D_MODEL))
    (dcg, dcv, gwg, gwv, gbg, gbv), got = _conv_gelu_bwd(u, da, conv_w, p["conv_b"], "conv_gelu_bwd",
                                                          plans=ex.carry("conv_gelu_bwd"))
    ex.done("conv_gelu_bwd", got)
    g_conv_w = jnp.concatenate([gwg, gwv], axis=1)
    g_conv_b = jnp.concatenate([gbg, gbv], axis=1)
    du_parts, got = _conv_input_bwd(dcg, dcv, conv_w, "conv_input_bwd", plans=ex.carry("conv_input_bwd"))
    ex.done("conv_input_bwd", got)
    du = jnp.concatenate(du_parts, axis=1)
    dh2 = _mm_nt_blk(du, W_up, "d_ffn_in")
    ex.grad("w_up", _mm_tn_blk(h2, du, N_CHIPS, "g_w_up"))
    (dx1, g_pre_ffn), got = _prenorm_bwd(dh2, x1, p["pre_ffn_norm"], dx2, "pre_ffn_norm_bwd",
                                         plans=ex.carry("pre_ffn_norm_bwd"))
    ex.done("pre_ffn_norm_bwd", got)
    dmo, g_post_mix = _postnorm_bwd(dx1, mo, p["post_mix_norm"], "post_mix_norm_bwd")
    dmerged = _mm_nt(dmo, W_out, "d_merged")
    ex.grad("w_out", _mm_tn(merged, dmo, "g_w_out").reshape(N_CHIPS, D_MODEL // N_CHIPS, D_MODEL))
    dza, dzh, dg0, dg1 = _merge_bwd(dmerged, za, zh, proj, "merge_bwd")
    dy = _mm_nt_blk(dza, W_a, "d_attn_out")
    ex.grad("w_branch_attn", _mm_tn_blk(y, dza, N_CHIPS, "g_w_branch_attn"))
    dyh = _mm_nt_blk(dzh, W_h, "d_hgrn_out")
    ex.grad("w_branch_hgrn", _mm_tn_blk(yh, dzh, N_CHIPS, "g_w_branch_hgrn"))
    dqkv, dbs = [], []
    for g in range(N_GROUPS):
        parts, db, got = _attn_bwd(proj, biases[g], lse, y, dy, g, f"attn_bwd{g}", plans=ex.carry(f"attn_bwd{g}"))
        ex.done(f"attn_bwd{g}", got)
        dqkv += parts
        dbs.append(db)
    g_rel_bias = _relbias_bwd(dbs, "rel_bias_bwd")
    dhg, g_lb_raw, g_hgrn_norm = _hgrn_bwd(proj, p["hgrn_lb_raw"], p["hgrn_norm"], o_h, states, dyh, "hgrn_bwd")
    dproj = jnp.concatenate([*[_bf(t) for t in dqkv], *dhg, dg0, dg1], axis=1)
    ex.grad("w_in", _mm_tn_blk(h1, dproj, N_CHIPS, "g_w_in"))
    dh1, got = _mm_nt_blk(dproj, ex.weight("w_in"), "d_proj_in", plans=ex.carry("d_proj_in"))
    ex.done("d_proj_in", got)
    (grad_x, g_pre_mix), got = _prenorm_bwd(dh1, xs, p["pre_mix_norm"], dx1, "pre_mix_norm_bwd",
                                            plans=ex.carry("pre_mix_norm_bwd"))
    ex.done("pre_mix_norm_bwd", got)
    small = dict(pre_mix_norm=g_pre_mix, rel_bias=g_rel_bias, hgrn_lb_raw=g_lb_raw, hgrn_norm=g_hgrn_norm,
                 post_mix_norm=g_post_mix, pre_ffn_norm=g_pre_ffn, conv_w=g_conv_w, conv_b=g_conv_b,
                 post_ffn_norm=g_post_ffn)
    return loss, grad_x, small


SMALL = ("pre_mix_norm", "rel_bias", "hgrn_lb_raw", "hgrn_norm", "post_mix_norm", "pre_ffn_norm", "conv_w", "conv_b",
         "post_ffn_norm")
BIG = ("w_in", "w_up", "w_down", "w_out", "w_branch_attn", "w_branch_hgrn")
LATE = BIG[1:]
WEIGHTS = ("pre_mix_norm", "w_in", "rel_bias", "hgrn_lb_raw", "hgrn_norm", "w_branch_attn", "w_branch_hgrn", "w_out",
           "post_mix_norm", "pre_ffn_norm", "w_up", "conv_w", "conv_b", "w_down", "post_ffn_norm")

SCHEDULE = {
    "proj_in": [("gather_ici", LATE)],
    "attn_fwd0": [("gather_pass", LATE)],
    "conv_gelu_bwd": [("pair", ("w_down",))],
    "conv_input_bwd": [("chip", ("w_down",))],
    "pre_ffn_norm_bwd": [("pair", ("w_up",))],
    "attn_bwd0": [("chip", ("w_up",)), ("pair", ("w_out", "w_branch_attn", "w_branch_hgrn"))],
    "attn_bwd1": [("chip", ("w_out", "w_branch_attn", "w_branch_hgrn"))],
    "d_proj_in": [("pair", ("w_in",))],
    "pre_mix_norm_bwd": [("chip", ("w_in",))],
}


class _Exchange:
    def __init__(self, place, slots, conv_w_shard):
        self.place, self.slots, self.conv_w_shard = place, dict(slots), conv_w_shard
        self.conv_w = None
        self.g, self.from_sibling, self.pair_sums, self.arrived = {}, {}, {}, {}
        self.pending = []

    def weight(self, name):
        if name == "conv_w":
            return self.conv_w
        w = self.slots[name]
        return w.reshape(-1, D_MODEL) if name in ("w_out", "w_down") else w

    def grad(self, name, g):
        self.g[name] = g

    def carry(self, point):
        plans = []
        self.pending = SCHEDULE.get(point, [])
        for kind, names in self.pending:
            if kind == "gather_ici":
                plans.append(_gather_ici_plan([self.slots[n] for n in names], [self.conv_w_shard]))
            elif kind == "gather_pass":
                plans.append(_gather_pass_plan([self.slots[n] for n in names]))
            elif kind == "pair":
                plans.append(_pair_plan([self.g[n] for n in names]))
            else:
                for n in names:
                    self.pair_sums[n] = _pair_sum(self.g[n], self.from_sibling[n], self.place[1:2], f"pair_sum_{n}")
                plans.append(_chip_plan([self.pair_sums[n] for n in names]))
        return plans

    def done(self, point, carried):
        for (kind, names), got in zip(self.pending, carried):
            if kind in ("gather_ici", "gather_pass"):
                self.slots.update(zip(names, got))
                if kind == "gather_ici":
                    self.conv_w = got[len(names)].transpose(1, 0, 2).reshape(3, 2 * D_FF)
            elif kind == "pair":
                self.from_sibling.update(zip(names, got))
            else:
                self.arrived.update(zip(names, got))

    def reduced(self):
        halves = [_chip_sum(self.arrived[n], self.pair_sums[n], self.place, f"chip_sum_{n}") for n in BIG]
        return dict(zip(BIG, _pair_concat(halves, "pair_concat")))


def kernel(x, pre_mix_norm, w_in, rel_bias, hgrn_lb_raw, hgrn_norm, w_branch_attn, w_branch_hgrn, w_out, post_mix_norm, pre_ffn_norm, w_up, conv_w, conv_b, w_down, post_ffn_norm, loss_target, m_pre_mix_norm, m_w_in, m_rel_bias, m_hgrn_lb_raw, m_hgrn_norm, m_w_branch_attn, m_w_branch_hgrn, m_w_out, m_post_mix_norm, m_pre_ffn_norm, m_w_up, m_conv_w, m_conv_b, m_w_down, m_post_ffn_norm, v_pre_mix_norm, v_w_in, v_rel_bias, v_hgrn_lb_raw, v_hgrn_norm, v_w_branch_attn, v_w_branch_hgrn, v_w_out, v_post_mix_norm, v_pre_ffn_norm, v_w_up, v_conv_w, v_conv_b, v_w_down, v_post_ffn_norm):
    w = dict(pre_mix_norm=pre_mix_norm, w_in=w_in, rel_bias=rel_bias, hgrn_lb_raw=hgrn_lb_raw, hgrn_norm=hgrn_norm,
             w_branch_attn=w_branch_attn, w_branch_hgrn=w_branch_hgrn, w_out=w_out, post_mix_norm=post_mix_norm,
             pre_ffn_norm=pre_ffn_norm, w_up=w_up, conv_w=conv_w, conv_b=conv_b, w_down=w_down,
             post_ffn_norm=post_ffn_norm)
    m = dict(pre_mix_norm=m_pre_mix_norm, w_in=m_w_in, rel_bias=m_rel_bias, hgrn_lb_raw=m_hgrn_lb_raw,
             hgrn_norm=m_hgrn_norm, w_branch_attn=m_w_branch_attn, w_branch_hgrn=m_w_branch_hgrn, w_out=m_w_out,
             post_mix_norm=m_post_mix_norm, pre_ffn_norm=m_pre_ffn_norm, w_up=m_w_up, conv_w=m_conv_w,
             conv_b=m_conv_b, w_down=m_w_down, post_ffn_norm=m_post_ffn_norm)
    v = dict(pre_mix_norm=v_pre_mix_norm, w_in=v_w_in, rel_bias=v_rel_bias, hgrn_lb_raw=v_hgrn_lb_raw,
             hgrn_norm=v_hgrn_norm, w_branch_attn=v_w_branch_attn, w_branch_hgrn=v_w_branch_hgrn, w_out=v_w_out,
             post_mix_norm=v_post_mix_norm, pre_ffn_norm=v_pre_ffn_norm, w_up=v_w_up, conv_w=v_conv_w,
             conv_b=v_conv_b, w_down=v_w_down, post_ffn_norm=v_post_ffn_norm)
    shard2d = {n: (w[n][0] if w[n].ndim == 3 else w[n]) for n in WEIGHTS}
    chip = 2 * lax.axis_index("x") + lax.axis_index("y")
    core = lax.axis_index("c")

    place = jnp.stack([chip, core]).astype(jnp.int32)
    slots = {n: _cast_into_slot(shard2d[n], place, f"cast_{n}") for n in BIG}
    slots["w_in"] = _gather_weights([slots["w_in"]], [], "gather_w_in")[0]
    ex = _Exchange(place, slots, shard2d["conv_w"])
    loss, grad_x, small = _local_step(x[0], loss_target[0], {n: w[n] for n in SMALL if n != "conv_w"}, ex)

    flat = [small[n].reshape(-1) for n in SMALL] + [loss.reshape(-1)]
    sizes = [t.shape[0] for t in flat]
    summed = _all_sum(jnp.concatenate(flat).reshape(-1, LANES), "sum_small").reshape(-1)
    offs = [sum(sizes[:i]) for i in range(len(sizes))]
    grads = {}
    for n, o, sz in zip(SMALL, offs, sizes):
        grads[n] = summed[o:o + sz].reshape(small[n].shape)
    loss_total = summed[offs[-1]]
    cw = 2 * D_FF // N_CHIPS
    grads["conv_w"] = lax.dynamic_slice(grads["conv_w"], (0, chip * cw), (3, cw))

    grads.update(ex.reduced())

    out_g, out_d, out_m, out_v = [], [], [], []
    for n in WEIGHTS:
        d2, m2, v2 = _adamw(shard2d[n], grads[n], m[n].reshape(shard2d[n].shape), v[n].reshape(shard2d[n].shape),
                            f"adamw_{n}")
        shape = w[n].shape
        out_g.append(grads[n].reshape(shape))
        out_d.append(d2.reshape(shape))
        out_m.append(m2.reshape(shape))
        out_v.append(v2.reshape(shape))
    return (loss_total, grad_x[None], *out_g, *out_d, *out_m, *out_v)
```

```python
import functools
import math

import jax
import jax.numpy as jnp
from jax import lax
from jax.experimental import pallas as pl
from jax.experimental.pallas import tpu as pltpu

F32 = jnp.float32
BF16 = jnp.bfloat16
MESH = pl.DeviceIdType.MESH

D_MODEL = 1024
N_GROUPS = 3
DILATIONS = (1, 4, 16)
HEADS = 8
HEAD_DIM = 64
GROUP_W = HEADS * HEAD_DIM
QKV_W = N_GROUPS * 3 * GROUP_W
BLK = 128
NEG_INF = -1e30
NUM_BUCKETS = 32
MAX_EXACT = 16
MAX_DISTANCE = 2048
HG_HEADS = 4
HG_DK = 128
HG_W = HG_HEADS * HG_DK
HG_CHUNK = 32
HG_TILE = 256
IN_W = QKV_W + 4 * HG_W + 2 * D_MODEL
D_FF = 2816
EPS = 1e-6
N_CHIPS = 4
N_DEV = 8
LANES = 128

ADAM_LR, ADAM_B1, ADAM_B2, ADAM_EPS, ADAM_WD, ADAM_STEP = 0.001, 0.9, 0.999, 1e-08, 0.01, 10

VMEM_LIMIT = 56 * 1024 * 1024


def _cp(n_axes):
    return pltpu.CompilerParams(dimension_semantics=("arbitrary",) * n_axes, vmem_limit_bytes=VMEM_LIMIT)


def _sds(shape, dtype):
    return jax.ShapeDtypeStruct(tuple(shape), dtype)


def _sigmoid(v):
    return 1.0 / (1.0 + jnp.exp(-v))


def _bf(v):
    return v.astype(BF16)


def _dot(a, b, dims):
    return lax.dot_general(a, b, (dims, ((), ())), preferred_element_type=F32)


NN = ((1,), (0,))
NT = ((1,), (1,))
TN = ((0,), (0,))

ANY = pl.BlockSpec(memory_space=pl.ANY)


class _Plan:
    def __init__(self, copies, n_sems, ins=(), inouts=(), outs=()):
        self.copies, self.n_sems = copies, n_sems
        self.ins, self.inouts, self.outs = list(ins), list(inouts), list(outs)


def _call(body, plans=None, *, name, grid, in_specs, out_specs, out_shape, args, scratch_shapes=()):
    plans = list(plans or ())
    in_specs, out_specs, out_shape = list(in_specs), list(out_specs), list(out_shape)
    scratch_shapes = list(scratch_shapes)
    n_in, n_out, n_scr = len(in_specs), len(out_specs), len(scratch_shapes)
    x_in, x_out, aliases, spans = [], [], {}, []
    for p in plans:
        i0, o0 = len(x_in), len(x_out)
        x_in += p.ins
        for a in p.inouts:
            aliases[n_in + len(x_in)] = n_out + len(x_out)
            x_in.append(a)
            x_out.append(_sds(a.shape, a.dtype))
        x_out += p.outs
        spans.append((i0, len(p.ins), o0, len(p.inouts), len(p.outs)))
    sems = [pltpu.SemaphoreType.DMA((p.n_sems,)) for p in plans for _ in range(3)]

    def wrapped(*refs):
        xi = refs[n_in:n_in + len(x_in)]
        base = n_in + len(x_in)
        xo = refs[base + n_out:base + n_out + len(x_out)]
        sbase = base + n_out + len(x_out)
        xs = refs[sbase + n_scr:]
        ids = [pl.program_id(k) for k in range(len(grid))]
        first = functools.reduce(jnp.logical_and, [i == 0 for i in ids])
        last = functools.reduce(jnp.logical_and, [i == g - 1 for i, g in zip(ids, grid)])

        def descriptors(k):
            i0, ni, o0, nio, no = spans[k]
            return plans[k].copies(xi[i0:i0 + ni], xo[o0:o0 + nio], xo[o0 + nio:o0 + nio + no], *xs[3 * k:3 * k + 3])

        @pl.when(first)
        def _():
            for k in range(len(plans)):
                sends, _, local = descriptors(k)
                for cp in (*sends, *local):
                    cp.start()

        body(*refs[:n_in], *refs[base:base + n_out], *refs[sbase:sbase + n_scr])

        @pl.when(last)
        def _():
            for k in range(len(plans)):
                sends, recvs, local = descriptors(k)
                for cp in recvs:
                    cp.wait_recv()
                for cp in sends:
                    cp.wait_send()
                for cp in local:
                    cp.wait()

    res = pl.pallas_call(
        wrapped if plans else body, name=name, grid=grid, in_specs=in_specs + [ANY] * len(x_in),
        out_specs=out_specs + [ANY] * len(x_out), out_shape=out_shape + x_out, input_output_aliases=aliases,
        scratch_shapes=scratch_shapes + sems, compiler_params=_cp(len(grid)))(*args, *x_in)
    res = list(res)
    carried = [res[n_out + o0:n_out + o0 + nio + no] for (_, _, o0, nio, no) in spans]
    return res[:n_out], carried


def _mm_nn_blk(a, wg, name, tm=512, plans=None):
    M, K = a.shape
    nb, _, Nb = wg.shape

    def body(a_ref, w_ref, o_ref):
        o_ref[...] = _dot(_bf(a_ref[...]), w_ref[...], NN)

    (out,), carried = _call(
        body, plans, name=name, grid=(nb, M // tm),
        in_specs=[pl.BlockSpec((tm, K), lambda j, i: (i, 0)), pl.BlockSpec((None, K, Nb), lambda j, i: (j, 0, 0))],
        out_specs=[pl.BlockSpec((tm, Nb), lambda j, i: (i, j))],
        out_shape=[_sds((M, nb * Nb), F32)], args=(a, wg))
    return out if plans is None else (out, carried)


def _mm_nt_blk(dy, wg, name, tm=1024, plans=None):
    M = dy.shape[0]
    nb, K, Nb = wg.shape

    def body(dy_ref, w_ref, o_ref):
        j = pl.program_id(1)
        r = _dot(_bf(dy_ref[...]), w_ref[...], NT)

        @pl.when(j == 0)
        def _():
            o_ref[...] = r

        @pl.when(j > 0)
        def _():
            o_ref[...] += r

    (out,), carried = _call(
        body, plans, name=name, grid=(M // tm, nb),
        in_specs=[pl.BlockSpec((tm, Nb), lambda i, j: (i, j)), pl.BlockSpec((None, K, Nb), lambda i, j: (j, 0, 0))],
        out_specs=[pl.BlockSpec((tm, K), lambda i, j: (i, 0))],
        out_shape=[_sds((M, K), F32)], args=(dy, wg))
    return out if plans is None else (out, carried)


def _mm_tn_blk(x, dy, nb, name, tk=512, x_cols=None, plans=None):
    T, Mx = x.shape
    xk, Mx = (0, Mx) if x_cols is None else x_cols
    Nb = dy.shape[1] // nb

    def body(x_ref, dy_ref, o_ref):
        t = pl.program_id(1)
        r = _dot(_bf(x_ref[...]), _bf(dy_ref[...]), TN)

        @pl.when(t == 0)
        def _():
            o_ref[...] = r

        @pl.when(t > 0)
        def _():
            o_ref[...] += r

    (out,), carried = _call(
        body, plans, name=name, grid=(nb, T // tk),
        in_specs=[pl.BlockSpec((tk, Mx), lambda j, t: (t, xk)), pl.BlockSpec((tk, Nb), lambda j, t: (t, j))],
        out_specs=[pl.BlockSpec((None, Mx, Nb), lambda j, t: (j, 0, 0))],
        out_shape=[_sds((nb, Mx, Nb), F32)], args=(x, dy))
    return out if plans is None else (out, carried)


def _mm_nn(a, w, name, tm=512):
    M, K = a.shape
    N = w.shape[1]

    def body(a_ref, w_ref, o_ref):
        o_ref[...] = _dot(_bf(a_ref[...]), w_ref[...], NN)

    return pl.pallas_call(
        body, name=name, grid=(M // tm,),
        in_specs=[pl.BlockSpec((tm, K), lambda i: (i, 0)), pl.BlockSpec((K, N), lambda i: (0, 0))],
        out_specs=pl.BlockSpec((tm, N), lambda i: (i, 0)),
        out_shape=_sds((M, N), F32), compiler_params=_cp(1))(a, w)


def _mm_nt(dy, w, name, tm=512):
    M, N = dy.shape
    K = w.shape[0]

    def body(dy_ref, w_ref, o_ref):
        o_ref[...] = _dot(_bf(dy_ref[...]), w_ref[...], NT)

    return pl.pallas_call(
        body, name=name, grid=(M // tm,),
        in_specs=[pl.BlockSpec((tm, N), lambda i: (i, 0)), pl.BlockSpec((K, N), lambda i: (0, 0))],
        out_specs=pl.BlockSpec((tm, K), lambda i: (i, 0)),
        out_shape=_sds((M, K), F32), compiler_params=_cp(1))(dy, w)


def _mm_tn(x, dy, name, tk=512):
    T, Mx = x.shape
    N = dy.shape[1]

    def body(x_ref, dy_ref, o_ref):
        t = pl.program_id(0)
        r = _dot(_bf(x_ref[...]), _bf(dy_ref[...]), TN)

        @pl.when(t == 0)
        def _():
            o_ref[...] = r

        @pl.when(t > 0)
        def _():
            o_ref[...] += r

    return pl.pallas_call(
        body, name=name, grid=(T // tk,),
        in_specs=[pl.BlockSpec((tk, Mx), lambda t: (t, 0)), pl.BlockSpec((tk, N), lambda t: (t, 0))],
        out_specs=pl.BlockSpec((Mx, N), lambda t: (0, 0)),
        out_shape=_sds((Mx, N), F32), compiler_params=_cp(1))(x, dy)


def _tile(arr, bw, col=lambda c: 0):
    return ("tile", arr, bw, col)


def _full(arr):
    return ("full", arr)


def _out_tile(width, dtype, bw, col=lambda c: 0):
    return ("tile", width, dtype, bw, col)


def _out_acc(rows, width, bw, col=lambda c: 0):
    return ("acc", rows, width, bw, col)


def _rows_call(name, body, n_rows, tm, ncol, ins, outs, plans=None):
    in_specs, args = [], []
    for e in ins:
        if e[0] == "tile":
            _, arr, bw, col = e
            in_specs.append(pl.BlockSpec((tm, bw), functools.partial(lambda c, i, col: (i, col(c)), col=col)))
        else:
            arr = e[1]
            in_specs.append(pl.BlockSpec(arr.shape, functools.partial(lambda c, i, nd: (0,) * nd, nd=arr.ndim)))
        args.append(arr)
    out_specs, out_shape = [], []
    for e in outs:
        if e[0] == "tile":
            _, width, dtype, bw, col = e
            out_specs.append(pl.BlockSpec((tm, bw), functools.partial(lambda c, i, col: (i, col(c)), col=col)))
            out_shape.append(_sds((n_rows, width), dtype))
        else:
            _, rows, width, bw, col = e
            out_specs.append(pl.BlockSpec((rows, bw), functools.partial(lambda c, i, col: (0, col(c)), col=col)))
            out_shape.append(_sds((rows, width), F32))
    out, carried = _call(body, plans, name=name, grid=(ncol, n_rows // tm), in_specs=in_specs, out_specs=out_specs,
                         out_shape=out_shape, args=args)
    return out if plans is None else (out, carried)


def _acc(ref, val):
    i = pl.program_id(1)

    @pl.when(i == 0)
    def _():
        ref[...] = val

    @pl.when(i > 0)
    def _():
        ref[...] += val


def _rinv(z):
    return lax.rsqrt(jnp.mean(z * z, axis=-1, keepdims=True) + EPS)


def _norm_bwd(dy, zhat, r, w):
    dyw = dy * w
    return r * (dyw - zhat * jnp.mean(dyw * zhat, axis=-1, keepdims=True))


def _norm_fwd(x, w, name):
    def body(x_ref, w_ref, h_ref):
        xv = x_ref[...]
        h_ref[...] = _bf(xv * _rinv(xv) * w_ref[...])

    return _rows_call(name, body, x.shape[0], 512, 1, [_tile(x, D_MODEL), _full(w)],
                      [_out_tile(D_MODEL, BF16, D_MODEL)])[0]


def _prenorm_bwd(dh, xin, w, dres, name, plans=None):
    def body(dh_ref, x_ref, w_ref, dres_ref, dx_ref, dw_ref):
        xv = x_ref[...]
        r = _rinv(xv)
        xhat = xv * r
        dhv = dh_ref[...]
        dx_ref[...] = dres_ref[...] + _norm_bwd(dhv, xhat, r, w_ref[...])
        _acc(dw_ref, jnp.sum(dhv * xhat, axis=0, keepdims=True))

    return _rows_call(name, body, xin.shape[0], 512, 1,
                      [_tile(dh, D_MODEL), _tile(xin, D_MODEL), _full(w), _tile(dres, D_MODEL)],
                      [_out_tile(D_MODEL, F32, D_MODEL), _out_acc(1, D_MODEL, D_MODEL)], plans)


def _postnorm_bwd(dout, z, w, name):
    def body(do_ref, z_ref, w_ref, dz_ref, dw_ref):
        zv = z_ref[...]
        r = _rinv(zv)
        zhat = zv * r
        dov = do_ref[...]
        dz_ref[...] = _bf(_norm_bwd(dov, zhat, r, w_ref[...]))
        _acc(dw_ref, jnp.sum(dov * zhat, axis=0, keepdims=True))

    return _rows_call(name, body, z.shape[0], 512, 1, [_tile(dout, D_MODEL), _tile(z, D_MODEL), _full(w)],
                      [_out_tile(D_MODEL, BF16, D_MODEL), _out_acc(1, D_MODEL, D_MODEL)])


def _t5_bucket(dist):
    n = jnp.maximum(dist, 0)
    nf = jnp.maximum(n, 1).astype(F32)
    large = MAX_EXACT + (jnp.log(nf / MAX_EXACT) / math.log(MAX_DISTANCE / MAX_EXACT)
                         * (NUM_BUCKETS - MAX_EXACT)).astype(jnp.int32)
    large = jnp.minimum(large, NUM_BUCKETS - 1)
    return jnp.where(n < MAX_EXACT, n, large)


def _band_rel():
    return jnp.arange(BLK)[:, None] + BLK - jnp.arange(2 * BLK)[None, :]


def _band_mask(n):
    row = lax.broadcasted_iota(jnp.int32, (BLK, 2 * BLK), 0)
    col = lax.broadcasted_iota(jnp.int32, (BLK, 2 * BLK), 1)
    rel = row + BLK - col
    return (rel >= 0) & (rel <= BLK) & ((col >= BLK) | (n > 0))


RES_UNROLL = 4


def _heads_per_step(d):
    return HEADS if d == 1 else LANES // HEAD_DIM


def _sub_rows(r, d):
    return pl.ds(r, BLK, stride=d) if d > 1 else pl.ds(0, BLK)


def _for_residues(d, fn):
    if d <= RES_UNROLL:
        for r in range(d):
            fn(r)
    else:
        def group(i, carry):
            for k in range(RES_UNROLL):
                fn(i * RES_UNROLL + k)
            return carry

        lax.fori_loop(0, d // RES_UNROLL, group, 0)


def _attn_specs(d, g, qblock):
    cw = _heads_per_step(d) * HEAD_DIM

    def col(part, hp):
        return (g * 3 + part) * (GROUP_W // cw) + hp

    def cur(part):
        return pl.BlockSpec((d * BLK, cw), lambda hp, n: (qblock(n), col(part, hp)))

    def prev(part):
        return pl.BlockSpec((d * BLK, cw), lambda hp, n: (jnp.maximum(qblock(n) - 1, 0), col(part, hp)))

    return cur, prev


def _attn_fwd(proj, bias, g, name, plans=None):
    S = proj.shape[0]
    d = DILATIONS[g]
    NB = S // (d * BLK)
    hps = _heads_per_step(d)

    def body(q_ref, kp_ref, kc_ref, vp_ref, vc_ref, b_ref, o_ref, lse_ref):
        hp = pl.program_id(0)
        mask = _band_mask(pl.program_id(1))

        def residue(r):
            rows = _sub_rows(r, d)
            q2 = q_ref[rows, :]
            k2 = jnp.concatenate([kp_ref[rows, :], kc_ref[rows, :]], axis=0)
            v2 = jnp.concatenate([vp_ref[rows, :], vc_ref[rows, :]], axis=0)
            outs, lses = [], []
            for hh in range(hps):
                hs = slice(hh * HEAD_DIM, (hh + 1) * HEAD_DIM)
                s = _dot(_bf(q2[:, hs]), _bf(k2[:, hs]), NT) * (HEAD_DIM ** -0.5) + b_ref[hp * hps + hh]
                s = jnp.where(mask, s, NEG_INF)
                m = jnp.max(s, axis=-1, keepdims=True)
                p = jnp.exp(s - m)
                l = jnp.sum(p, axis=-1, keepdims=True)
                outs.append(_dot(_bf(p), _bf(v2[:, hs]), NN) / l)
                lses.append(jnp.broadcast_to(m + jnp.log(l), (BLK, HEAD_DIM)))
            o_ref[rows, :] = jnp.concatenate(outs, axis=1)
            lse_ref[rows, :] = jnp.concatenate(lses, axis=1)

        _for_residues(d, residue)

    cur, prev = _attn_specs(d, g, lambda n: n)
    out = pl.BlockSpec((d * BLK, hps * HEAD_DIM), lambda hp, n: (n, hp))
    res, carried = _call(
        body, plans, name=name, grid=(HEADS // hps, NB),
        in_specs=[cur(0), prev(1), cur(1), prev(2), cur(2),
                  pl.BlockSpec((HEADS, BLK, 2 * BLK), lambda hp, n: (0, 0, 0))],
        out_specs=[out, out], out_shape=[_sds((S, GROUP_W), F32)] * 2,
        args=(proj, proj, proj, proj, proj, bias))
    return res if plans is None else (res, carried)


def _attn_merge(os_, lses, name):
    def body(o0, o1, o2, l0, l1, l2, y_ref, lse_ref):
        a, b, c = l0[...], l1[...], l2[...]
        m = jnp.maximum(jnp.maximum(a, b), c)
        ea, eb, ec = jnp.exp(a - m), jnp.exp(b - m), jnp.exp(c - m)
        den = ea + eb + ec
        y_ref[...] = (ea * o0[...] + eb * o1[...] + ec * o2[...]) / den
        lse_ref[...] = m + jnp.log(den)

    S = os_[0].shape[0]
    return _rows_call(name, body, S, 512, 1, [_tile(t, GROUP_W) for t in (*os_, *lses)],
                      [_out_tile(GROUP_W, F32, GROUP_W)] * 2)


def _attn_bwd(proj, bias, lse, y, dy, g, name, plans=None):
    S = proj.shape[0]
    d = DILATIONS[g]
    NB = S // (d * BLK)
    hps = _heads_per_step(d)

    def body(q_ref, kp_ref, kc_ref, vp_ref, vc_ref, b_ref, l_ref, y_ref, dy_ref,
             dq_ref, dk_ref, dv_ref, db_ref, ck_ref, cv_ref):
        hp, n = pl.program_id(0), pl.program_id(1)

        @pl.when((hp == 0) & (n == 0))
        def _():
            db_ref[...] = jnp.zeros_like(db_ref)

        @pl.when(n == 0)
        def _():
            ck_ref[...] = jnp.zeros_like(ck_ref)
            cv_ref[...] = jnp.zeros_like(cv_ref)

        @pl.when(n < NB)
        def _():
            mask = _band_mask(n)

            def residue(r):
                rows = _sub_rows(r, d)
                q2 = q_ref[rows, :]
                k2 = jnp.concatenate([kp_ref[rows, :], kc_ref[rows, :]], axis=0)
                v2 = jnp.concatenate([vp_ref[rows, :], vc_ref[rows, :]], axis=0)
                l2, y2, dy2 = l_ref[rows, :], y_ref[rows, :], dy_ref[rows, :]
                dqs, dks, dvs = [], [], []
                for hh in range(hps):
                    hs = slice(hh * HEAD_DIM, (hh + 1) * HEAD_DIM)
                    q, kb, vb = _bf(q2[:, hs]), _bf(k2[:, hs]), _bf(v2[:, hs])
                    s = _dot(q, kb, NT) * (HEAD_DIM ** -0.5) + b_ref[hp * hps + hh]
                    s = jnp.where(mask, s, NEG_INF)
                    p = jnp.exp(s - l2[:, hh * HEAD_DIM:hh * HEAD_DIM + 1])
                    dyh = dy2[:, hs]
                    delta = jnp.sum(dyh * y2[:, hs], axis=-1, keepdims=True)
                    dyb = _bf(dyh)
                    ds = p * (_dot(dyb, vb, NT) - delta)
                    db_ref[hp * hps + hh] += ds
                    dsb = _bf(ds * (HEAD_DIM ** -0.5))
                    dqs.append(_dot(dsb, kb, NN))
                    dks.append(_dot(dsb, q, TN))
                    dvs.append(_dot(_bf(p), dyb, TN))
                dkb = jnp.concatenate(dks, axis=1)
                dvb = jnp.concatenate(dvs, axis=1)
                dq_ref[rows, :] = jnp.concatenate(dqs, axis=1)
                dk_ref[rows, :] = ck_ref[rows, :] + dkb[:BLK]
                dv_ref[rows, :] = cv_ref[rows, :] + dvb[:BLK]
                ck_ref[rows, :] = dkb[BLK:]
                cv_ref[rows, :] = dvb[BLK:]

            _for_residues(d, residue)

        @pl.when(n == NB)
        def _():
            dk_ref[...] = ck_ref[...]
            dv_ref[...] = cv_ref[...]

    def qn(n):
        return jnp.minimum(n, NB - 1)

    cur, prev = _attn_specs(d, g, qn)
    cw = hps * HEAD_DIM
    row = pl.BlockSpec((d * BLK, cw), lambda hp, n: (qn(n), hp))
    done = pl.BlockSpec((d * BLK, cw), lambda hp, n: (jnp.maximum(n - 1, 0), hp))
    (dq, dk, dv, db), carried = _call(
        body, plans, name=name, grid=(HEADS // hps, NB + 1),
        in_specs=[cur(0), prev(1), cur(1), prev(2), cur(2),
                  pl.BlockSpec((HEADS, BLK, 2 * BLK), lambda hp, n: (0, 0, 0)), row, row, row],
        out_specs=[row, done, done, pl.BlockSpec((HEADS, BLK, 2 * BLK), lambda hp, n: (0, 0, 0))],
        out_shape=[_sds((S, GROUP_W), F32)] * 3 + [_sds((HEADS, BLK, 2 * BLK), F32)],
        scratch_shapes=[pltpu.VMEM((d * BLK, cw), F32)] * 2,
        args=(proj, proj, proj, proj, proj, bias, lse, y, dy))
    return ([dq, dk, dv], db) if plans is None else ([dq, dk, dv], db, carried)


BAND = BLK * 2 * BLK


def _bucket_onehot():
    buckets = jnp.stack([_t5_bucket(_band_rel() * d) for d in DILATIONS]).reshape(N_GROUPS, 1, BAND)
    return (buckets == jnp.arange(NUM_BUCKETS).reshape(1, NUM_BUCKETS, 1)).astype(F32)


def _relbias_fwd(rel_bias, name):
    table = rel_bias.reshape(NUM_BUCKETS, N_GROUPS, HEADS).transpose(1, 0, 2)

    def body(t_ref, oh_ref, o_ref):
        o_ref[...] = lax.dot_general(t_ref[...], oh_ref[...], (TN, ((), ())), preferred_element_type=F32,
                                     precision=lax.Precision.HIGHEST)

    out = pl.pallas_call(
        body, name=name, grid=(N_GROUPS,),
        in_specs=[pl.BlockSpec((None, NUM_BUCKETS, HEADS), lambda g: (g, 0, 0)),
                  pl.BlockSpec((None, NUM_BUCKETS, BAND), lambda g: (g, 0, 0))],
        out_specs=pl.BlockSpec((None, HEADS, BAND), lambda g: (g, 0, 0)),
        out_shape=_sds((N_GROUPS, HEADS, BAND), F32), compiler_params=_cp(1))(table, _bucket_onehot())
    return out.reshape(N_GROUPS, HEADS, BLK, 2 * BLK)


def _relbias_bwd(dbs, name):
    band = BAND
    onehot = _bucket_onehot()
    dbf = jnp.stack([db.reshape(HEADS, band) for db in dbs])

    def body(oh_ref, db_ref, o_ref):
        o_ref[...] = lax.dot_general(oh_ref[...], db_ref[...], (NT, ((), ())), preferred_element_type=F32,
                                     precision=lax.Precision.HIGHEST)

    out = pl.pallas_call(
        body, name=name, grid=(N_GROUPS,),
        in_specs=[pl.BlockSpec((None, NUM_BUCKETS, band), lambda g: (g, 0, 0)),
                  pl.BlockSpec((None, HEADS, band), lambda g: (g, 0, 0))],
        out_specs=pl.BlockSpec((None, NUM_BUCKETS, HEADS), lambda g: (g, 0, 0)),
        out_shape=_sds((N_GROUPS, NUM_BUCKETS, HEADS), F32), compiler_params=_cp(1))(onehot, dbf)
    return out.transpose(1, 0, 2).reshape(NUM_BUCKETS, N_GROUPS * HEADS)


def _chunk_pos(shape):
    return lax.broadcasted_iota(jnp.int32, shape, 0) % HG_CHUNK


def _chunk_cumsum(v):
    pos = _chunk_pos(v.shape)
    s = 1
    while s < HG_CHUNK:
        v = v + jnp.where(pos >= s, pltpu.roll(v, s, 0), 0.0)
        s *= 2
    return v


def _chunk_rev_cumsum(v):
    pos = _chunk_pos(v.shape)
    n = v.shape[0]
    s = 1
    while s < HG_CHUNK:
        v = v + jnp.where(pos < HG_CHUNK - s, pltpu.roll(v, n - s, 0), 0.0)
        s *= 2
    return v


def _lower_bound(raw):
    a0, a1 = raw[0:1], raw[1:2]
    m = jnp.maximum(a0, a1)
    e0, e1 = jnp.exp(a0 - m), jnp.exp(a1 - m)
    return e0 / (e0 + e1)


def _hg_gates(qr, fr, lb):
    sf = _sigmoid(fr)
    f = lb + (1.0 - lb) * sf
    sq = _sigmoid(qr)
    return qr * sq, sq, f, sf


HG_COL0 = QKV_W // HG_W


def _hgrn_fwd(proj, lb_raw, nw, name):
    S = proj.shape[0]
    ncs = HG_TILE // HG_CHUNK
    tril = jnp.tril(jnp.ones((HG_CHUNK, HG_CHUNK), dtype=bool))

    def body(q_ref, f_ref, i_ref, og_ref, lb_ref, nw_ref, y_ref, o_ref, st_ref, state):
        @pl.when(pl.program_id(0) == 0)
        def _():
            state[...] = jnp.zeros_like(state)

        lb = _lower_bound(lb_ref[...])
        q, _, f, _ = _hg_gates(q_ref[...], f_ref[...], lb)
        k = 1.0 - f
        G = _chunk_cumsum(jnp.log(f))
        row = lax.broadcasted_iota(jnp.int32, (HG_CHUNK, HG_CHUNK), 0)
        col = lax.broadcasted_iota(jnp.int32, (HG_CHUNK, HG_CHUNK), 1)
        for h in range(HG_HEADS):
            hs = slice(h * HG_DK, (h + 1) * HG_DK)
            st = state[h]
            for c in range(ncs):
                cs = slice(c * HG_CHUNK, (c + 1) * HG_CHUNK)
                Gc = G[cs, hs]
                gl = Gc[HG_CHUNK - 1:HG_CHUNK]
                qt = _bf(q[cs, hs] * jnp.exp(Gc))
                kt = _bf(k[cs, hs] * jnp.exp(-Gc))
                kd = _bf(k[cs, hs] * jnp.exp(gl - Gc))
                v = _bf(i_ref[cs, hs])
                A = jnp.where(row >= col, _dot(qt, kt, NT), 0.0)
                o_ref[cs, hs] = _dot(_bf(A), v, NN) + _dot(qt, _bf(st), NT)
                st_ref[c, h] = st
                st = st * jnp.exp(gl) + _dot(v, kd, TN)
            state[h] = st
            oh = o_ref[:, hs]
            og = og_ref[:, hs]
            y_ref[:, hs] = oh * _rinv(oh) * nw_ref[...] * (og * _sigmoid(og))

    def colspec(j):
        return pl.BlockSpec((HG_TILE, HG_W), lambda i: (i, HG_COL0 + j))

    return pl.pallas_call(
        body, name=name, grid=(S // HG_TILE,),
        in_specs=[colspec(0), colspec(1), colspec(2), colspec(3),
                  pl.BlockSpec((2, HG_W), lambda i: (0, 0)), pl.BlockSpec((1, HG_DK), lambda i: (0, 0))],
        out_specs=[pl.BlockSpec((HG_TILE, HG_W), lambda i: (i, 0))] * 2
        + [pl.BlockSpec((ncs, HG_HEADS, HG_DK, HG_DK), lambda i: (i, 0, 0, 0))],
        out_shape=[_sds((S, HG_W), F32)] * 2 + [_sds((S // HG_CHUNK, HG_HEADS, HG_DK, HG_DK), F32)],
        scratch_shapes=[pltpu.VMEM((HG_HEADS, HG_DK, HG_DK), F32)],
        compiler_params=_cp(1))(proj, proj, proj, proj, lb_raw, nw)


def _hgrn_bwd(proj, lb_raw, nw, o, states, dy, name):
    S = proj.shape[0]
    ncs = HG_TILE // HG_CHUNK
    nt = S // HG_TILE

    def body(q_ref, f_ref, i_ref, og_ref, lb_ref, nw_ref, o_ref, st_ref, dy_ref,
             dq_ref, df_ref, di_ref, dog_ref, dlb_ref, dnw_ref, dstate, do_s, dG_s, dgl_s, dk_s, dlb_s):
        step = pl.program_id(0)

        @pl.when(step == 0)
        def _():
            dstate[...] = jnp.zeros_like(dstate)
            dlb_s[...] = jnp.zeros_like(dlb_s)
            dnw_ref[...] = jnp.zeros_like(dnw_ref)

        lb = _lower_bound(lb_ref[...])
        qr = q_ref[...]
        q, sq, f, sf = _hg_gates(qr, f_ref[...], lb)
        k = 1.0 - f
        G = _chunk_cumsum(jnp.log(f))
        nwv = nw_ref[...]
        row = lax.broadcasted_iota(jnp.int32, (HG_CHUNK, HG_CHUNK), 0)
        col = lax.broadcasted_iota(jnp.int32, (HG_CHUNK, HG_CHUNK), 1)
        for h in range(HG_HEADS):
            hs = slice(h * HG_DK, (h + 1) * HG_DK)
            oh = o_ref[:, hs]
            r = _rinv(oh)
            ohat = oh * r
            og = og_ref[:, hs]
            sg = _sigmoid(og)
            dyh = dy_ref[:, hs]
            don = dyh * (og * sg)
            dog_ref[:, hs] = _bf(dyh * (ohat * nwv) * (sg * (1.0 + og * (1.0 - sg))))
            dnw_ref[...] += jnp.sum(don * ohat, axis=0, keepdims=True)
            do_s[:, hs] = _norm_bwd(don, ohat, r, nwv)
            dst = dstate[h]
            for c in reversed(range(ncs)):
                cs = slice(c * HG_CHUNK, (c + 1) * HG_CHUNK)
                Gc = G[cs, hs]
                gl = Gc[HG_CHUNK - 1:HG_CHUNK]
                eG, enG, edG, egl = jnp.exp(Gc), jnp.exp(-Gc), jnp.exp(gl - Gc), jnp.exp(gl)
                qt, kt, kd = q[cs, hs] * eG, k[cs, hs] * enG, k[cs, hs] * edG
                qtb, ktb, kdb = _bf(qt), _bf(kt), _bf(kd)
                v = _bf(i_ref[cs, hs])
                do = _bf(do_s[cs, hs])
                st = st_ref[c, h]
                dstb = _bf(dst)
                A = jnp.where(row >= col, _dot(qtb, ktb, NT), 0.0)
                dA = _bf(jnp.where(row >= col, _dot(do, v, NT), 0.0))
                di_ref[cs, hs] = _bf(_dot(_bf(A), do, TN) + _dot(kdb, dstb, NT))
                dqt = _dot(dA, ktb, NN) + _dot(do, _bf(st), NN)
                dkt = _dot(dA, qtb, TN)
                dkd = _dot(v, dstb, NN)
                dgl = egl * jnp.sum(st * dst, axis=0, keepdims=True) + jnp.sum(dkd * kd, axis=0, keepdims=True)
                dst = dst * egl + _dot(do, qtb, TN)
                dq_ref[cs, hs] = _bf(dqt * eG * (sq[cs, hs] * (1.0 + qr[cs, hs] * (1.0 - sq[cs, hs]))))
                dk_s[cs, hs] = dkt * enG + dkd * edG
                dG_s[cs, hs] = dqt * qt - dkt * kt - dkd * kd
                dgl_s[cs, hs] = jnp.broadcast_to(dgl, (HG_CHUNK, HG_DK))
            dstate[h] = dst
        dg = _chunk_rev_cumsum(dG_s[...]) + dgl_s[...]
        dfv = dg / f - dk_s[...]
        df_ref[...] = _bf(dfv * (1.0 - lb) * sf * (1.0 - sf))
        dlb_s[...] += jnp.sum(dfv * (1.0 - sf), axis=0, keepdims=True)

        @pl.when(step == nt - 1)
        def _():
            t = dlb_s[...] * lb * (1.0 - lb)
            dlb_ref[...] = jnp.concatenate([t, -t], axis=0)

    def colspec(j):
        return pl.BlockSpec((HG_TILE, HG_W), lambda i: (nt - 1 - i, HG_COL0 + j))

    tile = pl.BlockSpec((HG_TILE, HG_W), lambda i: (nt - 1 - i, 0))
    outs = pl.pallas_call(
        body, name=name, grid=(nt,),
        in_specs=[colspec(0), colspec(1), colspec(2), colspec(3),
                  pl.BlockSpec((2, HG_W), lambda i: (0, 0)), pl.BlockSpec((1, HG_DK), lambda i: (0, 0)),
                  tile, pl.BlockSpec((ncs, HG_HEADS, HG_DK, HG_DK), lambda i: (nt - 1 - i, 0, 0, 0)), tile],
        out_specs=[tile] * 4 + [pl.BlockSpec((2, HG_W), lambda i: (0, 0)), pl.BlockSpec((1, HG_DK), lambda i: (0, 0))],
        out_shape=[_sds((S, HG_W), BF16)] * 4 + [_sds((2, HG_W), F32), _sds((1, HG_DK), F32)],
        scratch_shapes=[pltpu.VMEM((HG_HEADS, HG_DK, HG_DK), F32)] + [pltpu.VMEM((HG_TILE, HG_W), F32)] * 4
        + [pltpu.VMEM((1, HG_W), F32)],
        compiler_params=_cp(1))(proj, proj, proj, proj, lb_raw, nw, o, states, dy)
    return outs[:4], outs[4], outs[5]


GATE_COL0 = (QKV_W + 4 * HG_W) // GROUP_W
HALF_D = D_MODEL // 2


def _gate_tiles(proj):
    return [_tile(proj, HALF_D, lambda c: GATE_COL0 + c), _tile(proj, HALF_D, lambda c: GATE_COL0 + 2 + c)]


def _merge_fwd(za, zh, proj, name):
    def body(za_ref, zh_ref, g0_ref, g1_ref, m_ref):
        m_ref[...] = _bf(_sigmoid(g0_ref[...]) * za_ref[...] + _sigmoid(g1_ref[...]) * zh_ref[...])

    col = lambda c: c
    return _rows_call(name, body, za.shape[0], 512, 2,
                      [_tile(za, HALF_D, col), _tile(zh, HALF_D, col), *_gate_tiles(proj)],
                      [_out_tile(D_MODEL, BF16, HALF_D, col)])[0]


def _merge_bwd(dm, za, zh, proj, name):
    def body(dm_ref, za_ref, zh_ref, g0_ref, g1_ref, dza_ref, dzh_ref, dg0_ref, dg1_ref):
        dmv = dm_ref[...]
        s0, s1 = _sigmoid(g0_ref[...]), _sigmoid(g1_ref[...])
        dza_ref[...] = _bf(dmv * s0)
        dzh_ref[...] = _bf(dmv * s1)
        dg0_ref[...] = _bf(dmv * za_ref[...] * s0 * (1.0 - s0))
        dg1_ref[...] = _bf(dmv * zh_ref[...] * s1 * (1.0 - s1))

    col = lambda c: c
    return _rows_call(name, body, za.shape[0], 512, 2,
                      [_tile(dm, HALF_D, col), _tile(za, HALF_D, col), _tile(zh, HALF_D, col), *_gate_tiles(proj)],
                      [_out_tile(D_MODEL, BF16, HALF_D, col)] * 4)


def _mix_out(mo, x, w_post, w_pre, name):
    def body(mo_ref, x_ref, wp_ref, wf_ref, x1_ref, h2_ref):
        z = mo_ref[...]
        x1 = x_ref[...] + z * _rinv(z) * wp_ref[...]
        x1_ref[...] = x1
        h2_ref[...] = _bf(x1 * _rinv(x1) * wf_ref[...])

    return _rows_call(name, body, x.shape[0], 512, 1,
                      [_tile(mo, D_MODEL), _tile(x, D_MODEL), _full(w_post), _full(w_pre)],
                      [_out_tile(D_MODEL, F32, D_MODEL), _out_tile(D_MODEL, BF16, D_MODEL)])


def _loss_head(ffo, x1, tgt, w, name):
    def body(f_ref, x1_ref, t_ref, w_ref, dx_ref, df_ref, dw_ref, loss_ref):
        z = f_ref[...]
        r = _rinv(z)
        zhat = z * r
        wv = w_ref[...]
        e = x1_ref[...] + zhat * wv - t_ref[...]
        dx = e * (1.0 / D_MODEL)
        dx_ref[...] = dx
        df_ref[...] = _bf(_norm_bwd(dx, zhat, r, wv))
        _acc(dw_ref, jnp.sum(dx * zhat, axis=0, keepdims=True))
        part = 0.5 * jnp.sum(jnp.sum(e * e, axis=1, keepdims=True), axis=0, keepdims=True) * (1.0 / D_MODEL)
        _acc(loss_ref, jnp.broadcast_to(part, (1, LANES)))

    return _rows_call(name, body, x1.shape[0], 512, 1,
                      [_tile(ffo, D_MODEL), _tile(x1, D_MODEL), _tile(tgt, D_MODEL), _full(w)],
                      [_out_tile(D_MODEL, F32, D_MODEL), _out_tile(D_MODEL, BF16, D_MODEL),
                       _out_acc(1, D_MODEL, D_MODEL), _out_acc(1, LANES, LANES)])


CONV_CB = D_FF // 2
CONV_TM = 512
HALO = 8
SQRT_HALF = 0.7071067811865476
INV_SQRT_2PI = 0.3989422804014327


def _conv_taps(u_ref, halo_ref, first):
    u = u_ref[...]
    row = lax.broadcasted_iota(jnp.int32, u.shape, 0)
    p1 = jnp.where(first, 0.0, halo_ref[HALO - 1:HALO, :])
    p2 = jnp.where(first, 0.0, halo_ref[HALO - 2:HALO - 1, :])
    u1 = jnp.where(row == 0, p1, pltpu.roll(u, 1, 0))
    u2 = jnp.where(row == 0, p2, jnp.where(row == 1, p1, pltpu.roll(u, 2, 0)))
    return u2, u1, u


def _conv(taps, w_ref, b_ref):
    return b_ref[...] + w_ref[0:1, :] * taps[0] + w_ref[1:2, :] * taps[1] + w_ref[2:3, :] * taps[2]


def _conv_specs(tm):
    nh = tm // HALO
    nc = D_FF // CONV_CB

    def tile(off):
        return pl.BlockSpec((tm, CONV_CB), lambda c, i: (i, off + c))

    def halo(off):
        return pl.BlockSpec((HALO, CONV_CB), lambda c, i: (jnp.maximum(i * nh - 1, 0), off + c))

    def small(rows, off):
        return pl.BlockSpec((rows, CONV_CB), lambda c, i: (0, off + c))

    return nc, tile, halo, small


def _conv_gelu_fwd(u, cw, cb, name):
    S = u.shape[0]
    tm = CONV_TM
    nc, tile, halo, small = _conv_specs(tm)

    def body(ug, hg, uv, hv, wg, wv, bg, bv, a_ref):
        first = pl.program_id(1) == 0
        cg = _conv(_conv_taps(ug, hg, first), wg, bg)
        cv = _conv(_conv_taps(uv, hv, first), wv, bv)
        a_ref[...] = _bf(0.5 * cg * (1.0 + lax.erf(cg * SQRT_HALF)) * cv)

    return pl.pallas_call(
        body, name=name, grid=(nc, S // tm),
        in_specs=[tile(0), halo(0), tile(nc), halo(nc), small(3, 0), small(3, nc), small(1, 0), small(1, nc)],
        out_specs=tile(0), out_shape=_sds((S, D_FF), BF16), compiler_params=_cp(2))(u, u, u, u, cw, cw, cb, cb)


def _conv_gelu_bwd(u, da, cw, cb, name, plans=None):
    S = u.shape[0]
    tm = CONV_TM
    nc, tile, halo, small = _conv_specs(tm)

    def body(ug, hg, uv, hv, wg, wv, bg, bv, da_ref, dcg_ref, dcv_ref, dwg_ref, dwv_ref, dbg_ref, dbv_ref):
        first = pl.program_id(1) == 0
        tg = _conv_taps(ug, hg, first)
        tv = _conv_taps(uv, hv, first)
        cg = _conv(tg, wg, bg)
        cv = _conv(tv, wv, bv)
        phi = 0.5 * (1.0 + lax.erf(cg * SQRT_HALF))
        dav = da_ref[...]
        dcg = dav * cv * (phi + cg * jnp.exp(-0.5 * cg * cg) * INV_SQRT_2PI)
        dcv = dav * (cg * phi)
        dcg_ref[...] = dcg
        dcv_ref[...] = dcv
        for dc, taps, dw_ref, db_ref in ((dcg, tg, dwg_ref, dbg_ref), (dcv, tv, dwv_ref, dbv_ref)):
            _acc(db_ref, jnp.sum(dc, axis=0, keepdims=True))
            for j in range(3):
                _acc(dw_ref.at[j:j + 1, :], jnp.sum(dc * taps[j], axis=0, keepdims=True))

    res, carried = _call(
        body, plans, name=name, grid=(nc, S // tm),
        in_specs=[tile(0), halo(0), tile(nc), halo(nc), small(3, 0), small(3, nc), small(1, 0), small(1, nc), tile(0)],
        out_specs=[tile(0), tile(0), small(3, 0), small(3, 0), small(1, 0), small(1, 0)],
        out_shape=[_sds((S, D_FF), F32)] * 2 + [_sds((3, D_FF), F32)] * 2 + [_sds((1, D_FF), F32)] * 2,
        args=(u, u, u, u, cw, cw, cb, cb, da))
    return res if plans is None else (res, carried)


def _conv_input_bwd(dcg, dcv, cw, name, plans=None):
    S = dcg.shape[0]
    tm = CONV_TM
    nc, tile, _, small = _conv_specs(tm)
    nh = tm // HALO
    nt = S // tm

    def nxt(off):
        return pl.BlockSpec((HALO, CONV_CB), lambda c, i: (jnp.minimum((i + 1) * nh, S // HALO - 1), off + c))

    def body(g_ref, ng_ref, v_ref, nv_ref, wg, wv, dug_ref, duv_ref):
        last = pl.program_id(1) == nt - 1
        for dc_ref, n_ref, w_ref, du_ref in ((g_ref, ng_ref, wg, dug_ref), (v_ref, nv_ref, wv, duv_ref)):
            dc = dc_ref[...]
            row = lax.broadcasted_iota(jnp.int32, dc.shape, 0)
            n1 = jnp.where(last, 0.0, n_ref[0:1, :])
            n2 = jnp.where(last, 0.0, n_ref[1:2, :])
            d1 = jnp.where(row == tm - 1, n1, pltpu.roll(dc, tm - 1, 0))
            d2 = jnp.where(row == tm - 1, n2, jnp.where(row == tm - 2, n1, pltpu.roll(dc, tm - 2, 0)))
            du_ref[...] = _bf(w_ref[2:3, :] * dc + w_ref[1:2, :] * d1 + w_ref[0:1, :] * d2)

    res, carried = _call(
        body, plans, name=name, grid=(nc, nt),
        in_specs=[tile(0), nxt(0), tile(0), nxt(0), small(3, 0), small(3, nc)],
        out_specs=[tile(0), tile(0)], out_shape=[_sds((S, D_FF), BF16)] * 2,
        args=(dcg, dcg, dcv, dcv, cw, cw))
    return res if plans is None else (res, carried)


def _row_tile(n, cap):
    best = n
    for t in range(16, cap + 1, 16):
        if n % t == 0:
            best = t
    return best if best <= cap else n


def _adamw(w, g, m, v, name):
    R, C = w.shape
    tr = _row_tile(R, max(16, (512 * 1024) // (4 * C) // 16 * 16))

    def body(w_ref, g_ref, m_ref, v_ref, d_ref, nm_ref, nv_ref):
        gv = g_ref[...]
        nm = ADAM_B1 * m_ref[...] + (1.0 - ADAM_B1) * gv
        nv = ADAM_B2 * v_ref[...] + (1.0 - ADAM_B2) * (gv * gv)
        m_hat = nm / (1.0 - ADAM_B1 ** ADAM_STEP)
        v_hat = nv / (1.0 - ADAM_B2 ** ADAM_STEP)
        d_ref[...] = -ADAM_LR * (m_hat / (jnp.sqrt(v_hat) + ADAM_EPS) + ADAM_WD * w_ref[...])
        nm_ref[...] = nm
        nv_ref[...] = nv

    spec = pl.BlockSpec((tr, C), lambda i: (i, 0))
    return pl.pallas_call(body, name=name, grid=(R // tr,), in_specs=[spec] * 4, out_specs=[spec] * 3,
                          out_shape=[_sds((R, C), F32)] * 3, compiler_params=_cp(1))(w, g, m, v)


def _pair_sum(gfull, rcv, c_idx, name):
    nb, R, C = gfull.shape
    half = R // 2
    tr = _row_tile(half, 256)
    nt = half // tr

    def body(c_ref, g_ref, r_ref, o_ref):
        o_ref[...] = _bf(g_ref[...] + r_ref[...])

    return pl.pallas_call(
        body, name=name,
        grid_spec=pltpu.PrefetchScalarGridSpec(
            num_scalar_prefetch=1, grid=(nb, nt),
            in_specs=[pl.BlockSpec((None, tr, C), lambda j, i, c_ref: (j, c_ref[0] * nt + i, 0)),
                      pl.BlockSpec((None, tr, C), lambda j, i, c_ref: (j, i, 0))],
            out_specs=pl.BlockSpec((None, tr, C), lambda j, i, c_ref: (j, i, 0))),
        out_shape=_sds((nb, half, C), BF16), compiler_params=_cp(2))(c_idx, gfull, rcv)


def _chip_sum(arrived, own, place, name, piece=0, n_pieces=1, into=None):
    nb, H, C = arrived.shape
    tr = _row_tile(H, 256)
    nt = H // tr
    extra = [] if into is None else [into]

    def body(pl_ref, *refs):
        o_ref = refs[nb + 1 + len(extra)]
        me = pl_ref[0]
        acc = None
        for k in range(nb):
            term = jnp.where(me == k, refs[nb][...], refs[k][...]).astype(F32)
            acc = term if acc is None else acc + term
        o_ref[...] = acc

    def other(k):
        return pl.BlockSpec((None, tr, C), lambda i, p: (jnp.where(p[0] == k, (k + 1) % nb, k), i, 0))

    return pl.pallas_call(
        body, name=name,
        grid_spec=pltpu.PrefetchScalarGridSpec(
            num_scalar_prefetch=1, grid=(nt,),
            in_specs=[other(k) for k in range(nb)] + [pl.BlockSpec((None, tr, C), lambda i, p: (p[0], i, 0))]
            + [ANY] * len(extra),
            out_specs=pl.BlockSpec((tr, C), lambda i, p: ((2 * piece + p[1]) * nt + i, 0))),
        out_shape=_sds((n_pieces * 2 * H, C), F32),
        input_output_aliases={1 + nb + 1: 0} if extra else {},
        compiler_params=_cp(1))(place, *([arrived] * nb), own, *extra)


def _cast_into_slot(shard, place, name):
    R, C = shard.shape
    tr = _row_tile(R, 256)

    def body(pl_ref, s_ref, o_ref):
        o_ref[...] = _bf(s_ref[...])

    return pl.pallas_call(
        body, name=name,
        grid_spec=pltpu.PrefetchScalarGridSpec(
            num_scalar_prefetch=1, grid=(R // tr,),
            in_specs=[pl.BlockSpec((tr, C), lambda i, p: (i, 0))],
            out_specs=pl.BlockSpec((None, tr, C), lambda i, p: (p[0], i, 0))),
        out_shape=_sds((N_CHIPS, R, C), BF16), compiler_params=_cp(1))(place, shard)


def _place():
    x, y, c = lax.axis_index("x"), lax.axis_index("y"), lax.axis_index("c")
    chips = [(1 - x, y), (x, 1 - y), (1 - x, 1 - y)]
    return x, y, c, chips


def _chip_id(px, py):
    return 2 * px + py


def _remote(src, dst, send_sems, recv_sems, k, to):
    return pltpu.make_async_remote_copy(src_ref=src, dst_ref=dst, send_sem=send_sems.at[k], recv_sem=recv_sems.at[k],
                                        device_id=to, device_id_type=MESH)


def _gather_weights(slots, wholes, name):
    ns, nw = len(slots), len(wholes)
    n = ns + nw

    def body(*refs):
        ins = refs[ns:n]
        outs = refs[n:2 * n]
        send_sems, recv_sems, local_sems = refs[2 * n:]
        x, y, c, chips = _place()
        me = _chip_id(x, y)
        sib = (x, y, 1 - c)
        local = [pltpu.make_async_copy(ins[b], outs[ns + b].at[me], local_sems.at[b]) for b in range(nw)]
        for cp in local:
            cp.start()
        sent = []
        for a in range(n):
            R = outs[a].shape[1]
            rows = pl.ds(c * (R // 2), R // 2) if a < ns else pl.ds(0, R)
            src = outs[a].at[me, rows] if a < ns else ins[a - ns]
            for j, chip in enumerate(chips):
                cp = _remote(src, outs[a].at[me, rows], send_sems, recv_sems, 6 * a + j, (*chip, c))
                cp.start()
                sent.append(cp)
        for a in range(n):
            R = outs[a].shape[1]
            rows = pl.ds(c * (R // 2), R // 2) if a < ns else pl.ds(0, R)
            for j, chip in enumerate(chips):
                landed = outs[a].at[_chip_id(*chip), rows]
                _remote(landed, landed, send_sems, recv_sems, 6 * a + j, (*chip, c)).wait_recv()
                if a < ns:
                    cp = _remote(landed, landed, send_sems, recv_sems, 6 * a + 3 + j, sib)
                    cp.start()
                    sent.append(cp)
        for a in range(ns):
            R = outs[a].shape[1]
            other = pl.ds((1 - c) * (R // 2), R // 2)
            for j, chip in enumerate(chips):
                passed = outs[a].at[_chip_id(*chip), other]
                _remote(passed, passed, send_sems, recv_sems, 6 * a + 3 + j, sib).wait_recv()
        for cp in sent:
            cp.wait_send()
        for cp in local:
            cp.wait()

    return pl.pallas_call(
        body, name=name, in_specs=[ANY] * n, out_specs=[ANY] * n,
        out_shape=[_sds(s.shape, s.dtype) for s in slots] + [_sds((N_CHIPS, *s.shape), s.dtype) for s in wholes],
        input_output_aliases={a: a for a in range(ns)},
        scratch_shapes=[pltpu.SemaphoreType.DMA((6 * n,)), pltpu.SemaphoreType.DMA((6 * n,)),
                        pltpu.SemaphoreType.DMA((max(nw, 1),))])(*slots, *wholes)


def _gather_ici_plan(slots, wholes):
    ns, nw = len(slots), len(wholes)

    def copies(ins, ios, outs, send_sems, recv_sems, local_sems):
        x, y, c, chips = _place()
        me = _chip_id(x, y)
        sends, recvs = [], []
        for a in range(ns + nw):
            dst = ios[a] if a < ns else outs[a - ns]
            R = dst.shape[1]
            rows = pl.ds(c * (R // 2), R // 2) if a < ns else pl.ds(0, R)
            src = dst.at[me, rows] if a < ns else ins[a - ns]
            for j, chip in enumerate(chips):
                sends.append(_remote(src, dst.at[me, rows], send_sems, recv_sems, 3 * a + j, (*chip, c)))
                landed = dst.at[_chip_id(*chip), rows]
                recvs.append(_remote(landed, landed, send_sems, recv_sems, 3 * a + j, (*chip, c)))
        local = [pltpu.make_async_copy(ins[b], outs[b].at[me], local_sems.at[b]) for b in range(nw)]
        return sends, recvs, local

    return _Plan(copies, 3 * (ns + nw), ins=wholes, inouts=slots,
                 outs=[_sds((N_CHIPS, *s.shape), s.dtype) for s in wholes])


def _gather_pass_plan(slots):
    def copies(ins, ios, outs, send_sems, recv_sems, local_sems):
        x, y, c, chips = _place()
        sib = (x, y, 1 - c)
        sends, recvs = [], []
        for a, buf in enumerate(ios):
            half = buf.shape[1] // 2
            for j, chip in enumerate(chips):
                mine = buf.at[_chip_id(*chip), pl.ds(c * half, half)]
                other = buf.at[_chip_id(*chip), pl.ds((1 - c) * half, half)]
                sends.append(_remote(mine, mine, send_sems, recv_sems, 3 * a + j, sib))
                recvs.append(_remote(other, other, send_sems, recv_sems, 3 * a + j, sib))
        return sends, recvs, []

    return _Plan(copies, 3 * len(slots), inouts=slots)


def _pair_plan(grads):
    def copies(ins, ios, outs, send_sems, recv_sems, local_sems):
        x, y, c, _ = _place()
        sib = (x, y, 1 - c)
        sends, recvs = [], []
        for a, g in enumerate(ins):
            half = g.shape[1] // 2
            sends.append(_remote(g.at[:, pl.ds((1 - c) * half, half), :], outs[a], send_sems, recv_sems, a, sib))
            recvs.append(_remote(outs[a], outs[a], send_sems, recv_sems, a, sib))
        return sends, recvs, []

    return _Plan(copies, len(grads), ins=grads,
                 outs=[_sds((g.shape[0], g.shape[1] // 2, g.shape[2]), g.dtype) for g in grads])


def _chip_plan(parts):
    def copies(ins, ios, outs, send_sems, recv_sems, local_sems):
        x, y, c, chips = _place()
        me = _chip_id(x, y)
        sends, recvs = [], []
        for a, part in enumerate(ins):
            for j, chip in enumerate(chips):
                sends.append(_remote(part.at[_chip_id(*chip)], outs[a].at[me], send_sems, recv_sems, 3 * a + j, (*chip, c)))
                landed = outs[a].at[_chip_id(*chip)]
                recvs.append(_remote(landed, landed, send_sems, recv_sems, 3 * a + j, (*chip, c)))
        return sends, recvs, []

    return _Plan(copies, 3 * len(parts), ins=parts, outs=[_sds(p.shape, p.dtype) for p in parts])


def _pair_concat(fulls, pieces, name):
    n = len(fulls)
    spans = [(a, k) for a in range(n) for k in range(pieces[a])]

    def body(*refs):
        outs = refs[n:2 * n]
        send_sems, recv_sems = refs[2 * n:]
        x, y, c, _ = _place()
        cps = []
        for s, (a, k) in enumerate(spans):
            H = outs[a].shape[0] // (2 * pieces[a])
            mine = outs[a].at[pl.ds((2 * k + c) * H, H)]
            cp = _remote(mine, mine, send_sems, recv_sems, s, (x, y, 1 - c))
            cp.start()
            cps.append(cp)
        for s, (a, k) in enumerate(spans):
            H = outs[a].shape[0] // (2 * pieces[a])
            other = outs[a].at[pl.ds((2 * k + 1 - c) * H, H)]
            _remote(other, other, send_sems, recv_sems, s, (x, y, 1 - c)).wait_recv()
            cps[s].wait_send()

    return pl.pallas_call(
        body, name=name, in_specs=[ANY] * n, out_specs=[ANY] * n,
        out_shape=[_sds(f.shape, f.dtype) for f in fulls], input_output_aliases={a: a for a in range(n)},
        scratch_shapes=[pltpu.SemaphoreType.DMA((len(spans),)), pltpu.SemaphoreType.DMA((len(spans),))])(*fulls)


def _all_sum(pack, name):
    R, C = pack.shape

    def body(p_ref, o_ref, buf, send_sems, recv_sems):
        x, y, c, _ = _place()
        me = 4 * x + 2 * y + c
        buf[me] = p_ref[...]
        cps = []
        for k in range(1, N_DEV):
            to = (x ^ (k >> 2), y ^ ((k >> 1) & 1), c ^ (k & 1))
            cp = _remote(p_ref, buf.at[me], send_sems, recv_sems, k - 1, to)
            cp.start()
            cps.append(cp)
        for k in range(1, N_DEV):
            frm = (x ^ (k >> 2), y ^ ((k >> 1) & 1), c ^ (k & 1))
            slot = buf.at[4 * frm[0] + 2 * frm[1] + frm[2]]
            _remote(slot, slot, send_sems, recv_sems, k - 1, frm).wait_recv()
        acc = buf[0]
        for k in range(1, N_DEV):
            acc = acc + buf[k]
        o_ref[...] = acc
        for cp in cps:
            cp.wait_send()

    vm = pl.BlockSpec(memory_space=pltpu.VMEM)
    return pl.pallas_call(
        body, name=name, in_specs=[vm], out_specs=vm, out_shape=_sds((R, C), F32),
        scratch_shapes=[pltpu.VMEM((N_DEV, R, C), F32), pltpu.SemaphoreType.DMA((N_DEV - 1,)),
                        pltpu.SemaphoreType.DMA((N_DEV - 1,))])(pack)


def _local_step(xs, tgt, p, ex):
    h1 = _norm_fwd(xs, p["pre_mix_norm"], "pre_mix_norm")
    proj, got = _mm_nn_blk(h1, ex.weight("w_in"), "proj_in", plans=ex.carry("proj_in"))
    ex.done("proj_in", got)
    biases = _relbias_fwd(p["rel_bias"], "rel_bias_fwd")
    fw0, got = _attn_fwd(proj, biases[0], 0, "attn_fwd0", plans=ex.carry("attn_fwd0"))
    ex.done("attn_fwd0", got)
    fw = [fw0] + [_attn_fwd(proj, biases[g], g, f"attn_fwd{g}") for g in range(1, N_GROUPS)]
    y, lse = _attn_merge([t[0] for t in fw], [t[1] for t in fw], "attn_merge")
    yh, o_h, states = _hgrn_fwd(proj, p["hgrn_lb_raw"], p["hgrn_norm"], "hgrn_fwd")
    W_a, W_h, W_out = ex.weight("w_branch_attn"), ex.weight("w_branch_hgrn"), ex.weight("w_out")
    W_up, W_down, conv_w = ex.weight("w_up"), ex.weight("w_down"), ex.weight("conv_w")
    za = _mm_nn_blk(y, W_a, "branch_attn")
    zh = _mm_nn_blk(yh, W_h, "branch_hgrn")
    merged = _merge_fwd(za, zh, proj, "merge_fwd")
    mo = _mm_nn(merged, W_out, "mix_out")
    x1, h2 = _mix_out(mo, xs, p["post_mix_norm"], p["pre_ffn_norm"], "mix_residual")
    u = _mm_nn_blk(h2, W_up, "ffn_up")
    a = _conv_gelu_fwd(u, conv_w, p["conv_b"], "conv_gelu_fwd")
    ffo = _mm_nn(a, W_down, "ffn_down")
    dx2, dff, g_post_ffn, loss = _loss_head(ffo, x1, tgt, p["post_ffn_norm"], "loss_head")

    da = _mm_nt(dff, W_down, "d_ffn_act")
    ex.grad("w_down", _mm_tn(a, dff, "g_w_down").reshape(N_CHIPS, D_FF // N_CHIPS, D_MODEL))
    (dcg, dcv, gwg, gwv, gbg, gbv), got = _conv_gelu_bwd(u, da, conv_w, p["conv_b"], "conv_gelu_bwd",
                                                          plans=ex.carry("conv_gelu_bwd"))
    ex.done("conv_gelu_bwd", got)
    g_conv_w = jnp.concatenate([gwg, gwv], axis=1)
    g_conv_b = jnp.concatenate([gbg, gbv], axis=1)
    du_parts, got = _conv_input_bwd(dcg, dcv, conv_w, "conv_input_bwd", plans=ex.carry("conv_input_bwd"))
    ex.done("conv_input_bwd", got)
    du = jnp.concatenate(du_parts, axis=1)
    dh2 = _mm_nt_blk(du, W_up, "d_ffn_in")
    ex.grad("w_up", _mm_tn_blk(h2, du, N_CHIPS, "g_w_up"))
    (dx1, g_pre_ffn), got = _prenorm_bwd(dh2, x1, p["pre_ffn_norm"], dx2, "pre_ffn_norm_bwd",
                                         plans=ex.carry("pre_ffn_norm_bwd"))
    ex.done("pre_ffn_norm_bwd", got)
    dmo, g_post_mix = _postnorm_bwd(dx1, mo, p["post_mix_norm"], "post_mix_norm_bwd")
    dmerged = _mm_nt(dmo, W_out, "d_merged")
    ex.grad("w_out", _mm_tn(merged, dmo, "g_w_out").reshape(N_CHIPS, D_MODEL // N_CHIPS, D_MODEL))
    dza, dzh, dg0, dg1 = _merge_bwd(dmerged, za, zh, proj, "merge_bwd")
    dy = _mm_nt_blk(dza, W_a, "d_attn_out")
    ex.grad("w_branch_attn", _mm_tn_blk(y, dza, N_CHIPS, "g_w_branch_attn"))
    dyh = _mm_nt_blk(dzh, W_h, "d_hgrn_out")
    ex.grad("w_branch_hgrn", _mm_tn_blk(yh, dzh, N_CHIPS, "g_w_branch_hgrn"))
    dqkv, dbs = [], []
    for g in range(N_GROUPS):
        parts, db, got = _attn_bwd(proj, biases[g], lse, y, dy, g, f"attn_bwd{g}", plans=ex.carry(f"attn_bwd{g}"))
        ex.done(f"attn_bwd{g}", got)
        dqkv += parts
        dbs.append(db)
    g_rel_bias = _relbias_bwd(dbs, "rel_bias_bwd")
    dhg, g_lb_raw, g_hgrn_norm = _hgrn_bwd(proj, p["hgrn_lb_raw"], p["hgrn_norm"], o_h, states, dyh, "hgrn_bwd")
    dproj = jnp.concatenate([*[_bf(t) for t in dqkv], *dhg, dg0, dg1], axis=1)
    for k, piece in enumerate(PIECES["w_in"]):
        g, got = _mm_tn_blk(h1, dproj, N_CHIPS, f"g_{piece}", x_cols=(k, D_MODEL // 2), plans=ex.carry(f"g_{piece}"))
        ex.done(f"g_{piece}", got)
        ex.grad(piece, g)
    dh1, got = _mm_nt_blk(dproj, ex.weight("w_in"), "d_proj_in", plans=ex.carry("d_proj_in"))
    ex.done("d_proj_in", got)
    (grad_x, g_pre_mix), got = _prenorm_bwd(dh1, xs, p["pre_mix_norm"], dx1, "pre_mix_norm_bwd",
                                            plans=ex.carry("pre_mix_norm_bwd"))
    ex.done("pre_mix_norm_bwd", got)
    small = dict(pre_mix_norm=g_pre_mix, rel_bias=g_rel_bias, hgrn_lb_raw=g_lb_raw, hgrn_norm=g_hgrn_norm,
                 post_mix_norm=g_post_mix, pre_ffn_norm=g_pre_ffn, conv_w=g_conv_w, conv_b=g_conv_b,
                 post_ffn_norm=g_post_ffn)
    return loss, grad_x, small


SMALL = ("pre_mix_norm", "rel_bias", "hgrn_lb_raw", "hgrn_norm", "post_mix_norm", "pre_ffn_norm", "conv_w", "conv_b",
         "post_ffn_norm")
BIG = ("w_in", "w_up", "w_down", "w_out", "w_branch_attn", "w_branch_hgrn")
LATE = BIG[1:]
WEIGHTS = ("pre_mix_norm", "w_in", "rel_bias", "hgrn_lb_raw", "hgrn_norm", "w_branch_attn", "w_branch_hgrn", "w_out",
           "post_mix_norm", "pre_ffn_norm", "w_up", "conv_w", "conv_b", "w_down", "post_ffn_norm")

SCHEDULE = {
    "proj_in": [("gather_ici", LATE)],
    "attn_fwd0": [("gather_pass", LATE)],
    "conv_gelu_bwd": [("pair", ("w_down",))],
    "conv_input_bwd": [("chip", ("w_down",))],
    "pre_ffn_norm_bwd": [("pair", ("w_up",))],
    "attn_bwd0": [("chip", ("w_up",)), ("pair", ("w_out", "w_branch_attn", "w_branch_hgrn"))],
    "attn_bwd1": [("chip", ("w_out", "w_branch_attn", "w_branch_hgrn"))],
    "g_w_in_b": [("pair", ("w_in_a",))],
    "d_proj_in": [("chip", ("w_in_a",)), ("pair", ("w_in_b",))],
    "pre_mix_norm_bwd": [("chip", ("w_in_b",))],
}
PIECES = {n: (n,) for n in BIG}
PIECES["w_in"] = ("w_in_a", "w_in_b")


class _Exchange:
    def __init__(self, place, slots, conv_w_shard):
        self.place, self.slots, self.conv_w_shard = place, dict(slots), conv_w_shard
        self.conv_w = None
        self.g, self.from_sibling, self.pair_sums, self.arrived = {}, {}, {}, {}
        self.pending = []

    def weight(self, name):
        if name == "conv_w":
            return self.conv_w
        w = self.slots[name]
        return w.reshape(-1, D_MODEL) if name in ("w_out", "w_down") else w

    def grad(self, name, g):
        self.g[name] = g

    def carry(self, point):
        plans = []
        self.pending = SCHEDULE.get(point, [])
        for kind, names in self.pending:
            if kind == "gather_ici":
                plans.append(_gather_ici_plan([self.slots[n] for n in names], [self.conv_w_shard]))
            elif kind == "gather_pass":
                plans.append(_gather_pass_plan([self.slots[n] for n in names]))
            elif kind == "pair":
                plans.append(_pair_plan([self.g[n] for n in names]))
            else:
                for n in names:
                    self.pair_sums[n] = _pair_sum(self.g[n], self.from_sibling[n], self.place[1:2], f"pair_sum_{n}")
                plans.append(_chip_plan([self.pair_sums[n] for n in names]))
        return plans

    def done(self, point, carried):
        for (kind, names), got in zip(self.pending, carried):
            if kind in ("gather_ici", "gather_pass"):
                self.slots.update(zip(names, got))
                if kind == "gather_ici":
                    self.conv_w = got[len(names)].transpose(1, 0, 2).reshape(3, 2 * D_FF)
            elif kind == "pair":
                self.from_sibling.update(zip(names, got))
            else:
                self.arrived.update(zip(names, got))

    def reduced(self):
        shards = []
        for n in BIG:
            full = None
            for k, piece in enumerate(PIECES[n]):
                full = _chip_sum(self.arrived[piece], self.pair_sums[piece], self.place, f"chip_sum_{piece}",
                                 piece=k, n_pieces=len(PIECES[n]), into=full)
            shards.append(full)
        return dict(zip(BIG, _pair_concat(shards, [len(PIECES[n]) for n in BIG], "pair_concat")))


def kernel(x, pre_mix_norm, w_in, rel_bias, hgrn_lb_raw, hgrn_norm, w_branch_attn, w_branch_hgrn, w_out, post_mix_norm, pre_ffn_norm, w_up, conv_w, conv_b, w_down, post_ffn_norm, loss_target, m_pre_mix_norm, m_w_in, m_rel_bias, m_hgrn_lb_raw, m_hgrn_norm, m_w_branch_attn, m_w_branch_hgrn, m_w_out, m_post_mix_norm, m_pre_ffn_norm, m_w_up, m_conv_w, m_conv_b, m_w_down, m_post_ffn_norm, v_pre_mix_norm, v_w_in, v_rel_bias, v_hgrn_lb_raw, v_hgrn_norm, v_w_branch_attn, v_w_branch_hgrn, v_w_out, v_post_mix_norm, v_pre_ffn_norm, v_w_up, v_conv_w, v_conv_b, v_w_down, v_post_ffn_norm):
    w = dict(pre_mix_norm=pre_mix_norm, w_in=w_in, rel_bias=rel_bias, hgrn_lb_raw=hgrn_lb_raw, hgrn_norm=hgrn_norm,
             w_branch_attn=w_branch_attn, w_branch_hgrn=w_branch_hgrn, w_out=w_out, post_mix_norm=post_mix_norm,
             pre_ffn_norm=pre_ffn_norm, w_up=w_up, conv_w=conv_w, conv_b=conv_b, w_down=w_down,
             post_ffn_norm=post_ffn_norm)
    m = dict(pre_mix_norm=m_pre_mix_norm, w_in=m_w_in, rel_bias=m_rel_bias, hgrn_lb_raw=m_hgrn_lb_raw,
             hgrn_norm=m_hgrn_norm, w_branch_attn=m_w_branch_attn, w_branch_hgrn=m_w_branch_hgrn, w_out=m_w_out,
             post_mix_norm=m_post_mix_norm, pre_ffn_norm=m_pre_ffn_norm, w_up=m_w_up, conv_w=m_conv_w,
             conv_b=m_conv_b, w_down=m_w_down, post_ffn_norm=m_post_ffn_norm)
    v = dict(pre_mix_norm=v_pre_mix_norm, w_in=v_w_in, rel_bias=v_rel_bias, hgrn_lb_raw=v_hgrn_lb_raw,
             hgrn_norm=v_hgrn_norm, w_branch_attn=v_w_branch_attn, w_branch_hgrn=v_w_branch_hgrn, w_out=v_w_out,
             post_mix_norm=v_post_mix_norm, pre_ffn_norm=v_pre_ffn_norm, w_up=v_w_up, conv_w=v_conv_w,
             conv_b=v_conv_b, w_down=v_w_down, post_ffn_norm=v_post_ffn_norm)
    shard2d = {n: (w[n][0] if w[n].ndim == 3 else w[n]) for n in WEIGHTS}
    chip = 2 * lax.axis_index("x") + lax.axis_index("y")
    core = lax.axis_index("c")

    place = jnp.stack([chip, core]).astype(jnp.int32)
    slots = {n: _cast_into_slot(shard2d[n], place, f"cast_{n}") for n in BIG}
    slots["w_in"] = _gather_weights([slots["w_in"]], [], "gather_w_in")[0]
    ex = _Exchange(place, slots, shard2d["conv_w"])
    loss, grad_x, small = _local_step(x[0], loss_target[0], {n: w[n] for n in SMALL if n != "conv_w"}, ex)

    flat = [small[n].reshape(-1) for n in SMALL] + [loss.reshape(-1)]
    sizes = [t.shape[0] for t in flat]
    summed = _all_sum(jnp.concatenate(flat).reshape(-1, LANES), "sum_small").reshape(-1)
    offs = [sum(sizes[:i]) for i in range(len(sizes))]
    grads = {}
    for n, o, sz in zip(SMALL, offs, sizes):
        grads[n] = summed[o:o + sz].reshape(small[n].shape)
    loss_total = summed[offs[-1]]
    cw = 2 * D_FF // N_CHIPS
    grads["conv_w"] = lax.dynamic_slice(grads["conv_w"], (0, chip * cw), (3, cw))

    grads.update(ex.reduced())

    out_g, out_d, out_m, out_v = [], [], [], []
    for n in WEIGHTS:
        d2, m2, v2 = _adamw(shard2d[n], grads[n], m[n].reshape(shard2d[n].shape), v[n].reshape(shard2d[n].shape),
                            f"adamw_{n}")
        shape = w[n].shape
        out_g.append(grads[n].reshape(shape))
        out_d.append(d2.reshape(shape))
        out_m.append(m2.reshape(shape))
        out_v.append(v2.reshape(shape))
    return (loss_total, grad_x[None], *out_g, *out_d, *out_m, *out_v)
```

```python
import functools
import math

import jax
import jax.numpy as jnp
from jax import lax
from jax.experimental import pallas as pl
from jax.experimental.pallas import tpu as pltpu

F32 = jnp.float32
BF16 = jnp.bfloat16
MESH = pl.DeviceIdType.MESH

D_MODEL = 1024
N_GROUPS = 3
DILATIONS = (1, 4, 16)
HEADS = 8
HEAD_DIM = 64
GROUP_W = HEADS * HEAD_DIM
QKV_W = N_GROUPS * 3 * GROUP_W
BLK = 128
NEG_INF = -1e30
NUM_BUCKETS = 32
MAX_EXACT = 16
MAX_DISTANCE = 2048
HG_HEADS = 4
HG_DK = 128
HG_W = HG_HEADS * HG_DK
HG_CHUNK = 32
HG_TILE = 256
IN_W = QKV_W + 4 * HG_W + 2 * D_MODEL
D_FF = 2816
EPS = 1e-6
N_CHIPS = 4
N_DEV = 8
LANES = 128

ADAM_LR, ADAM_B1, ADAM_B2, ADAM_EPS, ADAM_WD, ADAM_STEP = 0.001, 0.9, 0.999, 1e-08, 0.01, 10

VMEM_LIMIT = 56 * 1024 * 1024


def _cp(n_axes):
    return pltpu.CompilerParams(dimension_semantics=("arbitrary",) * n_axes, vmem_limit_bytes=VMEM_LIMIT)


def _sds(shape, dtype):
    return jax.ShapeDtypeStruct(tuple(shape), dtype)


def _sigmoid(v):
    return 1.0 / (1.0 + jnp.exp(-v))


def _bf(v):
    return v.astype(BF16)


def _dot(a, b, dims):
    return lax.dot_general(a, b, (dims, ((), ())), preferred_element_type=F32)


NN = ((1,), (0,))
NT = ((1,), (1,))
TN = ((0,), (0,))

ANY = pl.BlockSpec(memory_space=pl.ANY)


class _Plan:
    def __init__(self, copies, n_sems, ins=(), inouts=(), outs=()):
        self.copies, self.n_sems = copies, n_sems
        self.ins, self.inouts, self.outs = list(ins), list(inouts), list(outs)


def _call(body, plans=None, *, name, grid, in_specs, out_specs, out_shape, args, scratch_shapes=()):
    plans = list(plans or ())
    in_specs, out_specs, out_shape = list(in_specs), list(out_specs), list(out_shape)
    scratch_shapes = list(scratch_shapes)
    n_in, n_out, n_scr = len(in_specs), len(out_specs), len(scratch_shapes)
    x_in, x_out, aliases, spans = [], [], {}, []
    for p in plans:
        i0, o0 = len(x_in), len(x_out)
        x_in += p.ins
        for a in p.inouts:
            aliases[n_in + len(x_in)] = n_out + len(x_out)
            x_in.append(a)
            x_out.append(_sds(a.shape, a.dtype))
        x_out += p.outs
        spans.append((i0, len(p.ins), o0, len(p.inouts), len(p.outs)))
    sems = [pltpu.SemaphoreType.DMA((p.n_sems,)) for p in plans for _ in range(3)]

    def wrapped(*refs):
        xi = refs[n_in:n_in + len(x_in)]
        base = n_in + len(x_in)
        xo = refs[base + n_out:base + n_out + len(x_out)]
        sbase = base + n_out + len(x_out)
        xs = refs[sbase + n_scr:]
        ids = [pl.program_id(k) for k in range(len(grid))]
        first = functools.reduce(jnp.logical_and, [i == 0 for i in ids])
        last = functools.reduce(jnp.logical_and, [i == g - 1 for i, g in zip(ids, grid)])

        def descriptors(k):
            i0, ni, o0, nio, no = spans[k]
            return plans[k].copies(xi[i0:i0 + ni], xo[o0:o0 + nio], xo[o0 + nio:o0 + nio + no], *xs[3 * k:3 * k + 3])

        @pl.when(first)
        def _():
            for k in range(len(plans)):
                sends, _, local = descriptors(k)
                for cp in (*sends, *local):
                    cp.start()

        body(*refs[:n_in], *refs[base:base + n_out], *refs[sbase:sbase + n_scr])

        @pl.when(last)
        def _():
            for k in range(len(plans)):
                sends, recvs, local = descriptors(k)
                for cp in recvs:
                    cp.wait_recv()
                for cp in sends:
                    cp.wait_send()
                for cp in local:
                    cp.wait()

    res = pl.pallas_call(
        wrapped if plans else body, name=name, grid=grid, in_specs=in_specs + [ANY] * len(x_in),
        out_specs=out_specs + [ANY] * len(x_out), out_shape=out_shape + x_out, input_output_aliases=aliases,
        scratch_shapes=scratch_shapes + sems, compiler_params=_cp(len(grid)))(*args, *x_in)
    res = list(res)
    carried = [res[n_out + o0:n_out + o0 + nio + no] for (_, _, o0, nio, no) in spans]
    return res[:n_out], carried


def _mm_nn_blk(a, wg, name, tm=512, plans=None):
    M, K = a.shape
    nb, _, Nb = wg.shape

    def body(a_ref, w_ref, o_ref):
        o_ref[...] = _dot(_bf(a_ref[...]), w_ref[...], NN)

    (out,), carried = _call(
        body, plans, name=name, grid=(nb, M // tm),
        in_specs=[pl.BlockSpec((tm, K), lambda j, i: (i, 0)), pl.BlockSpec((None, K, Nb), lambda j, i: (j, 0, 0))],
        out_specs=[pl.BlockSpec((tm, Nb), lambda j, i: (i, j))],
        out_shape=[_sds((M, nb * Nb), F32)], args=(a, wg))
    return out if plans is None else (out, carried)


def _mm_nt_blk(dy, wg, name, tm=1024, plans=None):
    M = dy.shape[0]
    nb, K, Nb = wg.shape

    def body(dy_ref, w_ref, o_ref):
        j = pl.program_id(1)
        r = _dot(_bf(dy_ref[...]), w_ref[...], NT)

        @pl.when(j == 0)
        def _():
            o_ref[...] = r

        @pl.when(j > 0)
        def _():
            o_ref[...] += r

    (out,), carried = _call(
        body, plans, name=name, grid=(M // tm, nb),
        in_specs=[pl.BlockSpec((tm, Nb), lambda i, j: (i, j)), pl.BlockSpec((None, K, Nb), lambda i, j: (j, 0, 0))],
        out_specs=[pl.BlockSpec((tm, K), lambda i, j: (i, 0))],
        out_shape=[_sds((M, K), F32)], args=(dy, wg))
    return out if plans is None else (out, carried)


def _mm_tn_blk(x, dy, nb, name, tk=512, x_cols=None, plans=None):
    T, Mx = x.shape
    xk, Mx = (0, Mx) if x_cols is None else x_cols
    Nb = dy.shape[1] // nb

    def body(x_ref, dy_ref, o_ref):
        t = pl.program_id(1)
        r = _dot(_bf(x_ref[...]), _bf(dy_ref[...]), TN)

        @pl.when(t == 0)
        def _():
            o_ref[...] = r

        @pl.when(t > 0)
        def _():
            o_ref[...] += r

    (out,), carried = _call(
        body, plans, name=name, grid=(nb, T // tk),
        in_specs=[pl.BlockSpec((tk, Mx), lambda j, t: (t, xk)), pl.BlockSpec((tk, Nb), lambda j, t: (t, j))],
        out_specs=[pl.BlockSpec((None, Mx, Nb), lambda j, t: (j, 0, 0))],
        out_shape=[_sds((nb, Mx, Nb), F32)], args=(x, dy))
    return out if plans is None else (out, carried)


def _mm_nn(a, w, name, tm=512):
    M, K = a.shape
    N = w.shape[1]

    def body(a_ref, w_ref, o_ref):
        o_ref[...] = _dot(_bf(a_ref[...]), w_ref[...], NN)

    return pl.pallas_call(
        body, name=name, grid=(M // tm,),
        in_specs=[pl.BlockSpec((tm, K), lambda i: (i, 0)), pl.BlockSpec((K, N), lambda i: (0, 0))],
        out_specs=pl.BlockSpec((tm, N), lambda i: (i, 0)),
        out_shape=_sds((M, N), F32), compiler_params=_cp(1))(a, w)


def _mm_nt(dy, w, name, tm=512):
    M, N = dy.shape
    K = w.shape[0]

    def body(dy_ref, w_ref, o_ref):
        o_ref[...] = _dot(_bf(dy_ref[...]), w_ref[...], NT)

    return pl.pallas_call(
        body, name=name, grid=(M // tm,),
        in_specs=[pl.BlockSpec((tm, N), lambda i: (i, 0)), pl.BlockSpec((K, N), lambda i: (0, 0))],
        out_specs=pl.BlockSpec((tm, K), lambda i: (i, 0)),
        out_shape=_sds((M, K), F32), compiler_params=_cp(1))(dy, w)


def _mm_tn(x, dy, name, tk=512):
    T, Mx = x.shape
    N = dy.shape[1]

    def body(x_ref, dy_ref, o_ref):
        t = pl.program_id(0)
        r = _dot(_bf(x_ref[...]), _bf(dy_ref[...]), TN)

        @pl.when(t == 0)
        def _():
            o_ref[...] = r

        @pl.when(t > 0)
        def _():
            o_ref[...] += r

    return pl.pallas_call(
        body, name=name, grid=(T // tk,),
        in_specs=[pl.BlockSpec((tk, Mx), lambda t: (t, 0)), pl.BlockSpec((tk, N), lambda t: (t, 0))],
        out_specs=pl.BlockSpec((Mx, N), lambda t: (0, 0)),
        out_shape=_sds((Mx, N), F32), compiler_params=_cp(1))(x, dy)


def _tile(arr, bw, col=lambda c: 0):
    return ("tile", arr, bw, col)


def _full(arr):
    return ("full", arr)


def _out_tile(width, dtype, bw, col=lambda c: 0):
    return ("tile", width, dtype, bw, col)


def _out_acc(rows, width, bw, col=lambda c: 0):
    return ("acc", rows, width, bw, col)


def _rows_call(name, body, n_rows, tm, ncol, ins, outs, plans=None):
    in_specs, args = [], []
    for e in ins:
        if e[0] == "tile":
            _, arr, bw, col = e
            in_specs.append(pl.BlockSpec((tm, bw), functools.partial(lambda c, i, col: (i, col(c)), col=col)))
        else:
            arr = e[1]
            in_specs.append(pl.BlockSpec(arr.shape, functools.partial(lambda c, i, nd: (0,) * nd, nd=arr.ndim)))
        args.append(arr)
    out_specs, out_shape = [], []
    for e in outs:
        if e[0] == "tile":
            _, width, dtype, bw, col = e
            out_specs.append(pl.BlockSpec((tm, bw), functools.partial(lambda c, i, col: (i, col(c)), col=col)))
            out_shape.append(_sds((n_rows, width), dtype))
        else:
            _, rows, width, bw, col = e
            out_specs.append(pl.BlockSpec((rows, bw), functools.partial(lambda c, i, col: (0, col(c)), col=col)))
            out_shape.append(_sds((rows, width), F32))
    out, carried = _call(body, plans, name=name, grid=(ncol, n_rows // tm), in_specs=in_specs, out_specs=out_specs,
                         out_shape=out_shape, args=args)
    return out if plans is None else (out, carried)


def _acc(ref, val):
    i = pl.program_id(1)

    @pl.when(i == 0)
    def _():
        ref[...] = val

    @pl.when(i > 0)
    def _():
        ref[...] += val


def _rinv(z):
    return lax.rsqrt(jnp.mean(z * z, axis=-1, keepdims=True) + EPS)


def _norm_bwd(dy, zhat, r, w):
    dyw = dy * w
    return r * (dyw - zhat * jnp.mean(dyw * zhat, axis=-1, keepdims=True))


def _norm_fwd(x, w, name):
    def body(x_ref, w_ref, h_ref):
        xv = x_ref[...]
        h_ref[...] = _bf(xv * _rinv(xv) * w_ref[...])

    return _rows_call(name, body, x.shape[0], 512, 1, [_tile(x, D_MODEL), _full(w)],
                      [_out_tile(D_MODEL, BF16, D_MODEL)])[0]


def _prenorm_bwd(dh, xin, w, dres, name, plans=None):
    def body(dh_ref, x_ref, w_ref, dres_ref, dx_ref, dw_ref):
        xv = x_ref[...]
        r = _rinv(xv)
        xhat = xv * r
        dhv = dh_ref[...]
        dx_ref[...] = dres_ref[...] + _norm_bwd(dhv, xhat, r, w_ref[...])
        _acc(dw_ref, jnp.sum(dhv * xhat, axis=0, keepdims=True))

    return _rows_call(name, body, xin.shape[0], 512, 1,
                      [_tile(dh, D_MODEL), _tile(xin, D_MODEL), _full(w), _tile(dres, D_MODEL)],
                      [_out_tile(D_MODEL, F32, D_MODEL), _out_acc(1, D_MODEL, D_MODEL)], plans)


def _postnorm_bwd(dout, z, w, name):
    def body(do_ref, z_ref, w_ref, dz_ref, dw_ref):
        zv = z_ref[...]
        r = _rinv(zv)
        zhat = zv * r
        dov = do_ref[...]
        dz_ref[...] = _bf(_norm_bwd(dov, zhat, r, w_ref[...]))
        _acc(dw_ref, jnp.sum(dov * zhat, axis=0, keepdims=True))

    return _rows_call(name, body, z.shape[0], 512, 1, [_tile(dout, D_MODEL), _tile(z, D_MODEL), _full(w)],
                      [_out_tile(D_MODEL, BF16, D_MODEL), _out_acc(1, D_MODEL, D_MODEL)])


def _t5_bucket(dist):
    n = jnp.maximum(dist, 0)
    nf = jnp.maximum(n, 1).astype(F32)
    large = MAX_EXACT + (jnp.log(nf / MAX_EXACT) / math.log(MAX_DISTANCE / MAX_EXACT)
                         * (NUM_BUCKETS - MAX_EXACT)).astype(jnp.int32)
    large = jnp.minimum(large, NUM_BUCKETS - 1)
    return jnp.where(n < MAX_EXACT, n, large)


def _band_rel():
    return jnp.arange(BLK)[:, None] + BLK - jnp.arange(2 * BLK)[None, :]


def _band_mask(n):
    row = lax.broadcasted_iota(jnp.int32, (BLK, 2 * BLK), 0)
    col = lax.broadcasted_iota(jnp.int32, (BLK, 2 * BLK), 1)
    rel = row + BLK - col
    return (rel >= 0) & (rel <= BLK) & ((col >= BLK) | (n > 0))


RES_UNROLL = 4


def _heads_per_step(d):
    return HEADS if d == 1 else LANES // HEAD_DIM


def _sub_rows(r, d):
    return pl.ds(r, BLK, stride=d) if d > 1 else pl.ds(0, BLK)


def _for_residues(d, fn):
    if d <= RES_UNROLL:
        for r in range(d):
            fn(r)
    else:
        def group(i, carry):
            for k in range(RES_UNROLL):
                fn(i * RES_UNROLL + k)
            return carry

        lax.fori_loop(0, d // RES_UNROLL, group, 0)


def _attn_specs(d, g, qblock):
    cw = _heads_per_step(d) * HEAD_DIM

    def col(part, hp):
        return (g * 3 + part) * (GROUP_W // cw) + hp

    def cur(part):
        return pl.BlockSpec((d * BLK, cw), lambda hp, n: (qblock(n), col(part, hp)))

    def prev(part):
        return pl.BlockSpec((d * BLK, cw), lambda hp, n: (jnp.maximum(qblock(n) - 1, 0), col(part, hp)))

    return cur, prev


def _attn_fwd(proj, bias, g, name, plans=None):
    S = proj.shape[0]
    d = DILATIONS[g]
    NB = S // (d * BLK)
    hps = _heads_per_step(d)

    def body(q_ref, kp_ref, kc_ref, vp_ref, vc_ref, b_ref, o_ref, lse_ref):
        hp = pl.program_id(0)
        mask = _band_mask(pl.program_id(1))

        def residue(r):
            rows = _sub_rows(r, d)
            q2 = q_ref[rows, :]
            k2 = jnp.concatenate([kp_ref[rows, :], kc_ref[rows, :]], axis=0)
            v2 = jnp.concatenate([vp_ref[rows, :], vc_ref[rows, :]], axis=0)
            outs, lses = [], []
            for hh in range(hps):
                hs = slice(hh * HEAD_DIM, (hh + 1) * HEAD_DIM)
                s = _dot(_bf(q2[:, hs]), _bf(k2[:, hs]), NT) * (HEAD_DIM ** -0.5) + b_ref[hp * hps + hh]
                s = jnp.where(mask, s, NEG_INF)
                m = jnp.max(s, axis=-1, keepdims=True)
                p = jnp.exp(s - m)
                l = jnp.sum(p, axis=-1, keepdims=True)
                outs.append(_dot(_bf(p), _bf(v2[:, hs]), NN) / l)
                lses.append(jnp.broadcast_to(m + jnp.log(l), (BLK, HEAD_DIM)))
            o_ref[rows, :] = jnp.concatenate(outs, axis=1)
            lse_ref[rows, :] = jnp.concatenate(lses, axis=1)

        _for_residues(d, residue)

    cur, prev = _attn_specs(d, g, lambda n: n)
    out = pl.BlockSpec((d * BLK, hps * HEAD_DIM), lambda hp, n: (n, hp))
    res, carried = _call(
        body, plans, name=name, grid=(HEADS // hps, NB),
        in_specs=[cur(0), prev(1), cur(1), prev(2), cur(2),
                  pl.BlockSpec((HEADS, BLK, 2 * BLK), lambda hp, n: (0, 0, 0))],
        out_specs=[out, out], out_shape=[_sds((S, GROUP_W), F32)] * 2,
        args=(proj, proj, proj, proj, proj, bias))
    return res if plans is None else (res, carried)


def _attn_merge(os_, lses, name):
    def body(o0, o1, o2, l0, l1, l2, y_ref, lse_ref):
        a, b, c = l0[...], l1[...], l2[...]
        m = jnp.maximum(jnp.maximum(a, b), c)
        ea, eb, ec = jnp.exp(a - m), jnp.exp(b - m), jnp.exp(c - m)
        den = ea + eb + ec
        y_ref[...] = (ea * o0[...] + eb * o1[...] + ec * o2[...]) / den
        lse_ref[...] = m + jnp.log(den)

    S = os_[0].shape[0]
    return _rows_call(name, body, S, 512, 1, [_tile(t, GROUP_W) for t in (*os_, *lses)],
                      [_out_tile(GROUP_W, F32, GROUP_W)] * 2)


def _attn_bwd(proj, bias, lse, y, dy, g, name, plans=None):
    S = proj.shape[0]
    d = DILATIONS[g]
    NB = S // (d * BLK)
    hps = _heads_per_step(d)

    def body(q_ref, kp_ref, kc_ref, vp_ref, vc_ref, b_ref, l_ref, y_ref, dy_ref,
             dq_ref, dk_ref, dv_ref, db_ref, ck_ref, cv_ref):
        hp, n = pl.program_id(0), pl.program_id(1)

        @pl.when((hp == 0) & (n == 0))
        def _():
            db_ref[...] = jnp.zeros_like(db_ref)

        @pl.when(n == 0)
        def _():
            ck_ref[...] = jnp.zeros_like(ck_ref)
            cv_ref[...] = jnp.zeros_like(cv_ref)

        @pl.when(n < NB)
        def _():
            mask = _band_mask(n)

            def residue(r):
                rows = _sub_rows(r, d)
                q2 = q_ref[rows, :]
                k2 = jnp.concatenate([kp_ref[rows, :], kc_ref[rows, :]], axis=0)
                v2 = jnp.concatenate([vp_ref[rows, :], vc_ref[rows, :]], axis=0)
                l2, y2, dy2 = l_ref[rows, :], y_ref[rows, :], dy_ref[rows, :]
                dqs, dks, dvs = [], [], []
                for hh in range(hps):
                    hs = slice(hh * HEAD_DIM, (hh + 1) * HEAD_DIM)
                    q, kb, vb = _bf(q2[:, hs]), _bf(k2[:, hs]), _bf(v2[:, hs])
                    s = _dot(q, kb, NT) * (HEAD_DIM ** -0.5) + b_ref[hp * hps + hh]
                    s = jnp.where(mask, s, NEG_INF)
                    p = jnp.exp(s - l2[:, hh * HEAD_DIM:hh * HEAD_DIM + 1])
                    dyh = dy2[:, hs]
                    delta = jnp.sum(dyh * y2[:, hs], axis=-1, keepdims=True)
                    dyb = _bf(dyh)
                    ds = p * (_dot(dyb, vb, NT) - delta)
                    db_ref[hp * hps + hh] += ds
                    dsb = _bf(ds * (HEAD_DIM ** -0.5))
                    dqs.append(_dot(dsb, kb, NN))
                    dks.append(_dot(dsb, q, TN))
                    dvs.append(_dot(_bf(p), dyb, TN))
                dkb = jnp.concatenate(dks, axis=1)
                dvb = jnp.concatenate(dvs, axis=1)
                dq_ref[rows, :] = jnp.concatenate(dqs, axis=1)
                dk_ref[rows, :] = ck_ref[rows, :] + dkb[:BLK]
                dv_ref[rows, :] = cv_ref[rows, :] + dvb[:BLK]
                ck_ref[rows, :] = dkb[BLK:]
                cv_ref[rows, :] = dvb[BLK:]

            _for_residues(d, residue)

        @pl.when(n == NB)
        def _():
            dk_ref[...] = ck_ref[...]
            dv_ref[...] = cv_ref[...]

    def qn(n):
        return jnp.minimum(n, NB - 1)

    cur, prev = _attn_specs(d, g, qn)
    cw = hps * HEAD_DIM
    row = pl.BlockSpec((d * BLK, cw), lambda hp, n: (qn(n), hp))
    done = pl.BlockSpec((d * BLK, cw), lambda hp, n: (jnp.maximum(n - 1, 0), hp))
    (dq, dk, dv, db), carried = _call(
        body, plans, name=name, grid=(HEADS // hps, NB + 1),
        in_specs=[cur(0), prev(1), cur(1), prev(2), cur(2),
                  pl.BlockSpec((HEADS, BLK, 2 * BLK), lambda hp, n: (0, 0, 0)), row, row, row],
        out_specs=[row, done, done, pl.BlockSpec((HEADS, BLK, 2 * BLK), lambda hp, n: (0, 0, 0))],
        out_shape=[_sds((S, GROUP_W), F32)] * 3 + [_sds((HEADS, BLK, 2 * BLK), F32)],
        scratch_shapes=[pltpu.VMEM((d * BLK, cw), F32)] * 2,
        args=(proj, proj, proj, proj, proj, bias, lse, y, dy))
    return ([dq, dk, dv], db) if plans is None else ([dq, dk, dv], db, carried)


BAND = BLK * 2 * BLK


def _bucket_onehot():
    buckets = jnp.stack([_t5_bucket(_band_rel() * d) for d in DILATIONS]).reshape(N_GROUPS, 1, BAND)
    return (buckets == jnp.arange(NUM_BUCKETS).reshape(1, NUM_BUCKETS, 1)).astype(F32)


def _relbias_fwd(rel_bias, name):
    table = rel_bias.reshape(NUM_BUCKETS, N_GROUPS, HEADS).transpose(1, 0, 2)

    def body(t_ref, oh_ref, o_ref):
        o_ref[...] = lax.dot_general(t_ref[...], oh_ref[...], (TN, ((), ())), preferred_element_type=F32,
                                     precision=lax.Precision.HIGHEST)

    out = pl.pallas_call(
        body, name=name, grid=(N_GROUPS,),
        in_specs=[pl.BlockSpec((None, NUM_BUCKETS, HEADS), lambda g: (g, 0, 0)),
                  pl.BlockSpec((None, NUM_BUCKETS, BAND), lambda g: (g, 0, 0))],
        out_specs=pl.BlockSpec((None, HEADS, BAND), lambda g: (g, 0, 0)),
        out_shape=_sds((N_GROUPS, HEADS, BAND), F32), compiler_params=_cp(1))(table, _bucket_onehot())
    return out.reshape(N_GROUPS, HEADS, BLK, 2 * BLK)


def _relbias_bwd(dbs, name):
    band = BAND
    onehot = _bucket_onehot()
    dbf = jnp.stack([db.reshape(HEADS, band) for db in dbs])

    def body(oh_ref, db_ref, o_ref):
        o_ref[...] = lax.dot_general(oh_ref[...], db_ref[...], (NT, ((), ())), preferred_element_type=F32,
                                     precision=lax.Precision.HIGHEST)

    out = pl.pallas_call(
        body, name=name, grid=(N_GROUPS,),
        in_specs=[pl.BlockSpec((None, NUM_BUCKETS, band), lambda g: (g, 0, 0)),
                  pl.BlockSpec((None, HEADS, band), lambda g: (g, 0, 0))],
        out_specs=pl.BlockSpec((None, NUM_BUCKETS, HEADS), lambda g: (g, 0, 0)),
        out_shape=_sds((N_GROUPS, NUM_BUCKETS, HEADS), F32), compiler_params=_cp(1))(onehot, dbf)
    return out.transpose(1, 0, 2).reshape(NUM_BUCKETS, N_GROUPS * HEADS)


def _chunk_pos(shape):
    return lax.broadcasted_iota(jnp.int32, shape, 0) % HG_CHUNK


def _chunk_cumsum(v):
    pos = _chunk_pos(v.shape)
    s = 1
    while s < HG_CHUNK:
        v = v + jnp.where(pos >= s, pltpu.roll(v, s, 0), 0.0)
        s *= 2
    return v


def _chunk_rev_cumsum(v):
    pos = _chunk_pos(v.shape)
    n = v.shape[0]
    s = 1
    while s < HG_CHUNK:
        v = v + jnp.where(pos < HG_CHUNK - s, pltpu.roll(v, n - s, 0), 0.0)
        s *= 2
    return v


def _lower_bound(raw):
    a0, a1 = raw[0:1], raw[1:2]
    m = jnp.maximum(a0, a1)
    e0, e1 = jnp.exp(a0 - m), jnp.exp(a1 - m)
    return e0 / (e0 + e1)


def _hg_gates(qr, fr, lb):
    sf = _sigmoid(fr)
    f = lb + (1.0 - lb) * sf
    sq = _sigmoid(qr)
    return qr * sq, sq, f, sf


HG_COL0 = QKV_W // HG_W


def _hgrn_fwd(proj, lb_raw, nw, name):
    S = proj.shape[0]
    ncs = HG_TILE // HG_CHUNK
    tril = jnp.tril(jnp.ones((HG_CHUNK, HG_CHUNK), dtype=bool))

    def body(q_ref, f_ref, i_ref, og_ref, lb_ref, nw_ref, y_ref, o_ref, st_ref, state):
        @pl.when(pl.program_id(0) == 0)
        def _():
            state[...] = jnp.zeros_like(state)

        lb = _lower_bound(lb_ref[...])
        q, _, f, _ = _hg_gates(q_ref[...], f_ref[...], lb)
        k = 1.0 - f
        G = _chunk_cumsum(jnp.log(f))
        row = lax.broadcasted_iota(jnp.int32, (HG_CHUNK, HG_CHUNK), 0)
        col = lax.broadcasted_iota(jnp.int32, (HG_CHUNK, HG_CHUNK), 1)
        heads = [slice(h * HG_DK, (h + 1) * HG_DK) for h in range(HG_HEADS)]
        sts = [state[h] for h in range(HG_HEADS)]
        for c in range(ncs):
            cs = slice(c * HG_CHUNK, (c + 1) * HG_CHUNK)
            for h, hs in enumerate(heads):
                Gc = G[cs, hs]
                gl = Gc[HG_CHUNK - 1:HG_CHUNK]
                qt = _bf(q[cs, hs] * jnp.exp(Gc))
                kt = _bf(k[cs, hs] * jnp.exp(-Gc))
                kd = _bf(k[cs, hs] * jnp.exp(gl - Gc))
                v = _bf(i_ref[cs, hs])
                A = jnp.where(row >= col, _dot(qt, kt, NT), 0.0)
                o_ref[cs, hs] = _dot(_bf(A), v, NN) + _dot(qt, _bf(sts[h]), NT)
                st_ref[c, h] = sts[h]
                sts[h] = sts[h] * jnp.exp(gl) + _dot(v, kd, TN)
        for h, hs in enumerate(heads):
            state[h] = sts[h]
            oh = o_ref[:, hs]
            og = og_ref[:, hs]
            y_ref[:, hs] = oh * _rinv(oh) * nw_ref[...] * (og * _sigmoid(og))

    def colspec(j):
        return pl.BlockSpec((HG_TILE, HG_W), lambda i: (i, HG_COL0 + j))

    return pl.pallas_call(
        body, name=name, grid=(S // HG_TILE,),
        in_specs=[colspec(0), colspec(1), colspec(2), colspec(3),
                  pl.BlockSpec((2, HG_W), lambda i: (0, 0)), pl.BlockSpec((1, HG_DK), lambda i: (0, 0))],
        out_specs=[pl.BlockSpec((HG_TILE, HG_W), lambda i: (i, 0))] * 2
        + [pl.BlockSpec((ncs, HG_HEADS, HG_DK, HG_DK), lambda i: (i, 0, 0, 0))],
        out_shape=[_sds((S, HG_W), F32)] * 2 + [_sds((S // HG_CHUNK, HG_HEADS, HG_DK, HG_DK), F32)],
        scratch_shapes=[pltpu.VMEM((HG_HEADS, HG_DK, HG_DK), F32)],
        compiler_params=_cp(1))(proj, proj, proj, proj, lb_raw, nw)


def _hgrn_bwd(proj, lb_raw, nw, o, states, dy, name):
    S = proj.shape[0]
    ncs = HG_TILE // HG_CHUNK
    nt = S // HG_TILE

    def body(q_ref, f_ref, i_ref, og_ref, lb_ref, nw_ref, o_ref, st_ref, dy_ref,
             dq_ref, df_ref, di_ref, dog_ref, dlb_ref, dnw_ref, dstate, do_s, dG_s, dgl_s, dk_s, dlb_s):
        step = pl.program_id(0)

        @pl.when(step == 0)
        def _():
            dstate[...] = jnp.zeros_like(dstate)
            dlb_s[...] = jnp.zeros_like(dlb_s)
            dnw_ref[...] = jnp.zeros_like(dnw_ref)

        lb = _lower_bound(lb_ref[...])
        qr = q_ref[...]
        q, sq, f, sf = _hg_gates(qr, f_ref[...], lb)
        k = 1.0 - f
        G = _chunk_cumsum(jnp.log(f))
        nwv = nw_ref[...]
        row = lax.broadcasted_iota(jnp.int32, (HG_CHUNK, HG_CHUNK), 0)
        col = lax.broadcasted_iota(jnp.int32, (HG_CHUNK, HG_CHUNK), 1)
        for h in range(HG_HEADS):
            hs = slice(h * HG_DK, (h + 1) * HG_DK)
            oh = o_ref[:, hs]
            r = _rinv(oh)
            ohat = oh * r
            og = og_ref[:, hs]
            sg = _sigmoid(og)
            dyh = dy_ref[:, hs]
            don = dyh * (og * sg)
            dog_ref[:, hs] = _bf(dyh * (ohat * nwv) * (sg * (1.0 + og * (1.0 - sg))))
            dnw_ref[...] += jnp.sum(don * ohat, axis=0, keepdims=True)
            do_s[:, hs] = _norm_bwd(don, ohat, r, nwv)
        dsts = [dstate[h] for h in range(HG_HEADS)]
        for c in reversed(range(ncs)):
            cs = slice(c * HG_CHUNK, (c + 1) * HG_CHUNK)
            for h in range(HG_HEADS):
                hs = slice(h * HG_DK, (h + 1) * HG_DK)
                dst = dsts[h]
                Gc = G[cs, hs]
                gl = Gc[HG_CHUNK - 1:HG_CHUNK]
                eG, enG, edG, egl = jnp.exp(Gc), jnp.exp(-Gc), jnp.exp(gl - Gc), jnp.exp(gl)
                qt, kt, kd = q[cs, hs] * eG, k[cs, hs] * enG, k[cs, hs] * edG
                qtb, ktb, kdb = _bf(qt), _bf(kt), _bf(kd)
                v = _bf(i_ref[cs, hs])
                do = _bf(do_s[cs, hs])
                st = st_ref[c, h]
                dstb = _bf(dst)
                A = jnp.where(row >= col, _dot(qtb, ktb, NT), 0.0)
                dA = _bf(jnp.where(row >= col, _dot(do, v, NT), 0.0))
                di_ref[cs, hs] = _bf(_dot(_bf(A), do, TN) + _dot(kdb, dstb, NT))
                dqt = _dot(dA, ktb, NN) + _dot(do, _bf(st), NN)
                dkt = _dot(dA, qtb, TN)
                dkd = _dot(v, dstb, NN)
                dgl = egl * jnp.sum(st * dst, axis=0, keepdims=True) + jnp.sum(dkd * kd, axis=0, keepdims=True)
                dsts[h] = dst * egl + _dot(do, qtb, TN)
                dq_ref[cs, hs] = _bf(dqt * eG * (sq[cs, hs] * (1.0 + qr[cs, hs] * (1.0 - sq[cs, hs]))))
                dk_s[cs, hs] = dkt * enG + dkd * edG
                dG_s[cs, hs] = dqt * qt - dkt * kt - dkd * kd
                dgl_s[cs, hs] = jnp.broadcast_to(dgl, (HG_CHUNK, HG_DK))
        for h in range(HG_HEADS):
            dstate[h] = dsts[h]
        dg = _chunk_rev_cumsum(dG_s[...]) + dgl_s[...]
        dfv = dg / f - dk_s[...]
        df_ref[...] = _bf(dfv * (1.0 - lb) * sf * (1.0 - sf))
        dlb_s[...] += jnp.sum(dfv * (1.0 - sf), axis=0, keepdims=True)

        @pl.when(step == nt - 1)
        def _():
            t = dlb_s[...] * lb * (1.0 - lb)
            dlb_ref[...] = jnp.concatenate([t, -t], axis=0)

    def colspec(j):
        return pl.BlockSpec((HG_TILE, HG_W), lambda i: (nt - 1 - i, HG_COL0 + j))

    tile = pl.BlockSpec((HG_TILE, HG_W), lambda i: (nt - 1 - i, 0))
    outs = pl.pallas_call(
        body, name=name, grid=(nt,),
        in_specs=[colspec(0), colspec(1), colspec(2), colspec(3),
                  pl.BlockSpec((2, HG_W), lambda i: (0, 0)), pl.BlockSpec((1, HG_DK), lambda i: (0, 0)),
                  tile, pl.BlockSpec((ncs, HG_HEADS, HG_DK, HG_DK), lambda i: (nt - 1 - i, 0, 0, 0)), tile],
        out_specs=[tile] * 4 + [pl.BlockSpec((2, HG_W), lambda i: (0, 0)), pl.BlockSpec((1, HG_DK), lambda i: (0, 0))],
        out_shape=[_sds((S, HG_W), BF16)] * 4 + [_sds((2, HG_W), F32), _sds((1, HG_DK), F32)],
        scratch_shapes=[pltpu.VMEM((HG_HEADS, HG_DK, HG_DK), F32)] + [pltpu.VMEM((HG_TILE, HG_W), F32)] * 4
        + [pltpu.VMEM((1, HG_W), F32)],
        compiler_params=_cp(1))(proj, proj, proj, proj, lb_raw, nw, o, states, dy)
    return outs[:4], outs[4], outs[5]


GATE_COL0 = (QKV_W + 4 * HG_W) // GROUP_W
HALF_D = D_MODEL // 2


def _gate_tiles(proj):
    return [_tile(proj, HALF_D, lambda c: GATE_COL0 + c), _tile(proj, HALF_D, lambda c: GATE_COL0 + 2 + c)]


def _merge_fwd(za, zh, proj, name):
    def body(za_ref, zh_ref, g0_ref, g1_ref, m_ref):
        m_ref[...] = _bf(_sigmoid(g0_ref[...]) * za_ref[...] + _sigmoid(g1_ref[...]) * zh_ref[...])

    col = lambda c: c
    return _rows_call(name, body, za.shape[0], 512, 2,
                      [_tile(za, HALF_D, col), _tile(zh, HALF_D, col), *_gate_tiles(proj)],
                      [_out_tile(D_MODEL, BF16, HALF_D, col)])[0]


def _merge_bwd(dm, za, zh, proj, name):
    def body(dm_ref, za_ref, zh_ref, g0_ref, g1_ref, dza_ref, dzh_ref, dg0_ref, dg1_ref):
        dmv = dm_ref[...]
        s0, s1 = _sigmoid(g0_ref[...]), _sigmoid(g1_ref[...])
        dza_ref[...] = _bf(dmv * s0)
        dzh_ref[...] = _bf(dmv * s1)
        dg0_ref[...] = _bf(dmv * za_ref[...] * s0 * (1.0 - s0))
        dg1_ref[...] = _bf(dmv * zh_ref[...] * s1 * (1.0 - s1))

    col = lambda c: c
    return _rows_call(name, body, za.shape[0], 512, 2,
                      [_tile(dm, HALF_D, col), _tile(za, HALF_D, col), _tile(zh, HALF_D, col), *_gate_tiles(proj)],
                      [_out_tile(D_MODEL, BF16, HALF_D, col)] * 4)


def _mix_out(mo, x, w_post, w_pre, name):
    def body(mo_ref, x_ref, wp_ref, wf_ref, x1_ref, h2_ref):
        z = mo_ref[...]
        x1 = x_ref[...] + z * _rinv(z) * wp_ref[...]
        x1_ref[...] = x1
        h2_ref[...] = _bf(x1 * _rinv(x1) * wf_ref[...])

    return _rows_call(name, body, x.shape[0], 512, 1,
                      [_tile(mo, D_MODEL), _tile(x, D_MODEL), _full(w_post), _full(w_pre)],
                      [_out_tile(D_MODEL, F32, D_MODEL), _out_tile(D_MODEL, BF16, D_MODEL)])


def _loss_head(ffo, x1, tgt, w, name):
    def body(f_ref, x1_ref, t_ref, w_ref, dx_ref, df_ref, dw_ref, loss_ref):
        z = f_ref[...]
        r = _rinv(z)
        zhat = z * r
        wv = w_ref[...]
        e = x1_ref[...] + zhat * wv - t_ref[...]
        dx = e * (1.0 / D_MODEL)
        dx_ref[...] = dx
        df_ref[...] = _bf(_norm_bwd(dx, zhat, r, wv))
        _acc(dw_ref, jnp.sum(dx * zhat, axis=0, keepdims=True))
        part = 0.5 * jnp.sum(jnp.sum(e * e, axis=1, keepdims=True), axis=0, keepdims=True) * (1.0 / D_MODEL)
        _acc(loss_ref, jnp.broadcast_to(part, (1, LANES)))

    return _rows_call(name, body, x1.shape[0], 512, 1,
                      [_tile(ffo, D_MODEL), _tile(x1, D_MODEL), _tile(tgt, D_MODEL), _full(w)],
                      [_out_tile(D_MODEL, F32, D_MODEL), _out_tile(D_MODEL, BF16, D_MODEL),
                       _out_acc(1, D_MODEL, D_MODEL), _out_acc(1, LANES, LANES)])


CONV_CB = D_FF // 2
CONV_TM = 512
HALO = 8
SQRT_HALF = 0.7071067811865476
INV_SQRT_2PI = 0.3989422804014327


def _conv_taps(u_ref, halo_ref, first):
    u = u_ref[...]
    row = lax.broadcasted_iota(jnp.int32, u.shape, 0)
    p1 = jnp.where(first, 0.0, halo_ref[HALO - 1:HALO, :])
    p2 = jnp.where(first, 0.0, halo_ref[HALO - 2:HALO - 1, :])
    u1 = jnp.where(row == 0, p1, pltpu.roll(u, 1, 0))
    u2 = jnp.where(row == 0, p2, jnp.where(row == 1, p1, pltpu.roll(u, 2, 0)))
    return u2, u1, u


def _conv(taps, w_ref, b_ref):
    return b_ref[...] + w_ref[0:1, :] * taps[0] + w_ref[1:2, :] * taps[1] + w_ref[2:3, :] * taps[2]


def _conv_specs(tm):
    nh = tm // HALO
    nc = D_FF // CONV_CB

    def tile(off):
        return pl.BlockSpec((tm, CONV_CB), lambda c, i: (i, off + c))

    def halo(off):
        return pl.BlockSpec((HALO, CONV_CB), lambda c, i: (jnp.maximum(i * nh - 1, 0), off + c))

    def small(rows, off):
        return pl.BlockSpec((rows, CONV_CB), lambda c, i: (0, off + c))

    return nc, tile, halo, small


def _conv_gelu_fwd(u, cw, cb, name):
    S = u.shape[0]
    tm = CONV_TM
    nc, tile, halo, small = _conv_specs(tm)

    def body(ug, hg, uv, hv, wg, wv, bg, bv, a_ref):
        first = pl.program_id(1) == 0
        cg = _conv(_conv_taps(ug, hg, first), wg, bg)
        cv = _conv(_conv_taps(uv, hv, first), wv, bv)
        a_ref[...] = _bf(0.5 * cg * (1.0 + lax.erf(cg * SQRT_HALF)) * cv)

    return pl.pallas_call(
        body, name=name, grid=(nc, S // tm),
        in_specs=[tile(0), halo(0), tile(nc), halo(nc), small(3, 0), small(3, nc), small(1, 0), small(1, nc)],
        out_specs=tile(0), out_shape=_sds((S, D_FF), BF16), compiler_params=_cp(2))(u, u, u, u, cw, cw, cb, cb)


def _conv_gelu_bwd(u, da, cw, cb, name, plans=None):
    S = u.shape[0]
    tm = CONV_TM
    nc, tile, halo, small = _conv_specs(tm)

    def body(ug, hg, uv, hv, wg, wv, bg, bv, da_ref, dcg_ref, dcv_ref, dwg_ref, dwv_ref, dbg_ref, dbv_ref):
        first = pl.program_id(1) == 0
        tg = _conv_taps(ug, hg, first)
        tv = _conv_taps(uv, hv, first)
        cg = _conv(tg, wg, bg)
        cv = _conv(tv, wv, bv)
        phi = 0.5 * (1.0 + lax.erf(cg * SQRT_HALF))
        dav = da_ref[...]
        dcg = dav * cv * (phi + cg * jnp.exp(-0.5 * cg * cg) * INV_SQRT_2PI)
        dcv = dav * (cg * phi)
        dcg_ref[...] = dcg
        dcv_ref[...] = dcv
        for dc, taps, dw_ref, db_ref in ((dcg, tg, dwg_ref, dbg_ref), (dcv, tv, dwv_ref, dbv_ref)):
            _acc(db_ref, jnp.sum(dc, axis=0, keepdims=True))
            for j in range(3):
                _acc(dw_ref.at[j:j + 1, :], jnp.sum(dc * taps[j], axis=0, keepdims=True))

    res, carried = _call(
        body, plans, name=name, grid=(nc, S // tm),
        in_specs=[tile(0), halo(0), tile(nc), halo(nc), small(3, 0), small(3, nc), small(1, 0), small(1, nc), tile(0)],
        out_specs=[tile(0), tile(0), small(3, 0), small(3, 0), small(1, 0), small(1, 0)],
        out_shape=[_sds((S, D_FF), F32)] * 2 + [_sds((3, D_FF), F32)] * 2 + [_sds((1, D_FF), F32)] * 2,
        args=(u, u, u, u, cw, cw, cb, cb, da))
    return res if plans is None else (res, carried)


def _conv_input_bwd(dcg, dcv, cw, name, plans=None):
    S = dcg.shape[0]
    tm = CONV_TM
    nc, tile, _, small = _conv_specs(tm)
    nh = tm // HALO
    nt = S // tm

    def nxt(off):
        return pl.BlockSpec((HALO, CONV_CB), lambda c, i: (jnp.minimum((i + 1) * nh, S // HALO - 1), off + c))

    def body(g_ref, ng_ref, v_ref, nv_ref, wg, wv, dug_ref, duv_ref):
        last = pl.program_id(1) == nt - 1
        for dc_ref, n_ref, w_ref, du_ref in ((g_ref, ng_ref, wg, dug_ref), (v_ref, nv_ref, wv, duv_ref)):
            dc = dc_ref[...]
            row = lax.broadcasted_iota(jnp.int32, dc.shape, 0)
            n1 = jnp.where(last, 0.0, n_ref[0:1, :])
            n2 = jnp.where(last, 0.0, n_ref[1:2, :])
            d1 = jnp.where(row == tm - 1, n1, pltpu.roll(dc, tm - 1, 0))
            d2 = jnp.where(row == tm - 1, n2, jnp.where(row == tm - 2, n1, pltpu.roll(dc, tm - 2, 0)))
            du_ref[...] = _bf(w_ref[2:3, :] * dc + w_ref[1:2, :] * d1 + w_ref[0:1, :] * d2)

    res, carried = _call(
        body, plans, name=name, grid=(nc, nt),
        in_specs=[tile(0), nxt(0), tile(0), nxt(0), small(3, 0), small(3, nc)],
        out_specs=[tile(0), tile(0)], out_shape=[_sds((S, D_FF), BF16)] * 2,
        args=(dcg, dcg, dcv, dcv, cw, cw))
    return res if plans is None else (res, carried)


def _row_tile(n, cap):
    best = n
    for t in range(16, cap + 1, 16):
        if n % t == 0:
            best = t
    return best if best <= cap else n


def _adamw(w, g, m, v, name):
    R, C = w.shape
    tr = _row_tile(R, max(16, (2 * 1024 * 1024) // (4 * C) // 16 * 16))

    def body(w_ref, g_ref, m_ref, v_ref, d_ref, nm_ref, nv_ref):
        gv = g_ref[...]
        nm = ADAM_B1 * m_ref[...] + (1.0 - ADAM_B1) * gv
        nv = ADAM_B2 * v_ref[...] + (1.0 - ADAM_B2) * (gv * gv)
        m_hat = nm / (1.0 - ADAM_B1 ** ADAM_STEP)
        v_hat = nv / (1.0 - ADAM_B2 ** ADAM_STEP)
        d_ref[...] = -ADAM_LR * (m_hat / (jnp.sqrt(v_hat) + ADAM_EPS) + ADAM_WD * w_ref[...])
        nm_ref[...] = nm
        nv_ref[...] = nv

    spec = pl.BlockSpec((tr, C), lambda i: (i, 0))
    return pl.pallas_call(body, name=name, grid=(R // tr,), in_specs=[spec] * 4, out_specs=[spec] * 3,
                          out_shape=[_sds((R, C), F32)] * 3, compiler_params=_cp(1))(w, g, m, v)


def _pair_sum(gfull, rcv, c_idx, name):
    nb, R, C = gfull.shape
    half = R // 2
    tr = _row_tile(half, 256)
    nt = half // tr

    def body(c_ref, g_ref, r_ref, o_ref):
        o_ref[...] = _bf(g_ref[...] + r_ref[...])

    return pl.pallas_call(
        body, name=name,
        grid_spec=pltpu.PrefetchScalarGridSpec(
            num_scalar_prefetch=1, grid=(nb, nt),
            in_specs=[pl.BlockSpec((None, tr, C), lambda j, i, c_ref: (j, c_ref[0] * nt + i, 0)),
                      pl.BlockSpec((None, tr, C), lambda j, i, c_ref: (j, i, 0))],
            out_specs=pl.BlockSpec((None, tr, C), lambda j, i, c_ref: (j, i, 0))),
        out_shape=_sds((nb, half, C), BF16), compiler_params=_cp(2))(c_idx, gfull, rcv)


def _chip_sum(arrived, own, place, name, row0, shard_rows, into=None):
    nb, H, C = arrived.shape
    tr = _row_tile(H, 256)
    nt = H // tr
    assert row0 % tr == 0
    extra = [] if into is None else [into]

    def body(pl_ref, *refs):
        o_ref = refs[nb + 1 + len(extra)]
        me = pl_ref[0]
        acc = None
        for k in range(nb):
            term = jnp.where(me == k, refs[nb][...], refs[k][...]).astype(F32)
            acc = term if acc is None else acc + term
        o_ref[...] = acc

    def other(k):
        return pl.BlockSpec((None, tr, C), lambda i, p: (jnp.where(p[0] == k, (k + 1) % nb, k), i, 0))

    return pl.pallas_call(
        body, name=name,
        grid_spec=pltpu.PrefetchScalarGridSpec(
            num_scalar_prefetch=1, grid=(nt,),
            in_specs=[other(k) for k in range(nb)] + [pl.BlockSpec((None, tr, C), lambda i, p: (p[0], i, 0))]
            + [ANY] * len(extra),
            out_specs=pl.BlockSpec((tr, C), lambda i, p: (row0 // tr + p[1] * nt + i, 0))),
        out_shape=_sds((shard_rows, C), F32),
        input_output_aliases={1 + nb + 1: 0} if extra else {},
        compiler_params=_cp(1))(place, *([arrived] * nb), own, *extra)


def _cast_into_slot(shard, place, name):
    R, C = shard.shape
    tr = _row_tile(R, 256)

    def body(pl_ref, s_ref, o_ref):
        o_ref[...] = _bf(s_ref[...])

    return pl.pallas_call(
        body, name=name,
        grid_spec=pltpu.PrefetchScalarGridSpec(
            num_scalar_prefetch=1, grid=(R // tr,),
            in_specs=[pl.BlockSpec((tr, C), lambda i, p: (i, 0))],
            out_specs=pl.BlockSpec((None, tr, C), lambda i, p: (p[0], i, 0))),
        out_shape=_sds((N_CHIPS, R, C), BF16), compiler_params=_cp(1))(place, shard)


def _place():
    x, y, c = lax.axis_index("x"), lax.axis_index("y"), lax.axis_index("c")
    chips = [(1 - x, y), (x, 1 - y), (1 - x, 1 - y)]
    return x, y, c, chips


def _chip_id(px, py):
    return 2 * px + py


def _remote(src, dst, send_sems, recv_sems, k, to):
    return pltpu.make_async_remote_copy(src_ref=src, dst_ref=dst, send_sem=send_sems.at[k], recv_sem=recv_sems.at[k],
                                        device_id=to, device_id_type=MESH)


def _gather_weights(slots, wholes, name):
    ns, nw = len(slots), len(wholes)
    n = ns + nw

    def body(*refs):
        ins = refs[ns:n]
        outs = refs[n:2 * n]
        send_sems, recv_sems, local_sems = refs[2 * n:]
        x, y, c, chips = _place()
        me = _chip_id(x, y)
        sib = (x, y, 1 - c)
        local = [pltpu.make_async_copy(ins[b], outs[ns + b].at[me], local_sems.at[b]) for b in range(nw)]
        for cp in local:
            cp.start()
        sent = []
        for a in range(n):
            R = outs[a].shape[1]
            rows = pl.ds(c * (R // 2), R // 2) if a < ns else pl.ds(0, R)
            src = outs[a].at[me, rows] if a < ns else ins[a - ns]
            for j, chip in enumerate(chips):
                cp = _remote(src, outs[a].at[me, rows], send_sems, recv_sems, 6 * a + j, (*chip, c))
                cp.start()
                sent.append(cp)
        for a in range(n):
            R = outs[a].shape[1]
            rows = pl.ds(c * (R // 2), R // 2) if a < ns else pl.ds(0, R)
            for j, chip in enumerate(chips):
                landed = outs[a].at[_chip_id(*chip), rows]
                _remote(landed, landed, send_sems, recv_sems, 6 * a + j, (*chip, c)).wait_recv()
                if a < ns:
                    cp = _remote(landed, landed, send_sems, recv_sems, 6 * a + 3 + j, sib)
                    cp.start()
                    sent.append(cp)
        for a in range(ns):
            R = outs[a].shape[1]
            other = pl.ds((1 - c) * (R // 2), R // 2)
            for j, chip in enumerate(chips):
                passed = outs[a].at[_chip_id(*chip), other]
                _remote(passed, passed, send_sems, recv_sems, 6 * a + 3 + j, sib).wait_recv()
        for cp in sent:
            cp.wait_send()
        for cp in local:
            cp.wait()

    return pl.pallas_call(
        body, name=name, in_specs=[ANY] * n, out_specs=[ANY] * n,
        out_shape=[_sds(s.shape, s.dtype) for s in slots] + [_sds((N_CHIPS, *s.shape), s.dtype) for s in wholes],
        input_output_aliases={a: a for a in range(ns)},
        scratch_shapes=[pltpu.SemaphoreType.DMA((6 * n,)), pltpu.SemaphoreType.DMA((6 * n,)),
                        pltpu.SemaphoreType.DMA((max(nw, 1),))])(*slots, *wholes)


def _gather_ici_plan(slots, wholes):
    ns, nw = len(slots), len(wholes)

    def copies(ins, ios, outs, send_sems, recv_sems, local_sems):
        x, y, c, chips = _place()
        me = _chip_id(x, y)
        sends, recvs = [], []
        for a in range(ns + nw):
            dst = ios[a] if a < ns else outs[a - ns]
            R = dst.shape[1]
            rows = pl.ds(c * (R // 2), R // 2) if a < ns else pl.ds(0, R)
            src = dst.at[me, rows] if a < ns else ins[a - ns]
            for j, chip in enumerate(chips):
                sends.append(_remote(src, dst.at[me, rows], send_sems, recv_sems, 3 * a + j, (*chip, c)))
                landed = dst.at[_chip_id(*chip), rows]
                recvs.append(_remote(landed, landed, send_sems, recv_sems, 3 * a + j, (*chip, c)))
        local = [pltpu.make_async_copy(ins[b], outs[b].at[me], local_sems.at[b]) for b in range(nw)]
        return sends, recvs, local

    return _Plan(copies, 3 * (ns + nw), ins=wholes, inouts=slots,
                 outs=[_sds((N_CHIPS, *s.shape), s.dtype) for s in wholes])


def _gather_pass_plan(slots):
    def copies(ins, ios, outs, send_sems, recv_sems, local_sems):
        x, y, c, chips = _place()
        sib = (x, y, 1 - c)
        sends, recvs = [], []
        for a, buf in enumerate(ios):
            half = buf.shape[1] // 2
            for j, chip in enumerate(chips):
                mine = buf.at[_chip_id(*chip), pl.ds(c * half, half)]
                other = buf.at[_chip_id(*chip), pl.ds((1 - c) * half, half)]
                sends.append(_remote(mine, mine, send_sems, recv_sems, 3 * a + j, sib))
                recvs.append(_remote(other, other, send_sems, recv_sems, 3 * a + j, sib))
        return sends, recvs, []

    return _Plan(copies, 3 * len(slots), inouts=slots)


def _pair_plan(grads):
    def copies(ins, ios, outs, send_sems, recv_sems, local_sems):
        x, y, c, _ = _place()
        sib = (x, y, 1 - c)
        sends, recvs = [], []
        for a, g in enumerate(ins):
            half = g.shape[1] // 2
            sends.append(_remote(g.at[:, pl.ds((1 - c) * half, half), :], outs[a], send_sems, recv_sems, a, sib))
            recvs.append(_remote(outs[a], outs[a], send_sems, recv_sems, a, sib))
        return sends, recvs, []

    return _Plan(copies, len(grads), ins=grads,
                 outs=[_sds((g.shape[0], g.shape[1] // 2, g.shape[2]), g.dtype) for g in grads])


def _chip_plan(parts):
    def copies(ins, ios, outs, send_sems, recv_sems, local_sems):
        x, y, c, chips = _place()
        me = _chip_id(x, y)
        sends, recvs = [], []
        for a, part in enumerate(ins):
            for j, chip in enumerate(chips):
                sends.append(_remote(part.at[_chip_id(*chip)], outs[a].at[me], send_sems, recv_sems, 3 * a + j, (*chip, c)))
                landed = outs[a].at[_chip_id(*chip)]
                recvs.append(_remote(landed, landed, send_sems, recv_sems, 3 * a + j, (*chip, c)))
        return sends, recvs, []

    return _Plan(copies, 3 * len(parts), ins=parts, outs=[_sds(p.shape, p.dtype) for p in parts])


def _pair_concat(fulls, pieces, name):
    n = len(fulls)
    spans = [(a, row0, rows // 2) for a in range(n) for (row0, rows) in pieces[a]]

    def body(*refs):
        outs = refs[n:2 * n]
        send_sems, recv_sems = refs[2 * n:]
        x, y, c, _ = _place()
        cps = []
        for s, (a, row0, H) in enumerate(spans):
            mine = outs[a].at[pl.ds(row0 + c * H, H)]
            cp = _remote(mine, mine, send_sems, recv_sems, s, (x, y, 1 - c))
            cp.start()
            cps.append(cp)
        for s, (a, row0, H) in enumerate(spans):
            other = outs[a].at[pl.ds(row0 + (1 - c) * H, H)]
            _remote(other, other, send_sems, recv_sems, s, (x, y, 1 - c)).wait_recv()
            cps[s].wait_send()

    return pl.pallas_call(
        body, name=name, in_specs=[ANY] * n, out_specs=[ANY] * n,
        out_shape=[_sds(f.shape, f.dtype) for f in fulls], input_output_aliases={a: a for a in range(n)},
        scratch_shapes=[pltpu.SemaphoreType.DMA((len(spans),)), pltpu.SemaphoreType.DMA((len(spans),))])(*fulls)


def _all_sum(pack, name):
    R, C = pack.shape

    def body(p_ref, o_ref, buf, send_sems, recv_sems):
        x, y, c, _ = _place()
        me = 4 * x + 2 * y + c
        buf[me] = p_ref[...]
        cps = []
        for k in range(1, N_DEV):
            to = (x ^ (k >> 2), y ^ ((k >> 1) & 1), c ^ (k & 1))
            cp = _remote(p_ref, buf.at[me], send_sems, recv_sems, k - 1, to)
            cp.start()
            cps.append(cp)
        for k in range(1, N_DEV):
            frm = (x ^ (k >> 2), y ^ ((k >> 1) & 1), c ^ (k & 1))
            slot = buf.at[4 * frm[0] + 2 * frm[1] + frm[2]]
            _remote(slot, slot, send_sems, recv_sems, k - 1, frm).wait_recv()
        acc = buf[0]
        for k in range(1, N_DEV):
            acc = acc + buf[k]
        o_ref[...] = acc
        for cp in cps:
            cp.wait_send()

    vm = pl.BlockSpec(memory_space=pltpu.VMEM)
    return pl.pallas_call(
        body, name=name, in_specs=[vm], out_specs=vm, out_shape=_sds((R, C), F32),
        scratch_shapes=[pltpu.VMEM((N_DEV, R, C), F32), pltpu.SemaphoreType.DMA((N_DEV - 1,)),
                        pltpu.SemaphoreType.DMA((N_DEV - 1,))])(pack)


def _local_step(xs, tgt, p, ex):
    h1 = _norm_fwd(xs, p["pre_mix_norm"], "pre_mix_norm")
    proj, got = _mm_nn_blk(h1, ex.weight("w_in"), "proj_in", plans=ex.carry("proj_in"))
    ex.done("proj_in", got)
    biases = _relbias_fwd(p["rel_bias"], "rel_bias_fwd")
    fw = []
    for g in range(N_GROUPS):
        res, got = _attn_fwd(proj, biases[g], g, f"attn_fwd{g}", plans=ex.carry(f"attn_fwd{g}"))
        ex.done(f"attn_fwd{g}", got)
        fw.append(res)
    y, lse = _attn_merge([t[0] for t in fw], [t[1] for t in fw], "attn_merge")
    yh, o_h, states = _hgrn_fwd(proj, p["hgrn_lb_raw"], p["hgrn_norm"], "hgrn_fwd")
    W_a, W_h, W_out = ex.weight("w_branch_attn"), ex.weight("w_branch_hgrn"), ex.weight("w_out")
    W_up, W_down, conv_w = ex.weight("w_up"), ex.weight("w_down"), ex.weight("conv_w")
    za = _mm_nn_blk(y, W_a, "branch_attn")
    zh = _mm_nn_blk(yh, W_h, "branch_hgrn")
    merged = _merge_fwd(za, zh, proj, "merge_fwd")
    mo = _mm_nn(merged, W_out, "mix_out")
    x1, h2 = _mix_out(mo, xs, p["post_mix_norm"], p["pre_ffn_norm"], "mix_residual")
    u = _mm_nn_blk(h2, W_up, "ffn_up")
    a = _conv_gelu_fwd(u, conv_w, p["conv_b"], "conv_gelu_fwd")
    ffo = _mm_nn(a, W_down, "ffn_down")
    dx2, dff, g_post_ffn, loss = _loss_head(ffo, x1, tgt, p["post_ffn_norm"], "loss_head")

    da = _mm_nt(dff, W_down, "d_ffn_act")
    ex.grad("w_down", _mm_tn(a, dff, "g_w_down").reshape(N_CHIPS, D_FF // N_CHIPS, D_MODEL))
    (dcg, dcv, gwg, gwv, gbg, gbv), got = _conv_gelu_bwd(u, da, conv_w, p["conv_b"], "conv_gelu_bwd",
                                                          plans=ex.carry("conv_gelu_bwd"))
    ex.done("conv_gelu_bwd", got)
    g_conv_w = jnp.concatenate([gwg, gwv], axis=1)
    g_conv_b = jnp.concatenate([gbg, gbv], axis=1)
    du_parts, got = _conv_input_bwd(dcg, dcv, conv_w, "conv_input_bwd", plans=ex.carry("conv_input_bwd"))
    ex.done("conv_input_bwd", got)
    du = jnp.concatenate(du_parts, axis=1)
    dh2 = _mm_nt_blk(du, W_up, "d_ffn_in")
    ex.grad("w_up", _mm_tn_blk(h2, du, N_CHIPS, "g_w_up"))
    (dx1, g_pre_ffn), got = _prenorm_bwd(dh2, x1, p["pre_ffn_norm"], dx2, "pre_ffn_norm_bwd",
                                         plans=ex.carry("pre_ffn_norm_bwd"))
    ex.done("pre_ffn_norm_bwd", got)
    dmo, g_post_mix = _postnorm_bwd(dx1, mo, p["post_mix_norm"], "post_mix_norm_bwd")
    dmerged = _mm_nt(dmo, W_out, "d_merged")
    ex.grad("w_out", _mm_tn(merged, dmo, "g_w_out").reshape(N_CHIPS, D_MODEL // N_CHIPS, D_MODEL))
    dza, dzh, dg0, dg1 = _merge_bwd(dmerged, za, zh, proj, "merge_bwd")
    dy = _mm_nt_blk(dza, W_a, "d_attn_out")
    ex.grad("w_branch_attn", _mm_tn_blk(y, dza, N_CHIPS, "g_w_branch_attn"))
    dyh = _mm_nt_blk(dzh, W_h, "d_hgrn_out")
    ex.grad("w_branch_hgrn", _mm_tn_blk(yh, dzh, N_CHIPS, "g_w_branch_hgrn"))
    dqkv, dbs = [], []
    for g in range(N_GROUPS):
        parts, db, got = _attn_bwd(proj, biases[g], lse, y, dy, g, f"attn_bwd{g}", plans=ex.carry(f"attn_bwd{g}"))
        ex.done(f"attn_bwd{g}", got)
        dqkv += parts
        dbs.append(db)
    g_rel_bias = _relbias_bwd(dbs, "rel_bias_bwd")
    dhg, g_lb_raw, g_hgrn_norm = _hgrn_bwd(proj, p["hgrn_lb_raw"], p["hgrn_norm"], o_h, states, dyh, "hgrn_bwd")
    dproj = jnp.concatenate([*[_bf(t) for t in dqkv], *dhg, dg0, dg1], axis=1)
    for piece, row0, rows in PIECES["w_in"]:
        g, got = _mm_tn_blk(h1, dproj, N_CHIPS, f"g_{piece}", x_cols=(row0 // rows, rows), plans=ex.carry(f"g_{piece}"))
        ex.done(f"g_{piece}", got)
        ex.grad(piece, g)
    dh1, got = _mm_nt_blk(dproj, ex.weight("w_in"), "d_proj_in", plans=ex.carry("d_proj_in"))
    ex.done("d_proj_in", got)
    (grad_x, g_pre_mix), got = _prenorm_bwd(dh1, xs, p["pre_mix_norm"], dx1, "pre_mix_norm_bwd",
                                            plans=ex.carry("pre_mix_norm_bwd"))
    ex.done("pre_mix_norm_bwd", got)
    small = dict(pre_mix_norm=g_pre_mix, rel_bias=g_rel_bias, hgrn_lb_raw=g_lb_raw, hgrn_norm=g_hgrn_norm,
                 post_mix_norm=g_post_mix, pre_ffn_norm=g_pre_ffn, conv_w=g_conv_w, conv_b=g_conv_b,
                 post_ffn_norm=g_post_ffn)
    return loss, grad_x, small


SMALL = ("pre_mix_norm", "rel_bias", "hgrn_lb_raw", "hgrn_norm", "post_mix_norm", "pre_ffn_norm", "conv_w", "conv_b",
         "post_ffn_norm")
BIG = ("w_in", "w_up", "w_down", "w_out", "w_branch_attn", "w_branch_hgrn")
WEIGHTS = ("pre_mix_norm", "w_in", "rel_bias", "hgrn_lb_raw", "hgrn_norm", "w_branch_attn", "w_branch_hgrn", "w_out",
           "post_mix_norm", "pre_ffn_norm", "w_up", "conv_w", "conv_b", "w_down", "post_ffn_norm")
MIXER = ("w_out", "w_branch_attn", "w_branch_hgrn")

SCHEDULE = {
    "proj_in": [("gather_ici_cw", ("w_up",) + MIXER)],
    "attn_fwd0": [("gather_pass", ("w_up",) + MIXER), ("gather_ici", ("w_down",))],
    "attn_fwd1": [("gather_pass", ("w_down",))],
    "conv_gelu_bwd": [("pair", ("w_down",))],
    "conv_input_bwd": [("chip", ("w_down",))],
    "pre_ffn_norm_bwd": [("pair", ("w_up",))],
    "attn_bwd0": [("chip", ("w_up",)), ("pair", MIXER)],
    "attn_bwd1": [("chip", MIXER)],
    "g_w_in_b": [("pair", ("w_in_a",))],
    "g_w_in_c": [("pair", ("w_in_b",))],
    "d_proj_in": [("chip", ("w_in_a", "w_in_b")), ("pair", ("w_in_c",))],
    "pre_mix_norm_bwd": [("chip", ("w_in_c",))],
}
SHARD_ROWS = dict(w_in=D_MODEL, w_up=D_MODEL, w_down=D_FF // N_CHIPS, w_out=D_MODEL // N_CHIPS,
                  w_branch_attn=GROUP_W, w_branch_hgrn=HG_W)
PIECES = {n: ((n, 0, SHARD_ROWS[n]),) for n in BIG}
PIECES["w_in"] = (("w_in_a", 0, 512), ("w_in_b", 512, 256), ("w_in_c", 768, 256))


class _Exchange:
    def __init__(self, place, slots, conv_w_shard):
        self.place, self.slots, self.conv_w_shard = place, dict(slots), conv_w_shard
        self.conv_w = None
        self.g, self.from_sibling, self.pair_sums, self.arrived = {}, {}, {}, {}
        self.pending = []

    def weight(self, name):
        if name == "conv_w":
            return self.conv_w
        w = self.slots[name]
        return w.reshape(-1, D_MODEL) if name in ("w_out", "w_down") else w

    def grad(self, name, g):
        self.g[name] = g

    def carry(self, point):
        plans = []
        self.pending = SCHEDULE.get(point, [])
        for kind, names in self.pending:
            if kind in ("gather_ici", "gather_ici_cw"):
                wholes = [self.conv_w_shard] if kind == "gather_ici_cw" else []
                plans.append(_gather_ici_plan([self.slots[n] for n in names], wholes))
            elif kind == "gather_pass":
                plans.append(_gather_pass_plan([self.slots[n] for n in names]))
            elif kind == "pair":
                plans.append(_pair_plan([self.g[n] for n in names]))
            else:
                for n in names:
                    self.pair_sums[n] = _pair_sum(self.g[n], self.from_sibling[n], self.place[1:2], f"pair_sum_{n}")
                plans.append(_chip_plan([self.pair_sums[n] for n in names]))
        return plans

    def done(self, point, carried):
        for (kind, names), got in zip(self.pending, carried):
            if kind in ("gather_ici", "gather_ici_cw", "gather_pass"):
                self.slots.update(zip(names, got))
                if kind == "gather_ici_cw":
                    self.conv_w = got[len(names)].transpose(1, 0, 2).reshape(3, 2 * D_FF)
            elif kind == "pair":
                self.from_sibling.update(zip(names, got))
            else:
                self.arrived.update(zip(names, got))

    def reduced(self):
        shards = []
        for n in BIG:
            full = None
            for piece, row0, _ in PIECES[n]:
                full = _chip_sum(self.arrived[piece], self.pair_sums[piece], self.place, f"chip_sum_{piece}",
                                 row0, SHARD_ROWS[n], into=full)
            shards.append(full)
        pieces = [[(row0, rows) for _, row0, rows in PIECES[n]] for n in BIG]
        return dict(zip(BIG, _pair_concat(shards, pieces, "pair_concat")))


def kernel(x, pre_mix_norm, w_in, rel_bias, hgrn_lb_raw, hgrn_norm, w_branch_attn, w_branch_hgrn, w_out, post_mix_norm, pre_ffn_norm, w_up, conv_w, conv_b, w_down, post_ffn_norm, loss_target, m_pre_mix_norm, m_w_in, m_rel_bias, m_hgrn_lb_raw, m_hgrn_norm, m_w_branch_attn, m_w_branch_hgrn, m_w_out, m_post_mix_norm, m_pre_ffn_norm, m_w_up, m_conv_w, m_conv_b, m_w_down, m_post_ffn_norm, v_pre_mix_norm, v_w_in, v_rel_bias, v_hgrn_lb_raw, v_hgrn_norm, v_w_branch_attn, v_w_branch_hgrn, v_w_out, v_post_mix_norm, v_pre_ffn_norm, v_w_up, v_conv_w, v_conv_b, v_w_down, v_post_ffn_norm):
    w = dict(pre_mix_norm=pre_mix_norm, w_in=w_in, rel_bias=rel_bias, hgrn_lb_raw=hgrn_lb_raw, hgrn_norm=hgrn_norm,
             w_branch_attn=w_branch_attn, w_branch_hgrn=w_branch_hgrn, w_out=w_out, post_mix_norm=post_mix_norm,
             pre_ffn_norm=pre_ffn_norm, w_up=w_up, conv_w=conv_w, conv_b=conv_b, w_down=w_down,
             post_ffn_norm=post_ffn_norm)
    m = dict(pre_mix_norm=m_pre_mix_norm, w_in=m_w_in, rel_bias=m_rel_bias, hgrn_lb_raw=m_hgrn_lb_raw,
             hgrn_norm=m_hgrn_norm, w_branch_attn=m_w_branch_attn, w_branch_hgrn=m_w_branch_hgrn, w_out=m_w_out,
             post_mix_norm=m_post_mix_norm, pre_ffn_norm=m_pre_ffn_norm, w_up=m_w_up, conv_w=m_conv_w,
             conv_b=m_conv_b, w_down=m_w_down, post_ffn_norm=m_post_ffn_norm)
    v = dict(pre_mix_norm=v_pre_mix_norm, w_in=v_w_in, rel_bias=v_rel_bias, hgrn_lb_raw=v_hgrn_lb_raw,
             hgrn_norm=v_hgrn_norm, w_branch_attn=v_w_branch_attn, w_branch_hgrn=v_w_branch_hgrn, w_out=v_w_out,
             post_mix_norm=v_post_mix_norm, pre_ffn_norm=v_pre_ffn_norm, w_up=v_w_up, conv_w=v_conv_w,
             conv_b=v_conv_b, w_down=v_w_down, post_ffn_norm=v_post_ffn_norm)
    shard2d = {n: (w[n][0] if w[n].ndim == 3 else w[n]) for n in WEIGHTS}
    chip = 2 * lax.axis_index("x") + lax.axis_index("y")
    core = lax.axis_index("c")

    place = jnp.stack([chip, core]).astype(jnp.int32)
    slots = {n: _cast_into_slot(shard2d[n], place, f"cast_{n}") for n in BIG}
    slots["w_in"] = _gather_weights([slots["w_in"]], [], "gather_w_in")[0]
    ex = _Exchange(place, slots, shard2d["conv_w"])
    loss, grad_x, small = _local_step(x[0], loss_target[0], {n: w[n] for n in SMALL if n != "conv_w"}, ex)

    flat = [small[n].reshape(-1) for n in SMALL] + [loss.reshape(-1)]
    sizes = [t.shape[0] for t in flat]
    summed = _all_sum(jnp.concatenate(flat).reshape(-1, LANES), "sum_small").reshape(-1)
    offs = [sum(sizes[:i]) for i in range(len(sizes))]
    grads = {}
    for n, o, sz in zip(SMALL, offs, sizes):
        grads[n] = summed[o:o + sz].reshape(small[n].shape)
    loss_total = summed[offs[-1]]
    cw = 2 * D_FF // N_CHIPS
    grads["conv_w"] = lax.dynamic_slice(grads["conv_w"], (0, chip * cw), (3, cw))

    grads.update(ex.reduced())

    out_g, out_d, out_m, out_v = [], [], [], []
    for n in WEIGHTS:
        d2, m2, v2 = _adamw(shard2d[n], grads[n], m[n].reshape(shard2d[n].shape), v[n].reshape(shard2d[n].shape),
                            f"adamw_{n}")
        shape = w[n].shape
        out_g.append(grads[n].reshape(shape))
        out_d.append(d2.reshape(shape))
        out_m.append(m2.reshape(shape))
        out_v.append(v2.reshape(shape))
    return (loss_total, grad_x[None], *out_g, *out_d, *out_m, *out_v)
```

```python
import functools
import math

import jax
import jax.numpy as jnp
from jax import lax
from jax.experimental import pallas as pl
from jax.experimental.pallas import tpu as pltpu

F32 = jnp.float32
BF16 = jnp.bfloat16
MESH = pl.DeviceIdType.MESH

D_MODEL = 1024
N_GROUPS = 3
DILATIONS = (1, 4, 16)
HEADS = 8
HEAD_DIM = 64
GROUP_W = HEADS * HEAD_DIM
QKV_W = N_GROUPS * 3 * GROUP_W
BLK = 128
NEG_INF = -1e30
NUM_BUCKETS = 32
MAX_EXACT = 16
MAX_DISTANCE = 2048
HG_HEADS = 4
HG_DK = 128
HG_W = HG_HEADS * HG_DK
HG_CHUNK = 32
HG_TILE = 256
IN_W = QKV_W + 4 * HG_W + 2 * D_MODEL
D_FF = 2816
EPS = 1e-6
N_CHIPS = 4
N_DEV = 8
LANES = 128

ADAM_LR, ADAM_B1, ADAM_B2, ADAM_EPS, ADAM_WD, ADAM_STEP = 0.001, 0.9, 0.999, 1e-08, 0.01, 10

VMEM_LIMIT = 56 * 1024 * 1024


def _cp(n_axes):
    return pltpu.CompilerParams(dimension_semantics=("arbitrary",) * n_axes, vmem_limit_bytes=VMEM_LIMIT)


def _sds(shape, dtype):
    return jax.ShapeDtypeStruct(tuple(shape), dtype)


def _sigmoid(v):
    return 1.0 / (1.0 + jnp.exp(-v))


def _bf(v):
    return v.astype(BF16)


def _dot(a, b, dims):
    return lax.dot_general(a, b, (dims, ((), ())), preferred_element_type=F32)


NN = ((1,), (0,))
NT = ((1,), (1,))
TN = ((0,), (0,))

ANY = pl.BlockSpec(memory_space=pl.ANY)


class _Plan:
    def __init__(self, copies, n_sems, ins=(), inouts=(), outs=()):
        self.copies, self.n_sems = copies, n_sems
        self.ins, self.inouts, self.outs = list(ins), list(inouts), list(outs)


def _call(body, plans=None, *, name, grid, in_specs, out_specs, out_shape, args, scratch_shapes=()):
    plans = list(plans or ())
    in_specs, out_specs, out_shape = list(in_specs), list(out_specs), list(out_shape)
    scratch_shapes = list(scratch_shapes)
    n_in, n_out, n_scr = len(in_specs), len(out_specs), len(scratch_shapes)
    x_in, x_out, aliases, spans = [], [], {}, []
    for p in plans:
        i0, o0 = len(x_in), len(x_out)
        x_in += p.ins
        for a in p.inouts:
            aliases[n_in + len(x_in)] = n_out + len(x_out)
            x_in.append(a)
            x_out.append(_sds(a.shape, a.dtype))
        x_out += p.outs
        spans.append((i0, len(p.ins), o0, len(p.inouts), len(p.outs)))
    sems = [pltpu.SemaphoreType.DMA((p.n_sems,)) for p in plans for _ in range(3)]

    def wrapped(*refs):
        xi = refs[n_in:n_in + len(x_in)]
        base = n_in + len(x_in)
        xo = refs[base + n_out:base + n_out + len(x_out)]
        sbase = base + n_out + len(x_out)
        xs = refs[sbase + n_scr:]
        ids = [pl.program_id(k) for k in range(len(grid))]
        first = functools.reduce(jnp.logical_and, [i == 0 for i in ids])
        last = functools.reduce(jnp.logical_and, [i == g - 1 for i, g in zip(ids, grid)])

        def descriptors(k):
            i0, ni, o0, nio, no = spans[k]
            return plans[k].copies(xi[i0:i0 + ni], xo[o0:o0 + nio], xo[o0 + nio:o0 + nio + no], *xs[3 * k:3 * k + 3])

        @pl.when(first)
        def _():
            for k in range(len(plans)):
                sends, _, local = descriptors(k)
                for cp in (*sends, *local):
                    cp.start()

        body(*refs[:n_in], *refs[base:base + n_out], *refs[sbase:sbase + n_scr])

        @pl.when(last)
        def _():
            for k in range(len(plans)):
                sends, recvs, local = descriptors(k)
                for cp in recvs:
                    cp.wait_recv()
                for cp in sends:
                    cp.wait_send()
                for cp in local:
                    cp.wait()

    res = pl.pallas_call(
        wrapped if plans else body, name=name, grid=grid, in_specs=in_specs + [ANY] * len(x_in),
        out_specs=out_specs + [ANY] * len(x_out), out_shape=out_shape + x_out, input_output_aliases=aliases,
        scratch_shapes=scratch_shapes + sems, compiler_params=_cp(len(grid)))(*args, *x_in)
    res = list(res)
    carried = [res[n_out + o0:n_out + o0 + nio + no] for (_, _, o0, nio, no) in spans]
    return res[:n_out], carried


def _mm_nn_blk(a, wg, name, tm=512, plans=None):
    M, K = a.shape
    nb, _, Nb = wg.shape

    def body(a_ref, w_ref, o_ref):
        o_ref[...] = _dot(_bf(a_ref[...]), w_ref[...], NN)

    (out,), carried = _call(
        body, plans, name=name, grid=(nb, M // tm),
        in_specs=[pl.BlockSpec((tm, K), lambda j, i: (i, 0)), pl.BlockSpec((None, K, Nb), lambda j, i: (j, 0, 0))],
        out_specs=[pl.BlockSpec((tm, Nb), lambda j, i: (i, j))],
        out_shape=[_sds((M, nb * Nb), F32)], args=(a, wg))
    return out if plans is None else (out, carried)


def _mm_nt_blk(dy, wg, name, tm=1024, plans=None):
    M = dy.shape[0]
    nb, K, Nb = wg.shape

    def body(dy_ref, w_ref, o_ref):
        j = pl.program_id(1)
        r = _dot(_bf(dy_ref[...]), w_ref[...], NT)

        @pl.when(j == 0)
        def _():
            o_ref[...] = r

        @pl.when(j > 0)
        def _():
            o_ref[...] += r

    (out,), carried = _call(
        body, plans, name=name, grid=(M // tm, nb),
        in_specs=[pl.BlockSpec((tm, Nb), lambda i, j: (i, j)), pl.BlockSpec((None, K, Nb), lambda i, j: (j, 0, 0))],
        out_specs=[pl.BlockSpec((tm, K), lambda i, j: (i, 0))],
        out_shape=[_sds((M, K), F32)], args=(dy, wg))
    return out if plans is None else (out, carried)


def _mm_tn_blk(xt, dy, nb, name, tk=512, x_rows=None, plans=None):
    Mx, T = xt.shape
    xk, Mx = (0, Mx) if x_rows is None else x_rows
    Nb = dy.shape[1] // nb

    def body(x_ref, dy_ref, o_ref):
        t = pl.program_id(1)
        r = _dot(x_ref[...], _bf(dy_ref[...]), NN)

        @pl.when(t == 0)
        def _():
            o_ref[...] = r

        @pl.when(t > 0)
        def _():
            o_ref[...] += r

    (out,), carried = _call(
        body, plans, name=name, grid=(nb, T // tk),
        in_specs=[pl.BlockSpec((Mx, tk), lambda j, t: (xk, t)), pl.BlockSpec((tk, Nb), lambda j, t: (t, j))],
        out_specs=[pl.BlockSpec((None, Mx, Nb), lambda j, t: (j, 0, 0))],
        out_shape=[_sds((nb, Mx, Nb), F32)], args=(xt, dy))
    return out if plans is None else (out, carried)


def _mm_nn(a, w, name, tm=512):
    M, K = a.shape
    N = w.shape[1]

    def body(a_ref, w_ref, o_ref):
        o_ref[...] = _dot(_bf(a_ref[...]), w_ref[...], NN)

    return pl.pallas_call(
        body, name=name, grid=(M // tm,),
        in_specs=[pl.BlockSpec((tm, K), lambda i: (i, 0)), pl.BlockSpec((K, N), lambda i: (0, 0))],
        out_specs=pl.BlockSpec((tm, N), lambda i: (i, 0)),
        out_shape=_sds((M, N), F32), compiler_params=_cp(1))(a, w)


def _mm_nt(dy, w, name, tm=512):
    M, N = dy.shape
    K = w.shape[0]

    def body(dy_ref, w_ref, o_ref):
        o_ref[...] = _dot(_bf(dy_ref[...]), w_ref[...], NT)

    return pl.pallas_call(
        body, name=name, grid=(M // tm,),
        in_specs=[pl.BlockSpec((tm, N), lambda i: (i, 0)), pl.BlockSpec((K, N), lambda i: (0, 0))],
        out_specs=pl.BlockSpec((tm, K), lambda i: (i, 0)),
        out_shape=_sds((M, K), F32), compiler_params=_cp(1))(dy, w)


def _mm_tn(xt, dy, name, tk=512):
    Mx, T = xt.shape
    N = dy.shape[1]

    def body(x_ref, dy_ref, o_ref):
        t = pl.program_id(0)
        r = _dot(x_ref[...], _bf(dy_ref[...]), NN)

        @pl.when(t == 0)
        def _():
            o_ref[...] = r

        @pl.when(t > 0)
        def _():
            o_ref[...] += r

    return pl.pallas_call(
        body, name=name, grid=(T // tk,),
        in_specs=[pl.BlockSpec((Mx, tk), lambda t: (0, t)), pl.BlockSpec((tk, N), lambda t: (t, 0))],
        out_specs=pl.BlockSpec((Mx, N), lambda t: (0, 0)),
        out_shape=_sds((Mx, N), F32), compiler_params=_cp(1))(xt, dy)


def _tile(arr, bw, col=lambda c: 0):
    return ("tile", arr, bw, col)


def _full(arr):
    return ("full", arr)


def _out_tile(width, dtype, bw, col=lambda c: 0):
    return ("tile", width, dtype, bw, col)


def _out_acc(rows, width, bw, col=lambda c: 0):
    return ("acc", rows, width, bw, col)


def _rows_call(name, body, n_rows, tm, ncol, ins, outs, plans=None):
    in_specs, args = [], []
    for e in ins:
        if e[0] == "tile":
            _, arr, bw, col = e
            in_specs.append(pl.BlockSpec((tm, bw), functools.partial(lambda c, i, col: (i, col(c)), col=col)))
        else:
            arr = e[1]
            in_specs.append(pl.BlockSpec(arr.shape, functools.partial(lambda c, i, nd: (0,) * nd, nd=arr.ndim)))
        args.append(arr)
    out_specs, out_shape = [], []
    for e in outs:
        if e[0] == "tile":
            _, width, dtype, bw, col = e
            out_specs.append(pl.BlockSpec((tm, bw), functools.partial(lambda c, i, col: (i, col(c)), col=col)))
            out_shape.append(_sds((n_rows, width), dtype))
        else:
            _, rows, width, bw, col = e
            out_specs.append(pl.BlockSpec((rows, bw), functools.partial(lambda c, i, col: (0, col(c)), col=col)))
            out_shape.append(_sds((rows, width), F32))
    out, carried = _call(body, plans, name=name, grid=(ncol, n_rows // tm), in_specs=in_specs, out_specs=out_specs,
                         out_shape=out_shape, args=args)
    return out if plans is None else (out, carried)


def _acc(ref, val):
    i = pl.program_id(1)

    @pl.when(i == 0)
    def _():
        ref[...] = val

    @pl.when(i > 0)
    def _():
        ref[...] += val


def _rinv(z):
    return lax.rsqrt(jnp.mean(z * z, axis=-1, keepdims=True) + EPS)


def _norm_bwd(dy, zhat, r, w):
    dyw = dy * w
    return r * (dyw - zhat * jnp.mean(dyw * zhat, axis=-1, keepdims=True))


def _norm_fwd(x, w, name):
    def body(x_ref, w_ref, h_ref):
        xv = x_ref[...]
        h_ref[...] = _bf(xv * _rinv(xv) * w_ref[...])

    return _rows_call(name, body, x.shape[0], 512, 1, [_tile(x, D_MODEL), _full(w)],
                      [_out_tile(D_MODEL, BF16, D_MODEL)])[0]


def _prenorm_bwd(dh, xin, w, dres, name, plans=None):
    def body(dh_ref, x_ref, w_ref, dres_ref, dx_ref, dw_ref):
        xv = x_ref[...]
        r = _rinv(xv)
        xhat = xv * r
        dhv = dh_ref[...]
        dx_ref[...] = dres_ref[...] + _norm_bwd(dhv, xhat, r, w_ref[...])
        _acc(dw_ref, jnp.sum(dhv * xhat, axis=0, keepdims=True))

    return _rows_call(name, body, xin.shape[0], 512, 1,
                      [_tile(dh, D_MODEL), _tile(xin, D_MODEL), _full(w), _tile(dres, D_MODEL)],
                      [_out_tile(D_MODEL, F32, D_MODEL), _out_acc(1, D_MODEL, D_MODEL)], plans)


def _postnorm_bwd(dout, z, w, name):
    def body(do_ref, z_ref, w_ref, dz_ref, dw_ref):
        zv = z_ref[...]
        r = _rinv(zv)
        zhat = zv * r
        dov = do_ref[...]
        dz_ref[...] = _bf(_norm_bwd(dov, zhat, r, w_ref[...]))
        _acc(dw_ref, jnp.sum(dov * zhat, axis=0, keepdims=True))

    return _rows_call(name, body, z.shape[0], 512, 1, [_tile(dout, D_MODEL), _tile(z, D_MODEL), _full(w)],
                      [_out_tile(D_MODEL, BF16, D_MODEL), _out_acc(1, D_MODEL, D_MODEL)])


def _t5_bucket(dist):
    n = jnp.maximum(dist, 0)
    nf = jnp.maximum(n, 1).astype(F32)
    large = MAX_EXACT + (jnp.log(nf / MAX_EXACT) / math.log(MAX_DISTANCE / MAX_EXACT)
                         * (NUM_BUCKETS - MAX_EXACT)).astype(jnp.int32)
    large = jnp.minimum(large, NUM_BUCKETS - 1)
    return jnp.where(n < MAX_EXACT, n, large)


def _band_rel():
    return jnp.arange(BLK)[:, None] + BLK - jnp.arange(2 * BLK)[None, :]


def _band_mask(n):
    row = lax.broadcasted_iota(jnp.int32, (BLK, 2 * BLK), 0)
    col = lax.broadcasted_iota(jnp.int32, (BLK, 2 * BLK), 1)
    rel = row + BLK - col
    return (rel >= 0) & (rel <= BLK) & ((col >= BLK) | (n > 0))


RES_UNROLL = 4


def _heads_per_step(d):
    return HEADS if d == 1 else LANES // HEAD_DIM


def _sub_rows(r, d):
    return pl.ds(r, BLK, stride=d) if d > 1 else pl.ds(0, BLK)


def _for_residues(d, fn):
    if d <= RES_UNROLL:
        for r in range(d):
            fn(r)
    else:
        def group(i, carry):
            for k in range(RES_UNROLL):
                fn(i * RES_UNROLL + k)
            return carry

        lax.fori_loop(0, d // RES_UNROLL, group, 0)


def _attn_specs(d, g, qblock):
    cw = _heads_per_step(d) * HEAD_DIM

    def col(part, hp):
        return (g * 3 + part) * (GROUP_W // cw) + hp

    def cur(part):
        return pl.BlockSpec((d * BLK, cw), lambda hp, n: (qblock(n), col(part, hp)))

    def prev(part):
        return pl.BlockSpec((d * BLK, cw), lambda hp, n: (jnp.maximum(qblock(n) - 1, 0), col(part, hp)))

    return cur, prev


def _attn_fwd(proj, bias, g, name, plans=None):
    S = proj.shape[0]
    d = DILATIONS[g]
    NB = S // (d * BLK)
    hps = _heads_per_step(d)

    def body(q_ref, kp_ref, kc_ref, vp_ref, vc_ref, b_ref, o_ref, lse_ref):
        hp = pl.program_id(0)
        mask = _band_mask(pl.program_id(1))

        def residue(r):
            rows = _sub_rows(r, d)
            q2 = q_ref[rows, :]
            k2 = jnp.concatenate([kp_ref[rows, :], kc_ref[rows, :]], axis=0)
            v2 = jnp.concatenate([vp_ref[rows, :], vc_ref[rows, :]], axis=0)
            outs, lses = [], []
            for hh in range(hps):
                hs = slice(hh * HEAD_DIM, (hh + 1) * HEAD_DIM)
                s = _dot(_bf(q2[:, hs]), _bf(k2[:, hs]), NT) * (HEAD_DIM ** -0.5) + b_ref[hp * hps + hh]
                s = jnp.where(mask, s, NEG_INF)
                m = jnp.max(s, axis=-1, keepdims=True)
                p = jnp.exp(s - m)
                l = jnp.sum(p, axis=-1, keepdims=True)
                outs.append(_dot(_bf(p), _bf(v2[:, hs]), NN) / l)
                lses.append(jnp.broadcast_to(m + jnp.log(l), (BLK, HEAD_DIM)))
            o_ref[rows, :] = jnp.concatenate(outs, axis=1)
            lse_ref[rows, :] = jnp.concatenate(lses, axis=1)

        _for_residues(d, residue)

    cur, prev = _attn_specs(d, g, lambda n: n)
    out = pl.BlockSpec((d * BLK, hps * HEAD_DIM), lambda hp, n: (n, hp))
    res, carried = _call(
        body, plans, name=name, grid=(HEADS // hps, NB),
        in_specs=[cur(0), prev(1), cur(1), prev(2), cur(2),
                  pl.BlockSpec((HEADS, BLK, 2 * BLK), lambda hp, n: (0, 0, 0))],
        out_specs=[out, out], out_shape=[_sds((S, GROUP_W), F32)] * 2,
        args=(proj, proj, proj, proj, proj, bias))
    return res if plans is None else (res, carried)


def _attn_merge(os_, lses, name):
    def body(o0, o1, o2, l0, l1, l2, y_ref, lse_ref):
        a, b, c = l0[...], l1[...], l2[...]
        m = jnp.maximum(jnp.maximum(a, b), c)
        ea, eb, ec = jnp.exp(a - m), jnp.exp(b - m), jnp.exp(c - m)
        den = ea + eb + ec
        y_ref[...] = (ea * o0[...] + eb * o1[...] + ec * o2[...]) / den
        lse_ref[...] = m + jnp.log(den)

    S = os_[0].shape[0]
    return _rows_call(name, body, S, 512, 1, [_tile(t, GROUP_W) for t in (*os_, *lses)],
                      [_out_tile(GROUP_W, F32, GROUP_W)] * 2)


def _attn_bwd(proj, bias, lse, y, dy, g, name, plans=None):
    S = proj.shape[0]
    d = DILATIONS[g]
    NB = S // (d * BLK)
    hps = _heads_per_step(d)

    def body(q_ref, kp_ref, kc_ref, vp_ref, vc_ref, b_ref, l_ref, y_ref, dy_ref,
             dq_ref, dk_ref, dv_ref, db_ref, ck_ref, cv_ref):
        hp, n = pl.program_id(0), pl.program_id(1)

        @pl.when((hp == 0) & (n == 0))
        def _():
            db_ref[...] = jnp.zeros_like(db_ref)

        @pl.when(n == 0)
        def _():
            ck_ref[...] = jnp.zeros_like(ck_ref)
            cv_ref[...] = jnp.zeros_like(cv_ref)

        @pl.when(n < NB)
        def _():
            mask = _band_mask(n)

            def residue(r):
                rows = _sub_rows(r, d)
                q2 = q_ref[rows, :]
                k2 = jnp.concatenate([kp_ref[rows, :], kc_ref[rows, :]], axis=0)
                v2 = jnp.concatenate([vp_ref[rows, :], vc_ref[rows, :]], axis=0)
                l2, y2, dy2 = l_ref[rows, :], y_ref[rows, :], dy_ref[rows, :]
                dqs, dks, dvs = [], [], []
                for hh in range(hps):
                    hs = slice(hh * HEAD_DIM, (hh + 1) * HEAD_DIM)
                    q, kb, vb = _bf(q2[:, hs]), _bf(k2[:, hs]), _bf(v2[:, hs])
                    s = _dot(q, kb, NT) * (HEAD_DIM ** -0.5) + b_ref[hp * hps + hh]
                    s = jnp.where(mask, s, NEG_INF)
                    p = jnp.exp(s - l2[:, hh * HEAD_DIM:hh * HEAD_DIM + 1])
                    dyh = dy2[:, hs]
                    delta = jnp.sum(dyh * y2[:, hs], axis=-1, keepdims=True)
                    dyb = _bf(dyh)
                    ds = p * (_dot(dyb, vb, NT) - delta)
                    db_ref[hp * hps + hh] += ds
                    dsb = _bf(ds * (HEAD_DIM ** -0.5))
                    dqs.append(_dot(dsb, kb, NN))
                    dks.append(_dot(dsb, q, TN))
                    dvs.append(_dot(_bf(p), dyb, TN))
                dkb = jnp.concatenate(dks, axis=1)
                dvb = jnp.concatenate(dvs, axis=1)
                dq_ref[rows, :] = jnp.concatenate(dqs, axis=1)
                dk_ref[rows, :] = ck_ref[rows, :] + dkb[:BLK]
                dv_ref[rows, :] = cv_ref[rows, :] + dvb[:BLK]
                ck_ref[rows, :] = dkb[BLK:]
                cv_ref[rows, :] = dvb[BLK:]

            _for_residues(d, residue)

        @pl.when(n == NB)
        def _():
            dk_ref[...] = ck_ref[...]
            dv_ref[...] = cv_ref[...]

    def qn(n):
        return jnp.minimum(n, NB - 1)

    cur, prev = _attn_specs(d, g, qn)
    cw = hps * HEAD_DIM
    row = pl.BlockSpec((d * BLK, cw), lambda hp, n: (qn(n), hp))
    done = pl.BlockSpec((d * BLK, cw), lambda hp, n: (jnp.maximum(n - 1, 0), hp))
    (dq, dk, dv, db), carried = _call(
        body, plans, name=name, grid=(HEADS // hps, NB + 1),
        in_specs=[cur(0), prev(1), cur(1), prev(2), cur(2),
                  pl.BlockSpec((HEADS, BLK, 2 * BLK), lambda hp, n: (0, 0, 0)), row, row, row],
        out_specs=[row, done, done, pl.BlockSpec((HEADS, BLK, 2 * BLK), lambda hp, n: (0, 0, 0))],
        out_shape=[_sds((S, GROUP_W), F32)] * 3 + [_sds((HEADS, BLK, 2 * BLK), F32)],
        scratch_shapes=[pltpu.VMEM((d * BLK, cw), F32)] * 2,
        args=(proj, proj, proj, proj, proj, bias, lse, y, dy))
    return ([dq, dk, dv], db) if plans is None else ([dq, dk, dv], db, carried)


BAND = BLK * 2 * BLK


def _bucket_onehot():
    buckets = jnp.stack([_t5_bucket(_band_rel() * d) for d in DILATIONS]).reshape(N_GROUPS, 1, BAND)
    return (buckets == jnp.arange(NUM_BUCKETS).reshape(1, NUM_BUCKETS, 1)).astype(F32)


def _relbias_fwd(rel_bias, name):
    table = rel_bias.reshape(NUM_BUCKETS, N_GROUPS, HEADS).transpose(1, 0, 2)

    def body(t_ref, oh_ref, o_ref):
        o_ref[...] = lax.dot_general(t_ref[...], oh_ref[...], (TN, ((), ())), preferred_element_type=F32,
                                     precision=lax.Precision.HIGHEST)

    out = pl.pallas_call(
        body, name=name, grid=(N_GROUPS,),
        in_specs=[pl.BlockSpec((None, NUM_BUCKETS, HEADS), lambda g: (g, 0, 0)),
                  pl.BlockSpec((None, NUM_BUCKETS, BAND), lambda g: (g, 0, 0))],
        out_specs=pl.BlockSpec((None, HEADS, BAND), lambda g: (g, 0, 0)),
        out_shape=_sds((N_GROUPS, HEADS, BAND), F32), compiler_params=_cp(1))(table, _bucket_onehot())
    return out.reshape(N_GROUPS, HEADS, BLK, 2 * BLK)


def _relbias_bwd(dbs, name):
    band = BAND
    onehot = _bucket_onehot()
    dbf = jnp.stack([db.reshape(HEADS, band) for db in dbs])

    def body(oh_ref, db_ref, o_ref):
        o_ref[...] = lax.dot_general(oh_ref[...], db_ref[...], (NT, ((), ())), preferred_element_type=F32,
                                     precision=lax.Precision.HIGHEST)

    out = pl.pallas_call(
        body, name=name, grid=(N_GROUPS,),
        in_specs=[pl.BlockSpec((None, NUM_BUCKETS, band), lambda g: (g, 0, 0)),
                  pl.BlockSpec((None, HEADS, band), lambda g: (g, 0, 0))],
        out_specs=pl.BlockSpec((None, NUM_BUCKETS, HEADS), lambda g: (g, 0, 0)),
        out_shape=_sds((N_GROUPS, NUM_BUCKETS, HEADS), F32), compiler_params=_cp(1))(onehot, dbf)
    return out.transpose(1, 0, 2).reshape(NUM_BUCKETS, N_GROUPS * HEADS)


def _chunk_pos(shape):
    return lax.broadcasted_iota(jnp.int32, shape, 0) % HG_CHUNK


def _chunk_cumsum(v):
    pos = _chunk_pos(v.shape)
    s = 1
    while s < HG_CHUNK:
        v = v + jnp.where(pos >= s, pltpu.roll(v, s, 0), 0.0)
        s *= 2
    return v


def _chunk_rev_cumsum(v):
    pos = _chunk_pos(v.shape)
    n = v.shape[0]
    s = 1
    while s < HG_CHUNK:
        v = v + jnp.where(pos < HG_CHUNK - s, pltpu.roll(v, n - s, 0), 0.0)
        s *= 2
    return v


def _lower_bound(raw):
    a0, a1 = raw[0:1], raw[1:2]
    m = jnp.maximum(a0, a1)
    e0, e1 = jnp.exp(a0 - m), jnp.exp(a1 - m)
    return e0 / (e0 + e1)


def _hg_gates(qr, fr, lb):
    sf = _sigmoid(fr)
    f = lb + (1.0 - lb) * sf
    sq = _sigmoid(qr)
    return qr * sq, sq, f, sf


HG_COL0 = QKV_W // HG_W


def _hgrn_fwd(proj, lb_raw, nw, name):
    S = proj.shape[0]
    ncs = HG_TILE // HG_CHUNK
    tril = jnp.tril(jnp.ones((HG_CHUNK, HG_CHUNK), dtype=bool))

    def body(q_ref, f_ref, i_ref, og_ref, lb_ref, nw_ref, y_ref, o_ref, st_ref, state):
        @pl.when(pl.program_id(0) == 0)
        def _():
            state[...] = jnp.zeros_like(state)

        lb = _lower_bound(lb_ref[...])
        q, _, f, _ = _hg_gates(q_ref[...], f_ref[...], lb)
        k = 1.0 - f
        G = _chunk_cumsum(jnp.log(f))
        row = lax.broadcasted_iota(jnp.int32, (HG_CHUNK, HG_CHUNK), 0)
        col = lax.broadcasted_iota(jnp.int32, (HG_CHUNK, HG_CHUNK), 1)
        heads = [slice(h * HG_DK, (h + 1) * HG_DK) for h in range(HG_HEADS)]
        sts = [state[h] for h in range(HG_HEADS)]
        for c in range(ncs):
            cs = slice(c * HG_CHUNK, (c + 1) * HG_CHUNK)
            for h, hs in enumerate(heads):
                Gc = G[cs, hs]
                gl = Gc[HG_CHUNK - 1:HG_CHUNK]
                qt = _bf(q[cs, hs] * jnp.exp(Gc))
                kt = _bf(k[cs, hs] * jnp.exp(-Gc))
                kd = _bf(k[cs, hs] * jnp.exp(gl - Gc))
                v = _bf(i_ref[cs, hs])
                A = jnp.where(row >= col, _dot(qt, kt, NT), 0.0)
                o_ref[cs, hs] = _dot(_bf(A), v, NN) + _dot(qt, _bf(sts[h]), NT)
                st_ref[c, h] = sts[h]
                sts[h] = sts[h] * jnp.exp(gl) + _dot(v, kd, TN)
        for h, hs in enumerate(heads):
            state[h] = sts[h]
            oh = o_ref[:, hs]
            og = og_ref[:, hs]
            y_ref[:, hs] = oh * _rinv(oh) * nw_ref[...] * (og * _sigmoid(og))

    def colspec(j):
        return pl.BlockSpec((HG_TILE, HG_W), lambda i: (i, HG_COL0 + j))

    return pl.pallas_call(
        body, name=name, grid=(S // HG_TILE,),
        in_specs=[colspec(0), colspec(1), colspec(2), colspec(3),
                  pl.BlockSpec((2, HG_W), lambda i: (0, 0)), pl.BlockSpec((1, HG_DK), lambda i: (0, 0))],
        out_specs=[pl.BlockSpec((HG_TILE, HG_W), lambda i: (i, 0))] * 2
        + [pl.BlockSpec((ncs, HG_HEADS, HG_DK, HG_DK), lambda i: (i, 0, 0, 0))],
        out_shape=[_sds((S, HG_W), F32)] * 2 + [_sds((S // HG_CHUNK, HG_HEADS, HG_DK, HG_DK), F32)],
        scratch_shapes=[pltpu.VMEM((HG_HEADS, HG_DK, HG_DK), F32)],
        compiler_params=_cp(1))(proj, proj, proj, proj, lb_raw, nw)


def _hgrn_bwd(proj, lb_raw, nw, o, states, dy, name):
    S = proj.shape[0]
    ncs = HG_TILE // HG_CHUNK
    nt = S // HG_TILE

    def body(q_ref, f_ref, i_ref, og_ref, lb_ref, nw_ref, o_ref, st_ref, dy_ref,
             dq_ref, df_ref, di_ref, dog_ref, dlb_ref, dnw_ref, dstate, do_s, dG_s, dgl_s, dk_s, dlb_s):
        step = pl.program_id(0)

        @pl.when(step == 0)
        def _():
            dstate[...] = jnp.zeros_like(dstate)
            dlb_s[...] = jnp.zeros_like(dlb_s)
            dnw_ref[...] = jnp.zeros_like(dnw_ref)

        lb = _lower_bound(lb_ref[...])
        qr = q_ref[...]
        q, sq, f, sf = _hg_gates(qr, f_ref[...], lb)
        k = 1.0 - f
        G = _chunk_cumsum(jnp.log(f))
        nwv = nw_ref[...]
        row = lax.broadcasted_iota(jnp.int32, (HG_CHUNK, HG_CHUNK), 0)
        col = lax.broadcasted_iota(jnp.int32, (HG_CHUNK, HG_CHUNK), 1)
        for h in range(HG_HEADS):
            hs = slice(h * HG_DK, (h + 1) * HG_DK)
            oh = o_ref[:, hs]
            r = _rinv(oh)
            ohat = oh * r
            og = og_ref[:, hs]
            sg = _sigmoid(og)
            dyh = dy_ref[:, hs]
            don = dyh * (og * sg)
            dog_ref[:, hs] = _bf(dyh * (ohat * nwv) * (sg * (1.0 + og * (1.0 - sg))))
            dnw_ref[...] += jnp.sum(don * ohat, axis=0, keepdims=True)
            do_s[:, hs] = _norm_bwd(don, ohat, r, nwv)
        dsts = [dstate[h] for h in range(HG_HEADS)]
        for c in reversed(range(ncs)):
            cs = slice(c * HG_CHUNK, (c + 1) * HG_CHUNK)
            for h in range(HG_HEADS):
                hs = slice(h * HG_DK, (h + 1) * HG_DK)
                dst = dsts[h]
                Gc = G[cs, hs]
                gl = Gc[HG_CHUNK - 1:HG_CHUNK]
                eG, enG, edG, egl = jnp.exp(Gc), jnp.exp(-Gc), jnp.exp(gl - Gc), jnp.exp(gl)
                qt, kt, kd = q[cs, hs] * eG, k[cs, hs] * enG, k[cs, hs] * edG
                qtb, ktb, kdb = _bf(qt), _bf(kt), _bf(kd)
                v = _bf(i_ref[cs, hs])
                do = _bf(do_s[cs, hs])
                st = st_ref[c, h]
                dstb = _bf(dst)
                A = jnp.where(row >= col, _dot(qtb, ktb, NT), 0.0)
                dA = _bf(jnp.where(row >= col, _dot(do, v, NT), 0.0))
                di_ref[cs, hs] = _bf(_dot(_bf(A), do, TN) + _dot(kdb, dstb, NT))
                dqt = _dot(dA, ktb, NN) + _dot(do, _bf(st), NN)
                dkt = _dot(dA, qtb, TN)
                dkd = _dot(v, dstb, NN)
                dgl = egl * jnp.sum(st * dst, axis=0, keepdims=True) + jnp.sum(dkd * kd, axis=0, keepdims=True)
                dsts[h] = dst * egl + _dot(do, qtb, TN)
                dq_ref[cs, hs] = _bf(dqt * eG * (sq[cs, hs] * (1.0 + qr[cs, hs] * (1.0 - sq[cs, hs]))))
                dk_s[cs, hs] = dkt * enG + dkd * edG
                dG_s[cs, hs] = dqt * qt - dkt * kt - dkd * kd
                dgl_s[cs, hs] = jnp.broadcast_to(dgl, (HG_CHUNK, HG_DK))
        for h in range(HG_HEADS):
            dstate[h] = dsts[h]
        dg = _chunk_rev_cumsum(dG_s[...]) + dgl_s[...]
        dfv = dg / f - dk_s[...]
        df_ref[...] = _bf(dfv * (1.0 - lb) * sf * (1.0 - sf))
        dlb_s[...] += jnp.sum(dfv * (1.0 - sf), axis=0, keepdims=True)

        @pl.when(step == nt - 1)
        def _():
            t = dlb_s[...] * lb * (1.0 - lb)
            dlb_ref[...] = jnp.concatenate([t, -t], axis=0)

    def colspec(j):
        return pl.BlockSpec((HG_TILE, HG_W), lambda i: (nt - 1 - i, HG_COL0 + j))

    tile = pl.BlockSpec((HG_TILE, HG_W), lambda i: (nt - 1 - i, 0))
    outs = pl.pallas_call(
        body, name=name, grid=(nt,),
        in_specs=[colspec(0), colspec(1), colspec(2), colspec(3),
                  pl.BlockSpec((2, HG_W), lambda i: (0, 0)), pl.BlockSpec((1, HG_DK), lambda i: (0, 0)),
                  tile, pl.BlockSpec((ncs, HG_HEADS, HG_DK, HG_DK), lambda i: (nt - 1 - i, 0, 0, 0)), tile],
        out_specs=[tile] * 4 + [pl.BlockSpec((2, HG_W), lambda i: (0, 0)), pl.BlockSpec((1, HG_DK), lambda i: (0, 0))],
        out_shape=[_sds((S, HG_W), BF16)] * 4 + [_sds((2, HG_W), F32), _sds((1, HG_DK), F32)],
        scratch_shapes=[pltpu.VMEM((HG_HEADS, HG_DK, HG_DK), F32)] + [pltpu.VMEM((HG_TILE, HG_W), F32)] * 4
        + [pltpu.VMEM((1, HG_W), F32)],
        compiler_params=_cp(1))(proj, proj, proj, proj, lb_raw, nw, o, states, dy)
    return outs[:4], outs[4], outs[5]


GATE_COL0 = (QKV_W + 4 * HG_W) // GROUP_W
HALF_D = D_MODEL // 2


def _gate_tiles(proj):
    return [_tile(proj, HALF_D, lambda c: GATE_COL0 + c), _tile(proj, HALF_D, lambda c: GATE_COL0 + 2 + c)]


def _merge_fwd(za, zh, proj, name):
    def body(za_ref, zh_ref, g0_ref, g1_ref, m_ref):
        m_ref[...] = _bf(_sigmoid(g0_ref[...]) * za_ref[...] + _sigmoid(g1_ref[...]) * zh_ref[...])

    col = lambda c: c
    return _rows_call(name, body, za.shape[0], 512, 2,
                      [_tile(za, HALF_D, col), _tile(zh, HALF_D, col), *_gate_tiles(proj)],
                      [_out_tile(D_MODEL, BF16, HALF_D, col)])[0]


def _merge_bwd(dm, za, zh, proj, name):
    def body(dm_ref, za_ref, zh_ref, g0_ref, g1_ref, dza_ref, dzh_ref, dg0_ref, dg1_ref):
        dmv = dm_ref[...]
        s0, s1 = _sigmoid(g0_ref[...]), _sigmoid(g1_ref[...])
        dza_ref[...] = _bf(dmv * s0)
        dzh_ref[...] = _bf(dmv * s1)
        dg0_ref[...] = _bf(dmv * za_ref[...] * s0 * (1.0 - s0))
        dg1_ref[...] = _bf(dmv * zh_ref[...] * s1 * (1.0 - s1))

    col = lambda c: c
    return _rows_call(name, body, za.shape[0], 512, 2,
                      [_tile(dm, HALF_D, col), _tile(za, HALF_D, col), _tile(zh, HALF_D, col), *_gate_tiles(proj)],
                      [_out_tile(D_MODEL, BF16, HALF_D, col)] * 4)


def _mix_out(mo, x, w_post, w_pre, name):
    def body(mo_ref, x_ref, wp_ref, wf_ref, x1_ref, h2_ref):
        z = mo_ref[...]
        x1 = x_ref[...] + z * _rinv(z) * wp_ref[...]
        x1_ref[...] = x1
        h2_ref[...] = _bf(x1 * _rinv(x1) * wf_ref[...])

    return _rows_call(name, body, x.shape[0], 512, 1,
                      [_tile(mo, D_MODEL), _tile(x, D_MODEL), _full(w_post), _full(w_pre)],
                      [_out_tile(D_MODEL, F32, D_MODEL), _out_tile(D_MODEL, BF16, D_MODEL)])


def _loss_head(ffo, x1, tgt, w, name):
    def body(f_ref, x1_ref, t_ref, w_ref, dx_ref, df_ref, dw_ref, loss_ref):
        z = f_ref[...]
        r = _rinv(z)
        zhat = z * r
        wv = w_ref[...]
        e = x1_ref[...] + zhat * wv - t_ref[...]
        dx = e * (1.0 / D_MODEL)
        dx_ref[...] = dx
        df_ref[...] = _bf(_norm_bwd(dx, zhat, r, wv))
        _acc(dw_ref, jnp.sum(dx * zhat, axis=0, keepdims=True))
        part = 0.5 * jnp.sum(jnp.sum(e * e, axis=1, keepdims=True), axis=0, keepdims=True) * (1.0 / D_MODEL)
        _acc(loss_ref, jnp.broadcast_to(part, (1, LANES)))

    return _rows_call(name, body, x1.shape[0], 512, 1,
                      [_tile(ffo, D_MODEL), _tile(x1, D_MODEL), _tile(tgt, D_MODEL), _full(w)],
                      [_out_tile(D_MODEL, F32, D_MODEL), _out_tile(D_MODEL, BF16, D_MODEL),
                       _out_acc(1, D_MODEL, D_MODEL), _out_acc(1, LANES, LANES)])


CONV_CB = D_FF // 2
CONV_TM = 512
HALO = 8
SQRT_HALF = 0.7071067811865476
INV_SQRT_2PI = 0.3989422804014327


def _conv_taps(u_ref, halo_ref, first):
    u = u_ref[...]
    row = lax.broadcasted_iota(jnp.int32, u.shape, 0)
    p1 = jnp.where(first, 0.0, halo_ref[HALO - 1:HALO, :])
    p2 = jnp.where(first, 0.0, halo_ref[HALO - 2:HALO - 1, :])
    u1 = jnp.where(row == 0, p1, pltpu.roll(u, 1, 0))
    u2 = jnp.where(row == 0, p2, jnp.where(row == 1, p1, pltpu.roll(u, 2, 0)))
    return u2, u1, u


def _conv(taps, w_ref, b_ref):
    return b_ref[...] + w_ref[0:1, :] * taps[0] + w_ref[1:2, :] * taps[1] + w_ref[2:3, :] * taps[2]


def _conv_specs(tm):
    nh = tm // HALO
    nc = D_FF // CONV_CB

    def tile(off):
        return pl.BlockSpec((tm, CONV_CB), lambda c, i: (i, off + c))

    def halo(off):
        return pl.BlockSpec((HALO, CONV_CB), lambda c, i: (jnp.maximum(i * nh - 1, 0), off + c))

    def small(rows, off):
        return pl.BlockSpec((rows, CONV_CB), lambda c, i: (0, off + c))

    return nc, tile, halo, small


def _conv_gelu_fwd(u, cw, cb, name):
    S = u.shape[0]
    tm = CONV_TM
    nc, tile, halo, small = _conv_specs(tm)

    def body(ug, hg, uv, hv, wg, wv, bg, bv, a_ref):
        first = pl.program_id(1) == 0
        cg = _conv(_conv_taps(ug, hg, first), wg, bg)
        cv = _conv(_conv_taps(uv, hv, first), wv, bv)
        a_ref[...] = _bf(0.5 * cg * (1.0 + lax.erf(cg * SQRT_HALF)) * cv)

    return pl.pallas_call(
        body, name=name, grid=(nc, S // tm),
        in_specs=[tile(0), halo(0), tile(nc), halo(nc), small(3, 0), small(3, nc), small(1, 0), small(1, nc)],
        out_specs=tile(0), out_shape=_sds((S, D_FF), BF16), compiler_params=_cp(2))(u, u, u, u, cw, cw, cb, cb)


def _conv_gelu_bwd(u, da, cw, cb, name, plans=None):
    S = u.shape[0]
    tm = CONV_TM
    nc, tile, halo, small = _conv_specs(tm)

    def body(ug, hg, uv, hv, wg, wv, bg, bv, da_ref, dcg_ref, dcv_ref, dwg_ref, dwv_ref, dbg_ref, dbv_ref):
        first = pl.program_id(1) == 0
        tg = _conv_taps(ug, hg, first)
        tv = _conv_taps(uv, hv, first)
        cg = _conv(tg, wg, bg)
        cv = _conv(tv, wv, bv)
        phi = 0.5 * (1.0 + lax.erf(cg * SQRT_HALF))
        dav = da_ref[...]
        dcg = dav * cv * (phi + cg * jnp.exp(-0.5 * cg * cg) * INV_SQRT_2PI)
        dcv = dav * (cg * phi)
        dcg_ref[...] = dcg
        dcv_ref[...] = dcv
        for dc, taps, dw_ref, db_ref in ((dcg, tg, dwg_ref, dbg_ref), (dcv, tv, dwv_ref, dbv_ref)):
            _acc(db_ref, jnp.sum(dc, axis=0, keepdims=True))
            for j in range(3):
                _acc(dw_ref.at[j:j + 1, :], jnp.sum(dc * taps[j], axis=0, keepdims=True))

    res, carried = _call(
        body, plans, name=name, grid=(nc, S // tm),
        in_specs=[tile(0), halo(0), tile(nc), halo(nc), small(3, 0), small(3, nc), small(1, 0), small(1, nc), tile(0)],
        out_specs=[tile(0), tile(0), small(3, 0), small(3, 0), small(1, 0), small(1, 0)],
        out_shape=[_sds((S, D_FF), F32)] * 2 + [_sds((3, D_FF), F32)] * 2 + [_sds((1, D_FF), F32)] * 2,
        args=(u, u, u, u, cw, cw, cb, cb, da))
    return res if plans is None else (res, carried)


def _conv_input_bwd(dcg, dcv, cw, name, plans=None):
    S = dcg.shape[0]
    tm = CONV_TM
    nc, tile, _, small = _conv_specs(tm)
    nh = tm // HALO
    nt = S // tm

    def nxt(off):
        return pl.BlockSpec((HALO, CONV_CB), lambda c, i: (jnp.minimum((i + 1) * nh, S // HALO - 1), off + c))

    def body(g_ref, ng_ref, v_ref, nv_ref, wg, wv, dug_ref, duv_ref):
        last = pl.program_id(1) == nt - 1
        for dc_ref, n_ref, w_ref, du_ref in ((g_ref, ng_ref, wg, dug_ref), (v_ref, nv_ref, wv, duv_ref)):
            dc = dc_ref[...]
            row = lax.broadcasted_iota(jnp.int32, dc.shape, 0)
            n1 = jnp.where(last, 0.0, n_ref[0:1, :])
            n2 = jnp.where(last, 0.0, n_ref[1:2, :])
            d1 = jnp.where(row == tm - 1, n1, pltpu.roll(dc, tm - 1, 0))
            d2 = jnp.where(row == tm - 1, n2, jnp.where(row == tm - 2, n1, pltpu.roll(dc, tm - 2, 0)))
            du_ref[...] = _bf(w_ref[2:3, :] * dc + w_ref[1:2, :] * d1 + w_ref[0:1, :] * d2)

    res, carried = _call(
        body, plans, name=name, grid=(nc, nt),
        in_specs=[tile(0), nxt(0), tile(0), nxt(0), small(3, 0), small(3, nc)],
        out_specs=[tile(0), tile(0)], out_shape=[_sds((S, D_FF), BF16)] * 2,
        args=(dcg, dcg, dcv, dcv, cw, cw))
    return res if plans is None else (res, carried)


def _row_tile(n, cap):
    best = n
    for t in range(16, cap + 1, 16):
        if n % t == 0:
            best = t
    return best if best <= cap else n


def _adamw(w, g, m, v, name):
    R, C = w.shape
    tr = _row_tile(R, max(16, (2 * 1024 * 1024) // (4 * C) // 16 * 16))

    def body(w_ref, g_ref, m_ref, v_ref, d_ref, nm_ref, nv_ref):
        gv = g_ref[...]
        nm = ADAM_B1 * m_ref[...] + (1.0 - ADAM_B1) * gv
        nv = ADAM_B2 * v_ref[...] + (1.0 - ADAM_B2) * (gv * gv)
        m_hat = nm / (1.0 - ADAM_B1 ** ADAM_STEP)
        v_hat = nv / (1.0 - ADAM_B2 ** ADAM_STEP)
        d_ref[...] = -ADAM_LR * (m_hat / (jnp.sqrt(v_hat) + ADAM_EPS) + ADAM_WD * w_ref[...])
        nm_ref[...] = nm
        nv_ref[...] = nv

    spec = pl.BlockSpec((tr, C), lambda i: (i, 0))
    return pl.pallas_call(body, name=name, grid=(R // tr,), in_specs=[spec] * 4, out_specs=[spec] * 3,
                          out_shape=[_sds((R, C), F32)] * 3, compiler_params=_cp(1))(w, g, m, v)


def _pair_sum(gfull, rcv, c_idx, name):
    nb, R, C = gfull.shape
    half = R // 2
    tr = _row_tile(half, 256)
    nt = half // tr

    def body(c_ref, g_ref, r_ref, o_ref):
        o_ref[...] = _bf(g_ref[...] + r_ref[...])

    return pl.pallas_call(
        body, name=name,
        grid_spec=pltpu.PrefetchScalarGridSpec(
            num_scalar_prefetch=1, grid=(nb, nt),
            in_specs=[pl.BlockSpec((None, tr, C), lambda j, i, c_ref: (j, c_ref[0] * nt + i, 0)),
                      pl.BlockSpec((None, tr, C), lambda j, i, c_ref: (j, i, 0))],
            out_specs=pl.BlockSpec((None, tr, C), lambda j, i, c_ref: (j, i, 0))),
        out_shape=_sds((nb, half, C), BF16), compiler_params=_cp(2))(c_idx, gfull, rcv)


def _chip_sum(arrived, own, place, name, row0, shard_rows, into=None):
    nb, H, C = arrived.shape
    tr = _row_tile(H, 256)
    nt = H // tr
    assert row0 % tr == 0
    extra = [] if into is None else [into]

    def body(pl_ref, *refs):
        o_ref = refs[nb + 1 + len(extra)]
        me = pl_ref[0]
        acc = None
        for k in range(nb):
            term = jnp.where(me == k, refs[nb][...], refs[k][...]).astype(F32)
            acc = term if acc is None else acc + term
        o_ref[...] = acc

    def other(k):
        return pl.BlockSpec((None, tr, C), lambda i, p: (jnp.where(p[0] == k, (k + 1) % nb, k), i, 0))

    return pl.pallas_call(
        body, name=name,
        grid_spec=pltpu.PrefetchScalarGridSpec(
            num_scalar_prefetch=1, grid=(nt,),
            in_specs=[other(k) for k in range(nb)] + [pl.BlockSpec((None, tr, C), lambda i, p: (p[0], i, 0))]
            + [ANY] * len(extra),
            out_specs=pl.BlockSpec((tr, C), lambda i, p: (row0 // tr + p[1] * nt + i, 0))),
        out_shape=_sds((shard_rows, C), F32),
        input_output_aliases={1 + nb + 1: 0} if extra else {},
        compiler_params=_cp(1))(place, *([arrived] * nb), own, *extra)


def _cast_into_slot(shard, place, name):
    R, C = shard.shape
    tr = _row_tile(R, 256)

    def body(pl_ref, s_ref, o_ref):
        o_ref[...] = _bf(s_ref[...])

    return pl.pallas_call(
        body, name=name,
        grid_spec=pltpu.PrefetchScalarGridSpec(
            num_scalar_prefetch=1, grid=(R // tr,),
            in_specs=[pl.BlockSpec((tr, C), lambda i, p: (i, 0))],
            out_specs=pl.BlockSpec((None, tr, C), lambda i, p: (p[0], i, 0))),
        out_shape=_sds((N_CHIPS, R, C), BF16), compiler_params=_cp(1))(place, shard)


def _place():
    x, y, c = lax.axis_index("x"), lax.axis_index("y"), lax.axis_index("c")
    chips = [(1 - x, y), (x, 1 - y), (1 - x, 1 - y)]
    return x, y, c, chips


def _chip_id(px, py):
    return 2 * px + py


def _remote(src, dst, send_sems, recv_sems, k, to):
    return pltpu.make_async_remote_copy(src_ref=src, dst_ref=dst, send_sem=send_sems.at[k], recv_sem=recv_sems.at[k],
                                        device_id=to, device_id_type=MESH)


def _gather_weights(slots, wholes, name):
    ns, nw = len(slots), len(wholes)
    n = ns + nw

    def body(*refs):
        ins = refs[ns:n]
        outs = refs[n:2 * n]
        send_sems, recv_sems, local_sems = refs[2 * n:]
        x, y, c, chips = _place()
        me = _chip_id(x, y)
        sib = (x, y, 1 - c)
        local = [pltpu.make_async_copy(ins[b], outs[ns + b].at[me], local_sems.at[b]) for b in range(nw)]
        for cp in local:
            cp.start()
        sent = []
        for a in range(n):
            R = outs[a].shape[1]
            rows = pl.ds(c * (R // 2), R // 2) if a < ns else pl.ds(0, R)
            src = outs[a].at[me, rows] if a < ns else ins[a - ns]
            for j, chip in enumerate(chips):
                cp = _remote(src, outs[a].at[me, rows], send_sems, recv_sems, 6 * a + j, (*chip, c))
                cp.start()
                sent.append(cp)
        for a in range(n):
            R = outs[a].shape[1]
            rows = pl.ds(c * (R // 2), R // 2) if a < ns else pl.ds(0, R)
            for j, chip in enumerate(chips):
                landed = outs[a].at[_chip_id(*chip), rows]
                _remote(landed, landed, send_sems, recv_sems, 6 * a + j, (*chip, c)).wait_recv()
                if a < ns:
                    cp = _remote(landed, landed, send_sems, recv_sems, 6 * a + 3 + j, sib)
                    cp.start()
                    sent.append(cp)
        for a in range(ns):
            R = outs[a].shape[1]
            other = pl.ds((1 - c) * (R // 2), R // 2)
            for j, chip in enumerate(chips):
                passed = outs[a].at[_chip_id(*chip), other]
                _remote(passed, passed, send_sems, recv_sems, 6 * a + 3 + j, sib).wait_recv()
        for cp in sent:
            cp.wait_send()
        for cp in local:
            cp.wait()

    return pl.pallas_call(
        body, name=name, in_specs=[ANY] * n, out_specs=[ANY] * n,
        out_shape=[_sds(s.shape, s.dtype) for s in slots] + [_sds((N_CHIPS, *s.shape), s.dtype) for s in wholes],
        input_output_aliases={a: a for a in range(ns)},
        scratch_shapes=[pltpu.SemaphoreType.DMA((6 * n,)), pltpu.SemaphoreType.DMA((6 * n,)),
                        pltpu.SemaphoreType.DMA((max(nw, 1),))])(*slots, *wholes)


def _gather_ici_plan(slots, wholes):
    ns, nw = len(slots), len(wholes)

    def copies(ins, ios, outs, send_sems, recv_sems, local_sems):
        x, y, c, chips = _place()
        me = _chip_id(x, y)
        sends, recvs = [], []
        for a in range(ns + nw):
            dst = ios[a] if a < ns else outs[a - ns]
            R = dst.shape[1]
            rows = pl.ds(c * (R // 2), R // 2) if a < ns else pl.ds(0, R)
            src = dst.at[me, rows] if a < ns else ins[a - ns]
            for j, chip in enumerate(chips):
                sends.append(_remote(src, dst.at[me, rows], send_sems, recv_sems, 3 * a + j, (*chip, c)))
                landed = dst.at[_chip_id(*chip), rows]
                recvs.append(_remote(landed, landed, send_sems, recv_sems, 3 * a + j, (*chip, c)))
        local = [pltpu.make_async_copy(ins[b], outs[b].at[me], local_sems.at[b]) for b in range(nw)]
        return sends, recvs, local

    return _Plan(copies, 3 * (ns + nw), ins=wholes, inouts=slots,
                 outs=[_sds((N_CHIPS, *s.shape), s.dtype) for s in wholes])


def _gather_pass_plan(slots):
    def copies(ins, ios, outs, send_sems, recv_sems, local_sems):
        x, y, c, chips = _place()
        sib = (x, y, 1 - c)
        sends, recvs = [], []
        for a, buf in enumerate(ios):
            half = buf.shape[1] // 2
            for j, chip in enumerate(chips):
                mine = buf.at[_chip_id(*chip), pl.ds(c * half, half)]
                other = buf.at[_chip_id(*chip), pl.ds((1 - c) * half, half)]
                sends.append(_remote(mine, mine, send_sems, recv_sems, 3 * a + j, sib))
                recvs.append(_remote(other, other, send_sems, recv_sems, 3 * a + j, sib))
        return sends, recvs, []

    return _Plan(copies, 3 * len(slots), inouts=slots)


def _pair_plan(grads):
    def copies(ins, ios, outs, send_sems, recv_sems, local_sems):
        x, y, c, _ = _place()
        sib = (x, y, 1 - c)
        sends, recvs = [], []
        for a, g in enumerate(ins):
            half = g.shape[1] // 2
            sends.append(_remote(g.at[:, pl.ds((1 - c) * half, half), :], outs[a], send_sems, recv_sems, a, sib))
            recvs.append(_remote(outs[a], outs[a], send_sems, recv_sems, a, sib))
        return sends, recvs, []

    return _Plan(copies, len(grads), ins=grads,
                 outs=[_sds((g.shape[0], g.shape[1] // 2, g.shape[2]), g.dtype) for g in grads])


def _chip_plan(parts):
    def copies(ins, ios, outs, send_sems, recv_sems, local_sems):
        x, y, c, chips = _place()
        me = _chip_id(x, y)
        sends, recvs = [], []
        for a, part in enumerate(ins):
            for j, chip in enumerate(chips):
                sends.append(_remote(part.at[_chip_id(*chip)], outs[a].at[me], send_sems, recv_sems, 3 * a + j, (*chip, c)))
                landed = outs[a].at[_chip_id(*chip)]
                recvs.append(_remote(landed, landed, send_sems, recv_sems, 3 * a + j, (*chip, c)))
        return sends, recvs, []

    return _Plan(copies, 3 * len(parts), ins=parts, outs=[_sds(p.shape, p.dtype) for p in parts])


def _pair_concat(fulls, pieces, name):
    n = len(fulls)
    spans = [(a, row0, rows // 2) for a in range(n) for (row0, rows) in pieces[a]]

    def body(*refs):
        outs = refs[n:2 * n]
        send_sems, recv_sems = refs[2 * n:]
        x, y, c, _ = _place()
        cps = []
        for s, (a, row0, H) in enumerate(spans):
            mine = outs[a].at[pl.ds(row0 + c * H, H)]
            cp = _remote(mine, mine, send_sems, recv_sems, s, (x, y, 1 - c))
            cp.start()
            cps.append(cp)
        for s, (a, row0, H) in enumerate(spans):
            other = outs[a].at[pl.ds(row0 + (1 - c) * H, H)]
            _remote(other, other, send_sems, recv_sems, s, (x, y, 1 - c)).wait_recv()
            cps[s].wait_send()

    return pl.pallas_call(
        body, name=name, in_specs=[ANY] * n, out_specs=[ANY] * n,
        out_shape=[_sds(f.shape, f.dtype) for f in fulls], input_output_aliases={a: a for a in range(n)},
        scratch_shapes=[pltpu.SemaphoreType.DMA((len(spans),)), pltpu.SemaphoreType.DMA((len(spans),))])(*fulls)


def _all_sum(pack, name):
    R, C = pack.shape

    def body(p_ref, o_ref, buf, send_sems, recv_sems):
        x, y, c, _ = _place()
        me = 4 * x + 2 * y + c
        buf[me] = p_ref[...]
        cps = []
        for k in range(1, N_DEV):
            to = (x ^ (k >> 2), y ^ ((k >> 1) & 1), c ^ (k & 1))
            cp = _remote(p_ref, buf.at[me], send_sems, recv_sems, k - 1, to)
            cp.start()
            cps.append(cp)
        for k in range(1, N_DEV):
            frm = (x ^ (k >> 2), y ^ ((k >> 1) & 1), c ^ (k & 1))
            slot = buf.at[4 * frm[0] + 2 * frm[1] + frm[2]]
            _remote(slot, slot, send_sems, recv_sems, k - 1, frm).wait_recv()
        acc = buf[0]
        for k in range(1, N_DEV):
            acc = acc + buf[k]
        o_ref[...] = acc
        for cp in cps:
            cp.wait_send()

    vm = pl.BlockSpec(memory_space=pltpu.VMEM)
    return pl.pallas_call(
        body, name=name, in_specs=[vm], out_specs=vm, out_shape=_sds((R, C), F32),
        scratch_shapes=[pltpu.VMEM((N_DEV, R, C), F32), pltpu.SemaphoreType.DMA((N_DEV - 1,)),
                        pltpu.SemaphoreType.DMA((N_DEV - 1,))])(pack)


def _local_step(xs, tgt, p, ex):
    h1 = _norm_fwd(xs, p["pre_mix_norm"], "pre_mix_norm")
    proj, got = _mm_nn_blk(h1, ex.weight("w_in"), "proj_in", plans=ex.carry("proj_in"))
    ex.done("proj_in", got)
    biases = _relbias_fwd(p["rel_bias"], "rel_bias_fwd")
    fw = []
    for g in range(N_GROUPS):
        res, got = _attn_fwd(proj, biases[g], g, f"attn_fwd{g}", plans=ex.carry(f"attn_fwd{g}"))
        ex.done(f"attn_fwd{g}", got)
        fw.append(res)
    y, lse = _attn_merge([t[0] for t in fw], [t[1] for t in fw], "attn_merge")
    yh, o_h, states = _hgrn_fwd(proj, p["hgrn_lb_raw"], p["hgrn_norm"], "hgrn_fwd")
    W_a, W_h, W_out = ex.weight("w_branch_attn"), ex.weight("w_branch_hgrn"), ex.weight("w_out")
    W_up, W_down, conv_w = ex.weight("w_up"), ex.weight("w_down"), ex.weight("conv_w")
    za = _mm_nn_blk(y, W_a, "branch_attn")
    zh = _mm_nn_blk(yh, W_h, "branch_hgrn")
    merged = _merge_fwd(za, zh, proj, "merge_fwd")
    mo = _mm_nn(merged, W_out, "mix_out")
    x1, h2 = _mix_out(mo, xs, p["post_mix_norm"], p["pre_ffn_norm"], "mix_residual")
    u = _mm_nn_blk(h2, W_up, "ffn_up")
    a = _conv_gelu_fwd(u, conv_w, p["conv_b"], "conv_gelu_fwd")
    ffo = _mm_nn(a, W_down, "ffn_down")
    dx2, dff, g_post_ffn, loss = _loss_head(ffo, x1, tgt, p["post_ffn_norm"], "loss_head")

    da = _mm_nt(dff, W_down, "d_ffn_act")
    ex.grad("w_down", _mm_tn(a.T, dff, "g_w_down").reshape(N_CHIPS, D_FF // N_CHIPS, D_MODEL))
    (dcg, dcv, gwg, gwv, gbg, gbv), got = _conv_gelu_bwd(u, da, conv_w, p["conv_b"], "conv_gelu_bwd",
                                                          plans=ex.carry("conv_gelu_bwd"))
    ex.done("conv_gelu_bwd", got)
    g_conv_w = jnp.concatenate([gwg, gwv], axis=1)
    g_conv_b = jnp.concatenate([gbg, gbv], axis=1)
    du_parts, got = _conv_input_bwd(dcg, dcv, conv_w, "conv_input_bwd", plans=ex.carry("conv_input_bwd"))
    ex.done("conv_input_bwd", got)
    du = jnp.concatenate(du_parts, axis=1)
    dh2 = _mm_nt_blk(du, W_up, "d_ffn_in")
    ex.grad("w_up", _mm_tn_blk(h2.T, du, N_CHIPS, "g_w_up"))
    (dx1, g_pre_ffn), got = _prenorm_bwd(dh2, x1, p["pre_ffn_norm"], dx2, "pre_ffn_norm_bwd",
                                         plans=ex.carry("pre_ffn_norm_bwd"))
    ex.done("pre_ffn_norm_bwd", got)
    dmo, g_post_mix = _postnorm_bwd(dx1, mo, p["post_mix_norm"], "post_mix_norm_bwd")
    dmerged = _mm_nt(dmo, W_out, "d_merged")
    ex.grad("w_out", _mm_tn(merged.T, dmo, "g_w_out").reshape(N_CHIPS, D_MODEL // N_CHIPS, D_MODEL))
    dza, dzh, dg0, dg1 = _merge_bwd(dmerged, za, zh, proj, "merge_bwd")
    dy = _mm_nt_blk(dza, W_a, "d_attn_out")
    ex.grad("w_branch_attn", _mm_tn_blk(_bf(y).T, dza, N_CHIPS, "g_w_branch_attn"))
    dyh = _mm_nt_blk(dzh, W_h, "d_hgrn_out")
    ex.grad("w_branch_hgrn", _mm_tn_blk(_bf(yh).T, dzh, N_CHIPS, "g_w_branch_hgrn"))
    dqkv, dbs = [], []
    for g in range(N_GROUPS):
        parts, db, got = _attn_bwd(proj, biases[g], lse, y, dy, g, f"attn_bwd{g}", plans=ex.carry(f"attn_bwd{g}"))
        ex.done(f"attn_bwd{g}", got)
        dqkv += parts
        dbs.append(db)
    g_rel_bias = _relbias_bwd(dbs, "rel_bias_bwd")
    dhg, g_lb_raw, g_hgrn_norm = _hgrn_bwd(proj, p["hgrn_lb_raw"], p["hgrn_norm"], o_h, states, dyh, "hgrn_bwd")
    dproj = jnp.concatenate([*[_bf(t) for t in dqkv], *dhg, dg0, dg1], axis=1)
    h1t = h1.T
    for piece, row0, rows in PIECES["w_in"]:
        g, got = _mm_tn_blk(h1t, dproj, N_CHIPS, f"g_{piece}", x_rows=(row0 // rows, rows), plans=ex.carry(f"g_{piece}"))
        ex.done(f"g_{piece}", got)
        ex.grad(piece, g)
    dh1, got = _mm_nt_blk(dproj, ex.weight("w_in"), "d_proj_in", plans=ex.carry("d_proj_in"))
    ex.done("d_proj_in", got)
    (grad_x, g_pre_mix), got = _prenorm_bwd(dh1, xs, p["pre_mix_norm"], dx1, "pre_mix_norm_bwd",
                                            plans=ex.carry("pre_mix_norm_bwd"))
    ex.done("pre_mix_norm_bwd", got)
    small = dict(pre_mix_norm=g_pre_mix, rel_bias=g_rel_bias, hgrn_lb_raw=g_lb_raw, hgrn_norm=g_hgrn_norm,
                 post_mix_norm=g_post_mix, pre_ffn_norm=g_pre_ffn, conv_w=g_conv_w, conv_b=g_conv_b,
                 post_ffn_norm=g_post_ffn)
    return loss, grad_x, small


SMALL = ("pre_mix_norm", "rel_bias", "hgrn_lb_raw", "hgrn_norm", "post_mix_norm", "pre_ffn_norm", "conv_w", "conv_b",
         "post_ffn_norm")
BIG = ("w_in", "w_up", "w_down", "w_out", "w_branch_attn", "w_branch_hgrn")
WEIGHTS = ("pre_mix_norm", "w_in", "rel_bias", "hgrn_lb_raw", "hgrn_norm", "w_branch_attn", "w_branch_hgrn", "w_out",
           "post_mix_norm", "pre_ffn_norm", "w_up", "conv_w", "conv_b", "w_down", "post_ffn_norm")
MIXER = ("w_out", "w_branch_attn", "w_branch_hgrn")

SCHEDULE = {
    "proj_in": [("gather_ici_cw", ("w_up",) + MIXER)],
    "attn_fwd0": [("gather_pass", ("w_up",) + MIXER), ("gather_ici", ("w_down",))],
    "attn_fwd1": [("gather_pass", ("w_down",))],
    "conv_gelu_bwd": [("pair", ("w_down",))],
    "conv_input_bwd": [("chip", ("w_down",))],
    "pre_ffn_norm_bwd": [("pair", ("w_up",))],
    "attn_bwd0": [("chip", ("w_up",)), ("pair", MIXER)],
    "attn_bwd1": [("chip", MIXER)],
    "g_w_in_b": [("pair", ("w_in_a",))],
    "g_w_in_c": [("pair", ("w_in_b",))],
    "d_proj_in": [("chip", ("w_in_a", "w_in_b")), ("pair", ("w_in_c",))],
    "pre_mix_norm_bwd": [("chip", ("w_in_c",))],
}
SHARD_ROWS = dict(w_in=D_MODEL, w_up=D_MODEL, w_down=D_FF // N_CHIPS, w_out=D_MODEL // N_CHIPS,
                  w_branch_attn=GROUP_W, w_branch_hgrn=HG_W)
PIECES = {n: ((n, 0, SHARD_ROWS[n]),) for n in BIG}
PIECES["w_in"] = (("w_in_a", 0, 512), ("w_in_b", 512, 256), ("w_in_c", 768, 256))


class _Exchange:
    def __init__(self, place, slots, conv_w_shard):
        self.place, self.slots, self.conv_w_shard = place, dict(slots), conv_w_shard
        self.conv_w = None
        self.g, self.from_sibling, self.pair_sums, self.arrived = {}, {}, {}, {}
        self.pending = []

    def weight(self, name):
        if name == "conv_w":
            return self.conv_w
        w = self.slots[name]
        return w.reshape(-1, D_MODEL) if name in ("w_out", "w_down") else w

    def grad(self, name, g):
        self.g[name] = g

    def carry(self, point):
        plans = []
        self.pending = SCHEDULE.get(point, [])
        for kind, names in self.pending:
            if kind in ("gather_ici", "gather_ici_cw"):
                wholes = [self.conv_w_shard] if kind == "gather_ici_cw" else []
                plans.append(_gather_ici_plan([self.slots[n] for n in names], wholes))
            elif kind == "gather_pass":
                plans.append(_gather_pass_plan([self.slots[n] for n in names]))
            elif kind == "pair":
                plans.append(_pair_plan([self.g[n] for n in names]))
            else:
                for n in names:
                    self.pair_sums[n] = _pair_sum(self.g[n], self.from_sibling[n], self.place[1:2], f"pair_sum_{n}")
                plans.append(_chip_plan([self.pair_sums[n] for n in names]))
        return plans

    def done(self, point, carried):
        for (kind, names), got in zip(self.pending, carried):
            if kind in ("gather_ici", "gather_ici_cw", "gather_pass"):
                self.slots.update(zip(names, got))
                if kind == "gather_ici_cw":
                    self.conv_w = got[len(names)].transpose(1, 0, 2).reshape(3, 2 * D_FF)
            elif kind == "pair":
                self.from_sibling.update(zip(names, got))
            else:
                self.arrived.update(zip(names, got))

    def reduced(self):
        shards = []
        for n in BIG:
            full = None
            for piece, row0, _ in PIECES[n]:
                full = _chip_sum(self.arrived[piece], self.pair_sums[piece], self.place, f"chip_sum_{piece}",
                                 row0, SHARD_ROWS[n], into=full)
            shards.append(full)
        pieces = [[(row0, rows) for _, row0, rows in PIECES[n]] for n in BIG]
        return dict(zip(BIG, _pair_concat(shards, pieces, "pair_concat")))


def kernel(x, pre_mix_norm, w_in, rel_bias, hgrn_lb_raw, hgrn_norm, w_branch_attn, w_branch_hgrn, w_out, post_mix_norm, pre_ffn_norm, w_up, conv_w, conv_b, w_down, post_ffn_norm, loss_target, m_pre_mix_norm, m_w_in, m_rel_bias, m_hgrn_lb_raw, m_hgrn_norm, m_w_branch_attn, m_w_branch_hgrn, m_w_out, m_post_mix_norm, m_pre_ffn_norm, m_w_up, m_conv_w, m_conv_b, m_w_down, m_post_ffn_norm, v_pre_mix_norm, v_w_in, v_rel_bias, v_hgrn_lb_raw, v_hgrn_norm, v_w_branch_attn, v_w_branch_hgrn, v_w_out, v_post_mix_norm, v_pre_ffn_norm, v_w_up, v_conv_w, v_conv_b, v_w_down, v_post_ffn_norm):
    w = dict(pre_mix_norm=pre_mix_norm, w_in=w_in, rel_bias=rel_bias, hgrn_lb_raw=hgrn_lb_raw, hgrn_norm=hgrn_norm,
             w_branch_attn=w_branch_attn, w_branch_hgrn=w_branch_hgrn, w_out=w_out, post_mix_norm=post_mix_norm,
             pre_ffn_norm=pre_ffn_norm, w_up=w_up, conv_w=conv_w, conv_b=conv_b, w_down=w_down,
             post_ffn_norm=post_ffn_norm)
    m = dict(pre_mix_norm=m_pre_mix_norm, w_in=m_w_in, rel_bias=m_rel_bias, hgrn_lb_raw=m_hgrn_lb_raw,
             hgrn_norm=m_hgrn_norm, w_branch_attn=m_w_branch_attn, w_branch_hgrn=m_w_branch_hgrn, w_out=m_w_out,
             post_mix_norm=m_post_mix_norm, pre_ffn_norm=m_pre_ffn_norm, w_up=m_w_up, conv_w=m_conv_w,
             conv_b=m_conv_b, w_down=m_w_down, post_ffn_norm=m_post_ffn_norm)
    v = dict(pre_mix_norm=v_pre_mix_norm, w_in=v_w_in, rel_bias=v_rel_bias, hgrn_lb_raw=v_hgrn_lb_raw,
             hgrn_norm=v_hgrn_norm, w_branch_attn=v_w_branch_attn, w_branch_hgrn=v_w_branch_hgrn, w_out=v_w_out,
             post_mix_norm=v_post_mix_norm, pre_ffn_norm=v_pre_ffn_norm, w_up=v_w_up, conv_w=v_conv_w,
             conv_b=v_conv_b, w_down=v_w_down, post_ffn_norm=v_post_ffn_norm)
    shard2d = {n: (w[n][0] if w[n].ndim == 3 else w[n]) for n in WEIGHTS}
    chip = 2 * lax.axis_index("x") + lax.axis_index("y")
    core = lax.axis_index("c")

    place = jnp.stack([chip, core]).astype(jnp.int32)
    slots = {n: _cast_into_slot(shard2d[n], place, f"cast_{n}") for n in BIG}
    slots["w_in"] = _gather_weights([slots["w_in"]], [], "gather_w_in")[0]
    ex = _Exchange(place, slots, shard2d["conv_w"])
    loss, grad_x, small = _local_step(x[0], loss_target[0], {n: w[n] for n in SMALL if n != "conv_w"}, ex)

    flat = [small[n].reshape(-1) for n in SMALL] + [loss.reshape(-1)]
    sizes = [t.shape[0] for t in flat]
    summed = _all_sum(jnp.concatenate(flat).reshape(-1, LANES), "sum_small").reshape(-1)
    offs = [sum(sizes[:i]) for i in range(len(sizes))]
    grads = {}
    for n, o, sz in zip(SMALL, offs, sizes):
        grads[n] = summed[o:o + sz].reshape(small[n].shape)
    loss_total = summed[offs[-1]]
    cw = 2 * D_FF // N_CHIPS
    grads["conv_w"] = lax.dynamic_slice(grads["conv_w"], (0, chip * cw), (3, cw))

    grads.update(ex.reduced())

    out_g, out_d, out_m, out_v = [], [], [], []
    for n in WEIGHTS:
        d2, m2, v2 = _adamw(shard2d[n], grads[n], m[n].reshape(shard2d[n].shape), v[n].reshape(shard2d[n].shape),
                            f"adamw_{n}")
        shape = w[n].shape
        out_g.append(grads[n].reshape(shape))
        out_d.append(d2.reshape(shape))
        out_m.append(m2.reshape(shape))
        out_v.append(v2.reshape(shape))
    return (loss_total, grad_x[None], *out_g, *out_d, *out_m, *out_v)
```

```python
import functools
import math

import jax
import jax.numpy as jnp
from jax import lax
from jax.experimental import pallas as pl
from jax.experimental.pallas import tpu as pltpu

F32 = jnp.float32
BF16 = jnp.bfloat16
MESH = pl.DeviceIdType.MESH

D_MODEL = 1024
N_GROUPS = 3
DILATIONS = (1, 4, 16)
HEADS = 8
HEAD_DIM = 64
GROUP_W = HEADS * HEAD_DIM
QKV_W = N_GROUPS * 3 * GROUP_W
BLK = 128
NEG_INF = -1e30
NUM_BUCKETS = 32
MAX_EXACT = 16
MAX_DISTANCE = 2048
HG_HEADS = 4
HG_DK = 128
HG_W = HG_HEADS * HG_DK
HG_CHUNK = 32
HG_TILE = 256
IN_W = QKV_W + 4 * HG_W + 2 * D_MODEL
D_FF = 2816
EPS = 1e-6
N_CHIPS = 4
N_DEV = 8
LANES = 128

ADAM_LR, ADAM_B1, ADAM_B2, ADAM_EPS, ADAM_WD, ADAM_STEP = 0.001, 0.9, 0.999, 1e-08, 0.01, 10

VMEM_LIMIT = 56 * 1024 * 1024


def _cp(n_axes):
    return pltpu.CompilerParams(dimension_semantics=("arbitrary",) * n_axes, vmem_limit_bytes=VMEM_LIMIT)


def _sds(shape, dtype):
    return jax.ShapeDtypeStruct(tuple(shape), dtype)


def _sigmoid(v):
    return 1.0 / (1.0 + jnp.exp(-v))


def _bf(v):
    return v.astype(BF16)


def _dot(a, b, dims):
    return lax.dot_general(a, b, (dims, ((), ())), preferred_element_type=F32)


NN = ((1,), (0,))
NT = ((1,), (1,))
TN = ((0,), (0,))

ANY = pl.BlockSpec(memory_space=pl.ANY)


class _Plan:
    def __init__(self, copies, n_sems, ins=(), inouts=(), outs=()):
        self.copies, self.n_sems = copies, n_sems
        self.ins, self.inouts, self.outs = list(ins), list(inouts), list(outs)


def _call(body, plans=None, *, name, grid, in_specs, out_specs, out_shape, args, scratch_shapes=()):
    plans = list(plans or ())
    in_specs, out_specs, out_shape = list(in_specs), list(out_specs), list(out_shape)
    scratch_shapes = list(scratch_shapes)
    n_in, n_out, n_scr = len(in_specs), len(out_specs), len(scratch_shapes)
    x_in, x_out, aliases, spans = [], [], {}, []
    for p in plans:
        i0, o0 = len(x_in), len(x_out)
        x_in += p.ins
        for a in p.inouts:
            aliases[n_in + len(x_in)] = n_out + len(x_out)
            x_in.append(a)
            x_out.append(_sds(a.shape, a.dtype))
        x_out += p.outs
        spans.append((i0, len(p.ins), o0, len(p.inouts), len(p.outs)))
    sems = [pltpu.SemaphoreType.DMA((p.n_sems,)) for p in plans for _ in range(3)]

    def wrapped(*refs):
        xi = refs[n_in:n_in + len(x_in)]
        base = n_in + len(x_in)
        xo = refs[base + n_out:base + n_out + len(x_out)]
        sbase = base + n_out + len(x_out)
        xs = refs[sbase + n_scr:]
        ids = [pl.program_id(k) for k in range(len(grid))]
        first = functools.reduce(jnp.logical_and, [i == 0 for i in ids])
        last = functools.reduce(jnp.logical_and, [i == g - 1 for i, g in zip(ids, grid)])

        def descriptors(k):
            i0, ni, o0, nio, no = spans[k]
            return plans[k].copies(xi[i0:i0 + ni], xo[o0:o0 + nio], xo[o0 + nio:o0 + nio + no], *xs[3 * k:3 * k + 3])

        @pl.when(first)
        def _():
            for k in range(len(plans)):
                sends, _, local = descriptors(k)
                for cp in (*sends, *local):
                    cp.start()

        body(*refs[:n_in], *refs[base:base + n_out], *refs[sbase:sbase + n_scr])

        @pl.when(last)
        def _():
            for k in range(len(plans)):
                sends, recvs, local = descriptors(k)
                for cp in recvs:
                    cp.wait_recv()
                for cp in sends:
                    cp.wait_send()
                for cp in local:
                    cp.wait()

    res = pl.pallas_call(
        wrapped if plans else body, name=name, grid=grid, in_specs=in_specs + [ANY] * len(x_in),
        out_specs=out_specs + [ANY] * len(x_out), out_shape=out_shape + x_out, input_output_aliases=aliases,
        scratch_shapes=scratch_shapes + sems, compiler_params=_cp(len(grid)))(*args, *x_in)
    res = list(res)
    carried = [res[n_out + o0:n_out + o0 + nio + no] for (_, _, o0, nio, no) in spans]
    return res[:n_out], carried


def _mm_nn_blk(a, wg, name, tm=512, plans=None):
    M, K = a.shape
    nb, _, Nb = wg.shape

    def body(a_ref, w_ref, o_ref):
        o_ref[...] = _dot(_bf(a_ref[...]), w_ref[...], NN)

    (out,), carried = _call(
        body, plans, name=name, grid=(nb, M // tm),
        in_specs=[pl.BlockSpec((tm, K), lambda j, i: (i, 0)), pl.BlockSpec((None, K, Nb), lambda j, i: (j, 0, 0))],
        out_specs=[pl.BlockSpec((tm, Nb), lambda j, i: (i, j))],
        out_shape=[_sds((M, nb * Nb), F32)], args=(a, wg))
    return out if plans is None else (out, carried)


def _mm_nt_blk(dy, wg, name, tm=1024, plans=None):
    M = dy.shape[0]
    nb, K, Nb = wg.shape

    def body(dy_ref, w_ref, o_ref):
        j = pl.program_id(1)
        r = _dot(_bf(dy_ref[...]), w_ref[...], NT)

        @pl.when(j == 0)
        def _():
            o_ref[...] = r

        @pl.when(j > 0)
        def _():
            o_ref[...] += r

    (out,), carried = _call(
        body, plans, name=name, grid=(M // tm, nb),
        in_specs=[pl.BlockSpec((tm, Nb), lambda i, j: (i, j)), pl.BlockSpec((None, K, Nb), lambda i, j: (j, 0, 0))],
        out_specs=[pl.BlockSpec((tm, K), lambda i, j: (i, 0))],
        out_shape=[_sds((M, K), F32)], args=(dy, wg))
    return out if plans is None else (out, carried)


def _mm_tn_blk(x, dy, nb, name, tk=512, x_cols=None, plans=None, together=False):
    T, Mx = x.shape
    xk, Mx = (0, Mx) if x_cols is None else x_cols
    Nb = dy.shape[1] // nb
    nj = nb if together else 1

    def body(x_ref, dy_ref, o_ref):
        t = pl.program_id(1)
        r = _dot(_bf(x_ref[...]), _bf(dy_ref[...]), TN)
        for j in range(nj):
            rj = r[:, j * Nb:(j + 1) * Nb]

            @pl.when(t == 0)
            def _():
                o_ref[j] = rj

            @pl.when(t > 0)
            def _():
                o_ref[j] += rj

    (out,), carried = _call(
        body, plans, name=name, grid=(nb // nj, T // tk),
        in_specs=[pl.BlockSpec((tk, Mx), lambda j, t: (t, xk)), pl.BlockSpec((tk, nj * Nb), lambda j, t: (t, j))],
        out_specs=[pl.BlockSpec((nj, Mx, Nb), lambda j, t: (j, 0, 0))],
        out_shape=[_sds((nb, Mx, Nb), F32)], args=(x, dy))
    return out if plans is None else (out, carried)


def _mm_nt(dy, w, name, tm=512):
    M, N = dy.shape
    K = w.shape[0]

    def body(dy_ref, w_ref, o_ref):
        o_ref[...] = _dot(_bf(dy_ref[...]), w_ref[...], NT)

    return pl.pallas_call(
        body, name=name, grid=(M // tm,),
        in_specs=[pl.BlockSpec((tm, N), lambda i: (i, 0)), pl.BlockSpec((K, N), lambda i: (0, 0))],
        out_specs=pl.BlockSpec((tm, K), lambda i: (i, 0)),
        out_shape=_sds((M, K), F32), compiler_params=_cp(1))(dy, w)


def _mm_tn(x, dy, name, tk=512):
    T, Mx = x.shape
    N = dy.shape[1]

    def body(x_ref, dy_ref, o_ref):
        t = pl.program_id(0)
        r = _dot(_bf(x_ref[...]), _bf(dy_ref[...]), TN)

        @pl.when(t == 0)
        def _():
            o_ref[...] = r

        @pl.when(t > 0)
        def _():
            o_ref[...] += r

    return pl.pallas_call(
        body, name=name, grid=(T // tk,),
        in_specs=[pl.BlockSpec((tk, Mx), lambda t: (t, 0)), pl.BlockSpec((tk, N), lambda t: (t, 0))],
        out_specs=pl.BlockSpec((Mx, N), lambda t: (0, 0)),
        out_shape=_sds((Mx, N), F32), compiler_params=_cp(1))(x, dy)


def _tile(arr, bw, col=lambda c: 0):
    return ("tile", arr, bw, col)


def _full(arr):
    return ("full", arr)


def _out_tile(width, dtype, bw, col=lambda c: 0):
    return ("tile", width, dtype, bw, col)


def _out_acc(rows, width, bw, col=lambda c: 0):
    return ("acc", rows, width, bw, col)


def _rows_call(name, body, n_rows, tm, ncol, ins, outs, plans=None):
    in_specs, args = [], []
    for e in ins:
        if e[0] == "tile":
            _, arr, bw, col = e
            in_specs.append(pl.BlockSpec((tm, bw), functools.partial(lambda c, i, col: (i, col(c)), col=col)))
        else:
            arr = e[1]
            in_specs.append(pl.BlockSpec(arr.shape, functools.partial(lambda c, i, nd: (0,) * nd, nd=arr.ndim)))
        args.append(arr)
    out_specs, out_shape = [], []
    for e in outs:
        if e[0] == "tile":
            _, width, dtype, bw, col = e
            out_specs.append(pl.BlockSpec((tm, bw), functools.partial(lambda c, i, col: (i, col(c)), col=col)))
            out_shape.append(_sds((n_rows, width), dtype))
        else:
            _, rows, width, bw, col = e
            out_specs.append(pl.BlockSpec((rows, bw), functools.partial(lambda c, i, col: (0, col(c)), col=col)))
            out_shape.append(_sds((rows, width), F32))
    out, carried = _call(body, plans, name=name, grid=(ncol, n_rows // tm), in_specs=in_specs, out_specs=out_specs,
                         out_shape=out_shape, args=args)
    return out if plans is None else (out, carried)


def _acc(ref, val):
    i = pl.program_id(1)

    @pl.when(i == 0)
    def _():
        ref[...] = val

    @pl.when(i > 0)
    def _():
        ref[...] += val


def _rinv(z):
    return lax.rsqrt(jnp.mean(z * z, axis=-1, keepdims=True) + EPS)


def _norm_bwd(dy, zhat, r, w):
    dyw = dy * w
    return r * (dyw - zhat * jnp.mean(dyw * zhat, axis=-1, keepdims=True))


def _norm_fwd(x, w, name):
    def body(x_ref, w_ref, h_ref):
        xv = x_ref[...]
        h_ref[...] = _bf(xv * _rinv(xv) * w_ref[...])

    return _rows_call(name, body, x.shape[0], 512, 1, [_tile(x, D_MODEL), _full(w)],
                      [_out_tile(D_MODEL, BF16, D_MODEL)])[0]


def _prenorm_bwd(dh, xin, w, dres, name, plans=None):
    def body(dh_ref, x_ref, w_ref, dres_ref, dx_ref, dw_ref):
        xv = x_ref[...]
        r = _rinv(xv)
        xhat = xv * r
        dhv = dh_ref[...]
        dx_ref[...] = dres_ref[...] + _norm_bwd(dhv, xhat, r, w_ref[...])
        _acc(dw_ref, jnp.sum(dhv * xhat, axis=0, keepdims=True))

    return _rows_call(name, body, xin.shape[0], 512, 1,
                      [_tile(dh, D_MODEL), _tile(xin, D_MODEL), _full(w), _tile(dres, D_MODEL)],
                      [_out_tile(D_MODEL, F32, D_MODEL), _out_acc(1, D_MODEL, D_MODEL)], plans)


def _postnorm_bwd(dout, z, w, w_mat, name):
    def body(do_ref, z_ref, w_ref, wm_ref, dz_ref, dm_ref, dw_ref):
        zv = z_ref[...]
        r = _rinv(zv)
        zhat = zv * r
        dov = do_ref[...]
        dz = _bf(_norm_bwd(dov, zhat, r, w_ref[...]))
        dz_ref[...] = dz
        dm_ref[...] = _dot(dz, wm_ref[...], NT)
        _acc(dw_ref, jnp.sum(dov * zhat, axis=0, keepdims=True))

    return _rows_call(name, body, z.shape[0], 512, 1,
                      [_tile(dout, D_MODEL), _tile(z, D_MODEL), _full(w), _full(w_mat)],
                      [_out_tile(D_MODEL, BF16, D_MODEL), _out_tile(D_MODEL, F32, D_MODEL),
                       _out_acc(1, D_MODEL, D_MODEL)])


def _t5_bucket(dist):
    n = jnp.maximum(dist, 0)
    nf = jnp.maximum(n, 1).astype(F32)
    large = MAX_EXACT + (jnp.log(nf / MAX_EXACT) / math.log(MAX_DISTANCE / MAX_EXACT)
                         * (NUM_BUCKETS - MAX_EXACT)).astype(jnp.int32)
    large = jnp.minimum(large, NUM_BUCKETS - 1)
    return jnp.where(n < MAX_EXACT, n, large)


def _band_rel():
    return jnp.arange(BLK)[:, None] + BLK - jnp.arange(2 * BLK)[None, :]


def _band_mask(n):
    row = lax.broadcasted_iota(jnp.int32, (BLK, 2 * BLK), 0)
    col = lax.broadcasted_iota(jnp.int32, (BLK, 2 * BLK), 1)
    rel = row + BLK - col
    return (rel >= 0) & (rel <= BLK) & ((col >= BLK) | (n > 0))


RES_UNROLL = 4


def _heads_per_step(d):
    return HEADS if d == 1 else LANES // HEAD_DIM


def _sub_rows(r, d):
    return pl.ds(r, BLK, stride=d) if d > 1 else pl.ds(0, BLK)


def _for_residues(d, fn):
    if d <= RES_UNROLL:
        for r in range(d):
            fn(r)
    else:
        def group(i, carry):
            for k in range(RES_UNROLL):
                fn(i * RES_UNROLL + k)
            return carry

        lax.fori_loop(0, d // RES_UNROLL, group, 0)


def _attn_specs(d, g, qblock):
    cw = _heads_per_step(d) * HEAD_DIM

    def col(part, hp):
        return (g * 3 + part) * (GROUP_W // cw) + hp

    def cur(part):
        return pl.BlockSpec((d * BLK, cw), lambda hp, n: (qblock(n), col(part, hp)))

    def prev(part):
        return pl.BlockSpec((d * BLK, cw), lambda hp, n: (jnp.maximum(qblock(n) - 1, 0), col(part, hp)))

    return cur, prev


def _attn_fwd(proj, bias, g, name, plans=None):
    S = proj.shape[0]
    d = DILATIONS[g]
    NB = S // (d * BLK)
    hps = _heads_per_step(d)

    def body(q_ref, kp_ref, kc_ref, vp_ref, vc_ref, b_ref, o_ref, lse_ref):
        hp = pl.program_id(0)
        mask = _band_mask(pl.program_id(1))

        def residue(r):
            rows = _sub_rows(r, d)
            q2 = q_ref[rows, :]
            k2 = jnp.concatenate([kp_ref[rows, :], kc_ref[rows, :]], axis=0)
            v2 = jnp.concatenate([vp_ref[rows, :], vc_ref[rows, :]], axis=0)
            outs, lses = [], []
            for hh in range(hps):
                hs = slice(hh * HEAD_DIM, (hh + 1) * HEAD_DIM)
                s = _dot(_bf(q2[:, hs]), _bf(k2[:, hs]), NT) * (HEAD_DIM ** -0.5) + b_ref[hp * hps + hh]
                s = jnp.where(mask, s, NEG_INF)
                m = jnp.max(s, axis=-1, keepdims=True)
                p = jnp.exp(s - m)
                l = jnp.sum(p, axis=-1, keepdims=True)
                outs.append(_dot(_bf(p), _bf(v2[:, hs]), NN) / l)
                lses.append(jnp.broadcast_to(m + jnp.log(l), (BLK, HEAD_DIM)))
            o_ref[rows, :] = jnp.concatenate(outs, axis=1)
            lse_ref[rows, :] = jnp.concatenate(lses, axis=1)

        _for_residues(d, residue)

    cur, prev = _attn_specs(d, g, lambda n: n)
    out = pl.BlockSpec((d * BLK, hps * HEAD_DIM), lambda hp, n: (n, hp))
    res, carried = _call(
        body, plans, name=name, grid=(HEADS // hps, NB),
        in_specs=[cur(0), prev(1), cur(1), prev(2), cur(2),
                  pl.BlockSpec((HEADS, BLK, 2 * BLK), lambda hp, n: (0, 0, 0))],
        out_specs=[out, out], out_shape=[_sds((S, GROUP_W), F32)] * 2,
        args=(proj, proj, proj, proj, proj, bias))
    return res if plans is None else (res, carried)


def _attn_merge(os_, lses, name):
    def body(o0, o1, o2, l0, l1, l2, y_ref, lse_ref):
        a, b, c = l0[...], l1[...], l2[...]
        m = jnp.maximum(jnp.maximum(a, b), c)
        ea, eb, ec = jnp.exp(a - m), jnp.exp(b - m), jnp.exp(c - m)
        den = ea + eb + ec
        y_ref[...] = (ea * o0[...] + eb * o1[...] + ec * o2[...]) / den
        lse_ref[...] = m + jnp.log(den)

    S = os_[0].shape[0]
    return _rows_call(name, body, S, 512, 1, [_tile(t, GROUP_W) for t in (*os_, *lses)],
                      [_out_tile(GROUP_W, F32, GROUP_W)] * 2)


def _attn_bwd(proj, bias, lse, y, dy, g, name, plans=None):
    S = proj.shape[0]
    d = DILATIONS[g]
    NB = S // (d * BLK)
    hps = _heads_per_step(d)

    def body(q_ref, kp_ref, kc_ref, vp_ref, vc_ref, b_ref, l_ref, y_ref, dy_ref,
             dq_ref, dk_ref, dv_ref, db_ref, ck_ref, cv_ref):
        hp, n = pl.program_id(0), pl.program_id(1)

        @pl.when((hp == 0) & (n == 0))
        def _():
            db_ref[...] = jnp.zeros_like(db_ref)

        @pl.when(n == 0)
        def _():
            ck_ref[...] = jnp.zeros_like(ck_ref)
            cv_ref[...] = jnp.zeros_like(cv_ref)

        @pl.when(n < NB)
        def _():
            mask = _band_mask(n)

            def residue(r):
                rows = _sub_rows(r, d)
                q2 = q_ref[rows, :]
                k2 = jnp.concatenate([kp_ref[rows, :], kc_ref[rows, :]], axis=0)
                v2 = jnp.concatenate([vp_ref[rows, :], vc_ref[rows, :]], axis=0)
                l2, y2, dy2 = l_ref[rows, :], y_ref[rows, :], dy_ref[rows, :]
                dqs, dks, dvs = [], [], []
                for hh in range(hps):
                    hs = slice(hh * HEAD_DIM, (hh + 1) * HEAD_DIM)
                    q, kb, vb = _bf(q2[:, hs]), _bf(k2[:, hs]), _bf(v2[:, hs])
                    s = _dot(q, kb, NT) * (HEAD_DIM ** -0.5) + b_ref[hp * hps + hh]
                    s = jnp.where(mask, s, NEG_INF)
                    p = jnp.exp(s - l2[:, hh * HEAD_DIM:hh * HEAD_DIM + 1])
                    dyh = dy2[:, hs]
                    delta = jnp.sum(dyh * y2[:, hs], axis=-1, keepdims=True)
                    dyb = _bf(dyh)
                    ds = p * (_dot(dyb, vb, NT) - delta)
                    db_ref[hp * hps + hh] += ds
                    dsb = _bf(ds * (HEAD_DIM ** -0.5))
                    dqs.append(_dot(dsb, kb, NN))
                    dks.append(_dot(dsb, q, TN))
                    dvs.append(_dot(_bf(p), dyb, TN))
                dkb = jnp.concatenate(dks, axis=1)
                dvb = jnp.concatenate(dvs, axis=1)
                dq_ref[rows, :] = jnp.concatenate(dqs, axis=1)
                dk_ref[rows, :] = ck_ref[rows, :] + dkb[:BLK]
                dv_ref[rows, :] = cv_ref[rows, :] + dvb[:BLK]
                ck_ref[rows, :] = dkb[BLK:]
                cv_ref[rows, :] = dvb[BLK:]

            _for_residues(d, residue)

        @pl.when(n == NB)
        def _():
            dk_ref[...] = ck_ref[...]
            dv_ref[...] = cv_ref[...]

    def qn(n):
        return jnp.minimum(n, NB - 1)

    cur, prev = _attn_specs(d, g, qn)
    cw = hps * HEAD_DIM
    row = pl.BlockSpec((d * BLK, cw), lambda hp, n: (qn(n), hp))
    done = pl.BlockSpec((d * BLK, cw), lambda hp, n: (jnp.maximum(n - 1, 0), hp))
    (dq, dk, dv, db), carried = _call(
        body, plans, name=name, grid=(HEADS // hps, NB + 1),
        in_specs=[cur(0), prev(1), cur(1), prev(2), cur(2),
                  pl.BlockSpec((HEADS, BLK, 2 * BLK), lambda hp, n: (0, 0, 0)), row, row, row],
        out_specs=[row, done, done, pl.BlockSpec((HEADS, BLK, 2 * BLK), lambda hp, n: (0, 0, 0))],
        out_shape=[_sds((S, GROUP_W), F32)] * 3 + [_sds((HEADS, BLK, 2 * BLK), F32)],
        scratch_shapes=[pltpu.VMEM((d * BLK, cw), F32)] * 2,
        args=(proj, proj, proj, proj, proj, bias, lse, y, dy))
    return ([dq, dk, dv], db) if plans is None else ([dq, dk, dv], db, carried)


BAND = BLK * 2 * BLK


def _bucket_onehot():
    buckets = jnp.stack([_t5_bucket(_band_rel() * d) for d in DILATIONS]).reshape(N_GROUPS, 1, BAND)
    return (buckets == jnp.arange(NUM_BUCKETS).reshape(1, NUM_BUCKETS, 1)).astype(F32)


def _relbias_fwd(rel_bias, name):
    table = rel_bias.reshape(NUM_BUCKETS, N_GROUPS, HEADS).transpose(1, 0, 2)

    def body(t_ref, oh_ref, o_ref):
        o_ref[...] = lax.dot_general(t_ref[...], oh_ref[...], (TN, ((), ())), preferred_element_type=F32,
                                     precision=lax.Precision.HIGHEST)

    out = pl.pallas_call(
        body, name=name, grid=(N_GROUPS,),
        in_specs=[pl.BlockSpec((None, NUM_BUCKETS, HEADS), lambda g: (g, 0, 0)),
                  pl.BlockSpec((None, NUM_BUCKETS, BAND), lambda g: (g, 0, 0))],
        out_specs=pl.BlockSpec((None, HEADS, BAND), lambda g: (g, 0, 0)),
        out_shape=_sds((N_GROUPS, HEADS, BAND), F32), compiler_params=_cp(1))(table, _bucket_onehot())
    return out.reshape(N_GROUPS, HEADS, BLK, 2 * BLK)


def _relbias_bwd(dbs, name):
    band = BAND
    onehot = _bucket_onehot()
    dbf = jnp.stack([db.reshape(HEADS, band) for db in dbs])

    def body(oh_ref, db_ref, o_ref):
        o_ref[...] = lax.dot_general(oh_ref[...], db_ref[...], (NT, ((), ())), preferred_element_type=F32,
                                     precision=lax.Precision.HIGHEST)

    out = pl.pallas_call(
        body, name=name, grid=(N_GROUPS,),
        in_specs=[pl.BlockSpec((None, NUM_BUCKETS, band), lambda g: (g, 0, 0)),
                  pl.BlockSpec((None, HEADS, band), lambda g: (g, 0, 0))],
        out_specs=pl.BlockSpec((None, NUM_BUCKETS, HEADS), lambda g: (g, 0, 0)),
        out_shape=_sds((N_GROUPS, NUM_BUCKETS, HEADS), F32), compiler_params=_cp(1))(onehot, dbf)
    return out.transpose(1, 0, 2).reshape(NUM_BUCKETS, N_GROUPS * HEADS)


def _chunk_pos(shape):
    return lax.broadcasted_iota(jnp.int32, shape, 0) % HG_CHUNK


def _chunk_cumsum(v):
    pos = _chunk_pos(v.shape)
    s = 1
    while s < HG_CHUNK:
        v = v + jnp.where(pos >= s, pltpu.roll(v, s, 0), 0.0)
        s *= 2
    return v


def _chunk_rev_cumsum(v):
    pos = _chunk_pos(v.shape)
    n = v.shape[0]
    s = 1
    while s < HG_CHUNK:
        v = v + jnp.where(pos < HG_CHUNK - s, pltpu.roll(v, n - s, 0), 0.0)
        s *= 2
    return v


def _lower_bound(raw):
    a0, a1 = raw[0:1], raw[1:2]
    m = jnp.maximum(a0, a1)
    e0, e1 = jnp.exp(a0 - m), jnp.exp(a1 - m)
    return e0 / (e0 + e1)


def _hg_gates(qr, fr, lb):
    sf = _sigmoid(fr)
    f = lb + (1.0 - lb) * sf
    sq = _sigmoid(qr)
    return qr * sq, sq, f, sf


HG_COL0 = QKV_W // HG_W


def _hgrn_fwd(proj, lb_raw, nw, name):
    S = proj.shape[0]
    ncs = HG_TILE // HG_CHUNK
    tril = jnp.tril(jnp.ones((HG_CHUNK, HG_CHUNK), dtype=bool))

    def body(q_ref, f_ref, i_ref, og_ref, lb_ref, nw_ref, y_ref, o_ref, st_ref, state):
        @pl.when(pl.program_id(0) == 0)
        def _():
            state[...] = jnp.zeros_like(state)

        lb = _lower_bound(lb_ref[...])
        q, _, f, _ = _hg_gates(q_ref[...], f_ref[...], lb)
        k = 1.0 - f
        G = _chunk_cumsum(jnp.log(f))
        row = lax.broadcasted_iota(jnp.int32, (HG_CHUNK, HG_CHUNK), 0)
        col = lax.broadcasted_iota(jnp.int32, (HG_CHUNK, HG_CHUNK), 1)
        heads = [slice(h * HG_DK, (h + 1) * HG_DK) for h in range(HG_HEADS)]
        sts = [state[h] for h in range(HG_HEADS)]
        for c in range(ncs):
            cs = slice(c * HG_CHUNK, (c + 1) * HG_CHUNK)
            for h, hs in enumerate(heads):
                Gc = G[cs, hs]
                gl = Gc[HG_CHUNK - 1:HG_CHUNK]
                qt = _bf(q[cs, hs] * jnp.exp(Gc))
                kt = _bf(k[cs, hs] * jnp.exp(-Gc))
                kd = _bf(k[cs, hs] * jnp.exp(gl - Gc))
                v = _bf(i_ref[cs, hs])
                A = jnp.where(row >= col, _dot(qt, kt, NT), 0.0)
                o_ref[cs, hs] = _dot(_bf(A), v, NN) + _dot(qt, _bf(sts[h]), NT)
                st_ref[c, h] = sts[h]
                sts[h] = sts[h] * jnp.exp(gl) + _dot(v, kd, TN)
        for h, hs in enumerate(heads):
            state[h] = sts[h]
            oh = o_ref[:, hs]
            og = og_ref[:, hs]
            y_ref[:, hs] = oh * _rinv(oh) * nw_ref[...] * (og * _sigmoid(og))

    def colspec(j):
        return pl.BlockSpec((HG_TILE, HG_W), lambda i: (i, HG_COL0 + j))

    return pl.pallas_call(
        body, name=name, grid=(S // HG_TILE,),
        in_specs=[colspec(0), colspec(1), colspec(2), colspec(3),
                  pl.BlockSpec((2, HG_W), lambda i: (0, 0)), pl.BlockSpec((1, HG_DK), lambda i: (0, 0))],
        out_specs=[pl.BlockSpec((HG_TILE, HG_W), lambda i: (i, 0))] * 2
        + [pl.BlockSpec((ncs, HG_HEADS, HG_DK, HG_DK), lambda i: (i, 0, 0, 0))],
        out_shape=[_sds((S, HG_W), F32)] * 2 + [_sds((S // HG_CHUNK, HG_HEADS, HG_DK, HG_DK), F32)],
        scratch_shapes=[pltpu.VMEM((HG_HEADS, HG_DK, HG_DK), F32)],
        compiler_params=_cp(1))(proj, proj, proj, proj, lb_raw, nw)


def _hgrn_bwd(proj, lb_raw, nw, o, states, dy, name):
    S = proj.shape[0]
    ncs = HG_TILE // HG_CHUNK
    nt = S // HG_TILE

    def body(q_ref, f_ref, i_ref, og_ref, lb_ref, nw_ref, o_ref, st_ref, dy_ref,
             dq_ref, df_ref, di_ref, dog_ref, dlb_ref, dnw_ref, dstate, do_s, dG_s, dgl_s, dk_s, dlb_s):
        step = pl.program_id(0)

        @pl.when(step == 0)
        def _():
            dstate[...] = jnp.zeros_like(dstate)
            dlb_s[...] = jnp.zeros_like(dlb_s)
            dnw_ref[...] = jnp.zeros_like(dnw_ref)

        lb = _lower_bound(lb_ref[...])
        qr = q_ref[...]
        q, sq, f, sf = _hg_gates(qr, f_ref[...], lb)
        k = 1.0 - f
        G = _chunk_cumsum(jnp.log(f))
        nwv = nw_ref[...]
        row = lax.broadcasted_iota(jnp.int32, (HG_CHUNK, HG_CHUNK), 0)
        col = lax.broadcasted_iota(jnp.int32, (HG_CHUNK, HG_CHUNK), 1)
        for h in range(HG_HEADS):
            hs = slice(h * HG_DK, (h + 1) * HG_DK)
            oh = o_ref[:, hs]
            r = _rinv(oh)
            ohat = oh * r
            og = og_ref[:, hs]
            sg = _sigmoid(og)
            dyh = dy_ref[:, hs]
            don = dyh * (og * sg)
            dog_ref[:, hs] = _bf(dyh * (ohat * nwv) * (sg * (1.0 + og * (1.0 - sg))))
            dnw_ref[...] += jnp.sum(don * ohat, axis=0, keepdims=True)
            do_s[:, hs] = _norm_bwd(don, ohat, r, nwv)
        dsts = [dstate[h] for h in range(HG_HEADS)]
        for c in reversed(range(ncs)):
            cs = slice(c * HG_CHUNK, (c + 1) * HG_CHUNK)
            for h in range(HG_HEADS):
                hs = slice(h * HG_DK, (h + 1) * HG_DK)
                dst = dsts[h]
                Gc = G[cs, hs]
                gl = Gc[HG_CHUNK - 1:HG_CHUNK]
                eG, enG, edG, egl = jnp.exp(Gc), jnp.exp(-Gc), jnp.exp(gl - Gc), jnp.exp(gl)
                qt, kt, kd = q[cs, hs] * eG, k[cs, hs] * enG, k[cs, hs] * edG
                qtb, ktb, kdb = _bf(qt), _bf(kt), _bf(kd)
                v = _bf(i_ref[cs, hs])
                do = _bf(do_s[cs, hs])
                st = st_ref[c, h]
                dstb = _bf(dst)
                A = jnp.where(row >= col, _dot(qtb, ktb, NT), 0.0)
                dA = _bf(jnp.where(row >= col, _dot(do, v, NT), 0.0))
                di_ref[cs, hs] = _bf(_dot(_bf(A), do, TN) + _dot(kdb, dstb, NT))
                dqt = _dot(dA, ktb, NN) + _dot(do, _bf(st), NN)
                dkt = _dot(dA, qtb, TN)
                dkd = _dot(v, dstb, NN)
                dgl = egl * jnp.sum(st * dst, axis=0, keepdims=True) + jnp.sum(dkd * kd, axis=0, keepdims=True)
                dsts[h] = dst * egl + _dot(do, qtb, TN)
                dq_ref[cs, hs] = _bf(dqt * eG * (sq[cs, hs] * (1.0 + qr[cs, hs] * (1.0 - sq[cs, hs]))))
                dk_s[cs, hs] = dkt * enG + dkd * edG
                dG_s[cs, hs] = dqt * qt - dkt * kt - dkd * kd
                dgl_s[cs, hs] = jnp.broadcast_to(dgl, (HG_CHUNK, HG_DK))
        for h in range(HG_HEADS):
            dstate[h] = dsts[h]
        dg = _chunk_rev_cumsum(dG_s[...]) + dgl_s[...]
        dfv = dg / f - dk_s[...]
        df_ref[...] = _bf(dfv * (1.0 - lb) * sf * (1.0 - sf))
        dlb_s[...] += jnp.sum(dfv * (1.0 - sf), axis=0, keepdims=True)

        @pl.when(step == nt - 1)
        def _():
            t = dlb_s[...] * lb * (1.0 - lb)
            dlb_ref[...] = jnp.concatenate([t, -t], axis=0)

    def colspec(j):
        return pl.BlockSpec((HG_TILE, HG_W), lambda i: (nt - 1 - i, HG_COL0 + j))

    tile = pl.BlockSpec((HG_TILE, HG_W), lambda i: (nt - 1 - i, 0))
    outs = pl.pallas_call(
        body, name=name, grid=(nt,),
        in_specs=[colspec(0), colspec(1), colspec(2), colspec(3),
                  pl.BlockSpec((2, HG_W), lambda i: (0, 0)), pl.BlockSpec((1, HG_DK), lambda i: (0, 0)),
                  tile, pl.BlockSpec((ncs, HG_HEADS, HG_DK, HG_DK), lambda i: (nt - 1 - i, 0, 0, 0)), tile],
        out_specs=[tile] * 4 + [pl.BlockSpec((2, HG_W), lambda i: (0, 0)), pl.BlockSpec((1, HG_DK), lambda i: (0, 0))],
        out_shape=[_sds((S, HG_W), BF16)] * 4 + [_sds((2, HG_W), F32), _sds((1, HG_DK), F32)],
        scratch_shapes=[pltpu.VMEM((HG_HEADS, HG_DK, HG_DK), F32)] + [pltpu.VMEM((HG_TILE, HG_W), F32)] * 4
        + [pltpu.VMEM((1, HG_W), F32)],
        compiler_params=_cp(1))(proj, proj, proj, proj, lb_raw, nw, o, states, dy)
    return outs[:4], outs[4], outs[5]


GATE_COL0 = (QKV_W + 4 * HG_W) // GROUP_W
HALF_D = D_MODEL // 2


def _gate_tiles(proj):
    return [_tile(proj, HALF_D, functools.partial(lambda c, k: GATE_COL0 + k, k=k)) for k in range(4)]


def _gates(g_refs):
    s0 = _sigmoid(jnp.concatenate([g_refs[0][...], g_refs[1][...]], axis=1))
    s1 = _sigmoid(jnp.concatenate([g_refs[2][...], g_refs[3][...]], axis=1))
    return s0, s1


def _branch_fwd(y, yh, proj, w_a, w_h, name):
    nb = w_a.shape[0]

    def body(y_ref, yh_ref, g0a, g0b, g1a, g1b, wa_ref, wh_ref, za_ref, zh_ref, m_ref):
        yb, yhb = _bf(y_ref[...]), _bf(yh_ref[...])
        za = jnp.concatenate([_dot(yb, wa_ref[j], NN) for j in range(nb)], axis=1)
        zh = jnp.concatenate([_dot(yhb, wh_ref[j], NN) for j in range(nb)], axis=1)
        s0, s1 = _gates((g0a, g0b, g1a, g1b))
        za_ref[...] = za
        zh_ref[...] = zh
        m_ref[...] = _bf(s0 * za + s1 * zh)

    return _rows_call(name, body, y.shape[0], 512, 1,
                      [_tile(y, GROUP_W), _tile(yh, HG_W), *_gate_tiles(proj), _full(w_a), _full(w_h)],
                      [_out_tile(D_MODEL, F32, D_MODEL)] * 2 + [_out_tile(D_MODEL, BF16, D_MODEL)])


def _branch_bwd(dm, za, zh, proj, w_a, w_h, name):
    nb, _, Nb = w_a.shape

    def body(dm_ref, za_ref, zh_ref, g0a, g0b, g1a, g1b, wa_ref, wh_ref,
             dza_ref, dzh_ref, dg0_ref, dg1_ref, dy_ref, dyh_ref):
        dmv = dm_ref[...]
        s0, s1 = _gates((g0a, g0b, g1a, g1b))
        dza, dzh = _bf(dmv * s0), _bf(dmv * s1)
        dza_ref[...] = dza
        dzh_ref[...] = dzh
        dg0_ref[...] = _bf(dmv * za_ref[...] * s0 * (1.0 - s0))
        dg1_ref[...] = _bf(dmv * zh_ref[...] * s1 * (1.0 - s1))
        dy_ref[...] = sum(_dot(dza[:, j * Nb:(j + 1) * Nb], wa_ref[j], NT) for j in range(nb))
        dyh_ref[...] = sum(_dot(dzh[:, j * Nb:(j + 1) * Nb], wh_ref[j], NT) for j in range(nb))

    return _rows_call(name, body, za.shape[0], 512, 1,
                      [_tile(dm, D_MODEL), _tile(za, D_MODEL), _tile(zh, D_MODEL), *_gate_tiles(proj),
                       _full(w_a), _full(w_h)],
                      [_out_tile(D_MODEL, BF16, D_MODEL)] * 4 + [_out_tile(GROUP_W, F32, GROUP_W),
                                                                 _out_tile(HG_W, F32, HG_W)])


def _mix_out(merged, w_out, x, w_post, w_pre, name):
    def body(m_ref, wo_ref, x_ref, wp_ref, wf_ref, mo_ref, x1_ref, h2_ref):
        z = _dot(m_ref[...], wo_ref[...], NN)
        mo_ref[...] = z
        x1 = x_ref[...] + z * _rinv(z) * wp_ref[...]
        x1_ref[...] = x1
        h2_ref[...] = _bf(x1 * _rinv(x1) * wf_ref[...])

    return _rows_call(name, body, x.shape[0], 512, 1,
                      [_tile(merged, D_MODEL), _full(w_out), _tile(x, D_MODEL), _full(w_post), _full(w_pre)],
                      [_out_tile(D_MODEL, F32, D_MODEL), _out_tile(D_MODEL, F32, D_MODEL),
                       _out_tile(D_MODEL, BF16, D_MODEL)])


def _loss_head(a, w_down, x1, tgt, w, name):
    def body(a_ref, wd_ref, x1_ref, t_ref, w_ref, dx_ref, df_ref, dw_ref, loss_ref):
        z = _dot(a_ref[...], wd_ref[...], NN)
        r = _rinv(z)
        zhat = z * r
        wv = w_ref[...]
        e = x1_ref[...] + zhat * wv - t_ref[...]
        dx = e * (1.0 / D_MODEL)
        dx_ref[...] = dx
        df_ref[...] = _bf(_norm_bwd(dx, zhat, r, wv))
        _acc(dw_ref, jnp.sum(dx * zhat, axis=0, keepdims=True))
        part = 0.5 * jnp.sum(jnp.sum(e * e, axis=1, keepdims=True), axis=0, keepdims=True) * (1.0 / D_MODEL)
        _acc(loss_ref, jnp.broadcast_to(part, (1, LANES)))

    return _rows_call(name, body, x1.shape[0], 512, 1,
                      [_tile(a, D_FF), _full(w_down), _tile(x1, D_MODEL), _tile(tgt, D_MODEL), _full(w)],
                      [_out_tile(D_MODEL, F32, D_MODEL), _out_tile(D_MODEL, BF16, D_MODEL),
                       _out_acc(1, D_MODEL, D_MODEL), _out_acc(1, LANES, LANES)])


CONV_CB = D_FF // 2
CONV_TM = 512
HALO = 8
SQRT_HALF = 0.7071067811865476
INV_SQRT_2PI = 0.3989422804014327


def _conv_taps(u_ref, halo_ref, first):
    u = u_ref[...]
    row = lax.broadcasted_iota(jnp.int32, u.shape, 0)
    p1 = jnp.where(first, 0.0, halo_ref[HALO - 1:HALO, :])
    p2 = jnp.where(first, 0.0, halo_ref[HALO - 2:HALO - 1, :])
    u1 = jnp.where(row == 0, p1, pltpu.roll(u, 1, 0))
    u2 = jnp.where(row == 0, p2, jnp.where(row == 1, p1, pltpu.roll(u, 2, 0)))
    return u2, u1, u


def _conv(taps, w_ref, b_ref):
    return b_ref[...] + w_ref[0:1, :] * taps[0] + w_ref[1:2, :] * taps[1] + w_ref[2:3, :] * taps[2]


def _conv_specs(tm):
    nh = tm // HALO
    nc = D_FF // CONV_CB

    def tile(off):
        return pl.BlockSpec((tm, CONV_CB), lambda c, i: (i, off + c))

    def halo(off):
        return pl.BlockSpec((HALO, CONV_CB), lambda c, i: (jnp.maximum(i * nh - 1, 0), off + c))

    def small(rows, off):
        return pl.BlockSpec((rows, CONV_CB), lambda c, i: (0, off + c))

    return nc, tile, halo, small


def _conv_gelu_fwd(u, cw, cb, name):
    S = u.shape[0]
    tm = CONV_TM
    nc, tile, halo, small = _conv_specs(tm)

    def body(ug, hg, uv, hv, wg, wv, bg, bv, a_ref):
        first = pl.program_id(1) == 0
        cg = _conv(_conv_taps(ug, hg, first), wg, bg)
        cv = _conv(_conv_taps(uv, hv, first), wv, bv)
        a_ref[...] = _bf(0.5 * cg * (1.0 + lax.erf(cg * SQRT_HALF)) * cv)

    return pl.pallas_call(
        body, name=name, grid=(nc, S // tm),
        in_specs=[tile(0), halo(0), tile(nc), halo(nc), small(3, 0), small(3, nc), small(1, 0), small(1, nc)],
        out_specs=tile(0), out_shape=_sds((S, D_FF), BF16), compiler_params=_cp(2))(u, u, u, u, cw, cw, cb, cb)


def _conv_gelu_bwd(u, da, cw, cb, name, plans=None):
    S = u.shape[0]
    tm = CONV_TM
    nc, tile, halo, small = _conv_specs(tm)

    def body(ug, hg, uv, hv, wg, wv, bg, bv, da_ref, dcg_ref, dcv_ref, dwg_ref, dwv_ref, dbg_ref, dbv_ref):
        first = pl.program_id(1) == 0
        tg = _conv_taps(ug, hg, first)
        tv = _conv_taps(uv, hv, first)
        cg = _conv(tg, wg, bg)
        cv = _conv(tv, wv, bv)
        phi = 0.5 * (1.0 + lax.erf(cg * SQRT_HALF))
        dav = da_ref[...]
        dcg = dav * cv * (phi + cg * jnp.exp(-0.5 * cg * cg) * INV_SQRT_2PI)
        dcv = dav * (cg * phi)
        dcg_ref[...] = dcg
        dcv_ref[...] = dcv
        for dc, taps, dw_ref, db_ref in ((dcg, tg, dwg_ref, dbg_ref), (dcv, tv, dwv_ref, dbv_ref)):
            _acc(db_ref, jnp.sum(dc, axis=0, keepdims=True))
            for j in range(3):
                _acc(dw_ref.at[j:j + 1, :], jnp.sum(dc * taps[j], axis=0, keepdims=True))

    res, carried = _call(
        body, plans, name=name, grid=(nc, S // tm),
        in_specs=[tile(0), halo(0), tile(nc), halo(nc), small(3, 0), small(3, nc), small(1, 0), small(1, nc), tile(0)],
        out_specs=[tile(0), tile(0), small(3, 0), small(3, 0), small(1, 0), small(1, 0)],
        out_shape=[_sds((S, D_FF), F32)] * 2 + [_sds((3, D_FF), F32)] * 2 + [_sds((1, D_FF), F32)] * 2,
        args=(u, u, u, u, cw, cw, cb, cb, da))
    return res if plans is None else (res, carried)


def _conv_input_bwd(dcg, dcv, cw, name, plans=None):
    S = dcg.shape[0]
    tm = CONV_TM
    nc, tile, _, small = _conv_specs(tm)
    nh = tm // HALO
    nt = S // tm

    def nxt(off):
        return pl.BlockSpec((HALO, CONV_CB), lambda c, i: (jnp.minimum((i + 1) * nh, S // HALO - 1), off + c))

    def body(g_ref, ng_ref, v_ref, nv_ref, wg, wv, dug_ref, duv_ref):
        last = pl.program_id(1) == nt - 1
        for dc_ref, n_ref, w_ref, du_ref in ((g_ref, ng_ref, wg, dug_ref), (v_ref, nv_ref, wv, duv_ref)):
            dc = dc_ref[...]
            row = lax.broadcasted_iota(jnp.int32, dc.shape, 0)
            n1 = jnp.where(last, 0.0, n_ref[0:1, :])
            n2 = jnp.where(last, 0.0, n_ref[1:2, :])
            d1 = jnp.where(row == tm - 1, n1, pltpu.roll(dc, tm - 1, 0))
            d2 = jnp.where(row == tm - 1, n2, jnp.where(row == tm - 2, n1, pltpu.roll(dc, tm - 2, 0)))
            du_ref[...] = _bf(w_ref[2:3, :] * dc + w_ref[1:2, :] * d1 + w_ref[0:1, :] * d2)

    res, carried = _call(
        body, plans, name=name, grid=(nc, nt),
        in_specs=[tile(0), nxt(0), tile(0), nxt(0), small(3, 0), small(3, nc)],
        out_specs=[tile(0), tile(0)], out_shape=[_sds((S, D_FF), BF16)] * 2,
        args=(dcg, dcg, dcv, dcv, cw, cw))
    return res if plans is None else (res, carried)


def _row_tile(n, cap):
    best = n
    for t in range(16, cap + 1, 16):
        if n % t == 0:
            best = t
    return best if best <= cap else n


def _adamw(w, g, m, v, name):
    R, C = w.shape
    tr = _row_tile(R, max(16, (2 * 1024 * 1024) // (4 * C) // 16 * 16))

    def body(w_ref, g_ref, m_ref, v_ref, d_ref, nm_ref, nv_ref):
        gv = g_ref[...]
        nm = ADAM_B1 * m_ref[...] + (1.0 - ADAM_B1) * gv
        nv = ADAM_B2 * v_ref[...] + (1.0 - ADAM_B2) * (gv * gv)
        m_hat = nm / (1.0 - ADAM_B1 ** ADAM_STEP)
        v_hat = nv / (1.0 - ADAM_B2 ** ADAM_STEP)
        d_ref[...] = -ADAM_LR * (m_hat / (jnp.sqrt(v_hat) + ADAM_EPS) + ADAM_WD * w_ref[...])
        nm_ref[...] = nm
        nv_ref[...] = nv

    spec = pl.BlockSpec((tr, C), lambda i: (i, 0))
    return pl.pallas_call(body, name=name, grid=(R // tr,), in_specs=[spec] * 4, out_specs=[spec] * 3,
                          out_shape=[_sds((R, C), F32)] * 3, compiler_params=_cp(1))(w, g, m, v)


def _pair_sum(gfull, rcv, c_idx, name):
    nb, R, C = gfull.shape
    half = R // 2
    tr = _row_tile(half, 256)
    nt = half // tr

    def body(c_ref, g_ref, r_ref, o_ref):
        o_ref[...] = _bf(g_ref[...] + r_ref[...])

    return pl.pallas_call(
        body, name=name,
        grid_spec=pltpu.PrefetchScalarGridSpec(
            num_scalar_prefetch=1, grid=(nb, nt),
            in_specs=[pl.BlockSpec((None, tr, C), lambda j, i, c_ref: (j, c_ref[0] * nt + i, 0)),
                      pl.BlockSpec((None, tr, C), lambda j, i, c_ref: (j, i, 0))],
            out_specs=pl.BlockSpec((None, tr, C), lambda j, i, c_ref: (j, i, 0))),
        out_shape=_sds((nb, half, C), BF16), compiler_params=_cp(2))(c_idx, gfull, rcv)


def _chip_sum(arrived, own, place, name, row0, shard_rows, into=None):
    nb, H, C = arrived.shape
    tr = _row_tile(H, 256)
    nt = H // tr
    assert row0 % tr == 0
    extra = [] if into is None else [into]

    def body(pl_ref, *refs):
        o_ref = refs[nb + 1 + len(extra)]
        me = pl_ref[0]
        acc = None
        for k in range(nb):
            term = jnp.where(me == k, refs[nb][...], refs[k][...]).astype(F32)
            acc = term if acc is None else acc + term
        o_ref[...] = acc

    def other(k):
        return pl.BlockSpec((None, tr, C), lambda i, p: (jnp.where(p[0] == k, (k + 1) % nb, k), i, 0))

    return pl.pallas_call(
        body, name=name,
        grid_spec=pltpu.PrefetchScalarGridSpec(
            num_scalar_prefetch=1, grid=(nt,),
            in_specs=[other(k) for k in range(nb)] + [pl.BlockSpec((None, tr, C), lambda i, p: (p[0], i, 0))]
            + [ANY] * len(extra),
            out_specs=pl.BlockSpec((tr, C), lambda i, p: (row0 // tr + p[1] * nt + i, 0))),
        out_shape=_sds((shard_rows, C), F32),
        input_output_aliases={1 + nb + 1: 0} if extra else {},
        compiler_params=_cp(1))(place, *([arrived] * nb), own, *extra)


def _cast_into_slot(shard, place, name):
    R, C = shard.shape
    tr = _row_tile(R, 256)

    def body(pl_ref, s_ref, o_ref):
        o_ref[...] = _bf(s_ref[...])

    return pl.pallas_call(
        body, name=name,
        grid_spec=pltpu.PrefetchScalarGridSpec(
            num_scalar_prefetch=1, grid=(R // tr,),
            in_specs=[pl.BlockSpec((tr, C), lambda i, p: (i, 0))],
            out_specs=pl.BlockSpec((None, tr, C), lambda i, p: (p[0], i, 0))),
        out_shape=_sds((N_CHIPS, R, C), BF16), compiler_params=_cp(1))(place, shard)


def _place():
    x, y, c = lax.axis_index("x"), lax.axis_index("y"), lax.axis_index("c")
    chips = [(1 - x, y), (x, 1 - y), (1 - x, 1 - y)]
    return x, y, c, chips


def _chip_id(px, py):
    return 2 * px + py


def _remote(src, dst, send_sems, recv_sems, k, to):
    return pltpu.make_async_remote_copy(src_ref=src, dst_ref=dst, send_sem=send_sems.at[k], recv_sem=recv_sems.at[k],
                                        device_id=to, device_id_type=MESH)


def _gather_weights(slots, wholes, name):
    ns, nw = len(slots), len(wholes)
    n = ns + nw

    def body(*refs):
        ins = refs[ns:n]
        outs = refs[n:2 * n]
        send_sems, recv_sems, local_sems = refs[2 * n:]
        x, y, c, chips = _place()
        me = _chip_id(x, y)
        sib = (x, y, 1 - c)
        local = [pltpu.make_async_copy(ins[b], outs[ns + b].at[me], local_sems.at[b]) for b in range(nw)]
        for cp in local:
            cp.start()
        sent = []
        for a in range(n):
            R = outs[a].shape[1]
            rows = pl.ds(c * (R // 2), R // 2) if a < ns else pl.ds(0, R)
            src = outs[a].at[me, rows] if a < ns else ins[a - ns]
            for j, chip in enumerate(chips):
                cp = _remote(src, outs[a].at[me, rows], send_sems, recv_sems, 6 * a + j, (*chip, c))
                cp.start()
                sent.append(cp)
        for a in range(n):
            R = outs[a].shape[1]
            rows = pl.ds(c * (R // 2), R // 2) if a < ns else pl.ds(0, R)
            for j, chip in enumerate(chips):
                landed = outs[a].at[_chip_id(*chip), rows]
                _remote(landed, landed, send_sems, recv_sems, 6 * a + j, (*chip, c)).wait_recv()
                if a < ns:
                    cp = _remote(landed, landed, send_sems, recv_sems, 6 * a + 3 + j, sib)
                    cp.start()
                    sent.append(cp)
        for a in range(ns):
            R = outs[a].shape[1]
            other = pl.ds((1 - c) * (R // 2), R // 2)
            for j, chip in enumerate(chips):
                passed = outs[a].at[_chip_id(*chip), other]
                _remote(passed, passed, send_sems, recv_sems, 6 * a + 3 + j, sib).wait_recv()
        for cp in sent:
            cp.wait_send()
        for cp in local:
            cp.wait()

    return pl.pallas_call(
        body, name=name, in_specs=[ANY] * n, out_specs=[ANY] * n,
        out_shape=[_sds(s.shape, s.dtype) for s in slots] + [_sds((N_CHIPS, *s.shape), s.dtype) for s in wholes],
        input_output_aliases={a: a for a in range(ns)},
        scratch_shapes=[pltpu.SemaphoreType.DMA((6 * n,)), pltpu.SemaphoreType.DMA((6 * n,)),
                        pltpu.SemaphoreType.DMA((max(nw, 1),))])(*slots, *wholes)


def _gather_ici_plan(slots, wholes):
    ns, nw = len(slots), len(wholes)

    def copies(ins, ios, outs, send_sems, recv_sems, local_sems):
        x, y, c, chips = _place()
        me = _chip_id(x, y)
        sends, recvs = [], []
        for a in range(ns + nw):
            dst = ios[a] if a < ns else outs[a - ns]
            R = dst.shape[1]
            rows = pl.ds(c * (R // 2), R // 2) if a < ns else pl.ds(0, R)
            src = dst.at[me, rows] if a < ns else ins[a - ns]
            for j, chip in enumerate(chips):
                sends.append(_remote(src, dst.at[me, rows], send_sems, recv_sems, 3 * a + j, (*chip, c)))
                landed = dst.at[_chip_id(*chip), rows]
                recvs.append(_remote(landed, landed, send_sems, recv_sems, 3 * a + j, (*chip, c)))
        local = [pltpu.make_async_copy(ins[b], outs[b].at[me], local_sems.at[b]) for b in range(nw)]
        return sends, recvs, local

    return _Plan(copies, 3 * (ns + nw), ins=wholes, inouts=slots,
                 outs=[_sds((N_CHIPS, *s.shape), s.dtype) for s in wholes])


def _gather_pass_plan(slots):
    def copies(ins, ios, outs, send_sems, recv_sems, local_sems):
        x, y, c, chips = _place()
        sib = (x, y, 1 - c)
        sends, recvs = [], []
        for a, buf in enumerate(ios):
            half = buf.shape[1] // 2
            for j, chip in enumerate(chips):
                mine = buf.at[_chip_id(*chip), pl.ds(c * half, half)]
                other = buf.at[_chip_id(*chip), pl.ds((1 - c) * half, half)]
                sends.append(_remote(mine, mine, send_sems, recv_sems, 3 * a + j, sib))
                recvs.append(_remote(other, other, send_sems, recv_sems, 3 * a + j, sib))
        return sends, recvs, []

    return _Plan(copies, 3 * len(slots), inouts=slots)


def _pair_plan(grads):
    def copies(ins, ios, outs, send_sems, recv_sems, local_sems):
        x, y, c, _ = _place()
        sib = (x, y, 1 - c)
        sends, recvs = [], []
        for a, g in enumerate(ins):
            half = g.shape[1] // 2
            sends.append(_remote(g.at[:, pl.ds((1 - c) * half, half), :], outs[a], send_sems, recv_sems, a, sib))
            recvs.append(_remote(outs[a], outs[a], send_sems, recv_sems, a, sib))
        return sends, recvs, []

    return _Plan(copies, len(grads), ins=grads,
                 outs=[_sds((g.shape[0], g.shape[1] // 2, g.shape[2]), g.dtype) for g in grads])


def _chip_plan(parts):
    def copies(ins, ios, outs, send_sems, recv_sems, local_sems):
        x, y, c, chips = _place()
        me = _chip_id(x, y)
        sends, recvs = [], []
        for a, part in enumerate(ins):
            for j, chip in enumerate(chips):
                sends.append(_remote(part.at[_chip_id(*chip)], outs[a].at[me], send_sems, recv_sems, 3 * a + j, (*chip, c)))
                landed = outs[a].at[_chip_id(*chip)]
                recvs.append(_remote(landed, landed, send_sems, recv_sems, 3 * a + j, (*chip, c)))
        return sends, recvs, []

    return _Plan(copies, 3 * len(parts), ins=parts, outs=[_sds(p.shape, p.dtype) for p in parts])


def _pair_concat(fulls, pieces, name):
    n = len(fulls)
    spans = [(a, row0, rows // 2) for a in range(n) for (row0, rows) in pieces[a]]

    def body(*refs):
        outs = refs[n:2 * n]
        send_sems, recv_sems = refs[2 * n:]
        x, y, c, _ = _place()
        cps = []
        for s, (a, row0, H) in enumerate(spans):
            mine = outs[a].at[pl.ds(row0 + c * H, H)]
            cp = _remote(mine, mine, send_sems, recv_sems, s, (x, y, 1 - c))
            cp.start()
            cps.append(cp)
        for s, (a, row0, H) in enumerate(spans):
            other = outs[a].at[pl.ds(row0 + (1 - c) * H, H)]
            _remote(other, other, send_sems, recv_sems, s, (x, y, 1 - c)).wait_recv()
            cps[s].wait_send()

    return pl.pallas_call(
        body, name=name, in_specs=[ANY] * n, out_specs=[ANY] * n,
        out_shape=[_sds(f.shape, f.dtype) for f in fulls], input_output_aliases={a: a for a in range(n)},
        scratch_shapes=[pltpu.SemaphoreType.DMA((len(spans),)), pltpu.SemaphoreType.DMA((len(spans),))])(*fulls)


def _all_sum(pack, name):
    R, C = pack.shape

    def body(p_ref, o_ref, buf, send_sems, recv_sems):
        x, y, c, _ = _place()
        me = 4 * x + 2 * y + c
        buf[me] = p_ref[...]
        cps = []
        for k in range(1, N_DEV):
            to = (x ^ (k >> 2), y ^ ((k >> 1) & 1), c ^ (k & 1))
            cp = _remote(p_ref, buf.at[me], send_sems, recv_sems, k - 1, to)
            cp.start()
            cps.append(cp)
        for k in range(1, N_DEV):
            frm = (x ^ (k >> 2), y ^ ((k >> 1) & 1), c ^ (k & 1))
            slot = buf.at[4 * frm[0] + 2 * frm[1] + frm[2]]
            _remote(slot, slot, send_sems, recv_sems, k - 1, frm).wait_recv()
        acc = buf[0]
        for k in range(1, N_DEV):
            acc = acc + buf[k]
        o_ref[...] = acc
        for cp in cps:
            cp.wait_send()

    vm = pl.BlockSpec(memory_space=pltpu.VMEM)
    return pl.pallas_call(
        body, name=name, in_specs=[vm], out_specs=vm, out_shape=_sds((R, C), F32),
        scratch_shapes=[pltpu.VMEM((N_DEV, R, C), F32), pltpu.SemaphoreType.DMA((N_DEV - 1,)),
                        pltpu.SemaphoreType.DMA((N_DEV - 1,))])(pack)


def _local_step(xs, tgt, p, ex):
    h1 = _norm_fwd(xs, p["pre_mix_norm"], "pre_mix_norm")
    proj, got = _mm_nn_blk(h1, ex.weight("w_in"), "proj_in", plans=ex.carry("proj_in"))
    ex.done("proj_in", got)
    biases = _relbias_fwd(p["rel_bias"], "rel_bias_fwd")
    fw = []
    for g in range(N_GROUPS):
        res, got = _attn_fwd(proj, biases[g], g, f"attn_fwd{g}", plans=ex.carry(f"attn_fwd{g}"))
        ex.done(f"attn_fwd{g}", got)
        fw.append(res)
    y, lse = _attn_merge([t[0] for t in fw], [t[1] for t in fw], "attn_merge")
    yh, o_h, states = _hgrn_fwd(proj, p["hgrn_lb_raw"], p["hgrn_norm"], "hgrn_fwd")
    W_a, W_h, W_out = ex.weight("w_branch_attn"), ex.weight("w_branch_hgrn"), ex.weight("w_out")
    W_up, W_down, conv_w = ex.weight("w_up"), ex.weight("w_down"), ex.weight("conv_w")
    za, zh, merged = _branch_fwd(y, yh, proj, W_a, W_h, "branch_fwd")
    mo, x1, h2 = _mix_out(merged, W_out, xs, p["post_mix_norm"], p["pre_ffn_norm"], "mix_out")
    u = _mm_nn_blk(h2, W_up, "ffn_up")
    a = _conv_gelu_fwd(u, conv_w, p["conv_b"], "conv_gelu_fwd")
    dx2, dff, g_post_ffn, loss = _loss_head(a, W_down, x1, tgt, p["post_ffn_norm"], "ffn_down_loss")

    da = _mm_nt(dff, W_down, "d_ffn_act")
    ex.grad("w_down", _mm_tn(a, dff, "g_w_down").reshape(N_CHIPS, D_FF // N_CHIPS, D_MODEL))
    (dcg, dcv, gwg, gwv, gbg, gbv), got = _conv_gelu_bwd(u, da, conv_w, p["conv_b"], "conv_gelu_bwd",
                                                          plans=ex.carry("conv_gelu_bwd"))
    ex.done("conv_gelu_bwd", got)
    g_conv_w = jnp.concatenate([gwg, gwv], axis=1)
    g_conv_b = jnp.concatenate([gbg, gbv], axis=1)
    du_parts, got = _conv_input_bwd(dcg, dcv, conv_w, "conv_input_bwd", plans=ex.carry("conv_input_bwd"))
    ex.done("conv_input_bwd", got)
    du = jnp.concatenate(du_parts, axis=1)
    dh2 = _mm_nt_blk(du, W_up, "d_ffn_in")
    ex.grad("w_up", _mm_tn_blk(h2, du, N_CHIPS, "g_w_up"))
    (dx1, g_pre_ffn), got = _prenorm_bwd(dh2, x1, p["pre_ffn_norm"], dx2, "pre_ffn_norm_bwd",
                                         plans=ex.carry("pre_ffn_norm_bwd"))
    ex.done("pre_ffn_norm_bwd", got)
    dmo, dmerged, g_post_mix = _postnorm_bwd(dx1, mo, p["post_mix_norm"], W_out, "post_mix_norm_bwd")
    ex.grad("w_out", _mm_tn(merged, dmo, "g_w_out").reshape(N_CHIPS, D_MODEL // N_CHIPS, D_MODEL))
    dza, dzh, dg0, dg1, dy, dyh = _branch_bwd(dmerged, za, zh, proj, W_a, W_h, "branch_bwd")
    ex.grad("w_branch_attn", _mm_tn_blk(y, dza, N_CHIPS, "g_w_branch_attn", together=True))
    ex.grad("w_branch_hgrn", _mm_tn_blk(yh, dzh, N_CHIPS, "g_w_branch_hgrn", together=True))
    dqkv, dbs = [], []
    for g in range(N_GROUPS):
        parts, db, got = _attn_bwd(proj, biases[g], lse, y, dy, g, f"attn_bwd{g}", plans=ex.carry(f"attn_bwd{g}"))
        ex.done(f"attn_bwd{g}", got)
        dqkv += parts
        dbs.append(db)
    g_rel_bias = _relbias_bwd(dbs, "rel_bias_bwd")
    dhg, g_lb_raw, g_hgrn_norm = _hgrn_bwd(proj, p["hgrn_lb_raw"], p["hgrn_norm"], o_h, states, dyh, "hgrn_bwd")
    dproj = jnp.concatenate([*[_bf(t) for t in dqkv], *dhg, dg0, dg1], axis=1)
    for piece, row0, rows in PIECES["w_in"]:
        g, got = _mm_tn_blk(h1, dproj, N_CHIPS, f"g_{piece}", x_cols=(row0 // rows, rows), plans=ex.carry(f"g_{piece}"))
        ex.done(f"g_{piece}", got)
        ex.grad(piece, g)
    dh1, got = _mm_nt_blk(dproj, ex.weight("w_in"), "d_proj_in", plans=ex.carry("d_proj_in"))
    ex.done("d_proj_in", got)
    (grad_x, g_pre_mix), got = _prenorm_bwd(dh1, xs, p["pre_mix_norm"], dx1, "pre_mix_norm_bwd",
                                            plans=ex.carry("pre_mix_norm_bwd"))
    ex.done("pre_mix_norm_bwd", got)
    small = dict(pre_mix_norm=g_pre_mix, rel_bias=g_rel_bias, hgrn_lb_raw=g_lb_raw, hgrn_norm=g_hgrn_norm,
                 post_mix_norm=g_post_mix, pre_ffn_norm=g_pre_ffn, conv_w=g_conv_w, conv_b=g_conv_b,
                 post_ffn_norm=g_post_ffn)
    return loss, grad_x, small


SMALL = ("pre_mix_norm", "rel_bias", "hgrn_lb_raw", "hgrn_norm", "post_mix_norm", "pre_ffn_norm", "conv_w", "conv_b",
         "post_ffn_norm")
BIG = ("w_in", "w_up", "w_down", "w_out", "w_branch_attn", "w_branch_hgrn")
WEIGHTS = ("pre_mix_norm", "w_in", "rel_bias", "hgrn_lb_raw", "hgrn_norm", "w_branch_attn", "w_branch_hgrn", "w_out",
           "post_mix_norm", "pre_ffn_norm", "w_up", "conv_w", "conv_b", "w_down", "post_ffn_norm")
MIXER = ("w_out", "w_branch_attn", "w_branch_hgrn")

SCHEDULE = {
    "proj_in": [("gather_ici_cw", ("w_up",) + MIXER)],
    "attn_fwd0": [("gather_pass", ("w_up",) + MIXER), ("gather_ici", ("w_down",))],
    "attn_fwd1": [("gather_pass", ("w_down",))],
    "conv_gelu_bwd": [("pair", ("w_down",))],
    "conv_input_bwd": [("chip", ("w_down",))],
    "pre_ffn_norm_bwd": [("pair", ("w_up",))],
    "attn_bwd0": [("chip", ("w_up",)), ("pair", MIXER)],
    "attn_bwd1": [("chip", MIXER)],
    "g_w_in_b": [("pair", ("w_in_a",))],
    "g_w_in_c": [("pair", ("w_in_b",))],
    "d_proj_in": [("chip", ("w_in_a", "w_in_b")), ("pair", ("w_in_c",))],
    "pre_mix_norm_bwd": [("chip", ("w_in_c",))],
}
SHARD_ROWS = dict(w_in=D_MODEL, w_up=D_MODEL, w_down=D_FF // N_CHIPS, w_out=D_MODEL // N_CHIPS,
                  w_branch_attn=GROUP_W, w_branch_hgrn=HG_W)
PIECES = {n: ((n, 0, SHARD_ROWS[n]),) for n in BIG}
PIECES["w_in"] = (("w_in_a", 0, 512), ("w_in_b", 512, 256), ("w_in_c", 768, 256))


class _Exchange:
    def __init__(self, place, slots, conv_w_shard):
        self.place, self.slots, self.conv_w_shard = place, dict(slots), conv_w_shard
        self.conv_w = None
        self.g, self.from_sibling, self.pair_sums, self.arrived = {}, {}, {}, {}
        self.pending = []

    def weight(self, name):
        if name == "conv_w":
            return self.conv_w
        w = self.slots[name]
        return w.reshape(-1, D_MODEL) if name in ("w_out", "w_down") else w

    def grad(self, name, g):
        self.g[name] = g

    def carry(self, point):
        plans = []
        self.pending = SCHEDULE.get(point, [])
        for kind, names in self.pending:
            if kind in ("gather_ici", "gather_ici_cw"):
                wholes = [self.conv_w_shard] if kind == "gather_ici_cw" else []
                plans.append(_gather_ici_plan([self.slots[n] for n in names], wholes))
            elif kind == "gather_pass":
                plans.append(_gather_pass_plan([self.slots[n] for n in names]))
            elif kind == "pair":
                plans.append(_pair_plan([self.g[n] for n in names]))
            else:
                for n in names:
                    self.pair_sums[n] = _pair_sum(self.g[n], self.from_sibling[n], self.place[1:2], f"pair_sum_{n}")
                plans.append(_chip_plan([self.pair_sums[n] for n in names]))
        return plans

    def done(self, point, carried):
        for (kind, names), got in zip(self.pending, carried):
            if kind in ("gather_ici", "gather_ici_cw", "gather_pass"):
                self.slots.update(zip(names, got))
                if kind == "gather_ici_cw":
                    self.conv_w = got[len(names)].transpose(1, 0, 2).reshape(3, 2 * D_FF)
            elif kind == "pair":
                self.from_sibling.update(zip(names, got))
            else:
                self.arrived.update(zip(names, got))

    def reduced(self):
        shards = []
        for n in BIG:
            full = None
            for piece, row0, _ in PIECES[n]:
                full = _chip_sum(self.arrived[piece], self.pair_sums[piece], self.place, f"chip_sum_{piece}",
                                 row0, SHARD_ROWS[n], into=full)
            shards.append(full)
        pieces = [[(row0, rows) for _, row0, rows in PIECES[n]] for n in BIG]
        return dict(zip(BIG, _pair_concat(shards, pieces, "pair_concat")))


def kernel(x, pre_mix_norm, w_in, rel_bias, hgrn_lb_raw, hgrn_norm, w_branch_attn, w_branch_hgrn, w_out, post_mix_norm, pre_ffn_norm, w_up, conv_w, conv_b, w_down, post_ffn_norm, loss_target, m_pre_mix_norm, m_w_in, m_rel_bias, m_hgrn_lb_raw, m_hgrn_norm, m_w_branch_attn, m_w_branch_hgrn, m_w_out, m_post_mix_norm, m_pre_ffn_norm, m_w_up, m_conv_w, m_conv_b, m_w_down, m_post_ffn_norm, v_pre_mix_norm, v_w_in, v_rel_bias, v_hgrn_lb_raw, v_hgrn_norm, v_w_branch_attn, v_w_branch_hgrn, v_w_out, v_post_mix_norm, v_pre_ffn_norm, v_w_up, v_conv_w, v_conv_b, v_w_down, v_post_ffn_norm):
    w = dict(pre_mix_norm=pre_mix_norm, w_in=w_in, rel_bias=rel_bias, hgrn_lb_raw=hgrn_lb_raw, hgrn_norm=hgrn_norm,
             w_branch_attn=w_branch_attn, w_branch_hgrn=w_branch_hgrn, w_out=w_out, post_mix_norm=post_mix_norm,
             pre_ffn_norm=pre_ffn_norm, w_up=w_up, conv_w=conv_w, conv_b=conv_b, w_down=w_down,
             post_ffn_norm=post_ffn_norm)
    m = dict(pre_mix_norm=m_pre_mix_norm, w_in=m_w_in, rel_bias=m_rel_bias, hgrn_lb_raw=m_hgrn_lb_raw,
             hgrn_norm=m_hgrn_norm, w_branch_attn=m_w_branch_attn, w_branch_hgrn=m_w_branch_hgrn, w_out=m_w_out,
             post_mix_norm=m_post_mix_norm, pre_ffn_norm=m_pre_ffn_norm, w_up=m_w_up, conv_w=m_conv_w,
             conv_b=m_conv_b, w_down=m_w_down, post_ffn_norm=m_post_ffn_norm)
    v = dict(pre_mix_norm=v_pre_mix_norm, w_in=v_w_in, rel_bias=v_rel_bias, hgrn_lb_raw=v_hgrn_lb_raw,
             hgrn_norm=v_hgrn_norm, w_branch_attn=v_w_branch_attn, w_branch_hgrn=v_w_branch_hgrn, w_out=v_w_out,
             post_mix_norm=v_post_mix_norm, pre_ffn_norm=v_pre_ffn_norm, w_up=v_w_up, conv_w=v_conv_w,
             conv_b=v_conv_b, w_down=v_w_down, post_ffn_norm=v_post_ffn_norm)
    shard2d = {n: (w[n][0] if w[n].ndim == 3 else w[n]) for n in WEIGHTS}
    chip = 2 * lax.axis_index("x") + lax.axis_index("y")
    core = lax.axis_index("c")

    place = jnp.stack([chip, core]).astype(jnp.int32)
    slots = {n: _cast_into_slot(shard2d[n], place, f"cast_{n}") for n in BIG}
    slots["w_in"] = _gather_weights([slots["w_in"]], [], "gather_w_in")[0]
    ex = _Exchange(place, slots, shard2d["conv_w"])
    loss, grad_x, small = _local_step(x[0], loss_target[0], {n: w[n] for n in SMALL if n != "conv_w"}, ex)

    flat = [small[n].reshape(-1) for n in SMALL] + [loss.reshape(-1)]
    sizes = [t.shape[0] for t in flat]
    summed = _all_sum(jnp.concatenate(flat).reshape(-1, LANES), "sum_small").reshape(-1)
    offs = [sum(sizes[:i]) for i in range(len(sizes))]
    grads = {}
    for n, o, sz in zip(SMALL, offs, sizes):
        grads[n] = summed[o:o + sz].reshape(small[n].shape)
    loss_total = summed[offs[-1]]
    cw = 2 * D_FF // N_CHIPS
    grads["conv_w"] = lax.dynamic_slice(grads["conv_w"], (0, chip * cw), (3, cw))

    grads.update(ex.reduced())

    out_g, out_d, out_m, out_v = [], [], [], []
    for n in WEIGHTS:
        d2, m2, v2 = _adamw(shard2d[n], grads[n], m[n].reshape(shard2d[n].shape), v[n].reshape(shard2d[n].shape),
                            f"adamw_{n}")
        shape = w[n].shape
        out_g.append(grads[n].reshape(shape))
        out_d.append(d2.reshape(shape))
        out_m.append(m2.reshape(shape))
        out_v.append(v2.reshape(shape))
    return (loss_total, grad_x[None], *out_g, *out_d, *out_m, *out_v)
```

```python
import functools
import math

import jax
import jax.numpy as jnp
from jax import lax
from jax.experimental import pallas as pl
from jax.experimental.pallas import tpu as pltpu

F32 = jnp.float32
BF16 = jnp.bfloat16
MESH = pl.DeviceIdType.MESH

D_MODEL = 1024
N_GROUPS = 3
DILATIONS = (1, 4, 16)
HEADS = 8
HEAD_DIM = 64
GROUP_W = HEADS * HEAD_DIM
QKV_W = N_GROUPS * 3 * GROUP_W
BLK = 128
NEG_INF = -1e30
NUM_BUCKETS = 32
MAX_EXACT = 16
MAX_DISTANCE = 2048
HG_HEADS = 4
HG_DK = 128
HG_W = HG_HEADS * HG_DK
HG_CHUNK = 32
HG_TILE = 256
IN_W = QKV_W + 4 * HG_W + 2 * D_MODEL
D_FF = 2816
EPS = 1e-6
N_CHIPS = 4
N_DEV = 8
LANES = 128

ADAM_LR, ADAM_B1, ADAM_B2, ADAM_EPS, ADAM_WD, ADAM_STEP = 0.001, 0.9, 0.999, 1e-08, 0.01, 10

VMEM_LIMIT = 56 * 1024 * 1024


def _cp(n_axes):
    return pltpu.CompilerParams(dimension_semantics=("arbitrary",) * n_axes, vmem_limit_bytes=VMEM_LIMIT)


def _sds(shape, dtype):
    return jax.ShapeDtypeStruct(tuple(shape), dtype)


def _sigmoid(v):
    return 1.0 / (1.0 + jnp.exp(-v))


def _bf(v):
    return v.astype(BF16)


def _dot(a, b, dims):
    return lax.dot_general(a, b, (dims, ((), ())), preferred_element_type=F32)


NN = ((1,), (0,))
NT = ((1,), (1,))
TN = ((0,), (0,))

ANY = pl.BlockSpec(memory_space=pl.ANY)


class _Plan:
    def __init__(self, copies, n_sems, ins=(), inouts=(), outs=()):
        self.copies, self.n_sems = copies, n_sems
        self.ins, self.inouts, self.outs = list(ins), list(inouts), list(outs)


def _call(body, plans=None, *, name, grid, in_specs, out_specs, out_shape, args, scratch_shapes=()):
    plans = list(plans or ())
    in_specs, out_specs, out_shape = list(in_specs), list(out_specs), list(out_shape)
    scratch_shapes = list(scratch_shapes)
    n_in, n_out, n_scr = len(in_specs), len(out_specs), len(scratch_shapes)
    x_in, x_out, aliases, spans = [], [], {}, []
    for p in plans:
        i0, o0 = len(x_in), len(x_out)
        x_in += p.ins
        for a in p.inouts:
            aliases[n_in + len(x_in)] = n_out + len(x_out)
            x_in.append(a)
            x_out.append(_sds(a.shape, a.dtype))
        x_out += p.outs
        spans.append((i0, len(p.ins), o0, len(p.inouts), len(p.outs)))
    sems = [pltpu.SemaphoreType.DMA((p.n_sems,)) for p in plans for _ in range(3)]

    def wrapped(*refs):
        xi = refs[n_in:n_in + len(x_in)]
        base = n_in + len(x_in)
        xo = refs[base + n_out:base + n_out + len(x_out)]
        sbase = base + n_out + len(x_out)
        xs = refs[sbase + n_scr:]
        ids = [pl.program_id(k) for k in range(len(grid))]
        first = functools.reduce(jnp.logical_and, [i == 0 for i in ids])
        last = functools.reduce(jnp.logical_and, [i == g - 1 for i, g in zip(ids, grid)])

        def descriptors(k):
            i0, ni, o0, nio, no = spans[k]
            return plans[k].copies(xi[i0:i0 + ni], xo[o0:o0 + nio], xo[o0 + nio:o0 + nio + no], *xs[3 * k:3 * k + 3])

        @pl.when(first)
        def _():
            for k in range(len(plans)):
                sends, _, local = descriptors(k)
                for cp in (*sends, *local):
                    cp.start()

        body(*refs[:n_in], *refs[base:base + n_out], *refs[sbase:sbase + n_scr])

        @pl.when(last)
        def _():
            for k in range(len(plans)):
                sends, recvs, local = descriptors(k)
                for cp in recvs:
                    cp.wait_recv()
                for cp in sends:
                    cp.wait_send()
                for cp in local:
                    cp.wait()

    res = pl.pallas_call(
        wrapped if plans else body, name=name, grid=grid, in_specs=in_specs + [ANY] * len(x_in),
        out_specs=out_specs + [ANY] * len(x_out), out_shape=out_shape + x_out, input_output_aliases=aliases,
        scratch_shapes=scratch_shapes + sems, compiler_params=_cp(len(grid)))(*args, *x_in)
    res = list(res)
    carried = [res[n_out + o0:n_out + o0 + nio + no] for (_, _, o0, nio, no) in spans]
    return res[:n_out], carried


def _mm_nn_blk(a, wg, name, tm=512, plans=None):
    M, K = a.shape
    nb, _, Nb = wg.shape

    def body(a_ref, w_ref, o_ref):
        o_ref[...] = _dot(_bf(a_ref[...]), w_ref[...], NN)

    (out,), carried = _call(
        body, plans, name=name, grid=(nb, M // tm),
        in_specs=[pl.BlockSpec((tm, K), lambda j, i: (i, 0)), pl.BlockSpec((None, K, Nb), lambda j, i: (j, 0, 0))],
        out_specs=[pl.BlockSpec((tm, Nb), lambda j, i: (i, j))],
        out_shape=[_sds((M, nb * Nb), F32)], args=(a, wg))
    return out if plans is None else (out, carried)


def _mm_nt_blk(dy, wg, name, tm=1024, plans=None):
    M = dy.shape[0]
    nb, K, Nb = wg.shape

    def body(dy_ref, w_ref, o_ref):
        j = pl.program_id(1)
        r = _dot(_bf(dy_ref[...]), w_ref[...], NT)

        @pl.when(j == 0)
        def _():
            o_ref[...] = r

        @pl.when(j > 0)
        def _():
            o_ref[...] += r

    (out,), carried = _call(
        body, plans, name=name, grid=(M // tm, nb),
        in_specs=[pl.BlockSpec((tm, Nb), lambda i, j: (i, j)), pl.BlockSpec((None, K, Nb), lambda i, j: (j, 0, 0))],
        out_specs=[pl.BlockSpec((tm, K), lambda i, j: (i, 0))],
        out_shape=[_sds((M, K), F32)], args=(dy, wg))
    return out if plans is None else (out, carried)


def _mm_tn_blk(x, dy, nb, name, tk=2048, x_cols=None, plans=None, together=False):
    T, Mx = x.shape
    xk, Mx = (0, Mx) if x_cols is None else x_cols
    Nb = dy.shape[1] // nb
    nj = nb if together else 1

    def body(x_ref, dy_ref, o_ref):
        t = pl.program_id(1)
        r = _dot(_bf(x_ref[...]), _bf(dy_ref[...]), TN)
        for j in range(nj):
            rj = r[:, j * Nb:(j + 1) * Nb]

            @pl.when(t == 0)
            def _():
                o_ref[j] = rj

            @pl.when(t > 0)
            def _():
                o_ref[j] += rj

    (out,), carried = _call(
        body, plans, name=name, grid=(nb // nj, T // tk),
        in_specs=[pl.BlockSpec((tk, Mx), lambda j, t: (t, xk)), pl.BlockSpec((tk, nj * Nb), lambda j, t: (t, j))],
        out_specs=[pl.BlockSpec((nj, Mx, Nb), lambda j, t: (j, 0, 0))],
        out_shape=[_sds((nb, Mx, Nb), F32)], args=(x, dy))
    return out if plans is None else (out, carried)


def _mm_nt(dy, w, name, tm=512):
    M, N = dy.shape
    K = w.shape[0]

    def body(dy_ref, w_ref, o_ref):
        o_ref[...] = _dot(_bf(dy_ref[...]), w_ref[...], NT)

    return pl.pallas_call(
        body, name=name, grid=(M // tm,),
        in_specs=[pl.BlockSpec((tm, N), lambda i: (i, 0)), pl.BlockSpec((K, N), lambda i: (0, 0))],
        out_specs=pl.BlockSpec((tm, K), lambda i: (i, 0)),
        out_shape=_sds((M, K), F32), compiler_params=_cp(1))(dy, w)


def _mm_tn(x, dy, name, tk=1024):
    T, Mx = x.shape
    N = dy.shape[1]

    def body(x_ref, dy_ref, o_ref):
        t = pl.program_id(0)
        r = _dot(_bf(x_ref[...]), _bf(dy_ref[...]), TN)

        @pl.when(t == 0)
        def _():
            o_ref[...] = r

        @pl.when(t > 0)
        def _():
            o_ref[...] += r

    return pl.pallas_call(
        body, name=name, grid=(T // tk,),
        in_specs=[pl.BlockSpec((tk, Mx), lambda t: (t, 0)), pl.BlockSpec((tk, N), lambda t: (t, 0))],
        out_specs=pl.BlockSpec((Mx, N), lambda t: (0, 0)),
        out_shape=_sds((Mx, N), F32), compiler_params=_cp(1))(x, dy)


def _tile(arr, bw, col=lambda c: 0):
    return ("tile", arr, bw, col)


def _full(arr):
    return ("full", arr)


def _out_tile(width, dtype, bw, col=lambda c: 0):
    return ("tile", width, dtype, bw, col)


def _out_acc(rows, width, bw, col=lambda c: 0):
    return ("acc", rows, width, bw, col)


def _rows_call(name, body, n_rows, tm, ncol, ins, outs, plans=None):
    in_specs, args = [], []
    for e in ins:
        if e[0] == "tile":
            _, arr, bw, col = e
            in_specs.append(pl.BlockSpec((tm, bw), functools.partial(lambda c, i, col: (i, col(c)), col=col)))
        else:
            arr = e[1]
            in_specs.append(pl.BlockSpec(arr.shape, functools.partial(lambda c, i, nd: (0,) * nd, nd=arr.ndim)))
        args.append(arr)
    out_specs, out_shape = [], []
    for e in outs:
        if e[0] == "tile":
            _, width, dtype, bw, col = e
            out_specs.append(pl.BlockSpec((tm, bw), functools.partial(lambda c, i, col: (i, col(c)), col=col)))
            out_shape.append(_sds((n_rows, width), dtype))
        else:
            _, rows, width, bw, col = e
            out_specs.append(pl.BlockSpec((rows, bw), functools.partial(lambda c, i, col: (0, col(c)), col=col)))
            out_shape.append(_sds((rows, width), F32))
    out, carried = _call(body, plans, name=name, grid=(ncol, n_rows // tm), in_specs=in_specs, out_specs=out_specs,
                         out_shape=out_shape, args=args)
    return out if plans is None else (out, carried)


def _acc(ref, val):
    i = pl.program_id(1)

    @pl.when(i == 0)
    def _():
        ref[...] = val

    @pl.when(i > 0)
    def _():
        ref[...] += val


def _rinv(z):
    return lax.rsqrt(jnp.mean(z * z, axis=-1, keepdims=True) + EPS)


def _norm_bwd(dy, zhat, r, w):
    dyw = dy * w
    return r * (dyw - zhat * jnp.mean(dyw * zhat, axis=-1, keepdims=True))


def _norm_fwd(x, w, name):
    def body(x_ref, w_ref, h_ref):
        xv = x_ref[...]
        h_ref[...] = _bf(xv * _rinv(xv) * w_ref[...])

    return _rows_call(name, body, x.shape[0], 512, 1, [_tile(x, D_MODEL), _full(w)],
                      [_out_tile(D_MODEL, BF16, D_MODEL)])[0]


def _prenorm_bwd(dh, xin, w, dres, name, plans=None):
    def body(dh_ref, x_ref, w_ref, dres_ref, dx_ref, dw_ref):
        xv = x_ref[...]
        r = _rinv(xv)
        xhat = xv * r
        dhv = dh_ref[...]
        dx_ref[...] = dres_ref[...] + _norm_bwd(dhv, xhat, r, w_ref[...])
        _acc(dw_ref, jnp.sum(dhv * xhat, axis=0, keepdims=True))

    return _rows_call(name, body, xin.shape[0], 512, 1,
                      [_tile(dh, D_MODEL), _tile(xin, D_MODEL), _full(w), _tile(dres, D_MODEL)],
                      [_out_tile(D_MODEL, F32, D_MODEL), _out_acc(1, D_MODEL, D_MODEL)], plans)


def _postnorm_bwd(dout, z, w, w_mat, name):
    def body(do_ref, z_ref, w_ref, wm_ref, dz_ref, dm_ref, dw_ref):
        zv = z_ref[...]
        r = _rinv(zv)
        zhat = zv * r
        dov = do_ref[...]
        dz = _bf(_norm_bwd(dov, zhat, r, w_ref[...]))
        dz_ref[...] = dz
        dm_ref[...] = _dot(dz, wm_ref[...], NT)
        _acc(dw_ref, jnp.sum(dov * zhat, axis=0, keepdims=True))

    return _rows_call(name, body, z.shape[0], 512, 1,
                      [_tile(dout, D_MODEL), _tile(z, D_MODEL), _full(w), _full(w_mat)],
                      [_out_tile(D_MODEL, BF16, D_MODEL), _out_tile(D_MODEL, F32, D_MODEL),
                       _out_acc(1, D_MODEL, D_MODEL)])


def _t5_bucket(dist):
    n = jnp.maximum(dist, 0)
    nf = jnp.maximum(n, 1).astype(F32)
    large = MAX_EXACT + (jnp.log(nf / MAX_EXACT) / math.log(MAX_DISTANCE / MAX_EXACT)
                         * (NUM_BUCKETS - MAX_EXACT)).astype(jnp.int32)
    large = jnp.minimum(large, NUM_BUCKETS - 1)
    return jnp.where(n < MAX_EXACT, n, large)


def _band_rel():
    return jnp.arange(BLK)[:, None] + BLK - jnp.arange(2 * BLK)[None, :]


def _band_mask(n):
    row = lax.broadcasted_iota(jnp.int32, (BLK, 2 * BLK), 0)
    col = lax.broadcasted_iota(jnp.int32, (BLK, 2 * BLK), 1)
    rel = row + BLK - col
    return (rel >= 0) & (rel <= BLK) & ((col >= BLK) | (n > 0))


RES_UNROLL = 4


def _heads_per_step(d):
    return HEADS if d == 1 else LANES // HEAD_DIM


def _sub_rows(r, d):
    return pl.ds(r, BLK, stride=d) if d > 1 else pl.ds(0, BLK)


def _for_residues(d, fn):
    if d <= RES_UNROLL:
        for r in range(d):
            fn(r)
    else:
        def group(i, carry):
            for k in range(RES_UNROLL):
                fn(i * RES_UNROLL + k)
            return carry

        lax.fori_loop(0, d // RES_UNROLL, group, 0)


def _attn_specs(d, g, qblock):
    cw = _heads_per_step(d) * HEAD_DIM

    def col(part, hp):
        return (g * 3 + part) * (GROUP_W // cw) + hp

    def cur(part):
        return pl.BlockSpec((d * BLK, cw), lambda hp, n: (qblock(n), col(part, hp)))

    def prev(part):
        return pl.BlockSpec((d * BLK, cw), lambda hp, n: (jnp.maximum(qblock(n) - 1, 0), col(part, hp)))

    return cur, prev


def _attn_fwd(proj, bias, g, name, plans=None):
    S = proj.shape[0]
    d = DILATIONS[g]
    NB = S // (d * BLK)
    hps = _heads_per_step(d)

    def body(q_ref, kp_ref, kc_ref, vp_ref, vc_ref, b_ref, o_ref, lse_ref):
        hp = pl.program_id(0)
        mask = _band_mask(pl.program_id(1))

        def residue(r):
            rows = _sub_rows(r, d)
            q2 = q_ref[rows, :]
            k2 = jnp.concatenate([kp_ref[rows, :], kc_ref[rows, :]], axis=0)
            v2 = jnp.concatenate([vp_ref[rows, :], vc_ref[rows, :]], axis=0)
            outs, lses = [], []
            for hh in range(hps):
                hs = slice(hh * HEAD_DIM, (hh + 1) * HEAD_DIM)
                s = _dot(_bf(q2[:, hs]), _bf(k2[:, hs]), NT) * (HEAD_DIM ** -0.5) + b_ref[hp * hps + hh]
                s = jnp.where(mask, s, NEG_INF)
                m = jnp.max(s, axis=-1, keepdims=True)
                p = jnp.exp(s - m)
                l = jnp.sum(p, axis=-1, keepdims=True)
                outs.append(_dot(_bf(p), _bf(v2[:, hs]), NN) / l)
                lses.append(jnp.broadcast_to(m + jnp.log(l), (BLK, HEAD_DIM)))
            o_ref[rows, :] = jnp.concatenate(outs, axis=1)
            lse_ref[rows, :] = jnp.concatenate(lses, axis=1)

        _for_residues(d, residue)

    cur, prev = _attn_specs(d, g, lambda n: n)
    out = pl.BlockSpec((d * BLK, hps * HEAD_DIM), lambda hp, n: (n, hp))
    res, carried = _call(
        body, plans, name=name, grid=(HEADS // hps, NB),
        in_specs=[cur(0), prev(1), cur(1), prev(2), cur(2),
                  pl.BlockSpec((HEADS, BLK, 2 * BLK), lambda hp, n: (0, 0, 0))],
        out_specs=[out, out], out_shape=[_sds((S, GROUP_W), F32)] * 2,
        args=(proj, proj, proj, proj, proj, bias))
    return res if plans is None else (res, carried)


def _attn_merge(os_, lses, name):
    def body(o0, o1, o2, l0, l1, l2, y_ref, lse_ref):
        a, b, c = l0[...], l1[...], l2[...]
        m = jnp.maximum(jnp.maximum(a, b), c)
        ea, eb, ec = jnp.exp(a - m), jnp.exp(b - m), jnp.exp(c - m)
        den = ea + eb + ec
        y_ref[...] = (ea * o0[...] + eb * o1[...] + ec * o2[...]) / den
        lse_ref[...] = m + jnp.log(den)

    S = os_[0].shape[0]
    return _rows_call(name, body, S, 512, 1, [_tile(t, GROUP_W) for t in (*os_, *lses)],
                      [_out_tile(GROUP_W, F32, GROUP_W)] * 2)


def _attn_bwd(proj, bias, lse, y, dy, g, name, plans=None):
    S = proj.shape[0]
    d = DILATIONS[g]
    NB = S // (d * BLK)
    hps = _heads_per_step(d)

    def body(q_ref, kp_ref, kc_ref, vp_ref, vc_ref, b_ref, l_ref, y_ref, dy_ref,
             dq_ref, dk_ref, dv_ref, db_ref, ck_ref, cv_ref):
        hp, n = pl.program_id(0), pl.program_id(1)

        @pl.when((hp == 0) & (n == 0))
        def _():
            db_ref[...] = jnp.zeros_like(db_ref)

        @pl.when(n == 0)
        def _():
            ck_ref[...] = jnp.zeros_like(ck_ref)
            cv_ref[...] = jnp.zeros_like(cv_ref)

        @pl.when(n < NB)
        def _():
            mask = _band_mask(n)

            def residue(r):
                rows = _sub_rows(r, d)
                q2 = q_ref[rows, :]
                k2 = jnp.concatenate([kp_ref[rows, :], kc_ref[rows, :]], axis=0)
                v2 = jnp.concatenate([vp_ref[rows, :], vc_ref[rows, :]], axis=0)
                l2, y2, dy2 = l_ref[rows, :], y_ref[rows, :], dy_ref[rows, :]
                dqs, dks, dvs = [], [], []
                for hh in range(hps):
                    hs = slice(hh * HEAD_DIM, (hh + 1) * HEAD_DIM)
                    q, kb, vb = _bf(q2[:, hs]), _bf(k2[:, hs]), _bf(v2[:, hs])
                    s = _dot(q, kb, NT) * (HEAD_DIM ** -0.5) + b_ref[hp * hps + hh]
                    s = jnp.where(mask, s, NEG_INF)
                    p = jnp.exp(s - l2[:, hh * HEAD_DIM:hh * HEAD_DIM + 1])
                    dyh = dy2[:, hs]
                    delta = jnp.sum(dyh * y2[:, hs], axis=-1, keepdims=True)
                    dyb = _bf(dyh)
                    ds = p * (_dot(dyb, vb, NT) - delta)
                    db_ref[hp * hps + hh] += ds
                    dsb = _bf(ds * (HEAD_DIM ** -0.5))
                    dqs.append(_dot(dsb, kb, NN))
                    dks.append(_dot(dsb, q, TN))
                    dvs.append(_dot(_bf(p), dyb, TN))
                dkb = jnp.concatenate(dks, axis=1)
                dvb = jnp.concatenate(dvs, axis=1)
                dq_ref[rows, :] = jnp.concatenate(dqs, axis=1)
                dk_ref[rows, :] = ck_ref[rows, :] + dkb[:BLK]
                dv_ref[rows, :] = cv_ref[rows, :] + dvb[:BLK]
                ck_ref[rows, :] = dkb[BLK:]
                cv_ref[rows, :] = dvb[BLK:]

            _for_residues(d, residue)

        @pl.when(n == NB)
        def _():
            dk_ref[...] = ck_ref[...]
            dv_ref[...] = cv_ref[...]

    def qn(n):
        return jnp.minimum(n, NB - 1)

    cur, prev = _attn_specs(d, g, qn)
    cw = hps * HEAD_DIM
    row = pl.BlockSpec((d * BLK, cw), lambda hp, n: (qn(n), hp))
    done = pl.BlockSpec((d * BLK, cw), lambda hp, n: (jnp.maximum(n - 1, 0), hp))
    (dq, dk, dv, db), carried = _call(
        body, plans, name=name, grid=(HEADS // hps, NB + 1),
        in_specs=[cur(0), prev(1), cur(1), prev(2), cur(2),
                  pl.BlockSpec((HEADS, BLK, 2 * BLK), lambda hp, n: (0, 0, 0)), row, row, row],
        out_specs=[row, done, done, pl.BlockSpec((HEADS, BLK, 2 * BLK), lambda hp, n: (0, 0, 0))],
        out_shape=[_sds((S, GROUP_W), F32)] * 3 + [_sds((HEADS, BLK, 2 * BLK), F32)],
        scratch_shapes=[pltpu.VMEM((d * BLK, cw), F32)] * 2,
        args=(proj, proj, proj, proj, proj, bias, lse, y, dy))
    return ([dq, dk, dv], db) if plans is None else ([dq, dk, dv], db, carried)


BAND = BLK * 2 * BLK


def _bucket_onehot():
    buckets = jnp.stack([_t5_bucket(_band_rel() * d) for d in DILATIONS]).reshape(N_GROUPS, 1, BAND)
    return (buckets == jnp.arange(NUM_BUCKETS).reshape(1, NUM_BUCKETS, 1)).astype(F32)


def _relbias_fwd(rel_bias, name):
    table = rel_bias.reshape(NUM_BUCKETS, N_GROUPS, HEADS).transpose(1, 0, 2)

    def body(t_ref, oh_ref, o_ref):
        o_ref[...] = lax.dot_general(t_ref[...], oh_ref[...], (TN, ((), ())), preferred_element_type=F32,
                                     precision=lax.Precision.HIGHEST)

    out = pl.pallas_call(
        body, name=name, grid=(N_GROUPS,),
        in_specs=[pl.BlockSpec((None, NUM_BUCKETS, HEADS), lambda g: (g, 0, 0)),
                  pl.BlockSpec((None, NUM_BUCKETS, BAND), lambda g: (g, 0, 0))],
        out_specs=pl.BlockSpec((None, HEADS, BAND), lambda g: (g, 0, 0)),
        out_shape=_sds((N_GROUPS, HEADS, BAND), F32), compiler_params=_cp(1))(table, _bucket_onehot())
    return out.reshape(N_GROUPS, HEADS, BLK, 2 * BLK)


def _relbias_bwd(dbs, name):
    band = BAND
    onehot = _bucket_onehot()
    dbf = jnp.stack([db.reshape(HEADS, band) for db in dbs])

    def body(oh_ref, db_ref, o_ref):
        o_ref[...] = lax.dot_general(oh_ref[...], db_ref[...], (NT, ((), ())), preferred_element_type=F32,
                                     precision=lax.Precision.HIGHEST)

    out = pl.pallas_call(
        body, name=name, grid=(N_GROUPS,),
        in_specs=[pl.BlockSpec((None, NUM_BUCKETS, band), lambda g: (g, 0, 0)),
                  pl.BlockSpec((None, HEADS, band), lambda g: (g, 0, 0))],
        out_specs=pl.BlockSpec((None, NUM_BUCKETS, HEADS), lambda g: (g, 0, 0)),
        out_shape=_sds((N_GROUPS, NUM_BUCKETS, HEADS), F32), compiler_params=_cp(1))(onehot, dbf)
    return out.transpose(1, 0, 2).reshape(NUM_BUCKETS, N_GROUPS * HEADS)


def _chunk_pos(shape):
    return lax.broadcasted_iota(jnp.int32, shape, 0) % HG_CHUNK


def _chunk_cumsum(v):
    pos = _chunk_pos(v.shape)
    s = 1
    while s < HG_CHUNK:
        v = v + jnp.where(pos >= s, pltpu.roll(v, s, 0), 0.0)
        s *= 2
    return v


def _chunk_rev_cumsum(v):
    pos = _chunk_pos(v.shape)
    n = v.shape[0]
    s = 1
    while s < HG_CHUNK:
        v = v + jnp.where(pos < HG_CHUNK - s, pltpu.roll(v, n - s, 0), 0.0)
        s *= 2
    return v


def _lower_bound(raw):
    a0, a1 = raw[0:1], raw[1:2]
    m = jnp.maximum(a0, a1)
    e0, e1 = jnp.exp(a0 - m), jnp.exp(a1 - m)
    return e0 / (e0 + e1)


def _hg_gates(qr, fr, lb):
    sf = _sigmoid(fr)
    f = lb + (1.0 - lb) * sf
    sq = _sigmoid(qr)
    return qr * sq, sq, f, sf


HG_COL0 = QKV_W // HG_W


def _hgrn_fwd(proj, lb_raw, nw, name):
    S = proj.shape[0]
    ncs = HG_TILE // HG_CHUNK
    tril = jnp.tril(jnp.ones((HG_CHUNK, HG_CHUNK), dtype=bool))

    def body(q_ref, f_ref, i_ref, og_ref, lb_ref, nw_ref, y_ref, o_ref, st_ref, state):
        @pl.when(pl.program_id(0) == 0)
        def _():
            state[...] = jnp.zeros_like(state)

        lb = _lower_bound(lb_ref[...])
        q, _, f, _ = _hg_gates(q_ref[...], f_ref[...], lb)
        k = 1.0 - f
        G = _chunk_cumsum(jnp.log(f))
        row = lax.broadcasted_iota(jnp.int32, (HG_CHUNK, HG_CHUNK), 0)
        col = lax.broadcasted_iota(jnp.int32, (HG_CHUNK, HG_CHUNK), 1)
        heads = [slice(h * HG_DK, (h + 1) * HG_DK) for h in range(HG_HEADS)]
        sts = [state[h] for h in range(HG_HEADS)]
        for c in range(ncs):
            cs = slice(c * HG_CHUNK, (c + 1) * HG_CHUNK)
            for h, hs in enumerate(heads):
                Gc = G[cs, hs]
                gl = Gc[HG_CHUNK - 1:HG_CHUNK]
                qt = _bf(q[cs, hs] * jnp.exp(Gc))
                kt = _bf(k[cs, hs] * jnp.exp(-Gc))
                kd = _bf(k[cs, hs] * jnp.exp(gl - Gc))
                v = _bf(i_ref[cs, hs])
                A = jnp.where(row >= col, _dot(qt, kt, NT), 0.0)
                o_ref[cs, hs] = _dot(_bf(A), v, NN) + _dot(qt, _bf(sts[h]), NT)
                st_ref[c, h] = sts[h]
                sts[h] = sts[h] * jnp.exp(gl) + _dot(v, kd, TN)
        for h, hs in enumerate(heads):
            state[h] = sts[h]
            oh = o_ref[:, hs]
            og = og_ref[:, hs]
            y_ref[:, hs] = oh * _rinv(oh) * nw_ref[...] * (og * _sigmoid(og))

    def colspec(j):
        return pl.BlockSpec((HG_TILE, HG_W), lambda i: (i, HG_COL0 + j))

    return pl.pallas_call(
        body, name=name, grid=(S // HG_TILE,),
        in_specs=[colspec(0), colspec(1), colspec(2), colspec(3),
                  pl.BlockSpec((2, HG_W), lambda i: (0, 0)), pl.BlockSpec((1, HG_DK), lambda i: (0, 0))],
        out_specs=[pl.BlockSpec((HG_TILE, HG_W), lambda i: (i, 0))] * 2
        + [pl.BlockSpec((ncs, HG_HEADS, HG_DK, HG_DK), lambda i: (i, 0, 0, 0))],
        out_shape=[_sds((S, HG_W), F32)] * 2 + [_sds((S // HG_CHUNK, HG_HEADS, HG_DK, HG_DK), F32)],
        scratch_shapes=[pltpu.VMEM((HG_HEADS, HG_DK, HG_DK), F32)],
        compiler_params=_cp(1))(proj, proj, proj, proj, lb_raw, nw)


def _hgrn_bwd(proj, lb_raw, nw, o, states, dy, name):
    S = proj.shape[0]
    ncs = HG_TILE // HG_CHUNK
    nt = S // HG_TILE

    def body(q_ref, f_ref, i_ref, og_ref, lb_ref, nw_ref, o_ref, st_ref, dy_ref,
             dq_ref, df_ref, di_ref, dog_ref, dlb_ref, dnw_ref, dstate, do_s, dG_s, dgl_s, dk_s, dlb_s):
        step = pl.program_id(0)

        @pl.when(step == 0)
        def _():
            dstate[...] = jnp.zeros_like(dstate)
            dlb_s[...] = jnp.zeros_like(dlb_s)
            dnw_ref[...] = jnp.zeros_like(dnw_ref)

        lb = _lower_bound(lb_ref[...])
        qr = q_ref[...]
        q, sq, f, sf = _hg_gates(qr, f_ref[...], lb)
        k = 1.0 - f
        G = _chunk_cumsum(jnp.log(f))
        nwv = nw_ref[...]
        row = lax.broadcasted_iota(jnp.int32, (HG_CHUNK, HG_CHUNK), 0)
        col = lax.broadcasted_iota(jnp.int32, (HG_CHUNK, HG_CHUNK), 1)
        for h in range(HG_HEADS):
            hs = slice(h * HG_DK, (h + 1) * HG_DK)
            oh = o_ref[:, hs]
            r = _rinv(oh)
            ohat = oh * r
            og = og_ref[:, hs]
            sg = _sigmoid(og)
            dyh = dy_ref[:, hs]
            don = dyh * (og * sg)
            dog_ref[:, hs] = _bf(dyh * (ohat * nwv) * (sg * (1.0 + og * (1.0 - sg))))
            dnw_ref[...] += jnp.sum(don * ohat, axis=0, keepdims=True)
            do_s[:, hs] = _norm_bwd(don, ohat, r, nwv)
        dsts = [dstate[h] for h in range(HG_HEADS)]
        for c in reversed(range(ncs)):
            cs = slice(c * HG_CHUNK, (c + 1) * HG_CHUNK)
            for h in range(HG_HEADS):
                hs = slice(h * HG_DK, (h + 1) * HG_DK)
                dst = dsts[h]
                Gc = G[cs, hs]
                gl = Gc[HG_CHUNK - 1:HG_CHUNK]
                eG, enG, edG, egl = jnp.exp(Gc), jnp.exp(-Gc), jnp.exp(gl - Gc), jnp.exp(gl)
                qt, kt, kd = q[cs, hs] * eG, k[cs, hs] * enG, k[cs, hs] * edG
                qtb, ktb, kdb = _bf(qt), _bf(kt), _bf(kd)
                v = _bf(i_ref[cs, hs])
                do = _bf(do_s[cs, hs])
                st = st_ref[c, h]
                dstb = _bf(dst)
                A = jnp.where(row >= col, _dot(qtb, ktb, NT), 0.0)
                dA = _bf(jnp.where(row >= col, _dot(do, v, NT), 0.0))
                di_ref[cs, hs] = _bf(_dot(_bf(A), do, TN) + _dot(kdb, dstb, NT))
                dqt = _dot(dA, ktb, NN) + _dot(do, _bf(st), NN)
                dkt = _dot(dA, qtb, TN)
                dkd = _dot(v, dstb, NN)
                dgl = egl * jnp.sum(st * dst, axis=0, keepdims=True) + jnp.sum(dkd * kd, axis=0, keepdims=True)
                dsts[h] = dst * egl + _dot(do, qtb, TN)
                dq_ref[cs, hs] = _bf(dqt * eG * (sq[cs, hs] * (1.0 + qr[cs, hs] * (1.0 - sq[cs, hs]))))
                dk_s[cs, hs] = dkt * enG + dkd * edG
                dG_s[cs, hs] = dqt * qt - dkt * kt - dkd * kd
                dgl_s[cs, hs] = jnp.broadcast_to(dgl, (HG_CHUNK, HG_DK))
        for h in range(HG_HEADS):
            dstate[h] = dsts[h]
        dg = _chunk_rev_cumsum(dG_s[...]) + dgl_s[...]
        dfv = dg / f - dk_s[...]
        df_ref[...] = _bf(dfv * (1.0 - lb) * sf * (1.0 - sf))
        dlb_s[...] += jnp.sum(dfv * (1.0 - sf), axis=0, keepdims=True)

        @pl.when(step == nt - 1)
        def _():
            t = dlb_s[...] * lb * (1.0 - lb)
            dlb_ref[...] = jnp.concatenate([t, -t], axis=0)

    def colspec(j):
        return pl.BlockSpec((HG_TILE, HG_W), lambda i: (nt - 1 - i, HG_COL0 + j))

    tile = pl.BlockSpec((HG_TILE, HG_W), lambda i: (nt - 1 - i, 0))
    outs = pl.pallas_call(
        body, name=name, grid=(nt,),
        in_specs=[colspec(0), colspec(1), colspec(2), colspec(3),
                  pl.BlockSpec((2, HG_W), lambda i: (0, 0)), pl.BlockSpec((1, HG_DK), lambda i: (0, 0)),
                  tile, pl.BlockSpec((ncs, HG_HEADS, HG_DK, HG_DK), lambda i: (nt - 1 - i, 0, 0, 0)), tile],
        out_specs=[tile] * 4 + [pl.BlockSpec((2, HG_W), lambda i: (0, 0)), pl.BlockSpec((1, HG_DK), lambda i: (0, 0))],
        out_shape=[_sds((S, HG_W), BF16)] * 4 + [_sds((2, HG_W), F32), _sds((1, HG_DK), F32)],
        scratch_shapes=[pltpu.VMEM((HG_HEADS, HG_DK, HG_DK), F32)] + [pltpu.VMEM((HG_TILE, HG_W), F32)] * 4
        + [pltpu.VMEM((1, HG_W), F32)],
        compiler_params=_cp(1))(proj, proj, proj, proj, lb_raw, nw, o, states, dy)
    return outs[:4], outs[4], outs[5]


GATE_COL0 = (QKV_W + 4 * HG_W) // GROUP_W
HALF_D = D_MODEL // 2


def _gate_tiles(proj):
    return [_tile(proj, HALF_D, functools.partial(lambda c, k: GATE_COL0 + k, k=k)) for k in range(4)]


def _gates(g_refs):
    s0 = _sigmoid(jnp.concatenate([g_refs[0][...], g_refs[1][...]], axis=1))
    s1 = _sigmoid(jnp.concatenate([g_refs[2][...], g_refs[3][...]], axis=1))
    return s0, s1


def _branch_fwd(y, yh, proj, w_a, w_h, name):
    nb = w_a.shape[0]

    def body(y_ref, yh_ref, g0a, g0b, g1a, g1b, wa_ref, wh_ref, za_ref, zh_ref, m_ref):
        yb, yhb = _bf(y_ref[...]), _bf(yh_ref[...])
        za = jnp.concatenate([_dot(yb, wa_ref[j], NN) for j in range(nb)], axis=1)
        zh = jnp.concatenate([_dot(yhb, wh_ref[j], NN) for j in range(nb)], axis=1)
        s0, s1 = _gates((g0a, g0b, g1a, g1b))
        za_ref[...] = za
        zh_ref[...] = zh
        m_ref[...] = _bf(s0 * za + s1 * zh)

    return _rows_call(name, body, y.shape[0], 512, 1,
                      [_tile(y, GROUP_W), _tile(yh, HG_W), *_gate_tiles(proj), _full(w_a), _full(w_h)],
                      [_out_tile(D_MODEL, F32, D_MODEL)] * 2 + [_out_tile(D_MODEL, BF16, D_MODEL)])


def _branch_bwd(dm, za, zh, proj, w_a, w_h, name):
    nb, _, Nb = w_a.shape

    def body(dm_ref, za_ref, zh_ref, g0a, g0b, g1a, g1b, wa_ref, wh_ref,
             dza_ref, dzh_ref, dg0_ref, dg1_ref, dy_ref, dyh_ref):
        dmv = dm_ref[...]
        s0, s1 = _gates((g0a, g0b, g1a, g1b))
        dza, dzh = _bf(dmv * s0), _bf(dmv * s1)
        dza_ref[...] = dza
        dzh_ref[...] = dzh
        dg0_ref[...] = _bf(dmv * za_ref[...] * s0 * (1.0 - s0))
        dg1_ref[...] = _bf(dmv * zh_ref[...] * s1 * (1.0 - s1))
        dy_ref[...] = sum(_dot(dza[:, j * Nb:(j + 1) * Nb], wa_ref[j], NT) for j in range(nb))
        dyh_ref[...] = sum(_dot(dzh[:, j * Nb:(j + 1) * Nb], wh_ref[j], NT) for j in range(nb))

    return _rows_call(name, body, za.shape[0], 512, 1,
                      [_tile(dm, D_MODEL), _tile(za, D_MODEL), _tile(zh, D_MODEL), *_gate_tiles(proj),
                       _full(w_a), _full(w_h)],
                      [_out_tile(D_MODEL, BF16, D_MODEL)] * 4 + [_out_tile(GROUP_W, F32, GROUP_W),
                                                                 _out_tile(HG_W, F32, HG_W)])


def _mix_out(merged, w_out, x, w_post, w_pre, name):
    def body(m_ref, wo_ref, x_ref, wp_ref, wf_ref, mo_ref, x1_ref, h2_ref):
        z = _dot(m_ref[...], wo_ref[...], NN)
        mo_ref[...] = z
        x1 = x_ref[...] + z * _rinv(z) * wp_ref[...]
        x1_ref[...] = x1
        h2_ref[...] = _bf(x1 * _rinv(x1) * wf_ref[...])

    return _rows_call(name, body, x.shape[0], 512, 1,
                      [_tile(merged, D_MODEL), _full(w_out), _tile(x, D_MODEL), _full(w_post), _full(w_pre)],
                      [_out_tile(D_MODEL, F32, D_MODEL), _out_tile(D_MODEL, F32, D_MODEL),
                       _out_tile(D_MODEL, BF16, D_MODEL)])


def _loss_head(a, w_down, x1, tgt, w, name):
    def body(a_ref, wd_ref, x1_ref, t_ref, w_ref, dx_ref, df_ref, dw_ref, loss_ref):
        z = _dot(a_ref[...], wd_ref[...], NN)
        r = _rinv(z)
        zhat = z * r
        wv = w_ref[...]
        e = x1_ref[...] + zhat * wv - t_ref[...]
        dx = e * (1.0 / D_MODEL)
        dx_ref[...] = dx
        df_ref[...] = _bf(_norm_bwd(dx, zhat, r, wv))
        _acc(dw_ref, jnp.sum(dx * zhat, axis=0, keepdims=True))
        part = 0.5 * jnp.sum(jnp.sum(e * e, axis=1, keepdims=True), axis=0, keepdims=True) * (1.0 / D_MODEL)
        _acc(loss_ref, jnp.broadcast_to(part, (1, LANES)))

    return _rows_call(name, body, x1.shape[0], 512, 1,
                      [_tile(a, D_FF), _full(w_down), _tile(x1, D_MODEL), _tile(tgt, D_MODEL), _full(w)],
                      [_out_tile(D_MODEL, F32, D_MODEL), _out_tile(D_MODEL, BF16, D_MODEL),
                       _out_acc(1, D_MODEL, D_MODEL), _out_acc(1, LANES, LANES)])


CONV_CB = D_FF // 2
CONV_TM = 512
HALO = 8
SQRT_HALF = 0.7071067811865476
INV_SQRT_2PI = 0.3989422804014327


def _conv_taps(u_ref, halo_ref, first):
    u = u_ref[...]
    row = lax.broadcasted_iota(jnp.int32, u.shape, 0)
    p1 = jnp.where(first, 0.0, halo_ref[HALO - 1:HALO, :])
    p2 = jnp.where(first, 0.0, halo_ref[HALO - 2:HALO - 1, :])
    u1 = jnp.where(row == 0, p1, pltpu.roll(u, 1, 0))
    u2 = jnp.where(row == 0, p2, jnp.where(row == 1, p1, pltpu.roll(u, 2, 0)))
    return u2, u1, u


def _conv(taps, w_ref, b_ref):
    return b_ref[...] + w_ref[0:1, :] * taps[0] + w_ref[1:2, :] * taps[1] + w_ref[2:3, :] * taps[2]


def _conv_specs(tm):
    nh = tm // HALO
    nc = D_FF // CONV_CB

    def tile(off):
        return pl.BlockSpec((tm, CONV_CB), lambda c, i: (i, off + c))

    def halo(off):
        return pl.BlockSpec((HALO, CONV_CB), lambda c, i: (jnp.maximum(i * nh - 1, 0), off + c))

    def small(rows, off):
        return pl.BlockSpec((rows, CONV_CB), lambda c, i: (0, off + c))

    return nc, tile, halo, small


def _conv_gelu_fwd(u, cw, cb, name):
    S = u.shape[0]
    tm = CONV_TM
    nc, tile, halo, small = _conv_specs(tm)

    def body(ug, hg, uv, hv, wg, wv, bg, bv, a_ref):
        first = pl.program_id(1) == 0
        cg = _conv(_conv_taps(ug, hg, first), wg, bg)
        cv = _conv(_conv_taps(uv, hv, first), wv, bv)
        a_ref[...] = _bf(0.5 * cg * (1.0 + lax.erf(cg * SQRT_HALF)) * cv)

    return pl.pallas_call(
        body, name=name, grid=(nc, S // tm),
        in_specs=[tile(0), halo(0), tile(nc), halo(nc), small(3, 0), small(3, nc), small(1, 0), small(1, nc)],
        out_specs=tile(0), out_shape=_sds((S, D_FF), BF16), compiler_params=_cp(2))(u, u, u, u, cw, cw, cb, cb)


def _conv_gelu_bwd(u, da, cw, cb, name, plans=None):
    S = u.shape[0]
    tm = CONV_TM
    nc, tile, halo, small = _conv_specs(tm)

    def body(ug, hg, uv, hv, wg, wv, bg, bv, da_ref, dcg_ref, dcv_ref, dwg_ref, dwv_ref, dbg_ref, dbv_ref):
        first = pl.program_id(1) == 0
        tg = _conv_taps(ug, hg, first)
        tv = _conv_taps(uv, hv, first)
        cg = _conv(tg, wg, bg)
        cv = _conv(tv, wv, bv)
        phi = 0.5 * (1.0 + lax.erf(cg * SQRT_HALF))
        dav = da_ref[...]
        dcg = dav * cv * (phi + cg * jnp.exp(-0.5 * cg * cg) * INV_SQRT_2PI)
        dcv = dav * (cg * phi)
        dcg_ref[...] = dcg
        dcv_ref[...] = dcv
        for dc, taps, dw_ref, db_ref in ((dcg, tg, dwg_ref, dbg_ref), (dcv, tv, dwv_ref, dbv_ref)):
            _acc(db_ref, jnp.sum(dc, axis=0, keepdims=True))
            for j in range(3):
                _acc(dw_ref.at[j:j + 1, :], jnp.sum(dc * taps[j], axis=0, keepdims=True))

    res, carried = _call(
        body, plans, name=name, grid=(nc, S // tm),
        in_specs=[tile(0), halo(0), tile(nc), halo(nc), small(3, 0), small(3, nc), small(1, 0), small(1, nc), tile(0)],
        out_specs=[tile(0), tile(0), small(3, 0), small(3, 0), small(1, 0), small(1, 0)],
        out_shape=[_sds((S, D_FF), F32)] * 2 + [_sds((3, D_FF), F32)] * 2 + [_sds((1, D_FF), F32)] * 2,
        args=(u, u, u, u, cw, cw, cb, cb, da))
    return res if plans is None else (res, carried)


def _conv_input_bwd(dcg, dcv, cw, name, plans=None):
    S = dcg.shape[0]
    tm = CONV_TM
    nc, tile, _, small = _conv_specs(tm)
    nh = tm // HALO
    nt = S // tm

    def nxt(off):
        return pl.BlockSpec((HALO, CONV_CB), lambda c, i: (jnp.minimum((i + 1) * nh, S // HALO - 1), off + c))

    def body(g_ref, ng_ref, v_ref, nv_ref, wg, wv, dug_ref, duv_ref):
        last = pl.program_id(1) == nt - 1
        for dc_ref, n_ref, w_ref, du_ref in ((g_ref, ng_ref, wg, dug_ref), (v_ref, nv_ref, wv, duv_ref)):
            dc = dc_ref[...]
            row = lax.broadcasted_iota(jnp.int32, dc.shape, 0)
            n1 = jnp.where(last, 0.0, n_ref[0:1, :])
            n2 = jnp.where(last, 0.0, n_ref[1:2, :])
            d1 = jnp.where(row == tm - 1, n1, pltpu.roll(dc, tm - 1, 0))
            d2 = jnp.where(row == tm - 1, n2, jnp.where(row == tm - 2, n1, pltpu.roll(dc, tm - 2, 0)))
            du_ref[...] = _bf(w_ref[2:3, :] * dc + w_ref[1:2, :] * d1 + w_ref[0:1, :] * d2)

    res, carried = _call(
        body, plans, name=name, grid=(nc, nt),
        in_specs=[tile(0), nxt(0), tile(0), nxt(0), small(3, 0), small(3, nc)],
        out_specs=[tile(0), tile(0)], out_shape=[_sds((S, D_FF), BF16)] * 2,
        args=(dcg, dcg, dcv, dcv, cw, cw))
    return res if plans is None else (res, carried)


def _row_tile(n, cap):
    best = n
    for t in range(16, cap + 1, 16):
        if n % t == 0:
            best = t
    return best if best <= cap else n


def _rows_for_bytes(nbytes, cols):
    return max(16, nbytes // (4 * cols) // 16 * 16)


def _adamw(w, g, m, v, name):
    R, C = w.shape
    tr = _row_tile(R, _rows_for_bytes(2 << 20, C))

    def body(w_ref, g_ref, m_ref, v_ref, d_ref, nm_ref, nv_ref):
        gv = g_ref[...]
        nm = ADAM_B1 * m_ref[...] + (1.0 - ADAM_B1) * gv
        nv = ADAM_B2 * v_ref[...] + (1.0 - ADAM_B2) * (gv * gv)
        m_hat = nm / (1.0 - ADAM_B1 ** ADAM_STEP)
        v_hat = nv / (1.0 - ADAM_B2 ** ADAM_STEP)
        d_ref[...] = -ADAM_LR * (m_hat / (jnp.sqrt(v_hat) + ADAM_EPS) + ADAM_WD * w_ref[...])
        nm_ref[...] = nm
        nv_ref[...] = nv

    spec = pl.BlockSpec((tr, C), lambda i: (i, 0))
    return pl.pallas_call(body, name=name, grid=(R // tr,), in_specs=[spec] * 4, out_specs=[spec] * 3,
                          out_shape=[_sds((R, C), F32)] * 3, compiler_params=_cp(1))(w, g, m, v)


def _pair_sum(gfull, rcv, c_idx, name):
    nb, R, C = gfull.shape
    half = R // 2
    tr = _row_tile(half, _rows_for_bytes(2 << 20, C))
    nt = half // tr

    def body(c_ref, g_ref, r_ref, o_ref):
        o_ref[...] = _bf(g_ref[...] + r_ref[...])

    return pl.pallas_call(
        body, name=name,
        grid_spec=pltpu.PrefetchScalarGridSpec(
            num_scalar_prefetch=1, grid=(nb, nt),
            in_specs=[pl.BlockSpec((None, tr, C), lambda j, i, c_ref: (j, c_ref[0] * nt + i, 0)),
                      pl.BlockSpec((None, tr, C), lambda j, i, c_ref: (j, i, 0))],
            out_specs=pl.BlockSpec((None, tr, C), lambda j, i, c_ref: (j, i, 0))),
        out_shape=_sds((nb, half, C), BF16), compiler_params=_cp(2))(c_idx, gfull, rcv)


def _chip_sum(arrived, own, place, name):
    nb, H, C = arrived.shape
    tr = _row_tile(H, _rows_for_bytes(2 << 20, C))
    nt = H // tr

    def body(pl_ref, *refs):
        o_ref = refs[nb + 1]
        me = pl_ref[0]
        acc = None
        for k in range(nb):
            term = jnp.where(me == k, refs[nb][...], refs[k][...]).astype(F32)
            acc = term if acc is None else acc + term
        o_ref[...] = acc

    def other(k):
        return pl.BlockSpec((None, tr, C), lambda i, p: (jnp.where(p[0] == k, (k + 1) % nb, k), i, 0))

    return pl.pallas_call(
        body, name=name,
        grid_spec=pltpu.PrefetchScalarGridSpec(
            num_scalar_prefetch=1, grid=(nt,),
            in_specs=[other(k) for k in range(nb)] + [pl.BlockSpec((None, tr, C), lambda i, p: (p[0], i, 0))],
            out_specs=pl.BlockSpec((tr, C), lambda i, p: (p[1] * nt + i, 0))),
        out_shape=_sds((2 * H, C), F32), compiler_params=_cp(1))(place, *([arrived] * nb), own)


def _cast_into_slot(shard, place, name):
    R, C = shard.shape
    tr = _row_tile(R, 256)

    def body(pl_ref, s_ref, o_ref):
        o_ref[...] = _bf(s_ref[...])

    return pl.pallas_call(
        body, name=name,
        grid_spec=pltpu.PrefetchScalarGridSpec(
            num_scalar_prefetch=1, grid=(R // tr,),
            in_specs=[pl.BlockSpec((tr, C), lambda i, p: (i, 0))],
            out_specs=pl.BlockSpec((None, tr, C), lambda i, p: (p[0], i, 0))),
        out_shape=_sds((N_CHIPS, R, C), BF16), compiler_params=_cp(1))(place, shard)


def _place():
    x, y, c = lax.axis_index("x"), lax.axis_index("y"), lax.axis_index("c")
    chips = [(1 - x, y), (x, 1 - y), (1 - x, 1 - y)]
    return x, y, c, chips


def _chip_id(px, py):
    return 2 * px + py


def _remote(src, dst, send_sems, recv_sems, k, to):
    return pltpu.make_async_remote_copy(src_ref=src, dst_ref=dst, send_sem=send_sems.at[k], recv_sem=recv_sems.at[k],
                                        device_id=to, device_id_type=MESH)


def _gather_weights(slots, wholes, name):
    ns, nw = len(slots), len(wholes)
    n = ns + nw

    def body(*refs):
        ins = refs[ns:n]
        outs = refs[n:2 * n]
        send_sems, recv_sems, local_sems = refs[2 * n:]
        x, y, c, chips = _place()
        me = _chip_id(x, y)
        sib = (x, y, 1 - c)
        local = [pltpu.make_async_copy(ins[b], outs[ns + b].at[me], local_sems.at[b]) for b in range(nw)]
        for cp in local:
            cp.start()
        sent = []
        for a in range(n):
            R = outs[a].shape[1]
            rows = pl.ds(c * (R // 2), R // 2) if a < ns else pl.ds(0, R)
            src = outs[a].at[me, rows] if a < ns else ins[a - ns]
            for j, chip in enumerate(chips):
                cp = _remote(src, outs[a].at[me, rows], send_sems, recv_sems, 6 * a + j, (*chip, c))
                cp.start()
                sent.append(cp)
        for a in range(n):
            R = outs[a].shape[1]
            rows = pl.ds(c * (R // 2), R // 2) if a < ns else pl.ds(0, R)
            for j, chip in enumerate(chips):
                landed = outs[a].at[_chip_id(*chip), rows]
                _remote(landed, landed, send_sems, recv_sems, 6 * a + j, (*chip, c)).wait_recv()
                if a < ns:
                    cp = _remote(landed, landed, send_sems, recv_sems, 6 * a + 3 + j, sib)
                    cp.start()
                    sent.append(cp)
        for a in range(ns):
            R = outs[a].shape[1]
            other = pl.ds((1 - c) * (R // 2), R // 2)
            for j, chip in enumerate(chips):
                passed = outs[a].at[_chip_id(*chip), other]
                _remote(passed, passed, send_sems, recv_sems, 6 * a + 3 + j, sib).wait_recv()
        for cp in sent:
            cp.wait_send()
        for cp in local:
            cp.wait()

    return pl.pallas_call(
        body, name=name, in_specs=[ANY] * n, out_specs=[ANY] * n,
        out_shape=[_sds(s.shape, s.dtype) for s in slots] + [_sds((N_CHIPS, *s.shape), s.dtype) for s in wholes],
        input_output_aliases={a: a for a in range(ns)},
        scratch_shapes=[pltpu.SemaphoreType.DMA((6 * n,)), pltpu.SemaphoreType.DMA((6 * n,)),
                        pltpu.SemaphoreType.DMA((max(nw, 1),))])(*slots, *wholes)


def _gather_ici_plan(slots, wholes):
    ns, nw = len(slots), len(wholes)

    def copies(ins, ios, outs, send_sems, recv_sems, local_sems):
        x, y, c, chips = _place()
        me = _chip_id(x, y)
        sends, recvs = [], []
        for a in range(ns + nw):
            dst = ios[a] if a < ns else outs[a - ns]
            R = dst.shape[1]
            rows = pl.ds(c * (R // 2), R // 2) if a < ns else pl.ds(0, R)
            src = dst.at[me, rows] if a < ns else ins[a - ns]
            for j, chip in enumerate(chips):
                sends.append(_remote(src, dst.at[me, rows], send_sems, recv_sems, 3 * a + j, (*chip, c)))
                landed = dst.at[_chip_id(*chip), rows]
                recvs.append(_remote(landed, landed, send_sems, recv_sems, 3 * a + j, (*chip, c)))
        local = [pltpu.make_async_copy(ins[b], outs[b].at[me], local_sems.at[b]) for b in range(nw)]
        return sends, recvs, local

    return _Plan(copies, 3 * (ns + nw), ins=wholes, inouts=slots,
                 outs=[_sds((N_CHIPS, *s.shape), s.dtype) for s in wholes])


def _gather_pass_plan(slots):
    def copies(ins, ios, outs, send_sems, recv_sems, local_sems):
        x, y, c, chips = _place()
        sib = (x, y, 1 - c)
        sends, recvs = [], []
        for a, buf in enumerate(ios):
            half = buf.shape[1] // 2
            for j, chip in enumerate(chips):
                mine = buf.at[_chip_id(*chip), pl.ds(c * half, half)]
                other = buf.at[_chip_id(*chip), pl.ds((1 - c) * half, half)]
                sends.append(_remote(mine, mine, send_sems, recv_sems, 3 * a + j, sib))
                recvs.append(_remote(other, other, send_sems, recv_sems, 3 * a + j, sib))
        return sends, recvs, []

    return _Plan(copies, 3 * len(slots), inouts=slots)


def _pair_plan(grads):
    def copies(ins, ios, outs, send_sems, recv_sems, local_sems):
        x, y, c, _ = _place()
        sib = (x, y, 1 - c)
        sends, recvs = [], []
        for a, g in enumerate(ins):
            half = g.shape[1] // 2
            sends.append(_remote(g.at[:, pl.ds((1 - c) * half, half), :], outs[a], send_sems, recv_sems, a, sib))
            recvs.append(_remote(outs[a], outs[a], send_sems, recv_sems, a, sib))
        return sends, recvs, []

    return _Plan(copies, len(grads), ins=grads,
                 outs=[_sds((g.shape[0], g.shape[1] // 2, g.shape[2]), g.dtype) for g in grads])


def _chip_plan(parts):
    def copies(ins, ios, outs, send_sems, recv_sems, local_sems):
        x, y, c, chips = _place()
        me = _chip_id(x, y)
        sends, recvs = [], []
        for a, part in enumerate(ins):
            for j, chip in enumerate(chips):
                sends.append(_remote(part.at[_chip_id(*chip)], outs[a].at[me], send_sems, recv_sems, 3 * a + j, (*chip, c)))
                landed = outs[a].at[_chip_id(*chip)]
                recvs.append(_remote(landed, landed, send_sems, recv_sems, 3 * a + j, (*chip, c)))
        return sends, recvs, []

    return _Plan(copies, 3 * len(parts), ins=parts, outs=[_sds(p.shape, p.dtype) for p in parts])


def _pair_concat(fulls, name):
    n = len(fulls)

    def body(*refs):
        outs = refs[n:2 * n]
        send_sems, recv_sems = refs[2 * n:]
        x, y, c, _ = _place()
        cps = []
        for a in range(n):
            H = outs[a].shape[0] // 2
            mine = outs[a].at[pl.ds(c * H, H)]
            cp = _remote(mine, mine, send_sems, recv_sems, a, (x, y, 1 - c))
            cp.start()
            cps.append(cp)
        for a, cp in enumerate(cps):
            H = outs[a].shape[0] // 2
            other = outs[a].at[pl.ds((1 - c) * H, H)]
            _remote(other, other, send_sems, recv_sems, a, (x, y, 1 - c)).wait_recv()
            cp.wait_send()

    return pl.pallas_call(
        body, name=name, in_specs=[ANY] * n, out_specs=[ANY] * n,
        out_shape=[_sds(f.shape, f.dtype) for f in fulls], input_output_aliases={a: a for a in range(n)},
        scratch_shapes=[pltpu.SemaphoreType.DMA((n,)), pltpu.SemaphoreType.DMA((n,))])(*fulls)


def _all_sum(pack, name):
    R, C = pack.shape

    def body(p_ref, o_ref, buf, send_sems, recv_sems):
        x, y, c, _ = _place()
        me = 4 * x + 2 * y + c
        buf[me] = p_ref[...]
        cps = []
        for k in range(1, N_DEV):
            to = (x ^ (k >> 2), y ^ ((k >> 1) & 1), c ^ (k & 1))
            cp = _remote(p_ref, buf.at[me], send_sems, recv_sems, k - 1, to)
            cp.start()
            cps.append(cp)
        for k in range(1, N_DEV):
            frm = (x ^ (k >> 2), y ^ ((k >> 1) & 1), c ^ (k & 1))
            slot = buf.at[4 * frm[0] + 2 * frm[1] + frm[2]]
            _remote(slot, slot, send_sems, recv_sems, k - 1, frm).wait_recv()
        acc = buf[0]
        for k in range(1, N_DEV):
            acc = acc + buf[k]
        o_ref[...] = acc
        for cp in cps:
            cp.wait_send()

    vm = pl.BlockSpec(memory_space=pltpu.VMEM)
    return pl.pallas_call(
        body, name=name, in_specs=[vm], out_specs=vm, out_shape=_sds((R, C), F32),
        scratch_shapes=[pltpu.VMEM((N_DEV, R, C), F32), pltpu.SemaphoreType.DMA((N_DEV - 1,)),
                        pltpu.SemaphoreType.DMA((N_DEV - 1,))])(pack)


def _local_step(xs, tgt, p, ex):
    h1 = _norm_fwd(xs, p["pre_mix_norm"], "pre_mix_norm")
    proj, got = _mm_nn_blk(h1, ex.weight("w_in"), "proj_in", plans=ex.carry("proj_in"))
    ex.done("proj_in", got)
    biases = _relbias_fwd(p["rel_bias"], "rel_bias_fwd")
    fw = []
    for g in range(N_GROUPS):
        res, got = _attn_fwd(proj, biases[g], g, f"attn_fwd{g}", plans=ex.carry(f"attn_fwd{g}"))
        ex.done(f"attn_fwd{g}", got)
        fw.append(res)
    y, lse = _attn_merge([t[0] for t in fw], [t[1] for t in fw], "attn_merge")
    yh, o_h, states = _hgrn_fwd(proj, p["hgrn_lb_raw"], p["hgrn_norm"], "hgrn_fwd")
    W_a, W_h, W_out = ex.weight("w_branch_attn"), ex.weight("w_branch_hgrn"), ex.weight("w_out")
    W_up, W_down, conv_w = ex.weight("w_up"), ex.weight("w_down"), ex.weight("conv_w")
    za, zh, merged = _branch_fwd(y, yh, proj, W_a, W_h, "branch_fwd")
    mo, x1, h2 = _mix_out(merged, W_out, xs, p["post_mix_norm"], p["pre_ffn_norm"], "mix_out")
    u = _mm_nn_blk(h2, W_up, "ffn_up")
    a = _conv_gelu_fwd(u, conv_w, p["conv_b"], "conv_gelu_fwd")
    dx2, dff, g_post_ffn, loss = _loss_head(a, W_down, x1, tgt, p["post_ffn_norm"], "ffn_down_loss")

    da = _mm_nt(dff, W_down, "d_ffn_act")
    ex.grad("w_down", _mm_tn(a, dff, "g_w_down").reshape(N_CHIPS, D_FF // N_CHIPS, D_MODEL))
    (dcg, dcv, gwg, gwv, gbg, gbv), got = _conv_gelu_bwd(u, da, conv_w, p["conv_b"], "conv_gelu_bwd",
                                                          plans=ex.carry("conv_gelu_bwd"))
    ex.done("conv_gelu_bwd", got)
    g_conv_w = jnp.concatenate([gwg, gwv], axis=1)
    g_conv_b = jnp.concatenate([gbg, gbv], axis=1)
    du_parts, got = _conv_input_bwd(dcg, dcv, conv_w, "conv_input_bwd", plans=ex.carry("conv_input_bwd"))
    ex.done("conv_input_bwd", got)
    du = jnp.concatenate(du_parts, axis=1)
    dh2 = _mm_nt_blk(du, W_up, "d_ffn_in")
    ex.grad("w_up", _mm_tn_blk(h2, du, N_CHIPS, "g_w_up"))
    (dx1, g_pre_ffn), got = _prenorm_bwd(dh2, x1, p["pre_ffn_norm"], dx2, "pre_ffn_norm_bwd",
                                         plans=ex.carry("pre_ffn_norm_bwd"))
    ex.done("pre_ffn_norm_bwd", got)
    dmo, dmerged, g_post_mix = _postnorm_bwd(dx1, mo, p["post_mix_norm"], W_out, "post_mix_norm_bwd")
    ex.grad("w_out", _mm_tn(merged, dmo, "g_w_out").reshape(N_CHIPS, D_MODEL // N_CHIPS, D_MODEL))
    dza, dzh, dg0, dg1, dy, dyh = _branch_bwd(dmerged, za, zh, proj, W_a, W_h, "branch_bwd")
    ex.grad("w_branch_attn", _mm_tn_blk(y, dza, N_CHIPS, "g_w_branch_attn", together=True))
    ex.grad("w_branch_hgrn", _mm_tn_blk(yh, dzh, N_CHIPS, "g_w_branch_hgrn", together=True))
    dqkv, dbs = [], []
    for g in range(N_GROUPS):
        parts, db, got = _attn_bwd(proj, biases[g], lse, y, dy, g, f"attn_bwd{g}", plans=ex.carry(f"attn_bwd{g}"))
        ex.done(f"attn_bwd{g}", got)
        dqkv += parts
        dbs.append(db)
    g_rel_bias = _relbias_bwd(dbs, "rel_bias_bwd")
    dhg, g_lb_raw, g_hgrn_norm = _hgrn_bwd(proj, p["hgrn_lb_raw"], p["hgrn_norm"], o_h, states, dyh, "hgrn_bwd")
    dproj = jnp.concatenate([*[_bf(t) for t in dqkv], *dhg, dg0, dg1], axis=1)
    for k, piece in enumerate(W_IN_PIECES):
        g, got = _mm_tn_blk(h1, dproj, N_CHIPS, f"g_{piece}", x_cols=(k, D_MODEL // len(W_IN_PIECES)),
                            plans=ex.carry(f"g_{piece}"))
        ex.done(f"g_{piece}", got)
        ex.grad(piece, g)
    dh1, got = _mm_nt_blk(dproj, ex.weight("w_in"), "d_proj_in", plans=ex.carry("d_proj_in"))
    ex.done("d_proj_in", got)
    (grad_x, g_pre_mix), got = _prenorm_bwd(dh1, xs, p["pre_mix_norm"], dx1, "pre_mix_norm_bwd",
                                            plans=ex.carry("pre_mix_norm_bwd"))
    ex.done("pre_mix_norm_bwd", got)
    small = dict(pre_mix_norm=g_pre_mix, rel_bias=g_rel_bias, hgrn_lb_raw=g_lb_raw, hgrn_norm=g_hgrn_norm,
                 post_mix_norm=g_post_mix, pre_ffn_norm=g_pre_ffn, conv_w=g_conv_w, conv_b=g_conv_b,
                 post_ffn_norm=g_post_ffn)
    return loss, grad_x, small


SMALL = ("pre_mix_norm", "rel_bias", "hgrn_lb_raw", "hgrn_norm", "post_mix_norm", "pre_ffn_norm", "conv_w", "conv_b",
         "post_ffn_norm")
BIG = ("w_in", "w_up", "w_down", "w_out", "w_branch_attn", "w_branch_hgrn")
WEIGHTS = ("pre_mix_norm", "w_in", "rel_bias", "hgrn_lb_raw", "hgrn_norm", "w_branch_attn", "w_branch_hgrn", "w_out",
           "post_mix_norm", "pre_ffn_norm", "w_up", "conv_w", "conv_b", "w_down", "post_ffn_norm")
MIXER = ("w_out", "w_branch_attn", "w_branch_hgrn")

SCHEDULE = {
    "proj_in": [("gather_ici_cw", ("w_up",) + MIXER)],
    "attn_fwd0": [("gather_pass", ("w_up",) + MIXER), ("gather_ici", ("w_down",))],
    "attn_fwd1": [("gather_pass", ("w_down",))],
    "conv_gelu_bwd": [("pair", ("w_down",))],
    "conv_input_bwd": [("chip", ("w_down",))],
    "pre_ffn_norm_bwd": [("pair", ("w_up",))],
    "attn_bwd0": [("chip", ("w_up",)), ("pair", MIXER)],
    "attn_bwd1": [("chip", MIXER)],
    "g_w_in_b": [("pair", ("w_in_a",))],
    "d_proj_in": [("chip", ("w_in_a",)), ("pair", ("w_in_b",))],
    "pre_mix_norm_bwd": [("chip", ("w_in_b",))],
}
W_IN_PIECES = ("w_in_a", "w_in_b")
REDUCED = W_IN_PIECES + BIG[1:]


class _Exchange:
    def __init__(self, place, slots, conv_w_shard):
        self.place, self.slots, self.conv_w_shard = place, dict(slots), conv_w_shard
        self.conv_w = None
        self.g, self.from_sibling, self.pair_sums, self.arrived = {}, {}, {}, {}
        self.pending = []

    def weight(self, name):
        if name == "conv_w":
            return self.conv_w
        w = self.slots[name]
        return w.reshape(-1, D_MODEL) if name in ("w_out", "w_down") else w

    def grad(self, name, g):
        self.g[name] = g

    def carry(self, point):
        plans = []
        self.pending = SCHEDULE.get(point, [])
        for kind, names in self.pending:
            if kind in ("gather_ici", "gather_ici_cw"):
                wholes = [self.conv_w_shard] if kind == "gather_ici_cw" else []
                plans.append(_gather_ici_plan([self.slots[n] for n in names], wholes))
            elif kind == "gather_pass":
                plans.append(_gather_pass_plan([self.slots[n] for n in names]))
            elif kind == "pair":
                plans.append(_pair_plan([self.g[n] for n in names]))
            else:
                for n in names:
                    self.pair_sums[n] = _pair_sum(self.g[n], self.from_sibling[n], self.place[1:2], f"pair_sum_{n}")
                plans.append(_chip_plan([self.pair_sums[n] for n in names]))
        return plans

    def done(self, point, carried):
        for (kind, names), got in zip(self.pending, carried):
            if kind in ("gather_ici", "gather_ici_cw", "gather_pass"):
                self.slots.update(zip(names, got))
                if kind == "gather_ici_cw":
                    self.conv_w = got[len(names)].transpose(1, 0, 2).reshape(3, 2 * D_FF)
            elif kind == "pair":
                self.from_sibling.update(zip(names, got))
            else:
                self.arrived.update(zip(names, got))

    def reduced(self):
        halves = [_chip_sum(self.arrived[n], self.pair_sums[n], self.place, f"chip_sum_{n}") for n in REDUCED]
        out = dict(zip(REDUCED, _pair_concat(halves, "pair_concat")))
        out["w_in"] = jnp.concatenate([out.pop(n) for n in W_IN_PIECES], axis=0)
        return out


def kernel(x, pre_mix_norm, w_in, rel_bias, hgrn_lb_raw, hgrn_norm, w_branch_attn, w_branch_hgrn, w_out, post_mix_norm, pre_ffn_norm, w_up, conv_w, conv_b, w_down, post_ffn_norm, loss_target, m_pre_mix_norm, m_w_in, m_rel_bias, m_hgrn_lb_raw, m_hgrn_norm, m_w_branch_attn, m_w_branch_hgrn, m_w_out, m_post_mix_norm, m_pre_ffn_norm, m_w_up, m_conv_w, m_conv_b, m_w_down, m_post_ffn_norm, v_pre_mix_norm, v_w_in, v_rel_bias, v_hgrn_lb_raw, v_hgrn_norm, v_w_branch_attn, v_w_branch_hgrn, v_w_out, v_post_mix_norm, v_pre_ffn_norm, v_w_up, v_conv_w, v_conv_b, v_w_down, v_post_ffn_norm):
    w = dict(pre_mix_norm=pre_mix_norm, w_in=w_in, rel_bias=rel_bias, hgrn_lb_raw=hgrn_lb_raw, hgrn_norm=hgrn_norm,
             w_branch_attn=w_branch_attn, w_branch_hgrn=w_branch_hgrn, w_out=w_out, post_mix_norm=post_mix_norm,
             pre_ffn_norm=pre_ffn_norm, w_up=w_up, conv_w=conv_w, conv_b=conv_b, w_down=w_down,
             post_ffn_norm=post_ffn_norm)
    m = dict(pre_mix_norm=m_pre_mix_norm, w_in=m_w_in, rel_bias=m_rel_bias, hgrn_lb_raw=m_hgrn_lb_raw,
             hgrn_norm=m_hgrn_norm, w_branch_attn=m_w_branch_attn, w_branch_hgrn=m_w_branch_hgrn, w_out=m_w_out,
             post_mix_norm=m_post_mix_norm, pre_ffn_norm=m_pre_ffn_norm, w_up=m_w_up, conv_w=m_conv_w,
             conv_b=m_conv_b, w_down=m_w_down, post_ffn_norm=m_post_ffn_norm)
    v = dict(pre_mix_norm=v_pre_mix_norm, w_in=v_w_in, rel_bias=v_rel_bias, hgrn_lb_raw=v_hgrn_lb_raw,
             hgrn_norm=v_hgrn_norm, w_branch_attn=v_w_branch_attn, w_branch_hgrn=v_w_branch_hgrn, w_out=v_w_out,
             post_mix_norm=v_post_mix_norm, pre_ffn_norm=v_pre_ffn_norm, w_up=v_w_up, conv_w=v_conv_w,
             conv_b=v_conv_b, w_down=v_w_down, post_ffn_norm=v_post_ffn_norm)
    shard2d = {n: (w[n][0] if w[n].ndim == 3 else w[n]) for n in WEIGHTS}
    chip = 2 * lax.axis_index("x") + lax.axis_index("y")
    core = lax.axis_index("c")

    place = jnp.stack([chip, core]).astype(jnp.int32)
    slots = {n: _cast_into_slot(shard2d[n], place, f"cast_{n}") for n in BIG}
    slots["w_in"] = _gather_weights([slots["w_in"]], [], "gather_w_in")[0]
    ex = _Exchange(place, slots, shard2d["conv_w"])
    loss, grad_x, small = _local_step(x[0], loss_target[0], {n: w[n] for n in SMALL if n != "conv_w"}, ex)

    flat = [small[n].reshape(-1) for n in SMALL] + [loss.reshape(-1)]
    sizes = [t.shape[0] for t in flat]
    summed = _all_sum(jnp.concatenate(flat).reshape(-1, LANES), "sum_small").reshape(-1)
    offs = [sum(sizes[:i]) for i in range(len(sizes))]
    grads = {}
    for n, o, sz in zip(SMALL, offs, sizes):
        grads[n] = summed[o:o + sz].reshape(small[n].shape)
    loss_total = summed[offs[-1]]
    cw = 2 * D_FF // N_CHIPS
    grads["conv_w"] = lax.dynamic_slice(grads["conv_w"], (0, chip * cw), (3, cw))

    grads.update(ex.reduced())

    out_g, out_d, out_m, out_v = [], [], [], []
    for n in WEIGHTS:
        d2, m2, v2 = _adamw(shard2d[n], grads[n], m[n].reshape(shard2d[n].shape), v[n].reshape(shard2d[n].shape),
                            f"adamw_{n}")
        shape = w[n].shape
        out_g.append(grads[n].reshape(shape))
        out_d.append(d2.reshape(shape))
        out_m.append(m2.reshape(shape))
        out_v.append(v2.reshape(shape))
    return (loss_total, grad_x[None], *out_g, *out_d, *out_m, *out_v)
```

```python
import functools
import math

import jax
import jax.numpy as jnp
from jax import lax
from jax.experimental import pallas as pl
from jax.experimental.pallas import tpu as pltpu

F32 = jnp.float32
BF16 = jnp.bfloat16
MESH = pl.DeviceIdType.MESH

D_MODEL = 1024
N_GROUPS = 3
DILATIONS = (1, 4, 16)
HEADS = 8
HEAD_DIM = 64
GROUP_W = HEADS * HEAD_DIM
QKV_W = N_GROUPS * 3 * GROUP_W
BLK = 128
NEG_INF = -1e30
NUM_BUCKETS = 32
MAX_EXACT = 16
MAX_DISTANCE = 2048
HG_HEADS = 4
HG_DK = 128
HG_W = HG_HEADS * HG_DK
HG_CHUNK = 32
HG_TILE = 256
IN_W = QKV_W + 4 * HG_W + 2 * D_MODEL
D_FF = 2816
EPS = 1e-6
N_CHIPS = 4
N_DEV = 8
LANES = 128

ADAM_LR, ADAM_B1, ADAM_B2, ADAM_EPS, ADAM_WD, ADAM_STEP = 0.001, 0.9, 0.999, 1e-08, 0.01, 10

VMEM_LIMIT = 56 * 1024 * 1024


def _cp(n_axes):
    return pltpu.CompilerParams(dimension_semantics=("arbitrary",) * n_axes, vmem_limit_bytes=VMEM_LIMIT)


def _sds(shape, dtype):
    return jax.ShapeDtypeStruct(tuple(shape), dtype)


def _sigmoid(v):
    return 1.0 / (1.0 + jnp.exp(-v))


def _bf(v):
    return v.astype(BF16)


def _dot(a, b, dims):
    return lax.dot_general(a, b, (dims, ((), ())), preferred_element_type=F32)


NN = ((1,), (0,))
NT = ((1,), (1,))
TN = ((0,), (0,))

ANY = pl.BlockSpec(memory_space=pl.ANY)


class _Plan:
    def __init__(self, copies, n_sems, ins=(), inouts=(), outs=()):
        self.copies, self.n_sems = copies, n_sems
        self.ins, self.inouts, self.outs = list(ins), list(inouts), list(outs)


def _call(body, plans=None, *, name, grid, in_specs, out_specs, out_shape, args, scratch_shapes=()):
    plans = list(plans or ())
    in_specs, out_specs, out_shape = list(in_specs), list(out_specs), list(out_shape)
    scratch_shapes = list(scratch_shapes)
    n_in, n_out, n_scr = len(in_specs), len(out_specs), len(scratch_shapes)
    x_in, x_out, aliases, spans = [], [], {}, []
    for p in plans:
        i0, o0 = len(x_in), len(x_out)
        x_in += p.ins
        for a in p.inouts:
            aliases[n_in + len(x_in)] = n_out + len(x_out)
            x_in.append(a)
            x_out.append(_sds(a.shape, a.dtype))
        x_out += p.outs
        spans.append((i0, len(p.ins), o0, len(p.inouts), len(p.outs)))
    sems = [pltpu.SemaphoreType.DMA((p.n_sems,)) for p in plans for _ in range(3)]

    def wrapped(*refs):
        xi = refs[n_in:n_in + len(x_in)]
        base = n_in + len(x_in)
        xo = refs[base + n_out:base + n_out + len(x_out)]
        sbase = base + n_out + len(x_out)
        xs = refs[sbase + n_scr:]
        ids = [pl.program_id(k) for k in range(len(grid))]
        first = functools.reduce(jnp.logical_and, [i == 0 for i in ids])
        last = functools.reduce(jnp.logical_and, [i == g - 1 for i, g in zip(ids, grid)])

        def descriptors(k):
            i0, ni, o0, nio, no = spans[k]
            return plans[k].copies(xi[i0:i0 + ni], xo[o0:o0 + nio], xo[o0 + nio:o0 + nio + no], *xs[3 * k:3 * k + 3])

        @pl.when(first)
        def _():
            for k in range(len(plans)):
                sends, _, local = descriptors(k)
                for cp in (*sends, *local):
                    cp.start()

        body(*refs[:n_in], *refs[base:base + n_out], *refs[sbase:sbase + n_scr])

        @pl.when(last)
        def _():
            for k in range(len(plans)):
                sends, recvs, local = descriptors(k)
                for cp in recvs:
                    cp.wait_recv()
                for cp in sends:
                    cp.wait_send()
                for cp in local:
                    cp.wait()

    res = pl.pallas_call(
        wrapped if plans else body, name=name, grid=grid, in_specs=in_specs + [ANY] * len(x_in),
        out_specs=out_specs + [ANY] * len(x_out), out_shape=out_shape + x_out, input_output_aliases=aliases,
        scratch_shapes=scratch_shapes + sems, compiler_params=_cp(len(grid)))(*args, *x_in)
    res = list(res)
    carried = [res[n_out + o0:n_out + o0 + nio + no] for (_, _, o0, nio, no) in spans]
    return res[:n_out], carried


def _mm_nn_blk(a, wg, name, tm=512, plans=None):
    M, K = a.shape
    nb, _, Nb = wg.shape

    def body(a_ref, w_ref, o_ref):
        o_ref[...] = _dot(_bf(a_ref[...]), w_ref[...], NN)

    (out,), carried = _call(
        body, plans, name=name, grid=(nb, M // tm),
        in_specs=[pl.BlockSpec((tm, K), lambda j, i: (i, 0)), pl.BlockSpec((None, K, Nb), lambda j, i: (j, 0, 0))],
        out_specs=[pl.BlockSpec((tm, Nb), lambda j, i: (i, j))],
        out_shape=[_sds((M, nb * Nb), F32)], args=(a, wg))
    return out if plans is None else (out, carried)


def _mm_nt_blk(dy, wg, name, tm=1024, plans=None):
    M = dy.shape[0]
    nb, K, Nb = wg.shape

    def body(dy_ref, w_ref, o_ref):
        j = pl.program_id(1)
        r = _dot(_bf(dy_ref[...]), w_ref[...], NT)

        @pl.when(j == 0)
        def _():
            o_ref[...] = r

        @pl.when(j > 0)
        def _():
            o_ref[...] += r

    (out,), carried = _call(
        body, plans, name=name, grid=(M // tm, nb),
        in_specs=[pl.BlockSpec((tm, Nb), lambda i, j: (i, j)), pl.BlockSpec((None, K, Nb), lambda i, j: (j, 0, 0))],
        out_specs=[pl.BlockSpec((tm, K), lambda i, j: (i, 0))],
        out_shape=[_sds((M, K), F32)], args=(dy, wg))
    return out if plans is None else (out, carried)


def _mm_tn_blk(x, dy, nb, name, tk=2048, x_cols=None, plans=None, together=False):
    T, Mx = x.shape
    xk, Mx = (0, Mx) if x_cols is None else x_cols
    Nb = dy.shape[1] // nb
    nj = nb if together else 1

    def body(x_ref, dy_ref, o_ref):
        t = pl.program_id(1)
        r = _dot(_bf(x_ref[...]), _bf(dy_ref[...]), TN)
        for j in range(nj):
            rj = r[:, j * Nb:(j + 1) * Nb]

            @pl.when(t == 0)
            def _():
                o_ref[j] = rj

            @pl.when(t > 0)
            def _():
                o_ref[j] += rj

    (out,), carried = _call(
        body, plans, name=name, grid=(nb // nj, T // tk),
        in_specs=[pl.BlockSpec((tk, Mx), lambda j, t: (t, xk)), pl.BlockSpec((tk, nj * Nb), lambda j, t: (t, j))],
        out_specs=[pl.BlockSpec((nj, Mx, Nb), lambda j, t: (j, 0, 0))],
        out_shape=[_sds((nb, Mx, Nb), F32)], args=(x, dy))
    return out if plans is None else (out, carried)


def _mm_nt(dy, w, name, tm=512):
    M, N = dy.shape
    K = w.shape[0]

    def body(dy_ref, w_ref, o_ref):
        o_ref[...] = _dot(_bf(dy_ref[...]), w_ref[...], NT)

    return pl.pallas_call(
        body, name=name, grid=(M // tm,),
        in_specs=[pl.BlockSpec((tm, N), lambda i: (i, 0)), pl.BlockSpec((K, N), lambda i: (0, 0))],
        out_specs=pl.BlockSpec((tm, K), lambda i: (i, 0)),
        out_shape=_sds((M, K), F32), compiler_params=_cp(1))(dy, w)


def _mm_tn(x, dy, name, tk=1024):
    T, Mx = x.shape
    N = dy.shape[1]

    def body(x_ref, dy_ref, o_ref):
        t = pl.program_id(0)
        r = _dot(_bf(x_ref[...]), _bf(dy_ref[...]), TN)

        @pl.when(t == 0)
        def _():
            o_ref[...] = r

        @pl.when(t > 0)
        def _():
            o_ref[...] += r

    return pl.pallas_call(
        body, name=name, grid=(T // tk,),
        in_specs=[pl.BlockSpec((tk, Mx), lambda t: (t, 0)), pl.BlockSpec((tk, N), lambda t: (t, 0))],
        out_specs=pl.BlockSpec((Mx, N), lambda t: (0, 0)),
        out_shape=_sds((Mx, N), F32), compiler_params=_cp(1))(x, dy)


def _tile(arr, bw, col=lambda c: 0):
    return ("tile", arr, bw, col)


def _full(arr):
    return ("full", arr)


def _out_tile(width, dtype, bw, col=lambda c: 0):
    return ("tile", width, dtype, bw, col)


def _out_acc(rows, width, bw, col=lambda c: 0):
    return ("acc", rows, width, bw, col)


def _rows_call(name, body, n_rows, tm, ncol, ins, outs, plans=None):
    in_specs, args = [], []
    for e in ins:
        if e[0] == "tile":
            _, arr, bw, col = e
            in_specs.append(pl.BlockSpec((tm, bw), functools.partial(lambda c, i, col: (i, col(c)), col=col)))
        else:
            arr = e[1]
            in_specs.append(pl.BlockSpec(arr.shape, functools.partial(lambda c, i, nd: (0,) * nd, nd=arr.ndim)))
        args.append(arr)
    out_specs, out_shape = [], []
    for e in outs:
        if e[0] == "tile":
            _, width, dtype, bw, col = e
            out_specs.append(pl.BlockSpec((tm, bw), functools.partial(lambda c, i, col: (i, col(c)), col=col)))
            out_shape.append(_sds((n_rows, width), dtype))
        else:
            _, rows, width, bw, col = e
            out_specs.append(pl.BlockSpec((rows, bw), functools.partial(lambda c, i, col: (0, col(c)), col=col)))
            out_shape.append(_sds((rows, width), F32))
    out, carried = _call(body, plans, name=name, grid=(ncol, n_rows // tm), in_specs=in_specs, out_specs=out_specs,
                         out_shape=out_shape, args=args)
    return out if plans is None else (out, carried)


def _acc(ref, val):
    i = pl.program_id(1)

    @pl.when(i == 0)
    def _():
        ref[...] = val

    @pl.when(i > 0)
    def _():
        ref[...] += val


def _rinv(z):
    return lax.rsqrt(jnp.mean(z * z, axis=-1, keepdims=True) + EPS)


def _norm_bwd(dy, zhat, r, w):
    dyw = dy * w
    return r * (dyw - zhat * jnp.mean(dyw * zhat, axis=-1, keepdims=True))


def _norm_fwd(x, w, name):
    def body(x_ref, w_ref, h_ref):
        xv = x_ref[...]
        h_ref[...] = _bf(xv * _rinv(xv) * w_ref[...])

    return _rows_call(name, body, x.shape[0], 512, 1, [_tile(x, D_MODEL), _full(w)],
                      [_out_tile(D_MODEL, BF16, D_MODEL)])[0]


def _prenorm_bwd(dh, xin, w, dres, name, plans=None):
    def body(dh_ref, x_ref, w_ref, dres_ref, dx_ref, dw_ref):
        xv = x_ref[...]
        r = _rinv(xv)
        xhat = xv * r
        dhv = dh_ref[...]
        dx_ref[...] = dres_ref[...] + _norm_bwd(dhv, xhat, r, w_ref[...])
        _acc(dw_ref, jnp.sum(dhv * xhat, axis=0, keepdims=True))

    return _rows_call(name, body, xin.shape[0], 512, 1,
                      [_tile(dh, D_MODEL), _tile(xin, D_MODEL), _full(w), _tile(dres, D_MODEL)],
                      [_out_tile(D_MODEL, F32, D_MODEL), _out_acc(1, D_MODEL, D_MODEL)], plans)


def _postnorm_bwd(dout, z, w, w_mat, name):
    def body(do_ref, z_ref, w_ref, wm_ref, dz_ref, dm_ref, dw_ref):
        zv = z_ref[...]
        r = _rinv(zv)
        zhat = zv * r
        dov = do_ref[...]
        dz = _bf(_norm_bwd(dov, zhat, r, w_ref[...]))
        dz_ref[...] = dz
        dm_ref[...] = _dot(dz, wm_ref[...], NT)
        _acc(dw_ref, jnp.sum(dov * zhat, axis=0, keepdims=True))

    return _rows_call(name, body, z.shape[0], 512, 1,
                      [_tile(dout, D_MODEL), _tile(z, D_MODEL), _full(w), _full(w_mat)],
                      [_out_tile(D_MODEL, BF16, D_MODEL), _out_tile(D_MODEL, F32, D_MODEL),
                       _out_acc(1, D_MODEL, D_MODEL)])


def _t5_bucket(dist):
    n = jnp.maximum(dist, 0)
    nf = jnp.maximum(n, 1).astype(F32)
    large = MAX_EXACT + (jnp.log(nf / MAX_EXACT) / math.log(MAX_DISTANCE / MAX_EXACT)
                         * (NUM_BUCKETS - MAX_EXACT)).astype(jnp.int32)
    large = jnp.minimum(large, NUM_BUCKETS - 1)
    return jnp.where(n < MAX_EXACT, n, large)


def _band_rel():
    return jnp.arange(BLK)[:, None] + BLK - jnp.arange(2 * BLK)[None, :]


def _band_mask(n):
    row = lax.broadcasted_iota(jnp.int32, (BLK, 2 * BLK), 0)
    col = lax.broadcasted_iota(jnp.int32, (BLK, 2 * BLK), 1)
    rel = row + BLK - col
    return (rel >= 0) & (rel <= BLK) & ((col >= BLK) | (n > 0))


RES_UNROLL = 4


def _heads_per_step(d):
    return HEADS if d == 1 else LANES // HEAD_DIM


def _sub_rows(r, d):
    return pl.ds(r, BLK, stride=d) if d > 1 else pl.ds(0, BLK)


def _for_residues(d, fn):
    if d <= RES_UNROLL:
        for r in range(d):
            fn(r)
    else:
        def group(i, carry):
            for k in range(RES_UNROLL):
                fn(i * RES_UNROLL + k)
            return carry

        lax.fori_loop(0, d // RES_UNROLL, group, 0)


def _attn_specs(d, g, qblock):
    cw = _heads_per_step(d) * HEAD_DIM

    def col(part, hp):
        return (g * 3 + part) * (GROUP_W // cw) + hp

    def cur(part):
        return pl.BlockSpec((d * BLK, cw), lambda hp, n: (qblock(n), col(part, hp)))

    def prev(part):
        return pl.BlockSpec((d * BLK, cw), lambda hp, n: (jnp.maximum(qblock(n) - 1, 0), col(part, hp)))

    return cur, prev


def _attn_fwd(proj, bias, g, name, plans=None):
    S = proj.shape[0]
    d = DILATIONS[g]
    NB = S // (d * BLK)
    hps = _heads_per_step(d)

    def body(q_ref, kp_ref, kc_ref, vp_ref, vc_ref, b_ref, o_ref, lse_ref):
        hp = pl.program_id(0)
        mask = _band_mask(pl.program_id(1))

        def residue(r):
            rows = _sub_rows(r, d)
            q2 = q_ref[rows, :]
            k2 = jnp.concatenate([kp_ref[rows, :], kc_ref[rows, :]], axis=0)
            v2 = jnp.concatenate([vp_ref[rows, :], vc_ref[rows, :]], axis=0)
            outs, lses = [], []
            for hh in range(hps):
                hs = slice(hh * HEAD_DIM, (hh + 1) * HEAD_DIM)
                s = _dot(_bf(q2[:, hs]), _bf(k2[:, hs]), NT) * (HEAD_DIM ** -0.5) + b_ref[hp * hps + hh]
                s = jnp.where(mask, s, NEG_INF)
                m = jnp.max(s, axis=-1, keepdims=True)
                p = jnp.exp(s - m)
                l = jnp.sum(p, axis=-1, keepdims=True)
                outs.append(_dot(_bf(p), _bf(v2[:, hs]), NN) / l)
                lses.append(jnp.broadcast_to(m + jnp.log(l), (BLK, HEAD_DIM)))
            o_ref[rows, :] = jnp.concatenate(outs, axis=1)
            lse_ref[rows, :] = jnp.concatenate(lses, axis=1)

        _for_residues(d, residue)

    cur, prev = _attn_specs(d, g, lambda n: n)
    out = pl.BlockSpec((d * BLK, hps * HEAD_DIM), lambda hp, n: (n, hp))
    res, carried = _call(
        body, plans, name=name, grid=(HEADS // hps, NB),
        in_specs=[cur(0), prev(1), cur(1), prev(2), cur(2),
                  pl.BlockSpec((HEADS, BLK, 2 * BLK), lambda hp, n: (0, 0, 0))],
        out_specs=[out, out], out_shape=[_sds((S, GROUP_W), F32)] * 2,
        args=(proj, proj, proj, proj, proj, bias))
    return res if plans is None else (res, carried)


def _attn_merge(os_, lses, name):
    def body(o0, o1, o2, l0, l1, l2, y_ref, lse_ref):
        a, b, c = l0[...], l1[...], l2[...]
        m = jnp.maximum(jnp.maximum(a, b), c)
        ea, eb, ec = jnp.exp(a - m), jnp.exp(b - m), jnp.exp(c - m)
        den = ea + eb + ec
        y_ref[...] = (ea * o0[...] + eb * o1[...] + ec * o2[...]) / den
        lse_ref[...] = m + jnp.log(den)

    S = os_[0].shape[0]
    return _rows_call(name, body, S, 512, 1, [_tile(t, GROUP_W) for t in (*os_, *lses)],
                      [_out_tile(GROUP_W, F32, GROUP_W)] * 2)


def _attn_bwd(proj, bias, lse, y, dy, g, name, plans=None):
    S = proj.shape[0]
    d = DILATIONS[g]
    NB = S // (d * BLK)
    hps = _heads_per_step(d)

    def body(q_ref, kp_ref, kc_ref, vp_ref, vc_ref, b_ref, l_ref, y_ref, dy_ref,
             dq_ref, dk_ref, dv_ref, db_ref, ck_ref, cv_ref):
        hp, n = pl.program_id(0), pl.program_id(1)

        @pl.when((hp == 0) & (n == 0))
        def _():
            db_ref[...] = jnp.zeros_like(db_ref)

        @pl.when(n == 0)
        def _():
            ck_ref[...] = jnp.zeros_like(ck_ref)
            cv_ref[...] = jnp.zeros_like(cv_ref)

        @pl.when(n < NB)
        def _():
            mask = _band_mask(n)

            def residue(r):
                rows = _sub_rows(r, d)
                q2 = q_ref[rows, :]
                k2 = jnp.concatenate([kp_ref[rows, :], kc_ref[rows, :]], axis=0)
                v2 = jnp.concatenate([vp_ref[rows, :], vc_ref[rows, :]], axis=0)
                l2, y2, dy2 = l_ref[rows, :], y_ref[rows, :], dy_ref[rows, :]
                dqs, dks, dvs = [], [], []
                for hh in range(hps):
                    hs = slice(hh * HEAD_DIM, (hh + 1) * HEAD_DIM)
                    q, kb, vb = _bf(q2[:, hs]), _bf(k2[:, hs]), _bf(v2[:, hs])
                    s = _dot(q, kb, NT) * (HEAD_DIM ** -0.5) + b_ref[hp * hps + hh]
                    s = jnp.where(mask, s, NEG_INF)
                    p = jnp.exp(s - l2[:, hh * HEAD_DIM:hh * HEAD_DIM + 1])
                    dyh = dy2[:, hs]
                    delta = jnp.sum(dyh * y2[:, hs], axis=-1, keepdims=True)
                    dyb = _bf(dyh)
                    ds = p * (_dot(dyb, vb, NT) - delta)
                    db_ref[hp * hps + hh] += ds
                    dsb = _bf(ds * (HEAD_DIM ** -0.5))
                    dqs.append(_dot(dsb, kb, NN))
                    dks.append(_dot(dsb, q, TN))
                    dvs.append(_dot(_bf(p), dyb, TN))
                dkb = jnp.concatenate(dks, axis=1)
                dvb = jnp.concatenate(dvs, axis=1)
                dq_ref[rows, :] = jnp.concatenate(dqs, axis=1)
                dk_ref[rows, :] = ck_ref[rows, :] + dkb[:BLK]
                dv_ref[rows, :] = cv_ref[rows, :] + dvb[:BLK]
                ck_ref[rows, :] = dkb[BLK:]
                cv_ref[rows, :] = dvb[BLK:]

            _for_residues(d, residue)

        @pl.when(n == NB)
        def _():
            dk_ref[...] = ck_ref[...]
            dv_ref[...] = cv_ref[...]

    def qn(n):
        return jnp.minimum(n, NB - 1)

    cur, prev = _attn_specs(d, g, qn)
    cw = hps * HEAD_DIM
    row = pl.BlockSpec((d * BLK, cw), lambda hp, n: (qn(n), hp))
    done = pl.BlockSpec((d * BLK, cw), lambda hp, n: (jnp.maximum(n - 1, 0), hp))
    (dq, dk, dv, db), carried = _call(
        body, plans, name=name, grid=(HEADS // hps, NB + 1),
        in_specs=[cur(0), prev(1), cur(1), prev(2), cur(2),
                  pl.BlockSpec((HEADS, BLK, 2 * BLK), lambda hp, n: (0, 0, 0)), row, row, row],
        out_specs=[row, done, done, pl.BlockSpec((HEADS, BLK, 2 * BLK), lambda hp, n: (0, 0, 0))],
        out_shape=[_sds((S, GROUP_W), F32)] * 3 + [_sds((HEADS, BLK, 2 * BLK), F32)],
        scratch_shapes=[pltpu.VMEM((d * BLK, cw), F32)] * 2,
        args=(proj, proj, proj, proj, proj, bias, lse, y, dy))
    return ([dq, dk, dv], db) if plans is None else ([dq, dk, dv], db, carried)


BAND = BLK * 2 * BLK


def _bucket_onehot():
    buckets = jnp.stack([_t5_bucket(_band_rel() * d) for d in DILATIONS]).reshape(N_GROUPS, 1, BAND)
    return (buckets == jnp.arange(NUM_BUCKETS).reshape(1, NUM_BUCKETS, 1)).astype(F32)


def _relbias_fwd(rel_bias, name):
    table = rel_bias.reshape(NUM_BUCKETS, N_GROUPS, HEADS).transpose(1, 0, 2)

    def body(t_ref, oh_ref, o_ref):
        o_ref[...] = lax.dot_general(t_ref[...], oh_ref[...], (TN, ((), ())), preferred_element_type=F32,
                                     precision=lax.Precision.HIGHEST)

    out = pl.pallas_call(
        body, name=name, grid=(N_GROUPS,),
        in_specs=[pl.BlockSpec((None, NUM_BUCKETS, HEADS), lambda g: (g, 0, 0)),
                  pl.BlockSpec((None, NUM_BUCKETS, BAND), lambda g: (g, 0, 0))],
        out_specs=pl.BlockSpec((None, HEADS, BAND), lambda g: (g, 0, 0)),
        out_shape=_sds((N_GROUPS, HEADS, BAND), F32), compiler_params=_cp(1))(table, _bucket_onehot())
    return out.reshape(N_GROUPS, HEADS, BLK, 2 * BLK)


def _relbias_bwd(dbs, name):
    band = BAND
    onehot = _bucket_onehot()
    dbf = jnp.stack([db.reshape(HEADS, band) for db in dbs])

    def body(oh_ref, db_ref, o_ref):
        o_ref[...] = lax.dot_general(oh_ref[...], db_ref[...], (NT, ((), ())), preferred_element_type=F32,
                                     precision=lax.Precision.HIGHEST)

    out = pl.pallas_call(
        body, name=name, grid=(N_GROUPS,),
        in_specs=[pl.BlockSpec((None, NUM_BUCKETS, band), lambda g: (g, 0, 0)),
                  pl.BlockSpec((None, HEADS, band), lambda g: (g, 0, 0))],
        out_specs=pl.BlockSpec((None, NUM_BUCKETS, HEADS), lambda g: (g, 0, 0)),
        out_shape=_sds((N_GROUPS, NUM_BUCKETS, HEADS), F32), compiler_params=_cp(1))(onehot, dbf)
    return out.transpose(1, 0, 2).reshape(NUM_BUCKETS, N_GROUPS * HEADS)


def _chunk_pos(shape):
    return lax.broadcasted_iota(jnp.int32, shape, 0) % HG_CHUNK


def _chunk_cumsum(v):
    pos = _chunk_pos(v.shape)
    s = 1
    while s < HG_CHUNK:
        v = v + jnp.where(pos >= s, pltpu.roll(v, s, 0), 0.0)
        s *= 2
    return v


def _chunk_rev_cumsum(v):
    pos = _chunk_pos(v.shape)
    n = v.shape[0]
    s = 1
    while s < HG_CHUNK:
        v = v + jnp.where(pos < HG_CHUNK - s, pltpu.roll(v, n - s, 0), 0.0)
        s *= 2
    return v


def _lower_bound(raw):
    a0, a1 = raw[0:1], raw[1:2]
    m = jnp.maximum(a0, a1)
    e0, e1 = jnp.exp(a0 - m), jnp.exp(a1 - m)
    return e0 / (e0 + e1)


def _hg_gates(qr, fr, lb):
    sf = _sigmoid(fr)
    f = lb + (1.0 - lb) * sf
    sq = _sigmoid(qr)
    return qr * sq, sq, f, sf


HG_COL0 = QKV_W // HG_W


def _hgrn_fwd(proj, lb_raw, nw, name):
    S = proj.shape[0]
    ncs = HG_TILE // HG_CHUNK
    tril = jnp.tril(jnp.ones((HG_CHUNK, HG_CHUNK), dtype=bool))

    def body(q_ref, f_ref, i_ref, og_ref, lb_ref, nw_ref, y_ref, o_ref, st_ref, state):
        @pl.when(pl.program_id(0) == 0)
        def _():
            state[...] = jnp.zeros_like(state)

        lb = _lower_bound(lb_ref[...])
        q, _, f, _ = _hg_gates(q_ref[...], f_ref[...], lb)
        k = 1.0 - f
        G = _chunk_cumsum(jnp.log(f))
        row = lax.broadcasted_iota(jnp.int32, (HG_CHUNK, HG_CHUNK), 0)
        col = lax.broadcasted_iota(jnp.int32, (HG_CHUNK, HG_CHUNK), 1)
        heads = [slice(h * HG_DK, (h + 1) * HG_DK) for h in range(HG_HEADS)]
        sts = [state[h] for h in range(HG_HEADS)]
        for c in range(ncs):
            cs = slice(c * HG_CHUNK, (c + 1) * HG_CHUNK)
            for h, hs in enumerate(heads):
                Gc = G[cs, hs]
                gl = Gc[HG_CHUNK - 1:HG_CHUNK]
                qt = _bf(q[cs, hs] * jnp.exp(Gc))
                kt = _bf(k[cs, hs] * jnp.exp(-Gc))
                kd = _bf(k[cs, hs] * jnp.exp(gl - Gc))
                v = _bf(i_ref[cs, hs])
                A = jnp.where(row >= col, _dot(qt, kt, NT), 0.0)
                o_ref[cs, hs] = _dot(_bf(A), v, NN) + _dot(qt, _bf(sts[h]), NT)
                st_ref[c, h] = sts[h]
                sts[h] = sts[h] * jnp.exp(gl) + _dot(v, kd, TN)
        for h, hs in enumerate(heads):
            state[h] = sts[h]
            oh = o_ref[:, hs]
            og = og_ref[:, hs]
            y_ref[:, hs] = oh * _rinv(oh) * nw_ref[...] * (og * _sigmoid(og))

    def colspec(j):
        return pl.BlockSpec((HG_TILE, HG_W), lambda i: (i, HG_COL0 + j))

    return pl.pallas_call(
        body, name=name, grid=(S // HG_TILE,),
        in_specs=[colspec(0), colspec(1), colspec(2), colspec(3),
                  pl.BlockSpec((2, HG_W), lambda i: (0, 0)), pl.BlockSpec((1, HG_DK), lambda i: (0, 0))],
        out_specs=[pl.BlockSpec((HG_TILE, HG_W), lambda i: (i, 0))] * 2
        + [pl.BlockSpec((ncs, HG_HEADS, HG_DK, HG_DK), lambda i: (i, 0, 0, 0))],
        out_shape=[_sds((S, HG_W), F32)] * 2 + [_sds((S // HG_CHUNK, HG_HEADS, HG_DK, HG_DK), F32)],
        scratch_shapes=[pltpu.VMEM((HG_HEADS, HG_DK, HG_DK), F32)],
        compiler_params=_cp(1))(proj, proj, proj, proj, lb_raw, nw)


def _hgrn_bwd(proj, lb_raw, nw, o, states, dy, name):
    S = proj.shape[0]
    ncs = HG_TILE // HG_CHUNK
    nt = S // HG_TILE

    def body(q_ref, f_ref, i_ref, og_ref, lb_ref, nw_ref, o_ref, st_ref, dy_ref,
             dq_ref, df_ref, di_ref, dog_ref, dlb_ref, dnw_ref, dstate, do_s, dG_s, dgl_s, dk_s, dlb_s):
        step = pl.program_id(0)

        @pl.when(step == 0)
        def _():
            dstate[...] = jnp.zeros_like(dstate)
            dlb_s[...] = jnp.zeros_like(dlb_s)
            dnw_ref[...] = jnp.zeros_like(dnw_ref)

        lb = _lower_bound(lb_ref[...])
        qr = q_ref[...]
        q, sq, f, sf = _hg_gates(qr, f_ref[...], lb)
        k = 1.0 - f
        G = _chunk_cumsum(jnp.log(f))
        nwv = nw_ref[...]
        row = lax.broadcasted_iota(jnp.int32, (HG_CHUNK, HG_CHUNK), 0)
        col = lax.broadcasted_iota(jnp.int32, (HG_CHUNK, HG_CHUNK), 1)
        for h in range(HG_HEADS):
            hs = slice(h * HG_DK, (h + 1) * HG_DK)
            oh = o_ref[:, hs]
            r = _rinv(oh)
            ohat = oh * r
            og = og_ref[:, hs]
            sg = _sigmoid(og)
            dyh = dy_ref[:, hs]
            don = dyh * (og * sg)
            dog_ref[:, hs] = _bf(dyh * (ohat * nwv) * (sg * (1.0 + og * (1.0 - sg))))
            dnw_ref[...] += jnp.sum(don * ohat, axis=0, keepdims=True)
            do_s[:, hs] = _norm_bwd(don, ohat, r, nwv)
        dsts = [dstate[h] for h in range(HG_HEADS)]
        for c in reversed(range(ncs)):
            cs = slice(c * HG_CHUNK, (c + 1) * HG_CHUNK)
            for h in range(HG_HEADS):
                hs = slice(h * HG_DK, (h + 1) * HG_DK)
                dst = dsts[h]
                Gc = G[cs, hs]
                gl = Gc[HG_CHUNK - 1:HG_CHUNK]
                eG, enG, edG, egl = jnp.exp(Gc), jnp.exp(-Gc), jnp.exp(gl - Gc), jnp.exp(gl)
                qt, kt, kd = q[cs, hs] * eG, k[cs, hs] * enG, k[cs, hs] * edG
                qtb, ktb, kdb = _bf(qt), _bf(kt), _bf(kd)
                v = _bf(i_ref[cs, hs])
                do = _bf(do_s[cs, hs])
                st = st_ref[c, h]
                dstb = _bf(dst)
                A = jnp.where(row >= col, _dot(qtb, ktb, NT), 0.0)
                dA = _bf(jnp.where(row >= col, _dot(do, v, NT), 0.0))
                di_ref[cs, hs] = _bf(_dot(_bf(A), do, TN) + _dot(kdb, dstb, NT))
                dqt = _dot(dA, ktb, NN) + _dot(do, _bf(st), NN)
                dkt = _dot(dA, qtb, TN)
                dkd = _dot(v, dstb, NN)
                dgl = egl * jnp.sum(st * dst, axis=0, keepdims=True) + jnp.sum(dkd * kd, axis=0, keepdims=True)
                dsts[h] = dst * egl + _dot(do, qtb, TN)
                dq_ref[cs, hs] = _bf(dqt * eG * (sq[cs, hs] * (1.0 + qr[cs, hs] * (1.0 - sq[cs, hs]))))
                dk_s[cs, hs] = dkt * enG + dkd * edG
                dG_s[cs, hs] = dqt * qt - dkt * kt - dkd * kd
                dgl_s[cs, hs] = jnp.broadcast_to(dgl, (HG_CHUNK, HG_DK))
        for h in range(HG_HEADS):
            dstate[h] = dsts[h]
        dg = _chunk_rev_cumsum(dG_s[...]) + dgl_s[...]
        dfv = dg / f - dk_s[...]
        df_ref[...] = _bf(dfv * (1.0 - lb) * sf * (1.0 - sf))
        dlb_s[...] += jnp.sum(dfv * (1.0 - sf), axis=0, keepdims=True)

        @pl.when(step == nt - 1)
        def _():
            t = dlb_s[...] * lb * (1.0 - lb)
            dlb_ref[...] = jnp.concatenate([t, -t], axis=0)

    def colspec(j):
        return pl.BlockSpec((HG_TILE, HG_W), lambda i: (nt - 1 - i, HG_COL0 + j))

    tile = pl.BlockSpec((HG_TILE, HG_W), lambda i: (nt - 1 - i, 0))
    outs = pl.pallas_call(
        body, name=name, grid=(nt,),
        in_specs=[colspec(0), colspec(1), colspec(2), colspec(3),
                  pl.BlockSpec((2, HG_W), lambda i: (0, 0)), pl.BlockSpec((1, HG_DK), lambda i: (0, 0)),
                  tile, pl.BlockSpec((ncs, HG_HEADS, HG_DK, HG_DK), lambda i: (nt - 1 - i, 0, 0, 0)), tile],
        out_specs=[tile] * 4 + [pl.BlockSpec((2, HG_W), lambda i: (0, 0)), pl.BlockSpec((1, HG_DK), lambda i: (0, 0))],
        out_shape=[_sds((S, HG_W), BF16)] * 4 + [_sds((2, HG_W), F32), _sds((1, HG_DK), F32)],
        scratch_shapes=[pltpu.VMEM((HG_HEADS, HG_DK, HG_DK), F32)] + [pltpu.VMEM((HG_TILE, HG_W), F32)] * 4
        + [pltpu.VMEM((1, HG_W), F32)],
        compiler_params=_cp(1))(proj, proj, proj, proj, lb_raw, nw, o, states, dy)
    return outs[:4], outs[4], outs[5]


GATE_COL0 = (QKV_W + 4 * HG_W) // GROUP_W
HALF_D = D_MODEL // 2


def _gate_tiles(proj):
    return [_tile(proj, HALF_D, functools.partial(lambda c, k: GATE_COL0 + k, k=k)) for k in range(4)]


def _gates(g_refs):
    s0 = _sigmoid(jnp.concatenate([g_refs[0][...], g_refs[1][...]], axis=1))
    s1 = _sigmoid(jnp.concatenate([g_refs[2][...], g_refs[3][...]], axis=1))
    return s0, s1


def _branch_fwd(y, yh, proj, w_a, w_h, name):
    nb = w_a.shape[0]

    def body(y_ref, yh_ref, g0a, g0b, g1a, g1b, wa_ref, wh_ref, za_ref, zh_ref, m_ref):
        yb, yhb = _bf(y_ref[...]), _bf(yh_ref[...])
        za = jnp.concatenate([_dot(yb, wa_ref[j], NN) for j in range(nb)], axis=1)
        zh = jnp.concatenate([_dot(yhb, wh_ref[j], NN) for j in range(nb)], axis=1)
        s0, s1 = _gates((g0a, g0b, g1a, g1b))
        za_ref[...] = za
        zh_ref[...] = zh
        m_ref[...] = _bf(s0 * za + s1 * zh)

    return _rows_call(name, body, y.shape[0], 512, 1,
                      [_tile(y, GROUP_W), _tile(yh, HG_W), *_gate_tiles(proj), _full(w_a), _full(w_h)],
                      [_out_tile(D_MODEL, F32, D_MODEL)] * 2 + [_out_tile(D_MODEL, BF16, D_MODEL)])


def _branch_bwd(dm, za, zh, proj, w_a, w_h, name):
    nb, _, Nb = w_a.shape

    def body(dm_ref, za_ref, zh_ref, g0a, g0b, g1a, g1b, wa_ref, wh_ref,
             dza_ref, dzh_ref, dg0_ref, dg1_ref, dy_ref, dyh_ref):
        dmv = dm_ref[...]
        s0, s1 = _gates((g0a, g0b, g1a, g1b))
        dza, dzh = _bf(dmv * s0), _bf(dmv * s1)
        dza_ref[...] = dza
        dzh_ref[...] = dzh
        dg0_ref[...] = _bf(dmv * za_ref[...] * s0 * (1.0 - s0))
        dg1_ref[...] = _bf(dmv * zh_ref[...] * s1 * (1.0 - s1))
        dy_ref[...] = sum(_dot(dza[:, j * Nb:(j + 1) * Nb], wa_ref[j], NT) for j in range(nb))
        dyh_ref[...] = sum(_dot(dzh[:, j * Nb:(j + 1) * Nb], wh_ref[j], NT) for j in range(nb))

    return _rows_call(name, body, za.shape[0], 512, 1,
                      [_tile(dm, D_MODEL), _tile(za, D_MODEL), _tile(zh, D_MODEL), *_gate_tiles(proj),
                       _full(w_a), _full(w_h)],
                      [_out_tile(D_MODEL, BF16, D_MODEL)] * 4 + [_out_tile(GROUP_W, F32, GROUP_W),
                                                                 _out_tile(HG_W, F32, HG_W)])


def _mix_out(merged, w_out, x, w_post, w_pre, name):
    def body(m_ref, wo_ref, x_ref, wp_ref, wf_ref, mo_ref, x1_ref, h2_ref):
        z = _dot(m_ref[...], wo_ref[...], NN)
        mo_ref[...] = z
        x1 = x_ref[...] + z * _rinv(z) * wp_ref[...]
        x1_ref[...] = x1
        h2_ref[...] = _bf(x1 * _rinv(x1) * wf_ref[...])

    return _rows_call(name, body, x.shape[0], 512, 1,
                      [_tile(merged, D_MODEL), _full(w_out), _tile(x, D_MODEL), _full(w_post), _full(w_pre)],
                      [_out_tile(D_MODEL, F32, D_MODEL), _out_tile(D_MODEL, F32, D_MODEL),
                       _out_tile(D_MODEL, BF16, D_MODEL)])


def _loss_head(a, w_down, x1, tgt, w, name):
    def body(a_ref, wd_ref, x1_ref, t_ref, w_ref, dx_ref, df_ref, dw_ref, loss_ref):
        z = _dot(a_ref[...], wd_ref[...], NN)
        r = _rinv(z)
        zhat = z * r
        wv = w_ref[...]
        e = x1_ref[...] + zhat * wv - t_ref[...]
        dx = e * (1.0 / D_MODEL)
        dx_ref[...] = dx
        df_ref[...] = _bf(_norm_bwd(dx, zhat, r, wv))
        _acc(dw_ref, jnp.sum(dx * zhat, axis=0, keepdims=True))
        part = 0.5 * jnp.sum(jnp.sum(e * e, axis=1, keepdims=True), axis=0, keepdims=True) * (1.0 / D_MODEL)
        _acc(loss_ref, jnp.broadcast_to(part, (1, LANES)))

    return _rows_call(name, body, x1.shape[0], 512, 1,
                      [_tile(a, D_FF), _full(w_down), _tile(x1, D_MODEL), _tile(tgt, D_MODEL), _full(w)],
                      [_out_tile(D_MODEL, F32, D_MODEL), _out_tile(D_MODEL, BF16, D_MODEL),
                       _out_acc(1, D_MODEL, D_MODEL), _out_acc(1, LANES, LANES)])


CONV_CB = D_FF // 2
CONV_TM = 512
HALO = 8
SQRT_HALF = 0.7071067811865476
INV_SQRT_2PI = 0.3989422804014327


CONV_RS = 32


def _lane_tiles():
    return [slice(k * LANES, (k + 1) * LANES) for k in range(CONV_CB // LANES)]


def _strip_start(i):
    return pl.multiple_of(i * CONV_RS, CONV_RS)


def _strip_taps(u_ref, halo_ref, r0, cs, first_strip, first_tile):
    if first_strip:
        before = jnp.where(first_tile, 0.0, halo_ref[:, cs])
        blk = jnp.concatenate([before, u_ref[0:CONV_RS, cs]], axis=0)
    else:
        blk = u_ref[pl.ds(pl.multiple_of(r0 - HALO, HALO), CONV_RS + HALO), cs]
    return pltpu.roll(blk, 2, 0)[HALO:], pltpu.roll(blk, 1, 0)[HALO:], blk[HALO:]


def _conv(taps, w_ref, b_ref, cs):
    return b_ref[:, cs] + w_ref[0:1, cs] * taps[0] + w_ref[1:2, cs] * taps[1] + w_ref[2:3, cs] * taps[2]


def _conv_specs(tm):
    nh = tm // HALO
    nc = D_FF // CONV_CB

    def tile(off):
        return pl.BlockSpec((tm, CONV_CB), lambda c, i: (i, off + c))

    def halo(off):
        return pl.BlockSpec((HALO, CONV_CB), lambda c, i: (jnp.maximum(i * nh - 1, 0), off + c))

    def small(rows, off):
        return pl.BlockSpec((rows, CONV_CB), lambda c, i: (0, off + c))

    return nc, tile, halo, small


def _conv_gelu_fwd(u, cw, cb, name):
    S = u.shape[0]
    tm = CONV_TM
    nc, tile, halo, small = _conv_specs(tm)

    def body(ug, hg, uv, hv, wg, wv, bg, bv, a_ref):
        first_tile = pl.program_id(1) == 0

        def strip(r0, first_strip):
            for cs in _lane_tiles():
                cg = _conv(_strip_taps(ug, hg, r0, cs, first_strip, first_tile), wg, bg, cs)
                cv = _conv(_strip_taps(uv, hv, r0, cs, first_strip, first_tile), wv, bv, cs)
                a_ref[pl.ds(r0, CONV_RS), cs] = _bf(0.5 * cg * (1.0 + lax.erf(cg * SQRT_HALF)) * cv)

        strip(0, True)
        lax.fori_loop(1, tm // CONV_RS, lambda k, c: (strip(_strip_start(k), False), c)[1], 0)

    return pl.pallas_call(
        body, name=name, grid=(nc, S // tm),
        in_specs=[tile(0), halo(0), tile(nc), halo(nc), small(3, 0), small(3, nc), small(1, 0), small(1, nc)],
        out_specs=tile(0), out_shape=_sds((S, D_FF), BF16), compiler_params=_cp(2))(u, u, u, u, cw, cw, cb, cb)


def _conv_gelu_bwd(u, da, cw, cb, name, plans=None):
    S = u.shape[0]
    tm = CONV_TM
    nt = S // tm
    nc, tile, halo, small = _conv_specs(tm)

    def body(ug, hg, uv, hv, wg, wv, bg, bv, da_ref, dcg_ref, dcv_ref, dwg_ref, dwv_ref, dbg_ref, dbv_ref, acc):
        i = pl.program_id(1)
        first_tile = i == 0

        @pl.when(first_tile)
        def _():
            acc[...] = jnp.zeros_like(acc)

        def strip(r0, first_strip):
            rows = pl.ds(r0, CONV_RS)
            for cs in _lane_tiles():
                tg = _strip_taps(ug, hg, r0, cs, first_strip, first_tile)
                tv = _strip_taps(uv, hv, r0, cs, first_strip, first_tile)
                cg = _conv(tg, wg, bg, cs)
                cv = _conv(tv, wv, bv, cs)
                phi = 0.5 * (1.0 + lax.erf(cg * SQRT_HALF))
                dav = da_ref[rows, cs]
                dcg = dav * cv * (phi + cg * jnp.exp(-0.5 * cg * cg) * INV_SQRT_2PI)
                dcv = dav * (cg * phi)
                dcg_ref[rows, cs] = dcg
                dcv_ref[rows, cs] = dcv
                for half, (dc, taps) in enumerate(((dcg, tg), (dcv, tv))):
                    for j in range(3):
                        acc[4 * half + j, :, cs] += dc * taps[j]
                    acc[4 * half + 3, :, cs] += dc

        strip(0, True)
        lax.fori_loop(1, tm // CONV_RS, lambda k, c: (strip(_strip_start(k), False), c)[1], 0)

        @pl.when(i == nt - 1)
        def _():
            for half, (dw_ref, db_ref) in enumerate(((dwg_ref, dbg_ref), (dwv_ref, dbv_ref))):
                for j in range(3):
                    dw_ref[j:j + 1, :] = jnp.sum(acc[4 * half + j], axis=0, keepdims=True)
                db_ref[...] = jnp.sum(acc[4 * half + 3], axis=0, keepdims=True)

    res, carried = _call(
        body, plans, name=name, grid=(nc, nt),
        in_specs=[tile(0), halo(0), tile(nc), halo(nc), small(3, 0), small(3, nc), small(1, 0), small(1, nc), tile(0)],
        out_specs=[tile(0), tile(0), small(3, 0), small(3, 0), small(1, 0), small(1, 0)],
        out_shape=[_sds((S, D_FF), F32)] * 2 + [_sds((3, D_FF), F32)] * 2 + [_sds((1, D_FF), F32)] * 2,
        scratch_shapes=[pltpu.VMEM((8, CONV_RS, CONV_CB), F32)], args=(u, u, u, u, cw, cw, cb, cb, da))
    return res if plans is None else (res, carried)


def _conv_input_bwd(dcg, dcv, cw, name, plans=None):
    S = dcg.shape[0]
    tm = CONV_TM
    nc, tile, _, small = _conv_specs(tm)
    nh = tm // HALO
    nt = S // tm
    n = CONV_RS + HALO

    def nxt(off):
        return pl.BlockSpec((HALO, CONV_CB), lambda c, i: (jnp.minimum((i + 1) * nh, S // HALO - 1), off + c))

    def body(g_ref, ng_ref, v_ref, nv_ref, wg, wv, dug_ref, duv_ref):
        last_tile = pl.program_id(1) == nt - 1

        def strip(r0, last_strip):
            for cs in _lane_tiles():
                for dc_ref, n_ref, w_ref, du_ref in ((g_ref, ng_ref, wg, dug_ref), (v_ref, nv_ref, wv, duv_ref)):
                    if last_strip:
                        after = jnp.where(last_tile, 0.0, n_ref[:, cs])
                        blk = jnp.concatenate([dc_ref[tm - CONV_RS:tm, cs], after], axis=0)
                    else:
                        blk = dc_ref[pl.ds(r0, n), cs]
                    d1 = pltpu.roll(blk, n - 1, 0)[:CONV_RS]
                    d2 = pltpu.roll(blk, n - 2, 0)[:CONV_RS]
                    du_ref[pl.ds(r0, CONV_RS), cs] = _bf(w_ref[2:3, cs] * blk[:CONV_RS] + w_ref[1:2, cs] * d1
                                                         + w_ref[0:1, cs] * d2)

        lax.fori_loop(0, tm // CONV_RS - 1, lambda k, c: (strip(_strip_start(k), False), c)[1], 0)
        strip(tm - CONV_RS, True)

    res, carried = _call(
        body, plans, name=name, grid=(nc, nt),
        in_specs=[tile(0), nxt(0), tile(0), nxt(0), small(3, 0), small(3, nc)],
        out_specs=[tile(0), tile(0)], out_shape=[_sds((S, D_FF), BF16)] * 2,
        args=(dcg, dcg, dcv, dcv, cw, cw))
    return res if plans is None else (res, carried)


def _row_tile(n, cap):
    best = n
    for t in range(16, cap + 1, 16):
        if n % t == 0:
            best = t
    return best if best <= cap else n


def _rows_for_bytes(nbytes, cols):
    return max(16, nbytes // (4 * cols) // 16 * 16)


def _adamw(w, g, m, v, name):
    R, C = w.shape
    tr = _row_tile(R, _rows_for_bytes(2 << 20, C))

    def body(w_ref, g_ref, m_ref, v_ref, d_ref, nm_ref, nv_ref):
        gv = g_ref[...]
        nm = ADAM_B1 * m_ref[...] + (1.0 - ADAM_B1) * gv
        nv = ADAM_B2 * v_ref[...] + (1.0 - ADAM_B2) * (gv * gv)
        m_hat = nm / (1.0 - ADAM_B1 ** ADAM_STEP)
        v_hat = nv / (1.0 - ADAM_B2 ** ADAM_STEP)
        d_ref[...] = -ADAM_LR * (m_hat / (jnp.sqrt(v_hat) + ADAM_EPS) + ADAM_WD * w_ref[...])
        nm_ref[...] = nm
        nv_ref[...] = nv

    spec = pl.BlockSpec((tr, C), lambda i: (i, 0))
    return pl.pallas_call(body, name=name, grid=(R // tr,), in_specs=[spec] * 4, out_specs=[spec] * 3,
                          out_shape=[_sds((R, C), F32)] * 3, compiler_params=_cp(1))(w, g, m, v)


def _pair_sum(gfull, rcv, c_idx, name):
    nb, R, C = gfull.shape
    half = R // 2
    tr = _row_tile(half, _rows_for_bytes(2 << 20, C))
    nt = half // tr

    def body(c_ref, g_ref, r_ref, o_ref):
        o_ref[...] = _bf(g_ref[...] + r_ref[...])

    return pl.pallas_call(
        body, name=name,
        grid_spec=pltpu.PrefetchScalarGridSpec(
            num_scalar_prefetch=1, grid=(nb, nt),
            in_specs=[pl.BlockSpec((None, tr, C), lambda j, i, c_ref: (j, c_ref[0] * nt + i, 0)),
                      pl.BlockSpec((None, tr, C), lambda j, i, c_ref: (j, i, 0))],
            out_specs=pl.BlockSpec((None, tr, C), lambda j, i, c_ref: (j, i, 0))),
        out_shape=_sds((nb, half, C), BF16), compiler_params=_cp(2))(c_idx, gfull, rcv)


def _chip_sum(arrived, own, place, name):
    nb, H, C = arrived.shape
    tr = _row_tile(H, _rows_for_bytes(2 << 20, C))
    nt = H // tr

    def body(pl_ref, *refs):
        o_ref = refs[nb + 1]
        me = pl_ref[0]
        acc = None
        for k in range(nb):
            term = jnp.where(me == k, refs[nb][...], refs[k][...]).astype(F32)
            acc = term if acc is None else acc + term
        o_ref[...] = acc

    def other(k):
        return pl.BlockSpec((None, tr, C), lambda i, p: (jnp.where(p[0] == k, (k + 1) % nb, k), i, 0))

    return pl.pallas_call(
        body, name=name,
        grid_spec=pltpu.PrefetchScalarGridSpec(
            num_scalar_prefetch=1, grid=(nt,),
            in_specs=[other(k) for k in range(nb)] + [pl.BlockSpec((None, tr, C), lambda i, p: (p[0], i, 0))],
            out_specs=pl.BlockSpec((tr, C), lambda i, p: (p[1] * nt + i, 0))),
        out_shape=_sds((2 * H, C), F32), compiler_params=_cp(1))(place, *([arrived] * nb), own)


def _cast_into_slot(shard, place, name):
    R, C = shard.shape
    tr = _row_tile(R, 256)

    def body(pl_ref, s_ref, o_ref):
        o_ref[...] = _bf(s_ref[...])

    return pl.pallas_call(
        body, name=name,
        grid_spec=pltpu.PrefetchScalarGridSpec(
            num_scalar_prefetch=1, grid=(R // tr,),
            in_specs=[pl.BlockSpec((tr, C), lambda i, p: (i, 0))],
            out_specs=pl.BlockSpec((None, tr, C), lambda i, p: (p[0], i, 0))),
        out_shape=_sds((N_CHIPS, R, C), BF16), compiler_params=_cp(1))(place, shard)


def _place():
    x, y, c = lax.axis_index("x"), lax.axis_index("y"), lax.axis_index("c")
    chips = [(1 - x, y), (x, 1 - y), (1 - x, 1 - y)]
    return x, y, c, chips


def _chip_id(px, py):
    return 2 * px + py


def _remote(src, dst, send_sems, recv_sems, k, to):
    return pltpu.make_async_remote_copy(src_ref=src, dst_ref=dst, send_sem=send_sems.at[k], recv_sem=recv_sems.at[k],
                                        device_id=to, device_id_type=MESH)


def _gather_weights(slots, wholes, name):
    ns, nw = len(slots), len(wholes)
    n = ns + nw

    def body(*refs):
        ins = refs[ns:n]
        outs = refs[n:2 * n]
        send_sems, recv_sems, local_sems = refs[2 * n:]
        x, y, c, chips = _place()
        me = _chip_id(x, y)
        sib = (x, y, 1 - c)
        local = [pltpu.make_async_copy(ins[b], outs[ns + b].at[me], local_sems.at[b]) for b in range(nw)]
        for cp in local:
            cp.start()
        sent = []
        for a in range(n):
            R = outs[a].shape[1]
            rows = pl.ds(c * (R // 2), R // 2) if a < ns else pl.ds(0, R)
            src = outs[a].at[me, rows] if a < ns else ins[a - ns]
            for j, chip in enumerate(chips):
                cp = _remote(src, outs[a].at[me, rows], send_sems, recv_sems, 6 * a + j, (*chip, c))
                cp.start()
                sent.append(cp)
        for a in range(n):
            R = outs[a].shape[1]
            rows = pl.ds(c * (R // 2), R // 2) if a < ns else pl.ds(0, R)
            for j, chip in enumerate(chips):
                landed = outs[a].at[_chip_id(*chip), rows]
                _remote(landed, landed, send_sems, recv_sems, 6 * a + j, (*chip, c)).wait_recv()
                if a < ns:
                    cp = _remote(landed, landed, send_sems, recv_sems, 6 * a + 3 + j, sib)
                    cp.start()
                    sent.append(cp)
        for a in range(ns):
            R = outs[a].shape[1]
            other = pl.ds((1 - c) * (R // 2), R // 2)
            for j, chip in enumerate(chips):
                passed = outs[a].at[_chip_id(*chip), other]
                _remote(passed, passed, send_sems, recv_sems, 6 * a + 3 + j, sib).wait_recv()
        for cp in sent:
            cp.wait_send()
        for cp in local:
            cp.wait()

    return pl.pallas_call(
        body, name=name, in_specs=[ANY] * n, out_specs=[ANY] * n,
        out_shape=[_sds(s.shape, s.dtype) for s in slots] + [_sds((N_CHIPS, *s.shape), s.dtype) for s in wholes],
        input_output_aliases={a: a for a in range(ns)},
        scratch_shapes=[pltpu.SemaphoreType.DMA((6 * n,)), pltpu.SemaphoreType.DMA((6 * n,)),
                        pltpu.SemaphoreType.DMA((max(nw, 1),))])(*slots, *wholes)


def _gather_ici_plan(slots, wholes):
    ns, nw = len(slots), len(wholes)

    def copies(ins, ios, outs, send_sems, recv_sems, local_sems):
        x, y, c, chips = _place()
        me = _chip_id(x, y)
        sends, recvs = [], []
        for a in range(ns + nw):
            dst = ios[a] if a < ns else outs[a - ns]
            R = dst.shape[1]
            rows = pl.ds(c * (R // 2), R // 2) if a < ns else pl.ds(0, R)
            src = dst.at[me, rows] if a < ns else ins[a - ns]
            for j, chip in enumerate(chips):
                sends.append(_remote(src, dst.at[me, rows], send_sems, recv_sems, 3 * a + j, (*chip, c)))
                landed = dst.at[_chip_id(*chip), rows]
                recvs.append(_remote(landed, landed, send_sems, recv_sems, 3 * a + j, (*chip, c)))
        local = [pltpu.make_async_copy(ins[b], outs[b].at[me], local_sems.at[b]) for b in range(nw)]
        return sends, recvs, local

    return _Plan(copies, 3 * (ns + nw), ins=wholes, inouts=slots,
                 outs=[_sds((N_CHIPS, *s.shape), s.dtype) for s in wholes])


def _gather_pass_plan(slots):
    def copies(ins, ios, outs, send_sems, recv_sems, local_sems):
        x, y, c, chips = _place()
        sib = (x, y, 1 - c)
        sends, recvs = [], []
        for a, buf in enumerate(ios):
            half = buf.shape[1] // 2
            for j, chip in enumerate(chips):
                mine = buf.at[_chip_id(*chip), pl.ds(c * half, half)]
                other = buf.at[_chip_id(*chip), pl.ds((1 - c) * half, half)]
                sends.append(_remote(mine, mine, send_sems, recv_sems, 3 * a + j, sib))
                recvs.append(_remote(other, other, send_sems, recv_sems, 3 * a + j, sib))
        return sends, recvs, []

    return _Plan(copies, 3 * len(slots), inouts=slots)


def _pair_plan(grads):
    def copies(ins, ios, outs, send_sems, recv_sems, local_sems):
        x, y, c, _ = _place()
        sib = (x, y, 1 - c)
        sends, recvs = [], []
        for a, g in enumerate(ins):
            half = g.shape[1] // 2
            sends.append(_remote(g.at[:, pl.ds((1 - c) * half, half), :], outs[a], send_sems, recv_sems, a, sib))
            recvs.append(_remote(outs[a], outs[a], send_sems, recv_sems, a, sib))
        return sends, recvs, []

    return _Plan(copies, len(grads), ins=grads,
                 outs=[_sds((g.shape[0], g.shape[1] // 2, g.shape[2]), g.dtype) for g in grads])


def _chip_plan(parts):
    def copies(ins, ios, outs, send_sems, recv_sems, local_sems):
        x, y, c, chips = _place()
        me = _chip_id(x, y)
        sends, recvs = [], []
        for a, part in enumerate(ins):
            for j, chip in enumerate(chips):
                sends.append(_remote(part.at[_chip_id(*chip)], outs[a].at[me], send_sems, recv_sems, 3 * a + j, (*chip, c)))
                landed = outs[a].at[_chip_id(*chip)]
                recvs.append(_remote(landed, landed, send_sems, recv_sems, 3 * a + j, (*chip, c)))
        return sends, recvs, []

    return _Plan(copies, 3 * len(parts), ins=parts, outs=[_sds(p.shape, p.dtype) for p in parts])


def _pair_concat(fulls, name):
    n = len(fulls)

    def body(*refs):
        outs = refs[n:2 * n]
        send_sems, recv_sems = refs[2 * n:]
        x, y, c, _ = _place()
        cps = []
        for a in range(n):
            H = outs[a].shape[0] // 2
            mine = outs[a].at[pl.ds(c * H, H)]
            cp = _remote(mine, mine, send_sems, recv_sems, a, (x, y, 1 - c))
            cp.start()
            cps.append(cp)
        for a, cp in enumerate(cps):
            H = outs[a].shape[0] // 2
            other = outs[a].at[pl.ds((1 - c) * H, H)]
            _remote(other, other, send_sems, recv_sems, a, (x, y, 1 - c)).wait_recv()
            cp.wait_send()

    return pl.pallas_call(
        body, name=name, in_specs=[ANY] * n, out_specs=[ANY] * n,
        out_shape=[_sds(f.shape, f.dtype) for f in fulls], input_output_aliases={a: a for a in range(n)},
        scratch_shapes=[pltpu.SemaphoreType.DMA((n,)), pltpu.SemaphoreType.DMA((n,))])(*fulls)


def _all_sum(pack, name):
    R, C = pack.shape

    def body(p_ref, o_ref, buf, send_sems, recv_sems):
        x, y, c, _ = _place()
        me = 4 * x + 2 * y + c
        buf[me] = p_ref[...]
        cps = []
        for k in range(1, N_DEV):
            to = (x ^ (k >> 2), y ^ ((k >> 1) & 1), c ^ (k & 1))
            cp = _remote(p_ref, buf.at[me], send_sems, recv_sems, k - 1, to)
            cp.start()
            cps.append(cp)
        for k in range(1, N_DEV):
            frm = (x ^ (k >> 2), y ^ ((k >> 1) & 1), c ^ (k & 1))
            slot = buf.at[4 * frm[0] + 2 * frm[1] + frm[2]]
            _remote(slot, slot, send_sems, recv_sems, k - 1, frm).wait_recv()
        acc = buf[0]
        for k in range(1, N_DEV):
            acc = acc + buf[k]
        o_ref[...] = acc
        for cp in cps:
            cp.wait_send()

    vm = pl.BlockSpec(memory_space=pltpu.VMEM)
    return pl.pallas_call(
        body, name=name, in_specs=[vm], out_specs=vm, out_shape=_sds((R, C), F32),
        scratch_shapes=[pltpu.VMEM((N_DEV, R, C), F32), pltpu.SemaphoreType.DMA((N_DEV - 1,)),
                        pltpu.SemaphoreType.DMA((N_DEV - 1,))])(pack)


def _local_step(xs, tgt, p, ex):
    h1 = _norm_fwd(xs, p["pre_mix_norm"], "pre_mix_norm")
    proj, got = _mm_nn_blk(h1, ex.weight("w_in"), "proj_in", plans=ex.carry("proj_in"))
    ex.done("proj_in", got)
    biases = _relbias_fwd(p["rel_bias"], "rel_bias_fwd")
    fw = []
    for g in range(N_GROUPS):
        res, got = _attn_fwd(proj, biases[g], g, f"attn_fwd{g}", plans=ex.carry(f"attn_fwd{g}"))
        ex.done(f"attn_fwd{g}", got)
        fw.append(res)
    y, lse = _attn_merge([t[0] for t in fw], [t[1] for t in fw], "attn_merge")
    yh, o_h, states = _hgrn_fwd(proj, p["hgrn_lb_raw"], p["hgrn_norm"], "hgrn_fwd")
    W_a, W_h, W_out = ex.weight("w_branch_attn"), ex.weight("w_branch_hgrn"), ex.weight("w_out")
    W_up, W_down, conv_w = ex.weight("w_up"), ex.weight("w_down"), ex.weight("conv_w")
    za, zh, merged = _branch_fwd(y, yh, proj, W_a, W_h, "branch_fwd")
    mo, x1, h2 = _mix_out(merged, W_out, xs, p["post_mix_norm"], p["pre_ffn_norm"], "mix_out")
    u = _mm_nn_blk(h2, W_up, "ffn_up")
    a = _conv_gelu_fwd(u, conv_w, p["conv_b"], "conv_gelu_fwd")
    dx2, dff, g_post_ffn, loss = _loss_head(a, W_down, x1, tgt, p["post_ffn_norm"], "ffn_down_loss")

    da = _mm_nt(dff, W_down, "d_ffn_act")
    ex.grad("w_down", _mm_tn(a, dff, "g_w_down").reshape(N_CHIPS, D_FF // N_CHIPS, D_MODEL))
    (dcg, dcv, gwg, gwv, gbg, gbv), got = _conv_gelu_bwd(u, da, conv_w, p["conv_b"], "conv_gelu_bwd",
                                                          plans=ex.carry("conv_gelu_bwd"))
    ex.done("conv_gelu_bwd", got)
    g_conv_w = jnp.concatenate([gwg, gwv], axis=1)
    g_conv_b = jnp.concatenate([gbg, gbv], axis=1)
    du_parts, got = _conv_input_bwd(dcg, dcv, conv_w, "conv_input_bwd", plans=ex.carry("conv_input_bwd"))
    ex.done("conv_input_bwd", got)
    du = jnp.concatenate(du_parts, axis=1)
    dh2 = _mm_nt_blk(du, W_up, "d_ffn_in")
    ex.grad("w_up", _mm_tn_blk(h2, du, N_CHIPS, "g_w_up"))
    (dx1, g_pre_ffn), got = _prenorm_bwd(dh2, x1, p["pre_ffn_norm"], dx2, "pre_ffn_norm_bwd",
                                         plans=ex.carry("pre_ffn_norm_bwd"))
    ex.done("pre_ffn_norm_bwd", got)
    dmo, dmerged, g_post_mix = _postnorm_bwd(dx1, mo, p["post_mix_norm"], W_out, "post_mix_norm_bwd")
    ex.grad("w_out", _mm_tn(merged, dmo, "g_w_out").reshape(N_CHIPS, D_MODEL // N_CHIPS, D_MODEL))
    dza, dzh, dg0, dg1, dy, dyh = _branch_bwd(dmerged, za, zh, proj, W_a, W_h, "branch_bwd")
    ex.grad("w_branch_attn", _mm_tn_blk(y, dza, N_CHIPS, "g_w_branch_attn", together=True))
    ex.grad("w_branch_hgrn", _mm_tn_blk(yh, dzh, N_CHIPS, "g_w_branch_hgrn", together=True))
    dqkv, dbs = [], []
    for g in range(N_GROUPS):
        parts, db, got = _attn_bwd(proj, biases[g], lse, y, dy, g, f"attn_bwd{g}", plans=ex.carry(f"attn_bwd{g}"))
        ex.done(f"attn_bwd{g}", got)
        dqkv += parts
        dbs.append(db)
    g_rel_bias = _relbias_bwd(dbs, "rel_bias_bwd")
    dhg, g_lb_raw, g_hgrn_norm = _hgrn_bwd(proj, p["hgrn_lb_raw"], p["hgrn_norm"], o_h, states, dyh, "hgrn_bwd")
    dproj = jnp.concatenate([*[_bf(t) for t in dqkv], *dhg, dg0, dg1], axis=1)
    for k, piece in enumerate(W_IN_PIECES):
        g, got = _mm_tn_blk(h1, dproj, N_CHIPS, f"g_{piece}", x_cols=(k, D_MODEL // len(W_IN_PIECES)),
                            plans=ex.carry(f"g_{piece}"))
        ex.done(f"g_{piece}", got)
        ex.grad(piece, g)
    dh1, got = _mm_nt_blk(dproj, ex.weight("w_in"), "d_proj_in", plans=ex.carry("d_proj_in"))
    ex.done("d_proj_in", got)
    (grad_x, g_pre_mix), got = _prenorm_bwd(dh1, xs, p["pre_mix_norm"], dx1, "pre_mix_norm_bwd",
                                            plans=ex.carry("pre_mix_norm_bwd"))
    ex.done("pre_mix_norm_bwd", got)
    small = dict(pre_mix_norm=g_pre_mix, rel_bias=g_rel_bias, hgrn_lb_raw=g_lb_raw, hgrn_norm=g_hgrn_norm,
                 post_mix_norm=g_post_mix, pre_ffn_norm=g_pre_ffn, conv_w=g_conv_w, conv_b=g_conv_b,
                 post_ffn_norm=g_post_ffn)
    return loss, grad_x, small


SMALL = ("pre_mix_norm", "rel_bias", "hgrn_lb_raw", "hgrn_norm", "post_mix_norm", "pre_ffn_norm", "conv_w", "conv_b",
         "post_ffn_norm")
BIG = ("w_in", "w_up", "w_down", "w_out", "w_branch_attn", "w_branch_hgrn")
WEIGHTS = ("pre_mix_norm", "w_in", "rel_bias", "hgrn_lb_raw", "hgrn_norm", "w_branch_attn", "w_branch_hgrn", "w_out",
           "post_mix_norm", "pre_ffn_norm", "w_up", "conv_w", "conv_b", "w_down", "post_ffn_norm")
MIXER = ("w_out", "w_branch_attn", "w_branch_hgrn")

SCHEDULE = {
    "proj_in": [("gather_ici_cw", ("w_up",) + MIXER)],
    "attn_fwd0": [("gather_pass", ("w_up",) + MIXER), ("gather_ici", ("w_down",))],
    "attn_fwd1": [("gather_pass", ("w_down",))],
    "conv_gelu_bwd": [("pair", ("w_down",))],
    "conv_input_bwd": [("chip", ("w_down",))],
    "pre_ffn_norm_bwd": [("pair", ("w_up",))],
    "attn_bwd0": [("chip", ("w_up",)), ("pair", MIXER)],
    "attn_bwd1": [("chip", MIXER)],
    "g_w_in_b": [("pair", ("w_in_a",))],
    "g_w_in_c": [("chip", ("w_in_a",)), ("pair", ("w_in_b",))],
    "g_w_in_d": [("chip", ("w_in_b",)), ("pair", ("w_in_c",))],
    "d_proj_in": [("chip", ("w_in_c",)), ("pair", ("w_in_d",))],
    "pre_mix_norm_bwd": [("chip", ("w_in_d",))],
}
W_IN_PIECES = ("w_in_a", "w_in_b", "w_in_c", "w_in_d")
REDUCED = W_IN_PIECES + BIG[1:]


class _Exchange:
    def __init__(self, place, slots, conv_w_shard):
        self.place, self.slots, self.conv_w_shard = place, dict(slots), conv_w_shard
        self.conv_w = None
        self.g, self.from_sibling, self.pair_sums, self.arrived = {}, {}, {}, {}
        self.pending = []

    def weight(self, name):
        if name == "conv_w":
            return self.conv_w
        w = self.slots[name]
        return w.reshape(-1, D_MODEL) if name in ("w_out", "w_down") else w

    def grad(self, name, g):
        self.g[name] = g

    def carry(self, point):
        plans = []
        self.pending = SCHEDULE.get(point, [])
        for kind, names in self.pending:
            if kind in ("gather_ici", "gather_ici_cw"):
                wholes = [self.conv_w_shard] if kind == "gather_ici_cw" else []
                plans.append(_gather_ici_plan([self.slots[n] for n in names], wholes))
            elif kind == "gather_pass":
                plans.append(_gather_pass_plan([self.slots[n] for n in names]))
            elif kind == "pair":
                plans.append(_pair_plan([self.g[n] for n in names]))
            else:
                for n in names:
                    self.pair_sums[n] = _pair_sum(self.g[n], self.from_sibling[n], self.place[1:2], f"pair_sum_{n}")
                plans.append(_chip_plan([self.pair_sums[n] for n in names]))
        return plans

    def done(self, point, carried):
        for (kind, names), got in zip(self.pending, carried):
            if kind in ("gather_ici", "gather_ici_cw", "gather_pass"):
                self.slots.update(zip(names, got))
                if kind == "gather_ici_cw":
                    self.conv_w = got[len(names)].transpose(1, 0, 2).reshape(3, 2 * D_FF)
            elif kind == "pair":
                self.from_sibling.update(zip(names, got))
            else:
                self.arrived.update(zip(names, got))

    def reduced(self):
        halves = [_chip_sum(self.arrived[n], self.pair_sums[n], self.place, f"chip_sum_{n}") for n in REDUCED]
        out = dict(zip(REDUCED, _pair_concat(halves, "pair_concat")))
        out["w_in"] = jnp.concatenate([out.pop(n) for n in W_IN_PIECES], axis=0)
        return out


def kernel(x, pre_mix_norm, w_in, rel_bias, hgrn_lb_raw, hgrn_norm, w_branch_attn, w_branch_hgrn, w_out, post_mix_norm, pre_ffn_norm, w_up, conv_w, conv_b, w_down, post_ffn_norm, loss_target, m_pre_mix_norm, m_w_in, m_rel_bias, m_hgrn_lb_raw, m_hgrn_norm, m_w_branch_attn, m_w_branch_hgrn, m_w_out, m_post_mix_norm, m_pre_ffn_norm, m_w_up, m_conv_w, m_conv_b, m_w_down, m_post_ffn_norm, v_pre_mix_norm, v_w_in, v_rel_bias, v_hgrn_lb_raw, v_hgrn_norm, v_w_branch_attn, v_w_branch_hgrn, v_w_out, v_post_mix_norm, v_pre_ffn_norm, v_w_up, v_conv_w, v_conv_b, v_w_down, v_post_ffn_norm):
    w = dict(pre_mix_norm=pre_mix_norm, w_in=w_in, rel_bias=rel_bias, hgrn_lb_raw=hgrn_lb_raw, hgrn_norm=hgrn_norm,
             w_branch_attn=w_branch_attn, w_branch_hgrn=w_branch_hgrn, w_out=w_out, post_mix_norm=post_mix_norm,
             pre_ffn_norm=pre_ffn_norm, w_up=w_up, conv_w=conv_w, conv_b=conv_b, w_down=w_down,
             post_ffn_norm=post_ffn_norm)
    m = dict(pre_mix_norm=m_pre_mix_norm, w_in=m_w_in, rel_bias=m_rel_bias, hgrn_lb_raw=m_hgrn_lb_raw,
             hgrn_norm=m_hgrn_norm, w_branch_attn=m_w_branch_attn, w_branch_hgrn=m_w_branch_hgrn, w_out=m_w_out,
             post_mix_norm=m_post_mix_norm, pre_ffn_norm=m_pre_ffn_norm, w_up=m_w_up, conv_w=m_conv_w,
             conv_b=m_conv_b, w_down=m_w_down, post_ffn_norm=m_post_ffn_norm)
    v = dict(pre_mix_norm=v_pre_mix_norm, w_in=v_w_in, rel_bias=v_rel_bias, hgrn_lb_raw=v_hgrn_lb_raw,
             hgrn_norm=v_hgrn_norm, w_branch_attn=v_w_branch_attn, w_branch_hgrn=v_w_branch_hgrn, w_out=v_w_out,
             post_mix_norm=v_post_mix_norm, pre_ffn_norm=v_pre_ffn_norm, w_up=v_w_up, conv_w=v_conv_w,
             conv_b=v_conv_b, w_down=v_w_down, post_ffn_norm=v_post_ffn_norm)
    shard2d = {n: (w[n][0] if w[n].ndim == 3 else w[n]) for n in WEIGHTS}
    chip = 2 * lax.axis_index("x") + lax.axis_index("y")
    core = lax.axis_index("c")

    place = jnp.stack([chip, core]).astype(jnp.int32)
    slots = {n: _cast_into_slot(shard2d[n], place, f"cast_{n}") for n in BIG}
    slots["w_in"] = _gather_weights([slots["w_in"]], [], "gather_w_in")[0]
    ex = _Exchange(place, slots, shard2d["conv_w"])
    loss, grad_x, small = _local_step(x[0], loss_target[0], {n: w[n] for n in SMALL if n != "conv_w"}, ex)

    flat = [small[n].reshape(-1) for n in SMALL] + [loss.reshape(-1)]
    sizes = [t.shape[0] for t in flat]
    summed = _all_sum(jnp.concatenate(flat).reshape(-1, LANES), "sum_small").reshape(-1)
    offs = [sum(sizes[:i]) for i in range(len(sizes))]
    grads = {}
    for n, o, sz in zip(SMALL, offs, sizes):
        grads[n] = summed[o:o + sz].reshape(small[n].shape)
    loss_total = summed[offs[-1]]
    cw = 2 * D_FF // N_CHIPS
    grads["conv_w"] = lax.dynamic_slice(grads["conv_w"], (0, chip * cw), (3, cw))

    grads.update(ex.reduced())

    out_g, out_d, out_m, out_v = [], [], [], []
    for n in WEIGHTS:
        d2, m2, v2 = _adamw(shard2d[n], grads[n], m[n].reshape(shard2d[n].shape), v[n].reshape(shard2d[n].shape),
                            f"adamw_{n}")
        shape = w[n].shape
        out_g.append(grads[n].reshape(shape))
        out_d.append(d2.reshape(shape))
        out_m.append(m2.reshape(shape))
        out_v.append(v2.reshape(shape))
    return (loss_total, grad_x[None], *out_g, *out_d, *out_m, *out_v)
```

```python
import functools
import math

import jax
import jax.numpy as jnp
from jax import lax
from jax.experimental import pallas as pl
from jax.experimental.pallas import tpu as pltpu

F32 = jnp.float32
BF16 = jnp.bfloat16
MESH = pl.DeviceIdType.MESH

D_MODEL = 1024
N_GROUPS = 3
DILATIONS = (1, 4, 16)
HEADS = 8
HEAD_DIM = 64
GROUP_W = HEADS * HEAD_DIM
QKV_W = N_GROUPS * 3 * GROUP_W
BLK = 128
NEG_INF = -1e30
NUM_BUCKETS = 32
MAX_EXACT = 16
MAX_DISTANCE = 2048
HG_HEADS = 4
HG_DK = 128
HG_W = HG_HEADS * HG_DK
HG_CHUNK = 32
HG_TILE = 256
IN_W = QKV_W + 4 * HG_W + 2 * D_MODEL
D_FF = 2816
EPS = 1e-6
N_CHIPS = 4
N_DEV = 8
LANES = 128

ADAM_LR, ADAM_B1, ADAM_B2, ADAM_EPS, ADAM_WD, ADAM_STEP = 0.001, 0.9, 0.999, 1e-08, 0.01, 10

VMEM_LIMIT = 56 * 1024 * 1024


def _cp(n_axes):
    return pltpu.CompilerParams(dimension_semantics=("arbitrary",) * n_axes, vmem_limit_bytes=VMEM_LIMIT)


def _sds(shape, dtype):
    return jax.ShapeDtypeStruct(tuple(shape), dtype)


def _sigmoid(v):
    return 1.0 / (1.0 + jnp.exp(-v))


def _bf(v):
    return v.astype(BF16)


def _dot(a, b, dims):
    return lax.dot_general(a, b, (dims, ((), ())), preferred_element_type=F32)


NN = ((1,), (0,))
NT = ((1,), (1,))
TN = ((0,), (0,))

ANY = pl.BlockSpec(memory_space=pl.ANY)


class _Plan:
    def __init__(self, copies, n_sems, ins=(), inouts=(), outs=()):
        self.copies, self.n_sems = copies, n_sems
        self.ins, self.inouts, self.outs = list(ins), list(inouts), list(outs)


def _call(body, plans=None, *, name, grid, in_specs, out_specs, out_shape, args, scratch_shapes=()):
    plans = list(plans or ())
    in_specs, out_specs, out_shape = list(in_specs), list(out_specs), list(out_shape)
    scratch_shapes = list(scratch_shapes)
    n_in, n_out, n_scr = len(in_specs), len(out_specs), len(scratch_shapes)
    x_in, x_out, aliases, spans = [], [], {}, []
    for p in plans:
        i0, o0 = len(x_in), len(x_out)
        x_in += p.ins
        for a in p.inouts:
            aliases[n_in + len(x_in)] = n_out + len(x_out)
            x_in.append(a)
            x_out.append(_sds(a.shape, a.dtype))
        x_out += p.outs
        spans.append((i0, len(p.ins), o0, len(p.inouts), len(p.outs)))
    sems = [pltpu.SemaphoreType.DMA((p.n_sems,)) for p in plans for _ in range(3)]

    def wrapped(*refs):
        xi = refs[n_in:n_in + len(x_in)]
        base = n_in + len(x_in)
        xo = refs[base + n_out:base + n_out + len(x_out)]
        sbase = base + n_out + len(x_out)
        xs = refs[sbase + n_scr:]
        ids = [pl.program_id(k) for k in range(len(grid))]
        first = functools.reduce(jnp.logical_and, [i == 0 for i in ids])
        last = functools.reduce(jnp.logical_and, [i == g - 1 for i, g in zip(ids, grid)])

        def descriptors(k):
            i0, ni, o0, nio, no = spans[k]
            return plans[k].copies(xi[i0:i0 + ni], xo[o0:o0 + nio], xo[o0 + nio:o0 + nio + no], *xs[3 * k:3 * k + 3])

        @pl.when(first)
        def _():
            for k in range(len(plans)):
                sends, _, local = descriptors(k)
                for cp in (*sends, *local):
                    cp.start()

        body(*refs[:n_in], *refs[base:base + n_out], *refs[sbase:sbase + n_scr])

        @pl.when(last)
        def _():
            for k in range(len(plans)):
                sends, recvs, local = descriptors(k)
                for cp in recvs:
                    cp.wait_recv()
                for cp in sends:
                    cp.wait_send()
                for cp in local:
                    cp.wait()

    res = pl.pallas_call(
        wrapped if plans else body, name=name, grid=grid, in_specs=in_specs + [ANY] * len(x_in),
        out_specs=out_specs + [ANY] * len(x_out), out_shape=out_shape + x_out, input_output_aliases=aliases,
        scratch_shapes=scratch_shapes + sems, compiler_params=_cp(len(grid)))(*args, *x_in)
    res = list(res)
    carried = [res[n_out + o0:n_out + o0 + nio + no] for (_, _, o0, nio, no) in spans]
    return res[:n_out], carried


def _mm_nn_blk(a, wg, name, tm=512, plans=None):
    M, K = a.shape
    nb, _, Nb = wg.shape

    def body(a_ref, w_ref, o_ref):
        o_ref[...] = _dot(_bf(a_ref[...]), w_ref[...], NN)

    (out,), carried = _call(
        body, plans, name=name, grid=(nb, M // tm),
        in_specs=[pl.BlockSpec((tm, K), lambda j, i: (i, 0)), pl.BlockSpec((None, K, Nb), lambda j, i: (j, 0, 0))],
        out_specs=[pl.BlockSpec((tm, Nb), lambda j, i: (i, j))],
        out_shape=[_sds((M, nb * Nb), F32)], args=(a, wg))
    return out if plans is None else (out, carried)


def _mm_nt_blk(dy, wg, name, tm=1024, plans=None):
    M = dy.shape[0]
    nb, K, Nb = wg.shape

    def body(dy_ref, w_ref, o_ref):
        j = pl.program_id(1)
        r = _dot(_bf(dy_ref[...]), w_ref[...], NT)

        @pl.when(j == 0)
        def _():
            o_ref[...] = r

        @pl.when(j > 0)
        def _():
            o_ref[...] += r

    (out,), carried = _call(
        body, plans, name=name, grid=(M // tm, nb),
        in_specs=[pl.BlockSpec((tm, Nb), lambda i, j: (i, j)), pl.BlockSpec((None, K, Nb), lambda i, j: (j, 0, 0))],
        out_specs=[pl.BlockSpec((tm, K), lambda i, j: (i, 0))],
        out_shape=[_sds((M, K), F32)], args=(dy, wg))
    return out if plans is None else (out, carried)


def _mm_tn_blk(x, dy, nb, name, tk=2048, x_cols=None, plans=None, together=False):
    T, Mx = x.shape
    xk, Mx = (0, Mx) if x_cols is None else x_cols
    Nb = dy.shape[1] // nb
    nj = nb if together else 1

    def body(x_ref, dy_ref, o_ref):
        t = pl.program_id(1)
        r = _dot(_bf(x_ref[...]), _bf(dy_ref[...]), TN)
        for j in range(nj):
            rj = r[:, j * Nb:(j + 1) * Nb]

            @pl.when(t == 0)
            def _():
                o_ref[j] = rj

            @pl.when(t > 0)
            def _():
                o_ref[j] += rj

    (out,), carried = _call(
        body, plans, name=name, grid=(nb // nj, T // tk),
        in_specs=[pl.BlockSpec((tk, Mx), lambda j, t: (t, xk)), pl.BlockSpec((tk, nj * Nb), lambda j, t: (t, j))],
        out_specs=[pl.BlockSpec((nj, Mx, Nb), lambda j, t: (j, 0, 0))],
        out_shape=[_sds((nb, Mx, Nb), F32)], args=(x, dy))
    return out if plans is None else (out, carried)


def _mm_nt(dy, w, name, tm=512):
    M, N = dy.shape
    K = w.shape[0]

    def body(dy_ref, w_ref, o_ref):
        o_ref[...] = _dot(_bf(dy_ref[...]), w_ref[...], NT)

    return pl.pallas_call(
        body, name=name, grid=(M // tm,),
        in_specs=[pl.BlockSpec((tm, N), lambda i: (i, 0)), pl.BlockSpec((K, N), lambda i: (0, 0))],
        out_specs=pl.BlockSpec((tm, K), lambda i: (i, 0)),
        out_shape=_sds((M, K), F32), compiler_params=_cp(1))(dy, w)


def _mm_tn(x, dy, name, tk=1024):
    T, Mx = x.shape
    N = dy.shape[1]

    def body(x_ref, dy_ref, o_ref):
        t = pl.program_id(0)
        r = _dot(_bf(x_ref[...]), _bf(dy_ref[...]), TN)

        @pl.when(t == 0)
        def _():
            o_ref[...] = r

        @pl.when(t > 0)
        def _():
            o_ref[...] += r

    return pl.pallas_call(
        body, name=name, grid=(T // tk,),
        in_specs=[pl.BlockSpec((tk, Mx), lambda t: (t, 0)), pl.BlockSpec((tk, N), lambda t: (t, 0))],
        out_specs=pl.BlockSpec((Mx, N), lambda t: (0, 0)),
        out_shape=_sds((Mx, N), F32), compiler_params=_cp(1))(x, dy)


def _tile(arr, bw, col=lambda c: 0):
    return ("tile", arr, bw, col)


def _full(arr):
    return ("full", arr)


def _out_tile(width, dtype, bw, col=lambda c: 0):
    return ("tile", width, dtype, bw, col)


def _out_acc(rows, width, bw, col=lambda c: 0):
    return ("acc", rows, width, bw, col)


def _rows_call(name, body, n_rows, tm, ncol, ins, outs, plans=None):
    in_specs, args = [], []
    for e in ins:
        if e[0] == "tile":
            _, arr, bw, col = e
            in_specs.append(pl.BlockSpec((tm, bw), functools.partial(lambda c, i, col: (i, col(c)), col=col)))
        else:
            arr = e[1]
            in_specs.append(pl.BlockSpec(arr.shape, functools.partial(lambda c, i, nd: (0,) * nd, nd=arr.ndim)))
        args.append(arr)
    out_specs, out_shape = [], []
    for e in outs:
        if e[0] == "tile":
            _, width, dtype, bw, col = e
            out_specs.append(pl.BlockSpec((tm, bw), functools.partial(lambda c, i, col: (i, col(c)), col=col)))
            out_shape.append(_sds((n_rows, width), dtype))
        else:
            _, rows, width, bw, col = e
            out_specs.append(pl.BlockSpec((rows, bw), functools.partial(lambda c, i, col: (0, col(c)), col=col)))
            out_shape.append(_sds((rows, width), F32))
    out, carried = _call(body, plans, name=name, grid=(ncol, n_rows // tm), in_specs=in_specs, out_specs=out_specs,
                         out_shape=out_shape, args=args)
    return out if plans is None else (out, carried)


def _acc(ref, val):
    i = pl.program_id(1)

    @pl.when(i == 0)
    def _():
        ref[...] = val

    @pl.when(i > 0)
    def _():
        ref[...] += val


def _rinv(z):
    return lax.rsqrt(jnp.mean(z * z, axis=-1, keepdims=True) + EPS)


def _norm_bwd(dy, zhat, r, w):
    dyw = dy * w
    return r * (dyw - zhat * jnp.mean(dyw * zhat, axis=-1, keepdims=True))


def _norm_fwd(x, w, name):
    def body(x_ref, w_ref, h_ref):
        xv = x_ref[...]
        h_ref[...] = _bf(xv * _rinv(xv) * w_ref[...])

    return _rows_call(name, body, x.shape[0], 512, 1, [_tile(x, D_MODEL), _full(w)],
                      [_out_tile(D_MODEL, BF16, D_MODEL)])[0]


def _prenorm_bwd(dh, xin, w, dres, name, plans=None):
    def body(dh_ref, x_ref, w_ref, dres_ref, dx_ref, dw_ref):
        xv = x_ref[...]
        r = _rinv(xv)
        xhat = xv * r
        dhv = dh_ref[...]
        dx_ref[...] = dres_ref[...] + _norm_bwd(dhv, xhat, r, w_ref[...])
        _acc(dw_ref, jnp.sum(dhv * xhat, axis=0, keepdims=True))

    return _rows_call(name, body, xin.shape[0], 512, 1,
                      [_tile(dh, D_MODEL), _tile(xin, D_MODEL), _full(w), _tile(dres, D_MODEL)],
                      [_out_tile(D_MODEL, F32, D_MODEL), _out_acc(1, D_MODEL, D_MODEL)], plans)


def _postnorm_bwd(dout, z, w, w_mat, name):
    def body(do_ref, z_ref, w_ref, wm_ref, dz_ref, dm_ref, dw_ref):
        zv = z_ref[...]
        r = _rinv(zv)
        zhat = zv * r
        dov = do_ref[...]
        dz = _bf(_norm_bwd(dov, zhat, r, w_ref[...]))
        dz_ref[...] = dz
        dm_ref[...] = _dot(dz, wm_ref[...], NT)
        _acc(dw_ref, jnp.sum(dov * zhat, axis=0, keepdims=True))

    return _rows_call(name, body, z.shape[0], 512, 1,
                      [_tile(dout, D_MODEL), _tile(z, D_MODEL), _full(w), _full(w_mat)],
                      [_out_tile(D_MODEL, BF16, D_MODEL), _out_tile(D_MODEL, F32, D_MODEL),
                       _out_acc(1, D_MODEL, D_MODEL)])


def _t5_bucket(dist):
    n = jnp.maximum(dist, 0)
    nf = jnp.maximum(n, 1).astype(F32)
    large = MAX_EXACT + (jnp.log(nf / MAX_EXACT) / math.log(MAX_DISTANCE / MAX_EXACT)
                         * (NUM_BUCKETS - MAX_EXACT)).astype(jnp.int32)
    large = jnp.minimum(large, NUM_BUCKETS - 1)
    return jnp.where(n < MAX_EXACT, n, large)


def _band_rel():
    return jnp.arange(BLK)[:, None] + BLK - jnp.arange(2 * BLK)[None, :]


def _band_valid():
    rel = _band_rel()
    window = (rel >= 0) & (rel <= BLK)
    first = window & (jnp.arange(2 * BLK)[None, :] >= BLK)
    return jnp.stack([first, window]).astype(F32).reshape(2, 1, BAND)


RES_UNROLL = 4


def _heads_per_step(d):
    return HEADS if d == 1 else LANES // HEAD_DIM


def _sub_rows(r, d):
    return pl.ds(r, BLK, stride=d) if d > 1 else pl.ds(0, BLK)


def _for_residues(d, fn):
    if d <= RES_UNROLL:
        for r in range(d):
            fn(r)
    else:
        def group(i, carry):
            for k in range(RES_UNROLL):
                fn(i * RES_UNROLL + k)
            return carry

        lax.fori_loop(0, d // RES_UNROLL, group, 0)


def _attn_specs(d, g, qblock):
    cw = _heads_per_step(d) * HEAD_DIM

    def col(part, hp):
        return (g * 3 + part) * (GROUP_W // cw) + hp

    def cur(part):
        return pl.BlockSpec((d * BLK, cw), lambda hp, n: (qblock(n), col(part, hp)))

    def prev(part):
        return pl.BlockSpec((d * BLK, cw), lambda hp, n: (jnp.maximum(qblock(n) - 1, 0), col(part, hp)))

    return cur, prev


def _attn_fwd(proj, bias, g, name, plans=None):
    S = proj.shape[0]
    d = DILATIONS[g]
    NB = S // (d * BLK)
    hps = _heads_per_step(d)

    def body(q_ref, kp_ref, kc_ref, vp_ref, vc_ref, b_ref, o_ref, lse_ref):
        hp = pl.program_id(0)
        later = jnp.minimum(pl.program_id(1), 1)

        def residue(r):
            rows = _sub_rows(r, d)
            q2 = q_ref[rows, :]
            k2 = jnp.concatenate([kp_ref[rows, :], kc_ref[rows, :]], axis=0)
            v2 = jnp.concatenate([vp_ref[rows, :], vc_ref[rows, :]], axis=0)
            outs, lses = [], []
            for hh in range(hps):
                hs = slice(hh * HEAD_DIM, (hh + 1) * HEAD_DIM)
                s = _dot(_bf(q2[:, hs]), _bf(k2[:, hs]), NT) * (HEAD_DIM ** -0.5) + b_ref[later, hp * hps + hh]
                m = jnp.max(s, axis=-1, keepdims=True)
                p = jnp.exp(s - m)
                l = jnp.sum(p, axis=-1, keepdims=True)
                outs.append(_dot(_bf(p), _bf(v2[:, hs]), NN) / l)
                lses.append(jnp.broadcast_to(m + jnp.log(l), (BLK, HEAD_DIM)))
            o_ref[rows, :] = jnp.concatenate(outs, axis=1)
            lse_ref[rows, :] = jnp.concatenate(lses, axis=1)

        _for_residues(d, residue)

    cur, prev = _attn_specs(d, g, lambda n: n)
    out = pl.BlockSpec((d * BLK, hps * HEAD_DIM), lambda hp, n: (n, hp))
    res, carried = _call(
        body, plans, name=name, grid=(HEADS // hps, NB),
        in_specs=[cur(0), prev(1), cur(1), prev(2), cur(2),
                  pl.BlockSpec((2, HEADS, BLK, 2 * BLK), lambda hp, n: (0, 0, 0, 0))],
        out_specs=[out, out], out_shape=[_sds((S, GROUP_W), F32)] * 2,
        args=(proj, proj, proj, proj, proj, bias))
    return res if plans is None else (res, carried)


def _attn_merge(os_, lses, name):
    def body(o0, o1, o2, l0, l1, l2, y_ref, lse_ref):
        a, b, c = l0[...], l1[...], l2[...]
        m = jnp.maximum(jnp.maximum(a, b), c)
        ea, eb, ec = jnp.exp(a - m), jnp.exp(b - m), jnp.exp(c - m)
        den = ea + eb + ec
        y_ref[...] = (ea * o0[...] + eb * o1[...] + ec * o2[...]) / den
        lse_ref[...] = m + jnp.log(den)

    S = os_[0].shape[0]
    return _rows_call(name, body, S, 512, 1, [_tile(t, GROUP_W) for t in (*os_, *lses)],
                      [_out_tile(GROUP_W, F32, GROUP_W)] * 2)


def _attn_bwd(proj, bias, lse, y, dy, g, name, plans=None):
    S = proj.shape[0]
    d = DILATIONS[g]
    NB = S // (d * BLK)
    hps = _heads_per_step(d)

    def body(q_ref, kp_ref, kc_ref, vp_ref, vc_ref, b_ref, l_ref, y_ref, dy_ref,
             dq_ref, dk_ref, dv_ref, db_ref, ck_ref, cv_ref):
        hp, n = pl.program_id(0), pl.program_id(1)

        @pl.when((hp == 0) & (n == 0))
        def _():
            db_ref[...] = jnp.zeros_like(db_ref)

        @pl.when(n == 0)
        def _():
            ck_ref[...] = jnp.zeros_like(ck_ref)
            cv_ref[...] = jnp.zeros_like(cv_ref)

        @pl.when(n < NB)
        def _():
            later = jnp.minimum(n, 1)

            def residue(r):
                rows = _sub_rows(r, d)
                q2 = q_ref[rows, :]
                k2 = jnp.concatenate([kp_ref[rows, :], kc_ref[rows, :]], axis=0)
                v2 = jnp.concatenate([vp_ref[rows, :], vc_ref[rows, :]], axis=0)
                l2, y2, dy2 = l_ref[rows, :], y_ref[rows, :], dy_ref[rows, :]
                dqs, dks, dvs = [], [], []
                for hh in range(hps):
                    hs = slice(hh * HEAD_DIM, (hh + 1) * HEAD_DIM)
                    q, kb, vb = _bf(q2[:, hs]), _bf(k2[:, hs]), _bf(v2[:, hs])
                    s = _dot(q, kb, NT) * (HEAD_DIM ** -0.5) + b_ref[later, hp * hps + hh]
                    p = jnp.exp(s - l2[:, hh * HEAD_DIM:hh * HEAD_DIM + 1])
                    dyh = dy2[:, hs]
                    delta = jnp.sum(dyh * y2[:, hs], axis=-1, keepdims=True)
                    dyb = _bf(dyh)
                    ds = p * (_dot(dyb, vb, NT) - delta)
                    db_ref[hp * hps + hh] += ds
                    dsb = _bf(ds * (HEAD_DIM ** -0.5))
                    dqs.append(_dot(dsb, kb, NN))
                    dks.append(_dot(dsb, q, TN))
                    dvs.append(_dot(_bf(p), dyb, TN))
                dkb = jnp.concatenate(dks, axis=1)
                dvb = jnp.concatenate(dvs, axis=1)
                dq_ref[rows, :] = jnp.concatenate(dqs, axis=1)
                dk_ref[rows, :] = ck_ref[rows, :] + dkb[:BLK]
                dv_ref[rows, :] = cv_ref[rows, :] + dvb[:BLK]
                ck_ref[rows, :] = dkb[BLK:]
                cv_ref[rows, :] = dvb[BLK:]

            _for_residues(d, residue)

        @pl.when(n == NB)
        def _():
            dk_ref[...] = ck_ref[...]
            dv_ref[...] = cv_ref[...]

    def qn(n):
        return jnp.minimum(n, NB - 1)

    cur, prev = _attn_specs(d, g, qn)
    cw = hps * HEAD_DIM
    row = pl.BlockSpec((d * BLK, cw), lambda hp, n: (qn(n), hp))
    done = pl.BlockSpec((d * BLK, cw), lambda hp, n: (jnp.maximum(n - 1, 0), hp))
    (dq, dk, dv, db), carried = _call(
        body, plans, name=name, grid=(HEADS // hps, NB + 1),
        in_specs=[cur(0), prev(1), cur(1), prev(2), cur(2),
                  pl.BlockSpec((2, HEADS, BLK, 2 * BLK), lambda hp, n: (0, 0, 0, 0)), row, row, row],
        out_specs=[row, done, done, pl.BlockSpec((HEADS, BLK, 2 * BLK), lambda hp, n: (0, 0, 0))],
        out_shape=[_sds((S, GROUP_W), F32)] * 3 + [_sds((HEADS, BLK, 2 * BLK), F32)],
        scratch_shapes=[pltpu.VMEM((d * BLK, cw), F32)] * 2,
        args=(proj, proj, proj, proj, proj, bias, lse, y, dy))
    return ([dq, dk, dv], db) if plans is None else ([dq, dk, dv], db, carried)


BAND = BLK * 2 * BLK


def _bucket_onehot():
    buckets = jnp.stack([_t5_bucket(_band_rel() * d) for d in DILATIONS]).reshape(N_GROUPS, 1, BAND)
    return (buckets == jnp.arange(NUM_BUCKETS).reshape(1, NUM_BUCKETS, 1)).astype(F32)


def _relbias_fwd(rel_bias, name):
    table = rel_bias.reshape(NUM_BUCKETS, N_GROUPS, HEADS).transpose(1, 0, 2)

    def body(t_ref, oh_ref, valid_ref, o_ref):
        bias = lax.dot_general(t_ref[...], oh_ref[...], (TN, ((), ())), preferred_element_type=F32,
                               precision=lax.Precision.HIGHEST)
        for k in range(2):
            o_ref[k] = jnp.where(valid_ref[k] > 0.5, bias, NEG_INF)

    out = pl.pallas_call(
        body, name=name, grid=(N_GROUPS,),
        in_specs=[pl.BlockSpec((None, NUM_BUCKETS, HEADS), lambda g: (g, 0, 0)),
                  pl.BlockSpec((None, NUM_BUCKETS, BAND), lambda g: (g, 0, 0)),
                  pl.BlockSpec((2, 1, BAND), lambda g: (0, 0, 0))],
        out_specs=pl.BlockSpec((None, 2, HEADS, BAND), lambda g: (g, 0, 0, 0)),
        out_shape=_sds((N_GROUPS, 2, HEADS, BAND), F32), compiler_params=_cp(1))(table, _bucket_onehot(), _band_valid())
    return out.reshape(N_GROUPS, 2, HEADS, BLK, 2 * BLK)


def _relbias_bwd(dbs, name):
    band = BAND
    onehot = _bucket_onehot()
    dbf = jnp.stack([db.reshape(HEADS, band) for db in dbs])

    def body(oh_ref, db_ref, o_ref):
        o_ref[...] = lax.dot_general(oh_ref[...], db_ref[...], (NT, ((), ())), preferred_element_type=F32,
                                     precision=lax.Precision.HIGHEST)

    out = pl.pallas_call(
        body, name=name, grid=(N_GROUPS,),
        in_specs=[pl.BlockSpec((None, NUM_BUCKETS, band), lambda g: (g, 0, 0)),
                  pl.BlockSpec((None, HEADS, band), lambda g: (g, 0, 0))],
        out_specs=pl.BlockSpec((None, NUM_BUCKETS, HEADS), lambda g: (g, 0, 0)),
        out_shape=_sds((N_GROUPS, NUM_BUCKETS, HEADS), F32), compiler_params=_cp(1))(onehot, dbf)
    return out.transpose(1, 0, 2).reshape(NUM_BUCKETS, N_GROUPS * HEADS)


def _chunk_pos(shape):
    return lax.broadcasted_iota(jnp.int32, shape, 0) % HG_CHUNK


def _chunk_cumsum(v):
    pos = _chunk_pos(v.shape)
    s = 1
    while s < HG_CHUNK:
        v = v + jnp.where(pos >= s, pltpu.roll(v, s, 0), 0.0)
        s *= 2
    return v


def _chunk_rev_cumsum(v):
    pos = _chunk_pos(v.shape)
    n = v.shape[0]
    s = 1
    while s < HG_CHUNK:
        v = v + jnp.where(pos < HG_CHUNK - s, pltpu.roll(v, n - s, 0), 0.0)
        s *= 2
    return v


def _lower_bound(raw):
    a0, a1 = raw[0:1], raw[1:2]
    m = jnp.maximum(a0, a1)
    e0, e1 = jnp.exp(a0 - m), jnp.exp(a1 - m)
    return e0 / (e0 + e1)


def _hg_gates(qr, fr, lb):
    sf = _sigmoid(fr)
    f = lb + (1.0 - lb) * sf
    sq = _sigmoid(qr)
    return qr * sq, sq, f, sf


HG_COL0 = QKV_W // HG_W


def _hgrn_fwd(proj, lb_raw, nw, name):
    S = proj.shape[0]
    ncs = HG_TILE // HG_CHUNK
    tril = jnp.tril(jnp.ones((HG_CHUNK, HG_CHUNK), dtype=bool))

    def body(q_ref, f_ref, i_ref, og_ref, lb_ref, nw_ref, y_ref, o_ref, st_ref, state):
        @pl.when(pl.program_id(0) == 0)
        def _():
            state[...] = jnp.zeros_like(state)

        lb = _lower_bound(lb_ref[...])
        q, _, f, _ = _hg_gates(q_ref[...], f_ref[...], lb)
        k = 1.0 - f
        G = _chunk_cumsum(jnp.log(f))
        row = lax.broadcasted_iota(jnp.int32, (HG_CHUNK, HG_CHUNK), 0)
        col = lax.broadcasted_iota(jnp.int32, (HG_CHUNK, HG_CHUNK), 1)
        heads = [slice(h * HG_DK, (h + 1) * HG_DK) for h in range(HG_HEADS)]
        sts = [state[h] for h in range(HG_HEADS)]
        for c in range(ncs):
            cs = slice(c * HG_CHUNK, (c + 1) * HG_CHUNK)
            for h, hs in enumerate(heads):
                Gc = G[cs, hs]
                gl = Gc[HG_CHUNK - 1:HG_CHUNK]
                qt = _bf(q[cs, hs] * jnp.exp(Gc))
                kt = _bf(k[cs, hs] * jnp.exp(-Gc))
                kd = _bf(k[cs, hs] * jnp.exp(gl - Gc))
                v = _bf(i_ref[cs, hs])
                A = jnp.where(row >= col, _dot(qt, kt, NT), 0.0)
                o_ref[cs, hs] = _dot(_bf(A), v, NN) + _dot(qt, _bf(sts[h]), NT)
                st_ref[c, h] = sts[h]
                sts[h] = sts[h] * jnp.exp(gl) + _dot(v, kd, TN)
        for h, hs in enumerate(heads):
            state[h] = sts[h]
            oh = o_ref[:, hs]
            og = og_ref[:, hs]
            y_ref[:, hs] = oh * _rinv(oh) * nw_ref[...] * (og * _sigmoid(og))

    def colspec(j):
        return pl.BlockSpec((HG_TILE, HG_W), lambda i: (i, HG_COL0 + j))

    return pl.pallas_call(
        body, name=name, grid=(S // HG_TILE,),
        in_specs=[colspec(0), colspec(1), colspec(2), colspec(3),
                  pl.BlockSpec((2, HG_W), lambda i: (0, 0)), pl.BlockSpec((1, HG_DK), lambda i: (0, 0))],
        out_specs=[pl.BlockSpec((HG_TILE, HG_W), lambda i: (i, 0))] * 2
        + [pl.BlockSpec((ncs, HG_HEADS, HG_DK, HG_DK), lambda i: (i, 0, 0, 0))],
        out_shape=[_sds((S, HG_W), F32)] * 2 + [_sds((S // HG_CHUNK, HG_HEADS, HG_DK, HG_DK), F32)],
        scratch_shapes=[pltpu.VMEM((HG_HEADS, HG_DK, HG_DK), F32)],
        compiler_params=_cp(1))(proj, proj, proj, proj, lb_raw, nw)


def _hgrn_bwd(proj, lb_raw, nw, o, states, dy, name):
    S = proj.shape[0]
    ncs = HG_TILE // HG_CHUNK
    nt = S // HG_TILE

    def body(q_ref, f_ref, i_ref, og_ref, lb_ref, nw_ref, o_ref, st_ref, dy_ref,
             dq_ref, df_ref, di_ref, dog_ref, dlb_ref, dnw_ref, dstate, do_s, dG_s, dgl_s, dk_s, dlb_s):
        step = pl.program_id(0)

        @pl.when(step == 0)
        def _():
            dstate[...] = jnp.zeros_like(dstate)
            dlb_s[...] = jnp.zeros_like(dlb_s)
            dnw_ref[...] = jnp.zeros_like(dnw_ref)

        lb = _lower_bound(lb_ref[...])
        qr = q_ref[...]
        q, sq, f, sf = _hg_gates(qr, f_ref[...], lb)
        k = 1.0 - f
        G = _chunk_cumsum(jnp.log(f))
        nwv = nw_ref[...]
        row = lax.broadcasted_iota(jnp.int32, (HG_CHUNK, HG_CHUNK), 0)
        col = lax.broadcasted_iota(jnp.int32, (HG_CHUNK, HG_CHUNK), 1)
        for h in range(HG_HEADS):
            hs = slice(h * HG_DK, (h + 1) * HG_DK)
            oh = o_ref[:, hs]
            r = _rinv(oh)
            ohat = oh * r
            og = og_ref[:, hs]
            sg = _sigmoid(og)
            dyh = dy_ref[:, hs]
            don = dyh * (og * sg)
            dog_ref[:, hs] = _bf(dyh * (ohat * nwv) * (sg * (1.0 + og * (1.0 - sg))))
            dnw_ref[...] += jnp.sum(don * ohat, axis=0, keepdims=True)
            do_s[:, hs] = _norm_bwd(don, ohat, r, nwv)
        dsts = [dstate[h] for h in range(HG_HEADS)]
        for c in reversed(range(ncs)):
            cs = slice(c * HG_CHUNK, (c + 1) * HG_CHUNK)
            for h in range(HG_HEADS):
                hs = slice(h * HG_DK, (h + 1) * HG_DK)
                dst = dsts[h]
                Gc = G[cs, hs]
                gl = Gc[HG_CHUNK - 1:HG_CHUNK]
                eG, enG, edG, egl = jnp.exp(Gc), jnp.exp(-Gc), jnp.exp(gl - Gc), jnp.exp(gl)
                qt, kt, kd = q[cs, hs] * eG, k[cs, hs] * enG, k[cs, hs] * edG
                qtb, ktb, kdb = _bf(qt), _bf(kt), _bf(kd)
                v = _bf(i_ref[cs, hs])
                do = _bf(do_s[cs, hs])
                st = st_ref[c, h]
                dstb = _bf(dst)
                A = jnp.where(row >= col, _dot(qtb, ktb, NT), 0.0)
                dA = _bf(jnp.where(row >= col, _dot(do, v, NT), 0.0))
                di_ref[cs, hs] = _bf(_dot(_bf(A), do, TN) + _dot(kdb, dstb, NT))
                dqt = _dot(dA, ktb, NN) + _dot(do, _bf(st), NN)
                dkt = _dot(dA, qtb, TN)
                dkd = _dot(v, dstb, NN)
                dgl = egl * jnp.sum(st * dst, axis=0, keepdims=True) + jnp.sum(dkd * kd, axis=0, keepdims=True)
                dsts[h] = dst * egl + _dot(do, qtb, TN)
                dq_ref[cs, hs] = _bf(dqt * eG * (sq[cs, hs] * (1.0 + qr[cs, hs] * (1.0 - sq[cs, hs]))))
                dk_s[cs, hs] = dkt * enG + dkd * edG
                dG_s[cs, hs] = dqt * qt - dkt * kt - dkd * kd
                dgl_s[cs, hs] = jnp.broadcast_to(dgl, (HG_CHUNK, HG_DK))
        for h in range(HG_HEADS):
            dstate[h] = dsts[h]
        dg = _chunk_rev_cumsum(dG_s[...]) + dgl_s[...]
        dfv = dg / f - dk_s[...]
        df_ref[...] = _bf(dfv * (1.0 - lb) * sf * (1.0 - sf))
        dlb_s[...] += jnp.sum(dfv * (1.0 - sf), axis=0, keepdims=True)

        @pl.when(step == nt - 1)
        def _():
            t = dlb_s[...] * lb * (1.0 - lb)
            dlb_ref[...] = jnp.concatenate([t, -t], axis=0)

    def colspec(j):
        return pl.BlockSpec((HG_TILE, HG_W), lambda i: (nt - 1 - i, HG_COL0 + j))

    tile = pl.BlockSpec((HG_TILE, HG_W), lambda i: (nt - 1 - i, 0))
    outs = pl.pallas_call(
        body, name=name, grid=(nt,),
        in_specs=[colspec(0), colspec(1), colspec(2), colspec(3),
                  pl.BlockSpec((2, HG_W), lambda i: (0, 0)), pl.BlockSpec((1, HG_DK), lambda i: (0, 0)),
                  tile, pl.BlockSpec((ncs, HG_HEADS, HG_DK, HG_DK), lambda i: (nt - 1 - i, 0, 0, 0)), tile],
        out_specs=[tile] * 4 + [pl.BlockSpec((2, HG_W), lambda i: (0, 0)), pl.BlockSpec((1, HG_DK), lambda i: (0, 0))],
        out_shape=[_sds((S, HG_W), BF16)] * 4 + [_sds((2, HG_W), F32), _sds((1, HG_DK), F32)],
        scratch_shapes=[pltpu.VMEM((HG_HEADS, HG_DK, HG_DK), F32)] + [pltpu.VMEM((HG_TILE, HG_W), F32)] * 4
        + [pltpu.VMEM((1, HG_W), F32)],
        compiler_params=_cp(1))(proj, proj, proj, proj, lb_raw, nw, o, states, dy)
    return outs[:4], outs[4], outs[5]


GATE_COL0 = (QKV_W + 4 * HG_W) // GROUP_W
HALF_D = D_MODEL // 2


def _gate_tiles(proj):
    return [_tile(proj, HALF_D, functools.partial(lambda c, k: GATE_COL0 + k, k=k)) for k in range(4)]


def _gates(g_refs):
    s0 = _sigmoid(jnp.concatenate([g_refs[0][...], g_refs[1][...]], axis=1))
    s1 = _sigmoid(jnp.concatenate([g_refs[2][...], g_refs[3][...]], axis=1))
    return s0, s1


def _branch_fwd(y, yh, proj, w_a, w_h, name):
    nb = w_a.shape[0]

    def body(y_ref, yh_ref, g0a, g0b, g1a, g1b, wa_ref, wh_ref, za_ref, zh_ref, m_ref):
        yb, yhb = _bf(y_ref[...]), _bf(yh_ref[...])
        za = jnp.concatenate([_dot(yb, wa_ref[j], NN) for j in range(nb)], axis=1)
        zh = jnp.concatenate([_dot(yhb, wh_ref[j], NN) for j in range(nb)], axis=1)
        s0, s1 = _gates((g0a, g0b, g1a, g1b))
        za_ref[...] = za
        zh_ref[...] = zh
        m_ref[...] = _bf(s0 * za + s1 * zh)

    return _rows_call(name, body, y.shape[0], 512, 1,
                      [_tile(y, GROUP_W), _tile(yh, HG_W), *_gate_tiles(proj), _full(w_a), _full(w_h)],
                      [_out_tile(D_MODEL, F32, D_MODEL)] * 2 + [_out_tile(D_MODEL, BF16, D_MODEL)])


def _branch_bwd(dm, za, zh, proj, w_a, w_h, name):
    nb, _, Nb = w_a.shape

    def body(dm_ref, za_ref, zh_ref, g0a, g0b, g1a, g1b, wa_ref, wh_ref,
             dza_ref, dzh_ref, dg0_ref, dg1_ref, dy_ref, dyh_ref):
        dmv = dm_ref[...]
        s0, s1 = _gates((g0a, g0b, g1a, g1b))
        dza, dzh = _bf(dmv * s0), _bf(dmv * s1)
        dza_ref[...] = dza
        dzh_ref[...] = dzh
        dg0_ref[...] = _bf(dmv * za_ref[...] * s0 * (1.0 - s0))
        dg1_ref[...] = _bf(dmv * zh_ref[...] * s1 * (1.0 - s1))
        dy_ref[...] = sum(_dot(dza[:, j * Nb:(j + 1) * Nb], wa_ref[j], NT) for j in range(nb))
        dyh_ref[...] = sum(_dot(dzh[:, j * Nb:(j + 1) * Nb], wh_ref[j], NT) for j in range(nb))

    return _rows_call(name, body, za.shape[0], 512, 1,
                      [_tile(dm, D_MODEL), _tile(za, D_MODEL), _tile(zh, D_MODEL), *_gate_tiles(proj),
                       _full(w_a), _full(w_h)],
                      [_out_tile(D_MODEL, BF16, D_MODEL)] * 4 + [_out_tile(GROUP_W, F32, GROUP_W),
                                                                 _out_tile(HG_W, F32, HG_W)])


def _mix_out(merged, w_out, x, w_post, w_pre, name):
    def body(m_ref, wo_ref, x_ref, wp_ref, wf_ref, mo_ref, x1_ref, h2_ref):
        z = _dot(m_ref[...], wo_ref[...], NN)
        mo_ref[...] = z
        x1 = x_ref[...] + z * _rinv(z) * wp_ref[...]
        x1_ref[...] = x1
        h2_ref[...] = _bf(x1 * _rinv(x1) * wf_ref[...])

    return _rows_call(name, body, x.shape[0], 512, 1,
                      [_tile(merged, D_MODEL), _full(w_out), _tile(x, D_MODEL), _full(w_post), _full(w_pre)],
                      [_out_tile(D_MODEL, F32, D_MODEL), _out_tile(D_MODEL, F32, D_MODEL),
                       _out_tile(D_MODEL, BF16, D_MODEL)])


def _loss_head(a, w_down, x1, tgt, w, name):
    def body(a_ref, wd_ref, x1_ref, t_ref, w_ref, dx_ref, df_ref, dw_ref, loss_ref):
        z = _dot(a_ref[...], wd_ref[...], NN)
        r = _rinv(z)
        zhat = z * r
        wv = w_ref[...]
        e = x1_ref[...] + zhat * wv - t_ref[...]
        dx = e * (1.0 / D_MODEL)
        dx_ref[...] = dx
        df_ref[...] = _bf(_norm_bwd(dx, zhat, r, wv))
        _acc(dw_ref, jnp.sum(dx * zhat, axis=0, keepdims=True))
        part = 0.5 * jnp.sum(jnp.sum(e * e, axis=1, keepdims=True), axis=0, keepdims=True) * (1.0 / D_MODEL)
        _acc(loss_ref, jnp.broadcast_to(part, (1, LANES)))

    return _rows_call(name, body, x1.shape[0], 512, 1,
                      [_tile(a, D_FF), _full(w_down), _tile(x1, D_MODEL), _tile(tgt, D_MODEL), _full(w)],
                      [_out_tile(D_MODEL, F32, D_MODEL), _out_tile(D_MODEL, BF16, D_MODEL),
                       _out_acc(1, D_MODEL, D_MODEL), _out_acc(1, LANES, LANES)])


CONV_CB = D_FF // 2
CONV_TM = 512
HALO = 8
SQRT_HALF = 0.7071067811865476
INV_SQRT_2PI = 0.3989422804014327


CONV_RS = 32


def _lane_tiles():
    return [slice(k * LANES, (k + 1) * LANES) for k in range(CONV_CB // LANES)]


def _strip_start(i):
    return pl.multiple_of(i * CONV_RS, CONV_RS)


def _strip_taps(u_ref, halo_ref, r0, cs, first_strip, first_tile):
    if first_strip:
        before = jnp.where(first_tile, 0.0, halo_ref[:, cs])
        blk = jnp.concatenate([before, u_ref[0:CONV_RS, cs]], axis=0)
    else:
        blk = u_ref[pl.ds(pl.multiple_of(r0 - HALO, HALO), CONV_RS + HALO), cs]
    return pltpu.roll(blk, 2, 0)[HALO:], pltpu.roll(blk, 1, 0)[HALO:], blk[HALO:]


def _conv(taps, w_ref, b_ref, cs):
    return b_ref[:, cs] + w_ref[0:1, cs] * taps[0] + w_ref[1:2, cs] * taps[1] + w_ref[2:3, cs] * taps[2]


def _conv_specs(tm):
    nh = tm // HALO
    nc = D_FF // CONV_CB

    def tile(off):
        return pl.BlockSpec((tm, CONV_CB), lambda c, i: (i, off + c))

    def halo(off):
        return pl.BlockSpec((HALO, CONV_CB), lambda c, i: (jnp.maximum(i * nh - 1, 0), off + c))

    def small(rows, off):
        return pl.BlockSpec((rows, CONV_CB), lambda c, i: (0, off + c))

    return nc, tile, halo, small


def _conv_gelu_fwd(u, cw, cb, name):
    S = u.shape[0]
    tm = CONV_TM
    nc, tile, halo, small = _conv_specs(tm)

    def body(ug, hg, uv, hv, wg, wv, bg, bv, a_ref):
        first_tile = pl.program_id(1) == 0

        def strip(r0, first_strip):
            for cs in _lane_tiles():
                cg = _conv(_strip_taps(ug, hg, r0, cs, first_strip, first_tile), wg, bg, cs)
                cv = _conv(_strip_taps(uv, hv, r0, cs, first_strip, first_tile), wv, bv, cs)
                a_ref[pl.ds(r0, CONV_RS), cs] = _bf(0.5 * cg * (1.0 + lax.erf(cg * SQRT_HALF)) * cv)

        strip(0, True)
        lax.fori_loop(1, tm // CONV_RS, lambda k, c: (strip(_strip_start(k), False), c)[1], 0)

    return pl.pallas_call(
        body, name=name, grid=(nc, S // tm),
        in_specs=[tile(0), halo(0), tile(nc), halo(nc), small(3, 0), small(3, nc), small(1, 0), small(1, nc)],
        out_specs=tile(0), out_shape=_sds((S, D_FF), BF16), compiler_params=_cp(2))(u, u, u, u, cw, cw, cb, cb)


def _conv_gelu_bwd(u, da, cw, cb, name, plans=None):
    S = u.shape[0]
    tm = CONV_TM
    nt = S // tm
    nc, tile, halo, small = _conv_specs(tm)

    def body(ug, hg, uv, hv, wg, wv, bg, bv, da_ref, dcg_ref, dcv_ref, dwg_ref, dwv_ref, dbg_ref, dbv_ref, acc):
        i = pl.program_id(1)
        first_tile = i == 0

        @pl.when(first_tile)
        def _():
            acc[...] = jnp.zeros_like(acc)

        def strip(r0, first_strip):
            rows = pl.ds(r0, CONV_RS)
            for cs in _lane_tiles():
                tg = _strip_taps(ug, hg, r0, cs, first_strip, first_tile)
                tv = _strip_taps(uv, hv, r0, cs, first_strip, first_tile)
                cg = _conv(tg, wg, bg, cs)
                cv = _conv(tv, wv, bv, cs)
                phi = 0.5 * (1.0 + lax.erf(cg * SQRT_HALF))
                dav = da_ref[rows, cs]
                dcg = dav * cv * (phi + cg * jnp.exp(-0.5 * cg * cg) * INV_SQRT_2PI)
                dcv = dav * (cg * phi)
                dcg_ref[rows, cs] = dcg
                dcv_ref[rows, cs] = dcv
                for half, (dc, taps) in enumerate(((dcg, tg), (dcv, tv))):
                    for j in range(3):
                        acc[4 * half + j, :, cs] += dc * taps[j]
                    acc[4 * half + 3, :, cs] += dc

        strip(0, True)
        lax.fori_loop(1, tm // CONV_RS, lambda k, c: (strip(_strip_start(k), False), c)[1], 0)

        @pl.when(i == nt - 1)
        def _():
            for half, (dw_ref, db_ref) in enumerate(((dwg_ref, dbg_ref), (dwv_ref, dbv_ref))):
                for j in range(3):
                    dw_ref[j:j + 1, :] = jnp.sum(acc[4 * half + j], axis=0, keepdims=True)
                db_ref[...] = jnp.sum(acc[4 * half + 3], axis=0, keepdims=True)

    res, carried = _call(
        body, plans, name=name, grid=(nc, nt),
        in_specs=[tile(0), halo(0), tile(nc), halo(nc), small(3, 0), small(3, nc), small(1, 0), small(1, nc), tile(0)],
        out_specs=[tile(0), tile(0), small(3, 0), small(3, 0), small(1, 0), small(1, 0)],
        out_shape=[_sds((S, D_FF), F32)] * 2 + [_sds((3, D_FF), F32)] * 2 + [_sds((1, D_FF), F32)] * 2,
        scratch_shapes=[pltpu.VMEM((8, CONV_RS, CONV_CB), F32)], args=(u, u, u, u, cw, cw, cb, cb, da))
    return res if plans is None else (res, carried)


def _conv_input_bwd(dcg, dcv, cw, name, plans=None):
    S = dcg.shape[0]
    tm = CONV_TM
    nc, tile, _, small = _conv_specs(tm)
    nh = tm // HALO
    nt = S // tm
    n = CONV_RS + HALO

    def nxt(off):
        return pl.BlockSpec((HALO, CONV_CB), lambda c, i: (jnp.minimum((i + 1) * nh, S // HALO - 1), off + c))

    def body(g_ref, ng_ref, v_ref, nv_ref, wg, wv, dug_ref, duv_ref):
        last_tile = pl.program_id(1) == nt - 1

        def strip(r0, last_strip):
            for cs in _lane_tiles():
                for dc_ref, n_ref, w_ref, du_ref in ((g_ref, ng_ref, wg, dug_ref), (v_ref, nv_ref, wv, duv_ref)):
                    if last_strip:
                        after = jnp.where(last_tile, 0.0, n_ref[:, cs])
                        blk = jnp.concatenate([dc_ref[tm - CONV_RS:tm, cs], after], axis=0)
                    else:
                        blk = dc_ref[pl.ds(r0, n), cs]
                    d1 = pltpu.roll(blk, n - 1, 0)[:CONV_RS]
                    d2 = pltpu.roll(blk, n - 2, 0)[:CONV_RS]
                    du_ref[pl.ds(r0, CONV_RS), cs] = _bf(w_ref[2:3, cs] * blk[:CONV_RS] + w_ref[1:2, cs] * d1
                                                         + w_ref[0:1, cs] * d2)

        lax.fori_loop(0, tm // CONV_RS - 1, lambda k, c: (strip(_strip_start(k), False), c)[1], 0)
        strip(tm - CONV_RS, True)

    res, carried = _call(
        body, plans, name=name, grid=(nc, nt),
        in_specs=[tile(0), nxt(0), tile(0), nxt(0), small(3, 0), small(3, nc)],
        out_specs=[tile(0), tile(0)], out_shape=[_sds((S, D_FF), BF16)] * 2,
        args=(dcg, dcg, dcv, dcv, cw, cw))
    return res if plans is None else (res, carried)


def _row_tile(n, cap):
    best = n
    for t in range(16, cap + 1, 16):
        if n % t == 0:
            best = t
    return best if best <= cap else n


def _rows_for_bytes(nbytes, cols):
    return max(16, nbytes // (4 * cols) // 16 * 16)


def _adamw(w, g, m, v, name):
    R, C = w.shape
    tr = _row_tile(R, _rows_for_bytes(2 << 20, C))

    def body(w_ref, g_ref, m_ref, v_ref, d_ref, nm_ref, nv_ref):
        gv = g_ref[...]
        nm = ADAM_B1 * m_ref[...] + (1.0 - ADAM_B1) * gv
        nv = ADAM_B2 * v_ref[...] + (1.0 - ADAM_B2) * (gv * gv)
        m_hat = nm / (1.0 - ADAM_B1 ** ADAM_STEP)
        v_hat = nv / (1.0 - ADAM_B2 ** ADAM_STEP)
        d_ref[...] = -ADAM_LR * (m_hat / (jnp.sqrt(v_hat) + ADAM_EPS) + ADAM_WD * w_ref[...])
        nm_ref[...] = nm
        nv_ref[...] = nv

    spec = pl.BlockSpec((tr, C), lambda i: (i, 0))
    return pl.pallas_call(body, name=name, grid=(R // tr,), in_specs=[spec] * 4, out_specs=[spec] * 3,
                          out_shape=[_sds((R, C), F32)] * 3, compiler_params=_cp(1))(w, g, m, v)


def _pair_sum(gfull, rcv, c_idx, name):
    nb, R, C = gfull.shape
    half = R // 2
    tr = _row_tile(half, _rows_for_bytes(2 << 20, C))
    nt = half // tr

    def body(c_ref, g_ref, r_ref, o_ref):
        o_ref[...] = _bf(g_ref[...] + r_ref[...])

    return pl.pallas_call(
        body, name=name,
        grid_spec=pltpu.PrefetchScalarGridSpec(
            num_scalar_prefetch=1, grid=(nb, nt),
            in_specs=[pl.BlockSpec((None, tr, C), lambda j, i, c_ref: (j, c_ref[0] * nt + i, 0)),
                      pl.BlockSpec((None, tr, C), lambda j, i, c_ref: (j, i, 0))],
            out_specs=pl.BlockSpec((None, tr, C), lambda j, i, c_ref: (j, i, 0))),
        out_shape=_sds((nb, half, C), BF16), compiler_params=_cp(2))(c_idx, gfull, rcv)


def _chip_sum(arrived, own, place, name):
    nb, H, C = arrived.shape
    tr = _row_tile(H, _rows_for_bytes(2 << 20, C))
    nt = H // tr

    def body(pl_ref, *refs):
        o_ref = refs[nb + 1]
        me = pl_ref[0]
        acc = None
        for k in range(nb):
            term = jnp.where(me == k, refs[nb][...], refs[k][...]).astype(F32)
            acc = term if acc is None else acc + term
        o_ref[...] = acc

    def other(k):
        return pl.BlockSpec((None, tr, C), lambda i, p: (jnp.where(p[0] == k, (k + 1) % nb, k), i, 0))

    return pl.pallas_call(
        body, name=name,
        grid_spec=pltpu.PrefetchScalarGridSpec(
            num_scalar_prefetch=1, grid=(nt,),
            in_specs=[other(k) for k in range(nb)] + [pl.BlockSpec((None, tr, C), lambda i, p: (p[0], i, 0))],
            out_specs=pl.BlockSpec((tr, C), lambda i, p: (p[1] * nt + i, 0))),
        out_shape=_sds((2 * H, C), F32), compiler_params=_cp(1))(place, *([arrived] * nb), own)


def _cast_into_slot(shard, place, name):
    R, C = shard.shape
    tr = _row_tile(R, 256)

    def body(pl_ref, s_ref, o_ref):
        o_ref[...] = _bf(s_ref[...])

    return pl.pallas_call(
        body, name=name,
        grid_spec=pltpu.PrefetchScalarGridSpec(
            num_scalar_prefetch=1, grid=(R // tr,),
            in_specs=[pl.BlockSpec((tr, C), lambda i, p: (i, 0))],
            out_specs=pl.BlockSpec((None, tr, C), lambda i, p: (p[0], i, 0))),
        out_shape=_sds((N_CHIPS, R, C), BF16), compiler_params=_cp(1))(place, shard)


def _place():
    x, y, c = lax.axis_index("x"), lax.axis_index("y"), lax.axis_index("c")
    chips = [(1 - x, y), (x, 1 - y), (1 - x, 1 - y)]
    return x, y, c, chips


def _chip_id(px, py):
    return 2 * px + py


def _remote(src, dst, send_sems, recv_sems, k, to):
    return pltpu.make_async_remote_copy(src_ref=src, dst_ref=dst, send_sem=send_sems.at[k], recv_sem=recv_sems.at[k],
                                        device_id=to, device_id_type=MESH)


def _gather_weights(slots, wholes, name):
    ns, nw = len(slots), len(wholes)
    n = ns + nw

    def body(*refs):
        ins = refs[ns:n]
        outs = refs[n:2 * n]
        send_sems, recv_sems, local_sems = refs[2 * n:]
        x, y, c, chips = _place()
        me = _chip_id(x, y)
        sib = (x, y, 1 - c)
        local = [pltpu.make_async_copy(ins[b], outs[ns + b].at[me], local_sems.at[b]) for b in range(nw)]
        for cp in local:
            cp.start()
        sent = []
        for a in range(n):
            R = outs[a].shape[1]
            rows = pl.ds(c * (R // 2), R // 2) if a < ns else pl.ds(0, R)
            src = outs[a].at[me, rows] if a < ns else ins[a - ns]
            for j, chip in enumerate(chips):
                cp = _remote(src, outs[a].at[me, rows], send_sems, recv_sems, 6 * a + j, (*chip, c))
                cp.start()
                sent.append(cp)
        for a in range(n):
            R = outs[a].shape[1]
            rows = pl.ds(c * (R // 2), R // 2) if a < ns else pl.ds(0, R)
            for j, chip in enumerate(chips):
                landed = outs[a].at[_chip_id(*chip), rows]
                _remote(landed, landed, send_sems, recv_sems, 6 * a + j, (*chip, c)).wait_recv()
                if a < ns:
                    cp = _remote(landed, landed, send_sems, recv_sems, 6 * a + 3 + j, sib)
                    cp.start()
                    sent.append(cp)
        for a in range(ns):
            R = outs[a].shape[1]
            other = pl.ds((1 - c) * (R // 2), R // 2)
            for j, chip in enumerate(chips):
                passed = outs[a].at[_chip_id(*chip), other]
                _remote(passed, passed, send_sems, recv_sems, 6 * a + 3 + j, sib).wait_recv()
        for cp in sent:
            cp.wait_send()
        for cp in local:
            cp.wait()

    return pl.pallas_call(
        body, name=name, in_specs=[ANY] * n, out_specs=[ANY] * n,
        out_shape=[_sds(s.shape, s.dtype) for s in slots] + [_sds((N_CHIPS, *s.shape), s.dtype) for s in wholes],
        input_output_aliases={a: a for a in range(ns)},
        scratch_shapes=[pltpu.SemaphoreType.DMA((6 * n,)), pltpu.SemaphoreType.DMA((6 * n,)),
                        pltpu.SemaphoreType.DMA((max(nw, 1),))])(*slots, *wholes)


def _gather_ici_plan(slots, wholes):
    ns, nw = len(slots), len(wholes)

    def copies(ins, ios, outs, send_sems, recv_sems, local_sems):
        x, y, c, chips = _place()
        me = _chip_id(x, y)
        sends, recvs = [], []
        for a in range(ns + nw):
            dst = ios[a] if a < ns else outs[a - ns]
            R = dst.shape[1]
            rows = pl.ds(c * (R // 2), R // 2) if a < ns else pl.ds(0, R)
            src = dst.at[me, rows] if a < ns else ins[a - ns]
            for j, chip in enumerate(chips):
                sends.append(_remote(src, dst.at[me, rows], send_sems, recv_sems, 3 * a + j, (*chip, c)))
                landed = dst.at[_chip_id(*chip), rows]
                recvs.append(_remote(landed, landed, send_sems, recv_sems, 3 * a + j, (*chip, c)))
        local = [pltpu.make_async_copy(ins[b], outs[b].at[me], local_sems.at[b]) for b in range(nw)]
        return sends, recvs, local

    return _Plan(copies, 3 * (ns + nw), ins=wholes, inouts=slots,
                 outs=[_sds((N_CHIPS, *s.shape), s.dtype) for s in wholes])


def _gather_pass_plan(slots):
    def copies(ins, ios, outs, send_sems, recv_sems, local_sems):
        x, y, c, chips = _place()
        sib = (x, y, 1 - c)
        sends, recvs = [], []
        for a, buf in enumerate(ios):
            half = buf.shape[1] // 2
            for j, chip in enumerate(chips):
                mine = buf.at[_chip_id(*chip), pl.ds(c * half, half)]
                other = buf.at[_chip_id(*chip), pl.ds((1 - c) * half, half)]
                sends.append(_remote(mine, mine, send_sems, recv_sems, 3 * a + j, sib))
                recvs.append(_remote(other, other, send_sems, recv_sems, 3 * a + j, sib))
        return sends, recvs, []

    return _Plan(copies, 3 * len(slots), inouts=slots)


def _pair_plan(grads):
    def copies(ins, ios, outs, send_sems, recv_sems, local_sems):
        x, y, c, _ = _place()
        sib = (x, y, 1 - c)
        sends, recvs = [], []
        for a, g in enumerate(ins):
            half = g.shape[1] // 2
            sends.append(_remote(g.at[:, pl.ds((1 - c) * half, half), :], outs[a], send_sems, recv_sems, a, sib))
            recvs.append(_remote(outs[a], outs[a], send_sems, recv_sems, a, sib))
        return sends, recvs, []

    return _Plan(copies, len(grads), ins=grads,
                 outs=[_sds((g.shape[0], g.shape[1] // 2, g.shape[2]), g.dtype) for g in grads])


def _chip_plan(parts):
    def copies(ins, ios, outs, send_sems, recv_sems, local_sems):
        x, y, c, chips = _place()
        me = _chip_id(x, y)
        sends, recvs = [], []
        for a, part in enumerate(ins):
            for j, chip in enumerate(chips):
                sends.append(_remote(part.at[_chip_id(*chip)], outs[a].at[me], send_sems, recv_sems, 3 * a + j, (*chip, c)))
                landed = outs[a].at[_chip_id(*chip)]
                recvs.append(_remote(landed, landed, send_sems, recv_sems, 3 * a + j, (*chip, c)))
        return sends, recvs, []

    return _Plan(copies, 3 * len(parts), ins=parts, outs=[_sds(p.shape, p.dtype) for p in parts])


def _pair_concat(fulls, name):
    n = len(fulls)

    def body(*refs):
        outs = refs[n:2 * n]
        send_sems, recv_sems = refs[2 * n:]
        x, y, c, _ = _place()
        cps = []
        for a in range(n):
            H = outs[a].shape[0] // 2
            mine = outs[a].at[pl.ds(c * H, H)]
            cp = _remote(mine, mine, send_sems, recv_sems, a, (x, y, 1 - c))
            cp.start()
            cps.append(cp)
        for a, cp in enumerate(cps):
            H = outs[a].shape[0] // 2
            other = outs[a].at[pl.ds((1 - c) * H, H)]
            _remote(other, other, send_sems, recv_sems, a, (x, y, 1 - c)).wait_recv()
            cp.wait_send()

    return pl.pallas_call(
        body, name=name, in_specs=[ANY] * n, out_specs=[ANY] * n,
        out_shape=[_sds(f.shape, f.dtype) for f in fulls], input_output_aliases={a: a for a in range(n)},
        scratch_shapes=[pltpu.SemaphoreType.DMA((n,)), pltpu.SemaphoreType.DMA((n,))])(*fulls)


def _all_sum(pack, name):
    R, C = pack.shape

    def body(p_ref, o_ref, buf, send_sems, recv_sems):
        x, y, c, _ = _place()
        me = 4 * x + 2 * y + c
        buf[me] = p_ref[...]
        cps = []
        for k in range(1, N_DEV):
            to = (x ^ (k >> 2), y ^ ((k >> 1) & 1), c ^ (k & 1))
            cp = _remote(p_ref, buf.at[me], send_sems, recv_sems, k - 1, to)
            cp.start()
            cps.append(cp)
        for k in range(1, N_DEV):
            frm = (x ^ (k >> 2), y ^ ((k >> 1) & 1), c ^ (k & 1))
            slot = buf.at[4 * frm[0] + 2 * frm[1] + frm[2]]
            _remote(slot, slot, send_sems, recv_sems, k - 1, frm).wait_recv()
        acc = buf[0]
        for k in range(1, N_DEV):
            acc = acc + buf[k]
        o_ref[...] = acc
        for cp in cps:
            cp.wait_send()

    vm = pl.BlockSpec(memory_space=pltpu.VMEM)
    return pl.pallas_call(
        body, name=name, in_specs=[vm], out_specs=vm, out_shape=_sds((R, C), F32),
        scratch_shapes=[pltpu.VMEM((N_DEV, R, C), F32), pltpu.SemaphoreType.DMA((N_DEV - 1,)),
                        pltpu.SemaphoreType.DMA((N_DEV - 1,))])(pack)


def _local_step(xs, tgt, p, ex):
    h1 = _norm_fwd(xs, p["pre_mix_norm"], "pre_mix_norm")
    proj, got = _mm_nn_blk(h1, ex.weight("w_in"), "proj_in", plans=ex.carry("proj_in"))
    ex.done("proj_in", got)
    biases = _relbias_fwd(p["rel_bias"], "rel_bias_fwd")
    fw = []
    for g in range(N_GROUPS):
        res, got = _attn_fwd(proj, biases[g], g, f"attn_fwd{g}", plans=ex.carry(f"attn_fwd{g}"))
        ex.done(f"attn_fwd{g}", got)
        fw.append(res)
    y, lse = _attn_merge([t[0] for t in fw], [t[1] for t in fw], "attn_merge")
    yh, o_h, states = _hgrn_fwd(proj, p["hgrn_lb_raw"], p["hgrn_norm"], "hgrn_fwd")
    W_a, W_h, W_out = ex.weight("w_branch_attn"), ex.weight("w_branch_hgrn"), ex.weight("w_out")
    W_up, W_down, conv_w = ex.weight("w_up"), ex.weight("w_down"), ex.weight("conv_w")
    za, zh, merged = _branch_fwd(y, yh, proj, W_a, W_h, "branch_fwd")
    mo, x1, h2 = _mix_out(merged, W_out, xs, p["post_mix_norm"], p["pre_ffn_norm"], "mix_out")
    u = _mm_nn_blk(h2, W_up, "ffn_up")
    a = _conv_gelu_fwd(u, conv_w, p["conv_b"], "conv_gelu_fwd")
    dx2, dff, g_post_ffn, loss = _loss_head(a, W_down, x1, tgt, p["post_ffn_norm"], "ffn_down_loss")

    da = _mm_nt(dff, W_down, "d_ffn_act")
    ex.grad("w_down", _mm_tn(a, dff, "g_w_down").reshape(N_CHIPS, D_FF // N_CHIPS, D_MODEL))
    (dcg, dcv, gwg, gwv, gbg, gbv), got = _conv_gelu_bwd(u, da, conv_w, p["conv_b"], "conv_gelu_bwd",
                                                          plans=ex.carry("conv_gelu_bwd"))
    ex.done("conv_gelu_bwd", got)
    g_conv_w = jnp.concatenate([gwg, gwv], axis=1)
    g_conv_b = jnp.concatenate([gbg, gbv], axis=1)
    du_parts, got = _conv_input_bwd(dcg, dcv, conv_w, "conv_input_bwd", plans=ex.carry("conv_input_bwd"))
    ex.done("conv_input_bwd", got)
    du = jnp.concatenate(du_parts, axis=1)
    dh2 = _mm_nt_blk(du, W_up, "d_ffn_in")
    ex.grad("w_up", _mm_tn_blk(h2, du, N_CHIPS, "g_w_up"))
    (dx1, g_pre_ffn), got = _prenorm_bwd(dh2, x1, p["pre_ffn_norm"], dx2, "pre_ffn_norm_bwd",
                                         plans=ex.carry("pre_ffn_norm_bwd"))
    ex.done("pre_ffn_norm_bwd", got)
    dmo, dmerged, g_post_mix = _postnorm_bwd(dx1, mo, p["post_mix_norm"], W_out, "post_mix_norm_bwd")
    ex.grad("w_out", _mm_tn(merged, dmo, "g_w_out").reshape(N_CHIPS, D_MODEL // N_CHIPS, D_MODEL))
    dza, dzh, dg0, dg1, dy, dyh = _branch_bwd(dmerged, za, zh, proj, W_a, W_h, "branch_bwd")
    ex.grad("w_branch_attn", _mm_tn_blk(y, dza, N_CHIPS, "g_w_branch_attn", together=True))
    ex.grad("w_branch_hgrn", _mm_tn_blk(yh, dzh, N_CHIPS, "g_w_branch_hgrn", together=True))
    dqkv, dbs = [], []
    for g in range(N_GROUPS):
        parts, db, got = _attn_bwd(proj, biases[g], lse, y, dy, g, f"attn_bwd{g}", plans=ex.carry(f"attn_bwd{g}"))
        ex.done(f"attn_bwd{g}", got)
        dqkv += parts
        dbs.append(db)
    g_rel_bias = _relbias_bwd(dbs, "rel_bias_bwd")
    dhg, g_lb_raw, g_hgrn_norm = _hgrn_bwd(proj, p["hgrn_lb_raw"], p["hgrn_norm"], o_h, states, dyh, "hgrn_bwd")
    dproj = jnp.concatenate([*[_bf(t) for t in dqkv], *dhg, dg0, dg1], axis=1)
    for piece in W_IN_PIECES:
        g, got = _mm_tn_blk(h1, dproj, N_CHIPS, f"g_{piece}", x_cols=W_IN_ROWS[piece],
                            plans=ex.carry(f"g_{piece}"))
        ex.done(f"g_{piece}", got)
        ex.grad(piece, g)
    dh1, got = _mm_nt_blk(dproj, ex.weight("w_in"), "d_proj_in", plans=ex.carry("d_proj_in"))
    ex.done("d_proj_in", got)
    (grad_x, g_pre_mix), got = _prenorm_bwd(dh1, xs, p["pre_mix_norm"], dx1, "pre_mix_norm_bwd",
                                            plans=ex.carry("pre_mix_norm_bwd"))
    ex.done("pre_mix_norm_bwd", got)
    small = dict(pre_mix_norm=g_pre_mix, rel_bias=g_rel_bias, hgrn_lb_raw=g_lb_raw, hgrn_norm=g_hgrn_norm,
                 post_mix_norm=g_post_mix, pre_ffn_norm=g_pre_ffn, conv_w=g_conv_w, conv_b=g_conv_b,
                 post_ffn_norm=g_post_ffn)
    return loss, grad_x, small


SMALL = ("pre_mix_norm", "rel_bias", "hgrn_lb_raw", "hgrn_norm", "post_mix_norm", "pre_ffn_norm", "conv_w", "conv_b",
         "post_ffn_norm")
BIG = ("w_in", "w_up", "w_down", "w_out", "w_branch_attn", "w_branch_hgrn")
WEIGHTS = ("pre_mix_norm", "w_in", "rel_bias", "hgrn_lb_raw", "hgrn_norm", "w_branch_attn", "w_branch_hgrn", "w_out",
           "post_mix_norm", "pre_ffn_norm", "w_up", "conv_w", "conv_b", "w_down", "post_ffn_norm")
MIXER = ("w_out", "w_branch_attn", "w_branch_hgrn")

SCHEDULE = {
    "proj_in": [("gather_ici_cw", ("w_up",) + MIXER)],
    "attn_fwd0": [("gather_pass", ("w_up",) + MIXER), ("gather_ici", ("w_down",))],
    "attn_fwd1": [("gather_pass", ("w_down",))],
    "conv_gelu_bwd": [("pair", ("w_down",))],
    "conv_input_bwd": [("chip", ("w_down",))],
    "pre_ffn_norm_bwd": [("pair", ("w_up",))],
    "attn_bwd0": [("chip", ("w_up",)), ("pair", MIXER)],
    "attn_bwd1": [("chip", MIXER)],
    "g_w_in_b": [("pair", ("w_in_a",))],
    "d_proj_in": [("chip", ("w_in_a",)), ("pair", ("w_in_b",))],
    "pre_mix_norm_bwd": [("chip", ("w_in_b",))],
}
W_IN_ROWS = dict(w_in_a=(0, 768), w_in_b=(3, 256))
W_IN_PIECES = tuple(W_IN_ROWS)
REDUCED = W_IN_PIECES + BIG[1:]


class _Exchange:
    def __init__(self, place, slots, conv_w_shard):
        self.place, self.slots, self.conv_w_shard = place, dict(slots), conv_w_shard
        self.conv_w = None
        self.g, self.from_sibling, self.pair_sums, self.arrived = {}, {}, {}, {}
        self.pending = []

    def weight(self, name):
        if name == "conv_w":
            return self.conv_w
        w = self.slots[name]
        return w.reshape(-1, D_MODEL) if name in ("w_out", "w_down") else w

    def grad(self, name, g):
        self.g[name] = g

    def carry(self, point):
        plans = []
        self.pending = SCHEDULE.get(point, [])
        for kind, names in self.pending:
            if kind in ("gather_ici", "gather_ici_cw"):
                wholes = [self.conv_w_shard] if kind == "gather_ici_cw" else []
                plans.append(_gather_ici_plan([self.slots[n] for n in names], wholes))
            elif kind == "gather_pass":
                plans.append(_gather_pass_plan([self.slots[n] for n in names]))
            elif kind == "pair":
                plans.append(_pair_plan([self.g[n] for n in names]))
            else:
                for n in names:
                    self.pair_sums[n] = _pair_sum(self.g[n], self.from_sibling[n], self.place[1:2], f"pair_sum_{n}")
                plans.append(_chip_plan([self.pair_sums[n] for n in names]))
        return plans

    def done(self, point, carried):
        for (kind, names), got in zip(self.pending, carried):
            if kind in ("gather_ici", "gather_ici_cw", "gather_pass"):
                self.slots.update(zip(names, got))
                if kind == "gather_ici_cw":
                    self.conv_w = got[len(names)].transpose(1, 0, 2).reshape(3, 2 * D_FF)
            elif kind == "pair":
                self.from_sibling.update(zip(names, got))
            else:
                self.arrived.update(zip(names, got))

    def reduced(self):
        halves = [_chip_sum(self.arrived[n], self.pair_sums[n], self.place, f"chip_sum_{n}") for n in REDUCED]
        out = dict(zip(REDUCED, _pair_concat(halves, "pair_concat")))
        out["w_in"] = jnp.concatenate([out.pop(n) for n in W_IN_PIECES], axis=0)
        return out


def kernel(x, pre_mix_norm, w_in, rel_bias, hgrn_lb_raw, hgrn_norm, w_branch_attn, w_branch_hgrn, w_out, post_mix_norm, pre_ffn_norm, w_up, conv_w, conv_b, w_down, post_ffn_norm, loss_target, m_pre_mix_norm, m_w_in, m_rel_bias, m_hgrn_lb_raw, m_hgrn_norm, m_w_branch_attn, m_w_branch_hgrn, m_w_out, m_post_mix_norm, m_pre_ffn_norm, m_w_up, m_conv_w, m_conv_b, m_w_down, m_post_ffn_norm, v_pre_mix_norm, v_w_in, v_rel_bias, v_hgrn_lb_raw, v_hgrn_norm, v_w_branch_attn, v_w_branch_hgrn, v_w_out, v_post_mix_norm, v_pre_ffn_norm, v_w_up, v_conv_w, v_conv_b, v_w_down, v_post_ffn_norm):
    w = dict(pre_mix_norm=pre_mix_norm, w_in=w_in, rel_bias=rel_bias, hgrn_lb_raw=hgrn_lb_raw, hgrn_norm=hgrn_norm,
             w_branch_attn=w_branch_attn, w_branch_hgrn=w_branch_hgrn, w_out=w_out, post_mix_norm=post_mix_norm,
             pre_ffn_norm=pre_ffn_norm, w_up=w_up, conv_w=conv_w, conv_b=conv_b, w_down=w_down,
             post_ffn_norm=post_ffn_norm)
    m = dict(pre_mix_norm=m_pre_mix_norm, w_in=m_w_in, rel_bias=m_rel_bias, hgrn_lb_raw=m_hgrn_lb_raw,
             hgrn_norm=m_hgrn_norm, w_branch_attn=m_w_branch_attn, w_branch_hgrn=m_w_branch_hgrn, w_out=m_w_out,
             post_mix_norm=m_post_mix_norm, pre_ffn_norm=m_pre_ffn_norm, w_up=m_w_up, conv_w=m_conv_w,
             conv_b=m_conv_b, w_down=m_w_down, post_ffn_norm=m_post_ffn_norm)
    v = dict(pre_mix_norm=v_pre_mix_norm, w_in=v_w_in, rel_bias=v_rel_bias, hgrn_lb_raw=v_hgrn_lb_raw,
             hgrn_norm=v_hgrn_norm, w_branch_attn=v_w_branch_attn, w_branch_hgrn=v_w_branch_hgrn, w_out=v_w_out,
             post_mix_norm=v_post_mix_norm, pre_ffn_norm=v_pre_ffn_norm, w_up=v_w_up, conv_w=v_conv_w,
             conv_b=v_conv_b, w_down=v_w_down, post_ffn_norm=v_post_ffn_norm)
    shard2d = {n: (w[n][0] if w[n].ndim == 3 else w[n]) for n in WEIGHTS}
    chip = 2 * lax.axis_index("x") + lax.axis_index("y")
    core = lax.axis_index("c")

    place = jnp.stack([chip, core]).astype(jnp.int32)
    slots = {n: _cast_into_slot(shard2d[n], place, f"cast_{n}") for n in BIG}
    slots["w_in"] = _gather_weights([slots["w_in"]], [], "gather_w_in")[0]
    ex = _Exchange(place, slots, shard2d["conv_w"])
    loss, grad_x, small = _local_step(x[0], loss_target[0], {n: w[n] for n in SMALL if n != "conv_w"}, ex)

    flat = [small[n].reshape(-1) for n in SMALL] + [loss.reshape(-1)]
    sizes = [t.shape[0] for t in flat]
    summed = _all_sum(jnp.concatenate(flat).reshape(-1, LANES), "sum_small").reshape(-1)
    offs = [sum(sizes[:i]) for i in range(len(sizes))]
    grads = {}
    for n, o, sz in zip(SMALL, offs, sizes):
        grads[n] = summed[o:o + sz].reshape(small[n].shape)
    loss_total = summed[offs[-1]]
    cw = 2 * D_FF // N_CHIPS
    grads["conv_w"] = lax.dynamic_slice(grads["conv_w"], (0, chip * cw), (3, cw))

    grads.update(ex.reduced())

    out_g, out_d, out_m, out_v = [], [], [], []
    for n in WEIGHTS:
        d2, m2, v2 = _adamw(shard2d[n], grads[n], m[n].reshape(shard2d[n].shape), v[n].reshape(shard2d[n].shape),
                            f"adamw_{n}")
        shape = w[n].shape
        out_g.append(grads[n].reshape(shape))
        out_d.append(d2.reshape(shape))
        out_m.append(m2.reshape(shape))
        out_v.append(v2.reshape(shape))
    return (loss_total, grad_x[None], *out_g, *out_d, *out_m, *out_v)
```

```python
import functools
import math

import jax
import jax.numpy as jnp
from jax import lax
from jax.experimental import pallas as pl
from jax.experimental.pallas import tpu as pltpu

F32 = jnp.float32
BF16 = jnp.bfloat16
MESH = pl.DeviceIdType.MESH

D_MODEL = 1024
N_GROUPS = 3
DILATIONS = (1, 4, 16)
HEADS = 8
HEAD_DIM = 64
GROUP_W = HEADS * HEAD_DIM
QKV_W = N_GROUPS * 3 * GROUP_W
BLK = 128
NEG_INF = -1e30
NUM_BUCKETS = 32
MAX_EXACT = 16
MAX_DISTANCE = 2048
HG_HEADS = 4
HG_DK = 128
HG_W = HG_HEADS * HG_DK
HG_CHUNK = 32
HG_TILE = 256
IN_W = QKV_W + 4 * HG_W + 2 * D_MODEL
D_FF = 2816
EPS = 1e-6
N_CHIPS = 4
N_DEV = 8
LANES = 128

ADAM_LR, ADAM_B1, ADAM_B2, ADAM_EPS, ADAM_WD, ADAM_STEP = 0.001, 0.9, 0.999, 1e-08, 0.01, 10

VMEM_LIMIT = 56 * 1024 * 1024


def _cp(n_axes):
    return pltpu.CompilerParams(dimension_semantics=("arbitrary",) * n_axes, vmem_limit_bytes=VMEM_LIMIT)


def _sds(shape, dtype):
    return jax.ShapeDtypeStruct(tuple(shape), dtype)


def _sigmoid(v):
    return 1.0 / (1.0 + jnp.exp(-v))


def _bf(v):
    return v.astype(BF16)


def _dot(a, b, dims):
    return lax.dot_general(a, b, (dims, ((), ())), preferred_element_type=F32)


NN = ((1,), (0,))
NT = ((1,), (1,))
TN = ((0,), (0,))

ANY = pl.BlockSpec(memory_space=pl.ANY)


class _Plan:
    def __init__(self, copies, n_sems, ins=(), inouts=(), outs=()):
        self.copies, self.n_sems = copies, n_sems
        self.ins, self.inouts, self.outs = list(ins), list(inouts), list(outs)


def _call(body, plans=None, *, name, grid, in_specs, out_specs, out_shape, args, scratch_shapes=()):
    plans = list(plans or ())
    in_specs, out_specs, out_shape = list(in_specs), list(out_specs), list(out_shape)
    scratch_shapes = list(scratch_shapes)
    n_in, n_out, n_scr = len(in_specs), len(out_specs), len(scratch_shapes)
    x_in, x_out, aliases, spans = [], [], {}, []
    for p in plans:
        i0, o0 = len(x_in), len(x_out)
        x_in += p.ins
        for a in p.inouts:
            aliases[n_in + len(x_in)] = n_out + len(x_out)
            x_in.append(a)
            x_out.append(_sds(a.shape, a.dtype))
        x_out += p.outs
        spans.append((i0, len(p.ins), o0, len(p.inouts), len(p.outs)))
    sems = [pltpu.SemaphoreType.DMA((p.n_sems,)) for p in plans for _ in range(3)]

    def wrapped(*refs):
        xi = refs[n_in:n_in + len(x_in)]
        base = n_in + len(x_in)
        xo = refs[base + n_out:base + n_out + len(x_out)]
        sbase = base + n_out + len(x_out)
        xs = refs[sbase + n_scr:]
        ids = [pl.program_id(k) for k in range(len(grid))]
        first = functools.reduce(jnp.logical_and, [i == 0 for i in ids])
        last = functools.reduce(jnp.logical_and, [i == g - 1 for i, g in zip(ids, grid)])

        def descriptors(k):
            i0, ni, o0, nio, no = spans[k]
            return plans[k].copies(xi[i0:i0 + ni], xo[o0:o0 + nio], xo[o0 + nio:o0 + nio + no], *xs[3 * k:3 * k + 3])

        @pl.when(first)
        def _():
            for k in range(len(plans)):
                sends, _, local = descriptors(k)
                for cp in (*sends, *local):
                    cp.start()

        body(*refs[:n_in], *refs[base:base + n_out], *refs[sbase:sbase + n_scr])

        @pl.when(last)
        def _():
            for k in range(len(plans)):
                sends, recvs, local = descriptors(k)
                for cp in recvs:
                    cp.wait_recv()
                for cp in sends:
                    cp.wait_send()
                for cp in local:
                    cp.wait()

    res = pl.pallas_call(
        wrapped if plans else body, name=name, grid=grid, in_specs=in_specs + [ANY] * len(x_in),
        out_specs=out_specs + [ANY] * len(x_out), out_shape=out_shape + x_out, input_output_aliases=aliases,
        scratch_shapes=scratch_shapes + sems, compiler_params=_cp(len(grid)))(*args, *x_in)
    res = list(res)
    carried = [res[n_out + o0:n_out + o0 + nio + no] for (_, _, o0, nio, no) in spans]
    return res[:n_out], carried


def _mm_nn_blk(a, wg, name, tm=512, plans=None):
    M, K = a.shape
    nb, _, Nb = wg.shape

    def body(a_ref, w_ref, o_ref):
        o_ref[...] = _dot(_bf(a_ref[...]), w_ref[...], NN)

    (out,), carried = _call(
        body, plans, name=name, grid=(nb, M // tm),
        in_specs=[pl.BlockSpec((tm, K), lambda j, i: (i, 0)), pl.BlockSpec((None, K, Nb), lambda j, i: (j, 0, 0))],
        out_specs=[pl.BlockSpec((tm, Nb), lambda j, i: (i, j))],
        out_shape=[_sds((M, nb * Nb), F32)], args=(a, wg))
    return out if plans is None else (out, carried)


def _mm_nt_blk(dy, wg, name, tm=1024, plans=None):
    M = dy.shape[0]
    nb, K, Nb = wg.shape

    def body(dy_ref, w_ref, o_ref):
        j = pl.program_id(1)
        r = _dot(_bf(dy_ref[...]), w_ref[...], NT)

        @pl.when(j == 0)
        def _():
            o_ref[...] = r

        @pl.when(j > 0)
        def _():
            o_ref[...] += r

    (out,), carried = _call(
        body, plans, name=name, grid=(M // tm, nb),
        in_specs=[pl.BlockSpec((tm, Nb), lambda i, j: (i, j)), pl.BlockSpec((None, K, Nb), lambda i, j: (j, 0, 0))],
        out_specs=[pl.BlockSpec((tm, K), lambda i, j: (i, 0))],
        out_shape=[_sds((M, K), F32)], args=(dy, wg))
    return out if plans is None else (out, carried)


def _mm_tn_blk(x, dy, nb, name, tk=2048, x_cols=None, plans=None, together=False):
    T, Mx = x.shape
    xk, Mx = (0, Mx) if x_cols is None else x_cols
    Nb = dy.shape[1] // nb
    nj = nb if together else 1

    def body(x_ref, dy_ref, o_ref):
        t = pl.program_id(1)
        r = _dot(_bf(x_ref[...]), _bf(dy_ref[...]), TN)
        for j in range(nj):
            rj = r[:, j * Nb:(j + 1) * Nb]

            @pl.when(t == 0)
            def _():
                o_ref[j] = rj

            @pl.when(t > 0)
            def _():
                o_ref[j] += rj

    (out,), carried = _call(
        body, plans, name=name, grid=(nb // nj, T // tk),
        in_specs=[pl.BlockSpec((tk, Mx), lambda j, t: (t, xk)), pl.BlockSpec((tk, nj * Nb), lambda j, t: (t, j))],
        out_specs=[pl.BlockSpec((nj, Mx, Nb), lambda j, t: (j, 0, 0))],
        out_shape=[_sds((nb, Mx, Nb), F32)], args=(x, dy))
    return out if plans is None else (out, carried)


def _mm_nt(dy, w, name, tm=512):
    M, N = dy.shape
    K = w.shape[0]

    def body(dy_ref, w_ref, o_ref):
        o_ref[...] = _dot(_bf(dy_ref[...]), w_ref[...], NT)

    return pl.pallas_call(
        body, name=name, grid=(M // tm,),
        in_specs=[pl.BlockSpec((tm, N), lambda i: (i, 0)), pl.BlockSpec((K, N), lambda i: (0, 0))],
        out_specs=pl.BlockSpec((tm, K), lambda i: (i, 0)),
        out_shape=_sds((M, K), F32), compiler_params=_cp(1))(dy, w)


def _mm_tn(x, dy, name, tk=1024):
    T, Mx = x.shape
    N = dy.shape[1]

    def body(x_ref, dy_ref, o_ref):
        t = pl.program_id(0)
        r = _dot(_bf(x_ref[...]), _bf(dy_ref[...]), TN)

        @pl.when(t == 0)
        def _():
            o_ref[...] = r

        @pl.when(t > 0)
        def _():
            o_ref[...] += r

    return pl.pallas_call(
        body, name=name, grid=(T // tk,),
        in_specs=[pl.BlockSpec((tk, Mx), lambda t: (t, 0)), pl.BlockSpec((tk, N), lambda t: (t, 0))],
        out_specs=pl.BlockSpec((Mx, N), lambda t: (0, 0)),
        out_shape=_sds((Mx, N), F32), compiler_params=_cp(1))(x, dy)


def _tile(arr, bw, col=lambda c: 0):
    return ("tile", arr, bw, col)


def _full(arr):
    return ("full", arr)


def _out_tile(width, dtype, bw, col=lambda c: 0):
    return ("tile", width, dtype, bw, col)


def _out_acc(rows, width, bw, col=lambda c: 0):
    return ("acc", rows, width, bw, col)


def _rows_call(name, body, n_rows, tm, ncol, ins, outs, plans=None):
    in_specs, args = [], []
    for e in ins:
        if e[0] == "tile":
            _, arr, bw, col = e
            in_specs.append(pl.BlockSpec((tm, bw), functools.partial(lambda c, i, col: (i, col(c)), col=col)))
        else:
            arr = e[1]
            in_specs.append(pl.BlockSpec(arr.shape, functools.partial(lambda c, i, nd: (0,) * nd, nd=arr.ndim)))
        args.append(arr)
    out_specs, out_shape = [], []
    for e in outs:
        if e[0] == "tile":
            _, width, dtype, bw, col = e
            out_specs.append(pl.BlockSpec((tm, bw), functools.partial(lambda c, i, col: (i, col(c)), col=col)))
            out_shape.append(_sds((n_rows, width), dtype))
        else:
            _, rows, width, bw, col = e
            out_specs.append(pl.BlockSpec((rows, bw), functools.partial(lambda c, i, col: (0, col(c)), col=col)))
            out_shape.append(_sds((rows, width), F32))
    out, carried = _call(body, plans, name=name, grid=(ncol, n_rows // tm), in_specs=in_specs, out_specs=out_specs,
                         out_shape=out_shape, args=args)
    return out if plans is None else (out, carried)


def _acc(ref, val):
    i = pl.program_id(1)

    @pl.when(i == 0)
    def _():
        ref[...] = val

    @pl.when(i > 0)
    def _():
        ref[...] += val


def _rinv(z):
    return lax.rsqrt(jnp.mean(z * z, axis=-1, keepdims=True) + EPS)


def _norm_bwd(dy, zhat, r, w):
    dyw = dy * w
    return r * (dyw - zhat * jnp.mean(dyw * zhat, axis=-1, keepdims=True))


def _norm_fwd(x, w, name):
    def body(x_ref, w_ref, h_ref):
        xv = x_ref[...]
        h_ref[...] = _bf(xv * _rinv(xv) * w_ref[...])

    return _rows_call(name, body, x.shape[0], 512, 1, [_tile(x, D_MODEL), _full(w)],
                      [_out_tile(D_MODEL, BF16, D_MODEL)])[0]


def _prenorm_bwd(dh, xin, w, dres, name, plans=None):
    def body(dh_ref, x_ref, w_ref, dres_ref, dx_ref, dw_ref):
        xv = x_ref[...]
        r = _rinv(xv)
        xhat = xv * r
        dhv = dh_ref[...]
        dx_ref[...] = dres_ref[...] + _norm_bwd(dhv, xhat, r, w_ref[...])
        _acc(dw_ref, jnp.sum(dhv * xhat, axis=0, keepdims=True))

    return _rows_call(name, body, xin.shape[0], 512, 1,
                      [_tile(dh, D_MODEL), _tile(xin, D_MODEL), _full(w), _tile(dres, D_MODEL)],
                      [_out_tile(D_MODEL, F32, D_MODEL), _out_acc(1, D_MODEL, D_MODEL)], plans)


def _postnorm_bwd(dout, z, w, w_mat, name):
    def body(do_ref, z_ref, w_ref, wm_ref, dz_ref, dm_ref, dw_ref):
        zv = z_ref[...]
        r = _rinv(zv)
        zhat = zv * r
        dov = do_ref[...]
        dz = _bf(_norm_bwd(dov, zhat, r, w_ref[...]))
        dz_ref[...] = dz
        dm_ref[...] = _dot(dz, wm_ref[...], NT)
        _acc(dw_ref, jnp.sum(dov * zhat, axis=0, keepdims=True))

    return _rows_call(name, body, z.shape[0], 512, 1,
                      [_tile(dout, D_MODEL), _tile(z, D_MODEL), _full(w), _full(w_mat)],
                      [_out_tile(D_MODEL, BF16, D_MODEL), _out_tile(D_MODEL, F32, D_MODEL),
                       _out_acc(1, D_MODEL, D_MODEL)])


def _t5_bucket(dist):
    n = jnp.maximum(dist, 0)
    nf = jnp.maximum(n, 1).astype(F32)
    large = MAX_EXACT + (jnp.log(nf / MAX_EXACT) / math.log(MAX_DISTANCE / MAX_EXACT)
                         * (NUM_BUCKETS - MAX_EXACT)).astype(jnp.int32)
    large = jnp.minimum(large, NUM_BUCKETS - 1)
    return jnp.where(n < MAX_EXACT, n, large)


def _band_rel():
    return jnp.arange(BLK)[:, None] + BLK - jnp.arange(2 * BLK)[None, :]


def _band_valid():
    rel = _band_rel()
    window = (rel >= 0) & (rel <= BLK)
    first = window & (jnp.arange(2 * BLK)[None, :] >= BLK)
    return jnp.stack([first, window]).astype(F32).reshape(2, 1, BAND)


RES_UNROLL = 4
PAIR = LANES // HEAD_DIM


def _pair_lanes():
    first = lax.broadcasted_iota(jnp.int32, (1, LANES), 1) < HEAD_DIM
    return first, jnp.logical_not(first)


def _heads_per_step(d):
    return HEADS if d == 1 else LANES // HEAD_DIM


def _sub_rows(r, d):
    return pl.ds(r, BLK, stride=d) if d > 1 else pl.ds(0, BLK)


def _for_residues(d, fn):
    if d <= RES_UNROLL:
        for r in range(d):
            fn(r)
    else:
        def group(i, carry):
            for k in range(RES_UNROLL):
                fn(i * RES_UNROLL + k)
            return carry

        lax.fori_loop(0, d // RES_UNROLL, group, 0)


def _attn_specs(d, g, qblock):
    cw = _heads_per_step(d) * HEAD_DIM

    def col(part, hp):
        return (g * 3 + part) * (GROUP_W // cw) + hp

    def cur(part):
        return pl.BlockSpec((d * BLK, cw), lambda hp, n: (qblock(n), col(part, hp)))

    def prev(part):
        return pl.BlockSpec((d * BLK, cw), lambda hp, n: (jnp.maximum(qblock(n) - 1, 0), col(part, hp)))

    return cur, prev


def _attn_fwd(proj, bias, g, name, plans=None):
    S = proj.shape[0]
    d = DILATIONS[g]
    NB = S // (d * BLK)
    hps = _heads_per_step(d)

    def body(q_ref, kp_ref, kc_ref, vp_ref, vc_ref, b_ref, o_ref, lse_ref):
        hp = pl.program_id(0)
        later = jnp.minimum(pl.program_id(1), 1)

        def residue(r):
            rows = _sub_rows(r, d)
            q2 = q_ref[rows, :]
            k2 = jnp.concatenate([kp_ref[rows, :], kc_ref[rows, :]], axis=0)
            v2 = jnp.concatenate([vp_ref[rows, :], vc_ref[rows, :]], axis=0)
            outs, lses = [], []
            for pp in range(hps // PAIR):
                ps = slice(pp * LANES, (pp + 1) * LANES)
                qp, kp, vp = _bf(q2[:, ps]), _bf(k2[:, ps]), _bf(v2[:, ps])
                o_h, lse_h = [], []
                for hh, own in enumerate(_pair_lanes()):
                    s = _dot(qp, jnp.where(own, kp, 0), NT) * (HEAD_DIM ** -0.5) + b_ref[later, hp * hps + pp * PAIR + hh]
                    m = jnp.max(s, axis=-1, keepdims=True)
                    p = jnp.exp(s - m)
                    l = jnp.sum(p, axis=-1, keepdims=True)
                    o_h.append(_dot(_bf(p), vp, NN) / l)
                    lse_h.append(m + jnp.log(l))
                first = _pair_lanes()[0]
                outs.append(jnp.where(first, o_h[0], o_h[1]))
                lses.append(jnp.where(first, lse_h[0], lse_h[1]))
            o_ref[rows, :] = outs[0] if len(outs) == 1 else jnp.concatenate(outs, axis=1)
            lse_ref[rows, :] = lses[0] if len(lses) == 1 else jnp.concatenate(lses, axis=1)

        _for_residues(d, residue)

    cur, prev = _attn_specs(d, g, lambda n: n)
    out = pl.BlockSpec((d * BLK, hps * HEAD_DIM), lambda hp, n: (n, hp))
    res, carried = _call(
        body, plans, name=name, grid=(HEADS // hps, NB),
        in_specs=[cur(0), prev(1), cur(1), prev(2), cur(2),
                  pl.BlockSpec((2, HEADS, BLK, 2 * BLK), lambda hp, n: (0, 0, 0, 0))],
        out_specs=[out, out], out_shape=[_sds((S, GROUP_W), F32)] * 2,
        args=(proj, proj, proj, proj, proj, bias))
    return res if plans is None else (res, carried)


def _attn_merge(os_, lses, name):
    def body(o0, o1, o2, l0, l1, l2, y_ref, lse_ref):
        a, b, c = l0[...], l1[...], l2[...]
        m = jnp.maximum(jnp.maximum(a, b), c)
        ea, eb, ec = jnp.exp(a - m), jnp.exp(b - m), jnp.exp(c - m)
        den = ea + eb + ec
        y_ref[...] = (ea * o0[...] + eb * o1[...] + ec * o2[...]) / den
        lse_ref[...] = m + jnp.log(den)

    S = os_[0].shape[0]
    return _rows_call(name, body, S, 512, 1, [_tile(t, GROUP_W) for t in (*os_, *lses)],
                      [_out_tile(GROUP_W, F32, GROUP_W)] * 2)


def _attn_bwd(proj, bias, lse, y, dy, g, name, plans=None):
    S = proj.shape[0]
    d = DILATIONS[g]
    NB = S // (d * BLK)
    hps = _heads_per_step(d)

    def body(q_ref, kp_ref, kc_ref, vp_ref, vc_ref, b_ref, l_ref, y_ref, dy_ref,
             dq_ref, dk_ref, dv_ref, db_ref, ck_ref, cv_ref):
        hp, n = pl.program_id(0), pl.program_id(1)

        @pl.when((hp == 0) & (n == 0))
        def _():
            db_ref[...] = jnp.zeros_like(db_ref)

        @pl.when(n == 0)
        def _():
            ck_ref[...] = jnp.zeros_like(ck_ref)
            cv_ref[...] = jnp.zeros_like(cv_ref)

        @pl.when(n < NB)
        def _():
            later = jnp.minimum(n, 1)

            def residue(r):
                rows = _sub_rows(r, d)
                q2 = q_ref[rows, :]
                k2 = jnp.concatenate([kp_ref[rows, :], kc_ref[rows, :]], axis=0)
                v2 = jnp.concatenate([vp_ref[rows, :], vc_ref[rows, :]], axis=0)
                l2, y2, dy2 = l_ref[rows, :], y_ref[rows, :], dy_ref[rows, :]
                dqs, dks, dvs = [], [], []
                for pp in range(hps // PAIR):
                    ps = slice(pp * LANES, (pp + 1) * LANES)
                    qp, kp, vp = _bf(q2[:, ps]), _bf(k2[:, ps]), _bf(v2[:, ps])
                    dyp, yp = dy2[:, ps], y2[:, ps]
                    dq_h, dk_h, dv_h = [], [], []
                    for hh, own in enumerate(_pair_lanes()):
                        head = hp * hps + pp * PAIR + hh
                        s = _dot(qp, jnp.where(own, kp, 0), NT) * (HEAD_DIM ** -0.5) + b_ref[later, head]
                        p = jnp.exp(s - l2[:, pp * LANES + hh * HEAD_DIM:pp * LANES + hh * HEAD_DIM + 1])
                        dyh = jnp.where(own, dyp, 0.0)
                        delta = jnp.sum(dyh * yp, axis=-1, keepdims=True)
                        ds = p * (_dot(_bf(dyh), vp, NT) - delta)
                        db_ref[head] += ds
                        dsb = _bf(ds * (HEAD_DIM ** -0.5))
                        dq_h.append(_dot(dsb, kp, NN))
                        dk_h.append(_dot(dsb, qp, TN))
                        dv_h.append(_dot(_bf(p), _bf(dyp), TN))
                    first = _pair_lanes()[0]
                    dqs.append(jnp.where(first, dq_h[0], dq_h[1]))
                    dks.append(jnp.where(first, dk_h[0], dk_h[1]))
                    dvs.append(jnp.where(first, dv_h[0], dv_h[1]))
                dkb = dks[0] if len(dks) == 1 else jnp.concatenate(dks, axis=1)
                dvb = dvs[0] if len(dvs) == 1 else jnp.concatenate(dvs, axis=1)
                dq_ref[rows, :] = dqs[0] if len(dqs) == 1 else jnp.concatenate(dqs, axis=1)
                dk_ref[rows, :] = ck_ref[rows, :] + dkb[:BLK]
                dv_ref[rows, :] = cv_ref[rows, :] + dvb[:BLK]
                ck_ref[rows, :] = dkb[BLK:]
                cv_ref[rows, :] = dvb[BLK:]

            _for_residues(d, residue)

        @pl.when(n == NB)
        def _():
            dk_ref[...] = ck_ref[...]
            dv_ref[...] = cv_ref[...]

    def qn(n):
        return jnp.minimum(n, NB - 1)

    cur, prev = _attn_specs(d, g, qn)
    cw = hps * HEAD_DIM
    row = pl.BlockSpec((d * BLK, cw), lambda hp, n: (qn(n), hp))
    done = pl.BlockSpec((d * BLK, cw), lambda hp, n: (jnp.maximum(n - 1, 0), hp))
    (dq, dk, dv, db), carried = _call(
        body, plans, name=name, grid=(HEADS // hps, NB + 1),
        in_specs=[cur(0), prev(1), cur(1), prev(2), cur(2),
                  pl.BlockSpec((2, HEADS, BLK, 2 * BLK), lambda hp, n: (0, 0, 0, 0)), row, row, row],
        out_specs=[row, done, done, pl.BlockSpec((HEADS, BLK, 2 * BLK), lambda hp, n: (0, 0, 0))],
        out_shape=[_sds((S, GROUP_W), F32)] * 3 + [_sds((HEADS, BLK, 2 * BLK), F32)],
        scratch_shapes=[pltpu.VMEM((d * BLK, cw), F32)] * 2,
        args=(proj, proj, proj, proj, proj, bias, lse, y, dy))
    return ([dq, dk, dv], db) if plans is None else ([dq, dk, dv], db, carried)


BAND = BLK * 2 * BLK


def _bucket_onehot():
    buckets = jnp.stack([_t5_bucket(_band_rel() * d) for d in DILATIONS]).reshape(N_GROUPS, 1, BAND)
    return (buckets == jnp.arange(NUM_BUCKETS).reshape(1, NUM_BUCKETS, 1)).astype(F32)


def _relbias_fwd(rel_bias, name):
    table = rel_bias.reshape(NUM_BUCKETS, N_GROUPS, HEADS).transpose(1, 0, 2)

    def body(t_ref, oh_ref, valid_ref, o_ref):
        bias = lax.dot_general(t_ref[...], oh_ref[...], (TN, ((), ())), preferred_element_type=F32,
                               precision=lax.Precision.HIGHEST)
        for k in range(2):
            o_ref[k] = jnp.where(valid_ref[k] > 0.5, bias, NEG_INF)

    out = pl.pallas_call(
        body, name=name, grid=(N_GROUPS,),
        in_specs=[pl.BlockSpec((None, NUM_BUCKETS, HEADS), lambda g: (g, 0, 0)),
                  pl.BlockSpec((None, NUM_BUCKETS, BAND), lambda g: (g, 0, 0)),
                  pl.BlockSpec((2, 1, BAND), lambda g: (0, 0, 0))],
        out_specs=pl.BlockSpec((None, 2, HEADS, BAND), lambda g: (g, 0, 0, 0)),
        out_shape=_sds((N_GROUPS, 2, HEADS, BAND), F32), compiler_params=_cp(1))(table, _bucket_onehot(), _band_valid())
    return out.reshape(N_GROUPS, 2, HEADS, BLK, 2 * BLK)


def _relbias_bwd(dbs, name):
    band = BAND
    onehot = _bucket_onehot()
    dbf = jnp.stack([db.reshape(HEADS, band) for db in dbs])

    def body(oh_ref, db_ref, o_ref):
        o_ref[...] = lax.dot_general(oh_ref[...], db_ref[...], (NT, ((), ())), preferred_element_type=F32,
                                     precision=lax.Precision.HIGHEST)

    out = pl.pallas_call(
        body, name=name, grid=(N_GROUPS,),
        in_specs=[pl.BlockSpec((None, NUM_BUCKETS, band), lambda g: (g, 0, 0)),
                  pl.BlockSpec((None, HEADS, band), lambda g: (g, 0, 0))],
        out_specs=pl.BlockSpec((None, NUM_BUCKETS, HEADS), lambda g: (g, 0, 0)),
        out_shape=_sds((N_GROUPS, NUM_BUCKETS, HEADS), F32), compiler_params=_cp(1))(onehot, dbf)
    return out.transpose(1, 0, 2).reshape(NUM_BUCKETS, N_GROUPS * HEADS)


def _chunk_pos(shape):
    return lax.broadcasted_iota(jnp.int32, shape, 0) % HG_CHUNK


def _chunk_cumsum(v):
    pos = _chunk_pos(v.shape)
    s = 1
    while s < HG_CHUNK:
        v = v + jnp.where(pos >= s, pltpu.roll(v, s, 0), 0.0)
        s *= 2
    return v


def _chunk_rev_cumsum(v):
    pos = _chunk_pos(v.shape)
    n = v.shape[0]
    s = 1
    while s < HG_CHUNK:
        v = v + jnp.where(pos < HG_CHUNK - s, pltpu.roll(v, n - s, 0), 0.0)
        s *= 2
    return v


def _lower_bound(raw):
    a0, a1 = raw[0:1], raw[1:2]
    m = jnp.maximum(a0, a1)
    e0, e1 = jnp.exp(a0 - m), jnp.exp(a1 - m)
    return e0 / (e0 + e1)


def _hg_gates(qr, fr, lb):
    sf = _sigmoid(fr)
    f = lb + (1.0 - lb) * sf
    sq = _sigmoid(qr)
    return qr * sq, sq, f, sf


HG_COL0 = QKV_W // HG_W


def _hgrn_fwd(proj, lb_raw, nw, name):
    S = proj.shape[0]
    ncs = HG_TILE // HG_CHUNK
    tril = jnp.tril(jnp.ones((HG_CHUNK, HG_CHUNK), dtype=bool))

    def body(q_ref, f_ref, i_ref, og_ref, lb_ref, nw_ref, y_ref, o_ref, st_ref, state):
        @pl.when(pl.program_id(0) == 0)
        def _():
            state[...] = jnp.zeros_like(state)

        lb = _lower_bound(lb_ref[...])
        q, _, f, _ = _hg_gates(q_ref[...], f_ref[...], lb)
        k = 1.0 - f
        G = _chunk_cumsum(jnp.log(f))
        row = lax.broadcasted_iota(jnp.int32, (HG_CHUNK, HG_CHUNK), 0)
        col = lax.broadcasted_iota(jnp.int32, (HG_CHUNK, HG_CHUNK), 1)
        heads = [slice(h * HG_DK, (h + 1) * HG_DK) for h in range(HG_HEADS)]
        sts = [state[h] for h in range(HG_HEADS)]
        for c in range(ncs):
            cs = slice(c * HG_CHUNK, (c + 1) * HG_CHUNK)
            for h, hs in enumerate(heads):
                Gc = G[cs, hs]
                gl = Gc[HG_CHUNK - 1:HG_CHUNK]
                qt = _bf(q[cs, hs] * jnp.exp(Gc))
                kt = _bf(k[cs, hs] * jnp.exp(-Gc))
                kd = _bf(k[cs, hs] * jnp.exp(gl - Gc))
                v = _bf(i_ref[cs, hs])
                A = jnp.where(row >= col, _dot(qt, kt, NT), 0.0)
                o_ref[cs, hs] = _dot(_bf(A), v, NN) + _dot(qt, _bf(sts[h]), NT)
                st_ref[c, h] = sts[h]
                sts[h] = sts[h] * jnp.exp(gl) + _dot(v, kd, TN)
        for h, hs in enumerate(heads):
            state[h] = sts[h]
            oh = o_ref[:, hs]
            og = og_ref[:, hs]
            y_ref[:, hs] = oh * _rinv(oh) * nw_ref[...] * (og * _sigmoid(og))

    def colspec(j):
        return pl.BlockSpec((HG_TILE, HG_W), lambda i: (i, HG_COL0 + j))

    return pl.pallas_call(
        body, name=name, grid=(S // HG_TILE,),
        in_specs=[colspec(0), colspec(1), colspec(2), colspec(3),
                  pl.BlockSpec((2, HG_W), lambda i: (0, 0)), pl.BlockSpec((1, HG_DK), lambda i: (0, 0))],
        out_specs=[pl.BlockSpec((HG_TILE, HG_W), lambda i: (i, 0))] * 2
        + [pl.BlockSpec((ncs, HG_HEADS, HG_DK, HG_DK), lambda i: (i, 0, 0, 0))],
        out_shape=[_sds((S, HG_W), F32)] * 2 + [_sds((S // HG_CHUNK, HG_HEADS, HG_DK, HG_DK), F32)],
        scratch_shapes=[pltpu.VMEM((HG_HEADS, HG_DK, HG_DK), F32)],
        compiler_params=_cp(1))(proj, proj, proj, proj, lb_raw, nw)


def _hgrn_bwd(proj, lb_raw, nw, o, states, dy, name):
    S = proj.shape[0]
    ncs = HG_TILE // HG_CHUNK
    nt = S // HG_TILE

    def body(q_ref, f_ref, i_ref, og_ref, lb_ref, nw_ref, o_ref, st_ref, dy_ref,
             dq_ref, df_ref, di_ref, dog_ref, dlb_ref, dnw_ref, dstate, do_s, dG_s, dgl_s, dk_s, dlb_s):
        step = pl.program_id(0)

        @pl.when(step == 0)
        def _():
            dstate[...] = jnp.zeros_like(dstate)
            dlb_s[...] = jnp.zeros_like(dlb_s)
            dnw_ref[...] = jnp.zeros_like(dnw_ref)

        lb = _lower_bound(lb_ref[...])
        qr = q_ref[...]
        q, sq, f, sf = _hg_gates(qr, f_ref[...], lb)
        k = 1.0 - f
        G = _chunk_cumsum(jnp.log(f))
        nwv = nw_ref[...]
        row = lax.broadcasted_iota(jnp.int32, (HG_CHUNK, HG_CHUNK), 0)
        col = lax.broadcasted_iota(jnp.int32, (HG_CHUNK, HG_CHUNK), 1)
        for h in range(HG_HEADS):
            hs = slice(h * HG_DK, (h + 1) * HG_DK)
            oh = o_ref[:, hs]
            r = _rinv(oh)
            ohat = oh * r
            og = og_ref[:, hs]
            sg = _sigmoid(og)
            dyh = dy_ref[:, hs]
            don = dyh * (og * sg)
            dog_ref[:, hs] = _bf(dyh * (ohat * nwv) * (sg * (1.0 + og * (1.0 - sg))))
            dnw_ref[...] += jnp.sum(don * ohat, axis=0, keepdims=True)
            do_s[:, hs] = _norm_bwd(don, ohat, r, nwv)
        dsts = [dstate[h] for h in range(HG_HEADS)]
        for c in reversed(range(ncs)):
            cs = slice(c * HG_CHUNK, (c + 1) * HG_CHUNK)
            for h in range(HG_HEADS):
                hs = slice(h * HG_DK, (h + 1) * HG_DK)
                dst = dsts[h]
                Gc = G[cs, hs]
                gl = Gc[HG_CHUNK - 1:HG_CHUNK]
                eG, enG, edG, egl = jnp.exp(Gc), jnp.exp(-Gc), jnp.exp(gl - Gc), jnp.exp(gl)
                qt, kt, kd = q[cs, hs] * eG, k[cs, hs] * enG, k[cs, hs] * edG
                qtb, ktb, kdb = _bf(qt), _bf(kt), _bf(kd)
                v = _bf(i_ref[cs, hs])
                do = _bf(do_s[cs, hs])
                st = st_ref[c, h]
                dstb = _bf(dst)
                A = jnp.where(row >= col, _dot(qtb, ktb, NT), 0.0)
                dA = _bf(jnp.where(row >= col, _dot(do, v, NT), 0.0))
                di_ref[cs, hs] = _bf(_dot(_bf(A), do, TN) + _dot(kdb, dstb, NT))
                dqt = _dot(dA, ktb, NN) + _dot(do, _bf(st), NN)
                dkt = _dot(dA, qtb, TN)
                dkd = _dot(v, dstb, NN)
                dgl = egl * jnp.sum(st * dst, axis=0, keepdims=True) + jnp.sum(dkd * kd, axis=0, keepdims=True)
                dsts[h] = dst * egl + _dot(do, qtb, TN)
                dq_ref[cs, hs] = _bf(dqt * eG * (sq[cs, hs] * (1.0 + qr[cs, hs] * (1.0 - sq[cs, hs]))))
                dk_s[cs, hs] = dkt * enG + dkd * edG
                dG_s[cs, hs] = dqt * qt - dkt * kt - dkd * kd
                dgl_s[cs, hs] = jnp.broadcast_to(dgl, (HG_CHUNK, HG_DK))
        for h in range(HG_HEADS):
            dstate[h] = dsts[h]
        dg = _chunk_rev_cumsum(dG_s[...]) + dgl_s[...]
        dfv = dg / f - dk_s[...]
        df_ref[...] = _bf(dfv * (1.0 - lb) * sf * (1.0 - sf))
        dlb_s[...] += jnp.sum(dfv * (1.0 - sf), axis=0, keepdims=True)

        @pl.when(step == nt - 1)
        def _():
            t = dlb_s[...] * lb * (1.0 - lb)
            dlb_ref[...] = jnp.concatenate([t, -t], axis=0)

    def colspec(j):
        return pl.BlockSpec((HG_TILE, HG_W), lambda i: (nt - 1 - i, HG_COL0 + j))

    tile = pl.BlockSpec((HG_TILE, HG_W), lambda i: (nt - 1 - i, 0))
    outs = pl.pallas_call(
        body, name=name, grid=(nt,),
        in_specs=[colspec(0), colspec(1), colspec(2), colspec(3),
                  pl.BlockSpec((2, HG_W), lambda i: (0, 0)), pl.BlockSpec((1, HG_DK), lambda i: (0, 0)),
                  tile, pl.BlockSpec((ncs, HG_HEADS, HG_DK, HG_DK), lambda i: (nt - 1 - i, 0, 0, 0)), tile],
        out_specs=[tile] * 4 + [pl.BlockSpec((2, HG_W), lambda i: (0, 0)), pl.BlockSpec((1, HG_DK), lambda i: (0, 0))],
        out_shape=[_sds((S, HG_W), BF16)] * 4 + [_sds((2, HG_W), F32), _sds((1, HG_DK), F32)],
        scratch_shapes=[pltpu.VMEM((HG_HEADS, HG_DK, HG_DK), F32)] + [pltpu.VMEM((HG_TILE, HG_W), F32)] * 4
        + [pltpu.VMEM((1, HG_W), F32)],
        compiler_params=_cp(1))(proj, proj, proj, proj, lb_raw, nw, o, states, dy)
    return outs[:4], outs[4], outs[5]


GATE_COL0 = (QKV_W + 4 * HG_W) // GROUP_W
HALF_D = D_MODEL // 2


def _gate_tiles(proj):
    return [_tile(proj, HALF_D, functools.partial(lambda c, k: GATE_COL0 + k, k=k)) for k in range(4)]


def _gates(g_refs):
    s0 = _sigmoid(jnp.concatenate([g_refs[0][...], g_refs[1][...]], axis=1))
    s1 = _sigmoid(jnp.concatenate([g_refs[2][...], g_refs[3][...]], axis=1))
    return s0, s1


def _branch_fwd(y, yh, proj, w_a, w_h, name):
    nb = w_a.shape[0]

    def body(y_ref, yh_ref, g0a, g0b, g1a, g1b, wa_ref, wh_ref, za_ref, zh_ref, m_ref):
        yb, yhb = _bf(y_ref[...]), _bf(yh_ref[...])
        za = jnp.concatenate([_dot(yb, wa_ref[j], NN) for j in range(nb)], axis=1)
        zh = jnp.concatenate([_dot(yhb, wh_ref[j], NN) for j in range(nb)], axis=1)
        s0, s1 = _gates((g0a, g0b, g1a, g1b))
        za_ref[...] = za
        zh_ref[...] = zh
        m_ref[...] = _bf(s0 * za + s1 * zh)

    return _rows_call(name, body, y.shape[0], 512, 1,
                      [_tile(y, GROUP_W), _tile(yh, HG_W), *_gate_tiles(proj), _full(w_a), _full(w_h)],
                      [_out_tile(D_MODEL, F32, D_MODEL)] * 2 + [_out_tile(D_MODEL, BF16, D_MODEL)])


def _branch_bwd(dm, za, zh, proj, w_a, w_h, name):
    nb, _, Nb = w_a.shape

    def body(dm_ref, za_ref, zh_ref, g0a, g0b, g1a, g1b, wa_ref, wh_ref,
             dza_ref, dzh_ref, dg0_ref, dg1_ref, dy_ref, dyh_ref):
        dmv = dm_ref[...]
        s0, s1 = _gates((g0a, g0b, g1a, g1b))
        dza, dzh = _bf(dmv * s0), _bf(dmv * s1)
        dza_ref[...] = dza
        dzh_ref[...] = dzh
        dg0_ref[...] = _bf(dmv * za_ref[...] * s0 * (1.0 - s0))
        dg1_ref[...] = _bf(dmv * zh_ref[...] * s1 * (1.0 - s1))
        dy_ref[...] = sum(_dot(dza[:, j * Nb:(j + 1) * Nb], wa_ref[j], NT) for j in range(nb))
        dyh_ref[...] = sum(_dot(dzh[:, j * Nb:(j + 1) * Nb], wh_ref[j], NT) for j in range(nb))

    return _rows_call(name, body, za.shape[0], 512, 1,
                      [_tile(dm, D_MODEL), _tile(za, D_MODEL), _tile(zh, D_MODEL), *_gate_tiles(proj),
                       _full(w_a), _full(w_h)],
                      [_out_tile(D_MODEL, BF16, D_MODEL)] * 4 + [_out_tile(GROUP_W, F32, GROUP_W),
                                                                 _out_tile(HG_W, F32, HG_W)])


def _mix_out(merged, w_out, x, w_post, w_pre, name):
    def body(m_ref, wo_ref, x_ref, wp_ref, wf_ref, mo_ref, x1_ref, h2_ref):
        z = _dot(m_ref[...], wo_ref[...], NN)
        mo_ref[...] = z
        x1 = x_ref[...] + z * _rinv(z) * wp_ref[...]
        x1_ref[...] = x1
        h2_ref[...] = _bf(x1 * _rinv(x1) * wf_ref[...])

    return _rows_call(name, body, x.shape[0], 512, 1,
                      [_tile(merged, D_MODEL), _full(w_out), _tile(x, D_MODEL), _full(w_post), _full(w_pre)],
                      [_out_tile(D_MODEL, F32, D_MODEL), _out_tile(D_MODEL, F32, D_MODEL),
                       _out_tile(D_MODEL, BF16, D_MODEL)])


def _loss_head(a, w_down, x1, tgt, w, name):
    def body(a_ref, wd_ref, x1_ref, t_ref, w_ref, dx_ref, df_ref, dw_ref, loss_ref):
        z = _dot(a_ref[...], wd_ref[...], NN)
        r = _rinv(z)
        zhat = z * r
        wv = w_ref[...]
        e = x1_ref[...] + zhat * wv - t_ref[...]
        dx = e * (1.0 / D_MODEL)
        dx_ref[...] = dx
        df_ref[...] = _bf(_norm_bwd(dx, zhat, r, wv))
        _acc(dw_ref, jnp.sum(dx * zhat, axis=0, keepdims=True))
        part = 0.5 * jnp.sum(jnp.sum(e * e, axis=1, keepdims=True), axis=0, keepdims=True) * (1.0 / D_MODEL)
        _acc(loss_ref, jnp.broadcast_to(part, (1, LANES)))

    return _rows_call(name, body, x1.shape[0], 512, 1,
                      [_tile(a, D_FF), _full(w_down), _tile(x1, D_MODEL), _tile(tgt, D_MODEL), _full(w)],
                      [_out_tile(D_MODEL, F32, D_MODEL), _out_tile(D_MODEL, BF16, D_MODEL),
                       _out_acc(1, D_MODEL, D_MODEL), _out_acc(1, LANES, LANES)])


CONV_CB = D_FF // 2
CONV_TM = 512
HALO = 8
SQRT_HALF = 0.7071067811865476
INV_SQRT_2PI = 0.3989422804014327


CONV_RS = 32


def _lane_tiles():
    return [slice(k * LANES, (k + 1) * LANES) for k in range(CONV_CB // LANES)]


def _strip_start(i):
    return pl.multiple_of(i * CONV_RS, CONV_RS)


def _strip_taps(u_ref, halo_ref, r0, cs, first_strip, first_tile):
    if first_strip:
        before = jnp.where(first_tile, 0.0, halo_ref[:, cs])
        blk = jnp.concatenate([before, u_ref[0:CONV_RS, cs]], axis=0)
    else:
        blk = u_ref[pl.ds(pl.multiple_of(r0 - HALO, HALO), CONV_RS + HALO), cs]
    return pltpu.roll(blk, 2, 0)[HALO:], pltpu.roll(blk, 1, 0)[HALO:], blk[HALO:]


def _conv(taps, w_ref, b_ref, cs):
    return b_ref[:, cs] + w_ref[0:1, cs] * taps[0] + w_ref[1:2, cs] * taps[1] + w_ref[2:3, cs] * taps[2]


def _conv_specs(tm):
    nh = tm // HALO
    nc = D_FF // CONV_CB

    def tile(off):
        return pl.BlockSpec((tm, CONV_CB), lambda c, i: (i, off + c))

    def halo(off):
        return pl.BlockSpec((HALO, CONV_CB), lambda c, i: (jnp.maximum(i * nh - 1, 0), off + c))

    def small(rows, off):
        return pl.BlockSpec((rows, CONV_CB), lambda c, i: (0, off + c))

    return nc, tile, halo, small


def _conv_gelu_fwd(u, cw, cb, name):
    S = u.shape[0]
    tm = CONV_TM
    nc, tile, halo, small = _conv_specs(tm)

    def body(ug, hg, uv, hv, wg, wv, bg, bv, a_ref):
        first_tile = pl.program_id(1) == 0

        def strip(r0, first_strip):
            for cs in _lane_tiles():
                cg = _conv(_strip_taps(ug, hg, r0, cs, first_strip, first_tile), wg, bg, cs)
                cv = _conv(_strip_taps(uv, hv, r0, cs, first_strip, first_tile), wv, bv, cs)
                a_ref[pl.ds(r0, CONV_RS), cs] = _bf(0.5 * cg * (1.0 + lax.erf(cg * SQRT_HALF)) * cv)

        strip(0, True)
        lax.fori_loop(1, tm // CONV_RS, lambda k, c: (strip(_strip_start(k), False), c)[1], 0)

    return pl.pallas_call(
        body, name=name, grid=(nc, S // tm),
        in_specs=[tile(0), halo(0), tile(nc), halo(nc), small(3, 0), small(3, nc), small(1, 0), small(1, nc)],
        out_specs=tile(0), out_shape=_sds((S, D_FF), BF16), compiler_params=_cp(2))(u, u, u, u, cw, cw, cb, cb)


def _conv_gelu_bwd(u, da, cw, cb, name, plans=None):
    S = u.shape[0]
    tm = CONV_TM
    nt = S // tm
    nc, tile, halo, small = _conv_specs(tm)

    def body(ug, hg, uv, hv, wg, wv, bg, bv, da_ref, dcg_ref, dcv_ref, dwg_ref, dwv_ref, dbg_ref, dbv_ref, acc):
        i = pl.program_id(1)
        first_tile = i == 0

        @pl.when(first_tile)
        def _():
            acc[...] = jnp.zeros_like(acc)

        def strip(r0, first_strip):
            rows = pl.ds(r0, CONV_RS)
            for cs in _lane_tiles():
                tg = _strip_taps(ug, hg, r0, cs, first_strip, first_tile)
                tv = _strip_taps(uv, hv, r0, cs, first_strip, first_tile)
                cg = _conv(tg, wg, bg, cs)
                cv = _conv(tv, wv, bv, cs)
                phi = 0.5 * (1.0 + lax.erf(cg * SQRT_HALF))
                dav = da_ref[rows, cs]
                dcg = dav * cv * (phi + cg * jnp.exp(-0.5 * cg * cg) * INV_SQRT_2PI)
                dcv = dav * (cg * phi)
                dcg_ref[rows, cs] = dcg
                dcv_ref[rows, cs] = dcv
                for half, (dc, taps) in enumerate(((dcg, tg), (dcv, tv))):
                    for j in range(3):
                        acc[4 * half + j, :, cs] += dc * taps[j]
                    acc[4 * half + 3, :, cs] += dc

        strip(0, True)
        lax.fori_loop(1, tm // CONV_RS, lambda k, c: (strip(_strip_start(k), False), c)[1], 0)

        @pl.when(i == nt - 1)
        def _():
            for half, (dw_ref, db_ref) in enumerate(((dwg_ref, dbg_ref), (dwv_ref, dbv_ref))):
                for j in range(3):
                    dw_ref[j:j + 1, :] = jnp.sum(acc[4 * half + j], axis=0, keepdims=True)
                db_ref[...] = jnp.sum(acc[4 * half + 3], axis=0, keepdims=True)

    res, carried = _call(
        body, plans, name=name, grid=(nc, nt),
        in_specs=[tile(0), halo(0), tile(nc), halo(nc), small(3, 0), small(3, nc), small(1, 0), small(1, nc), tile(0)],
        out_specs=[tile(0), tile(0), small(3, 0), small(3, 0), small(1, 0), small(1, 0)],
        out_shape=[_sds((S, D_FF), F32)] * 2 + [_sds((3, D_FF), F32)] * 2 + [_sds((1, D_FF), F32)] * 2,
        scratch_shapes=[pltpu.VMEM((8, CONV_RS, CONV_CB), F32)], args=(u, u, u, u, cw, cw, cb, cb, da))
    return res if plans is None else (res, carried)


def _conv_input_bwd(dcg, dcv, cw, name, plans=None):
    S = dcg.shape[0]
    tm = CONV_TM
    nc, tile, _, small = _conv_specs(tm)
    nh = tm // HALO
    nt = S // tm
    n = CONV_RS + HALO

    def nxt(off):
        return pl.BlockSpec((HALO, CONV_CB), lambda c, i: (jnp.minimum((i + 1) * nh, S // HALO - 1), off + c))

    def body(g_ref, ng_ref, v_ref, nv_ref, wg, wv, dug_ref, duv_ref):
        last_tile = pl.program_id(1) == nt - 1

        def strip(r0, last_strip):
            for cs in _lane_tiles():
                for dc_ref, n_ref, w_ref, du_ref in ((g_ref, ng_ref, wg, dug_ref), (v_ref, nv_ref, wv, duv_ref)):
                    if last_strip:
                        after = jnp.where(last_tile, 0.0, n_ref[:, cs])
                        blk = jnp.concatenate([dc_ref[tm - CONV_RS:tm, cs], after], axis=0)
                    else:
                        blk = dc_ref[pl.ds(r0, n), cs]
                    d1 = pltpu.roll(blk, n - 1, 0)[:CONV_RS]
                    d2 = pltpu.roll(blk, n - 2, 0)[:CONV_RS]
                    du_ref[pl.ds(r0, CONV_RS), cs] = _bf(w_ref[2:3, cs] * blk[:CONV_RS] + w_ref[1:2, cs] * d1
                                                         + w_ref[0:1, cs] * d2)

        lax.fori_loop(0, tm // CONV_RS - 1, lambda k, c: (strip(_strip_start(k), False), c)[1], 0)
        strip(tm - CONV_RS, True)

    res, carried = _call(
        body, plans, name=name, grid=(nc, nt),
        in_specs=[tile(0), nxt(0), tile(0), nxt(0), small(3, 0), small(3, nc)],
        out_specs=[tile(0), tile(0)], out_shape=[_sds((S, D_FF), BF16)] * 2,
        args=(dcg, dcg, dcv, dcv, cw, cw))
    return res if plans is None else (res, carried)


def _row_tile(n, cap):
    best = n
    for t in range(16, cap + 1, 16):
        if n % t == 0:
            best = t
    return best if best <= cap else n


def _rows_for_bytes(nbytes, cols):
    return max(16, nbytes // (4 * cols) // 16 * 16)


def _adamw(w, g, m, v, name):
    R, C = w.shape
    tr = _row_tile(R, _rows_for_bytes(2 << 20, C))

    def body(w_ref, g_ref, m_ref, v_ref, d_ref, nm_ref, nv_ref):
        gv = g_ref[...]
        nm = ADAM_B1 * m_ref[...] + (1.0 - ADAM_B1) * gv
        nv = ADAM_B2 * v_ref[...] + (1.0 - ADAM_B2) * (gv * gv)
        m_hat = nm / (1.0 - ADAM_B1 ** ADAM_STEP)
        v_hat = nv / (1.0 - ADAM_B2 ** ADAM_STEP)
        d_ref[...] = -ADAM_LR * (m_hat / (jnp.sqrt(v_hat) + ADAM_EPS) + ADAM_WD * w_ref[...])
        nm_ref[...] = nm
        nv_ref[...] = nv

    spec = pl.BlockSpec((tr, C), lambda i: (i, 0))
    return pl.pallas_call(body, name=name, grid=(R // tr,), in_specs=[spec] * 4, out_specs=[spec] * 3,
                          out_shape=[_sds((R, C), F32)] * 3, compiler_params=_cp(1))(w, g, m, v)


def _pair_sum(gfull, rcv, c_idx, name):
    nb, R, C = gfull.shape
    half = R // 2
    tr = _row_tile(half, _rows_for_bytes(2 << 20, C))
    nt = half // tr

    def body(c_ref, g_ref, r_ref, o_ref):
        o_ref[...] = _bf(g_ref[...] + r_ref[...])

    return pl.pallas_call(
        body, name=name,
        grid_spec=pltpu.PrefetchScalarGridSpec(
            num_scalar_prefetch=1, grid=(nb, nt),
            in_specs=[pl.BlockSpec((None, tr, C), lambda j, i, c_ref: (j, c_ref[0] * nt + i, 0)),
                      pl.BlockSpec((None, tr, C), lambda j, i, c_ref: (j, i, 0))],
            out_specs=pl.BlockSpec((None, tr, C), lambda j, i, c_ref: (j, i, 0))),
        out_shape=_sds((nb, half, C), BF16), compiler_params=_cp(2))(c_idx, gfull, rcv)


def _chip_sum(arrived, own, place, name):
    nb, H, C = arrived.shape
    tr = _row_tile(H, _rows_for_bytes(2 << 20, C))
    nt = H // tr

    def body(pl_ref, *refs):
        o_ref = refs[nb + 1]
        me = pl_ref[0]
        acc = None
        for k in range(nb):
            term = jnp.where(me == k, refs[nb][...], refs[k][...]).astype(F32)
            acc = term if acc is None else acc + term
        o_ref[...] = acc

    def other(k):
        return pl.BlockSpec((None, tr, C), lambda i, p: (jnp.where(p[0] == k, (k + 1) % nb, k), i, 0))

    return pl.pallas_call(
        body, name=name,
        grid_spec=pltpu.PrefetchScalarGridSpec(
            num_scalar_prefetch=1, grid=(nt,),
            in_specs=[other(k) for k in range(nb)] + [pl.BlockSpec((None, tr, C), lambda i, p: (p[0], i, 0))],
            out_specs=pl.BlockSpec((tr, C), lambda i, p: (p[1] * nt + i, 0))),
        out_shape=_sds((2 * H, C), F32), compiler_params=_cp(1))(place, *([arrived] * nb), own)


def _cast_into_slot(shard, place, name):
    R, C = shard.shape
    tr = _row_tile(R, 256)

    def body(pl_ref, s_ref, o_ref):
        o_ref[...] = _bf(s_ref[...])

    return pl.pallas_call(
        body, name=name,
        grid_spec=pltpu.PrefetchScalarGridSpec(
            num_scalar_prefetch=1, grid=(R // tr,),
            in_specs=[pl.BlockSpec((tr, C), lambda i, p: (i, 0))],
            out_specs=pl.BlockSpec((None, tr, C), lambda i, p: (p[0], i, 0))),
        out_shape=_sds((N_CHIPS, R, C), BF16), compiler_params=_cp(1))(place, shard)


def _place():
    x, y, c = lax.axis_index("x"), lax.axis_index("y"), lax.axis_index("c")
    chips = [(1 - x, y), (x, 1 - y), (1 - x, 1 - y)]
    return x, y, c, chips


def _chip_id(px, py):
    return 2 * px + py


def _remote(src, dst, send_sems, recv_sems, k, to):
    return pltpu.make_async_remote_copy(src_ref=src, dst_ref=dst, send_sem=send_sems.at[k], recv_sem=recv_sems.at[k],
                                        device_id=to, device_id_type=MESH)


def _gather_weights(slots, wholes, name):
    ns, nw = len(slots), len(wholes)
    n = ns + nw

    def body(*refs):
        ins = refs[ns:n]
        outs = refs[n:2 * n]
        send_sems, recv_sems, local_sems = refs[2 * n:]
        x, y, c, chips = _place()
        me = _chip_id(x, y)
        sib = (x, y, 1 - c)
        local = [pltpu.make_async_copy(ins[b], outs[ns + b].at[me], local_sems.at[b]) for b in range(nw)]
        for cp in local:
            cp.start()
        sent = []
        for a in range(n):
            R = outs[a].shape[1]
            rows = pl.ds(c * (R // 2), R // 2) if a < ns else pl.ds(0, R)
            src = outs[a].at[me, rows] if a < ns else ins[a - ns]
            for j, chip in enumerate(chips):
                cp = _remote(src, outs[a].at[me, rows], send_sems, recv_sems, 6 * a + j, (*chip, c))
                cp.start()
                sent.append(cp)
        for a in range(n):
            R = outs[a].shape[1]
            rows = pl.ds(c * (R // 2), R // 2) if a < ns else pl.ds(0, R)
            for j, chip in enumerate(chips):
                landed = outs[a].at[_chip_id(*chip), rows]
                _remote(landed, landed, send_sems, recv_sems, 6 * a + j, (*chip, c)).wait_recv()
                if a < ns:
                    cp = _remote(landed, landed, send_sems, recv_sems, 6 * a + 3 + j, sib)
                    cp.start()
                    sent.append(cp)
        for a in range(ns):
            R = outs[a].shape[1]
            other = pl.ds((1 - c) * (R // 2), R // 2)
            for j, chip in enumerate(chips):
                passed = outs[a].at[_chip_id(*chip), other]
                _remote(passed, passed, send_sems, recv_sems, 6 * a + 3 + j, sib).wait_recv()
        for cp in sent:
            cp.wait_send()
        for cp in local:
            cp.wait()

    return pl.pallas_call(
        body, name=name, in_specs=[ANY] * n, out_specs=[ANY] * n,
        out_shape=[_sds(s.shape, s.dtype) for s in slots] + [_sds((N_CHIPS, *s.shape), s.dtype) for s in wholes],
        input_output_aliases={a: a for a in range(ns)},
        scratch_shapes=[pltpu.SemaphoreType.DMA((6 * n,)), pltpu.SemaphoreType.DMA((6 * n,)),
                        pltpu.SemaphoreType.DMA((max(nw, 1),))])(*slots, *wholes)


def _gather_ici_plan(slots, wholes):
    ns, nw = len(slots), len(wholes)

    def copies(ins, ios, outs, send_sems, recv_sems, local_sems):
        x, y, c, chips = _place()
        me = _chip_id(x, y)
        sends, recvs = [], []
        for a in range(ns + nw):
            dst = ios[a] if a < ns else outs[a - ns]
            R = dst.shape[1]
            rows = pl.ds(c * (R // 2), R // 2) if a < ns else pl.ds(0, R)
            src = dst.at[me, rows] if a < ns else ins[a - ns]
            for j, chip in enumerate(chips):
                sends.append(_remote(src, dst.at[me, rows], send_sems, recv_sems, 3 * a + j, (*chip, c)))
                landed = dst.at[_chip_id(*chip), rows]
                recvs.append(_remote(landed, landed, send_sems, recv_sems, 3 * a + j, (*chip, c)))
        local = [pltpu.make_async_copy(ins[b], outs[b].at[me], local_sems.at[b]) for b in range(nw)]
        return sends, recvs, local

    return _Plan(copies, 3 * (ns + nw), ins=wholes, inouts=slots,
                 outs=[_sds((N_CHIPS, *s.shape), s.dtype) for s in wholes])


def _gather_pass_plan(slots):
    def copies(ins, ios, outs, send_sems, recv_sems, local_sems):
        x, y, c, chips = _place()
        sib = (x, y, 1 - c)
        sends, recvs = [], []
        for a, buf in enumerate(ios):
            half = buf.shape[1] // 2
            for j, chip in enumerate(chips):
                mine = buf.at[_chip_id(*chip), pl.ds(c * half, half)]
                other = buf.at[_chip_id(*chip), pl.ds((1 - c) * half, half)]
                sends.append(_remote(mine, mine, send_sems, recv_sems, 3 * a + j, sib))
                recvs.append(_remote(other, other, send_sems, recv_sems, 3 * a + j, sib))
        return sends, recvs, []

    return _Plan(copies, 3 * len(slots), inouts=slots)


def _pair_plan(grads):
    def copies(ins, ios, outs, send_sems, recv_sems, local_sems):
        x, y, c, _ = _place()
        sib = (x, y, 1 - c)
        sends, recvs = [], []
        for a, g in enumerate(ins):
            half = g.shape[1] // 2
            sends.append(_remote(g.at[:, pl.ds((1 - c) * half, half), :], outs[a], send_sems, recv_sems, a, sib))
            recvs.append(_remote(outs[a], outs[a], send_sems, recv_sems, a, sib))
        return sends, recvs, []

    return _Plan(copies, len(grads), ins=grads,
                 outs=[_sds((g.shape[0], g.shape[1] // 2, g.shape[2]), g.dtype) for g in grads])


def _chip_plan(parts):
    def copies(ins, ios, outs, send_sems, recv_sems, local_sems):
        x, y, c, chips = _place()
        me = _chip_id(x, y)
        sends, recvs = [], []
        for a, part in enumerate(ins):
            for j, chip in enumerate(chips):
                sends.append(_remote(part.at[_chip_id(*chip)], outs[a].at[me], send_sems, recv_sems, 3 * a + j, (*chip, c)))
                landed = outs[a].at[_chip_id(*chip)]
                recvs.append(_remote(landed, landed, send_sems, recv_sems, 3 * a + j, (*chip, c)))
        return sends, recvs, []

    return _Plan(copies, 3 * len(parts), ins=parts, outs=[_sds(p.shape, p.dtype) for p in parts])


def _pair_concat(fulls, name):
    n = len(fulls)

    def body(*refs):
        outs = refs[n:2 * n]
        send_sems, recv_sems = refs[2 * n:]
        x, y, c, _ = _place()
        cps = []
        for a in range(n):
            H = outs[a].shape[0] // 2
            mine = outs[a].at[pl.ds(c * H, H)]
            cp = _remote(mine, mine, send_sems, recv_sems, a, (x, y, 1 - c))
            cp.start()
            cps.append(cp)
        for a, cp in enumerate(cps):
            H = outs[a].shape[0] // 2
            other = outs[a].at[pl.ds((1 - c) * H, H)]
            _remote(other, other, send_sems, recv_sems, a, (x, y, 1 - c)).wait_recv()
            cp.wait_send()

    return pl.pallas_call(
        body, name=name, in_specs=[ANY] * n, out_specs=[ANY] * n,
        out_shape=[_sds(f.shape, f.dtype) for f in fulls], input_output_aliases={a: a for a in range(n)},
        scratch_shapes=[pltpu.SemaphoreType.DMA((n,)), pltpu.SemaphoreType.DMA((n,))])(*fulls)


def _all_sum(pack, name):
    R, C = pack.shape

    def body(p_ref, o_ref, buf, send_sems, recv_sems):
        x, y, c, _ = _place()
        me = 4 * x + 2 * y + c
        buf[me] = p_ref[...]
        cps = []
        for k in range(1, N_DEV):
            to = (x ^ (k >> 2), y ^ ((k >> 1) & 1), c ^ (k & 1))
            cp = _remote(p_ref, buf.at[me], send_sems, recv_sems, k - 1, to)
            cp.start()
            cps.append(cp)
        for k in range(1, N_DEV):
            frm = (x ^ (k >> 2), y ^ ((k >> 1) & 1), c ^ (k & 1))
            slot = buf.at[4 * frm[0] + 2 * frm[1] + frm[2]]
            _remote(slot, slot, send_sems, recv_sems, k - 1, frm).wait_recv()
        acc = buf[0]
        for k in range(1, N_DEV):
            acc = acc + buf[k]
        o_ref[...] = acc
        for cp in cps:
            cp.wait_send()

    vm = pl.BlockSpec(memory_space=pltpu.VMEM)
    return pl.pallas_call(
        body, name=name, in_specs=[vm], out_specs=vm, out_shape=_sds((R, C), F32),
        scratch_shapes=[pltpu.VMEM((N_DEV, R, C), F32), pltpu.SemaphoreType.DMA((N_DEV - 1,)),
                        pltpu.SemaphoreType.DMA((N_DEV - 1,))])(pack)


def _local_step(xs, tgt, p, ex):
    h1 = _norm_fwd(xs, p["pre_mix_norm"], "pre_mix_norm")
    proj, got = _mm_nn_blk(h1, ex.weight("w_in"), "proj_in", plans=ex.carry("proj_in"))
    ex.done("proj_in", got)
    biases = _relbias_fwd(p["rel_bias"], "rel_bias_fwd")
    fw = []
    for g in range(N_GROUPS):
        res, got = _attn_fwd(proj, biases[g], g, f"attn_fwd{g}", plans=ex.carry(f"attn_fwd{g}"))
        ex.done(f"attn_fwd{g}", got)
        fw.append(res)
    y, lse = _attn_merge([t[0] for t in fw], [t[1] for t in fw], "attn_merge")
    yh, o_h, states = _hgrn_fwd(proj, p["hgrn_lb_raw"], p["hgrn_norm"], "hgrn_fwd")
    W_a, W_h, W_out = ex.weight("w_branch_attn"), ex.weight("w_branch_hgrn"), ex.weight("w_out")
    W_up, W_down, conv_w = ex.weight("w_up"), ex.weight("w_down"), ex.weight("conv_w")
    za, zh, merged = _branch_fwd(y, yh, proj, W_a, W_h, "branch_fwd")
    mo, x1, h2 = _mix_out(merged, W_out, xs, p["post_mix_norm"], p["pre_ffn_norm"], "mix_out")
    u = _mm_nn_blk(h2, W_up, "ffn_up")
    a = _conv_gelu_fwd(u, conv_w, p["conv_b"], "conv_gelu_fwd")
    dx2, dff, g_post_ffn, loss = _loss_head(a, W_down, x1, tgt, p["post_ffn_norm"], "ffn_down_loss")

    da = _mm_nt(dff, W_down, "d_ffn_act")
    ex.grad("w_down", _mm_tn(a, dff, "g_w_down").reshape(N_CHIPS, D_FF // N_CHIPS, D_MODEL))
    (dcg, dcv, gwg, gwv, gbg, gbv), got = _conv_gelu_bwd(u, da, conv_w, p["conv_b"], "conv_gelu_bwd",
                                                          plans=ex.carry("conv_gelu_bwd"))
    ex.done("conv_gelu_bwd", got)
    g_conv_w = jnp.concatenate([gwg, gwv], axis=1)
    g_conv_b = jnp.concatenate([gbg, gbv], axis=1)
    du_parts, got = _conv_input_bwd(dcg, dcv, conv_w, "conv_input_bwd", plans=ex.carry("conv_input_bwd"))
    ex.done("conv_input_bwd", got)
    du = jnp.concatenate(du_parts, axis=1)
    dh2 = _mm_nt_blk(du, W_up, "d_ffn_in")
    ex.grad("w_up", _mm_tn_blk(h2, du, N_CHIPS, "g_w_up"))
    (dx1, g_pre_ffn), got = _prenorm_bwd(dh2, x1, p["pre_ffn_norm"], dx2, "pre_ffn_norm_bwd",
                                         plans=ex.carry("pre_ffn_norm_bwd"))
    ex.done("pre_ffn_norm_bwd", got)
    dmo, dmerged, g_post_mix = _postnorm_bwd(dx1, mo, p["post_mix_norm"], W_out, "post_mix_norm_bwd")
    ex.grad("w_out", _mm_tn(merged, dmo, "g_w_out").reshape(N_CHIPS, D_MODEL // N_CHIPS, D_MODEL))
    dza, dzh, dg0, dg1, dy, dyh = _branch_bwd(dmerged, za, zh, proj, W_a, W_h, "branch_bwd")
    ex.grad("w_branch_attn", _mm_tn_blk(y, dza, N_CHIPS, "g_w_branch_attn", together=True))
    ex.grad("w_branch_hgrn", _mm_tn_blk(yh, dzh, N_CHIPS, "g_w_branch_hgrn", together=True))
    dqkv, dbs = [], []
    for g in range(N_GROUPS):
        parts, db, got = _attn_bwd(proj, biases[g], lse, y, dy, g, f"attn_bwd{g}", plans=ex.carry(f"attn_bwd{g}"))
        ex.done(f"attn_bwd{g}", got)
        dqkv += parts
        dbs.append(db)
    g_rel_bias = _relbias_bwd(dbs, "rel_bias_bwd")
    dhg, g_lb_raw, g_hgrn_norm = _hgrn_bwd(proj, p["hgrn_lb_raw"], p["hgrn_norm"], o_h, states, dyh, "hgrn_bwd")
    dproj = jnp.concatenate([*[_bf(t) for t in dqkv], *dhg, dg0, dg1], axis=1)
    for piece in W_IN_PIECES:
        g, got = _mm_tn_blk(h1, dproj, N_CHIPS, f"g_{piece}", x_cols=W_IN_ROWS[piece],
                            plans=ex.carry(f"g_{piece}"))
        ex.done(f"g_{piece}", got)
        ex.grad(piece, g)
    dh1, got = _mm_nt_blk(dproj, ex.weight("w_in"), "d_proj_in", plans=ex.carry("d_proj_in"))
    ex.done("d_proj_in", got)
    (grad_x, g_pre_mix), got = _prenorm_bwd(dh1, xs, p["pre_mix_norm"], dx1, "pre_mix_norm_bwd",
                                            plans=ex.carry("pre_mix_norm_bwd"))
    ex.done("pre_mix_norm_bwd", got)
    small = dict(pre_mix_norm=g_pre_mix, rel_bias=g_rel_bias, hgrn_lb_raw=g_lb_raw, hgrn_norm=g_hgrn_norm,
                 post_mix_norm=g_post_mix, pre_ffn_norm=g_pre_ffn, conv_w=g_conv_w, conv_b=g_conv_b,
                 post_ffn_norm=g_post_ffn)
    return loss, grad_x, small


SMALL = ("pre_mix_norm", "rel_bias", "hgrn_lb_raw", "hgrn_norm", "post_mix_norm", "pre_ffn_norm", "conv_w", "conv_b",
         "post_ffn_norm")
BIG = ("w_in", "w_up", "w_down", "w_out", "w_branch_attn", "w_branch_hgrn")
WEIGHTS = ("pre_mix_norm", "w_in", "rel_bias", "hgrn_lb_raw", "hgrn_norm", "w_branch_attn", "w_branch_hgrn", "w_out",
           "post_mix_norm", "pre_ffn_norm", "w_up", "conv_w", "conv_b", "w_down", "post_ffn_norm")
MIXER = ("w_out", "w_branch_attn", "w_branch_hgrn")

SCHEDULE = {
    "proj_in": [("gather_ici_cw", ("w_up",) + MIXER)],
    "attn_fwd0": [("gather_pass", ("w_up",) + MIXER), ("gather_ici", ("w_down",))],
    "attn_fwd1": [("gather_pass", ("w_down",))],
    "conv_gelu_bwd": [("pair", ("w_down",))],
    "conv_input_bwd": [("chip", ("w_down",))],
    "pre_ffn_norm_bwd": [("pair", ("w_up",))],
    "attn_bwd0": [("chip", ("w_up",)), ("pair", MIXER)],
    "attn_bwd1": [("chip", MIXER)],
    "g_w_in_b": [("pair", ("w_in_a",))],
    "d_proj_in": [("chip", ("w_in_a",)), ("pair", ("w_in_b",))],
    "pre_mix_norm_bwd": [("chip", ("w_in_b",))],
}
W_IN_ROWS = dict(w_in_a=(0, 768), w_in_b=(3, 256))
W_IN_PIECES = tuple(W_IN_ROWS)
REDUCED = W_IN_PIECES + BIG[1:]


class _Exchange:
    def __init__(self, place, slots, conv_w_shard):
        self.place, self.slots, self.conv_w_shard = place, dict(slots), conv_w_shard
        self.conv_w = None
        self.g, self.from_sibling, self.pair_sums, self.arrived = {}, {}, {}, {}
        self.pending = []

    def weight(self, name):
        if name == "conv_w":
            return self.conv_w
        w = self.slots[name]
        return w.reshape(-1, D_MODEL) if name in ("w_out", "w_down") else w

    def grad(self, name, g):
        self.g[name] = g

    def carry(self, point):
        plans = []
        self.pending = SCHEDULE.get(point, [])
        for kind, names in self.pending:
            if kind in ("gather_ici", "gather_ici_cw"):
                wholes = [self.conv_w_shard] if kind == "gather_ici_cw" else []
                plans.append(_gather_ici_plan([self.slots[n] for n in names], wholes))
            elif kind == "gather_pass":
                plans.append(_gather_pass_plan([self.slots[n] for n in names]))
            elif kind == "pair":
                plans.append(_pair_plan([self.g[n] for n in names]))
            else:
                for n in names:
                    self.pair_sums[n] = _pair_sum(self.g[n], self.from_sibling[n], self.place[1:2], f"pair_sum_{n}")
                plans.append(_chip_plan([self.pair_sums[n] for n in names]))
        return plans

    def done(self, point, carried):
        for (kind, names), got in zip(self.pending, carried):
            if kind in ("gather_ici", "gather_ici_cw", "gather_pass"):
                self.slots.update(zip(names, got))
                if kind == "gather_ici_cw":
                    self.conv_w = got[len(names)].transpose(1, 0, 2).reshape(3, 2 * D_FF)
            elif kind == "pair":
                self.from_sibling.update(zip(names, got))
            else:
                self.arrived.update(zip(names, got))

    def reduced(self):
        halves = [_chip_sum(self.arrived[n], self.pair_sums[n], self.place, f"chip_sum_{n}") for n in REDUCED]
        out = dict(zip(REDUCED, _pair_concat(halves, "pair_concat")))
        out["w_in"] = jnp.concatenate([out.pop(n) for n in W_IN_PIECES], axis=0)
        return out


def kernel(x, pre_mix_norm, w_in, rel_bias, hgrn_lb_raw, hgrn_norm, w_branch_attn, w_branch_hgrn, w_out, post_mix_norm, pre_ffn_norm, w_up, conv_w, conv_b, w_down, post_ffn_norm, loss_target, m_pre_mix_norm, m_w_in, m_rel_bias, m_hgrn_lb_raw, m_hgrn_norm, m_w_branch_attn, m_w_branch_hgrn, m_w_out, m_post_mix_norm, m_pre_ffn_norm, m_w_up, m_conv_w, m_conv_b, m_w_down, m_post_ffn_norm, v_pre_mix_norm, v_w_in, v_rel_bias, v_hgrn_lb_raw, v_hgrn_norm, v_w_branch_attn, v_w_branch_hgrn, v_w_out, v_post_mix_norm, v_pre_ffn_norm, v_w_up, v_conv_w, v_conv_b, v_w_down, v_post_ffn_norm):
    w = dict(pre_mix_norm=pre_mix_norm, w_in=w_in, rel_bias=rel_bias, hgrn_lb_raw=hgrn_lb_raw, hgrn_norm=hgrn_norm,
             w_branch_attn=w_branch_attn, w_branch_hgrn=w_branch_hgrn, w_out=w_out, post_mix_norm=post_mix_norm,
             pre_ffn_norm=pre_ffn_norm, w_up=w_up, conv_w=conv_w, conv_b=conv_b, w_down=w_down,
             post_ffn_norm=post_ffn_norm)
    m = dict(pre_mix_norm=m_pre_mix_norm, w_in=m_w_in, rel_bias=m_rel_bias, hgrn_lb_raw=m_hgrn_lb_raw,
             hgrn_norm=m_hgrn_norm, w_branch_attn=m_w_branch_attn, w_branch_hgrn=m_w_branch_hgrn, w_out=m_w_out,
             post_mix_norm=m_post_mix_norm, pre_ffn_norm=m_pre_ffn_norm, w_up=m_w_up, conv_w=m_conv_w,
             conv_b=m_conv_b, w_down=m_w_down, post_ffn_norm=m_post_ffn_norm)
    v = dict(pre_mix_norm=v_pre_mix_norm, w_in=v_w_in, rel_bias=v_rel_bias, hgrn_lb_raw=v_hgrn_lb_raw,
             hgrn_norm=v_hgrn_norm, w_branch_attn=v_w_branch_attn, w_branch_hgrn=v_w_branch_hgrn, w_out=v_w_out,
             post_mix_norm=v_post_mix_norm, pre_ffn_norm=v_pre_ffn_norm, w_up=v_w_up, conv_w=v_conv_w,
             conv_b=v_conv_b, w_down=v_w_down, post_ffn_norm=v_post_ffn_norm)
    shard2d = {n: (w[n][0] if w[n].ndim == 3 else w[n]) for n in WEIGHTS}
    chip = 2 * lax.axis_index("x") + lax.axis_index("y")
    core = lax.axis_index("c")

    place = jnp.stack([chip, core]).astype(jnp.int32)
    slots = {n: _cast_into_slot(shard2d[n], place, f"cast_{n}") for n in BIG}
    slots["w_in"] = _gather_weights([slots["w_in"]], [], "gather_w_in")[0]
    ex = _Exchange(place, slots, shard2d["conv_w"])
    loss, grad_x, small = _local_step(x[0], loss_target[0], {n: w[n] for n in SMALL if n != "conv_w"}, ex)

    flat = [small[n].reshape(-1) for n in SMALL] + [loss.reshape(-1)]
    sizes = [t.shape[0] for t in flat]
    summed = _all_sum(jnp.concatenate(flat).reshape(-1, LANES), "sum_small").reshape(-1)
    offs = [sum(sizes[:i]) for i in range(len(sizes))]
    grads = {}
    for n, o, sz in zip(SMALL, offs, sizes):
        grads[n] = summed[o:o + sz].reshape(small[n].shape)
    loss_total = summed[offs[-1]]
    cw = 2 * D_FF // N_CHIPS
    grads["conv_w"] = lax.dynamic_slice(grads["conv_w"], (0, chip * cw), (3, cw))

    grads.update(ex.reduced())

    out_g, out_d, out_m, out_v = [], [], [], []
    for n in WEIGHTS:
        d2, m2, v2 = _adamw(shard2d[n], grads[n], m[n].reshape(shard2d[n].shape), v[n].reshape(shard2d[n].shape),
                            f"adamw_{n}")
        shape = w[n].shape
        out_g.append(grads[n].reshape(shape))
        out_d.append(d2.reshape(shape))
        out_m.append(m2.reshape(shape))
        out_v.append(v2.reshape(shape))
    return (loss_total, grad_x[None], *out_g, *out_d, *out_m, *out_v)
```

```python
import functools
import math

import jax
import jax.numpy as jnp
from jax import lax
from jax.experimental import pallas as pl
from jax.experimental.pallas import tpu as pltpu

F32 = jnp.float32
BF16 = jnp.bfloat16
MESH = pl.DeviceIdType.MESH

D_MODEL = 1024
N_GROUPS = 3
DILATIONS = (1, 4, 16)
HEADS = 8
HEAD_DIM = 64
GROUP_W = HEADS * HEAD_DIM
QKV_W = N_GROUPS * 3 * GROUP_W
BLK = 128
NEG_INF = -1e30
NUM_BUCKETS = 32
MAX_EXACT = 16
MAX_DISTANCE = 2048
HG_HEADS = 4
HG_DK = 128
HG_W = HG_HEADS * HG_DK
HG_CHUNK = 32
HG_TILE = 256
IN_W = QKV_W + 4 * HG_W + 2 * D_MODEL
D_FF = 2816
EPS = 1e-6
N_CHIPS = 4
N_DEV = 8
LANES = 128

ADAM_LR, ADAM_B1, ADAM_B2, ADAM_EPS, ADAM_WD, ADAM_STEP = 0.001, 0.9, 0.999, 1e-08, 0.01, 10

VMEM_LIMIT = 56 * 1024 * 1024


def _cp(n_axes):
    return pltpu.CompilerParams(dimension_semantics=("arbitrary",) * n_axes, vmem_limit_bytes=VMEM_LIMIT)


def _sds(shape, dtype):
    return jax.ShapeDtypeStruct(tuple(shape), dtype)


def _sigmoid(v):
    return 1.0 / (1.0 + jnp.exp(-v))


def _bf(v):
    return v.astype(BF16)


def _dot(a, b, dims):
    return lax.dot_general(a, b, (dims, ((), ())), preferred_element_type=F32)


NN = ((1,), (0,))
NT = ((1,), (1,))
TN = ((0,), (0,))

ANY = pl.BlockSpec(memory_space=pl.ANY)


class _Plan:
    def __init__(self, copies, n_sems, ins=(), inouts=(), outs=()):
        self.copies, self.n_sems = copies, n_sems
        self.ins, self.inouts, self.outs = list(ins), list(inouts), list(outs)


def _call(body, plans=None, *, name, grid, in_specs, out_specs, out_shape, args, scratch_shapes=()):
    plans = list(plans or ())
    in_specs, out_specs, out_shape = list(in_specs), list(out_specs), list(out_shape)
    scratch_shapes = list(scratch_shapes)
    n_in, n_out, n_scr = len(in_specs), len(out_specs), len(scratch_shapes)
    x_in, x_out, aliases, spans = [], [], {}, []
    for p in plans:
        i0, o0 = len(x_in), len(x_out)
        x_in += p.ins
        for a in p.inouts:
            aliases[n_in + len(x_in)] = n_out + len(x_out)
            x_in.append(a)
            x_out.append(_sds(a.shape, a.dtype))
        x_out += p.outs
        spans.append((i0, len(p.ins), o0, len(p.inouts), len(p.outs)))
    sems = [pltpu.SemaphoreType.DMA((p.n_sems,)) for p in plans for _ in range(3)]

    def wrapped(*refs):
        xi = refs[n_in:n_in + len(x_in)]
        base = n_in + len(x_in)
        xo = refs[base + n_out:base + n_out + len(x_out)]
        sbase = base + n_out + len(x_out)
        xs = refs[sbase + n_scr:]
        ids = [pl.program_id(k) for k in range(len(grid))]
        first = functools.reduce(jnp.logical_and, [i == 0 for i in ids])
        last = functools.reduce(jnp.logical_and, [i == g - 1 for i, g in zip(ids, grid)])

        def descriptors(k):
            i0, ni, o0, nio, no = spans[k]
            return plans[k].copies(xi[i0:i0 + ni], xo[o0:o0 + nio], xo[o0 + nio:o0 + nio + no], *xs[3 * k:3 * k + 3])

        @pl.when(first)
        def _():
            for k in range(len(plans)):
                sends, _, local = descriptors(k)
                for cp in (*sends, *local):
                    cp.start()

        body(*refs[:n_in], *refs[base:base + n_out], *refs[sbase:sbase + n_scr])

        @pl.when(last)
        def _():
            for k in range(len(plans)):
                sends, recvs, local = descriptors(k)
                for cp in recvs:
                    cp.wait_recv()
                for cp in sends:
                    cp.wait_send()
                for cp in local:
                    cp.wait()

    res = pl.pallas_call(
        wrapped if plans else body, name=name, grid=grid, in_specs=in_specs + [ANY] * len(x_in),
        out_specs=out_specs + [ANY] * len(x_out), out_shape=out_shape + x_out, input_output_aliases=aliases,
        scratch_shapes=scratch_shapes + sems, compiler_params=_cp(len(grid)))(*args, *x_in)
    res = list(res)
    carried = [res[n_out + o0:n_out + o0 + nio + no] for (_, _, o0, nio, no) in spans]
    return res[:n_out], carried


def _mm_nn_blk(a, wg, name, tm=512, plans=None):
    M, K = a.shape
    nb, _, Nb = wg.shape

    def body(a_ref, w_ref, o_ref):
        o_ref[...] = _dot(_bf(a_ref[...]), w_ref[...], NN)

    (out,), carried = _call(
        body, plans, name=name, grid=(nb, M // tm),
        in_specs=[pl.BlockSpec((tm, K), lambda j, i: (i, 0)), pl.BlockSpec((None, K, Nb), lambda j, i: (j, 0, 0))],
        out_specs=[pl.BlockSpec((tm, Nb), lambda j, i: (i, j))],
        out_shape=[_sds((M, nb * Nb), F32)], args=(a, wg))
    return out if plans is None else (out, carried)


def _mm_nt_blk(dy, wg, name, tm=1024, plans=None):
    M = dy.shape[0]
    nb, K, Nb = wg.shape

    def body(dy_ref, w_ref, o_ref):
        j = pl.program_id(1)
        r = _dot(_bf(dy_ref[...]), w_ref[...], NT)

        @pl.when(j == 0)
        def _():
            o_ref[...] = r

        @pl.when(j > 0)
        def _():
            o_ref[...] += r

    (out,), carried = _call(
        body, plans, name=name, grid=(M // tm, nb),
        in_specs=[pl.BlockSpec((tm, Nb), lambda i, j: (i, j)), pl.BlockSpec((None, K, Nb), lambda i, j: (j, 0, 0))],
        out_specs=[pl.BlockSpec((tm, K), lambda i, j: (i, 0))],
        out_shape=[_sds((M, K), F32)], args=(dy, wg))
    return out if plans is None else (out, carried)


def _mm_tn_blk(x, dy, nb, name, tk=2048, x_cols=None, plans=None, together=False):
    T, Mx = x.shape
    xk, Mx = (0, Mx) if x_cols is None else x_cols
    Nb = dy.shape[1] // nb
    nj = nb if together else 1

    def body(x_ref, dy_ref, o_ref):
        t = pl.program_id(1)
        r = _dot(_bf(x_ref[...]), _bf(dy_ref[...]), TN)
        for j in range(nj):
            rj = r[:, j * Nb:(j + 1) * Nb]

            @pl.when(t == 0)
            def _():
                o_ref[j] = rj

            @pl.when(t > 0)
            def _():
                o_ref[j] += rj

    (out,), carried = _call(
        body, plans, name=name, grid=(nb // nj, T // tk),
        in_specs=[pl.BlockSpec((tk, Mx), lambda j, t: (t, xk)), pl.BlockSpec((tk, nj * Nb), lambda j, t: (t, j))],
        out_specs=[pl.BlockSpec((nj, Mx, Nb), lambda j, t: (j, 0, 0))],
        out_shape=[_sds((nb, Mx, Nb), F32)], args=(x, dy))
    return out if plans is None else (out, carried)


def _mm_nt(dy, w, name, tm=512):
    M, N = dy.shape
    K = w.shape[0]

    def body(dy_ref, w_ref, o_ref):
        o_ref[...] = _dot(_bf(dy_ref[...]), w_ref[...], NT)

    return pl.pallas_call(
        body, name=name, grid=(M // tm,),
        in_specs=[pl.BlockSpec((tm, N), lambda i: (i, 0)), pl.BlockSpec((K, N), lambda i: (0, 0))],
        out_specs=pl.BlockSpec((tm, K), lambda i: (i, 0)),
        out_shape=_sds((M, K), F32), compiler_params=_cp(1))(dy, w)


def _mm_tn(x, dy, name, tk=1024):
    T, Mx = x.shape
    N = dy.shape[1]

    def body(x_ref, dy_ref, o_ref):
        t = pl.program_id(0)
        r = _dot(_bf(x_ref[...]), _bf(dy_ref[...]), TN)

        @pl.when(t == 0)
        def _():
            o_ref[...] = r

        @pl.when(t > 0)
        def _():
            o_ref[...] += r

    return pl.pallas_call(
        body, name=name, grid=(T // tk,),
        in_specs=[pl.BlockSpec((tk, Mx), lambda t: (t, 0)), pl.BlockSpec((tk, N), lambda t: (t, 0))],
        out_specs=pl.BlockSpec((Mx, N), lambda t: (0, 0)),
        out_shape=_sds((Mx, N), F32), compiler_params=_cp(1))(x, dy)


def _tile(arr, bw, col=lambda c: 0):
    return ("tile", arr, bw, col)


def _full(arr):
    return ("full", arr)


def _out_tile(width, dtype, bw, col=lambda c: 0):
    return ("tile", width, dtype, bw, col)


def _out_acc(rows, width, bw, col=lambda c: 0):
    return ("acc", rows, width, bw, col)


def _rows_call(name, body, n_rows, tm, ncol, ins, outs, plans=None):
    in_specs, args = [], []
    for e in ins:
        if e[0] == "tile":
            _, arr, bw, col = e
            in_specs.append(pl.BlockSpec((tm, bw), functools.partial(lambda c, i, col: (i, col(c)), col=col)))
        else:
            arr = e[1]
            in_specs.append(pl.BlockSpec(arr.shape, functools.partial(lambda c, i, nd: (0,) * nd, nd=arr.ndim)))
        args.append(arr)
    out_specs, out_shape = [], []
    for e in outs:
        if e[0] == "tile":
            _, width, dtype, bw, col = e
            out_specs.append(pl.BlockSpec((tm, bw), functools.partial(lambda c, i, col: (i, col(c)), col=col)))
            out_shape.append(_sds((n_rows, width), dtype))
        else:
            _, rows, width, bw, col = e
            out_specs.append(pl.BlockSpec((rows, bw), functools.partial(lambda c, i, col: (0, col(c)), col=col)))
            out_shape.append(_sds((rows, width), F32))
    out, carried = _call(body, plans, name=name, grid=(ncol, n_rows // tm), in_specs=in_specs, out_specs=out_specs,
                         out_shape=out_shape, args=args)
    return out if plans is None else (out, carried)


def _acc(ref, val):
    i = pl.program_id(1)

    @pl.when(i == 0)
    def _():
        ref[...] = val

    @pl.when(i > 0)
    def _():
        ref[...] += val


def _rinv(z):
    return lax.rsqrt(jnp.mean(z * z, axis=-1, keepdims=True) + EPS)


def _norm_bwd(dy, zhat, r, w):
    dyw = dy * w
    return r * (dyw - zhat * jnp.mean(dyw * zhat, axis=-1, keepdims=True))


def _norm_proj(x, w, wg, name, tm=1024, plans=None):
    M, K = x.shape
    nb, _, Nb = wg.shape

    def body(x_ref, w_ref, wg_ref, o_ref, h_ref):
        @pl.when(pl.program_id(1) == 0)
        def _():
            xv = x_ref[...]
            h_ref[...] = _bf(xv * _rinv(xv) * w_ref[...])

        o_ref[...] = _dot(h_ref[...], wg_ref[...], NN)

    (out, h), carried = _call(
        body, plans, name=name, grid=(M // tm, nb),
        in_specs=[pl.BlockSpec((tm, K), lambda i, j: (i, 0)), pl.BlockSpec((1, K), lambda i, j: (0, 0)),
                  pl.BlockSpec((None, K, Nb), lambda i, j: (j, 0, 0))],
        out_specs=[pl.BlockSpec((tm, Nb), lambda i, j: (i, j)), pl.BlockSpec((tm, K), lambda i, j: (i, 0))],
        out_shape=[_sds((M, nb * Nb), F32), _sds((M, K), BF16)], args=(x, w, wg))
    return (out, h) if plans is None else (out, h, carried)


def _prenorm_bwd(dh, xin, w, dres, name, plans=None):
    def body(dh_ref, x_ref, w_ref, dres_ref, dx_ref, dw_ref):
        xv = x_ref[...]
        r = _rinv(xv)
        xhat = xv * r
        dhv = dh_ref[...]
        dx_ref[...] = dres_ref[...] + _norm_bwd(dhv, xhat, r, w_ref[...])
        _acc(dw_ref, jnp.sum(dhv * xhat, axis=0, keepdims=True))

    return _rows_call(name, body, xin.shape[0], 512, 1,
                      [_tile(dh, D_MODEL), _tile(xin, D_MODEL), _full(w), _tile(dres, D_MODEL)],
                      [_out_tile(D_MODEL, F32, D_MODEL), _out_acc(1, D_MODEL, D_MODEL)], plans)


def _postnorm_bwd(dout, z, w, w_mat, name):
    def body(do_ref, z_ref, w_ref, wm_ref, dz_ref, dm_ref, dw_ref):
        zv = z_ref[...]
        r = _rinv(zv)
        zhat = zv * r
        dov = do_ref[...]
        dz = _bf(_norm_bwd(dov, zhat, r, w_ref[...]))
        dz_ref[...] = dz
        dm_ref[...] = _dot(dz, wm_ref[...], NT)
        _acc(dw_ref, jnp.sum(dov * zhat, axis=0, keepdims=True))

    return _rows_call(name, body, z.shape[0], 512, 1,
                      [_tile(dout, D_MODEL), _tile(z, D_MODEL), _full(w), _full(w_mat)],
                      [_out_tile(D_MODEL, BF16, D_MODEL), _out_tile(D_MODEL, F32, D_MODEL),
                       _out_acc(1, D_MODEL, D_MODEL)])


def _t5_bucket(dist):
    n = jnp.maximum(dist, 0)
    nf = jnp.maximum(n, 1).astype(F32)
    large = MAX_EXACT + (jnp.log(nf / MAX_EXACT) / math.log(MAX_DISTANCE / MAX_EXACT)
                         * (NUM_BUCKETS - MAX_EXACT)).astype(jnp.int32)
    large = jnp.minimum(large, NUM_BUCKETS - 1)
    return jnp.where(n < MAX_EXACT, n, large)


def _band_rel():
    return jnp.arange(BLK)[:, None] + BLK - jnp.arange(2 * BLK)[None, :]


def _band_valid():
    rel = _band_rel()
    window = (rel >= 0) & (rel <= BLK)
    first = window & (jnp.arange(2 * BLK)[None, :] >= BLK)
    return jnp.stack([first, window]).astype(F32).reshape(2, 1, BAND)


RES_UNROLL = 4
PAIR = LANES // HEAD_DIM


def _pair_lanes():
    first = lax.broadcasted_iota(jnp.int32, (1, LANES), 1) < HEAD_DIM
    return first, jnp.logical_not(first)


def _heads_per_step(d):
    return HEADS if d == 1 else LANES // HEAD_DIM


def _sub_rows(r, d):
    return pl.ds(r, BLK, stride=d) if d > 1 else pl.ds(0, BLK)


def _for_residues(d, fn):
    if d <= RES_UNROLL:
        for r in range(d):
            fn(r)
    else:
        def group(i, carry):
            for k in range(RES_UNROLL):
                fn(i * RES_UNROLL + k)
            return carry

        lax.fori_loop(0, d // RES_UNROLL, group, 0)


def _attn_specs(d, g, qblock):
    cw = _heads_per_step(d) * HEAD_DIM

    def col(part, hp):
        return (g * 3 + part) * (GROUP_W // cw) + hp

    def cur(part):
        return pl.BlockSpec((d * BLK, cw), lambda hp, n: (qblock(n), col(part, hp)))

    def prev(part):
        return pl.BlockSpec((d * BLK, cw), lambda hp, n: (jnp.maximum(qblock(n) - 1, 0), col(part, hp)))

    return cur, prev


def _attn_fwd(proj, bias, g, name, plans=None):
    S = proj.shape[0]
    d = DILATIONS[g]
    NB = S // (d * BLK)
    hps = _heads_per_step(d)

    def body(q_ref, kp_ref, kc_ref, vp_ref, vc_ref, b_ref, o_ref, lse_ref):
        hp = pl.program_id(0)
        later = jnp.minimum(pl.program_id(1), 1)

        def residue(r):
            rows = _sub_rows(r, d)
            q2 = q_ref[rows, :]
            k2 = jnp.concatenate([kp_ref[rows, :], kc_ref[rows, :]], axis=0)
            v2 = jnp.concatenate([vp_ref[rows, :], vc_ref[rows, :]], axis=0)
            outs, lses = [], []
            for pp in range(hps // PAIR):
                ps = slice(pp * LANES, (pp + 1) * LANES)
                qp, kp, vp = _bf(q2[:, ps]), _bf(k2[:, ps]), _bf(v2[:, ps])
                o_h, lse_h = [], []
                for hh, own in enumerate(_pair_lanes()):
                    s = _dot(qp, jnp.where(own, kp, 0), NT) * (HEAD_DIM ** -0.5) + b_ref[later, hp * hps + pp * PAIR + hh]
                    m = jnp.max(s, axis=-1, keepdims=True)
                    p = jnp.exp(s - m)
                    l = jnp.sum(p, axis=-1, keepdims=True)
                    o_h.append(_dot(_bf(p), vp, NN) / l)
                    lse_h.append(m + jnp.log(l))
                first = _pair_lanes()[0]
                outs.append(jnp.where(first, o_h[0], o_h[1]))
                lses.append(jnp.where(first, lse_h[0], lse_h[1]))
            o_ref[rows, :] = outs[0] if len(outs) == 1 else jnp.concatenate(outs, axis=1)
            lse_ref[rows, :] = lses[0] if len(lses) == 1 else jnp.concatenate(lses, axis=1)

        _for_residues(d, residue)

    cur, prev = _attn_specs(d, g, lambda n: n)
    out = pl.BlockSpec((d * BLK, hps * HEAD_DIM), lambda hp, n: (n, hp))
    res, carried = _call(
        body, plans, name=name, grid=(HEADS // hps, NB),
        in_specs=[cur(0), prev(1), cur(1), prev(2), cur(2),
                  pl.BlockSpec((2, HEADS, BLK, 2 * BLK), lambda hp, n: (0, 0, 0, 0))],
        out_specs=[out, out], out_shape=[_sds((S, GROUP_W), F32)] * 2,
        args=(proj, proj, proj, proj, proj, bias))
    return res if plans is None else (res, carried)


def _attn_bwd(proj, bias, lse, y, dy, g, name, plans=None):
    S = proj.shape[0]
    d = DILATIONS[g]
    NB = S // (d * BLK)
    hps = _heads_per_step(d)

    def body(q_ref, kp_ref, kc_ref, vp_ref, vc_ref, b_ref, l_ref, y_ref, dy_ref,
             dq_ref, dk_ref, dv_ref, db_ref, ck_ref, cv_ref):
        hp, n = pl.program_id(0), pl.program_id(1)

        @pl.when((hp == 0) & (n == 0))
        def _():
            db_ref[...] = jnp.zeros_like(db_ref)

        @pl.when(n == 0)
        def _():
            ck_ref[...] = jnp.zeros_like(ck_ref)
            cv_ref[...] = jnp.zeros_like(cv_ref)

        @pl.when(n < NB)
        def _():
            later = jnp.minimum(n, 1)

            def residue(r):
                rows = _sub_rows(r, d)
                q2 = q_ref[rows, :]
                k2 = jnp.concatenate([kp_ref[rows, :], kc_ref[rows, :]], axis=0)
                v2 = jnp.concatenate([vp_ref[rows, :], vc_ref[rows, :]], axis=0)
                l2, y2, dy2 = l_ref[rows, :], y_ref[rows, :], dy_ref[rows, :]
                dqs, dks, dvs = [], [], []
                for pp in range(hps // PAIR):
                    ps = slice(pp * LANES, (pp + 1) * LANES)
                    qp, kp, vp = _bf(q2[:, ps]), _bf(k2[:, ps]), _bf(v2[:, ps])
                    dyp, yp = dy2[:, ps], y2[:, ps]
                    dq_h, dk_h, dv_h = [], [], []
                    for hh, own in enumerate(_pair_lanes()):
                        head = hp * hps + pp * PAIR + hh
                        s = _dot(qp, jnp.where(own, kp, 0), NT) * (HEAD_DIM ** -0.5) + b_ref[later, head]
                        p = jnp.exp(s - l2[:, pp * LANES + hh * HEAD_DIM:pp * LANES + hh * HEAD_DIM + 1])
                        dyh = jnp.where(own, dyp, 0.0)
                        delta = jnp.sum(dyh * yp, axis=-1, keepdims=True)
                        ds = p * (_dot(_bf(dyh), vp, NT) - delta)
                        db_ref[head] += ds
                        dsb = _bf(ds * (HEAD_DIM ** -0.5))
                        dq_h.append(_dot(dsb, kp, NN))
                        dk_h.append(_dot(dsb, qp, TN))
                        dv_h.append(_dot(_bf(p), _bf(dyp), TN))
                    first = _pair_lanes()[0]
                    dqs.append(jnp.where(first, dq_h[0], dq_h[1]))
                    dks.append(jnp.where(first, dk_h[0], dk_h[1]))
                    dvs.append(jnp.where(first, dv_h[0], dv_h[1]))
                dkb = dks[0] if len(dks) == 1 else jnp.concatenate(dks, axis=1)
                dvb = dvs[0] if len(dvs) == 1 else jnp.concatenate(dvs, axis=1)
                dq_ref[rows, :] = dqs[0] if len(dqs) == 1 else jnp.concatenate(dqs, axis=1)
                dk_ref[rows, :] = ck_ref[rows, :] + dkb[:BLK]
                dv_ref[rows, :] = cv_ref[rows, :] + dvb[:BLK]
                ck_ref[rows, :] = dkb[BLK:]
                cv_ref[rows, :] = dvb[BLK:]

            _for_residues(d, residue)

        @pl.when(n == NB)
        def _():
            dk_ref[...] = ck_ref[...]
            dv_ref[...] = cv_ref[...]

    def qn(n):
        return jnp.minimum(n, NB - 1)

    cur, prev = _attn_specs(d, g, qn)
    cw = hps * HEAD_DIM
    row = pl.BlockSpec((d * BLK, cw), lambda hp, n: (qn(n), hp))
    done = pl.BlockSpec((d * BLK, cw), lambda hp, n: (jnp.maximum(n - 1, 0), hp))
    (dq, dk, dv, db), carried = _call(
        body, plans, name=name, grid=(HEADS // hps, NB + 1),
        in_specs=[cur(0), prev(1), cur(1), prev(2), cur(2),
                  pl.BlockSpec((2, HEADS, BLK, 2 * BLK), lambda hp, n: (0, 0, 0, 0)), row, row, row],
        out_specs=[row, done, done, pl.BlockSpec((HEADS, BLK, 2 * BLK), lambda hp, n: (0, 0, 0))],
        out_shape=[_sds((S, GROUP_W), F32)] * 3 + [_sds((HEADS, BLK, 2 * BLK), F32)],
        scratch_shapes=[pltpu.VMEM((d * BLK, cw), F32)] * 2,
        args=(proj, proj, proj, proj, proj, bias, lse, y, dy))
    return ([dq, dk, dv], db) if plans is None else ([dq, dk, dv], db, carried)


BAND = BLK * 2 * BLK


def _bucket_onehot():
    buckets = jnp.stack([_t5_bucket(_band_rel() * d) for d in DILATIONS]).reshape(N_GROUPS, 1, BAND)
    return (buckets == jnp.arange(NUM_BUCKETS).reshape(1, NUM_BUCKETS, 1)).astype(F32)


def _relbias_fwd(rel_bias, name):
    table = rel_bias.reshape(NUM_BUCKETS, N_GROUPS, HEADS).transpose(1, 0, 2)

    def body(t_ref, oh_ref, valid_ref, o_ref):
        bias = lax.dot_general(t_ref[...], oh_ref[...], (TN, ((), ())), preferred_element_type=F32,
                               precision=lax.Precision.HIGHEST)
        for k in range(2):
            o_ref[k] = jnp.where(valid_ref[k] > 0.5, bias, NEG_INF)

    out = pl.pallas_call(
        body, name=name, grid=(N_GROUPS,),
        in_specs=[pl.BlockSpec((None, NUM_BUCKETS, HEADS), lambda g: (g, 0, 0)),
                  pl.BlockSpec((None, NUM_BUCKETS, BAND), lambda g: (g, 0, 0)),
                  pl.BlockSpec((2, 1, BAND), lambda g: (0, 0, 0))],
        out_specs=pl.BlockSpec((None, 2, HEADS, BAND), lambda g: (g, 0, 0, 0)),
        out_shape=_sds((N_GROUPS, 2, HEADS, BAND), F32), compiler_params=_cp(1))(table, _bucket_onehot(), _band_valid())
    return out.reshape(N_GROUPS, 2, HEADS, BLK, 2 * BLK)


def _relbias_bwd(dbs, name):
    band = BAND
    onehot = _bucket_onehot()
    dbf = jnp.stack([db.reshape(HEADS, band) for db in dbs])

    def body(oh_ref, db_ref, o_ref):
        o_ref[...] = lax.dot_general(oh_ref[...], db_ref[...], (NT, ((), ())), preferred_element_type=F32,
                                     precision=lax.Precision.HIGHEST)

    out = pl.pallas_call(
        body, name=name, grid=(N_GROUPS,),
        in_specs=[pl.BlockSpec((None, NUM_BUCKETS, band), lambda g: (g, 0, 0)),
                  pl.BlockSpec((None, HEADS, band), lambda g: (g, 0, 0))],
        out_specs=pl.BlockSpec((None, NUM_BUCKETS, HEADS), lambda g: (g, 0, 0)),
        out_shape=_sds((N_GROUPS, NUM_BUCKETS, HEADS), F32), compiler_params=_cp(1))(onehot, dbf)
    return out.transpose(1, 0, 2).reshape(NUM_BUCKETS, N_GROUPS * HEADS)


def _chunk_pos(shape):
    return lax.broadcasted_iota(jnp.int32, shape, 0) % HG_CHUNK


def _chunk_cumsum(v):
    pos = _chunk_pos(v.shape)
    s = 1
    while s < HG_CHUNK:
        v = v + jnp.where(pos >= s, pltpu.roll(v, s, 0), 0.0)
        s *= 2
    return v


def _chunk_rev_cumsum(v):
    pos = _chunk_pos(v.shape)
    n = v.shape[0]
    s = 1
    while s < HG_CHUNK:
        v = v + jnp.where(pos < HG_CHUNK - s, pltpu.roll(v, n - s, 0), 0.0)
        s *= 2
    return v


def _lower_bound(raw):
    a0, a1 = raw[0:1], raw[1:2]
    m = jnp.maximum(a0, a1)
    e0, e1 = jnp.exp(a0 - m), jnp.exp(a1 - m)
    return e0 / (e0 + e1)


def _hg_gates(qr, fr, lb):
    sf = _sigmoid(fr)
    f = lb + (1.0 - lb) * sf
    sq = _sigmoid(qr)
    return qr * sq, sq, f, sf


HG_COL0 = QKV_W // HG_W


def _hgrn_fwd(proj, lb_raw, nw, name):
    S = proj.shape[0]
    ncs = HG_TILE // HG_CHUNK
    tril = jnp.tril(jnp.ones((HG_CHUNK, HG_CHUNK), dtype=bool))

    def body(q_ref, f_ref, i_ref, og_ref, lb_ref, nw_ref, y_ref, o_ref, st_ref, state):
        @pl.when(pl.program_id(0) == 0)
        def _():
            state[...] = jnp.zeros_like(state)

        lb = _lower_bound(lb_ref[...])
        q, _, f, _ = _hg_gates(q_ref[...], f_ref[...], lb)
        k = 1.0 - f
        G = _chunk_cumsum(jnp.log(f))
        row = lax.broadcasted_iota(jnp.int32, (HG_CHUNK, HG_CHUNK), 0)
        col = lax.broadcasted_iota(jnp.int32, (HG_CHUNK, HG_CHUNK), 1)
        heads = [slice(h * HG_DK, (h + 1) * HG_DK) for h in range(HG_HEADS)]
        sts = [state[h] for h in range(HG_HEADS)]
        for c in range(ncs):
            cs = slice(c * HG_CHUNK, (c + 1) * HG_CHUNK)
            for h, hs in enumerate(heads):
                Gc = G[cs, hs]
                gl = Gc[HG_CHUNK - 1:HG_CHUNK]
                qt = _bf(q[cs, hs] * jnp.exp(Gc))
                kt = _bf(k[cs, hs] * jnp.exp(-Gc))
                kd = _bf(k[cs, hs] * jnp.exp(gl - Gc))
                v = _bf(i_ref[cs, hs])
                A = jnp.where(row >= col, _dot(qt, kt, NT), 0.0)
                o_ref[cs, hs] = _dot(_bf(A), v, NN) + _dot(qt, _bf(sts[h]), NT)
                st_ref[c, h] = sts[h]
                sts[h] = sts[h] * jnp.exp(gl) + _dot(v, kd, TN)
        for h, hs in enumerate(heads):
            state[h] = sts[h]
            oh = o_ref[:, hs]
            og = og_ref[:, hs]
            y_ref[:, hs] = oh * _rinv(oh) * nw_ref[...] * (og * _sigmoid(og))

    def colspec(j):
        return pl.BlockSpec((HG_TILE, HG_W), lambda i: (i, HG_COL0 + j))

    return pl.pallas_call(
        body, name=name, grid=(S // HG_TILE,),
        in_specs=[colspec(0), colspec(1), colspec(2), colspec(3),
                  pl.BlockSpec((2, HG_W), lambda i: (0, 0)), pl.BlockSpec((1, HG_DK), lambda i: (0, 0))],
        out_specs=[pl.BlockSpec((HG_TILE, HG_W), lambda i: (i, 0))] * 2
        + [pl.BlockSpec((ncs, HG_HEADS, HG_DK, HG_DK), lambda i: (i, 0, 0, 0))],
        out_shape=[_sds((S, HG_W), F32)] * 2 + [_sds((S // HG_CHUNK, HG_HEADS, HG_DK, HG_DK), F32)],
        scratch_shapes=[pltpu.VMEM((HG_HEADS, HG_DK, HG_DK), F32)],
        compiler_params=_cp(1))(proj, proj, proj, proj, lb_raw, nw)


def _hgrn_bwd(proj, lb_raw, nw, o, states, dy, name):
    S = proj.shape[0]
    ncs = HG_TILE // HG_CHUNK
    nt = S // HG_TILE

    def body(q_ref, f_ref, i_ref, og_ref, lb_ref, nw_ref, o_ref, st_ref, dy_ref,
             dq_ref, df_ref, di_ref, dog_ref, dlb_ref, dnw_ref, dstate, do_s, dG_s, dgl_s, dk_s, dlb_s):
        step = pl.program_id(0)

        @pl.when(step == 0)
        def _():
            dstate[...] = jnp.zeros_like(dstate)
            dlb_s[...] = jnp.zeros_like(dlb_s)
            dnw_ref[...] = jnp.zeros_like(dnw_ref)

        lb = _lower_bound(lb_ref[...])
        qr = q_ref[...]
        q, sq, f, sf = _hg_gates(qr, f_ref[...], lb)
        k = 1.0 - f
        G = _chunk_cumsum(jnp.log(f))
        nwv = nw_ref[...]
        row = lax.broadcasted_iota(jnp.int32, (HG_CHUNK, HG_CHUNK), 0)
        col = lax.broadcasted_iota(jnp.int32, (HG_CHUNK, HG_CHUNK), 1)
        for h in range(HG_HEADS):
            hs = slice(h * HG_DK, (h + 1) * HG_DK)
            oh = o_ref[:, hs]
            r = _rinv(oh)
            ohat = oh * r
            og = og_ref[:, hs]
            sg = _sigmoid(og)
            dyh = dy_ref[:, hs]
            don = dyh * (og * sg)
            dog_ref[:, hs] = _bf(dyh * (ohat * nwv) * (sg * (1.0 + og * (1.0 - sg))))
            dnw_ref[...] += jnp.sum(don * ohat, axis=0, keepdims=True)
            do_s[:, hs] = _norm_bwd(don, ohat, r, nwv)
        dsts = [dstate[h] for h in range(HG_HEADS)]
        for c in reversed(range(ncs)):
            cs = slice(c * HG_CHUNK, (c + 1) * HG_CHUNK)
            for h in range(HG_HEADS):
                hs = slice(h * HG_DK, (h + 1) * HG_DK)
                dst = dsts[h]
                Gc = G[cs, hs]
                gl = Gc[HG_CHUNK - 1:HG_CHUNK]
                eG, enG, edG, egl = jnp.exp(Gc), jnp.exp(-Gc), jnp.exp(gl - Gc), jnp.exp(gl)
                qt, kt, kd = q[cs, hs] * eG, k[cs, hs] * enG, k[cs, hs] * edG
                qtb, ktb, kdb = _bf(qt), _bf(kt), _bf(kd)
                v = _bf(i_ref[cs, hs])
                do = _bf(do_s[cs, hs])
                st = st_ref[c, h]
                dstb = _bf(dst)
                A = jnp.where(row >= col, _dot(qtb, ktb, NT), 0.0)
                dA = _bf(jnp.where(row >= col, _dot(do, v, NT), 0.0))
                di_ref[cs, hs] = _bf(_dot(_bf(A), do, TN) + _dot(kdb, dstb, NT))
                dqt = _dot(dA, ktb, NN) + _dot(do, _bf(st), NN)
                dkt = _dot(dA, qtb, TN)
                dkd = _dot(v, dstb, NN)
                dgl = egl * jnp.sum(st * dst, axis=0, keepdims=True) + jnp.sum(dkd * kd, axis=0, keepdims=True)
                dsts[h] = dst * egl + _dot(do, qtb, TN)
                dq_ref[cs, hs] = _bf(dqt * eG * (sq[cs, hs] * (1.0 + qr[cs, hs] * (1.0 - sq[cs, hs]))))
                dk_s[cs, hs] = dkt * enG + dkd * edG
                dG_s[cs, hs] = dqt * qt - dkt * kt - dkd * kd
                dgl_s[cs, hs] = jnp.broadcast_to(dgl, (HG_CHUNK, HG_DK))
        for h in range(HG_HEADS):
            dstate[h] = dsts[h]
        dg = _chunk_rev_cumsum(dG_s[...]) + dgl_s[...]
        dfv = dg / f - dk_s[...]
        df_ref[...] = _bf(dfv * (1.0 - lb) * sf * (1.0 - sf))
        dlb_s[...] += jnp.sum(dfv * (1.0 - sf), axis=0, keepdims=True)

        @pl.when(step == nt - 1)
        def _():
            t = dlb_s[...] * lb * (1.0 - lb)
            dlb_ref[...] = jnp.concatenate([t, -t], axis=0)

    def colspec(j):
        return pl.BlockSpec((HG_TILE, HG_W), lambda i: (nt - 1 - i, HG_COL0 + j))

    tile = pl.BlockSpec((HG_TILE, HG_W), lambda i: (nt - 1 - i, 0))
    outs = pl.pallas_call(
        body, name=name, grid=(nt,),
        in_specs=[colspec(0), colspec(1), colspec(2), colspec(3),
                  pl.BlockSpec((2, HG_W), lambda i: (0, 0)), pl.BlockSpec((1, HG_DK), lambda i: (0, 0)),
                  tile, pl.BlockSpec((ncs, HG_HEADS, HG_DK, HG_DK), lambda i: (nt - 1 - i, 0, 0, 0)), tile],
        out_specs=[tile] * 4 + [pl.BlockSpec((2, HG_W), lambda i: (0, 0)), pl.BlockSpec((1, HG_DK), lambda i: (0, 0))],
        out_shape=[_sds((S, HG_W), BF16)] * 4 + [_sds((2, HG_W), F32), _sds((1, HG_DK), F32)],
        scratch_shapes=[pltpu.VMEM((HG_HEADS, HG_DK, HG_DK), F32)] + [pltpu.VMEM((HG_TILE, HG_W), F32)] * 4
        + [pltpu.VMEM((1, HG_W), F32)],
        compiler_params=_cp(1))(proj, proj, proj, proj, lb_raw, nw, o, states, dy)
    return outs[:4], outs[4], outs[5]


GATE_COL0 = (QKV_W + 4 * HG_W) // GROUP_W
HALF_D = D_MODEL // 2


def _gate_tiles(proj):
    return [_tile(proj, HALF_D, functools.partial(lambda c, k: GATE_COL0 + k, k=k)) for k in range(4)]


def _gates(g_refs):
    s0 = _sigmoid(jnp.concatenate([g_refs[0][...], g_refs[1][...]], axis=1))
    s1 = _sigmoid(jnp.concatenate([g_refs[2][...], g_refs[3][...]], axis=1))
    return s0, s1


def _branch_fwd(os_, lses, yh, proj, w_a, w_h, name):
    nb = w_a.shape[0]

    def body(o0, o1, o2, l0, l1, l2, yh_ref, g0a, g0b, g1a, g1b, wa_ref, wh_ref,
             y_ref, lse_ref, za_ref, zh_ref, m_ref):
        a, b, c = l0[...], l1[...], l2[...]
        m = jnp.maximum(jnp.maximum(a, b), c)
        ea, eb, ec = jnp.exp(a - m), jnp.exp(b - m), jnp.exp(c - m)
        den = ea + eb + ec
        y = (ea * o0[...] + eb * o1[...] + ec * o2[...]) / den
        y_ref[...] = y
        lse_ref[...] = m + jnp.log(den)
        yb, yhb = _bf(y), _bf(yh_ref[...])
        za = jnp.concatenate([_dot(yb, wa_ref[j], NN) for j in range(nb)], axis=1)
        zh = jnp.concatenate([_dot(yhb, wh_ref[j], NN) for j in range(nb)], axis=1)
        s0, s1 = _gates((g0a, g0b, g1a, g1b))
        za_ref[...] = za
        zh_ref[...] = zh
        m_ref[...] = _bf(s0 * za + s1 * zh)

    return _rows_call(name, body, yh.shape[0], 512, 1,
                      [*[_tile(t, GROUP_W) for t in (*os_, *lses)], _tile(yh, HG_W), *_gate_tiles(proj),
                       _full(w_a), _full(w_h)],
                      [_out_tile(GROUP_W, F32, GROUP_W)] * 2 + [_out_tile(D_MODEL, F32, D_MODEL)] * 2
                      + [_out_tile(D_MODEL, BF16, D_MODEL)])


def _branch_bwd(dm, za, zh, proj, w_a, w_h, name):
    nb, _, Nb = w_a.shape

    def body(dm_ref, za_ref, zh_ref, g0a, g0b, g1a, g1b, wa_ref, wh_ref,
             dza_ref, dzh_ref, dg0_ref, dg1_ref, dy_ref, dyh_ref):
        dmv = dm_ref[...]
        s0, s1 = _gates((g0a, g0b, g1a, g1b))
        dza, dzh = _bf(dmv * s0), _bf(dmv * s1)
        dza_ref[...] = dza
        dzh_ref[...] = dzh
        dg0_ref[...] = _bf(dmv * za_ref[...] * s0 * (1.0 - s0))
        dg1_ref[...] = _bf(dmv * zh_ref[...] * s1 * (1.0 - s1))
        dy_ref[...] = sum(_dot(dza[:, j * Nb:(j + 1) * Nb], wa_ref[j], NT) for j in range(nb))
        dyh_ref[...] = sum(_dot(dzh[:, j * Nb:(j + 1) * Nb], wh_ref[j], NT) for j in range(nb))

    return _rows_call(name, body, za.shape[0], 512, 1,
                      [_tile(dm, D_MODEL), _tile(za, D_MODEL), _tile(zh, D_MODEL), *_gate_tiles(proj),
                       _full(w_a), _full(w_h)],
                      [_out_tile(D_MODEL, BF16, D_MODEL)] * 4 + [_out_tile(GROUP_W, F32, GROUP_W),
                                                                 _out_tile(HG_W, F32, HG_W)])


def _mix_out(merged, w_out, x, w_post, w_pre, name):
    def body(m_ref, wo_ref, x_ref, wp_ref, wf_ref, mo_ref, x1_ref, h2_ref):
        z = _dot(m_ref[...], wo_ref[...], NN)
        mo_ref[...] = z
        x1 = x_ref[...] + z * _rinv(z) * wp_ref[...]
        x1_ref[...] = x1
        h2_ref[...] = _bf(x1 * _rinv(x1) * wf_ref[...])

    return _rows_call(name, body, x.shape[0], 512, 1,
                      [_tile(merged, D_MODEL), _full(w_out), _tile(x, D_MODEL), _full(w_post), _full(w_pre)],
                      [_out_tile(D_MODEL, F32, D_MODEL), _out_tile(D_MODEL, F32, D_MODEL),
                       _out_tile(D_MODEL, BF16, D_MODEL)])


def _loss_head(a, w_down, x1, tgt, w, name):
    def body(a_ref, wd_ref, x1_ref, t_ref, w_ref, dx_ref, df_ref, dw_ref, loss_ref):
        z = _dot(a_ref[...], wd_ref[...], NN)
        r = _rinv(z)
        zhat = z * r
        wv = w_ref[...]
        e = x1_ref[...] + zhat * wv - t_ref[...]
        dx = e * (1.0 / D_MODEL)
        dx_ref[...] = dx
        df_ref[...] = _bf(_norm_bwd(dx, zhat, r, wv))
        _acc(dw_ref, jnp.sum(dx * zhat, axis=0, keepdims=True))
        part = 0.5 * jnp.sum(jnp.sum(e * e, axis=1, keepdims=True), axis=0, keepdims=True) * (1.0 / D_MODEL)
        _acc(loss_ref, jnp.broadcast_to(part, (1, LANES)))

    return _rows_call(name, body, x1.shape[0], 512, 1,
                      [_tile(a, D_FF), _full(w_down), _tile(x1, D_MODEL), _tile(tgt, D_MODEL), _full(w)],
                      [_out_tile(D_MODEL, F32, D_MODEL), _out_tile(D_MODEL, BF16, D_MODEL),
                       _out_acc(1, D_MODEL, D_MODEL), _out_acc(1, LANES, LANES)])


CONV_CB = D_FF // 2
CONV_TM = 512
HALO = 8
SQRT_HALF = 0.7071067811865476
INV_SQRT_2PI = 0.3989422804014327


CONV_RS = 32


def _lane_tiles():
    return [slice(k * LANES, (k + 1) * LANES) for k in range(CONV_CB // LANES)]


def _strip_start(i):
    return pl.multiple_of(i * CONV_RS, CONV_RS)


def _strip_taps(u_ref, halo_ref, r0, cs, first_strip, first_tile):
    if first_strip:
        before = jnp.where(first_tile, 0.0, halo_ref[:, cs])
        blk = jnp.concatenate([before, u_ref[0:CONV_RS, cs]], axis=0)
    else:
        blk = u_ref[pl.ds(pl.multiple_of(r0 - HALO, HALO), CONV_RS + HALO), cs]
    return pltpu.roll(blk, 2, 0)[HALO:], pltpu.roll(blk, 1, 0)[HALO:], blk[HALO:]


def _conv(taps, w_ref, b_ref, cs):
    return b_ref[:, cs] + w_ref[0:1, cs] * taps[0] + w_ref[1:2, cs] * taps[1] + w_ref[2:3, cs] * taps[2]


def _conv_specs(tm):
    nh = tm // HALO
    nc = D_FF // CONV_CB

    def tile(off):
        return pl.BlockSpec((tm, CONV_CB), lambda c, i: (i, off + c))

    def halo(off):
        return pl.BlockSpec((HALO, CONV_CB), lambda c, i: (jnp.maximum(i * nh - 1, 0), off + c))

    def small(rows, off):
        return pl.BlockSpec((rows, CONV_CB), lambda c, i: (0, off + c))

    return nc, tile, halo, small


def _conv_gelu_fwd(u, cw, cb, name):
    S = u.shape[0]
    tm = CONV_TM
    nc, tile, halo, small = _conv_specs(tm)

    def body(ug, hg, uv, hv, wg, wv, bg, bv, a_ref):
        first_tile = pl.program_id(1) == 0

        def strip(r0, first_strip):
            for cs in _lane_tiles():
                cg = _conv(_strip_taps(ug, hg, r0, cs, first_strip, first_tile), wg, bg, cs)
                cv = _conv(_strip_taps(uv, hv, r0, cs, first_strip, first_tile), wv, bv, cs)
                a_ref[pl.ds(r0, CONV_RS), cs] = _bf(0.5 * cg * (1.0 + lax.erf(cg * SQRT_HALF)) * cv)

        strip(0, True)
        lax.fori_loop(1, tm // CONV_RS, lambda k, c: (strip(_strip_start(k), False), c)[1], 0)

    return pl.pallas_call(
        body, name=name, grid=(nc, S // tm),
        in_specs=[tile(0), halo(0), tile(nc), halo(nc), small(3, 0), small(3, nc), small(1, 0), small(1, nc)],
        out_specs=tile(0), out_shape=_sds((S, D_FF), BF16), compiler_params=_cp(2))(u, u, u, u, cw, cw, cb, cb)


def _conv_gelu_bwd(u, da, cw, cb, name, plans=None):
    S = u.shape[0]
    tm = CONV_TM
    nt = S // tm
    nc, tile, halo, small = _conv_specs(tm)

    def body(ug, hg, uv, hv, wg, wv, bg, bv, da_ref, dcg_ref, dcv_ref, dwg_ref, dwv_ref, dbg_ref, dbv_ref, acc):
        i = pl.program_id(1)
        first_tile = i == 0

        @pl.when(first_tile)
        def _():
            acc[...] = jnp.zeros_like(acc)

        def strip(r0, first_strip):
            rows = pl.ds(r0, CONV_RS)
            for cs in _lane_tiles():
                tg = _strip_taps(ug, hg, r0, cs, first_strip, first_tile)
                tv = _strip_taps(uv, hv, r0, cs, first_strip, first_tile)
                cg = _conv(tg, wg, bg, cs)
                cv = _conv(tv, wv, bv, cs)
                phi = 0.5 * (1.0 + lax.erf(cg * SQRT_HALF))
                dav = da_ref[rows, cs]
                dcg = dav * cv * (phi + cg * jnp.exp(-0.5 * cg * cg) * INV_SQRT_2PI)
                dcv = dav * (cg * phi)
                dcg_ref[rows, cs] = dcg
                dcv_ref[rows, cs] = dcv
                for half, (dc, taps) in enumerate(((dcg, tg), (dcv, tv))):
                    for j in range(3):
                        acc[4 * half + j, :, cs] += dc * taps[j]
                    acc[4 * half + 3, :, cs] += dc

        strip(0, True)
        lax.fori_loop(1, tm // CONV_RS, lambda k, c: (strip(_strip_start(k), False), c)[1], 0)

        @pl.when(i == nt - 1)
        def _():
            for half, (dw_ref, db_ref) in enumerate(((dwg_ref, dbg_ref), (dwv_ref, dbv_ref))):
                for j in range(3):
                    dw_ref[j:j + 1, :] = jnp.sum(acc[4 * half + j], axis=0, keepdims=True)
                db_ref[...] = jnp.sum(acc[4 * half + 3], axis=0, keepdims=True)

    res, carried = _call(
        body, plans, name=name, grid=(nc, nt),
        in_specs=[tile(0), halo(0), tile(nc), halo(nc), small(3, 0), small(3, nc), small(1, 0), small(1, nc), tile(0)],
        out_specs=[tile(0), tile(0), small(3, 0), small(3, 0), small(1, 0), small(1, 0)],
        out_shape=[_sds((S, D_FF), F32)] * 2 + [_sds((3, D_FF), F32)] * 2 + [_sds((1, D_FF), F32)] * 2,
        scratch_shapes=[pltpu.VMEM((8, CONV_RS, CONV_CB), F32)], args=(u, u, u, u, cw, cw, cb, cb, da))
    return res if plans is None else (res, carried)


def _conv_input_bwd(dcg, dcv, cw, name, plans=None):
    S = dcg.shape[0]
    tm = CONV_TM // 2
    nh = tm // HALO
    nt = S // tm
    n = CONV_RS + HALO
    tile = pl.BlockSpec((tm, D_FF), lambda i: (i, 0))
    nxt = pl.BlockSpec((HALO, D_FF), lambda i: (jnp.minimum((i + 1) * nh, S // HALO - 1), 0))

    def body(g_ref, ng_ref, v_ref, nv_ref, w_ref, du_ref):
        last_tile = pl.program_id(0) == nt - 1

        def strip(r0, last_strip):
            for half, (dc_ref, n_ref) in enumerate(((g_ref, ng_ref), (v_ref, nv_ref))):
                for k in range(D_FF // LANES):
                    cs = slice(k * LANES, (k + 1) * LANES)
                    ws = slice(half * D_FF + k * LANES, half * D_FF + (k + 1) * LANES)
                    if last_strip:
                        after = jnp.where(last_tile, 0.0, n_ref[:, cs])
                        blk = jnp.concatenate([dc_ref[tm - CONV_RS:tm, cs], after], axis=0)
                    else:
                        blk = dc_ref[pl.ds(r0, n), cs]
                    d1 = pltpu.roll(blk, n - 1, 0)[:CONV_RS]
                    d2 = pltpu.roll(blk, n - 2, 0)[:CONV_RS]
                    du_ref[pl.ds(r0, CONV_RS), ws] = _bf(w_ref[2:3, ws] * blk[:CONV_RS] + w_ref[1:2, ws] * d1
                                                         + w_ref[0:1, ws] * d2)

        lax.fori_loop(0, tm // CONV_RS - 1, lambda k, c: (strip(_strip_start(k), False), c)[1], 0)
        strip(tm - CONV_RS, True)

    (du,), carried = _call(
        body, plans, name=name, grid=(nt,),
        in_specs=[tile, nxt, tile, nxt, pl.BlockSpec((3, 2 * D_FF), lambda i: (0, 0))],
        out_specs=[pl.BlockSpec((tm, 2 * D_FF), lambda i: (i, 0))], out_shape=[_sds((S, 2 * D_FF), BF16)],
        args=(dcg, dcg, dcv, dcv, cw))
    return du if plans is None else (du, carried)


def _row_tile(n, cap):
    best = n
    for t in range(16, cap + 1, 16):
        if n % t == 0:
            best = t
    return best if best <= cap else n


def _rows_for_bytes(nbytes, cols):
    return max(16, nbytes // (4 * cols) // 16 * 16)


def _adamw(w, g, m, v, name):
    R, C = w.shape
    tr = _row_tile(R, _rows_for_bytes(2 << 20, C))

    def body(w_ref, g_ref, m_ref, v_ref, d_ref, nm_ref, nv_ref):
        gv = g_ref[...]
        nm = ADAM_B1 * m_ref[...] + (1.0 - ADAM_B1) * gv
        nv = ADAM_B2 * v_ref[...] + (1.0 - ADAM_B2) * (gv * gv)
        m_hat = nm / (1.0 - ADAM_B1 ** ADAM_STEP)
        v_hat = nv / (1.0 - ADAM_B2 ** ADAM_STEP)
        d_ref[...] = -ADAM_LR * (m_hat / (jnp.sqrt(v_hat) + ADAM_EPS) + ADAM_WD * w_ref[...])
        nm_ref[...] = nm
        nv_ref[...] = nv

    spec = pl.BlockSpec((tr, C), lambda i: (i, 0))
    return pl.pallas_call(body, name=name, grid=(R // tr,), in_specs=[spec] * 4, out_specs=[spec] * 3,
                          out_shape=[_sds((R, C), F32)] * 3, compiler_params=_cp(1))(w, g, m, v)


def _pair_sum(gfull, rcv, c_idx, name):
    nb, R, C = gfull.shape
    half = R // 2
    tr = _row_tile(half, _rows_for_bytes(2 << 20, C))
    nt = half // tr

    def body(c_ref, g_ref, r_ref, o_ref):
        o_ref[...] = _bf(g_ref[...] + r_ref[...])

    return pl.pallas_call(
        body, name=name,
        grid_spec=pltpu.PrefetchScalarGridSpec(
            num_scalar_prefetch=1, grid=(nb, nt),
            in_specs=[pl.BlockSpec((None, tr, C), lambda j, i, c_ref: (j, c_ref[0] * nt + i, 0)),
                      pl.BlockSpec((None, tr, C), lambda j, i, c_ref: (j, i, 0))],
            out_specs=pl.BlockSpec((None, tr, C), lambda j, i, c_ref: (j, i, 0))),
        out_shape=_sds((nb, half, C), BF16), compiler_params=_cp(2))(c_idx, gfull, rcv)


def _chip_sum(arrived, own, place, name):
    nb, H, C = arrived.shape
    tr = _row_tile(H, _rows_for_bytes(2 << 20, C))
    nt = H // tr

    def body(pl_ref, *refs):
        o_ref = refs[nb + 1]
        me = pl_ref[0]
        acc = None
        for k in range(nb):
            term = jnp.where(me == k, refs[nb][...], refs[k][...]).astype(F32)
            acc = term if acc is None else acc + term
        o_ref[...] = acc

    def other(k):
        return pl.BlockSpec((None, tr, C), lambda i, p: (jnp.where(p[0] == k, (k + 1) % nb, k), i, 0))

    return pl.pallas_call(
        body, name=name,
        grid_spec=pltpu.PrefetchScalarGridSpec(
            num_scalar_prefetch=1, grid=(nt,),
            in_specs=[other(k) for k in range(nb)] + [pl.BlockSpec((None, tr, C), lambda i, p: (p[0], i, 0))],
            out_specs=pl.BlockSpec((tr, C), lambda i, p: (p[1] * nt + i, 0))),
        out_shape=_sds((2 * H, C), F32), compiler_params=_cp(1))(place, *([arrived] * nb), own)


def _cast_into_slot(shard, place, name):
    R, C = shard.shape
    tr = _row_tile(R, 256)

    def body(pl_ref, s_ref, o_ref):
        o_ref[...] = _bf(s_ref[...])

    return pl.pallas_call(
        body, name=name,
        grid_spec=pltpu.PrefetchScalarGridSpec(
            num_scalar_prefetch=1, grid=(R // tr,),
            in_specs=[pl.BlockSpec((tr, C), lambda i, p: (i, 0))],
            out_specs=pl.BlockSpec((None, tr, C), lambda i, p: (p[0], i, 0))),
        out_shape=_sds((N_CHIPS, R, C), BF16), compiler_params=_cp(1))(place, shard)


def _place():
    x, y, c = lax.axis_index("x"), lax.axis_index("y"), lax.axis_index("c")
    chips = [(1 - x, y), (x, 1 - y), (1 - x, 1 - y)]
    return x, y, c, chips


def _chip_id(px, py):
    return 2 * px + py


def _remote(src, dst, send_sems, recv_sems, k, to):
    return pltpu.make_async_remote_copy(src_ref=src, dst_ref=dst, send_sem=send_sems.at[k], recv_sem=recv_sems.at[k],
                                        device_id=to, device_id_type=MESH)


def _gather_weights(slots, wholes, name):
    ns, nw = len(slots), len(wholes)
    n = ns + nw

    def body(*refs):
        ins = refs[ns:n]
        outs = refs[n:2 * n]
        send_sems, recv_sems, local_sems = refs[2 * n:]
        x, y, c, chips = _place()
        me = _chip_id(x, y)
        sib = (x, y, 1 - c)
        local = [pltpu.make_async_copy(ins[b], outs[ns + b].at[me], local_sems.at[b]) for b in range(nw)]
        for cp in local:
            cp.start()
        sent = []
        for a in range(n):
            R = outs[a].shape[1]
            rows = pl.ds(c * (R // 2), R // 2) if a < ns else pl.ds(0, R)
            src = outs[a].at[me, rows] if a < ns else ins[a - ns]
            for j, chip in enumerate(chips):
                cp = _remote(src, outs[a].at[me, rows], send_sems, recv_sems, 6 * a + j, (*chip, c))
                cp.start()
                sent.append(cp)
        for a in range(n):
            R = outs[a].shape[1]
            rows = pl.ds(c * (R // 2), R // 2) if a < ns else pl.ds(0, R)
            for j, chip in enumerate(chips):
                landed = outs[a].at[_chip_id(*chip), rows]
                _remote(landed, landed, send_sems, recv_sems, 6 * a + j, (*chip, c)).wait_recv()
                if a < ns:
                    cp = _remote(landed, landed, send_sems, recv_sems, 6 * a + 3 + j, sib)
                    cp.start()
                    sent.append(cp)
        for a in range(ns):
            R = outs[a].shape[1]
            other = pl.ds((1 - c) * (R // 2), R // 2)
            for j, chip in enumerate(chips):
                passed = outs[a].at[_chip_id(*chip), other]
                _remote(passed, passed, send_sems, recv_sems, 6 * a + 3 + j, sib).wait_recv()
        for cp in sent:
            cp.wait_send()
        for cp in local:
            cp.wait()

    return pl.pallas_call(
        body, name=name, in_specs=[ANY] * n, out_specs=[ANY] * n,
        out_shape=[_sds(s.shape, s.dtype) for s in slots] + [_sds((N_CHIPS, *s.shape), s.dtype) for s in wholes],
        input_output_aliases={a: a for a in range(ns)},
        scratch_shapes=[pltpu.SemaphoreType.DMA((6 * n,)), pltpu.SemaphoreType.DMA((6 * n,)),
                        pltpu.SemaphoreType.DMA((max(nw, 1),))])(*slots, *wholes)


def _gather_ici_plan(slots, wholes):
    ns, nw = len(slots), len(wholes)

    def copies(ins, ios, outs, send_sems, recv_sems, local_sems):
        x, y, c, chips = _place()
        me = _chip_id(x, y)
        sends, recvs = [], []
        for a in range(ns + nw):
            dst = ios[a] if a < ns else outs[a - ns]
            R = dst.shape[1]
            rows = pl.ds(c * (R // 2), R // 2) if a < ns else pl.ds(0, R)
            src = dst.at[me, rows] if a < ns else ins[a - ns]
            for j, chip in enumerate(chips):
                sends.append(_remote(src, dst.at[me, rows], send_sems, recv_sems, 3 * a + j, (*chip, c)))
                landed = dst.at[_chip_id(*chip), rows]
                recvs.append(_remote(landed, landed, send_sems, recv_sems, 3 * a + j, (*chip, c)))
        local = [pltpu.make_async_copy(ins[b], outs[b].at[me], local_sems.at[b]) for b in range(nw)]
        return sends, recvs, local

    return _Plan(copies, 3 * (ns + nw), ins=wholes, inouts=slots,
                 outs=[_sds((N_CHIPS, *s.shape), s.dtype) for s in wholes])


def _gather_pass_plan(slots):
    def copies(ins, ios, outs, send_sems, recv_sems, local_sems):
        x, y, c, chips = _place()
        sib = (x, y, 1 - c)
        sends, recvs = [], []
        for a, buf in enumerate(ios):
            half = buf.shape[1] // 2
            for j, chip in enumerate(chips):
                mine = buf.at[_chip_id(*chip), pl.ds(c * half, half)]
                other = buf.at[_chip_id(*chip), pl.ds((1 - c) * half, half)]
                sends.append(_remote(mine, mine, send_sems, recv_sems, 3 * a + j, sib))
                recvs.append(_remote(other, other, send_sems, recv_sems, 3 * a + j, sib))
        return sends, recvs, []

    return _Plan(copies, 3 * len(slots), inouts=slots)


def _pair_plan(grads):
    def copies(ins, ios, outs, send_sems, recv_sems, local_sems):
        x, y, c, _ = _place()
        sib = (x, y, 1 - c)
        sends, recvs = [], []
        for a, g in enumerate(ins):
            half = g.shape[1] // 2
            sends.append(_remote(g.at[:, pl.ds((1 - c) * half, half), :], outs[a], send_sems, recv_sems, a, sib))
            recvs.append(_remote(outs[a], outs[a], send_sems, recv_sems, a, sib))
        return sends, recvs, []

    return _Plan(copies, len(grads), ins=grads,
                 outs=[_sds((g.shape[0], g.shape[1] // 2, g.shape[2]), g.dtype) for g in grads])


def _chip_plan(parts):
    def copies(ins, ios, outs, send_sems, recv_sems, local_sems):
        x, y, c, chips = _place()
        me = _chip_id(x, y)
        sends, recvs = [], []
        for a, part in enumerate(ins):
            for j, chip in enumerate(chips):
                sends.append(_remote(part.at[_chip_id(*chip)], outs[a].at[me], send_sems, recv_sems, 3 * a + j, (*chip, c)))
                landed = outs[a].at[_chip_id(*chip)]
                recvs.append(_remote(landed, landed, send_sems, recv_sems, 3 * a + j, (*chip, c)))
        return sends, recvs, []

    return _Plan(copies, 3 * len(parts), ins=parts, outs=[_sds(p.shape, p.dtype) for p in parts])


def _pair_concat(fulls, name):
    n = len(fulls)

    def body(*refs):
        outs = refs[n:2 * n]
        send_sems, recv_sems = refs[2 * n:]
        x, y, c, _ = _place()
        cps = []
        for a in range(n):
            H = outs[a].shape[0] // 2
            mine = outs[a].at[pl.ds(c * H, H)]
            cp = _remote(mine, mine, send_sems, recv_sems, a, (x, y, 1 - c))
            cp.start()
            cps.append(cp)
        for a, cp in enumerate(cps):
            H = outs[a].shape[0] // 2
            other = outs[a].at[pl.ds((1 - c) * H, H)]
            _remote(other, other, send_sems, recv_sems, a, (x, y, 1 - c)).wait_recv()
            cp.wait_send()

    return pl.pallas_call(
        body, name=name, in_specs=[ANY] * n, out_specs=[ANY] * n,
        out_shape=[_sds(f.shape, f.dtype) for f in fulls], input_output_aliases={a: a for a in range(n)},
        scratch_shapes=[pltpu.SemaphoreType.DMA((n,)), pltpu.SemaphoreType.DMA((n,))])(*fulls)


def _all_sum(pack, name):
    R, C = pack.shape

    def body(p_ref, o_ref, buf, send_sems, recv_sems):
        x, y, c, _ = _place()
        me = 4 * x + 2 * y + c
        buf[me] = p_ref[...]
        cps = []
        for k in range(1, N_DEV):
            to = (x ^ (k >> 2), y ^ ((k >> 1) & 1), c ^ (k & 1))
            cp = _remote(p_ref, buf.at[me], send_sems, recv_sems, k - 1, to)
            cp.start()
            cps.append(cp)
        for k in range(1, N_DEV):
            frm = (x ^ (k >> 2), y ^ ((k >> 1) & 1), c ^ (k & 1))
            slot = buf.at[4 * frm[0] + 2 * frm[1] + frm[2]]
            _remote(slot, slot, send_sems, recv_sems, k - 1, frm).wait_recv()
        acc = buf[0]
        for k in range(1, N_DEV):
            acc = acc + buf[k]
        o_ref[...] = acc
        for cp in cps:
            cp.wait_send()

    vm = pl.BlockSpec(memory_space=pltpu.VMEM)
    return pl.pallas_call(
        body, name=name, in_specs=[vm], out_specs=vm, out_shape=_sds((R, C), F32),
        scratch_shapes=[pltpu.VMEM((N_DEV, R, C), F32), pltpu.SemaphoreType.DMA((N_DEV - 1,)),
                        pltpu.SemaphoreType.DMA((N_DEV - 1,))])(pack)


def _local_step(xs, tgt, p, ex):
    proj, h1, got = _norm_proj(xs, p["pre_mix_norm"], ex.weight("w_in"), "proj_in", plans=ex.carry("proj_in"))
    ex.done("proj_in", got)
    biases = _relbias_fwd(p["rel_bias"], "rel_bias_fwd")
    fw = []
    for g in range(N_GROUPS):
        res, got = _attn_fwd(proj, biases[g], g, f"attn_fwd{g}", plans=ex.carry(f"attn_fwd{g}"))
        ex.done(f"attn_fwd{g}", got)
        fw.append(res)
    yh, o_h, states = _hgrn_fwd(proj, p["hgrn_lb_raw"], p["hgrn_norm"], "hgrn_fwd")
    W_a, W_h, W_out = ex.weight("w_branch_attn"), ex.weight("w_branch_hgrn"), ex.weight("w_out")
    W_up, W_down, conv_w = ex.weight("w_up"), ex.weight("w_down"), ex.weight("conv_w")
    y, lse, za, zh, merged = _branch_fwd([t[0] for t in fw], [t[1] for t in fw], yh, proj, W_a, W_h, "branch_fwd")
    mo, x1, h2 = _mix_out(merged, W_out, xs, p["post_mix_norm"], p["pre_ffn_norm"], "mix_out")
    u = _mm_nn_blk(h2, W_up, "ffn_up")
    a = _conv_gelu_fwd(u, conv_w, p["conv_b"], "conv_gelu_fwd")
    dx2, dff, g_post_ffn, loss = _loss_head(a, W_down, x1, tgt, p["post_ffn_norm"], "ffn_down_loss")

    da = _mm_nt(dff, W_down, "d_ffn_act")
    ex.grad("w_down", _mm_tn(a, dff, "g_w_down").reshape(N_CHIPS, D_FF // N_CHIPS, D_MODEL))
    (dcg, dcv, gwg, gwv, gbg, gbv), got = _conv_gelu_bwd(u, da, conv_w, p["conv_b"], "conv_gelu_bwd",
                                                          plans=ex.carry("conv_gelu_bwd"))
    ex.done("conv_gelu_bwd", got)
    g_conv_w = jnp.concatenate([gwg, gwv], axis=1)
    g_conv_b = jnp.concatenate([gbg, gbv], axis=1)
    du, got = _conv_input_bwd(dcg, dcv, conv_w, "conv_input_bwd", plans=ex.carry("conv_input_bwd"))
    ex.done("conv_input_bwd", got)
    dh2 = _mm_nt_blk(du, W_up, "d_ffn_in")
    ex.grad("w_up", _mm_tn_blk(h2, du, N_CHIPS, "g_w_up"))
    (dx1, g_pre_ffn), got = _prenorm_bwd(dh2, x1, p["pre_ffn_norm"], dx2, "pre_ffn_norm_bwd",
                                         plans=ex.carry("pre_ffn_norm_bwd"))
    ex.done("pre_ffn_norm_bwd", got)
    dmo, dmerged, g_post_mix = _postnorm_bwd(dx1, mo, p["post_mix_norm"], W_out, "post_mix_norm_bwd")
    ex.grad("w_out", _mm_tn(merged, dmo, "g_w_out").reshape(N_CHIPS, D_MODEL // N_CHIPS, D_MODEL))
    dza, dzh, dg0, dg1, dy, dyh = _branch_bwd(dmerged, za, zh, proj, W_a, W_h, "branch_bwd")
    ex.grad("w_branch_attn", _mm_tn_blk(y, dza, N_CHIPS, "g_w_branch_attn", together=True))
    ex.grad("w_branch_hgrn", _mm_tn_blk(yh, dzh, N_CHIPS, "g_w_branch_hgrn", together=True))
    dqkv, dbs = [], []
    for g in range(N_GROUPS):
        parts, db, got = _attn_bwd(proj, biases[g], lse, y, dy, g, f"attn_bwd{g}", plans=ex.carry(f"attn_bwd{g}"))
        ex.done(f"attn_bwd{g}", got)
        dqkv += parts
        dbs.append(db)
    g_rel_bias = _relbias_bwd(dbs, "rel_bias_bwd")
    dhg, g_lb_raw, g_hgrn_norm = _hgrn_bwd(proj, p["hgrn_lb_raw"], p["hgrn_norm"], o_h, states, dyh, "hgrn_bwd")
    dproj = jnp.concatenate([*[_bf(t) for t in dqkv], *dhg, dg0, dg1], axis=1)
    for piece in W_IN_PIECES:
        g, got = _mm_tn_blk(h1, dproj, N_CHIPS, f"g_{piece}", x_cols=W_IN_ROWS[piece],
                            plans=ex.carry(f"g_{piece}"))
        ex.done(f"g_{piece}", got)
        ex.grad(piece, g)
    dh1, got = _mm_nt_blk(dproj, ex.weight("w_in"), "d_proj_in", plans=ex.carry("d_proj_in"))
    ex.done("d_proj_in", got)
    (grad_x, g_pre_mix), got = _prenorm_bwd(dh1, xs, p["pre_mix_norm"], dx1, "pre_mix_norm_bwd",
                                            plans=ex.carry("pre_mix_norm_bwd"))
    ex.done("pre_mix_norm_bwd", got)
    small = dict(pre_mix_norm=g_pre_mix, rel_bias=g_rel_bias, hgrn_lb_raw=g_lb_raw, hgrn_norm=g_hgrn_norm,
                 post_mix_norm=g_post_mix, pre_ffn_norm=g_pre_ffn, conv_w=g_conv_w, conv_b=g_conv_b,
                 post_ffn_norm=g_post_ffn)
    return loss, grad_x, small


SMALL = ("pre_mix_norm", "rel_bias", "hgrn_lb_raw", "hgrn_norm", "post_mix_norm", "pre_ffn_norm", "conv_w", "conv_b",
         "post_ffn_norm")
BIG = ("w_in", "w_up", "w_down", "w_out", "w_branch_attn", "w_branch_hgrn")
WEIGHTS = ("pre_mix_norm", "w_in", "rel_bias", "hgrn_lb_raw", "hgrn_norm", "w_branch_attn", "w_branch_hgrn", "w_out",
           "post_mix_norm", "pre_ffn_norm", "w_up", "conv_w", "conv_b", "w_down", "post_ffn_norm")
MIXER = ("w_out", "w_branch_attn", "w_branch_hgrn")

SCHEDULE = {
    "proj_in": [("gather_ici_cw", ("w_up",) + MIXER)],
    "attn_fwd0": [("gather_pass", ("w_up",) + MIXER), ("gather_ici", ("w_down",))],
    "attn_fwd1": [("gather_pass", ("w_down",))],
    "conv_gelu_bwd": [("pair", ("w_down",))],
    "conv_input_bwd": [("chip", ("w_down",))],
    "pre_ffn_norm_bwd": [("pair", ("w_up",))],
    "attn_bwd0": [("chip", ("w_up",)), ("pair", MIXER)],
    "attn_bwd1": [("chip", MIXER)],
    "g_w_in_b": [("pair", ("w_in_a",))],
    "d_proj_in": [("chip", ("w_in_a",)), ("pair", ("w_in_b",))],
    "pre_mix_norm_bwd": [("chip", ("w_in_b",))],
}
W_IN_ROWS = dict(w_in_a=(0, 768), w_in_b=(3, 256))
W_IN_PIECES = tuple(W_IN_ROWS)
REDUCED = W_IN_PIECES + BIG[1:]


class _Exchange:
    def __init__(self, place, slots, conv_w_shard):
        self.place, self.slots, self.conv_w_shard = place, dict(slots), conv_w_shard
        self.conv_w = None
        self.g, self.from_sibling, self.pair_sums, self.arrived = {}, {}, {}, {}
        self.pending = []

    def weight(self, name):
        if name == "conv_w":
            return self.conv_w
        w = self.slots[name]
        return w.reshape(-1, D_MODEL) if name in ("w_out", "w_down") else w

    def grad(self, name, g):
        self.g[name] = g

    def carry(self, point):
        plans = []
        self.pending = SCHEDULE.get(point, [])
        for kind, names in self.pending:
            if kind in ("gather_ici", "gather_ici_cw"):
                wholes = [self.conv_w_shard] if kind == "gather_ici_cw" else []
                plans.append(_gather_ici_plan([self.slots[n] for n in names], wholes))
            elif kind == "gather_pass":
                plans.append(_gather_pass_plan([self.slots[n] for n in names]))
            elif kind == "pair":
                plans.append(_pair_plan([self.g[n] for n in names]))
            else:
                for n in names:
                    self.pair_sums[n] = _pair_sum(self.g[n], self.from_sibling[n], self.place[1:2], f"pair_sum_{n}")
                plans.append(_chip_plan([self.pair_sums[n] for n in names]))
        return plans

    def done(self, point, carried):
        for (kind, names), got in zip(self.pending, carried):
            if kind in ("gather_ici", "gather_ici_cw", "gather_pass"):
                self.slots.update(zip(names, got))
                if kind == "gather_ici_cw":
                    self.conv_w = got[len(names)].transpose(1, 0, 2).reshape(3, 2 * D_FF)
            elif kind == "pair":
                self.from_sibling.update(zip(names, got))
            else:
                self.arrived.update(zip(names, got))

    def reduced(self):
        halves = [_chip_sum(self.arrived[n], self.pair_sums[n], self.place, f"chip_sum_{n}") for n in REDUCED]
        out = dict(zip(REDUCED, _pair_concat(halves, "pair_concat")))
        out["w_in"] = jnp.concatenate([out.pop(n) for n in W_IN_PIECES], axis=0)
        return out


def kernel(x, pre_mix_norm, w_in, rel_bias, hgrn_lb_raw, hgrn_norm, w_branch_attn, w_branch_hgrn, w_out, post_mix_norm, pre_ffn_norm, w_up, conv_w, conv_b, w_down, post_ffn_norm, loss_target, m_pre_mix_norm, m_w_in, m_rel_bias, m_hgrn_lb_raw, m_hgrn_norm, m_w_branch_attn, m_w_branch_hgrn, m_w_out, m_post_mix_norm, m_pre_ffn_norm, m_w_up, m_conv_w, m_conv_b, m_w_down, m_post_ffn_norm, v_pre_mix_norm, v_w_in, v_rel_bias, v_hgrn_lb_raw, v_hgrn_norm, v_w_branch_attn, v_w_branch_hgrn, v_w_out, v_post_mix_norm, v_pre_ffn_norm, v_w_up, v_conv_w, v_conv_b, v_w_down, v_post_ffn_norm):
    w = dict(pre_mix_norm=pre_mix_norm, w_in=w_in, rel_bias=rel_bias, hgrn_lb_raw=hgrn_lb_raw, hgrn_norm=hgrn_norm,
             w_branch_attn=w_branch_attn, w_branch_hgrn=w_branch_hgrn, w_out=w_out, post_mix_norm=post_mix_norm,
             pre_ffn_norm=pre_ffn_norm, w_up=w_up, conv_w=conv_w, conv_b=conv_b, w_down=w_down,
             post_ffn_norm=post_ffn_norm)
    m = dict(pre_mix_norm=m_pre_mix_norm, w_in=m_w_in, rel_bias=m_rel_bias, hgrn_lb_raw=m_hgrn_lb_raw,
             hgrn_norm=m_hgrn_norm, w_branch_attn=m_w_branch_attn, w_branch_hgrn=m_w_branch_hgrn, w_out=m_w_out,
             post_mix_norm=m_post_mix_norm, pre_ffn_norm=m_pre_ffn_norm, w_up=m_w_up, conv_w=m_conv_w,
             conv_b=m_conv_b, w_down=m_w_down, post_ffn_norm=m_post_ffn_norm)
    v = dict(pre_mix_norm=v_pre_mix_norm, w_in=v_w_in, rel_bias=v_rel_bias, hgrn_lb_raw=v_hgrn_lb_raw,
             hgrn_norm=v_hgrn_norm, w_branch_attn=v_w_branch_attn, w_branch_hgrn=v_w_branch_hgrn, w_out=v_w_out,
             post_mix_norm=v_post_mix_norm, pre_ffn_norm=v_pre_ffn_norm, w_up=v_w_up, conv_w=v_conv_w,
             conv_b=v_conv_b, w_down=v_w_down, post_ffn_norm=v_post_ffn_norm)
    shard2d = {n: (w[n][0] if w[n].ndim == 3 else w[n]) for n in WEIGHTS}
    chip = 2 * lax.axis_index("x") + lax.axis_index("y")
    core = lax.axis_index("c")

    place = jnp.stack([chip, core]).astype(jnp.int32)
    slots = {n: _cast_into_slot(shard2d[n], place, f"cast_{n}") for n in BIG}
    slots["w_in"] = _gather_weights([slots["w_in"]], [], "gather_w_in")[0]
    ex = _Exchange(place, slots, shard2d["conv_w"])
    loss, grad_x, small = _local_step(x[0], loss_target[0], {n: w[n] for n in SMALL if n != "conv_w"}, ex)

    flat = [small[n].reshape(-1) for n in SMALL] + [loss.reshape(-1)]
    sizes = [t.shape[0] for t in flat]
    summed = _all_sum(jnp.concatenate(flat).reshape(-1, LANES), "sum_small").reshape(-1)
    offs = [sum(sizes[:i]) for i in range(len(sizes))]
    grads = {}
    for n, o, sz in zip(SMALL, offs, sizes):
        grads[n] = summed[o:o + sz].reshape(small[n].shape)
    loss_total = summed[offs[-1]]
    cw = 2 * D_FF // N_CHIPS
    grads["conv_w"] = lax.dynamic_slice(grads["conv_w"], (0, chip * cw), (3, cw))

    grads.update(ex.reduced())

    out_g, out_d, out_m, out_v = [], [], [], []
    for n in WEIGHTS:
        d2, m2, v2 = _adamw(shard2d[n], grads[n], m[n].reshape(shard2d[n].shape), v[n].reshape(shard2d[n].shape),
                            f"adamw_{n}")
        shape = w[n].shape
        out_g.append(grads[n].reshape(shape))
        out_d.append(d2.reshape(shape))
        out_m.append(m2.reshape(shape))
        out_v.append(v2.reshape(shape))
    return (loss_total, grad_x[None], *out_g, *out_d, *out_m, *out_v)
```

```python
import functools
import math

import jax
import jax.numpy as jnp
from jax import lax
from jax.experimental import pallas as pl
from jax.experimental.pallas import tpu as pltpu

F32 = jnp.float32
BF16 = jnp.bfloat16
MESH = pl.DeviceIdType.MESH

D_MODEL = 1024
N_GROUPS = 3
DILATIONS = (1, 4, 16)
HEADS = 8
HEAD_DIM = 64
GROUP_W = HEADS * HEAD_DIM
QKV_W = N_GROUPS * 3 * GROUP_W
BLK = 128
NEG_INF = -1e30
NUM_BUCKETS = 32
MAX_EXACT = 16
MAX_DISTANCE = 2048
HG_HEADS = 4
HG_DK = 128
HG_W = HG_HEADS * HG_DK
HG_CHUNK = 32
HG_TILE = 256
IN_W = QKV_W + 4 * HG_W + 2 * D_MODEL
D_FF = 2816
EPS = 1e-6
N_CHIPS = 4
N_DEV = 8
LANES = 128

ADAM_LR, ADAM_B1, ADAM_B2, ADAM_EPS, ADAM_WD, ADAM_STEP = 0.001, 0.9, 0.999, 1e-08, 0.01, 10

VMEM_LIMIT = 56 * 1024 * 1024


def _cp(n_axes):
    return pltpu.CompilerParams(dimension_semantics=("arbitrary",) * n_axes, vmem_limit_bytes=VMEM_LIMIT)


def _sds(shape, dtype):
    return jax.ShapeDtypeStruct(tuple(shape), dtype)


def _sigmoid(v):
    return 1.0 / (1.0 + jnp.exp(-v))


def _bf(v):
    return v.astype(BF16)


def _dot(a, b, dims):
    return lax.dot_general(a, b, (dims, ((), ())), preferred_element_type=F32)


NN = ((1,), (0,))
NT = ((1,), (1,))
TN = ((0,), (0,))

ANY = pl.BlockSpec(memory_space=pl.ANY)


class _Plan:
    def __init__(self, copies, n_sems, ins=(), inouts=(), outs=()):
        self.copies, self.n_sems = copies, n_sems
        self.ins, self.inouts, self.outs = list(ins), list(inouts), list(outs)


def _call(body, plans=None, *, name, grid, in_specs, out_specs, out_shape, args, scratch_shapes=()):
    plans = list(plans or ())
    in_specs, out_specs, out_shape = list(in_specs), list(out_specs), list(out_shape)
    scratch_shapes = list(scratch_shapes)
    n_in, n_out, n_scr = len(in_specs), len(out_specs), len(scratch_shapes)
    x_in, x_out, aliases, spans = [], [], {}, []
    for p in plans:
        i0, o0 = len(x_in), len(x_out)
        x_in += p.ins
        for a in p.inouts:
            aliases[n_in + len(x_in)] = n_out + len(x_out)
            x_in.append(a)
            x_out.append(_sds(a.shape, a.dtype))
        x_out += p.outs
        spans.append((i0, len(p.ins), o0, len(p.inouts), len(p.outs)))
    sems = [pltpu.SemaphoreType.DMA((p.n_sems,)) for p in plans for _ in range(3)]

    def wrapped(*refs):
        xi = refs[n_in:n_in + len(x_in)]
        base = n_in + len(x_in)
        xo = refs[base + n_out:base + n_out + len(x_out)]
        sbase = base + n_out + len(x_out)
        xs = refs[sbase + n_scr:]
        ids = [pl.program_id(k) for k in range(len(grid))]
        first = functools.reduce(jnp.logical_and, [i == 0 for i in ids])
        last = functools.reduce(jnp.logical_and, [i == g - 1 for i, g in zip(ids, grid)])

        def descriptors(k):
            i0, ni, o0, nio, no = spans[k]
            return plans[k].copies(xi[i0:i0 + ni], xo[o0:o0 + nio], xo[o0 + nio:o0 + nio + no], *xs[3 * k:3 * k + 3])

        @pl.when(first)
        def _():
            for k in range(len(plans)):
                sends, _, local = descriptors(k)
                for cp in (*sends, *local):
                    cp.start()

        body(*refs[:n_in], *refs[base:base + n_out], *refs[sbase:sbase + n_scr])

        @pl.when(last)
        def _():
            for k in range(len(plans)):
                sends, recvs, local = descriptors(k)
                for cp in recvs:
                    cp.wait_recv()
                for cp in sends:
                    cp.wait_send()
                for cp in local:
                    cp.wait()

    res = pl.pallas_call(
        wrapped if plans else body, name=name, grid=grid, in_specs=in_specs + [ANY] * len(x_in),
        out_specs=out_specs + [ANY] * len(x_out), out_shape=out_shape + x_out, input_output_aliases=aliases,
        scratch_shapes=scratch_shapes + sems, compiler_params=_cp(len(grid)))(*args, *x_in)
    res = list(res)
    carried = [res[n_out + o0:n_out + o0 + nio + no] for (_, _, o0, nio, no) in spans]
    return res[:n_out], carried


def _mm_nn_blk(a, wg, name, tm=512, plans=None):
    M, K = a.shape
    nb, _, Nb = wg.shape

    def body(a_ref, w_ref, o_ref):
        o_ref[...] = _dot(_bf(a_ref[...]), w_ref[...], NN)

    (out,), carried = _call(
        body, plans, name=name, grid=(nb, M // tm),
        in_specs=[pl.BlockSpec((tm, K), lambda j, i: (i, 0)), pl.BlockSpec((None, K, Nb), lambda j, i: (j, 0, 0))],
        out_specs=[pl.BlockSpec((tm, Nb), lambda j, i: (i, j))],
        out_shape=[_sds((M, nb * Nb), F32)], args=(a, wg))
    return out if plans is None else (out, carried)


def _mm_nt_blk(dy, wg, name, tm=1024, plans=None):
    M = dy.shape[0]
    nb, K, Nb = wg.shape

    def body(dy_ref, w_ref, o_ref):
        j = pl.program_id(1)
        r = _dot(_bf(dy_ref[...]), w_ref[...], NT)

        @pl.when(j == 0)
        def _():
            o_ref[...] = r

        @pl.when(j > 0)
        def _():
            o_ref[...] += r

    (out,), carried = _call(
        body, plans, name=name, grid=(M // tm, nb),
        in_specs=[pl.BlockSpec((tm, Nb), lambda i, j: (i, j)), pl.BlockSpec((None, K, Nb), lambda i, j: (j, 0, 0))],
        out_specs=[pl.BlockSpec((tm, K), lambda i, j: (i, 0))],
        out_shape=[_sds((M, K), F32)], args=(dy, wg))
    return out if plans is None else (out, carried)


def _mm_tn_blk(x, dy, nb, name, tk=2048, x_cols=None, plans=None, together=False):
    T, Mx = x.shape
    xk, Mx = (0, Mx) if x_cols is None else x_cols
    Nb = dy.shape[1] // nb
    nj = nb if together else 1

    def body(x_ref, dy_ref, o_ref):
        t = pl.program_id(1)
        r = _dot(_bf(x_ref[...]), _bf(dy_ref[...]), TN)
        for j in range(nj):
            rj = r[:, j * Nb:(j + 1) * Nb]

            @pl.when(t == 0)
            def _():
                o_ref[j] = rj

            @pl.when(t > 0)
            def _():
                o_ref[j] += rj

    (out,), carried = _call(
        body, plans, name=name, grid=(nb // nj, T // tk),
        in_specs=[pl.BlockSpec((tk, Mx), lambda j, t: (t, xk)), pl.BlockSpec((tk, nj * Nb), lambda j, t: (t, j))],
        out_specs=[pl.BlockSpec((nj, Mx, Nb), lambda j, t: (j, 0, 0))],
        out_shape=[_sds((nb, Mx, Nb), F32)], args=(x, dy))
    return out if plans is None else (out, carried)


def _mm_nt(dy, w, name, tm=512):
    M, N = dy.shape
    K = w.shape[0]

    def body(dy_ref, w_ref, o_ref):
        o_ref[...] = _dot(_bf(dy_ref[...]), w_ref[...], NT)

    return pl.pallas_call(
        body, name=name, grid=(M // tm,),
        in_specs=[pl.BlockSpec((tm, N), lambda i: (i, 0)), pl.BlockSpec((K, N), lambda i: (0, 0))],
        out_specs=pl.BlockSpec((tm, K), lambda i: (i, 0)),
        out_shape=_sds((M, K), F32), compiler_params=_cp(1))(dy, w)


def _mm_tn(x, dy, name, tk=1024):
    T, Mx = x.shape
    N = dy.shape[1]

    def body(x_ref, dy_ref, o_ref):
        t = pl.program_id(0)
        r = _dot(_bf(x_ref[...]), _bf(dy_ref[...]), TN)

        @pl.when(t == 0)
        def _():
            o_ref[...] = r

        @pl.when(t > 0)
        def _():
            o_ref[...] += r

    return pl.pallas_call(
        body, name=name, grid=(T // tk,),
        in_specs=[pl.BlockSpec((tk, Mx), lambda t: (t, 0)), pl.BlockSpec((tk, N), lambda t: (t, 0))],
        out_specs=pl.BlockSpec((Mx, N), lambda t: (0, 0)),
        out_shape=_sds((Mx, N), F32), compiler_params=_cp(1))(x, dy)


def _tile(arr, bw, col=lambda c: 0):
    return ("tile", arr, bw, col)


def _full(arr):
    return ("full", arr)


def _out_tile(width, dtype, bw, col=lambda c: 0):
    return ("tile", width, dtype, bw, col)


def _out_acc(rows, width, bw, col=lambda c: 0):
    return ("acc", rows, width, bw, col)


def _rows_call(name, body, n_rows, tm, ncol, ins, outs, plans=None):
    in_specs, args = [], []
    for e in ins:
        if e[0] == "tile":
            _, arr, bw, col = e
            in_specs.append(pl.BlockSpec((tm, bw), functools.partial(lambda c, i, col: (i, col(c)), col=col)))
        else:
            arr = e[1]
            in_specs.append(pl.BlockSpec(arr.shape, functools.partial(lambda c, i, nd: (0,) * nd, nd=arr.ndim)))
        args.append(arr)
    out_specs, out_shape = [], []
    for e in outs:
        if e[0] == "tile":
            _, width, dtype, bw, col = e
            out_specs.append(pl.BlockSpec((tm, bw), functools.partial(lambda c, i, col: (i, col(c)), col=col)))
            out_shape.append(_sds((n_rows, width), dtype))
        else:
            _, rows, width, bw, col = e
            out_specs.append(pl.BlockSpec((rows, bw), functools.partial(lambda c, i, col: (0, col(c)), col=col)))
            out_shape.append(_sds((rows, width), F32))
    out, carried = _call(body, plans, name=name, grid=(ncol, n_rows // tm), in_specs=in_specs, out_specs=out_specs,
                         out_shape=out_shape, args=args)
    return out if plans is None else (out, carried)


def _acc(ref, val):
    i = pl.program_id(1)

    @pl.when(i == 0)
    def _():
        ref[...] = val

    @pl.when(i > 0)
    def _():
        ref[...] += val


def _rinv(z):
    return lax.rsqrt(jnp.mean(z * z, axis=-1, keepdims=True) + EPS)


def _norm_bwd(dy, zhat, r, w):
    dyw = dy * w
    return r * (dyw - zhat * jnp.mean(dyw * zhat, axis=-1, keepdims=True))


def _norm_proj(x, w, wg, name, tm=1024, plans=None):
    M, K = x.shape
    nb, _, Nb = wg.shape

    def body(x_ref, w_ref, wg_ref, o_ref, h_ref):
        @pl.when(pl.program_id(1) == 0)
        def _():
            xv = x_ref[...]
            h_ref[...] = _bf(xv * _rinv(xv) * w_ref[...])

        o_ref[...] = _dot(h_ref[...], wg_ref[...], NN)

    (out, h), carried = _call(
        body, plans, name=name, grid=(M // tm, nb),
        in_specs=[pl.BlockSpec((tm, K), lambda i, j: (i, 0)), pl.BlockSpec((1, K), lambda i, j: (0, 0)),
                  pl.BlockSpec((None, K, Nb), lambda i, j: (j, 0, 0))],
        out_specs=[pl.BlockSpec((tm, Nb), lambda i, j: (i, j)), pl.BlockSpec((tm, K), lambda i, j: (i, 0))],
        out_shape=[_sds((M, nb * Nb), F32), _sds((M, K), BF16)], args=(x, w, wg))
    return (out, h) if plans is None else (out, h, carried)


def _prenorm_bwd(dh, xin, w, dres, name, plans=None):
    def body(dh_ref, x_ref, w_ref, dres_ref, dx_ref, dw_ref):
        xv = x_ref[...]
        r = _rinv(xv)
        xhat = xv * r
        dhv = dh_ref[...]
        dx_ref[...] = dres_ref[...] + _norm_bwd(dhv, xhat, r, w_ref[...])
        _acc(dw_ref, jnp.sum(dhv * xhat, axis=0, keepdims=True))

    return _rows_call(name, body, xin.shape[0], 512, 1,
                      [_tile(dh, D_MODEL), _tile(xin, D_MODEL), _full(w), _tile(dres, D_MODEL)],
                      [_out_tile(D_MODEL, F32, D_MODEL), _out_acc(1, D_MODEL, D_MODEL)], plans)


def _postnorm_bwd(dout, z, w, w_mat, name):
    def body(do_ref, z_ref, w_ref, wm_ref, dz_ref, dm_ref, dw_ref):
        zv = z_ref[...]
        r = _rinv(zv)
        zhat = zv * r
        dov = do_ref[...]
        dz = _bf(_norm_bwd(dov, zhat, r, w_ref[...]))
        dz_ref[...] = dz
        dm_ref[...] = _dot(dz, wm_ref[...], NT)
        _acc(dw_ref, jnp.sum(dov * zhat, axis=0, keepdims=True))

    return _rows_call(name, body, z.shape[0], 512, 1,
                      [_tile(dout, D_MODEL), _tile(z, D_MODEL), _full(w), _full(w_mat)],
                      [_out_tile(D_MODEL, BF16, D_MODEL), _out_tile(D_MODEL, F32, D_MODEL),
                       _out_acc(1, D_MODEL, D_MODEL)])


def _t5_bucket(dist):
    n = jnp.maximum(dist, 0)
    nf = jnp.maximum(n, 1).astype(F32)
    large = MAX_EXACT + (jnp.log(nf / MAX_EXACT) / math.log(MAX_DISTANCE / MAX_EXACT)
                         * (NUM_BUCKETS - MAX_EXACT)).astype(jnp.int32)
    large = jnp.minimum(large, NUM_BUCKETS - 1)
    return jnp.where(n < MAX_EXACT, n, large)


def _band_rel():
    return jnp.arange(BLK)[:, None] + BLK - jnp.arange(2 * BLK)[None, :]


def _band_valid():
    rel = _band_rel()
    window = (rel >= 0) & (rel <= BLK)
    first = window & (jnp.arange(2 * BLK)[None, :] >= BLK)
    return jnp.stack([first, window]).astype(F32).reshape(2, 1, BAND)


RES_UNROLL = 4
PAIR = LANES // HEAD_DIM


def _pair_lanes():
    first = lax.broadcasted_iota(jnp.int32, (1, LANES), 1) < HEAD_DIM
    return first, jnp.logical_not(first)


def _heads_per_step(d):
    return HEADS if d == 1 else LANES // HEAD_DIM


def _sub_rows(r, d):
    return pl.ds(r, BLK, stride=d) if d > 1 else pl.ds(0, BLK)


def _for_residues(d, fn):
    if d <= RES_UNROLL:
        for r in range(d):
            fn(r)
    else:
        def group(i, carry):
            for k in range(RES_UNROLL):
                fn(i * RES_UNROLL + k)
            return carry

        lax.fori_loop(0, d // RES_UNROLL, group, 0)


def _attn_specs(d, g, qblock):
    cw = _heads_per_step(d) * HEAD_DIM

    def col(part, hp):
        return (g * 3 + part) * (GROUP_W // cw) + hp

    def cur(part):
        return pl.BlockSpec((d * BLK, cw), lambda hp, n: (qblock(n), col(part, hp)))

    def prev(part):
        return pl.BlockSpec((d * BLK, cw), lambda hp, n: (jnp.maximum(qblock(n) - 1, 0), col(part, hp)))

    return cur, prev


def _attn_fwd(proj, bias, g, name, plans=None):
    S = proj.shape[0]
    d = DILATIONS[g]
    NB = S // (d * BLK)
    hps = _heads_per_step(d)

    def body(q_ref, kp_ref, kc_ref, vp_ref, vc_ref, b_ref, o_ref, lse_ref):
        hp = pl.program_id(0)
        later = jnp.minimum(pl.program_id(1), 1)

        def residue(r):
            rows = _sub_rows(r, d)
            q2 = q_ref[rows, :]
            k2 = jnp.concatenate([kp_ref[rows, :], kc_ref[rows, :]], axis=0)
            v2 = jnp.concatenate([vp_ref[rows, :], vc_ref[rows, :]], axis=0)
            outs, lses = [], []
            for pp in range(hps // PAIR):
                ps = slice(pp * LANES, (pp + 1) * LANES)
                qp, kp, vp = _bf(q2[:, ps]), _bf(k2[:, ps]), _bf(v2[:, ps])
                o_h, lse_h = [], []
                for hh, own in enumerate(_pair_lanes()):
                    s = _dot(qp, jnp.where(own, kp, 0), NT) * (HEAD_DIM ** -0.5) + b_ref[later, hp * hps + pp * PAIR + hh]
                    m = jnp.max(s, axis=-1, keepdims=True)
                    p = jnp.exp(s - m)
                    l = jnp.sum(p, axis=-1, keepdims=True)
                    o_h.append(_dot(_bf(p), vp, NN) / l)
                    lse_h.append(m + jnp.log(l))
                first = _pair_lanes()[0]
                outs.append(jnp.where(first, o_h[0], o_h[1]))
                lses.append(jnp.where(first, lse_h[0], lse_h[1]))
            o_ref[rows, :] = outs[0] if len(outs) == 1 else jnp.concatenate(outs, axis=1)
            lse_ref[rows, :] = lses[0] if len(lses) == 1 else jnp.concatenate(lses, axis=1)

        _for_residues(d, residue)

    cur, prev = _attn_specs(d, g, lambda n: n)
    out = pl.BlockSpec((d * BLK, hps * HEAD_DIM), lambda hp, n: (n, hp))
    res, carried = _call(
        body, plans, name=name, grid=(HEADS // hps, NB),
        in_specs=[cur(0), prev(1), cur(1), prev(2), cur(2),
                  pl.BlockSpec((2, HEADS, BLK, 2 * BLK), lambda hp, n: (0, 0, 0, 0))],
        out_specs=[out, out], out_shape=[_sds((S, GROUP_W), F32)] * 2,
        args=(proj, proj, proj, proj, proj, bias))
    return res if plans is None else (res, carried)


def _attn_bwd(proj, bias, lse, y, dy, g, name, plans=None):
    S = proj.shape[0]
    d = DILATIONS[g]
    NB = S // (d * BLK)
    hps = _heads_per_step(d)

    def body(q_ref, kp_ref, kc_ref, vp_ref, vc_ref, b_ref, l_ref, y_ref, dy_ref,
             dq_ref, dk_ref, dv_ref, db_ref, ck_ref, cv_ref):
        hp, n = pl.program_id(0), pl.program_id(1)

        @pl.when((hp == 0) & (n == 0))
        def _():
            db_ref[...] = jnp.zeros_like(db_ref)

        @pl.when(n == 0)
        def _():
            ck_ref[...] = jnp.zeros_like(ck_ref)
            cv_ref[...] = jnp.zeros_like(cv_ref)

        @pl.when(n < NB)
        def _():
            later = jnp.minimum(n, 1)

            def residue(r):
                rows = _sub_rows(r, d)
                q2 = q_ref[rows, :]
                k2 = jnp.concatenate([kp_ref[rows, :], kc_ref[rows, :]], axis=0)
                v2 = jnp.concatenate([vp_ref[rows, :], vc_ref[rows, :]], axis=0)
                l2, y2, dy2 = l_ref[rows, :], y_ref[rows, :], dy_ref[rows, :]
                dqs, dks, dvs = [], [], []
                for pp in range(hps // PAIR):
                    ps = slice(pp * LANES, (pp + 1) * LANES)
                    qp, kp, vp = _bf(q2[:, ps]), _bf(k2[:, ps]), _bf(v2[:, ps])
                    dyp, yp = dy2[:, ps], y2[:, ps]
                    dq_h, dk_h, dv_h = [], [], []
                    for hh, own in enumerate(_pair_lanes()):
                        head = hp * hps + pp * PAIR + hh
                        s = _dot(qp, jnp.where(own, kp, 0), NT) * (HEAD_DIM ** -0.5) + b_ref[later, head]
                        p = jnp.exp(s - l2[:, pp * LANES + hh * HEAD_DIM:pp * LANES + hh * HEAD_DIM + 1])
                        dyh = jnp.where(own, dyp, 0.0)
                        delta = jnp.sum(dyh * yp, axis=-1, keepdims=True)
                        ds = p * (_dot(_bf(dyh), vp, NT) - delta)
                        db_ref[head] += ds
                        dsb = _bf(ds * (HEAD_DIM ** -0.5))
                        dq_h.append(_dot(dsb, kp, NN))
                        dk_h.append(_dot(dsb, qp, TN))
                        dv_h.append(_dot(_bf(p), _bf(dyp), TN))
                    first = _pair_lanes()[0]
                    dqs.append(jnp.where(first, dq_h[0], dq_h[1]))
                    dks.append(jnp.where(first, dk_h[0], dk_h[1]))
                    dvs.append(jnp.where(first, dv_h[0], dv_h[1]))
                dkb = dks[0] if len(dks) == 1 else jnp.concatenate(dks, axis=1)
                dvb = dvs[0] if len(dvs) == 1 else jnp.concatenate(dvs, axis=1)
                dq_ref[rows, :] = dqs[0] if len(dqs) == 1 else jnp.concatenate(dqs, axis=1)
                dk_ref[rows, :] = ck_ref[rows, :] + dkb[:BLK]
                dv_ref[rows, :] = cv_ref[rows, :] + dvb[:BLK]
                ck_ref[rows, :] = dkb[BLK:]
                cv_ref[rows, :] = dvb[BLK:]

            _for_residues(d, residue)

        @pl.when(n == NB)
        def _():
            dk_ref[...] = ck_ref[...]
            dv_ref[...] = cv_ref[...]

    def qn(n):
        return jnp.minimum(n, NB - 1)

    cur, prev = _attn_specs(d, g, qn)
    cw = hps * HEAD_DIM
    row = pl.BlockSpec((d * BLK, cw), lambda hp, n: (qn(n), hp))
    done = pl.BlockSpec((d * BLK, cw), lambda hp, n: (jnp.maximum(n - 1, 0), hp))
    (dq, dk, dv, db), carried = _call(
        body, plans, name=name, grid=(HEADS // hps, NB + 1),
        in_specs=[cur(0), prev(1), cur(1), prev(2), cur(2),
                  pl.BlockSpec((2, HEADS, BLK, 2 * BLK), lambda hp, n: (0, 0, 0, 0)), row, row, row],
        out_specs=[row, done, done, pl.BlockSpec((HEADS, BLK, 2 * BLK), lambda hp, n: (0, 0, 0))],
        out_shape=[_sds((S, GROUP_W), F32)] * 3 + [_sds((HEADS, BLK, 2 * BLK), F32)],
        scratch_shapes=[pltpu.VMEM((d * BLK, cw), F32)] * 2,
        args=(proj, proj, proj, proj, proj, bias, lse, y, dy))
    return ([dq, dk, dv], db) if plans is None else ([dq, dk, dv], db, carried)


BAND = BLK * 2 * BLK


def _bucket_onehot():
    buckets = jnp.stack([_t5_bucket(_band_rel() * d) for d in DILATIONS]).reshape(N_GROUPS, 1, BAND)
    return (buckets == jnp.arange(NUM_BUCKETS).reshape(1, NUM_BUCKETS, 1)).astype(F32)


def _relbias_fwd(rel_bias, name):
    table = rel_bias.reshape(NUM_BUCKETS, N_GROUPS, HEADS).transpose(1, 0, 2)

    def body(t_ref, oh_ref, valid_ref, o_ref):
        bias = lax.dot_general(t_ref[...], oh_ref[...], (TN, ((), ())), preferred_element_type=F32,
                               precision=lax.Precision.HIGHEST)
        for k in range(2):
            o_ref[k] = jnp.where(valid_ref[k] > 0.5, bias, NEG_INF)

    out = pl.pallas_call(
        body, name=name, grid=(N_GROUPS,),
        in_specs=[pl.BlockSpec((None, NUM_BUCKETS, HEADS), lambda g: (g, 0, 0)),
                  pl.BlockSpec((None, NUM_BUCKETS, BAND), lambda g: (g, 0, 0)),
                  pl.BlockSpec((2, 1, BAND), lambda g: (0, 0, 0))],
        out_specs=pl.BlockSpec((None, 2, HEADS, BAND), lambda g: (g, 0, 0, 0)),
        out_shape=_sds((N_GROUPS, 2, HEADS, BAND), F32), compiler_params=_cp(1))(table, _bucket_onehot(), _band_valid())
    return out.reshape(N_GROUPS, 2, HEADS, BLK, 2 * BLK)


def _relbias_bwd(dbs, name):
    band = BAND
    onehot = _bucket_onehot()
    dbf = jnp.stack([db.reshape(HEADS, band) for db in dbs])

    def body(oh_ref, db_ref, o_ref):
        o_ref[...] = lax.dot_general(oh_ref[...], db_ref[...], (NT, ((), ())), preferred_element_type=F32,
                                     precision=lax.Precision.HIGHEST)

    out = pl.pallas_call(
        body, name=name, grid=(N_GROUPS,),
        in_specs=[pl.BlockSpec((None, NUM_BUCKETS, band), lambda g: (g, 0, 0)),
                  pl.BlockSpec((None, HEADS, band), lambda g: (g, 0, 0))],
        out_specs=pl.BlockSpec((None, NUM_BUCKETS, HEADS), lambda g: (g, 0, 0)),
        out_shape=_sds((N_GROUPS, NUM_BUCKETS, HEADS), F32), compiler_params=_cp(1))(onehot, dbf)
    return out.transpose(1, 0, 2).reshape(NUM_BUCKETS, N_GROUPS * HEADS)


def _chunk_pos(shape):
    return lax.broadcasted_iota(jnp.int32, shape, 0) % HG_CHUNK


def _chunk_cumsum(v):
    pos = _chunk_pos(v.shape)
    s = 1
    while s < HG_CHUNK:
        v = v + jnp.where(pos >= s, pltpu.roll(v, s, 0), 0.0)
        s *= 2
    return v


def _chunk_rev_cumsum(v):
    pos = _chunk_pos(v.shape)
    n = v.shape[0]
    s = 1
    while s < HG_CHUNK:
        v = v + jnp.where(pos < HG_CHUNK - s, pltpu.roll(v, n - s, 0), 0.0)
        s *= 2
    return v


def _lower_bound(raw):
    a0, a1 = raw[0:1], raw[1:2]
    m = jnp.maximum(a0, a1)
    e0, e1 = jnp.exp(a0 - m), jnp.exp(a1 - m)
    return e0 / (e0 + e1)


def _hg_gates(qr, fr, lb):
    sf = _sigmoid(fr)
    f = lb + (1.0 - lb) * sf
    sq = _sigmoid(qr)
    return qr * sq, sq, f, sf


HG_COL0 = QKV_W // HG_W


def _hgrn_fwd(proj, lb_raw, nw, name):
    S = proj.shape[0]
    ncs = HG_TILE // HG_CHUNK
    tril = jnp.tril(jnp.ones((HG_CHUNK, HG_CHUNK), dtype=bool))

    def body(q_ref, f_ref, i_ref, og_ref, lb_ref, nw_ref, y_ref, o_ref, st_ref, state):
        @pl.when(pl.program_id(0) == 0)
        def _():
            state[...] = jnp.zeros_like(state)

        lb = _lower_bound(lb_ref[...])
        q, _, f, _ = _hg_gates(q_ref[...], f_ref[...], lb)
        k = 1.0 - f
        G = _chunk_cumsum(jnp.log(f))
        row = lax.broadcasted_iota(jnp.int32, (HG_CHUNK, HG_CHUNK), 0)
        col = lax.broadcasted_iota(jnp.int32, (HG_CHUNK, HG_CHUNK), 1)
        heads = [slice(h * HG_DK, (h + 1) * HG_DK) for h in range(HG_HEADS)]
        sts = [state[h] for h in range(HG_HEADS)]
        for c in range(ncs):
            cs = slice(c * HG_CHUNK, (c + 1) * HG_CHUNK)
            for h, hs in enumerate(heads):
                Gc = G[cs, hs]
                gl = Gc[HG_CHUNK - 1:HG_CHUNK]
                qt = _bf(q[cs, hs] * jnp.exp(Gc))
                kt = _bf(k[cs, hs] * jnp.exp(-Gc))
                kd = _bf(k[cs, hs] * jnp.exp(gl - Gc))
                v = _bf(i_ref[cs, hs])
                A = jnp.where(row >= col, _dot(qt, kt, NT), 0.0)
                o_ref[cs, hs] = _dot(_bf(A), v, NN) + _dot(qt, _bf(sts[h]), NT)
                st_ref[c, h] = sts[h]
                sts[h] = sts[h] * jnp.exp(gl) + _dot(v, kd, TN)
        for h, hs in enumerate(heads):
            state[h] = sts[h]
            oh = o_ref[:, hs]
            og = og_ref[:, hs]
            y_ref[:, hs] = oh * _rinv(oh) * nw_ref[...] * (og * _sigmoid(og))

    def colspec(j):
        return pl.BlockSpec((HG_TILE, HG_W), lambda i: (i, HG_COL0 + j))

    return pl.pallas_call(
        body, name=name, grid=(S // HG_TILE,),
        in_specs=[colspec(0), colspec(1), colspec(2), colspec(3),
                  pl.BlockSpec((2, HG_W), lambda i: (0, 0)), pl.BlockSpec((1, HG_DK), lambda i: (0, 0))],
        out_specs=[pl.BlockSpec((HG_TILE, HG_W), lambda i: (i, 0))] * 2
        + [pl.BlockSpec((ncs, HG_HEADS, HG_DK, HG_DK), lambda i: (i, 0, 0, 0))],
        out_shape=[_sds((S, HG_W), F32)] * 2 + [_sds((S // HG_CHUNK, HG_HEADS, HG_DK, HG_DK), F32)],
        scratch_shapes=[pltpu.VMEM((HG_HEADS, HG_DK, HG_DK), F32)],
        compiler_params=_cp(1))(proj, proj, proj, proj, lb_raw, nw)


def _hgrn_bwd(proj, lb_raw, nw, o, states, dy, d_attn, d_gates, name):
    S = proj.shape[0]
    ncs = HG_TILE // HG_CHUNK
    nt = S // HG_TILE
    n_a, n_g = len(d_attn), len(d_gates)
    own = [slice(QKV_W + j * HG_W, QKV_W + (j + 1) * HG_W) for j in range(4)]

    def body(q_ref, f_ref, i_ref, og_ref, lb_ref, nw_ref, o_ref, st_ref, dy_ref, *rest):
        attn_refs, gate_refs = rest[:n_a], rest[n_a:n_a + n_g]
        dp_ref, dlb_ref, dnw_ref, dstate, do_s, dG_s, dgl_s, dk_s, dlb_s = rest[n_a + n_g:]
        dq_ref, df_ref, di_ref, dog_ref = (dp_ref.at[:, cols] for cols in own)
        step = pl.program_id(0)
        for k, a_ref in enumerate(attn_refs):
            dp_ref[:, k * GROUP_W:(k + 1) * GROUP_W] = _bf(a_ref[...])
        for k, g_ref in enumerate(gate_refs):
            dp_ref[:, QKV_W + 4 * HG_W + k * D_MODEL:QKV_W + 4 * HG_W + (k + 1) * D_MODEL] = g_ref[...]

        @pl.when(step == 0)
        def _():
            dstate[...] = jnp.zeros_like(dstate)
            dlb_s[...] = jnp.zeros_like(dlb_s)
            dnw_ref[...] = jnp.zeros_like(dnw_ref)

        lb = _lower_bound(lb_ref[...])
        qr = q_ref[...]
        q, sq, f, sf = _hg_gates(qr, f_ref[...], lb)
        k = 1.0 - f
        G = _chunk_cumsum(jnp.log(f))
        nwv = nw_ref[...]
        row = lax.broadcasted_iota(jnp.int32, (HG_CHUNK, HG_CHUNK), 0)
        col = lax.broadcasted_iota(jnp.int32, (HG_CHUNK, HG_CHUNK), 1)
        for h in range(HG_HEADS):
            hs = slice(h * HG_DK, (h + 1) * HG_DK)
            oh = o_ref[:, hs]
            r = _rinv(oh)
            ohat = oh * r
            og = og_ref[:, hs]
            sg = _sigmoid(og)
            dyh = dy_ref[:, hs]
            don = dyh * (og * sg)
            dog_ref[:, hs] = _bf(dyh * (ohat * nwv) * (sg * (1.0 + og * (1.0 - sg))))
            dnw_ref[...] += jnp.sum(don * ohat, axis=0, keepdims=True)
            do_s[:, hs] = _norm_bwd(don, ohat, r, nwv)
        dsts = [dstate[h] for h in range(HG_HEADS)]
        for c in reversed(range(ncs)):
            cs = slice(c * HG_CHUNK, (c + 1) * HG_CHUNK)
            for h in range(HG_HEADS):
                hs = slice(h * HG_DK, (h + 1) * HG_DK)
                dst = dsts[h]
                Gc = G[cs, hs]
                gl = Gc[HG_CHUNK - 1:HG_CHUNK]
                eG, enG, edG, egl = jnp.exp(Gc), jnp.exp(-Gc), jnp.exp(gl - Gc), jnp.exp(gl)
                qt, kt, kd = q[cs, hs] * eG, k[cs, hs] * enG, k[cs, hs] * edG
                qtb, ktb, kdb = _bf(qt), _bf(kt), _bf(kd)
                v = _bf(i_ref[cs, hs])
                do = _bf(do_s[cs, hs])
                st = st_ref[c, h]
                dstb = _bf(dst)
                A = jnp.where(row >= col, _dot(qtb, ktb, NT), 0.0)
                dA = _bf(jnp.where(row >= col, _dot(do, v, NT), 0.0))
                di_ref[cs, hs] = _bf(_dot(_bf(A), do, TN) + _dot(kdb, dstb, NT))
                dqt = _dot(dA, ktb, NN) + _dot(do, _bf(st), NN)
                dkt = _dot(dA, qtb, TN)
                dkd = _dot(v, dstb, NN)
                dgl = egl * jnp.sum(st * dst, axis=0, keepdims=True) + jnp.sum(dkd * kd, axis=0, keepdims=True)
                dsts[h] = dst * egl + _dot(do, qtb, TN)
                dq_ref[cs, hs] = _bf(dqt * eG * (sq[cs, hs] * (1.0 + qr[cs, hs] * (1.0 - sq[cs, hs]))))
                dk_s[cs, hs] = dkt * enG + dkd * edG
                dG_s[cs, hs] = dqt * qt - dkt * kt - dkd * kd
                dgl_s[cs, hs] = jnp.broadcast_to(dgl, (HG_CHUNK, HG_DK))
        for h in range(HG_HEADS):
            dstate[h] = dsts[h]
        dg = _chunk_rev_cumsum(dG_s[...]) + dgl_s[...]
        dfv = dg / f - dk_s[...]
        df_ref[...] = _bf(dfv * (1.0 - lb) * sf * (1.0 - sf))
        dlb_s[...] += jnp.sum(dfv * (1.0 - sf), axis=0, keepdims=True)

        @pl.when(step == nt - 1)
        def _():
            t = dlb_s[...] * lb * (1.0 - lb)
            dlb_ref[...] = jnp.concatenate([t, -t], axis=0)

    def colspec(j):
        return pl.BlockSpec((HG_TILE, HG_W), lambda i: (nt - 1 - i, HG_COL0 + j))

    def rows(width):
        return pl.BlockSpec((HG_TILE, width), lambda i: (nt - 1 - i, 0))

    tile = rows(HG_W)
    return pl.pallas_call(
        body, name=name, grid=(nt,),
        in_specs=[colspec(0), colspec(1), colspec(2), colspec(3),
                  pl.BlockSpec((2, HG_W), lambda i: (0, 0)), pl.BlockSpec((1, HG_DK), lambda i: (0, 0)),
                  tile, pl.BlockSpec((ncs, HG_HEADS, HG_DK, HG_DK), lambda i: (nt - 1 - i, 0, 0, 0)), tile]
        + [rows(GROUP_W)] * n_a + [rows(D_MODEL)] * n_g,
        out_specs=[rows(IN_W), pl.BlockSpec((2, HG_W), lambda i: (0, 0)), pl.BlockSpec((1, HG_DK), lambda i: (0, 0))],
        out_shape=[_sds((S, IN_W), BF16), _sds((2, HG_W), F32), _sds((1, HG_DK), F32)],
        scratch_shapes=[pltpu.VMEM((HG_HEADS, HG_DK, HG_DK), F32)] + [pltpu.VMEM((HG_TILE, HG_W), F32)] * 4
        + [pltpu.VMEM((1, HG_W), F32)],
        compiler_params=_cp(1))(proj, proj, proj, proj, lb_raw, nw, o, states, dy, *d_attn, *d_gates)


GATE_COL0 = (QKV_W + 4 * HG_W) // GROUP_W
HALF_D = D_MODEL // 2


def _gate_tiles(proj):
    return [_tile(proj, HALF_D, functools.partial(lambda c, k: GATE_COL0 + k, k=k)) for k in range(4)]


def _gates(g_refs):
    s0 = _sigmoid(jnp.concatenate([g_refs[0][...], g_refs[1][...]], axis=1))
    s1 = _sigmoid(jnp.concatenate([g_refs[2][...], g_refs[3][...]], axis=1))
    return s0, s1


def _branch_fwd(os_, lses, yh, proj, w_a, w_h, name):
    nb = w_a.shape[0]

    def body(o0, o1, o2, l0, l1, l2, yh_ref, g0a, g0b, g1a, g1b, wa_ref, wh_ref,
             y_ref, lse_ref, za_ref, zh_ref, m_ref):
        a, b, c = l0[...], l1[...], l2[...]
        m = jnp.maximum(jnp.maximum(a, b), c)
        ea, eb, ec = jnp.exp(a - m), jnp.exp(b - m), jnp.exp(c - m)
        den = ea + eb + ec
        y = (ea * o0[...] + eb * o1[...] + ec * o2[...]) / den
        y_ref[...] = y
        lse_ref[...] = m + jnp.log(den)
        yb, yhb = _bf(y), _bf(yh_ref[...])
        za = jnp.concatenate([_dot(yb, wa_ref[j], NN) for j in range(nb)], axis=1)
        zh = jnp.concatenate([_dot(yhb, wh_ref[j], NN) for j in range(nb)], axis=1)
        s0, s1 = _gates((g0a, g0b, g1a, g1b))
        za_ref[...] = za
        zh_ref[...] = zh
        m_ref[...] = _bf(s0 * za + s1 * zh)

    return _rows_call(name, body, yh.shape[0], 512, 1,
                      [*[_tile(t, GROUP_W) for t in (*os_, *lses)], _tile(yh, HG_W), *_gate_tiles(proj),
                       _full(w_a), _full(w_h)],
                      [_out_tile(GROUP_W, F32, GROUP_W)] * 2 + [_out_tile(D_MODEL, F32, D_MODEL)] * 2
                      + [_out_tile(D_MODEL, BF16, D_MODEL)])


def _branch_bwd(dm, za, zh, proj, w_a, w_h, name):
    nb, _, Nb = w_a.shape

    def body(dm_ref, za_ref, zh_ref, g0a, g0b, g1a, g1b, wa_ref, wh_ref,
             dza_ref, dzh_ref, dg0_ref, dg1_ref, dy_ref, dyh_ref):
        dmv = dm_ref[...]
        s0, s1 = _gates((g0a, g0b, g1a, g1b))
        dza, dzh = _bf(dmv * s0), _bf(dmv * s1)
        dza_ref[...] = dza
        dzh_ref[...] = dzh
        dg0_ref[...] = _bf(dmv * za_ref[...] * s0 * (1.0 - s0))
        dg1_ref[...] = _bf(dmv * zh_ref[...] * s1 * (1.0 - s1))
        dy_ref[...] = sum(_dot(dza[:, j * Nb:(j + 1) * Nb], wa_ref[j], NT) for j in range(nb))
        dyh_ref[...] = sum(_dot(dzh[:, j * Nb:(j + 1) * Nb], wh_ref[j], NT) for j in range(nb))

    return _rows_call(name, body, za.shape[0], 512, 1,
                      [_tile(dm, D_MODEL), _tile(za, D_MODEL), _tile(zh, D_MODEL), *_gate_tiles(proj),
                       _full(w_a), _full(w_h)],
                      [_out_tile(D_MODEL, BF16, D_MODEL)] * 4 + [_out_tile(GROUP_W, F32, GROUP_W),
                                                                 _out_tile(HG_W, F32, HG_W)])


def _mix_out(merged, w_out, x, w_post, w_pre, name):
    def body(m_ref, wo_ref, x_ref, wp_ref, wf_ref, mo_ref, x1_ref, h2_ref):
        z = _dot(m_ref[...], wo_ref[...], NN)
        mo_ref[...] = z
        x1 = x_ref[...] + z * _rinv(z) * wp_ref[...]
        x1_ref[...] = x1
        h2_ref[...] = _bf(x1 * _rinv(x1) * wf_ref[...])

    return _rows_call(name, body, x.shape[0], 512, 1,
                      [_tile(merged, D_MODEL), _full(w_out), _tile(x, D_MODEL), _full(w_post), _full(w_pre)],
                      [_out_tile(D_MODEL, F32, D_MODEL), _out_tile(D_MODEL, F32, D_MODEL),
                       _out_tile(D_MODEL, BF16, D_MODEL)])


def _loss_head(a, w_down, x1, tgt, w, name):
    def body(a_ref, wd_ref, x1_ref, t_ref, w_ref, dx_ref, df_ref, dw_ref, loss_ref):
        z = _dot(a_ref[...], wd_ref[...], NN)
        r = _rinv(z)
        zhat = z * r
        wv = w_ref[...]
        e = x1_ref[...] + zhat * wv - t_ref[...]
        dx = e * (1.0 / D_MODEL)
        dx_ref[...] = dx
        df_ref[...] = _bf(_norm_bwd(dx, zhat, r, wv))
        _acc(dw_ref, jnp.sum(dx * zhat, axis=0, keepdims=True))
        part = 0.5 * jnp.sum(jnp.sum(e * e, axis=1, keepdims=True), axis=0, keepdims=True) * (1.0 / D_MODEL)
        _acc(loss_ref, jnp.broadcast_to(part, (1, LANES)))

    return _rows_call(name, body, x1.shape[0], 512, 1,
                      [_tile(a, D_FF), _full(w_down), _tile(x1, D_MODEL), _tile(tgt, D_MODEL), _full(w)],
                      [_out_tile(D_MODEL, F32, D_MODEL), _out_tile(D_MODEL, BF16, D_MODEL),
                       _out_acc(1, D_MODEL, D_MODEL), _out_acc(1, LANES, LANES)])


CONV_CB = D_FF // 2
CONV_TM = 512
HALO = 8
SQRT_HALF = 0.7071067811865476
INV_SQRT_2PI = 0.3989422804014327


CONV_RS = 32


def _lane_tiles():
    return [slice(k * LANES, (k + 1) * LANES) for k in range(CONV_CB // LANES)]


def _strip_start(i):
    return pl.multiple_of(i * CONV_RS, CONV_RS)


def _strip_taps(u_ref, halo_ref, r0, cs, first_strip, first_tile):
    if first_strip:
        before = jnp.where(first_tile, 0.0, halo_ref[:, cs])
        blk = jnp.concatenate([before, u_ref[0:CONV_RS, cs]], axis=0)
    else:
        blk = u_ref[pl.ds(pl.multiple_of(r0 - HALO, HALO), CONV_RS + HALO), cs]
    return pltpu.roll(blk, 2, 0)[HALO:], pltpu.roll(blk, 1, 0)[HALO:], blk[HALO:]


def _conv(taps, w_ref, b_ref, cs):
    return b_ref[:, cs] + w_ref[0:1, cs] * taps[0] + w_ref[1:2, cs] * taps[1] + w_ref[2:3, cs] * taps[2]


def _conv_specs(tm):
    nh = tm // HALO
    nc = D_FF // CONV_CB

    def tile(off):
        return pl.BlockSpec((tm, CONV_CB), lambda c, i: (i, off + c))

    def halo(off):
        return pl.BlockSpec((HALO, CONV_CB), lambda c, i: (jnp.maximum(i * nh - 1, 0), off + c))

    def small(rows, off):
        return pl.BlockSpec((rows, CONV_CB), lambda c, i: (0, off + c))

    return nc, tile, halo, small


def _conv_gelu_fwd(u, cw, cb, name):
    S = u.shape[0]
    tm = CONV_TM
    nc, tile, halo, small = _conv_specs(tm)

    def body(ug, hg, uv, hv, wg, wv, bg, bv, a_ref):
        first_tile = pl.program_id(1) == 0

        def strip(r0, first_strip):
            for cs in _lane_tiles():
                cg = _conv(_strip_taps(ug, hg, r0, cs, first_strip, first_tile), wg, bg, cs)
                cv = _conv(_strip_taps(uv, hv, r0, cs, first_strip, first_tile), wv, bv, cs)
                a_ref[pl.ds(r0, CONV_RS), cs] = _bf(0.5 * cg * (1.0 + lax.erf(cg * SQRT_HALF)) * cv)

        strip(0, True)
        lax.fori_loop(1, tm // CONV_RS, lambda k, c: (strip(_strip_start(k), False), c)[1], 0)

    return pl.pallas_call(
        body, name=name, grid=(nc, S // tm),
        in_specs=[tile(0), halo(0), tile(nc), halo(nc), small(3, 0), small(3, nc), small(1, 0), small(1, nc)],
        out_specs=tile(0), out_shape=_sds((S, D_FF), BF16), compiler_params=_cp(2))(u, u, u, u, cw, cw, cb, cb)


def _conv_gelu_bwd(u, da, cw, cb, name, plans=None):
    S = u.shape[0]
    tm = CONV_TM
    nt = S // tm
    nc, tile, halo, small = _conv_specs(tm)

    def body(ug, hg, uv, hv, wg, wv, bg, bv, da_ref, dcg_ref, dcv_ref, dwg_ref, dwv_ref, dbg_ref, dbv_ref, acc):
        i = pl.program_id(1)
        first_tile = i == 0

        @pl.when(first_tile)
        def _():
            acc[...] = jnp.zeros_like(acc)

        def strip(r0, first_strip):
            rows = pl.ds(r0, CONV_RS)
            for cs in _lane_tiles():
                tg = _strip_taps(ug, hg, r0, cs, first_strip, first_tile)
                tv = _strip_taps(uv, hv, r0, cs, first_strip, first_tile)
                cg = _conv(tg, wg, bg, cs)
                cv = _conv(tv, wv, bv, cs)
                phi = 0.5 * (1.0 + lax.erf(cg * SQRT_HALF))
                dav = da_ref[rows, cs]
                dcg = dav * cv * (phi + cg * jnp.exp(-0.5 * cg * cg) * INV_SQRT_2PI)
                dcv = dav * (cg * phi)
                dcg_ref[rows, cs] = dcg
                dcv_ref[rows, cs] = dcv
                for half, (dc, taps) in enumerate(((dcg, tg), (dcv, tv))):
                    for j in range(3):
                        acc[4 * half + j, :, cs] += dc * taps[j]
                    acc[4 * half + 3, :, cs] += dc

        strip(0, True)
        lax.fori_loop(1, tm // CONV_RS, lambda k, c: (strip(_strip_start(k), False), c)[1], 0)

        @pl.when(i == nt - 1)
        def _():
            for half, (dw_ref, db_ref) in enumerate(((dwg_ref, dbg_ref), (dwv_ref, dbv_ref))):
                for j in range(3):
                    dw_ref[j:j + 1, :] = jnp.sum(acc[4 * half + j], axis=0, keepdims=True)
                db_ref[...] = jnp.sum(acc[4 * half + 3], axis=0, keepdims=True)

    res, carried = _call(
        body, plans, name=name, grid=(nc, nt),
        in_specs=[tile(0), halo(0), tile(nc), halo(nc), small(3, 0), small(3, nc), small(1, 0), small(1, nc), tile(0)],
        out_specs=[tile(0), tile(0), small(3, 0), small(3, 0), small(1, 0), small(1, 0)],
        out_shape=[_sds((S, D_FF), F32)] * 2 + [_sds((3, D_FF), F32)] * 2 + [_sds((1, D_FF), F32)] * 2,
        scratch_shapes=[pltpu.VMEM((8, CONV_RS, CONV_CB), F32)], args=(u, u, u, u, cw, cw, cb, cb, da))
    return res if plans is None else (res, carried)


def _conv_input_bwd(dcg, dcv, cw, name, plans=None):
    S = dcg.shape[0]
    tm = CONV_TM // 2
    nh = tm // HALO
    nt = S // tm
    n = CONV_RS + HALO
    tile = pl.BlockSpec((tm, D_FF), lambda i: (i, 0))
    nxt = pl.BlockSpec((HALO, D_FF), lambda i: (jnp.minimum((i + 1) * nh, S // HALO - 1), 0))

    def body(g_ref, ng_ref, v_ref, nv_ref, w_ref, du_ref):
        last_tile = pl.program_id(0) == nt - 1

        def strip(r0, last_strip):
            for half, (dc_ref, n_ref) in enumerate(((g_ref, ng_ref), (v_ref, nv_ref))):
                for k in range(D_FF // LANES):
                    cs = slice(k * LANES, (k + 1) * LANES)
                    ws = slice(half * D_FF + k * LANES, half * D_FF + (k + 1) * LANES)
                    if last_strip:
                        after = jnp.where(last_tile, 0.0, n_ref[:, cs])
                        blk = jnp.concatenate([dc_ref[tm - CONV_RS:tm, cs], after], axis=0)
                    else:
                        blk = dc_ref[pl.ds(r0, n), cs]
                    d1 = pltpu.roll(blk, n - 1, 0)[:CONV_RS]
                    d2 = pltpu.roll(blk, n - 2, 0)[:CONV_RS]
                    du_ref[pl.ds(r0, CONV_RS), ws] = _bf(w_ref[2:3, ws] * blk[:CONV_RS] + w_ref[1:2, ws] * d1
                                                         + w_ref[0:1, ws] * d2)

        lax.fori_loop(0, tm // CONV_RS - 1, lambda k, c: (strip(_strip_start(k), False), c)[1], 0)
        strip(tm - CONV_RS, True)

    (du,), carried = _call(
        body, plans, name=name, grid=(nt,),
        in_specs=[tile, nxt, tile, nxt, pl.BlockSpec((3, 2 * D_FF), lambda i: (0, 0))],
        out_specs=[pl.BlockSpec((tm, 2 * D_FF), lambda i: (i, 0))], out_shape=[_sds((S, 2 * D_FF), BF16)],
        args=(dcg, dcg, dcv, dcv, cw))
    return du if plans is None else (du, carried)


def _row_tile(n, cap):
    best = n
    for t in range(16, cap + 1, 16):
        if n % t == 0:
            best = t
    return best if best <= cap else n


def _rows_for_bytes(nbytes, cols):
    return max(16, nbytes // (4 * cols) // 16 * 16)


def _adamw(w, g, m, v, name):
    R, C = w.shape
    tr = _row_tile(R, _rows_for_bytes(2 << 20, C))

    def body(w_ref, g_ref, m_ref, v_ref, d_ref, nm_ref, nv_ref):
        gv = g_ref[...]
        nm = ADAM_B1 * m_ref[...] + (1.0 - ADAM_B1) * gv
        nv = ADAM_B2 * v_ref[...] + (1.0 - ADAM_B2) * (gv * gv)
        m_hat = nm / (1.0 - ADAM_B1 ** ADAM_STEP)
        v_hat = nv / (1.0 - ADAM_B2 ** ADAM_STEP)
        d_ref[...] = -ADAM_LR * (m_hat / (jnp.sqrt(v_hat) + ADAM_EPS) + ADAM_WD * w_ref[...])
        nm_ref[...] = nm
        nv_ref[...] = nv

    spec = pl.BlockSpec((tr, C), lambda i: (i, 0))
    return pl.pallas_call(body, name=name, grid=(R // tr,), in_specs=[spec] * 4, out_specs=[spec] * 3,
                          out_shape=[_sds((R, C), F32)] * 3, compiler_params=_cp(1))(w, g, m, v)


def _pair_sum(gfull, rcv, c_idx, name):
    nb, R, C = gfull.shape
    half = R // 2
    tr = _row_tile(half, _rows_for_bytes(2 << 20, C))
    nt = half // tr

    def body(c_ref, g_ref, r_ref, o_ref):
        o_ref[...] = _bf(g_ref[...] + r_ref[...])

    return pl.pallas_call(
        body, name=name,
        grid_spec=pltpu.PrefetchScalarGridSpec(
            num_scalar_prefetch=1, grid=(nb, nt),
            in_specs=[pl.BlockSpec((None, tr, C), lambda j, i, c_ref: (j, c_ref[0] * nt + i, 0)),
                      pl.BlockSpec((None, tr, C), lambda j, i, c_ref: (j, i, 0))],
            out_specs=pl.BlockSpec((None, tr, C), lambda j, i, c_ref: (j, i, 0))),
        out_shape=_sds((nb, half, C), BF16), compiler_params=_cp(2))(c_idx, gfull, rcv)


def _chip_sum(arrived, own, place, name):
    nb, H, C = arrived.shape
    tr = _row_tile(H, _rows_for_bytes(2 << 20, C))
    nt = H // tr

    def body(pl_ref, *refs):
        o_ref = refs[nb + 1]
        me = pl_ref[0]
        acc = None
        for k in range(nb):
            term = jnp.where(me == k, refs[nb][...], refs[k][...]).astype(F32)
            acc = term if acc is None else acc + term
        o_ref[...] = acc

    def other(k):
        return pl.BlockSpec((None, tr, C), lambda i, p: (jnp.where(p[0] == k, (k + 1) % nb, k), i, 0))

    return pl.pallas_call(
        body, name=name,
        grid_spec=pltpu.PrefetchScalarGridSpec(
            num_scalar_prefetch=1, grid=(nt,),
            in_specs=[other(k) for k in range(nb)] + [pl.BlockSpec((None, tr, C), lambda i, p: (p[0], i, 0))],
            out_specs=pl.BlockSpec((tr, C), lambda i, p: (p[1] * nt + i, 0))),
        out_shape=_sds((2 * H, C), F32), compiler_params=_cp(1))(place, *([arrived] * nb), own)


def _cast_into_slot(shard, place, name):
    R, C = shard.shape
    tr = _row_tile(R, 256)

    def body(pl_ref, s_ref, o_ref):
        o_ref[...] = _bf(s_ref[...])

    return pl.pallas_call(
        body, name=name,
        grid_spec=pltpu.PrefetchScalarGridSpec(
            num_scalar_prefetch=1, grid=(R // tr,),
            in_specs=[pl.BlockSpec((tr, C), lambda i, p: (i, 0))],
            out_specs=pl.BlockSpec((None, tr, C), lambda i, p: (p[0], i, 0))),
        out_shape=_sds((N_CHIPS, R, C), BF16), compiler_params=_cp(1))(place, shard)


def _place():
    x, y, c = lax.axis_index("x"), lax.axis_index("y"), lax.axis_index("c")
    chips = [(1 - x, y), (x, 1 - y), (1 - x, 1 - y)]
    return x, y, c, chips


def _chip_id(px, py):
    return 2 * px + py


def _remote(src, dst, send_sems, recv_sems, k, to):
    return pltpu.make_async_remote_copy(src_ref=src, dst_ref=dst, send_sem=send_sems.at[k], recv_sem=recv_sems.at[k],
                                        device_id=to, device_id_type=MESH)


def _gather_weights(slots, wholes, name):
    ns, nw = len(slots), len(wholes)
    n = ns + nw

    def body(*refs):
        ins = refs[ns:n]
        outs = refs[n:2 * n]
        send_sems, recv_sems, local_sems = refs[2 * n:]
        x, y, c, chips = _place()
        me = _chip_id(x, y)
        sib = (x, y, 1 - c)
        local = [pltpu.make_async_copy(ins[b], outs[ns + b].at[me], local_sems.at[b]) for b in range(nw)]
        for cp in local:
            cp.start()
        sent = []
        for a in range(n):
            R = outs[a].shape[1]
            rows = pl.ds(c * (R // 2), R // 2) if a < ns else pl.ds(0, R)
            src = outs[a].at[me, rows] if a < ns else ins[a - ns]
            for j, chip in enumerate(chips):
                cp = _remote(src, outs[a].at[me, rows], send_sems, recv_sems, 6 * a + j, (*chip, c))
                cp.start()
                sent.append(cp)
        for a in range(n):
            R = outs[a].shape[1]
            rows = pl.ds(c * (R // 2), R // 2) if a < ns else pl.ds(0, R)
            for j, chip in enumerate(chips):
                landed = outs[a].at[_chip_id(*chip), rows]
                _remote(landed, landed, send_sems, recv_sems, 6 * a + j, (*chip, c)).wait_recv()
                if a < ns:
                    cp = _remote(landed, landed, send_sems, recv_sems, 6 * a + 3 + j, sib)
                    cp.start()
                    sent.append(cp)
        for a in range(ns):
            R = outs[a].shape[1]
            other = pl.ds((1 - c) * (R // 2), R // 2)
            for j, chip in enumerate(chips):
                passed = outs[a].at[_chip_id(*chip), other]
                _remote(passed, passed, send_sems, recv_sems, 6 * a + 3 + j, sib).wait_recv()
        for cp in sent:
            cp.wait_send()
        for cp in local:
            cp.wait()

    return pl.pallas_call(
        body, name=name, in_specs=[ANY] * n, out_specs=[ANY] * n,
        out_shape=[_sds(s.shape, s.dtype) for s in slots] + [_sds((N_CHIPS, *s.shape), s.dtype) for s in wholes],
        input_output_aliases={a: a for a in range(ns)},
        scratch_shapes=[pltpu.SemaphoreType.DMA((6 * n,)), pltpu.SemaphoreType.DMA((6 * n,)),
                        pltpu.SemaphoreType.DMA((max(nw, 1),))])(*slots, *wholes)


def _gather_ici_plan(slots, wholes):
    ns, nw = len(slots), len(wholes)

    def copies(ins, ios, outs, send_sems, recv_sems, local_sems):
        x, y, c, chips = _place()
        me = _chip_id(x, y)
        sends, recvs = [], []
        for a in range(ns + nw):
            dst = ios[a] if a < ns else outs[a - ns]
            R = dst.shape[1]
            rows = pl.ds(c * (R // 2), R // 2) if a < ns else pl.ds(0, R)
            src = dst.at[me, rows] if a < ns else ins[a - ns]
            for j, chip in enumerate(chips):
                sends.append(_remote(src, dst.at[me, rows], send_sems, recv_sems, 3 * a + j, (*chip, c)))
                landed = dst.at[_chip_id(*chip), rows]
                recvs.append(_remote(landed, landed, send_sems, recv_sems, 3 * a + j, (*chip, c)))
        local = [pltpu.make_async_copy(ins[b], outs[b].at[me], local_sems.at[b]) for b in range(nw)]
        return sends, recvs, local

    return _Plan(copies, 3 * (ns + nw), ins=wholes, inouts=slots,
                 outs=[_sds((N_CHIPS, *s.shape), s.dtype) for s in wholes])


def _gather_pass_plan(slots):
    def copies(ins, ios, outs, send_sems, recv_sems, local_sems):
        x, y, c, chips = _place()
        sib = (x, y, 1 - c)
        sends, recvs = [], []
        for a, buf in enumerate(ios):
            half = buf.shape[1] // 2
            for j, chip in enumerate(chips):
                mine = buf.at[_chip_id(*chip), pl.ds(c * half, half)]
                other = buf.at[_chip_id(*chip), pl.ds((1 - c) * half, half)]
                sends.append(_remote(mine, mine, send_sems, recv_sems, 3 * a + j, sib))
                recvs.append(_remote(other, other, send_sems, recv_sems, 3 * a + j, sib))
        return sends, recvs, []

    return _Plan(copies, 3 * len(slots), inouts=slots)


def _pair_plan(grads):
    def copies(ins, ios, outs, send_sems, recv_sems, local_sems):
        x, y, c, _ = _place()
        sib = (x, y, 1 - c)
        sends, recvs = [], []
        for a, g in enumerate(ins):
            half = g.shape[1] // 2
            sends.append(_remote(g.at[:, pl.ds((1 - c) * half, half), :], outs[a], send_sems, recv_sems, a, sib))
            recvs.append(_remote(outs[a], outs[a], send_sems, recv_sems, a, sib))
        return sends, recvs, []

    return _Plan(copies, len(grads), ins=grads,
                 outs=[_sds((g.shape[0], g.shape[1] // 2, g.shape[2]), g.dtype) for g in grads])


def _chip_plan(parts):
    def copies(ins, ios, outs, send_sems, recv_sems, local_sems):
        x, y, c, chips = _place()
        me = _chip_id(x, y)
        sends, recvs = [], []
        for a, part in enumerate(ins):
            for j, chip in enumerate(chips):
                sends.append(_remote(part.at[_chip_id(*chip)], outs[a].at[me], send_sems, recv_sems, 3 * a + j, (*chip, c)))
                landed = outs[a].at[_chip_id(*chip)]
                recvs.append(_remote(landed, landed, send_sems, recv_sems, 3 * a + j, (*chip, c)))
        return sends, recvs, []

    return _Plan(copies, 3 * len(parts), ins=parts, outs=[_sds(p.shape, p.dtype) for p in parts])


def _pair_concat(fulls, name):
    n = len(fulls)

    def body(*refs):
        outs = refs[n:2 * n]
        send_sems, recv_sems = refs[2 * n:]
        x, y, c, _ = _place()
        cps = []
        for a in range(n):
            H = outs[a].shape[0] // 2
            mine = outs[a].at[pl.ds(c * H, H)]
            cp = _remote(mine, mine, send_sems, recv_sems, a, (x, y, 1 - c))
            cp.start()
            cps.append(cp)
        for a, cp in enumerate(cps):
            H = outs[a].shape[0] // 2
            other = outs[a].at[pl.ds((1 - c) * H, H)]
            _remote(other, other, send_sems, recv_sems, a, (x, y, 1 - c)).wait_recv()
            cp.wait_send()

    return pl.pallas_call(
        body, name=name, in_specs=[ANY] * n, out_specs=[ANY] * n,
        out_shape=[_sds(f.shape, f.dtype) for f in fulls], input_output_aliases={a: a for a in range(n)},
        scratch_shapes=[pltpu.SemaphoreType.DMA((n,)), pltpu.SemaphoreType.DMA((n,))])(*fulls)


def _all_sum(pack, name):
    R, C = pack.shape

    def body(p_ref, o_ref, buf, send_sems, recv_sems):
        x, y, c, _ = _place()
        me = 4 * x + 2 * y + c
        buf[me] = p_ref[...]
        cps = []
        for k in range(1, N_DEV):
            to = (x ^ (k >> 2), y ^ ((k >> 1) & 1), c ^ (k & 1))
            cp = _remote(p_ref, buf.at[me], send_sems, recv_sems, k - 1, to)
            cp.start()
            cps.append(cp)
        for k in range(1, N_DEV):
            frm = (x ^ (k >> 2), y ^ ((k >> 1) & 1), c ^ (k & 1))
            slot = buf.at[4 * frm[0] + 2 * frm[1] + frm[2]]
            _remote(slot, slot, send_sems, recv_sems, k - 1, frm).wait_recv()
        acc = buf[0]
        for k in range(1, N_DEV):
            acc = acc + buf[k]
        o_ref[...] = acc
        for cp in cps:
            cp.wait_send()

    vm = pl.BlockSpec(memory_space=pltpu.VMEM)
    return pl.pallas_call(
        body, name=name, in_specs=[vm], out_specs=vm, out_shape=_sds((R, C), F32),
        scratch_shapes=[pltpu.VMEM((N_DEV, R, C), F32), pltpu.SemaphoreType.DMA((N_DEV - 1,)),
                        pltpu.SemaphoreType.DMA((N_DEV - 1,))])(pack)


def _local_step(xs, tgt, p, ex):
    proj, h1, got = _norm_proj(xs, p["pre_mix_norm"], ex.weight("w_in"), "proj_in", plans=ex.carry("proj_in"))
    ex.done("proj_in", got)
    biases = _relbias_fwd(p["rel_bias"], "rel_bias_fwd")
    fw = []
    for g in range(N_GROUPS):
        res, got = _attn_fwd(proj, biases[g], g, f"attn_fwd{g}", plans=ex.carry(f"attn_fwd{g}"))
        ex.done(f"attn_fwd{g}", got)
        fw.append(res)
    yh, o_h, states = _hgrn_fwd(proj, p["hgrn_lb_raw"], p["hgrn_norm"], "hgrn_fwd")
    W_a, W_h, W_out = ex.weight("w_branch_attn"), ex.weight("w_branch_hgrn"), ex.weight("w_out")
    W_up, W_down, conv_w = ex.weight("w_up"), ex.weight("w_down"), ex.weight("conv_w")
    y, lse, za, zh, merged = _branch_fwd([t[0] for t in fw], [t[1] for t in fw], yh, proj, W_a, W_h, "branch_fwd")
    mo, x1, h2 = _mix_out(merged, W_out, xs, p["post_mix_norm"], p["pre_ffn_norm"], "mix_out")
    u = _mm_nn_blk(h2, W_up, "ffn_up")
    a = _conv_gelu_fwd(u, conv_w, p["conv_b"], "conv_gelu_fwd")
    dx2, dff, g_post_ffn, loss = _loss_head(a, W_down, x1, tgt, p["post_ffn_norm"], "ffn_down_loss")

    da = _mm_nt(dff, W_down, "d_ffn_act")
    ex.grad("w_down", _mm_tn(a, dff, "g_w_down").reshape(N_CHIPS, D_FF // N_CHIPS, D_MODEL))
    (dcg, dcv, gwg, gwv, gbg, gbv), got = _conv_gelu_bwd(u, da, conv_w, p["conv_b"], "conv_gelu_bwd",
                                                          plans=ex.carry("conv_gelu_bwd"))
    ex.done("conv_gelu_bwd", got)
    g_conv_w = jnp.concatenate([gwg, gwv], axis=1)
    g_conv_b = jnp.concatenate([gbg, gbv], axis=1)
    du, got = _conv_input_bwd(dcg, dcv, conv_w, "conv_input_bwd", plans=ex.carry("conv_input_bwd"))
    ex.done("conv_input_bwd", got)
    dh2 = _mm_nt_blk(du, W_up, "d_ffn_in")
    ex.grad("w_up", _mm_tn_blk(h2, du, N_CHIPS, "g_w_up"))
    (dx1, g_pre_ffn), got = _prenorm_bwd(dh2, x1, p["pre_ffn_norm"], dx2, "pre_ffn_norm_bwd",
                                         plans=ex.carry("pre_ffn_norm_bwd"))
    ex.done("pre_ffn_norm_bwd", got)
    dmo, dmerged, g_post_mix = _postnorm_bwd(dx1, mo, p["post_mix_norm"], W_out, "post_mix_norm_bwd")
    ex.grad("w_out", _mm_tn(merged, dmo, "g_w_out").reshape(N_CHIPS, D_MODEL // N_CHIPS, D_MODEL))
    dza, dzh, dg0, dg1, dy, dyh = _branch_bwd(dmerged, za, zh, proj, W_a, W_h, "branch_bwd")
    ex.grad("w_branch_attn", _mm_tn_blk(y, dza, N_CHIPS, "g_w_branch_attn", together=True))
    ex.grad("w_branch_hgrn", _mm_tn_blk(yh, dzh, N_CHIPS, "g_w_branch_hgrn", together=True))
    dqkv, dbs = [], []
    for g in range(N_GROUPS):
        parts, db, got = _attn_bwd(proj, biases[g], lse, y, dy, g, f"attn_bwd{g}", plans=ex.carry(f"attn_bwd{g}"))
        ex.done(f"attn_bwd{g}", got)
        dqkv += parts
        dbs.append(db)
    g_rel_bias = _relbias_bwd(dbs, "rel_bias_bwd")
    dproj, g_lb_raw, g_hgrn_norm = _hgrn_bwd(proj, p["hgrn_lb_raw"], p["hgrn_norm"], o_h, states, dyh, dqkv,
                                             [dg0, dg1], "hgrn_bwd")
    for piece in W_IN_PIECES:
        g, got = _mm_tn_blk(h1, dproj, N_CHIPS, f"g_{piece}", x_cols=W_IN_ROWS[piece],
                            plans=ex.carry(f"g_{piece}"))
        ex.done(f"g_{piece}", got)
        ex.grad(piece, g)
    dh1, got = _mm_nt_blk(dproj, ex.weight("w_in"), "d_proj_in", plans=ex.carry("d_proj_in"))
    ex.done("d_proj_in", got)
    (grad_x, g_pre_mix), got = _prenorm_bwd(dh1, xs, p["pre_mix_norm"], dx1, "pre_mix_norm_bwd",
                                            plans=ex.carry("pre_mix_norm_bwd"))
    ex.done("pre_mix_norm_bwd", got)
    small = dict(pre_mix_norm=g_pre_mix, rel_bias=g_rel_bias, hgrn_lb_raw=g_lb_raw, hgrn_norm=g_hgrn_norm,
                 post_mix_norm=g_post_mix, pre_ffn_norm=g_pre_ffn, conv_w=g_conv_w, conv_b=g_conv_b,
                 post_ffn_norm=g_post_ffn)
    return loss, grad_x, small


SMALL = ("pre_mix_norm", "rel_bias", "hgrn_lb_raw", "hgrn_norm", "post_mix_norm", "pre_ffn_norm", "conv_w", "conv_b",
         "post_ffn_norm")
BIG = ("w_in", "w_up", "w_down", "w_out", "w_branch_attn", "w_branch_hgrn")
WEIGHTS = ("pre_mix_norm", "w_in", "rel_bias", "hgrn_lb_raw", "hgrn_norm", "w_branch_attn", "w_branch_hgrn", "w_out",
           "post_mix_norm", "pre_ffn_norm", "w_up", "conv_w", "conv_b", "w_down", "post_ffn_norm")
MIXER = ("w_out", "w_branch_attn", "w_branch_hgrn")

SCHEDULE = {
    "proj_in": [("gather_ici_cw", ("w_up",) + MIXER)],
    "attn_fwd0": [("gather_pass", ("w_up",) + MIXER), ("gather_ici", ("w_down",))],
    "attn_fwd1": [("gather_pass", ("w_down",))],
    "conv_gelu_bwd": [("pair", ("w_down",))],
    "conv_input_bwd": [("chip", ("w_down",))],
    "pre_ffn_norm_bwd": [("pair", ("w_up",))],
    "attn_bwd0": [("chip", ("w_up",)), ("pair", MIXER)],
    "attn_bwd1": [("chip", MIXER)],
    "g_w_in_b": [("pair", ("w_in_a",))],
    "d_proj_in": [("chip", ("w_in_a",)), ("pair", ("w_in_b",))],
    "pre_mix_norm_bwd": [("chip", ("w_in_b",))],
}
W_IN_ROWS = dict(w_in_a=(0, 768), w_in_b=(3, 256))
W_IN_PIECES = tuple(W_IN_ROWS)
REDUCED = W_IN_PIECES + BIG[1:]


class _Exchange:
    def __init__(self, place, slots, conv_w_shard):
        self.place, self.slots, self.conv_w_shard = place, dict(slots), conv_w_shard
        self.conv_w = None
        self.g, self.from_sibling, self.pair_sums, self.arrived = {}, {}, {}, {}
        self.pending = []

    def weight(self, name):
        if name == "conv_w":
            return self.conv_w
        w = self.slots[name]
        return w.reshape(-1, D_MODEL) if name in ("w_out", "w_down") else w

    def grad(self, name, g):
        self.g[name] = g

    def carry(self, point):
        plans = []
        self.pending = SCHEDULE.get(point, [])
        for kind, names in self.pending:
            if kind in ("gather_ici", "gather_ici_cw"):
                wholes = [self.conv_w_shard] if kind == "gather_ici_cw" else []
                plans.append(_gather_ici_plan([self.slots[n] for n in names], wholes))
            elif kind == "gather_pass":
                plans.append(_gather_pass_plan([self.slots[n] for n in names]))
            elif kind == "pair":
                plans.append(_pair_plan([self.g[n] for n in names]))
            else:
                for n in names:
                    self.pair_sums[n] = _pair_sum(self.g[n], self.from_sibling[n], self.place[1:2], f"pair_sum_{n}")
                plans.append(_chip_plan([self.pair_sums[n] for n in names]))
        return plans

    def done(self, point, carried):
        for (kind, names), got in zip(self.pending, carried):
            if kind in ("gather_ici", "gather_ici_cw", "gather_pass"):
                self.slots.update(zip(names, got))
                if kind == "gather_ici_cw":
                    self.conv_w = got[len(names)].transpose(1, 0, 2).reshape(3, 2 * D_FF)
            elif kind == "pair":
                self.from_sibling.update(zip(names, got))
            else:
                self.arrived.update(zip(names, got))

    def reduced(self):
        halves = [_chip_sum(self.arrived[n], self.pair_sums[n], self.place, f"chip_sum_{n}") for n in REDUCED]
        out = dict(zip(REDUCED, _pair_concat(halves, "pair_concat")))
        out["w_in"] = jnp.concatenate([out.pop(n) for n in W_IN_PIECES], axis=0)
        return out


def kernel(x, pre_mix_norm, w_in, rel_bias, hgrn_lb_raw, hgrn_norm, w_branch_attn, w_branch_hgrn, w_out, post_mix_norm, pre_ffn_norm, w_up, conv_w, conv_b, w_down, post_ffn_norm, loss_target, m_pre_mix_norm, m_w_in, m_rel_bias, m_hgrn_lb_raw, m_hgrn_norm, m_w_branch_attn, m_w_branch_hgrn, m_w_out, m_post_mix_norm, m_pre_ffn_norm, m_w_up, m_conv_w, m_conv_b, m_w_down, m_post_ffn_norm, v_pre_mix_norm, v_w_in, v_rel_bias, v_hgrn_lb_raw, v_hgrn_norm, v_w_branch_attn, v_w_branch_hgrn, v_w_out, v_post_mix_norm, v_pre_ffn_norm, v_w_up, v_conv_w, v_conv_b, v_w_down, v_post_ffn_norm):
    w = dict(pre_mix_norm=pre_mix_norm, w_in=w_in, rel_bias=rel_bias, hgrn_lb_raw=hgrn_lb_raw, hgrn_norm=hgrn_norm,
             w_branch_attn=w_branch_attn, w_branch_hgrn=w_branch_hgrn, w_out=w_out, post_mix_norm=post_mix_norm,
             pre_ffn_norm=pre_ffn_norm, w_up=w_up, conv_w=conv_w, conv_b=conv_b, w_down=w_down,
             post_ffn_norm=post_ffn_norm)
    m = dict(pre_mix_norm=m_pre_mix_norm, w_in=m_w_in, rel_bias=m_rel_bias, hgrn_lb_raw=m_hgrn_lb_raw,
             hgrn_norm=m_hgrn_norm, w_branch_attn=m_w_branch_attn, w_branch_hgrn=m_w_branch_hgrn, w_out=m_w_out,
             post_mix_norm=m_post_mix_norm, pre_ffn_norm=m_pre_ffn_norm, w_up=m_w_up, conv_w=m_conv_w,
             conv_b=m_conv_b, w_down=m_w_down, post_ffn_norm=m_post_ffn_norm)
    v = dict(pre_mix_norm=v_pre_mix_norm, w_in=v_w_in, rel_bias=v_rel_bias, hgrn_lb_raw=v_hgrn_lb_raw,
             hgrn_norm=v_hgrn_norm, w_branch_attn=v_w_branch_attn, w_branch_hgrn=v_w_branch_hgrn, w_out=v_w_out,
             post_mix_norm=v_post_mix_norm, pre_ffn_norm=v_pre_ffn_norm, w_up=v_w_up, conv_w=v_conv_w,
             conv_b=v_conv_b, w_down=v_w_down, post_ffn_norm=v_post_ffn_norm)
    shard2d = {n: (w[n][0] if w[n].ndim == 3 else w[n]) for n in WEIGHTS}
    chip = 2 * lax.axis_index("x") + lax.axis_index("y")
    core = lax.axis_index("c")

    place = jnp.stack([chip, core]).astype(jnp.int32)
    slots = {n: _cast_into_slot(shard2d[n], place, f"cast_{n}") for n in BIG}
    slots["w_in"] = _gather_weights([slots["w_in"]], [], "gather_w_in")[0]
    ex = _Exchange(place, slots, shard2d["conv_w"])
    loss, grad_x, small = _local_step(x[0], loss_target[0], {n: w[n] for n in SMALL if n != "conv_w"}, ex)

    flat = [small[n].reshape(-1) for n in SMALL] + [loss.reshape(-1)]
    sizes = [t.shape[0] for t in flat]
    summed = _all_sum(jnp.concatenate(flat).reshape(-1, LANES), "sum_small").reshape(-1)
    offs = [sum(sizes[:i]) for i in range(len(sizes))]
    grads = {}
    for n, o, sz in zip(SMALL, offs, sizes):
        grads[n] = summed[o:o + sz].reshape(small[n].shape)
    loss_total = summed[offs[-1]]
    cw = 2 * D_FF // N_CHIPS
    grads["conv_w"] = lax.dynamic_slice(grads["conv_w"], (0, chip * cw), (3, cw))

    grads.update(ex.reduced())

    out_g, out_d, out_m, out_v = [], [], [], []
    for n in WEIGHTS:
        d2, m2, v2 = _adamw(shard2d[n], grads[n], m[n].reshape(shard2d[n].shape), v[n].reshape(shard2d[n].shape),
                            f"adamw_{n}")
        shape = w[n].shape
        out_g.append(grads[n].reshape(shape))
        out_d.append(d2.reshape(shape))
        out_m.append(m2.reshape(shape))
        out_v.append(v2.reshape(shape))
    return (loss_total, grad_x[None], *out_g, *out_d, *out_m, *out_v)
```

```python
import functools
import math

import jax
import jax.numpy as jnp
from jax import lax
from jax.experimental import pallas as pl
from jax.experimental.pallas import tpu as pltpu

F32 = jnp.float32
BF16 = jnp.bfloat16
MESH = pl.DeviceIdType.MESH

D_MODEL = 1024
N_GROUPS = 3
DILATIONS = (1, 4, 16)
HEADS = 8
HEAD_DIM = 64
GROUP_W = HEADS * HEAD_DIM
QKV_W = N_GROUPS * 3 * GROUP_W
BLK = 128
NEG_INF = -1e30
NUM_BUCKETS = 32
MAX_EXACT = 16
MAX_DISTANCE = 2048
HG_HEADS = 4
HG_DK = 128
HG_W = HG_HEADS * HG_DK
HG_CHUNK = 32
HG_TILE = 256
IN_W = QKV_W + 4 * HG_W + 2 * D_MODEL
D_FF = 2816
EPS = 1e-6
N_CHIPS = 4
N_DEV = 8
LANES = 128

ADAM_LR, ADAM_B1, ADAM_B2, ADAM_EPS, ADAM_WD, ADAM_STEP = 0.001, 0.9, 0.999, 1e-08, 0.01, 10

VMEM_LIMIT = 56 * 1024 * 1024


def _cp(n_axes):
    return pltpu.CompilerParams(dimension_semantics=("arbitrary",) * n_axes, vmem_limit_bytes=VMEM_LIMIT)


def _sds(shape, dtype):
    return jax.ShapeDtypeStruct(tuple(shape), dtype)


def _sigmoid(v):
    return 1.0 / (1.0 + jnp.exp(-v))


def _bf(v):
    return v.astype(BF16)


def _dot(a, b, dims):
    return lax.dot_general(a, b, (dims, ((), ())), preferred_element_type=F32)


NN = ((1,), (0,))
NT = ((1,), (1,))
TN = ((0,), (0,))

ANY = pl.BlockSpec(memory_space=pl.ANY)


class _Plan:
    def __init__(self, copies, n_sems, ins=(), inouts=(), outs=()):
        self.copies, self.n_sems = copies, n_sems
        self.ins, self.inouts, self.outs = list(ins), list(inouts), list(outs)


def _call(body, plans=None, *, name, grid, in_specs, out_specs, out_shape, args, scratch_shapes=()):
    plans = list(plans or ())
    in_specs, out_specs, out_shape = list(in_specs), list(out_specs), list(out_shape)
    scratch_shapes = list(scratch_shapes)
    n_in, n_out, n_scr = len(in_specs), len(out_specs), len(scratch_shapes)
    x_in, x_out, aliases, spans = [], [], {}, []
    for p in plans:
        i0, o0 = len(x_in), len(x_out)
        x_in += p.ins
        for a in p.inouts:
            aliases[n_in + len(x_in)] = n_out + len(x_out)
            x_in.append(a)
            x_out.append(_sds(a.shape, a.dtype))
        x_out += p.outs
        spans.append((i0, len(p.ins), o0, len(p.inouts), len(p.outs)))
    sems = [pltpu.SemaphoreType.DMA((p.n_sems,)) for p in plans for _ in range(3)]

    def wrapped(*refs):
        xi = refs[n_in:n_in + len(x_in)]
        base = n_in + len(x_in)
        xo = refs[base + n_out:base + n_out + len(x_out)]
        sbase = base + n_out + len(x_out)
        xs = refs[sbase + n_scr:]
        ids = [pl.program_id(k) for k in range(len(grid))]
        first = functools.reduce(jnp.logical_and, [i == 0 for i in ids])
        last = functools.reduce(jnp.logical_and, [i == g - 1 for i, g in zip(ids, grid)])

        def descriptors(k):
            i0, ni, o0, nio, no = spans[k]
            return plans[k].copies(xi[i0:i0 + ni], xo[o0:o0 + nio], xo[o0 + nio:o0 + nio + no], *xs[3 * k:3 * k + 3])

        @pl.when(first)
        def _():
            for k in range(len(plans)):
                sends, _, local = descriptors(k)
                for cp in (*sends, *local):
                    cp.start()

        body(*refs[:n_in], *refs[base:base + n_out], *refs[sbase:sbase + n_scr])

        @pl.when(last)
        def _():
            for k in range(len(plans)):
                sends, recvs, local = descriptors(k)
                for cp in recvs:
                    cp.wait_recv()
                for cp in sends:
                    cp.wait_send()
                for cp in local:
                    cp.wait()

    res = pl.pallas_call(
        wrapped if plans else body, name=name, grid=grid, in_specs=in_specs + [ANY] * len(x_in),
        out_specs=out_specs + [ANY] * len(x_out), out_shape=out_shape + x_out, input_output_aliases=aliases,
        scratch_shapes=scratch_shapes + sems, compiler_params=_cp(len(grid)))(*args, *x_in)
    res = list(res)
    carried = [res[n_out + o0:n_out + o0 + nio + no] for (_, _, o0, nio, no) in spans]
    return res[:n_out], carried


def _mm_nn_blk(a, wg, name, tm=512, plans=None):
    M, K = a.shape
    nb, _, Nb = wg.shape

    def body(a_ref, w_ref, o_ref):
        o_ref[...] = _dot(_bf(a_ref[...]), w_ref[...], NN)

    (out,), carried = _call(
        body, plans, name=name, grid=(nb, M // tm),
        in_specs=[pl.BlockSpec((tm, K), lambda j, i: (i, 0)), pl.BlockSpec((None, K, Nb), lambda j, i: (j, 0, 0))],
        out_specs=[pl.BlockSpec((tm, Nb), lambda j, i: (i, j))],
        out_shape=[_sds((M, nb * Nb), F32)], args=(a, wg))
    return out if plans is None else (out, carried)


def _mm_nt_blk(dy, wg, name, tm=1024, plans=None):
    M = dy.shape[0]
    nb, K, Nb = wg.shape

    def body(dy_ref, w_ref, o_ref):
        j = pl.program_id(1)
        r = _dot(_bf(dy_ref[...]), w_ref[...], NT)

        @pl.when(j == 0)
        def _():
            o_ref[...] = r

        @pl.when(j > 0)
        def _():
            o_ref[...] += r

    (out,), carried = _call(
        body, plans, name=name, grid=(M // tm, nb),
        in_specs=[pl.BlockSpec((tm, Nb), lambda i, j: (i, j)), pl.BlockSpec((None, K, Nb), lambda i, j: (j, 0, 0))],
        out_specs=[pl.BlockSpec((tm, K), lambda i, j: (i, 0))],
        out_shape=[_sds((M, K), F32)], args=(dy, wg))
    return out if plans is None else (out, carried)


def _mm_tn_blk(x, dy, nb, name, tk=2048, x_cols=None, plans=None, together=False):
    T, Mx = x.shape
    xk, Mx = (0, Mx) if x_cols is None else x_cols
    Nb = dy.shape[1] // nb
    nj = nb if together else 1

    def body(x_ref, dy_ref, o_ref):
        t = pl.program_id(1)
        r = _dot(_bf(x_ref[...]), _bf(dy_ref[...]), TN)
        for j in range(nj):
            rj = r[:, j * Nb:(j + 1) * Nb]

            @pl.when(t == 0)
            def _():
                o_ref[j] = rj

            @pl.when(t > 0)
            def _():
                o_ref[j] += rj

    (out,), carried = _call(
        body, plans, name=name, grid=(nb // nj, T // tk),
        in_specs=[pl.BlockSpec((tk, Mx), lambda j, t: (t, xk)), pl.BlockSpec((tk, nj * Nb), lambda j, t: (t, j))],
        out_specs=[pl.BlockSpec((nj, Mx, Nb), lambda j, t: (j, 0, 0))],
        out_shape=[_sds((nb, Mx, Nb), F32)], args=(x, dy))
    return out if plans is None else (out, carried)


def _mm_tn(x, dy, name, tk=1024):
    T, Mx = x.shape
    N = dy.shape[1]

    def body(x_ref, dy_ref, o_ref):
        t = pl.program_id(0)
        r = _dot(_bf(x_ref[...]), _bf(dy_ref[...]), TN)

        @pl.when(t == 0)
        def _():
            o_ref[...] = r

        @pl.when(t > 0)
        def _():
            o_ref[...] += r

    return pl.pallas_call(
        body, name=name, grid=(T // tk,),
        in_specs=[pl.BlockSpec((tk, Mx), lambda t: (t, 0)), pl.BlockSpec((tk, N), lambda t: (t, 0))],
        out_specs=pl.BlockSpec((Mx, N), lambda t: (0, 0)),
        out_shape=_sds((Mx, N), F32), compiler_params=_cp(1))(x, dy)


def _tile(arr, bw, col=lambda c: 0):
    return ("tile", arr, bw, col)


def _full(arr):
    return ("full", arr)


def _out_tile(width, dtype, bw, col=lambda c: 0):
    return ("tile", width, dtype, bw, col)


def _out_acc(rows, width, bw, col=lambda c: 0):
    return ("acc", rows, width, bw, col)


def _rows_call(name, body, n_rows, tm, ncol, ins, outs, plans=None):
    in_specs, args = [], []
    for e in ins:
        if e[0] == "tile":
            _, arr, bw, col = e
            in_specs.append(pl.BlockSpec((tm, bw), functools.partial(lambda c, i, col: (i, col(c)), col=col)))
        else:
            arr = e[1]
            in_specs.append(pl.BlockSpec(arr.shape, functools.partial(lambda c, i, nd: (0,) * nd, nd=arr.ndim)))
        args.append(arr)
    out_specs, out_shape = [], []
    for e in outs:
        if e[0] == "tile":
            _, width, dtype, bw, col = e
            out_specs.append(pl.BlockSpec((tm, bw), functools.partial(lambda c, i, col: (i, col(c)), col=col)))
            out_shape.append(_sds((n_rows, width), dtype))
        else:
            _, rows, width, bw, col = e
            out_specs.append(pl.BlockSpec((rows, bw), functools.partial(lambda c, i, col: (0, col(c)), col=col)))
            out_shape.append(_sds((rows, width), F32))
    out, carried = _call(body, plans, name=name, grid=(ncol, n_rows // tm), in_specs=in_specs, out_specs=out_specs,
                         out_shape=out_shape, args=args)
    return out if plans is None else (out, carried)


def _acc(ref, val):
    i = pl.program_id(1)

    @pl.when(i == 0)
    def _():
        ref[...] = val

    @pl.when(i > 0)
    def _():
        ref[...] += val


def _rinv(z):
    return lax.rsqrt(jnp.mean(z * z, axis=-1, keepdims=True) + EPS)


def _norm_bwd(dy, zhat, r, w):
    dyw = dy * w
    return r * (dyw - zhat * jnp.mean(dyw * zhat, axis=-1, keepdims=True))


def _norm_proj(x, w, wg, name, tm=1024, plans=None):
    M, K = x.shape
    nb, _, Nb = wg.shape

    def body(x_ref, w_ref, wg_ref, o_ref, h_ref):
        @pl.when(pl.program_id(1) == 0)
        def _():
            xv = x_ref[...]
            h_ref[...] = _bf(xv * _rinv(xv) * w_ref[...])

        o_ref[...] = _dot(h_ref[...], wg_ref[...], NN)

    (out, h), carried = _call(
        body, plans, name=name, grid=(M // tm, nb),
        in_specs=[pl.BlockSpec((tm, K), lambda i, j: (i, 0)), pl.BlockSpec((1, K), lambda i, j: (0, 0)),
                  pl.BlockSpec((None, K, Nb), lambda i, j: (j, 0, 0))],
        out_specs=[pl.BlockSpec((tm, Nb), lambda i, j: (i, j)), pl.BlockSpec((tm, K), lambda i, j: (i, 0))],
        out_shape=[_sds((M, nb * Nb), F32), _sds((M, K), BF16)], args=(x, w, wg))
    return (out, h) if plans is None else (out, h, carried)


def _prenorm_bwd(dh, xin, w, dres, name, plans=None):
    def body(dh_ref, x_ref, w_ref, dres_ref, dx_ref, dw_ref):
        xv = x_ref[...]
        r = _rinv(xv)
        xhat = xv * r
        dhv = dh_ref[...]
        dx_ref[...] = dres_ref[...] + _norm_bwd(dhv, xhat, r, w_ref[...])
        _acc(dw_ref, jnp.sum(dhv * xhat, axis=0, keepdims=True))

    return _rows_call(name, body, xin.shape[0], 512, 1,
                      [_tile(dh, D_MODEL), _tile(xin, D_MODEL), _full(w), _tile(dres, D_MODEL)],
                      [_out_tile(D_MODEL, F32, D_MODEL), _out_acc(1, D_MODEL, D_MODEL)], plans)


def _postnorm_bwd(dout, z, w, w_mat, name):
    def body(do_ref, z_ref, w_ref, wm_ref, dz_ref, dm_ref, dw_ref):
        zv = z_ref[...]
        r = _rinv(zv)
        zhat = zv * r
        dov = do_ref[...]
        dz = _bf(_norm_bwd(dov, zhat, r, w_ref[...]))
        dz_ref[...] = dz
        dm_ref[...] = _dot(dz, wm_ref[...], NT)
        _acc(dw_ref, jnp.sum(dov * zhat, axis=0, keepdims=True))

    return _rows_call(name, body, z.shape[0], 512, 1,
                      [_tile(dout, D_MODEL), _tile(z, D_MODEL), _full(w), _full(w_mat)],
                      [_out_tile(D_MODEL, BF16, D_MODEL), _out_tile(D_MODEL, F32, D_MODEL),
                       _out_acc(1, D_MODEL, D_MODEL)])


def _t5_bucket(dist):
    n = jnp.maximum(dist, 0)
    nf = jnp.maximum(n, 1).astype(F32)
    large = MAX_EXACT + (jnp.log(nf / MAX_EXACT) / math.log(MAX_DISTANCE / MAX_EXACT)
                         * (NUM_BUCKETS - MAX_EXACT)).astype(jnp.int32)
    large = jnp.minimum(large, NUM_BUCKETS - 1)
    return jnp.where(n < MAX_EXACT, n, large)


def _band_rel():
    return jnp.arange(BLK)[:, None] + BLK - jnp.arange(2 * BLK)[None, :]


def _band_valid():
    rel = _band_rel()
    window = (rel >= 0) & (rel <= BLK)
    first = window & (jnp.arange(2 * BLK)[None, :] >= BLK)
    return jnp.stack([first, window]).astype(F32).reshape(2, 1, BAND)


RES_UNROLL = 4
PAIR = LANES // HEAD_DIM


def _pair_lanes():
    first = lax.broadcasted_iota(jnp.int32, (1, LANES), 1) < HEAD_DIM
    return first, jnp.logical_not(first)


def _heads_per_step(d):
    return HEADS if d == 1 else LANES // HEAD_DIM


def _sub_rows(r, d):
    return pl.ds(r, BLK, stride=d) if d > 1 else pl.ds(0, BLK)


def _for_residues(d, fn):
    if d <= RES_UNROLL:
        for r in range(d):
            fn(r)
    else:
        def group(i, carry):
            for k in range(RES_UNROLL):
                fn(i * RES_UNROLL + k)
            return carry

        lax.fori_loop(0, d // RES_UNROLL, group, 0)


def _attn_specs(d, g, qblock):
    cw = _heads_per_step(d) * HEAD_DIM

    def col(part, hp):
        return (g * 3 + part) * (GROUP_W // cw) + hp

    def cur(part):
        return pl.BlockSpec((d * BLK, cw), lambda hp, n: (qblock(n), col(part, hp)))

    def prev(part):
        return pl.BlockSpec((d * BLK, cw), lambda hp, n: (jnp.maximum(qblock(n) - 1, 0), col(part, hp)))

    return cur, prev


def _attn_fwd(proj, bias, g, name, plans=None):
    S = proj.shape[0]
    d = DILATIONS[g]
    NB = S // (d * BLK)
    hps = _heads_per_step(d)

    def body(q_ref, kp_ref, kc_ref, vp_ref, vc_ref, b_ref, o_ref, lse_ref):
        hp = pl.program_id(0)
        later = jnp.minimum(pl.program_id(1), 1)

        def residue(r):
            rows = _sub_rows(r, d)
            q2 = q_ref[rows, :]
            k2 = jnp.concatenate([kp_ref[rows, :], kc_ref[rows, :]], axis=0)
            v2 = jnp.concatenate([vp_ref[rows, :], vc_ref[rows, :]], axis=0)
            outs, lses = [], []
            for pp in range(hps // PAIR):
                ps = slice(pp * LANES, (pp + 1) * LANES)
                qp, kp, vp = _bf(q2[:, ps]), _bf(k2[:, ps]), _bf(v2[:, ps])
                o_h, lse_h = [], []
                for hh, own in enumerate(_pair_lanes()):
                    s = _dot(qp, jnp.where(own, kp, 0), NT) * (HEAD_DIM ** -0.5) + b_ref[later, hp * hps + pp * PAIR + hh]
                    m = jnp.max(s, axis=-1, keepdims=True)
                    p = jnp.exp(s - m)
                    l = jnp.sum(p, axis=-1, keepdims=True)
                    o_h.append(_dot(_bf(p), vp, NN) / l)
                    lse_h.append(m + jnp.log(l))
                first = _pair_lanes()[0]
                outs.append(jnp.where(first, o_h[0], o_h[1]))
                lses.append(jnp.where(first, lse_h[0], lse_h[1]))
            o_ref[rows, :] = outs[0] if len(outs) == 1 else jnp.concatenate(outs, axis=1)
            lse_ref[rows, :] = lses[0] if len(lses) == 1 else jnp.concatenate(lses, axis=1)

        _for_residues(d, residue)

    cur, prev = _attn_specs(d, g, lambda n: n)
    out = pl.BlockSpec((d * BLK, hps * HEAD_DIM), lambda hp, n: (n, hp))
    res, carried = _call(
        body, plans, name=name, grid=(HEADS // hps, NB),
        in_specs=[cur(0), prev(1), cur(1), prev(2), cur(2),
                  pl.BlockSpec((2, HEADS, BLK, 2 * BLK), lambda hp, n: (0, 0, 0, 0))],
        out_specs=[out, out], out_shape=[_sds((S, GROUP_W), F32)] * 2,
        args=(proj, proj, proj, proj, proj, bias))
    return res if plans is None else (res, carried)


def _attn_bwd(proj, bias, lse, y, dy, g, name, plans=None):
    S = proj.shape[0]
    d = DILATIONS[g]
    NB = S // (d * BLK)
    hps = _heads_per_step(d)

    def body(q_ref, kp_ref, kc_ref, vp_ref, vc_ref, b_ref, l_ref, y_ref, dy_ref,
             dq_ref, dk_ref, dv_ref, db_ref, ck_ref, cv_ref):
        hp, n = pl.program_id(0), pl.program_id(1)

        @pl.when((hp == 0) & (n == 0))
        def _():
            db_ref[...] = jnp.zeros_like(db_ref)

        @pl.when(n == 0)
        def _():
            ck_ref[...] = jnp.zeros_like(ck_ref)
            cv_ref[...] = jnp.zeros_like(cv_ref)

        @pl.when(n < NB)
        def _():
            later = jnp.minimum(n, 1)

            def residue(r):
                rows = _sub_rows(r, d)
                q2 = q_ref[rows, :]
                k2 = jnp.concatenate([kp_ref[rows, :], kc_ref[rows, :]], axis=0)
                v2 = jnp.concatenate([vp_ref[rows, :], vc_ref[rows, :]], axis=0)
                l2, y2, dy2 = l_ref[rows, :], y_ref[rows, :], dy_ref[rows, :]
                dqs, dks, dvs = [], [], []
                for pp in range(hps // PAIR):
                    ps = slice(pp * LANES, (pp + 1) * LANES)
                    qp, kp, vp = _bf(q2[:, ps]), _bf(k2[:, ps]), _bf(v2[:, ps])
                    dyp, yp = dy2[:, ps], y2[:, ps]
                    dq_h, dk_h, dv_h = [], [], []
                    for hh, own in enumerate(_pair_lanes()):
                        head = hp * hps + pp * PAIR + hh
                        s = _dot(qp, jnp.where(own, kp, 0), NT) * (HEAD_DIM ** -0.5) + b_ref[later, head]
                        p = jnp.exp(s - l2[:, pp * LANES + hh * HEAD_DIM:pp * LANES + hh * HEAD_DIM + 1])
                        dyh = jnp.where(own, dyp, 0.0)
                        delta = jnp.sum(dyh * yp, axis=-1, keepdims=True)
                        ds = p * (_dot(_bf(dyh), vp, NT) - delta)
                        db_ref[head] += ds
                        dsb = _bf(ds * (HEAD_DIM ** -0.5))
                        dq_h.append(_dot(dsb, kp, NN))
                        dk_h.append(_dot(dsb, qp, TN))
                        dv_h.append(_dot(_bf(p), _bf(dyp), TN))
                    first = _pair_lanes()[0]
                    dqs.append(jnp.where(first, dq_h[0], dq_h[1]))
                    dks.append(jnp.where(first, dk_h[0], dk_h[1]))
                    dvs.append(jnp.where(first, dv_h[0], dv_h[1]))
                dkb = dks[0] if len(dks) == 1 else jnp.concatenate(dks, axis=1)
                dvb = dvs[0] if len(dvs) == 1 else jnp.concatenate(dvs, axis=1)
                dq_ref[rows, :] = dqs[0] if len(dqs) == 1 else jnp.concatenate(dqs, axis=1)
                dk_ref[rows, :] = ck_ref[rows, :] + dkb[:BLK]
                dv_ref[rows, :] = cv_ref[rows, :] + dvb[:BLK]
                ck_ref[rows, :] = dkb[BLK:]
                cv_ref[rows, :] = dvb[BLK:]

            _for_residues(d, residue)

        @pl.when(n == NB)
        def _():
            dk_ref[...] = ck_ref[...]
            dv_ref[...] = cv_ref[...]

    def qn(n):
        return jnp.minimum(n, NB - 1)

    cur, prev = _attn_specs(d, g, qn)
    cw = hps * HEAD_DIM
    row = pl.BlockSpec((d * BLK, cw), lambda hp, n: (qn(n), hp))
    done = pl.BlockSpec((d * BLK, cw), lambda hp, n: (jnp.maximum(n - 1, 0), hp))
    (dq, dk, dv, db), carried = _call(
        body, plans, name=name, grid=(HEADS // hps, NB + 1),
        in_specs=[cur(0), prev(1), cur(1), prev(2), cur(2),
                  pl.BlockSpec((2, HEADS, BLK, 2 * BLK), lambda hp, n: (0, 0, 0, 0)), row, row, row],
        out_specs=[row, done, done, pl.BlockSpec((HEADS, BLK, 2 * BLK), lambda hp, n: (0, 0, 0))],
        out_shape=[_sds((S, GROUP_W), F32)] * 3 + [_sds((HEADS, BLK, 2 * BLK), F32)],
        scratch_shapes=[pltpu.VMEM((d * BLK, cw), F32)] * 2,
        args=(proj, proj, proj, proj, proj, bias, lse, y, dy))
    return ([dq, dk, dv], db) if plans is None else ([dq, dk, dv], db, carried)


BAND = BLK * 2 * BLK


def _bucket_onehot():
    buckets = jnp.stack([_t5_bucket(_band_rel() * d) for d in DILATIONS]).reshape(N_GROUPS, 1, BAND)
    return (buckets == jnp.arange(NUM_BUCKETS).reshape(1, NUM_BUCKETS, 1)).astype(F32)


def _relbias_fwd(rel_bias, name):
    table = rel_bias.reshape(NUM_BUCKETS, N_GROUPS, HEADS).transpose(1, 0, 2)

    def body(t_ref, oh_ref, valid_ref, o_ref):
        bias = lax.dot_general(t_ref[...], oh_ref[...], (TN, ((), ())), preferred_element_type=F32,
                               precision=lax.Precision.HIGHEST)
        for k in range(2):
            o_ref[k] = jnp.where(valid_ref[k] > 0.5, bias, NEG_INF)

    out = pl.pallas_call(
        body, name=name, grid=(N_GROUPS,),
        in_specs=[pl.BlockSpec((None, NUM_BUCKETS, HEADS), lambda g: (g, 0, 0)),
                  pl.BlockSpec((None, NUM_BUCKETS, BAND), lambda g: (g, 0, 0)),
                  pl.BlockSpec((2, 1, BAND), lambda g: (0, 0, 0))],
        out_specs=pl.BlockSpec((None, 2, HEADS, BAND), lambda g: (g, 0, 0, 0)),
        out_shape=_sds((N_GROUPS, 2, HEADS, BAND), F32), compiler_params=_cp(1))(table, _bucket_onehot(), _band_valid())
    return out.reshape(N_GROUPS, 2, HEADS, BLK, 2 * BLK)


def _relbias_bwd(dbs, name):
    band = BAND
    onehot = _bucket_onehot()
    dbf = jnp.stack([db.reshape(HEADS, band) for db in dbs])

    def body(oh_ref, db_ref, o_ref):
        o_ref[...] = lax.dot_general(oh_ref[...], db_ref[...], (NT, ((), ())), preferred_element_type=F32,
                                     precision=lax.Precision.HIGHEST)

    out = pl.pallas_call(
        body, name=name, grid=(N_GROUPS,),
        in_specs=[pl.BlockSpec((None, NUM_BUCKETS, band), lambda g: (g, 0, 0)),
                  pl.BlockSpec((None, HEADS, band), lambda g: (g, 0, 0))],
        out_specs=pl.BlockSpec((None, NUM_BUCKETS, HEADS), lambda g: (g, 0, 0)),
        out_shape=_sds((N_GROUPS, NUM_BUCKETS, HEADS), F32), compiler_params=_cp(1))(onehot, dbf)
    return out.transpose(1, 0, 2).reshape(NUM_BUCKETS, N_GROUPS * HEADS)


def _chunk_pos(shape):
    return lax.broadcasted_iota(jnp.int32, shape, 0) % HG_CHUNK


def _chunk_cumsum(v):
    pos = _chunk_pos(v.shape)
    s = 1
    while s < HG_CHUNK:
        v = v + jnp.where(pos >= s, pltpu.roll(v, s, 0), 0.0)
        s *= 2
    return v


def _chunk_rev_cumsum(v):
    pos = _chunk_pos(v.shape)
    n = v.shape[0]
    s = 1
    while s < HG_CHUNK:
        v = v + jnp.where(pos < HG_CHUNK - s, pltpu.roll(v, n - s, 0), 0.0)
        s *= 2
    return v


def _lower_bound(raw):
    a0, a1 = raw[0:1], raw[1:2]
    m = jnp.maximum(a0, a1)
    e0, e1 = jnp.exp(a0 - m), jnp.exp(a1 - m)
    return e0 / (e0 + e1)


def _hg_gates(qr, fr, lb):
    sf = _sigmoid(fr)
    f = lb + (1.0 - lb) * sf
    sq = _sigmoid(qr)
    return qr * sq, sq, f, sf


HG_COL0 = QKV_W // HG_W


def _hgrn_fwd(proj, lb_raw, nw, name):
    S = proj.shape[0]
    ncs = HG_TILE // HG_CHUNK
    tril = jnp.tril(jnp.ones((HG_CHUNK, HG_CHUNK), dtype=bool))

    def body(q_ref, f_ref, i_ref, og_ref, lb_ref, nw_ref, y_ref, o_ref, st_ref, state):
        @pl.when(pl.program_id(0) == 0)
        def _():
            state[...] = jnp.zeros_like(state)

        lb = _lower_bound(lb_ref[...])
        q, _, f, _ = _hg_gates(q_ref[...], f_ref[...], lb)
        k = 1.0 - f
        G = _chunk_cumsum(jnp.log(f))
        row = lax.broadcasted_iota(jnp.int32, (HG_CHUNK, HG_CHUNK), 0)
        col = lax.broadcasted_iota(jnp.int32, (HG_CHUNK, HG_CHUNK), 1)
        heads = [slice(h * HG_DK, (h + 1) * HG_DK) for h in range(HG_HEADS)]
        sts = [state[h] for h in range(HG_HEADS)]
        for c in range(ncs):
            cs = slice(c * HG_CHUNK, (c + 1) * HG_CHUNK)
            for h, hs in enumerate(heads):
                Gc = G[cs, hs]
                gl = Gc[HG_CHUNK - 1:HG_CHUNK]
                qt = _bf(q[cs, hs] * jnp.exp(Gc))
                kt = _bf(k[cs, hs] * jnp.exp(-Gc))
                kd = _bf(k[cs, hs] * jnp.exp(gl - Gc))
                v = _bf(i_ref[cs, hs])
                A = jnp.where(row >= col, _dot(qt, kt, NT), 0.0)
                o_ref[cs, hs] = _dot(_bf(A), v, NN) + _dot(qt, _bf(sts[h]), NT)
                st_ref[c, h] = sts[h]
                sts[h] = sts[h] * jnp.exp(gl) + _dot(v, kd, TN)
        for h, hs in enumerate(heads):
            state[h] = sts[h]
            oh = o_ref[:, hs]
            og = og_ref[:, hs]
            y_ref[:, hs] = oh * _rinv(oh) * nw_ref[...] * (og * _sigmoid(og))

    def colspec(j):
        return pl.BlockSpec((HG_TILE, HG_W), lambda i: (i, HG_COL0 + j))

    return pl.pallas_call(
        body, name=name, grid=(S // HG_TILE,),
        in_specs=[colspec(0), colspec(1), colspec(2), colspec(3),
                  pl.BlockSpec((2, HG_W), lambda i: (0, 0)), pl.BlockSpec((1, HG_DK), lambda i: (0, 0))],
        out_specs=[pl.BlockSpec((HG_TILE, HG_W), lambda i: (i, 0))] * 2
        + [pl.BlockSpec((ncs, HG_HEADS, HG_DK, HG_DK), lambda i: (i, 0, 0, 0))],
        out_shape=[_sds((S, HG_W), F32)] * 2 + [_sds((S // HG_CHUNK, HG_HEADS, HG_DK, HG_DK), F32)],
        scratch_shapes=[pltpu.VMEM((HG_HEADS, HG_DK, HG_DK), F32)],
        compiler_params=_cp(1))(proj, proj, proj, proj, lb_raw, nw)


def _hgrn_bwd(proj, lb_raw, nw, o, states, dy, d_attn, d_gates, name):
    S = proj.shape[0]
    ncs = HG_TILE // HG_CHUNK
    nt = S // HG_TILE
    n_a, n_g = len(d_attn), len(d_gates)
    own = [slice(QKV_W + j * HG_W, QKV_W + (j + 1) * HG_W) for j in range(4)]

    def body(q_ref, f_ref, i_ref, og_ref, lb_ref, nw_ref, o_ref, st_ref, dy_ref, *rest):
        attn_refs, gate_refs = rest[:n_a], rest[n_a:n_a + n_g]
        dp_ref, dlb_ref, dnw_ref, dstate, do_s, dG_s, dgl_s, dk_s, dlb_s = rest[n_a + n_g:]
        dq_ref, df_ref, di_ref, dog_ref = (dp_ref.at[:, cols] for cols in own)
        step = pl.program_id(0)
        for k, a_ref in enumerate(attn_refs):
            dp_ref[:, k * GROUP_W:(k + 1) * GROUP_W] = _bf(a_ref[...])
        for k, g_ref in enumerate(gate_refs):
            dp_ref[:, QKV_W + 4 * HG_W + k * D_MODEL:QKV_W + 4 * HG_W + (k + 1) * D_MODEL] = g_ref[...]

        @pl.when(step == 0)
        def _():
            dstate[...] = jnp.zeros_like(dstate)
            dlb_s[...] = jnp.zeros_like(dlb_s)
            dnw_ref[...] = jnp.zeros_like(dnw_ref)

        lb = _lower_bound(lb_ref[...])
        qr = q_ref[...]
        q, sq, f, sf = _hg_gates(qr, f_ref[...], lb)
        k = 1.0 - f
        G = _chunk_cumsum(jnp.log(f))
        nwv = nw_ref[...]
        row = lax.broadcasted_iota(jnp.int32, (HG_CHUNK, HG_CHUNK), 0)
        col = lax.broadcasted_iota(jnp.int32, (HG_CHUNK, HG_CHUNK), 1)
        for h in range(HG_HEADS):
            hs = slice(h * HG_DK, (h + 1) * HG_DK)
            oh = o_ref[:, hs]
            r = _rinv(oh)
            ohat = oh * r
            og = og_ref[:, hs]
            sg = _sigmoid(og)
            dyh = dy_ref[:, hs]
            don = dyh * (og * sg)
            dog_ref[:, hs] = _bf(dyh * (ohat * nwv) * (sg * (1.0 + og * (1.0 - sg))))
            dnw_ref[...] += jnp.sum(don * ohat, axis=0, keepdims=True)
            do_s[:, hs] = _norm_bwd(don, ohat, r, nwv)
        dsts = [dstate[h] for h in range(HG_HEADS)]
        for c in reversed(range(ncs)):
            cs = slice(c * HG_CHUNK, (c + 1) * HG_CHUNK)
            for h in range(HG_HEADS):
                hs = slice(h * HG_DK, (h + 1) * HG_DK)
                dst = dsts[h]
                Gc = G[cs, hs]
                gl = Gc[HG_CHUNK - 1:HG_CHUNK]
                eG, enG, edG, egl = jnp.exp(Gc), jnp.exp(-Gc), jnp.exp(gl - Gc), jnp.exp(gl)
                qt, kt, kd = q[cs, hs] * eG, k[cs, hs] * enG, k[cs, hs] * edG
                qtb, ktb, kdb = _bf(qt), _bf(kt), _bf(kd)
                v = _bf(i_ref[cs, hs])
                do = _bf(do_s[cs, hs])
                st = st_ref[c, h]
                dstb = _bf(dst)
                A = jnp.where(row >= col, _dot(qtb, ktb, NT), 0.0)
                dA = _bf(jnp.where(row >= col, _dot(do, v, NT), 0.0))
                di_ref[cs, hs] = _bf(_dot(_bf(A), do, TN) + _dot(kdb, dstb, NT))
                dqt = _dot(dA, ktb, NN) + _dot(do, _bf(st), NN)
                dkt = _dot(dA, qtb, TN)
                dkd = _dot(v, dstb, NN)
                dgl = egl * jnp.sum(st * dst, axis=0, keepdims=True) + jnp.sum(dkd * kd, axis=0, keepdims=True)
                dsts[h] = dst * egl + _dot(do, qtb, TN)
                dq_ref[cs, hs] = _bf(dqt * eG * (sq[cs, hs] * (1.0 + qr[cs, hs] * (1.0 - sq[cs, hs]))))
                dk_s[cs, hs] = dkt * enG + dkd * edG
                dG_s[cs, hs] = dqt * qt - dkt * kt - dkd * kd
                dgl_s[cs, hs] = jnp.broadcast_to(dgl, (HG_CHUNK, HG_DK))
        for h in range(HG_HEADS):
            dstate[h] = dsts[h]
        dg = _chunk_rev_cumsum(dG_s[...]) + dgl_s[...]
        dfv = dg / f - dk_s[...]
        df_ref[...] = _bf(dfv * (1.0 - lb) * sf * (1.0 - sf))
        dlb_s[...] += jnp.sum(dfv * (1.0 - sf), axis=0, keepdims=True)

        @pl.when(step == nt - 1)
        def _():
            t = dlb_s[...] * lb * (1.0 - lb)
            dlb_ref[...] = jnp.concatenate([t, -t], axis=0)

    def colspec(j):
        return pl.BlockSpec((HG_TILE, HG_W), lambda i: (nt - 1 - i, HG_COL0 + j))

    def rows(width):
        return pl.BlockSpec((HG_TILE, width), lambda i: (nt - 1 - i, 0))

    tile = rows(HG_W)
    return pl.pallas_call(
        body, name=name, grid=(nt,),
        in_specs=[colspec(0), colspec(1), colspec(2), colspec(3),
                  pl.BlockSpec((2, HG_W), lambda i: (0, 0)), pl.BlockSpec((1, HG_DK), lambda i: (0, 0)),
                  tile, pl.BlockSpec((ncs, HG_HEADS, HG_DK, HG_DK), lambda i: (nt - 1 - i, 0, 0, 0)), tile]
        + [rows(GROUP_W)] * n_a + [rows(D_MODEL)] * n_g,
        out_specs=[rows(IN_W), pl.BlockSpec((2, HG_W), lambda i: (0, 0)), pl.BlockSpec((1, HG_DK), lambda i: (0, 0))],
        out_shape=[_sds((S, IN_W), BF16), _sds((2, HG_W), F32), _sds((1, HG_DK), F32)],
        scratch_shapes=[pltpu.VMEM((HG_HEADS, HG_DK, HG_DK), F32)] + [pltpu.VMEM((HG_TILE, HG_W), F32)] * 4
        + [pltpu.VMEM((1, HG_W), F32)],
        compiler_params=_cp(1))(proj, proj, proj, proj, lb_raw, nw, o, states, dy, *d_attn, *d_gates)


GATE_COL0 = (QKV_W + 4 * HG_W) // GROUP_W
HALF_D = D_MODEL // 2


def _gate_tiles(proj):
    return [_tile(proj, HALF_D, functools.partial(lambda c, k: GATE_COL0 + k, k=k)) for k in range(4)]


def _gates(g_refs):
    s0 = _sigmoid(jnp.concatenate([g_refs[0][...], g_refs[1][...]], axis=1))
    s1 = _sigmoid(jnp.concatenate([g_refs[2][...], g_refs[3][...]], axis=1))
    return s0, s1


def _branch_fwd(os_, lses, yh, proj, w_a, w_h, name):
    nb = w_a.shape[0]

    def body(o0, o1, o2, l0, l1, l2, yh_ref, g0a, g0b, g1a, g1b, wa_ref, wh_ref,
             y_ref, lse_ref, za_ref, zh_ref, m_ref):
        a, b, c = l0[...], l1[...], l2[...]
        m = jnp.maximum(jnp.maximum(a, b), c)
        ea, eb, ec = jnp.exp(a - m), jnp.exp(b - m), jnp.exp(c - m)
        den = ea + eb + ec
        y = (ea * o0[...] + eb * o1[...] + ec * o2[...]) / den
        y_ref[...] = y
        lse_ref[...] = m + jnp.log(den)
        yb, yhb = _bf(y), _bf(yh_ref[...])
        za = jnp.concatenate([_dot(yb, wa_ref[j], NN) for j in range(nb)], axis=1)
        zh = jnp.concatenate([_dot(yhb, wh_ref[j], NN) for j in range(nb)], axis=1)
        s0, s1 = _gates((g0a, g0b, g1a, g1b))
        za_ref[...] = za
        zh_ref[...] = zh
        m_ref[...] = _bf(s0 * za + s1 * zh)

    return _rows_call(name, body, yh.shape[0], 512, 1,
                      [*[_tile(t, GROUP_W) for t in (*os_, *lses)], _tile(yh, HG_W), *_gate_tiles(proj),
                       _full(w_a), _full(w_h)],
                      [_out_tile(GROUP_W, F32, GROUP_W)] * 2 + [_out_tile(D_MODEL, F32, D_MODEL)] * 2
                      + [_out_tile(D_MODEL, BF16, D_MODEL)])


def _branch_bwd(dm, za, zh, proj, w_a, w_h, name):
    nb, _, Nb = w_a.shape

    def body(dm_ref, za_ref, zh_ref, g0a, g0b, g1a, g1b, wa_ref, wh_ref,
             dza_ref, dzh_ref, dg0_ref, dg1_ref, dy_ref, dyh_ref):
        dmv = dm_ref[...]
        s0, s1 = _gates((g0a, g0b, g1a, g1b))
        dza, dzh = _bf(dmv * s0), _bf(dmv * s1)
        dza_ref[...] = dza
        dzh_ref[...] = dzh
        dg0_ref[...] = _bf(dmv * za_ref[...] * s0 * (1.0 - s0))
        dg1_ref[...] = _bf(dmv * zh_ref[...] * s1 * (1.0 - s1))
        dy_ref[...] = sum(_dot(dza[:, j * Nb:(j + 1) * Nb], wa_ref[j], NT) for j in range(nb))
        dyh_ref[...] = sum(_dot(dzh[:, j * Nb:(j + 1) * Nb], wh_ref[j], NT) for j in range(nb))

    return _rows_call(name, body, za.shape[0], 512, 1,
                      [_tile(dm, D_MODEL), _tile(za, D_MODEL), _tile(zh, D_MODEL), *_gate_tiles(proj),
                       _full(w_a), _full(w_h)],
                      [_out_tile(D_MODEL, BF16, D_MODEL)] * 4 + [_out_tile(GROUP_W, F32, GROUP_W),
                                                                 _out_tile(HG_W, F32, HG_W)])


def _mix_out(merged, w_out, x, w_post, w_pre, name):
    def body(m_ref, wo_ref, x_ref, wp_ref, wf_ref, mo_ref, x1_ref, h2_ref):
        z = _dot(m_ref[...], wo_ref[...], NN)
        mo_ref[...] = z
        x1 = x_ref[...] + z * _rinv(z) * wp_ref[...]
        x1_ref[...] = x1
        h2_ref[...] = _bf(x1 * _rinv(x1) * wf_ref[...])

    return _rows_call(name, body, x.shape[0], 512, 1,
                      [_tile(merged, D_MODEL), _full(w_out), _tile(x, D_MODEL), _full(w_post), _full(w_pre)],
                      [_out_tile(D_MODEL, F32, D_MODEL), _out_tile(D_MODEL, F32, D_MODEL),
                       _out_tile(D_MODEL, BF16, D_MODEL)])


def _loss_head(a, w_down, x1, tgt, w, name):
    def body(a_ref, wd_ref, x1_ref, t_ref, w_ref, dx_ref, df_ref, dw_ref, loss_ref):
        z = _dot(a_ref[...], wd_ref[...], NN)
        r = _rinv(z)
        zhat = z * r
        wv = w_ref[...]
        e = x1_ref[...] + zhat * wv - t_ref[...]
        dx = e * (1.0 / D_MODEL)
        dx_ref[...] = dx
        df_ref[...] = _bf(_norm_bwd(dx, zhat, r, wv))
        _acc(dw_ref, jnp.sum(dx * zhat, axis=0, keepdims=True))
        part = 0.5 * jnp.sum(jnp.sum(e * e, axis=1, keepdims=True), axis=0, keepdims=True) * (1.0 / D_MODEL)
        _acc(loss_ref, jnp.broadcast_to(part, (1, LANES)))

    return _rows_call(name, body, x1.shape[0], 512, 1,
                      [_tile(a, D_FF), _full(w_down), _tile(x1, D_MODEL), _tile(tgt, D_MODEL), _full(w)],
                      [_out_tile(D_MODEL, F32, D_MODEL), _out_tile(D_MODEL, BF16, D_MODEL),
                       _out_acc(1, D_MODEL, D_MODEL), _out_acc(1, LANES, LANES)])


CONV_CB = D_FF // 2
CONV_TM = 512
HALO = 8
SQRT_HALF = 0.7071067811865476
INV_SQRT_2PI = 0.3989422804014327


CONV_RS = 32


def _lane_tiles():
    return [slice(k * LANES, (k + 1) * LANES) for k in range(CONV_CB // LANES)]


def _strip_start(i):
    return pl.multiple_of(i * CONV_RS, CONV_RS)


def _strip_taps(u_ref, halo_ref, r0, cs, first_strip, first_tile):
    if first_strip:
        before = jnp.where(first_tile, 0.0, halo_ref[:, cs])
        blk = jnp.concatenate([before, u_ref[0:CONV_RS, cs]], axis=0)
    else:
        blk = u_ref[pl.ds(pl.multiple_of(r0 - HALO, HALO), CONV_RS + HALO), cs]
    return pltpu.roll(blk, 2, 0)[HALO:], pltpu.roll(blk, 1, 0)[HALO:], blk[HALO:]


def _conv(taps, w_ref, b_ref, cs):
    return b_ref[:, cs] + w_ref[0:1, cs] * taps[0] + w_ref[1:2, cs] * taps[1] + w_ref[2:3, cs] * taps[2]


def _conv_specs(tm):
    nh = tm // HALO
    nc = D_FF // CONV_CB

    def tile(off):
        return pl.BlockSpec((tm, CONV_CB), lambda c, i: (i, off + c))

    def halo(off):
        return pl.BlockSpec((HALO, CONV_CB), lambda c, i: (jnp.maximum(i * nh - 1, 0), off + c))

    def small(rows, off):
        return pl.BlockSpec((rows, CONV_CB), lambda c, i: (0, off + c))

    return nc, tile, halo, small


def _conv_gelu_fwd(u, cw, cb, name):
    S = u.shape[0]
    tm = CONV_TM
    nc, tile, halo, small = _conv_specs(tm)

    def body(ug, hg, uv, hv, wg, wv, bg, bv, a_ref):
        first_tile = pl.program_id(1) == 0

        def strip(r0, first_strip):
            for cs in _lane_tiles():
                cg = _conv(_strip_taps(ug, hg, r0, cs, first_strip, first_tile), wg, bg, cs)
                cv = _conv(_strip_taps(uv, hv, r0, cs, first_strip, first_tile), wv, bv, cs)
                a_ref[pl.ds(r0, CONV_RS), cs] = _bf(0.5 * cg * (1.0 + lax.erf(cg * SQRT_HALF)) * cv)

        strip(0, True)
        lax.fori_loop(1, tm // CONV_RS, lambda k, c: (strip(_strip_start(k), False), c)[1], 0)

    return pl.pallas_call(
        body, name=name, grid=(nc, S // tm),
        in_specs=[tile(0), halo(0), tile(nc), halo(nc), small(3, 0), small(3, nc), small(1, 0), small(1, nc)],
        out_specs=tile(0), out_shape=_sds((S, D_FF), BF16), compiler_params=_cp(2))(u, u, u, u, cw, cw, cb, cb)


def _conv_gelu_bwd(u, dff, w_down, cw, cb, name, plans=None):
    S = u.shape[0]
    tm = CONV_TM
    nt = S // tm
    nc, tile, halo, small = _conv_specs(tm)

    def body(ug, hg, uv, hv, wg, wv, bg, bv, dff_ref, wd_ref, dcg_ref, dcv_ref, dwg_ref, dwv_ref, dbg_ref, dbv_ref,
             acc, da_ref):
        i = pl.program_id(1)
        first_tile = i == 0
        da_ref[...] = _dot(dff_ref[...], wd_ref[...], NT)

        @pl.when(first_tile)
        def _():
            acc[...] = jnp.zeros_like(acc)

        def strip(r0, first_strip):
            rows = pl.ds(r0, CONV_RS)
            for cs in _lane_tiles():
                tg = _strip_taps(ug, hg, r0, cs, first_strip, first_tile)
                tv = _strip_taps(uv, hv, r0, cs, first_strip, first_tile)
                cg = _conv(tg, wg, bg, cs)
                cv = _conv(tv, wv, bv, cs)
                phi = 0.5 * (1.0 + lax.erf(cg * SQRT_HALF))
                dav = da_ref[rows, cs]
                dcg = dav * cv * (phi + cg * jnp.exp(-0.5 * cg * cg) * INV_SQRT_2PI)
                dcv = dav * (cg * phi)
                dcg_ref[rows, cs] = dcg
                dcv_ref[rows, cs] = dcv
                for half, (dc, taps) in enumerate(((dcg, tg), (dcv, tv))):
                    for j in range(3):
                        acc[4 * half + j, :, cs] += dc * taps[j]
                    acc[4 * half + 3, :, cs] += dc

        strip(0, True)
        lax.fori_loop(1, tm // CONV_RS, lambda k, c: (strip(_strip_start(k), False), c)[1], 0)

        @pl.when(i == nt - 1)
        def _():
            for half, (dw_ref, db_ref) in enumerate(((dwg_ref, dbg_ref), (dwv_ref, dbv_ref))):
                for j in range(3):
                    dw_ref[j:j + 1, :] = jnp.sum(acc[4 * half + j], axis=0, keepdims=True)
                db_ref[...] = jnp.sum(acc[4 * half + 3], axis=0, keepdims=True)

    res, carried = _call(
        body, plans, name=name, grid=(nc, nt),
        in_specs=[tile(0), halo(0), tile(nc), halo(nc), small(3, 0), small(3, nc), small(1, 0), small(1, nc),
                  pl.BlockSpec((tm, D_MODEL), lambda c, i: (i, 0)), pl.BlockSpec((CONV_CB, D_MODEL), lambda c, i: (c, 0))],
        out_specs=[tile(0), tile(0), small(3, 0), small(3, 0), small(1, 0), small(1, 0)],
        out_shape=[_sds((S, D_FF), F32)] * 2 + [_sds((3, D_FF), F32)] * 2 + [_sds((1, D_FF), F32)] * 2,
        scratch_shapes=[pltpu.VMEM((8, CONV_RS, CONV_CB), F32), pltpu.VMEM((tm, CONV_CB), F32)],
        args=(u, u, u, u, cw, cw, cb, cb, dff, w_down))
    return res if plans is None else (res, carried)


def _conv_input_bwd(dcg, dcv, cw, name, plans=None):
    S = dcg.shape[0]
    tm = CONV_TM // 2
    nh = tm // HALO
    nt = S // tm
    n = CONV_RS + HALO
    tile = pl.BlockSpec((tm, D_FF), lambda i: (i, 0))
    nxt = pl.BlockSpec((HALO, D_FF), lambda i: (jnp.minimum((i + 1) * nh, S // HALO - 1), 0))

    def body(g_ref, ng_ref, v_ref, nv_ref, w_ref, du_ref):
        last_tile = pl.program_id(0) == nt - 1

        def strip(r0, last_strip):
            for half, (dc_ref, n_ref) in enumerate(((g_ref, ng_ref), (v_ref, nv_ref))):
                for k in range(D_FF // LANES):
                    cs = slice(k * LANES, (k + 1) * LANES)
                    ws = slice(half * D_FF + k * LANES, half * D_FF + (k + 1) * LANES)
                    if last_strip:
                        after = jnp.where(last_tile, 0.0, n_ref[:, cs])
                        blk = jnp.concatenate([dc_ref[tm - CONV_RS:tm, cs], after], axis=0)
                    else:
                        blk = dc_ref[pl.ds(r0, n), cs]
                    d1 = pltpu.roll(blk, n - 1, 0)[:CONV_RS]
                    d2 = pltpu.roll(blk, n - 2, 0)[:CONV_RS]
                    du_ref[pl.ds(r0, CONV_RS), ws] = _bf(w_ref[2:3, ws] * blk[:CONV_RS] + w_ref[1:2, ws] * d1
                                                         + w_ref[0:1, ws] * d2)

        lax.fori_loop(0, tm // CONV_RS - 1, lambda k, c: (strip(_strip_start(k), False), c)[1], 0)
        strip(tm - CONV_RS, True)

    (du,), carried = _call(
        body, plans, name=name, grid=(nt,),
        in_specs=[tile, nxt, tile, nxt, pl.BlockSpec((3, 2 * D_FF), lambda i: (0, 0))],
        out_specs=[pl.BlockSpec((tm, 2 * D_FF), lambda i: (i, 0))], out_shape=[_sds((S, 2 * D_FF), BF16)],
        args=(dcg, dcg, dcv, dcv, cw))
    return du if plans is None else (du, carried)


def _row_tile(n, cap):
    best = n
    for t in range(16, cap + 1, 16):
        if n % t == 0:
            best = t
    return best if best <= cap else n


def _rows_for_bytes(nbytes, cols):
    return max(16, nbytes // (4 * cols) // 16 * 16)


def _adamw(w, g, m, v, name):
    R, C = w.shape
    tr = _row_tile(R, _rows_for_bytes(2 << 20, C))

    def body(w_ref, g_ref, m_ref, v_ref, d_ref, nm_ref, nv_ref):
        gv = g_ref[...]
        nm = ADAM_B1 * m_ref[...] + (1.0 - ADAM_B1) * gv
        nv = ADAM_B2 * v_ref[...] + (1.0 - ADAM_B2) * (gv * gv)
        m_hat = nm / (1.0 - ADAM_B1 ** ADAM_STEP)
        v_hat = nv / (1.0 - ADAM_B2 ** ADAM_STEP)
        d_ref[...] = -ADAM_LR * (m_hat / (jnp.sqrt(v_hat) + ADAM_EPS) + ADAM_WD * w_ref[...])
        nm_ref[...] = nm
        nv_ref[...] = nv

    spec = pl.BlockSpec((tr, C), lambda i: (i, 0))
    return pl.pallas_call(body, name=name, grid=(R // tr,), in_specs=[spec] * 4, out_specs=[spec] * 3,
                          out_shape=[_sds((R, C), F32)] * 3, compiler_params=_cp(1))(w, g, m, v)


def _pair_sum(gfull, rcv, c_idx, name):
    nb, R, C = gfull.shape
    half = R // 2
    tr = _row_tile(half, _rows_for_bytes(2 << 20, C))
    nt = half // tr

    def body(c_ref, g_ref, r_ref, o_ref):
        o_ref[...] = _bf(g_ref[...] + r_ref[...])

    return pl.pallas_call(
        body, name=name,
        grid_spec=pltpu.PrefetchScalarGridSpec(
            num_scalar_prefetch=1, grid=(nb, nt),
            in_specs=[pl.BlockSpec((None, tr, C), lambda j, i, c_ref: (j, c_ref[0] * nt + i, 0)),
                      pl.BlockSpec((None, tr, C), lambda j, i, c_ref: (j, i, 0))],
            out_specs=pl.BlockSpec((None, tr, C), lambda j, i, c_ref: (j, i, 0))),
        out_shape=_sds((nb, half, C), BF16), compiler_params=_cp(2))(c_idx, gfull, rcv)


def _chip_sum(arrived, own, place, name):
    nb, H, C = arrived.shape
    tr = _row_tile(H, _rows_for_bytes(2 << 20, C))
    nt = H // tr

    def body(pl_ref, *refs):
        o_ref = refs[nb + 1]
        me = pl_ref[0]
        acc = None
        for k in range(nb):
            term = jnp.where(me == k, refs[nb][...], refs[k][...]).astype(F32)
            acc = term if acc is None else acc + term
        o_ref[...] = acc

    def other(k):
        return pl.BlockSpec((None, tr, C), lambda i, p: (jnp.where(p[0] == k, (k + 1) % nb, k), i, 0))

    return pl.pallas_call(
        body, name=name,
        grid_spec=pltpu.PrefetchScalarGridSpec(
            num_scalar_prefetch=1, grid=(nt,),
            in_specs=[other(k) for k in range(nb)] + [pl.BlockSpec((None, tr, C), lambda i, p: (p[0], i, 0))],
            out_specs=pl.BlockSpec((tr, C), lambda i, p: (p[1] * nt + i, 0))),
        out_shape=_sds((2 * H, C), F32), compiler_params=_cp(1))(place, *([arrived] * nb), own)


def _cast_into_slot(shard, place, name):
    R, C = shard.shape
    tr = _row_tile(R, 256)

    def body(pl_ref, s_ref, o_ref):
        o_ref[...] = _bf(s_ref[...])

    return pl.pallas_call(
        body, name=name,
        grid_spec=pltpu.PrefetchScalarGridSpec(
            num_scalar_prefetch=1, grid=(R // tr,),
            in_specs=[pl.BlockSpec((tr, C), lambda i, p: (i, 0))],
            out_specs=pl.BlockSpec((None, tr, C), lambda i, p: (p[0], i, 0))),
        out_shape=_sds((N_CHIPS, R, C), BF16), compiler_params=_cp(1))(place, shard)


def _place():
    x, y, c = lax.axis_index("x"), lax.axis_index("y"), lax.axis_index("c")
    chips = [(1 - x, y), (x, 1 - y), (1 - x, 1 - y)]
    return x, y, c, chips


def _chip_id(px, py):
    return 2 * px + py


def _remote(src, dst, send_sems, recv_sems, k, to):
    return pltpu.make_async_remote_copy(src_ref=src, dst_ref=dst, send_sem=send_sems.at[k], recv_sem=recv_sems.at[k],
                                        device_id=to, device_id_type=MESH)


def _gather_weights(slots, wholes, name):
    ns, nw = len(slots), len(wholes)
    n = ns + nw

    def body(*refs):
        ins = refs[ns:n]
        outs = refs[n:2 * n]
        send_sems, recv_sems, local_sems = refs[2 * n:]
        x, y, c, chips = _place()
        me = _chip_id(x, y)
        sib = (x, y, 1 - c)
        local = [pltpu.make_async_copy(ins[b], outs[ns + b].at[me], local_sems.at[b]) for b in range(nw)]
        for cp in local:
            cp.start()
        sent = []
        for a in range(n):
            R = outs[a].shape[1]
            rows = pl.ds(c * (R // 2), R // 2) if a < ns else pl.ds(0, R)
            src = outs[a].at[me, rows] if a < ns else ins[a - ns]
            for j, chip in enumerate(chips):
                cp = _remote(src, outs[a].at[me, rows], send_sems, recv_sems, 6 * a + j, (*chip, c))
                cp.start()
                sent.append(cp)
        for a in range(n):
            R = outs[a].shape[1]
            rows = pl.ds(c * (R // 2), R // 2) if a < ns else pl.ds(0, R)
            for j, chip in enumerate(chips):
                landed = outs[a].at[_chip_id(*chip), rows]
                _remote(landed, landed, send_sems, recv_sems, 6 * a + j, (*chip, c)).wait_recv()
                if a < ns:
                    cp = _remote(landed, landed, send_sems, recv_sems, 6 * a + 3 + j, sib)
                    cp.start()
                    sent.append(cp)
        for a in range(ns):
            R = outs[a].shape[1]
            other = pl.ds((1 - c) * (R // 2), R // 2)
            for j, chip in enumerate(chips):
                passed = outs[a].at[_chip_id(*chip), other]
                _remote(passed, passed, send_sems, recv_sems, 6 * a + 3 + j, sib).wait_recv()
        for cp in sent:
            cp.wait_send()
        for cp in local:
            cp.wait()

    return pl.pallas_call(
        body, name=name, in_specs=[ANY] * n, out_specs=[ANY] * n,
        out_shape=[_sds(s.shape, s.dtype) for s in slots] + [_sds((N_CHIPS, *s.shape), s.dtype) for s in wholes],
        input_output_aliases={a: a for a in range(ns)},
        scratch_shapes=[pltpu.SemaphoreType.DMA((6 * n,)), pltpu.SemaphoreType.DMA((6 * n,)),
                        pltpu.SemaphoreType.DMA((max(nw, 1),))])(*slots, *wholes)


def _gather_ici_plan(slots, wholes):
    ns, nw = len(slots), len(wholes)

    def copies(ins, ios, outs, send_sems, recv_sems, local_sems):
        x, y, c, chips = _place()
        me = _chip_id(x, y)
        sends, recvs = [], []
        for a in range(ns + nw):
            dst = ios[a] if a < ns else outs[a - ns]
            R = dst.shape[1]
            rows = pl.ds(c * (R // 2), R // 2) if a < ns else pl.ds(0, R)
            src = dst.at[me, rows] if a < ns else ins[a - ns]
            for j, chip in enumerate(chips):
                sends.append(_remote(src, dst.at[me, rows], send_sems, recv_sems, 3 * a + j, (*chip, c)))
                landed = dst.at[_chip_id(*chip), rows]
                recvs.append(_remote(landed, landed, send_sems, recv_sems, 3 * a + j, (*chip, c)))
        local = [pltpu.make_async_copy(ins[b], outs[b].at[me], local_sems.at[b]) for b in range(nw)]
        return sends, recvs, local

    return _Plan(copies, 3 * (ns + nw), ins=wholes, inouts=slots,
                 outs=[_sds((N_CHIPS, *s.shape), s.dtype) for s in wholes])


def _gather_pass_plan(slots):
    def copies(ins, ios, outs, send_sems, recv_sems, local_sems):
        x, y, c, chips = _place()
        sib = (x, y, 1 - c)
        sends, recvs = [], []
        for a, buf in enumerate(ios):
            half = buf.shape[1] // 2
            for j, chip in enumerate(chips):
                mine = buf.at[_chip_id(*chip), pl.ds(c * half, half)]
                other = buf.at[_chip_id(*chip), pl.ds((1 - c) * half, half)]
                sends.append(_remote(mine, mine, send_sems, recv_sems, 3 * a + j, sib))
                recvs.append(_remote(other, other, send_sems, recv_sems, 3 * a + j, sib))
        return sends, recvs, []

    return _Plan(copies, 3 * len(slots), inouts=slots)


def _pair_plan(grads):
    def copies(ins, ios, outs, send_sems, recv_sems, local_sems):
        x, y, c, _ = _place()
        sib = (x, y, 1 - c)
        sends, recvs = [], []
        for a, g in enumerate(ins):
            half = g.shape[1] // 2
            sends.append(_remote(g.at[:, pl.ds((1 - c) * half, half), :], outs[a], send_sems, recv_sems, a, sib))
            recvs.append(_remote(outs[a], outs[a], send_sems, recv_sems, a, sib))
        return sends, recvs, []

    return _Plan(copies, len(grads), ins=grads,
                 outs=[_sds((g.shape[0], g.shape[1] // 2, g.shape[2]), g.dtype) for g in grads])


def _chip_plan(parts):
    def copies(ins, ios, outs, send_sems, recv_sems, local_sems):
        x, y, c, chips = _place()
        me = _chip_id(x, y)
        sends, recvs = [], []
        for a, part in enumerate(ins):
            for j, chip in enumerate(chips):
                sends.append(_remote(part.at[_chip_id(*chip)], outs[a].at[me], send_sems, recv_sems, 3 * a + j, (*chip, c)))
                landed = outs[a].at[_chip_id(*chip)]
                recvs.append(_remote(landed, landed, send_sems, recv_sems, 3 * a + j, (*chip, c)))
        return sends, recvs, []

    return _Plan(copies, 3 * len(parts), ins=parts, outs=[_sds(p.shape, p.dtype) for p in parts])


def _pair_concat(fulls, name):
    n = len(fulls)

    def body(*refs):
        outs = refs[n:2 * n]
        send_sems, recv_sems = refs[2 * n:]
        x, y, c, _ = _place()
        cps = []
        for a in range(n):
            H = outs[a].shape[0] // 2
            mine = outs[a].at[pl.ds(c * H, H)]
            cp = _remote(mine, mine, send_sems, recv_sems, a, (x, y, 1 - c))
            cp.start()
            cps.append(cp)
        for a, cp in enumerate(cps):
            H = outs[a].shape[0] // 2
            other = outs[a].at[pl.ds((1 - c) * H, H)]
            _remote(other, other, send_sems, recv_sems, a, (x, y, 1 - c)).wait_recv()
            cp.wait_send()

    return pl.pallas_call(
        body, name=name, in_specs=[ANY] * n, out_specs=[ANY] * n,
        out_shape=[_sds(f.shape, f.dtype) for f in fulls], input_output_aliases={a: a for a in range(n)},
        scratch_shapes=[pltpu.SemaphoreType.DMA((n,)), pltpu.SemaphoreType.DMA((n,))])(*fulls)


def _all_sum(pack, name):
    R, C = pack.shape

    def body(p_ref, o_ref, buf, send_sems, recv_sems):
        x, y, c, _ = _place()
        me = 4 * x + 2 * y + c
        buf[me] = p_ref[...]
        cps = []
        for k in range(1, N_DEV):
            to = (x ^ (k >> 2), y ^ ((k >> 1) & 1), c ^ (k & 1))
            cp = _remote(p_ref, buf.at[me], send_sems, recv_sems, k - 1, to)
            cp.start()
            cps.append(cp)
        for k in range(1, N_DEV):
            frm = (x ^ (k >> 2), y ^ ((k >> 1) & 1), c ^ (k & 1))
            slot = buf.at[4 * frm[0] + 2 * frm[1] + frm[2]]
            _remote(slot, slot, send_sems, recv_sems, k - 1, frm).wait_recv()
        acc = buf[0]
        for k in range(1, N_DEV):
            acc = acc + buf[k]
        o_ref[...] = acc
        for cp in cps:
            cp.wait_send()

    vm = pl.BlockSpec(memory_space=pltpu.VMEM)
    return pl.pallas_call(
        body, name=name, in_specs=[vm], out_specs=vm, out_shape=_sds((R, C), F32),
        scratch_shapes=[pltpu.VMEM((N_DEV, R, C), F32), pltpu.SemaphoreType.DMA((N_DEV - 1,)),
                        pltpu.SemaphoreType.DMA((N_DEV - 1,))])(pack)


def _local_step(xs, tgt, p, ex):
    proj, h1, got = _norm_proj(xs, p["pre_mix_norm"], ex.weight("w_in"), "proj_in", plans=ex.carry("proj_in"))
    ex.done("proj_in", got)
    biases = _relbias_fwd(p["rel_bias"], "rel_bias_fwd")
    fw = []
    for g in range(N_GROUPS):
        res, got = _attn_fwd(proj, biases[g], g, f"attn_fwd{g}", plans=ex.carry(f"attn_fwd{g}"))
        ex.done(f"attn_fwd{g}", got)
        fw.append(res)
    yh, o_h, states = _hgrn_fwd(proj, p["hgrn_lb_raw"], p["hgrn_norm"], "hgrn_fwd")
    W_a, W_h, W_out = ex.weight("w_branch_attn"), ex.weight("w_branch_hgrn"), ex.weight("w_out")
    W_up, W_down, conv_w = ex.weight("w_up"), ex.weight("w_down"), ex.weight("conv_w")
    y, lse, za, zh, merged = _branch_fwd([t[0] for t in fw], [t[1] for t in fw], yh, proj, W_a, W_h, "branch_fwd")
    mo, x1, h2 = _mix_out(merged, W_out, xs, p["post_mix_norm"], p["pre_ffn_norm"], "mix_out")
    u = _mm_nn_blk(h2, W_up, "ffn_up")
    a = _conv_gelu_fwd(u, conv_w, p["conv_b"], "conv_gelu_fwd")
    dx2, dff, g_post_ffn, loss = _loss_head(a, W_down, x1, tgt, p["post_ffn_norm"], "ffn_down_loss")

    ex.grad("w_down", _mm_tn(a, dff, "g_w_down").reshape(N_CHIPS, D_FF // N_CHIPS, D_MODEL))
    (dcg, dcv, gwg, gwv, gbg, gbv), got = _conv_gelu_bwd(u, dff, W_down, conv_w, p["conv_b"], "conv_gelu_bwd",
                                                          plans=ex.carry("conv_gelu_bwd"))
    ex.done("conv_gelu_bwd", got)
    g_conv_w = jnp.concatenate([gwg, gwv], axis=1)
    g_conv_b = jnp.concatenate([gbg, gbv], axis=1)
    du, got = _conv_input_bwd(dcg, dcv, conv_w, "conv_input_bwd", plans=ex.carry("conv_input_bwd"))
    ex.done("conv_input_bwd", got)
    dh2 = _mm_nt_blk(du, W_up, "d_ffn_in")
    ex.grad("w_up", _mm_tn_blk(h2, du, N_CHIPS, "g_w_up"))
    (dx1, g_pre_ffn), got = _prenorm_bwd(dh2, x1, p["pre_ffn_norm"], dx2, "pre_ffn_norm_bwd",
                                         plans=ex.carry("pre_ffn_norm_bwd"))
    ex.done("pre_ffn_norm_bwd", got)
    dmo, dmerged, g_post_mix = _postnorm_bwd(dx1, mo, p["post_mix_norm"], W_out, "post_mix_norm_bwd")
    ex.grad("w_out", _mm_tn(merged, dmo, "g_w_out").reshape(N_CHIPS, D_MODEL // N_CHIPS, D_MODEL))
    dza, dzh, dg0, dg1, dy, dyh = _branch_bwd(dmerged, za, zh, proj, W_a, W_h, "branch_bwd")
    ex.grad("w_branch_attn", _mm_tn_blk(y, dza, N_CHIPS, "g_w_branch_attn", together=True))
    ex.grad("w_branch_hgrn", _mm_tn_blk(yh, dzh, N_CHIPS, "g_w_branch_hgrn", together=True))
    dqkv, dbs = [], []
    for g in range(N_GROUPS):
        parts, db, got = _attn_bwd(proj, biases[g], lse, y, dy, g, f"attn_bwd{g}", plans=ex.carry(f"attn_bwd{g}"))
        ex.done(f"attn_bwd{g}", got)
        dqkv += parts
        dbs.append(db)
    g_rel_bias = _relbias_bwd(dbs, "rel_bias_bwd")
    dproj, g_lb_raw, g_hgrn_norm = _hgrn_bwd(proj, p["hgrn_lb_raw"], p["hgrn_norm"], o_h, states, dyh, dqkv,
                                             [dg0, dg1], "hgrn_bwd")
    for piece in W_IN_PIECES:
        g, got = _mm_tn_blk(h1, dproj, N_CHIPS, f"g_{piece}", x_cols=W_IN_ROWS[piece],
                            plans=ex.carry(f"g_{piece}"))
        ex.done(f"g_{piece}", got)
        ex.grad(piece, g)
    dh1, got = _mm_nt_blk(dproj, ex.weight("w_in"), "d_proj_in", plans=ex.carry("d_proj_in"))
    ex.done("d_proj_in", got)
    (grad_x, g_pre_mix), got = _prenorm_bwd(dh1, xs, p["pre_mix_norm"], dx1, "pre_mix_norm_bwd",
                                            plans=ex.carry("pre_mix_norm_bwd"))
    ex.done("pre_mix_norm_bwd", got)
    small = dict(pre_mix_norm=g_pre_mix, rel_bias=g_rel_bias, hgrn_lb_raw=g_lb_raw, hgrn_norm=g_hgrn_norm,
                 post_mix_norm=g_post_mix, pre_ffn_norm=g_pre_ffn, conv_w=g_conv_w, conv_b=g_conv_b,
                 post_ffn_norm=g_post_ffn)
    return loss, grad_x, small


SMALL = ("pre_mix_norm", "rel_bias", "hgrn_lb_raw", "hgrn_norm", "post_mix_norm", "pre_ffn_norm", "conv_w", "conv_b",
         "post_ffn_norm")
BIG = ("w_in", "w_up", "w_down", "w_out", "w_branch_attn", "w_branch_hgrn")
WEIGHTS = ("pre_mix_norm", "w_in", "rel_bias", "hgrn_lb_raw", "hgrn_norm", "w_branch_attn", "w_branch_hgrn", "w_out",
           "post_mix_norm", "pre_ffn_norm", "w_up", "conv_w", "conv_b", "w_down", "post_ffn_norm")
MIXER = ("w_out", "w_branch_attn", "w_branch_hgrn")

SCHEDULE = {
    "proj_in": [("gather_ici_cw", ("w_up",) + MIXER)],
    "attn_fwd0": [("gather_pass", ("w_up",) + MIXER), ("gather_ici", ("w_down",))],
    "attn_fwd1": [("gather_pass", ("w_down",))],
    "conv_gelu_bwd": [("pair", ("w_down",))],
    "conv_input_bwd": [("chip", ("w_down",))],
    "pre_ffn_norm_bwd": [("pair", ("w_up",))],
    "attn_bwd0": [("chip", ("w_up",)), ("pair", MIXER)],
    "attn_bwd1": [("chip", MIXER)],
    "g_w_in_b": [("pair", ("w_in_a",))],
    "d_proj_in": [("chip", ("w_in_a",)), ("pair", ("w_in_b",))],
    "pre_mix_norm_bwd": [("chip", ("w_in_b",))],
}
W_IN_ROWS = dict(w_in_a=(0, 768), w_in_b=(3, 256))
W_IN_PIECES = tuple(W_IN_ROWS)
REDUCED = W_IN_PIECES + BIG[1:]


class _Exchange:
    def __init__(self, place, slots, conv_w_shard):
        self.place, self.slots, self.conv_w_shard = place, dict(slots), conv_w_shard
        self.conv_w = None
        self.g, self.from_sibling, self.pair_sums, self.arrived = {}, {}, {}, {}
        self.pending = []

    def weight(self, name):
        if name == "conv_w":
            return self.conv_w
        w = self.slots[name]
        return w.reshape(-1, D_MODEL) if name in ("w_out", "w_down") else w

    def grad(self, name, g):
        self.g[name] = g

    def carry(self, point):
        plans = []
        self.pending = SCHEDULE.get(point, [])
        for kind, names in self.pending:
            if kind in ("gather_ici", "gather_ici_cw"):
                wholes = [self.conv_w_shard] if kind == "gather_ici_cw" else []
                plans.append(_gather_ici_plan([self.slots[n] for n in names], wholes))
            elif kind == "gather_pass":
                plans.append(_gather_pass_plan([self.slots[n] for n in names]))
            elif kind == "pair":
                plans.append(_pair_plan([self.g[n] for n in names]))
            else:
                for n in names:
                    self.pair_sums[n] = _pair_sum(self.g[n], self.from_sibling[n], self.place[1:2], f"pair_sum_{n}")
                plans.append(_chip_plan([self.pair_sums[n] for n in names]))
        return plans

    def done(self, point, carried):
        for (kind, names), got in zip(self.pending, carried):
            if kind in ("gather_ici", "gather_ici_cw", "gather_pass"):
                self.slots.update(zip(names, got))
                if kind == "gather_ici_cw":
                    self.conv_w = got[len(names)].transpose(1, 0, 2).reshape(3, 2 * D_FF)
            elif kind == "pair":
                self.from_sibling.update(zip(names, got))
            else:
                self.arrived.update(zip(names, got))

    def reduced(self):
        halves = [_chip_sum(self.arrived[n], self.pair_sums[n], self.place, f"chip_sum_{n}") for n in REDUCED]
        out = dict(zip(REDUCED, _pair_concat(halves, "pair_concat")))
        out["w_in"] = jnp.concatenate([out.pop(n) for n in W_IN_PIECES], axis=0)
        return out


def kernel(x, pre_mix_norm, w_in, rel_bias, hgrn_lb_raw, hgrn_norm, w_branch_attn, w_branch_hgrn, w_out, post_mix_norm, pre_ffn_norm, w_up, conv_w, conv_b, w_down, post_ffn_norm, loss_target, m_pre_mix_norm, m_w_in, m_rel_bias, m_hgrn_lb_raw, m_hgrn_norm, m_w_branch_attn, m_w_branch_hgrn, m_w_out, m_post_mix_norm, m_pre_ffn_norm, m_w_up, m_conv_w, m_conv_b, m_w_down, m_post_ffn_norm, v_pre_mix_norm, v_w_in, v_rel_bias, v_hgrn_lb_raw, v_hgrn_norm, v_w_branch_attn, v_w_branch_hgrn, v_w_out, v_post_mix_norm, v_pre_ffn_norm, v_w_up, v_conv_w, v_conv_b, v_w_down, v_post_ffn_norm):
    w = dict(pre_mix_norm=pre_mix_norm, w_in=w_in, rel_bias=rel_bias, hgrn_lb_raw=hgrn_lb_raw, hgrn_norm=hgrn_norm,
             w_branch_attn=w_branch_attn, w_branch_hgrn=w_branch_hgrn, w_out=w_out, post_mix_norm=post_mix_norm,
             pre_ffn_norm=pre_ffn_norm, w_up=w_up, conv_w=conv_w, conv_b=conv_b, w_down=w_down,
             post_ffn_norm=post_ffn_norm)
    m = dict(pre_mix_norm=m_pre_mix_norm, w_in=m_w_in, rel_bias=m_rel_bias, hgrn_lb_raw=m_hgrn_lb_raw,
             hgrn_norm=m_hgrn_norm, w_branch_attn=m_w_branch_attn, w_branch_hgrn=m_w_branch_hgrn, w_out=m_w_out,
             post_mix_norm=m_post_mix_norm, pre_ffn_norm=m_pre_ffn_norm, w_up=m_w_up, conv_w=m_conv_w,
             conv_b=m_conv_b, w_down=m_w_down, post_ffn_norm=m_post_ffn_norm)
    v = dict(pre_mix_norm=v_pre_mix_norm, w_in=v_w_in, rel_bias=v_rel_bias, hgrn_lb_raw=v_hgrn_lb_raw,
             hgrn_norm=v_hgrn_norm, w_branch_attn=v_w_branch_attn, w_branch_hgrn=v_w_branch_hgrn, w_out=v_w_out,
             post_mix_norm=v_post_mix_norm, pre_ffn_norm=v_pre_ffn_norm, w_up=v_w_up, conv_w=v_conv_w,
             conv_b=v_conv_b, w_down=v_w_down, post_ffn_norm=v_post_ffn_norm)
    shard2d = {n: (w[n][0] if w[n].ndim == 3 else w[n]) for n in WEIGHTS}
    chip = 2 * lax.axis_index("x") + lax.axis_index("y")
    core = lax.axis_index("c")

    place = jnp.stack([chip, core]).astype(jnp.int32)
    slots = {n: _cast_into_slot(shard2d[n], place, f"cast_{n}") for n in BIG}
    slots["w_in"] = _gather_weights([slots["w_in"]], [], "gather_w_in")[0]
    ex = _Exchange(place, slots, shard2d["conv_w"])
    loss, grad_x, small = _local_step(x[0], loss_target[0], {n: w[n] for n in SMALL if n != "conv_w"}, ex)

    flat = [small[n].reshape(-1) for n in SMALL] + [loss.reshape(-1)]
    sizes = [t.shape[0] for t in flat]
    summed = _all_sum(jnp.concatenate(flat).reshape(-1, LANES), "sum_small").reshape(-1)
    offs = [sum(sizes[:i]) for i in range(len(sizes))]
    grads = {}
    for n, o, sz in zip(SMALL, offs, sizes):
        grads[n] = summed[o:o + sz].reshape(small[n].shape)
    loss_total = summed[offs[-1]]
    cw = 2 * D_FF // N_CHIPS
    grads["conv_w"] = lax.dynamic_slice(grads["conv_w"], (0, chip * cw), (3, cw))

    grads.update(ex.reduced())

    out_g, out_d, out_m, out_v = [], [], [], []
    for n in WEIGHTS:
        d2, m2, v2 = _adamw(shard2d[n], grads[n], m[n].reshape(shard2d[n].shape), v[n].reshape(shard2d[n].shape),
                            f"adamw_{n}")
        shape = w[n].shape
        out_g.append(grads[n].reshape(shape))
        out_d.append(d2.reshape(shape))
        out_m.append(m2.reshape(shape))
        out_v.append(v2.reshape(shape))
    return (loss_total, grad_x[None], *out_g, *out_d, *out_m, *out_v)
```

```python
import functools
import math

import jax
import jax.numpy as jnp
from jax import lax
from jax.experimental import pallas as pl
from jax.experimental.pallas import tpu as pltpu

F32 = jnp.float32
BF16 = jnp.bfloat16
MESH = pl.DeviceIdType.MESH

D_MODEL = 1024
N_GROUPS = 3
DILATIONS = (1, 4, 16)
HEADS = 8
HEAD_DIM = 64
GROUP_W = HEADS * HEAD_DIM
QKV_W = N_GROUPS * 3 * GROUP_W
BLK = 128
NEG_INF = -1e30
NUM_BUCKETS = 32
MAX_EXACT = 16
MAX_DISTANCE = 2048
HG_HEADS = 4
HG_DK = 128
HG_W = HG_HEADS * HG_DK
HG_CHUNK = 32
HG_TILE = 256
IN_W = QKV_W + 4 * HG_W + 2 * D_MODEL
D_FF = 2816
EPS = 1e-6
N_CHIPS = 4
N_DEV = 8
LANES = 128

ADAM_LR, ADAM_B1, ADAM_B2, ADAM_EPS, ADAM_WD, ADAM_STEP = 0.001, 0.9, 0.999, 1e-08, 0.01, 10

VMEM_LIMIT = 56 * 1024 * 1024


def _cp(n_axes):
    return pltpu.CompilerParams(dimension_semantics=("arbitrary",) * n_axes, vmem_limit_bytes=VMEM_LIMIT)


def _sds(shape, dtype):
    return jax.ShapeDtypeStruct(tuple(shape), dtype)


def _sigmoid(v):
    return 1.0 / (1.0 + jnp.exp(-v))


def _bf(v):
    return v.astype(BF16)


def _dot(a, b, dims):
    return lax.dot_general(a, b, (dims, ((), ())), preferred_element_type=F32)


NN = ((1,), (0,))
NT = ((1,), (1,))
TN = ((0,), (0,))

ANY = pl.BlockSpec(memory_space=pl.ANY)


class _Plan:
    def __init__(self, copies, n_sems, ins=(), inouts=(), outs=()):
        self.copies, self.n_sems = copies, n_sems
        self.ins, self.inouts, self.outs = list(ins), list(inouts), list(outs)


def _call(body, plans=None, *, name, grid, in_specs, out_specs, out_shape, args, scratch_shapes=()):
    plans = list(plans or ())
    in_specs, out_specs, out_shape = list(in_specs), list(out_specs), list(out_shape)
    scratch_shapes = list(scratch_shapes)
    n_in, n_out, n_scr = len(in_specs), len(out_specs), len(scratch_shapes)
    x_in, x_out, aliases, spans = [], [], {}, []
    for p in plans:
        i0, o0 = len(x_in), len(x_out)
        x_in += p.ins
        for a in p.inouts:
            aliases[n_in + len(x_in)] = n_out + len(x_out)
            x_in.append(a)
            x_out.append(_sds(a.shape, a.dtype))
        x_out += p.outs
        spans.append((i0, len(p.ins), o0, len(p.inouts), len(p.outs)))
    sems = [pltpu.SemaphoreType.DMA((p.n_sems,)) for p in plans for _ in range(3)]

    def wrapped(*refs):
        xi = refs[n_in:n_in + len(x_in)]
        base = n_in + len(x_in)
        xo = refs[base + n_out:base + n_out + len(x_out)]
        sbase = base + n_out + len(x_out)
        xs = refs[sbase + n_scr:]
        ids = [pl.program_id(k) for k in range(len(grid))]
        first = functools.reduce(jnp.logical_and, [i == 0 for i in ids])
        last = functools.reduce(jnp.logical_and, [i == g - 1 for i, g in zip(ids, grid)])

        def descriptors(k):
            i0, ni, o0, nio, no = spans[k]
            return plans[k].copies(xi[i0:i0 + ni], xo[o0:o0 + nio], xo[o0 + nio:o0 + nio + no], *xs[3 * k:3 * k + 3])

        @pl.when(first)
        def _():
            for k in range(len(plans)):
                sends, _, local = descriptors(k)
                for cp in (*sends, *local):
                    cp.start()

        body(*refs[:n_in], *refs[base:base + n_out], *refs[sbase:sbase + n_scr])

        @pl.when(last)
        def _():
            for k in range(len(plans)):
                sends, recvs, local = descriptors(k)
                for cp in recvs:
                    cp.wait_recv()
                for cp in sends:
                    cp.wait_send()
                for cp in local:
                    cp.wait()

    res = pl.pallas_call(
        wrapped if plans else body, name=name, grid=grid, in_specs=in_specs + [ANY] * len(x_in),
        out_specs=out_specs + [ANY] * len(x_out), out_shape=out_shape + x_out, input_output_aliases=aliases,
        scratch_shapes=scratch_shapes + sems, compiler_params=_cp(len(grid)))(*args, *x_in)
    res = list(res)
    carried = [res[n_out + o0:n_out + o0 + nio + no] for (_, _, o0, nio, no) in spans]
    return res[:n_out], carried


def _mm_nn_blk(a, wg, name, tm=512, plans=None):
    M, K = a.shape
    nb, _, Nb = wg.shape

    def body(a_ref, w_ref, o_ref):
        o_ref[...] = _dot(_bf(a_ref[...]), w_ref[...], NN)

    (out,), carried = _call(
        body, plans, name=name, grid=(nb, M // tm),
        in_specs=[pl.BlockSpec((tm, K), lambda j, i: (i, 0)), pl.BlockSpec((None, K, Nb), lambda j, i: (j, 0, 0))],
        out_specs=[pl.BlockSpec((tm, Nb), lambda j, i: (i, j))],
        out_shape=[_sds((M, nb * Nb), F32)], args=(a, wg))
    return out if plans is None else (out, carried)


def _mm_nt_blk(dy, wg, name, tm=1024, plans=None):
    M = dy.shape[0]
    nb, K, Nb = wg.shape

    def body(dy_ref, w_ref, o_ref):
        j = pl.program_id(1)
        r = _dot(_bf(dy_ref[...]), w_ref[...], NT)

        @pl.when(j == 0)
        def _():
            o_ref[...] = r

        @pl.when(j > 0)
        def _():
            o_ref[...] += r

    (out,), carried = _call(
        body, plans, name=name, grid=(M // tm, nb),
        in_specs=[pl.BlockSpec((tm, Nb), lambda i, j: (i, j)), pl.BlockSpec((None, K, Nb), lambda i, j: (j, 0, 0))],
        out_specs=[pl.BlockSpec((tm, K), lambda i, j: (i, 0))],
        out_shape=[_sds((M, K), F32)], args=(dy, wg))
    return out if plans is None else (out, carried)


def _mm_tn_blk(x, dy, nb, name, tk=2048, x_cols=None, plans=None, together=False):
    T, Mx = x.shape
    xk, Mx = (0, Mx) if x_cols is None else x_cols
    Nb = dy.shape[1] // nb
    nj = nb if together else 1

    def body(x_ref, dy_ref, o_ref):
        t = pl.program_id(1)
        r = _dot(_bf(x_ref[...]), _bf(dy_ref[...]), TN)
        for j in range(nj):
            rj = r[:, j * Nb:(j + 1) * Nb]

            @pl.when(t == 0)
            def _():
                o_ref[j] = rj

            @pl.when(t > 0)
            def _():
                o_ref[j] += rj

    (out,), carried = _call(
        body, plans, name=name, grid=(nb // nj, T // tk),
        in_specs=[pl.BlockSpec((tk, Mx), lambda j, t: (t, xk)), pl.BlockSpec((tk, nj * Nb), lambda j, t: (t, j))],
        out_specs=[pl.BlockSpec((nj, Mx, Nb), lambda j, t: (j, 0, 0))],
        out_shape=[_sds((nb, Mx, Nb), F32)], args=(x, dy))
    return out if plans is None else (out, carried)


def _mm_tn(x, dy, name, tk=1024):
    T, Mx = x.shape
    N = dy.shape[1]

    def body(x_ref, dy_ref, o_ref):
        t = pl.program_id(0)
        r = _dot(_bf(x_ref[...]), _bf(dy_ref[...]), TN)

        @pl.when(t == 0)
        def _():
            o_ref[...] = r

        @pl.when(t > 0)
        def _():
            o_ref[...] += r

    return pl.pallas_call(
        body, name=name, grid=(T // tk,),
        in_specs=[pl.BlockSpec((tk, Mx), lambda t: (t, 0)), pl.BlockSpec((tk, N), lambda t: (t, 0))],
        out_specs=pl.BlockSpec((Mx, N), lambda t: (0, 0)),
        out_shape=_sds((Mx, N), F32), compiler_params=_cp(1))(x, dy)


def _tile(arr, bw, col=lambda c: 0):
    return ("tile", arr, bw, col)


def _full(arr):
    return ("full", arr)


def _out_tile(width, dtype, bw, col=lambda c: 0):
    return ("tile", width, dtype, bw, col)


def _out_acc(rows, width, bw, col=lambda c: 0):
    return ("acc", rows, width, bw, col)


def _rows_call(name, body, n_rows, tm, ncol, ins, outs, plans=None):
    in_specs, args = [], []
    for e in ins:
        if e[0] == "tile":
            _, arr, bw, col = e
            in_specs.append(pl.BlockSpec((tm, bw), functools.partial(lambda c, i, col: (i, col(c)), col=col)))
        else:
            arr = e[1]
            in_specs.append(pl.BlockSpec(arr.shape, functools.partial(lambda c, i, nd: (0,) * nd, nd=arr.ndim)))
        args.append(arr)
    out_specs, out_shape = [], []
    for e in outs:
        if e[0] == "tile":
            _, width, dtype, bw, col = e
            out_specs.append(pl.BlockSpec((tm, bw), functools.partial(lambda c, i, col: (i, col(c)), col=col)))
            out_shape.append(_sds((n_rows, width), dtype))
        else:
            _, rows, width, bw, col = e
            out_specs.append(pl.BlockSpec((rows, bw), functools.partial(lambda c, i, col: (0, col(c)), col=col)))
            out_shape.append(_sds((rows, width), F32))
    out, carried = _call(body, plans, name=name, grid=(ncol, n_rows // tm), in_specs=in_specs, out_specs=out_specs,
                         out_shape=out_shape, args=args)
    return out if plans is None else (out, carried)


def _acc(ref, val):
    i = pl.program_id(1)

    @pl.when(i == 0)
    def _():
        ref[...] = val

    @pl.when(i > 0)
    def _():
        ref[...] += val


def _rinv(z):
    return lax.rsqrt(jnp.mean(z * z, axis=-1, keepdims=True) + EPS)


def _norm_bwd(dy, zhat, r, w):
    dyw = dy * w
    return r * (dyw - zhat * jnp.mean(dyw * zhat, axis=-1, keepdims=True))


def _norm_proj(x, w, wg, name, tm=1024, plans=None):
    M, K = x.shape
    nb, _, Nb = wg.shape

    def body(x_ref, w_ref, wg_ref, o_ref, h_ref):
        @pl.when(pl.program_id(1) == 0)
        def _():
            xv = x_ref[...]
            h_ref[...] = _bf(xv * _rinv(xv) * w_ref[...])

        o_ref[...] = _dot(h_ref[...], wg_ref[...], NN)

    (out, h), carried = _call(
        body, plans, name=name, grid=(M // tm, nb),
        in_specs=[pl.BlockSpec((tm, K), lambda i, j: (i, 0)), pl.BlockSpec((1, K), lambda i, j: (0, 0)),
                  pl.BlockSpec((None, K, Nb), lambda i, j: (j, 0, 0))],
        out_specs=[pl.BlockSpec((tm, Nb), lambda i, j: (i, j)), pl.BlockSpec((tm, K), lambda i, j: (i, 0))],
        out_shape=[_sds((M, nb * Nb), F32), _sds((M, K), BF16)], args=(x, w, wg))
    return (out, h) if plans is None else (out, h, carried)


def _prenorm_bwd(dh, xin, w, dres, name, plans=None):
    def body(dh_ref, x_ref, w_ref, dres_ref, dx_ref, dw_ref):
        xv = x_ref[...]
        r = _rinv(xv)
        xhat = xv * r
        dhv = dh_ref[...]
        dx_ref[...] = dres_ref[...] + _norm_bwd(dhv, xhat, r, w_ref[...])
        _acc(dw_ref, jnp.sum(dhv * xhat, axis=0, keepdims=True))

    return _rows_call(name, body, xin.shape[0], 512, 1,
                      [_tile(dh, D_MODEL), _tile(xin, D_MODEL), _full(w), _tile(dres, D_MODEL)],
                      [_out_tile(D_MODEL, F32, D_MODEL), _out_acc(1, D_MODEL, D_MODEL)], plans)


def _postnorm_bwd(dout, z, w, w_mat, name):
    def body(do_ref, z_ref, w_ref, wm_ref, dz_ref, dm_ref, dw_ref):
        zv = z_ref[...]
        r = _rinv(zv)
        zhat = zv * r
        dov = do_ref[...]
        dz = _bf(_norm_bwd(dov, zhat, r, w_ref[...]))
        dz_ref[...] = dz
        dm_ref[...] = _dot(dz, wm_ref[...], NT)
        _acc(dw_ref, jnp.sum(dov * zhat, axis=0, keepdims=True))

    return _rows_call(name, body, z.shape[0], 512, 1,
                      [_tile(dout, D_MODEL), _tile(z, D_MODEL), _full(w), _full(w_mat)],
                      [_out_tile(D_MODEL, BF16, D_MODEL), _out_tile(D_MODEL, F32, D_MODEL),
                       _out_acc(1, D_MODEL, D_MODEL)])


def _t5_bucket(dist):
    n = jnp.maximum(dist, 0)
    nf = jnp.maximum(n, 1).astype(F32)
    large = MAX_EXACT + (jnp.log(nf / MAX_EXACT) / math.log(MAX_DISTANCE / MAX_EXACT)
                         * (NUM_BUCKETS - MAX_EXACT)).astype(jnp.int32)
    large = jnp.minimum(large, NUM_BUCKETS - 1)
    return jnp.where(n < MAX_EXACT, n, large)


def _band_rel():
    return jnp.arange(BLK)[:, None] + BLK - jnp.arange(2 * BLK)[None, :]


def _band_valid():
    rel = _band_rel()
    window = (rel >= 0) & (rel <= BLK)
    first = window & (jnp.arange(2 * BLK)[None, :] >= BLK)
    return jnp.stack([first, window]).astype(F32).reshape(2, 1, BAND)


RES_UNROLL = 4
PAIR = LANES // HEAD_DIM


def _pair_lanes():
    first = lax.broadcasted_iota(jnp.int32, (1, LANES), 1) < HEAD_DIM
    return first, jnp.logical_not(first)


def _heads_per_step(d):
    return HEADS if d == 1 else LANES // HEAD_DIM


def _sub_rows(r, d):
    return pl.ds(r, BLK, stride=d) if d > 1 else pl.ds(0, BLK)


def _for_residues(d, fn):
    if d <= RES_UNROLL:
        for r in range(d):
            fn(r)
    else:
        def group(i, carry):
            for k in range(RES_UNROLL):
                fn(i * RES_UNROLL + k)
            return carry

        lax.fori_loop(0, d // RES_UNROLL, group, 0)


def _attn_specs(d, g, qblock):
    cw = _heads_per_step(d) * HEAD_DIM

    def col(part, hp):
        return (g * 3 + part) * (GROUP_W // cw) + hp

    def cur(part):
        return pl.BlockSpec((d * BLK, cw), lambda hp, n: (qblock(n), col(part, hp)))

    def prev(part):
        return pl.BlockSpec((d * BLK, cw), lambda hp, n: (jnp.maximum(qblock(n) - 1, 0), col(part, hp)))

    return cur, prev


def _attn_fwd(proj, bias, g, name, plans=None):
    S = proj.shape[0]
    d = DILATIONS[g]
    NB = S // (d * BLK)
    hps = _heads_per_step(d)

    def body(q_ref, kp_ref, kc_ref, vp_ref, vc_ref, b_ref, o_ref, lse_ref):
        hp = pl.program_id(0)
        later = jnp.minimum(pl.program_id(1), 1)

        def residue(r):
            rows = _sub_rows(r, d)
            q2 = q_ref[rows, :]
            k2 = jnp.concatenate([kp_ref[rows, :], kc_ref[rows, :]], axis=0)
            v2 = jnp.concatenate([vp_ref[rows, :], vc_ref[rows, :]], axis=0)
            outs, lses = [], []
            for pp in range(hps // PAIR):
                ps = slice(pp * LANES, (pp + 1) * LANES)
                qp, kp, vp = _bf(q2[:, ps]), _bf(k2[:, ps]), _bf(v2[:, ps])
                o_h, lse_h = [], []
                for hh, own in enumerate(_pair_lanes()):
                    s = _dot(qp, jnp.where(own, kp, 0), NT) * (HEAD_DIM ** -0.5) + b_ref[later, hp * hps + pp * PAIR + hh]
                    m = jnp.max(s, axis=-1, keepdims=True)
                    p = jnp.exp(s - m)
                    l = jnp.sum(p, axis=-1, keepdims=True)
                    o_h.append(_dot(_bf(p), vp, NN) / l)
                    lse_h.append(m + jnp.log(l))
                first = _pair_lanes()[0]
                outs.append(jnp.where(first, o_h[0], o_h[1]))
                lses.append(jnp.where(first, lse_h[0], lse_h[1]))
            o_ref[rows, :] = outs[0] if len(outs) == 1 else jnp.concatenate(outs, axis=1)
            lse_ref[rows, :] = lses[0] if len(lses) == 1 else jnp.concatenate(lses, axis=1)

        _for_residues(d, residue)

    cur, prev = _attn_specs(d, g, lambda n: n)
    out = pl.BlockSpec((d * BLK, hps * HEAD_DIM), lambda hp, n: (n, hp))
    res, carried = _call(
        body, plans, name=name, grid=(HEADS // hps, NB),
        in_specs=[cur(0), prev(1), cur(1), prev(2), cur(2),
                  pl.BlockSpec((2, HEADS, BLK, 2 * BLK), lambda hp, n: (0, 0, 0, 0))],
        out_specs=[out, out], out_shape=[_sds((S, GROUP_W), F32)] * 2,
        args=(proj, proj, proj, proj, proj, bias))
    return res if plans is None else (res, carried)


def _attn_bwd(proj, bias, lse, y, dy, g, name, plans=None):
    S = proj.shape[0]
    d = DILATIONS[g]
    NB = S // (d * BLK)
    hps = _heads_per_step(d)

    def body(q_ref, kp_ref, kc_ref, vp_ref, vc_ref, b_ref, l_ref, y_ref, dy_ref,
             dq_ref, dk_ref, dv_ref, db_ref, ck_ref, cv_ref):
        hp, n = pl.program_id(0), pl.program_id(1)

        @pl.when((hp == 0) & (n == 0))
        def _():
            db_ref[...] = jnp.zeros_like(db_ref)

        @pl.when(n == 0)
        def _():
            ck_ref[...] = jnp.zeros_like(ck_ref)
            cv_ref[...] = jnp.zeros_like(cv_ref)

        @pl.when(n < NB)
        def _():
            later = jnp.minimum(n, 1)

            def residue(r):
                rows = _sub_rows(r, d)
                q2 = q_ref[rows, :]
                k2 = jnp.concatenate([kp_ref[rows, :], kc_ref[rows, :]], axis=0)
                v2 = jnp.concatenate([vp_ref[rows, :], vc_ref[rows, :]], axis=0)
                l2, y2, dy2 = l_ref[rows, :], y_ref[rows, :], dy_ref[rows, :]
                dqs, dks, dvs = [], [], []
                for pp in range(hps // PAIR):
                    ps = slice(pp * LANES, (pp + 1) * LANES)
                    qp, kp, vp = _bf(q2[:, ps]), _bf(k2[:, ps]), _bf(v2[:, ps])
                    dyp, yp = dy2[:, ps], y2[:, ps]
                    dq_h, dk_h, dv_h = [], [], []
                    for hh, own in enumerate(_pair_lanes()):
                        head = hp * hps + pp * PAIR + hh
                        s = _dot(qp, jnp.where(own, kp, 0), NT) * (HEAD_DIM ** -0.5) + b_ref[later, head]
                        p = jnp.exp(s - l2[:, pp * LANES + hh * HEAD_DIM:pp * LANES + hh * HEAD_DIM + 1])
                        dyh = jnp.where(own, dyp, 0.0)
                        delta = jnp.sum(dyh * yp, axis=-1, keepdims=True)
                        ds = p * (_dot(_bf(dyh), vp, NT) - delta)
                        db_ref[head] += ds
                        dsb = _bf(ds * (HEAD_DIM ** -0.5))
                        dq_h.append(_dot(dsb, kp, NN))
                        dk_h.append(_dot(dsb, qp, TN))
                        dv_h.append(_dot(_bf(p), _bf(dyp), TN))
                    first = _pair_lanes()[0]
                    dqs.append(jnp.where(first, dq_h[0], dq_h[1]))
                    dks.append(jnp.where(first, dk_h[0], dk_h[1]))
                    dvs.append(jnp.where(first, dv_h[0], dv_h[1]))
                dkb = dks[0] if len(dks) == 1 else jnp.concatenate(dks, axis=1)
                dvb = dvs[0] if len(dvs) == 1 else jnp.concatenate(dvs, axis=1)
                dq_ref[rows, :] = dqs[0] if len(dqs) == 1 else jnp.concatenate(dqs, axis=1)
                dk_ref[rows, :] = ck_ref[rows, :] + dkb[:BLK]
                dv_ref[rows, :] = cv_ref[rows, :] + dvb[:BLK]
                ck_ref[rows, :] = dkb[BLK:]
                cv_ref[rows, :] = dvb[BLK:]

            _for_residues(d, residue)

        @pl.when(n == NB)
        def _():
            dk_ref[...] = ck_ref[...]
            dv_ref[...] = cv_ref[...]

    def qn(n):
        return jnp.minimum(n, NB - 1)

    cur, prev = _attn_specs(d, g, qn)
    cw = hps * HEAD_DIM
    row = pl.BlockSpec((d * BLK, cw), lambda hp, n: (qn(n), hp))
    done = pl.BlockSpec((d * BLK, cw), lambda hp, n: (jnp.maximum(n - 1, 0), hp))
    (dq, dk, dv, db), carried = _call(
        body, plans, name=name, grid=(HEADS // hps, NB + 1),
        in_specs=[cur(0), prev(1), cur(1), prev(2), cur(2),
                  pl.BlockSpec((2, HEADS, BLK, 2 * BLK), lambda hp, n: (0, 0, 0, 0)), row, row, row],
        out_specs=[row, done, done, pl.BlockSpec((HEADS, BLK, 2 * BLK), lambda hp, n: (0, 0, 0))],
        out_shape=[_sds((S, GROUP_W), F32)] * 3 + [_sds((HEADS, BLK, 2 * BLK), F32)],
        scratch_shapes=[pltpu.VMEM((d * BLK, cw), F32)] * 2,
        args=(proj, proj, proj, proj, proj, bias, lse, y, dy))
    return ([dq, dk, dv], db) if plans is None else ([dq, dk, dv], db, carried)


BAND = BLK * 2 * BLK


def _bucket_onehot():
    buckets = jnp.stack([_t5_bucket(_band_rel() * d) for d in DILATIONS]).reshape(N_GROUPS, 1, BAND)
    return (buckets == jnp.arange(NUM_BUCKETS).reshape(1, NUM_BUCKETS, 1)).astype(F32)


def _relbias_fwd(rel_bias, name):
    table = rel_bias.reshape(NUM_BUCKETS, N_GROUPS, HEADS).transpose(1, 0, 2)

    def body(t_ref, oh_ref, valid_ref, o_ref):
        bias = lax.dot_general(t_ref[...], oh_ref[...], (TN, ((), ())), preferred_element_type=F32,
                               precision=lax.Precision.HIGHEST)
        for k in range(2):
            o_ref[k] = jnp.where(valid_ref[k] > 0.5, bias, NEG_INF)

    out = pl.pallas_call(
        body, name=name, grid=(N_GROUPS,),
        in_specs=[pl.BlockSpec((None, NUM_BUCKETS, HEADS), lambda g: (g, 0, 0)),
                  pl.BlockSpec((None, NUM_BUCKETS, BAND), lambda g: (g, 0, 0)),
                  pl.BlockSpec((2, 1, BAND), lambda g: (0, 0, 0))],
        out_specs=pl.BlockSpec((None, 2, HEADS, BAND), lambda g: (g, 0, 0, 0)),
        out_shape=_sds((N_GROUPS, 2, HEADS, BAND), F32), compiler_params=_cp(1))(table, _bucket_onehot(), _band_valid())
    return out.reshape(N_GROUPS, 2, HEADS, BLK, 2 * BLK)


def _relbias_bwd(dbs, name):
    band = BAND
    onehot = _bucket_onehot()
    dbf = jnp.stack([db.reshape(HEADS, band) for db in dbs])

    def body(oh_ref, db_ref, o_ref):
        o_ref[...] = lax.dot_general(oh_ref[...], db_ref[...], (NT, ((), ())), preferred_element_type=F32,
                                     precision=lax.Precision.HIGHEST)

    out = pl.pallas_call(
        body, name=name, grid=(N_GROUPS,),
        in_specs=[pl.BlockSpec((None, NUM_BUCKETS, band), lambda g: (g, 0, 0)),
                  pl.BlockSpec((None, HEADS, band), lambda g: (g, 0, 0))],
        out_specs=pl.BlockSpec((None, NUM_BUCKETS, HEADS), lambda g: (g, 0, 0)),
        out_shape=_sds((N_GROUPS, NUM_BUCKETS, HEADS), F32), compiler_params=_cp(1))(onehot, dbf)
    return out.transpose(1, 0, 2).reshape(NUM_BUCKETS, N_GROUPS * HEADS)


def _chunk_pos(shape):
    return lax.broadcasted_iota(jnp.int32, shape, 0) % HG_CHUNK


def _chunk_cumsum(v):
    pos = _chunk_pos(v.shape)
    s = 1
    while s < HG_CHUNK:
        v = v + jnp.where(pos >= s, pltpu.roll(v, s, 0), 0.0)
        s *= 2
    return v


def _chunk_rev_cumsum(v):
    pos = _chunk_pos(v.shape)
    n = v.shape[0]
    s = 1
    while s < HG_CHUNK:
        v = v + jnp.where(pos < HG_CHUNK - s, pltpu.roll(v, n - s, 0), 0.0)
        s *= 2
    return v


def _lower_bound(raw):
    a0, a1 = raw[0:1], raw[1:2]
    m = jnp.maximum(a0, a1)
    e0, e1 = jnp.exp(a0 - m), jnp.exp(a1 - m)
    return e0 / (e0 + e1)


def _hg_gates(qr, fr, lb):
    sf = _sigmoid(fr)
    f = lb + (1.0 - lb) * sf
    sq = _sigmoid(qr)
    return qr * sq, sq, f, sf


HG_COL0 = QKV_W // HG_W


def _hgrn_fwd(proj, lb_raw, nw, name):
    S = proj.shape[0]
    ncs = HG_TILE // HG_CHUNK
    tril = jnp.tril(jnp.ones((HG_CHUNK, HG_CHUNK), dtype=bool))

    def body(q_ref, f_ref, i_ref, og_ref, lb_ref, nw_ref, y_ref, o_ref, st_ref, state):
        @pl.when(pl.program_id(0) == 0)
        def _():
            state[...] = jnp.zeros_like(state)

        lb = _lower_bound(lb_ref[...])
        q, _, f, _ = _hg_gates(q_ref[...], f_ref[...], lb)
        k = 1.0 - f
        G = _chunk_cumsum(jnp.log(f))
        row = lax.broadcasted_iota(jnp.int32, (HG_CHUNK, HG_CHUNK), 0)
        col = lax.broadcasted_iota(jnp.int32, (HG_CHUNK, HG_CHUNK), 1)
        heads = [slice(h * HG_DK, (h + 1) * HG_DK) for h in range(HG_HEADS)]
        sts = [state[h] for h in range(HG_HEADS)]
        for c in range(ncs):
            cs = slice(c * HG_CHUNK, (c + 1) * HG_CHUNK)
            for h, hs in enumerate(heads):
                Gc = G[cs, hs]
                gl = Gc[HG_CHUNK - 1:HG_CHUNK]
                qt = _bf(q[cs, hs] * jnp.exp(Gc))
                kt = _bf(k[cs, hs] * jnp.exp(-Gc))
                kd = _bf(k[cs, hs] * jnp.exp(gl - Gc))
                v = _bf(i_ref[cs, hs])
                A = jnp.where(row >= col, _dot(qt, kt, NT), 0.0)
                o_ref[cs, hs] = _dot(_bf(A), v, NN) + _dot(qt, _bf(sts[h]), NT)
                st_ref[c, h] = sts[h]
                sts[h] = sts[h] * jnp.exp(gl) + _dot(v, kd, TN)
        for h, hs in enumerate(heads):
            state[h] = sts[h]
            oh = o_ref[:, hs]
            og = og_ref[:, hs]
            y_ref[:, hs] = oh * _rinv(oh) * nw_ref[...] * (og * _sigmoid(og))

    def colspec(j):
        return pl.BlockSpec((HG_TILE, HG_W), lambda i: (i, HG_COL0 + j))

    return pl.pallas_call(
        body, name=name, grid=(S // HG_TILE,),
        in_specs=[colspec(0), colspec(1), colspec(2), colspec(3),
                  pl.BlockSpec((2, HG_W), lambda i: (0, 0)), pl.BlockSpec((1, HG_DK), lambda i: (0, 0))],
        out_specs=[pl.BlockSpec((HG_TILE, HG_W), lambda i: (i, 0))] * 2
        + [pl.BlockSpec((ncs, HG_HEADS, HG_DK, HG_DK), lambda i: (i, 0, 0, 0))],
        out_shape=[_sds((S, HG_W), F32)] * 2 + [_sds((S // HG_CHUNK, HG_HEADS, HG_DK, HG_DK), F32)],
        scratch_shapes=[pltpu.VMEM((HG_HEADS, HG_DK, HG_DK), F32)],
        compiler_params=_cp(1))(proj, proj, proj, proj, lb_raw, nw)


def _hgrn_bwd(proj, lb_raw, nw, o, states, dy, d_attn, d_gates, name):
    S = proj.shape[0]
    ncs = HG_TILE // HG_CHUNK
    nt = S // HG_TILE
    n_a, n_g = len(d_attn), len(d_gates)
    own = [slice(QKV_W + j * HG_W, QKV_W + (j + 1) * HG_W) for j in range(4)]

    def body(q_ref, f_ref, i_ref, og_ref, lb_ref, nw_ref, o_ref, st_ref, dy_ref, *rest):
        attn_refs, gate_refs = rest[:n_a], rest[n_a:n_a + n_g]
        dp_ref, dlb_ref, dnw_ref, dstate, do_s, dG_s, dgl_s, dk_s, dlb_s = rest[n_a + n_g:]
        dq_ref, df_ref, di_ref, dog_ref = (dp_ref.at[:, cols] for cols in own)
        step = pl.program_id(0)
        for k, a_ref in enumerate(attn_refs):
            dp_ref[:, k * GROUP_W:(k + 1) * GROUP_W] = _bf(a_ref[...])
        for k, g_ref in enumerate(gate_refs):
            dp_ref[:, QKV_W + 4 * HG_W + k * D_MODEL:QKV_W + 4 * HG_W + (k + 1) * D_MODEL] = g_ref[...]

        @pl.when(step == 0)
        def _():
            dstate[...] = jnp.zeros_like(dstate)
            dlb_s[...] = jnp.zeros_like(dlb_s)
            dnw_ref[...] = jnp.zeros_like(dnw_ref)

        lb = _lower_bound(lb_ref[...])
        qr = q_ref[...]
        q, sq, f, sf = _hg_gates(qr, f_ref[...], lb)
        k = 1.0 - f
        G = _chunk_cumsum(jnp.log(f))
        nwv = nw_ref[...]
        row = lax.broadcasted_iota(jnp.int32, (HG_CHUNK, HG_CHUNK), 0)
        col = lax.broadcasted_iota(jnp.int32, (HG_CHUNK, HG_CHUNK), 1)
        for h in range(HG_HEADS):
            hs = slice(h * HG_DK, (h + 1) * HG_DK)
            oh = o_ref[:, hs]
            r = _rinv(oh)
            ohat = oh * r
            og = og_ref[:, hs]
            sg = _sigmoid(og)
            dyh = dy_ref[:, hs]
            don = dyh * (og * sg)
            dog_ref[:, hs] = _bf(dyh * (ohat * nwv) * (sg * (1.0 + og * (1.0 - sg))))
            dnw_ref[...] += jnp.sum(don * ohat, axis=0, keepdims=True)
            do_s[:, hs] = _norm_bwd(don, ohat, r, nwv)
        dsts = [dstate[h] for h in range(HG_HEADS)]
        for c in reversed(range(ncs)):
            cs = slice(c * HG_CHUNK, (c + 1) * HG_CHUNK)
            for h in range(HG_HEADS):
                hs = slice(h * HG_DK, (h + 1) * HG_DK)
                dst = dsts[h]
                Gc = G[cs, hs]
                gl = Gc[HG_CHUNK - 1:HG_CHUNK]
                eG, enG, edG, egl = jnp.exp(Gc), jnp.exp(-Gc), jnp.exp(gl - Gc), jnp.exp(gl)
                qt, kt, kd = q[cs, hs] * eG, k[cs, hs] * enG, k[cs, hs] * edG
                qtb, ktb, kdb = _bf(qt), _bf(kt), _bf(kd)
                v = _bf(i_ref[cs, hs])
                do = _bf(do_s[cs, hs])
                st = st_ref[c, h]
                dstb = _bf(dst)
                A = jnp.where(row >= col, _dot(qtb, ktb, NT), 0.0)
                dA = _bf(jnp.where(row >= col, _dot(do, v, NT), 0.0))
                di_ref[cs, hs] = _bf(_dot(_bf(A), do, TN) + _dot(kdb, dstb, NT))
                dqt = _dot(dA, ktb, NN) + _dot(do, _bf(st), NN)
                dkt = _dot(dA, qtb, TN)
                dkd = _dot(v, dstb, NN)
                dgl = egl * jnp.sum(st * dst, axis=0, keepdims=True) + jnp.sum(dkd * kd, axis=0, keepdims=True)
                dsts[h] = dst * egl + _dot(do, qtb, TN)
                dq_ref[cs, hs] = _bf(dqt * eG * (sq[cs, hs] * (1.0 + qr[cs, hs] * (1.0 - sq[cs, hs]))))
                dk_s[cs, hs] = dkt * enG + dkd * edG
                dG_s[cs, hs] = dqt * qt - dkt * kt - dkd * kd
                dgl_s[cs, hs] = jnp.broadcast_to(dgl, (HG_CHUNK, HG_DK))
        for h in range(HG_HEADS):
            dstate[h] = dsts[h]
        dg = _chunk_rev_cumsum(dG_s[...]) + dgl_s[...]
        dfv = dg / f - dk_s[...]
        df_ref[...] = _bf(dfv * (1.0 - lb) * sf * (1.0 - sf))
        dlb_s[...] += jnp.sum(dfv * (1.0 - sf), axis=0, keepdims=True)

        @pl.when(step == nt - 1)
        def _():
            t = dlb_s[...] * lb * (1.0 - lb)
            dlb_ref[...] = jnp.concatenate([t, -t], axis=0)

    def colspec(j):
        return pl.BlockSpec((HG_TILE, HG_W), lambda i: (nt - 1 - i, HG_COL0 + j))

    def rows(width):
        return pl.BlockSpec((HG_TILE, width), lambda i: (nt - 1 - i, 0))

    tile = rows(HG_W)
    return pl.pallas_call(
        body, name=name, grid=(nt,),
        in_specs=[colspec(0), colspec(1), colspec(2), colspec(3),
                  pl.BlockSpec((2, HG_W), lambda i: (0, 0)), pl.BlockSpec((1, HG_DK), lambda i: (0, 0)),
                  tile, pl.BlockSpec((ncs, HG_HEADS, HG_DK, HG_DK), lambda i: (nt - 1 - i, 0, 0, 0)), tile]
        + [rows(GROUP_W)] * n_a + [rows(D_MODEL)] * n_g,
        out_specs=[rows(IN_W), pl.BlockSpec((2, HG_W), lambda i: (0, 0)), pl.BlockSpec((1, HG_DK), lambda i: (0, 0))],
        out_shape=[_sds((S, IN_W), BF16), _sds((2, HG_W), F32), _sds((1, HG_DK), F32)],
        scratch_shapes=[pltpu.VMEM((HG_HEADS, HG_DK, HG_DK), F32)] + [pltpu.VMEM((HG_TILE, HG_W), F32)] * 4
        + [pltpu.VMEM((1, HG_W), F32)],
        compiler_params=_cp(1))(proj, proj, proj, proj, lb_raw, nw, o, states, dy, *d_attn, *d_gates)


GATE_COL0 = (QKV_W + 4 * HG_W) // GROUP_W
HALF_D = D_MODEL // 2


def _gate_tiles(proj):
    return [_tile(proj, HALF_D, functools.partial(lambda c, k: GATE_COL0 + k, k=k)) for k in range(4)]


def _gates(g_refs):
    s0 = _sigmoid(jnp.concatenate([g_refs[0][...], g_refs[1][...]], axis=1))
    s1 = _sigmoid(jnp.concatenate([g_refs[2][...], g_refs[3][...]], axis=1))
    return s0, s1


def _branch_fwd(os_, lses, yh, proj, w_a, w_h, name):
    nb = w_a.shape[0]

    def body(o0, o1, o2, l0, l1, l2, yh_ref, g0a, g0b, g1a, g1b, wa_ref, wh_ref,
             y_ref, lse_ref, za_ref, zh_ref, m_ref):
        a, b, c = l0[...], l1[...], l2[...]
        m = jnp.maximum(jnp.maximum(a, b), c)
        ea, eb, ec = jnp.exp(a - m), jnp.exp(b - m), jnp.exp(c - m)
        den = ea + eb + ec
        y = (ea * o0[...] + eb * o1[...] + ec * o2[...]) / den
        y_ref[...] = y
        lse_ref[...] = m + jnp.log(den)
        yb, yhb = _bf(y), _bf(yh_ref[...])
        za = jnp.concatenate([_dot(yb, wa_ref[j], NN) for j in range(nb)], axis=1)
        zh = jnp.concatenate([_dot(yhb, wh_ref[j], NN) for j in range(nb)], axis=1)
        s0, s1 = _gates((g0a, g0b, g1a, g1b))
        za_ref[...] = za
        zh_ref[...] = zh
        m_ref[...] = _bf(s0 * za + s1 * zh)

    return _rows_call(name, body, yh.shape[0], 512, 1,
                      [*[_tile(t, GROUP_W) for t in (*os_, *lses)], _tile(yh, HG_W), *_gate_tiles(proj),
                       _full(w_a), _full(w_h)],
                      [_out_tile(GROUP_W, F32, GROUP_W)] * 2 + [_out_tile(D_MODEL, F32, D_MODEL)] * 2
                      + [_out_tile(D_MODEL, BF16, D_MODEL)])


def _branch_bwd(dm, za, zh, proj, w_a, w_h, name, plans=None):
    nb, _, Nb = w_a.shape

    def body(dm_ref, za_ref, zh_ref, g0a, g0b, g1a, g1b, wa_ref, wh_ref,
             dza_ref, dzh_ref, dg0_ref, dg1_ref, dy_ref, dyh_ref):
        dmv = dm_ref[...]
        s0, s1 = _gates((g0a, g0b, g1a, g1b))
        dza, dzh = _bf(dmv * s0), _bf(dmv * s1)
        dza_ref[...] = dza
        dzh_ref[...] = dzh
        dg0_ref[...] = _bf(dmv * za_ref[...] * s0 * (1.0 - s0))
        dg1_ref[...] = _bf(dmv * zh_ref[...] * s1 * (1.0 - s1))
        dy_ref[...] = sum(_dot(dza[:, j * Nb:(j + 1) * Nb], wa_ref[j], NT) for j in range(nb))
        dyh_ref[...] = sum(_dot(dzh[:, j * Nb:(j + 1) * Nb], wh_ref[j], NT) for j in range(nb))

    return _rows_call(name, body, za.shape[0], 512, 1,
                      [_tile(dm, D_MODEL), _tile(za, D_MODEL), _tile(zh, D_MODEL), *_gate_tiles(proj),
                       _full(w_a), _full(w_h)],
                      [_out_tile(D_MODEL, BF16, D_MODEL)] * 4 + [_out_tile(GROUP_W, F32, GROUP_W),
                                                                 _out_tile(HG_W, F32, HG_W)], plans)


def _mix_out(merged, w_out, x, w_post, w_pre, name):
    def body(m_ref, wo_ref, x_ref, wp_ref, wf_ref, mo_ref, x1_ref, h2_ref):
        z = _dot(m_ref[...], wo_ref[...], NN)
        mo_ref[...] = z
        x1 = x_ref[...] + z * _rinv(z) * wp_ref[...]
        x1_ref[...] = x1
        h2_ref[...] = _bf(x1 * _rinv(x1) * wf_ref[...])

    return _rows_call(name, body, x.shape[0], 512, 1,
                      [_tile(merged, D_MODEL), _full(w_out), _tile(x, D_MODEL), _full(w_post), _full(w_pre)],
                      [_out_tile(D_MODEL, F32, D_MODEL), _out_tile(D_MODEL, F32, D_MODEL),
                       _out_tile(D_MODEL, BF16, D_MODEL)])


def _loss_head(a, w_down, x1, tgt, w, name):
    def body(a_ref, wd_ref, x1_ref, t_ref, w_ref, dx_ref, df_ref, dw_ref, loss_ref):
        z = _dot(a_ref[...], wd_ref[...], NN)
        r = _rinv(z)
        zhat = z * r
        wv = w_ref[...]
        e = x1_ref[...] + zhat * wv - t_ref[...]
        dx = e * (1.0 / D_MODEL)
        dx_ref[...] = dx
        df_ref[...] = _bf(_norm_bwd(dx, zhat, r, wv))
        _acc(dw_ref, jnp.sum(dx * zhat, axis=0, keepdims=True))
        part = 0.5 * jnp.sum(jnp.sum(e * e, axis=1, keepdims=True), axis=0, keepdims=True) * (1.0 / D_MODEL)
        _acc(loss_ref, jnp.broadcast_to(part, (1, LANES)))

    return _rows_call(name, body, x1.shape[0], 512, 1,
                      [_tile(a, D_FF), _full(w_down), _tile(x1, D_MODEL), _tile(tgt, D_MODEL), _full(w)],
                      [_out_tile(D_MODEL, F32, D_MODEL), _out_tile(D_MODEL, BF16, D_MODEL),
                       _out_acc(1, D_MODEL, D_MODEL), _out_acc(1, LANES, LANES)])


CONV_CB = D_FF // 2
CONV_TM = 512
HALO = 8
SQRT_HALF = 0.7071067811865476
INV_SQRT_2PI = 0.3989422804014327


CONV_RS = 32


def _lane_tiles():
    return [slice(k * LANES, (k + 1) * LANES) for k in range(CONV_CB // LANES)]


def _strip_start(i):
    return pl.multiple_of(i * CONV_RS, CONV_RS)


def _strip_taps(u_ref, halo_ref, r0, cs, first_strip, first_tile):
    if first_strip:
        before = jnp.where(first_tile, 0.0, halo_ref[:, cs])
        blk = jnp.concatenate([before, u_ref[0:CONV_RS, cs]], axis=0)
    else:
        blk = u_ref[pl.ds(pl.multiple_of(r0 - HALO, HALO), CONV_RS + HALO), cs]
    return pltpu.roll(blk, 2, 0)[HALO:], pltpu.roll(blk, 1, 0)[HALO:], blk[HALO:]


def _conv(taps, w_ref, b_ref, cs):
    return b_ref[:, cs] + w_ref[0:1, cs] * taps[0] + w_ref[1:2, cs] * taps[1] + w_ref[2:3, cs] * taps[2]


def _conv_specs(tm):
    nh = tm // HALO
    nc = D_FF // CONV_CB

    def tile(off):
        return pl.BlockSpec((tm, CONV_CB), lambda c, i: (i, off + c))

    def halo(off):
        return pl.BlockSpec((HALO, CONV_CB), lambda c, i: (jnp.maximum(i * nh - 1, 0), off + c))

    def small(rows, off):
        return pl.BlockSpec((rows, CONV_CB), lambda c, i: (0, off + c))

    return nc, tile, halo, small


def _conv_gelu_fwd(u, cw, cb, name):
    S = u.shape[0]
    tm = CONV_TM
    nc, tile, halo, small = _conv_specs(tm)

    def body(ug, hg, uv, hv, wg, wv, bg, bv, a_ref):
        first_tile = pl.program_id(1) == 0

        def strip(r0, first_strip):
            for cs in _lane_tiles():
                cg = _conv(_strip_taps(ug, hg, r0, cs, first_strip, first_tile), wg, bg, cs)
                cv = _conv(_strip_taps(uv, hv, r0, cs, first_strip, first_tile), wv, bv, cs)
                a_ref[pl.ds(r0, CONV_RS), cs] = _bf(0.5 * cg * (1.0 + lax.erf(cg * SQRT_HALF)) * cv)

        strip(0, True)
        lax.fori_loop(1, tm // CONV_RS, lambda k, c: (strip(_strip_start(k), False), c)[1], 0)

    return pl.pallas_call(
        body, name=name, grid=(nc, S // tm),
        in_specs=[tile(0), halo(0), tile(nc), halo(nc), small(3, 0), small(3, nc), small(1, 0), small(1, nc)],
        out_specs=tile(0), out_shape=_sds((S, D_FF), BF16), compiler_params=_cp(2))(u, u, u, u, cw, cw, cb, cb)


def _conv_gelu_bwd(u, dff, w_down, cw, cb, name, plans=None):
    S = u.shape[0]
    tm = CONV_TM
    nt = S // tm
    nc, tile, halo, small = _conv_specs(tm)

    def body(ug, hg, uv, hv, wg, wv, bg, bv, dff_ref, wd_ref, dcg_ref, dcv_ref, dwg_ref, dwv_ref, dbg_ref, dbv_ref,
             acc, da_ref):
        i = pl.program_id(1)
        first_tile = i == 0
        da_ref[...] = _dot(dff_ref[...], wd_ref[...], NT)

        @pl.when(first_tile)
        def _():
            acc[...] = jnp.zeros_like(acc)

        def strip(r0, first_strip):
            rows = pl.ds(r0, CONV_RS)
            for cs in _lane_tiles():
                tg = _strip_taps(ug, hg, r0, cs, first_strip, first_tile)
                tv = _strip_taps(uv, hv, r0, cs, first_strip, first_tile)
                cg = _conv(tg, wg, bg, cs)
                cv = _conv(tv, wv, bv, cs)
                phi = 0.5 * (1.0 + lax.erf(cg * SQRT_HALF))
                dav = da_ref[rows, cs]
                dcg = dav * cv * (phi + cg * jnp.exp(-0.5 * cg * cg) * INV_SQRT_2PI)
                dcv = dav * (cg * phi)
                dcg_ref[rows, cs] = dcg
                dcv_ref[rows, cs] = dcv
                for half, (dc, taps) in enumerate(((dcg, tg), (dcv, tv))):
                    for j in range(3):
                        acc[4 * half + j, :, cs] += dc * taps[j]
                    acc[4 * half + 3, :, cs] += dc

        strip(0, True)
        lax.fori_loop(1, tm // CONV_RS, lambda k, c: (strip(_strip_start(k), False), c)[1], 0)

        @pl.when(i == nt - 1)
        def _():
            for half, (dw_ref, db_ref) in enumerate(((dwg_ref, dbg_ref), (dwv_ref, dbv_ref))):
                for j in range(3):
                    dw_ref[j:j + 1, :] = jnp.sum(acc[4 * half + j], axis=0, keepdims=True)
                db_ref[...] = jnp.sum(acc[4 * half + 3], axis=0, keepdims=True)

    res, carried = _call(
        body, plans, name=name, grid=(nc, nt),
        in_specs=[tile(0), halo(0), tile(nc), halo(nc), small(3, 0), small(3, nc), small(1, 0), small(1, nc),
                  pl.BlockSpec((tm, D_MODEL), lambda c, i: (i, 0)), pl.BlockSpec((CONV_CB, D_MODEL), lambda c, i: (c, 0))],
        out_specs=[tile(0), tile(0), small(3, 0), small(3, 0), small(1, 0), small(1, 0)],
        out_shape=[_sds((S, D_FF), F32)] * 2 + [_sds((3, D_FF), F32)] * 2 + [_sds((1, D_FF), F32)] * 2,
        scratch_shapes=[pltpu.VMEM((8, CONV_RS, CONV_CB), F32), pltpu.VMEM((tm, CONV_CB), F32)],
        args=(u, u, u, u, cw, cw, cb, cb, dff, w_down))
    return res if plans is None else (res, carried)


def _conv_input_bwd(dcg, dcv, cw, name, plans=None):
    S = dcg.shape[0]
    tm = CONV_TM // 2
    nh = tm // HALO
    nt = S // tm
    n = CONV_RS + HALO
    tile = pl.BlockSpec((tm, D_FF), lambda i: (i, 0))
    nxt = pl.BlockSpec((HALO, D_FF), lambda i: (jnp.minimum((i + 1) * nh, S // HALO - 1), 0))

    def body(g_ref, ng_ref, v_ref, nv_ref, w_ref, du_ref):
        last_tile = pl.program_id(0) == nt - 1

        def strip(r0, last_strip):
            for half, (dc_ref, n_ref) in enumerate(((g_ref, ng_ref), (v_ref, nv_ref))):
                for k in range(D_FF // LANES):
                    cs = slice(k * LANES, (k + 1) * LANES)
                    ws = slice(half * D_FF + k * LANES, half * D_FF + (k + 1) * LANES)
                    if last_strip:
                        after = jnp.where(last_tile, 0.0, n_ref[:, cs])
                        blk = jnp.concatenate([dc_ref[tm - CONV_RS:tm, cs], after], axis=0)
                    else:
                        blk = dc_ref[pl.ds(r0, n), cs]
                    d1 = pltpu.roll(blk, n - 1, 0)[:CONV_RS]
                    d2 = pltpu.roll(blk, n - 2, 0)[:CONV_RS]
                    du_ref[pl.ds(r0, CONV_RS), ws] = _bf(w_ref[2:3, ws] * blk[:CONV_RS] + w_ref[1:2, ws] * d1
                                                         + w_ref[0:1, ws] * d2)

        lax.fori_loop(0, tm // CONV_RS - 1, lambda k, c: (strip(_strip_start(k), False), c)[1], 0)
        strip(tm - CONV_RS, True)

    (du,), carried = _call(
        body, plans, name=name, grid=(nt,),
        in_specs=[tile, nxt, tile, nxt, pl.BlockSpec((3, 2 * D_FF), lambda i: (0, 0))],
        out_specs=[pl.BlockSpec((tm, 2 * D_FF), lambda i: (i, 0))], out_shape=[_sds((S, 2 * D_FF), BF16)],
        args=(dcg, dcg, dcv, dcv, cw))
    return du if plans is None else (du, carried)


def _row_tile(n, cap):
    best = n
    for t in range(16, cap + 1, 16):
        if n % t == 0:
            best = t
    return best if best <= cap else n


def _rows_for_bytes(nbytes, cols):
    return max(16, nbytes // (4 * cols) // 16 * 16)


def _adamw(w, g, m, v, name):
    R, C = w.shape
    tr = _row_tile(R, _rows_for_bytes(2 << 20, C))

    def body(w_ref, g_ref, m_ref, v_ref, d_ref, nm_ref, nv_ref):
        gv = g_ref[...]
        nm = ADAM_B1 * m_ref[...] + (1.0 - ADAM_B1) * gv
        nv = ADAM_B2 * v_ref[...] + (1.0 - ADAM_B2) * (gv * gv)
        m_hat = nm / (1.0 - ADAM_B1 ** ADAM_STEP)
        v_hat = nv / (1.0 - ADAM_B2 ** ADAM_STEP)
        d_ref[...] = -ADAM_LR * (m_hat / (jnp.sqrt(v_hat) + ADAM_EPS) + ADAM_WD * w_ref[...])
        nm_ref[...] = nm
        nv_ref[...] = nv

    spec = pl.BlockSpec((tr, C), lambda i: (i, 0))
    return pl.pallas_call(body, name=name, grid=(R // tr,), in_specs=[spec] * 4, out_specs=[spec] * 3,
                          out_shape=[_sds((R, C), F32)] * 3, compiler_params=_cp(1))(w, g, m, v)


def _pair_sum(gfull, rcv, c_idx, name):
    nb, R, C = gfull.shape
    half = R // 2
    tr = _row_tile(half, _rows_for_bytes(2 << 20, C))
    nt = half // tr

    def body(c_ref, g_ref, r_ref, o_ref):
        o_ref[...] = _bf(g_ref[...] + r_ref[...])

    return pl.pallas_call(
        body, name=name,
        grid_spec=pltpu.PrefetchScalarGridSpec(
            num_scalar_prefetch=1, grid=(nb, nt),
            in_specs=[pl.BlockSpec((None, tr, C), lambda j, i, c_ref: (j, c_ref[0] * nt + i, 0)),
                      pl.BlockSpec((None, tr, C), lambda j, i, c_ref: (j, i, 0))],
            out_specs=pl.BlockSpec((None, tr, C), lambda j, i, c_ref: (j, i, 0))),
        out_shape=_sds((nb, half, C), BF16), compiler_params=_cp(2))(c_idx, gfull, rcv)


def _chip_sum(arrived, own, place, name):
    nb, H, C = arrived.shape
    tr = _row_tile(H, _rows_for_bytes(2 << 20, C))
    nt = H // tr

    def body(pl_ref, *refs):
        o_ref = refs[nb + 1]
        me = pl_ref[0]
        acc = None
        for k in range(nb):
            term = jnp.where(me == k, refs[nb][...], refs[k][...]).astype(F32)
            acc = term if acc is None else acc + term
        o_ref[...] = acc

    def other(k):
        return pl.BlockSpec((None, tr, C), lambda i, p: (jnp.where(p[0] == k, (k + 1) % nb, k), i, 0))

    return pl.pallas_call(
        body, name=name,
        grid_spec=pltpu.PrefetchScalarGridSpec(
            num_scalar_prefetch=1, grid=(nt,),
            in_specs=[other(k) for k in range(nb)] + [pl.BlockSpec((None, tr, C), lambda i, p: (p[0], i, 0))],
            out_specs=pl.BlockSpec((tr, C), lambda i, p: (p[1] * nt + i, 0))),
        out_shape=_sds((2 * H, C), F32), compiler_params=_cp(1))(place, *([arrived] * nb), own)


def _cast_into_slot(shard, place, name):
    R, C = shard.shape
    tr = _row_tile(R, 256)

    def body(pl_ref, s_ref, o_ref):
        o_ref[...] = _bf(s_ref[...])

    return pl.pallas_call(
        body, name=name,
        grid_spec=pltpu.PrefetchScalarGridSpec(
            num_scalar_prefetch=1, grid=(R // tr,),
            in_specs=[pl.BlockSpec((tr, C), lambda i, p: (i, 0))],
            out_specs=pl.BlockSpec((None, tr, C), lambda i, p: (p[0], i, 0))),
        out_shape=_sds((N_CHIPS, R, C), BF16), compiler_params=_cp(1))(place, shard)


def _place():
    x, y, c = lax.axis_index("x"), lax.axis_index("y"), lax.axis_index("c")
    chips = [(1 - x, y), (x, 1 - y), (1 - x, 1 - y)]
    return x, y, c, chips


def _chip_id(px, py):
    return 2 * px + py


def _remote(src, dst, send_sems, recv_sems, k, to):
    return pltpu.make_async_remote_copy(src_ref=src, dst_ref=dst, send_sem=send_sems.at[k], recv_sem=recv_sems.at[k],
                                        device_id=to, device_id_type=MESH)


def _gather_weights(slots, wholes, name):
    ns, nw = len(slots), len(wholes)
    n = ns + nw

    def body(*refs):
        ins = refs[ns:n]
        outs = refs[n:2 * n]
        send_sems, recv_sems, local_sems = refs[2 * n:]
        x, y, c, chips = _place()
        me = _chip_id(x, y)
        sib = (x, y, 1 - c)
        local = [pltpu.make_async_copy(ins[b], outs[ns + b].at[me], local_sems.at[b]) for b in range(nw)]
        for cp in local:
            cp.start()
        sent = []
        for a in range(n):
            R = outs[a].shape[1]
            rows = pl.ds(c * (R // 2), R // 2) if a < ns else pl.ds(0, R)
            src = outs[a].at[me, rows] if a < ns else ins[a - ns]
            for j, chip in enumerate(chips):
                cp = _remote(src, outs[a].at[me, rows], send_sems, recv_sems, 6 * a + j, (*chip, c))
                cp.start()
                sent.append(cp)
        for a in range(n):
            R = outs[a].shape[1]
            rows = pl.ds(c * (R // 2), R // 2) if a < ns else pl.ds(0, R)
            for j, chip in enumerate(chips):
                landed = outs[a].at[_chip_id(*chip), rows]
                _remote(landed, landed, send_sems, recv_sems, 6 * a + j, (*chip, c)).wait_recv()
                if a < ns:
                    cp = _remote(landed, landed, send_sems, recv_sems, 6 * a + 3 + j, sib)
                    cp.start()
                    sent.append(cp)
        for a in range(ns):
            R = outs[a].shape[1]
            other = pl.ds((1 - c) * (R // 2), R // 2)
            for j, chip in enumerate(chips):
                passed = outs[a].at[_chip_id(*chip), other]
                _remote(passed, passed, send_sems, recv_sems, 6 * a + 3 + j, sib).wait_recv()
        for cp in sent:
            cp.wait_send()
        for cp in local:
            cp.wait()

    return pl.pallas_call(
        body, name=name, in_specs=[ANY] * n, out_specs=[ANY] * n,
        out_shape=[_sds(s.shape, s.dtype) for s in slots] + [_sds((N_CHIPS, *s.shape), s.dtype) for s in wholes],
        input_output_aliases={a: a for a in range(ns)},
        scratch_shapes=[pltpu.SemaphoreType.DMA((6 * n,)), pltpu.SemaphoreType.DMA((6 * n,)),
                        pltpu.SemaphoreType.DMA((max(nw, 1),))])(*slots, *wholes)


def _gather_ici_plan(slots, wholes):
    ns, nw = len(slots), len(wholes)

    def copies(ins, ios, outs, send_sems, recv_sems, local_sems):
        x, y, c, chips = _place()
        me = _chip_id(x, y)
        sends, recvs = [], []
        for a in range(ns + nw):
            dst = ios[a] if a < ns else outs[a - ns]
            R = dst.shape[1]
            rows = pl.ds(c * (R // 2), R // 2) if a < ns else pl.ds(0, R)
            src = dst.at[me, rows] if a < ns else ins[a - ns]
            for j, chip in enumerate(chips):
                sends.append(_remote(src, dst.at[me, rows], send_sems, recv_sems, 3 * a + j, (*chip, c)))
                landed = dst.at[_chip_id(*chip), rows]
                recvs.append(_remote(landed, landed, send_sems, recv_sems, 3 * a + j, (*chip, c)))
        local = [pltpu.make_async_copy(ins[b], outs[b].at[me], local_sems.at[b]) for b in range(nw)]
        return sends, recvs, local

    return _Plan(copies, 3 * (ns + nw), ins=wholes, inouts=slots,
                 outs=[_sds((N_CHIPS, *s.shape), s.dtype) for s in wholes])


def _gather_pass_plan(slots):
    def copies(ins, ios, outs, send_sems, recv_sems, local_sems):
        x, y, c, chips = _place()
        sib = (x, y, 1 - c)
        sends, recvs = [], []
        for a, buf in enumerate(ios):
            half = buf.shape[1] // 2
            for j, chip in enumerate(chips):
                mine = buf.at[_chip_id(*chip), pl.ds(c * half, half)]
                other = buf.at[_chip_id(*chip), pl.ds((1 - c) * half, half)]
                sends.append(_remote(mine, mine, send_sems, recv_sems, 3 * a + j, sib))
                recvs.append(_remote(other, other, send_sems, recv_sems, 3 * a + j, sib))
        return sends, recvs, []

    return _Plan(copies, 3 * len(slots), inouts=slots)


def _pair_plan(grads):
    def copies(ins, ios, outs, send_sems, recv_sems, local_sems):
        x, y, c, _ = _place()
        sib = (x, y, 1 - c)
        sends, recvs = [], []
        for a, g in enumerate(ins):
            half = g.shape[1] // 2
            sends.append(_remote(g.at[:, pl.ds((1 - c) * half, half), :], outs[a], send_sems, recv_sems, a, sib))
            recvs.append(_remote(outs[a], outs[a], send_sems, recv_sems, a, sib))
        return sends, recvs, []

    return _Plan(copies, len(grads), ins=grads,
                 outs=[_sds((g.shape[0], g.shape[1] // 2, g.shape[2]), g.dtype) for g in grads])


def _chip_plan(parts):
    def copies(ins, ios, outs, send_sems, recv_sems, local_sems):
        x, y, c, chips = _place()
        me = _chip_id(x, y)
        sends, recvs = [], []
        for a, part in enumerate(ins):
            for j, chip in enumerate(chips):
                sends.append(_remote(part.at[_chip_id(*chip)], outs[a].at[me], send_sems, recv_sems, 3 * a + j, (*chip, c)))
                landed = outs[a].at[_chip_id(*chip)]
                recvs.append(_remote(landed, landed, send_sems, recv_sems, 3 * a + j, (*chip, c)))
        return sends, recvs, []

    return _Plan(copies, 3 * len(parts), ins=parts, outs=[_sds(p.shape, p.dtype) for p in parts])


def _all_sum(pack, fulls, name):
    R, C = pack.shape
    n = len(fulls)

    def body(p_ref, *refs):
        o_ref, halves = refs[n], refs[n + 1:2 * n + 1]
        buf, send_sems, recv_sems, pair_send, pair_recv = refs[2 * n + 1:]
        x, y, c, _ = _place()
        sib = (x, y, 1 - c)
        pair = []
        for a, full in enumerate(halves):
            H = full.shape[0] // 2
            mine = full.at[pl.ds(c * H, H)]
            cp = _remote(mine, mine, pair_send, pair_recv, a, sib)
            cp.start()
            pair.append(cp)
        me = 4 * x + 2 * y + c
        buf[me] = p_ref[...]
        cps = []
        for k in range(1, N_DEV):
            to = (x ^ (k >> 2), y ^ ((k >> 1) & 1), c ^ (k & 1))
            cp = _remote(p_ref, buf.at[me], send_sems, recv_sems, k - 1, to)
            cp.start()
            cps.append(cp)
        for k in range(1, N_DEV):
            frm = (x ^ (k >> 2), y ^ ((k >> 1) & 1), c ^ (k & 1))
            slot = buf.at[4 * frm[0] + 2 * frm[1] + frm[2]]
            _remote(slot, slot, send_sems, recv_sems, k - 1, frm).wait_recv()
        acc = buf[0]
        for k in range(1, N_DEV):
            acc = acc + buf[k]
        o_ref[...] = acc
        for cp in cps:
            cp.wait_send()
        for a, (full, cp) in enumerate(zip(halves, pair)):
            H = full.shape[0] // 2
            other = full.at[pl.ds((1 - c) * H, H)]
            _remote(other, other, pair_send, pair_recv, a, sib).wait_recv()
            cp.wait_send()

    vm = pl.BlockSpec(memory_space=pltpu.VMEM)
    res = pl.pallas_call(
        body, name=name, in_specs=[vm] + [ANY] * n, out_specs=[vm] + [ANY] * n,
        out_shape=[_sds((R, C), F32)] + [_sds(f.shape, f.dtype) for f in fulls],
        input_output_aliases={1 + a: 1 + a for a in range(n)},
        scratch_shapes=[pltpu.VMEM((N_DEV, R, C), F32), pltpu.SemaphoreType.DMA((N_DEV - 1,)),
                        pltpu.SemaphoreType.DMA((N_DEV - 1,)), pltpu.SemaphoreType.DMA((n,)),
                        pltpu.SemaphoreType.DMA((n,))])(pack, *fulls)
    return res[0], list(res[1:])


def _local_step(xs, tgt, p, ex):
    proj, h1, got = _norm_proj(xs, p["pre_mix_norm"], ex.weight("w_in"), "proj_in", plans=ex.carry("proj_in"))
    ex.done("proj_in", got)
    biases = _relbias_fwd(p["rel_bias"], "rel_bias_fwd")
    fw = []
    for g in range(N_GROUPS):
        res, got = _attn_fwd(proj, biases[g], g, f"attn_fwd{g}", plans=ex.carry(f"attn_fwd{g}"))
        ex.done(f"attn_fwd{g}", got)
        fw.append(res)
    yh, o_h, states = _hgrn_fwd(proj, p["hgrn_lb_raw"], p["hgrn_norm"], "hgrn_fwd")
    W_a, W_h, W_out = ex.weight("w_branch_attn"), ex.weight("w_branch_hgrn"), ex.weight("w_out")
    W_up, W_down, conv_w = ex.weight("w_up"), ex.weight("w_down"), ex.weight("conv_w")
    y, lse, za, zh, merged = _branch_fwd([t[0] for t in fw], [t[1] for t in fw], yh, proj, W_a, W_h, "branch_fwd")
    mo, x1, h2 = _mix_out(merged, W_out, xs, p["post_mix_norm"], p["pre_ffn_norm"], "mix_out")
    u = _mm_nn_blk(h2, W_up, "ffn_up")
    a = _conv_gelu_fwd(u, conv_w, p["conv_b"], "conv_gelu_fwd")
    dx2, dff, g_post_ffn, loss = _loss_head(a, W_down, x1, tgt, p["post_ffn_norm"], "ffn_down_loss")

    ex.grad("w_down", _mm_tn(a, dff, "g_w_down").reshape(N_CHIPS, D_FF // N_CHIPS, D_MODEL))
    (dcg, dcv, gwg, gwv, gbg, gbv), got = _conv_gelu_bwd(u, dff, W_down, conv_w, p["conv_b"], "conv_gelu_bwd",
                                                          plans=ex.carry("conv_gelu_bwd"))
    ex.done("conv_gelu_bwd", got)
    g_conv_w = jnp.concatenate([gwg, gwv], axis=1)
    g_conv_b = jnp.concatenate([gbg, gbv], axis=1)
    du, got = _conv_input_bwd(dcg, dcv, conv_w, "conv_input_bwd", plans=ex.carry("conv_input_bwd"))
    ex.done("conv_input_bwd", got)
    dh2 = _mm_nt_blk(du, W_up, "d_ffn_in")
    ex.grad("w_up", _mm_tn_blk(h2, du, N_CHIPS, "g_w_up"))
    dx1, g_pre_ffn = _prenorm_bwd(dh2, x1, p["pre_ffn_norm"], dx2, "pre_ffn_norm_bwd")
    dmo, dmerged, g_post_mix = _postnorm_bwd(dx1, mo, p["post_mix_norm"], W_out, "post_mix_norm_bwd")
    ex.grad("w_out", _mm_tn(merged, dmo, "g_w_out").reshape(N_CHIPS, D_MODEL // N_CHIPS, D_MODEL))
    (dza, dzh, dg0, dg1, dy, dyh), got = _branch_bwd(dmerged, za, zh, proj, W_a, W_h, "branch_bwd",
                                                     plans=ex.carry("branch_bwd"))
    ex.done("branch_bwd", got)
    ex.grad("w_branch_attn", _mm_tn_blk(y, dza, N_CHIPS, "g_w_branch_attn", together=True))
    ex.grad("w_branch_hgrn", _mm_tn_blk(yh, dzh, N_CHIPS, "g_w_branch_hgrn", together=True))
    dqkv, dbs = [], []
    for g in range(N_GROUPS):
        parts, db, got = _attn_bwd(proj, biases[g], lse, y, dy, g, f"attn_bwd{g}", plans=ex.carry(f"attn_bwd{g}"))
        ex.done(f"attn_bwd{g}", got)
        dqkv += parts
        dbs.append(db)
    g_rel_bias = _relbias_bwd(dbs, "rel_bias_bwd")
    dproj, g_lb_raw, g_hgrn_norm = _hgrn_bwd(proj, p["hgrn_lb_raw"], p["hgrn_norm"], o_h, states, dyh, dqkv,
                                             [dg0, dg1], "hgrn_bwd")
    for piece in W_IN_PIECES:
        g, got = _mm_tn_blk(h1, dproj, N_CHIPS, f"g_{piece}", x_cols=W_IN_ROWS[piece],
                            plans=ex.carry(f"g_{piece}"))
        ex.done(f"g_{piece}", got)
        ex.grad(piece, g)
    dh1, got = _mm_nt_blk(dproj, ex.weight("w_in"), "d_proj_in", plans=ex.carry("d_proj_in"))
    ex.done("d_proj_in", got)
    (grad_x, g_pre_mix), got = _prenorm_bwd(dh1, xs, p["pre_mix_norm"], dx1, "pre_mix_norm_bwd",
                                            plans=ex.carry("pre_mix_norm_bwd"))
    ex.done("pre_mix_norm_bwd", got)
    small = dict(pre_mix_norm=g_pre_mix, rel_bias=g_rel_bias, hgrn_lb_raw=g_lb_raw, hgrn_norm=g_hgrn_norm,
                 post_mix_norm=g_post_mix, pre_ffn_norm=g_pre_ffn, conv_w=g_conv_w, conv_b=g_conv_b,
                 post_ffn_norm=g_post_ffn)
    return loss, grad_x, small


SMALL = ("pre_mix_norm", "rel_bias", "hgrn_lb_raw", "hgrn_norm", "post_mix_norm", "pre_ffn_norm", "conv_w", "conv_b",
         "post_ffn_norm")
BIG = ("w_in", "w_up", "w_down", "w_out", "w_branch_attn", "w_branch_hgrn")
WEIGHTS = ("pre_mix_norm", "w_in", "rel_bias", "hgrn_lb_raw", "hgrn_norm", "w_branch_attn", "w_branch_hgrn", "w_out",
           "post_mix_norm", "pre_ffn_norm", "w_up", "conv_w", "conv_b", "w_down", "post_ffn_norm")
MIXER = ("w_out", "w_branch_attn", "w_branch_hgrn")

SCHEDULE = {
    "proj_in": [("gather_ici_cw", ("w_up",) + MIXER)],
    "attn_fwd0": [("gather_pass", ("w_up",) + MIXER), ("gather_ici", ("w_down",))],
    "attn_fwd1": [("gather_pass", ("w_down",))],
    "conv_gelu_bwd": [("pair", ("w_down",))],
    "conv_input_bwd": [("chip", ("w_down",))],
    "branch_bwd": [("pair", ("w_up",))],
    "attn_bwd0": [("chip", ("w_up",)), ("pair", MIXER)],
    "attn_bwd1": [("chip", MIXER)],
    "g_w_in_b": [("pair", ("w_in_a",))],
    "d_proj_in": [("chip", ("w_in_a",)), ("pair", ("w_in_b",))],
    "pre_mix_norm_bwd": [("chip", ("w_in_b",))],
}
W_IN_ROWS = dict(w_in_a=(0, 768), w_in_b=(3, 256))
W_IN_PIECES = tuple(W_IN_ROWS)
REDUCED = W_IN_PIECES + BIG[1:]


class _Exchange:
    def __init__(self, place, slots, conv_w_shard):
        self.place, self.slots, self.conv_w_shard = place, dict(slots), conv_w_shard
        self.conv_w = None
        self.g, self.from_sibling, self.pair_sums, self.arrived = {}, {}, {}, {}
        self.pending = []

    def weight(self, name):
        if name == "conv_w":
            return self.conv_w
        w = self.slots[name]
        return w.reshape(-1, D_MODEL) if name in ("w_out", "w_down") else w

    def grad(self, name, g):
        self.g[name] = g

    def carry(self, point):
        plans = []
        self.pending = SCHEDULE.get(point, [])
        for kind, names in self.pending:
            if kind in ("gather_ici", "gather_ici_cw"):
                wholes = [self.conv_w_shard] if kind == "gather_ici_cw" else []
                plans.append(_gather_ici_plan([self.slots[n] for n in names], wholes))
            elif kind == "gather_pass":
                plans.append(_gather_pass_plan([self.slots[n] for n in names]))
            elif kind == "pair":
                plans.append(_pair_plan([self.g[n] for n in names]))
            else:
                for n in names:
                    self.pair_sums[n] = _pair_sum(self.g[n], self.from_sibling[n], self.place[1:2], f"pair_sum_{n}")
                plans.append(_chip_plan([self.pair_sums[n] for n in names]))
        return plans

    def done(self, point, carried):
        for (kind, names), got in zip(self.pending, carried):
            if kind in ("gather_ici", "gather_ici_cw", "gather_pass"):
                self.slots.update(zip(names, got))
                if kind == "gather_ici_cw":
                    self.conv_w = got[len(names)].transpose(1, 0, 2).reshape(3, 2 * D_FF)
            elif kind == "pair":
                self.from_sibling.update(zip(names, got))
            else:
                self.arrived.update(zip(names, got))

    def reduced_halves(self):
        return [_chip_sum(self.arrived[n], self.pair_sums[n], self.place, f"chip_sum_{n}") for n in REDUCED]


def kernel(x, pre_mix_norm, w_in, rel_bias, hgrn_lb_raw, hgrn_norm, w_branch_attn, w_branch_hgrn, w_out, post_mix_norm, pre_ffn_norm, w_up, conv_w, conv_b, w_down, post_ffn_norm, loss_target, m_pre_mix_norm, m_w_in, m_rel_bias, m_hgrn_lb_raw, m_hgrn_norm, m_w_branch_attn, m_w_branch_hgrn, m_w_out, m_post_mix_norm, m_pre_ffn_norm, m_w_up, m_conv_w, m_conv_b, m_w_down, m_post_ffn_norm, v_pre_mix_norm, v_w_in, v_rel_bias, v_hgrn_lb_raw, v_hgrn_norm, v_w_branch_attn, v_w_branch_hgrn, v_w_out, v_post_mix_norm, v_pre_ffn_norm, v_w_up, v_conv_w, v_conv_b, v_w_down, v_post_ffn_norm):
    w = dict(pre_mix_norm=pre_mix_norm, w_in=w_in, rel_bias=rel_bias, hgrn_lb_raw=hgrn_lb_raw, hgrn_norm=hgrn_norm,
             w_branch_attn=w_branch_attn, w_branch_hgrn=w_branch_hgrn, w_out=w_out, post_mix_norm=post_mix_norm,
             pre_ffn_norm=pre_ffn_norm, w_up=w_up, conv_w=conv_w, conv_b=conv_b, w_down=w_down,
             post_ffn_norm=post_ffn_norm)
    m = dict(pre_mix_norm=m_pre_mix_norm, w_in=m_w_in, rel_bias=m_rel_bias, hgrn_lb_raw=m_hgrn_lb_raw,
             hgrn_norm=m_hgrn_norm, w_branch_attn=m_w_branch_attn, w_branch_hgrn=m_w_branch_hgrn, w_out=m_w_out,
             post_mix_norm=m_post_mix_norm, pre_ffn_norm=m_pre_ffn_norm, w_up=m_w_up, conv_w=m_conv_w,
             conv_b=m_conv_b, w_down=m_w_down, post_ffn_norm=m_post_ffn_norm)
    v = dict(pre_mix_norm=v_pre_mix_norm, w_in=v_w_in, rel_bias=v_rel_bias, hgrn_lb_raw=v_hgrn_lb_raw,
             hgrn_norm=v_hgrn_norm, w_branch_attn=v_w_branch_attn, w_branch_hgrn=v_w_branch_hgrn, w_out=v_w_out,
             post_mix_norm=v_post_mix_norm, pre_ffn_norm=v_pre_ffn_norm, w_up=v_w_up, conv_w=v_conv_w,
             conv_b=v_conv_b, w_down=v_w_down, post_ffn_norm=v_post_ffn_norm)
    shard2d = {n: (w[n][0] if w[n].ndim == 3 else w[n]) for n in WEIGHTS}
    chip = 2 * lax.axis_index("x") + lax.axis_index("y")
    core = lax.axis_index("c")

    place = jnp.stack([chip, core]).astype(jnp.int32)
    slots = {n: _cast_into_slot(shard2d[n], place, f"cast_{n}") for n in BIG}
    slots["w_in"] = _gather_weights([slots["w_in"]], [], "gather_w_in")[0]
    ex = _Exchange(place, slots, shard2d["conv_w"])
    loss, grad_x, small = _local_step(x[0], loss_target[0], {n: w[n] for n in SMALL if n != "conv_w"}, ex)

    flat = [small[n].reshape(-1) for n in SMALL] + [loss.reshape(-1)]
    sizes = [t.shape[0] for t in flat]
    summed, wholes = _all_sum(jnp.concatenate(flat).reshape(-1, LANES), ex.reduced_halves(), "sum_small")
    summed = summed.reshape(-1)
    offs = [sum(sizes[:i]) for i in range(len(sizes))]
    grads = {}
    for n, o, sz in zip(SMALL, offs, sizes):
        grads[n] = summed[o:o + sz].reshape(small[n].shape)
    loss_total = summed[offs[-1]]
    cw = 2 * D_FF // N_CHIPS
    grads["conv_w"] = lax.dynamic_slice(grads["conv_w"], (0, chip * cw), (3, cw))

    big = dict(zip(REDUCED, wholes))
    big["w_in"] = jnp.concatenate([big.pop(n) for n in W_IN_PIECES], axis=0)
    grads.update(big)

    out_g, out_d, out_m, out_v = [], [], [], []
    for n in WEIGHTS:
        d2, m2, v2 = _adamw(shard2d[n], grads[n], m[n].reshape(shard2d[n].shape), v[n].reshape(shard2d[n].shape),
                            f"adamw_{n}")
        shape = w[n].shape
        out_g.append(grads[n].reshape(shape))
        out_d.append(d2.reshape(shape))
        out_m.append(m2.reshape(shape))
        out_v.append(v2.reshape(shape))
    return (loss_total, grad_x[None], *out_g, *out_d, *out_m, *out_v)
```

```python
import functools
import math

import jax
import jax.numpy as jnp
from jax import lax
from jax.experimental import pallas as pl
from jax.experimental.pallas import tpu as pltpu

F32 = jnp.float32
BF16 = jnp.bfloat16
MESH = pl.DeviceIdType.MESH

D_MODEL = 1024
N_GROUPS = 3
DILATIONS = (1, 4, 16)
HEADS = 8
HEAD_DIM = 64
GROUP_W = HEADS * HEAD_DIM
QKV_W = N_GROUPS * 3 * GROUP_W
BLK = 128
NEG_INF = -1e30
NUM_BUCKETS = 32
MAX_EXACT = 16
MAX_DISTANCE = 2048
HG_HEADS = 4
HG_DK = 128
HG_W = HG_HEADS * HG_DK
HG_CHUNK = 32
HG_TILE = 256
IN_W = QKV_W + 4 * HG_W + 2 * D_MODEL
D_FF = 2816
EPS = 1e-6
N_CHIPS = 4
N_DEV = 8
LANES = 128

ADAM_LR, ADAM_B1, ADAM_B2, ADAM_EPS, ADAM_WD, ADAM_STEP = 0.001, 0.9, 0.999, 1e-08, 0.01, 10

VMEM_LIMIT = 56 * 1024 * 1024


def _cp(n_axes):
    return pltpu.CompilerParams(dimension_semantics=("arbitrary",) * n_axes, vmem_limit_bytes=VMEM_LIMIT)


def _sds(shape, dtype):
    return jax.ShapeDtypeStruct(tuple(shape), dtype)


def _sigmoid(v):
    return 1.0 / (1.0 + jnp.exp(-v))


def _bf(v):
    return v.astype(BF16)


def _dot(a, b, dims):
    return lax.dot_general(a, b, (dims, ((), ())), preferred_element_type=F32)


NN = ((1,), (0,))
NT = ((1,), (1,))
TN = ((0,), (0,))

ANY = pl.BlockSpec(memory_space=pl.ANY)


class _Plan:
    def __init__(self, copies, n_sems, ins=(), inouts=(), outs=()):
        self.copies, self.n_sems = copies, n_sems
        self.ins, self.inouts, self.outs = list(ins), list(inouts), list(outs)


def _call(body, plans=None, *, name, grid, in_specs, out_specs, out_shape, args, scratch_shapes=()):
    plans = list(plans or ())
    in_specs, out_specs, out_shape = list(in_specs), list(out_specs), list(out_shape)
    scratch_shapes = list(scratch_shapes)
    n_in, n_out, n_scr = len(in_specs), len(out_specs), len(scratch_shapes)
    x_in, x_out, aliases, spans = [], [], {}, []
    for p in plans:
        i0, o0 = len(x_in), len(x_out)
        x_in += p.ins
        for a in p.inouts:
            aliases[n_in + len(x_in)] = n_out + len(x_out)
            x_in.append(a)
            x_out.append(_sds(a.shape, a.dtype))
        x_out += p.outs
        spans.append((i0, len(p.ins), o0, len(p.inouts), len(p.outs)))
    sems = [pltpu.SemaphoreType.DMA((p.n_sems,)) for p in plans for _ in range(3)]

    def wrapped(*refs):
        xi = refs[n_in:n_in + len(x_in)]
        base = n_in + len(x_in)
        xo = refs[base + n_out:base + n_out + len(x_out)]
        sbase = base + n_out + len(x_out)
        xs = refs[sbase + n_scr:]
        ids = [pl.program_id(k) for k in range(len(grid))]
        first = functools.reduce(jnp.logical_and, [i == 0 for i in ids])
        last = functools.reduce(jnp.logical_and, [i == g - 1 for i, g in zip(ids, grid)])

        def descriptors(k):
            i0, ni, o0, nio, no = spans[k]
            return plans[k].copies(xi[i0:i0 + ni], xo[o0:o0 + nio], xo[o0 + nio:o0 + nio + no], *xs[3 * k:3 * k + 3])

        @pl.when(first)
        def _():
            for k in range(len(plans)):
                sends, _, local = descriptors(k)
                for cp in (*sends, *local):
                    cp.start()

        body(*refs[:n_in], *refs[base:base + n_out], *refs[sbase:sbase + n_scr])

        @pl.when(last)
        def _():
            for k in range(len(plans)):
                sends, recvs, local = descriptors(k)
                for cp in recvs:
                    cp.wait_recv()
                for cp in sends:
                    cp.wait_send()
                for cp in local:
                    cp.wait()

    res = pl.pallas_call(
        wrapped if plans else body, name=name, grid=grid, in_specs=in_specs + [ANY] * len(x_in),
        out_specs=out_specs + [ANY] * len(x_out), out_shape=out_shape + x_out, input_output_aliases=aliases,
        scratch_shapes=scratch_shapes + sems, compiler_params=_cp(len(grid)))(*args, *x_in)
    res = list(res)
    carried = [res[n_out + o0:n_out + o0 + nio + no] for (_, _, o0, nio, no) in spans]
    return res[:n_out], carried


def _mm_nn_blk(a, wg, name, tm=512, plans=None):
    M, K = a.shape
    nb, _, Nb = wg.shape

    def body(a_ref, w_ref, o_ref):
        o_ref[...] = _dot(_bf(a_ref[...]), w_ref[...], NN)

    (out,), carried = _call(
        body, plans, name=name, grid=(nb, M // tm),
        in_specs=[pl.BlockSpec((tm, K), lambda j, i: (i, 0)), pl.BlockSpec((None, K, Nb), lambda j, i: (j, 0, 0))],
        out_specs=[pl.BlockSpec((tm, Nb), lambda j, i: (i, j))],
        out_shape=[_sds((M, nb * Nb), F32)], args=(a, wg))
    return out if plans is None else (out, carried)


def _mm_nt_blk(dy, wg, name, tm=1024, plans=None):
    M = dy.shape[0]
    nb, K, Nb = wg.shape

    def body(dy_ref, w_ref, o_ref):
        j = pl.program_id(1)
        r = _dot(_bf(dy_ref[...]), w_ref[...], NT)

        @pl.when(j == 0)
        def _():
            o_ref[...] = r

        @pl.when(j > 0)
        def _():
            o_ref[...] += r

    (out,), carried = _call(
        body, plans, name=name, grid=(M // tm, nb),
        in_specs=[pl.BlockSpec((tm, Nb), lambda i, j: (i, j)), pl.BlockSpec((None, K, Nb), lambda i, j: (j, 0, 0))],
        out_specs=[pl.BlockSpec((tm, K), lambda i, j: (i, 0))],
        out_shape=[_sds((M, K), F32)], args=(dy, wg))
    return out if plans is None else (out, carried)


def _mm_tn_blk(x, dy, nb, name, tk=2048, x_cols=None, plans=None, together=False):
    T, Mx = x.shape
    xk, Mx = (0, Mx) if x_cols is None else x_cols
    Nb = dy.shape[1] // nb
    nj = nb if together else 1

    def body(x_ref, dy_ref, o_ref):
        t = pl.program_id(1)
        r = _dot(_bf(x_ref[...]), _bf(dy_ref[...]), TN)
        for j in range(nj):
            rj = r[:, j * Nb:(j + 1) * Nb]

            @pl.when(t == 0)
            def _():
                o_ref[j] = rj

            @pl.when(t > 0)
            def _():
                o_ref[j] += rj

    (out,), carried = _call(
        body, plans, name=name, grid=(nb // nj, T // tk),
        in_specs=[pl.BlockSpec((tk, Mx), lambda j, t: (t, xk)), pl.BlockSpec((tk, nj * Nb), lambda j, t: (t, j))],
        out_specs=[pl.BlockSpec((nj, Mx, Nb), lambda j, t: (j, 0, 0))],
        out_shape=[_sds((nb, Mx, Nb), F32)], args=(x, dy))
    return out if plans is None else (out, carried)


def _mm_tn(x, dy, name, tk=1024):
    T, Mx = x.shape
    N = dy.shape[1]

    def body(x_ref, dy_ref, o_ref):
        t = pl.program_id(0)
        r = _dot(_bf(x_ref[...]), _bf(dy_ref[...]), TN)

        @pl.when(t == 0)
        def _():
            o_ref[...] = r

        @pl.when(t > 0)
        def _():
            o_ref[...] += r

    return pl.pallas_call(
        body, name=name, grid=(T // tk,),
        in_specs=[pl.BlockSpec((tk, Mx), lambda t: (t, 0)), pl.BlockSpec((tk, N), lambda t: (t, 0))],
        out_specs=pl.BlockSpec((Mx, N), lambda t: (0, 0)),
        out_shape=_sds((Mx, N), F32), compiler_params=_cp(1))(x, dy)


def _tile(arr, bw, col=lambda c: 0):
    return ("tile", arr, bw, col)


def _full(arr):
    return ("full", arr)


def _out_tile(width, dtype, bw, col=lambda c: 0):
    return ("tile", width, dtype, bw, col)


def _out_acc(rows, width, bw, col=lambda c: 0):
    return ("acc", rows, width, bw, col)


def _rows_call(name, body, n_rows, tm, ncol, ins, outs, plans=None):
    in_specs, args = [], []
    for e in ins:
        if e[0] == "tile":
            _, arr, bw, col = e
            in_specs.append(pl.BlockSpec((tm, bw), functools.partial(lambda c, i, col: (i, col(c)), col=col)))
        else:
            arr = e[1]
            in_specs.append(pl.BlockSpec(arr.shape, functools.partial(lambda c, i, nd: (0,) * nd, nd=arr.ndim)))
        args.append(arr)
    out_specs, out_shape = [], []
    for e in outs:
        if e[0] == "tile":
            _, width, dtype, bw, col = e
            out_specs.append(pl.BlockSpec((tm, bw), functools.partial(lambda c, i, col: (i, col(c)), col=col)))
            out_shape.append(_sds((n_rows, width), dtype))
        else:
            _, rows, width, bw, col = e
            out_specs.append(pl.BlockSpec((rows, bw), functools.partial(lambda c, i, col: (0, col(c)), col=col)))
            out_shape.append(_sds((rows, width), F32))
    out, carried = _call(body, plans, name=name, grid=(ncol, n_rows // tm), in_specs=in_specs, out_specs=out_specs,
                         out_shape=out_shape, args=args)
    return out if plans is None else (out, carried)


def _acc(ref, val):
    i = pl.program_id(1)

    @pl.when(i == 0)
    def _():
        ref[...] = val

    @pl.when(i > 0)
    def _():
        ref[...] += val


def _rinv(z):
    return lax.rsqrt(jnp.mean(z * z, axis=-1, keepdims=True) + EPS)


def _norm_bwd(dy, zhat, r, w):
    dyw = dy * w
    return r * (dyw - zhat * jnp.mean(dyw * zhat, axis=-1, keepdims=True))


def _norm_fwd(x, w, name):
    def body(x_ref, w_ref, h_ref):
        xv = x_ref[...]
        h_ref[...] = _bf(xv * _rinv(xv) * w_ref[...])

    return _rows_call(name, body, x.shape[0], 512, 1, [_tile(x, D_MODEL), _full(w)],
                      [_out_tile(D_MODEL, BF16, D_MODEL)])[0]


def _prenorm_bwd(dh, xin, w, dres, name, plans=None):
    def body(dh_ref, x_ref, w_ref, dres_ref, dx_ref, dw_ref):
        xv = x_ref[...]
        r = _rinv(xv)
        xhat = xv * r
        dhv = dh_ref[...]
        dx_ref[...] = dres_ref[...] + _norm_bwd(dhv, xhat, r, w_ref[...])
        _acc(dw_ref, jnp.sum(dhv * xhat, axis=0, keepdims=True))

    return _rows_call(name, body, xin.shape[0], 512, 1,
                      [_tile(dh, D_MODEL), _tile(xin, D_MODEL), _full(w), _tile(dres, D_MODEL)],
                      [_out_tile(D_MODEL, F32, D_MODEL), _out_acc(1, D_MODEL, D_MODEL)], plans)


def _postnorm_bwd(dout, z, w, w_mat, name):
    def body(do_ref, z_ref, w_ref, wm_ref, dz_ref, dm_ref, dw_ref):
        zv = z_ref[...]
        r = _rinv(zv)
        zhat = zv * r
        dov = do_ref[...]
        dz = _bf(_norm_bwd(dov, zhat, r, w_ref[...]))
        dz_ref[...] = dz
        dm_ref[...] = _dot(dz, wm_ref[...], NT)
        _acc(dw_ref, jnp.sum(dov * zhat, axis=0, keepdims=True))

    return _rows_call(name, body, z.shape[0], 512, 1,
                      [_tile(dout, D_MODEL), _tile(z, D_MODEL), _full(w), _full(w_mat)],
                      [_out_tile(D_MODEL, BF16, D_MODEL), _out_tile(D_MODEL, F32, D_MODEL),
                       _out_acc(1, D_MODEL, D_MODEL)])


def _t5_bucket(dist):
    n = jnp.maximum(dist, 0)
    nf = jnp.maximum(n, 1).astype(F32)
    large = MAX_EXACT + (jnp.log(nf / MAX_EXACT) / math.log(MAX_DISTANCE / MAX_EXACT)
                         * (NUM_BUCKETS - MAX_EXACT)).astype(jnp.int32)
    large = jnp.minimum(large, NUM_BUCKETS - 1)
    return jnp.where(n < MAX_EXACT, n, large)


def _band_rel():
    return jnp.arange(BLK)[:, None] + BLK - jnp.arange(2 * BLK)[None, :]


def _band_valid():
    rel = _band_rel()
    window = (rel >= 0) & (rel <= BLK)
    first = window & (jnp.arange(2 * BLK)[None, :] >= BLK)
    return jnp.stack([first, window]).astype(F32).reshape(2, 1, BAND)


RES_UNROLL = 4
PAIR = LANES // HEAD_DIM


def _pair_lanes():
    first = lax.broadcasted_iota(jnp.int32, (1, LANES), 1) < HEAD_DIM
    return first, jnp.logical_not(first)


def _heads_per_step(d):
    return HEADS if d == 1 else LANES // HEAD_DIM


def _sub_rows(r, d):
    return pl.ds(r, BLK, stride=d) if d > 1 else pl.ds(0, BLK)


def _for_residues(d, fn):
    if d <= RES_UNROLL:
        for r in range(d):
            fn(r)
    else:
        def group(i, carry):
            for k in range(RES_UNROLL):
                fn(i * RES_UNROLL + k)
            return carry

        lax.fori_loop(0, d // RES_UNROLL, group, 0)


def _attn_specs(d, g, qblock):
    cw = _heads_per_step(d) * HEAD_DIM

    def col(part, hp):
        return (g * 3 + part) * (GROUP_W // cw) + hp

    def cur(part):
        return pl.BlockSpec((d * BLK, cw), lambda hp, n: (qblock(n), col(part, hp)))

    def prev(part):
        return pl.BlockSpec((d * BLK, cw), lambda hp, n: (jnp.maximum(qblock(n) - 1, 0), col(part, hp)))

    return cur, prev


def _attn_fwd(proj, bias, g, name, plans=None):
    S = proj.shape[0]
    d = DILATIONS[g]
    NB = S // (d * BLK)
    hps = _heads_per_step(d)

    def body(q_ref, kp_ref, kc_ref, vp_ref, vc_ref, b_ref, o_ref, lse_ref):
        hp = pl.program_id(0)
        later = jnp.minimum(pl.program_id(1), 1)

        def residue(r):
            rows = _sub_rows(r, d)
            q2 = q_ref[rows, :]
            k2 = jnp.concatenate([kp_ref[rows, :], kc_ref[rows, :]], axis=0)
            v2 = jnp.concatenate([vp_ref[rows, :], vc_ref[rows, :]], axis=0)
            outs, lses = [], []
            for pp in range(hps // PAIR):
                ps = slice(pp * LANES, (pp + 1) * LANES)
                qp, kp, vp = _bf(q2[:, ps]), _bf(k2[:, ps]), _bf(v2[:, ps])
                o_h, lse_h = [], []
                for hh, own in enumerate(_pair_lanes()):
                    s = _dot(qp, jnp.where(own, kp, 0), NT) * (HEAD_DIM ** -0.5) + b_ref[later, hp * hps + pp * PAIR + hh]
                    m = jnp.max(s, axis=-1, keepdims=True)
                    p = jnp.exp(s - m)
                    l = jnp.sum(p, axis=-1, keepdims=True)
                    o_h.append(_dot(_bf(p), vp, NN) / l)
                    lse_h.append(m + jnp.log(l))
                first = _pair_lanes()[0]
                outs.append(jnp.where(first, o_h[0], o_h[1]))
                lses.append(jnp.where(first, lse_h[0], lse_h[1]))
            o_ref[rows, :] = outs[0] if len(outs) == 1 else jnp.concatenate(outs, axis=1)
            lse_ref[rows, :] = lses[0] if len(lses) == 1 else jnp.concatenate(lses, axis=1)

        _for_residues(d, residue)

    cur, prev = _attn_specs(d, g, lambda n: n)
    out = pl.BlockSpec((d * BLK, hps * HEAD_DIM), lambda hp, n: (n, hp))
    res, carried = _call(
        body, plans, name=name, grid=(HEADS // hps, NB),
        in_specs=[cur(0), prev(1), cur(1), prev(2), cur(2),
                  pl.BlockSpec((2, HEADS, BLK, 2 * BLK), lambda hp, n: (0, 0, 0, 0))],
        out_specs=[out, out], out_shape=[_sds((S, GROUP_W), F32)] * 2,
        args=(proj, proj, proj, proj, proj, bias))
    return res if plans is None else (res, carried)


def _attn_bwd(proj, bias, lse, y, dy, g, name, plans=None):
    S = proj.shape[0]
    d = DILATIONS[g]
    NB = S // (d * BLK)
    hps = _heads_per_step(d)

    def body(q_ref, kp_ref, kc_ref, vp_ref, vc_ref, b_ref, l_ref, y_ref, dy_ref,
             dq_ref, dk_ref, dv_ref, db_ref, ck_ref, cv_ref):
        hp, n = pl.program_id(0), pl.program_id(1)

        @pl.when((hp == 0) & (n == 0))
        def _():
            db_ref[...] = jnp.zeros_like(db_ref)

        @pl.when(n == 0)
        def _():
            ck_ref[...] = jnp.zeros_like(ck_ref)
            cv_ref[...] = jnp.zeros_like(cv_ref)

        @pl.when(n < NB)
        def _():
            later = jnp.minimum(n, 1)

            def residue(r):
                rows = _sub_rows(r, d)
                q2 = q_ref[rows, :]
                k2 = jnp.concatenate([kp_ref[rows, :], kc_ref[rows, :]], axis=0)
                v2 = jnp.concatenate([vp_ref[rows, :], vc_ref[rows, :]], axis=0)
                l2, y2, dy2 = l_ref[rows, :], y_ref[rows, :], dy_ref[rows, :]
                dqs, dks, dvs = [], [], []
                for pp in range(hps // PAIR):
                    ps = slice(pp * LANES, (pp + 1) * LANES)
                    qp, kp, vp = _bf(q2[:, ps]), _bf(k2[:, ps]), _bf(v2[:, ps])
                    dyp, yp = dy2[:, ps], y2[:, ps]
                    dq_h, dk_h, dv_h = [], [], []
                    for hh, own in enumerate(_pair_lanes()):
                        head = hp * hps + pp * PAIR + hh
                        s = _dot(qp, jnp.where(own, kp, 0), NT) * (HEAD_DIM ** -0.5) + b_ref[later, head]
                        p = jnp.exp(s - l2[:, pp * LANES + hh * HEAD_DIM:pp * LANES + hh * HEAD_DIM + 1])
                        dyh = jnp.where(own, dyp, 0.0)
                        delta = jnp.sum(dyh * yp, axis=-1, keepdims=True)
                        ds = p * (_dot(_bf(dyh), vp, NT) - delta)
                        db_ref[head] += ds
                        dsb = _bf(ds * (HEAD_DIM ** -0.5))
                        dq_h.append(_dot(dsb, kp, NN))
                        dk_h.append(_dot(dsb, qp, TN))
                        dv_h.append(_dot(_bf(p), _bf(dyp), TN))
                    first = _pair_lanes()[0]
                    dqs.append(jnp.where(first, dq_h[0], dq_h[1]))
                    dks.append(jnp.where(first, dk_h[0], dk_h[1]))
                    dvs.append(jnp.where(first, dv_h[0], dv_h[1]))
                dkb = dks[0] if len(dks) == 1 else jnp.concatenate(dks, axis=1)
                dvb = dvs[0] if len(dvs) == 1 else jnp.concatenate(dvs, axis=1)
                dq_ref[rows, :] = dqs[0] if len(dqs) == 1 else jnp.concatenate(dqs, axis=1)
                dk_ref[rows, :] = ck_ref[rows, :] + dkb[:BLK]
                dv_ref[rows, :] = cv_ref[rows, :] + dvb[:BLK]
                ck_ref[rows, :] = dkb[BLK:]
                cv_ref[rows, :] = dvb[BLK:]

            _for_residues(d, residue)

        @pl.when(n == NB)
        def _():
            dk_ref[...] = ck_ref[...]
            dv_ref[...] = cv_ref[...]

    def qn(n):
        return jnp.minimum(n, NB - 1)

    cur, prev = _attn_specs(d, g, qn)
    cw = hps * HEAD_DIM
    row = pl.BlockSpec((d * BLK, cw), lambda hp, n: (qn(n), hp))
    done = pl.BlockSpec((d * BLK, cw), lambda hp, n: (jnp.maximum(n - 1, 0), hp))
    (dq, dk, dv, db), carried = _call(
        body, plans, name=name, grid=(HEADS // hps, NB + 1),
        in_specs=[cur(0), prev(1), cur(1), prev(2), cur(2),
                  pl.BlockSpec((2, HEADS, BLK, 2 * BLK), lambda hp, n: (0, 0, 0, 0)), row, row, row],
        out_specs=[row, done, done, pl.BlockSpec((HEADS, BLK, 2 * BLK), lambda hp, n: (0, 0, 0))],
        out_shape=[_sds((S, GROUP_W), F32)] * 3 + [_sds((HEADS, BLK, 2 * BLK), F32)],
        scratch_shapes=[pltpu.VMEM((d * BLK, cw), F32)] * 2,
        args=(proj, proj, proj, proj, proj, bias, lse, y, dy))
    return ([dq, dk, dv], db) if plans is None else ([dq, dk, dv], db, carried)


BAND = BLK * 2 * BLK


def _bucket_onehot():
    buckets = jnp.stack([_t5_bucket(_band_rel() * d) for d in DILATIONS]).reshape(N_GROUPS, 1, BAND)
    return (buckets == jnp.arange(NUM_BUCKETS).reshape(1, NUM_BUCKETS, 1)).astype(F32)


def _relbias_fwd(rel_bias, name):
    table = rel_bias.reshape(NUM_BUCKETS, N_GROUPS, HEADS).transpose(1, 0, 2)

    def body(t_ref, oh_ref, valid_ref, o_ref):
        bias = lax.dot_general(t_ref[...], oh_ref[...], (TN, ((), ())), preferred_element_type=F32,
                               precision=lax.Precision.HIGHEST)
        for k in range(2):
            o_ref[k] = jnp.where(valid_ref[k] > 0.5, bias, NEG_INF)

    out = pl.pallas_call(
        body, name=name, grid=(N_GROUPS,),
        in_specs=[pl.BlockSpec((None, NUM_BUCKETS, HEADS), lambda g: (g, 0, 0)),
                  pl.BlockSpec((None, NUM_BUCKETS, BAND), lambda g: (g, 0, 0)),
                  pl.BlockSpec((2, 1, BAND), lambda g: (0, 0, 0))],
        out_specs=pl.BlockSpec((None, 2, HEADS, BAND), lambda g: (g, 0, 0, 0)),
        out_shape=_sds((N_GROUPS, 2, HEADS, BAND), F32), compiler_params=_cp(1))(table, _bucket_onehot(), _band_valid())
    return out.reshape(N_GROUPS, 2, HEADS, BLK, 2 * BLK)


def _relbias_bwd(dbs, name):
    band = BAND
    onehot = _bucket_onehot()
    dbf = jnp.stack([db.reshape(HEADS, band) for db in dbs])

    def body(oh_ref, db_ref, o_ref):
        o_ref[...] = lax.dot_general(oh_ref[...], db_ref[...], (NT, ((), ())), preferred_element_type=F32,
                                     precision=lax.Precision.HIGHEST)

    out = pl.pallas_call(
        body, name=name, grid=(N_GROUPS,),
        in_specs=[pl.BlockSpec((None, NUM_BUCKETS, band), lambda g: (g, 0, 0)),
                  pl.BlockSpec((None, HEADS, band), lambda g: (g, 0, 0))],
        out_specs=pl.BlockSpec((None, NUM_BUCKETS, HEADS), lambda g: (g, 0, 0)),
        out_shape=_sds((N_GROUPS, NUM_BUCKETS, HEADS), F32), compiler_params=_cp(1))(onehot, dbf)
    return out.transpose(1, 0, 2).reshape(NUM_BUCKETS, N_GROUPS * HEADS)


def _chunk_pos(shape):
    return lax.broadcasted_iota(jnp.int32, shape, 0) % HG_CHUNK


def _chunk_cumsum(v):
    pos = _chunk_pos(v.shape)
    s = 1
    while s < HG_CHUNK:
        v = v + jnp.where(pos >= s, pltpu.roll(v, s, 0), 0.0)
        s *= 2
    return v


def _chunk_rev_cumsum(v):
    pos = _chunk_pos(v.shape)
    n = v.shape[0]
    s = 1
    while s < HG_CHUNK:
        v = v + jnp.where(pos < HG_CHUNK - s, pltpu.roll(v, n - s, 0), 0.0)
        s *= 2
    return v


def _lower_bound(raw):
    a0, a1 = raw[0:1], raw[1:2]
    m = jnp.maximum(a0, a1)
    e0, e1 = jnp.exp(a0 - m), jnp.exp(a1 - m)
    return e0 / (e0 + e1)


def _hg_gates(qr, fr, lb):
    sf = _sigmoid(fr)
    f = lb + (1.0 - lb) * sf
    sq = _sigmoid(qr)
    return qr * sq, sq, f, sf


HG_COL0 = QKV_W // HG_W


def _hgrn_fwd(proj, lb_raw, nw, name, plans=None):
    S = proj.shape[0]
    ncs = HG_TILE // HG_CHUNK
    tril = jnp.tril(jnp.ones((HG_CHUNK, HG_CHUNK), dtype=bool))

    def body(q_ref, f_ref, i_ref, og_ref, lb_ref, nw_ref, y_ref, o_ref, st_ref, state):
        @pl.when(pl.program_id(0) == 0)
        def _():
            state[...] = jnp.zeros_like(state)

        lb = _lower_bound(lb_ref[...])
        q, _, f, _ = _hg_gates(q_ref[...], f_ref[...], lb)
        k = 1.0 - f
        G = _chunk_cumsum(jnp.log(f))
        row = lax.broadcasted_iota(jnp.int32, (HG_CHUNK, HG_CHUNK), 0)
        col = lax.broadcasted_iota(jnp.int32, (HG_CHUNK, HG_CHUNK), 1)
        heads = [slice(h * HG_DK, (h + 1) * HG_DK) for h in range(HG_HEADS)]
        sts = [state[h] for h in range(HG_HEADS)]
        for c in range(ncs):
            cs = slice(c * HG_CHUNK, (c + 1) * HG_CHUNK)
            for h, hs in enumerate(heads):
                Gc = G[cs, hs]
                gl = Gc[HG_CHUNK - 1:HG_CHUNK]
                qt = _bf(q[cs, hs] * jnp.exp(Gc))
                kt = _bf(k[cs, hs] * jnp.exp(-Gc))
                kd = _bf(k[cs, hs] * jnp.exp(gl - Gc))
                v = _bf(i_ref[cs, hs])
                A = jnp.where(row >= col, _dot(qt, kt, NT), 0.0)
                o_ref[cs, hs] = _dot(_bf(A), v, NN) + _dot(qt, _bf(sts[h]), NT)
                st_ref[c, h] = sts[h]
                sts[h] = sts[h] * jnp.exp(gl) + _dot(v, kd, TN)
        for h, hs in enumerate(heads):
            state[h] = sts[h]
            oh = o_ref[:, hs]
            og = og_ref[:, hs]
            y_ref[:, hs] = oh * _rinv(oh) * nw_ref[...] * (og * _sigmoid(og))

    def colspec(j):
        return pl.BlockSpec((HG_TILE, HG_W), lambda i: (i, HG_COL0 + j))

    res, carried = _call(
        body, plans, name=name, grid=(S // HG_TILE,),
        in_specs=[colspec(0), colspec(1), colspec(2), colspec(3),
                  pl.BlockSpec((2, HG_W), lambda i: (0, 0)), pl.BlockSpec((1, HG_DK), lambda i: (0, 0))],
        out_specs=[pl.BlockSpec((HG_TILE, HG_W), lambda i: (i, 0))] * 2
        + [pl.BlockSpec((ncs, HG_HEADS, HG_DK, HG_DK), lambda i: (i, 0, 0, 0))],
        out_shape=[_sds((S, HG_W), F32)] * 2 + [_sds((S // HG_CHUNK, HG_HEADS, HG_DK, HG_DK), F32)],
        scratch_shapes=[pltpu.VMEM((HG_HEADS, HG_DK, HG_DK), F32)],
        args=(proj, proj, proj, proj, lb_raw, nw))
    return res if plans is None else (res, carried)


def _hgrn_bwd(proj, lb_raw, nw, o, states, dy, d_attn, d_gates, name):
    S = proj.shape[0]
    ncs = HG_TILE // HG_CHUNK
    nt = S // HG_TILE
    n_a, n_g = len(d_attn), len(d_gates)
    own = [slice(QKV_W + j * HG_W, QKV_W + (j + 1) * HG_W) for j in range(4)]

    def body(q_ref, f_ref, i_ref, og_ref, lb_ref, nw_ref, o_ref, st_ref, dy_ref, *rest):
        attn_refs, gate_refs = rest[:n_a], rest[n_a:n_a + n_g]
        dp_ref, dlb_ref, dnw_ref, dstate, do_s, dG_s, dgl_s, dk_s, dlb_s = rest[n_a + n_g:]
        dq_ref, df_ref, di_ref, dog_ref = (dp_ref.at[:, cols] for cols in own)
        step = pl.program_id(0)
        for k, a_ref in enumerate(attn_refs):
            dp_ref[:, k * GROUP_W:(k + 1) * GROUP_W] = _bf(a_ref[...])
        for k, g_ref in enumerate(gate_refs):
            dp_ref[:, QKV_W + 4 * HG_W + k * D_MODEL:QKV_W + 4 * HG_W + (k + 1) * D_MODEL] = g_ref[...]

        @pl.when(step == 0)
        def _():
            dstate[...] = jnp.zeros_like(dstate)
            dlb_s[...] = jnp.zeros_like(dlb_s)
            dnw_ref[...] = jnp.zeros_like(dnw_ref)

        lb = _lower_bound(lb_ref[...])
        qr = q_ref[...]
        q, sq, f, sf = _hg_gates(qr, f_ref[...], lb)
        k = 1.0 - f
        G = _chunk_cumsum(jnp.log(f))
        nwv = nw_ref[...]
        row = lax.broadcasted_iota(jnp.int32, (HG_CHUNK, HG_CHUNK), 0)
        col = lax.broadcasted_iota(jnp.int32, (HG_CHUNK, HG_CHUNK), 1)
        for h in range(HG_HEADS):
            hs = slice(h * HG_DK, (h + 1) * HG_DK)
            oh = o_ref[:, hs]
            r = _rinv(oh)
            ohat = oh * r
            og = og_ref[:, hs]
            sg = _sigmoid(og)
            dyh = dy_ref[:, hs]
            don = dyh * (og * sg)
            dog_ref[:, hs] = _bf(dyh * (ohat * nwv) * (sg * (1.0 + og * (1.0 - sg))))
            dnw_ref[...] += jnp.sum(don * ohat, axis=0, keepdims=True)
            do_s[:, hs] = _norm_bwd(don, ohat, r, nwv)
        dsts = [dstate[h] for h in range(HG_HEADS)]
        for c in reversed(range(ncs)):
            cs = slice(c * HG_CHUNK, (c + 1) * HG_CHUNK)
            for h in range(HG_HEADS):
                hs = slice(h * HG_DK, (h + 1) * HG_DK)
                dst = dsts[h]
                Gc = G[cs, hs]
                gl = Gc[HG_CHUNK - 1:HG_CHUNK]
                eG, enG, edG, egl = jnp.exp(Gc), jnp.exp(-Gc), jnp.exp(gl - Gc), jnp.exp(gl)
                qt, kt, kd = q[cs, hs] * eG, k[cs, hs] * enG, k[cs, hs] * edG
                qtb, ktb, kdb = _bf(qt), _bf(kt), _bf(kd)
                v = _bf(i_ref[cs, hs])
                do = _bf(do_s[cs, hs])
                st = st_ref[c, h]
                dstb = _bf(dst)
                A = jnp.where(row >= col, _dot(qtb, ktb, NT), 0.0)
                dA = _bf(jnp.where(row >= col, _dot(do, v, NT), 0.0))
                di_ref[cs, hs] = _bf(_dot(_bf(A), do, TN) + _dot(kdb, dstb, NT))
                dqt = _dot(dA, ktb, NN) + _dot(do, _bf(st), NN)
                dkt = _dot(dA, qtb, TN)
                dkd = _dot(v, dstb, NN)
                dgl = egl * jnp.sum(st * dst, axis=0, keepdims=True) + jnp.sum(dkd * kd, axis=0, keepdims=True)
                dsts[h] = dst * egl + _dot(do, qtb, TN)
                dq_ref[cs, hs] = _bf(dqt * eG * (sq[cs, hs] * (1.0 + qr[cs, hs] * (1.0 - sq[cs, hs]))))
                dk_s[cs, hs] = dkt * enG + dkd * edG
                dG_s[cs, hs] = dqt * qt - dkt * kt - dkd * kd
                dgl_s[cs, hs] = jnp.broadcast_to(dgl, (HG_CHUNK, HG_DK))
        for h in range(HG_HEADS):
            dstate[h] = dsts[h]
        dg = _chunk_rev_cumsum(dG_s[...]) + dgl_s[...]
        dfv = dg / f - dk_s[...]
        df_ref[...] = _bf(dfv * (1.0 - lb) * sf * (1.0 - sf))
        dlb_s[...] += jnp.sum(dfv * (1.0 - sf), axis=0, keepdims=True)

        @pl.when(step == nt - 1)
        def _():
            t = dlb_s[...] * lb * (1.0 - lb)
            dlb_ref[...] = jnp.concatenate([t, -t], axis=0)

    def colspec(j):
        return pl.BlockSpec((HG_TILE, HG_W), lambda i: (nt - 1 - i, HG_COL0 + j))

    def rows(width):
        return pl.BlockSpec((HG_TILE, width), lambda i: (nt - 1 - i, 0))

    tile = rows(HG_W)
    return pl.pallas_call(
        body, name=name, grid=(nt,),
        in_specs=[colspec(0), colspec(1), colspec(2), colspec(3),
                  pl.BlockSpec((2, HG_W), lambda i: (0, 0)), pl.BlockSpec((1, HG_DK), lambda i: (0, 0)),
                  tile, pl.BlockSpec((ncs, HG_HEADS, HG_DK, HG_DK), lambda i: (nt - 1 - i, 0, 0, 0)), tile]
        + [rows(GROUP_W)] * n_a + [rows(D_MODEL)] * n_g,
        out_specs=[rows(IN_W), pl.BlockSpec((2, HG_W), lambda i: (0, 0)), pl.BlockSpec((1, HG_DK), lambda i: (0, 0))],
        out_shape=[_sds((S, IN_W), BF16), _sds((2, HG_W), F32), _sds((1, HG_DK), F32)],
        scratch_shapes=[pltpu.VMEM((HG_HEADS, HG_DK, HG_DK), F32)] + [pltpu.VMEM((HG_TILE, HG_W), F32)] * 4
        + [pltpu.VMEM((1, HG_W), F32)],
        compiler_params=_cp(1))(proj, proj, proj, proj, lb_raw, nw, o, states, dy, *d_attn, *d_gates)


GATE_COL0 = (QKV_W + 4 * HG_W) // GROUP_W
HALF_D = D_MODEL // 2


def _gate_tiles(proj):
    return [_tile(proj, HALF_D, functools.partial(lambda c, k: GATE_COL0 + k, k=k)) for k in range(4)]


def _gates(g_refs):
    s0 = _sigmoid(jnp.concatenate([g_refs[0][...], g_refs[1][...]], axis=1))
    s1 = _sigmoid(jnp.concatenate([g_refs[2][...], g_refs[3][...]], axis=1))
    return s0, s1


def _branch_fwd(os_, lses, yh, proj, w_a, w_h, name):
    nb = w_a.shape[0]

    def body(o0, o1, o2, l0, l1, l2, yh_ref, g0a, g0b, g1a, g1b, wa_ref, wh_ref,
             y_ref, lse_ref, za_ref, zh_ref, m_ref):
        a, b, c = l0[...], l1[...], l2[...]
        m = jnp.maximum(jnp.maximum(a, b), c)
        ea, eb, ec = jnp.exp(a - m), jnp.exp(b - m), jnp.exp(c - m)
        den = ea + eb + ec
        y = (ea * o0[...] + eb * o1[...] + ec * o2[...]) / den
        y_ref[...] = y
        lse_ref[...] = m + jnp.log(den)
        yb, yhb = _bf(y), _bf(yh_ref[...])
        za = jnp.concatenate([_dot(yb, wa_ref[j], NN) for j in range(nb)], axis=1)
        zh = jnp.concatenate([_dot(yhb, wh_ref[j], NN) for j in range(nb)], axis=1)
        s0, s1 = _gates((g0a, g0b, g1a, g1b))
        za_ref[...] = za
        zh_ref[...] = zh
        m_ref[...] = _bf(s0 * za + s1 * zh)

    return _rows_call(name, body, yh.shape[0], 512, 1,
                      [*[_tile(t, GROUP_W) for t in (*os_, *lses)], _tile(yh, HG_W), *_gate_tiles(proj),
                       _full(w_a), _full(w_h)],
                      [_out_tile(GROUP_W, F32, GROUP_W)] * 2 + [_out_tile(D_MODEL, F32, D_MODEL)] * 2
                      + [_out_tile(D_MODEL, BF16, D_MODEL)])


def _branch_bwd(dm, za, zh, proj, w_a, w_h, name, plans=None):
    nb, _, Nb = w_a.shape

    def body(dm_ref, za_ref, zh_ref, g0a, g0b, g1a, g1b, wa_ref, wh_ref,
             dza_ref, dzh_ref, dg0_ref, dg1_ref, dy_ref, dyh_ref):
        dmv = dm_ref[...]
        s0, s1 = _gates((g0a, g0b, g1a, g1b))
        dza, dzh = _bf(dmv * s0), _bf(dmv * s1)
        dza_ref[...] = dza
        dzh_ref[...] = dzh
        dg0_ref[...] = _bf(dmv * za_ref[...] * s0 * (1.0 - s0))
        dg1_ref[...] = _bf(dmv * zh_ref[...] * s1 * (1.0 - s1))
        dy_ref[...] = sum(_dot(dza[:, j * Nb:(j + 1) * Nb], wa_ref[j], NT) for j in range(nb))
        dyh_ref[...] = sum(_dot(dzh[:, j * Nb:(j + 1) * Nb], wh_ref[j], NT) for j in range(nb))

    return _rows_call(name, body, za.shape[0], 512, 1,
                      [_tile(dm, D_MODEL), _tile(za, D_MODEL), _tile(zh, D_MODEL), *_gate_tiles(proj),
                       _full(w_a), _full(w_h)],
                      [_out_tile(D_MODEL, BF16, D_MODEL)] * 4 + [_out_tile(GROUP_W, F32, GROUP_W),
                                                                 _out_tile(HG_W, F32, HG_W)], plans)


def _mix_out(merged, w_out, x, w_post, w_pre, name):
    def body(m_ref, wo_ref, x_ref, wp_ref, wf_ref, mo_ref, x1_ref, h2_ref):
        z = _dot(m_ref[...], wo_ref[...], NN)
        mo_ref[...] = z
        x1 = x_ref[...] + z * _rinv(z) * wp_ref[...]
        x1_ref[...] = x1
        h2_ref[...] = _bf(x1 * _rinv(x1) * wf_ref[...])

    return _rows_call(name, body, x.shape[0], 512, 1,
                      [_tile(merged, D_MODEL), _full(w_out), _tile(x, D_MODEL), _full(w_post), _full(w_pre)],
                      [_out_tile(D_MODEL, F32, D_MODEL), _out_tile(D_MODEL, F32, D_MODEL),
                       _out_tile(D_MODEL, BF16, D_MODEL)])


def _loss_head(a, w_down, x1, tgt, w, name):
    def body(a_ref, wd_ref, x1_ref, t_ref, w_ref, dx_ref, df_ref, dw_ref, loss_ref):
        z = _dot(a_ref[...], wd_ref[...], NN)
        r = _rinv(z)
        zhat = z * r
        wv = w_ref[...]
        e = x1_ref[...] + zhat * wv - t_ref[...]
        dx = e * (1.0 / D_MODEL)
        dx_ref[...] = dx
        df_ref[...] = _bf(_norm_bwd(dx, zhat, r, wv))
        _acc(dw_ref, jnp.sum(dx * zhat, axis=0, keepdims=True))
        part = 0.5 * jnp.sum(jnp.sum(e * e, axis=1, keepdims=True), axis=0, keepdims=True) * (1.0 / D_MODEL)
        _acc(loss_ref, jnp.broadcast_to(part, (1, LANES)))

    return _rows_call(name, body, x1.shape[0], 512, 1,
                      [_tile(a, D_FF), _full(w_down), _tile(x1, D_MODEL), _tile(tgt, D_MODEL), _full(w)],
                      [_out_tile(D_MODEL, F32, D_MODEL), _out_tile(D_MODEL, BF16, D_MODEL),
                       _out_acc(1, D_MODEL, D_MODEL), _out_acc(1, LANES, LANES)])


CONV_CB = D_FF // 2
CONV_TM = 512
HALO = 8
SQRT_HALF = 0.7071067811865476
INV_SQRT_2PI = 0.3989422804014327


CONV_RS = 32


def _lane_tiles():
    return [slice(k * LANES, (k + 1) * LANES) for k in range(CONV_CB // LANES)]


def _strip_start(i):
    return pl.multiple_of(i * CONV_RS, CONV_RS)


def _strip_taps(u_ref, halo_ref, r0, cs, first_strip, first_tile):
    if first_strip:
        before = jnp.where(first_tile, 0.0, halo_ref[:, cs])
        blk = jnp.concatenate([before, u_ref[0:CONV_RS, cs]], axis=0)
    else:
        blk = u_ref[pl.ds(pl.multiple_of(r0 - HALO, HALO), CONV_RS + HALO), cs]
    return pltpu.roll(blk, 2, 0)[HALO:], pltpu.roll(blk, 1, 0)[HALO:], blk[HALO:]


def _conv(taps, w_ref, b_ref, cs):
    return b_ref[:, cs] + w_ref[0:1, cs] * taps[0] + w_ref[1:2, cs] * taps[1] + w_ref[2:3, cs] * taps[2]


def _conv_specs(tm):
    nh = tm // HALO
    nc = D_FF // CONV_CB

    def tile(off):
        return pl.BlockSpec((tm, CONV_CB), lambda c, i: (i, off + c))

    def halo(off):
        return pl.BlockSpec((HALO, CONV_CB), lambda c, i: (jnp.maximum(i * nh - 1, 0), off + c))

    def small(rows, off):
        return pl.BlockSpec((rows, CONV_CB), lambda c, i: (0, off + c))

    return nc, tile, halo, small


def _conv_gelu_fwd(u, cw, cb, name):
    S = u.shape[0]
    tm = CONV_TM
    nc, tile, halo, small = _conv_specs(tm)

    def body(ug, hg, uv, hv, wg, wv, bg, bv, a_ref):
        first_tile = pl.program_id(1) == 0

        def strip(r0, first_strip):
            for cs in _lane_tiles():
                cg = _conv(_strip_taps(ug, hg, r0, cs, first_strip, first_tile), wg, bg, cs)
                cv = _conv(_strip_taps(uv, hv, r0, cs, first_strip, first_tile), wv, bv, cs)
                a_ref[pl.ds(r0, CONV_RS), cs] = _bf(0.5 * cg * (1.0 + lax.erf(cg * SQRT_HALF)) * cv)

        strip(0, True)
        lax.fori_loop(1, tm // CONV_RS, lambda k, c: (strip(_strip_start(k), False), c)[1], 0)

    return pl.pallas_call(
        body, name=name, grid=(nc, S // tm),
        in_specs=[tile(0), halo(0), tile(nc), halo(nc), small(3, 0), small(3, nc), small(1, 0), small(1, nc)],
        out_specs=tile(0), out_shape=_sds((S, D_FF), BF16), compiler_params=_cp(2))(u, u, u, u, cw, cw, cb, cb)


def _conv_gelu_bwd(u, dff, w_down, cw, cb, name, plans=None):
    S = u.shape[0]
    tm = CONV_TM
    nt = S // tm
    nc, tile, halo, small = _conv_specs(tm)

    def body(ug, hg, uv, hv, wg, wv, bg, bv, dff_ref, wd_ref, dcg_ref, dcv_ref, dwg_ref, dwv_ref, dbg_ref, dbv_ref,
             acc, da_ref):
        i = pl.program_id(1)
        first_tile = i == 0
        da_ref[...] = _dot(dff_ref[...], wd_ref[...], NT)

        @pl.when(first_tile)
        def _():
            acc[...] = jnp.zeros_like(acc)

        def strip(r0, first_strip):
            rows = pl.ds(r0, CONV_RS)
            for cs in _lane_tiles():
                tg = _strip_taps(ug, hg, r0, cs, first_strip, first_tile)
                tv = _strip_taps(uv, hv, r0, cs, first_strip, first_tile)
                cg = _conv(tg, wg, bg, cs)
                cv = _conv(tv, wv, bv, cs)
                phi = 0.5 * (1.0 + lax.erf(cg * SQRT_HALF))
                dav = da_ref[rows, cs]
                dcg = dav * cv * (phi + cg * jnp.exp(-0.5 * cg * cg) * INV_SQRT_2PI)
                dcv = dav * (cg * phi)
                dcg_ref[rows, cs] = dcg
                dcv_ref[rows, cs] = dcv
                for half, (dc, taps) in enumerate(((dcg, tg), (dcv, tv))):
                    for j in range(3):
                        acc[4 * half + j, :, cs] += dc * taps[j]
                    acc[4 * half + 3, :, cs] += dc

        strip(0, True)
        lax.fori_loop(1, tm // CONV_RS, lambda k, c: (strip(_strip_start(k), False), c)[1], 0)

        @pl.when(i == nt - 1)
        def _():
            for half, (dw_ref, db_ref) in enumerate(((dwg_ref, dbg_ref), (dwv_ref, dbv_ref))):
                for j in range(3):
                    dw_ref[j:j + 1, :] = jnp.sum(acc[4 * half + j], axis=0, keepdims=True)
                db_ref[...] = jnp.sum(acc[4 * half + 3], axis=0, keepdims=True)

    res, carried = _call(
        body, plans, name=name, grid=(nc, nt),
        in_specs=[tile(0), halo(0), tile(nc), halo(nc), small(3, 0), small(3, nc), small(1, 0), small(1, nc),
                  pl.BlockSpec((tm, D_MODEL), lambda c, i: (i, 0)), pl.BlockSpec((CONV_CB, D_MODEL), lambda c, i: (c, 0))],
        out_specs=[tile(0), tile(0), small(3, 0), small(3, 0), small(1, 0), small(1, 0)],
        out_shape=[_sds((S, D_FF), F32)] * 2 + [_sds((3, D_FF), F32)] * 2 + [_sds((1, D_FF), F32)] * 2,
        scratch_shapes=[pltpu.VMEM((8, CONV_RS, CONV_CB), F32), pltpu.VMEM((tm, CONV_CB), F32)],
        args=(u, u, u, u, cw, cw, cb, cb, dff, w_down))
    return res if plans is None else (res, carried)


def _conv_input_bwd(dcg, dcv, cw, name, plans=None):
    S = dcg.shape[0]
    tm = CONV_TM // 2
    nh = tm // HALO
    nt = S // tm
    n = CONV_RS + HALO
    tile = pl.BlockSpec((tm, D_FF), lambda i: (i, 0))
    nxt = pl.BlockSpec((HALO, D_FF), lambda i: (jnp.minimum((i + 1) * nh, S // HALO - 1), 0))

    def body(g_ref, ng_ref, v_ref, nv_ref, w_ref, du_ref):
        last_tile = pl.program_id(0) == nt - 1

        def strip(r0, last_strip):
            for half, (dc_ref, n_ref) in enumerate(((g_ref, ng_ref), (v_ref, nv_ref))):
                for k in range(D_FF // LANES):
                    cs = slice(k * LANES, (k + 1) * LANES)
                    ws = slice(half * D_FF + k * LANES, half * D_FF + (k + 1) * LANES)
                    if last_strip:
                        after = jnp.where(last_tile, 0.0, n_ref[:, cs])
                        blk = jnp.concatenate([dc_ref[tm - CONV_RS:tm, cs], after], axis=0)
                    else:
                        blk = dc_ref[pl.ds(r0, n), cs]
                    d1 = pltpu.roll(blk, n - 1, 0)[:CONV_RS]
                    d2 = pltpu.roll(blk, n - 2, 0)[:CONV_RS]
                    du_ref[pl.ds(r0, CONV_RS), ws] = _bf(w_ref[2:3, ws] * blk[:CONV_RS] + w_ref[1:2, ws] * d1
                                                         + w_ref[0:1, ws] * d2)

        lax.fori_loop(0, tm // CONV_RS - 1, lambda k, c: (strip(_strip_start(k), False), c)[1], 0)
        strip(tm - CONV_RS, True)

    (du,), carried = _call(
        body, plans, name=name, grid=(nt,),
        in_specs=[tile, nxt, tile, nxt, pl.BlockSpec((3, 2 * D_FF), lambda i: (0, 0))],
        out_specs=[pl.BlockSpec((tm, 2 * D_FF), lambda i: (i, 0))], out_shape=[_sds((S, 2 * D_FF), BF16)],
        args=(dcg, dcg, dcv, dcv, cw))
    return du if plans is None else (du, carried)


def _row_tile(n, cap):
    best = n
    for t in range(16, cap + 1, 16):
        if n % t == 0:
            best = t
    return best if best <= cap else n


def _rows_for_bytes(nbytes, cols):
    return max(16, nbytes // (4 * cols) // 16 * 16)


def _adamw(w, g, m, v, name):
    R, C = w.shape
    tr = _row_tile(R, _rows_for_bytes(2 << 20, C))

    def body(w_ref, g_ref, m_ref, v_ref, d_ref, nm_ref, nv_ref):
        gv = g_ref[...]
        nm = ADAM_B1 * m_ref[...] + (1.0 - ADAM_B1) * gv
        nv = ADAM_B2 * v_ref[...] + (1.0 - ADAM_B2) * (gv * gv)
        m_hat = nm / (1.0 - ADAM_B1 ** ADAM_STEP)
        v_hat = nv / (1.0 - ADAM_B2 ** ADAM_STEP)
        d_ref[...] = -ADAM_LR * (m_hat / (jnp.sqrt(v_hat) + ADAM_EPS) + ADAM_WD * w_ref[...])
        nm_ref[...] = nm
        nv_ref[...] = nv

    spec = pl.BlockSpec((tr, C), lambda i: (i, 0))
    return pl.pallas_call(body, name=name, grid=(R // tr,), in_specs=[spec] * 4, out_specs=[spec] * 3,
                          out_shape=[_sds((R, C), F32)] * 3, compiler_params=_cp(1))(w, g, m, v)


def _pair_sum(gfull, rcv, c_idx, name):
    nb, R, C = gfull.shape
    half = R // 2
    tr = _row_tile(half, _rows_for_bytes(2 << 20, C))
    nt = half // tr

    def body(c_ref, g_ref, r_ref, o_ref):
        o_ref[...] = _bf(g_ref[...] + r_ref[...])

    return pl.pallas_call(
        body, name=name,
        grid_spec=pltpu.PrefetchScalarGridSpec(
            num_scalar_prefetch=1, grid=(nb, nt),
            in_specs=[pl.BlockSpec((None, tr, C), lambda j, i, c_ref: (j, c_ref[0] * nt + i, 0)),
                      pl.BlockSpec((None, tr, C), lambda j, i, c_ref: (j, i, 0))],
            out_specs=pl.BlockSpec((None, tr, C), lambda j, i, c_ref: (j, i, 0))),
        out_shape=_sds((nb, half, C), BF16), compiler_params=_cp(2))(c_idx, gfull, rcv)


def _chip_sum(arrived, own, place, name):
    nb, H, C = arrived.shape
    tr = _row_tile(H, _rows_for_bytes(2 << 20, C))
    nt = H // tr

    def body(pl_ref, *refs):
        o_ref = refs[nb + 1]
        me = pl_ref[0]
        acc = None
        for k in range(nb):
            term = jnp.where(me == k, refs[nb][...], refs[k][...]).astype(F32)
            acc = term if acc is None else acc + term
        o_ref[...] = acc

    def other(k):
        return pl.BlockSpec((None, tr, C), lambda i, p: (jnp.where(p[0] == k, (k + 1) % nb, k), i, 0))

    return pl.pallas_call(
        body, name=name,
        grid_spec=pltpu.PrefetchScalarGridSpec(
            num_scalar_prefetch=1, grid=(nt,),
            in_specs=[other(k) for k in range(nb)] + [pl.BlockSpec((None, tr, C), lambda i, p: (p[0], i, 0))],
            out_specs=pl.BlockSpec((tr, C), lambda i, p: (p[1] * nt + i, 0))),
        out_shape=_sds((2 * H, C), F32), compiler_params=_cp(1))(place, *([arrived] * nb), own)


def _cast_into_slot(shard, place, name):
    R, C = shard.shape
    tr = _row_tile(R, 256)

    def body(pl_ref, s_ref, o_ref):
        o_ref[...] = _bf(s_ref[...])

    return pl.pallas_call(
        body, name=name,
        grid_spec=pltpu.PrefetchScalarGridSpec(
            num_scalar_prefetch=1, grid=(R // tr,),
            in_specs=[pl.BlockSpec((tr, C), lambda i, p: (i, 0))],
            out_specs=pl.BlockSpec((None, tr, C), lambda i, p: (p[0], i, 0))),
        out_shape=_sds((N_CHIPS, R, C), BF16), compiler_params=_cp(1))(place, shard)


def _place():
    x, y, c = lax.axis_index("x"), lax.axis_index("y"), lax.axis_index("c")
    chips = [(1 - x, y), (x, 1 - y), (1 - x, 1 - y)]
    return x, y, c, chips


def _chip_id(px, py):
    return 2 * px + py


def _remote(src, dst, send_sems, recv_sems, k, to):
    return pltpu.make_async_remote_copy(src_ref=src, dst_ref=dst, send_sem=send_sems.at[k], recv_sem=recv_sems.at[k],
                                        device_id=to, device_id_type=MESH)


def _proj_gathered(h, slot, place, name, tm=512):
    M, K = h.shape
    nb, _, Nb = slot.shape
    half = K // 2
    nt = M // tm
    cx, cy = place[0] // 2, place[0] % 2
    order = jnp.stack([place[0], _chip_id(1 - cx, cy), _chip_id(cx, 1 - cy), _chip_id(1 - cx, 1 - cy)]).astype(jnp.int32)

    def body(order_ref, h_ref, slot_in, o_ref, slot_ref, w_buf, ici_send, ici_recv, pass_send, pass_recv, load_sem):
        b, i = pl.program_id(0), pl.program_id(1)
        x, y, c, chips = _place()
        me = _chip_id(x, y)
        sib = (x, y, 1 - c)
        mine, other = pl.ds(c * half, half), pl.ds((1 - c) * half, half)

        def sent(k):
            blk = slot_ref.at[me, mine]
            return _remote(blk, blk, ici_send, ici_recv, k, (*chips[k], c))

        def landed(k):
            blk = slot_ref.at[_chip_id(*chips[k]), mine]
            return _remote(blk, blk, ici_send, ici_recv, k, (*chips[k], c))

        def passed(k, rows):
            blk = slot_ref.at[_chip_id(*chips[k]), rows]
            return _remote(blk, blk, pass_send, pass_recv, k, sib)

        @pl.when((b == 0) & (i == 0))
        def _():
            for k in range(len(chips)):
                sent(k).start()

        for k in range(len(chips)):
            @pl.when((b == k + 1) & (i == 0))
            def _(k=k):
                landed(k).wait_recv()
                passed(k, mine).start()
                passed(k, other).wait_recv()

        @pl.when(i == 0)
        def _():
            load = pltpu.make_async_copy(slot_ref.at[order_ref[b]], w_buf, load_sem.at[0])
            load.start()
            load.wait()

        o_ref[...] = _dot(h_ref[...], w_buf[...], NN)

        @pl.when((b == nb - 1) & (i == nt - 1))
        def _():
            for k in range(len(chips)):
                sent(k).wait_send()
                passed(k, mine).wait_send()

    n_peers = N_CHIPS - 1
    return pl.pallas_call(
        body, name=name,
        grid_spec=pltpu.PrefetchScalarGridSpec(
            num_scalar_prefetch=1, grid=(nb, nt),
            in_specs=[pl.BlockSpec((tm, K), lambda b, i, o: (i, 0)), ANY],
            out_specs=[pl.BlockSpec((tm, Nb), lambda b, i, o: (i, o[b])), ANY],
            scratch_shapes=[pltpu.VMEM((K, Nb), BF16)] + [pltpu.SemaphoreType.DMA((n_peers,))] * 4
            + [pltpu.SemaphoreType.DMA((1,))]),
        out_shape=[_sds((M, nb * Nb), F32), _sds(slot.shape, slot.dtype)],
        input_output_aliases={2: 1}, compiler_params=_cp(2))(order, h, slot)


def _gather_ici_plan(slots, wholes):
    ns, nw = len(slots), len(wholes)

    def copies(ins, ios, outs, send_sems, recv_sems, local_sems):
        x, y, c, chips = _place()
        me = _chip_id(x, y)
        sends, recvs = [], []
        for a in range(ns + nw):
            dst = ios[a] if a < ns else outs[a - ns]
            R = dst.shape[1]
            rows = pl.ds(c * (R // 2), R // 2) if a < ns else pl.ds(0, R)
            src = dst.at[me, rows] if a < ns else ins[a - ns]
            for j, chip in enumerate(chips):
                sends.append(_remote(src, dst.at[me, rows], send_sems, recv_sems, 3 * a + j, (*chip, c)))
                landed = dst.at[_chip_id(*chip), rows]
                recvs.append(_remote(landed, landed, send_sems, recv_sems, 3 * a + j, (*chip, c)))
        local = [pltpu.make_async_copy(ins[b], outs[b].at[me], local_sems.at[b]) for b in range(nw)]
        return sends, recvs, local

    return _Plan(copies, 3 * (ns + nw), ins=wholes, inouts=slots,
                 outs=[_sds((N_CHIPS, *s.shape), s.dtype) for s in wholes])


def _gather_pass_plan(slots):
    def copies(ins, ios, outs, send_sems, recv_sems, local_sems):
        x, y, c, chips = _place()
        sib = (x, y, 1 - c)
        sends, recvs = [], []
        for a, buf in enumerate(ios):
            half = buf.shape[1] // 2
            for j, chip in enumerate(chips):
                mine = buf.at[_chip_id(*chip), pl.ds(c * half, half)]
                other = buf.at[_chip_id(*chip), pl.ds((1 - c) * half, half)]
                sends.append(_remote(mine, mine, send_sems, recv_sems, 3 * a + j, sib))
                recvs.append(_remote(other, other, send_sems, recv_sems, 3 * a + j, sib))
        return sends, recvs, []

    return _Plan(copies, 3 * len(slots), inouts=slots)


def _pair_plan(grads):
    def copies(ins, ios, outs, send_sems, recv_sems, local_sems):
        x, y, c, _ = _place()
        sib = (x, y, 1 - c)
        sends, recvs = [], []
        for a, g in enumerate(ins):
            half = g.shape[1] // 2
            sends.append(_remote(g.at[:, pl.ds((1 - c) * half, half), :], outs[a], send_sems, recv_sems, a, sib))
            recvs.append(_remote(outs[a], outs[a], send_sems, recv_sems, a, sib))
        return sends, recvs, []

    return _Plan(copies, len(grads), ins=grads,
                 outs=[_sds((g.shape[0], g.shape[1] // 2, g.shape[2]), g.dtype) for g in grads])


def _chip_plan(parts):
    def copies(ins, ios, outs, send_sems, recv_sems, local_sems):
        x, y, c, chips = _place()
        me = _chip_id(x, y)
        sends, recvs = [], []
        for a, part in enumerate(ins):
            for j, chip in enumerate(chips):
                sends.append(_remote(part.at[_chip_id(*chip)], outs[a].at[me], send_sems, recv_sems, 3 * a + j, (*chip, c)))
                landed = outs[a].at[_chip_id(*chip)]
                recvs.append(_remote(landed, landed, send_sems, recv_sems, 3 * a + j, (*chip, c)))
        return sends, recvs, []

    return _Plan(copies, 3 * len(parts), ins=parts, outs=[_sds(p.shape, p.dtype) for p in parts])


def _all_sum(pack, fulls, name):
    R, C = pack.shape
    n = len(fulls)

    def body(p_ref, *refs):
        o_ref, halves = refs[n], refs[n + 1:2 * n + 1]
        buf, send_sems, recv_sems, pair_send, pair_recv = refs[2 * n + 1:]
        x, y, c, _ = _place()
        sib = (x, y, 1 - c)
        pair = []
        for a, full in enumerate(halves):
            H = full.shape[0] // 2
            mine = full.at[pl.ds(c * H, H)]
            cp = _remote(mine, mine, pair_send, pair_recv, a, sib)
            cp.start()
            pair.append(cp)
        me = 4 * x + 2 * y + c
        buf[me] = p_ref[...]
        cps = []
        for k in range(1, N_DEV):
            to = (x ^ (k >> 2), y ^ ((k >> 1) & 1), c ^ (k & 1))
            cp = _remote(p_ref, buf.at[me], send_sems, recv_sems, k - 1, to)
            cp.start()
            cps.append(cp)
        for k in range(1, N_DEV):
            frm = (x ^ (k >> 2), y ^ ((k >> 1) & 1), c ^ (k & 1))
            slot = buf.at[4 * frm[0] + 2 * frm[1] + frm[2]]
            _remote(slot, slot, send_sems, recv_sems, k - 1, frm).wait_recv()
        acc = buf[0]
        for k in range(1, N_DEV):
            acc = acc + buf[k]
        o_ref[...] = acc
        for cp in cps:
            cp.wait_send()
        for a, (full, cp) in enumerate(zip(halves, pair)):
            H = full.shape[0] // 2
            other = full.at[pl.ds((1 - c) * H, H)]
            _remote(other, other, pair_send, pair_recv, a, sib).wait_recv()
            cp.wait_send()

    vm = pl.BlockSpec(memory_space=pltpu.VMEM)
    res = pl.pallas_call(
        body, name=name, in_specs=[vm] + [ANY] * n, out_specs=[vm] + [ANY] * n,
        out_shape=[_sds((R, C), F32)] + [_sds(f.shape, f.dtype) for f in fulls],
        input_output_aliases={1 + a: 1 + a for a in range(n)},
        scratch_shapes=[pltpu.VMEM((N_DEV, R, C), F32), pltpu.SemaphoreType.DMA((N_DEV - 1,)),
                        pltpu.SemaphoreType.DMA((N_DEV - 1,)), pltpu.SemaphoreType.DMA((n,)),
                        pltpu.SemaphoreType.DMA((n,))])(pack, *fulls)
    return res[0], list(res[1:])


def _local_step(xs, tgt, p, ex):
    h1 = _norm_fwd(xs, p["pre_mix_norm"], "pre_mix_norm")
    proj = ex.project(h1)
    biases = _relbias_fwd(p["rel_bias"], "rel_bias_fwd")
    fw = []
    for g in range(N_GROUPS):
        res, got = _attn_fwd(proj, biases[g], g, f"attn_fwd{g}", plans=ex.carry(f"attn_fwd{g}"))
        ex.done(f"attn_fwd{g}", got)
        fw.append(res)
    (yh, o_h, states), got = _hgrn_fwd(proj, p["hgrn_lb_raw"], p["hgrn_norm"], "hgrn_fwd", plans=ex.carry("hgrn_fwd"))
    ex.done("hgrn_fwd", got)
    W_a, W_h, W_out = ex.weight("w_branch_attn"), ex.weight("w_branch_hgrn"), ex.weight("w_out")
    W_up, W_down, conv_w = ex.weight("w_up"), ex.weight("w_down"), ex.weight("conv_w")
    y, lse, za, zh, merged = _branch_fwd([t[0] for t in fw], [t[1] for t in fw], yh, proj, W_a, W_h, "branch_fwd")
    mo, x1, h2 = _mix_out(merged, W_out, xs, p["post_mix_norm"], p["pre_ffn_norm"], "mix_out")
    u = _mm_nn_blk(h2, W_up, "ffn_up")
    a = _conv_gelu_fwd(u, conv_w, p["conv_b"], "conv_gelu_fwd")
    dx2, dff, g_post_ffn, loss = _loss_head(a, W_down, x1, tgt, p["post_ffn_norm"], "ffn_down_loss")

    ex.grad("w_down", _mm_tn(a, dff, "g_w_down").reshape(N_CHIPS, D_FF // N_CHIPS, D_MODEL))
    (dcg, dcv, gwg, gwv, gbg, gbv), got = _conv_gelu_bwd(u, dff, W_down, conv_w, p["conv_b"], "conv_gelu_bwd",
                                                          plans=ex.carry("conv_gelu_bwd"))
    ex.done("conv_gelu_bwd", got)
    g_conv_w = jnp.concatenate([gwg, gwv], axis=1)
    g_conv_b = jnp.concatenate([gbg, gbv], axis=1)
    du, got = _conv_input_bwd(dcg, dcv, conv_w, "conv_input_bwd", plans=ex.carry("conv_input_bwd"))
    ex.done("conv_input_bwd", got)
    dh2 = _mm_nt_blk(du, W_up, "d_ffn_in")
    ex.grad("w_up", _mm_tn_blk(h2, du, N_CHIPS, "g_w_up"))
    dx1, g_pre_ffn = _prenorm_bwd(dh2, x1, p["pre_ffn_norm"], dx2, "pre_ffn_norm_bwd")
    dmo, dmerged, g_post_mix = _postnorm_bwd(dx1, mo, p["post_mix_norm"], W_out, "post_mix_norm_bwd")
    ex.grad("w_out", _mm_tn(merged, dmo, "g_w_out").reshape(N_CHIPS, D_MODEL // N_CHIPS, D_MODEL))
    (dza, dzh, dg0, dg1, dy, dyh), got = _branch_bwd(dmerged, za, zh, proj, W_a, W_h, "branch_bwd",
                                                     plans=ex.carry("branch_bwd"))
    ex.done("branch_bwd", got)
    ex.grad("w_branch_attn", _mm_tn_blk(y, dza, N_CHIPS, "g_w_branch_attn", together=True))
    ex.grad("w_branch_hgrn", _mm_tn_blk(yh, dzh, N_CHIPS, "g_w_branch_hgrn", together=True))
    dqkv, dbs = [], []
    for g in range(N_GROUPS):
        parts, db, got = _attn_bwd(proj, biases[g], lse, y, dy, g, f"attn_bwd{g}", plans=ex.carry(f"attn_bwd{g}"))
        ex.done(f"attn_bwd{g}", got)
        dqkv += parts
        dbs.append(db)
    g_rel_bias = _relbias_bwd(dbs, "rel_bias_bwd")
    dproj, g_lb_raw, g_hgrn_norm = _hgrn_bwd(proj, p["hgrn_lb_raw"], p["hgrn_norm"], o_h, states, dyh, dqkv,
                                             [dg0, dg1], "hgrn_bwd")
    for piece in W_IN_PIECES:
        g, got = _mm_tn_blk(h1, dproj, N_CHIPS, f"g_{piece}", x_cols=W_IN_ROWS[piece],
                            plans=ex.carry(f"g_{piece}"))
        ex.done(f"g_{piece}", got)
        ex.grad(piece, g)
    dh1, got = _mm_nt_blk(dproj, ex.weight("w_in"), "d_proj_in", plans=ex.carry("d_proj_in"))
    ex.done("d_proj_in", got)
    (grad_x, g_pre_mix), got = _prenorm_bwd(dh1, xs, p["pre_mix_norm"], dx1, "pre_mix_norm_bwd",
                                            plans=ex.carry("pre_mix_norm_bwd"))
    ex.done("pre_mix_norm_bwd", got)
    small = dict(pre_mix_norm=g_pre_mix, rel_bias=g_rel_bias, hgrn_lb_raw=g_lb_raw, hgrn_norm=g_hgrn_norm,
                 post_mix_norm=g_post_mix, pre_ffn_norm=g_pre_ffn, conv_w=g_conv_w, conv_b=g_conv_b,
                 post_ffn_norm=g_post_ffn)
    return loss, grad_x, small


SMALL = ("pre_mix_norm", "rel_bias", "hgrn_lb_raw", "hgrn_norm", "post_mix_norm", "pre_ffn_norm", "conv_w", "conv_b",
         "post_ffn_norm")
BIG = ("w_in", "w_up", "w_down", "w_out", "w_branch_attn", "w_branch_hgrn")
WEIGHTS = ("pre_mix_norm", "w_in", "rel_bias", "hgrn_lb_raw", "hgrn_norm", "w_branch_attn", "w_branch_hgrn", "w_out",
           "post_mix_norm", "pre_ffn_norm", "w_up", "conv_w", "conv_b", "w_down", "post_ffn_norm")
MIXER = ("w_out", "w_branch_attn", "w_branch_hgrn")

SCHEDULE = {
    "attn_fwd0": [("gather_ici_cw", ("w_up",))],
    "attn_fwd1": [("gather_pass", ("w_up",)), ("gather_ici", MIXER)],
    "attn_fwd2": [("gather_pass", MIXER), ("gather_ici", ("w_down",))],
    "hgrn_fwd": [("gather_pass", ("w_down",))],
    "conv_gelu_bwd": [("pair", ("w_down",))],
    "conv_input_bwd": [("chip", ("w_down",))],
    "branch_bwd": [("pair", ("w_up",))],
    "attn_bwd0": [("chip", ("w_up",)), ("pair", MIXER)],
    "attn_bwd1": [("chip", MIXER)],
    "g_w_in_b": [("pair", ("w_in_a",))],
    "d_proj_in": [("chip", ("w_in_a",)), ("pair", ("w_in_b",))],
    "pre_mix_norm_bwd": [("chip", ("w_in_b",))],
}
W_IN_ROWS = dict(w_in_a=(0, 768), w_in_b=(3, 256))
W_IN_PIECES = tuple(W_IN_ROWS)
REDUCED = W_IN_PIECES + BIG[1:]


class _Exchange:
    def __init__(self, place, slots, conv_w_shard):
        self.place, self.slots, self.conv_w_shard = place, dict(slots), conv_w_shard
        self.conv_w = None
        self.g, self.from_sibling, self.pair_sums, self.arrived = {}, {}, {}, {}
        self.pending = []

    def weight(self, name):
        if name == "conv_w":
            return self.conv_w
        w = self.slots[name]
        return w.reshape(-1, D_MODEL) if name in ("w_out", "w_down") else w

    def project(self, h):
        proj, self.slots["w_in"] = _proj_gathered(h, self.slots["w_in"], self.place, "proj_in")
        return proj

    def grad(self, name, g):
        self.g[name] = g

    def carry(self, point):
        plans = []
        self.pending = SCHEDULE.get(point, [])
        for kind, names in self.pending:
            if kind in ("gather_ici", "gather_ici_cw"):
                wholes = [self.conv_w_shard] if kind == "gather_ici_cw" else []
                plans.append(_gather_ici_plan([self.slots[n] for n in names], wholes))
            elif kind == "gather_pass":
                plans.append(_gather_pass_plan([self.slots[n] for n in names]))
            elif kind == "pair":
                plans.append(_pair_plan([self.g[n] for n in names]))
            else:
                for n in names:
                    self.pair_sums[n] = _pair_sum(self.g[n], self.from_sibling[n], self.place[1:2], f"pair_sum_{n}")
                plans.append(_chip_plan([self.pair_sums[n] for n in names]))
        return plans

    def done(self, point, carried):
        for (kind, names), got in zip(self.pending, carried):
            if kind in ("gather_ici", "gather_ici_cw", "gather_pass"):
                self.slots.update(zip(names, got))
                if kind == "gather_ici_cw":
                    self.conv_w = got[len(names)].transpose(1, 0, 2).reshape(3, 2 * D_FF)
            elif kind == "pair":
                self.from_sibling.update(zip(names, got))
            else:
                self.arrived.update(zip(names, got))

    def reduced_halves(self):
        return [_chip_sum(self.arrived[n], self.pair_sums[n], self.place, f"chip_sum_{n}") for n in REDUCED]


def kernel(x, pre_mix_norm, w_in, rel_bias, hgrn_lb_raw, hgrn_norm, w_branch_attn, w_branch_hgrn, w_out, post_mix_norm, pre_ffn_norm, w_up, conv_w, conv_b, w_down, post_ffn_norm, loss_target, m_pre_mix_norm, m_w_in, m_rel_bias, m_hgrn_lb_raw, m_hgrn_norm, m_w_branch_attn, m_w_branch_hgrn, m_w_out, m_post_mix_norm, m_pre_ffn_norm, m_w_up, m_conv_w, m_conv_b, m_w_down, m_post_ffn_norm, v_pre_mix_norm, v_w_in, v_rel_bias, v_hgrn_lb_raw, v_hgrn_norm, v_w_branch_attn, v_w_branch_hgrn, v_w_out, v_post_mix_norm, v_pre_ffn_norm, v_w_up, v_conv_w, v_conv_b, v_w_down, v_post_ffn_norm):
    w = dict(pre_mix_norm=pre_mix_norm, w_in=w_in, rel_bias=rel_bias, hgrn_lb_raw=hgrn_lb_raw, hgrn_norm=hgrn_norm,
             w_branch_attn=w_branch_attn, w_branch_hgrn=w_branch_hgrn, w_out=w_out, post_mix_norm=post_mix_norm,
             pre_ffn_norm=pre_ffn_norm, w_up=w_up, conv_w=conv_w, conv_b=conv_b, w_down=w_down,
             post_ffn_norm=post_ffn_norm)
    m = dict(pre_mix_norm=m_pre_mix_norm, w_in=m_w_in, rel_bias=m_rel_bias, hgrn_lb_raw=m_hgrn_lb_raw,
             hgrn_norm=m_hgrn_norm, w_branch_attn=m_w_branch_attn, w_branch_hgrn=m_w_branch_hgrn, w_out=m_w_out,
             post_mix_norm=m_post_mix_norm, pre_ffn_norm=m_pre_ffn_norm, w_up=m_w_up, conv_w=m_conv_w,
             conv_b=m_conv_b, w_down=m_w_down, post_ffn_norm=m_post_ffn_norm)
    v = dict(pre_mix_norm=v_pre_mix_norm, w_in=v_w_in, rel_bias=v_rel_bias, hgrn_lb_raw=v_hgrn_lb_raw,
             hgrn_norm=v_hgrn_norm, w_branch_attn=v_w_branch_attn, w_branch_hgrn=v_w_branch_hgrn, w_out=v_w_out,
             post_mix_norm=v_post_mix_norm, pre_ffn_norm=v_pre_ffn_norm, w_up=v_w_up, conv_w=v_conv_w,
             conv_b=v_conv_b, w_down=v_w_down, post_ffn_norm=v_post_ffn_norm)
    shard2d = {n: (w[n][0] if w[n].ndim == 3 else w[n]) for n in WEIGHTS}
    chip = 2 * lax.axis_index("x") + lax.axis_index("y")
    core = lax.axis_index("c")

    place = jnp.stack([chip, core]).astype(jnp.int32)
    slots = {n: _cast_into_slot(shard2d[n], place, f"cast_{n}") for n in BIG}
    ex = _Exchange(place, slots, shard2d["conv_w"])
    loss, grad_x, small = _local_step(x[0], loss_target[0], {n: w[n] for n in SMALL if n != "conv_w"}, ex)

    flat = [small[n].reshape(-1) for n in SMALL] + [loss.reshape(-1)]
    sizes = [t.shape[0] for t in flat]
    summed, wholes = _all_sum(jnp.concatenate(flat).reshape(-1, LANES), ex.reduced_halves(), "sum_small")
    summed = summed.reshape(-1)
    offs = [sum(sizes[:i]) for i in range(len(sizes))]
    grads = {}
    for n, o, sz in zip(SMALL, offs, sizes):
        grads[n] = summed[o:o + sz].reshape(small[n].shape)
    loss_total = summed[offs[-1]]
    cw = 2 * D_FF // N_CHIPS
    grads["conv_w"] = lax.dynamic_slice(grads["conv_w"], (0, chip * cw), (3, cw))

    big = dict(zip(REDUCED, wholes))
    big["w_in"] = jnp.concatenate([big.pop(n) for n in W_IN_PIECES], axis=0)
    grads.update(big)

    out_g, out_d, out_m, out_v = [], [], [], []
    for n in WEIGHTS:
        d2, m2, v2 = _adamw(shard2d[n], grads[n], m[n].reshape(shard2d[n].shape), v[n].reshape(shard2d[n].shape),
                            f"adamw_{n}")
        shape = w[n].shape
        out_g.append(grads[n].reshape(shape))
        out_d.append(d2.reshape(shape))
        out_m.append(m2.reshape(shape))
        out_v.append(v2.reshape(shape))
    return (loss_total, grad_x[None], *out_g, *out_d, *out_m, *out_v)
```

```python
import functools
import math

import jax
import jax.numpy as jnp
from jax import lax
from jax.experimental import pallas as pl
from jax.experimental.pallas import tpu as pltpu

F32 = jnp.float32
BF16 = jnp.bfloat16
MESH = pl.DeviceIdType.MESH

D_MODEL = 1024
N_GROUPS = 3
DILATIONS = (1, 4, 16)
HEADS = 8
HEAD_DIM = 64
GROUP_W = HEADS * HEAD_DIM
QKV_W = N_GROUPS * 3 * GROUP_W
BLK = 128
NEG_INF = -1e30
NUM_BUCKETS = 32
MAX_EXACT = 16
MAX_DISTANCE = 2048
HG_HEADS = 4
HG_DK = 128
HG_W = HG_HEADS * HG_DK
HG_CHUNK = 32
HG_TILE = 256
IN_W = QKV_W + 4 * HG_W + 2 * D_MODEL
D_FF = 2816
EPS = 1e-6
N_CHIPS = 4
N_DEV = 8
LANES = 128

ADAM_LR, ADAM_B1, ADAM_B2, ADAM_EPS, ADAM_WD, ADAM_STEP = 0.001, 0.9, 0.999, 1e-08, 0.01, 10

VMEM_LIMIT = 56 * 1024 * 1024


def _cp(n_axes):
    return pltpu.CompilerParams(dimension_semantics=("arbitrary",) * n_axes, vmem_limit_bytes=VMEM_LIMIT)


def _sds(shape, dtype):
    return jax.ShapeDtypeStruct(tuple(shape), dtype)


def _sigmoid(v):
    return 1.0 / (1.0 + jnp.exp(-v))


def _bf(v):
    return v.astype(BF16)


def _dot(a, b, dims):
    return lax.dot_general(a, b, (dims, ((), ())), preferred_element_type=F32)


NN = ((1,), (0,))
NT = ((1,), (1,))
TN = ((0,), (0,))

ANY = pl.BlockSpec(memory_space=pl.ANY)


class _Plan:
    def __init__(self, copies, n_sems, ins=(), inouts=(), outs=()):
        self.copies, self.n_sems = copies, n_sems
        self.ins, self.inouts, self.outs = list(ins), list(inouts), list(outs)


def _call(body, plans=None, *, name, grid, in_specs, out_specs, out_shape, args, scratch_shapes=()):
    plans = list(plans or ())
    in_specs, out_specs, out_shape = list(in_specs), list(out_specs), list(out_shape)
    scratch_shapes = list(scratch_shapes)
    n_in, n_out, n_scr = len(in_specs), len(out_specs), len(scratch_shapes)
    x_in, x_out, aliases, spans = [], [], {}, []
    for p in plans:
        i0, o0 = len(x_in), len(x_out)
        x_in += p.ins
        for a in p.inouts:
            aliases[n_in + len(x_in)] = n_out + len(x_out)
            x_in.append(a)
            x_out.append(_sds(a.shape, a.dtype))
        x_out += p.outs
        spans.append((i0, len(p.ins), o0, len(p.inouts), len(p.outs)))
    sems = [pltpu.SemaphoreType.DMA((p.n_sems,)) for p in plans for _ in range(3)]

    def wrapped(*refs):
        xi = refs[n_in:n_in + len(x_in)]
        base = n_in + len(x_in)
        xo = refs[base + n_out:base + n_out + len(x_out)]
        sbase = base + n_out + len(x_out)
        xs = refs[sbase + n_scr:]
        ids = [pl.program_id(k) for k in range(len(grid))]
        first = functools.reduce(jnp.logical_and, [i == 0 for i in ids])
        last = functools.reduce(jnp.logical_and, [i == g - 1 for i, g in zip(ids, grid)])

        def descriptors(k):
            i0, ni, o0, nio, no = spans[k]
            return plans[k].copies(xi[i0:i0 + ni], xo[o0:o0 + nio], xo[o0 + nio:o0 + nio + no], *xs[3 * k:3 * k + 3])

        @pl.when(first)
        def _():
            for k in range(len(plans)):
                sends, _, local = descriptors(k)
                for cp in (*sends, *local):
                    cp.start()

        body(*refs[:n_in], *refs[base:base + n_out], *refs[sbase:sbase + n_scr])

        @pl.when(last)
        def _():
            for k in range(len(plans)):
                sends, recvs, local = descriptors(k)
                for cp in recvs:
                    cp.wait_recv()
                for cp in sends:
                    cp.wait_send()
                for cp in local:
                    cp.wait()

    res = pl.pallas_call(
        wrapped if plans else body, name=name, grid=grid, in_specs=in_specs + [ANY] * len(x_in),
        out_specs=out_specs + [ANY] * len(x_out), out_shape=out_shape + x_out, input_output_aliases=aliases,
        scratch_shapes=scratch_shapes + sems, compiler_params=_cp(len(grid)))(*args, *x_in)
    res = list(res)
    carried = [res[n_out + o0:n_out + o0 + nio + no] for (_, _, o0, nio, no) in spans]
    return res[:n_out], carried


def _mm_nn_blk(a, wg, name, tm=512, plans=None):
    M, K = a.shape
    nb, _, Nb = wg.shape

    def body(a_ref, w_ref, o_ref):
        o_ref[...] = _dot(_bf(a_ref[...]), w_ref[...], NN)

    (out,), carried = _call(
        body, plans, name=name, grid=(nb, M // tm),
        in_specs=[pl.BlockSpec((tm, K), lambda j, i: (i, 0)), pl.BlockSpec((None, K, Nb), lambda j, i: (j, 0, 0))],
        out_specs=[pl.BlockSpec((tm, Nb), lambda j, i: (i, j))],
        out_shape=[_sds((M, nb * Nb), F32)], args=(a, wg))
    return out if plans is None else (out, carried)


def _mm_nt_blk(dy, wg, name, tm=1024, plans=None):
    M = dy.shape[0]
    nb, K, Nb = wg.shape

    def body(dy_ref, w_ref, o_ref):
        j = pl.program_id(1)
        r = _dot(_bf(dy_ref[...]), w_ref[...], NT)

        @pl.when(j == 0)
        def _():
            o_ref[...] = r

        @pl.when(j > 0)
        def _():
            o_ref[...] += r

    (out,), carried = _call(
        body, plans, name=name, grid=(M // tm, nb),
        in_specs=[pl.BlockSpec((tm, Nb), lambda i, j: (i, j)), pl.BlockSpec((None, K, Nb), lambda i, j: (j, 0, 0))],
        out_specs=[pl.BlockSpec((tm, K), lambda i, j: (i, 0))],
        out_shape=[_sds((M, K), F32)], args=(dy, wg))
    return out if plans is None else (out, carried)


def _mm_tn_blk(x, dy, nb, name, tk=2048, x_cols=None, plans=None, together=False):
    T, Mx = x.shape
    xk, Mx = (0, Mx) if x_cols is None else x_cols
    Nb = dy.shape[1] // nb
    nj = nb if together else 1

    def body(x_ref, dy_ref, o_ref):
        t = pl.program_id(1)
        r = _dot(_bf(x_ref[...]), _bf(dy_ref[...]), TN)
        for j in range(nj):
            rj = r[:, j * Nb:(j + 1) * Nb]

            @pl.when(t == 0)
            def _():
                o_ref[j] = rj

            @pl.when(t > 0)
            def _():
                o_ref[j] += rj

    (out,), carried = _call(
        body, plans, name=name, grid=(nb // nj, T // tk),
        in_specs=[pl.BlockSpec((tk, Mx), lambda j, t: (t, xk)), pl.BlockSpec((tk, nj * Nb), lambda j, t: (t, j))],
        out_specs=[pl.BlockSpec((nj, Mx, Nb), lambda j, t: (j, 0, 0))],
        out_shape=[_sds((nb, Mx, Nb), F32)], args=(x, dy))
    return out if plans is None else (out, carried)


def _mm_tn(x, dy, name, tk=1024):
    T, Mx = x.shape
    N = dy.shape[1]

    def body(x_ref, dy_ref, o_ref):
        t = pl.program_id(0)
        r = _dot(_bf(x_ref[...]), _bf(dy_ref[...]), TN)

        @pl.when(t == 0)
        def _():
            o_ref[...] = r

        @pl.when(t > 0)
        def _():
            o_ref[...] += r

    return pl.pallas_call(
        body, name=name, grid=(T // tk,),
        in_specs=[pl.BlockSpec((tk, Mx), lambda t: (t, 0)), pl.BlockSpec((tk, N), lambda t: (t, 0))],
        out_specs=pl.BlockSpec((Mx, N), lambda t: (0, 0)),
        out_shape=_sds((Mx, N), F32), compiler_params=_cp(1))(x, dy)


def _tile(arr, bw, col=lambda c: 0):
    return ("tile", arr, bw, col)


def _full(arr):
    return ("full", arr)


def _out_tile(width, dtype, bw, col=lambda c: 0):
    return ("tile", width, dtype, bw, col)


def _out_acc(rows, width, bw, col=lambda c: 0):
    return ("acc", rows, width, bw, col)


def _rows_call(name, body, n_rows, tm, ncol, ins, outs, plans=None):
    in_specs, args = [], []
    for e in ins:
        if e[0] == "tile":
            _, arr, bw, col = e
            in_specs.append(pl.BlockSpec((tm, bw), functools.partial(lambda c, i, col: (i, col(c)), col=col)))
        else:
            arr = e[1]
            in_specs.append(pl.BlockSpec(arr.shape, functools.partial(lambda c, i, nd: (0,) * nd, nd=arr.ndim)))
        args.append(arr)
    out_specs, out_shape = [], []
    for e in outs:
        if e[0] == "tile":
            _, width, dtype, bw, col = e
            out_specs.append(pl.BlockSpec((tm, bw), functools.partial(lambda c, i, col: (i, col(c)), col=col)))
            out_shape.append(_sds((n_rows, width), dtype))
        else:
            _, rows, width, bw, col = e
            out_specs.append(pl.BlockSpec((rows, bw), functools.partial(lambda c, i, col: (0, col(c)), col=col)))
            out_shape.append(_sds((rows, width), F32))
    out, carried = _call(body, plans, name=name, grid=(ncol, n_rows // tm), in_specs=in_specs, out_specs=out_specs,
                         out_shape=out_shape, args=args)
    return out if plans is None else (out, carried)


def _acc(ref, val):
    i = pl.program_id(1)

    @pl.when(i == 0)
    def _():
        ref[...] = val

    @pl.when(i > 0)
    def _():
        ref[...] += val


def _rinv(z):
    return lax.rsqrt(jnp.mean(z * z, axis=-1, keepdims=True) + EPS)


def _norm_bwd(dy, zhat, r, w):
    dyw = dy * w
    return r * (dyw - zhat * jnp.mean(dyw * zhat, axis=-1, keepdims=True))


def _norm_fwd(x, w, name):
    def body(x_ref, w_ref, h_ref):
        xv = x_ref[...]
        h_ref[...] = _bf(xv * _rinv(xv) * w_ref[...])

    return _rows_call(name, body, x.shape[0], 512, 1, [_tile(x, D_MODEL), _full(w)],
                      [_out_tile(D_MODEL, BF16, D_MODEL)])[0]


def _prenorm_bwd(dh, xin, w, dres, name, plans=None):
    def body(dh_ref, x_ref, w_ref, dres_ref, dx_ref, dw_ref):
        xv = x_ref[...]
        r = _rinv(xv)
        xhat = xv * r
        dhv = dh_ref[...]
        dx_ref[...] = dres_ref[...] + _norm_bwd(dhv, xhat, r, w_ref[...])
        _acc(dw_ref, jnp.sum(dhv * xhat, axis=0, keepdims=True))

    return _rows_call(name, body, xin.shape[0], 512, 1,
                      [_tile(dh, D_MODEL), _tile(xin, D_MODEL), _full(w), _tile(dres, D_MODEL)],
                      [_out_tile(D_MODEL, F32, D_MODEL), _out_acc(1, D_MODEL, D_MODEL)], plans)


def _postnorm_bwd(dout, z, w, w_mat, name):
    def body(do_ref, z_ref, w_ref, wm_ref, dz_ref, dm_ref, dw_ref):
        zv = z_ref[...]
        r = _rinv(zv)
        zhat = zv * r
        dov = do_ref[...]
        dz = _bf(_norm_bwd(dov, zhat, r, w_ref[...]))
        dz_ref[...] = dz
        dm_ref[...] = _dot(dz, wm_ref[...], NT)
        _acc(dw_ref, jnp.sum(dov * zhat, axis=0, keepdims=True))

    return _rows_call(name, body, z.shape[0], 512, 1,
                      [_tile(dout, D_MODEL), _tile(z, D_MODEL), _full(w), _full(w_mat)],
                      [_out_tile(D_MODEL, BF16, D_MODEL), _out_tile(D_MODEL, F32, D_MODEL),
                       _out_acc(1, D_MODEL, D_MODEL)])


def _t5_bucket(dist):
    n = jnp.maximum(dist, 0)
    nf = jnp.maximum(n, 1).astype(F32)
    large = MAX_EXACT + (jnp.log(nf / MAX_EXACT) / math.log(MAX_DISTANCE / MAX_EXACT)
                         * (NUM_BUCKETS - MAX_EXACT)).astype(jnp.int32)
    large = jnp.minimum(large, NUM_BUCKETS - 1)
    return jnp.where(n < MAX_EXACT, n, large)


def _band_rel():
    return jnp.arange(BLK)[:, None] + BLK - jnp.arange(2 * BLK)[None, :]


def _band_valid():
    rel = _band_rel()
    window = (rel >= 0) & (rel <= BLK)
    first = window & (jnp.arange(2 * BLK)[None, :] >= BLK)
    return jnp.stack([first, window]).astype(F32).reshape(2, 1, BAND)


RES_UNROLL = 4
PAIR = LANES // HEAD_DIM


def _pair_lanes():
    first = lax.broadcasted_iota(jnp.int32, (1, LANES), 1) < HEAD_DIM
    return first, jnp.logical_not(first)


def _heads_per_step(d):
    return HEADS if d == 1 else LANES // HEAD_DIM


def _sub_rows(r, d):
    return pl.ds(r, BLK, stride=d) if d > 1 else pl.ds(0, BLK)


def _for_residues(d, fn):
    if d <= RES_UNROLL:
        for r in range(d):
            fn(r)
    else:
        def group(i, carry):
            for k in range(RES_UNROLL):
                fn(i * RES_UNROLL + k)
            return carry

        lax.fori_loop(0, d // RES_UNROLL, group, 0)


def _attn_specs(d, g, qblock):
    cw = _heads_per_step(d) * HEAD_DIM

    def col(part, hp):
        return (g * 3 + part) * (GROUP_W // cw) + hp

    def cur(part):
        return pl.BlockSpec((d * BLK, cw), lambda hp, n: (qblock(n), col(part, hp)))

    def prev(part):
        return pl.BlockSpec((d * BLK, cw), lambda hp, n: (jnp.maximum(qblock(n) - 1, 0), col(part, hp)))

    return cur, prev


def _attn_fwd(proj, bias, g, name, plans=None):
    S = proj.shape[0]
    d = DILATIONS[g]
    NB = S // (d * BLK)
    hps = _heads_per_step(d)

    def body(q_ref, kp_ref, kc_ref, vp_ref, vc_ref, b_ref, o_ref, lse_ref):
        hp = pl.program_id(0)
        later = jnp.minimum(pl.program_id(1), 1)

        def residue(r):
            rows = _sub_rows(r, d)
            q2 = q_ref[rows, :]
            k2 = jnp.concatenate([kp_ref[rows, :], kc_ref[rows, :]], axis=0)
            v2 = jnp.concatenate([vp_ref[rows, :], vc_ref[rows, :]], axis=0)
            outs, lses = [], []
            for pp in range(hps // PAIR):
                ps = slice(pp * LANES, (pp + 1) * LANES)
                qp, kp, vp = _bf(q2[:, ps]), _bf(k2[:, ps]), _bf(v2[:, ps])
                o_h, lse_h = [], []
                for hh, own in enumerate(_pair_lanes()):
                    s = _dot(qp, jnp.where(own, kp, 0), NT) * (HEAD_DIM ** -0.5) + b_ref[later, hp * hps + pp * PAIR + hh]
                    m = jnp.max(s, axis=-1, keepdims=True)
                    p = jnp.exp(s - m)
                    l = jnp.sum(p, axis=-1, keepdims=True)
                    o_h.append(_dot(_bf(p), vp, NN) / l)
                    lse_h.append(m + jnp.log(l))
                first = _pair_lanes()[0]
                outs.append(jnp.where(first, o_h[0], o_h[1]))
                lses.append(jnp.where(first, lse_h[0], lse_h[1]))
            o_ref[rows, :] = outs[0] if len(outs) == 1 else jnp.concatenate(outs, axis=1)
            lse_ref[rows, :] = lses[0] if len(lses) == 1 else jnp.concatenate(lses, axis=1)

        _for_residues(d, residue)

    cur, prev = _attn_specs(d, g, lambda n: n)
    out = pl.BlockSpec((d * BLK, hps * HEAD_DIM), lambda hp, n: (n, hp))
    res, carried = _call(
        body, plans, name=name, grid=(HEADS // hps, NB),
        in_specs=[cur(0), prev(1), cur(1), prev(2), cur(2),
                  pl.BlockSpec((2, HEADS, BLK, 2 * BLK), lambda hp, n: (0, 0, 0, 0))],
        out_specs=[out, out], out_shape=[_sds((S, GROUP_W), F32)] * 2,
        args=(proj, proj, proj, proj, proj, bias))
    return res if plans is None else (res, carried)


def _attn_bwd(proj, bias, lse, y, dy, g, name, plans=None):
    S = proj.shape[0]
    d = DILATIONS[g]
    NB = S // (d * BLK)
    hps = _heads_per_step(d)

    def body(q_ref, kp_ref, kc_ref, vp_ref, vc_ref, b_ref, l_ref, y_ref, dy_ref,
             dq_ref, dk_ref, dv_ref, db_ref, ck_ref, cv_ref):
        hp, n = pl.program_id(0), pl.program_id(1)

        @pl.when((hp == 0) & (n == 0))
        def _():
            db_ref[...] = jnp.zeros_like(db_ref)

        @pl.when(n == 0)
        def _():
            ck_ref[...] = jnp.zeros_like(ck_ref)
            cv_ref[...] = jnp.zeros_like(cv_ref)

        @pl.when(n < NB)
        def _():
            later = jnp.minimum(n, 1)

            def residue(r):
                rows = _sub_rows(r, d)
                q2 = q_ref[rows, :]
                k2 = jnp.concatenate([kp_ref[rows, :], kc_ref[rows, :]], axis=0)
                v2 = jnp.concatenate([vp_ref[rows, :], vc_ref[rows, :]], axis=0)
                l2, y2, dy2 = l_ref[rows, :], y_ref[rows, :], dy_ref[rows, :]
                dqs, dks, dvs = [], [], []
                for pp in range(hps // PAIR):
                    ps = slice(pp * LANES, (pp + 1) * LANES)
                    qp, kp, vp = _bf(q2[:, ps]), _bf(k2[:, ps]), _bf(v2[:, ps])
                    dyp, yp = dy2[:, ps], y2[:, ps]
                    dq_h, dk_h, dv_h = [], [], []
                    for hh, own in enumerate(_pair_lanes()):
                        head = hp * hps + pp * PAIR + hh
                        s = _dot(qp, jnp.where(own, kp, 0), NT) * (HEAD_DIM ** -0.5) + b_ref[later, head]
                        p = jnp.exp(s - l2[:, pp * LANES + hh * HEAD_DIM:pp * LANES + hh * HEAD_DIM + 1])
                        dyh = jnp.where(own, dyp, 0.0)
                        delta = jnp.sum(dyh * yp, axis=-1, keepdims=True)
                        ds = p * (_dot(_bf(dyh), vp, NT) - delta)
                        db_ref[head] += ds
                        dsb = _bf(ds * (HEAD_DIM ** -0.5))
                        dq_h.append(_dot(dsb, kp, NN))
                        dk_h.append(_dot(dsb, qp, TN))
                        dv_h.append(_dot(_bf(p), _bf(dyp), TN))
                    first = _pair_lanes()[0]
                    dqs.append(jnp.where(first, dq_h[0], dq_h[1]))
                    dks.append(jnp.where(first, dk_h[0], dk_h[1]))
                    dvs.append(jnp.where(first, dv_h[0], dv_h[1]))
                dkb = dks[0] if len(dks) == 1 else jnp.concatenate(dks, axis=1)
                dvb = dvs[0] if len(dvs) == 1 else jnp.concatenate(dvs, axis=1)
                dq_ref[rows, :] = dqs[0] if len(dqs) == 1 else jnp.concatenate(dqs, axis=1)
                dk_ref[rows, :] = ck_ref[rows, :] + dkb[:BLK]
                dv_ref[rows, :] = cv_ref[rows, :] + dvb[:BLK]
                ck_ref[rows, :] = dkb[BLK:]
                cv_ref[rows, :] = dvb[BLK:]

            _for_residues(d, residue)

        @pl.when(n == NB)
        def _():
            dk_ref[...] = ck_ref[...]
            dv_ref[...] = cv_ref[...]

    def qn(n):
        return jnp.minimum(n, NB - 1)

    cur, prev = _attn_specs(d, g, qn)
    cw = hps * HEAD_DIM
    row = pl.BlockSpec((d * BLK, cw), lambda hp, n: (qn(n), hp))
    done = pl.BlockSpec((d * BLK, cw), lambda hp, n: (jnp.maximum(n - 1, 0), hp))
    (dq, dk, dv, db), carried = _call(
        body, plans, name=name, grid=(HEADS // hps, NB + 1),
        in_specs=[cur(0), prev(1), cur(1), prev(2), cur(2),
                  pl.BlockSpec((2, HEADS, BLK, 2 * BLK), lambda hp, n: (0, 0, 0, 0)), row, row, row],
        out_specs=[row, done, done, pl.BlockSpec((HEADS, BLK, 2 * BLK), lambda hp, n: (0, 0, 0))],
        out_shape=[_sds((S, GROUP_W), F32)] * 3 + [_sds((HEADS, BLK, 2 * BLK), F32)],
        scratch_shapes=[pltpu.VMEM((d * BLK, cw), F32)] * 2,
        args=(proj, proj, proj, proj, proj, bias, lse, y, dy))
    return ([dq, dk, dv], db) if plans is None else ([dq, dk, dv], db, carried)


BAND = BLK * 2 * BLK


def _bucket_onehot():
    buckets = jnp.stack([_t5_bucket(_band_rel() * d) for d in DILATIONS]).reshape(N_GROUPS, 1, BAND)
    return (buckets == jnp.arange(NUM_BUCKETS).reshape(1, NUM_BUCKETS, 1)).astype(F32)


def _relbias_fwd(rel_bias, name):
    table = rel_bias.reshape(NUM_BUCKETS, N_GROUPS, HEADS).transpose(1, 0, 2)

    def body(t_ref, oh_ref, valid_ref, o_ref):
        bias = lax.dot_general(t_ref[...], oh_ref[...], (TN, ((), ())), preferred_element_type=F32,
                               precision=lax.Precision.HIGHEST)
        for k in range(2):
            o_ref[k] = jnp.where(valid_ref[k] > 0.5, bias, NEG_INF)

    out = pl.pallas_call(
        body, name=name, grid=(N_GROUPS,),
        in_specs=[pl.BlockSpec((None, NUM_BUCKETS, HEADS), lambda g: (g, 0, 0)),
                  pl.BlockSpec((None, NUM_BUCKETS, BAND), lambda g: (g, 0, 0)),
                  pl.BlockSpec((2, 1, BAND), lambda g: (0, 0, 0))],
        out_specs=pl.BlockSpec((None, 2, HEADS, BAND), lambda g: (g, 0, 0, 0)),
        out_shape=_sds((N_GROUPS, 2, HEADS, BAND), F32), compiler_params=_cp(1))(table, _bucket_onehot(), _band_valid())
    return out.reshape(N_GROUPS, 2, HEADS, BLK, 2 * BLK)


def _relbias_bwd(dbs, name):
    band = BAND
    onehot = _bucket_onehot()
    dbf = jnp.stack([db.reshape(HEADS, band) for db in dbs])

    def body(oh_ref, db_ref, o_ref):
        o_ref[...] = lax.dot_general(oh_ref[...], db_ref[...], (NT, ((), ())), preferred_element_type=F32,
                                     precision=lax.Precision.HIGHEST)

    out = pl.pallas_call(
        body, name=name, grid=(N_GROUPS,),
        in_specs=[pl.BlockSpec((None, NUM_BUCKETS, band), lambda g: (g, 0, 0)),
                  pl.BlockSpec((None, HEADS, band), lambda g: (g, 0, 0))],
        out_specs=pl.BlockSpec((None, NUM_BUCKETS, HEADS), lambda g: (g, 0, 0)),
        out_shape=_sds((N_GROUPS, NUM_BUCKETS, HEADS), F32), compiler_params=_cp(1))(onehot, dbf)
    return out.transpose(1, 0, 2).reshape(NUM_BUCKETS, N_GROUPS * HEADS)


def _chunk_pos(shape):
    return lax.broadcasted_iota(jnp.int32, shape, 0) % HG_CHUNK


def _chunk_cumsum(v):
    pos = _chunk_pos(v.shape)
    s = 1
    while s < HG_CHUNK:
        v = v + jnp.where(pos >= s, pltpu.roll(v, s, 0), 0.0)
        s *= 2
    return v


def _chunk_rev_cumsum(v):
    pos = _chunk_pos(v.shape)
    n = v.shape[0]
    s = 1
    while s < HG_CHUNK:
        v = v + jnp.where(pos < HG_CHUNK - s, pltpu.roll(v, n - s, 0), 0.0)
        s *= 2
    return v


def _lower_bound(raw):
    a0, a1 = raw[0:1], raw[1:2]
    m = jnp.maximum(a0, a1)
    e0, e1 = jnp.exp(a0 - m), jnp.exp(a1 - m)
    return e0 / (e0 + e1)


def _hg_gates(qr, fr, lb):
    sf = _sigmoid(fr)
    f = lb + (1.0 - lb) * sf
    sq = _sigmoid(qr)
    return qr * sq, sq, f, sf


HG_COL0 = QKV_W // HG_W


def _hgrn_fwd(proj, lb_raw, nw, name, plans=None):
    S = proj.shape[0]
    ncs = HG_TILE // HG_CHUNK
    tril = jnp.tril(jnp.ones((HG_CHUNK, HG_CHUNK), dtype=bool))

    def body(q_ref, f_ref, i_ref, og_ref, lb_ref, nw_ref, y_ref, o_ref, st_ref, state):
        @pl.when(pl.program_id(0) == 0)
        def _():
            state[...] = jnp.zeros_like(state)

        lb = _lower_bound(lb_ref[...])
        q, _, f, _ = _hg_gates(q_ref[...], f_ref[...], lb)
        k = 1.0 - f
        G = _chunk_cumsum(jnp.log(f))
        row = lax.broadcasted_iota(jnp.int32, (HG_CHUNK, HG_CHUNK), 0)
        col = lax.broadcasted_iota(jnp.int32, (HG_CHUNK, HG_CHUNK), 1)
        heads = [slice(h * HG_DK, (h + 1) * HG_DK) for h in range(HG_HEADS)]
        sts = [state[h] for h in range(HG_HEADS)]
        for c in range(ncs):
            cs = slice(c * HG_CHUNK, (c + 1) * HG_CHUNK)
            for h, hs in enumerate(heads):
                Gc = G[cs, hs]
                gl = Gc[HG_CHUNK - 1:HG_CHUNK]
                qt = _bf(q[cs, hs] * jnp.exp(Gc))
                kt = _bf(k[cs, hs] * jnp.exp(-Gc))
                kd = _bf(k[cs, hs] * jnp.exp(gl - Gc))
                v = _bf(i_ref[cs, hs])
                A = jnp.where(row >= col, _dot(qt, kt, NT), 0.0)
                o_ref[cs, hs] = _dot(_bf(A), v, NN) + _dot(qt, _bf(sts[h]), NT)
                st_ref[c, h] = sts[h]
                sts[h] = sts[h] * jnp.exp(gl) + _dot(v, kd, TN)
        for h, hs in enumerate(heads):
            state[h] = sts[h]
            oh = o_ref[:, hs]
            og = og_ref[:, hs]
            y_ref[:, hs] = oh * _rinv(oh) * nw_ref[...] * (og * _sigmoid(og))

    def colspec(j):
        return pl.BlockSpec((HG_TILE, HG_W), lambda i: (i, HG_COL0 + j))

    res, carried = _call(
        body, plans, name=name, grid=(S // HG_TILE,),
        in_specs=[colspec(0), colspec(1), colspec(2), colspec(3),
                  pl.BlockSpec((2, HG_W), lambda i: (0, 0)), pl.BlockSpec((1, HG_DK), lambda i: (0, 0))],
        out_specs=[pl.BlockSpec((HG_TILE, HG_W), lambda i: (i, 0))] * 2
        + [pl.BlockSpec((ncs, HG_HEADS, HG_DK, HG_DK), lambda i: (i, 0, 0, 0))],
        out_shape=[_sds((S, HG_W), F32)] * 2 + [_sds((S // HG_CHUNK, HG_HEADS, HG_DK, HG_DK), F32)],
        scratch_shapes=[pltpu.VMEM((HG_HEADS, HG_DK, HG_DK), F32)],
        args=(proj, proj, proj, proj, lb_raw, nw))
    return res if plans is None else (res, carried)


def _hgrn_bwd(proj, lb_raw, nw, o, states, dy, d_attn, d_gates, name):
    S = proj.shape[0]
    ncs = HG_TILE // HG_CHUNK
    nt = S // HG_TILE
    n_a, n_g = len(d_attn), len(d_gates)
    own = [slice(QKV_W + j * HG_W, QKV_W + (j + 1) * HG_W) for j in range(4)]

    def body(q_ref, f_ref, i_ref, og_ref, lb_ref, nw_ref, o_ref, st_ref, dy_ref, *rest):
        attn_refs, gate_refs = rest[:n_a], rest[n_a:n_a + n_g]
        dp_ref, dlb_ref, dnw_ref, dstate, do_s, dG_s, dgl_s, dk_s, dlb_s = rest[n_a + n_g:]
        dq_ref, df_ref, di_ref, dog_ref = (dp_ref.at[:, cols] for cols in own)
        step = pl.program_id(0)
        for k, a_ref in enumerate(attn_refs):
            dp_ref[:, k * GROUP_W:(k + 1) * GROUP_W] = _bf(a_ref[...])
        for k, g_ref in enumerate(gate_refs):
            dp_ref[:, QKV_W + 4 * HG_W + k * D_MODEL:QKV_W + 4 * HG_W + (k + 1) * D_MODEL] = g_ref[...]

        @pl.when(step == 0)
        def _():
            dstate[...] = jnp.zeros_like(dstate)
            dlb_s[...] = jnp.zeros_like(dlb_s)
            dnw_ref[...] = jnp.zeros_like(dnw_ref)

        lb = _lower_bound(lb_ref[...])
        qr = q_ref[...]
        q, sq, f, sf = _hg_gates(qr, f_ref[...], lb)
        k = 1.0 - f
        G = _chunk_cumsum(jnp.log(f))
        nwv = nw_ref[...]
        row = lax.broadcasted_iota(jnp.int32, (HG_CHUNK, HG_CHUNK), 0)
        col = lax.broadcasted_iota(jnp.int32, (HG_CHUNK, HG_CHUNK), 1)
        for h in range(HG_HEADS):
            hs = slice(h * HG_DK, (h + 1) * HG_DK)
            oh = o_ref[:, hs]
            r = _rinv(oh)
            ohat = oh * r
            og = og_ref[:, hs]
            sg = _sigmoid(og)
            dyh = dy_ref[:, hs]
            don = dyh * (og * sg)
            dog_ref[:, hs] = _bf(dyh * (ohat * nwv) * (sg * (1.0 + og * (1.0 - sg))))
            dnw_ref[...] += jnp.sum(don * ohat, axis=0, keepdims=True)
            do_s[:, hs] = _norm_bwd(don, ohat, r, nwv)
        dsts = [dstate[h] for h in range(HG_HEADS)]
        for c in reversed(range(ncs)):
            cs = slice(c * HG_CHUNK, (c + 1) * HG_CHUNK)
            for h in range(HG_HEADS):
                hs = slice(h * HG_DK, (h + 1) * HG_DK)
                dst = dsts[h]
                Gc = G[cs, hs]
                gl = Gc[HG_CHUNK - 1:HG_CHUNK]
                eG, enG, edG, egl = jnp.exp(Gc), jnp.exp(-Gc), jnp.exp(gl - Gc), jnp.exp(gl)
                qt, kt, kd = q[cs, hs] * eG, k[cs, hs] * enG, k[cs, hs] * edG
                qtb, ktb, kdb = _bf(qt), _bf(kt), _bf(kd)
                v = _bf(i_ref[cs, hs])
                do = _bf(do_s[cs, hs])
                st = st_ref[c, h]
                dstb = _bf(dst)
                A = jnp.where(row >= col, _dot(qtb, ktb, NT), 0.0)
                dA = _bf(jnp.where(row >= col, _dot(do, v, NT), 0.0))
                di_ref[cs, hs] = _bf(_dot(_bf(A), do, TN) + _dot(kdb, dstb, NT))
                dqt = _dot(dA, ktb, NN) + _dot(do, _bf(st), NN)
                dkt = _dot(dA, qtb, TN)
                dkd = _dot(v, dstb, NN)
                dgl = egl * jnp.sum(st * dst, axis=0, keepdims=True) + jnp.sum(dkd * kd, axis=0, keepdims=True)
                dsts[h] = dst * egl + _dot(do, qtb, TN)
                dq_ref[cs, hs] = _bf(dqt * eG * (sq[cs, hs] * (1.0 + qr[cs, hs] * (1.0 - sq[cs, hs]))))
                dk_s[cs, hs] = dkt * enG + dkd * edG
                dG_s[cs, hs] = dqt * qt - dkt * kt - dkd * kd
                dgl_s[cs, hs] = jnp.broadcast_to(dgl, (HG_CHUNK, HG_DK))
        for h in range(HG_HEADS):
            dstate[h] = dsts[h]
        dg = _chunk_rev_cumsum(dG_s[...]) + dgl_s[...]
        dfv = dg / f - dk_s[...]
        df_ref[...] = _bf(dfv * (1.0 - lb) * sf * (1.0 - sf))
        dlb_s[...] += jnp.sum(dfv * (1.0 - sf), axis=0, keepdims=True)

        @pl.when(step == nt - 1)
        def _():
            t = dlb_s[...] * lb * (1.0 - lb)
            dlb_ref[...] = jnp.concatenate([t, -t], axis=0)

    def colspec(j):
        return pl.BlockSpec((HG_TILE, HG_W), lambda i: (nt - 1 - i, HG_COL0 + j))

    def rows(width):
        return pl.BlockSpec((HG_TILE, width), lambda i: (nt - 1 - i, 0))

    tile = rows(HG_W)
    return pl.pallas_call(
        body, name=name, grid=(nt,),
        in_specs=[colspec(0), colspec(1), colspec(2), colspec(3),
                  pl.BlockSpec((2, HG_W), lambda i: (0, 0)), pl.BlockSpec((1, HG_DK), lambda i: (0, 0)),
                  tile, pl.BlockSpec((ncs, HG_HEADS, HG_DK, HG_DK), lambda i: (nt - 1 - i, 0, 0, 0)), tile]
        + [rows(GROUP_W)] * n_a + [rows(D_MODEL)] * n_g,
        out_specs=[rows(IN_W), pl.BlockSpec((2, HG_W), lambda i: (0, 0)), pl.BlockSpec((1, HG_DK), lambda i: (0, 0))],
        out_shape=[_sds((S, IN_W), BF16), _sds((2, HG_W), F32), _sds((1, HG_DK), F32)],
        scratch_shapes=[pltpu.VMEM((HG_HEADS, HG_DK, HG_DK), F32)] + [pltpu.VMEM((HG_TILE, HG_W), F32)] * 4
        + [pltpu.VMEM((1, HG_W), F32)],
        compiler_params=_cp(1))(proj, proj, proj, proj, lb_raw, nw, o, states, dy, *d_attn, *d_gates)


GATE_COL0 = (QKV_W + 4 * HG_W) // GROUP_W
HALF_D = D_MODEL // 2


def _gate_tiles(proj):
    return [_tile(proj, HALF_D, functools.partial(lambda c, k: GATE_COL0 + k, k=k)) for k in range(4)]


def _gates(g_refs):
    s0 = _sigmoid(jnp.concatenate([g_refs[0][...], g_refs[1][...]], axis=1))
    s1 = _sigmoid(jnp.concatenate([g_refs[2][...], g_refs[3][...]], axis=1))
    return s0, s1


def _branch_fwd(os_, lses, yh, proj, w_a, w_h, name):
    nb = w_a.shape[0]

    def body(o0, o1, o2, l0, l1, l2, yh_ref, g0a, g0b, g1a, g1b, wa_ref, wh_ref,
             y_ref, lse_ref, za_ref, zh_ref, m_ref):
        a, b, c = l0[...], l1[...], l2[...]
        m = jnp.maximum(jnp.maximum(a, b), c)
        ea, eb, ec = jnp.exp(a - m), jnp.exp(b - m), jnp.exp(c - m)
        den = ea + eb + ec
        y = (ea * o0[...] + eb * o1[...] + ec * o2[...]) / den
        y_ref[...] = y
        lse_ref[...] = m + jnp.log(den)
        yb, yhb = _bf(y), _bf(yh_ref[...])
        za = jnp.concatenate([_dot(yb, wa_ref[j], NN) for j in range(nb)], axis=1)
        zh = jnp.concatenate([_dot(yhb, wh_ref[j], NN) for j in range(nb)], axis=1)
        s0, s1 = _gates((g0a, g0b, g1a, g1b))
        za_ref[...] = za
        zh_ref[...] = zh
        m_ref[...] = _bf(s0 * za + s1 * zh)

    return _rows_call(name, body, yh.shape[0], 512, 1,
                      [*[_tile(t, GROUP_W) for t in (*os_, *lses)], _tile(yh, HG_W), *_gate_tiles(proj),
                       _full(w_a), _full(w_h)],
                      [_out_tile(GROUP_W, F32, GROUP_W)] * 2 + [_out_tile(D_MODEL, F32, D_MODEL)] * 2
                      + [_out_tile(D_MODEL, BF16, D_MODEL)])


def _branch_bwd(dm, za, zh, proj, w_a, w_h, name, plans=None):
    nb, _, Nb = w_a.shape

    def body(dm_ref, za_ref, zh_ref, g0a, g0b, g1a, g1b, wa_ref, wh_ref,
             dza_ref, dzh_ref, dg0_ref, dg1_ref, dy_ref, dyh_ref):
        dmv = dm_ref[...]
        s0, s1 = _gates((g0a, g0b, g1a, g1b))
        dza, dzh = _bf(dmv * s0), _bf(dmv * s1)
        dza_ref[...] = dza
        dzh_ref[...] = dzh
        dg0_ref[...] = _bf(dmv * za_ref[...] * s0 * (1.0 - s0))
        dg1_ref[...] = _bf(dmv * zh_ref[...] * s1 * (1.0 - s1))
        dy_ref[...] = sum(_dot(dza[:, j * Nb:(j + 1) * Nb], wa_ref[j], NT) for j in range(nb))
        dyh_ref[...] = sum(_dot(dzh[:, j * Nb:(j + 1) * Nb], wh_ref[j], NT) for j in range(nb))

    return _rows_call(name, body, za.shape[0], 512, 1,
                      [_tile(dm, D_MODEL), _tile(za, D_MODEL), _tile(zh, D_MODEL), *_gate_tiles(proj),
                       _full(w_a), _full(w_h)],
                      [_out_tile(D_MODEL, BF16, D_MODEL)] * 4 + [_out_tile(GROUP_W, F32, GROUP_W),
                                                                 _out_tile(HG_W, F32, HG_W)], plans)


def _mix_out(merged, w_out, x, w_post, w_pre, name):
    def body(m_ref, wo_ref, x_ref, wp_ref, wf_ref, mo_ref, x1_ref, h2_ref):
        z = _dot(m_ref[...], wo_ref[...], NN)
        mo_ref[...] = z
        x1 = x_ref[...] + z * _rinv(z) * wp_ref[...]
        x1_ref[...] = x1
        h2_ref[...] = _bf(x1 * _rinv(x1) * wf_ref[...])

    return _rows_call(name, body, x.shape[0], 512, 1,
                      [_tile(merged, D_MODEL), _full(w_out), _tile(x, D_MODEL), _full(w_post), _full(w_pre)],
                      [_out_tile(D_MODEL, F32, D_MODEL), _out_tile(D_MODEL, F32, D_MODEL),
                       _out_tile(D_MODEL, BF16, D_MODEL)])


def _loss_head(a, w_down, x1, tgt, w, name):
    def body(a_ref, wd_ref, x1_ref, t_ref, w_ref, dx_ref, df_ref, dw_ref, loss_ref):
        z = _dot(a_ref[...], wd_ref[...], NN)
        r = _rinv(z)
        zhat = z * r
        wv = w_ref[...]
        e = x1_ref[...] + zhat * wv - t_ref[...]
        dx = e * (1.0 / D_MODEL)
        dx_ref[...] = dx
        df_ref[...] = _bf(_norm_bwd(dx, zhat, r, wv))
        _acc(dw_ref, jnp.sum(dx * zhat, axis=0, keepdims=True))
        part = 0.5 * jnp.sum(jnp.sum(e * e, axis=1, keepdims=True), axis=0, keepdims=True) * (1.0 / D_MODEL)
        _acc(loss_ref, jnp.broadcast_to(part, (1, LANES)))

    return _rows_call(name, body, x1.shape[0], 512, 1,
                      [_tile(a, D_FF), _full(w_down), _tile(x1, D_MODEL), _tile(tgt, D_MODEL), _full(w)],
                      [_out_tile(D_MODEL, F32, D_MODEL), _out_tile(D_MODEL, BF16, D_MODEL),
                       _out_acc(1, D_MODEL, D_MODEL), _out_acc(1, LANES, LANES)])


CONV_CB = D_FF // 2
CONV_TM = 512
HALO = 8
SQRT_HALF = 0.7071067811865476
INV_SQRT_2PI = 0.3989422804014327


CONV_RS = 32


def _lane_tiles():
    return [slice(k * LANES, (k + 1) * LANES) for k in range(CONV_CB // LANES)]


def _strip_start(i):
    return pl.multiple_of(i * CONV_RS, CONV_RS)


def _strip_taps(u_ref, halo_ref, r0, cs, first_strip, first_tile):
    if first_strip:
        before = jnp.where(first_tile, 0.0, halo_ref[:, cs])
        blk = jnp.concatenate([before, u_ref[0:CONV_RS, cs]], axis=0)
    else:
        blk = u_ref[pl.ds(pl.multiple_of(r0 - HALO, HALO), CONV_RS + HALO), cs]
    return pltpu.roll(blk, 2, 0)[HALO:], pltpu.roll(blk, 1, 0)[HALO:], blk[HALO:]


def _conv(taps, w_ref, b_ref, cs):
    return b_ref[:, cs] + w_ref[0:1, cs] * taps[0] + w_ref[1:2, cs] * taps[1] + w_ref[2:3, cs] * taps[2]


def _conv_specs(tm):
    nh = tm // HALO
    nc = D_FF // CONV_CB

    def tile(off):
        return pl.BlockSpec((tm, CONV_CB), lambda c, i: (i, off + c))

    def halo(off):
        return pl.BlockSpec((HALO, CONV_CB), lambda c, i: (jnp.maximum(i * nh - 1, 0), off + c))

    def small(rows, off):
        return pl.BlockSpec((rows, CONV_CB), lambda c, i: (0, off + c))

    return nc, tile, halo, small


def _conv_gelu_fwd(u, cw, cb, name, plans=None):
    S = u.shape[0]
    tm = CONV_TM
    nc, tile, halo, small = _conv_specs(tm)

    def body(ug, hg, uv, hv, wg, wv, bg, bv, a_ref):
        first_tile = pl.program_id(1) == 0

        def strip(r0, first_strip):
            for cs in _lane_tiles():
                cg = _conv(_strip_taps(ug, hg, r0, cs, first_strip, first_tile), wg, bg, cs)
                cv = _conv(_strip_taps(uv, hv, r0, cs, first_strip, first_tile), wv, bv, cs)
                a_ref[pl.ds(r0, CONV_RS), cs] = _bf(0.5 * cg * (1.0 + lax.erf(cg * SQRT_HALF)) * cv)

        strip(0, True)
        lax.fori_loop(1, tm // CONV_RS, lambda k, c: (strip(_strip_start(k), False), c)[1], 0)

    (a,), carried = _call(
        body, plans, name=name, grid=(nc, S // tm),
        in_specs=[tile(0), halo(0), tile(nc), halo(nc), small(3, 0), small(3, nc), small(1, 0), small(1, nc)],
        out_specs=[tile(0)], out_shape=[_sds((S, D_FF), BF16)], args=(u, u, u, u, cw, cw, cb, cb))
    return a if plans is None else (a, carried)


def _conv_gelu_bwd(u, dff, w_down, cw, cb, name, plans=None):
    S = u.shape[0]
    tm = CONV_TM
    nt = S // tm
    nc, tile, halo, small = _conv_specs(tm)

    def body(ug, hg, uv, hv, wg, wv, bg, bv, dff_ref, wd_ref, dcg_ref, dcv_ref, dwg_ref, dwv_ref, dbg_ref, dbv_ref,
             acc, da_ref):
        i = pl.program_id(1)
        first_tile = i == 0
        da_ref[...] = _dot(dff_ref[...], wd_ref[...], NT)

        @pl.when(first_tile)
        def _():
            acc[...] = jnp.zeros_like(acc)

        def strip(r0, first_strip):
            rows = pl.ds(r0, CONV_RS)
            for cs in _lane_tiles():
                tg = _strip_taps(ug, hg, r0, cs, first_strip, first_tile)
                tv = _strip_taps(uv, hv, r0, cs, first_strip, first_tile)
                cg = _conv(tg, wg, bg, cs)
                cv = _conv(tv, wv, bv, cs)
                phi = 0.5 * (1.0 + lax.erf(cg * SQRT_HALF))
                dav = da_ref[rows, cs]
                dcg = dav * cv * (phi + cg * jnp.exp(-0.5 * cg * cg) * INV_SQRT_2PI)
                dcv = dav * (cg * phi)
                dcg_ref[rows, cs] = dcg
                dcv_ref[rows, cs] = dcv
                for half, (dc, taps) in enumerate(((dcg, tg), (dcv, tv))):
                    for j in range(3):
                        acc[4 * half + j, :, cs] += dc * taps[j]
                    acc[4 * half + 3, :, cs] += dc

        strip(0, True)
        lax.fori_loop(1, tm // CONV_RS, lambda k, c: (strip(_strip_start(k), False), c)[1], 0)

        @pl.when(i == nt - 1)
        def _():
            for half, (dw_ref, db_ref) in enumerate(((dwg_ref, dbg_ref), (dwv_ref, dbv_ref))):
                for j in range(3):
                    dw_ref[j:j + 1, :] = jnp.sum(acc[4 * half + j], axis=0, keepdims=True)
                db_ref[...] = jnp.sum(acc[4 * half + 3], axis=0, keepdims=True)

    res, carried = _call(
        body, plans, name=name, grid=(nc, nt),
        in_specs=[tile(0), halo(0), tile(nc), halo(nc), small(3, 0), small(3, nc), small(1, 0), small(1, nc),
                  pl.BlockSpec((tm, D_MODEL), lambda c, i: (i, 0)), pl.BlockSpec((CONV_CB, D_MODEL), lambda c, i: (c, 0))],
        out_specs=[tile(0), tile(0), small(3, 0), small(3, 0), small(1, 0), small(1, 0)],
        out_shape=[_sds((S, D_FF), F32)] * 2 + [_sds((3, D_FF), F32)] * 2 + [_sds((1, D_FF), F32)] * 2,
        scratch_shapes=[pltpu.VMEM((8, CONV_RS, CONV_CB), F32), pltpu.VMEM((tm, CONV_CB), F32)],
        args=(u, u, u, u, cw, cw, cb, cb, dff, w_down))
    return res if plans is None else (res, carried)


def _conv_input_bwd(dcg, dcv, cw, name, plans=None):
    S = dcg.shape[0]
    tm = CONV_TM // 2
    nh = tm // HALO
    nt = S // tm
    n = CONV_RS + HALO
    tile = pl.BlockSpec((tm, D_FF), lambda i: (i, 0))
    nxt = pl.BlockSpec((HALO, D_FF), lambda i: (jnp.minimum((i + 1) * nh, S // HALO - 1), 0))

    def body(g_ref, ng_ref, v_ref, nv_ref, w_ref, du_ref):
        last_tile = pl.program_id(0) == nt - 1

        def strip(r0, last_strip):
            for half, (dc_ref, n_ref) in enumerate(((g_ref, ng_ref), (v_ref, nv_ref))):
                for k in range(D_FF // LANES):
                    cs = slice(k * LANES, (k + 1) * LANES)
                    ws = slice(half * D_FF + k * LANES, half * D_FF + (k + 1) * LANES)
                    if last_strip:
                        after = jnp.where(last_tile, 0.0, n_ref[:, cs])
                        blk = jnp.concatenate([dc_ref[tm - CONV_RS:tm, cs], after], axis=0)
                    else:
                        blk = dc_ref[pl.ds(r0, n), cs]
                    d1 = pltpu.roll(blk, n - 1, 0)[:CONV_RS]
                    d2 = pltpu.roll(blk, n - 2, 0)[:CONV_RS]
                    du_ref[pl.ds(r0, CONV_RS), ws] = _bf(w_ref[2:3, ws] * blk[:CONV_RS] + w_ref[1:2, ws] * d1
                                                         + w_ref[0:1, ws] * d2)

        lax.fori_loop(0, tm // CONV_RS - 1, lambda k, c: (strip(_strip_start(k), False), c)[1], 0)
        strip(tm - CONV_RS, True)

    (du,), carried = _call(
        body, plans, name=name, grid=(nt,),
        in_specs=[tile, nxt, tile, nxt, pl.BlockSpec((3, 2 * D_FF), lambda i: (0, 0))],
        out_specs=[pl.BlockSpec((tm, 2 * D_FF), lambda i: (i, 0))], out_shape=[_sds((S, 2 * D_FF), BF16)],
        args=(dcg, dcg, dcv, dcv, cw))
    return du if plans is None else (du, carried)


def _row_tile(n, cap):
    best = n
    for t in range(16, cap + 1, 16):
        if n % t == 0:
            best = t
    return best if best <= cap else n


def _rows_for_bytes(nbytes, cols):
    return max(16, nbytes // (4 * cols) // 16 * 16)


def _adamw(w, g, m, v, name):
    R, C = w.shape
    tr = _row_tile(R, _rows_for_bytes(2 << 20, C))

    def body(w_ref, g_ref, m_ref, v_ref, d_ref, nm_ref, nv_ref):
        gv = g_ref[...]
        nm = ADAM_B1 * m_ref[...] + (1.0 - ADAM_B1) * gv
        nv = ADAM_B2 * v_ref[...] + (1.0 - ADAM_B2) * (gv * gv)
        m_hat = nm / (1.0 - ADAM_B1 ** ADAM_STEP)
        v_hat = nv / (1.0 - ADAM_B2 ** ADAM_STEP)
        d_ref[...] = -ADAM_LR * (m_hat / (jnp.sqrt(v_hat) + ADAM_EPS) + ADAM_WD * w_ref[...])
        nm_ref[...] = nm
        nv_ref[...] = nv

    spec = pl.BlockSpec((tr, C), lambda i: (i, 0))
    return pl.pallas_call(body, name=name, grid=(R // tr,), in_specs=[spec] * 4, out_specs=[spec] * 3,
                          out_shape=[_sds((R, C), F32)] * 3, compiler_params=_cp(1))(w, g, m, v)


def _pair_sum(gfull, rcv, c_idx, name):
    nb, R, C = gfull.shape
    half = R // 2
    tr = _row_tile(half, _rows_for_bytes(2 << 20, C))
    nt = half // tr

    def body(c_ref, g_ref, r_ref, o_ref):
        o_ref[...] = _bf(g_ref[...] + r_ref[...])

    return pl.pallas_call(
        body, name=name,
        grid_spec=pltpu.PrefetchScalarGridSpec(
            num_scalar_prefetch=1, grid=(nb, nt),
            in_specs=[pl.BlockSpec((None, tr, C), lambda j, i, c_ref: (j, c_ref[0] * nt + i, 0)),
                      pl.BlockSpec((None, tr, C), lambda j, i, c_ref: (j, i, 0))],
            out_specs=pl.BlockSpec((None, tr, C), lambda j, i, c_ref: (j, i, 0))),
        out_shape=_sds((nb, half, C), BF16), compiler_params=_cp(2))(c_idx, gfull, rcv)


def _chip_sum(arrived, own, place, name):
    nb, H, C = arrived.shape
    tr = _row_tile(H, _rows_for_bytes(2 << 20, C))
    nt = H // tr

    def body(pl_ref, *refs):
        o_ref = refs[nb + 1]
        me = pl_ref[0]
        acc = None
        for k in range(nb):
            term = jnp.where(me == k, refs[nb][...], refs[k][...]).astype(F32)
            acc = term if acc is None else acc + term
        o_ref[...] = acc

    def other(k):
        return pl.BlockSpec((None, tr, C), lambda i, p: (jnp.where(p[0] == k, (k + 1) % nb, k), i, 0))

    return pl.pallas_call(
        body, name=name,
        grid_spec=pltpu.PrefetchScalarGridSpec(
            num_scalar_prefetch=1, grid=(nt,),
            in_specs=[other(k) for k in range(nb)] + [pl.BlockSpec((None, tr, C), lambda i, p: (p[0], i, 0))],
            out_specs=pl.BlockSpec((tr, C), lambda i, p: (p[1] * nt + i, 0))),
        out_shape=_sds((2 * H, C), F32), compiler_params=_cp(1))(place, *([arrived] * nb), own)


def _cast_into_slot(shard, place, name):
    R, C = shard.shape
    tr = _row_tile(R, 256)

    def body(pl_ref, s_ref, o_ref):
        o_ref[...] = _bf(s_ref[...])

    return pl.pallas_call(
        body, name=name,
        grid_spec=pltpu.PrefetchScalarGridSpec(
            num_scalar_prefetch=1, grid=(R // tr,),
            in_specs=[pl.BlockSpec((tr, C), lambda i, p: (i, 0))],
            out_specs=pl.BlockSpec((None, tr, C), lambda i, p: (p[0], i, 0))),
        out_shape=_sds((N_CHIPS, R, C), BF16), compiler_params=_cp(1))(place, shard)


def _place():
    x, y, c = lax.axis_index("x"), lax.axis_index("y"), lax.axis_index("c")
    chips = [(1 - x, y), (x, 1 - y), (1 - x, 1 - y)]
    return x, y, c, chips


def _chip_id(px, py):
    return 2 * px + py


def _remote(src, dst, send_sems, recv_sems, k, to):
    return pltpu.make_async_remote_copy(src_ref=src, dst_ref=dst, send_sem=send_sems.at[k], recv_sem=recv_sems.at[k],
                                        device_id=to, device_id_type=MESH)


def _proj_gathered(h, slot, place, name, tm=512):
    M, K = h.shape
    nb, _, Nb = slot.shape
    half = K // 2
    nt = M // tm
    cx, cy = place[0] // 2, place[0] % 2
    order = jnp.stack([place[0], _chip_id(1 - cx, cy), _chip_id(cx, 1 - cy), _chip_id(1 - cx, 1 - cy)]).astype(jnp.int32)

    def body(order_ref, h_ref, slot_in, o_ref, slot_ref, w_buf, ici_send, ici_recv, pass_send, pass_recv, load_sem):
        b, i = pl.program_id(0), pl.program_id(1)
        x, y, c, chips = _place()
        me = _chip_id(x, y)
        sib = (x, y, 1 - c)
        mine, other = pl.ds(c * half, half), pl.ds((1 - c) * half, half)

        def sent(k):
            blk = slot_ref.at[me, mine]
            return _remote(blk, blk, ici_send, ici_recv, k, (*chips[k], c))

        def landed(k):
            blk = slot_ref.at[_chip_id(*chips[k]), mine]
            return _remote(blk, blk, ici_send, ici_recv, k, (*chips[k], c))

        def passed(k, rows):
            blk = slot_ref.at[_chip_id(*chips[k]), rows]
            return _remote(blk, blk, pass_send, pass_recv, k, sib)

        @pl.when((b == 0) & (i == 0))
        def _():
            for k in range(len(chips)):
                sent(k).start()

        for k in range(len(chips)):
            @pl.when((b == k + 1) & (i == 0))
            def _(k=k):
                landed(k).wait_recv()
                passed(k, mine).start()
                passed(k, other).wait_recv()

        @pl.when(i == 0)
        def _():
            load = pltpu.make_async_copy(slot_ref.at[order_ref[b]], w_buf, load_sem.at[0])
            load.start()
            load.wait()

        o_ref[...] = _dot(h_ref[...], w_buf[...], NN)

        @pl.when((b == nb - 1) & (i == nt - 1))
        def _():
            for k in range(len(chips)):
                sent(k).wait_send()
                passed(k, mine).wait_send()

    n_peers = N_CHIPS - 1
    return pl.pallas_call(
        body, name=name,
        grid_spec=pltpu.PrefetchScalarGridSpec(
            num_scalar_prefetch=1, grid=(nb, nt),
            in_specs=[pl.BlockSpec((tm, K), lambda b, i, o: (i, 0)), ANY],
            out_specs=[pl.BlockSpec((tm, Nb), lambda b, i, o: (i, o[b])), ANY],
            scratch_shapes=[pltpu.VMEM((K, Nb), BF16)] + [pltpu.SemaphoreType.DMA((n_peers,))] * 4
            + [pltpu.SemaphoreType.DMA((1,))]),
        out_shape=[_sds((M, nb * Nb), F32), _sds(slot.shape, slot.dtype)],
        input_output_aliases={2: 1}, compiler_params=_cp(2))(order, h, slot)


def _gather_ici_plan(slots, wholes):
    ns, nw = len(slots), len(wholes)

    def copies(ins, ios, outs, send_sems, recv_sems, local_sems):
        x, y, c, chips = _place()
        me = _chip_id(x, y)
        sends, recvs = [], []
        for a in range(ns + nw):
            dst = ios[a] if a < ns else outs[a - ns]
            R = dst.shape[1]
            rows = pl.ds(c * (R // 2), R // 2) if a < ns else pl.ds(0, R)
            src = dst.at[me, rows] if a < ns else ins[a - ns]
            for j, chip in enumerate(chips):
                sends.append(_remote(src, dst.at[me, rows], send_sems, recv_sems, 3 * a + j, (*chip, c)))
                landed = dst.at[_chip_id(*chip), rows]
                recvs.append(_remote(landed, landed, send_sems, recv_sems, 3 * a + j, (*chip, c)))
        local = [pltpu.make_async_copy(ins[b], outs[b].at[me], local_sems.at[b]) for b in range(nw)]
        return sends, recvs, local

    return _Plan(copies, 3 * (ns + nw), ins=wholes, inouts=slots,
                 outs=[_sds((N_CHIPS, *s.shape), s.dtype) for s in wholes])


def _gather_pass_plan(slots):
    def copies(ins, ios, outs, send_sems, recv_sems, local_sems):
        x, y, c, chips = _place()
        sib = (x, y, 1 - c)
        sends, recvs = [], []
        for a, buf in enumerate(ios):
            half = buf.shape[1] // 2
            for j, chip in enumerate(chips):
                mine = buf.at[_chip_id(*chip), pl.ds(c * half, half)]
                other = buf.at[_chip_id(*chip), pl.ds((1 - c) * half, half)]
                sends.append(_remote(mine, mine, send_sems, recv_sems, 3 * a + j, sib))
                recvs.append(_remote(other, other, send_sems, recv_sems, 3 * a + j, sib))
        return sends, recvs, []

    return _Plan(copies, 3 * len(slots), inouts=slots)


def _pair_plan(grads):
    def copies(ins, ios, outs, send_sems, recv_sems, local_sems):
        x, y, c, _ = _place()
        sib = (x, y, 1 - c)
        sends, recvs = [], []
        for a, g in enumerate(ins):
            half = g.shape[1] // 2
            sends.append(_remote(g.at[:, pl.ds((1 - c) * half, half), :], outs[a], send_sems, recv_sems, a, sib))
            recvs.append(_remote(outs[a], outs[a], send_sems, recv_sems, a, sib))
        return sends, recvs, []

    return _Plan(copies, len(grads), ins=grads,
                 outs=[_sds((g.shape[0], g.shape[1] // 2, g.shape[2]), g.dtype) for g in grads])


def _chip_plan(parts):
    def copies(ins, ios, outs, send_sems, recv_sems, local_sems):
        x, y, c, chips = _place()
        me = _chip_id(x, y)
        sends, recvs = [], []
        for a, part in enumerate(ins):
            for j, chip in enumerate(chips):
                sends.append(_remote(part.at[_chip_id(*chip)], outs[a].at[me], send_sems, recv_sems, 3 * a + j, (*chip, c)))
                landed = outs[a].at[_chip_id(*chip)]
                recvs.append(_remote(landed, landed, send_sems, recv_sems, 3 * a + j, (*chip, c)))
        return sends, recvs, []

    return _Plan(copies, 3 * len(parts), ins=parts, outs=[_sds(p.shape, p.dtype) for p in parts])


def _all_sum(pack, fulls, name):
    R, C = pack.shape
    n = len(fulls)

    def body(p_ref, *refs):
        o_ref, halves = refs[n], refs[n + 1:2 * n + 1]
        buf, send_sems, recv_sems, pair_send, pair_recv = refs[2 * n + 1:]
        x, y, c, _ = _place()
        sib = (x, y, 1 - c)
        pair = []
        for a, full in enumerate(halves):
            H = full.shape[0] // 2
            mine = full.at[pl.ds(c * H, H)]
            cp = _remote(mine, mine, pair_send, pair_recv, a, sib)
            cp.start()
            pair.append(cp)
        me = 4 * x + 2 * y + c
        buf[me] = p_ref[...]
        cps = []
        for k in range(1, N_DEV):
            to = (x ^ (k >> 2), y ^ ((k >> 1) & 1), c ^ (k & 1))
            cp = _remote(p_ref, buf.at[me], send_sems, recv_sems, k - 1, to)
            cp.start()
            cps.append(cp)
        for k in range(1, N_DEV):
            frm = (x ^ (k >> 2), y ^ ((k >> 1) & 1), c ^ (k & 1))
            slot = buf.at[4 * frm[0] + 2 * frm[1] + frm[2]]
            _remote(slot, slot, send_sems, recv_sems, k - 1, frm).wait_recv()
        acc = buf[0]
        for k in range(1, N_DEV):
            acc = acc + buf[k]
        o_ref[...] = acc
        for cp in cps:
            cp.wait_send()
        for a, (full, cp) in enumerate(zip(halves, pair)):
            H = full.shape[0] // 2
            other = full.at[pl.ds((1 - c) * H, H)]
            _remote(other, other, pair_send, pair_recv, a, sib).wait_recv()
            cp.wait_send()

    vm = pl.BlockSpec(memory_space=pltpu.VMEM)
    res = pl.pallas_call(
        body, name=name, in_specs=[vm] + [ANY] * n, out_specs=[vm] + [ANY] * n,
        out_shape=[_sds((R, C), F32)] + [_sds(f.shape, f.dtype) for f in fulls],
        input_output_aliases={1 + a: 1 + a for a in range(n)},
        scratch_shapes=[pltpu.VMEM((N_DEV, R, C), F32), pltpu.SemaphoreType.DMA((N_DEV - 1,)),
                        pltpu.SemaphoreType.DMA((N_DEV - 1,)), pltpu.SemaphoreType.DMA((n,)),
                        pltpu.SemaphoreType.DMA((n,))])(pack, *fulls)
    return res[0], list(res[1:])


def _local_step(xs, tgt, p, ex):
    h1 = _norm_fwd(xs, p["pre_mix_norm"], "pre_mix_norm")
    proj = ex.project(h1)
    biases = _relbias_fwd(p["rel_bias"], "rel_bias_fwd")
    fw = []
    for g in range(N_GROUPS):
        res, got = _attn_fwd(proj, biases[g], g, f"attn_fwd{g}", plans=ex.carry(f"attn_fwd{g}"))
        ex.done(f"attn_fwd{g}", got)
        fw.append(res)
    (yh, o_h, states), got = _hgrn_fwd(proj, p["hgrn_lb_raw"], p["hgrn_norm"], "hgrn_fwd", plans=ex.carry("hgrn_fwd"))
    ex.done("hgrn_fwd", got)
    W_a, W_h, W_out = ex.weight("w_branch_attn"), ex.weight("w_branch_hgrn"), ex.weight("w_out")
    W_up, conv_w = ex.weight("w_up"), ex.weight("conv_w")
    y, lse, za, zh, merged = _branch_fwd([t[0] for t in fw], [t[1] for t in fw], yh, proj, W_a, W_h, "branch_fwd")
    mo, x1, h2 = _mix_out(merged, W_out, xs, p["post_mix_norm"], p["pre_ffn_norm"], "mix_out")
    u, got = _mm_nn_blk(h2, W_up, "ffn_up", plans=ex.carry("ffn_up"))
    ex.done("ffn_up", got)
    a, got = _conv_gelu_fwd(u, conv_w, p["conv_b"], "conv_gelu_fwd", plans=ex.carry("conv_gelu_fwd"))
    ex.done("conv_gelu_fwd", got)
    W_down = ex.weight("w_down")
    dx2, dff, g_post_ffn, loss = _loss_head(a, W_down, x1, tgt, p["post_ffn_norm"], "ffn_down_loss")

    ex.grad("w_down", _mm_tn(a, dff, "g_w_down").reshape(N_CHIPS, D_FF // N_CHIPS, D_MODEL))
    (dcg, dcv, gwg, gwv, gbg, gbv), got = _conv_gelu_bwd(u, dff, W_down, conv_w, p["conv_b"], "conv_gelu_bwd",
                                                          plans=ex.carry("conv_gelu_bwd"))
    ex.done("conv_gelu_bwd", got)
    g_conv_w = jnp.concatenate([gwg, gwv], axis=1)
    g_conv_b = jnp.concatenate([gbg, gbv], axis=1)
    du, got = _conv_input_bwd(dcg, dcv, conv_w, "conv_input_bwd", plans=ex.carry("conv_input_bwd"))
    ex.done("conv_input_bwd", got)
    dh2 = _mm_nt_blk(du, W_up, "d_ffn_in")
    ex.grad("w_up", _mm_tn_blk(h2, du, N_CHIPS, "g_w_up"))
    dx1, g_pre_ffn = _prenorm_bwd(dh2, x1, p["pre_ffn_norm"], dx2, "pre_ffn_norm_bwd")
    dmo, dmerged, g_post_mix = _postnorm_bwd(dx1, mo, p["post_mix_norm"], W_out, "post_mix_norm_bwd")
    ex.grad("w_out", _mm_tn(merged, dmo, "g_w_out").reshape(N_CHIPS, D_MODEL // N_CHIPS, D_MODEL))
    (dza, dzh, dg0, dg1, dy, dyh), got = _branch_bwd(dmerged, za, zh, proj, W_a, W_h, "branch_bwd",
                                                     plans=ex.carry("branch_bwd"))
    ex.done("branch_bwd", got)
    ex.grad("w_branch_attn", _mm_tn_blk(y, dza, N_CHIPS, "g_w_branch_attn", together=True))
    ex.grad("w_branch_hgrn", _mm_tn_blk(yh, dzh, N_CHIPS, "g_w_branch_hgrn", together=True))
    dqkv, dbs = [], []
    for g in range(N_GROUPS):
        parts, db, got = _attn_bwd(proj, biases[g], lse, y, dy, g, f"attn_bwd{g}", plans=ex.carry(f"attn_bwd{g}"))
        ex.done(f"attn_bwd{g}", got)
        dqkv += parts
        dbs.append(db)
    g_rel_bias = _relbias_bwd(dbs, "rel_bias_bwd")
    dproj, g_lb_raw, g_hgrn_norm = _hgrn_bwd(proj, p["hgrn_lb_raw"], p["hgrn_norm"], o_h, states, dyh, dqkv,
                                             [dg0, dg1], "hgrn_bwd")
    for piece in W_IN_PIECES:
        g, got = _mm_tn_blk(h1, dproj, N_CHIPS, f"g_{piece}", x_cols=W_IN_ROWS[piece],
                            plans=ex.carry(f"g_{piece}"))
        ex.done(f"g_{piece}", got)
        ex.grad(piece, g)
    dh1, got = _mm_nt_blk(dproj, ex.weight("w_in"), "d_proj_in", plans=ex.carry("d_proj_in"))
    ex.done("d_proj_in", got)
    (grad_x, g_pre_mix), got = _prenorm_bwd(dh1, xs, p["pre_mix_norm"], dx1, "pre_mix_norm_bwd",
                                            plans=ex.carry("pre_mix_norm_bwd"))
    ex.done("pre_mix_norm_bwd", got)
    small = dict(pre_mix_norm=g_pre_mix, rel_bias=g_rel_bias, hgrn_lb_raw=g_lb_raw, hgrn_norm=g_hgrn_norm,
                 post_mix_norm=g_post_mix, pre_ffn_norm=g_pre_ffn, conv_w=g_conv_w, conv_b=g_conv_b,
                 post_ffn_norm=g_post_ffn)
    return loss, grad_x, small


SMALL = ("pre_mix_norm", "rel_bias", "hgrn_lb_raw", "hgrn_norm", "post_mix_norm", "pre_ffn_norm", "conv_w", "conv_b",
         "post_ffn_norm")
BIG = ("w_in", "w_up", "w_down", "w_out", "w_branch_attn", "w_branch_hgrn")
WEIGHTS = ("pre_mix_norm", "w_in", "rel_bias", "hgrn_lb_raw", "hgrn_norm", "w_branch_attn", "w_branch_hgrn", "w_out",
           "post_mix_norm", "pre_ffn_norm", "w_up", "conv_w", "conv_b", "w_down", "post_ffn_norm")
MIXER = ("w_out", "w_branch_attn", "w_branch_hgrn")

SCHEDULE = {
    "attn_fwd0": [("gather_ici_cw", MIXER)],
    "attn_fwd1": [("gather_pass", MIXER), ("gather_ici", ("w_up",))],
    "attn_fwd2": [("gather_pass", ("w_up",))],
    "ffn_up": [("gather_ici", ("w_down",))],
    "conv_gelu_fwd": [("gather_pass", ("w_down",))],
    "conv_gelu_bwd": [("pair", ("w_down",))],
    "conv_input_bwd": [("chip", ("w_down",))],
    "branch_bwd": [("pair", ("w_up",))],
    "attn_bwd0": [("chip", ("w_up",)), ("pair", MIXER)],
    "attn_bwd1": [("chip", MIXER)],
    "g_w_in_b": [("pair", ("w_in_a",))],
    "d_proj_in": [("chip", ("w_in_a",)), ("pair", ("w_in_b",))],
    "pre_mix_norm_bwd": [("chip", ("w_in_b",))],
}
W_IN_ROWS = dict(w_in_a=(0, 768), w_in_b=(3, 256))
W_IN_PIECES = tuple(W_IN_ROWS)
REDUCED = W_IN_PIECES + BIG[1:]


class _Exchange:
    def __init__(self, place, slots, conv_w_shard):
        self.place, self.slots, self.conv_w_shard = place, dict(slots), conv_w_shard
        self.conv_w = None
        self.g, self.from_sibling, self.pair_sums, self.arrived = {}, {}, {}, {}
        self.pending = []

    def weight(self, name):
        if name == "conv_w":
            return self.conv_w
        w = self.slots[name]
        return w.reshape(-1, D_MODEL) if name in ("w_out", "w_down") else w

    def project(self, h):
        proj, self.slots["w_in"] = _proj_gathered(h, self.slots["w_in"], self.place, "proj_in")
        return proj

    def grad(self, name, g):
        self.g[name] = g

    def carry(self, point):
        plans = []
        self.pending = SCHEDULE.get(point, [])
        for kind, names in self.pending:
            if kind in ("gather_ici", "gather_ici_cw"):
                wholes = [self.conv_w_shard] if kind == "gather_ici_cw" else []
                plans.append(_gather_ici_plan([self.slots[n] for n in names], wholes))
            elif kind == "gather_pass":
                plans.append(_gather_pass_plan([self.slots[n] for n in names]))
            elif kind == "pair":
                plans.append(_pair_plan([self.g[n] for n in names]))
            else:
                for n in names:
                    self.pair_sums[n] = _pair_sum(self.g[n], self.from_sibling[n], self.place[1:2], f"pair_sum_{n}")
                plans.append(_chip_plan([self.pair_sums[n] for n in names]))
        return plans

    def done(self, point, carried):
        for (kind, names), got in zip(self.pending, carried):
            if kind in ("gather_ici", "gather_ici_cw", "gather_pass"):
                self.slots.update(zip(names, got))
                if kind == "gather_ici_cw":
                    self.conv_w = got[len(names)].transpose(1, 0, 2).reshape(3, 2 * D_FF)
            elif kind == "pair":
                self.from_sibling.update(zip(names, got))
            else:
                self.arrived.update(zip(names, got))

    def reduced_halves(self):
        return [_chip_sum(self.arrived[n], self.pair_sums[n], self.place, f"chip_sum_{n}") for n in REDUCED]


def kernel(x, pre_mix_norm, w_in, rel_bias, hgrn_lb_raw, hgrn_norm, w_branch_attn, w_branch_hgrn, w_out, post_mix_norm, pre_ffn_norm, w_up, conv_w, conv_b, w_down, post_ffn_norm, loss_target, m_pre_mix_norm, m_w_in, m_rel_bias, m_hgrn_lb_raw, m_hgrn_norm, m_w_branch_attn, m_w_branch_hgrn, m_w_out, m_post_mix_norm, m_pre_ffn_norm, m_w_up, m_conv_w, m_conv_b, m_w_down, m_post_ffn_norm, v_pre_mix_norm, v_w_in, v_rel_bias, v_hgrn_lb_raw, v_hgrn_norm, v_w_branch_attn, v_w_branch_hgrn, v_w_out, v_post_mix_norm, v_pre_ffn_norm, v_w_up, v_conv_w, v_conv_b, v_w_down, v_post_ffn_norm):
    w = dict(pre_mix_norm=pre_mix_norm, w_in=w_in, rel_bias=rel_bias, hgrn_lb_raw=hgrn_lb_raw, hgrn_norm=hgrn_norm,
             w_branch_attn=w_branch_attn, w_branch_hgrn=w_branch_hgrn, w_out=w_out, post_mix_norm=post_mix_norm,
             pre_ffn_norm=pre_ffn_norm, w_up=w_up, conv_w=conv_w, conv_b=conv_b, w_down=w_down,
             post_ffn_norm=post_ffn_norm)
    m = dict(pre_mix_norm=m_pre_mix_norm, w_in=m_w_in, rel_bias=m_rel_bias, hgrn_lb_raw=m_hgrn_lb_raw,
             hgrn_norm=m_hgrn_norm, w_branch_attn=m_w_branch_attn, w_branch_hgrn=m_w_branch_hgrn, w_out=m_w_out,
             post_mix_norm=m_post_mix_norm, pre_ffn_norm=m_pre_ffn_norm, w_up=m_w_up, conv_w=m_conv_w,
             conv_b=m_conv_b, w_down=m_w_down, post_ffn_norm=m_post_ffn_norm)
    v = dict(pre_mix_norm=v_pre_mix_norm, w_in=v_w_in, rel_bias=v_rel_bias, hgrn_lb_raw=v_hgrn_lb_raw,
             hgrn_norm=v_hgrn_norm, w_branch_attn=v_w_branch_attn, w_branch_hgrn=v_w_branch_hgrn, w_out=v_w_out,
             post_mix_norm=v_post_mix_norm, pre_ffn_norm=v_pre_ffn_norm, w_up=v_w_up, conv_w=v_conv_w,
             conv_b=v_conv_b, w_down=v_w_down, post_ffn_norm=v_post_ffn_norm)
    shard2d = {n: (w[n][0] if w[n].ndim == 3 else w[n]) for n in WEIGHTS}
    chip = 2 * lax.axis_index("x") + lax.axis_index("y")
    core = lax.axis_index("c")

    place = jnp.stack([chip, core]).astype(jnp.int32)
    slots = {n: _cast_into_slot(shard2d[n], place, f"cast_{n}") for n in BIG}
    ex = _Exchange(place, slots, shard2d["conv_w"])
    loss, grad_x, small = _local_step(x[0], loss_target[0], {n: w[n] for n in SMALL if n != "conv_w"}, ex)

    flat = [small[n].reshape(-1) for n in SMALL] + [loss.reshape(-1)]
    sizes = [t.shape[0] for t in flat]
    summed, wholes = _all_sum(jnp.concatenate(flat).reshape(-1, LANES), ex.reduced_halves(), "sum_small")
    summed = summed.reshape(-1)
    offs = [sum(sizes[:i]) for i in range(len(sizes))]
    grads = {}
    for n, o, sz in zip(SMALL, offs, sizes):
        grads[n] = summed[o:o + sz].reshape(small[n].shape)
    loss_total = summed[offs[-1]]
    cw = 2 * D_FF // N_CHIPS
    grads["conv_w"] = lax.dynamic_slice(grads["conv_w"], (0, chip * cw), (3, cw))

    big = dict(zip(REDUCED, wholes))
    big["w_in"] = jnp.concatenate([big.pop(n) for n in W_IN_PIECES], axis=0)
    grads.update(big)

    out_g, out_d, out_m, out_v = [], [], [], []
    for n in WEIGHTS:
        d2, m2, v2 = _adamw(shard2d[n], grads[n], m[n].reshape(shard2d[n].shape), v[n].reshape(shard2d[n].shape),
                            f"adamw_{n}")
        shape = w[n].shape
        out_g.append(grads[n].reshape(shape))
        out_d.append(d2.reshape(shape))
        out_m.append(m2.reshape(shape))
        out_v.append(v2.reshape(shape))
    return (loss_total, grad_x[None], *out_g, *out_d, *out_m, *out_v)
```

```python
import functools
import math

import jax
import jax.numpy as jnp
from jax import lax
from jax.experimental import pallas as pl
from jax.experimental.pallas import tpu as pltpu

F32 = jnp.float32
BF16 = jnp.bfloat16
MESH = pl.DeviceIdType.MESH

D_MODEL = 1024
N_GROUPS = 3
DILATIONS = (1, 4, 16)
HEADS = 8
HEAD_DIM = 64
GROUP_W = HEADS * HEAD_DIM
QKV_W = N_GROUPS * 3 * GROUP_W
BLK = 128
NEG_INF = -1e30
NUM_BUCKETS = 32
MAX_EXACT = 16
MAX_DISTANCE = 2048
HG_HEADS = 4
HG_DK = 128
HG_W = HG_HEADS * HG_DK
HG_CHUNK = 32
HG_TILE = 256
IN_W = QKV_W + 4 * HG_W + 2 * D_MODEL
D_FF = 2816
EPS = 1e-6
N_CHIPS = 4
N_DEV = 8
LANES = 128

ADAM_LR, ADAM_B1, ADAM_B2, ADAM_EPS, ADAM_WD, ADAM_STEP = 0.001, 0.9, 0.999, 1e-08, 0.01, 10

VMEM_LIMIT = 56 * 1024 * 1024


def _cp(n_axes):
    return pltpu.CompilerParams(dimension_semantics=("arbitrary",) * n_axes, vmem_limit_bytes=VMEM_LIMIT)


def _sds(shape, dtype):
    return jax.ShapeDtypeStruct(tuple(shape), dtype)


def _sigmoid(v):
    return 1.0 / (1.0 + jnp.exp(-v))


def _bf(v):
    return v.astype(BF16)


def _dot(a, b, dims):
    return lax.dot_general(a, b, (dims, ((), ())), preferred_element_type=F32)


NN = ((1,), (0,))
NT = ((1,), (1,))
TN = ((0,), (0,))

ANY = pl.BlockSpec(memory_space=pl.ANY)


class _Plan:
    def __init__(self, copies, n_sems, ins=(), inouts=(), outs=()):
        self.copies, self.n_sems = copies, n_sems
        self.ins, self.inouts, self.outs = list(ins), list(inouts), list(outs)


def _call(body, plans=None, *, name, grid, in_specs, out_specs, out_shape, args, scratch_shapes=()):
    plans = list(plans or ())
    in_specs, out_specs, out_shape = list(in_specs), list(out_specs), list(out_shape)
    scratch_shapes = list(scratch_shapes)
    n_in, n_out, n_scr = len(in_specs), len(out_specs), len(scratch_shapes)
    x_in, x_out, aliases, spans = [], [], {}, []
    for p in plans:
        i0, o0 = len(x_in), len(x_out)
        x_in += p.ins
        for a in p.inouts:
            aliases[n_in + len(x_in)] = n_out + len(x_out)
            x_in.append(a)
            x_out.append(_sds(a.shape, a.dtype))
        x_out += p.outs
        spans.append((i0, len(p.ins), o0, len(p.inouts), len(p.outs)))
    sems = [pltpu.SemaphoreType.DMA((p.n_sems,)) for p in plans for _ in range(3)]

    def wrapped(*refs):
        xi = refs[n_in:n_in + len(x_in)]
        base = n_in + len(x_in)
        xo = refs[base + n_out:base + n_out + len(x_out)]
        sbase = base + n_out + len(x_out)
        xs = refs[sbase + n_scr:]
        ids = [pl.program_id(k) for k in range(len(grid))]
        first = functools.reduce(jnp.logical_and, [i == 0 for i in ids])
        last = functools.reduce(jnp.logical_and, [i == g - 1 for i, g in zip(ids, grid)])

        def descriptors(k):
            i0, ni, o0, nio, no = spans[k]
            return plans[k].copies(xi[i0:i0 + ni], xo[o0:o0 + nio], xo[o0 + nio:o0 + nio + no], *xs[3 * k:3 * k + 3])

        @pl.when(first)
        def _():
            for k in range(len(plans)):
                sends, _, local = descriptors(k)
                for cp in (*sends, *local):
                    cp.start()

        body(*refs[:n_in], *refs[base:base + n_out], *refs[sbase:sbase + n_scr])

        @pl.when(last)
        def _():
            for k in range(len(plans)):
                sends, recvs, local = descriptors(k)
                for cp in recvs:
                    cp.wait_recv()
                for cp in sends:
                    cp.wait_send()
                for cp in local:
                    cp.wait()

    res = pl.pallas_call(
        wrapped if plans else body, name=name, grid=grid, in_specs=in_specs + [ANY] * len(x_in),
        out_specs=out_specs + [ANY] * len(x_out), out_shape=out_shape + x_out, input_output_aliases=aliases,
        scratch_shapes=scratch_shapes + sems, compiler_params=_cp(len(grid)))(*args, *x_in)
    res = list(res)
    carried = [res[n_out + o0:n_out + o0 + nio + no] for (_, _, o0, nio, no) in spans]
    return res[:n_out], carried


def _mm_nn_blk(a, wg, name, tm=512, plans=None):
    M, K = a.shape
    nb, _, Nb = wg.shape

    def body(a_ref, w_ref, o_ref):
        o_ref[...] = _dot(_bf(a_ref[...]), w_ref[...], NN)

    (out,), carried = _call(
        body, plans, name=name, grid=(nb, M // tm),
        in_specs=[pl.BlockSpec((tm, K), lambda j, i: (i, 0)), pl.BlockSpec((None, K, Nb), lambda j, i: (j, 0, 0))],
        out_specs=[pl.BlockSpec((tm, Nb), lambda j, i: (i, j))],
        out_shape=[_sds((M, nb * Nb), F32)], args=(a, wg))
    return out if plans is None else (out, carried)


def _mm_nt_blk(dy, wg, name, tm=1024, plans=None):
    M = dy.shape[0]
    nb, K, Nb = wg.shape

    def body(dy_ref, w_ref, o_ref):
        j = pl.program_id(1)
        r = _dot(_bf(dy_ref[...]), w_ref[...], NT)

        @pl.when(j == 0)
        def _():
            o_ref[...] = r

        @pl.when(j > 0)
        def _():
            o_ref[...] += r

    (out,), carried = _call(
        body, plans, name=name, grid=(M // tm, nb),
        in_specs=[pl.BlockSpec((tm, Nb), lambda i, j: (i, j)), pl.BlockSpec((None, K, Nb), lambda i, j: (j, 0, 0))],
        out_specs=[pl.BlockSpec((tm, K), lambda i, j: (i, 0))],
        out_shape=[_sds((M, K), F32)], args=(dy, wg))
    return out if plans is None else (out, carried)


def _mm_tn_blk(x, dy, nb, name, tk=2048, x_cols=None, plans=None, together=False):
    T, Mx = x.shape
    xk, Mx = (0, Mx) if x_cols is None else x_cols
    Nb = dy.shape[1] // nb
    nj = nb if together else 1

    def body(x_ref, dy_ref, o_ref):
        t = pl.program_id(1)
        r = _dot(_bf(x_ref[...]), _bf(dy_ref[...]), TN)
        for j in range(nj):
            rj = r[:, j * Nb:(j + 1) * Nb]

            @pl.when(t == 0)
            def _():
                o_ref[j] = rj

            @pl.when(t > 0)
            def _():
                o_ref[j] += rj

    (out,), carried = _call(
        body, plans, name=name, grid=(nb // nj, T // tk),
        in_specs=[pl.BlockSpec((tk, Mx), lambda j, t: (t, xk)), pl.BlockSpec((tk, nj * Nb), lambda j, t: (t, j))],
        out_specs=[pl.BlockSpec((nj, Mx, Nb), lambda j, t: (j, 0, 0))],
        out_shape=[_sds((nb, Mx, Nb), F32)], args=(x, dy))
    return out if plans is None else (out, carried)


def _mm_tn(x, dy, name, tk=1024):
    T, Mx = x.shape
    N = dy.shape[1]

    def body(x_ref, dy_ref, o_ref):
        t = pl.program_id(0)
        r = _dot(_bf(x_ref[...]), _bf(dy_ref[...]), TN)

        @pl.when(t == 0)
        def _():
            o_ref[...] = r

        @pl.when(t > 0)
        def _():
            o_ref[...] += r

    return pl.pallas_call(
        body, name=name, grid=(T // tk,),
        in_specs=[pl.BlockSpec((tk, Mx), lambda t: (t, 0)), pl.BlockSpec((tk, N), lambda t: (t, 0))],
        out_specs=pl.BlockSpec((Mx, N), lambda t: (0, 0)),
        out_shape=_sds((Mx, N), F32), compiler_params=_cp(1))(x, dy)


def _tile(arr, bw, col=lambda c: 0):
    return ("tile", arr, bw, col)


def _full(arr):
    return ("full", arr)


def _out_tile(width, dtype, bw, col=lambda c: 0):
    return ("tile", width, dtype, bw, col)


def _out_acc(rows, width, bw, col=lambda c: 0):
    return ("acc", rows, width, bw, col)


def _rows_call(name, body, n_rows, tm, ncol, ins, outs, plans=None):
    in_specs, args = [], []
    for e in ins:
        if e[0] == "tile":
            _, arr, bw, col = e
            in_specs.append(pl.BlockSpec((tm, bw), functools.partial(lambda c, i, col: (i, col(c)), col=col)))
        else:
            arr = e[1]
            in_specs.append(pl.BlockSpec(arr.shape, functools.partial(lambda c, i, nd: (0,) * nd, nd=arr.ndim)))
        args.append(arr)
    out_specs, out_shape = [], []
    for e in outs:
        if e[0] == "tile":
            _, width, dtype, bw, col = e
            out_specs.append(pl.BlockSpec((tm, bw), functools.partial(lambda c, i, col: (i, col(c)), col=col)))
            out_shape.append(_sds((n_rows, width), dtype))
        else:
            _, rows, width, bw, col = e
            out_specs.append(pl.BlockSpec((rows, bw), functools.partial(lambda c, i, col: (0, col(c)), col=col)))
            out_shape.append(_sds((rows, width), F32))
    out, carried = _call(body, plans, name=name, grid=(ncol, n_rows // tm), in_specs=in_specs, out_specs=out_specs,
                         out_shape=out_shape, args=args)
    return out if plans is None else (out, carried)


def _acc(ref, val):
    i = pl.program_id(1)

    @pl.when(i == 0)
    def _():
        ref[...] = val

    @pl.when(i > 0)
    def _():
        ref[...] += val


def _rinv(z):
    return lax.rsqrt(jnp.mean(z * z, axis=-1, keepdims=True) + EPS)


def _norm_bwd(dy, zhat, r, w):
    dyw = dy * w
    return r * (dyw - zhat * jnp.mean(dyw * zhat, axis=-1, keepdims=True))


def _norm_fwd(x, w, name):
    def body(x_ref, w_ref, h_ref):
        xv = x_ref[...]
        h_ref[...] = _bf(xv * _rinv(xv) * w_ref[...])

    return _rows_call(name, body, x.shape[0], 512, 1, [_tile(x, D_MODEL), _full(w)],
                      [_out_tile(D_MODEL, BF16, D_MODEL)])[0]


def _prenorm_bwd(dh, xin, w, dres, name, plans=None):
    def body(dh_ref, x_ref, w_ref, dres_ref, dx_ref, dw_ref):
        xv = x_ref[...]
        r = _rinv(xv)
        xhat = xv * r
        dhv = dh_ref[...]
        dx_ref[...] = dres_ref[...] + _norm_bwd(dhv, xhat, r, w_ref[...])
        _acc(dw_ref, jnp.sum(dhv * xhat, axis=0, keepdims=True))

    return _rows_call(name, body, xin.shape[0], 512, 1,
                      [_tile(dh, D_MODEL), _tile(xin, D_MODEL), _full(w), _tile(dres, D_MODEL)],
                      [_out_tile(D_MODEL, F32, D_MODEL), _out_acc(1, D_MODEL, D_MODEL)], plans)


def _postnorm_bwd(dout, z, w, w_mat, name):
    def body(do_ref, z_ref, w_ref, wm_ref, dz_ref, dm_ref, dw_ref):
        zv = z_ref[...]
        r = _rinv(zv)
        zhat = zv * r
        dov = do_ref[...]
        dz = _bf(_norm_bwd(dov, zhat, r, w_ref[...]))
        dz_ref[...] = dz
        dm_ref[...] = _dot(dz, wm_ref[...], NT)
        _acc(dw_ref, jnp.sum(dov * zhat, axis=0, keepdims=True))

    return _rows_call(name, body, z.shape[0], 512, 1,
                      [_tile(dout, D_MODEL), _tile(z, D_MODEL), _full(w), _full(w_mat)],
                      [_out_tile(D_MODEL, BF16, D_MODEL), _out_tile(D_MODEL, F32, D_MODEL),
                       _out_acc(1, D_MODEL, D_MODEL)])


def _t5_bucket(dist):
    n = jnp.maximum(dist, 0)
    nf = jnp.maximum(n, 1).astype(F32)
    large = MAX_EXACT + (jnp.log(nf / MAX_EXACT) / math.log(MAX_DISTANCE / MAX_EXACT)
                         * (NUM_BUCKETS - MAX_EXACT)).astype(jnp.int32)
    large = jnp.minimum(large, NUM_BUCKETS - 1)
    return jnp.where(n < MAX_EXACT, n, large)


def _band_rel():
    return jnp.arange(BLK)[:, None] + BLK - jnp.arange(2 * BLK)[None, :]


def _band_valid():
    rel = _band_rel()
    window = (rel >= 0) & (rel <= BLK)
    first = window & (jnp.arange(2 * BLK)[None, :] >= BLK)
    return jnp.stack([first, window]).astype(F32).reshape(2, 1, BAND)


RES_UNROLL = 4
PAIR = LANES // HEAD_DIM


def _pair_lanes():
    first = lax.broadcasted_iota(jnp.int32, (1, LANES), 1) < HEAD_DIM
    return first, jnp.logical_not(first)


def _heads_per_step(d):
    return HEADS if d == 1 else LANES // HEAD_DIM


def _sub_rows(r, d):
    return pl.ds(r, BLK, stride=d) if d > 1 else pl.ds(0, BLK)


def _for_residues(d, fn):
    if d <= RES_UNROLL:
        for r in range(d):
            fn(r)
    else:
        def group(i, carry):
            for k in range(RES_UNROLL):
                fn(i * RES_UNROLL + k)
            return carry

        lax.fori_loop(0, d // RES_UNROLL, group, 0)


def _attn_specs(d, g, qblock):
    cw = _heads_per_step(d) * HEAD_DIM

    def col(part, hp):
        return (g * 3 + part) * (GROUP_W // cw) + hp

    def cur(part):
        return pl.BlockSpec((d * BLK, cw), lambda hp, n: (qblock(n), col(part, hp)))

    def prev(part):
        return pl.BlockSpec((d * BLK, cw), lambda hp, n: (jnp.maximum(qblock(n) - 1, 0), col(part, hp)))

    return cur, prev


def _attn_fwd(proj, bias, g, name, plans=None):
    S = proj.shape[0]
    d = DILATIONS[g]
    NB = S // (d * BLK)
    hps = _heads_per_step(d)

    def body(q_ref, kp_ref, kc_ref, vp_ref, vc_ref, b_ref, o_ref, lse_ref):
        hp = pl.program_id(0)
        later = jnp.minimum(pl.program_id(1), 1)

        def residue(r):
            rows = _sub_rows(r, d)
            q2 = q_ref[rows, :]
            k2 = jnp.concatenate([kp_ref[rows, :], kc_ref[rows, :]], axis=0)
            v2 = jnp.concatenate([vp_ref[rows, :], vc_ref[rows, :]], axis=0)
            outs, lses = [], []
            for pp in range(hps // PAIR):
                ps = slice(pp * LANES, (pp + 1) * LANES)
                qp, kp, vp = _bf(q2[:, ps]), _bf(k2[:, ps]), _bf(v2[:, ps])
                o_h, lse_h = [], []
                for hh, own in enumerate(_pair_lanes()):
                    s = _dot(qp, jnp.where(own, kp, 0), NT) * (HEAD_DIM ** -0.5) + b_ref[later, hp * hps + pp * PAIR + hh]
                    m = jnp.max(s, axis=-1, keepdims=True)
                    p = jnp.exp(s - m)
                    l = jnp.sum(p, axis=-1, keepdims=True)
                    o_h.append(_dot(_bf(p), vp, NN) / l)
                    lse_h.append(m + jnp.log(l))
                first = _pair_lanes()[0]
                outs.append(jnp.where(first, o_h[0], o_h[1]))
                lses.append(jnp.where(first, lse_h[0], lse_h[1]))
            o_ref[rows, :] = outs[0] if len(outs) == 1 else jnp.concatenate(outs, axis=1)
            lse_ref[rows, :] = lses[0] if len(lses) == 1 else jnp.concatenate(lses, axis=1)

        _for_residues(d, residue)

    cur, prev = _attn_specs(d, g, lambda n: n)
    out = pl.BlockSpec((d * BLK, hps * HEAD_DIM), lambda hp, n: (n, hp))
    res, carried = _call(
        body, plans, name=name, grid=(HEADS // hps, NB),
        in_specs=[cur(0), prev(1), cur(1), prev(2), cur(2),
                  pl.BlockSpec((2, HEADS, BLK, 2 * BLK), lambda hp, n: (0, 0, 0, 0))],
        out_specs=[out, out], out_shape=[_sds((S, GROUP_W), F32)] * 2,
        args=(proj, proj, proj, proj, proj, bias))
    return res if plans is None else (res, carried)


def _attn_bwd(proj, bias, lse, y, dy, g, name, plans=None):
    S = proj.shape[0]
    d = DILATIONS[g]
    NB = S // (d * BLK)
    hps = _heads_per_step(d)

    def body(q_ref, kp_ref, kc_ref, vp_ref, vc_ref, b_ref, l_ref, y_ref, dy_ref,
             dq_ref, dk_ref, dv_ref, db_ref, ck_ref, cv_ref):
        hp, n = pl.program_id(0), pl.program_id(1)

        @pl.when((hp == 0) & (n == 0))
        def _():
            db_ref[...] = jnp.zeros_like(db_ref)

        @pl.when(n == 0)
        def _():
            ck_ref[...] = jnp.zeros_like(ck_ref)
            cv_ref[...] = jnp.zeros_like(cv_ref)

        @pl.when(n < NB)
        def _():
            later = jnp.minimum(n, 1)

            def residue(r):
                rows = _sub_rows(r, d)
                q2 = q_ref[rows, :]
                k2 = jnp.concatenate([kp_ref[rows, :], kc_ref[rows, :]], axis=0)
                v2 = jnp.concatenate([vp_ref[rows, :], vc_ref[rows, :]], axis=0)
                l2, y2, dy2 = l_ref[rows, :], y_ref[rows, :], dy_ref[rows, :]
                dqs, dks, dvs = [], [], []
                for pp in range(hps // PAIR):
                    ps = slice(pp * LANES, (pp + 1) * LANES)
                    qp, kp, vp = _bf(q2[:, ps]), _bf(k2[:, ps]), _bf(v2[:, ps])
                    dyp, yp = dy2[:, ps], y2[:, ps]
                    dq_h, dk_h, dv_h = [], [], []
                    for hh, own in enumerate(_pair_lanes()):
                        head = hp * hps + pp * PAIR + hh
                        s = _dot(qp, jnp.where(own, kp, 0), NT) * (HEAD_DIM ** -0.5) + b_ref[later, head]
                        p = jnp.exp(s - l2[:, pp * LANES + hh * HEAD_DIM:pp * LANES + hh * HEAD_DIM + 1])
                        dyh = jnp.where(own, dyp, 0.0)
                        delta = jnp.sum(dyh * yp, axis=-1, keepdims=True)
                        ds = p * (_dot(_bf(dyh), vp, NT) - delta)
                        db_ref[head] += ds
                        dsb = _bf(ds * (HEAD_DIM ** -0.5))
                        dq_h.append(_dot(dsb, kp, NN))
                        dk_h.append(_dot(dsb, qp, TN))
                        dv_h.append(_dot(_bf(p), _bf(dyp), TN))
                    first = _pair_lanes()[0]
                    dqs.append(jnp.where(first, dq_h[0], dq_h[1]))
                    dks.append(jnp.where(first, dk_h[0], dk_h[1]))
                    dvs.append(jnp.where(first, dv_h[0], dv_h[1]))
                dkb = dks[0] if len(dks) == 1 else jnp.concatenate(dks, axis=1)
                dvb = dvs[0] if len(dvs) == 1 else jnp.concatenate(dvs, axis=1)
                dq_ref[rows, :] = dqs[0] if len(dqs) == 1 else jnp.concatenate(dqs, axis=1)
                dk_ref[rows, :] = ck_ref[rows, :] + dkb[:BLK]
                dv_ref[rows, :] = cv_ref[rows, :] + dvb[:BLK]
                ck_ref[rows, :] = dkb[BLK:]
                cv_ref[rows, :] = dvb[BLK:]

            _for_residues(d, residue)

        @pl.when(n == NB)
        def _():
            dk_ref[...] = ck_ref[...]
            dv_ref[...] = cv_ref[...]

    def qn(n):
        return jnp.minimum(n, NB - 1)

    cur, prev = _attn_specs(d, g, qn)
    cw = hps * HEAD_DIM
    row = pl.BlockSpec((d * BLK, cw), lambda hp, n: (qn(n), hp))
    done = pl.BlockSpec((d * BLK, cw), lambda hp, n: (jnp.maximum(n - 1, 0), hp))
    (dq, dk, dv, db), carried = _call(
        body, plans, name=name, grid=(HEADS // hps, NB + 1),
        in_specs=[cur(0), prev(1), cur(1), prev(2), cur(2),
                  pl.BlockSpec((2, HEADS, BLK, 2 * BLK), lambda hp, n: (0, 0, 0, 0)), row, row, row],
        out_specs=[row, done, done, pl.BlockSpec((HEADS, BLK, 2 * BLK), lambda hp, n: (0, 0, 0))],
        out_shape=[_sds((S, GROUP_W), F32)] * 3 + [_sds((HEADS, BLK, 2 * BLK), F32)],
        scratch_shapes=[pltpu.VMEM((d * BLK, cw), F32)] * 2,
        args=(proj, proj, proj, proj, proj, bias, lse, y, dy))
    return ([dq, dk, dv], db) if plans is None else ([dq, dk, dv], db, carried)


BAND = BLK * 2 * BLK


def _bucket_onehot():
    buckets = jnp.stack([_t5_bucket(_band_rel() * d) for d in DILATIONS]).reshape(N_GROUPS, 1, BAND)
    return (buckets == jnp.arange(NUM_BUCKETS).reshape(1, NUM_BUCKETS, 1)).astype(F32)


def _relbias_fwd(rel_bias, name):
    table = rel_bias.reshape(NUM_BUCKETS, N_GROUPS, HEADS).transpose(1, 0, 2)

    def body(t_ref, oh_ref, valid_ref, o_ref):
        bias = lax.dot_general(t_ref[...], oh_ref[...], (TN, ((), ())), preferred_element_type=F32,
                               precision=lax.Precision.HIGHEST)
        for k in range(2):
            o_ref[k] = jnp.where(valid_ref[k] > 0.5, bias, NEG_INF)

    out = pl.pallas_call(
        body, name=name, grid=(N_GROUPS,),
        in_specs=[pl.BlockSpec((None, NUM_BUCKETS, HEADS), lambda g: (g, 0, 0)),
                  pl.BlockSpec((None, NUM_BUCKETS, BAND), lambda g: (g, 0, 0)),
                  pl.BlockSpec((2, 1, BAND), lambda g: (0, 0, 0))],
        out_specs=pl.BlockSpec((None, 2, HEADS, BAND), lambda g: (g, 0, 0, 0)),
        out_shape=_sds((N_GROUPS, 2, HEADS, BAND), F32), compiler_params=_cp(1))(table, _bucket_onehot(), _band_valid())
    return out.reshape(N_GROUPS, 2, HEADS, BLK, 2 * BLK)


def _relbias_bwd(dbs, name):
    band = BAND
    onehot = _bucket_onehot()
    dbf = jnp.stack([db.reshape(HEADS, band) for db in dbs])

    def body(oh_ref, db_ref, o_ref):
        o_ref[...] = lax.dot_general(oh_ref[...], db_ref[...], (NT, ((), ())), preferred_element_type=F32,
                                     precision=lax.Precision.HIGHEST)

    out = pl.pallas_call(
        body, name=name, grid=(N_GROUPS,),
        in_specs=[pl.BlockSpec((None, NUM_BUCKETS, band), lambda g: (g, 0, 0)),
                  pl.BlockSpec((None, HEADS, band), lambda g: (g, 0, 0))],
        out_specs=pl.BlockSpec((None, NUM_BUCKETS, HEADS), lambda g: (g, 0, 0)),
        out_shape=_sds((N_GROUPS, NUM_BUCKETS, HEADS), F32), compiler_params=_cp(1))(onehot, dbf)
    return out.transpose(1, 0, 2).reshape(NUM_BUCKETS, N_GROUPS * HEADS)


def _chunk_pos(shape):
    return lax.broadcasted_iota(jnp.int32, shape, 0) % HG_CHUNK


def _chunk_cumsum(v):
    pos = _chunk_pos(v.shape)
    s = 1
    while s < HG_CHUNK:
        v = v + jnp.where(pos >= s, pltpu.roll(v, s, 0), 0.0)
        s *= 2
    return v


def _chunk_rev_cumsum(v):
    pos = _chunk_pos(v.shape)
    n = v.shape[0]
    s = 1
    while s < HG_CHUNK:
        v = v + jnp.where(pos < HG_CHUNK - s, pltpu.roll(v, n - s, 0), 0.0)
        s *= 2
    return v


def _lower_bound(raw):
    a0, a1 = raw[0:1], raw[1:2]
    m = jnp.maximum(a0, a1)
    e0, e1 = jnp.exp(a0 - m), jnp.exp(a1 - m)
    return e0 / (e0 + e1)


def _hg_gates(qr, fr, lb):
    sf = _sigmoid(fr)
    f = lb + (1.0 - lb) * sf
    sq = _sigmoid(qr)
    return qr * sq, sq, f, sf


HG_COL0 = QKV_W // HG_W


def _hgrn_fwd(proj, lb_raw, nw, name, plans=None):
    S = proj.shape[0]
    ncs = HG_TILE // HG_CHUNK
    tril = jnp.tril(jnp.ones((HG_CHUNK, HG_CHUNK), dtype=bool))

    def body(q_ref, f_ref, i_ref, og_ref, lb_ref, nw_ref, y_ref, o_ref, st_ref, state):
        @pl.when(pl.program_id(0) == 0)
        def _():
            state[...] = jnp.zeros_like(state)

        lb = _lower_bound(lb_ref[...])
        q, _, f, _ = _hg_gates(q_ref[...], f_ref[...], lb)
        k = 1.0 - f
        G = _chunk_cumsum(jnp.log(f))
        row = lax.broadcasted_iota(jnp.int32, (HG_CHUNK, HG_CHUNK), 0)
        col = lax.broadcasted_iota(jnp.int32, (HG_CHUNK, HG_CHUNK), 1)
        heads = [slice(h * HG_DK, (h + 1) * HG_DK) for h in range(HG_HEADS)]
        sts = [state[h] for h in range(HG_HEADS)]
        for c in range(ncs):
            cs = slice(c * HG_CHUNK, (c + 1) * HG_CHUNK)
            for h, hs in enumerate(heads):
                Gc = G[cs, hs]
                gl = Gc[HG_CHUNK - 1:HG_CHUNK]
                qt = _bf(q[cs, hs] * jnp.exp(Gc))
                kt = _bf(k[cs, hs] * jnp.exp(-Gc))
                kd = _bf(k[cs, hs] * jnp.exp(gl - Gc))
                v = _bf(i_ref[cs, hs])
                A = jnp.where(row >= col, _dot(qt, kt, NT), 0.0)
                o_ref[cs, hs] = _dot(_bf(A), v, NN) + _dot(qt, _bf(sts[h]), NT)
                st_ref[c, h] = sts[h]
                sts[h] = sts[h] * jnp.exp(gl) + _dot(v, kd, TN)
        for h, hs in enumerate(heads):
            state[h] = sts[h]
            oh = o_ref[:, hs]
            og = og_ref[:, hs]
            y_ref[:, hs] = oh * _rinv(oh) * nw_ref[...] * (og * _sigmoid(og))

    def colspec(j):
        return pl.BlockSpec((HG_TILE, HG_W), lambda i: (i, HG_COL0 + j))

    res, carried = _call(
        body, plans, name=name, grid=(S // HG_TILE,),
        in_specs=[colspec(0), colspec(1), colspec(2), colspec(3),
                  pl.BlockSpec((2, HG_W), lambda i: (0, 0)), pl.BlockSpec((1, HG_DK), lambda i: (0, 0))],
        out_specs=[pl.BlockSpec((HG_TILE, HG_W), lambda i: (i, 0))] * 2
        + [pl.BlockSpec((ncs, HG_HEADS, HG_DK, HG_DK), lambda i: (i, 0, 0, 0))],
        out_shape=[_sds((S, HG_W), F32)] * 2 + [_sds((S // HG_CHUNK, HG_HEADS, HG_DK, HG_DK), F32)],
        scratch_shapes=[pltpu.VMEM((HG_HEADS, HG_DK, HG_DK), F32)],
        args=(proj, proj, proj, proj, lb_raw, nw))
    return res if plans is None else (res, carried)


def _hgrn_bwd(proj, lb_raw, nw, o, states, dy, d_attn, d_gates, name):
    S = proj.shape[0]
    ncs = HG_TILE // HG_CHUNK
    nt = S // HG_TILE
    n_a, n_g = len(d_attn), len(d_gates)
    own = [slice(QKV_W + j * HG_W, QKV_W + (j + 1) * HG_W) for j in range(4)]

    def body(q_ref, f_ref, i_ref, og_ref, lb_ref, nw_ref, o_ref, st_ref, dy_ref, *rest):
        attn_refs, gate_refs = rest[:n_a], rest[n_a:n_a + n_g]
        dp_ref, dlb_ref, dnw_ref, dstate, do_s, dG_s, dgl_s, dk_s, dlb_s = rest[n_a + n_g:]
        dq_ref, df_ref, di_ref, dog_ref = (dp_ref.at[:, cols] for cols in own)
        step = pl.program_id(0)
        for k, a_ref in enumerate(attn_refs):
            dp_ref[:, k * GROUP_W:(k + 1) * GROUP_W] = _bf(a_ref[...])
        for k, g_ref in enumerate(gate_refs):
            dp_ref[:, QKV_W + 4 * HG_W + k * D_MODEL:QKV_W + 4 * HG_W + (k + 1) * D_MODEL] = g_ref[...]

        @pl.when(step == 0)
        def _():
            dstate[...] = jnp.zeros_like(dstate)
            dlb_s[...] = jnp.zeros_like(dlb_s)
            dnw_ref[...] = jnp.zeros_like(dnw_ref)

        lb = _lower_bound(lb_ref[...])
        qr = q_ref[...]
        q, sq, f, sf = _hg_gates(qr, f_ref[...], lb)
        k = 1.0 - f
        G = _chunk_cumsum(jnp.log(f))
        nwv = nw_ref[...]
        row = lax.broadcasted_iota(jnp.int32, (HG_CHUNK, HG_CHUNK), 0)
        col = lax.broadcasted_iota(jnp.int32, (HG_CHUNK, HG_CHUNK), 1)
        for h in range(HG_HEADS):
            hs = slice(h * HG_DK, (h + 1) * HG_DK)
            oh = o_ref[:, hs]
            r = _rinv(oh)
            ohat = oh * r
            og = og_ref[:, hs]
            sg = _sigmoid(og)
            dyh = dy_ref[:, hs]
            don = dyh * (og * sg)
            dog_ref[:, hs] = _bf(dyh * (ohat * nwv) * (sg * (1.0 + og * (1.0 - sg))))
            dnw_ref[...] += jnp.sum(don * ohat, axis=0, keepdims=True)
            do_s[:, hs] = _norm_bwd(don, ohat, r, nwv)
        dsts = [dstate[h] for h in range(HG_HEADS)]
        for c in reversed(range(ncs)):
            cs = slice(c * HG_CHUNK, (c + 1) * HG_CHUNK)
            for h in range(HG_HEADS):
                hs = slice(h * HG_DK, (h + 1) * HG_DK)
                dst = dsts[h]
                Gc = G[cs, hs]
                gl = Gc[HG_CHUNK - 1:HG_CHUNK]
                eG, enG, edG, egl = jnp.exp(Gc), jnp.exp(-Gc), jnp.exp(gl - Gc), jnp.exp(gl)
                qt, kt, kd = q[cs, hs] * eG, k[cs, hs] * enG, k[cs, hs] * edG
                qtb, ktb, kdb = _bf(qt), _bf(kt), _bf(kd)
                v = _bf(i_ref[cs, hs])
                do = _bf(do_s[cs, hs])
                st = st_ref[c, h]
                dstb = _bf(dst)
                A = jnp.where(row >= col, _dot(qtb, ktb, NT), 0.0)
                dA = _bf(jnp.where(row >= col, _dot(do, v, NT), 0.0))
                di_ref[cs, hs] = _bf(_dot(_bf(A), do, TN) + _dot(kdb, dstb, NT))
                dqt = _dot(dA, ktb, NN) + _dot(do, _bf(st), NN)
                dkt = _dot(dA, qtb, TN)
                dkd = _dot(v, dstb, NN)
                dgl = egl * jnp.sum(st * dst, axis=0, keepdims=True) + jnp.sum(dkd * kd, axis=0, keepdims=True)
                dsts[h] = dst * egl + _dot(do, qtb, TN)
                dq_ref[cs, hs] = _bf(dqt * eG * (sq[cs, hs] * (1.0 + qr[cs, hs] * (1.0 - sq[cs, hs]))))
                dk_s[cs, hs] = dkt * enG + dkd * edG
                dG_s[cs, hs] = dqt * qt - dkt * kt - dkd * kd
                dgl_s[cs, hs] = jnp.broadcast_to(dgl, (HG_CHUNK, HG_DK))
        for h in range(HG_HEADS):
            dstate[h] = dsts[h]
        dg = _chunk_rev_cumsum(dG_s[...]) + dgl_s[...]
        dfv = dg / f - dk_s[...]
        df_ref[...] = _bf(dfv * (1.0 - lb) * sf * (1.0 - sf))
        dlb_s[...] += jnp.sum(dfv * (1.0 - sf), axis=0, keepdims=True)

        @pl.when(step == nt - 1)
        def _():
            t = dlb_s[...] * lb * (1.0 - lb)
            dlb_ref[...] = jnp.concatenate([t, -t], axis=0)

    def colspec(j):
        return pl.BlockSpec((HG_TILE, HG_W), lambda i: (nt - 1 - i, HG_COL0 + j))

    def rows(width):
        return pl.BlockSpec((HG_TILE, width), lambda i: (nt - 1 - i, 0))

    tile = rows(HG_W)
    return pl.pallas_call(
        body, name=name, grid=(nt,),
        in_specs=[colspec(0), colspec(1), colspec(2), colspec(3),
                  pl.BlockSpec((2, HG_W), lambda i: (0, 0)), pl.BlockSpec((1, HG_DK), lambda i: (0, 0)),
                  tile, pl.BlockSpec((ncs, HG_HEADS, HG_DK, HG_DK), lambda i: (nt - 1 - i, 0, 0, 0)), tile]
        + [rows(GROUP_W)] * n_a + [rows(D_MODEL)] * n_g,
        out_specs=[rows(IN_W), pl.BlockSpec((2, HG_W), lambda i: (0, 0)), pl.BlockSpec((1, HG_DK), lambda i: (0, 0))],
        out_shape=[_sds((S, IN_W), BF16), _sds((2, HG_W), F32), _sds((1, HG_DK), F32)],
        scratch_shapes=[pltpu.VMEM((HG_HEADS, HG_DK, HG_DK), F32)] + [pltpu.VMEM((HG_TILE, HG_W), F32)] * 4
        + [pltpu.VMEM((1, HG_W), F32)],
        compiler_params=_cp(1))(proj, proj, proj, proj, lb_raw, nw, o, states, dy, *d_attn, *d_gates)


GATE_COL0 = (QKV_W + 4 * HG_W) // GROUP_W
HALF_D = D_MODEL // 2


def _gate_tiles(proj):
    return [_tile(proj, HALF_D, functools.partial(lambda c, k: GATE_COL0 + k, k=k)) for k in range(4)]


def _gates(g_refs):
    s0 = _sigmoid(jnp.concatenate([g_refs[0][...], g_refs[1][...]], axis=1))
    s1 = _sigmoid(jnp.concatenate([g_refs[2][...], g_refs[3][...]], axis=1))
    return s0, s1


def _branch_fwd(os_, lses, yh, proj, w_a, w_h, name):
    nb = w_a.shape[0]

    def body(o0, o1, o2, l0, l1, l2, yh_ref, g0a, g0b, g1a, g1b, wa_ref, wh_ref,
             y_ref, lse_ref, za_ref, zh_ref, m_ref):
        a, b, c = l0[...], l1[...], l2[...]
        m = jnp.maximum(jnp.maximum(a, b), c)
        ea, eb, ec = jnp.exp(a - m), jnp.exp(b - m), jnp.exp(c - m)
        den = ea + eb + ec
        y = (ea * o0[...] + eb * o1[...] + ec * o2[...]) / den
        y_ref[...] = y
        lse_ref[...] = m + jnp.log(den)
        yb, yhb = _bf(y), _bf(yh_ref[...])
        za = jnp.concatenate([_dot(yb, wa_ref[j], NN) for j in range(nb)], axis=1)
        zh = jnp.concatenate([_dot(yhb, wh_ref[j], NN) for j in range(nb)], axis=1)
        s0, s1 = _gates((g0a, g0b, g1a, g1b))
        za_ref[...] = za
        zh_ref[...] = zh
        m_ref[...] = _bf(s0 * za + s1 * zh)

    return _rows_call(name, body, yh.shape[0], 512, 1,
                      [*[_tile(t, GROUP_W) for t in (*os_, *lses)], _tile(yh, HG_W), *_gate_tiles(proj),
                       _full(w_a), _full(w_h)],
                      [_out_tile(GROUP_W, F32, GROUP_W)] * 2 + [_out_tile(D_MODEL, F32, D_MODEL)] * 2
                      + [_out_tile(D_MODEL, BF16, D_MODEL)])


def _branch_bwd(dm, za, zh, proj, w_a, w_h, name, plans=None):
    nb, _, Nb = w_a.shape

    def body(dm_ref, za_ref, zh_ref, g0a, g0b, g1a, g1b, wa_ref, wh_ref,
             dza_ref, dzh_ref, dg0_ref, dg1_ref, dy_ref, dyh_ref):
        dmv = dm_ref[...]
        s0, s1 = _gates((g0a, g0b, g1a, g1b))
        dza, dzh = _bf(dmv * s0), _bf(dmv * s1)
        dza_ref[...] = dza
        dzh_ref[...] = dzh
        dg0_ref[...] = _bf(dmv * za_ref[...] * s0 * (1.0 - s0))
        dg1_ref[...] = _bf(dmv * zh_ref[...] * s1 * (1.0 - s1))
        dy_ref[...] = sum(_dot(dza[:, j * Nb:(j + 1) * Nb], wa_ref[j], NT) for j in range(nb))
        dyh_ref[...] = sum(_dot(dzh[:, j * Nb:(j + 1) * Nb], wh_ref[j], NT) for j in range(nb))

    return _rows_call(name, body, za.shape[0], 512, 1,
                      [_tile(dm, D_MODEL), _tile(za, D_MODEL), _tile(zh, D_MODEL), *_gate_tiles(proj),
                       _full(w_a), _full(w_h)],
                      [_out_tile(D_MODEL, BF16, D_MODEL)] * 4 + [_out_tile(GROUP_W, F32, GROUP_W),
                                                                 _out_tile(HG_W, F32, HG_W)], plans)


def _mix_out(merged, w_out, x, w_post, w_pre, name):
    def body(m_ref, wo_ref, x_ref, wp_ref, wf_ref, mo_ref, x1_ref, h2_ref):
        z = _dot(m_ref[...], wo_ref[...], NN)
        mo_ref[...] = z
        x1 = x_ref[...] + z * _rinv(z) * wp_ref[...]
        x1_ref[...] = x1
        h2_ref[...] = _bf(x1 * _rinv(x1) * wf_ref[...])

    return _rows_call(name, body, x.shape[0], 512, 1,
                      [_tile(merged, D_MODEL), _full(w_out), _tile(x, D_MODEL), _full(w_post), _full(w_pre)],
                      [_out_tile(D_MODEL, F32, D_MODEL), _out_tile(D_MODEL, F32, D_MODEL),
                       _out_tile(D_MODEL, BF16, D_MODEL)])


def _loss_head(a, w_down, x1, tgt, w, name):
    def body(a_ref, wd_ref, x1_ref, t_ref, w_ref, dx_ref, df_ref, dw_ref, loss_ref):
        z = _dot(a_ref[...], wd_ref[...], NN)
        r = _rinv(z)
        zhat = z * r
        wv = w_ref[...]
        e = x1_ref[...] + zhat * wv - t_ref[...]
        dx = e * (1.0 / D_MODEL)
        dx_ref[...] = dx
        df_ref[...] = _bf(_norm_bwd(dx, zhat, r, wv))
        _acc(dw_ref, jnp.sum(dx * zhat, axis=0, keepdims=True))
        part = 0.5 * jnp.sum(jnp.sum(e * e, axis=1, keepdims=True), axis=0, keepdims=True) * (1.0 / D_MODEL)
        _acc(loss_ref, jnp.broadcast_to(part, (1, LANES)))

    return _rows_call(name, body, x1.shape[0], 512, 1,
                      [_tile(a, D_FF), _full(w_down), _tile(x1, D_MODEL), _tile(tgt, D_MODEL), _full(w)],
                      [_out_tile(D_MODEL, F32, D_MODEL), _out_tile(D_MODEL, BF16, D_MODEL),
                       _out_acc(1, D_MODEL, D_MODEL), _out_acc(1, LANES, LANES)])


CONV_CB = D_FF // 2
CONV_TM = 512
HALO = 8
SQRT_HALF = 0.7071067811865476
INV_SQRT_2PI = 0.3989422804014327


CONV_RS = 32


def _lane_tiles():
    return [slice(k * LANES, (k + 1) * LANES) for k in range(CONV_CB // LANES)]


def _strip_start(i):
    return pl.multiple_of(i * CONV_RS, CONV_RS)


def _strip_taps(u_ref, halo_ref, r0, cs, first_strip, first_tile):
    if first_strip:
        before = jnp.where(first_tile, 0.0, halo_ref[:, cs])
        blk = jnp.concatenate([before, u_ref[0:CONV_RS, cs]], axis=0)
    else:
        blk = u_ref[pl.ds(pl.multiple_of(r0 - HALO, HALO), CONV_RS + HALO), cs]
    return pltpu.roll(blk, 2, 0)[HALO:], pltpu.roll(blk, 1, 0)[HALO:], blk[HALO:]


def _conv(taps, w_ref, b_ref, cs):
    return b_ref[:, cs] + w_ref[0:1, cs] * taps[0] + w_ref[1:2, cs] * taps[1] + w_ref[2:3, cs] * taps[2]


def _conv_specs(tm):
    nh = tm // HALO
    nc = D_FF // CONV_CB

    def tile(off):
        return pl.BlockSpec((tm, CONV_CB), lambda c, i: (i, off + c))

    def halo(off):
        return pl.BlockSpec((HALO, CONV_CB), lambda c, i: (jnp.maximum(i * nh - 1, 0), off + c))

    def small(rows, off):
        return pl.BlockSpec((rows, CONV_CB), lambda c, i: (0, off + c))

    return nc, tile, halo, small


def _conv_gelu_fwd(u, cw, cb, name, plans=None):
    S = u.shape[0]
    tm = CONV_TM
    nc, tile, halo, small = _conv_specs(tm)

    def body(ug, hg, uv, hv, wg, wv, bg, bv, a_ref):
        first_tile = pl.program_id(1) == 0

        def strip(r0, first_strip):
            for cs in _lane_tiles():
                cg = _conv(_strip_taps(ug, hg, r0, cs, first_strip, first_tile), wg, bg, cs)
                cv = _conv(_strip_taps(uv, hv, r0, cs, first_strip, first_tile), wv, bv, cs)
                a_ref[pl.ds(r0, CONV_RS), cs] = _bf(0.5 * cg * (1.0 + lax.erf(cg * SQRT_HALF)) * cv)

        strip(0, True)
        lax.fori_loop(1, tm // CONV_RS, lambda k, c: (strip(_strip_start(k), False), c)[1], 0)

    (a,), carried = _call(
        body, plans, name=name, grid=(nc, S // tm),
        in_specs=[tile(0), halo(0), tile(nc), halo(nc), small(3, 0), small(3, nc), small(1, 0), small(1, nc)],
        out_specs=[tile(0)], out_shape=[_sds((S, D_FF), BF16)], args=(u, u, u, u, cw, cw, cb, cb))
    return a if plans is None else (a, carried)


def _conv_gelu_bwd(u, dff, w_down, cw, cb, name, plans=None):
    S = u.shape[0]
    tm = CONV_TM
    nt = S // tm
    nc, tile, halo, small = _conv_specs(tm)

    def body(ug, hg, uv, hv, wg, wv, bg, bv, dff_ref, wd_ref, dcg_ref, dcv_ref, dwg_ref, dwv_ref, dbg_ref, dbv_ref,
             acc, da_ref):
        i = pl.program_id(1)
        first_tile = i == 0
        da_ref[...] = _dot(dff_ref[...], wd_ref[...], NT)

        @pl.when(first_tile)
        def _():
            acc[...] = jnp.zeros_like(acc)

        def strip(r0, first_strip):
            rows = pl.ds(r0, CONV_RS)
            for cs in _lane_tiles():
                tg = _strip_taps(ug, hg, r0, cs, first_strip, first_tile)
                tv = _strip_taps(uv, hv, r0, cs, first_strip, first_tile)
                cg = _conv(tg, wg, bg, cs)
                cv = _conv(tv, wv, bv, cs)
                phi = 0.5 * (1.0 + lax.erf(cg * SQRT_HALF))
                dav = da_ref[rows, cs]
                dcg = dav * cv * (phi + cg * jnp.exp(-0.5 * cg * cg) * INV_SQRT_2PI)
                dcv = dav * (cg * phi)
                dcg_ref[rows, cs] = dcg
                dcv_ref[rows, cs] = dcv
                for half, (dc, taps) in enumerate(((dcg, tg), (dcv, tv))):
                    for j in range(3):
                        acc[4 * half + j, :, cs] += dc * taps[j]
                    acc[4 * half + 3, :, cs] += dc

        strip(0, True)
        lax.fori_loop(1, tm // CONV_RS, lambda k, c: (strip(_strip_start(k), False), c)[1], 0)

        @pl.when(i == nt - 1)
        def _():
            for half, (dw_ref, db_ref) in enumerate(((dwg_ref, dbg_ref), (dwv_ref, dbv_ref))):
                for j in range(3):
                    dw_ref[j:j + 1, :] = jnp.sum(acc[4 * half + j], axis=0, keepdims=True)
                db_ref[...] = jnp.sum(acc[4 * half + 3], axis=0, keepdims=True)

    res, carried = _call(
        body, plans, name=name, grid=(nc, nt),
        in_specs=[tile(0), halo(0), tile(nc), halo(nc), small(3, 0), small(3, nc), small(1, 0), small(1, nc),
                  pl.BlockSpec((tm, D_MODEL), lambda c, i: (i, 0)), pl.BlockSpec((CONV_CB, D_MODEL), lambda c, i: (c, 0))],
        out_specs=[tile(0), tile(0), small(3, 0), small(3, 0), small(1, 0), small(1, 0)],
        out_shape=[_sds((S, D_FF), F32)] * 2 + [_sds((3, D_FF), F32)] * 2 + [_sds((1, D_FF), F32)] * 2,
        scratch_shapes=[pltpu.VMEM((8, CONV_RS, CONV_CB), F32), pltpu.VMEM((tm, CONV_CB), F32)],
        args=(u, u, u, u, cw, cw, cb, cb, dff, w_down))
    return res if plans is None else (res, carried)


def _conv_input_bwd(dcg, dcv, cw, name, plans=None):
    S = dcg.shape[0]
    tm = CONV_TM // 2
    nh = tm // HALO
    nt = S // tm
    n = CONV_RS + HALO
    tile = pl.BlockSpec((tm, D_FF), lambda i: (i, 0))
    nxt = pl.BlockSpec((HALO, D_FF), lambda i: (jnp.minimum((i + 1) * nh, S // HALO - 1), 0))

    def body(g_ref, ng_ref, v_ref, nv_ref, w_ref, du_ref):
        last_tile = pl.program_id(0) == nt - 1

        def strip(r0, last_strip):
            for half, (dc_ref, n_ref) in enumerate(((g_ref, ng_ref), (v_ref, nv_ref))):
                for k in range(D_FF // LANES):
                    cs = slice(k * LANES, (k + 1) * LANES)
                    ws = slice(half * D_FF + k * LANES, half * D_FF + (k + 1) * LANES)
                    if last_strip:
                        after = jnp.where(last_tile, 0.0, n_ref[:, cs])
                        blk = jnp.concatenate([dc_ref[tm - CONV_RS:tm, cs], after], axis=0)
                    else:
                        blk = dc_ref[pl.ds(r0, n), cs]
                    d1 = pltpu.roll(blk, n - 1, 0)[:CONV_RS]
                    d2 = pltpu.roll(blk, n - 2, 0)[:CONV_RS]
                    du_ref[pl.ds(r0, CONV_RS), ws] = _bf(w_ref[2:3, ws] * blk[:CONV_RS] + w_ref[1:2, ws] * d1
                                                         + w_ref[0:1, ws] * d2)

        lax.fori_loop(0, tm // CONV_RS - 1, lambda k, c: (strip(_strip_start(k), False), c)[1], 0)
        strip(tm - CONV_RS, True)

    (du,), carried = _call(
        body, plans, name=name, grid=(nt,),
        in_specs=[tile, nxt, tile, nxt, pl.BlockSpec((3, 2 * D_FF), lambda i: (0, 0))],
        out_specs=[pl.BlockSpec((tm, 2 * D_FF), lambda i: (i, 0))], out_shape=[_sds((S, 2 * D_FF), BF16)],
        args=(dcg, dcg, dcv, dcv, cw))
    return du if plans is None else (du, carried)


def _row_tile(n, cap):
    best = n
    for t in range(16, cap + 1, 16):
        if n % t == 0:
            best = t
    return best if best <= cap else n


def _rows_for_bytes(nbytes, cols):
    return max(16, nbytes // (4 * cols) // 16 * 16)


def _adamw(w, g, m, v, name):
    R, C = w.shape
    tr = _row_tile(R, _rows_for_bytes(2 << 20, C))

    def body(w_ref, g_ref, m_ref, v_ref, d_ref, nm_ref, nv_ref):
        gv = g_ref[...]
        nm = ADAM_B1 * m_ref[...] + (1.0 - ADAM_B1) * gv
        nv = ADAM_B2 * v_ref[...] + (1.0 - ADAM_B2) * (gv * gv)
        m_hat = nm / (1.0 - ADAM_B1 ** ADAM_STEP)
        v_hat = nv / (1.0 - ADAM_B2 ** ADAM_STEP)
        d_ref[...] = -ADAM_LR * (m_hat / (jnp.sqrt(v_hat) + ADAM_EPS) + ADAM_WD * w_ref[...])
        nm_ref[...] = nm
        nv_ref[...] = nv

    spec = pl.BlockSpec((tr, C), lambda i: (i, 0))
    return pl.pallas_call(body, name=name, grid=(R // tr,), in_specs=[spec] * 4, out_specs=[spec] * 3,
                          out_shape=[_sds((R, C), F32)] * 3, compiler_params=_cp(1))(w, g, m, v)


def _pair_sum(gfull, rcv, c_idx, name):
    nb, R, C = gfull.shape
    half = R // 2
    tr = _row_tile(half, _rows_for_bytes(2 << 20, C))
    nt = half // tr

    def body(c_ref, g_ref, r_ref, o_ref):
        o_ref[...] = _bf(g_ref[...] + r_ref[...])

    return pl.pallas_call(
        body, name=name,
        grid_spec=pltpu.PrefetchScalarGridSpec(
            num_scalar_prefetch=1, grid=(nb, nt),
            in_specs=[pl.BlockSpec((None, tr, C), lambda j, i, c_ref: (j, c_ref[0] * nt + i, 0)),
                      pl.BlockSpec((None, tr, C), lambda j, i, c_ref: (j, i, 0))],
            out_specs=pl.BlockSpec((None, tr, C), lambda j, i, c_ref: (j, i, 0))),
        out_shape=_sds((nb, half, C), BF16), compiler_params=_cp(2))(c_idx, gfull, rcv)


def _chip_sum(arrived, own, place, name):
    nb, H, C = arrived.shape
    tr = _row_tile(H, _rows_for_bytes(2 << 20, C))
    nt = H // tr

    def body(pl_ref, *refs):
        o_ref = refs[nb + 1]
        me = pl_ref[0]
        acc = None
        for k in range(nb):
            term = jnp.where(me == k, refs[nb][...], refs[k][...]).astype(F32)
            acc = term if acc is None else acc + term
        o_ref[...] = acc

    def other(k):
        return pl.BlockSpec((None, tr, C), lambda i, p: (jnp.where(p[0] == k, (k + 1) % nb, k), i, 0))

    return pl.pallas_call(
        body, name=name,
        grid_spec=pltpu.PrefetchScalarGridSpec(
            num_scalar_prefetch=1, grid=(nt,),
            in_specs=[other(k) for k in range(nb)] + [pl.BlockSpec((None, tr, C), lambda i, p: (p[0], i, 0))],
            out_specs=pl.BlockSpec((tr, C), lambda i, p: (p[1] * nt + i, 0))),
        out_shape=_sds((2 * H, C), F32), compiler_params=_cp(1))(place, *([arrived] * nb), own)


def _cast_into_slot(shard, place, name):
    R, C = shard.shape
    tr = _row_tile(R, 256)

    def body(pl_ref, s_ref, o_ref):
        o_ref[...] = _bf(s_ref[...])

    return pl.pallas_call(
        body, name=name,
        grid_spec=pltpu.PrefetchScalarGridSpec(
            num_scalar_prefetch=1, grid=(R // tr,),
            in_specs=[pl.BlockSpec((tr, C), lambda i, p: (i, 0))],
            out_specs=pl.BlockSpec((None, tr, C), lambda i, p: (p[0], i, 0))),
        out_shape=_sds((N_CHIPS, R, C), BF16), compiler_params=_cp(1))(place, shard)


def _place():
    x, y, c = lax.axis_index("x"), lax.axis_index("y"), lax.axis_index("c")
    chips = [(1 - x, y), (x, 1 - y), (1 - x, 1 - y)]
    return x, y, c, chips


def _chip_id(px, py):
    return 2 * px + py


def _remote(src, dst, send_sems, recv_sems, k, to):
    return pltpu.make_async_remote_copy(src_ref=src, dst_ref=dst, send_sem=send_sems.at[k], recv_sem=recv_sems.at[k],
                                        device_id=to, device_id_type=MESH)


LOOKAHEAD = 3


def _proj_gathered(h, slot, place, name, tm=512):
    M, K = h.shape
    nb, _, Nb = slot.shape
    half = K // 2
    nt = M // tm
    cx, cy = place[0] // 2, place[0] % 2
    order = jnp.stack([place[0], _chip_id(1 - cx, cy), _chip_id(cx, 1 - cy), _chip_id(1 - cx, 1 - cy)]).astype(jnp.int32)

    def body(order_ref, h_ref, slot_in, o_ref, slot_ref, w_buf, ici_send, ici_recv, pass_send, pass_recv, load_sem):
        b, i = pl.program_id(0), pl.program_id(1)
        x, y, c, chips = _place()
        me = _chip_id(x, y)
        sib = (x, y, 1 - c)
        mine, other = pl.ds(c * half, half), pl.ds((1 - c) * half, half)

        def sent(k):
            blk = slot_ref.at[me, mine]
            return _remote(blk, blk, ici_send, ici_recv, k, (*chips[k], c))

        def landed(k):
            blk = slot_ref.at[_chip_id(*chips[k]), mine]
            return _remote(blk, blk, ici_send, ici_recv, k, (*chips[k], c))

        def passed(k, rows):
            blk = slot_ref.at[_chip_id(*chips[k]), rows]
            return _remote(blk, blk, pass_send, pass_recv, k, sib)

        @pl.when((b == 0) & (i == 0))
        def _():
            for k in range(len(chips)):
                sent(k).start()

        def load(blk):
            return pltpu.make_async_copy(slot_ref.at[order_ref[blk]], w_buf.at[blk % 2], load_sem.at[blk % 2])

        @pl.when((b == 0) & (i == 0))
        def _():
            load(0).start()

        for k in range(len(chips)):
            @pl.when((b == k) & (i == nt - LOOKAHEAD))
            def _(k=k):
                landed(k).wait_recv()
                passed(k, mine).start()
                passed(k, other).wait_recv()
                load(k + 1).start()

        for blk in range(nb):
            @pl.when((b == blk) & (i == 0))
            def _(blk=blk):
                load(blk).wait()

        o_ref[...] = _dot(h_ref[...], w_buf[b % 2], NN)

        @pl.when((b == nb - 1) & (i == nt - 1))
        def _():
            for k in range(len(chips)):
                sent(k).wait_send()
                passed(k, mine).wait_send()

    n_peers = N_CHIPS - 1
    return pl.pallas_call(
        body, name=name,
        grid_spec=pltpu.PrefetchScalarGridSpec(
            num_scalar_prefetch=1, grid=(nb, nt),
            in_specs=[pl.BlockSpec((tm, K), lambda b, i, o: (i, 0)), ANY],
            out_specs=[pl.BlockSpec((tm, Nb), lambda b, i, o: (i, o[b])), ANY],
            scratch_shapes=[pltpu.VMEM((2, K, Nb), BF16)] + [pltpu.SemaphoreType.DMA((n_peers,))] * 4
            + [pltpu.SemaphoreType.DMA((2,))]),
        out_shape=[_sds((M, nb * Nb), F32), _sds(slot.shape, slot.dtype)],
        input_output_aliases={2: 1}, compiler_params=_cp(2))(order, h, slot)


def _gather_ici_plan(slots, wholes):
    ns, nw = len(slots), len(wholes)

    def copies(ins, ios, outs, send_sems, recv_sems, local_sems):
        x, y, c, chips = _place()
        me = _chip_id(x, y)
        sends, recvs = [], []
        for a in range(ns + nw):
            dst = ios[a] if a < ns else outs[a - ns]
            R = dst.shape[1]
            rows = pl.ds(c * (R // 2), R // 2) if a < ns else pl.ds(0, R)
            src = dst.at[me, rows] if a < ns else ins[a - ns]
            for j, chip in enumerate(chips):
                sends.append(_remote(src, dst.at[me, rows], send_sems, recv_sems, 3 * a + j, (*chip, c)))
                landed = dst.at[_chip_id(*chip), rows]
                recvs.append(_remote(landed, landed, send_sems, recv_sems, 3 * a + j, (*chip, c)))
        local = [pltpu.make_async_copy(ins[b], outs[b].at[me], local_sems.at[b]) for b in range(nw)]
        return sends, recvs, local

    return _Plan(copies, 3 * (ns + nw), ins=wholes, inouts=slots,
                 outs=[_sds((N_CHIPS, *s.shape), s.dtype) for s in wholes])


def _gather_pass_plan(slots):
    def copies(ins, ios, outs, send_sems, recv_sems, local_sems):
        x, y, c, chips = _place()
        sib = (x, y, 1 - c)
        sends, recvs = [], []
        for a, buf in enumerate(ios):
            half = buf.shape[1] // 2
            for j, chip in enumerate(chips):
                mine = buf.at[_chip_id(*chip), pl.ds(c * half, half)]
                other = buf.at[_chip_id(*chip), pl.ds((1 - c) * half, half)]
                sends.append(_remote(mine, mine, send_sems, recv_sems, 3 * a + j, sib))
                recvs.append(_remote(other, other, send_sems, recv_sems, 3 * a + j, sib))
        return sends, recvs, []

    return _Plan(copies, 3 * len(slots), inouts=slots)


def _pair_plan(grads):
    def copies(ins, ios, outs, send_sems, recv_sems, local_sems):
        x, y, c, _ = _place()
        sib = (x, y, 1 - c)
        sends, recvs = [], []
        for a, g in enumerate(ins):
            half = g.shape[1] // 2
            sends.append(_remote(g.at[:, pl.ds((1 - c) * half, half), :], outs[a], send_sems, recv_sems, a, sib))
            recvs.append(_remote(outs[a], outs[a], send_sems, recv_sems, a, sib))
        return sends, recvs, []

    return _Plan(copies, len(grads), ins=grads,
                 outs=[_sds((g.shape[0], g.shape[1] // 2, g.shape[2]), g.dtype) for g in grads])


def _chip_plan(parts):
    def copies(ins, ios, outs, send_sems, recv_sems, local_sems):
        x, y, c, chips = _place()
        me = _chip_id(x, y)
        sends, recvs = [], []
        for a, part in enumerate(ins):
            for j, chip in enumerate(chips):
                sends.append(_remote(part.at[_chip_id(*chip)], outs[a].at[me], send_sems, recv_sems, 3 * a + j, (*chip, c)))
                landed = outs[a].at[_chip_id(*chip)]
                recvs.append(_remote(landed, landed, send_sems, recv_sems, 3 * a + j, (*chip, c)))
        return sends, recvs, []

    return _Plan(copies, 3 * len(parts), ins=parts, outs=[_sds(p.shape, p.dtype) for p in parts])


def _all_sum(pack, fulls, name):
    R, C = pack.shape
    n = len(fulls)

    def body(p_ref, *refs):
        o_ref, halves = refs[n], refs[n + 1:2 * n + 1]
        buf, send_sems, recv_sems, pair_send, pair_recv = refs[2 * n + 1:]
        x, y, c, _ = _place()
        sib = (x, y, 1 - c)
        pair = []
        for a, full in enumerate(halves):
            H = full.shape[0] // 2
            mine = full.at[pl.ds(c * H, H)]
            cp = _remote(mine, mine, pair_send, pair_recv, a, sib)
            cp.start()
            pair.append(cp)
        me = 4 * x + 2 * y + c
        buf[me] = p_ref[...]
        cps = []
        for k in range(1, N_DEV):
            to = (x ^ (k >> 2), y ^ ((k >> 1) & 1), c ^ (k & 1))
            cp = _remote(p_ref, buf.at[me], send_sems, recv_sems, k - 1, to)
            cp.start()
            cps.append(cp)
        for k in range(1, N_DEV):
            frm = (x ^ (k >> 2), y ^ ((k >> 1) & 1), c ^ (k & 1))
            slot = buf.at[4 * frm[0] + 2 * frm[1] + frm[2]]
            _remote(slot, slot, send_sems, recv_sems, k - 1, frm).wait_recv()
        acc = buf[0]
        for k in range(1, N_DEV):
            acc = acc + buf[k]
        o_ref[...] = acc
        for cp in cps:
            cp.wait_send()
        for a, (full, cp) in enumerate(zip(halves, pair)):
            H = full.shape[0] // 2
            other = full.at[pl.ds((1 - c) * H, H)]
            _remote(other, other, pair_send, pair_recv, a, sib).wait_recv()
            cp.wait_send()

    vm = pl.BlockSpec(memory_space=pltpu.VMEM)
    res = pl.pallas_call(
        body, name=name, in_specs=[vm] + [ANY] * n, out_specs=[vm] + [ANY] * n,
        out_shape=[_sds((R, C), F32)] + [_sds(f.shape, f.dtype) for f in fulls],
        input_output_aliases={1 + a: 1 + a for a in range(n)},
        scratch_shapes=[pltpu.VMEM((N_DEV, R, C), F32), pltpu.SemaphoreType.DMA((N_DEV - 1,)),
                        pltpu.SemaphoreType.DMA((N_DEV - 1,)), pltpu.SemaphoreType.DMA((n,)),
                        pltpu.SemaphoreType.DMA((n,))])(pack, *fulls)
    return res[0], list(res[1:])


def _local_step(xs, tgt, p, ex):
    h1 = _norm_fwd(xs, p["pre_mix_norm"], "pre_mix_norm")
    proj = ex.project(h1)
    biases = _relbias_fwd(p["rel_bias"], "rel_bias_fwd")
    fw = []
    for g in range(N_GROUPS):
        res, got = _attn_fwd(proj, biases[g], g, f"attn_fwd{g}", plans=ex.carry(f"attn_fwd{g}"))
        ex.done(f"attn_fwd{g}", got)
        fw.append(res)
    (yh, o_h, states), got = _hgrn_fwd(proj, p["hgrn_lb_raw"], p["hgrn_norm"], "hgrn_fwd", plans=ex.carry("hgrn_fwd"))
    ex.done("hgrn_fwd", got)
    W_a, W_h, W_out = ex.weight("w_branch_attn"), ex.weight("w_branch_hgrn"), ex.weight("w_out")
    W_up, conv_w = ex.weight("w_up"), ex.weight("conv_w")
    y, lse, za, zh, merged = _branch_fwd([t[0] for t in fw], [t[1] for t in fw], yh, proj, W_a, W_h, "branch_fwd")
    mo, x1, h2 = _mix_out(merged, W_out, xs, p["post_mix_norm"], p["pre_ffn_norm"], "mix_out")
    u, got = _mm_nn_blk(h2, W_up, "ffn_up", plans=ex.carry("ffn_up"))
    ex.done("ffn_up", got)
    a, got = _conv_gelu_fwd(u, conv_w, p["conv_b"], "conv_gelu_fwd", plans=ex.carry("conv_gelu_fwd"))
    ex.done("conv_gelu_fwd", got)
    W_down = ex.weight("w_down")
    dx2, dff, g_post_ffn, loss = _loss_head(a, W_down, x1, tgt, p["post_ffn_norm"], "ffn_down_loss")

    ex.grad("w_down", _mm_tn(a, dff, "g_w_down").reshape(N_CHIPS, D_FF // N_CHIPS, D_MODEL))
    (dcg, dcv, gwg, gwv, gbg, gbv), got = _conv_gelu_bwd(u, dff, W_down, conv_w, p["conv_b"], "conv_gelu_bwd",
                                                          plans=ex.carry("conv_gelu_bwd"))
    ex.done("conv_gelu_bwd", got)
    g_conv_w = jnp.concatenate([gwg, gwv], axis=1)
    g_conv_b = jnp.concatenate([gbg, gbv], axis=1)
    du, got = _conv_input_bwd(dcg, dcv, conv_w, "conv_input_bwd", plans=ex.carry("conv_input_bwd"))
    ex.done("conv_input_bwd", got)
    dh2 = _mm_nt_blk(du, W_up, "d_ffn_in")
    ex.grad("w_up", _mm_tn_blk(h2, du, N_CHIPS, "g_w_up"))
    dx1, g_pre_ffn = _prenorm_bwd(dh2, x1, p["pre_ffn_norm"], dx2, "pre_ffn_norm_bwd")
    dmo, dmerged, g_post_mix = _postnorm_bwd(dx1, mo, p["post_mix_norm"], W_out, "post_mix_norm_bwd")
    ex.grad("w_out", _mm_tn(merged, dmo, "g_w_out").reshape(N_CHIPS, D_MODEL // N_CHIPS, D_MODEL))
    (dza, dzh, dg0, dg1, dy, dyh), got = _branch_bwd(dmerged, za, zh, proj, W_a, W_h, "branch_bwd",
                                                     plans=ex.carry("branch_bwd"))
    ex.done("branch_bwd", got)
    ex.grad("w_branch_attn", _mm_tn_blk(y, dza, N_CHIPS, "g_w_branch_attn", together=True))
    ex.grad("w_branch_hgrn", _mm_tn_blk(yh, dzh, N_CHIPS, "g_w_branch_hgrn", together=True))
    dqkv, dbs = [], []
    for g in range(N_GROUPS):
        parts, db, got = _attn_bwd(proj, biases[g], lse, y, dy, g, f"attn_bwd{g}", plans=ex.carry(f"attn_bwd{g}"))
        ex.done(f"attn_bwd{g}", got)
        dqkv += parts
        dbs.append(db)
    g_rel_bias = _relbias_bwd(dbs, "rel_bias_bwd")
    dproj, g_lb_raw, g_hgrn_norm = _hgrn_bwd(proj, p["hgrn_lb_raw"], p["hgrn_norm"], o_h, states, dyh, dqkv,
                                             [dg0, dg1], "hgrn_bwd")
    for piece in W_IN_PIECES:
        g, got = _mm_tn_blk(h1, dproj, N_CHIPS, f"g_{piece}", x_cols=W_IN_ROWS[piece],
                            plans=ex.carry(f"g_{piece}"))
        ex.done(f"g_{piece}", got)
        ex.grad(piece, g)
    dh1, got = _mm_nt_blk(dproj, ex.weight("w_in"), "d_proj_in", plans=ex.carry("d_proj_in"))
    ex.done("d_proj_in", got)
    (grad_x, g_pre_mix), got = _prenorm_bwd(dh1, xs, p["pre_mix_norm"], dx1, "pre_mix_norm_bwd",
                                            plans=ex.carry("pre_mix_norm_bwd"))
    ex.done("pre_mix_norm_bwd", got)
    small = dict(pre_mix_norm=g_pre_mix, rel_bias=g_rel_bias, hgrn_lb_raw=g_lb_raw, hgrn_norm=g_hgrn_norm,
                 post_mix_norm=g_post_mix, pre_ffn_norm=g_pre_ffn, conv_w=g_conv_w, conv_b=g_conv_b,
                 post_ffn_norm=g_post_ffn)
    return loss, grad_x, small


SMALL = ("pre_mix_norm", "rel_bias", "hgrn_lb_raw", "hgrn_norm", "post_mix_norm", "pre_ffn_norm", "conv_w", "conv_b",
         "post_ffn_norm")
BIG = ("w_in", "w_up", "w_down", "w_out", "w_branch_attn", "w_branch_hgrn")
WEIGHTS = ("pre_mix_norm", "w_in", "rel_bias", "hgrn_lb_raw", "hgrn_norm", "w_branch_attn", "w_branch_hgrn", "w_out",
           "post_mix_norm", "pre_ffn_norm", "w_up", "conv_w", "conv_b", "w_down", "post_ffn_norm")
MIXER = ("w_out", "w_branch_attn", "w_branch_hgrn")

SCHEDULE = {
    "attn_fwd0": [("gather_ici_cw", MIXER)],
    "attn_fwd1": [("gather_pass", MIXER), ("gather_ici", ("w_up",))],
    "attn_fwd2": [("gather_pass", ("w_up",))],
    "ffn_up": [("gather_ici", ("w_down",))],
    "conv_gelu_fwd": [("gather_pass", ("w_down",))],
    "conv_gelu_bwd": [("pair", ("w_down",))],
    "conv_input_bwd": [("chip", ("w_down",))],
    "branch_bwd": [("pair", ("w_up",))],
    "attn_bwd0": [("chip", ("w_up",)), ("pair", MIXER)],
    "attn_bwd1": [("chip", MIXER)],
    "g_w_in_b": [("pair", ("w_in_a",))],
    "d_proj_in": [("chip", ("w_in_a",)), ("pair", ("w_in_b",))],
    "pre_mix_norm_bwd": [("chip", ("w_in_b",))],
}
W_IN_ROWS = dict(w_in_a=(0, 768), w_in_b=(3, 256))
W_IN_PIECES = tuple(W_IN_ROWS)
REDUCED = W_IN_PIECES + BIG[1:]


class _Exchange:
    def __init__(self, place, slots, conv_w_shard):
        self.place, self.slots, self.conv_w_shard = place, dict(slots), conv_w_shard
        self.conv_w = None
        self.g, self.from_sibling, self.pair_sums, self.arrived = {}, {}, {}, {}
        self.pending = []

    def weight(self, name):
        if name == "conv_w":
            return self.conv_w
        w = self.slots[name]
        return w.reshape(-1, D_MODEL) if name in ("w_out", "w_down") else w

    def project(self, h):
        proj, self.slots["w_in"] = _proj_gathered(h, self.slots["w_in"], self.place, "proj_in")
        return proj

    def grad(self, name, g):
        self.g[name] = g

    def carry(self, point):
        plans = []
        self.pending = SCHEDULE.get(point, [])
        for kind, names in self.pending:
            if kind in ("gather_ici", "gather_ici_cw"):
                wholes = [self.conv_w_shard] if kind == "gather_ici_cw" else []
                plans.append(_gather_ici_plan([self.slots[n] for n in names], wholes))
            elif kind == "gather_pass":
                plans.append(_gather_pass_plan([self.slots[n] for n in names]))
            elif kind == "pair":
                plans.append(_pair_plan([self.g[n] for n in names]))
            else:
                for n in names:
                    self.pair_sums[n] = _pair_sum(self.g[n], self.from_sibling[n], self.place[1:2], f"pair_sum_{n}")
                plans.append(_chip_plan([self.pair_sums[n] for n in names]))
        return plans

    def done(self, point, carried):
        for (kind, names), got in zip(self.pending, carried):
            if kind in ("gather_ici", "gather_ici_cw", "gather_pass"):
                self.slots.update(zip(names, got))
                if kind == "gather_ici_cw":
                    self.conv_w = got[len(names)].transpose(1, 0, 2).reshape(3, 2 * D_FF)
            elif kind == "pair":
                self.from_sibling.update(zip(names, got))
            else:
                self.arrived.update(zip(names, got))

    def reduced_halves(self):
        return [_chip_sum(self.arrived[n], self.pair_sums[n], self.place, f"chip_sum_{n}") for n in REDUCED]


def kernel(x, pre_mix_norm, w_in, rel_bias, hgrn_lb_raw, hgrn_norm, w_branch_attn, w_branch_hgrn, w_out, post_mix_norm, pre_ffn_norm, w_up, conv_w, conv_b, w_down, post_ffn_norm, loss_target, m_pre_mix_norm, m_w_in, m_rel_bias, m_hgrn_lb_raw, m_hgrn_norm, m_w_branch_attn, m_w_branch_hgrn, m_w_out, m_post_mix_norm, m_pre_ffn_norm, m_w_up, m_conv_w, m_conv_b, m_w_down, m_post_ffn_norm, v_pre_mix_norm, v_w_in, v_rel_bias, v_hgrn_lb_raw, v_hgrn_norm, v_w_branch_attn, v_w_branch_hgrn, v_w_out, v_post_mix_norm, v_pre_ffn_norm, v_w_up, v_conv_w, v_conv_b, v_w_down, v_post_ffn_norm):
    w = dict(pre_mix_norm=pre_mix_norm, w_in=w_in, rel_bias=rel_bias, hgrn_lb_raw=hgrn_lb_raw, hgrn_norm=hgrn_norm,
             w_branch_attn=w_branch_attn, w_branch_hgrn=w_branch_hgrn, w_out=w_out, post_mix_norm=post_mix_norm,
             pre_ffn_norm=pre_ffn_norm, w_up=w_up, conv_w=conv_w, conv_b=conv_b, w_down=w_down,
             post_ffn_norm=post_ffn_norm)
    m = dict(pre_mix_norm=m_pre_mix_norm, w_in=m_w_in, rel_bias=m_rel_bias, hgrn_lb_raw=m_hgrn_lb_raw,
             hgrn_norm=m_hgrn_norm, w_branch_attn=m_w_branch_attn, w_branch_hgrn=m_w_branch_hgrn, w_out=m_w_out,
             post_mix_norm=m_post_mix_norm, pre_ffn_norm=m_pre_ffn_norm, w_up=m_w_up, conv_w=m_conv_w,
             conv_b=m_conv_b, w_down=m_w_down, post_ffn_norm=m_post_ffn_norm)
    v = dict(pre_mix_norm=v_pre_mix_norm, w_in=v_w_in, rel_bias=v_rel_bias, hgrn_lb_raw=v_hgrn_lb_raw,
             hgrn_norm=v_hgrn_norm, w_branch_attn=v_w_branch_attn, w_branch_hgrn=v_w_branch_hgrn, w_out=v_w_out,
             post_mix_norm=v_post_mix_norm, pre_ffn_norm=v_pre_ffn_norm, w_up=v_w_up, conv_w=v_conv_w,
             conv_b=v_conv_b, w_down=v_w_down, post_ffn_norm=v_post_ffn_norm)
    shard2d = {n: (w[n][0] if w[n].ndim == 3 else w[n]) for n in WEIGHTS}
    chip = 2 * lax.axis_index("x") + lax.axis_index("y")
    core = lax.axis_index("c")

    place = jnp.stack([chip, core]).astype(jnp.int32)
    slots = {n: _cast_into_slot(shard2d[n], place, f"cast_{n}") for n in BIG}
    ex = _Exchange(place, slots, shard2d["conv_w"])
    loss, grad_x, small = _local_step(x[0], loss_target[0], {n: w[n] for n in SMALL if n != "conv_w"}, ex)

    flat = [small[n].reshape(-1) for n in SMALL] + [loss.reshape(-1)]
    sizes = [t.shape[0] for t in flat]
    summed, wholes = _all_sum(jnp.concatenate(flat).reshape(-1, LANES), ex.reduced_halves(), "sum_small")
    summed = summed.reshape(-1)
    offs = [sum(sizes[:i]) for i in range(len(sizes))]
    grads = {}
    for n, o, sz in zip(SMALL, offs, sizes):
        grads[n] = summed[o:o + sz].reshape(small[n].shape)
    loss_total = summed[offs[-1]]
    cw = 2 * D_FF // N_CHIPS
    grads["conv_w"] = lax.dynamic_slice(grads["conv_w"], (0, chip * cw), (3, cw))

    big = dict(zip(REDUCED, wholes))
    big["w_in"] = jnp.concatenate([big.pop(n) for n in W_IN_PIECES], axis=0)
    grads.update(big)

    out_g, out_d, out_m, out_v = [], [], [], []
    for n in WEIGHTS:
        d2, m2, v2 = _adamw(shard2d[n], grads[n], m[n].reshape(shard2d[n].shape), v[n].reshape(shard2d[n].shape),
                            f"adamw_{n}")
        shape = w[n].shape
        out_g.append(grads[n].reshape(shape))
        out_d.append(d2.reshape(shape))
        out_m.append(m2.reshape(shape))
        out_v.append(v2.reshape(shape))
    return (loss_total, grad_x[None], *out_g, *out_d, *out_m, *out_v)
```

```python
import functools
import math

import jax
import jax.numpy as jnp
from jax import lax
from jax.experimental import pallas as pl
from jax.experimental.pallas import tpu as pltpu

F32 = jnp.float32
BF16 = jnp.bfloat16
MESH = pl.DeviceIdType.MESH

D_MODEL = 1024
N_GROUPS = 3
DILATIONS = (1, 4, 16)
HEADS = 8
HEAD_DIM = 64
GROUP_W = HEADS * HEAD_DIM
QKV_W = N_GROUPS * 3 * GROUP_W
BLK = 128
NEG_INF = -1e30
NUM_BUCKETS = 32
MAX_EXACT = 16
MAX_DISTANCE = 2048
HG_HEADS = 4
HG_DK = 128
HG_W = HG_HEADS * HG_DK
HG_CHUNK = 32
HG_TILE = 256
IN_W = QKV_W + 4 * HG_W + 2 * D_MODEL
D_FF = 2816
EPS = 1e-6
N_CHIPS = 4
N_DEV = 8
LANES = 128

ADAM_LR, ADAM_B1, ADAM_B2, ADAM_EPS, ADAM_WD, ADAM_STEP = 0.001, 0.9, 0.999, 1e-08, 0.01, 10

VMEM_LIMIT = 56 * 1024 * 1024


def _cp(n_axes):
    return pltpu.CompilerParams(dimension_semantics=("arbitrary",) * n_axes, vmem_limit_bytes=VMEM_LIMIT)


def _sds(shape, dtype):
    return jax.ShapeDtypeStruct(tuple(shape), dtype)


def _sigmoid(v):
    return 1.0 / (1.0 + jnp.exp(-v))


def _bf(v):
    return v.astype(BF16)


def _dot(a, b, dims):
    return lax.dot_general(a, b, (dims, ((), ())), preferred_element_type=F32)


NN = ((1,), (0,))
NT = ((1,), (1,))
TN = ((0,), (0,))

ANY = pl.BlockSpec(memory_space=pl.ANY)


class _Plan:
    def __init__(self, copies, n_sems, ins=(), inouts=(), outs=()):
        self.copies, self.n_sems = copies, n_sems
        self.ins, self.inouts, self.outs = list(ins), list(inouts), list(outs)


def _call(body, plans=None, *, name, grid, in_specs, out_specs, out_shape, args, scratch_shapes=()):
    plans = list(plans or ())
    in_specs, out_specs, out_shape = list(in_specs), list(out_specs), list(out_shape)
    scratch_shapes = list(scratch_shapes)
    n_in, n_out, n_scr = len(in_specs), len(out_specs), len(scratch_shapes)
    x_in, x_out, aliases, spans = [], [], {}, []
    for p in plans:
        i0, o0 = len(x_in), len(x_out)
        x_in += p.ins
        for a in p.inouts:
            aliases[n_in + len(x_in)] = n_out + len(x_out)
            x_in.append(a)
            x_out.append(_sds(a.shape, a.dtype))
        x_out += p.outs
        spans.append((i0, len(p.ins), o0, len(p.inouts), len(p.outs)))
    sems = [pltpu.SemaphoreType.DMA((p.n_sems,)) for p in plans for _ in range(3)]

    def wrapped(*refs):
        xi = refs[n_in:n_in + len(x_in)]
        base = n_in + len(x_in)
        xo = refs[base + n_out:base + n_out + len(x_out)]
        sbase = base + n_out + len(x_out)
        xs = refs[sbase + n_scr:]
        ids = [pl.program_id(k) for k in range(len(grid))]
        first = functools.reduce(jnp.logical_and, [i == 0 for i in ids])
        last = functools.reduce(jnp.logical_and, [i == g - 1 for i, g in zip(ids, grid)])

        def descriptors(k):
            i0, ni, o0, nio, no = spans[k]
            return plans[k].copies(xi[i0:i0 + ni], xo[o0:o0 + nio], xo[o0 + nio:o0 + nio + no], *xs[3 * k:3 * k + 3])

        @pl.when(first)
        def _():
            for k in range(len(plans)):
                sends, _, local = descriptors(k)
                for cp in (*sends, *local):
                    cp.start()

        body(*refs[:n_in], *refs[base:base + n_out], *refs[sbase:sbase + n_scr])

        @pl.when(last)
        def _():
            for k in range(len(plans)):
                sends, recvs, local = descriptors(k)
                for cp in recvs:
                    cp.wait_recv()
                for cp in sends:
                    cp.wait_send()
                for cp in local:
                    cp.wait()

    res = pl.pallas_call(
        wrapped if plans else body, name=name, grid=grid, in_specs=in_specs + [ANY] * len(x_in),
        out_specs=out_specs + [ANY] * len(x_out), out_shape=out_shape + x_out, input_output_aliases=aliases,
        scratch_shapes=scratch_shapes + sems, compiler_params=_cp(len(grid)))(*args, *x_in)
    res = list(res)
    carried = [res[n_out + o0:n_out + o0 + nio + no] for (_, _, o0, nio, no) in spans]
    return res[:n_out], carried


def _mm_nn_blk(a, wg, name, tm=512, plans=None):
    M, K = a.shape
    nb, _, Nb = wg.shape

    def body(a_ref, w_ref, o_ref):
        o_ref[...] = _dot(_bf(a_ref[...]), w_ref[...], NN)

    (out,), carried = _call(
        body, plans, name=name, grid=(nb, M // tm),
        in_specs=[pl.BlockSpec((tm, K), lambda j, i: (i, 0)), pl.BlockSpec((None, K, Nb), lambda j, i: (j, 0, 0))],
        out_specs=[pl.BlockSpec((tm, Nb), lambda j, i: (i, j))],
        out_shape=[_sds((M, nb * Nb), F32)], args=(a, wg))
    return out if plans is None else (out, carried)


def _mm_nt_blk(dy, wg, name, tm=1024, plans=None):
    M = dy.shape[0]
    nb, K, Nb = wg.shape

    def body(dy_ref, w_ref, o_ref):
        j = pl.program_id(1)
        r = _dot(_bf(dy_ref[...]), w_ref[...], NT)

        @pl.when(j == 0)
        def _():
            o_ref[...] = r

        @pl.when(j > 0)
        def _():
            o_ref[...] += r

    (out,), carried = _call(
        body, plans, name=name, grid=(M // tm, nb),
        in_specs=[pl.BlockSpec((tm, Nb), lambda i, j: (i, j)), pl.BlockSpec((None, K, Nb), lambda i, j: (j, 0, 0))],
        out_specs=[pl.BlockSpec((tm, K), lambda i, j: (i, 0))],
        out_shape=[_sds((M, K), F32)], args=(dy, wg))
    return out if plans is None else (out, carried)


def _mm_tn_blk(x, dy, nb, name, tk=2048, x_cols=None, plans=None, together=False):
    T, Mx = x.shape
    xk, Mx = (0, Mx) if x_cols is None else x_cols
    Nb = dy.shape[1] // nb
    nj = nb if together else 1

    def body(x_ref, dy_ref, o_ref):
        t = pl.program_id(1)
        r = _dot(_bf(x_ref[...]), _bf(dy_ref[...]), TN)
        for j in range(nj):
            rj = r[:, j * Nb:(j + 1) * Nb]

            @pl.when(t == 0)
            def _():
                o_ref[j] = rj

            @pl.when(t > 0)
            def _():
                o_ref[j] += rj

    (out,), carried = _call(
        body, plans, name=name, grid=(nb // nj, T // tk),
        in_specs=[pl.BlockSpec((tk, Mx), lambda j, t: (t, xk)), pl.BlockSpec((tk, nj * Nb), lambda j, t: (t, j))],
        out_specs=[pl.BlockSpec((nj, Mx, Nb), lambda j, t: (j, 0, 0))],
        out_shape=[_sds((nb, Mx, Nb), F32)], args=(x, dy))
    return out if plans is None else (out, carried)


def _mm_tn(x, dy, name, tk=1024):
    T, Mx = x.shape
    N = dy.shape[1]

    def body(x_ref, dy_ref, o_ref):
        t = pl.program_id(0)
        r = _dot(_bf(x_ref[...]), _bf(dy_ref[...]), TN)

        @pl.when(t == 0)
        def _():
            o_ref[...] = r

        @pl.when(t > 0)
        def _():
            o_ref[...] += r

    return pl.pallas_call(
        body, name=name, grid=(T // tk,),
        in_specs=[pl.BlockSpec((tk, Mx), lambda t: (t, 0)), pl.BlockSpec((tk, N), lambda t: (t, 0))],
        out_specs=pl.BlockSpec((Mx, N), lambda t: (0, 0)),
        out_shape=_sds((Mx, N), F32), compiler_params=_cp(1))(x, dy)


def _tile(arr, bw, col=lambda c: 0):
    return ("tile", arr, bw, col)


def _full(arr):
    return ("full", arr)


def _out_tile(width, dtype, bw, col=lambda c: 0):
    return ("tile", width, dtype, bw, col)


def _out_acc(rows, width, bw, col=lambda c: 0):
    return ("acc", rows, width, bw, col)


def _rows_call(name, body, n_rows, tm, ncol, ins, outs, plans=None):
    in_specs, args = [], []
    for e in ins:
        if e[0] == "tile":
            _, arr, bw, col = e
            in_specs.append(pl.BlockSpec((tm, bw), functools.partial(lambda c, i, col: (i, col(c)), col=col)))
        else:
            arr = e[1]
            in_specs.append(pl.BlockSpec(arr.shape, functools.partial(lambda c, i, nd: (0,) * nd, nd=arr.ndim)))
        args.append(arr)
    out_specs, out_shape = [], []
    for e in outs:
        if e[0] == "tile":
            _, width, dtype, bw, col = e
            out_specs.append(pl.BlockSpec((tm, bw), functools.partial(lambda c, i, col: (i, col(c)), col=col)))
            out_shape.append(_sds((n_rows, width), dtype))
        else:
            _, rows, width, bw, col = e
            out_specs.append(pl.BlockSpec((rows, bw), functools.partial(lambda c, i, col: (0, col(c)), col=col)))
            out_shape.append(_sds((rows, width), F32))
    out, carried = _call(body, plans, name=name, grid=(ncol, n_rows // tm), in_specs=in_specs, out_specs=out_specs,
                         out_shape=out_shape, args=args)
    return out if plans is None else (out, carried)


def _acc(ref, val):
    i = pl.program_id(1)

    @pl.when(i == 0)
    def _():
        ref[...] = val

    @pl.when(i > 0)
    def _():
        ref[...] += val


def _rinv(z):
    return lax.rsqrt(jnp.mean(z * z, axis=-1, keepdims=True) + EPS)


def _norm_bwd(dy, zhat, r, w):
    dyw = dy * w
    return r * (dyw - zhat * jnp.mean(dyw * zhat, axis=-1, keepdims=True))


def _norm_fwd(x, w, name):
    def body(x_ref, w_ref, h_ref):
        xv = x_ref[...]
        h_ref[...] = _bf(xv * _rinv(xv) * w_ref[...])

    return _rows_call(name, body, x.shape[0], 512, 1, [_tile(x, D_MODEL), _full(w)],
                      [_out_tile(D_MODEL, BF16, D_MODEL)])[0]


def _prenorm_bwd(dh, xin, w, dres, name, plans=None):
    def body(dh_ref, x_ref, w_ref, dres_ref, dx_ref, dw_ref):
        xv = x_ref[...]
        r = _rinv(xv)
        xhat = xv * r
        dhv = dh_ref[...]
        dx_ref[...] = dres_ref[...] + _norm_bwd(dhv, xhat, r, w_ref[...])
        _acc(dw_ref, jnp.sum(dhv * xhat, axis=0, keepdims=True))

    return _rows_call(name, body, xin.shape[0], 512, 1,
                      [_tile(dh, D_MODEL), _tile(xin, D_MODEL), _full(w), _tile(dres, D_MODEL)],
                      [_out_tile(D_MODEL, F32, D_MODEL), _out_acc(1, D_MODEL, D_MODEL)], plans)


def _postnorm_bwd(dout, z, w, w_mat, name, plans=None):
    def body(do_ref, z_ref, w_ref, wm_ref, dz_ref, dm_ref, dw_ref):
        zv = z_ref[...]
        r = _rinv(zv)
        zhat = zv * r
        dov = do_ref[...]
        dz = _bf(_norm_bwd(dov, zhat, r, w_ref[...]))
        dz_ref[...] = dz
        dm_ref[...] = _dot(dz, wm_ref[...], NT)
        _acc(dw_ref, jnp.sum(dov * zhat, axis=0, keepdims=True))

    return _rows_call(name, body, z.shape[0], 512, 1,
                      [_tile(dout, D_MODEL), _tile(z, D_MODEL), _full(w), _full(w_mat)],
                      [_out_tile(D_MODEL, BF16, D_MODEL), _out_tile(D_MODEL, F32, D_MODEL),
                       _out_acc(1, D_MODEL, D_MODEL)], plans)


def _t5_bucket(dist):
    n = jnp.maximum(dist, 0)
    nf = jnp.maximum(n, 1).astype(F32)
    large = MAX_EXACT + (jnp.log(nf / MAX_EXACT) / math.log(MAX_DISTANCE / MAX_EXACT)
                         * (NUM_BUCKETS - MAX_EXACT)).astype(jnp.int32)
    large = jnp.minimum(large, NUM_BUCKETS - 1)
    return jnp.where(n < MAX_EXACT, n, large)


def _band_rel():
    return jnp.arange(BLK)[:, None] + BLK - jnp.arange(2 * BLK)[None, :]


def _band_valid():
    rel = _band_rel()
    window = (rel >= 0) & (rel <= BLK)
    first = window & (jnp.arange(2 * BLK)[None, :] >= BLK)
    return jnp.stack([first, window]).astype(F32).reshape(2, 1, BAND)


RES_UNROLL = 8
PAIR = LANES // HEAD_DIM


def _pair_lanes():
    first = lax.broadcasted_iota(jnp.int32, (1, LANES), 1) < HEAD_DIM
    return first, jnp.logical_not(first)


def _heads_per_step(d):
    return HEADS if d == 1 else LANES // HEAD_DIM


def _sub_rows(r, d):
    return pl.ds(r, BLK, stride=d) if d > 1 else pl.ds(0, BLK)


def _for_residues(d, fn):
    if d <= RES_UNROLL:
        for r in range(d):
            fn(r)
    else:
        def group(i, carry):
            for k in range(RES_UNROLL):
                fn(i * RES_UNROLL + k)
            return carry

        lax.fori_loop(0, d // RES_UNROLL, group, 0)


def _attn_specs(d, g, qblock):
    cw = _heads_per_step(d) * HEAD_DIM

    def col(part, hp):
        return (g * 3 + part) * (GROUP_W // cw) + hp

    def cur(part):
        return pl.BlockSpec((d * BLK, cw), lambda hp, n: (qblock(n), col(part, hp)))

    def prev(part):
        return pl.BlockSpec((d * BLK, cw), lambda hp, n: (jnp.maximum(qblock(n) - 1, 0), col(part, hp)))

    return cur, prev


def _attn_fwd(proj, bias, g, name, plans=None):
    S = proj.shape[0]
    d = DILATIONS[g]
    NB = S // (d * BLK)
    hps = _heads_per_step(d)

    def body(q_ref, kp_ref, kc_ref, vp_ref, vc_ref, b_ref, o_ref, lse_ref):
        hp = pl.program_id(0)
        later = jnp.minimum(pl.program_id(1), 1)

        def residue(r):
            rows = _sub_rows(r, d)
            q2 = q_ref[rows, :]
            k2 = jnp.concatenate([kp_ref[rows, :], kc_ref[rows, :]], axis=0)
            v2 = jnp.concatenate([vp_ref[rows, :], vc_ref[rows, :]], axis=0)
            outs, lses = [], []
            for pp in range(hps // PAIR):
                ps = slice(pp * LANES, (pp + 1) * LANES)
                qp, kp, vp = _bf(q2[:, ps]), _bf(k2[:, ps]), _bf(v2[:, ps])
                o_h, lse_h = [], []
                for hh, own in enumerate(_pair_lanes()):
                    s = _dot(qp, jnp.where(own, kp, 0), NT) * (HEAD_DIM ** -0.5) + b_ref[later, hp * hps + pp * PAIR + hh]
                    m = jnp.max(s, axis=-1, keepdims=True)
                    p = jnp.exp(s - m)
                    l = jnp.sum(p, axis=-1, keepdims=True)
                    o_h.append(_dot(_bf(p), vp, NN) / l)
                    lse_h.append(m + jnp.log(l))
                first = _pair_lanes()[0]
                outs.append(jnp.where(first, o_h[0], o_h[1]))
                lses.append(jnp.where(first, lse_h[0], lse_h[1]))
            o_ref[rows, :] = outs[0] if len(outs) == 1 else jnp.concatenate(outs, axis=1)
            lse_ref[rows, :] = lses[0] if len(lses) == 1 else jnp.concatenate(lses, axis=1)

        _for_residues(d, residue)

    cur, prev = _attn_specs(d, g, lambda n: n)
    out = pl.BlockSpec((d * BLK, hps * HEAD_DIM), lambda hp, n: (n, hp))
    res, carried = _call(
        body, plans, name=name, grid=(HEADS // hps, NB),
        in_specs=[cur(0), prev(1), cur(1), prev(2), cur(2),
                  pl.BlockSpec((2, HEADS, BLK, 2 * BLK), lambda hp, n: (0, 0, 0, 0))],
        out_specs=[out, out], out_shape=[_sds((S, GROUP_W), F32)] * 2,
        args=(proj, proj, proj, proj, proj, bias))
    return res if plans is None else (res, carried)


def _attn_bwd(proj, bias, lse, y, dy, g, name, plans=None):
    S = proj.shape[0]
    d = DILATIONS[g]
    NB = S // (d * BLK)
    hps = _heads_per_step(d)

    def body(q_ref, kp_ref, kc_ref, vp_ref, vc_ref, b_ref, l_ref, y_ref, dy_ref,
             dq_ref, dk_ref, dv_ref, db_ref, ck_ref, cv_ref):
        hp, n = pl.program_id(0), pl.program_id(1)

        @pl.when((hp == 0) & (n == 0))
        def _():
            db_ref[...] = jnp.zeros_like(db_ref)

        @pl.when(n == 0)
        def _():
            ck_ref[...] = jnp.zeros_like(ck_ref)
            cv_ref[...] = jnp.zeros_like(cv_ref)

        @pl.when(n < NB)
        def _():
            later = jnp.minimum(n, 1)

            def residue(r):
                rows = _sub_rows(r, d)
                q2 = q_ref[rows, :]
                k2 = jnp.concatenate([kp_ref[rows, :], kc_ref[rows, :]], axis=0)
                v2 = jnp.concatenate([vp_ref[rows, :], vc_ref[rows, :]], axis=0)
                l2, y2, dy2 = l_ref[rows, :], y_ref[rows, :], dy_ref[rows, :]
                dqs, dks, dvs = [], [], []
                for pp in range(hps // PAIR):
                    ps = slice(pp * LANES, (pp + 1) * LANES)
                    qp, kp, vp = _bf(q2[:, ps]), _bf(k2[:, ps]), _bf(v2[:, ps])
                    dyp, yp = dy2[:, ps], y2[:, ps]
                    dq_h, dk_h, dv_h = [], [], []
                    for hh, own in enumerate(_pair_lanes()):
                        head = hp * hps + pp * PAIR + hh
                        s = _dot(qp, jnp.where(own, kp, 0), NT) * (HEAD_DIM ** -0.5) + b_ref[later, head]
                        p = jnp.exp(s - l2[:, pp * LANES + hh * HEAD_DIM:pp * LANES + hh * HEAD_DIM + 1])
                        dyh = jnp.where(own, dyp, 0.0)
                        delta = jnp.sum(dyh * yp, axis=-1, keepdims=True)
                        ds = p * (_dot(_bf(dyh), vp, NT) - delta)
                        db_ref[head] += ds
                        dsb = _bf(ds * (HEAD_DIM ** -0.5))
                        dq_h.append(_dot(dsb, kp, NN))
                        dk_h.append(_dot(dsb, qp, TN))
                        dv_h.append(_dot(_bf(p), _bf(dyp), TN))
                    first = _pair_lanes()[0]
                    dqs.append(jnp.where(first, dq_h[0], dq_h[1]))
                    dks.append(jnp.where(first, dk_h[0], dk_h[1]))
                    dvs.append(jnp.where(first, dv_h[0], dv_h[1]))
                dkb = dks[0] if len(dks) == 1 else jnp.concatenate(dks, axis=1)
                dvb = dvs[0] if len(dvs) == 1 else jnp.concatenate(dvs, axis=1)
                dq_ref[rows, :] = dqs[0] if len(dqs) == 1 else jnp.concatenate(dqs, axis=1)
                dk_ref[rows, :] = ck_ref[rows, :] + dkb[:BLK]
                dv_ref[rows, :] = cv_ref[rows, :] + dvb[:BLK]
                ck_ref[rows, :] = dkb[BLK:]
                cv_ref[rows, :] = dvb[BLK:]

            _for_residues(d, residue)

        @pl.when(n == NB)
        def _():
            dk_ref[...] = ck_ref[...]
            dv_ref[...] = cv_ref[...]

    def qn(n):
        return jnp.minimum(n, NB - 1)

    cur, prev = _attn_specs(d, g, qn)
    cw = hps * HEAD_DIM
    row = pl.BlockSpec((d * BLK, cw), lambda hp, n: (qn(n), hp))
    done = pl.BlockSpec((d * BLK, cw), lambda hp, n: (jnp.maximum(n - 1, 0), hp))
    (dq, dk, dv, db), carried = _call(
        body, plans, name=name, grid=(HEADS // hps, NB + 1),
        in_specs=[cur(0), prev(1), cur(1), prev(2), cur(2),
                  pl.BlockSpec((2, HEADS, BLK, 2 * BLK), lambda hp, n: (0, 0, 0, 0)), row, row, row],
        out_specs=[row, done, done, pl.BlockSpec((HEADS, BLK, 2 * BLK), lambda hp, n: (0, 0, 0))],
        out_shape=[_sds((S, GROUP_W), F32)] * 3 + [_sds((HEADS, BLK, 2 * BLK), F32)],
        scratch_shapes=[pltpu.VMEM((d * BLK, cw), F32)] * 2,
        args=(proj, proj, proj, proj, proj, bias, lse, y, dy))
    return ([dq, dk, dv], db) if plans is None else ([dq, dk, dv], db, carried)


BAND = BLK * 2 * BLK


def _bucket_onehot():
    buckets = jnp.stack([_t5_bucket(_band_rel() * d) for d in DILATIONS]).reshape(N_GROUPS, 1, BAND)
    return (buckets == jnp.arange(NUM_BUCKETS).reshape(1, NUM_BUCKETS, 1)).astype(F32)


def _relbias_fwd(rel_bias, name):
    table = rel_bias.reshape(NUM_BUCKETS, N_GROUPS, HEADS).transpose(1, 0, 2)

    def body(t_ref, oh_ref, valid_ref, o_ref):
        bias = lax.dot_general(t_ref[...], oh_ref[...], (TN, ((), ())), preferred_element_type=F32,
                               precision=lax.Precision.HIGHEST)
        for k in range(2):
            o_ref[k] = jnp.where(valid_ref[k] > 0.5, bias, NEG_INF)

    out = pl.pallas_call(
        body, name=name, grid=(N_GROUPS,),
        in_specs=[pl.BlockSpec((None, NUM_BUCKETS, HEADS), lambda g: (g, 0, 0)),
                  pl.BlockSpec((None, NUM_BUCKETS, BAND), lambda g: (g, 0, 0)),
                  pl.BlockSpec((2, 1, BAND), lambda g: (0, 0, 0))],
        out_specs=pl.BlockSpec((None, 2, HEADS, BAND), lambda g: (g, 0, 0, 0)),
        out_shape=_sds((N_GROUPS, 2, HEADS, BAND), F32), compiler_params=_cp(1))(table, _bucket_onehot(), _band_valid())
    return out.reshape(N_GROUPS, 2, HEADS, BLK, 2 * BLK)


def _relbias_bwd(dbs, name):
    band = BAND
    onehot = _bucket_onehot()
    dbf = jnp.stack([db.reshape(HEADS, band) for db in dbs])

    def body(oh_ref, db_ref, o_ref):
        o_ref[...] = lax.dot_general(oh_ref[...], db_ref[...], (NT, ((), ())), preferred_element_type=F32,
                                     precision=lax.Precision.HIGHEST)

    out = pl.pallas_call(
        body, name=name, grid=(N_GROUPS,),
        in_specs=[pl.BlockSpec((None, NUM_BUCKETS, band), lambda g: (g, 0, 0)),
                  pl.BlockSpec((None, HEADS, band), lambda g: (g, 0, 0))],
        out_specs=pl.BlockSpec((None, NUM_BUCKETS, HEADS), lambda g: (g, 0, 0)),
        out_shape=_sds((N_GROUPS, NUM_BUCKETS, HEADS), F32), compiler_params=_cp(1))(onehot, dbf)
    return out.transpose(1, 0, 2).reshape(NUM_BUCKETS, N_GROUPS * HEADS)


def _chunk_pos(shape):
    return lax.broadcasted_iota(jnp.int32, shape, 0) % HG_CHUNK


def _chunk_cumsum(v):
    pos = _chunk_pos(v.shape)
    s = 1
    while s < HG_CHUNK:
        v = v + jnp.where(pos >= s, pltpu.roll(v, s, 0), 0.0)
        s *= 2
    return v


def _chunk_rev_cumsum(v):
    pos = _chunk_pos(v.shape)
    n = v.shape[0]
    s = 1
    while s < HG_CHUNK:
        v = v + jnp.where(pos < HG_CHUNK - s, pltpu.roll(v, n - s, 0), 0.0)
        s *= 2
    return v


def _lower_bound(raw):
    a0, a1 = raw[0:1], raw[1:2]
    m = jnp.maximum(a0, a1)
    e0, e1 = jnp.exp(a0 - m), jnp.exp(a1 - m)
    return e0 / (e0 + e1)


def _hg_gates(qr, fr, lb):
    sf = _sigmoid(fr)
    f = lb + (1.0 - lb) * sf
    sq = _sigmoid(qr)
    return qr * sq, sq, f, sf


HG_COL0 = QKV_W // HG_W


def _hgrn_fwd(proj, lb_raw, nw, name, plans=None):
    S = proj.shape[0]
    ncs = HG_TILE // HG_CHUNK
    tril = jnp.tril(jnp.ones((HG_CHUNK, HG_CHUNK), dtype=bool))

    def body(q_ref, f_ref, i_ref, og_ref, lb_ref, nw_ref, y_ref, o_ref, st_ref, state):
        @pl.when(pl.program_id(0) == 0)
        def _():
            state[...] = jnp.zeros_like(state)

        lb = _lower_bound(lb_ref[...])
        q, _, f, _ = _hg_gates(q_ref[...], f_ref[...], lb)
        k = 1.0 - f
        G = _chunk_cumsum(jnp.log(f))
        row = lax.broadcasted_iota(jnp.int32, (HG_CHUNK, HG_CHUNK), 0)
        col = lax.broadcasted_iota(jnp.int32, (HG_CHUNK, HG_CHUNK), 1)
        heads = [slice(h * HG_DK, (h + 1) * HG_DK) for h in range(HG_HEADS)]
        sts = [state[h] for h in range(HG_HEADS)]
        for c in range(ncs):
            cs = slice(c * HG_CHUNK, (c + 1) * HG_CHUNK)
            for h, hs in enumerate(heads):
                Gc = G[cs, hs]
                gl = Gc[HG_CHUNK - 1:HG_CHUNK]
                qt = _bf(q[cs, hs] * jnp.exp(Gc))
                kt = _bf(k[cs, hs] * jnp.exp(-Gc))
                kd = _bf(k[cs, hs] * jnp.exp(gl - Gc))
                v = _bf(i_ref[cs, hs])
                A = jnp.where(row >= col, _dot(qt, kt, NT), 0.0)
                o_ref[cs, hs] = _dot(_bf(A), v, NN) + _dot(qt, _bf(sts[h]), NT)
                st_ref[c, h] = sts[h]
                sts[h] = sts[h] * jnp.exp(gl) + _dot(v, kd, TN)
        for h, hs in enumerate(heads):
            state[h] = sts[h]
            oh = o_ref[:, hs]
            og = og_ref[:, hs]
            y_ref[:, hs] = oh * _rinv(oh) * nw_ref[...] * (og * _sigmoid(og))

    def colspec(j):
        return pl.BlockSpec((HG_TILE, HG_W), lambda i: (i, HG_COL0 + j))

    res, carried = _call(
        body, plans, name=name, grid=(S // HG_TILE,),
        in_specs=[colspec(0), colspec(1), colspec(2), colspec(3),
                  pl.BlockSpec((2, HG_W), lambda i: (0, 0)), pl.BlockSpec((1, HG_DK), lambda i: (0, 0))],
        out_specs=[pl.BlockSpec((HG_TILE, HG_W), lambda i: (i, 0))] * 2
        + [pl.BlockSpec((ncs, HG_HEADS, HG_DK, HG_DK), lambda i: (i, 0, 0, 0))],
        out_shape=[_sds((S, HG_W), F32)] * 2 + [_sds((S // HG_CHUNK, HG_HEADS, HG_DK, HG_DK), F32)],
        scratch_shapes=[pltpu.VMEM((HG_HEADS, HG_DK, HG_DK), F32)],
        args=(proj, proj, proj, proj, lb_raw, nw))
    return res if plans is None else (res, carried)


def _hgrn_bwd(proj, lb_raw, nw, o, states, dy, d_attn, d_gates, name):
    S = proj.shape[0]
    ncs = HG_TILE // HG_CHUNK
    nt = S // HG_TILE
    n_a, n_g = len(d_attn), len(d_gates)
    own = [slice(QKV_W + j * HG_W, QKV_W + (j + 1) * HG_W) for j in range(4)]

    def body(q_ref, f_ref, i_ref, og_ref, lb_ref, nw_ref, o_ref, st_ref, dy_ref, *rest):
        attn_refs, gate_refs = rest[:n_a], rest[n_a:n_a + n_g]
        dp_ref, dlb_ref, dnw_ref, dstate, do_s, dG_s, dgl_s, dk_s, dlb_s = rest[n_a + n_g:]
        dq_ref, df_ref, di_ref, dog_ref = (dp_ref.at[:, cols] for cols in own)
        step = pl.program_id(0)
        for k, a_ref in enumerate(attn_refs):
            dp_ref[:, k * GROUP_W:(k + 1) * GROUP_W] = _bf(a_ref[...])
        for k, g_ref in enumerate(gate_refs):
            dp_ref[:, QKV_W + 4 * HG_W + k * D_MODEL:QKV_W + 4 * HG_W + (k + 1) * D_MODEL] = g_ref[...]

        @pl.when(step == 0)
        def _():
            dstate[...] = jnp.zeros_like(dstate)
            dlb_s[...] = jnp.zeros_like(dlb_s)
            dnw_ref[...] = jnp.zeros_like(dnw_ref)

        lb = _lower_bound(lb_ref[...])
        qr = q_ref[...]
        q, sq, f, sf = _hg_gates(qr, f_ref[...], lb)
        k = 1.0 - f
        G = _chunk_cumsum(jnp.log(f))
        nwv = nw_ref[...]
        row = lax.broadcasted_iota(jnp.int32, (HG_CHUNK, HG_CHUNK), 0)
        col = lax.broadcasted_iota(jnp.int32, (HG_CHUNK, HG_CHUNK), 1)
        for h in range(HG_HEADS):
            hs = slice(h * HG_DK, (h + 1) * HG_DK)
            oh = o_ref[:, hs]
            r = _rinv(oh)
            ohat = oh * r
            og = og_ref[:, hs]
            sg = _sigmoid(og)
            dyh = dy_ref[:, hs]
            don = dyh * (og * sg)
            dog_ref[:, hs] = _bf(dyh * (ohat * nwv) * (sg * (1.0 + og * (1.0 - sg))))
            dnw_ref[...] += jnp.sum(don * ohat, axis=0, keepdims=True)
            do_s[:, hs] = _norm_bwd(don, ohat, r, nwv)
        dsts = [dstate[h] for h in range(HG_HEADS)]
        for c in reversed(range(ncs)):
            cs = slice(c * HG_CHUNK, (c + 1) * HG_CHUNK)
            for h in range(HG_HEADS):
                hs = slice(h * HG_DK, (h + 1) * HG_DK)
                dst = dsts[h]
                Gc = G[cs, hs]
                gl = Gc[HG_CHUNK - 1:HG_CHUNK]
                eG, enG, edG, egl = jnp.exp(Gc), jnp.exp(-Gc), jnp.exp(gl - Gc), jnp.exp(gl)
                qt, kt, kd = q[cs, hs] * eG, k[cs, hs] * enG, k[cs, hs] * edG
                qtb, ktb, kdb = _bf(qt), _bf(kt), _bf(kd)
                v = _bf(i_ref[cs, hs])
                do = _bf(do_s[cs, hs])
                st = st_ref[c, h]
                dstb = _bf(dst)
                A = jnp.where(row >= col, _dot(qtb, ktb, NT), 0.0)
                dA = _bf(jnp.where(row >= col, _dot(do, v, NT), 0.0))
                di_ref[cs, hs] = _bf(_dot(_bf(A), do, TN) + _dot(kdb, dstb, NT))
                dqt = _dot(dA, ktb, NN) + _dot(do, _bf(st), NN)
                dkt = _dot(dA, qtb, TN)
                dkd = _dot(v, dstb, NN)
                dgl = egl * jnp.sum(st * dst, axis=0, keepdims=True) + jnp.sum(dkd * kd, axis=0, keepdims=True)
                dsts[h] = dst * egl + _dot(do, qtb, TN)
                dq_ref[cs, hs] = _bf(dqt * eG * (sq[cs, hs] * (1.0 + qr[cs, hs] * (1.0 - sq[cs, hs]))))
                dk_s[cs, hs] = dkt * enG + dkd * edG
                dG_s[cs, hs] = dqt * qt - dkt * kt - dkd * kd
                dgl_s[cs, hs] = jnp.broadcast_to(dgl, (HG_CHUNK, HG_DK))
        for h in range(HG_HEADS):
            dstate[h] = dsts[h]
        dg = _chunk_rev_cumsum(dG_s[...]) + dgl_s[...]
        dfv = dg / f - dk_s[...]
        df_ref[...] = _bf(dfv * (1.0 - lb) * sf * (1.0 - sf))
        dlb_s[...] += jnp.sum(dfv * (1.0 - sf), axis=0, keepdims=True)

        @pl.when(step == nt - 1)
        def _():
            t = dlb_s[...] * lb * (1.0 - lb)
            dlb_ref[...] = jnp.concatenate([t, -t], axis=0)

    def colspec(j):
        return pl.BlockSpec((HG_TILE, HG_W), lambda i: (nt - 1 - i, HG_COL0 + j))

    def rows(width):
        return pl.BlockSpec((HG_TILE, width), lambda i: (nt - 1 - i, 0))

    tile = rows(HG_W)
    return pl.pallas_call(
        body, name=name, grid=(nt,),
        in_specs=[colspec(0), colspec(1), colspec(2), colspec(3),
                  pl.BlockSpec((2, HG_W), lambda i: (0, 0)), pl.BlockSpec((1, HG_DK), lambda i: (0, 0)),
                  tile, pl.BlockSpec((ncs, HG_HEADS, HG_DK, HG_DK), lambda i: (nt - 1 - i, 0, 0, 0)), tile]
        + [rows(GROUP_W)] * n_a + [rows(D_MODEL)] * n_g,
        out_specs=[rows(IN_W), pl.BlockSpec((2, HG_W), lambda i: (0, 0)), pl.BlockSpec((1, HG_DK), lambda i: (0, 0))],
        out_shape=[_sds((S, IN_W), BF16), _sds((2, HG_W), F32), _sds((1, HG_DK), F32)],
        scratch_shapes=[pltpu.VMEM((HG_HEADS, HG_DK, HG_DK), F32)] + [pltpu.VMEM((HG_TILE, HG_W), F32)] * 4
        + [pltpu.VMEM((1, HG_W), F32)],
        compiler_params=_cp(1))(proj, proj, proj, proj, lb_raw, nw, o, states, dy, *d_attn, *d_gates)


GATE_COL0 = (QKV_W + 4 * HG_W) // GROUP_W
HALF_D = D_MODEL // 2


def _gate_tiles(proj):
    return [_tile(proj, HALF_D, functools.partial(lambda c, k: GATE_COL0 + k, k=k)) for k in range(4)]


def _gates(g_refs):
    s0 = _sigmoid(jnp.concatenate([g_refs[0][...], g_refs[1][...]], axis=1))
    s1 = _sigmoid(jnp.concatenate([g_refs[2][...], g_refs[3][...]], axis=1))
    return s0, s1


def _branch_fwd(os_, lses, yh, proj, w_a, w_h, name):
    nb = w_a.shape[0]

    def body(o0, o1, o2, l0, l1, l2, yh_ref, g0a, g0b, g1a, g1b, wa_ref, wh_ref,
             y_ref, lse_ref, za_ref, zh_ref, m_ref):
        a, b, c = l0[...], l1[...], l2[...]
        m = jnp.maximum(jnp.maximum(a, b), c)
        ea, eb, ec = jnp.exp(a - m), jnp.exp(b - m), jnp.exp(c - m)
        den = ea + eb + ec
        y = (ea * o0[...] + eb * o1[...] + ec * o2[...]) / den
        y_ref[...] = y
        lse_ref[...] = m + jnp.log(den)
        yb, yhb = _bf(y), _bf(yh_ref[...])
        za = jnp.concatenate([_dot(yb, wa_ref[j], NN) for j in range(nb)], axis=1)
        zh = jnp.concatenate([_dot(yhb, wh_ref[j], NN) for j in range(nb)], axis=1)
        s0, s1 = _gates((g0a, g0b, g1a, g1b))
        za_ref[...] = za
        zh_ref[...] = zh
        m_ref[...] = _bf(s0 * za + s1 * zh)

    return _rows_call(name, body, yh.shape[0], 512, 1,
                      [*[_tile(t, GROUP_W) for t in (*os_, *lses)], _tile(yh, HG_W), *_gate_tiles(proj),
                       _full(w_a), _full(w_h)],
                      [_out_tile(GROUP_W, F32, GROUP_W)] * 2 + [_out_tile(D_MODEL, F32, D_MODEL)] * 2
                      + [_out_tile(D_MODEL, BF16, D_MODEL)])


def _branch_bwd(dm, za, zh, proj, w_a, w_h, name, plans=None):
    nb, _, Nb = w_a.shape

    def body(dm_ref, za_ref, zh_ref, g0a, g0b, g1a, g1b, wa_ref, wh_ref,
             dza_ref, dzh_ref, dg0_ref, dg1_ref, dy_ref, dyh_ref):
        dmv = dm_ref[...]
        s0, s1 = _gates((g0a, g0b, g1a, g1b))
        dza, dzh = _bf(dmv * s0), _bf(dmv * s1)
        dza_ref[...] = dza
        dzh_ref[...] = dzh
        dg0_ref[...] = _bf(dmv * za_ref[...] * s0 * (1.0 - s0))
        dg1_ref[...] = _bf(dmv * zh_ref[...] * s1 * (1.0 - s1))
        dy_ref[...] = sum(_dot(dza[:, j * Nb:(j + 1) * Nb], wa_ref[j], NT) for j in range(nb))
        dyh_ref[...] = sum(_dot(dzh[:, j * Nb:(j + 1) * Nb], wh_ref[j], NT) for j in range(nb))

    return _rows_call(name, body, za.shape[0], 512, 1,
                      [_tile(dm, D_MODEL), _tile(za, D_MODEL), _tile(zh, D_MODEL), *_gate_tiles(proj),
                       _full(w_a), _full(w_h)],
                      [_out_tile(D_MODEL, BF16, D_MODEL)] * 4 + [_out_tile(GROUP_W, F32, GROUP_W),
                                                                 _out_tile(HG_W, F32, HG_W)], plans)


def _mix_out(merged, w_out, x, w_post, w_pre, name):
    def body(m_ref, wo_ref, x_ref, wp_ref, wf_ref, mo_ref, x1_ref, h2_ref):
        z = _dot(m_ref[...], wo_ref[...], NN)
        mo_ref[...] = z
        x1 = x_ref[...] + z * _rinv(z) * wp_ref[...]
        x1_ref[...] = x1
        h2_ref[...] = _bf(x1 * _rinv(x1) * wf_ref[...])

    return _rows_call(name, body, x.shape[0], 512, 1,
                      [_tile(merged, D_MODEL), _full(w_out), _tile(x, D_MODEL), _full(w_post), _full(w_pre)],
                      [_out_tile(D_MODEL, F32, D_MODEL), _out_tile(D_MODEL, F32, D_MODEL),
                       _out_tile(D_MODEL, BF16, D_MODEL)])


def _loss_head(a, w_down, x1, tgt, w, name):
    def body(a_ref, wd_ref, x1_ref, t_ref, w_ref, dx_ref, df_ref, dw_ref, loss_ref):
        z = _dot(a_ref[...], wd_ref[...], NN)
        r = _rinv(z)
        zhat = z * r
        wv = w_ref[...]
        e = x1_ref[...] + zhat * wv - t_ref[...]
        dx = e * (1.0 / D_MODEL)
        dx_ref[...] = dx
        df_ref[...] = _bf(_norm_bwd(dx, zhat, r, wv))
        _acc(dw_ref, jnp.sum(dx * zhat, axis=0, keepdims=True))
        part = 0.5 * jnp.sum(jnp.sum(e * e, axis=1, keepdims=True), axis=0, keepdims=True) * (1.0 / D_MODEL)
        _acc(loss_ref, jnp.broadcast_to(part, (1, LANES)))

    return _rows_call(name, body, x1.shape[0], 512, 1,
                      [_tile(a, D_FF), _full(w_down), _tile(x1, D_MODEL), _tile(tgt, D_MODEL), _full(w)],
                      [_out_tile(D_MODEL, F32, D_MODEL), _out_tile(D_MODEL, BF16, D_MODEL),
                       _out_acc(1, D_MODEL, D_MODEL), _out_acc(1, LANES, LANES)])


CONV_CB = D_FF // 2
CONV_TM = 512
HALO = 8
SQRT_HALF = 0.7071067811865476
INV_SQRT_2PI = 0.3989422804014327


CONV_RS = 32


def _lane_tiles():
    return [slice(k * LANES, (k + 1) * LANES) for k in range(CONV_CB // LANES)]


def _strip_start(i):
    return pl.multiple_of(i * CONV_RS, CONV_RS)


def _strip_taps(u_ref, halo_ref, r0, cs, first_strip, first_tile):
    if first_strip:
        before = jnp.where(first_tile, 0.0, halo_ref[:, cs])
        blk = jnp.concatenate([before, u_ref[0:CONV_RS, cs]], axis=0)
    else:
        blk = u_ref[pl.ds(pl.multiple_of(r0 - HALO, HALO), CONV_RS + HALO), cs]
    return pltpu.roll(blk, 2, 0)[HALO:], pltpu.roll(blk, 1, 0)[HALO:], blk[HALO:]


def _conv(taps, w_ref, b_ref, cs):
    return b_ref[:, cs] + w_ref[0:1, cs] * taps[0] + w_ref[1:2, cs] * taps[1] + w_ref[2:3, cs] * taps[2]


def _conv_specs(tm):
    nh = tm // HALO
    nc = D_FF // CONV_CB

    def tile(off):
        return pl.BlockSpec((tm, CONV_CB), lambda c, i: (i, off + c))

    def halo(off):
        return pl.BlockSpec((HALO, CONV_CB), lambda c, i: (jnp.maximum(i * nh - 1, 0), off + c))

    def small(rows, off):
        return pl.BlockSpec((rows, CONV_CB), lambda c, i: (0, off + c))

    return nc, tile, halo, small


def _conv_gelu_fwd(u, cw, cb, name, plans=None):
    S = u.shape[0]
    tm = CONV_TM
    nc, tile, halo, small = _conv_specs(tm)

    def body(ug, hg, uv, hv, wg, wv, bg, bv, a_ref):
        first_tile = pl.program_id(1) == 0

        def strip(r0, first_strip):
            for cs in _lane_tiles():
                cg = _conv(_strip_taps(ug, hg, r0, cs, first_strip, first_tile), wg, bg, cs)
                cv = _conv(_strip_taps(uv, hv, r0, cs, first_strip, first_tile), wv, bv, cs)
                a_ref[pl.ds(r0, CONV_RS), cs] = _bf(0.5 * cg * (1.0 + lax.erf(cg * SQRT_HALF)) * cv)

        strip(0, True)
        lax.fori_loop(1, tm // CONV_RS, lambda k, c: (strip(_strip_start(k), False), c)[1], 0)

    (a,), carried = _call(
        body, plans, name=name, grid=(nc, S // tm),
        in_specs=[tile(0), halo(0), tile(nc), halo(nc), small(3, 0), small(3, nc), small(1, 0), small(1, nc)],
        out_specs=[tile(0)], out_shape=[_sds((S, D_FF), BF16)], args=(u, u, u, u, cw, cw, cb, cb))
    return a if plans is None else (a, carried)


def _conv_gelu_bwd(u, dff, w_down, cw, cb, name, plans=None):
    S = u.shape[0]
    tm = CONV_TM
    nt = S // tm
    nc, tile, halo, small = _conv_specs(tm)

    def body(ug, hg, uv, hv, wg, wv, bg, bv, dff_ref, wd_ref, dcg_ref, dcv_ref, dwg_ref, dwv_ref, dbg_ref, dbv_ref,
             acc, da_ref):
        i = pl.program_id(1)
        first_tile = i == 0
        da_ref[...] = _dot(dff_ref[...], wd_ref[...], NT)

        @pl.when(first_tile)
        def _():
            acc[...] = jnp.zeros_like(acc)

        def strip(r0, first_strip):
            rows = pl.ds(r0, CONV_RS)
            for cs in _lane_tiles():
                tg = _strip_taps(ug, hg, r0, cs, first_strip, first_tile)
                tv = _strip_taps(uv, hv, r0, cs, first_strip, first_tile)
                cg = _conv(tg, wg, bg, cs)
                cv = _conv(tv, wv, bv, cs)
                phi = 0.5 * (1.0 + lax.erf(cg * SQRT_HALF))
                dav = da_ref[rows, cs]
                dcg = dav * cv * (phi + cg * jnp.exp(-0.5 * cg * cg) * INV_SQRT_2PI)
                dcv = dav * (cg * phi)
                dcg_ref[rows, cs] = dcg
                dcv_ref[rows, cs] = dcv
                for half, (dc, taps) in enumerate(((dcg, tg), (dcv, tv))):
                    for j in range(3):
                        acc[4 * half + j, :, cs] += dc * taps[j]
                    acc[4 * half + 3, :, cs] += dc

        strip(0, True)
        lax.fori_loop(1, tm // CONV_RS, lambda k, c: (strip(_strip_start(k), False), c)[1], 0)

        @pl.when(i == nt - 1)
        def _():
            for half, (dw_ref, db_ref) in enumerate(((dwg_ref, dbg_ref), (dwv_ref, dbv_ref))):
                for j in range(3):
                    dw_ref[j:j + 1, :] = jnp.sum(acc[4 * half + j], axis=0, keepdims=True)
                db_ref[...] = jnp.sum(acc[4 * half + 3], axis=0, keepdims=True)

    res, carried = _call(
        body, plans, name=name, grid=(nc, nt),
        in_specs=[tile(0), halo(0), tile(nc), halo(nc), small(3, 0), small(3, nc), small(1, 0), small(1, nc),
                  pl.BlockSpec((tm, D_MODEL), lambda c, i: (i, 0)), pl.BlockSpec((CONV_CB, D_MODEL), lambda c, i: (c, 0))],
        out_specs=[tile(0), tile(0), small(3, 0), small(3, 0), small(1, 0), small(1, 0)],
        out_shape=[_sds((S, D_FF), F32)] * 2 + [_sds((3, D_FF), F32)] * 2 + [_sds((1, D_FF), F32)] * 2,
        scratch_shapes=[pltpu.VMEM((8, CONV_RS, CONV_CB), F32), pltpu.VMEM((tm, CONV_CB), F32)],
        args=(u, u, u, u, cw, cw, cb, cb, dff, w_down))
    return res if plans is None else (res, carried)


def _conv_input_bwd(dcg, dcv, cw, name, plans=None):
    S = dcg.shape[0]
    tm = CONV_TM // 2
    nh = tm // HALO
    nt = S // tm
    n = CONV_RS + HALO
    tile = pl.BlockSpec((tm, D_FF), lambda i: (i, 0))
    nxt = pl.BlockSpec((HALO, D_FF), lambda i: (jnp.minimum((i + 1) * nh, S // HALO - 1), 0))

    def body(g_ref, ng_ref, v_ref, nv_ref, w_ref, du_ref):
        last_tile = pl.program_id(0) == nt - 1

        def strip(r0, last_strip):
            for half, (dc_ref, n_ref) in enumerate(((g_ref, ng_ref), (v_ref, nv_ref))):
                for k in range(D_FF // LANES):
                    cs = slice(k * LANES, (k + 1) * LANES)
                    ws = slice(half * D_FF + k * LANES, half * D_FF + (k + 1) * LANES)
                    if last_strip:
                        after = jnp.where(last_tile, 0.0, n_ref[:, cs])
                        blk = jnp.concatenate([dc_ref[tm - CONV_RS:tm, cs], after], axis=0)
                    else:
                        blk = dc_ref[pl.ds(r0, n), cs]
                    d1 = pltpu.roll(blk, n - 1, 0)[:CONV_RS]
                    d2 = pltpu.roll(blk, n - 2, 0)[:CONV_RS]
                    du_ref[pl.ds(r0, CONV_RS), ws] = _bf(w_ref[2:3, ws] * blk[:CONV_RS] + w_ref[1:2, ws] * d1
                                                         + w_ref[0:1, ws] * d2)

        lax.fori_loop(0, tm // CONV_RS - 1, lambda k, c: (strip(_strip_start(k), False), c)[1], 0)
        strip(tm - CONV_RS, True)

    (du,), carried = _call(
        body, plans, name=name, grid=(nt,),
        in_specs=[tile, nxt, tile, nxt, pl.BlockSpec((3, 2 * D_FF), lambda i: (0, 0))],
        out_specs=[pl.BlockSpec((tm, 2 * D_FF), lambda i: (i, 0))], out_shape=[_sds((S, 2 * D_FF), BF16)],
        args=(dcg, dcg, dcv, dcv, cw))
    return du if plans is None else (du, carried)


def _row_tile(n, cap):
    best = n
    for t in range(16, cap + 1, 16):
        if n % t == 0:
            best = t
    return best if best <= cap else n


def _rows_for_bytes(nbytes, cols):
    return max(16, nbytes // (4 * cols) // 16 * 16)


def _adamw(w, g, m, v, name):
    R, C = w.shape
    tr = _row_tile(R, _rows_for_bytes(2 << 20, C))

    def body(w_ref, g_ref, m_ref, v_ref, d_ref, nm_ref, nv_ref):
        gv = g_ref[...]
        nm = ADAM_B1 * m_ref[...] + (1.0 - ADAM_B1) * gv
        nv = ADAM_B2 * v_ref[...] + (1.0 - ADAM_B2) * (gv * gv)
        m_hat = nm / (1.0 - ADAM_B1 ** ADAM_STEP)
        v_hat = nv / (1.0 - ADAM_B2 ** ADAM_STEP)
        d_ref[...] = -ADAM_LR * (m_hat / (jnp.sqrt(v_hat) + ADAM_EPS) + ADAM_WD * w_ref[...])
        nm_ref[...] = nm
        nv_ref[...] = nv

    spec = pl.BlockSpec((tr, C), lambda i: (i, 0))
    return pl.pallas_call(body, name=name, grid=(R // tr,), in_specs=[spec] * 4, out_specs=[spec] * 3,
                          out_shape=[_sds((R, C), F32)] * 3, compiler_params=_cp(1))(w, g, m, v)


def _pair_sum(gfull, rcv, c_idx, name):
    nb, R, C = gfull.shape
    half = R // 2
    tr = _row_tile(half, _rows_for_bytes(2 << 20, C))
    nt = half // tr

    def body(c_ref, g_ref, r_ref, o_ref):
        o_ref[...] = _bf(g_ref[...] + r_ref[...])

    return pl.pallas_call(
        body, name=name,
        grid_spec=pltpu.PrefetchScalarGridSpec(
            num_scalar_prefetch=1, grid=(nb, nt),
            in_specs=[pl.BlockSpec((None, tr, C), lambda j, i, c_ref: (j, c_ref[0] * nt + i, 0)),
                      pl.BlockSpec((None, tr, C), lambda j, i, c_ref: (j, i, 0))],
            out_specs=pl.BlockSpec((None, tr, C), lambda j, i, c_ref: (j, i, 0))),
        out_shape=_sds((nb, half, C), BF16), compiler_params=_cp(2))(c_idx, gfull, rcv)


def _chip_sum(arrived, own, place, name):
    nb, H, C = arrived.shape
    tr = _row_tile(H, _rows_for_bytes(2 << 20, C))
    nt = H // tr

    def body(pl_ref, *refs):
        o_ref = refs[nb + 1]
        me = pl_ref[0]
        acc = None
        for k in range(nb):
            term = jnp.where(me == k, refs[nb][...], refs[k][...]).astype(F32)
            acc = term if acc is None else acc + term
        o_ref[...] = acc

    def other(k):
        return pl.BlockSpec((None, tr, C), lambda i, p: (jnp.where(p[0] == k, (k + 1) % nb, k), i, 0))

    return pl.pallas_call(
        body, name=name,
        grid_spec=pltpu.PrefetchScalarGridSpec(
            num_scalar_prefetch=1, grid=(nt,),
            in_specs=[other(k) for k in range(nb)] + [pl.BlockSpec((None, tr, C), lambda i, p: (p[0], i, 0))],
            out_specs=pl.BlockSpec((tr, C), lambda i, p: (p[1] * nt + i, 0))),
        out_shape=_sds((2 * H, C), F32), compiler_params=_cp(1))(place, *([arrived] * nb), own)


def _cast_into_slot(shard, place, name):
    R, C = shard.shape
    tr = _row_tile(R, 256)

    def body(pl_ref, s_ref, o_ref):
        o_ref[...] = _bf(s_ref[...])

    return pl.pallas_call(
        body, name=name,
        grid_spec=pltpu.PrefetchScalarGridSpec(
            num_scalar_prefetch=1, grid=(R // tr,),
            in_specs=[pl.BlockSpec((tr, C), lambda i, p: (i, 0))],
            out_specs=pl.BlockSpec((None, tr, C), lambda i, p: (p[0], i, 0))),
        out_shape=_sds((N_CHIPS, R, C), BF16), compiler_params=_cp(1))(place, shard)


def _place():
    x, y, c = lax.axis_index("x"), lax.axis_index("y"), lax.axis_index("c")
    chips = [(1 - x, y), (x, 1 - y), (1 - x, 1 - y)]
    return x, y, c, chips


def _chip_id(px, py):
    return 2 * px + py


def _remote(src, dst, send_sems, recv_sems, k, to):
    return pltpu.make_async_remote_copy(src_ref=src, dst_ref=dst, send_sem=send_sems.at[k], recv_sem=recv_sems.at[k],
                                        device_id=to, device_id_type=MESH)


def _proj_gathered(h, slot, place, name, tm=512):
    M, K = h.shape
    nb, _, Nb = slot.shape
    half = K // 2
    nt = M // tm
    cx, cy = place[0] // 2, place[0] % 2
    order = jnp.stack([place[0], _chip_id(1 - cx, cy), _chip_id(cx, 1 - cy), _chip_id(1 - cx, 1 - cy)]).astype(jnp.int32)

    def body(order_ref, h_ref, slot_in, o_ref, slot_ref, w_buf, ici_send, ici_recv, pass_send, pass_recv, load_sem):
        b, i = pl.program_id(0), pl.program_id(1)
        x, y, c, chips = _place()
        me = _chip_id(x, y)
        sib = (x, y, 1 - c)
        mine, other = pl.ds(c * half, half), pl.ds((1 - c) * half, half)

        def sent(k):
            blk = slot_ref.at[me, mine]
            return _remote(blk, blk, ici_send, ici_recv, k, (*chips[k], c))

        def landed(k):
            blk = slot_ref.at[_chip_id(*chips[k]), mine]
            return _remote(blk, blk, ici_send, ici_recv, k, (*chips[k], c))

        def passed(k, rows):
            blk = slot_ref.at[_chip_id(*chips[k]), rows]
            return _remote(blk, blk, pass_send, pass_recv, k, sib)

        @pl.when((b == 0) & (i == 0))
        def _():
            for k in range(len(chips)):
                sent(k).start()

        for k in range(len(chips)):
            @pl.when((b == k + 1) & (i == 0))
            def _(k=k):
                landed(k).wait_recv()
                passed(k, mine).start()
                passed(k, other).wait_recv()

        @pl.when(i == 0)
        def _():
            load = pltpu.make_async_copy(slot_ref.at[order_ref[b]], w_buf, load_sem.at[0])
            load.start()
            load.wait()

        o_ref[...] = _dot(h_ref[...], w_buf[...], NN)

        @pl.when((b == nb - 1) & (i == nt - 1))
        def _():
            for k in range(len(chips)):
                sent(k).wait_send()
                passed(k, mine).wait_send()

    n_peers = N_CHIPS - 1
    return pl.pallas_call(
        body, name=name,
        grid_spec=pltpu.PrefetchScalarGridSpec(
            num_scalar_prefetch=1, grid=(nb, nt),
            in_specs=[pl.BlockSpec((tm, K), lambda b, i, o: (i, 0)), ANY],
            out_specs=[pl.BlockSpec((tm, Nb), lambda b, i, o: (i, o[b])), ANY],
            scratch_shapes=[pltpu.VMEM((K, Nb), BF16)] + [pltpu.SemaphoreType.DMA((n_peers,))] * 4
            + [pltpu.SemaphoreType.DMA((1,))]),
        out_shape=[_sds((M, nb * Nb), F32), _sds(slot.shape, slot.dtype)],
        input_output_aliases={2: 1}, compiler_params=_cp(2))(order, h, slot)


def _gather_ici_plan(slots, wholes):
    ns, nw = len(slots), len(wholes)

    def copies(ins, ios, outs, send_sems, recv_sems, local_sems):
        x, y, c, chips = _place()
        me = _chip_id(x, y)
        sends, recvs = [], []
        for a in range(ns + nw):
            dst = ios[a] if a < ns else outs[a - ns]
            R = dst.shape[1]
            rows = pl.ds(c * (R // 2), R // 2) if a < ns else pl.ds(0, R)
            src = dst.at[me, rows] if a < ns else ins[a - ns]
            for j, chip in enumerate(chips):
                sends.append(_remote(src, dst.at[me, rows], send_sems, recv_sems, 3 * a + j, (*chip, c)))
                landed = dst.at[_chip_id(*chip), rows]
                recvs.append(_remote(landed, landed, send_sems, recv_sems, 3 * a + j, (*chip, c)))
        local = [pltpu.make_async_copy(ins[b], outs[b].at[me], local_sems.at[b]) for b in range(nw)]
        return sends, recvs, local

    return _Plan(copies, 3 * (ns + nw), ins=wholes, inouts=slots,
                 outs=[_sds((N_CHIPS, *s.shape), s.dtype) for s in wholes])


def _gather_pass_plan(slots):
    def copies(ins, ios, outs, send_sems, recv_sems, local_sems):
        x, y, c, chips = _place()
        sib = (x, y, 1 - c)
        sends, recvs = [], []
        for a, buf in enumerate(ios):
            half = buf.shape[1] // 2
            for j, chip in enumerate(chips):
                mine = buf.at[_chip_id(*chip), pl.ds(c * half, half)]
                other = buf.at[_chip_id(*chip), pl.ds((1 - c) * half, half)]
                sends.append(_remote(mine, mine, send_sems, recv_sems, 3 * a + j, sib))
                recvs.append(_remote(other, other, send_sems, recv_sems, 3 * a + j, sib))
        return sends, recvs, []

    return _Plan(copies, 3 * len(slots), inouts=slots)


def _pair_plan(grads):
    def copies(ins, ios, outs, send_sems, recv_sems, local_sems):
        x, y, c, _ = _place()
        sib = (x, y, 1 - c)
        sends, recvs = [], []
        for a, g in enumerate(ins):
            half = g.shape[1] // 2
            sends.append(_remote(g.at[:, pl.ds((1 - c) * half, half), :], outs[a], send_sems, recv_sems, a, sib))
            recvs.append(_remote(outs[a], outs[a], send_sems, recv_sems, a, sib))
        return sends, recvs, []

    return _Plan(copies, len(grads), ins=grads,
                 outs=[_sds((g.shape[0], g.shape[1] // 2, g.shape[2]), g.dtype) for g in grads])


def _chip_plan(parts):
    def copies(ins, ios, outs, send_sems, recv_sems, local_sems):
        x, y, c, chips = _place()
        me = _chip_id(x, y)
        sends, recvs = [], []
        for a, part in enumerate(ins):
            for j, chip in enumerate(chips):
                sends.append(_remote(part.at[_chip_id(*chip)], outs[a].at[me], send_sems, recv_sems, 3 * a + j, (*chip, c)))
                landed = outs[a].at[_chip_id(*chip)]
                recvs.append(_remote(landed, landed, send_sems, recv_sems, 3 * a + j, (*chip, c)))
        return sends, recvs, []

    return _Plan(copies, 3 * len(parts), ins=parts, outs=[_sds(p.shape, p.dtype) for p in parts])


def _all_sum(pack, fulls, name):
    R, C = pack.shape
    n = len(fulls)

    def body(p_ref, *refs):
        o_ref, halves = refs[n], refs[n + 1:2 * n + 1]
        buf, send_sems, recv_sems, pair_send, pair_recv = refs[2 * n + 1:]
        x, y, c, _ = _place()
        sib = (x, y, 1 - c)
        pair = []
        for a, full in enumerate(halves):
            H = full.shape[0] // 2
            mine = full.at[pl.ds(c * H, H)]
            cp = _remote(mine, mine, pair_send, pair_recv, a, sib)
            cp.start()
            pair.append(cp)
        me = 4 * x + 2 * y + c
        buf[me] = p_ref[...]
        cps = []
        for k in range(1, N_DEV):
            to = (x ^ (k >> 2), y ^ ((k >> 1) & 1), c ^ (k & 1))
            cp = _remote(p_ref, buf.at[me], send_sems, recv_sems, k - 1, to)
            cp.start()
            cps.append(cp)
        for k in range(1, N_DEV):
            frm = (x ^ (k >> 2), y ^ ((k >> 1) & 1), c ^ (k & 1))
            slot = buf.at[4 * frm[0] + 2 * frm[1] + frm[2]]
            _remote(slot, slot, send_sems, recv_sems, k - 1, frm).wait_recv()
        acc = buf[0]
        for k in range(1, N_DEV):
            acc = acc + buf[k]
        o_ref[...] = acc
        for cp in cps:
            cp.wait_send()
        for a, (full, cp) in enumerate(zip(halves, pair)):
            H = full.shape[0] // 2
            other = full.at[pl.ds((1 - c) * H, H)]
            _remote(other, other, pair_send, pair_recv, a, sib).wait_recv()
            cp.wait_send()

    vm = pl.BlockSpec(memory_space=pltpu.VMEM)
    res = pl.pallas_call(
        body, name=name, in_specs=[vm] + [ANY] * n, out_specs=[vm] + [ANY] * n,
        out_shape=[_sds((R, C), F32)] + [_sds(f.shape, f.dtype) for f in fulls],
        input_output_aliases={1 + a: 1 + a for a in range(n)},
        scratch_shapes=[pltpu.VMEM((N_DEV, R, C), F32), pltpu.SemaphoreType.DMA((N_DEV - 1,)),
                        pltpu.SemaphoreType.DMA((N_DEV - 1,)), pltpu.SemaphoreType.DMA((n,)),
                        pltpu.SemaphoreType.DMA((n,))])(pack, *fulls)
    return res[0], list(res[1:])


def _local_step(xs, tgt, p, ex):
    h1 = _norm_fwd(xs, p["pre_mix_norm"], "pre_mix_norm")
    proj = ex.project(h1)
    biases = _relbias_fwd(p["rel_bias"], "rel_bias_fwd")
    fw = []
    for g in range(N_GROUPS):
        res, got = _attn_fwd(proj, biases[g], g, f"attn_fwd{g}", plans=ex.carry(f"attn_fwd{g}"))
        ex.done(f"attn_fwd{g}", got)
        fw.append(res)
    (yh, o_h, states), got = _hgrn_fwd(proj, p["hgrn_lb_raw"], p["hgrn_norm"], "hgrn_fwd", plans=ex.carry("hgrn_fwd"))
    ex.done("hgrn_fwd", got)
    W_a, W_h, W_out = ex.weight("w_branch_attn"), ex.weight("w_branch_hgrn"), ex.weight("w_out")
    W_up, conv_w = ex.weight("w_up"), ex.weight("conv_w")
    y, lse, za, zh, merged = _branch_fwd([t[0] for t in fw], [t[1] for t in fw], yh, proj, W_a, W_h, "branch_fwd")
    mo, x1, h2 = _mix_out(merged, W_out, xs, p["post_mix_norm"], p["pre_ffn_norm"], "mix_out")
    u, got = _mm_nn_blk(h2, W_up, "ffn_up", tm=1024, plans=ex.carry("ffn_up"))
    ex.done("ffn_up", got)
    a, got = _conv_gelu_fwd(u, conv_w, p["conv_b"], "conv_gelu_fwd", plans=ex.carry("conv_gelu_fwd"))
    ex.done("conv_gelu_fwd", got)
    W_down = ex.weight("w_down")
    dx2, dff, g_post_ffn, loss = _loss_head(a, W_down, x1, tgt, p["post_ffn_norm"], "ffn_down_loss")

    ex.grad("w_down", _mm_tn(a, dff, "g_w_down").reshape(N_CHIPS, D_FF // N_CHIPS, D_MODEL))
    (dcg, dcv, gwg, gwv, gbg, gbv), got = _conv_gelu_bwd(u, dff, W_down, conv_w, p["conv_b"], "conv_gelu_bwd",
                                                          plans=ex.carry("conv_gelu_bwd"))
    ex.done("conv_gelu_bwd", got)
    g_conv_w = jnp.concatenate([gwg, gwv], axis=1)
    g_conv_b = jnp.concatenate([gbg, gbv], axis=1)
    du, got = _conv_input_bwd(dcg, dcv, conv_w, "conv_input_bwd", plans=ex.carry("conv_input_bwd"))
    ex.done("conv_input_bwd", got)
    dh2 = _mm_nt_blk(du, W_up, "d_ffn_in")
    ex.grad("w_up", _mm_tn_blk(h2, du, N_CHIPS, "g_w_up"))
    dx1, g_pre_ffn = _prenorm_bwd(dh2, x1, p["pre_ffn_norm"], dx2, "pre_ffn_norm_bwd")
    (dmo, dmerged, g_post_mix), got = _postnorm_bwd(dx1, mo, p["post_mix_norm"], W_out, "post_mix_norm_bwd",
                                                    plans=ex.carry("post_mix_norm_bwd"))
    ex.done("post_mix_norm_bwd", got)
    ex.grad("w_out", _mm_tn(merged, dmo, "g_w_out").reshape(N_CHIPS, D_MODEL // N_CHIPS, D_MODEL))
    (dza, dzh, dg0, dg1, dy, dyh), got = _branch_bwd(dmerged, za, zh, proj, W_a, W_h, "branch_bwd",
                                                     plans=ex.carry("branch_bwd"))
    ex.done("branch_bwd", got)
    ex.grad("w_branch_attn", _mm_tn_blk(y, dza, N_CHIPS, "g_w_branch_attn", together=True))
    ex.grad("w_branch_hgrn", _mm_tn_blk(yh, dzh, N_CHIPS, "g_w_branch_hgrn", together=True))
    dqkv, dbs = [], []
    for g in range(N_GROUPS):
        parts, db, got = _attn_bwd(proj, biases[g], lse, y, dy, g, f"attn_bwd{g}", plans=ex.carry(f"attn_bwd{g}"))
        ex.done(f"attn_bwd{g}", got)
        dqkv += parts
        dbs.append(db)
    g_rel_bias = _relbias_bwd(dbs, "rel_bias_bwd")
    dproj, g_lb_raw, g_hgrn_norm = _hgrn_bwd(proj, p["hgrn_lb_raw"], p["hgrn_norm"], o_h, states, dyh, dqkv,
                                             [dg0, dg1], "hgrn_bwd")
    for piece in W_IN_PIECES:
        g, got = _mm_tn_blk(h1, dproj, N_CHIPS, f"g_{piece}", x_cols=W_IN_ROWS[piece],
                            plans=ex.carry(f"g_{piece}"))
        ex.done(f"g_{piece}", got)
        ex.grad(piece, g)
    dh1, got = _mm_nt_blk(dproj, ex.weight("w_in"), "d_proj_in", plans=ex.carry("d_proj_in"))
    ex.done("d_proj_in", got)
    (grad_x, g_pre_mix), got = _prenorm_bwd(dh1, xs, p["pre_mix_norm"], dx1, "pre_mix_norm_bwd",
                                            plans=ex.carry("pre_mix_norm_bwd"))
    ex.done("pre_mix_norm_bwd", got)
    small = dict(pre_mix_norm=g_pre_mix, rel_bias=g_rel_bias, hgrn_lb_raw=g_lb_raw, hgrn_norm=g_hgrn_norm,
                 post_mix_norm=g_post_mix, pre_ffn_norm=g_pre_ffn, conv_w=g_conv_w, conv_b=g_conv_b,
                 post_ffn_norm=g_post_ffn)
    return loss, grad_x, small


SMALL = ("pre_mix_norm", "rel_bias", "hgrn_lb_raw", "hgrn_norm", "post_mix_norm", "pre_ffn_norm", "conv_w", "conv_b",
         "post_ffn_norm")
BIG = ("w_in", "w_up", "w_down", "w_out", "w_branch_attn", "w_branch_hgrn")
WEIGHTS = ("pre_mix_norm", "w_in", "rel_bias", "hgrn_lb_raw", "hgrn_norm", "w_branch_attn", "w_branch_hgrn", "w_out",
           "post_mix_norm", "pre_ffn_norm", "w_up", "conv_w", "conv_b", "w_down", "post_ffn_norm")
MIXER = ("w_out", "w_branch_attn", "w_branch_hgrn")

SCHEDULE = {
    "attn_fwd0": [("gather_ici_cw", MIXER)],
    "attn_fwd1": [("gather_pass", MIXER), ("gather_ici", ("w_up",))],
    "attn_fwd2": [("gather_pass", ("w_up",))],
    "ffn_up": [("gather_ici", ("w_down",))],
    "conv_gelu_fwd": [("gather_pass", ("w_down",))],
    "conv_gelu_bwd": [("pair", ("w_down",))],
    "conv_input_bwd": [("chip", ("w_down",))],
    "post_mix_norm_bwd": [("pair", ("w_up",))],
    "attn_bwd0": [("chip", ("w_up",)), ("pair", MIXER)],
    "attn_bwd1": [("chip", MIXER)],
    "g_w_in_b": [("pair", ("w_in_a",))],
    "d_proj_in": [("chip", ("w_in_a",)), ("pair", ("w_in_b",))],
    "pre_mix_norm_bwd": [("chip", ("w_in_b",))],
}
W_IN_ROWS = dict(w_in_a=(0, 768), w_in_b=(3, 256))
W_IN_PIECES = tuple(W_IN_ROWS)
REDUCED = W_IN_PIECES + BIG[1:]


class _Exchange:
    def __init__(self, place, slots, conv_w_shard):
        self.place, self.slots, self.conv_w_shard = place, dict(slots), conv_w_shard
        self.conv_w = None
        self.g, self.from_sibling, self.pair_sums, self.arrived = {}, {}, {}, {}
        self.pending = []

    def weight(self, name):
        if name == "conv_w":
            return self.conv_w
        w = self.slots[name]
        return w.reshape(-1, D_MODEL) if name in ("w_out", "w_down") else w

    def project(self, h):
        proj, self.slots["w_in"] = _proj_gathered(h, self.slots["w_in"], self.place, "proj_in")
        return proj

    def grad(self, name, g):
        self.g[name] = g

    def carry(self, point):
        plans = []
        self.pending = SCHEDULE.get(point, [])
        for kind, names in self.pending:
            if kind in ("gather_ici", "gather_ici_cw"):
                wholes = [self.conv_w_shard] if kind == "gather_ici_cw" else []
                plans.append(_gather_ici_plan([self.slots[n] for n in names], wholes))
            elif kind == "gather_pass":
                plans.append(_gather_pass_plan([self.slots[n] for n in names]))
            elif kind == "pair":
                plans.append(_pair_plan([self.g[n] for n in names]))
            else:
                for n in names:
                    self.pair_sums[n] = _pair_sum(self.g[n], self.from_sibling[n], self.place[1:2], f"pair_sum_{n}")
                plans.append(_chip_plan([self.pair_sums[n] for n in names]))
        return plans

    def done(self, point, carried):
        for (kind, names), got in zip(self.pending, carried):
            if kind in ("gather_ici", "gather_ici_cw", "gather_pass"):
                self.slots.update(zip(names, got))
                if kind == "gather_ici_cw":
                    self.conv_w = got[len(names)].transpose(1, 0, 2).reshape(3, 2 * D_FF)
            elif kind == "pair":
                self.from_sibling.update(zip(names, got))
            else:
                self.arrived.update(zip(names, got))

    def reduced_halves(self):
        return [_chip_sum(self.arrived[n], self.pair_sums[n], self.place, f"chip_sum_{n}") for n in REDUCED]


def kernel(x, pre_mix_norm, w_in, rel_bias, hgrn_lb_raw, hgrn_norm, w_branch_attn, w_branch_hgrn, w_out, post_mix_norm, pre_ffn_norm, w_up, conv_w, conv_b, w_down, post_ffn_norm, loss_target, m_pre_mix_norm, m_w_in, m_rel_bias, m_hgrn_lb_raw, m_hgrn_norm, m_w_branch_attn, m_w_branch_hgrn, m_w_out, m_post_mix_norm, m_pre_ffn_norm, m_w_up, m_conv_w, m_conv_b, m_w_down, m_post_ffn_norm, v_pre_mix_norm, v_w_in, v_rel_bias, v_hgrn_lb_raw, v_hgrn_norm, v_w_branch_attn, v_w_branch_hgrn, v_w_out, v_post_mix_norm, v_pre_ffn_norm, v_w_up, v_conv_w, v_conv_b, v_w_down, v_post_ffn_norm):
    w = dict(pre_mix_norm=pre_mix_norm, w_in=w_in, rel_bias=rel_bias, hgrn_lb_raw=hgrn_lb_raw, hgrn_norm=hgrn_norm,
             w_branch_attn=w_branch_attn, w_branch_hgrn=w_branch_hgrn, w_out=w_out, post_mix_norm=post_mix_norm,
             pre_ffn_norm=pre_ffn_norm, w_up=w_up, conv_w=conv_w, conv_b=conv_b, w_down=w_down,
             post_ffn_norm=post_ffn_norm)
    m = dict(pre_mix_norm=m_pre_mix_norm, w_in=m_w_in, rel_bias=m_rel_bias, hgrn_lb_raw=m_hgrn_lb_raw,
             hgrn_norm=m_hgrn_norm, w_branch_attn=m_w_branch_attn, w_branch_hgrn=m_w_branch_hgrn, w_out=m_w_out,
             post_mix_norm=m_post_mix_norm, pre_ffn_norm=m_pre_ffn_norm, w_up=m_w_up, conv_w=m_conv_w,
             conv_b=m_conv_b, w_down=m_w_down, post_ffn_norm=m_post_ffn_norm)
    v = dict(pre_mix_norm=v_pre_mix_norm, w_in=v_w_in, rel_bias=v_rel_bias, hgrn_lb_raw=v_hgrn_lb_raw,
             hgrn_norm=v_hgrn_norm, w_branch_attn=v_w_branch_attn, w_branch_hgrn=v_w_branch_hgrn, w_out=v_w_out,
             post_mix_norm=v_post_mix_norm, pre_ffn_norm=v_pre_ffn_norm, w_up=v_w_up, conv_w=v_conv_w,
             conv_b=v_conv_b, w_down=v_w_down, post_ffn_norm=v_post_ffn_norm)
    shard2d = {n: (w[n][0] if w[n].ndim == 3 else w[n]) for n in WEIGHTS}
    chip = 2 * lax.axis_index("x") + lax.axis_index("y")
    core = lax.axis_index("c")

    place = jnp.stack([chip, core]).astype(jnp.int32)
    slots = {n: _cast_into_slot(shard2d[n], place, f"cast_{n}") for n in BIG}
    ex = _Exchange(place, slots, shard2d["conv_w"])
    loss, grad_x, small = _local_step(x[0], loss_target[0], {n: w[n] for n in SMALL if n != "conv_w"}, ex)

    flat = [small[n].reshape(-1) for n in SMALL] + [loss.reshape(-1)]
    sizes = [t.shape[0] for t in flat]
    summed, wholes = _all_sum(jnp.concatenate(flat).reshape(-1, LANES), ex.reduced_halves(), "sum_small")
    summed = summed.reshape(-1)
    offs = [sum(sizes[:i]) for i in range(len(sizes))]
    grads = {}
    for n, o, sz in zip(SMALL, offs, sizes):
        grads[n] = summed[o:o + sz].reshape(small[n].shape)
    loss_total = summed[offs[-1]]
    cw = 2 * D_FF // N_CHIPS
    grads["conv_w"] = lax.dynamic_slice(grads["conv_w"], (0, chip * cw), (3, cw))

    big = dict(zip(REDUCED, wholes))
    big["w_in"] = jnp.concatenate([big.pop(n) for n in W_IN_PIECES], axis=0)
    grads.update(big)

    out_g, out_d, out_m, out_v = [], [], [], []
    for n in WEIGHTS:
        d2, m2, v2 = _adamw(shard2d[n], grads[n], m[n].reshape(shard2d[n].shape), v[n].reshape(shard2d[n].shape),
                            f"adamw_{n}")
        shape = w[n].shape
        out_g.append(grads[n].reshape(shape))
        out_d.append(d2.reshape(shape))
        out_m.append(m2.reshape(shape))
        out_v.append(v2.reshape(shape))
    return (loss_total, grad_x[None], *out_g, *out_d, *out_m, *out_v)
```

```python
import functools
import math

import jax
import jax.numpy as jnp
from jax import lax
from jax.experimental import pallas as pl
from jax.experimental.pallas import tpu as pltpu

F32 = jnp.float32
BF16 = jnp.bfloat16
MESH = pl.DeviceIdType.MESH

D_MODEL = 1024
N_GROUPS = 3
DILATIONS = (1, 4, 16)
HEADS = 8
HEAD_DIM = 64
GROUP_W = HEADS * HEAD_DIM
QKV_W = N_GROUPS * 3 * GROUP_W
BLK = 128
NEG_INF = -1e30
NUM_BUCKETS = 32
MAX_EXACT = 16
MAX_DISTANCE = 2048
HG_HEADS = 4
HG_DK = 128
HG_W = HG_HEADS * HG_DK
HG_CHUNK = 32
HG_TILE = 256
IN_W = QKV_W + 4 * HG_W + 2 * D_MODEL
D_FF = 2816
EPS = 1e-6
N_CHIPS = 4
N_DEV = 8
LANES = 128

ADAM_LR, ADAM_B1, ADAM_B2, ADAM_EPS, ADAM_WD, ADAM_STEP = 0.001, 0.9, 0.999, 1e-08, 0.01, 10

VMEM_LIMIT = 56 * 1024 * 1024


def _cp(n_axes):
    return pltpu.CompilerParams(dimension_semantics=("arbitrary",) * n_axes, vmem_limit_bytes=VMEM_LIMIT)


def _sds(shape, dtype):
    return jax.ShapeDtypeStruct(tuple(shape), dtype)


def _sigmoid(v):
    return 1.0 / (1.0 + jnp.exp(-v))


def _bf(v):
    return v.astype(BF16)


def _dot(a, b, dims):
    return lax.dot_general(a, b, (dims, ((), ())), preferred_element_type=F32)


NN = ((1,), (0,))
NT = ((1,), (1,))
TN = ((0,), (0,))

ANY = pl.BlockSpec(memory_space=pl.ANY)


class _Plan:
    def __init__(self, copies, n_sems, ins=(), inouts=(), outs=()):
        self.copies, self.n_sems = copies, n_sems
        self.ins, self.inouts, self.outs = list(ins), list(inouts), list(outs)


def _call(body, plans=None, *, name, grid, in_specs, out_specs, out_shape, args, scratch_shapes=()):
    plans = list(plans or ())
    in_specs, out_specs, out_shape = list(in_specs), list(out_specs), list(out_shape)
    scratch_shapes = list(scratch_shapes)
    n_in, n_out, n_scr = len(in_specs), len(out_specs), len(scratch_shapes)
    x_in, x_out, aliases, spans = [], [], {}, []
    for p in plans:
        i0, o0 = len(x_in), len(x_out)
        x_in += p.ins
        for a in p.inouts:
            aliases[n_in + len(x_in)] = n_out + len(x_out)
            x_in.append(a)
            x_out.append(_sds(a.shape, a.dtype))
        x_out += p.outs
        spans.append((i0, len(p.ins), o0, len(p.inouts), len(p.outs)))
    sems = [pltpu.SemaphoreType.DMA((p.n_sems,)) for p in plans for _ in range(3)]

    def wrapped(*refs):
        xi = refs[n_in:n_in + len(x_in)]
        base = n_in + len(x_in)
        xo = refs[base + n_out:base + n_out + len(x_out)]
        sbase = base + n_out + len(x_out)
        xs = refs[sbase + n_scr:]
        ids = [pl.program_id(k) for k in range(len(grid))]
        first = functools.reduce(jnp.logical_and, [i == 0 for i in ids])
        last = functools.reduce(jnp.logical_and, [i == g - 1 for i, g in zip(ids, grid)])

        def descriptors(k):
            i0, ni, o0, nio, no = spans[k]
            return plans[k].copies(xi[i0:i0 + ni], xo[o0:o0 + nio], xo[o0 + nio:o0 + nio + no], *xs[3 * k:3 * k + 3])

        @pl.when(first)
        def _():
            for k in range(len(plans)):
                sends, _, local = descriptors(k)
                for cp in (*sends, *local):
                    cp.start()

        body(*refs[:n_in], *refs[base:base + n_out], *refs[sbase:sbase + n_scr])

        @pl.when(last)
        def _():
            for k in range(len(plans)):
                sends, recvs, local = descriptors(k)
                for cp in recvs:
                    cp.wait_recv()
                for cp in sends:
                    cp.wait_send()
                for cp in local:
                    cp.wait()

    res = pl.pallas_call(
        wrapped if plans else body, name=name, grid=grid, in_specs=in_specs + [ANY] * len(x_in),
        out_specs=out_specs + [ANY] * len(x_out), out_shape=out_shape + x_out, input_output_aliases=aliases,
        scratch_shapes=scratch_shapes + sems, compiler_params=_cp(len(grid)))(*args, *x_in)
    res = list(res)
    carried = [res[n_out + o0:n_out + o0 + nio + no] for (_, _, o0, nio, no) in spans]
    return res[:n_out], carried


def _mm_nn_blk(a, wg, name, tm=512, plans=None):
    M, K = a.shape
    nb, _, Nb = wg.shape

    def body(a_ref, w_ref, o_ref):
        o_ref[...] = _dot(_bf(a_ref[...]), w_ref[...], NN)

    (out,), carried = _call(
        body, plans, name=name, grid=(nb, M // tm),
        in_specs=[pl.BlockSpec((tm, K), lambda j, i: (i, 0)), pl.BlockSpec((None, K, Nb), lambda j, i: (j, 0, 0))],
        out_specs=[pl.BlockSpec((tm, Nb), lambda j, i: (i, j))],
        out_shape=[_sds((M, nb * Nb), F32)], args=(a, wg))
    return out if plans is None else (out, carried)


def _mm_nt_blk(dy, wg, name, tm=1024, plans=None):
    M = dy.shape[0]
    nb, K, Nb = wg.shape

    def body(dy_ref, w_ref, o_ref):
        j = pl.program_id(1)
        r = _dot(_bf(dy_ref[...]), w_ref[...], NT)

        @pl.when(j == 0)
        def _():
            o_ref[...] = r

        @pl.when(j > 0)
        def _():
            o_ref[...] += r

    (out,), carried = _call(
        body, plans, name=name, grid=(M // tm, nb),
        in_specs=[pl.BlockSpec((tm, Nb), lambda i, j: (i, j)), pl.BlockSpec((None, K, Nb), lambda i, j: (j, 0, 0))],
        out_specs=[pl.BlockSpec((tm, K), lambda i, j: (i, 0))],
        out_shape=[_sds((M, K), F32)], args=(dy, wg))
    return out if plans is None else (out, carried)


def _mm_tn_blk(x, dy, nb, name, tk=2048, x_cols=None, plans=None, together=False):
    T, Mx = x.shape
    xk, Mx = (0, Mx) if x_cols is None else x_cols
    Nb = dy.shape[1] // nb
    nj = nb if together else 1

    def body(x_ref, dy_ref, o_ref):
        t = pl.program_id(1)
        r = _dot(_bf(x_ref[...]), _bf(dy_ref[...]), TN)
        for j in range(nj):
            rj = r[:, j * Nb:(j + 1) * Nb]

            @pl.when(t == 0)
            def _():
                o_ref[j] = rj

            @pl.when(t > 0)
            def _():
                o_ref[j] += rj

    (out,), carried = _call(
        body, plans, name=name, grid=(nb // nj, T // tk),
        in_specs=[pl.BlockSpec((tk, Mx), lambda j, t: (t, xk)), pl.BlockSpec((tk, nj * Nb), lambda j, t: (t, j))],
        out_specs=[pl.BlockSpec((nj, Mx, Nb), lambda j, t: (j, 0, 0))],
        out_shape=[_sds((nb, Mx, Nb), F32)], args=(x, dy))
    return out if plans is None else (out, carried)


def _mm_tn(x, dy, name, tk=1024):
    T, Mx = x.shape
    N = dy.shape[1]

    def body(x_ref, dy_ref, o_ref):
        t = pl.program_id(0)
        r = _dot(_bf(x_ref[...]), _bf(dy_ref[...]), TN)

        @pl.when(t == 0)
        def _():
            o_ref[...] = r

        @pl.when(t > 0)
        def _():
            o_ref[...] += r

    return pl.pallas_call(
        body, name=name, grid=(T // tk,),
        in_specs=[pl.BlockSpec((tk, Mx), lambda t: (t, 0)), pl.BlockSpec((tk, N), lambda t: (t, 0))],
        out_specs=pl.BlockSpec((Mx, N), lambda t: (0, 0)),
        out_shape=_sds((Mx, N), F32), compiler_params=_cp(1))(x, dy)


def _tile(arr, bw, col=lambda c: 0):
    return ("tile", arr, bw, col)


def _full(arr):
    return ("full", arr)


def _out_tile(width, dtype, bw, col=lambda c: 0):
    return ("tile", width, dtype, bw, col)


def _out_acc(rows, width, bw, col=lambda c: 0):
    return ("acc", rows, width, bw, col)


def _rows_call(name, body, n_rows, tm, ncol, ins, outs, plans=None):
    in_specs, args = [], []
    for e in ins:
        if e[0] == "tile":
            _, arr, bw, col = e
            in_specs.append(pl.BlockSpec((tm, bw), functools.partial(lambda c, i, col: (i, col(c)), col=col)))
        else:
            arr = e[1]
            in_specs.append(pl.BlockSpec(arr.shape, functools.partial(lambda c, i, nd: (0,) * nd, nd=arr.ndim)))
        args.append(arr)
    out_specs, out_shape = [], []
    for e in outs:
        if e[0] == "tile":
            _, width, dtype, bw, col = e
            out_specs.append(pl.BlockSpec((tm, bw), functools.partial(lambda c, i, col: (i, col(c)), col=col)))
            out_shape.append(_sds((n_rows, width), dtype))
        else:
            _, rows, width, bw, col = e
            out_specs.append(pl.BlockSpec((rows, bw), functools.partial(lambda c, i, col: (0, col(c)), col=col)))
            out_shape.append(_sds((rows, width), F32))
    out, carried = _call(body, plans, name=name, grid=(ncol, n_rows // tm), in_specs=in_specs, out_specs=out_specs,
                         out_shape=out_shape, args=args)
    return out if plans is None else (out, carried)


def _acc(ref, val):
    i = pl.program_id(1)

    @pl.when(i == 0)
    def _():
        ref[...] = val

    @pl.when(i > 0)
    def _():
        ref[...] += val


def _rinv(z):
    return lax.rsqrt(jnp.mean(z * z, axis=-1, keepdims=True) + EPS)


def _norm_bwd(dy, zhat, r, w):
    dyw = dy * w
    return r * (dyw - zhat * jnp.mean(dyw * zhat, axis=-1, keepdims=True))


def _norm_fwd(x, w, name):
    def body(x_ref, w_ref, h_ref):
        xv = x_ref[...]
        h_ref[...] = _bf(xv * _rinv(xv) * w_ref[...])

    return _rows_call(name, body, x.shape[0], 512, 1, [_tile(x, D_MODEL), _full(w)],
                      [_out_tile(D_MODEL, BF16, D_MODEL)])[0]


def _prenorm_bwd(dh, xin, w, dres, name, plans=None):
    def body(dh_ref, x_ref, w_ref, dres_ref, dx_ref, dw_ref):
        xv = x_ref[...]
        r = _rinv(xv)
        xhat = xv * r
        dhv = dh_ref[...]
        dx_ref[...] = dres_ref[...] + _norm_bwd(dhv, xhat, r, w_ref[...])
        _acc(dw_ref, jnp.sum(dhv * xhat, axis=0, keepdims=True))

    return _rows_call(name, body, xin.shape[0], 512, 1,
                      [_tile(dh, D_MODEL), _tile(xin, D_MODEL), _full(w), _tile(dres, D_MODEL)],
                      [_out_tile(D_MODEL, F32, D_MODEL), _out_acc(1, D_MODEL, D_MODEL)], plans)


def _postnorm_bwd(dout, z, w, w_mat, name, plans=None):
    def body(do_ref, z_ref, w_ref, wm_ref, dz_ref, dm_ref, dw_ref):
        zv = z_ref[...]
        r = _rinv(zv)
        zhat = zv * r
        dov = do_ref[...]
        dz = _bf(_norm_bwd(dov, zhat, r, w_ref[...]))
        dz_ref[...] = dz
        dm_ref[...] = _dot(dz, wm_ref[...], NT)
        _acc(dw_ref, jnp.sum(dov * zhat, axis=0, keepdims=True))

    return _rows_call(name, body, z.shape[0], 512, 1,
                      [_tile(dout, D_MODEL), _tile(z, D_MODEL), _full(w), _full(w_mat)],
                      [_out_tile(D_MODEL, BF16, D_MODEL), _out_tile(D_MODEL, F32, D_MODEL),
                       _out_acc(1, D_MODEL, D_MODEL)], plans)


def _t5_bucket(dist):
    n = jnp.maximum(dist, 0)
    nf = jnp.maximum(n, 1).astype(F32)
    large = MAX_EXACT + (jnp.log(nf / MAX_EXACT) / math.log(MAX_DISTANCE / MAX_EXACT)
                         * (NUM_BUCKETS - MAX_EXACT)).astype(jnp.int32)
    large = jnp.minimum(large, NUM_BUCKETS - 1)
    return jnp.where(n < MAX_EXACT, n, large)


def _band_rel():
    return jnp.arange(BLK)[:, None] + BLK - jnp.arange(2 * BLK)[None, :]


def _band_valid():
    rel = _band_rel()
    window = (rel >= 0) & (rel <= BLK)
    first = window & (jnp.arange(2 * BLK)[None, :] >= BLK)
    return jnp.stack([first, window]).astype(F32).reshape(2, 1, BAND)


RES_UNROLL = 8
PAIR = LANES // HEAD_DIM


def _pair_lanes():
    first = lax.broadcasted_iota(jnp.int32, (1, LANES), 1) < HEAD_DIM
    return first, jnp.logical_not(first)


def _heads_per_step(d):
    return HEADS if d == 1 else LANES // HEAD_DIM


def _sub_rows(r, d):
    return pl.ds(r, BLK, stride=d) if d > 1 else pl.ds(0, BLK)


def _for_residues(d, fn):
    if d <= RES_UNROLL:
        for r in range(d):
            fn(r)
    else:
        def group(i, carry):
            for k in range(RES_UNROLL):
                fn(i * RES_UNROLL + k)
            return carry

        lax.fori_loop(0, d // RES_UNROLL, group, 0)


def _attn_specs(d, g, qblock):
    cw = _heads_per_step(d) * HEAD_DIM

    def col(part, hp):
        return (g * 3 + part) * (GROUP_W // cw) + hp

    def cur(part):
        return pl.BlockSpec((d * BLK, cw), lambda hp, n: (qblock(n), col(part, hp)))

    def prev(part):
        return pl.BlockSpec((d * BLK, cw), lambda hp, n: (jnp.maximum(qblock(n) - 1, 0), col(part, hp)))

    return cur, prev


def _attn_fwd(proj, bias, g, name, plans=None):
    S = proj.shape[0]
    d = DILATIONS[g]
    NB = S // (d * BLK)
    hps = _heads_per_step(d)

    def body(q_ref, kp_ref, kc_ref, vp_ref, vc_ref, b_ref, o_ref, lse_ref):
        hp = pl.program_id(0)
        later = jnp.minimum(pl.program_id(1), 1)

        def residue(r):
            rows = _sub_rows(r, d)
            q2 = q_ref[rows, :]
            k2 = jnp.concatenate([kp_ref[rows, :], kc_ref[rows, :]], axis=0)
            v2 = jnp.concatenate([vp_ref[rows, :], vc_ref[rows, :]], axis=0)
            outs, lses = [], []
            for pp in range(hps // PAIR):
                ps = slice(pp * LANES, (pp + 1) * LANES)
                qp, kp, vp = _bf(q2[:, ps]), _bf(k2[:, ps]), _bf(v2[:, ps])
                o_h, lse_h = [], []
                for hh, own in enumerate(_pair_lanes()):
                    s = _dot(qp, jnp.where(own, kp, 0), NT) * (HEAD_DIM ** -0.5) + b_ref[later, hp * hps + pp * PAIR + hh]
                    m = jnp.max(s, axis=-1, keepdims=True)
                    p = jnp.exp(s - m)
                    l = jnp.sum(p, axis=-1, keepdims=True)
                    o_h.append(_dot(_bf(p), vp, NN) / l)
                    lse_h.append(m + jnp.log(l))
                first = _pair_lanes()[0]
                outs.append(jnp.where(first, o_h[0], o_h[1]))
                lses.append(jnp.where(first, lse_h[0], lse_h[1]))
            o_ref[rows, :] = outs[0] if len(outs) == 1 else jnp.concatenate(outs, axis=1)
            lse_ref[rows, :] = lses[0] if len(lses) == 1 else jnp.concatenate(lses, axis=1)

        _for_residues(d, residue)

    cur, prev = _attn_specs(d, g, lambda n: n)
    out = pl.BlockSpec((d * BLK, hps * HEAD_DIM), lambda hp, n: (n, hp))
    res, carried = _call(
        body, plans, name=name, grid=(HEADS // hps, NB),
        in_specs=[cur(0), prev(1), cur(1), prev(2), cur(2),
                  pl.BlockSpec((2, HEADS, BLK, 2 * BLK), lambda hp, n: (0, 0, 0, 0))],
        out_specs=[out, out], out_shape=[_sds((S, GROUP_W), F32)] * 2,
        args=(proj, proj, proj, proj, proj, bias))
    return res if plans is None else (res, carried)


def _attn_bwd(proj, bias, lse, y, dy, g, name, plans=None):
    S = proj.shape[0]
    d = DILATIONS[g]
    NB = S // (d * BLK)
    hps = _heads_per_step(d)

    def body(q_ref, kp_ref, kc_ref, vp_ref, vc_ref, b_ref, l_ref, y_ref, dy_ref,
             dq_ref, dk_ref, dv_ref, db_ref, ck_ref, cv_ref):
        hp, n = pl.program_id(0), pl.program_id(1)

        @pl.when((hp == 0) & (n == 0))
        def _():
            db_ref[...] = jnp.zeros_like(db_ref)

        @pl.when(n == 0)
        def _():
            ck_ref[...] = jnp.zeros_like(ck_ref)
            cv_ref[...] = jnp.zeros_like(cv_ref)

        @pl.when(n < NB)
        def _():
            later = jnp.minimum(n, 1)

            def residue(r):
                rows = _sub_rows(r, d)
                q2 = q_ref[rows, :]
                k2 = jnp.concatenate([kp_ref[rows, :], kc_ref[rows, :]], axis=0)
                v2 = jnp.concatenate([vp_ref[rows, :], vc_ref[rows, :]], axis=0)
                l2, y2, dy2 = l_ref[rows, :], y_ref[rows, :], dy_ref[rows, :]
                dqs, dks, dvs = [], [], []
                for pp in range(hps // PAIR):
                    ps = slice(pp * LANES, (pp + 1) * LANES)
                    qp, kp, vp = _bf(q2[:, ps]), _bf(k2[:, ps]), _bf(v2[:, ps])
                    dyp, yp = dy2[:, ps], y2[:, ps]
                    dq_h, dk_h, dv_h = [], [], []
                    for hh, own in enumerate(_pair_lanes()):
                        head = hp * hps + pp * PAIR + hh
                        s = _dot(qp, jnp.where(own, kp, 0), NT) * (HEAD_DIM ** -0.5) + b_ref[later, head]
                        p = jnp.exp(s - l2[:, pp * LANES + hh * HEAD_DIM:pp * LANES + hh * HEAD_DIM + 1])
                        dyh = jnp.where(own, dyp, 0.0)
                        delta = jnp.sum(dyh * yp, axis=-1, keepdims=True)
                        ds = p * (_dot(_bf(dyh), vp, NT) - delta)
                        db_ref[head] += ds
                        dsb = _bf(ds * (HEAD_DIM ** -0.5))
                        dq_h.append(_dot(dsb, kp, NN))
                        dk_h.append(_dot(dsb, qp, TN))
                        dv_h.append(_dot(_bf(p), _bf(dyp), TN))
                    first = _pair_lanes()[0]
                    dqs.append(jnp.where(first, dq_h[0], dq_h[1]))
                    dks.append(jnp.where(first, dk_h[0], dk_h[1]))
                    dvs.append(jnp.where(first, dv_h[0], dv_h[1]))
                dkb = dks[0] if len(dks) == 1 else jnp.concatenate(dks, axis=1)
                dvb = dvs[0] if len(dvs) == 1 else jnp.concatenate(dvs, axis=1)
                dq_ref[rows, :] = dqs[0] if len(dqs) == 1 else jnp.concatenate(dqs, axis=1)
                dk_ref[rows, :] = ck_ref[rows, :] + dkb[:BLK]
                dv_ref[rows, :] = cv_ref[rows, :] + dvb[:BLK]
                ck_ref[rows, :] = dkb[BLK:]
                cv_ref[rows, :] = dvb[BLK:]

            _for_residues(d, residue)

        @pl.when(n == NB)
        def _():
            dk_ref[...] = ck_ref[...]
            dv_ref[...] = cv_ref[...]

    def qn(n):
        return jnp.minimum(n, NB - 1)

    cur, prev = _attn_specs(d, g, qn)
    cw = hps * HEAD_DIM
    row = pl.BlockSpec((d * BLK, cw), lambda hp, n: (qn(n), hp))
    done = pl.BlockSpec((d * BLK, cw), lambda hp, n: (jnp.maximum(n - 1, 0), hp))
    (dq, dk, dv, db), carried = _call(
        body, plans, name=name, grid=(HEADS // hps, NB + 1),
        in_specs=[cur(0), prev(1), cur(1), prev(2), cur(2),
                  pl.BlockSpec((2, HEADS, BLK, 2 * BLK), lambda hp, n: (0, 0, 0, 0)), row, row, row],
        out_specs=[row, done, done, pl.BlockSpec((HEADS, BLK, 2 * BLK), lambda hp, n: (0, 0, 0))],
        out_shape=[_sds((S, GROUP_W), F32)] * 3 + [_sds((HEADS, BLK, 2 * BLK), F32)],
        scratch_shapes=[pltpu.VMEM((d * BLK, cw), F32)] * 2,
        args=(proj, proj, proj, proj, proj, bias, lse, y, dy))
    return ([dq, dk, dv], db) if plans is None else ([dq, dk, dv], db, carried)


BAND = BLK * 2 * BLK


def _bucket_onehot():
    buckets = jnp.stack([_t5_bucket(_band_rel() * d) for d in DILATIONS]).reshape(N_GROUPS, 1, BAND)
    return (buckets == jnp.arange(NUM_BUCKETS).reshape(1, NUM_BUCKETS, 1)).astype(F32)


def _relbias_fwd(rel_bias, name):
    table = rel_bias.reshape(NUM_BUCKETS, N_GROUPS, HEADS).transpose(1, 0, 2)

    def body(t_ref, oh_ref, valid_ref, o_ref):
        bias = lax.dot_general(t_ref[...], oh_ref[...], (TN, ((), ())), preferred_element_type=F32,
                               precision=lax.Precision.HIGHEST)
        for k in range(2):
            o_ref[k] = jnp.where(valid_ref[k] > 0.5, bias, NEG_INF)

    out = pl.pallas_call(
        body, name=name, grid=(N_GROUPS,),
        in_specs=[pl.BlockSpec((None, NUM_BUCKETS, HEADS), lambda g: (g, 0, 0)),
                  pl.BlockSpec((None, NUM_BUCKETS, BAND), lambda g: (g, 0, 0)),
                  pl.BlockSpec((2, 1, BAND), lambda g: (0, 0, 0))],
        out_specs=pl.BlockSpec((None, 2, HEADS, BAND), lambda g: (g, 0, 0, 0)),
        out_shape=_sds((N_GROUPS, 2, HEADS, BAND), F32), compiler_params=_cp(1))(table, _bucket_onehot(), _band_valid())
    return out.reshape(N_GROUPS, 2, HEADS, BLK, 2 * BLK)


def _relbias_bwd(dbs, name):
    band = BAND
    onehot = _bucket_onehot()
    dbf = jnp.stack([db.reshape(HEADS, band) for db in dbs])

    def body(oh_ref, db_ref, o_ref):
        o_ref[...] = lax.dot_general(oh_ref[...], db_ref[...], (NT, ((), ())), preferred_element_type=F32,
                                     precision=lax.Precision.HIGHEST)

    out = pl.pallas_call(
        body, name=name, grid=(N_GROUPS,),
        in_specs=[pl.BlockSpec((None, NUM_BUCKETS, band), lambda g: (g, 0, 0)),
                  pl.BlockSpec((None, HEADS, band), lambda g: (g, 0, 0))],
        out_specs=pl.BlockSpec((None, NUM_BUCKETS, HEADS), lambda g: (g, 0, 0)),
        out_shape=_sds((N_GROUPS, NUM_BUCKETS, HEADS), F32), compiler_params=_cp(1))(onehot, dbf)
    return out.transpose(1, 0, 2).reshape(NUM_BUCKETS, N_GROUPS * HEADS)


def _chunk_pos(shape):
    return lax.broadcasted_iota(jnp.int32, shape, 0) % HG_CHUNK


def _chunk_cumsum(v):
    pos = _chunk_pos(v.shape)
    s = 1
    while s < HG_CHUNK:
        v = v + jnp.where(pos >= s, pltpu.roll(v, s, 0), 0.0)
        s *= 2
    return v


def _chunk_rev_cumsum(v):
    pos = _chunk_pos(v.shape)
    n = v.shape[0]
    s = 1
    while s < HG_CHUNK:
        v = v + jnp.where(pos < HG_CHUNK - s, pltpu.roll(v, n - s, 0), 0.0)
        s *= 2
    return v


def _lower_bound(raw):
    a0, a1 = raw[0:1], raw[1:2]
    m = jnp.maximum(a0, a1)
    e0, e1 = jnp.exp(a0 - m), jnp.exp(a1 - m)
    return e0 / (e0 + e1)


def _hg_gates(qr, fr, lb):
    sf = _sigmoid(fr)
    f = lb + (1.0 - lb) * sf
    sq = _sigmoid(qr)
    return qr * sq, sq, f, sf


HG_COL0 = QKV_W // HG_W


def _hgrn_fwd(proj, lb_raw, nw, name, plans=None):
    S = proj.shape[0]
    ncs = HG_TILE // HG_CHUNK
    tril = jnp.tril(jnp.ones((HG_CHUNK, HG_CHUNK), dtype=bool))

    def body(q_ref, f_ref, i_ref, og_ref, lb_ref, nw_ref, y_ref, o_ref, st_ref, state):
        @pl.when(pl.program_id(0) == 0)
        def _():
            state[...] = jnp.zeros_like(state)

        lb = _lower_bound(lb_ref[...])
        q, _, f, _ = _hg_gates(q_ref[...], f_ref[...], lb)
        k = 1.0 - f
        G = _chunk_cumsum(jnp.log(f))
        row = lax.broadcasted_iota(jnp.int32, (HG_CHUNK, HG_CHUNK), 0)
        col = lax.broadcasted_iota(jnp.int32, (HG_CHUNK, HG_CHUNK), 1)
        heads = [slice(h * HG_DK, (h + 1) * HG_DK) for h in range(HG_HEADS)]
        sts = [state[h] for h in range(HG_HEADS)]
        for c in range(ncs):
            cs = slice(c * HG_CHUNK, (c + 1) * HG_CHUNK)
            for h, hs in enumerate(heads):
                Gc = G[cs, hs]
                gl = Gc[HG_CHUNK - 1:HG_CHUNK]
                qt = _bf(q[cs, hs] * jnp.exp(Gc))
                kt = _bf(k[cs, hs] * jnp.exp(-Gc))
                kd = _bf(k[cs, hs] * jnp.exp(gl - Gc))
                v = _bf(i_ref[cs, hs])
                A = jnp.where(row >= col, _dot(qt, kt, NT), 0.0)
                o_ref[cs, hs] = _dot(_bf(A), v, NN) + _dot(qt, _bf(sts[h]), NT)
                st_ref[c, h] = sts[h]
                sts[h] = sts[h] * jnp.exp(gl) + _dot(v, kd, TN)
        for h, hs in enumerate(heads):
            state[h] = sts[h]
            oh = o_ref[:, hs]
            og = og_ref[:, hs]
            y_ref[:, hs] = oh * _rinv(oh) * nw_ref[...] * (og * _sigmoid(og))

    def colspec(j):
        return pl.BlockSpec((HG_TILE, HG_W), lambda i: (i, HG_COL0 + j))

    res, carried = _call(
        body, plans, name=name, grid=(S // HG_TILE,),
        in_specs=[colspec(0), colspec(1), colspec(2), colspec(3),
                  pl.BlockSpec((2, HG_W), lambda i: (0, 0)), pl.BlockSpec((1, HG_DK), lambda i: (0, 0))],
        out_specs=[pl.BlockSpec((HG_TILE, HG_W), lambda i: (i, 0))] * 2
        + [pl.BlockSpec((ncs, HG_HEADS, HG_DK, HG_DK), lambda i: (i, 0, 0, 0))],
        out_shape=[_sds((S, HG_W), F32)] * 2 + [_sds((S // HG_CHUNK, HG_HEADS, HG_DK, HG_DK), F32)],
        scratch_shapes=[pltpu.VMEM((HG_HEADS, HG_DK, HG_DK), F32)],
        args=(proj, proj, proj, proj, lb_raw, nw))
    return res if plans is None else (res, carried)


def _hgrn_bwd(proj, lb_raw, nw, o, states, dy, d_attn, d_gates, name):
    S = proj.shape[0]
    ncs = HG_TILE // HG_CHUNK
    nt = S // HG_TILE
    n_a, n_g = len(d_attn), len(d_gates)
    own = [slice(QKV_W + j * HG_W, QKV_W + (j + 1) * HG_W) for j in range(4)]

    def body(q_ref, f_ref, i_ref, og_ref, lb_ref, nw_ref, o_ref, st_ref, dy_ref, *rest):
        attn_refs, gate_refs = rest[:n_a], rest[n_a:n_a + n_g]
        dp_ref, dlb_ref, dnw_ref, dstate, do_s, dG_s, dgl_s, dk_s, dlb_s = rest[n_a + n_g:]
        dq_ref, df_ref, di_ref, dog_ref = (dp_ref.at[:, cols] for cols in own)
        step = pl.program_id(0)
        for k, a_ref in enumerate(attn_refs):
            dp_ref[:, k * GROUP_W:(k + 1) * GROUP_W] = _bf(a_ref[...])
        for k, g_ref in enumerate(gate_refs):
            dp_ref[:, QKV_W + 4 * HG_W + k * D_MODEL:QKV_W + 4 * HG_W + (k + 1) * D_MODEL] = g_ref[...]

        @pl.when(step == 0)
        def _():
            dstate[...] = jnp.zeros_like(dstate)
            dlb_s[...] = jnp.zeros_like(dlb_s)
            dnw_ref[...] = jnp.zeros_like(dnw_ref)

        lb = _lower_bound(lb_ref[...])
        qr = q_ref[...]
        q, sq, f, sf = _hg_gates(qr, f_ref[...], lb)
        k = 1.0 - f
        G = _chunk_cumsum(jnp.log(f))
        nwv = nw_ref[...]
        row = lax.broadcasted_iota(jnp.int32, (HG_CHUNK, HG_CHUNK), 0)
        col = lax.broadcasted_iota(jnp.int32, (HG_CHUNK, HG_CHUNK), 1)
        for h in range(HG_HEADS):
            hs = slice(h * HG_DK, (h + 1) * HG_DK)
            oh = o_ref[:, hs]
            r = _rinv(oh)
            ohat = oh * r
            og = og_ref[:, hs]
            sg = _sigmoid(og)
            dyh = dy_ref[:, hs]
            don = dyh * (og * sg)
            dog_ref[:, hs] = _bf(dyh * (ohat * nwv) * (sg * (1.0 + og * (1.0 - sg))))
            dnw_ref[...] += jnp.sum(don * ohat, axis=0, keepdims=True)
            do_s[:, hs] = _norm_bwd(don, ohat, r, nwv)
        dsts = [dstate[h] for h in range(HG_HEADS)]
        for c in reversed(range(ncs)):
            cs = slice(c * HG_CHUNK, (c + 1) * HG_CHUNK)
            for h in range(HG_HEADS):
                hs = slice(h * HG_DK, (h + 1) * HG_DK)
                dst = dsts[h]
                Gc = G[cs, hs]
                gl = Gc[HG_CHUNK - 1:HG_CHUNK]
                eG, enG, edG, egl = jnp.exp(Gc), jnp.exp(-Gc), jnp.exp(gl - Gc), jnp.exp(gl)
                qt, kt, kd = q[cs, hs] * eG, k[cs, hs] * enG, k[cs, hs] * edG
                qtb, ktb, kdb = _bf(qt), _bf(kt), _bf(kd)
                v = _bf(i_ref[cs, hs])
                do = _bf(do_s[cs, hs])
                st = st_ref[c, h]
                dstb = _bf(dst)
                A = jnp.where(row >= col, _dot(qtb, ktb, NT), 0.0)
                dA = _bf(jnp.where(row >= col, _dot(do, v, NT), 0.0))
                di_ref[cs, hs] = _bf(_dot(_bf(A), do, TN) + _dot(kdb, dstb, NT))
                dqt = _dot(dA, ktb, NN) + _dot(do, _bf(st), NN)
                dkt = _dot(dA, qtb, TN)
                dkd = _dot(v, dstb, NN)
                dgl = egl * jnp.sum(st * dst, axis=0, keepdims=True) + jnp.sum(dkd * kd, axis=0, keepdims=True)
                dsts[h] = dst * egl + _dot(do, qtb, TN)
                dq_ref[cs, hs] = _bf(dqt * eG * (sq[cs, hs] * (1.0 + qr[cs, hs] * (1.0 - sq[cs, hs]))))
                dk_s[cs, hs] = dkt * enG + dkd * edG
                dG_s[cs, hs] = dqt * qt - dkt * kt - dkd * kd
                dgl_s[cs, hs] = jnp.broadcast_to(dgl, (HG_CHUNK, HG_DK))
        for h in range(HG_HEADS):
            dstate[h] = dsts[h]
        dg = _chunk_rev_cumsum(dG_s[...]) + dgl_s[...]
        dfv = dg / f - dk_s[...]
        df_ref[...] = _bf(dfv * (1.0 - lb) * sf * (1.0 - sf))
        dlb_s[...] += jnp.sum(dfv * (1.0 - sf), axis=0, keepdims=True)

        @pl.when(step == nt - 1)
        def _():
            t = dlb_s[...] * lb * (1.0 - lb)
            dlb_ref[...] = jnp.concatenate([t, -t], axis=0)

    def colspec(j):
        return pl.BlockSpec((HG_TILE, HG_W), lambda i: (nt - 1 - i, HG_COL0 + j))

    def rows(width):
        return pl.BlockSpec((HG_TILE, width), lambda i: (nt - 1 - i, 0))

    tile = rows(HG_W)
    return pl.pallas_call(
        body, name=name, grid=(nt,),
        in_specs=[colspec(0), colspec(1), colspec(2), colspec(3),
                  pl.BlockSpec((2, HG_W), lambda i: (0, 0)), pl.BlockSpec((1, HG_DK), lambda i: (0, 0)),
                  tile, pl.BlockSpec((ncs, HG_HEADS, HG_DK, HG_DK), lambda i: (nt - 1 - i, 0, 0, 0)), tile]
        + [rows(GROUP_W)] * n_a + [rows(D_MODEL)] * n_g,
        out_specs=[rows(IN_W), pl.BlockSpec((2, HG_W), lambda i: (0, 0)), pl.BlockSpec((1, HG_DK), lambda i: (0, 0))],
        out_shape=[_sds((S, IN_W), BF16), _sds((2, HG_W), F32), _sds((1, HG_DK), F32)],
        scratch_shapes=[pltpu.VMEM((HG_HEADS, HG_DK, HG_DK), F32)] + [pltpu.VMEM((HG_TILE, HG_W), F32)] * 4
        + [pltpu.VMEM((1, HG_W), F32)],
        compiler_params=_cp(1))(proj, proj, proj, proj, lb_raw, nw, o, states, dy, *d_attn, *d_gates)


GATE_COL0 = (QKV_W + 4 * HG_W) // GROUP_W
HALF_D = D_MODEL // 2


def _gate_tiles(proj):
    return [_tile(proj, HALF_D, functools.partial(lambda c, k: GATE_COL0 + k, k=k)) for k in range(4)]


def _gates(g_refs):
    s0 = _sigmoid(jnp.concatenate([g_refs[0][...], g_refs[1][...]], axis=1))
    s1 = _sigmoid(jnp.concatenate([g_refs[2][...], g_refs[3][...]], axis=1))
    return s0, s1


def _branch_fwd(os_, lses, yh, proj, w_a, w_h, name):
    nb = w_a.shape[0]

    def body(o0, o1, o2, l0, l1, l2, yh_ref, g0a, g0b, g1a, g1b, wa_ref, wh_ref,
             y_ref, lse_ref, za_ref, zh_ref, m_ref):
        a, b, c = l0[...], l1[...], l2[...]
        m = jnp.maximum(jnp.maximum(a, b), c)
        ea, eb, ec = jnp.exp(a - m), jnp.exp(b - m), jnp.exp(c - m)
        den = ea + eb + ec
        y = (ea * o0[...] + eb * o1[...] + ec * o2[...]) / den
        y_ref[...] = y
        lse_ref[...] = m + jnp.log(den)
        yb, yhb = _bf(y), _bf(yh_ref[...])
        za = jnp.concatenate([_dot(yb, wa_ref[j], NN) for j in range(nb)], axis=1)
        zh = jnp.concatenate([_dot(yhb, wh_ref[j], NN) for j in range(nb)], axis=1)
        s0, s1 = _gates((g0a, g0b, g1a, g1b))
        za_ref[...] = za
        zh_ref[...] = zh
        m_ref[...] = _bf(s0 * za + s1 * zh)

    return _rows_call(name, body, yh.shape[0], 512, 1,
                      [*[_tile(t, GROUP_W) for t in (*os_, *lses)], _tile(yh, HG_W), *_gate_tiles(proj),
                       _full(w_a), _full(w_h)],
                      [_out_tile(GROUP_W, F32, GROUP_W)] * 2 + [_out_tile(D_MODEL, F32, D_MODEL)] * 2
                      + [_out_tile(D_MODEL, BF16, D_MODEL)])


def _branch_bwd(dm, za, zh, proj, w_a, w_h, name, plans=None):
    nb, _, Nb = w_a.shape

    def body(dm_ref, za_ref, zh_ref, g0a, g0b, g1a, g1b, wa_ref, wh_ref,
             dza_ref, dzh_ref, dg0_ref, dg1_ref, dy_ref, dyh_ref):
        dmv = dm_ref[...]
        s0, s1 = _gates((g0a, g0b, g1a, g1b))
        dza, dzh = _bf(dmv * s0), _bf(dmv * s1)
        dza_ref[...] = dza
        dzh_ref[...] = dzh
        dg0_ref[...] = _bf(dmv * za_ref[...] * s0 * (1.0 - s0))
        dg1_ref[...] = _bf(dmv * zh_ref[...] * s1 * (1.0 - s1))
        dy_ref[...] = sum(_dot(dza[:, j * Nb:(j + 1) * Nb], wa_ref[j], NT) for j in range(nb))
        dyh_ref[...] = sum(_dot(dzh[:, j * Nb:(j + 1) * Nb], wh_ref[j], NT) for j in range(nb))

    return _rows_call(name, body, za.shape[0], 512, 1,
                      [_tile(dm, D_MODEL), _tile(za, D_MODEL), _tile(zh, D_MODEL), *_gate_tiles(proj),
                       _full(w_a), _full(w_h)],
                      [_out_tile(D_MODEL, BF16, D_MODEL)] * 4 + [_out_tile(GROUP_W, F32, GROUP_W),
                                                                 _out_tile(HG_W, F32, HG_W)], plans)


def _mix_out(merged, w_out, x, w_post, w_pre, name):
    def body(m_ref, wo_ref, x_ref, wp_ref, wf_ref, mo_ref, x1_ref, h2_ref):
        z = _dot(m_ref[...], wo_ref[...], NN)
        mo_ref[...] = z
        x1 = x_ref[...] + z * _rinv(z) * wp_ref[...]
        x1_ref[...] = x1
        h2_ref[...] = _bf(x1 * _rinv(x1) * wf_ref[...])

    return _rows_call(name, body, x.shape[0], 512, 1,
                      [_tile(merged, D_MODEL), _full(w_out), _tile(x, D_MODEL), _full(w_post), _full(w_pre)],
                      [_out_tile(D_MODEL, F32, D_MODEL), _out_tile(D_MODEL, F32, D_MODEL),
                       _out_tile(D_MODEL, BF16, D_MODEL)])


def _loss_head(a, w_down, x1, tgt, w, name):
    def body(a_ref, wd_ref, x1_ref, t_ref, w_ref, dx_ref, df_ref, dw_ref, loss_ref):
        z = _dot(a_ref[...], wd_ref[...], NN)
        r = _rinv(z)
        zhat = z * r
        wv = w_ref[...]
        e = x1_ref[...] + zhat * wv - t_ref[...]
        dx = e * (1.0 / D_MODEL)
        dx_ref[...] = dx
        df_ref[...] = _bf(_norm_bwd(dx, zhat, r, wv))
        _acc(dw_ref, jnp.sum(dx * zhat, axis=0, keepdims=True))
        part = 0.5 * jnp.sum(jnp.sum(e * e, axis=1, keepdims=True), axis=0, keepdims=True) * (1.0 / D_MODEL)
        _acc(loss_ref, jnp.broadcast_to(part, (1, LANES)))

    return _rows_call(name, body, x1.shape[0], 512, 1,
                      [_tile(a, D_FF), _full(w_down), _tile(x1, D_MODEL), _tile(tgt, D_MODEL), _full(w)],
                      [_out_tile(D_MODEL, F32, D_MODEL), _out_tile(D_MODEL, BF16, D_MODEL),
                       _out_acc(1, D_MODEL, D_MODEL), _out_acc(1, LANES, LANES)])


CONV_CB = D_FF // 2
CONV_TM = 512
HALO = 8
SQRT_HALF = 0.7071067811865476
INV_SQRT_2PI = 0.3989422804014327


CONV_RS = 32


def _lane_tiles():
    return [slice(k * LANES, (k + 1) * LANES) for k in range(CONV_CB // LANES)]


def _strip_start(i):
    return pl.multiple_of(i * CONV_RS, CONV_RS)


def _strip_taps(u_ref, halo_ref, r0, cs, first_strip, first_tile):
    if first_strip:
        before = jnp.where(first_tile, 0.0, halo_ref[:, cs])
        blk = jnp.concatenate([before, u_ref[0:CONV_RS, cs]], axis=0)
    else:
        blk = u_ref[pl.ds(pl.multiple_of(r0 - HALO, HALO), CONV_RS + HALO), cs]
    return pltpu.roll(blk, 2, 0)[HALO:], pltpu.roll(blk, 1, 0)[HALO:], blk[HALO:]


def _conv(taps, w_ref, b_ref, cs):
    return b_ref[:, cs] + w_ref[0:1, cs] * taps[0] + w_ref[1:2, cs] * taps[1] + w_ref[2:3, cs] * taps[2]


def _conv_specs(tm):
    nh = tm // HALO
    nc = D_FF // CONV_CB

    def tile(off):
        return pl.BlockSpec((tm, CONV_CB), lambda c, i: (i, off + c))

    def halo(off):
        return pl.BlockSpec((HALO, CONV_CB), lambda c, i: (jnp.maximum(i * nh - 1, 0), off + c))

    def small(rows, off):
        return pl.BlockSpec((rows, CONV_CB), lambda c, i: (0, off + c))

    return nc, tile, halo, small


def _conv_gelu_fwd(u, cw, cb, name, plans=None):
    S = u.shape[0]
    tm = CONV_TM
    nc, tile, halo, small = _conv_specs(tm)

    def body(ug, hg, uv, hv, wg, wv, bg, bv, a_ref):
        first_tile = pl.program_id(1) == 0

        def strip(r0, first_strip):
            for cs in _lane_tiles():
                cg = _conv(_strip_taps(ug, hg, r0, cs, first_strip, first_tile), wg, bg, cs)
                cv = _conv(_strip_taps(uv, hv, r0, cs, first_strip, first_tile), wv, bv, cs)
                a_ref[pl.ds(r0, CONV_RS), cs] = _bf(0.5 * cg * (1.0 + lax.erf(cg * SQRT_HALF)) * cv)

        strip(0, True)
        lax.fori_loop(1, tm // CONV_RS, lambda k, c: (strip(_strip_start(k), False), c)[1], 0)

    (a,), carried = _call(
        body, plans, name=name, grid=(nc, S // tm),
        in_specs=[tile(0), halo(0), tile(nc), halo(nc), small(3, 0), small(3, nc), small(1, 0), small(1, nc)],
        out_specs=[tile(0)], out_shape=[_sds((S, D_FF), BF16)], args=(u, u, u, u, cw, cw, cb, cb))
    return a if plans is None else (a, carried)


def _conv_gelu_bwd(u, dff, w_down, cw, cb, name, plans=None):
    S = u.shape[0]
    tm = CONV_TM
    nt = S // tm
    nc, tile, halo, small = _conv_specs(tm)

    def body(ug, hg, uv, hv, wg, wv, bg, bv, dff_ref, wd_ref, dcg_ref, dcv_ref, dwg_ref, dwv_ref, dbg_ref, dbv_ref,
             acc, da_ref):
        i = pl.program_id(1)
        first_tile = i == 0
        da_ref[...] = _dot(dff_ref[...], wd_ref[...], NT)

        @pl.when(first_tile)
        def _():
            acc[...] = jnp.zeros_like(acc)

        def strip(r0, first_strip):
            rows = pl.ds(r0, CONV_RS)
            for cs in _lane_tiles():
                tg = _strip_taps(ug, hg, r0, cs, first_strip, first_tile)
                tv = _strip_taps(uv, hv, r0, cs, first_strip, first_tile)
                cg = _conv(tg, wg, bg, cs)
                cv = _conv(tv, wv, bv, cs)
                phi = 0.5 * (1.0 + lax.erf(cg * SQRT_HALF))
                dav = da_ref[rows, cs]
                dcg = dav * cv * (phi + cg * jnp.exp(-0.5 * cg * cg) * INV_SQRT_2PI)
                dcv = dav * (cg * phi)
                dcg_ref[rows, cs] = dcg
                dcv_ref[rows, cs] = dcv
                for half, (dc, taps) in enumerate(((dcg, tg), (dcv, tv))):
                    for j in range(3):
                        acc[4 * half + j, :, cs] += dc * taps[j]
                    acc[4 * half + 3, :, cs] += dc

        strip(0, True)
        lax.fori_loop(1, tm // CONV_RS, lambda k, c: (strip(_strip_start(k), False), c)[1], 0)

        @pl.when(i == nt - 1)
        def _():
            for half, (dw_ref, db_ref) in enumerate(((dwg_ref, dbg_ref), (dwv_ref, dbv_ref))):
                for j in range(3):
                    dw_ref[j:j + 1, :] = jnp.sum(acc[4 * half + j], axis=0, keepdims=True)
                db_ref[...] = jnp.sum(acc[4 * half + 3], axis=0, keepdims=True)

    res, carried = _call(
        body, plans, name=name, grid=(nc, nt),
        in_specs=[tile(0), halo(0), tile(nc), halo(nc), small(3, 0), small(3, nc), small(1, 0), small(1, nc),
                  pl.BlockSpec((tm, D_MODEL), lambda c, i: (i, 0)), pl.BlockSpec((CONV_CB, D_MODEL), lambda c, i: (c, 0))],
        out_specs=[tile(0), tile(0), small(3, 0), small(3, 0), small(1, 0), small(1, 0)],
        out_shape=[_sds((S, D_FF), F32)] * 2 + [_sds((3, D_FF), F32)] * 2 + [_sds((1, D_FF), F32)] * 2,
        scratch_shapes=[pltpu.VMEM((8, CONV_RS, CONV_CB), F32), pltpu.VMEM((tm, CONV_CB), F32)],
        args=(u, u, u, u, cw, cw, cb, cb, dff, w_down))
    return res if plans is None else (res, carried)


def _conv_input_bwd(dcg, dcv, cw, name, plans=None):
    S = dcg.shape[0]
    tm = CONV_TM // 2
    nh = tm // HALO
    nt = S // tm
    n = CONV_RS + HALO
    tile = pl.BlockSpec((tm, D_FF), lambda i: (i, 0))
    nxt = pl.BlockSpec((HALO, D_FF), lambda i: (jnp.minimum((i + 1) * nh, S // HALO - 1), 0))

    def body(g_ref, ng_ref, v_ref, nv_ref, w_ref, du_ref):
        last_tile = pl.program_id(0) == nt - 1

        def strip(r0, last_strip):
            for half, (dc_ref, n_ref) in enumerate(((g_ref, ng_ref), (v_ref, nv_ref))):
                for k in range(D_FF // LANES):
                    cs = slice(k * LANES, (k + 1) * LANES)
                    ws = slice(half * D_FF + k * LANES, half * D_FF + (k + 1) * LANES)
                    if last_strip:
                        after = jnp.where(last_tile, 0.0, n_ref[:, cs])
                        blk = jnp.concatenate([dc_ref[tm - CONV_RS:tm, cs], after], axis=0)
                    else:
                        blk = dc_ref[pl.ds(r0, n), cs]
                    d1 = pltpu.roll(blk, n - 1, 0)[:CONV_RS]
                    d2 = pltpu.roll(blk, n - 2, 0)[:CONV_RS]
                    du_ref[pl.ds(r0, CONV_RS), ws] = _bf(w_ref[2:3, ws] * blk[:CONV_RS] + w_ref[1:2, ws] * d1
                                                         + w_ref[0:1, ws] * d2)

        lax.fori_loop(0, tm // CONV_RS - 1, lambda k, c: (strip(_strip_start(k), False), c)[1], 0)
        strip(tm - CONV_RS, True)

    (du,), carried = _call(
        body, plans, name=name, grid=(nt,),
        in_specs=[tile, nxt, tile, nxt, pl.BlockSpec((3, 2 * D_FF), lambda i: (0, 0))],
        out_specs=[pl.BlockSpec((tm, 2 * D_FF), lambda i: (i, 0))], out_shape=[_sds((S, 2 * D_FF), BF16)],
        args=(dcg, dcg, dcv, dcv, cw))
    return du if plans is None else (du, carried)


def _row_tile(n, cap):
    best = n
    for t in range(16, cap + 1, 16):
        if n % t == 0:
            best = t
    return best if best <= cap else n


def _rows_for_bytes(nbytes, cols):
    return max(16, nbytes // (4 * cols) // 16 * 16)


def _adamw(w, g, m, v, name):
    R, C = w.shape
    tr = _row_tile(R, _rows_for_bytes(2 << 20, C))

    def body(w_ref, g_ref, m_ref, v_ref, d_ref, nm_ref, nv_ref):
        gv = g_ref[...]
        nm = ADAM_B1 * m_ref[...] + (1.0 - ADAM_B1) * gv
        nv = ADAM_B2 * v_ref[...] + (1.0 - ADAM_B2) * (gv * gv)
        m_hat = nm / (1.0 - ADAM_B1 ** ADAM_STEP)
        v_hat = nv / (1.0 - ADAM_B2 ** ADAM_STEP)
        d_ref[...] = -ADAM_LR * (m_hat / (jnp.sqrt(v_hat) + ADAM_EPS) + ADAM_WD * w_ref[...])
        nm_ref[...] = nm
        nv_ref[...] = nv

    spec = pl.BlockSpec((tr, C), lambda i: (i, 0))
    return pl.pallas_call(body, name=name, grid=(R // tr,), in_specs=[spec] * 4, out_specs=[spec] * 3,
                          out_shape=[_sds((R, C), F32)] * 3, compiler_params=_cp(1))(w, g, m, v)


def _pair_sum(gfull, rcv, c_idx, name):
    nb, R, C = gfull.shape
    half = R // 2
    tr = _row_tile(half, _rows_for_bytes(2 << 20, C))
    nt = half // tr

    def body(c_ref, g_ref, r_ref, o_ref):
        o_ref[...] = _bf(g_ref[...] + r_ref[...])

    return pl.pallas_call(
        body, name=name,
        grid_spec=pltpu.PrefetchScalarGridSpec(
            num_scalar_prefetch=1, grid=(nb, nt),
            in_specs=[pl.BlockSpec((None, tr, C), lambda j, i, c_ref: (j, c_ref[0] * nt + i, 0)),
                      pl.BlockSpec((None, tr, C), lambda j, i, c_ref: (j, i, 0))],
            out_specs=pl.BlockSpec((None, tr, C), lambda j, i, c_ref: (j, i, 0))),
        out_shape=_sds((nb, half, C), BF16), compiler_params=_cp(2))(c_idx, gfull, rcv)


def _chip_sum(arrived, own, place, name):
    nb, H, C = arrived.shape
    tr = _row_tile(H, _rows_for_bytes(2 << 20, C))
    nt = H // tr

    def body(pl_ref, *refs):
        o_ref = refs[nb + 1]
        me = pl_ref[0]
        acc = None
        for k in range(nb):
            term = jnp.where(me == k, refs[nb][...], refs[k][...]).astype(F32)
            acc = term if acc is None else acc + term
        o_ref[...] = acc

    def other(k):
        return pl.BlockSpec((None, tr, C), lambda i, p: (jnp.where(p[0] == k, (k + 1) % nb, k), i, 0))

    return pl.pallas_call(
        body, name=name,
        grid_spec=pltpu.PrefetchScalarGridSpec(
            num_scalar_prefetch=1, grid=(nt,),
            in_specs=[other(k) for k in range(nb)] + [pl.BlockSpec((None, tr, C), lambda i, p: (p[0], i, 0))],
            out_specs=pl.BlockSpec((tr, C), lambda i, p: (p[1] * nt + i, 0))),
        out_shape=_sds((2 * H, C), F32), compiler_params=_cp(1))(place, *([arrived] * nb), own)


def _cast_into_slot(shard, place, name):
    R, C = shard.shape
    tr = _row_tile(R, 256)

    def body(pl_ref, s_ref, o_ref):
        o_ref[...] = _bf(s_ref[...])

    return pl.pallas_call(
        body, name=name,
        grid_spec=pltpu.PrefetchScalarGridSpec(
            num_scalar_prefetch=1, grid=(R // tr,),
            in_specs=[pl.BlockSpec((tr, C), lambda i, p: (i, 0))],
            out_specs=pl.BlockSpec((None, tr, C), lambda i, p: (p[0], i, 0))),
        out_shape=_sds((N_CHIPS, R, C), BF16), compiler_params=_cp(1))(place, shard)


def _place():
    x, y, c = lax.axis_index("x"), lax.axis_index("y"), lax.axis_index("c")
    chips = [(1 - x, y), (x, 1 - y), (1 - x, 1 - y)]
    return x, y, c, chips


def _chip_id(px, py):
    return 2 * px + py


def _remote(src, dst, send_sems, recv_sems, k, to):
    return pltpu.make_async_remote_copy(src_ref=src, dst_ref=dst, send_sem=send_sems.at[k], recv_sem=recv_sems.at[k],
                                        device_id=to, device_id_type=MESH)


def _proj_gathered(h, slot, place, name, tm=512, plan=None):
    M, K = h.shape
    nb, _, Nb = slot.shape
    half = K // 2
    nt = M // tm
    cx, cy = place[0] // 2, place[0] % 2
    order = jnp.stack([place[0], _chip_id(1 - cx, cy), _chip_id(cx, 1 - cy), _chip_id(1 - cx, 1 - cy)]).astype(jnp.int32)

    p_in = [] if plan is None else plan.ins + plan.inouts
    p_out = [] if plan is None else [_sds(a.shape, a.dtype) for a in plan.inouts] + plan.outs
    n_pi, n_po = len(p_in), len(p_out)

    def body(order_ref, h_ref, slot_in, *refs):
        o_ref, slot_ref = refs[n_pi:n_pi + 2]
        w_buf, ici_send, ici_recv, pass_send, pass_recv, load_sem = refs[n_pi + 2 + n_po:n_pi + 2 + n_po + 6]

        def carried():
            if plan is None:
                return [], [], []
            ins = refs[:len(plan.ins)]
            outs = refs[n_pi + 2:n_pi + 2 + n_po]
            return plan.copies(ins, outs[:len(plan.inouts)], outs[len(plan.inouts):], *refs[n_pi + 2 + n_po + 6:])

        b, i = pl.program_id(0), pl.program_id(1)
        x, y, c, chips = _place()
        me = _chip_id(x, y)
        sib = (x, y, 1 - c)
        mine, other = pl.ds(c * half, half), pl.ds((1 - c) * half, half)

        def sent(k):
            blk = slot_ref.at[me, mine]
            return _remote(blk, blk, ici_send, ici_recv, k, (*chips[k], c))

        def landed(k):
            blk = slot_ref.at[_chip_id(*chips[k]), mine]
            return _remote(blk, blk, ici_send, ici_recv, k, (*chips[k], c))

        def passed(k, rows):
            blk = slot_ref.at[_chip_id(*chips[k]), rows]
            return _remote(blk, blk, pass_send, pass_recv, k, sib)

        @pl.when((b == 0) & (i == 0))
        def _():
            for k in range(len(chips)):
                sent(k).start()
            sends, _, local = carried()
            for cp in (*sends, *local):
                cp.start()

        for k in range(len(chips)):
            @pl.when((b == k + 1) & (i == 0))
            def _(k=k):
                landed(k).wait_recv()
                passed(k, mine).start()
                passed(k, other).wait_recv()

        @pl.when(i == 0)
        def _():
            load = pltpu.make_async_copy(slot_ref.at[order_ref[b]], w_buf, load_sem.at[0])
            load.start()
            load.wait()

        o_ref[...] = _dot(h_ref[...], w_buf[...], NN)

        @pl.when((b == nb - 1) & (i == nt - 1))
        def _():
            for k in range(len(chips)):
                sent(k).wait_send()
                passed(k, mine).wait_send()
            sends, recvs, local = carried()
            for cp in recvs:
                cp.wait_recv()
            for cp in sends:
                cp.wait_send()
            for cp in local:
                cp.wait()

    n_peers = N_CHIPS - 1
    return pl.pallas_call(
        body, name=name,
        grid_spec=pltpu.PrefetchScalarGridSpec(
            num_scalar_prefetch=1, grid=(nb, nt),
            in_specs=[pl.BlockSpec((tm, K), lambda b, i, o: (i, 0)), ANY] + [ANY] * n_pi,
            out_specs=[pl.BlockSpec((tm, Nb), lambda b, i, o: (i, o[b])), ANY] + [ANY] * n_po,
            scratch_shapes=[pltpu.VMEM((K, Nb), BF16)] + [pltpu.SemaphoreType.DMA((n_peers,))] * 4
            + [pltpu.SemaphoreType.DMA((1,))]
            + ([] if plan is None else [pltpu.SemaphoreType.DMA((plan.n_sems,))] * 3)),
        out_shape=[_sds((M, nb * Nb), F32), _sds(slot.shape, slot.dtype)] + p_out,
        input_output_aliases={2: 1, **({} if plan is None else
                                       {3 + len(plan.ins) + a: 2 + a for a in range(len(plan.inouts))})},
        compiler_params=_cp(2))(order, h, slot, *p_in)


def _gather_ici_plan(slots, wholes):
    ns, nw = len(slots), len(wholes)

    def copies(ins, ios, outs, send_sems, recv_sems, local_sems):
        x, y, c, chips = _place()
        me = _chip_id(x, y)
        sends, recvs = [], []
        for a in range(ns + nw):
            dst = ios[a] if a < ns else outs[a - ns]
            R = dst.shape[1]
            rows = pl.ds(c * (R // 2), R // 2) if a < ns else pl.ds(0, R)
            src = dst.at[me, rows] if a < ns else ins[a - ns]
            for j, chip in enumerate(chips):
                sends.append(_remote(src, dst.at[me, rows], send_sems, recv_sems, 3 * a + j, (*chip, c)))
                landed = dst.at[_chip_id(*chip), rows]
                recvs.append(_remote(landed, landed, send_sems, recv_sems, 3 * a + j, (*chip, c)))
        local = [pltpu.make_async_copy(ins[b], outs[b].at[me], local_sems.at[b]) for b in range(nw)]
        return sends, recvs, local

    return _Plan(copies, 3 * (ns + nw), ins=wholes, inouts=slots,
                 outs=[_sds((N_CHIPS, *s.shape), s.dtype) for s in wholes])


def _gather_pass_plan(slots):
    def copies(ins, ios, outs, send_sems, recv_sems, local_sems):
        x, y, c, chips = _place()
        sib = (x, y, 1 - c)
        sends, recvs = [], []
        for a, buf in enumerate(ios):
            half = buf.shape[1] // 2
            for j, chip in enumerate(chips):
                mine = buf.at[_chip_id(*chip), pl.ds(c * half, half)]
                other = buf.at[_chip_id(*chip), pl.ds((1 - c) * half, half)]
                sends.append(_remote(mine, mine, send_sems, recv_sems, 3 * a + j, sib))
                recvs.append(_remote(other, other, send_sems, recv_sems, 3 * a + j, sib))
        return sends, recvs, []

    return _Plan(copies, 3 * len(slots), inouts=slots)


def _pair_plan(grads):
    def copies(ins, ios, outs, send_sems, recv_sems, local_sems):
        x, y, c, _ = _place()
        sib = (x, y, 1 - c)
        sends, recvs = [], []
        for a, g in enumerate(ins):
            half = g.shape[1] // 2
            sends.append(_remote(g.at[:, pl.ds((1 - c) * half, half), :], outs[a], send_sems, recv_sems, a, sib))
            recvs.append(_remote(outs[a], outs[a], send_sems, recv_sems, a, sib))
        return sends, recvs, []

    return _Plan(copies, len(grads), ins=grads,
                 outs=[_sds((g.shape[0], g.shape[1] // 2, g.shape[2]), g.dtype) for g in grads])


def _chip_plan(parts):
    def copies(ins, ios, outs, send_sems, recv_sems, local_sems):
        x, y, c, chips = _place()
        me = _chip_id(x, y)
        sends, recvs = [], []
        for a, part in enumerate(ins):
            for j, chip in enumerate(chips):
                sends.append(_remote(part.at[_chip_id(*chip)], outs[a].at[me], send_sems, recv_sems, 3 * a + j, (*chip, c)))
                landed = outs[a].at[_chip_id(*chip)]
                recvs.append(_remote(landed, landed, send_sems, recv_sems, 3 * a + j, (*chip, c)))
        return sends, recvs, []

    return _Plan(copies, 3 * len(parts), ins=parts, outs=[_sds(p.shape, p.dtype) for p in parts])


def _all_sum(pack, fulls, name):
    R, C = pack.shape
    n = len(fulls)

    def body(p_ref, *refs):
        o_ref, halves = refs[n], refs[n + 1:2 * n + 1]
        buf, send_sems, recv_sems, pair_send, pair_recv = refs[2 * n + 1:]
        x, y, c, _ = _place()
        sib = (x, y, 1 - c)
        pair = []
        for a, full in enumerate(halves):
            H = full.shape[0] // 2
            mine = full.at[pl.ds(c * H, H)]
            cp = _remote(mine, mine, pair_send, pair_recv, a, sib)
            cp.start()
            pair.append(cp)
        me = 4 * x + 2 * y + c
        buf[me] = p_ref[...]
        cps = []
        for k in range(1, N_DEV):
            to = (x ^ (k >> 2), y ^ ((k >> 1) & 1), c ^ (k & 1))
            cp = _remote(p_ref, buf.at[me], send_sems, recv_sems, k - 1, to)
            cp.start()
            cps.append(cp)
        for k in range(1, N_DEV):
            frm = (x ^ (k >> 2), y ^ ((k >> 1) & 1), c ^ (k & 1))
            slot = buf.at[4 * frm[0] + 2 * frm[1] + frm[2]]
            _remote(slot, slot, send_sems, recv_sems, k - 1, frm).wait_recv()
        acc = buf[0]
        for k in range(1, N_DEV):
            acc = acc + buf[k]
        o_ref[...] = acc
        for cp in cps:
            cp.wait_send()
        for a, (full, cp) in enumerate(zip(halves, pair)):
            H = full.shape[0] // 2
            other = full.at[pl.ds((1 - c) * H, H)]
            _remote(other, other, pair_send, pair_recv, a, sib).wait_recv()
            cp.wait_send()

    vm = pl.BlockSpec(memory_space=pltpu.VMEM)
    res = pl.pallas_call(
        body, name=name, in_specs=[vm] + [ANY] * n, out_specs=[vm] + [ANY] * n,
        out_shape=[_sds((R, C), F32)] + [_sds(f.shape, f.dtype) for f in fulls],
        input_output_aliases={1 + a: 1 + a for a in range(n)},
        scratch_shapes=[pltpu.VMEM((N_DEV, R, C), F32), pltpu.SemaphoreType.DMA((N_DEV - 1,)),
                        pltpu.SemaphoreType.DMA((N_DEV - 1,)), pltpu.SemaphoreType.DMA((n,)),
                        pltpu.SemaphoreType.DMA((n,))])(pack, *fulls)
    return res[0], list(res[1:])


def _local_step(xs, tgt, p, ex):
    h1 = _norm_fwd(xs, p["pre_mix_norm"], "pre_mix_norm")
    proj = ex.project(h1)
    biases = _relbias_fwd(p["rel_bias"], "rel_bias_fwd")
    fw = []
    for g in range(N_GROUPS):
        res, got = _attn_fwd(proj, biases[g], g, f"attn_fwd{g}", plans=ex.carry(f"attn_fwd{g}"))
        ex.done(f"attn_fwd{g}", got)
        fw.append(res)
    (yh, o_h, states), got = _hgrn_fwd(proj, p["hgrn_lb_raw"], p["hgrn_norm"], "hgrn_fwd", plans=ex.carry("hgrn_fwd"))
    ex.done("hgrn_fwd", got)
    W_a, W_h, W_out = ex.weight("w_branch_attn"), ex.weight("w_branch_hgrn"), ex.weight("w_out")
    W_up, conv_w = ex.weight("w_up"), ex.weight("conv_w")
    y, lse, za, zh, merged = _branch_fwd([t[0] for t in fw], [t[1] for t in fw], yh, proj, W_a, W_h, "branch_fwd")
    mo, x1, h2 = _mix_out(merged, W_out, xs, p["post_mix_norm"], p["pre_ffn_norm"], "mix_out")
    u, got = _mm_nn_blk(h2, W_up, "ffn_up", tm=1024, plans=ex.carry("ffn_up"))
    ex.done("ffn_up", got)
    a, got = _conv_gelu_fwd(u, conv_w, p["conv_b"], "conv_gelu_fwd", plans=ex.carry("conv_gelu_fwd"))
    ex.done("conv_gelu_fwd", got)
    W_down = ex.weight("w_down")
    dx2, dff, g_post_ffn, loss = _loss_head(a, W_down, x1, tgt, p["post_ffn_norm"], "ffn_down_loss")

    ex.grad("w_down", _mm_tn(a, dff, "g_w_down").reshape(N_CHIPS, D_FF // N_CHIPS, D_MODEL))
    (dcg, dcv, gwg, gwv, gbg, gbv), got = _conv_gelu_bwd(u, dff, W_down, conv_w, p["conv_b"], "conv_gelu_bwd",
                                                          plans=ex.carry("conv_gelu_bwd"))
    ex.done("conv_gelu_bwd", got)
    g_conv_w = jnp.concatenate([gwg, gwv], axis=1)
    g_conv_b = jnp.concatenate([gbg, gbv], axis=1)
    du, got = _conv_input_bwd(dcg, dcv, conv_w, "conv_input_bwd", plans=ex.carry("conv_input_bwd"))
    ex.done("conv_input_bwd", got)
    dh2 = _mm_nt_blk(du, W_up, "d_ffn_in")
    ex.grad("w_up", _mm_tn_blk(h2, du, N_CHIPS, "g_w_up"))
    dx1, g_pre_ffn = _prenorm_bwd(dh2, x1, p["pre_ffn_norm"], dx2, "pre_ffn_norm_bwd")
    (dmo, dmerged, g_post_mix), got = _postnorm_bwd(dx1, mo, p["post_mix_norm"], W_out, "post_mix_norm_bwd",
                                                    plans=ex.carry("post_mix_norm_bwd"))
    ex.done("post_mix_norm_bwd", got)
    ex.grad("w_out", _mm_tn(merged, dmo, "g_w_out").reshape(N_CHIPS, D_MODEL // N_CHIPS, D_MODEL))
    (dza, dzh, dg0, dg1, dy, dyh), got = _branch_bwd(dmerged, za, zh, proj, W_a, W_h, "branch_bwd",
                                                     plans=ex.carry("branch_bwd"))
    ex.done("branch_bwd", got)
    ex.grad("w_branch_attn", _mm_tn_blk(y, dza, N_CHIPS, "g_w_branch_attn", together=True))
    ex.grad("w_branch_hgrn", _mm_tn_blk(yh, dzh, N_CHIPS, "g_w_branch_hgrn", together=True))
    dqkv, dbs = [], []
    for g in range(N_GROUPS):
        parts, db, got = _attn_bwd(proj, biases[g], lse, y, dy, g, f"attn_bwd{g}", plans=ex.carry(f"attn_bwd{g}"))
        ex.done(f"attn_bwd{g}", got)
        dqkv += parts
        dbs.append(db)
    g_rel_bias = _relbias_bwd(dbs, "rel_bias_bwd")
    dproj, g_lb_raw, g_hgrn_norm = _hgrn_bwd(proj, p["hgrn_lb_raw"], p["hgrn_norm"], o_h, states, dyh, dqkv,
                                             [dg0, dg1], "hgrn_bwd")
    for piece in W_IN_PIECES:
        g, got = _mm_tn_blk(h1, dproj, N_CHIPS, f"g_{piece}", x_cols=W_IN_ROWS[piece],
                            plans=ex.carry(f"g_{piece}"))
        ex.done(f"g_{piece}", got)
        ex.grad(piece, g)
    dh1, got = _mm_nt_blk(dproj, ex.weight("w_in"), "d_proj_in", plans=ex.carry("d_proj_in"))
    ex.done("d_proj_in", got)
    (grad_x, g_pre_mix), got = _prenorm_bwd(dh1, xs, p["pre_mix_norm"], dx1, "pre_mix_norm_bwd",
                                            plans=ex.carry("pre_mix_norm_bwd"))
    ex.done("pre_mix_norm_bwd", got)
    small = dict(pre_mix_norm=g_pre_mix, rel_bias=g_rel_bias, hgrn_lb_raw=g_lb_raw, hgrn_norm=g_hgrn_norm,
                 post_mix_norm=g_post_mix, pre_ffn_norm=g_pre_ffn, conv_w=g_conv_w, conv_b=g_conv_b,
                 post_ffn_norm=g_post_ffn)
    return loss, grad_x, small


SMALL = ("pre_mix_norm", "rel_bias", "hgrn_lb_raw", "hgrn_norm", "post_mix_norm", "pre_ffn_norm", "conv_w", "conv_b",
         "post_ffn_norm")
BIG = ("w_in", "w_up", "w_down", "w_out", "w_branch_attn", "w_branch_hgrn")
WEIGHTS = ("pre_mix_norm", "w_in", "rel_bias", "hgrn_lb_raw", "hgrn_norm", "w_branch_attn", "w_branch_hgrn", "w_out",
           "post_mix_norm", "pre_ffn_norm", "w_up", "conv_w", "conv_b", "w_down", "post_ffn_norm")
MIXER = ("w_out", "w_branch_attn", "w_branch_hgrn")

SCHEDULE = {
    "proj_in": [("gather_ici_cw", MIXER)],
    "attn_fwd0": [("gather_pass", MIXER), ("gather_ici", ("w_up",))],
    "attn_fwd1": [("gather_pass", ("w_up",))],
    "ffn_up": [("gather_ici", ("w_down",))],
    "conv_gelu_fwd": [("gather_pass", ("w_down",))],
    "conv_gelu_bwd": [("pair", ("w_down",))],
    "conv_input_bwd": [("chip", ("w_down",))],
    "post_mix_norm_bwd": [("pair", ("w_up",))],
    "attn_bwd0": [("chip", ("w_up",)), ("pair", MIXER)],
    "attn_bwd1": [("chip", MIXER)],
    "g_w_in_b": [("pair", ("w_in_a",))],
    "d_proj_in": [("chip", ("w_in_a",)), ("pair", ("w_in_b",))],
    "pre_mix_norm_bwd": [("chip", ("w_in_b",))],
}
W_IN_ROWS = dict(w_in_a=(0, 768), w_in_b=(3, 256))
W_IN_PIECES = tuple(W_IN_ROWS)
REDUCED = W_IN_PIECES + BIG[1:]


class _Exchange:
    def __init__(self, place, slots, conv_w_shard):
        self.place, self.slots, self.conv_w_shard = place, dict(slots), conv_w_shard
        self.conv_w = None
        self.g, self.from_sibling, self.pair_sums, self.arrived = {}, {}, {}, {}
        self.pending = []

    def weight(self, name):
        if name == "conv_w":
            return self.conv_w
        w = self.slots[name]
        return w.reshape(-1, D_MODEL) if name in ("w_out", "w_down") else w

    def project(self, h):
        (plan,) = self.carry("proj_in")
        proj, self.slots["w_in"], *got = _proj_gathered(h, self.slots["w_in"], self.place, "proj_in", plan=plan)
        self.done("proj_in", [got])
        return proj

    def grad(self, name, g):
        self.g[name] = g

    def carry(self, point):
        plans = []
        self.pending = SCHEDULE.get(point, [])
        for kind, names in self.pending:
            if kind in ("gather_ici", "gather_ici_cw"):
                wholes = [self.conv_w_shard] if kind == "gather_ici_cw" else []
                plans.append(_gather_ici_plan([self.slots[n] for n in names], wholes))
            elif kind == "gather_pass":
                plans.append(_gather_pass_plan([self.slots[n] for n in names]))
            elif kind == "pair":
                plans.append(_pair_plan([self.g[n] for n in names]))
            else:
                for n in names:
                    self.pair_sums[n] = _pair_sum(self.g[n], self.from_sibling[n], self.place[1:2], f"pair_sum_{n}")
                plans.append(_chip_plan([self.pair_sums[n] for n in names]))
        return plans

    def done(self, point, carried):
        for (kind, names), got in zip(self.pending, carried):
            if kind in ("gather_ici", "gather_ici_cw", "gather_pass"):
                self.slots.update(zip(names, got))
                if kind == "gather_ici_cw":
                    self.conv_w = got[len(names)].transpose(1, 0, 2).reshape(3, 2 * D_FF)
            elif kind == "pair":
                self.from_sibling.update(zip(names, got))
            else:
                self.arrived.update(zip(names, got))

    def reduced_halves(self):
        return [_chip_sum(self.arrived[n], self.pair_sums[n], self.place, f"chip_sum_{n}") for n in REDUCED]


def kernel(x, pre_mix_norm, w_in, rel_bias, hgrn_lb_raw, hgrn_norm, w_branch_attn, w_branch_hgrn, w_out, post_mix_norm, pre_ffn_norm, w_up, conv_w, conv_b, w_down, post_ffn_norm, loss_target, m_pre_mix_norm, m_w_in, m_rel_bias, m_hgrn_lb_raw, m_hgrn_norm, m_w_branch_attn, m_w_branch_hgrn, m_w_out, m_post_mix_norm, m_pre_ffn_norm, m_w_up, m_conv_w, m_conv_b, m_w_down, m_post_ffn_norm, v_pre_mix_norm, v_w_in, v_rel_bias, v_hgrn_lb_raw, v_hgrn_norm, v_w_branch_attn, v_w_branch_hgrn, v_w_out, v_post_mix_norm, v_pre_ffn_norm, v_w_up, v_conv_w, v_conv_b, v_w_down, v_post_ffn_norm):
    w = dict(pre_mix_norm=pre_mix_norm, w_in=w_in, rel_bias=rel_bias, hgrn_lb_raw=hgrn_lb_raw, hgrn_norm=hgrn_norm,
             w_branch_attn=w_branch_attn, w_branch_hgrn=w_branch_hgrn, w_out=w_out, post_mix_norm=post_mix_norm,
             pre_ffn_norm=pre_ffn_norm, w_up=w_up, conv_w=conv_w, conv_b=conv_b, w_down=w_down,
             post_ffn_norm=post_ffn_norm)
    m = dict(pre_mix_norm=m_pre_mix_norm, w_in=m_w_in, rel_bias=m_rel_bias, hgrn_lb_raw=m_hgrn_lb_raw,
             hgrn_norm=m_hgrn_norm, w_branch_attn=m_w_branch_attn, w_branch_hgrn=m_w_branch_hgrn, w_out=m_w_out,
             post_mix_norm=m_post_mix_norm, pre_ffn_norm=m_pre_ffn_norm, w_up=m_w_up, conv_w=m_conv_w,
             conv_b=m_conv_b, w_down=m_w_down, post_ffn_norm=m_post_ffn_norm)
    v = dict(pre_mix_norm=v_pre_mix_norm, w_in=v_w_in, rel_bias=v_rel_bias, hgrn_lb_raw=v_hgrn_lb_raw,
             hgrn_norm=v_hgrn_norm, w_branch_attn=v_w_branch_attn, w_branch_hgrn=v_w_branch_hgrn, w_out=v_w_out,
             post_mix_norm=v_post_mix_norm, pre_ffn_norm=v_pre_ffn_norm, w_up=v_w_up, conv_w=v_conv_w,
             conv_b=v_conv_b, w_down=v_w_down, post_ffn_norm=v_post_ffn_norm)
    shard2d = {n: (w[n][0] if w[n].ndim == 3 else w[n]) for n in WEIGHTS}
    chip = 2 * lax.axis_index("x") + lax.axis_index("y")
    core = lax.axis_index("c")

    place = jnp.stack([chip, core]).astype(jnp.int32)
    slots = {n: _cast_into_slot(shard2d[n], place, f"cast_{n}") for n in BIG}
    ex = _Exchange(place, slots, shard2d["conv_w"])
    loss, grad_x, small = _local_step(x[0], loss_target[0], {n: w[n] for n in SMALL if n != "conv_w"}, ex)

    flat = [small[n].reshape(-1) for n in SMALL] + [loss.reshape(-1)]
    sizes = [t.shape[0] for t in flat]
    summed, wholes = _all_sum(jnp.concatenate(flat).reshape(-1, LANES), ex.reduced_halves(), "sum_small")
    summed = summed.reshape(-1)
    offs = [sum(sizes[:i]) for i in range(len(sizes))]
    grads = {}
    for n, o, sz in zip(SMALL, offs, sizes):
        grads[n] = summed[o:o + sz].reshape(small[n].shape)
    loss_total = summed[offs[-1]]
    cw = 2 * D_FF // N_CHIPS
    grads["conv_w"] = lax.dynamic_slice(grads["conv_w"], (0, chip * cw), (3, cw))

    big = dict(zip(REDUCED, wholes))
    big["w_in"] = jnp.concatenate([big.pop(n) for n in W_IN_PIECES], axis=0)
    grads.update(big)

    out_g, out_d, out_m, out_v = [], [], [], []
    for n in WEIGHTS:
        d2, m2, v2 = _adamw(shard2d[n], grads[n], m[n].reshape(shard2d[n].shape), v[n].reshape(shard2d[n].shape),
                            f"adamw_{n}")
        shape = w[n].shape
        out_g.append(grads[n].reshape(shape))
        out_d.append(d2.reshape(shape))
        out_m.append(m2.reshape(shape))
        out_v.append(v2.reshape(shape))
    return (loss_total, grad_x[None], *out_g, *out_d, *out_m, *out_v)
```

```python
import functools
import math

import jax
import jax.numpy as jnp
from jax import lax
from jax.experimental import pallas as pl
from jax.experimental.pallas import tpu as pltpu

F32 = jnp.float32
BF16 = jnp.bfloat16
MESH = pl.DeviceIdType.MESH

D_MODEL = 1024
N_GROUPS = 3
DILATIONS = (1, 4, 16)
HEADS = 8
HEAD_DIM = 64
GROUP_W = HEADS * HEAD_DIM
QKV_W = N_GROUPS * 3 * GROUP_W
BLK = 128
NEG_INF = -1e30
NUM_BUCKETS = 32
MAX_EXACT = 16
MAX_DISTANCE = 2048
HG_HEADS = 4
HG_DK = 128
HG_W = HG_HEADS * HG_DK
HG_CHUNK = 32
HG_TILE = 256
IN_W = QKV_W + 4 * HG_W + 2 * D_MODEL
D_FF = 2816
EPS = 1e-6
N_CHIPS = 4
N_DEV = 8
LANES = 128

ADAM_LR, ADAM_B1, ADAM_B2, ADAM_EPS, ADAM_WD, ADAM_STEP = 0.001, 0.9, 0.999, 1e-08, 0.01, 10

VMEM_LIMIT = 56 * 1024 * 1024


def _cp(n_axes):
    return pltpu.CompilerParams(dimension_semantics=("arbitrary",) * n_axes, vmem_limit_bytes=VMEM_LIMIT)


def _sds(shape, dtype):
    return jax.ShapeDtypeStruct(tuple(shape), dtype)


def _sigmoid(v):
    return 1.0 / (1.0 + jnp.exp(-v))


def _bf(v):
    return v.astype(BF16)


def _dot(a, b, dims):
    return lax.dot_general(a, b, (dims, ((), ())), preferred_element_type=F32)


NN = ((1,), (0,))
NT = ((1,), (1,))
TN = ((0,), (0,))

ANY = pl.BlockSpec(memory_space=pl.ANY)


class _Plan:
    def __init__(self, copies, n_sems, ins=(), inouts=(), outs=()):
        self.copies, self.n_sems = copies, n_sems
        self.ins, self.inouts, self.outs = list(ins), list(inouts), list(outs)


def _call(body, plans=None, *, name, grid, in_specs, out_specs, out_shape, args, scratch_shapes=()):
    plans = list(plans or ())
    in_specs, out_specs, out_shape = list(in_specs), list(out_specs), list(out_shape)
    scratch_shapes = list(scratch_shapes)
    n_in, n_out, n_scr = len(in_specs), len(out_specs), len(scratch_shapes)
    x_in, x_out, aliases, spans = [], [], {}, []
    for p in plans:
        i0, o0 = len(x_in), len(x_out)
        x_in += p.ins
        for a in p.inouts:
            aliases[n_in + len(x_in)] = n_out + len(x_out)
            x_in.append(a)
            x_out.append(_sds(a.shape, a.dtype))
        x_out += p.outs
        spans.append((i0, len(p.ins), o0, len(p.inouts), len(p.outs)))
    sems = [pltpu.SemaphoreType.DMA((p.n_sems,)) for p in plans for _ in range(3)]

    def wrapped(*refs):
        xi = refs[n_in:n_in + len(x_in)]
        base = n_in + len(x_in)
        xo = refs[base + n_out:base + n_out + len(x_out)]
        sbase = base + n_out + len(x_out)
        xs = refs[sbase + n_scr:]
        ids = [pl.program_id(k) for k in range(len(grid))]
        first = functools.reduce(jnp.logical_and, [i == 0 for i in ids])
        last = functools.reduce(jnp.logical_and, [i == g - 1 for i, g in zip(ids, grid)])

        def descriptors(k):
            i0, ni, o0, nio, no = spans[k]
            return plans[k].copies(xi[i0:i0 + ni], xo[o0:o0 + nio], xo[o0 + nio:o0 + nio + no], *xs[3 * k:3 * k + 3])

        @pl.when(first)
        def _():
            for k in range(len(plans)):
                sends, _, local = descriptors(k)
                for cp in (*sends, *local):
                    cp.start()

        body(*refs[:n_in], *refs[base:base + n_out], *refs[sbase:sbase + n_scr])

        @pl.when(last)
        def _():
            for k in range(len(plans)):
                sends, recvs, local = descriptors(k)
                for cp in recvs:
                    cp.wait_recv()
                for cp in sends:
                    cp.wait_send()
                for cp in local:
                    cp.wait()

    res = pl.pallas_call(
        wrapped if plans else body, name=name, grid=grid, in_specs=in_specs + [ANY] * len(x_in),
        out_specs=out_specs + [ANY] * len(x_out), out_shape=out_shape + x_out, input_output_aliases=aliases,
        scratch_shapes=scratch_shapes + sems, compiler_params=_cp(len(grid)))(*args, *x_in)
    res = list(res)
    carried = [res[n_out + o0:n_out + o0 + nio + no] for (_, _, o0, nio, no) in spans]
    return res[:n_out], carried


def _mm_nn_blk(a, wg, name, tm=512, plans=None):
    M, K = a.shape
    nb, _, Nb = wg.shape

    def body(a_ref, w_ref, o_ref):
        o_ref[...] = _dot(_bf(a_ref[...]), w_ref[...], NN)

    (out,), carried = _call(
        body, plans, name=name, grid=(nb, M // tm),
        in_specs=[pl.BlockSpec((tm, K), lambda j, i: (i, 0)), pl.BlockSpec((None, K, Nb), lambda j, i: (j, 0, 0))],
        out_specs=[pl.BlockSpec((tm, Nb), lambda j, i: (i, j))],
        out_shape=[_sds((M, nb * Nb), F32)], args=(a, wg))
    return out if plans is None else (out, carried)


def _mm_nt_blk(dy, wg, name, tm=1024, plans=None):
    M = dy.shape[0]
    nb, K, Nb = wg.shape

    def body(dy_ref, w_ref, o_ref):
        j = pl.program_id(1)
        r = _dot(_bf(dy_ref[...]), w_ref[...], NT)

        @pl.when(j == 0)
        def _():
            o_ref[...] = r

        @pl.when(j > 0)
        def _():
            o_ref[...] += r

    (out,), carried = _call(
        body, plans, name=name, grid=(M // tm, nb),
        in_specs=[pl.BlockSpec((tm, Nb), lambda i, j: (i, j)), pl.BlockSpec((None, K, Nb), lambda i, j: (j, 0, 0))],
        out_specs=[pl.BlockSpec((tm, K), lambda i, j: (i, 0))],
        out_shape=[_sds((M, K), F32)], args=(dy, wg))
    return out if plans is None else (out, carried)


def _mm_tn_blk(x, dy, nb, name, tk=2048, x_cols=None, plans=None, together=False):
    T, Mx = x.shape
    xk, Mx = (0, Mx) if x_cols is None else x_cols
    Nb = dy.shape[1] // nb
    nj = nb if together else 1

    def body(x_ref, dy_ref, o_ref):
        t = pl.program_id(1)
        r = _dot(_bf(x_ref[...]), _bf(dy_ref[...]), TN)
        for j in range(nj):
            rj = r[:, j * Nb:(j + 1) * Nb]

            @pl.when(t == 0)
            def _():
                o_ref[j] = rj

            @pl.when(t > 0)
            def _():
                o_ref[j] += rj

    (out,), carried = _call(
        body, plans, name=name, grid=(nb // nj, T // tk),
        in_specs=[pl.BlockSpec((tk, Mx), lambda j, t: (t, xk)), pl.BlockSpec((tk, nj * Nb), lambda j, t: (t, j))],
        out_specs=[pl.BlockSpec((nj, Mx, Nb), lambda j, t: (j, 0, 0))],
        out_shape=[_sds((nb, Mx, Nb), F32)], args=(x, dy))
    return out if plans is None else (out, carried)


def _mm_tn(x, dy, name, tk=1024):
    T, Mx = x.shape
    N = dy.shape[1]

    def body(x_ref, dy_ref, o_ref):
        t = pl.program_id(0)
        r = _dot(_bf(x_ref[...]), _bf(dy_ref[...]), TN)

        @pl.when(t == 0)
        def _():
            o_ref[...] = r

        @pl.when(t > 0)
        def _():
            o_ref[...] += r

    return pl.pallas_call(
        body, name=name, grid=(T // tk,),
        in_specs=[pl.BlockSpec((tk, Mx), lambda t: (t, 0)), pl.BlockSpec((tk, N), lambda t: (t, 0))],
        out_specs=pl.BlockSpec((Mx, N), lambda t: (0, 0)),
        out_shape=_sds((Mx, N), F32), compiler_params=_cp(1))(x, dy)


def _tile(arr, bw, col=lambda c: 0):
    return ("tile", arr, bw, col)


def _full(arr):
    return ("full", arr)


def _out_tile(width, dtype, bw, col=lambda c: 0):
    return ("tile", width, dtype, bw, col)


def _out_acc(rows, width, bw, col=lambda c: 0):
    return ("acc", rows, width, bw, col)


def _rows_call(name, body, n_rows, tm, ncol, ins, outs, plans=None):
    in_specs, args = [], []
    for e in ins:
        if e[0] == "tile":
            _, arr, bw, col = e
            in_specs.append(pl.BlockSpec((tm, bw), functools.partial(lambda c, i, col: (i, col(c)), col=col)))
        else:
            arr = e[1]
            in_specs.append(pl.BlockSpec(arr.shape, functools.partial(lambda c, i, nd: (0,) * nd, nd=arr.ndim)))
        args.append(arr)
    out_specs, out_shape = [], []
    for e in outs:
        if e[0] == "tile":
            _, width, dtype, bw, col = e
            out_specs.append(pl.BlockSpec((tm, bw), functools.partial(lambda c, i, col: (i, col(c)), col=col)))
            out_shape.append(_sds((n_rows, width), dtype))
        else:
            _, rows, width, bw, col = e
            out_specs.append(pl.BlockSpec((rows, bw), functools.partial(lambda c, i, col: (0, col(c)), col=col)))
            out_shape.append(_sds((rows, width), F32))
    out, carried = _call(body, plans, name=name, grid=(ncol, n_rows // tm), in_specs=in_specs, out_specs=out_specs,
                         out_shape=out_shape, args=args)
    return out if plans is None else (out, carried)


def _acc(ref, val):
    i = pl.program_id(1)

    @pl.when(i == 0)
    def _():
        ref[...] = val

    @pl.when(i > 0)
    def _():
        ref[...] += val


def _rinv(z):
    return lax.rsqrt(jnp.mean(z * z, axis=-1, keepdims=True) + EPS)


def _norm_bwd(dy, zhat, r, w):
    dyw = dy * w
    return r * (dyw - zhat * jnp.mean(dyw * zhat, axis=-1, keepdims=True))


def _norm_fwd(x, w, name):
    def body(x_ref, w_ref, h_ref):
        xv = x_ref[...]
        h_ref[...] = _bf(xv * _rinv(xv) * w_ref[...])

    return _rows_call(name, body, x.shape[0], 512, 1, [_tile(x, D_MODEL), _full(w)],
                      [_out_tile(D_MODEL, BF16, D_MODEL)])[0]


def _prenorm_bwd(dh, xin, w, dres, name, plans=None):
    def body(dh_ref, x_ref, w_ref, dres_ref, dx_ref, dw_ref):
        xv = x_ref[...]
        r = _rinv(xv)
        xhat = xv * r
        dhv = dh_ref[...]
        dx_ref[...] = dres_ref[...] + _norm_bwd(dhv, xhat, r, w_ref[...])
        _acc(dw_ref, jnp.sum(dhv * xhat, axis=0, keepdims=True))

    return _rows_call(name, body, xin.shape[0], 512, 1,
                      [_tile(dh, D_MODEL), _tile(xin, D_MODEL), _full(w), _tile(dres, D_MODEL)],
                      [_out_tile(D_MODEL, F32, D_MODEL), _out_acc(1, D_MODEL, D_MODEL)], plans)


def _postnorm_bwd(dout, z, w, w_mat, name, plans=None):
    def body(do_ref, z_ref, w_ref, wm_ref, dz_ref, dm_ref, dw_ref):
        zv = z_ref[...]
        r = _rinv(zv)
        zhat = zv * r
        dov = do_ref[...]
        dz = _bf(_norm_bwd(dov, zhat, r, w_ref[...]))
        dz_ref[...] = dz
        dm_ref[...] = _dot(dz, wm_ref[...], NT)
        _acc(dw_ref, jnp.sum(dov * zhat, axis=0, keepdims=True))

    return _rows_call(name, body, z.shape[0], 512, 1,
                      [_tile(dout, D_MODEL), _tile(z, D_MODEL), _full(w), _full(w_mat)],
                      [_out_tile(D_MODEL, BF16, D_MODEL), _out_tile(D_MODEL, F32, D_MODEL),
                       _out_acc(1, D_MODEL, D_MODEL)], plans)


def _t5_bucket(dist):
    n = jnp.maximum(dist, 0)
    nf = jnp.maximum(n, 1).astype(F32)
    large = MAX_EXACT + (jnp.log(nf / MAX_EXACT) / math.log(MAX_DISTANCE / MAX_EXACT)
                         * (NUM_BUCKETS - MAX_EXACT)).astype(jnp.int32)
    large = jnp.minimum(large, NUM_BUCKETS - 1)
    return jnp.where(n < MAX_EXACT, n, large)


def _band_rel():
    return jnp.arange(BLK)[:, None] + BLK - jnp.arange(2 * BLK)[None, :]


def _band_valid():
    rel = _band_rel()
    window = (rel >= 0) & (rel <= BLK)
    first = window & (jnp.arange(2 * BLK)[None, :] >= BLK)
    return jnp.stack([first, window]).astype(F32).reshape(2, 1, BAND)


RES_UNROLL = 8
PAIR = LANES // HEAD_DIM


def _pair_lanes():
    first = lax.broadcasted_iota(jnp.int32, (1, LANES), 1) < HEAD_DIM
    return first, jnp.logical_not(first)


def _heads_per_step(d):
    return HEADS if d == 1 else LANES // HEAD_DIM


def _sub_rows(r, d):
    return pl.ds(r, BLK, stride=d) if d > 1 else pl.ds(0, BLK)


def _for_residues(d, fn):
    if d <= RES_UNROLL:
        for r in range(d):
            fn(r)
    else:
        def group(i, carry):
            for k in range(RES_UNROLL):
                fn(i * RES_UNROLL + k)
            return carry

        lax.fori_loop(0, d // RES_UNROLL, group, 0)


def _attn_specs(d, g, qblock):
    cw = _heads_per_step(d) * HEAD_DIM

    def col(part, hp):
        return (g * 3 + part) * (GROUP_W // cw) + hp

    def cur(part):
        return pl.BlockSpec((d * BLK, cw), lambda hp, n: (qblock(n), col(part, hp)))

    def prev(part):
        return pl.BlockSpec((d * BLK, cw), lambda hp, n: (jnp.maximum(qblock(n) - 1, 0), col(part, hp)))

    return cur, prev


def _attn_fwd(proj, bias, g, name, plans=None):
    S = proj.shape[0]
    d = DILATIONS[g]
    NB = S // (d * BLK)
    hps = _heads_per_step(d)

    def body(q_ref, kp_ref, kc_ref, vp_ref, vc_ref, b_ref, o_ref, lse_ref):
        hp = pl.program_id(0)
        later = jnp.minimum(pl.program_id(1), 1)

        def residue(r):
            rows = _sub_rows(r, d)
            q2 = q_ref[rows, :]
            k2 = jnp.concatenate([kp_ref[rows, :], kc_ref[rows, :]], axis=0)
            v2 = jnp.concatenate([vp_ref[rows, :], vc_ref[rows, :]], axis=0)
            outs, lses = [], []
            for pp in range(hps // PAIR):
                ps = slice(pp * LANES, (pp + 1) * LANES)
                qp, kp, vp = _bf(q2[:, ps]), _bf(k2[:, ps]), _bf(v2[:, ps])
                o_h, lse_h = [], []
                for hh, own in enumerate(_pair_lanes()):
                    s = _dot(qp, jnp.where(own, kp, 0), NT) * (HEAD_DIM ** -0.5) + b_ref[later, hp * hps + pp * PAIR + hh]
                    m = jnp.max(s, axis=-1, keepdims=True)
                    p = jnp.exp(s - m)
                    l = jnp.sum(p, axis=-1, keepdims=True)
                    o_h.append(_dot(_bf(p), vp, NN) / l)
                    lse_h.append(m + jnp.log(l))
                first = _pair_lanes()[0]
                outs.append(jnp.where(first, o_h[0], o_h[1]))
                lses.append(jnp.where(first, lse_h[0], lse_h[1]))
            o_ref[rows, :] = outs[0] if len(outs) == 1 else jnp.concatenate(outs, axis=1)
            lse_ref[rows, :] = lses[0] if len(lses) == 1 else jnp.concatenate(lses, axis=1)

        _for_residues(d, residue)

    cur, prev = _attn_specs(d, g, lambda n: n)
    out = pl.BlockSpec((d * BLK, hps * HEAD_DIM), lambda hp, n: (n, hp))
    res, carried = _call(
        body, plans, name=name, grid=(HEADS // hps, NB),
        in_specs=[cur(0), prev(1), cur(1), prev(2), cur(2),
                  pl.BlockSpec((2, HEADS, BLK, 2 * BLK), lambda hp, n: (0, 0, 0, 0))],
        out_specs=[out, out], out_shape=[_sds((S, GROUP_W), F32)] * 2,
        args=(proj, proj, proj, proj, proj, bias))
    return res if plans is None else (res, carried)


def _attn_bwd(proj, bias, lse, y, dy, g, name, plans=None):
    S = proj.shape[0]
    d = DILATIONS[g]
    NB = S // (d * BLK)
    hps = _heads_per_step(d)

    def body(q_ref, kp_ref, kc_ref, vp_ref, vc_ref, b_ref, l_ref, y_ref, dy_ref,
             dq_ref, dk_ref, dv_ref, db_ref, ck_ref, cv_ref):
        hp, n = pl.program_id(0), pl.program_id(1)

        @pl.when((hp == 0) & (n == 0))
        def _():
            db_ref[...] = jnp.zeros_like(db_ref)

        @pl.when(n == 0)
        def _():
            ck_ref[...] = jnp.zeros_like(ck_ref)
            cv_ref[...] = jnp.zeros_like(cv_ref)

        @pl.when(n < NB)
        def _():
            later = jnp.minimum(n, 1)

            def residue(r):
                rows = _sub_rows(r, d)
                q2 = q_ref[rows, :]
                k2 = jnp.concatenate([kp_ref[rows, :], kc_ref[rows, :]], axis=0)
                v2 = jnp.concatenate([vp_ref[rows, :], vc_ref[rows, :]], axis=0)
                l2, y2, dy2 = l_ref[rows, :], y_ref[rows, :], dy_ref[rows, :]
                dqs, dks, dvs = [], [], []
                for pp in range(hps // PAIR):
                    ps = slice(pp * LANES, (pp + 1) * LANES)
                    qp, kp, vp = _bf(q2[:, ps]), _bf(k2[:, ps]), _bf(v2[:, ps])
                    dyp, yp = dy2[:, ps], y2[:, ps]
                    dq_h, dk_h, dv_h = [], [], []
                    for hh, own in enumerate(_pair_lanes()):
                        head = hp * hps + pp * PAIR + hh
                        s = _dot(qp, jnp.where(own, kp, 0), NT) * (HEAD_DIM ** -0.5) + b_ref[later, head]
                        p = jnp.exp(s - l2[:, pp * LANES + hh * HEAD_DIM:pp * LANES + hh * HEAD_DIM + 1])
                        dyh = jnp.where(own, dyp, 0.0)
                        delta = jnp.sum(dyh * yp, axis=-1, keepdims=True)
                        ds = p * (_dot(_bf(dyh), vp, NT) - delta)
                        db_ref[head] += ds
                        dsb = _bf(ds * (HEAD_DIM ** -0.5))
                        dq_h.append(_dot(dsb, kp, NN))
                        dk_h.append(_dot(dsb, qp, TN))
                        dv_h.append(_dot(_bf(p), _bf(dyp), TN))
                    first = _pair_lanes()[0]
                    dqs.append(jnp.where(first, dq_h[0], dq_h[1]))
                    dks.append(jnp.where(first, dk_h[0], dk_h[1]))
                    dvs.append(jnp.where(first, dv_h[0], dv_h[1]))
                dkb = dks[0] if len(dks) == 1 else jnp.concatenate(dks, axis=1)
                dvb = dvs[0] if len(dvs) == 1 else jnp.concatenate(dvs, axis=1)
                dq_ref[rows, :] = dqs[0] if len(dqs) == 1 else jnp.concatenate(dqs, axis=1)
                dk_ref[rows, :] = ck_ref[rows, :] + dkb[:BLK]
                dv_ref[rows, :] = cv_ref[rows, :] + dvb[:BLK]
                ck_ref[rows, :] = dkb[BLK:]
                cv_ref[rows, :] = dvb[BLK:]

            _for_residues(d, residue)

        @pl.when(n == NB)
        def _():
            dk_ref[...] = ck_ref[...]
            dv_ref[...] = cv_ref[...]

    def qn(n):
        return jnp.minimum(n, NB - 1)

    cur, prev = _attn_specs(d, g, qn)
    cw = hps * HEAD_DIM
    row = pl.BlockSpec((d * BLK, cw), lambda hp, n: (qn(n), hp))
    done = pl.BlockSpec((d * BLK, cw), lambda hp, n: (jnp.maximum(n - 1, 0), hp))
    (dq, dk, dv, db), carried = _call(
        body, plans, name=name, grid=(HEADS // hps, NB + 1),
        in_specs=[cur(0), prev(1), cur(1), prev(2), cur(2),
                  pl.BlockSpec((2, HEADS, BLK, 2 * BLK), lambda hp, n: (0, 0, 0, 0)), row, row, row],
        out_specs=[row, done, done, pl.BlockSpec((HEADS, BLK, 2 * BLK), lambda hp, n: (0, 0, 0))],
        out_shape=[_sds((S, GROUP_W), F32)] * 3 + [_sds((HEADS, BLK, 2 * BLK), F32)],
        scratch_shapes=[pltpu.VMEM((d * BLK, cw), F32)] * 2,
        args=(proj, proj, proj, proj, proj, bias, lse, y, dy))
    return ([dq, dk, dv], db) if plans is None else ([dq, dk, dv], db, carried)


BAND = BLK * 2 * BLK


def _bucket_onehot():
    buckets = jnp.stack([_t5_bucket(_band_rel() * d) for d in DILATIONS]).reshape(N_GROUPS, 1, BAND)
    return (buckets == jnp.arange(NUM_BUCKETS).reshape(1, NUM_BUCKETS, 1)).astype(F32)


def _relbias_fwd(rel_bias, name):
    table = rel_bias.reshape(NUM_BUCKETS, N_GROUPS, HEADS).transpose(1, 0, 2)

    def body(t_ref, oh_ref, valid_ref, o_ref):
        bias = lax.dot_general(t_ref[...], oh_ref[...], (TN, ((), ())), preferred_element_type=F32,
                               precision=lax.Precision.HIGHEST)
        for k in range(2):
            o_ref[k] = jnp.where(valid_ref[k] > 0.5, bias, NEG_INF)

    out = pl.pallas_call(
        body, name=name, grid=(N_GROUPS,),
        in_specs=[pl.BlockSpec((None, NUM_BUCKETS, HEADS), lambda g: (g, 0, 0)),
                  pl.BlockSpec((None, NUM_BUCKETS, BAND), lambda g: (g, 0, 0)),
                  pl.BlockSpec((2, 1, BAND), lambda g: (0, 0, 0))],
        out_specs=pl.BlockSpec((None, 2, HEADS, BAND), lambda g: (g, 0, 0, 0)),
        out_shape=_sds((N_GROUPS, 2, HEADS, BAND), F32), compiler_params=_cp(1))(table, _bucket_onehot(), _band_valid())
    return out.reshape(N_GROUPS, 2, HEADS, BLK, 2 * BLK)


def _relbias_bwd(dbs, name):
    band = BAND
    onehot = _bucket_onehot()
    dbf = jnp.stack([db.reshape(HEADS, band) for db in dbs])

    def body(oh_ref, db_ref, o_ref):
        o_ref[...] = lax.dot_general(oh_ref[...], db_ref[...], (NT, ((), ())), preferred_element_type=F32,
                                     precision=lax.Precision.HIGHEST)

    out = pl.pallas_call(
        body, name=name, grid=(N_GROUPS,),
        in_specs=[pl.BlockSpec((None, NUM_BUCKETS, band), lambda g: (g, 0, 0)),
                  pl.BlockSpec((None, HEADS, band), lambda g: (g, 0, 0))],
        out_specs=pl.BlockSpec((None, NUM_BUCKETS, HEADS), lambda g: (g, 0, 0)),
        out_shape=_sds((N_GROUPS, NUM_BUCKETS, HEADS), F32), compiler_params=_cp(1))(onehot, dbf)
    return out.transpose(1, 0, 2).reshape(NUM_BUCKETS, N_GROUPS * HEADS)


def _chunk_pos(shape):
    return lax.broadcasted_iota(jnp.int32, shape, 0) % HG_CHUNK


def _chunk_cumsum(v):
    pos = _chunk_pos(v.shape)
    s = 1
    while s < HG_CHUNK:
        v = v + jnp.where(pos >= s, pltpu.roll(v, s, 0), 0.0)
        s *= 2
    return v


def _chunk_rev_cumsum(v):
    pos = _chunk_pos(v.shape)
    n = v.shape[0]
    s = 1
    while s < HG_CHUNK:
        v = v + jnp.where(pos < HG_CHUNK - s, pltpu.roll(v, n - s, 0), 0.0)
        s *= 2
    return v


def _lower_bound(raw):
    a0, a1 = raw[0:1], raw[1:2]
    m = jnp.maximum(a0, a1)
    e0, e1 = jnp.exp(a0 - m), jnp.exp(a1 - m)
    return e0 / (e0 + e1)


def _hg_gates(qr, fr, lb):
    sf = _sigmoid(fr)
    f = lb + (1.0 - lb) * sf
    sq = _sigmoid(qr)
    return qr * sq, sq, f, sf


HG_COL0 = QKV_W // HG_W


def _hgrn_fwd(proj, lb_raw, nw, name, plans=None):
    S = proj.shape[0]
    ncs = HG_TILE // HG_CHUNK
    tril = jnp.tril(jnp.ones((HG_CHUNK, HG_CHUNK), dtype=bool))

    def body(q_ref, f_ref, i_ref, og_ref, lb_ref, nw_ref, y_ref, o_ref, st_ref, state):
        @pl.when(pl.program_id(0) == 0)
        def _():
            state[...] = jnp.zeros_like(state)

        lb = _lower_bound(lb_ref[...])
        q, _, f, _ = _hg_gates(q_ref[...], f_ref[...], lb)
        k = 1.0 - f
        G = _chunk_cumsum(jnp.log(f))
        row = lax.broadcasted_iota(jnp.int32, (HG_CHUNK, HG_CHUNK), 0)
        col = lax.broadcasted_iota(jnp.int32, (HG_CHUNK, HG_CHUNK), 1)
        heads = [slice(h * HG_DK, (h + 1) * HG_DK) for h in range(HG_HEADS)]
        sts = [state[h] for h in range(HG_HEADS)]
        for c in range(ncs):
            cs = slice(c * HG_CHUNK, (c + 1) * HG_CHUNK)
            for h, hs in enumerate(heads):
                Gc = G[cs, hs]
                gl = Gc[HG_CHUNK - 1:HG_CHUNK]
                qt = _bf(q[cs, hs] * jnp.exp(Gc))
                kt = _bf(k[cs, hs] * jnp.exp(-Gc))
                kd = _bf(k[cs, hs] * jnp.exp(gl - Gc))
                v = _bf(i_ref[cs, hs])
                A = jnp.where(row >= col, _dot(qt, kt, NT), 0.0)
                o_ref[cs, hs] = _dot(_bf(A), v, NN) + _dot(qt, _bf(sts[h]), NT)
                st_ref[c, h] = sts[h]
                sts[h] = sts[h] * jnp.exp(gl) + _dot(v, kd, TN)
        for h, hs in enumerate(heads):
            state[h] = sts[h]
            oh = o_ref[:, hs]
            og = og_ref[:, hs]
            y_ref[:, hs] = oh * _rinv(oh) * nw_ref[...] * (og * _sigmoid(og))

    def colspec(j):
        return pl.BlockSpec((HG_TILE, HG_W), lambda i: (i, HG_COL0 + j))

    res, carried = _call(
        body, plans, name=name, grid=(S // HG_TILE,),
        in_specs=[colspec(0), colspec(1), colspec(2), colspec(3),
                  pl.BlockSpec((2, HG_W), lambda i: (0, 0)), pl.BlockSpec((1, HG_DK), lambda i: (0, 0))],
        out_specs=[pl.BlockSpec((HG_TILE, HG_W), lambda i: (i, 0))] * 2
        + [pl.BlockSpec((ncs, HG_HEADS, HG_DK, HG_DK), lambda i: (i, 0, 0, 0))],
        out_shape=[_sds((S, HG_W), F32)] * 2 + [_sds((S // HG_CHUNK, HG_HEADS, HG_DK, HG_DK), F32)],
        scratch_shapes=[pltpu.VMEM((HG_HEADS, HG_DK, HG_DK), F32)],
        args=(proj, proj, proj, proj, lb_raw, nw))
    return res if plans is None else (res, carried)


def _hgrn_bwd(proj, lb_raw, nw, o, states, dy, d_attn, d_gates, name):
    S = proj.shape[0]
    ncs = HG_TILE // HG_CHUNK
    nt = S // HG_TILE
    n_a, n_g = len(d_attn), len(d_gates)
    own = [slice(QKV_W + j * HG_W, QKV_W + (j + 1) * HG_W) for j in range(4)]

    def body(q_ref, f_ref, i_ref, og_ref, lb_ref, nw_ref, o_ref, st_ref, dy_ref, *rest):
        attn_refs, gate_refs = rest[:n_a], rest[n_a:n_a + n_g]
        dp_ref, dlb_ref, dnw_ref, dstate, do_s, dG_s, dgl_s, dk_s, dlb_s = rest[n_a + n_g:]
        dq_ref, df_ref, di_ref, dog_ref = (dp_ref.at[:, cols] for cols in own)
        step = pl.program_id(0)
        for k, a_ref in enumerate(attn_refs):
            dp_ref[:, k * GROUP_W:(k + 1) * GROUP_W] = _bf(a_ref[...])
        for k, g_ref in enumerate(gate_refs):
            dp_ref[:, QKV_W + 4 * HG_W + k * D_MODEL:QKV_W + 4 * HG_W + (k + 1) * D_MODEL] = g_ref[...]

        @pl.when(step == 0)
        def _():
            dstate[...] = jnp.zeros_like(dstate)
            dlb_s[...] = jnp.zeros_like(dlb_s)
            dnw_ref[...] = jnp.zeros_like(dnw_ref)

        lb = _lower_bound(lb_ref[...])
        qr = q_ref[...]
        q, sq, f, sf = _hg_gates(qr, f_ref[...], lb)
        k = 1.0 - f
        G = _chunk_cumsum(jnp.log(f))
        nwv = nw_ref[...]
        row = lax.broadcasted_iota(jnp.int32, (HG_CHUNK, HG_CHUNK), 0)
        col = lax.broadcasted_iota(jnp.int32, (HG_CHUNK, HG_CHUNK), 1)
        for h in range(HG_HEADS):
            hs = slice(h * HG_DK, (h + 1) * HG_DK)
            oh = o_ref[:, hs]
            r = _rinv(oh)
            ohat = oh * r
            og = og_ref[:, hs]
            sg = _sigmoid(og)
            dyh = dy_ref[:, hs]
            don = dyh * (og * sg)
            dog_ref[:, hs] = _bf(dyh * (ohat * nwv) * (sg * (1.0 + og * (1.0 - sg))))
            dnw_ref[...] += jnp.sum(don * ohat, axis=0, keepdims=True)
            do_s[:, hs] = _norm_bwd(don, ohat, r, nwv)
        dsts = [dstate[h] for h in range(HG_HEADS)]
        for c in reversed(range(ncs)):
            cs = slice(c * HG_CHUNK, (c + 1) * HG_CHUNK)
            for h in range(HG_HEADS):
                hs = slice(h * HG_DK, (h + 1) * HG_DK)
                dst = dsts[h]
                Gc = G[cs, hs]
                gl = Gc[HG_CHUNK - 1:HG_CHUNK]
                eG, enG, edG, egl = jnp.exp(Gc), jnp.exp(-Gc), jnp.exp(gl - Gc), jnp.exp(gl)
                qt, kt, kd = q[cs, hs] * eG, k[cs, hs] * enG, k[cs, hs] * edG
                qtb, ktb, kdb = _bf(qt), _bf(kt), _bf(kd)
                v = _bf(i_ref[cs, hs])
                do = _bf(do_s[cs, hs])
                st = st_ref[c, h]
                dstb = _bf(dst)
                A = jnp.where(row >= col, _dot(qtb, ktb, NT), 0.0)
                dA = _bf(jnp.where(row >= col, _dot(do, v, NT), 0.0))
                di_ref[cs, hs] = _bf(_dot(_bf(A), do, TN) + _dot(kdb, dstb, NT))
                dqt = _dot(dA, ktb, NN) + _dot(do, _bf(st), NN)
                dkt = _dot(dA, qtb, TN)
                dkd = _dot(v, dstb, NN)
                dgl = egl * jnp.sum(st * dst, axis=0, keepdims=True) + jnp.sum(dkd * kd, axis=0, keepdims=True)
                dsts[h] = dst * egl + _dot(do, qtb, TN)
                dq_ref[cs, hs] = _bf(dqt * eG * (sq[cs, hs] * (1.0 + qr[cs, hs] * (1.0 - sq[cs, hs]))))
                dk_s[cs, hs] = dkt * enG + dkd * edG
                dG_s[cs, hs] = dqt * qt - dkt * kt - dkd * kd
                dgl_s[cs, hs] = jnp.broadcast_to(dgl, (HG_CHUNK, HG_DK))
        for h in range(HG_HEADS):
            dstate[h] = dsts[h]
        dg = _chunk_rev_cumsum(dG_s[...]) + dgl_s[...]
        dfv = dg / f - dk_s[...]
        df_ref[...] = _bf(dfv * (1.0 - lb) * sf * (1.0 - sf))
        dlb_s[...] += jnp.sum(dfv * (1.0 - sf), axis=0, keepdims=True)

        @pl.when(step == nt - 1)
        def _():
            t = dlb_s[...] * lb * (1.0 - lb)
            dlb_ref[...] = jnp.concatenate([t, -t], axis=0)

    def colspec(j):
        return pl.BlockSpec((HG_TILE, HG_W), lambda i: (nt - 1 - i, HG_COL0 + j))

    def rows(width):
        return pl.BlockSpec((HG_TILE, width), lambda i: (nt - 1 - i, 0))

    tile = rows(HG_W)
    return pl.pallas_call(
        body, name=name, grid=(nt,),
        in_specs=[colspec(0), colspec(1), colspec(2), colspec(3),
                  pl.BlockSpec((2, HG_W), lambda i: (0, 0)), pl.BlockSpec((1, HG_DK), lambda i: (0, 0)),
                  tile, pl.BlockSpec((ncs, HG_HEADS, HG_DK, HG_DK), lambda i: (nt - 1 - i, 0, 0, 0)), tile]
        + [rows(GROUP_W)] * n_a + [rows(D_MODEL)] * n_g,
        out_specs=[rows(IN_W), pl.BlockSpec((2, HG_W), lambda i: (0, 0)), pl.BlockSpec((1, HG_DK), lambda i: (0, 0))],
        out_shape=[_sds((S, IN_W), BF16), _sds((2, HG_W), F32), _sds((1, HG_DK), F32)],
        scratch_shapes=[pltpu.VMEM((HG_HEADS, HG_DK, HG_DK), F32)] + [pltpu.VMEM((HG_TILE, HG_W), F32)] * 4
        + [pltpu.VMEM((1, HG_W), F32)],
        compiler_params=_cp(1))(proj, proj, proj, proj, lb_raw, nw, o, states, dy, *d_attn, *d_gates)


GATE_COL0 = (QKV_W + 4 * HG_W) // GROUP_W
HALF_D = D_MODEL // 2


def _gate_tiles(proj):
    return [_tile(proj, HALF_D, functools.partial(lambda c, k: GATE_COL0 + k, k=k)) for k in range(4)]


def _gates(g_refs):
    s0 = _sigmoid(jnp.concatenate([g_refs[0][...], g_refs[1][...]], axis=1))
    s1 = _sigmoid(jnp.concatenate([g_refs[2][...], g_refs[3][...]], axis=1))
    return s0, s1


def _branch_fwd(os_, lses, yh, proj, w_a, w_h, name, plans=None):
    nb = w_a.shape[0]

    def body(o0, o1, o2, l0, l1, l2, yh_ref, g0a, g0b, g1a, g1b, wa_ref, wh_ref,
             y_ref, lse_ref, za_ref, zh_ref, m_ref):
        a, b, c = l0[...], l1[...], l2[...]
        m = jnp.maximum(jnp.maximum(a, b), c)
        ea, eb, ec = jnp.exp(a - m), jnp.exp(b - m), jnp.exp(c - m)
        den = ea + eb + ec
        y = (ea * o0[...] + eb * o1[...] + ec * o2[...]) / den
        y_ref[...] = y
        lse_ref[...] = m + jnp.log(den)
        yb, yhb = _bf(y), _bf(yh_ref[...])
        za = jnp.concatenate([_dot(yb, wa_ref[j], NN) for j in range(nb)], axis=1)
        zh = jnp.concatenate([_dot(yhb, wh_ref[j], NN) for j in range(nb)], axis=1)
        s0, s1 = _gates((g0a, g0b, g1a, g1b))
        za_ref[...] = za
        zh_ref[...] = zh
        m_ref[...] = _bf(s0 * za + s1 * zh)

    return _rows_call(name, body, yh.shape[0], 512, 1,
                      [*[_tile(t, GROUP_W) for t in (*os_, *lses)], _tile(yh, HG_W), *_gate_tiles(proj),
                       _full(w_a), _full(w_h)],
                      [_out_tile(GROUP_W, F32, GROUP_W)] * 2 + [_out_tile(D_MODEL, F32, D_MODEL)] * 2
                      + [_out_tile(D_MODEL, BF16, D_MODEL)], plans)


def _branch_bwd(dm, za, zh, proj, w_a, w_h, name, plans=None):
    nb, _, Nb = w_a.shape

    def body(dm_ref, za_ref, zh_ref, g0a, g0b, g1a, g1b, wa_ref, wh_ref,
             dza_ref, dzh_ref, dg0_ref, dg1_ref, dy_ref, dyh_ref):
        dmv = dm_ref[...]
        s0, s1 = _gates((g0a, g0b, g1a, g1b))
        dza, dzh = _bf(dmv * s0), _bf(dmv * s1)
        dza_ref[...] = dza
        dzh_ref[...] = dzh
        dg0_ref[...] = _bf(dmv * za_ref[...] * s0 * (1.0 - s0))
        dg1_ref[...] = _bf(dmv * zh_ref[...] * s1 * (1.0 - s1))
        dy_ref[...] = sum(_dot(dza[:, j * Nb:(j + 1) * Nb], wa_ref[j], NT) for j in range(nb))
        dyh_ref[...] = sum(_dot(dzh[:, j * Nb:(j + 1) * Nb], wh_ref[j], NT) for j in range(nb))

    return _rows_call(name, body, za.shape[0], 512, 1,
                      [_tile(dm, D_MODEL), _tile(za, D_MODEL), _tile(zh, D_MODEL), *_gate_tiles(proj),
                       _full(w_a), _full(w_h)],
                      [_out_tile(D_MODEL, BF16, D_MODEL)] * 4 + [_out_tile(GROUP_W, F32, GROUP_W),
                                                                 _out_tile(HG_W, F32, HG_W)], plans)


def _mix_out(merged, w_out, x, w_post, w_pre, name):
    def body(m_ref, wo_ref, x_ref, wp_ref, wf_ref, mo_ref, x1_ref, h2_ref):
        z = _dot(m_ref[...], wo_ref[...], NN)
        mo_ref[...] = z
        x1 = x_ref[...] + z * _rinv(z) * wp_ref[...]
        x1_ref[...] = x1
        h2_ref[...] = _bf(x1 * _rinv(x1) * wf_ref[...])

    return _rows_call(name, body, x.shape[0], 512, 1,
                      [_tile(merged, D_MODEL), _full(w_out), _tile(x, D_MODEL), _full(w_post), _full(w_pre)],
                      [_out_tile(D_MODEL, F32, D_MODEL), _out_tile(D_MODEL, F32, D_MODEL),
                       _out_tile(D_MODEL, BF16, D_MODEL)])


def _loss_head(a, w_down, x1, tgt, w, name):
    def body(a_ref, wd_ref, x1_ref, t_ref, w_ref, dx_ref, df_ref, dw_ref, loss_ref):
        z = _dot(a_ref[...], wd_ref[...], NN)
        r = _rinv(z)
        zhat = z * r
        wv = w_ref[...]
        e = x1_ref[...] + zhat * wv - t_ref[...]
        dx = e * (1.0 / D_MODEL)
        dx_ref[...] = dx
        df_ref[...] = _bf(_norm_bwd(dx, zhat, r, wv))
        _acc(dw_ref, jnp.sum(dx * zhat, axis=0, keepdims=True))
        part = 0.5 * jnp.sum(jnp.sum(e * e, axis=1, keepdims=True), axis=0, keepdims=True) * (1.0 / D_MODEL)
        _acc(loss_ref, jnp.broadcast_to(part, (1, LANES)))

    return _rows_call(name, body, x1.shape[0], 512, 1,
                      [_tile(a, D_FF), _full(w_down), _tile(x1, D_MODEL), _tile(tgt, D_MODEL), _full(w)],
                      [_out_tile(D_MODEL, F32, D_MODEL), _out_tile(D_MODEL, BF16, D_MODEL),
                       _out_acc(1, D_MODEL, D_MODEL), _out_acc(1, LANES, LANES)])


CONV_CB = D_FF // 2
CONV_TM = 512
HALO = 8
SQRT_HALF = 0.7071067811865476
INV_SQRT_2PI = 0.3989422804014327


CONV_RS = 32


def _lane_tiles():
    return [slice(k * LANES, (k + 1) * LANES) for k in range(CONV_CB // LANES)]


def _strip_start(i):
    return pl.multiple_of(i * CONV_RS, CONV_RS)


def _strip_taps(u_ref, halo_ref, r0, cs, first_strip, first_tile):
    if first_strip:
        before = jnp.where(first_tile, 0.0, halo_ref[:, cs])
        blk = jnp.concatenate([before, u_ref[0:CONV_RS, cs]], axis=0)
    else:
        blk = u_ref[pl.ds(pl.multiple_of(r0 - HALO, HALO), CONV_RS + HALO), cs]
    return pltpu.roll(blk, 2, 0)[HALO:], pltpu.roll(blk, 1, 0)[HALO:], blk[HALO:]


def _conv(taps, w_ref, b_ref, cs):
    return b_ref[:, cs] + w_ref[0:1, cs] * taps[0] + w_ref[1:2, cs] * taps[1] + w_ref[2:3, cs] * taps[2]


def _conv_specs(tm):
    nh = tm // HALO
    nc = D_FF // CONV_CB

    def tile(off):
        return pl.BlockSpec((tm, CONV_CB), lambda c, i: (i, off + c))

    def halo(off):
        return pl.BlockSpec((HALO, CONV_CB), lambda c, i: (jnp.maximum(i * nh - 1, 0), off + c))

    def small(rows, off):
        return pl.BlockSpec((rows, CONV_CB), lambda c, i: (0, off + c))

    return nc, tile, halo, small


def _conv_gelu_fwd(u, cw, cb, name, plans=None):
    S = u.shape[0]
    tm = CONV_TM
    nc, tile, halo, small = _conv_specs(tm)

    def body(ug, hg, uv, hv, wg, wv, bg, bv, a_ref):
        first_tile = pl.program_id(1) == 0

        def strip(r0, first_strip):
            for cs in _lane_tiles():
                cg = _conv(_strip_taps(ug, hg, r0, cs, first_strip, first_tile), wg, bg, cs)
                cv = _conv(_strip_taps(uv, hv, r0, cs, first_strip, first_tile), wv, bv, cs)
                a_ref[pl.ds(r0, CONV_RS), cs] = _bf(0.5 * cg * (1.0 + lax.erf(cg * SQRT_HALF)) * cv)

        strip(0, True)
        lax.fori_loop(1, tm // CONV_RS, lambda k, c: (strip(_strip_start(k), False), c)[1], 0)

    (a,), carried = _call(
        body, plans, name=name, grid=(nc, S // tm),
        in_specs=[tile(0), halo(0), tile(nc), halo(nc), small(3, 0), small(3, nc), small(1, 0), small(1, nc)],
        out_specs=[tile(0)], out_shape=[_sds((S, D_FF), BF16)], args=(u, u, u, u, cw, cw, cb, cb))
    return a if plans is None else (a, carried)


def _conv_gelu_bwd(u, dff, w_down, cw, cb, name, plans=None):
    S = u.shape[0]
    tm = CONV_TM
    nt = S // tm
    nc, tile, halo, small = _conv_specs(tm)

    def body(ug, hg, uv, hv, wg, wv, bg, bv, dff_ref, wd_ref, dcg_ref, dcv_ref, dwg_ref, dwv_ref, dbg_ref, dbv_ref,
             acc, da_ref):
        i = pl.program_id(1)
        first_tile = i == 0
        da_ref[...] = _dot(dff_ref[...], wd_ref[...], NT)

        @pl.when(first_tile)
        def _():
            acc[...] = jnp.zeros_like(acc)

        def strip(r0, first_strip):
            rows = pl.ds(r0, CONV_RS)
            for cs in _lane_tiles():
                tg = _strip_taps(ug, hg, r0, cs, first_strip, first_tile)
                tv = _strip_taps(uv, hv, r0, cs, first_strip, first_tile)
                cg = _conv(tg, wg, bg, cs)
                cv = _conv(tv, wv, bv, cs)
                phi = 0.5 * (1.0 + lax.erf(cg * SQRT_HALF))
                dav = da_ref[rows, cs]
                dcg = dav * cv * (phi + cg * jnp.exp(-0.5 * cg * cg) * INV_SQRT_2PI)
                dcv = dav * (cg * phi)
                dcg_ref[rows, cs] = dcg
                dcv_ref[rows, cs] = dcv
                for half, (dc, taps) in enumerate(((dcg, tg), (dcv, tv))):
                    for j in range(3):
                        acc[4 * half + j, :, cs] += dc * taps[j]
                    acc[4 * half + 3, :, cs] += dc

        strip(0, True)
        lax.fori_loop(1, tm // CONV_RS, lambda k, c: (strip(_strip_start(k), False), c)[1], 0)

        @pl.when(i == nt - 1)
        def _():
            for half, (dw_ref, db_ref) in enumerate(((dwg_ref, dbg_ref), (dwv_ref, dbv_ref))):
                for j in range(3):
                    dw_ref[j:j + 1, :] = jnp.sum(acc[4 * half + j], axis=0, keepdims=True)
                db_ref[...] = jnp.sum(acc[4 * half + 3], axis=0, keepdims=True)

    res, carried = _call(
        body, plans, name=name, grid=(nc, nt),
        in_specs=[tile(0), halo(0), tile(nc), halo(nc), small(3, 0), small(3, nc), small(1, 0), small(1, nc),
                  pl.BlockSpec((tm, D_MODEL), lambda c, i: (i, 0)), pl.BlockSpec((CONV_CB, D_MODEL), lambda c, i: (c, 0))],
        out_specs=[tile(0), tile(0), small(3, 0), small(3, 0), small(1, 0), small(1, 0)],
        out_shape=[_sds((S, D_FF), F32)] * 2 + [_sds((3, D_FF), F32)] * 2 + [_sds((1, D_FF), F32)] * 2,
        scratch_shapes=[pltpu.VMEM((8, CONV_RS, CONV_CB), F32), pltpu.VMEM((tm, CONV_CB), F32)],
        args=(u, u, u, u, cw, cw, cb, cb, dff, w_down))
    return res if plans is None else (res, carried)


def _conv_input_bwd(dcg, dcv, cw, name, plans=None):
    S = dcg.shape[0]
    tm = CONV_TM // 2
    nh = tm // HALO
    nt = S // tm
    n = CONV_RS + HALO
    tile = pl.BlockSpec((tm, D_FF), lambda i: (i, 0))
    nxt = pl.BlockSpec((HALO, D_FF), lambda i: (jnp.minimum((i + 1) * nh, S // HALO - 1), 0))

    def body(g_ref, ng_ref, v_ref, nv_ref, w_ref, du_ref):
        last_tile = pl.program_id(0) == nt - 1

        def strip(r0, last_strip):
            for half, (dc_ref, n_ref) in enumerate(((g_ref, ng_ref), (v_ref, nv_ref))):
                for k in range(D_FF // LANES):
                    cs = slice(k * LANES, (k + 1) * LANES)
                    ws = slice(half * D_FF + k * LANES, half * D_FF + (k + 1) * LANES)
                    if last_strip:
                        after = jnp.where(last_tile, 0.0, n_ref[:, cs])
                        blk = jnp.concatenate([dc_ref[tm - CONV_RS:tm, cs], after], axis=0)
                    else:
                        blk = dc_ref[pl.ds(r0, n), cs]
                    d1 = pltpu.roll(blk, n - 1, 0)[:CONV_RS]
                    d2 = pltpu.roll(blk, n - 2, 0)[:CONV_RS]
                    du_ref[pl.ds(r0, CONV_RS), ws] = _bf(w_ref[2:3, ws] * blk[:CONV_RS] + w_ref[1:2, ws] * d1
                                                         + w_ref[0:1, ws] * d2)

        lax.fori_loop(0, tm // CONV_RS - 1, lambda k, c: (strip(_strip_start(k), False), c)[1], 0)
        strip(tm - CONV_RS, True)

    (du,), carried = _call(
        body, plans, name=name, grid=(nt,),
        in_specs=[tile, nxt, tile, nxt, pl.BlockSpec((3, 2 * D_FF), lambda i: (0, 0))],
        out_specs=[pl.BlockSpec((tm, 2 * D_FF), lambda i: (i, 0))], out_shape=[_sds((S, 2 * D_FF), BF16)],
        args=(dcg, dcg, dcv, dcv, cw))
    return du if plans is None else (du, carried)


def _row_tile(n, cap):
    best = n
    for t in range(16, cap + 1, 16):
        if n % t == 0:
            best = t
    return best if best <= cap else n


def _rows_for_bytes(nbytes, cols):
    return max(16, nbytes // (4 * cols) // 16 * 16)


def _adamw(w, g, m, v, name):
    R, C = w.shape
    tr = _row_tile(R, _rows_for_bytes(2 << 20, C))

    def body(w_ref, g_ref, m_ref, v_ref, d_ref, nm_ref, nv_ref):
        gv = g_ref[...]
        nm = ADAM_B1 * m_ref[...] + (1.0 - ADAM_B1) * gv
        nv = ADAM_B2 * v_ref[...] + (1.0 - ADAM_B2) * (gv * gv)
        m_hat = nm / (1.0 - ADAM_B1 ** ADAM_STEP)
        v_hat = nv / (1.0 - ADAM_B2 ** ADAM_STEP)
        d_ref[...] = -ADAM_LR * (m_hat / (jnp.sqrt(v_hat) + ADAM_EPS) + ADAM_WD * w_ref[...])
        nm_ref[...] = nm
        nv_ref[...] = nv

    spec = pl.BlockSpec((tr, C), lambda i: (i, 0))
    return pl.pallas_call(body, name=name, grid=(R // tr,), in_specs=[spec] * 4, out_specs=[spec] * 3,
                          out_shape=[_sds((R, C), F32)] * 3, compiler_params=_cp(1))(w, g, m, v)


def _pair_sum(gfull, rcv, c_idx, name):
    nb, R, C = gfull.shape
    half = R // 2
    tr = _row_tile(half, _rows_for_bytes(2 << 20, C))
    nt = half // tr

    def body(c_ref, g_ref, r_ref, o_ref):
        o_ref[...] = _bf(g_ref[...] + r_ref[...])

    return pl.pallas_call(
        body, name=name,
        grid_spec=pltpu.PrefetchScalarGridSpec(
            num_scalar_prefetch=1, grid=(nb, nt),
            in_specs=[pl.BlockSpec((None, tr, C), lambda j, i, c_ref: (j, c_ref[0] * nt + i, 0)),
                      pl.BlockSpec((None, tr, C), lambda j, i, c_ref: (j, i, 0))],
            out_specs=pl.BlockSpec((None, tr, C), lambda j, i, c_ref: (j, i, 0))),
        out_shape=_sds((nb, half, C), BF16), compiler_params=_cp(2))(c_idx, gfull, rcv)


def _chip_sum(arrived, own, place, name):
    nb, H, C = arrived.shape
    tr = _row_tile(H, _rows_for_bytes(2 << 20, C))
    nt = H // tr

    def body(pl_ref, *refs):
        o_ref = refs[nb + 1]
        me = pl_ref[0]
        acc = None
        for k in range(nb):
            term = jnp.where(me == k, refs[nb][...], refs[k][...]).astype(F32)
            acc = term if acc is None else acc + term
        o_ref[...] = acc

    def other(k):
        return pl.BlockSpec((None, tr, C), lambda i, p: (jnp.where(p[0] == k, (k + 1) % nb, k), i, 0))

    return pl.pallas_call(
        body, name=name,
        grid_spec=pltpu.PrefetchScalarGridSpec(
            num_scalar_prefetch=1, grid=(nt,),
            in_specs=[other(k) for k in range(nb)] + [pl.BlockSpec((None, tr, C), lambda i, p: (p[0], i, 0))],
            out_specs=pl.BlockSpec((tr, C), lambda i, p: (p[1] * nt + i, 0))),
        out_shape=_sds((2 * H, C), F32), compiler_params=_cp(1))(place, *([arrived] * nb), own)


def _cast_into_slot(shard, place, name):
    R, C = shard.shape
    tr = _row_tile(R, 256)

    def body(pl_ref, s_ref, o_ref):
        o_ref[...] = _bf(s_ref[...])

    return pl.pallas_call(
        body, name=name,
        grid_spec=pltpu.PrefetchScalarGridSpec(
            num_scalar_prefetch=1, grid=(R // tr,),
            in_specs=[pl.BlockSpec((tr, C), lambda i, p: (i, 0))],
            out_specs=pl.BlockSpec((None, tr, C), lambda i, p: (p[0], i, 0))),
        out_shape=_sds((N_CHIPS, R, C), BF16), compiler_params=_cp(1))(place, shard)


def _place():
    x, y, c = lax.axis_index("x"), lax.axis_index("y"), lax.axis_index("c")
    chips = [(1 - x, y), (x, 1 - y), (1 - x, 1 - y)]
    return x, y, c, chips


def _chip_id(px, py):
    return 2 * px + py


def _remote(src, dst, send_sems, recv_sems, k, to):
    return pltpu.make_async_remote_copy(src_ref=src, dst_ref=dst, send_sem=send_sems.at[k], recv_sem=recv_sems.at[k],
                                        device_id=to, device_id_type=MESH)


def _proj_gathered(h, slot, place, name, tm=512, plan=None):
    M, K = h.shape
    nb, _, Nb = slot.shape
    half = K // 2
    nt = M // tm
    cx, cy = place[0] // 2, place[0] % 2
    order = jnp.stack([place[0], _chip_id(1 - cx, cy), _chip_id(cx, 1 - cy), _chip_id(1 - cx, 1 - cy)]).astype(jnp.int32)

    p_in = [] if plan is None else plan.ins + plan.inouts
    p_out = [] if plan is None else [_sds(a.shape, a.dtype) for a in plan.inouts] + plan.outs
    n_pi, n_po = len(p_in), len(p_out)

    def body(order_ref, h_ref, slot_in, *refs):
        o_ref, slot_ref = refs[n_pi:n_pi + 2]
        w_buf, ici_send, ici_recv, pass_send, pass_recv, load_sem = refs[n_pi + 2 + n_po:n_pi + 2 + n_po + 6]

        def carried():
            if plan is None:
                return [], [], []
            ins = refs[:len(plan.ins)]
            outs = refs[n_pi + 2:n_pi + 2 + n_po]
            return plan.copies(ins, outs[:len(plan.inouts)], outs[len(plan.inouts):], *refs[n_pi + 2 + n_po + 6:])

        b, i = pl.program_id(0), pl.program_id(1)
        x, y, c, chips = _place()
        me = _chip_id(x, y)
        sib = (x, y, 1 - c)
        mine, other = pl.ds(c * half, half), pl.ds((1 - c) * half, half)

        def sent(k):
            blk = slot_ref.at[me, mine]
            return _remote(blk, blk, ici_send, ici_recv, k, (*chips[k], c))

        def landed(k):
            blk = slot_ref.at[_chip_id(*chips[k]), mine]
            return _remote(blk, blk, ici_send, ici_recv, k, (*chips[k], c))

        def passed(k, rows):
            blk = slot_ref.at[_chip_id(*chips[k]), rows]
            return _remote(blk, blk, pass_send, pass_recv, k, sib)

        @pl.when((b == 0) & (i == 0))
        def _():
            for k in range(len(chips)):
                sent(k).start()
            sends, _, local = carried()
            for cp in (*sends, *local):
                cp.start()

        for k in range(len(chips)):
            @pl.when((b == k + 1) & (i == 0))
            def _(k=k):
                landed(k).wait_recv()
                passed(k, mine).start()
                passed(k, other).wait_recv()

        @pl.when(i == 0)
        def _():
            load = pltpu.make_async_copy(slot_ref.at[order_ref[b]], w_buf, load_sem.at[0])
            load.start()
            load.wait()

        o_ref[...] = _dot(h_ref[...], w_buf[...], NN)

        @pl.when((b == nb - 1) & (i == nt - 1))
        def _():
            for k in range(len(chips)):
                sent(k).wait_send()
                passed(k, mine).wait_send()
            sends, recvs, local = carried()
            for cp in recvs:
                cp.wait_recv()
            for cp in sends:
                cp.wait_send()
            for cp in local:
                cp.wait()

    n_peers = N_CHIPS - 1
    return pl.pallas_call(
        body, name=name,
        grid_spec=pltpu.PrefetchScalarGridSpec(
            num_scalar_prefetch=1, grid=(nb, nt),
            in_specs=[pl.BlockSpec((tm, K), lambda b, i, o: (i, 0)), ANY] + [ANY] * n_pi,
            out_specs=[pl.BlockSpec((tm, Nb), lambda b, i, o: (i, o[b])), ANY] + [ANY] * n_po,
            scratch_shapes=[pltpu.VMEM((K, Nb), BF16)] + [pltpu.SemaphoreType.DMA((n_peers,))] * 4
            + [pltpu.SemaphoreType.DMA((1,))]
            + ([] if plan is None else [pltpu.SemaphoreType.DMA((plan.n_sems,))] * 3)),
        out_shape=[_sds((M, nb * Nb), F32), _sds(slot.shape, slot.dtype)] + p_out,
        input_output_aliases={2: 1, **({} if plan is None else
                                       {3 + len(plan.ins) + a: 2 + a for a in range(len(plan.inouts))})},
        compiler_params=_cp(2))(order, h, slot, *p_in)


def _gather_ici_plan(slots, wholes):
    ns, nw = len(slots), len(wholes)

    def copies(ins, ios, outs, send_sems, recv_sems, local_sems):
        x, y, c, chips = _place()
        me = _chip_id(x, y)
        sends, recvs = [], []
        for a in range(ns + nw):
            dst = ios[a] if a < ns else outs[a - ns]
            R = dst.shape[1]
            rows = pl.ds(c * (R // 2), R // 2) if a < ns else pl.ds(0, R)
            src = dst.at[me, rows] if a < ns else ins[a - ns]
            for j, chip in enumerate(chips):
                sends.append(_remote(src, dst.at[me, rows], send_sems, recv_sems, 3 * a + j, (*chip, c)))
                landed = dst.at[_chip_id(*chip), rows]
                recvs.append(_remote(landed, landed, send_sems, recv_sems, 3 * a + j, (*chip, c)))
        local = [pltpu.make_async_copy(ins[b], outs[b].at[me], local_sems.at[b]) for b in range(nw)]
        return sends, recvs, local

    return _Plan(copies, 3 * (ns + nw), ins=wholes, inouts=slots,
                 outs=[_sds((N_CHIPS, *s.shape), s.dtype) for s in wholes])


def _gather_pass_plan(slots):
    def copies(ins, ios, outs, send_sems, recv_sems, local_sems):
        x, y, c, chips = _place()
        sib = (x, y, 1 - c)
        sends, recvs = [], []
        for a, buf in enumerate(ios):
            half = buf.shape[1] // 2
            for j, chip in enumerate(chips):
                mine = buf.at[_chip_id(*chip), pl.ds(c * half, half)]
                other = buf.at[_chip_id(*chip), pl.ds((1 - c) * half, half)]
                sends.append(_remote(mine, mine, send_sems, recv_sems, 3 * a + j, sib))
                recvs.append(_remote(other, other, send_sems, recv_sems, 3 * a + j, sib))
        return sends, recvs, []

    return _Plan(copies, 3 * len(slots), inouts=slots)


def _pair_plan(grads):
    def copies(ins, ios, outs, send_sems, recv_sems, local_sems):
        x, y, c, _ = _place()
        sib = (x, y, 1 - c)
        sends, recvs = [], []
        for a, g in enumerate(ins):
            half = g.shape[1] // 2
            sends.append(_remote(g.at[:, pl.ds((1 - c) * half, half), :], outs[a], send_sems, recv_sems, a, sib))
            recvs.append(_remote(outs[a], outs[a], send_sems, recv_sems, a, sib))
        return sends, recvs, []

    return _Plan(copies, len(grads), ins=grads,
                 outs=[_sds((g.shape[0], g.shape[1] // 2, g.shape[2]), g.dtype) for g in grads])


def _chip_plan(parts):
    def copies(ins, ios, outs, send_sems, recv_sems, local_sems):
        x, y, c, chips = _place()
        me = _chip_id(x, y)
        sends, recvs = [], []
        for a, part in enumerate(ins):
            for j, chip in enumerate(chips):
                sends.append(_remote(part.at[_chip_id(*chip)], outs[a].at[me], send_sems, recv_sems, 3 * a + j, (*chip, c)))
                landed = outs[a].at[_chip_id(*chip)]
                recvs.append(_remote(landed, landed, send_sems, recv_sems, 3 * a + j, (*chip, c)))
        return sends, recvs, []

    return _Plan(copies, 3 * len(parts), ins=parts, outs=[_sds(p.shape, p.dtype) for p in parts])


def _all_sum(pack, fulls, name):
    R, C = pack.shape
    n = len(fulls)

    def body(p_ref, *refs):
        o_ref, halves = refs[n], refs[n + 1:2 * n + 1]
        buf, send_sems, recv_sems, pair_send, pair_recv = refs[2 * n + 1:]
        x, y, c, _ = _place()
        sib = (x, y, 1 - c)
        pair = []
        for a, full in enumerate(halves):
            H = full.shape[0] // 2
            mine = full.at[pl.ds(c * H, H)]
            cp = _remote(mine, mine, pair_send, pair_recv, a, sib)
            cp.start()
            pair.append(cp)
        me = 4 * x + 2 * y + c
        buf[me] = p_ref[...]
        cps = []
        for k in range(1, N_DEV):
            to = (x ^ (k >> 2), y ^ ((k >> 1) & 1), c ^ (k & 1))
            cp = _remote(p_ref, buf.at[me], send_sems, recv_sems, k - 1, to)
            cp.start()
            cps.append(cp)
        for k in range(1, N_DEV):
            frm = (x ^ (k >> 2), y ^ ((k >> 1) & 1), c ^ (k & 1))
            slot = buf.at[4 * frm[0] + 2 * frm[1] + frm[2]]
            _remote(slot, slot, send_sems, recv_sems, k - 1, frm).wait_recv()
        acc = buf[0]
        for k in range(1, N_DEV):
            acc = acc + buf[k]
        o_ref[...] = acc
        for cp in cps:
            cp.wait_send()
        for a, (full, cp) in enumerate(zip(halves, pair)):
            H = full.shape[0] // 2
            other = full.at[pl.ds((1 - c) * H, H)]
            _remote(other, other, pair_send, pair_recv, a, sib).wait_recv()
            cp.wait_send()

    vm = pl.BlockSpec(memory_space=pltpu.VMEM)
    res = pl.pallas_call(
        body, name=name, in_specs=[vm] + [ANY] * n, out_specs=[vm] + [ANY] * n,
        out_shape=[_sds((R, C), F32)] + [_sds(f.shape, f.dtype) for f in fulls],
        input_output_aliases={1 + a: 1 + a for a in range(n)},
        scratch_shapes=[pltpu.VMEM((N_DEV, R, C), F32), pltpu.SemaphoreType.DMA((N_DEV - 1,)),
                        pltpu.SemaphoreType.DMA((N_DEV - 1,)), pltpu.SemaphoreType.DMA((n,)),
                        pltpu.SemaphoreType.DMA((n,))])(pack, *fulls)
    return res[0], list(res[1:])


def _local_step(xs, tgt, p, ex):
    h1 = _norm_fwd(xs, p["pre_mix_norm"], "pre_mix_norm")
    proj = ex.project(h1)
    biases = _relbias_fwd(p["rel_bias"], "rel_bias_fwd")
    fw = []
    for g in range(N_GROUPS):
        res, got = _attn_fwd(proj, biases[g], g, f"attn_fwd{g}", plans=ex.carry(f"attn_fwd{g}"))
        ex.done(f"attn_fwd{g}", got)
        fw.append(res)
    (yh, o_h, states), got = _hgrn_fwd(proj, p["hgrn_lb_raw"], p["hgrn_norm"], "hgrn_fwd", plans=ex.carry("hgrn_fwd"))
    ex.done("hgrn_fwd", got)
    W_a, W_h, W_out = ex.weight("w_branch_attn"), ex.weight("w_branch_hgrn"), ex.weight("w_out")
    (y, lse, za, zh, merged), got = _branch_fwd([t[0] for t in fw], [t[1] for t in fw], yh, proj, W_a, W_h,
                                                "branch_fwd", plans=ex.carry("branch_fwd"))
    ex.done("branch_fwd", got)
    W_up, conv_w = ex.weight("w_up"), ex.weight("conv_w")
    mo, x1, h2 = _mix_out(merged, W_out, xs, p["post_mix_norm"], p["pre_ffn_norm"], "mix_out")
    u, got = _mm_nn_blk(h2, W_up, "ffn_up", tm=1024, plans=ex.carry("ffn_up"))
    ex.done("ffn_up", got)
    a, got = _conv_gelu_fwd(u, conv_w, p["conv_b"], "conv_gelu_fwd", plans=ex.carry("conv_gelu_fwd"))
    ex.done("conv_gelu_fwd", got)
    W_down = ex.weight("w_down")
    dx2, dff, g_post_ffn, loss = _loss_head(a, W_down, x1, tgt, p["post_ffn_norm"], "ffn_down_loss")

    ex.grad("w_down", _mm_tn(a, dff, "g_w_down").reshape(N_CHIPS, D_FF // N_CHIPS, D_MODEL))
    (dcg, dcv, gwg, gwv, gbg, gbv), got = _conv_gelu_bwd(u, dff, W_down, conv_w, p["conv_b"], "conv_gelu_bwd",
                                                          plans=ex.carry("conv_gelu_bwd"))
    ex.done("conv_gelu_bwd", got)
    g_conv_w = jnp.concatenate([gwg, gwv], axis=1)
    g_conv_b = jnp.concatenate([gbg, gbv], axis=1)
    du, got = _conv_input_bwd(dcg, dcv, conv_w, "conv_input_bwd", plans=ex.carry("conv_input_bwd"))
    ex.done("conv_input_bwd", got)
    dh2 = _mm_nt_blk(du, W_up, "d_ffn_in")
    ex.grad("w_up", _mm_tn_blk(h2, du, N_CHIPS, "g_w_up"))
    dx1, g_pre_ffn = _prenorm_bwd(dh2, x1, p["pre_ffn_norm"], dx2, "pre_ffn_norm_bwd")
    (dmo, dmerged, g_post_mix), got = _postnorm_bwd(dx1, mo, p["post_mix_norm"], W_out, "post_mix_norm_bwd",
                                                    plans=ex.carry("post_mix_norm_bwd"))
    ex.done("post_mix_norm_bwd", got)
    ex.grad("w_out", _mm_tn(merged, dmo, "g_w_out").reshape(N_CHIPS, D_MODEL // N_CHIPS, D_MODEL))
    (dza, dzh, dg0, dg1, dy, dyh), got = _branch_bwd(dmerged, za, zh, proj, W_a, W_h, "branch_bwd",
                                                     plans=ex.carry("branch_bwd"))
    ex.done("branch_bwd", got)
    ex.grad("w_branch_attn", _mm_tn_blk(y, dza, N_CHIPS, "g_w_branch_attn", together=True))
    ex.grad("w_branch_hgrn", _mm_tn_blk(yh, dzh, N_CHIPS, "g_w_branch_hgrn", together=True))
    dqkv, dbs = [], []
    for g in range(N_GROUPS):
        parts, db, got = _attn_bwd(proj, biases[g], lse, y, dy, g, f"attn_bwd{g}", plans=ex.carry(f"attn_bwd{g}"))
        ex.done(f"attn_bwd{g}", got)
        dqkv += parts
        dbs.append(db)
    g_rel_bias = _relbias_bwd(dbs, "rel_bias_bwd")
    dproj, g_lb_raw, g_hgrn_norm = _hgrn_bwd(proj, p["hgrn_lb_raw"], p["hgrn_norm"], o_h, states, dyh, dqkv,
                                             [dg0, dg1], "hgrn_bwd")
    for piece in W_IN_PIECES:
        g, got = _mm_tn_blk(h1, dproj, N_CHIPS, f"g_{piece}", x_cols=W_IN_ROWS[piece],
                            plans=ex.carry(f"g_{piece}"))
        ex.done(f"g_{piece}", got)
        ex.grad(piece, g)
    dh1, got = _mm_nt_blk(dproj, ex.weight("w_in"), "d_proj_in", plans=ex.carry("d_proj_in"))
    ex.done("d_proj_in", got)
    (grad_x, g_pre_mix), got = _prenorm_bwd(dh1, xs, p["pre_mix_norm"], dx1, "pre_mix_norm_bwd",
                                            plans=ex.carry("pre_mix_norm_bwd"))
    ex.done("pre_mix_norm_bwd", got)
    small = dict(pre_mix_norm=g_pre_mix, rel_bias=g_rel_bias, hgrn_lb_raw=g_lb_raw, hgrn_norm=g_hgrn_norm,
                 post_mix_norm=g_post_mix, pre_ffn_norm=g_pre_ffn, conv_w=g_conv_w, conv_b=g_conv_b,
                 post_ffn_norm=g_post_ffn)
    return loss, grad_x, small


SMALL = ("pre_mix_norm", "rel_bias", "hgrn_lb_raw", "hgrn_norm", "post_mix_norm", "pre_ffn_norm", "conv_w", "conv_b",
         "post_ffn_norm")
BIG = ("w_in", "w_up", "w_down", "w_out", "w_branch_attn", "w_branch_hgrn")
WEIGHTS = ("pre_mix_norm", "w_in", "rel_bias", "hgrn_lb_raw", "hgrn_norm", "w_branch_attn", "w_branch_hgrn", "w_out",
           "post_mix_norm", "pre_ffn_norm", "w_up", "conv_w", "conv_b", "w_down", "post_ffn_norm")
MIXER = ("w_out", "w_branch_attn", "w_branch_hgrn")

SCHEDULE = {
    "proj_in": [("gather_ici_cw", MIXER)],
    "attn_fwd0": [("gather_pass", MIXER)],
    "hgrn_fwd": [("gather_ici", ("w_up",))],
    "branch_fwd": [("gather_pass", ("w_up",))],
    "ffn_up": [("gather_ici", ("w_down",))],
    "conv_gelu_fwd": [("gather_pass", ("w_down",))],
    "conv_gelu_bwd": [("pair", ("w_down",))],
    "conv_input_bwd": [("chip", ("w_down",))],
    "post_mix_norm_bwd": [("pair", ("w_up",))],
    "attn_bwd0": [("chip", ("w_up",)), ("pair", MIXER)],
    "attn_bwd1": [("chip", MIXER)],
    "g_w_in_b": [("pair", ("w_in_a",))],
    "d_proj_in": [("chip", ("w_in_a",)), ("pair", ("w_in_b",))],
    "pre_mix_norm_bwd": [("chip", ("w_in_b",))],
}
W_IN_ROWS = dict(w_in_a=(0, 768), w_in_b=(3, 256))
W_IN_PIECES = tuple(W_IN_ROWS)
REDUCED = W_IN_PIECES + BIG[1:]


class _Exchange:
    def __init__(self, place, slots, conv_w_shard):
        self.place, self.slots, self.conv_w_shard = place, dict(slots), conv_w_shard
        self.conv_w = None
        self.g, self.from_sibling, self.pair_sums, self.arrived = {}, {}, {}, {}
        self.pending = []

    def weight(self, name):
        if name == "conv_w":
            return self.conv_w
        w = self.slots[name]
        return w.reshape(-1, D_MODEL) if name in ("w_out", "w_down") else w

    def project(self, h):
        (plan,) = self.carry("proj_in")
        proj, self.slots["w_in"], *got = _proj_gathered(h, self.slots["w_in"], self.place, "proj_in", plan=plan)
        self.done("proj_in", [got])
        return proj

    def grad(self, name, g):
        self.g[name] = g

    def carry(self, point):
        plans = []
        self.pending = SCHEDULE.get(point, [])
        for kind, names in self.pending:
            if kind in ("gather_ici", "gather_ici_cw"):
                wholes = [self.conv_w_shard] if kind == "gather_ici_cw" else []
                plans.append(_gather_ici_plan([self.slots[n] for n in names], wholes))
            elif kind == "gather_pass":
                plans.append(_gather_pass_plan([self.slots[n] for n in names]))
            elif kind == "pair":
                plans.append(_pair_plan([self.g[n] for n in names]))
            else:
                for n in names:
                    self.pair_sums[n] = _pair_sum(self.g[n], self.from_sibling[n], self.place[1:2], f"pair_sum_{n}")
                plans.append(_chip_plan([self.pair_sums[n] for n in names]))
        return plans

    def done(self, point, carried):
        for (kind, names), got in zip(self.pending, carried):
            if kind in ("gather_ici", "gather_ici_cw", "gather_pass"):
                self.slots.update(zip(names, got))
                if kind == "gather_ici_cw":
                    self.conv_w = got[len(names)].transpose(1, 0, 2).reshape(3, 2 * D_FF)
            elif kind == "pair":
                self.from_sibling.update(zip(names, got))
            else:
                self.arrived.update(zip(names, got))

    def reduced_halves(self):
        return [_chip_sum(self.arrived[n], self.pair_sums[n], self.place, f"chip_sum_{n}") for n in REDUCED]


def kernel(x, pre_mix_norm, w_in, rel_bias, hgrn_lb_raw, hgrn_norm, w_branch_attn, w_branch_hgrn, w_out, post_mix_norm, pre_ffn_norm, w_up, conv_w, conv_b, w_down, post_ffn_norm, loss_target, m_pre_mix_norm, m_w_in, m_rel_bias, m_hgrn_lb_raw, m_hgrn_norm, m_w_branch_attn, m_w_branch_hgrn, m_w_out, m_post_mix_norm, m_pre_ffn_norm, m_w_up, m_conv_w, m_conv_b, m_w_down, m_post_ffn_norm, v_pre_mix_norm, v_w_in, v_rel_bias, v_hgrn_lb_raw, v_hgrn_norm, v_w_branch_attn, v_w_branch_hgrn, v_w_out, v_post_mix_norm, v_pre_ffn_norm, v_w_up, v_conv_w, v_conv_b, v_w_down, v_post_ffn_norm):
    w = dict(pre_mix_norm=pre_mix_norm, w_in=w_in, rel_bias=rel_bias, hgrn_lb_raw=hgrn_lb_raw, hgrn_norm=hgrn_norm,
             w_branch_attn=w_branch_attn, w_branch_hgrn=w_branch_hgrn, w_out=w_out, post_mix_norm=post_mix_norm,
             pre_ffn_norm=pre_ffn_norm, w_up=w_up, conv_w=conv_w, conv_b=conv_b, w_down=w_down,
             post_ffn_norm=post_ffn_norm)
    m = dict(pre_mix_norm=m_pre_mix_norm, w_in=m_w_in, rel_bias=m_rel_bias, hgrn_lb_raw=m_hgrn_lb_raw,
             hgrn_norm=m_hgrn_norm, w_branch_attn=m_w_branch_attn, w_branch_hgrn=m_w_branch_hgrn, w_out=m_w_out,
             post_mix_norm=m_post_mix_norm, pre_ffn_norm=m_pre_ffn_norm, w_up=m_w_up, conv_w=m_conv_w,
             conv_b=m_conv_b, w_down=m_w_down, post_ffn_norm=m_post_ffn_norm)
    v = dict(pre_mix_norm=v_pre_mix_norm, w_in=v_w_in, rel_bias=v_rel_bias, hgrn_lb_raw=v_hgrn_lb_raw,
             hgrn_norm=v_hgrn_norm, w_branch_attn=v_w_branch_attn, w_branch_hgrn=v_w_branch_hgrn, w_out=v_w_out,
             post_mix_norm=v_post_mix_norm, pre_ffn_norm=v_pre_ffn_norm, w_up=v_w_up, conv_w=v_conv_w,
             conv_b=v_conv_b, w_down=v_w_down, post_ffn_norm=v_post_ffn_norm)
    shard2d = {n: (w[n][0] if w[n].ndim == 3 else w[n]) for n in WEIGHTS}
    chip = 2 * lax.axis_index("x") + lax.axis_index("y")
    core = lax.axis_index("c")

    place = jnp.stack([chip, core]).astype(jnp.int32)
    slots = {n: _cast_into_slot(shard2d[n], place, f"cast_{n}") for n in BIG}
    ex = _Exchange(place, slots, shard2d["conv_w"])
    loss, grad_x, small = _local_step(x[0], loss_target[0], {n: w[n] for n in SMALL if n != "conv_w"}, ex)

    flat = [small[n].reshape(-1) for n in SMALL] + [loss.reshape(-1)]
    sizes = [t.shape[0] for t in flat]
    summed, wholes = _all_sum(jnp.concatenate(flat).reshape(-1, LANES), ex.reduced_halves(), "sum_small")
    summed = summed.reshape(-1)
    offs = [sum(sizes[:i]) for i in range(len(sizes))]
    grads = {}
    for n, o, sz in zip(SMALL, offs, sizes):
        grads[n] = summed[o:o + sz].reshape(small[n].shape)
    loss_total = summed[offs[-1]]
    cw = 2 * D_FF // N_CHIPS
    grads["conv_w"] = lax.dynamic_slice(grads["conv_w"], (0, chip * cw), (3, cw))

    big = dict(zip(REDUCED, wholes))
    big["w_in"] = jnp.concatenate([big.pop(n) for n in W_IN_PIECES], axis=0)
    grads.update(big)

    out_g, out_d, out_m, out_v = [], [], [], []
    for n in WEIGHTS:
        d2, m2, v2 = _adamw(shard2d[n], grads[n], m[n].reshape(shard2d[n].shape), v[n].reshape(shard2d[n].shape),
                            f"adamw_{n}")
        shape = w[n].shape
        out_g.append(grads[n].reshape(shape))
        out_d.append(d2.reshape(shape))
        out_m.append(m2.reshape(shape))
        out_v.append(v2.reshape(shape))
    return (loss_total, grad_x[None], *out_g, *out_d, *out_m, *out_v)
```

```python
import functools
import math

import jax
import jax.numpy as jnp
from jax import lax
from jax.experimental import pallas as pl
from jax.experimental.pallas import tpu as pltpu

F32 = jnp.float32
BF16 = jnp.bfloat16
MESH = pl.DeviceIdType.MESH

D_MODEL = 1024
N_GROUPS = 3
DILATIONS = (1, 4, 16)
HEADS = 8
HEAD_DIM = 64
GROUP_W = HEADS * HEAD_DIM
QKV_W = N_GROUPS * 3 * GROUP_W
BLK = 128
NEG_INF = -1e30
NUM_BUCKETS = 32
MAX_EXACT = 16
MAX_DISTANCE = 2048
HG_HEADS = 4
HG_DK = 128
HG_W = HG_HEADS * HG_DK
HG_CHUNK = 32
HG_TILE = 256
IN_W = QKV_W + 4 * HG_W + 2 * D_MODEL
D_FF = 2816
EPS = 1e-6
N_CHIPS = 4
N_DEV = 8
LANES = 128

ADAM_LR, ADAM_B1, ADAM_B2, ADAM_EPS, ADAM_WD, ADAM_STEP = 0.001, 0.9, 0.999, 1e-08, 0.01, 10

VMEM_LIMIT = 56 * 1024 * 1024


def _cp(n_axes):
    return pltpu.CompilerParams(dimension_semantics=("arbitrary",) * n_axes, vmem_limit_bytes=VMEM_LIMIT)


def _sds(shape, dtype):
    return jax.ShapeDtypeStruct(tuple(shape), dtype)


def _sigmoid(v):
    return 1.0 / (1.0 + jnp.exp(-v))


def _bf(v):
    return v.astype(BF16)


def _dot(a, b, dims):
    return lax.dot_general(a, b, (dims, ((), ())), preferred_element_type=F32)


NN = ((1,), (0,))
NT = ((1,), (1,))
TN = ((0,), (0,))

ANY = pl.BlockSpec(memory_space=pl.ANY)


class _Plan:
    def __init__(self, copies, n_sems, ins=(), inouts=(), outs=()):
        self.copies, self.n_sems = copies, n_sems
        self.ins, self.inouts, self.outs = list(ins), list(inouts), list(outs)


def _call(body, plans=None, *, name, grid, in_specs, out_specs, out_shape, args, scratch_shapes=()):
    plans = list(plans or ())
    in_specs, out_specs, out_shape = list(in_specs), list(out_specs), list(out_shape)
    scratch_shapes = list(scratch_shapes)
    n_in, n_out, n_scr = len(in_specs), len(out_specs), len(scratch_shapes)
    x_in, x_out, aliases, spans = [], [], {}, []
    for p in plans:
        i0, o0 = len(x_in), len(x_out)
        x_in += p.ins
        for a in p.inouts:
            aliases[n_in + len(x_in)] = n_out + len(x_out)
            x_in.append(a)
            x_out.append(_sds(a.shape, a.dtype))
        x_out += p.outs
        spans.append((i0, len(p.ins), o0, len(p.inouts), len(p.outs)))
    sems = [pltpu.SemaphoreType.DMA((p.n_sems,)) for p in plans for _ in range(3)]

    def wrapped(*refs):
        xi = refs[n_in:n_in + len(x_in)]
        base = n_in + len(x_in)
        xo = refs[base + n_out:base + n_out + len(x_out)]
        sbase = base + n_out + len(x_out)
        xs = refs[sbase + n_scr:]
        ids = [pl.program_id(k) for k in range(len(grid))]
        first = functools.reduce(jnp.logical_and, [i == 0 for i in ids])
        last = functools.reduce(jnp.logical_and, [i == g - 1 for i, g in zip(ids, grid)])

        def descriptors(k):
            i0, ni, o0, nio, no = spans[k]
            return plans[k].copies(xi[i0:i0 + ni], xo[o0:o0 + nio], xo[o0 + nio:o0 + nio + no], *xs[3 * k:3 * k + 3])

        @pl.when(first)
        def _():
            for k in range(len(plans)):
                sends, _, local = descriptors(k)
                for cp in (*sends, *local):
                    cp.start()

        body(*refs[:n_in], *refs[base:base + n_out], *refs[sbase:sbase + n_scr])

        @pl.when(last)
        def _():
            for k in range(len(plans)):
                sends, recvs, local = descriptors(k)
                for cp in recvs:
                    cp.wait_recv()
                for cp in sends:
                    cp.wait_send()
                for cp in local:
                    cp.wait()

    res = pl.pallas_call(
        wrapped if plans else body, name=name, grid=grid, in_specs=in_specs + [ANY] * len(x_in),
        out_specs=out_specs + [ANY] * len(x_out), out_shape=out_shape + x_out, input_output_aliases=aliases,
        scratch_shapes=scratch_shapes + sems, compiler_params=_cp(len(grid)))(*args, *x_in)
    res = list(res)
    carried = [res[n_out + o0:n_out + o0 + nio + no] for (_, _, o0, nio, no) in spans]
    return res[:n_out], carried


def _mm_nn_blk(a, wg, name, tm=512, plans=None):
    M, K = a.shape
    nb, _, Nb = wg.shape

    def body(a_ref, w_ref, o_ref):
        o_ref[...] = _dot(_bf(a_ref[...]), w_ref[...], NN)

    (out,), carried = _call(
        body, plans, name=name, grid=(nb, M // tm),
        in_specs=[pl.BlockSpec((tm, K), lambda j, i: (i, 0)), pl.BlockSpec((None, K, Nb), lambda j, i: (j, 0, 0))],
        out_specs=[pl.BlockSpec((tm, Nb), lambda j, i: (i, j))],
        out_shape=[_sds((M, nb * Nb), F32)], args=(a, wg))
    return out if plans is None else (out, carried)


def _mm_nt_blk(dy, wg, name, tm=1024, plans=None):
    M = dy.shape[0]
    nb, K, Nb = wg.shape

    def body(dy_ref, w_ref, o_ref):
        j = pl.program_id(1)
        r = _dot(_bf(dy_ref[...]), w_ref[...], NT)

        @pl.when(j == 0)
        def _():
            o_ref[...] = r

        @pl.when(j > 0)
        def _():
            o_ref[...] += r

    (out,), carried = _call(
        body, plans, name=name, grid=(M // tm, nb),
        in_specs=[pl.BlockSpec((tm, Nb), lambda i, j: (i, j)), pl.BlockSpec((None, K, Nb), lambda i, j: (j, 0, 0))],
        out_specs=[pl.BlockSpec((tm, K), lambda i, j: (i, 0))],
        out_shape=[_sds((M, K), F32)], args=(dy, wg))
    return out if plans is None else (out, carried)


def _mm_tn_blk(x, dy, nb, name, tk=2048, x_cols=None, plans=None, together=False):
    T, Mx = x.shape
    xk, Mx = (0, Mx) if x_cols is None else x_cols
    Nb = dy.shape[1] // nb
    nj = nb if together else 1

    def body(x_ref, dy_ref, o_ref):
        t = pl.program_id(1)
        r = _dot(_bf(x_ref[...]), _bf(dy_ref[...]), TN)
        for j in range(nj):
            rj = r[:, j * Nb:(j + 1) * Nb]

            @pl.when(t == 0)
            def _():
                o_ref[j] = rj

            @pl.when(t > 0)
            def _():
                o_ref[j] += rj

    (out,), carried = _call(
        body, plans, name=name, grid=(nb // nj, T // tk),
        in_specs=[pl.BlockSpec((tk, Mx), lambda j, t: (t, xk)), pl.BlockSpec((tk, nj * Nb), lambda j, t: (t, j))],
        out_specs=[pl.BlockSpec((nj, Mx, Nb), lambda j, t: (j, 0, 0))],
        out_shape=[_sds((nb, Mx, Nb), F32)], args=(x, dy))
    return out if plans is None else (out, carried)


def _mm_tn(x, dy, name, tk=1024):
    T, Mx = x.shape
    N = dy.shape[1]

    def body(x_ref, dy_ref, o_ref):
        t = pl.program_id(0)
        r = _dot(_bf(x_ref[...]), _bf(dy_ref[...]), TN)

        @pl.when(t == 0)
        def _():
            o_ref[...] = r

        @pl.when(t > 0)
        def _():
            o_ref[...] += r

    return pl.pallas_call(
        body, name=name, grid=(T // tk,),
        in_specs=[pl.BlockSpec((tk, Mx), lambda t: (t, 0)), pl.BlockSpec((tk, N), lambda t: (t, 0))],
        out_specs=pl.BlockSpec((Mx, N), lambda t: (0, 0)),
        out_shape=_sds((Mx, N), F32), compiler_params=_cp(1))(x, dy)


def _tile(arr, bw, col=lambda c: 0):
    return ("tile", arr, bw, col)


def _full(arr):
    return ("full", arr)


def _out_tile(width, dtype, bw, col=lambda c: 0):
    return ("tile", width, dtype, bw, col)


def _out_acc(rows, width, bw, col=lambda c: 0):
    return ("acc", rows, width, bw, col)


def _rows_call(name, body, n_rows, tm, ncol, ins, outs, plans=None):
    in_specs, args = [], []
    for e in ins:
        if e[0] == "tile":
            _, arr, bw, col = e
            in_specs.append(pl.BlockSpec((tm, bw), functools.partial(lambda c, i, col: (i, col(c)), col=col)))
        else:
            arr = e[1]
            in_specs.append(pl.BlockSpec(arr.shape, functools.partial(lambda c, i, nd: (0,) * nd, nd=arr.ndim)))
        args.append(arr)
    out_specs, out_shape = [], []
    for e in outs:
        if e[0] == "tile":
            _, width, dtype, bw, col = e
            out_specs.append(pl.BlockSpec((tm, bw), functools.partial(lambda c, i, col: (i, col(c)), col=col)))
            out_shape.append(_sds((n_rows, width), dtype))
        else:
            _, rows, width, bw, col = e
            out_specs.append(pl.BlockSpec((rows, bw), functools.partial(lambda c, i, col: (0, col(c)), col=col)))
            out_shape.append(_sds((rows, width), F32))
    out, carried = _call(body, plans, name=name, grid=(ncol, n_rows // tm), in_specs=in_specs, out_specs=out_specs,
                         out_shape=out_shape, args=args)
    return out if plans is None else (out, carried)


def _acc(ref, val):
    i = pl.program_id(1)

    @pl.when(i == 0)
    def _():
        ref[...] = val

    @pl.when(i > 0)
    def _():
        ref[...] += val


def _rinv(z):
    return lax.rsqrt(jnp.mean(z * z, axis=-1, keepdims=True) + EPS)


def _norm_bwd(dy, zhat, r, w):
    dyw = dy * w
    return r * (dyw - zhat * jnp.mean(dyw * zhat, axis=-1, keepdims=True))


def _norm_fwd(x, w, name):
    def body(x_ref, w_ref, h_ref):
        xv = x_ref[...]
        h_ref[...] = _bf(xv * _rinv(xv) * w_ref[...])

    return _rows_call(name, body, x.shape[0], 512, 1, [_tile(x, D_MODEL), _full(w)],
                      [_out_tile(D_MODEL, BF16, D_MODEL)])[0]


def _prenorm_bwd(dh, xin, w, dres, name, plans=None):
    def body(dh_ref, x_ref, w_ref, dres_ref, dx_ref, dw_ref):
        xv = x_ref[...]
        r = _rinv(xv)
        xhat = xv * r
        dhv = dh_ref[...]
        dx_ref[...] = dres_ref[...] + _norm_bwd(dhv, xhat, r, w_ref[...])
        _acc(dw_ref, jnp.sum(dhv * xhat, axis=0, keepdims=True))

    return _rows_call(name, body, xin.shape[0], 512, 1,
                      [_tile(dh, D_MODEL), _tile(xin, D_MODEL), _full(w), _tile(dres, D_MODEL)],
                      [_out_tile(D_MODEL, F32, D_MODEL), _out_acc(1, D_MODEL, D_MODEL)], plans)


def _postnorm_bwd(dout, z, w, w_mat, name, plans=None):
    def body(do_ref, z_ref, w_ref, wm_ref, dz_ref, dm_ref, dw_ref):
        zv = z_ref[...]
        r = _rinv(zv)
        zhat = zv * r
        dov = do_ref[...]
        dz = _bf(_norm_bwd(dov, zhat, r, w_ref[...]))
        dz_ref[...] = dz
        dm_ref[...] = _dot(dz, wm_ref[...], NT)
        _acc(dw_ref, jnp.sum(dov * zhat, axis=0, keepdims=True))

    return _rows_call(name, body, z.shape[0], 512, 1,
                      [_tile(dout, D_MODEL), _tile(z, D_MODEL), _full(w), _full(w_mat)],
                      [_out_tile(D_MODEL, BF16, D_MODEL), _out_tile(D_MODEL, F32, D_MODEL),
                       _out_acc(1, D_MODEL, D_MODEL)], plans)


def _t5_bucket(dist):
    n = jnp.maximum(dist, 0)
    nf = jnp.maximum(n, 1).astype(F32)
    large = MAX_EXACT + (jnp.log(nf / MAX_EXACT) / math.log(MAX_DISTANCE / MAX_EXACT)
                         * (NUM_BUCKETS - MAX_EXACT)).astype(jnp.int32)
    large = jnp.minimum(large, NUM_BUCKETS - 1)
    return jnp.where(n < MAX_EXACT, n, large)


def _band_rel():
    return jnp.arange(BLK)[:, None] + BLK - jnp.arange(2 * BLK)[None, :]


def _band_valid():
    rel = _band_rel()
    window = (rel >= 0) & (rel <= BLK)
    first = window & (jnp.arange(2 * BLK)[None, :] >= BLK)
    return jnp.stack([first, window]).astype(F32).reshape(2, 1, BAND)


RES_UNROLL = 8
PAIR = LANES // HEAD_DIM


def _pair_lanes():
    first = lax.broadcasted_iota(jnp.int32, (1, LANES), 1) < HEAD_DIM
    return first, jnp.logical_not(first)


def _heads_per_step(d):
    return HEADS if d == 1 else LANES // HEAD_DIM


def _sub_rows(r, d):
    return pl.ds(r, BLK, stride=d) if d > 1 else pl.ds(0, BLK)


def _for_residues(d, fn):
    if d <= RES_UNROLL:
        for r in range(d):
            fn(r)
    else:
        def group(i, carry):
            for k in range(RES_UNROLL):
                fn(i * RES_UNROLL + k)
            return carry

        lax.fori_loop(0, d // RES_UNROLL, group, 0)


def _attn_specs(d, g, qblock):
    cw = _heads_per_step(d) * HEAD_DIM

    def col(part, hp):
        return (g * 3 + part) * (GROUP_W // cw) + hp

    def cur(part):
        return pl.BlockSpec((d * BLK, cw), lambda hp, n: (qblock(n), col(part, hp)))

    def prev(part):
        return pl.BlockSpec((d * BLK, cw), lambda hp, n: (jnp.maximum(qblock(n) - 1, 0), col(part, hp)))

    return cur, prev


def _attn_fwd(proj, bias, g, name, plans=None):
    S = proj.shape[0]
    d = DILATIONS[g]
    NB = S // (d * BLK)
    hps = _heads_per_step(d)

    def body(q_ref, kp_ref, kc_ref, vp_ref, vc_ref, b_ref, o_ref, lse_ref):
        hp = pl.program_id(0)
        later = jnp.minimum(pl.program_id(1), 1)

        def residue(r):
            rows = _sub_rows(r, d)
            q2 = q_ref[rows, :]
            k2 = jnp.concatenate([kp_ref[rows, :], kc_ref[rows, :]], axis=0)
            v2 = jnp.concatenate([vp_ref[rows, :], vc_ref[rows, :]], axis=0)
            outs, lses = [], []
            for pp in range(hps // PAIR):
                ps = slice(pp * LANES, (pp + 1) * LANES)
                qp, kp, vp = _bf(q2[:, ps]), _bf(k2[:, ps]), _bf(v2[:, ps])
                o_h, lse_h = [], []
                for hh, own in enumerate(_pair_lanes()):
                    s = _dot(qp, jnp.where(own, kp, 0), NT) * (HEAD_DIM ** -0.5) + b_ref[later, hp * hps + pp * PAIR + hh]
                    m = jnp.max(s, axis=-1, keepdims=True)
                    p = jnp.exp(s - m)
                    l = jnp.sum(p, axis=-1, keepdims=True)
                    o_h.append(_dot(_bf(p), vp, NN) / l)
                    lse_h.append(m + jnp.log(l))
                first = _pair_lanes()[0]
                outs.append(jnp.where(first, o_h[0], o_h[1]))
                lses.append(jnp.where(first, lse_h[0], lse_h[1]))
            o_ref[rows, :] = outs[0] if len(outs) == 1 else jnp.concatenate(outs, axis=1)
            lse_ref[rows, :] = lses[0] if len(lses) == 1 else jnp.concatenate(lses, axis=1)

        _for_residues(d, residue)

    cur, prev = _attn_specs(d, g, lambda n: n)
    out = pl.BlockSpec((d * BLK, hps * HEAD_DIM), lambda hp, n: (n, hp))
    res, carried = _call(
        body, plans, name=name, grid=(HEADS // hps, NB),
        in_specs=[cur(0), prev(1), cur(1), prev(2), cur(2),
                  pl.BlockSpec((2, HEADS, BLK, 2 * BLK), lambda hp, n: (0, 0, 0, 0))],
        out_specs=[out, out], out_shape=[_sds((S, GROUP_W), F32)] * 2,
        args=(proj, proj, proj, proj, proj, bias))
    return res if plans is None else (res, carried)


def _attn_bwd(proj, bias, lse, y, dy, g, name, plans=None):
    S = proj.shape[0]
    d = DILATIONS[g]
    NB = S // (d * BLK)
    hps = _heads_per_step(d)

    def body(q_ref, kp_ref, kc_ref, vp_ref, vc_ref, b_ref, l_ref, y_ref, dy_ref,
             dq_ref, dk_ref, dv_ref, db_ref, ck_ref, cv_ref):
        hp, n = pl.program_id(0), pl.program_id(1)

        @pl.when((hp == 0) & (n == 0))
        def _():
            db_ref[...] = jnp.zeros_like(db_ref)

        @pl.when(n == 0)
        def _():
            ck_ref[...] = jnp.zeros_like(ck_ref)
            cv_ref[...] = jnp.zeros_like(cv_ref)

        @pl.when(n < NB)
        def _():
            later = jnp.minimum(n, 1)

            def residue(r):
                rows = _sub_rows(r, d)
                q2 = q_ref[rows, :]
                k2 = jnp.concatenate([kp_ref[rows, :], kc_ref[rows, :]], axis=0)
                v2 = jnp.concatenate([vp_ref[rows, :], vc_ref[rows, :]], axis=0)
                l2, y2, dy2 = l_ref[rows, :], y_ref[rows, :], dy_ref[rows, :]
                dqs, dks, dvs = [], [], []
                for pp in range(hps // PAIR):
                    ps = slice(pp * LANES, (pp + 1) * LANES)
                    qp, kp, vp = _bf(q2[:, ps]), _bf(k2[:, ps]), _bf(v2[:, ps])
                    dyp, yp = dy2[:, ps], y2[:, ps]
                    dq_h, dk_h, dv_h = [], [], []
                    for hh, own in enumerate(_pair_lanes()):
                        head = hp * hps + pp * PAIR + hh
                        s = _dot(qp, jnp.where(own, kp, 0), NT) * (HEAD_DIM ** -0.5) + b_ref[later, head]
                        p = jnp.exp(s - l2[:, pp * LANES + hh * HEAD_DIM:pp * LANES + hh * HEAD_DIM + 1])
                        dyh = jnp.where(own, dyp, 0.0)
                        delta = jnp.sum(dyh * yp, axis=-1, keepdims=True)
                        ds = p * (_dot(_bf(dyh), vp, NT) - delta)
                        db_ref[head] += ds
                        dsb = _bf(ds * (HEAD_DIM ** -0.5))
                        dq_h.append(_dot(dsb, kp, NN))
                        dk_h.append(_dot(dsb, qp, TN))
                        dv_h.append(_dot(_bf(p), _bf(dyp), TN))
                    first = _pair_lanes()[0]
                    dqs.append(jnp.where(first, dq_h[0], dq_h[1]))
                    dks.append(jnp.where(first, dk_h[0], dk_h[1]))
                    dvs.append(jnp.where(first, dv_h[0], dv_h[1]))
                dkb = dks[0] if len(dks) == 1 else jnp.concatenate(dks, axis=1)
                dvb = dvs[0] if len(dvs) == 1 else jnp.concatenate(dvs, axis=1)
                dq_ref[rows, :] = dqs[0] if len(dqs) == 1 else jnp.concatenate(dqs, axis=1)
                dk_ref[rows, :] = ck_ref[rows, :] + dkb[:BLK]
                dv_ref[rows, :] = cv_ref[rows, :] + dvb[:BLK]
                ck_ref[rows, :] = dkb[BLK:]
                cv_ref[rows, :] = dvb[BLK:]

            _for_residues(d, residue)

        @pl.when(n == NB)
        def _():
            dk_ref[...] = ck_ref[...]
            dv_ref[...] = cv_ref[...]

    def qn(n):
        return jnp.minimum(n, NB - 1)

    cur, prev = _attn_specs(d, g, qn)
    cw = hps * HEAD_DIM
    row = pl.BlockSpec((d * BLK, cw), lambda hp, n: (qn(n), hp))
    done = pl.BlockSpec((d * BLK, cw), lambda hp, n: (jnp.maximum(n - 1, 0), hp))
    (dq, dk, dv, db), carried = _call(
        body, plans, name=name, grid=(HEADS // hps, NB + 1),
        in_specs=[cur(0), prev(1), cur(1), prev(2), cur(2),
                  pl.BlockSpec((2, HEADS, BLK, 2 * BLK), lambda hp, n: (0, 0, 0, 0)), row, row, row],
        out_specs=[row, done, done, pl.BlockSpec((HEADS, BLK, 2 * BLK), lambda hp, n: (0, 0, 0))],
        out_shape=[_sds((S, GROUP_W), F32)] * 3 + [_sds((HEADS, BLK, 2 * BLK), F32)],
        scratch_shapes=[pltpu.VMEM((d * BLK, cw), F32)] * 2,
        args=(proj, proj, proj, proj, proj, bias, lse, y, dy))
    return ([dq, dk, dv], db) if plans is None else ([dq, dk, dv], db, carried)


BAND = BLK * 2 * BLK


def _bucket_onehot():
    buckets = jnp.stack([_t5_bucket(_band_rel() * d) for d in DILATIONS]).reshape(N_GROUPS, 1, BAND)
    return (buckets == jnp.arange(NUM_BUCKETS).reshape(1, NUM_BUCKETS, 1)).astype(F32)


def _relbias_fwd(rel_bias, name):
    table = rel_bias.reshape(NUM_BUCKETS, N_GROUPS, HEADS).transpose(1, 0, 2)

    def body(t_ref, oh_ref, valid_ref, o_ref):
        bias = lax.dot_general(t_ref[...], oh_ref[...], (TN, ((), ())), preferred_element_type=F32,
                               precision=lax.Precision.HIGHEST)
        for k in range(2):
            o_ref[k] = jnp.where(valid_ref[k] > 0.5, bias, NEG_INF)

    out = pl.pallas_call(
        body, name=name, grid=(N_GROUPS,),
        in_specs=[pl.BlockSpec((None, NUM_BUCKETS, HEADS), lambda g: (g, 0, 0)),
                  pl.BlockSpec((None, NUM_BUCKETS, BAND), lambda g: (g, 0, 0)),
                  pl.BlockSpec((2, 1, BAND), lambda g: (0, 0, 0))],
        out_specs=pl.BlockSpec((None, 2, HEADS, BAND), lambda g: (g, 0, 0, 0)),
        out_shape=_sds((N_GROUPS, 2, HEADS, BAND), F32), compiler_params=_cp(1))(table, _bucket_onehot(), _band_valid())
    return out.reshape(N_GROUPS, 2, HEADS, BLK, 2 * BLK)


def _relbias_bwd(dbs, name):
    band = BAND
    onehot = _bucket_onehot()
    dbf = jnp.stack([db.reshape(HEADS, band) for db in dbs])

    def body(oh_ref, db_ref, o_ref):
        o_ref[...] = lax.dot_general(oh_ref[...], db_ref[...], (NT, ((), ())), preferred_element_type=F32,
                                     precision=lax.Precision.HIGHEST)

    out = pl.pallas_call(
        body, name=name, grid=(N_GROUPS,),
        in_specs=[pl.BlockSpec((None, NUM_BUCKETS, band), lambda g: (g, 0, 0)),
                  pl.BlockSpec((None, HEADS, band), lambda g: (g, 0, 0))],
        out_specs=pl.BlockSpec((None, NUM_BUCKETS, HEADS), lambda g: (g, 0, 0)),
        out_shape=_sds((N_GROUPS, NUM_BUCKETS, HEADS), F32), compiler_params=_cp(1))(onehot, dbf)
    return out.transpose(1, 0, 2).reshape(NUM_BUCKETS, N_GROUPS * HEADS)


def _chunk_pos(shape):
    return lax.broadcasted_iota(jnp.int32, shape, 0) % HG_CHUNK


def _chunk_cumsum(v):
    pos = _chunk_pos(v.shape)
    s = 1
    while s < HG_CHUNK:
        v = v + jnp.where(pos >= s, pltpu.roll(v, s, 0), 0.0)
        s *= 2
    return v


def _chunk_rev_cumsum(v):
    pos = _chunk_pos(v.shape)
    n = v.shape[0]
    s = 1
    while s < HG_CHUNK:
        v = v + jnp.where(pos < HG_CHUNK - s, pltpu.roll(v, n - s, 0), 0.0)
        s *= 2
    return v


def _lower_bound(raw):
    a0, a1 = raw[0:1], raw[1:2]
    m = jnp.maximum(a0, a1)
    e0, e1 = jnp.exp(a0 - m), jnp.exp(a1 - m)
    return e0 / (e0 + e1)


def _hg_gates(qr, fr, lb):
    sf = _sigmoid(fr)
    f = lb + (1.0 - lb) * sf
    sq = _sigmoid(qr)
    return qr * sq, sq, f, sf


HG_COL0 = QKV_W // HG_W


def _hgrn_fwd(proj, lb_raw, nw, name, plans=None):
    S = proj.shape[0]
    ncs = HG_TILE // HG_CHUNK
    tril = jnp.tril(jnp.ones((HG_CHUNK, HG_CHUNK), dtype=bool))

    def body(q_ref, f_ref, i_ref, og_ref, lb_ref, nw_ref, y_ref, o_ref, st_ref, state):
        @pl.when(pl.program_id(0) == 0)
        def _():
            state[...] = jnp.zeros_like(state)

        lb = _lower_bound(lb_ref[...])
        q, _, f, _ = _hg_gates(q_ref[...], f_ref[...], lb)
        k = 1.0 - f
        G = _chunk_cumsum(jnp.log(f))
        row = lax.broadcasted_iota(jnp.int32, (HG_CHUNK, HG_CHUNK), 0)
        col = lax.broadcasted_iota(jnp.int32, (HG_CHUNK, HG_CHUNK), 1)
        heads = [slice(h * HG_DK, (h + 1) * HG_DK) for h in range(HG_HEADS)]
        sts = [state[h] for h in range(HG_HEADS)]
        for c in range(ncs):
            cs = slice(c * HG_CHUNK, (c + 1) * HG_CHUNK)
            for h, hs in enumerate(heads):
                Gc = G[cs, hs]
                gl = Gc[HG_CHUNK - 1:HG_CHUNK]
                qt = _bf(q[cs, hs] * jnp.exp(Gc))
                kt = _bf(k[cs, hs] * jnp.exp(-Gc))
                kd = _bf(k[cs, hs] * jnp.exp(gl - Gc))
                v = _bf(i_ref[cs, hs])
                A = jnp.where(row >= col, _dot(qt, kt, NT), 0.0)
                o_ref[cs, hs] = _dot(_bf(A), v, NN) + _dot(qt, _bf(sts[h]), NT)
                st_ref[c, h] = sts[h]
                sts[h] = sts[h] * jnp.exp(gl) + _dot(v, kd, TN)
        for h, hs in enumerate(heads):
            state[h] = sts[h]
            oh = o_ref[:, hs]
            og = og_ref[:, hs]
            y_ref[:, hs] = oh * _rinv(oh) * nw_ref[...] * (og * _sigmoid(og))

    def colspec(j):
        return pl.BlockSpec((HG_TILE, HG_W), lambda i: (i, HG_COL0 + j))

    res, carried = _call(
        body, plans, name=name, grid=(S // HG_TILE,),
        in_specs=[colspec(0), colspec(1), colspec(2), colspec(3),
                  pl.BlockSpec((2, HG_W), lambda i: (0, 0)), pl.BlockSpec((1, HG_DK), lambda i: (0, 0))],
        out_specs=[pl.BlockSpec((HG_TILE, HG_W), lambda i: (i, 0))] * 2
        + [pl.BlockSpec((ncs, HG_HEADS, HG_DK, HG_DK), lambda i: (i, 0, 0, 0))],
        out_shape=[_sds((S, HG_W), F32)] * 2 + [_sds((S // HG_CHUNK, HG_HEADS, HG_DK, HG_DK), F32)],
        scratch_shapes=[pltpu.VMEM((HG_HEADS, HG_DK, HG_DK), F32)],
        args=(proj, proj, proj, proj, lb_raw, nw))
    return res if plans is None else (res, carried)


def _hgrn_bwd(proj, lb_raw, nw, o, states, dy, d_attn, d_gates, name):
    S = proj.shape[0]
    ncs = HG_TILE // HG_CHUNK
    nt = S // HG_TILE
    n_a, n_g = len(d_attn), len(d_gates)
    own = [slice(QKV_W + j * HG_W, QKV_W + (j + 1) * HG_W) for j in range(4)]

    def body(q_ref, f_ref, i_ref, og_ref, lb_ref, nw_ref, o_ref, st_ref, dy_ref, *rest):
        attn_refs, gate_refs = rest[:n_a], rest[n_a:n_a + n_g]
        dp_ref, dlb_ref, dnw_ref, dstate, do_s, dG_s, dgl_s, dk_s, dlb_s = rest[n_a + n_g:]
        dq_ref, df_ref, di_ref, dog_ref = (dp_ref.at[:, cols] for cols in own)
        step = pl.program_id(0)
        for k, a_ref in enumerate(attn_refs):
            dp_ref[:, k * GROUP_W:(k + 1) * GROUP_W] = _bf(a_ref[...])
        for k, g_ref in enumerate(gate_refs):
            dp_ref[:, QKV_W + 4 * HG_W + k * D_MODEL:QKV_W + 4 * HG_W + (k + 1) * D_MODEL] = g_ref[...]

        @pl.when(step == 0)
        def _():
            dstate[...] = jnp.zeros_like(dstate)
            dlb_s[...] = jnp.zeros_like(dlb_s)
            dnw_ref[...] = jnp.zeros_like(dnw_ref)

        lb = _lower_bound(lb_ref[...])
        qr = q_ref[...]
        q, sq, f, sf = _hg_gates(qr, f_ref[...], lb)
        k = 1.0 - f
        G = _chunk_cumsum(jnp.log(f))
        nwv = nw_ref[...]
        row = lax.broadcasted_iota(jnp.int32, (HG_CHUNK, HG_CHUNK), 0)
        col = lax.broadcasted_iota(jnp.int32, (HG_CHUNK, HG_CHUNK), 1)
        for h in range(HG_HEADS):
            hs = slice(h * HG_DK, (h + 1) * HG_DK)
            oh = o_ref[:, hs]
            r = _rinv(oh)
            ohat = oh * r
            og = og_ref[:, hs]
            sg = _sigmoid(og)
            dyh = dy_ref[:, hs]
            don = dyh * (og * sg)
            dog_ref[:, hs] = _bf(dyh * (ohat * nwv) * (sg * (1.0 + og * (1.0 - sg))))
            dnw_ref[...] += jnp.sum(don * ohat, axis=0, keepdims=True)
            do_s[:, hs] = _norm_bwd(don, ohat, r, nwv)
        dsts = [dstate[h] for h in range(HG_HEADS)]
        for c in reversed(range(ncs)):
            cs = slice(c * HG_CHUNK, (c + 1) * HG_CHUNK)
            for h in range(HG_HEADS):
                hs = slice(h * HG_DK, (h + 1) * HG_DK)
                dst = dsts[h]
                Gc = G[cs, hs]
                gl = Gc[HG_CHUNK - 1:HG_CHUNK]
                eG, enG, edG, egl = jnp.exp(Gc), jnp.exp(-Gc), jnp.exp(gl - Gc), jnp.exp(gl)
                qt, kt, kd = q[cs, hs] * eG, k[cs, hs] * enG, k[cs, hs] * edG
                qtb, ktb, kdb = _bf(qt), _bf(kt), _bf(kd)
                v = _bf(i_ref[cs, hs])
                do = _bf(do_s[cs, hs])
                st = st_ref[c, h]
                dstb = _bf(dst)
                A = jnp.where(row >= col, _dot(qtb, ktb, NT), 0.0)
                dA = _bf(jnp.where(row >= col, _dot(do, v, NT), 0.0))
                di_ref[cs, hs] = _bf(_dot(_bf(A), do, TN) + _dot(kdb, dstb, NT))
                dqt = _dot(dA, ktb, NN) + _dot(do, _bf(st), NN)
                dkt = _dot(dA, qtb, TN)
                dkd = _dot(v, dstb, NN)
                dgl = egl * jnp.sum(st * dst, axis=0, keepdims=True) + jnp.sum(dkd * kd, axis=0, keepdims=True)
                dsts[h] = dst * egl + _dot(do, qtb, TN)
                dq_ref[cs, hs] = _bf(dqt * eG * (sq[cs, hs] * (1.0 + qr[cs, hs] * (1.0 - sq[cs, hs]))))
                dk_s[cs, hs] = dkt * enG + dkd * edG
                dG_s[cs, hs] = dqt * qt - dkt * kt - dkd * kd
                dgl_s[cs, hs] = jnp.broadcast_to(dgl, (HG_CHUNK, HG_DK))
        for h in range(HG_HEADS):
            dstate[h] = dsts[h]
        dg = _chunk_rev_cumsum(dG_s[...]) + dgl_s[...]
        dfv = dg / f - dk_s[...]
        df_ref[...] = _bf(dfv * (1.0 - lb) * sf * (1.0 - sf))
        dlb_s[...] += jnp.sum(dfv * (1.0 - sf), axis=0, keepdims=True)

        @pl.when(step == nt - 1)
        def _():
            t = dlb_s[...] * lb * (1.0 - lb)
            dlb_ref[...] = jnp.concatenate([t, -t], axis=0)

    def colspec(j):
        return pl.BlockSpec((HG_TILE, HG_W), lambda i: (nt - 1 - i, HG_COL0 + j))

    def rows(width):
        return pl.BlockSpec((HG_TILE, width), lambda i: (nt - 1 - i, 0))

    tile = rows(HG_W)
    return pl.pallas_call(
        body, name=name, grid=(nt,),
        in_specs=[colspec(0), colspec(1), colspec(2), colspec(3),
                  pl.BlockSpec((2, HG_W), lambda i: (0, 0)), pl.BlockSpec((1, HG_DK), lambda i: (0, 0)),
                  tile, pl.BlockSpec((ncs, HG_HEADS, HG_DK, HG_DK), lambda i: (nt - 1 - i, 0, 0, 0)), tile]
        + [rows(GROUP_W)] * n_a + [rows(D_MODEL)] * n_g,
        out_specs=[rows(IN_W), pl.BlockSpec((2, HG_W), lambda i: (0, 0)), pl.BlockSpec((1, HG_DK), lambda i: (0, 0))],
        out_shape=[_sds((S, IN_W), BF16), _sds((2, HG_W), F32), _sds((1, HG_DK), F32)],
        scratch_shapes=[pltpu.VMEM((HG_HEADS, HG_DK, HG_DK), F32)] + [pltpu.VMEM((HG_TILE, HG_W), F32)] * 4
        + [pltpu.VMEM((1, HG_W), F32)],
        compiler_params=_cp(1))(proj, proj, proj, proj, lb_raw, nw, o, states, dy, *d_attn, *d_gates)


GATE_COL0 = (QKV_W + 4 * HG_W) // GROUP_W
HALF_D = D_MODEL // 2


def _gate_tiles(proj):
    return [_tile(proj, HALF_D, functools.partial(lambda c, k: GATE_COL0 + k, k=k)) for k in range(4)]


def _gates(g_refs):
    s0 = _sigmoid(jnp.concatenate([g_refs[0][...], g_refs[1][...]], axis=1))
    s1 = _sigmoid(jnp.concatenate([g_refs[2][...], g_refs[3][...]], axis=1))
    return s0, s1


def _branch_fwd(os_, lses, yh, proj, w_a, w_h, name, plans=None):
    nb = w_a.shape[0]

    def body(o0, o1, o2, l0, l1, l2, yh_ref, g0a, g0b, g1a, g1b, wa_ref, wh_ref,
             y_ref, lse_ref, za_ref, zh_ref, m_ref):
        a, b, c = l0[...], l1[...], l2[...]
        m = jnp.maximum(jnp.maximum(a, b), c)
        ea, eb, ec = jnp.exp(a - m), jnp.exp(b - m), jnp.exp(c - m)
        den = ea + eb + ec
        y = (ea * o0[...] + eb * o1[...] + ec * o2[...]) / den
        y_ref[...] = y
        lse_ref[...] = m + jnp.log(den)
        yb, yhb = _bf(y), _bf(yh_ref[...])
        za = jnp.concatenate([_dot(yb, wa_ref[j], NN) for j in range(nb)], axis=1)
        zh = jnp.concatenate([_dot(yhb, wh_ref[j], NN) for j in range(nb)], axis=1)
        s0, s1 = _gates((g0a, g0b, g1a, g1b))
        za_ref[...] = za
        zh_ref[...] = zh
        m_ref[...] = _bf(s0 * za + s1 * zh)

    return _rows_call(name, body, yh.shape[0], 512, 1,
                      [*[_tile(t, GROUP_W) for t in (*os_, *lses)], _tile(yh, HG_W), *_gate_tiles(proj),
                       _full(w_a), _full(w_h)],
                      [_out_tile(GROUP_W, F32, GROUP_W)] * 2 + [_out_tile(D_MODEL, F32, D_MODEL)] * 2
                      + [_out_tile(D_MODEL, BF16, D_MODEL)], plans)


def _branch_bwd(dm, za, zh, proj, w_a, w_h, name, plans=None):
    nb, _, Nb = w_a.shape

    def body(dm_ref, za_ref, zh_ref, g0a, g0b, g1a, g1b, wa_ref, wh_ref,
             dza_ref, dzh_ref, dg0_ref, dg1_ref, dy_ref, dyh_ref):
        dmv = dm_ref[...]
        s0, s1 = _gates((g0a, g0b, g1a, g1b))
        dza, dzh = _bf(dmv * s0), _bf(dmv * s1)
        dza_ref[...] = dza
        dzh_ref[...] = dzh
        dg0_ref[...] = _bf(dmv * za_ref[...] * s0 * (1.0 - s0))
        dg1_ref[...] = _bf(dmv * zh_ref[...] * s1 * (1.0 - s1))
        dy_ref[...] = sum(_dot(dza[:, j * Nb:(j + 1) * Nb], wa_ref[j], NT) for j in range(nb))
        dyh_ref[...] = sum(_dot(dzh[:, j * Nb:(j + 1) * Nb], wh_ref[j], NT) for j in range(nb))

    return _rows_call(name, body, za.shape[0], 512, 1,
                      [_tile(dm, D_MODEL), _tile(za, D_MODEL), _tile(zh, D_MODEL), *_gate_tiles(proj),
                       _full(w_a), _full(w_h)],
                      [_out_tile(D_MODEL, BF16, D_MODEL)] * 4 + [_out_tile(GROUP_W, F32, GROUP_W),
                                                                 _out_tile(HG_W, F32, HG_W)], plans)


def _mix_out(merged, w_out, x, w_post, w_pre, name):
    def body(m_ref, wo_ref, x_ref, wp_ref, wf_ref, mo_ref, x1_ref, h2_ref):
        z = _dot(m_ref[...], wo_ref[...], NN)
        mo_ref[...] = z
        x1 = x_ref[...] + z * _rinv(z) * wp_ref[...]
        x1_ref[...] = x1
        h2_ref[...] = _bf(x1 * _rinv(x1) * wf_ref[...])

    return _rows_call(name, body, x.shape[0], 512, 1,
                      [_tile(merged, D_MODEL), _full(w_out), _tile(x, D_MODEL), _full(w_post), _full(w_pre)],
                      [_out_tile(D_MODEL, F32, D_MODEL), _out_tile(D_MODEL, F32, D_MODEL),
                       _out_tile(D_MODEL, BF16, D_MODEL)])


def _loss_head(a, w_down, x1, tgt, w, name):
    def body(a_ref, wd_ref, x1_ref, t_ref, w_ref, dx_ref, df_ref, dw_ref, loss_ref):
        z = _dot(a_ref[...], wd_ref[...], NN)
        r = _rinv(z)
        zhat = z * r
        wv = w_ref[...]
        e = x1_ref[...] + zhat * wv - t_ref[...]
        dx = e * (1.0 / D_MODEL)
        dx_ref[...] = dx
        df_ref[...] = _bf(_norm_bwd(dx, zhat, r, wv))
        _acc(dw_ref, jnp.sum(dx * zhat, axis=0, keepdims=True))
        part = 0.5 * jnp.sum(jnp.sum(e * e, axis=1, keepdims=True), axis=0, keepdims=True) * (1.0 / D_MODEL)
        _acc(loss_ref, jnp.broadcast_to(part, (1, LANES)))

    return _rows_call(name, body, x1.shape[0], 512, 1,
                      [_tile(a, D_FF), _full(w_down), _tile(x1, D_MODEL), _tile(tgt, D_MODEL), _full(w)],
                      [_out_tile(D_MODEL, F32, D_MODEL), _out_tile(D_MODEL, BF16, D_MODEL),
                       _out_acc(1, D_MODEL, D_MODEL), _out_acc(1, LANES, LANES)])


CONV_CB = D_FF // 2
CONV_TM = 512
HALO = 8
SQRT_HALF = 0.7071067811865476
INV_SQRT_2PI = 0.3989422804014327


CONV_RS = 32


def _lane_tiles():
    return [slice(k * LANES, (k + 1) * LANES) for k in range(CONV_CB // LANES)]


def _strip_start(i):
    return pl.multiple_of(i * CONV_RS, CONV_RS)


def _strip_taps(u_ref, halo_ref, r0, cs, first_strip, first_tile):
    if first_strip:
        before = jnp.where(first_tile, 0.0, halo_ref[:, cs])
        blk = jnp.concatenate([before, u_ref[0:CONV_RS, cs]], axis=0)
    else:
        blk = u_ref[pl.ds(pl.multiple_of(r0 - HALO, HALO), CONV_RS + HALO), cs]
    return pltpu.roll(blk, 2, 0)[HALO:], pltpu.roll(blk, 1, 0)[HALO:], blk[HALO:]


def _conv(taps, w_ref, b_ref, cs):
    return b_ref[:, cs] + w_ref[0:1, cs] * taps[0] + w_ref[1:2, cs] * taps[1] + w_ref[2:3, cs] * taps[2]


def _conv_specs(tm):
    nh = tm // HALO
    nc = D_FF // CONV_CB

    def tile(off):
        return pl.BlockSpec((tm, CONV_CB), lambda c, i: (i, off + c))

    def halo(off):
        return pl.BlockSpec((HALO, CONV_CB), lambda c, i: (jnp.maximum(i * nh - 1, 0), off + c))

    def small(rows, off):
        return pl.BlockSpec((rows, CONV_CB), lambda c, i: (0, off + c))

    return nc, tile, halo, small


def _conv_gelu_fwd(u, cw, cb, name, plans=None):
    S = u.shape[0]
    tm = CONV_TM
    nc, tile, halo, small = _conv_specs(tm)

    def body(ug, hg, uv, hv, wg, wv, bg, bv, a_ref):
        first_tile = pl.program_id(1) == 0

        def strip(r0, first_strip):
            for cs in _lane_tiles():
                cg = _conv(_strip_taps(ug, hg, r0, cs, first_strip, first_tile), wg, bg, cs)
                cv = _conv(_strip_taps(uv, hv, r0, cs, first_strip, first_tile), wv, bv, cs)
                a_ref[pl.ds(r0, CONV_RS), cs] = _bf(0.5 * cg * (1.0 + lax.erf(cg * SQRT_HALF)) * cv)

        strip(0, True)
        lax.fori_loop(1, tm // CONV_RS, lambda k, c: (strip(_strip_start(k), False), c)[1], 0)

    (a,), carried = _call(
        body, plans, name=name, grid=(nc, S // tm),
        in_specs=[tile(0), halo(0), tile(nc), halo(nc), small(3, 0), small(3, nc), small(1, 0), small(1, nc)],
        out_specs=[tile(0)], out_shape=[_sds((S, D_FF), BF16)], args=(u, u, u, u, cw, cw, cb, cb))
    return a if plans is None else (a, carried)


def _conv_gelu_bwd(u, dff, w_down, cw, cb, name, plans=None):
    S = u.shape[0]
    tm = CONV_TM
    nt = S // tm
    nc, tile, halo, small = _conv_specs(tm)

    def body(ug, hg, uv, hv, wg, wv, bg, bv, dff_ref, wd_ref, dcg_ref, dcv_ref, dwg_ref, dwv_ref, dbg_ref, dbv_ref,
             acc, da_ref):
        i = pl.program_id(1)
        first_tile = i == 0
        da_ref[...] = _dot(dff_ref[...], wd_ref[...], NT)

        @pl.when(first_tile)
        def _():
            acc[...] = jnp.zeros_like(acc)

        def strip(r0, first_strip):
            rows = pl.ds(r0, CONV_RS)
            for cs in _lane_tiles():
                tg = _strip_taps(ug, hg, r0, cs, first_strip, first_tile)
                tv = _strip_taps(uv, hv, r0, cs, first_strip, first_tile)
                cg = _conv(tg, wg, bg, cs)
                cv = _conv(tv, wv, bv, cs)
                phi = 0.5 * (1.0 + lax.erf(cg * SQRT_HALF))
                dav = da_ref[rows, cs]
                dcg = dav * cv * (phi + cg * jnp.exp(-0.5 * cg * cg) * INV_SQRT_2PI)
                dcv = dav * (cg * phi)
                dcg_ref[rows, cs] = dcg
                dcv_ref[rows, cs] = dcv
                for half, (dc, taps) in enumerate(((dcg, tg), (dcv, tv))):
                    for j in range(3):
                        acc[4 * half + j, :, cs] += dc * taps[j]
                    acc[4 * half + 3, :, cs] += dc

        strip(0, True)
        lax.fori_loop(1, tm // CONV_RS, lambda k, c: (strip(_strip_start(k), False), c)[1], 0)

        @pl.when(i == nt - 1)
        def _():
            for half, (dw_ref, db_ref) in enumerate(((dwg_ref, dbg_ref), (dwv_ref, dbv_ref))):
                for j in range(3):
                    dw_ref[j:j + 1, :] = jnp.sum(acc[4 * half + j], axis=0, keepdims=True)
                db_ref[...] = jnp.sum(acc[4 * half + 3], axis=0, keepdims=True)

    res, carried = _call(
        body, plans, name=name, grid=(nc, nt),
        in_specs=[tile(0), halo(0), tile(nc), halo(nc), small(3, 0), small(3, nc), small(1, 0), small(1, nc),
                  pl.BlockSpec((tm, D_MODEL), lambda c, i: (i, 0)), pl.BlockSpec((CONV_CB, D_MODEL), lambda c, i: (c, 0))],
        out_specs=[tile(0), tile(0), small(3, 0), small(3, 0), small(1, 0), small(1, 0)],
        out_shape=[_sds((S, D_FF), F32)] * 2 + [_sds((3, D_FF), F32)] * 2 + [_sds((1, D_FF), F32)] * 2,
        scratch_shapes=[pltpu.VMEM((8, CONV_RS, CONV_CB), F32), pltpu.VMEM((tm, CONV_CB), F32)],
        args=(u, u, u, u, cw, cw, cb, cb, dff, w_down))
    return res if plans is None else (res, carried)


def _conv_input_bwd(dcg, dcv, cw, name, plans=None):
    S = dcg.shape[0]
    tm = CONV_TM // 2
    nh = tm // HALO
    nt = S // tm
    n = CONV_RS + HALO
    tile = pl.BlockSpec((tm, D_FF), lambda i: (i, 0))
    nxt = pl.BlockSpec((HALO, D_FF), lambda i: (jnp.minimum((i + 1) * nh, S // HALO - 1), 0))

    def body(g_ref, ng_ref, v_ref, nv_ref, w_ref, du_ref):
        last_tile = pl.program_id(0) == nt - 1

        def strip(r0, last_strip):
            for half, (dc_ref, n_ref) in enumerate(((g_ref, ng_ref), (v_ref, nv_ref))):
                for k in range(D_FF // LANES):
                    cs = slice(k * LANES, (k + 1) * LANES)
                    ws = slice(half * D_FF + k * LANES, half * D_FF + (k + 1) * LANES)
                    if last_strip:
                        after = jnp.where(last_tile, 0.0, n_ref[:, cs])
                        blk = jnp.concatenate([dc_ref[tm - CONV_RS:tm, cs], after], axis=0)
                    else:
                        blk = dc_ref[pl.ds(r0, n), cs]
                    d1 = pltpu.roll(blk, n - 1, 0)[:CONV_RS]
                    d2 = pltpu.roll(blk, n - 2, 0)[:CONV_RS]
                    du_ref[pl.ds(r0, CONV_RS), ws] = _bf(w_ref[2:3, ws] * blk[:CONV_RS] + w_ref[1:2, ws] * d1
                                                         + w_ref[0:1, ws] * d2)

        lax.fori_loop(0, tm // CONV_RS - 1, lambda k, c: (strip(_strip_start(k), False), c)[1], 0)
        strip(tm - CONV_RS, True)

    (du,), carried = _call(
        body, plans, name=name, grid=(nt,),
        in_specs=[tile, nxt, tile, nxt, pl.BlockSpec((3, 2 * D_FF), lambda i: (0, 0))],
        out_specs=[pl.BlockSpec((tm, 2 * D_FF), lambda i: (i, 0))], out_shape=[_sds((S, 2 * D_FF), BF16)],
        args=(dcg, dcg, dcv, dcv, cw))
    return du if plans is None else (du, carried)


def _row_tile(n, cap):
    best = n
    for t in range(16, cap + 1, 16):
        if n % t == 0:
            best = t
    return best if best <= cap else n


def _rows_for_bytes(nbytes, cols):
    return max(16, nbytes // (4 * cols) // 16 * 16)


def _adamw(w, g, m, v, name):
    R, C = w.shape
    tr = _row_tile(R, _rows_for_bytes(2 << 20, C))

    def body(w_ref, g_ref, m_ref, v_ref, d_ref, nm_ref, nv_ref):
        gv = g_ref[...]
        nm = ADAM_B1 * m_ref[...] + (1.0 - ADAM_B1) * gv
        nv = ADAM_B2 * v_ref[...] + (1.0 - ADAM_B2) * (gv * gv)
        m_hat = nm / (1.0 - ADAM_B1 ** ADAM_STEP)
        v_hat = nv / (1.0 - ADAM_B2 ** ADAM_STEP)
        d_ref[...] = -ADAM_LR * (m_hat / (jnp.sqrt(v_hat) + ADAM_EPS) + ADAM_WD * w_ref[...])
        nm_ref[...] = nm
        nv_ref[...] = nv

    spec = pl.BlockSpec((tr, C), lambda i: (i, 0))
    return pl.pallas_call(body, name=name, grid=(R // tr,), in_specs=[spec] * 4, out_specs=[spec] * 3,
                          out_shape=[_sds((R, C), F32)] * 3, compiler_params=_cp(1))(w, g, m, v)


def _pair_sum(gfull, rcv, c_idx, name):
    nb, R, C = gfull.shape
    half = R // 2
    tr = _row_tile(half, _rows_for_bytes(2 << 20, C))
    nt = half // tr

    def body(c_ref, g_ref, r_ref, o_ref):
        o_ref[...] = _bf(g_ref[...] + r_ref[...])

    return pl.pallas_call(
        body, name=name,
        grid_spec=pltpu.PrefetchScalarGridSpec(
            num_scalar_prefetch=1, grid=(nb, nt),
            in_specs=[pl.BlockSpec((None, tr, C), lambda j, i, c_ref: (j, c_ref[0] * nt + i, 0)),
                      pl.BlockSpec((None, tr, C), lambda j, i, c_ref: (j, i, 0))],
            out_specs=pl.BlockSpec((None, tr, C), lambda j, i, c_ref: (j, i, 0))),
        out_shape=_sds((nb, half, C), BF16), compiler_params=_cp(2))(c_idx, gfull, rcv)


def _chip_sum(arrived, own, place, name):
    nb, H, C = arrived.shape
    tr = _row_tile(H, _rows_for_bytes(2 << 20, C))
    nt = H // tr

    def body(pl_ref, *refs):
        o_ref = refs[nb + 1]
        me = pl_ref[0]
        acc = None
        for k in range(nb):
            term = jnp.where(me == k, refs[nb][...], refs[k][...]).astype(F32)
            acc = term if acc is None else acc + term
        o_ref[...] = acc

    def other(k):
        return pl.BlockSpec((None, tr, C), lambda i, p: (jnp.where(p[0] == k, (k + 1) % nb, k), i, 0))

    return pl.pallas_call(
        body, name=name,
        grid_spec=pltpu.PrefetchScalarGridSpec(
            num_scalar_prefetch=1, grid=(nt,),
            in_specs=[other(k) for k in range(nb)] + [pl.BlockSpec((None, tr, C), lambda i, p: (p[0], i, 0))],
            out_specs=pl.BlockSpec((tr, C), lambda i, p: (p[1] * nt + i, 0))),
        out_shape=_sds((2 * H, C), F32), compiler_params=_cp(1))(place, *([arrived] * nb), own)


def _cast_into_slot(shard, place, name):
    R, C = shard.shape
    tr = _row_tile(R, 256)

    def body(pl_ref, s_ref, o_ref):
        o_ref[...] = _bf(s_ref[...])

    return pl.pallas_call(
        body, name=name,
        grid_spec=pltpu.PrefetchScalarGridSpec(
            num_scalar_prefetch=1, grid=(R // tr,),
            in_specs=[pl.BlockSpec((tr, C), lambda i, p: (i, 0))],
            out_specs=pl.BlockSpec((None, tr, C), lambda i, p: (p[0], i, 0))),
        out_shape=_sds((N_CHIPS, R, C), BF16), compiler_params=_cp(1))(place, shard)


def _place():
    x, y, c = lax.axis_index("x"), lax.axis_index("y"), lax.axis_index("c")
    chips = [(1 - x, y), (x, 1 - y), (1 - x, 1 - y)]
    return x, y, c, chips


def _chip_id(px, py):
    return 2 * px + py


def _remote(src, dst, send_sems, recv_sems, k, to):
    return pltpu.make_async_remote_copy(src_ref=src, dst_ref=dst, send_sem=send_sems.at[k], recv_sem=recv_sems.at[k],
                                        device_id=to, device_id_type=MESH)


def _proj_gathered(h, slot, place, name, tm=512, plan=None):
    M, K = h.shape
    nb, _, Nb = slot.shape
    half = K // 2
    nt = M // tm
    cx, cy = place[0] // 2, place[0] % 2
    order = jnp.stack([place[0], _chip_id(1 - cx, cy), _chip_id(cx, 1 - cy), _chip_id(1 - cx, 1 - cy)]).astype(jnp.int32)

    p_in = [] if plan is None else plan.ins + plan.inouts
    p_out = [] if plan is None else [_sds(a.shape, a.dtype) for a in plan.inouts] + plan.outs
    n_pi, n_po = len(p_in), len(p_out)

    def body(order_ref, h_ref, slot_in, *refs):
        o_ref, slot_ref = refs[n_pi:n_pi + 2]
        w_buf, ici_send, ici_recv, pass_send, pass_recv, load_sem = refs[n_pi + 2 + n_po:n_pi + 2 + n_po + 6]

        def carried():
            if plan is None:
                return [], [], []
            ins = refs[:len(plan.ins)]
            outs = refs[n_pi + 2:n_pi + 2 + n_po]
            return plan.copies(ins, outs[:len(plan.inouts)], outs[len(plan.inouts):], *refs[n_pi + 2 + n_po + 6:])

        b, i = pl.program_id(0), pl.program_id(1)
        x, y, c, chips = _place()
        me = _chip_id(x, y)
        sib = (x, y, 1 - c)
        mine, other = pl.ds(c * half, half), pl.ds((1 - c) * half, half)

        def sent(k):
            blk = slot_ref.at[me, mine]
            return _remote(blk, blk, ici_send, ici_recv, k, (*chips[k], c))

        def landed(k):
            blk = slot_ref.at[_chip_id(*chips[k]), mine]
            return _remote(blk, blk, ici_send, ici_recv, k, (*chips[k], c))

        def passed(k, rows):
            blk = slot_ref.at[_chip_id(*chips[k]), rows]
            return _remote(blk, blk, pass_send, pass_recv, k, sib)

        @pl.when((b == 0) & (i == 0))
        def _():
            for k in range(len(chips)):
                sent(k).start()
            sends, _, local = carried()
            for cp in (*sends, *local):
                cp.start()

        for k in range(len(chips)):
            @pl.when((b == k + 1) & (i == 0))
            def _(k=k):
                landed(k).wait_recv()
                passed(k, mine).start()
                passed(k, other).wait_recv()

        @pl.when(i == 0)
        def _():
            load = pltpu.make_async_copy(slot_ref.at[order_ref[b]], w_buf, load_sem.at[0])
            load.start()
            load.wait()

        o_ref[...] = _dot(h_ref[...], w_buf[...], NN)

        @pl.when((b == nb - 1) & (i == nt - 1))
        def _():
            for k in range(len(chips)):
                sent(k).wait_send()
                passed(k, mine).wait_send()
            sends, recvs, local = carried()
            for cp in recvs:
                cp.wait_recv()
            for cp in sends:
                cp.wait_send()
            for cp in local:
                cp.wait()

    n_peers = N_CHIPS - 1
    return pl.pallas_call(
        body, name=name,
        grid_spec=pltpu.PrefetchScalarGridSpec(
            num_scalar_prefetch=1, grid=(nb, nt),
            in_specs=[pl.BlockSpec((tm, K), lambda b, i, o: (i, 0)), ANY] + [ANY] * n_pi,
            out_specs=[pl.BlockSpec((tm, Nb), lambda b, i, o: (i, o[b])), ANY] + [ANY] * n_po,
            scratch_shapes=[pltpu.VMEM((K, Nb), BF16)] + [pltpu.SemaphoreType.DMA((n_peers,))] * 4
            + [pltpu.SemaphoreType.DMA((1,))]
            + ([] if plan is None else [pltpu.SemaphoreType.DMA((plan.n_sems,))] * 3)),
        out_shape=[_sds((M, nb * Nb), F32), _sds(slot.shape, slot.dtype)] + p_out,
        input_output_aliases={2: 1, **({} if plan is None else
                                       {3 + len(plan.ins) + a: 2 + a for a in range(len(plan.inouts))})},
        compiler_params=_cp(2))(order, h, slot, *p_in)


def _gather_ici_plan(slots, wholes, part=None):
    ns, nw = len(slots), len(wholes)

    def copies(ins, ios, outs, send_sems, recv_sems, local_sems):
        x, y, c, chips = _place()
        me = _chip_id(x, y)
        sends, recvs = [], []
        for a in range(ns + nw):
            dst = ios[a] if a < ns else outs[a - ns]
            R = dst.shape[1]
            r0, nr = (0, R // 2) if part is None else part
            rows = pl.ds(c * (R // 2) + r0, nr) if a < ns else pl.ds(0, R)
            src = dst.at[me, rows] if a < ns else ins[a - ns]
            for j, chip in enumerate(chips):
                sends.append(_remote(src, dst.at[me, rows], send_sems, recv_sems, 3 * a + j, (*chip, c)))
                landed = dst.at[_chip_id(*chip), rows]
                recvs.append(_remote(landed, landed, send_sems, recv_sems, 3 * a + j, (*chip, c)))
        local = [pltpu.make_async_copy(ins[b], outs[b].at[me], local_sems.at[b]) for b in range(nw)]
        return sends, recvs, local

    return _Plan(copies, 3 * (ns + nw), ins=wholes, inouts=slots,
                 outs=[_sds((N_CHIPS, *s.shape), s.dtype) for s in wholes])


def _gather_pass_plan(slots):
    def copies(ins, ios, outs, send_sems, recv_sems, local_sems):
        x, y, c, chips = _place()
        sib = (x, y, 1 - c)
        sends, recvs = [], []
        for a, buf in enumerate(ios):
            half = buf.shape[1] // 2
            for j, chip in enumerate(chips):
                mine = buf.at[_chip_id(*chip), pl.ds(c * half, half)]
                other = buf.at[_chip_id(*chip), pl.ds((1 - c) * half, half)]
                sends.append(_remote(mine, mine, send_sems, recv_sems, 3 * a + j, sib))
                recvs.append(_remote(other, other, send_sems, recv_sems, 3 * a + j, sib))
        return sends, recvs, []

    return _Plan(copies, 3 * len(slots), inouts=slots)


def _pair_plan(grads):
    def copies(ins, ios, outs, send_sems, recv_sems, local_sems):
        x, y, c, _ = _place()
        sib = (x, y, 1 - c)
        sends, recvs = [], []
        for a, g in enumerate(ins):
            half = g.shape[1] // 2
            sends.append(_remote(g.at[:, pl.ds((1 - c) * half, half), :], outs[a], send_sems, recv_sems, a, sib))
            recvs.append(_remote(outs[a], outs[a], send_sems, recv_sems, a, sib))
        return sends, recvs, []

    return _Plan(copies, len(grads), ins=grads,
                 outs=[_sds((g.shape[0], g.shape[1] // 2, g.shape[2]), g.dtype) for g in grads])


def _chip_plan(parts):
    def copies(ins, ios, outs, send_sems, recv_sems, local_sems):
        x, y, c, chips = _place()
        me = _chip_id(x, y)
        sends, recvs = [], []
        for a, part in enumerate(ins):
            for j, chip in enumerate(chips):
                sends.append(_remote(part.at[_chip_id(*chip)], outs[a].at[me], send_sems, recv_sems, 3 * a + j, (*chip, c)))
                landed = outs[a].at[_chip_id(*chip)]
                recvs.append(_remote(landed, landed, send_sems, recv_sems, 3 * a + j, (*chip, c)))
        return sends, recvs, []

    return _Plan(copies, 3 * len(parts), ins=parts, outs=[_sds(p.shape, p.dtype) for p in parts])


def _all_sum(pack, fulls, name):
    R, C = pack.shape
    n = len(fulls)

    def body(p_ref, *refs):
        o_ref, halves = refs[n], refs[n + 1:2 * n + 1]
        buf, send_sems, recv_sems, pair_send, pair_recv = refs[2 * n + 1:]
        x, y, c, _ = _place()
        sib = (x, y, 1 - c)
        pair = []
        for a, full in enumerate(halves):
            H = full.shape[0] // 2
            mine = full.at[pl.ds(c * H, H)]
            cp = _remote(mine, mine, pair_send, pair_recv, a, sib)
            cp.start()
            pair.append(cp)
        me = 4 * x + 2 * y + c
        buf[me] = p_ref[...]
        cps = []
        for k in range(1, N_DEV):
            to = (x ^ (k >> 2), y ^ ((k >> 1) & 1), c ^ (k & 1))
            cp = _remote(p_ref, buf.at[me], send_sems, recv_sems, k - 1, to)
            cp.start()
            cps.append(cp)
        for k in range(1, N_DEV):
            frm = (x ^ (k >> 2), y ^ ((k >> 1) & 1), c ^ (k & 1))
            slot = buf.at[4 * frm[0] + 2 * frm[1] + frm[2]]
            _remote(slot, slot, send_sems, recv_sems, k - 1, frm).wait_recv()
        acc = buf[0]
        for k in range(1, N_DEV):
            acc = acc + buf[k]
        o_ref[...] = acc
        for cp in cps:
            cp.wait_send()
        for a, (full, cp) in enumerate(zip(halves, pair)):
            H = full.shape[0] // 2
            other = full.at[pl.ds((1 - c) * H, H)]
            _remote(other, other, pair_send, pair_recv, a, sib).wait_recv()
            cp.wait_send()

    vm = pl.BlockSpec(memory_space=pltpu.VMEM)
    res = pl.pallas_call(
        body, name=name, in_specs=[vm] + [ANY] * n, out_specs=[vm] + [ANY] * n,
        out_shape=[_sds((R, C), F32)] + [_sds(f.shape, f.dtype) for f in fulls],
        input_output_aliases={1 + a: 1 + a for a in range(n)},
        scratch_shapes=[pltpu.VMEM((N_DEV, R, C), F32), pltpu.SemaphoreType.DMA((N_DEV - 1,)),
                        pltpu.SemaphoreType.DMA((N_DEV - 1,)), pltpu.SemaphoreType.DMA((n,)),
                        pltpu.SemaphoreType.DMA((n,))])(pack, *fulls)
    return res[0], list(res[1:])


def _local_step(xs, tgt, p, ex):
    h1 = _norm_fwd(xs, p["pre_mix_norm"], "pre_mix_norm")
    proj = ex.project(h1)
    biases = _relbias_fwd(p["rel_bias"], "rel_bias_fwd")
    fw = []
    for g in range(N_GROUPS):
        res, got = _attn_fwd(proj, biases[g], g, f"attn_fwd{g}", plans=ex.carry(f"attn_fwd{g}"))
        ex.done(f"attn_fwd{g}", got)
        fw.append(res)
    (yh, o_h, states), got = _hgrn_fwd(proj, p["hgrn_lb_raw"], p["hgrn_norm"], "hgrn_fwd", plans=ex.carry("hgrn_fwd"))
    ex.done("hgrn_fwd", got)
    W_a, W_h, W_out = ex.weight("w_branch_attn"), ex.weight("w_branch_hgrn"), ex.weight("w_out")
    (y, lse, za, zh, merged), got = _branch_fwd([t[0] for t in fw], [t[1] for t in fw], yh, proj, W_a, W_h,
                                                "branch_fwd", plans=ex.carry("branch_fwd"))
    ex.done("branch_fwd", got)
    W_up, conv_w = ex.weight("w_up"), ex.weight("conv_w")
    mo, x1, h2 = _mix_out(merged, W_out, xs, p["post_mix_norm"], p["pre_ffn_norm"], "mix_out")
    u, got = _mm_nn_blk(h2, W_up, "ffn_up", tm=1024, plans=ex.carry("ffn_up"))
    ex.done("ffn_up", got)
    a, got = _conv_gelu_fwd(u, conv_w, p["conv_b"], "conv_gelu_fwd", plans=ex.carry("conv_gelu_fwd"))
    ex.done("conv_gelu_fwd", got)
    W_down = ex.weight("w_down")
    dx2, dff, g_post_ffn, loss = _loss_head(a, W_down, x1, tgt, p["post_ffn_norm"], "ffn_down_loss")

    ex.grad("w_down", _mm_tn(a, dff, "g_w_down").reshape(N_CHIPS, D_FF // N_CHIPS, D_MODEL))
    (dcg, dcv, gwg, gwv, gbg, gbv), got = _conv_gelu_bwd(u, dff, W_down, conv_w, p["conv_b"], "conv_gelu_bwd",
                                                          plans=ex.carry("conv_gelu_bwd"))
    ex.done("conv_gelu_bwd", got)
    g_conv_w = jnp.concatenate([gwg, gwv], axis=1)
    g_conv_b = jnp.concatenate([gbg, gbv], axis=1)
    du, got = _conv_input_bwd(dcg, dcv, conv_w, "conv_input_bwd", plans=ex.carry("conv_input_bwd"))
    ex.done("conv_input_bwd", got)
    dh2 = _mm_nt_blk(du, W_up, "d_ffn_in")
    ex.grad("w_up", _mm_tn_blk(h2, du, N_CHIPS, "g_w_up"))
    dx1, g_pre_ffn = _prenorm_bwd(dh2, x1, p["pre_ffn_norm"], dx2, "pre_ffn_norm_bwd")
    (dmo, dmerged, g_post_mix), got = _postnorm_bwd(dx1, mo, p["post_mix_norm"], W_out, "post_mix_norm_bwd",
                                                    plans=ex.carry("post_mix_norm_bwd"))
    ex.done("post_mix_norm_bwd", got)
    ex.grad("w_out", _mm_tn(merged, dmo, "g_w_out").reshape(N_CHIPS, D_MODEL // N_CHIPS, D_MODEL))
    (dza, dzh, dg0, dg1, dy, dyh), got = _branch_bwd(dmerged, za, zh, proj, W_a, W_h, "branch_bwd",
                                                     plans=ex.carry("branch_bwd"))
    ex.done("branch_bwd", got)
    ex.grad("w_branch_attn", _mm_tn_blk(y, dza, N_CHIPS, "g_w_branch_attn", together=True))
    ex.grad("w_branch_hgrn", _mm_tn_blk(yh, dzh, N_CHIPS, "g_w_branch_hgrn", together=True))
    dqkv, dbs = [], []
    for g in range(N_GROUPS):
        parts, db, got = _attn_bwd(proj, biases[g], lse, y, dy, g, f"attn_bwd{g}", plans=ex.carry(f"attn_bwd{g}"))
        ex.done(f"attn_bwd{g}", got)
        dqkv += parts
        dbs.append(db)
    g_rel_bias = _relbias_bwd(dbs, "rel_bias_bwd")
    dproj, g_lb_raw, g_hgrn_norm = _hgrn_bwd(proj, p["hgrn_lb_raw"], p["hgrn_norm"], o_h, states, dyh, dqkv,
                                             [dg0, dg1], "hgrn_bwd")
    for piece in W_IN_PIECES:
        g, got = _mm_tn_blk(h1, dproj, N_CHIPS, f"g_{piece}", x_cols=W_IN_ROWS[piece],
                            plans=ex.carry(f"g_{piece}"))
        ex.done(f"g_{piece}", got)
        ex.grad(piece, g)
    dh1, got = _mm_nt_blk(dproj, ex.weight("w_in"), "d_proj_in", plans=ex.carry("d_proj_in"))
    ex.done("d_proj_in", got)
    (grad_x, g_pre_mix), got = _prenorm_bwd(dh1, xs, p["pre_mix_norm"], dx1, "pre_mix_norm_bwd",
                                            plans=ex.carry("pre_mix_norm_bwd"))
    ex.done("pre_mix_norm_bwd", got)
    small = dict(pre_mix_norm=g_pre_mix, rel_bias=g_rel_bias, hgrn_lb_raw=g_lb_raw, hgrn_norm=g_hgrn_norm,
                 post_mix_norm=g_post_mix, pre_ffn_norm=g_pre_ffn, conv_w=g_conv_w, conv_b=g_conv_b,
                 post_ffn_norm=g_post_ffn)
    return loss, grad_x, small


SMALL = ("pre_mix_norm", "rel_bias", "hgrn_lb_raw", "hgrn_norm", "post_mix_norm", "pre_ffn_norm", "conv_w", "conv_b",
         "post_ffn_norm")
BIG = ("w_in", "w_up", "w_down", "w_out", "w_branch_attn", "w_branch_hgrn")
WEIGHTS = ("pre_mix_norm", "w_in", "rel_bias", "hgrn_lb_raw", "hgrn_norm", "w_branch_attn", "w_branch_hgrn", "w_out",
           "post_mix_norm", "pre_ffn_norm", "w_up", "conv_w", "conv_b", "w_down", "post_ffn_norm")
MIXER = ("w_out", "w_branch_attn", "w_branch_hgrn")

ICI_PARTS = {"gather_ici_1of3": (0, 176), "gather_ici_2of3": (176, 176), "gather_ici_3of3": (352, 160)}
SCHEDULE = {
    "proj_in": [("gather_ici_cw", MIXER)],
    "attn_fwd0": [("gather_pass", MIXER), ("gather_ici_1of3", ("w_up",))],
    "attn_fwd1": [("gather_ici_2of3", ("w_up",))],
    "attn_fwd2": [("gather_ici_3of3", ("w_up",))],
    "hgrn_fwd": [("gather_pass", ("w_up",))],
    "ffn_up": [("gather_ici", ("w_down",))],
    "conv_gelu_fwd": [("gather_pass", ("w_down",))],
    "conv_gelu_bwd": [("pair", ("w_down",))],
    "conv_input_bwd": [("chip", ("w_down",))],
    "post_mix_norm_bwd": [("pair", ("w_up",))],
    "attn_bwd0": [("chip", ("w_up",)), ("pair", MIXER)],
    "attn_bwd1": [("chip", MIXER)],
    "g_w_in_b": [("pair", ("w_in_a",))],
    "d_proj_in": [("chip", ("w_in_a",)), ("pair", ("w_in_b",))],
    "pre_mix_norm_bwd": [("chip", ("w_in_b",))],
}
W_IN_ROWS = dict(w_in_a=(0, 768), w_in_b=(3, 256))
W_IN_PIECES = tuple(W_IN_ROWS)
REDUCED = W_IN_PIECES + BIG[1:]


class _Exchange:
    def __init__(self, place, slots, conv_w_shard):
        self.place, self.slots, self.conv_w_shard = place, dict(slots), conv_w_shard
        self.conv_w = None
        self.g, self.from_sibling, self.pair_sums, self.arrived = {}, {}, {}, {}
        self.pending = []

    def weight(self, name):
        if name == "conv_w":
            return self.conv_w
        w = self.slots[name]
        return w.reshape(-1, D_MODEL) if name in ("w_out", "w_down") else w

    def project(self, h):
        (plan,) = self.carry("proj_in")
        proj, self.slots["w_in"], *got = _proj_gathered(h, self.slots["w_in"], self.place, "proj_in", plan=plan)
        self.done("proj_in", [got])
        return proj

    def grad(self, name, g):
        self.g[name] = g

    def carry(self, point):
        plans = []
        self.pending = SCHEDULE.get(point, [])
        for kind, names in self.pending:
            if kind in ("gather_ici", "gather_ici_cw") or kind in ICI_PARTS:
                wholes = [self.conv_w_shard] if kind == "gather_ici_cw" else []
                plans.append(_gather_ici_plan([self.slots[n] for n in names], wholes, ICI_PARTS.get(kind)))
            elif kind == "gather_pass":
                plans.append(_gather_pass_plan([self.slots[n] for n in names]))
            elif kind == "pair":
                plans.append(_pair_plan([self.g[n] for n in names]))
            else:
                for n in names:
                    self.pair_sums[n] = _pair_sum(self.g[n], self.from_sibling[n], self.place[1:2], f"pair_sum_{n}")
                plans.append(_chip_plan([self.pair_sums[n] for n in names]))
        return plans

    def done(self, point, carried):
        for (kind, names), got in zip(self.pending, carried):
            if kind in ("gather_ici", "gather_ici_cw", "gather_pass") or kind in ICI_PARTS:
                self.slots.update(zip(names, got))
                if kind == "gather_ici_cw":
                    self.conv_w = got[len(names)].transpose(1, 0, 2).reshape(3, 2 * D_FF)
            elif kind == "pair":
                self.from_sibling.update(zip(names, got))
            else:
                self.arrived.update(zip(names, got))

    def reduced_halves(self):
        return [_chip_sum(self.arrived[n], self.pair_sums[n], self.place, f"chip_sum_{n}") for n in REDUCED]


def kernel(x, pre_mix_norm, w_in, rel_bias, hgrn_lb_raw, hgrn_norm, w_branch_attn, w_branch_hgrn, w_out, post_mix_norm, pre_ffn_norm, w_up, conv_w, conv_b, w_down, post_ffn_norm, loss_target, m_pre_mix_norm, m_w_in, m_rel_bias, m_hgrn_lb_raw, m_hgrn_norm, m_w_branch_attn, m_w_branch_hgrn, m_w_out, m_post_mix_norm, m_pre_ffn_norm, m_w_up, m_conv_w, m_conv_b, m_w_down, m_post_ffn_norm, v_pre_mix_norm, v_w_in, v_rel_bias, v_hgrn_lb_raw, v_hgrn_norm, v_w_branch_attn, v_w_branch_hgrn, v_w_out, v_post_mix_norm, v_pre_ffn_norm, v_w_up, v_conv_w, v_conv_b, v_w_down, v_post_ffn_norm):
    w = dict(pre_mix_norm=pre_mix_norm, w_in=w_in, rel_bias=rel_bias, hgrn_lb_raw=hgrn_lb_raw, hgrn_norm=hgrn_norm,
             w_branch_attn=w_branch_attn, w_branch_hgrn=w_branch_hgrn, w_out=w_out, post_mix_norm=post_mix_norm,
             pre_ffn_norm=pre_ffn_norm, w_up=w_up, conv_w=conv_w, conv_b=conv_b, w_down=w_down,
             post_ffn_norm=post_ffn_norm)
    m = dict(pre_mix_norm=m_pre_mix_norm, w_in=m_w_in, rel_bias=m_rel_bias, hgrn_lb_raw=m_hgrn_lb_raw,
             hgrn_norm=m_hgrn_norm, w_branch_attn=m_w_branch_attn, w_branch_hgrn=m_w_branch_hgrn, w_out=m_w_out,
             post_mix_norm=m_post_mix_norm, pre_ffn_norm=m_pre_ffn_norm, w_up=m_w_up, conv_w=m_conv_w,
             conv_b=m_conv_b, w_down=m_w_down, post_ffn_norm=m_post_ffn_norm)
    v = dict(pre_mix_norm=v_pre_mix_norm, w_in=v_w_in, rel_bias=v_rel_bias, hgrn_lb_raw=v_hgrn_lb_raw,
             hgrn_norm=v_hgrn_norm, w_branch_attn=v_w_branch_attn, w_branch_hgrn=v_w_branch_hgrn, w_out=v_w_out,
             post_mix_norm=v_post_mix_norm, pre_ffn_norm=v_pre_ffn_norm, w_up=v_w_up, conv_w=v_conv_w,
             conv_b=v_conv_b, w_down=v_w_down, post_ffn_norm=v_post_ffn_norm)
    shard2d = {n: (w[n][0] if w[n].ndim == 3 else w[n]) for n in WEIGHTS}
    chip = 2 * lax.axis_index("x") + lax.axis_index("y")
    core = lax.axis_index("c")

    place = jnp.stack([chip, core]).astype(jnp.int32)
    slots = {n: _cast_into_slot(shard2d[n], place, f"cast_{n}") for n in BIG}
    ex = _Exchange(place, slots, shard2d["conv_w"])
    loss, grad_x, small = _local_step(x[0], loss_target[0], {n: w[n] for n in SMALL if n != "conv_w"}, ex)

    flat = [small[n].reshape(-1) for n in SMALL] + [loss.reshape(-1)]
    sizes = [t.shape[0] for t in flat]
    summed, wholes = _all_sum(jnp.concatenate(flat).reshape(-1, LANES), ex.reduced_halves(), "sum_small")
    summed = summed.reshape(-1)
    offs = [sum(sizes[:i]) for i in range(len(sizes))]
    grads = {}
    for n, o, sz in zip(SMALL, offs, sizes):
        grads[n] = summed[o:o + sz].reshape(small[n].shape)
    loss_total = summed[offs[-1]]
    cw = 2 * D_FF // N_CHIPS
    grads["conv_w"] = lax.dynamic_slice(grads["conv_w"], (0, chip * cw), (3, cw))

    big = dict(zip(REDUCED, wholes))
    big["w_in"] = jnp.concatenate([big.pop(n) for n in W_IN_PIECES], axis=0)
    grads.update(big)

    out_g, out_d, out_m, out_v = [], [], [], []
    for n in WEIGHTS:
        d2, m2, v2 = _adamw(shard2d[n], grads[n], m[n].reshape(shard2d[n].shape), v[n].reshape(shard2d[n].shape),
                            f"adamw_{n}")
        shape = w[n].shape
        out_g.append(grads[n].reshape(shape))
        out_d.append(d2.reshape(shape))
        out_m.append(m2.reshape(shape))
        out_v.append(v2.reshape(shape))
    return (loss_total, grad_x[None], *out_g, *out_d, *out_m, *out_v)
```

```python
import functools
import math

import jax
import jax.numpy as jnp
from jax import lax
from jax.experimental import pallas as pl
from jax.experimental.pallas import tpu as pltpu

F32 = jnp.float32
BF16 = jnp.bfloat16
MESH = pl.DeviceIdType.MESH

D_MODEL = 1024
N_GROUPS = 3
DILATIONS = (1, 4, 16)
HEADS = 8
HEAD_DIM = 64
GROUP_W = HEADS * HEAD_DIM
QKV_W = N_GROUPS * 3 * GROUP_W
BLK = 128
NEG_INF = -1e30
NUM_BUCKETS = 32
MAX_EXACT = 16
MAX_DISTANCE = 2048
HG_HEADS = 4
HG_DK = 128
HG_W = HG_HEADS * HG_DK
HG_CHUNK = 32
HG_TILE = 256
IN_W = QKV_W + 4 * HG_W + 2 * D_MODEL
D_FF = 2816
EPS = 1e-6
N_CHIPS = 4
N_DEV = 8
LANES = 128

ADAM_LR, ADAM_B1, ADAM_B2, ADAM_EPS, ADAM_WD, ADAM_STEP = 0.001, 0.9, 0.999, 1e-08, 0.01, 10

VMEM_LIMIT = 56 * 1024 * 1024


def _cp(n_axes):
    return pltpu.CompilerParams(dimension_semantics=("arbitrary",) * n_axes, vmem_limit_bytes=VMEM_LIMIT)


def _sds(shape, dtype):
    return jax.ShapeDtypeStruct(tuple(shape), dtype)


def _sigmoid(v):
    return 1.0 / (1.0 + jnp.exp(-v))


def _bf(v):
    return v.astype(BF16)


def _dot(a, b, dims):
    return lax.dot_general(a, b, (dims, ((), ())), preferred_element_type=F32)


NN = ((1,), (0,))
NT = ((1,), (1,))
TN = ((0,), (0,))

ANY = pl.BlockSpec(memory_space=pl.ANY)


class _Plan:
    def __init__(self, copies, n_sems, ins=(), inouts=(), outs=()):
        self.copies, self.n_sems = copies, n_sems
        self.ins, self.inouts, self.outs = list(ins), list(inouts), list(outs)


def _call(body, plans=None, *, name, grid, in_specs, out_specs, out_shape, args, scratch_shapes=()):
    plans = list(plans or ())
    in_specs, out_specs, out_shape = list(in_specs), list(out_specs), list(out_shape)
    scratch_shapes = list(scratch_shapes)
    n_in, n_out, n_scr = len(in_specs), len(out_specs), len(scratch_shapes)
    x_in, x_out, aliases, spans = [], [], {}, []
    for p in plans:
        i0, o0 = len(x_in), len(x_out)
        x_in += p.ins
        for a in p.inouts:
            aliases[n_in + len(x_in)] = n_out + len(x_out)
            x_in.append(a)
            x_out.append(_sds(a.shape, a.dtype))
        x_out += p.outs
        spans.append((i0, len(p.ins), o0, len(p.inouts), len(p.outs)))
    sems = [pltpu.SemaphoreType.DMA((p.n_sems,)) for p in plans for _ in range(3)]

    def wrapped(*refs):
        xi = refs[n_in:n_in + len(x_in)]
        base = n_in + len(x_in)
        xo = refs[base + n_out:base + n_out + len(x_out)]
        sbase = base + n_out + len(x_out)
        xs = refs[sbase + n_scr:]
        ids = [pl.program_id(k) for k in range(len(grid))]
        first = functools.reduce(jnp.logical_and, [i == 0 for i in ids])
        last = functools.reduce(jnp.logical_and, [i == g - 1 for i, g in zip(ids, grid)])

        def descriptors(k):
            i0, ni, o0, nio, no = spans[k]
            return plans[k].copies(xi[i0:i0 + ni], xo[o0:o0 + nio], xo[o0 + nio:o0 + nio + no], *xs[3 * k:3 * k + 3])

        @pl.when(first)
        def _():
            for k in range(len(plans)):
                sends, _, local = descriptors(k)
                for cp in (*sends, *local):
                    cp.start()

        body(*refs[:n_in], *refs[base:base + n_out], *refs[sbase:sbase + n_scr])

        @pl.when(last)
        def _():
            for k in range(len(plans)):
                sends, recvs, local = descriptors(k)
                for cp in recvs:
                    cp.wait_recv()
                for cp in sends:
                    cp.wait_send()
                for cp in local:
                    cp.wait()

    res = pl.pallas_call(
        wrapped if plans else body, name=name, grid=grid, in_specs=in_specs + [ANY] * len(x_in),
        out_specs=out_specs + [ANY] * len(x_out), out_shape=out_shape + x_out, input_output_aliases=aliases,
        scratch_shapes=scratch_shapes + sems, compiler_params=_cp(len(grid)))(*args, *x_in)
    res = list(res)
    carried = [res[n_out + o0:n_out + o0 + nio + no] for (_, _, o0, nio, no) in spans]
    return res[:n_out], carried


def _mm_nn_blk(a, wg, name, tm=512, plans=None):
    M, K = a.shape
    nb, _, Nb = wg.shape

    def body(a_ref, w_ref, o_ref):
        o_ref[...] = _dot(_bf(a_ref[...]), w_ref[...], NN)

    (out,), carried = _call(
        body, plans, name=name, grid=(nb, M // tm),
        in_specs=[pl.BlockSpec((tm, K), lambda j, i: (i, 0)), pl.BlockSpec((None, K, Nb), lambda j, i: (j, 0, 0))],
        out_specs=[pl.BlockSpec((tm, Nb), lambda j, i: (i, j))],
        out_shape=[_sds((M, nb * Nb), F32)], args=(a, wg))
    return out if plans is None else (out, carried)


def _mm_nt_blk(dy, wg, name, tm=1024, plans=None):
    M = dy.shape[0]
    nb, K, Nb = wg.shape

    def body(dy_ref, w_ref, o_ref):
        j = pl.program_id(1)
        r = _dot(_bf(dy_ref[...]), w_ref[...], NT)

        @pl.when(j == 0)
        def _():
            o_ref[...] = r

        @pl.when(j > 0)
        def _():
            o_ref[...] += r

    (out,), carried = _call(
        body, plans, name=name, grid=(M // tm, nb),
        in_specs=[pl.BlockSpec((tm, Nb), lambda i, j: (i, j)), pl.BlockSpec((None, K, Nb), lambda i, j: (j, 0, 0))],
        out_specs=[pl.BlockSpec((tm, K), lambda i, j: (i, 0))],
        out_shape=[_sds((M, K), F32)], args=(dy, wg))
    return out if plans is None else (out, carried)


def _mm_tn_blk(x, dy, nb, name, tk=2048, x_cols=None, plans=None, together=False):
    T, Mx = x.shape
    xk, Mx = (0, Mx) if x_cols is None else x_cols
    Nb = dy.shape[1] // nb
    nj = nb if together else 1

    def body(x_ref, dy_ref, o_ref):
        t = pl.program_id(1)
        r = _dot(_bf(x_ref[...]), _bf(dy_ref[...]), TN)
        for j in range(nj):
            rj = r[:, j * Nb:(j + 1) * Nb]

            @pl.when(t == 0)
            def _():
                o_ref[j] = rj

            @pl.when(t > 0)
            def _():
                o_ref[j] += rj

    (out,), carried = _call(
        body, plans, name=name, grid=(nb // nj, T // tk),
        in_specs=[pl.BlockSpec((tk, Mx), lambda j, t: (t, xk)), pl.BlockSpec((tk, nj * Nb), lambda j, t: (t, j))],
        out_specs=[pl.BlockSpec((nj, Mx, Nb), lambda j, t: (j, 0, 0))],
        out_shape=[_sds((nb, Mx, Nb), F32)], args=(x, dy))
    return out if plans is None else (out, carried)


def _mm_tn(x, dy, name, tk=1024):
    T, Mx = x.shape
    N = dy.shape[1]

    def body(x_ref, dy_ref, o_ref):
        t = pl.program_id(0)
        r = _dot(_bf(x_ref[...]), _bf(dy_ref[...]), TN)

        @pl.when(t == 0)
        def _():
            o_ref[...] = r

        @pl.when(t > 0)
        def _():
            o_ref[...] += r

    return pl.pallas_call(
        body, name=name, grid=(T // tk,),
        in_specs=[pl.BlockSpec((tk, Mx), lambda t: (t, 0)), pl.BlockSpec((tk, N), lambda t: (t, 0))],
        out_specs=pl.BlockSpec((Mx, N), lambda t: (0, 0)),
        out_shape=_sds((Mx, N), F32), compiler_params=_cp(1))(x, dy)


def _tile(arr, bw, col=lambda c: 0):
    return ("tile", arr, bw, col)


def _full(arr):
    return ("full", arr)


def _out_tile(width, dtype, bw, col=lambda c: 0):
    return ("tile", width, dtype, bw, col)


def _out_acc(rows, width, bw, col=lambda c: 0):
    return ("acc", rows, width, bw, col)


def _rows_call(name, body, n_rows, tm, ncol, ins, outs, plans=None):
    in_specs, args = [], []
    for e in ins:
        if e[0] == "tile":
            _, arr, bw, col = e
            in_specs.append(pl.BlockSpec((tm, bw), functools.partial(lambda c, i, col: (i, col(c)), col=col)))
        else:
            arr = e[1]
            in_specs.append(pl.BlockSpec(arr.shape, functools.partial(lambda c, i, nd: (0,) * nd, nd=arr.ndim)))
        args.append(arr)
    out_specs, out_shape = [], []
    for e in outs:
        if e[0] == "tile":
            _, width, dtype, bw, col = e
            out_specs.append(pl.BlockSpec((tm, bw), functools.partial(lambda c, i, col: (i, col(c)), col=col)))
            out_shape.append(_sds((n_rows, width), dtype))
        else:
            _, rows, width, bw, col = e
            out_specs.append(pl.BlockSpec((rows, bw), functools.partial(lambda c, i, col: (0, col(c)), col=col)))
            out_shape.append(_sds((rows, width), F32))
    out, carried = _call(body, plans, name=name, grid=(ncol, n_rows // tm), in_specs=in_specs, out_specs=out_specs,
                         out_shape=out_shape, args=args)
    return out if plans is None else (out, carried)


def _acc(ref, val):
    i = pl.program_id(1)

    @pl.when(i == 0)
    def _():
        ref[...] = val

    @pl.when(i > 0)
    def _():
        ref[...] += val


def _rinv(z):
    return lax.rsqrt(jnp.mean(z * z, axis=-1, keepdims=True) + EPS)


def _norm_bwd(dy, zhat, r, w):
    dyw = dy * w
    return r * (dyw - zhat * jnp.mean(dyw * zhat, axis=-1, keepdims=True))


def _prenorm_bwd(dh, xin, w, dres, name, plans=None):
    def body(dh_ref, x_ref, w_ref, dres_ref, dx_ref, dw_ref):
        xv = x_ref[...]
        r = _rinv(xv)
        xhat = xv * r
        dhv = dh_ref[...]
        dx_ref[...] = dres_ref[...] + _norm_bwd(dhv, xhat, r, w_ref[...])
        _acc(dw_ref, jnp.sum(dhv * xhat, axis=0, keepdims=True))

    return _rows_call(name, body, xin.shape[0], 512, 1,
                      [_tile(dh, D_MODEL), _tile(xin, D_MODEL), _full(w), _tile(dres, D_MODEL)],
                      [_out_tile(D_MODEL, F32, D_MODEL), _out_acc(1, D_MODEL, D_MODEL)], plans)


def _postnorm_bwd(dout, z, w, w_mat, name, plans=None):
    def body(do_ref, z_ref, w_ref, wm_ref, dz_ref, dm_ref, dw_ref):
        zv = z_ref[...]
        r = _rinv(zv)
        zhat = zv * r
        dov = do_ref[...]
        dz = _bf(_norm_bwd(dov, zhat, r, w_ref[...]))
        dz_ref[...] = dz
        dm_ref[...] = _dot(dz, wm_ref[...], NT)
        _acc(dw_ref, jnp.sum(dov * zhat, axis=0, keepdims=True))

    return _rows_call(name, body, z.shape[0], 512, 1,
                      [_tile(dout, D_MODEL), _tile(z, D_MODEL), _full(w), _full(w_mat)],
                      [_out_tile(D_MODEL, BF16, D_MODEL), _out_tile(D_MODEL, F32, D_MODEL),
                       _out_acc(1, D_MODEL, D_MODEL)], plans)


def _t5_bucket(dist):
    n = jnp.maximum(dist, 0)
    nf = jnp.maximum(n, 1).astype(F32)
    large = MAX_EXACT + (jnp.log(nf / MAX_EXACT) / math.log(MAX_DISTANCE / MAX_EXACT)
                         * (NUM_BUCKETS - MAX_EXACT)).astype(jnp.int32)
    large = jnp.minimum(large, NUM_BUCKETS - 1)
    return jnp.where(n < MAX_EXACT, n, large)


def _band_rel():
    return jnp.arange(BLK)[:, None] + BLK - jnp.arange(2 * BLK)[None, :]


def _band_valid():
    rel = _band_rel()
    window = (rel >= 0) & (rel <= BLK)
    first = window & (jnp.arange(2 * BLK)[None, :] >= BLK)
    return jnp.stack([first, window]).astype(F32).reshape(2, 1, BAND)


RES_UNROLL = 8
PAIR = LANES // HEAD_DIM


def _pair_lanes():
    first = lax.broadcasted_iota(jnp.int32, (1, LANES), 1) < HEAD_DIM
    return first, jnp.logical_not(first)


def _heads_per_step(d):
    return HEADS if d == 1 else LANES // HEAD_DIM


def _sub_rows(r, d):
    return pl.ds(r, BLK, stride=d) if d > 1 else pl.ds(0, BLK)


def _for_residues(d, fn):
    if d <= RES_UNROLL:
        for r in range(d):
            fn(r)
    else:
        def group(i, carry):
            for k in range(RES_UNROLL):
                fn(i * RES_UNROLL + k)
            return carry

        lax.fori_loop(0, d // RES_UNROLL, group, 0)


def _attn_specs(d, g, qblock):
    cw = _heads_per_step(d) * HEAD_DIM

    def col(part, hp):
        return (g * 3 + part) * (GROUP_W // cw) + hp

    def cur(part):
        return pl.BlockSpec((d * BLK, cw), lambda hp, n: (qblock(n), col(part, hp)))

    def prev(part):
        return pl.BlockSpec((d * BLK, cw), lambda hp, n: (jnp.maximum(qblock(n) - 1, 0), col(part, hp)))

    return cur, prev


def _attn_fwd(proj, bias, g, name, plans=None):
    S = proj.shape[0]
    d = DILATIONS[g]
    NB = S // (d * BLK)
    hps = _heads_per_step(d)

    def body(q_ref, kp_ref, kc_ref, vp_ref, vc_ref, b_ref, o_ref, lse_ref):
        hp = pl.program_id(0)
        later = jnp.minimum(pl.program_id(1), 1)

        def residue(r):
            rows = _sub_rows(r, d)
            q2 = q_ref[rows, :]
            k2 = jnp.concatenate([kp_ref[rows, :], kc_ref[rows, :]], axis=0)
            v2 = jnp.concatenate([vp_ref[rows, :], vc_ref[rows, :]], axis=0)
            outs, lses = [], []
            for pp in range(hps // PAIR):
                ps = slice(pp * LANES, (pp + 1) * LANES)
                qp, kp, vp = _bf(q2[:, ps]), _bf(k2[:, ps]), _bf(v2[:, ps])
                o_h, lse_h = [], []
                for hh, own in enumerate(_pair_lanes()):
                    s = _dot(qp, jnp.where(own, kp, 0), NT) * (HEAD_DIM ** -0.5) + b_ref[later, hp * hps + pp * PAIR + hh]
                    m = jnp.max(s, axis=-1, keepdims=True)
                    p = jnp.exp(s - m)
                    l = jnp.sum(p, axis=-1, keepdims=True)
                    o_h.append(_dot(_bf(p), vp, NN) / l)
                    lse_h.append(m + jnp.log(l))
                first = _pair_lanes()[0]
                outs.append(jnp.where(first, o_h[0], o_h[1]))
                lses.append(jnp.where(first, lse_h[0], lse_h[1]))
            o_ref[rows, :] = outs[0] if len(outs) == 1 else jnp.concatenate(outs, axis=1)
            lse_ref[rows, :] = lses[0] if len(lses) == 1 else jnp.concatenate(lses, axis=1)

        _for_residues(d, residue)

    cur, prev = _attn_specs(d, g, lambda n: n)
    out = pl.BlockSpec((d * BLK, hps * HEAD_DIM), lambda hp, n: (n, hp))
    res, carried = _call(
        body, plans, name=name, grid=(HEADS // hps, NB),
        in_specs=[cur(0), prev(1), cur(1), prev(2), cur(2),
                  pl.BlockSpec((2, HEADS, BLK, 2 * BLK), lambda hp, n: (0, 0, 0, 0))],
        out_specs=[out, out], out_shape=[_sds((S, GROUP_W), F32)] * 2,
        args=(proj, proj, proj, proj, proj, bias))
    return res if plans is None else (res, carried)


def _attn_bwd(proj, bias, lse, y, dy, g, name, plans=None):
    S = proj.shape[0]
    d = DILATIONS[g]
    NB = S // (d * BLK)
    hps = _heads_per_step(d)

    def body(q_ref, kp_ref, kc_ref, vp_ref, vc_ref, b_ref, l_ref, y_ref, dy_ref,
             dq_ref, dk_ref, dv_ref, db_ref, ck_ref, cv_ref):
        hp, n = pl.program_id(0), pl.program_id(1)

        @pl.when((hp == 0) & (n == 0))
        def _():
            db_ref[...] = jnp.zeros_like(db_ref)

        @pl.when(n == 0)
        def _():
            ck_ref[...] = jnp.zeros_like(ck_ref)
            cv_ref[...] = jnp.zeros_like(cv_ref)

        @pl.when(n < NB)
        def _():
            later = jnp.minimum(n, 1)

            def residue(r):
                rows = _sub_rows(r, d)
                q2 = q_ref[rows, :]
                k2 = jnp.concatenate([kp_ref[rows, :], kc_ref[rows, :]], axis=0)
                v2 = jnp.concatenate([vp_ref[rows, :], vc_ref[rows, :]], axis=0)
                l2, y2, dy2 = l_ref[rows, :], y_ref[rows, :], dy_ref[rows, :]
                dqs, dks, dvs = [], [], []
                for pp in range(hps // PAIR):
                    ps = slice(pp * LANES, (pp + 1) * LANES)
                    qp, kp, vp = _bf(q2[:, ps]), _bf(k2[:, ps]), _bf(v2[:, ps])
                    dyp, yp = dy2[:, ps], y2[:, ps]
                    dq_h, dk_h, dv_h = [], [], []
                    for hh, own in enumerate(_pair_lanes()):
                        head = hp * hps + pp * PAIR + hh
                        s = _dot(qp, jnp.where(own, kp, 0), NT) * (HEAD_DIM ** -0.5) + b_ref[later, head]
                        p = jnp.exp(s - l2[:, pp * LANES + hh * HEAD_DIM:pp * LANES + hh * HEAD_DIM + 1])
                        dyh = jnp.where(own, dyp, 0.0)
                        delta = jnp.sum(dyh * yp, axis=-1, keepdims=True)
                        ds = p * (_dot(_bf(dyh), vp, NT) - delta)
                        db_ref[head] += ds
                        dsb = _bf(ds * (HEAD_DIM ** -0.5))
                        dq_h.append(_dot(dsb, kp, NN))
                        dk_h.append(_dot(dsb, qp, TN))
                        dv_h.append(_dot(_bf(p), _bf(dyp), TN))
                    first = _pair_lanes()[0]
                    dqs.append(jnp.where(first, dq_h[0], dq_h[1]))
                    dks.append(jnp.where(first, dk_h[0], dk_h[1]))
                    dvs.append(jnp.where(first, dv_h[0], dv_h[1]))
                dkb = dks[0] if len(dks) == 1 else jnp.concatenate(dks, axis=1)
                dvb = dvs[0] if len(dvs) == 1 else jnp.concatenate(dvs, axis=1)
                dq_ref[rows, :] = dqs[0] if len(dqs) == 1 else jnp.concatenate(dqs, axis=1)
                dk_ref[rows, :] = ck_ref[rows, :] + dkb[:BLK]
                dv_ref[rows, :] = cv_ref[rows, :] + dvb[:BLK]
                ck_ref[rows, :] = dkb[BLK:]
                cv_ref[rows, :] = dvb[BLK:]

            _for_residues(d, residue)

        @pl.when(n == NB)
        def _():
            dk_ref[...] = ck_ref[...]
            dv_ref[...] = cv_ref[...]

    def qn(n):
        return jnp.minimum(n, NB - 1)

    cur, prev = _attn_specs(d, g, qn)
    cw = hps * HEAD_DIM
    row = pl.BlockSpec((d * BLK, cw), lambda hp, n: (qn(n), hp))
    done = pl.BlockSpec((d * BLK, cw), lambda hp, n: (jnp.maximum(n - 1, 0), hp))
    (dq, dk, dv, db), carried = _call(
        body, plans, name=name, grid=(HEADS // hps, NB + 1),
        in_specs=[cur(0), prev(1), cur(1), prev(2), cur(2),
                  pl.BlockSpec((2, HEADS, BLK, 2 * BLK), lambda hp, n: (0, 0, 0, 0)), row, row, row],
        out_specs=[row, done, done, pl.BlockSpec((HEADS, BLK, 2 * BLK), lambda hp, n: (0, 0, 0))],
        out_shape=[_sds((S, GROUP_W), F32)] * 3 + [_sds((HEADS, BLK, 2 * BLK), F32)],
        scratch_shapes=[pltpu.VMEM((d * BLK, cw), F32)] * 2,
        args=(proj, proj, proj, proj, proj, bias, lse, y, dy))
    return ([dq, dk, dv], db) if plans is None else ([dq, dk, dv], db, carried)


BAND = BLK * 2 * BLK


def _bucket_onehot():
    buckets = jnp.stack([_t5_bucket(_band_rel() * d) for d in DILATIONS]).reshape(N_GROUPS, 1, BAND)
    return (buckets == jnp.arange(NUM_BUCKETS).reshape(1, NUM_BUCKETS, 1)).astype(F32)


def _relbias_fwd(rel_bias, name):
    table = rel_bias.reshape(NUM_BUCKETS, N_GROUPS, HEADS).transpose(1, 0, 2)

    def body(t_ref, oh_ref, valid_ref, o_ref):
        bias = lax.dot_general(t_ref[...], oh_ref[...], (TN, ((), ())), preferred_element_type=F32,
                               precision=lax.Precision.HIGHEST)
        for k in range(2):
            o_ref[k] = jnp.where(valid_ref[k] > 0.5, bias, NEG_INF)

    out = pl.pallas_call(
        body, name=name, grid=(N_GROUPS,),
        in_specs=[pl.BlockSpec((None, NUM_BUCKETS, HEADS), lambda g: (g, 0, 0)),
                  pl.BlockSpec((None, NUM_BUCKETS, BAND), lambda g: (g, 0, 0)),
                  pl.BlockSpec((2, 1, BAND), lambda g: (0, 0, 0))],
        out_specs=pl.BlockSpec((None, 2, HEADS, BAND), lambda g: (g, 0, 0, 0)),
        out_shape=_sds((N_GROUPS, 2, HEADS, BAND), F32), compiler_params=_cp(1))(table, _bucket_onehot(), _band_valid())
    return out.reshape(N_GROUPS, 2, HEADS, BLK, 2 * BLK)


def _relbias_bwd(dbs, name):
    band = BAND
    onehot = _bucket_onehot()
    dbf = jnp.stack([db.reshape(HEADS, band) for db in dbs])

    def body(oh_ref, db_ref, o_ref):
        o_ref[...] = lax.dot_general(oh_ref[...], db_ref[...], (NT, ((), ())), preferred_element_type=F32,
                                     precision=lax.Precision.HIGHEST)

    out = pl.pallas_call(
        body, name=name, grid=(N_GROUPS,),
        in_specs=[pl.BlockSpec((None, NUM_BUCKETS, band), lambda g: (g, 0, 0)),
                  pl.BlockSpec((None, HEADS, band), lambda g: (g, 0, 0))],
        out_specs=pl.BlockSpec((None, NUM_BUCKETS, HEADS), lambda g: (g, 0, 0)),
        out_shape=_sds((N_GROUPS, NUM_BUCKETS, HEADS), F32), compiler_params=_cp(1))(onehot, dbf)
    return out.transpose(1, 0, 2).reshape(NUM_BUCKETS, N_GROUPS * HEADS)


def _chunk_pos(shape):
    return lax.broadcasted_iota(jnp.int32, shape, 0) % HG_CHUNK


def _chunk_cumsum(v):
    pos = _chunk_pos(v.shape)
    s = 1
    while s < HG_CHUNK:
        v = v + jnp.where(pos >= s, pltpu.roll(v, s, 0), 0.0)
        s *= 2
    return v


def _chunk_rev_cumsum(v):
    pos = _chunk_pos(v.shape)
    n = v.shape[0]
    s = 1
    while s < HG_CHUNK:
        v = v + jnp.where(pos < HG_CHUNK - s, pltpu.roll(v, n - s, 0), 0.0)
        s *= 2
    return v


def _lower_bound(raw):
    a0, a1 = raw[0:1], raw[1:2]
    m = jnp.maximum(a0, a1)
    e0, e1 = jnp.exp(a0 - m), jnp.exp(a1 - m)
    return e0 / (e0 + e1)


def _hg_gates(qr, fr, lb):
    sf = _sigmoid(fr)
    f = lb + (1.0 - lb) * sf
    sq = _sigmoid(qr)
    return qr * sq, sq, f, sf


HG_COL0 = QKV_W // HG_W


def _hgrn_fwd(proj, lb_raw, nw, name, plans=None):
    S = proj.shape[0]
    ncs = HG_TILE // HG_CHUNK
    tril = jnp.tril(jnp.ones((HG_CHUNK, HG_CHUNK), dtype=bool))

    def body(q_ref, f_ref, i_ref, og_ref, lb_ref, nw_ref, y_ref, o_ref, st_ref, state):
        @pl.when(pl.program_id(0) == 0)
        def _():
            state[...] = jnp.zeros_like(state)

        lb = _lower_bound(lb_ref[...])
        q, _, f, _ = _hg_gates(q_ref[...], f_ref[...], lb)
        k = 1.0 - f
        G = _chunk_cumsum(jnp.log(f))
        row = lax.broadcasted_iota(jnp.int32, (HG_CHUNK, HG_CHUNK), 0)
        col = lax.broadcasted_iota(jnp.int32, (HG_CHUNK, HG_CHUNK), 1)
        heads = [slice(h * HG_DK, (h + 1) * HG_DK) for h in range(HG_HEADS)]
        sts = [state[h] for h in range(HG_HEADS)]
        for c in range(ncs):
            cs = slice(c * HG_CHUNK, (c + 1) * HG_CHUNK)
            for h, hs in enumerate(heads):
                Gc = G[cs, hs]
                gl = Gc[HG_CHUNK - 1:HG_CHUNK]
                qt = _bf(q[cs, hs] * jnp.exp(Gc))
                kt = _bf(k[cs, hs] * jnp.exp(-Gc))
                kd = _bf(k[cs, hs] * jnp.exp(gl - Gc))
                v = _bf(i_ref[cs, hs])
                A = jnp.where(row >= col, _dot(qt, kt, NT), 0.0)
                o_ref[cs, hs] = _dot(_bf(A), v, NN) + _dot(qt, _bf(sts[h]), NT)
                st_ref[c, h] = sts[h]
                sts[h] = sts[h] * jnp.exp(gl) + _dot(v, kd, TN)
        for h, hs in enumerate(heads):
            state[h] = sts[h]
            oh = o_ref[:, hs]
            og = og_ref[:, hs]
            y_ref[:, hs] = oh * _rinv(oh) * nw_ref[...] * (og * _sigmoid(og))

    def colspec(j):
        return pl.BlockSpec((HG_TILE, HG_W), lambda i: (i, HG_COL0 + j))

    res, carried = _call(
        body, plans, name=name, grid=(S // HG_TILE,),
        in_specs=[colspec(0), colspec(1), colspec(2), colspec(3),
                  pl.BlockSpec((2, HG_W), lambda i: (0, 0)), pl.BlockSpec((1, HG_DK), lambda i: (0, 0))],
        out_specs=[pl.BlockSpec((HG_TILE, HG_W), lambda i: (i, 0))] * 2
        + [pl.BlockSpec((ncs, HG_HEADS, HG_DK, HG_DK), lambda i: (i, 0, 0, 0))],
        out_shape=[_sds((S, HG_W), F32)] * 2 + [_sds((S // HG_CHUNK, HG_HEADS, HG_DK, HG_DK), F32)],
        scratch_shapes=[pltpu.VMEM((HG_HEADS, HG_DK, HG_DK), F32)],
        args=(proj, proj, proj, proj, lb_raw, nw))
    return res if plans is None else (res, carried)


def _hgrn_bwd(proj, lb_raw, nw, o, states, dy, d_attn, d_gates, name):
    S = proj.shape[0]
    ncs = HG_TILE // HG_CHUNK
    nt = S // HG_TILE
    n_a, n_g = len(d_attn), len(d_gates)
    own = [slice(QKV_W + j * HG_W, QKV_W + (j + 1) * HG_W) for j in range(4)]

    def body(q_ref, f_ref, i_ref, og_ref, lb_ref, nw_ref, o_ref, st_ref, dy_ref, *rest):
        attn_refs, gate_refs = rest[:n_a], rest[n_a:n_a + n_g]
        dp_ref, dlb_ref, dnw_ref, dstate, do_s, dG_s, dgl_s, dk_s, dlb_s = rest[n_a + n_g:]
        dq_ref, df_ref, di_ref, dog_ref = (dp_ref.at[:, cols] for cols in own)
        step = pl.program_id(0)
        for k, a_ref in enumerate(attn_refs):
            dp_ref[:, k * GROUP_W:(k + 1) * GROUP_W] = _bf(a_ref[...])
        for k, g_ref in enumerate(gate_refs):
            dp_ref[:, QKV_W + 4 * HG_W + k * D_MODEL:QKV_W + 4 * HG_W + (k + 1) * D_MODEL] = g_ref[...]

        @pl.when(step == 0)
        def _():
            dstate[...] = jnp.zeros_like(dstate)
            dlb_s[...] = jnp.zeros_like(dlb_s)
            dnw_ref[...] = jnp.zeros_like(dnw_ref)

        lb = _lower_bound(lb_ref[...])
        qr = q_ref[...]
        q, sq, f, sf = _hg_gates(qr, f_ref[...], lb)
        k = 1.0 - f
        G = _chunk_cumsum(jnp.log(f))
        nwv = nw_ref[...]
        row = lax.broadcasted_iota(jnp.int32, (HG_CHUNK, HG_CHUNK), 0)
        col = lax.broadcasted_iota(jnp.int32, (HG_CHUNK, HG_CHUNK), 1)
        for h in range(HG_HEADS):
            hs = slice(h * HG_DK, (h + 1) * HG_DK)
            oh = o_ref[:, hs]
            r = _rinv(oh)
            ohat = oh * r
            og = og_ref[:, hs]
            sg = _sigmoid(og)
            dyh = dy_ref[:, hs]
            don = dyh * (og * sg)
            dog_ref[:, hs] = _bf(dyh * (ohat * nwv) * (sg * (1.0 + og * (1.0 - sg))))
            dnw_ref[...] += jnp.sum(don * ohat, axis=0, keepdims=True)
            do_s[:, hs] = _norm_bwd(don, ohat, r, nwv)
        dsts = [dstate[h] for h in range(HG_HEADS)]
        for c in reversed(range(ncs)):
            cs = slice(c * HG_CHUNK, (c + 1) * HG_CHUNK)
            for h in range(HG_HEADS):
                hs = slice(h * HG_DK, (h + 1) * HG_DK)
                dst = dsts[h]
                Gc = G[cs, hs]
                gl = Gc[HG_CHUNK - 1:HG_CHUNK]
                eG, enG, edG, egl = jnp.exp(Gc), jnp.exp(-Gc), jnp.exp(gl - Gc), jnp.exp(gl)
                qt, kt, kd = q[cs, hs] * eG, k[cs, hs] * enG, k[cs, hs] * edG
                qtb, ktb, kdb = _bf(qt), _bf(kt), _bf(kd)
                v = _bf(i_ref[cs, hs])
                do = _bf(do_s[cs, hs])
                st = st_ref[c, h]
                dstb = _bf(dst)
                A = jnp.where(row >= col, _dot(qtb, ktb, NT), 0.0)
                dA = _bf(jnp.where(row >= col, _dot(do, v, NT), 0.0))
                di_ref[cs, hs] = _bf(_dot(_bf(A), do, TN) + _dot(kdb, dstb, NT))
                dqt = _dot(dA, ktb, NN) + _dot(do, _bf(st), NN)
                dkt = _dot(dA, qtb, TN)
                dkd = _dot(v, dstb, NN)
                dgl = egl * jnp.sum(st * dst, axis=0, keepdims=True) + jnp.sum(dkd * kd, axis=0, keepdims=True)
                dsts[h] = dst * egl + _dot(do, qtb, TN)
                dq_ref[cs, hs] = _bf(dqt * eG * (sq[cs, hs] * (1.0 + qr[cs, hs] * (1.0 - sq[cs, hs]))))
                dk_s[cs, hs] = dkt * enG + dkd * edG
                dG_s[cs, hs] = dqt * qt - dkt * kt - dkd * kd
                dgl_s[cs, hs] = jnp.broadcast_to(dgl, (HG_CHUNK, HG_DK))
        for h in range(HG_HEADS):
            dstate[h] = dsts[h]
        dg = _chunk_rev_cumsum(dG_s[...]) + dgl_s[...]
        dfv = dg / f - dk_s[...]
        df_ref[...] = _bf(dfv * (1.0 - lb) * sf * (1.0 - sf))
        dlb_s[...] += jnp.sum(dfv * (1.0 - sf), axis=0, keepdims=True)

        @pl.when(step == nt - 1)
        def _():
            t = dlb_s[...] * lb * (1.0 - lb)
            dlb_ref[...] = jnp.concatenate([t, -t], axis=0)

    def colspec(j):
        return pl.BlockSpec((HG_TILE, HG_W), lambda i: (nt - 1 - i, HG_COL0 + j))

    def rows(width):
        return pl.BlockSpec((HG_TILE, width), lambda i: (nt - 1 - i, 0))

    tile = rows(HG_W)
    return pl.pallas_call(
        body, name=name, grid=(nt,),
        in_specs=[colspec(0), colspec(1), colspec(2), colspec(3),
                  pl.BlockSpec((2, HG_W), lambda i: (0, 0)), pl.BlockSpec((1, HG_DK), lambda i: (0, 0)),
                  tile, pl.BlockSpec((ncs, HG_HEADS, HG_DK, HG_DK), lambda i: (nt - 1 - i, 0, 0, 0)), tile]
        + [rows(GROUP_W)] * n_a + [rows(D_MODEL)] * n_g,
        out_specs=[rows(IN_W), pl.BlockSpec((2, HG_W), lambda i: (0, 0)), pl.BlockSpec((1, HG_DK), lambda i: (0, 0))],
        out_shape=[_sds((S, IN_W), BF16), _sds((2, HG_W), F32), _sds((1, HG_DK), F32)],
        scratch_shapes=[pltpu.VMEM((HG_HEADS, HG_DK, HG_DK), F32)] + [pltpu.VMEM((HG_TILE, HG_W), F32)] * 4
        + [pltpu.VMEM((1, HG_W), F32)],
        compiler_params=_cp(1))(proj, proj, proj, proj, lb_raw, nw, o, states, dy, *d_attn, *d_gates)


GATE_COL0 = (QKV_W + 4 * HG_W) // GROUP_W
HALF_D = D_MODEL // 2


def _gate_tiles(proj):
    return [_tile(proj, HALF_D, functools.partial(lambda c, k: GATE_COL0 + k, k=k)) for k in range(4)]


def _gates(g_refs):
    s0 = _sigmoid(jnp.concatenate([g_refs[0][...], g_refs[1][...]], axis=1))
    s1 = _sigmoid(jnp.concatenate([g_refs[2][...], g_refs[3][...]], axis=1))
    return s0, s1


def _branch_fwd(os_, lses, yh, proj, w_a, w_h, name, plans=None):
    nb = w_a.shape[0]

    def body(o0, o1, o2, l0, l1, l2, yh_ref, g0a, g0b, g1a, g1b, wa_ref, wh_ref,
             y_ref, lse_ref, za_ref, zh_ref, m_ref):
        a, b, c = l0[...], l1[...], l2[...]
        m = jnp.maximum(jnp.maximum(a, b), c)
        ea, eb, ec = jnp.exp(a - m), jnp.exp(b - m), jnp.exp(c - m)
        den = ea + eb + ec
        y = (ea * o0[...] + eb * o1[...] + ec * o2[...]) / den
        y_ref[...] = y
        lse_ref[...] = m + jnp.log(den)
        yb, yhb = _bf(y), _bf(yh_ref[...])
        za = jnp.concatenate([_dot(yb, wa_ref[j], NN) for j in range(nb)], axis=1)
        zh = jnp.concatenate([_dot(yhb, wh_ref[j], NN) for j in range(nb)], axis=1)
        s0, s1 = _gates((g0a, g0b, g1a, g1b))
        za_ref[...] = za
        zh_ref[...] = zh
        m_ref[...] = _bf(s0 * za + s1 * zh)

    return _rows_call(name, body, yh.shape[0], 512, 1,
                      [*[_tile(t, GROUP_W) for t in (*os_, *lses)], _tile(yh, HG_W), *_gate_tiles(proj),
                       _full(w_a), _full(w_h)],
                      [_out_tile(GROUP_W, F32, GROUP_W)] * 2 + [_out_tile(D_MODEL, F32, D_MODEL)] * 2
                      + [_out_tile(D_MODEL, BF16, D_MODEL)], plans)


def _branch_bwd(dm, za, zh, proj, w_a, w_h, name, plans=None):
    nb, _, Nb = w_a.shape

    def body(dm_ref, za_ref, zh_ref, g0a, g0b, g1a, g1b, wa_ref, wh_ref,
             dza_ref, dzh_ref, dg0_ref, dg1_ref, dy_ref, dyh_ref):
        dmv = dm_ref[...]
        s0, s1 = _gates((g0a, g0b, g1a, g1b))
        dza, dzh = _bf(dmv * s0), _bf(dmv * s1)
        dza_ref[...] = dza
        dzh_ref[...] = dzh
        dg0_ref[...] = _bf(dmv * za_ref[...] * s0 * (1.0 - s0))
        dg1_ref[...] = _bf(dmv * zh_ref[...] * s1 * (1.0 - s1))
        dy_ref[...] = sum(_dot(dza[:, j * Nb:(j + 1) * Nb], wa_ref[j], NT) for j in range(nb))
        dyh_ref[...] = sum(_dot(dzh[:, j * Nb:(j + 1) * Nb], wh_ref[j], NT) for j in range(nb))

    return _rows_call(name, body, za.shape[0], 512, 1,
                      [_tile(dm, D_MODEL), _tile(za, D_MODEL), _tile(zh, D_MODEL), *_gate_tiles(proj),
                       _full(w_a), _full(w_h)],
                      [_out_tile(D_MODEL, BF16, D_MODEL)] * 4 + [_out_tile(GROUP_W, F32, GROUP_W),
                                                                 _out_tile(HG_W, F32, HG_W)], plans)


def _mix_out(merged, w_out, x, w_post, w_pre, name):
    def body(m_ref, wo_ref, x_ref, wp_ref, wf_ref, mo_ref, x1_ref, h2_ref):
        z = _dot(m_ref[...], wo_ref[...], NN)
        mo_ref[...] = z
        x1 = x_ref[...] + z * _rinv(z) * wp_ref[...]
        x1_ref[...] = x1
        h2_ref[...] = _bf(x1 * _rinv(x1) * wf_ref[...])

    return _rows_call(name, body, x.shape[0], 512, 1,
                      [_tile(merged, D_MODEL), _full(w_out), _tile(x, D_MODEL), _full(w_post), _full(w_pre)],
                      [_out_tile(D_MODEL, F32, D_MODEL), _out_tile(D_MODEL, F32, D_MODEL),
                       _out_tile(D_MODEL, BF16, D_MODEL)])


def _loss_head(a, w_down, x1, tgt, w, name):
    def body(a_ref, wd_ref, x1_ref, t_ref, w_ref, dx_ref, df_ref, dw_ref, loss_ref):
        z = _dot(a_ref[...], wd_ref[...], NN)
        r = _rinv(z)
        zhat = z * r
        wv = w_ref[...]
        e = x1_ref[...] + zhat * wv - t_ref[...]
        dx = e * (1.0 / D_MODEL)
        dx_ref[...] = dx
        df_ref[...] = _bf(_norm_bwd(dx, zhat, r, wv))
        _acc(dw_ref, jnp.sum(dx * zhat, axis=0, keepdims=True))
        part = 0.5 * jnp.sum(jnp.sum(e * e, axis=1, keepdims=True), axis=0, keepdims=True) * (1.0 / D_MODEL)
        _acc(loss_ref, jnp.broadcast_to(part, (1, LANES)))

    return _rows_call(name, body, x1.shape[0], 512, 1,
                      [_tile(a, D_FF), _full(w_down), _tile(x1, D_MODEL), _tile(tgt, D_MODEL), _full(w)],
                      [_out_tile(D_MODEL, F32, D_MODEL), _out_tile(D_MODEL, BF16, D_MODEL),
                       _out_acc(1, D_MODEL, D_MODEL), _out_acc(1, LANES, LANES)])


CONV_CB = D_FF // 2
CONV_TM = 512
HALO = 8
SQRT_HALF = 0.7071067811865476
INV_SQRT_2PI = 0.3989422804014327


CONV_RS = 32


def _lane_tiles():
    return [slice(k * LANES, (k + 1) * LANES) for k in range(CONV_CB // LANES)]


def _strip_start(i):
    return pl.multiple_of(i * CONV_RS, CONV_RS)


def _strip_taps(u_ref, halo_ref, r0, cs, first_strip, first_tile):
    if first_strip:
        before = jnp.where(first_tile, 0.0, halo_ref[:, cs])
        blk = jnp.concatenate([before, u_ref[0:CONV_RS, cs]], axis=0)
    else:
        blk = u_ref[pl.ds(pl.multiple_of(r0 - HALO, HALO), CONV_RS + HALO), cs]
    return pltpu.roll(blk, 2, 0)[HALO:], pltpu.roll(blk, 1, 0)[HALO:], blk[HALO:]


def _conv(taps, w_ref, b_ref, cs):
    return b_ref[:, cs] + w_ref[0:1, cs] * taps[0] + w_ref[1:2, cs] * taps[1] + w_ref[2:3, cs] * taps[2]


def _conv_specs(tm):
    nh = tm // HALO
    nc = D_FF // CONV_CB

    def tile(off):
        return pl.BlockSpec((tm, CONV_CB), lambda c, i: (i, off + c))

    def halo(off):
        return pl.BlockSpec((HALO, CONV_CB), lambda c, i: (jnp.maximum(i * nh - 1, 0), off + c))

    def small(rows, off):
        return pl.BlockSpec((rows, CONV_CB), lambda c, i: (0, off + c))

    return nc, tile, halo, small


def _conv_gelu_fwd(u, cw, cb, name, plans=None):
    S = u.shape[0]
    tm = CONV_TM
    nc, tile, halo, small = _conv_specs(tm)

    def body(ug, hg, uv, hv, wg, wv, bg, bv, a_ref):
        first_tile = pl.program_id(1) == 0

        def strip(r0, first_strip):
            for cs in _lane_tiles():
                cg = _conv(_strip_taps(ug, hg, r0, cs, first_strip, first_tile), wg, bg, cs)
                cv = _conv(_strip_taps(uv, hv, r0, cs, first_strip, first_tile), wv, bv, cs)
                a_ref[pl.ds(r0, CONV_RS), cs] = _bf(0.5 * cg * (1.0 + lax.erf(cg * SQRT_HALF)) * cv)

        strip(0, True)
        lax.fori_loop(1, tm // CONV_RS, lambda k, c: (strip(_strip_start(k), False), c)[1], 0)

    (a,), carried = _call(
        body, plans, name=name, grid=(nc, S // tm),
        in_specs=[tile(0), halo(0), tile(nc), halo(nc), small(3, 0), small(3, nc), small(1, 0), small(1, nc)],
        out_specs=[tile(0)], out_shape=[_sds((S, D_FF), BF16)], args=(u, u, u, u, cw, cw, cb, cb))
    return a if plans is None else (a, carried)


def _conv_gelu_bwd(u, dff, w_down, cw, cb, name, plans=None):
    S = u.shape[0]
    tm = CONV_TM
    nt = S // tm
    nc, tile, halo, small = _conv_specs(tm)

    def body(ug, hg, uv, hv, wg, wv, bg, bv, dff_ref, wd_ref, dcg_ref, dcv_ref, dwg_ref, dwv_ref, dbg_ref, dbv_ref,
             acc, da_ref):
        i = pl.program_id(1)
        first_tile = i == 0
        da_ref[...] = _dot(dff_ref[...], wd_ref[...], NT)

        @pl.when(first_tile)
        def _():
            acc[...] = jnp.zeros_like(acc)

        def strip(r0, first_strip):
            rows = pl.ds(r0, CONV_RS)
            for cs in _lane_tiles():
                tg = _strip_taps(ug, hg, r0, cs, first_strip, first_tile)
                tv = _strip_taps(uv, hv, r0, cs, first_strip, first_tile)
                cg = _conv(tg, wg, bg, cs)
                cv = _conv(tv, wv, bv, cs)
                phi = 0.5 * (1.0 + lax.erf(cg * SQRT_HALF))
                dav = da_ref[rows, cs]
                dcg = dav * cv * (phi + cg * jnp.exp(-0.5 * cg * cg) * INV_SQRT_2PI)
                dcv = dav * (cg * phi)
                dcg_ref[rows, cs] = dcg
                dcv_ref[rows, cs] = dcv
                for half, (dc, taps) in enumerate(((dcg, tg), (dcv, tv))):
                    for j in range(3):
                        acc[4 * half + j, :, cs] += dc * taps[j]
                    acc[4 * half + 3, :, cs] += dc

        strip(0, True)
        lax.fori_loop(1, tm // CONV_RS, lambda k, c: (strip(_strip_start(k), False), c)[1], 0)

        @pl.when(i == nt - 1)
        def _():
            for half, (dw_ref, db_ref) in enumerate(((dwg_ref, dbg_ref), (dwv_ref, dbv_ref))):
                for j in range(3):
                    dw_ref[j:j + 1, :] = jnp.sum(acc[4 * half + j], axis=0, keepdims=True)
                db_ref[...] = jnp.sum(acc[4 * half + 3], axis=0, keepdims=True)

    res, carried = _call(
        body, plans, name=name, grid=(nc, nt),
        in_specs=[tile(0), halo(0), tile(nc), halo(nc), small(3, 0), small(3, nc), small(1, 0), small(1, nc),
                  pl.BlockSpec((tm, D_MODEL), lambda c, i: (i, 0)), pl.BlockSpec((CONV_CB, D_MODEL), lambda c, i: (c, 0))],
        out_specs=[tile(0), tile(0), small(3, 0), small(3, 0), small(1, 0), small(1, 0)],
        out_shape=[_sds((S, D_FF), F32)] * 2 + [_sds((3, D_FF), F32)] * 2 + [_sds((1, D_FF), F32)] * 2,
        scratch_shapes=[pltpu.VMEM((8, CONV_RS, CONV_CB), F32), pltpu.VMEM((tm, CONV_CB), F32)],
        args=(u, u, u, u, cw, cw, cb, cb, dff, w_down))
    return res if plans is None else (res, carried)


def _conv_input_bwd(dcg, dcv, cw, name, plans=None):
    S = dcg.shape[0]
    tm = CONV_TM // 2
    nh = tm // HALO
    nt = S // tm
    n = CONV_RS + HALO
    tile = pl.BlockSpec((tm, D_FF), lambda i: (i, 0))
    nxt = pl.BlockSpec((HALO, D_FF), lambda i: (jnp.minimum((i + 1) * nh, S // HALO - 1), 0))

    def body(g_ref, ng_ref, v_ref, nv_ref, w_ref, du_ref):
        last_tile = pl.program_id(0) == nt - 1

        def strip(r0, last_strip):
            for half, (dc_ref, n_ref) in enumerate(((g_ref, ng_ref), (v_ref, nv_ref))):
                for k in range(D_FF // LANES):
                    cs = slice(k * LANES, (k + 1) * LANES)
                    ws = slice(half * D_FF + k * LANES, half * D_FF + (k + 1) * LANES)
                    if last_strip:
                        after = jnp.where(last_tile, 0.0, n_ref[:, cs])
                        blk = jnp.concatenate([dc_ref[tm - CONV_RS:tm, cs], after], axis=0)
                    else:
                        blk = dc_ref[pl.ds(r0, n), cs]
                    d1 = pltpu.roll(blk, n - 1, 0)[:CONV_RS]
                    d2 = pltpu.roll(blk, n - 2, 0)[:CONV_RS]
                    du_ref[pl.ds(r0, CONV_RS), ws] = _bf(w_ref[2:3, ws] * blk[:CONV_RS] + w_ref[1:2, ws] * d1
                                                         + w_ref[0:1, ws] * d2)

        lax.fori_loop(0, tm // CONV_RS - 1, lambda k, c: (strip(_strip_start(k), False), c)[1], 0)
        strip(tm - CONV_RS, True)

    (du,), carried = _call(
        body, plans, name=name, grid=(nt,),
        in_specs=[tile, nxt, tile, nxt, pl.BlockSpec((3, 2 * D_FF), lambda i: (0, 0))],
        out_specs=[pl.BlockSpec((tm, 2 * D_FF), lambda i: (i, 0))], out_shape=[_sds((S, 2 * D_FF), BF16)],
        args=(dcg, dcg, dcv, dcv, cw))
    return du if plans is None else (du, carried)


def _row_tile(n, cap):
    best = n
    for t in range(16, cap + 1, 16):
        if n % t == 0:
            best = t
    return best if best <= cap else n


def _rows_for_bytes(nbytes, cols):
    return max(16, nbytes // (4 * cols) // 16 * 16)


def _adamw(w, g, m, v, name):
    R, C = w.shape
    tr = _row_tile(R, _rows_for_bytes(2 << 20, C))

    def body(w_ref, g_ref, m_ref, v_ref, d_ref, nm_ref, nv_ref):
        gv = g_ref[...]
        nm = ADAM_B1 * m_ref[...] + (1.0 - ADAM_B1) * gv
        nv = ADAM_B2 * v_ref[...] + (1.0 - ADAM_B2) * (gv * gv)
        m_hat = nm / (1.0 - ADAM_B1 ** ADAM_STEP)
        v_hat = nv / (1.0 - ADAM_B2 ** ADAM_STEP)
        d_ref[...] = -ADAM_LR * (m_hat / (jnp.sqrt(v_hat) + ADAM_EPS) + ADAM_WD * w_ref[...])
        nm_ref[...] = nm
        nv_ref[...] = nv

    spec = pl.BlockSpec((tr, C), lambda i: (i, 0))
    return pl.pallas_call(body, name=name, grid=(R // tr,), in_specs=[spec] * 4, out_specs=[spec] * 3,
                          out_shape=[_sds((R, C), F32)] * 3, compiler_params=_cp(1))(w, g, m, v)


def _pair_sum(gfull, rcv, c_idx, name):
    nb, R, C = gfull.shape
    half = R // 2
    tr = _row_tile(half, _rows_for_bytes(2 << 20, C))
    nt = half // tr

    def body(c_ref, g_ref, r_ref, o_ref):
        o_ref[...] = _bf(g_ref[...] + r_ref[...])

    return pl.pallas_call(
        body, name=name,
        grid_spec=pltpu.PrefetchScalarGridSpec(
            num_scalar_prefetch=1, grid=(nb, nt),
            in_specs=[pl.BlockSpec((None, tr, C), lambda j, i, c_ref: (j, c_ref[0] * nt + i, 0)),
                      pl.BlockSpec((None, tr, C), lambda j, i, c_ref: (j, i, 0))],
            out_specs=pl.BlockSpec((None, tr, C), lambda j, i, c_ref: (j, i, 0))),
        out_shape=_sds((nb, half, C), BF16), compiler_params=_cp(2))(c_idx, gfull, rcv)


def _chip_sum(arrived, own, place, name):
    nb, H, C = arrived.shape
    tr = _row_tile(H, _rows_for_bytes(2 << 20, C))
    nt = H // tr

    def body(pl_ref, *refs):
        o_ref = refs[nb + 1]
        me = pl_ref[0]
        acc = None
        for k in range(nb):
            term = jnp.where(me == k, refs[nb][...], refs[k][...]).astype(F32)
            acc = term if acc is None else acc + term
        o_ref[...] = acc

    def other(k):
        return pl.BlockSpec((None, tr, C), lambda i, p: (jnp.where(p[0] == k, (k + 1) % nb, k), i, 0))

    return pl.pallas_call(
        body, name=name,
        grid_spec=pltpu.PrefetchScalarGridSpec(
            num_scalar_prefetch=1, grid=(nt,),
            in_specs=[other(k) for k in range(nb)] + [pl.BlockSpec((None, tr, C), lambda i, p: (p[0], i, 0))],
            out_specs=pl.BlockSpec((tr, C), lambda i, p: (p[1] * nt + i, 0))),
        out_shape=_sds((2 * H, C), F32), compiler_params=_cp(1))(place, *([arrived] * nb), own)


def _cast_into_slot(shard, place, name):
    R, C = shard.shape
    tr = _row_tile(R, 256)

    def body(pl_ref, s_ref, o_ref):
        o_ref[...] = _bf(s_ref[...])

    return pl.pallas_call(
        body, name=name,
        grid_spec=pltpu.PrefetchScalarGridSpec(
            num_scalar_prefetch=1, grid=(R // tr,),
            in_specs=[pl.BlockSpec((tr, C), lambda i, p: (i, 0))],
            out_specs=pl.BlockSpec((None, tr, C), lambda i, p: (p[0], i, 0))),
        out_shape=_sds((N_CHIPS, R, C), BF16), compiler_params=_cp(1))(place, shard)


def _place():
    x, y, c = lax.axis_index("x"), lax.axis_index("y"), lax.axis_index("c")
    chips = [(1 - x, y), (x, 1 - y), (1 - x, 1 - y)]
    return x, y, c, chips


def _chip_id(px, py):
    return 2 * px + py


def _remote(src, dst, send_sems, recv_sems, k, to):
    return pltpu.make_async_remote_copy(src_ref=src, dst_ref=dst, send_sem=send_sems.at[k], recv_sem=recv_sems.at[k],
                                        device_id=to, device_id_type=MESH)


def _proj_gathered(x, w_norm, slot, place, name, tm=512, plan=None):
    M, K = x.shape
    nb, _, Nb = slot.shape
    half = K // 2
    nt = M // tm
    cx, cy = place[0] // 2, place[0] % 2
    order = jnp.stack([place[0], _chip_id(1 - cx, cy), _chip_id(cx, 1 - cy), _chip_id(1 - cx, 1 - cy)]).astype(jnp.int32)

    p_in = [] if plan is None else plan.ins + plan.inouts
    p_out = [] if plan is None else [_sds(a.shape, a.dtype) for a in plan.inouts] + plan.outs
    n_pi, n_po = len(p_in), len(p_out)

    def body(order_ref, x_ref, wn_ref, slot_in, *refs):
        o_ref, slot_ref, h_out = refs[n_pi:n_pi + 3]
        s0 = n_pi + 3 + n_po
        w_buf, hs, ici_send, ici_recv, pass_send, pass_recv, load_sem = refs[s0:s0 + 7]

        def carried():
            if plan is None:
                return [], [], []
            ins = refs[:len(plan.ins)]
            outs = refs[n_pi + 3:n_pi + 3 + n_po]
            return plan.copies(ins, outs[:len(plan.inouts)], outs[len(plan.inouts):], *refs[s0 + 7:])

        b, i = pl.program_id(0), pl.program_id(1)
        x, y, c, chips = _place()
        me = _chip_id(x, y)
        sib = (x, y, 1 - c)
        mine, other = pl.ds(c * half, half), pl.ds((1 - c) * half, half)

        def sent(k):
            blk = slot_ref.at[me, mine]
            return _remote(blk, blk, ici_send, ici_recv, k, (*chips[k], c))

        def landed(k):
            blk = slot_ref.at[_chip_id(*chips[k]), mine]
            return _remote(blk, blk, ici_send, ici_recv, k, (*chips[k], c))

        def passed(k, rows):
            blk = slot_ref.at[_chip_id(*chips[k]), rows]
            return _remote(blk, blk, pass_send, pass_recv, k, sib)

        @pl.when((b == 0) & (i == 0))
        def _():
            for k in range(len(chips)):
                sent(k).start()
            sends, _, local = carried()
            for cp in (*sends, *local):
                cp.start()

        for k in range(len(chips)):
            @pl.when((b == k + 1) & (i == 0))
            def _(k=k):
                landed(k).wait_recv()
                passed(k, mine).start()
                passed(k, other).wait_recv()

        @pl.when(i == 0)
        def _():
            load = pltpu.make_async_copy(slot_ref.at[order_ref[b]], w_buf, load_sem.at[0])
            load.start()
            load.wait()

        rows = pl.ds(pl.multiple_of(i * tm, tm), tm)
        keep_h = pltpu.make_async_copy(hs, h_out, load_sem.at[1])

        @pl.when(b == 0)
        def _():
            xv = x_ref[...]
            hs[rows, :] = _bf(xv * _rinv(xv) * wn_ref[...])

        @pl.when((b == 1) & (i == 0))
        def _():
            keep_h.start()

        o_ref[...] = _dot(hs[rows, :], w_buf[...], NN)

        @pl.when((b == nb - 1) & (i == nt - 1))
        def _():
            for k in range(len(chips)):
                sent(k).wait_send()
                passed(k, mine).wait_send()
            sends, recvs, local = carried()
            for cp in recvs:
                cp.wait_recv()
            for cp in sends:
                cp.wait_send()
            for cp in local:
                cp.wait()
            keep_h.wait()

    n_peers = N_CHIPS - 1
    return pl.pallas_call(
        body, name=name,
        grid_spec=pltpu.PrefetchScalarGridSpec(
            num_scalar_prefetch=1, grid=(nb, nt),
            in_specs=[pl.BlockSpec((tm, K), lambda b, i, o: (i, 0)), pl.BlockSpec((1, K), lambda b, i, o: (0, 0)), ANY]
            + [ANY] * n_pi,
            out_specs=[pl.BlockSpec((tm, Nb), lambda b, i, o: (i, o[b])), ANY, ANY] + [ANY] * n_po,
            scratch_shapes=[pltpu.VMEM((K, Nb), BF16), pltpu.VMEM((M, K), BF16)]
            + [pltpu.SemaphoreType.DMA((n_peers,))] * 4 + [pltpu.SemaphoreType.DMA((2,))]
            + ([] if plan is None else [pltpu.SemaphoreType.DMA((plan.n_sems,))] * 3)),
        out_shape=[_sds((M, nb * Nb), F32), _sds(slot.shape, slot.dtype), _sds((M, K), BF16)] + p_out,
        input_output_aliases={3: 1, **({} if plan is None else
                                       {4 + len(plan.ins) + a: 3 + a for a in range(len(plan.inouts))})},
        compiler_params=_cp(2))(order, x, w_norm, slot, *p_in)


def _gather_ici_plan(slots, wholes, part=None):
    ns, nw = len(slots), len(wholes)

    def copies(ins, ios, outs, send_sems, recv_sems, local_sems):
        x, y, c, chips = _place()
        me = _chip_id(x, y)
        sends, recvs = [], []
        for a in range(ns + nw):
            dst = ios[a] if a < ns else outs[a - ns]
            R = dst.shape[1]
            r0, nr = (0, R // 2) if part is None else part
            rows = pl.ds(c * (R // 2) + r0, nr) if a < ns else pl.ds(0, R)
            src = dst.at[me, rows] if a < ns else ins[a - ns]
            for j, chip in enumerate(chips):
                sends.append(_remote(src, dst.at[me, rows], send_sems, recv_sems, 3 * a + j, (*chip, c)))
                landed = dst.at[_chip_id(*chip), rows]
                recvs.append(_remote(landed, landed, send_sems, recv_sems, 3 * a + j, (*chip, c)))
        local = [pltpu.make_async_copy(ins[b], outs[b].at[me], local_sems.at[b]) for b in range(nw)]
        return sends, recvs, local

    return _Plan(copies, 3 * (ns + nw), ins=wholes, inouts=slots,
                 outs=[_sds((N_CHIPS, *s.shape), s.dtype) for s in wholes])


def _gather_pass_plan(slots):
    def copies(ins, ios, outs, send_sems, recv_sems, local_sems):
        x, y, c, chips = _place()
        sib = (x, y, 1 - c)
        sends, recvs = [], []
        for a, buf in enumerate(ios):
            half = buf.shape[1] // 2
            for j, chip in enumerate(chips):
                mine = buf.at[_chip_id(*chip), pl.ds(c * half, half)]
                other = buf.at[_chip_id(*chip), pl.ds((1 - c) * half, half)]
                sends.append(_remote(mine, mine, send_sems, recv_sems, 3 * a + j, sib))
                recvs.append(_remote(other, other, send_sems, recv_sems, 3 * a + j, sib))
        return sends, recvs, []

    return _Plan(copies, 3 * len(slots), inouts=slots)


def _pair_plan(grads):
    def copies(ins, ios, outs, send_sems, recv_sems, local_sems):
        x, y, c, _ = _place()
        sib = (x, y, 1 - c)
        sends, recvs = [], []
        for a, g in enumerate(ins):
            half = g.shape[1] // 2
            sends.append(_remote(g.at[:, pl.ds((1 - c) * half, half), :], outs[a], send_sems, recv_sems, a, sib))
            recvs.append(_remote(outs[a], outs[a], send_sems, recv_sems, a, sib))
        return sends, recvs, []

    return _Plan(copies, len(grads), ins=grads,
                 outs=[_sds((g.shape[0], g.shape[1] // 2, g.shape[2]), g.dtype) for g in grads])


def _chip_plan(parts):
    def copies(ins, ios, outs, send_sems, recv_sems, local_sems):
        x, y, c, chips = _place()
        me = _chip_id(x, y)
        sends, recvs = [], []
        for a, part in enumerate(ins):
            for j, chip in enumerate(chips):
                sends.append(_remote(part.at[_chip_id(*chip)], outs[a].at[me], send_sems, recv_sems, 3 * a + j, (*chip, c)))
                landed = outs[a].at[_chip_id(*chip)]
                recvs.append(_remote(landed, landed, send_sems, recv_sems, 3 * a + j, (*chip, c)))
        return sends, recvs, []

    return _Plan(copies, 3 * len(parts), ins=parts, outs=[_sds(p.shape, p.dtype) for p in parts])


def _all_sum(pack, fulls, name):
    R, C = pack.shape
    n = len(fulls)

    def body(p_ref, *refs):
        o_ref, halves = refs[n], refs[n + 1:2 * n + 1]
        buf, send_sems, recv_sems, pair_send, pair_recv = refs[2 * n + 1:]
        x, y, c, _ = _place()
        sib = (x, y, 1 - c)
        pair = []
        for a, full in enumerate(halves):
            H = full.shape[0] // 2
            mine = full.at[pl.ds(c * H, H)]
            cp = _remote(mine, mine, pair_send, pair_recv, a, sib)
            cp.start()
            pair.append(cp)
        me = 4 * x + 2 * y + c
        buf[me] = p_ref[...]
        cps = []
        for k in range(1, N_DEV):
            to = (x ^ (k >> 2), y ^ ((k >> 1) & 1), c ^ (k & 1))
            cp = _remote(p_ref, buf.at[me], send_sems, recv_sems, k - 1, to)
            cp.start()
            cps.append(cp)
        for k in range(1, N_DEV):
            frm = (x ^ (k >> 2), y ^ ((k >> 1) & 1), c ^ (k & 1))
            slot = buf.at[4 * frm[0] + 2 * frm[1] + frm[2]]
            _remote(slot, slot, send_sems, recv_sems, k - 1, frm).wait_recv()
        acc = buf[0]
        for k in range(1, N_DEV):
            acc = acc + buf[k]
        o_ref[...] = acc
        for cp in cps:
            cp.wait_send()
        for a, (full, cp) in enumerate(zip(halves, pair)):
            H = full.shape[0] // 2
            other = full.at[pl.ds((1 - c) * H, H)]
            _remote(other, other, pair_send, pair_recv, a, sib).wait_recv()
            cp.wait_send()

    vm = pl.BlockSpec(memory_space=pltpu.VMEM)
    res = pl.pallas_call(
        body, name=name, in_specs=[vm] + [ANY] * n, out_specs=[vm] + [ANY] * n,
        out_shape=[_sds((R, C), F32)] + [_sds(f.shape, f.dtype) for f in fulls],
        input_output_aliases={1 + a: 1 + a for a in range(n)},
        scratch_shapes=[pltpu.VMEM((N_DEV, R, C), F32), pltpu.SemaphoreType.DMA((N_DEV - 1,)),
                        pltpu.SemaphoreType.DMA((N_DEV - 1,)), pltpu.SemaphoreType.DMA((n,)),
                        pltpu.SemaphoreType.DMA((n,))])(pack, *fulls)
    return res[0], list(res[1:])


def _local_step(xs, tgt, p, ex):
    proj, h1 = ex.project(xs, p["pre_mix_norm"])
    biases = _relbias_fwd(p["rel_bias"], "rel_bias_fwd")
    fw = []
    for g in range(N_GROUPS):
        res, got = _attn_fwd(proj, biases[g], g, f"attn_fwd{g}", plans=ex.carry(f"attn_fwd{g}"))
        ex.done(f"attn_fwd{g}", got)
        fw.append(res)
    (yh, o_h, states), got = _hgrn_fwd(proj, p["hgrn_lb_raw"], p["hgrn_norm"], "hgrn_fwd", plans=ex.carry("hgrn_fwd"))
    ex.done("hgrn_fwd", got)
    W_a, W_h, W_out = ex.weight("w_branch_attn"), ex.weight("w_branch_hgrn"), ex.weight("w_out")
    (y, lse, za, zh, merged), got = _branch_fwd([t[0] for t in fw], [t[1] for t in fw], yh, proj, W_a, W_h,
                                                "branch_fwd", plans=ex.carry("branch_fwd"))
    ex.done("branch_fwd", got)
    W_up, conv_w = ex.weight("w_up"), ex.weight("conv_w")
    mo, x1, h2 = _mix_out(merged, W_out, xs, p["post_mix_norm"], p["pre_ffn_norm"], "mix_out")
    u, got = _mm_nn_blk(h2, W_up, "ffn_up", tm=1024, plans=ex.carry("ffn_up"))
    ex.done("ffn_up", got)
    a, got = _conv_gelu_fwd(u, conv_w, p["conv_b"], "conv_gelu_fwd", plans=ex.carry("conv_gelu_fwd"))
    ex.done("conv_gelu_fwd", got)
    W_down = ex.weight("w_down")
    dx2, dff, g_post_ffn, loss = _loss_head(a, W_down, x1, tgt, p["post_ffn_norm"], "ffn_down_loss")

    ex.grad("w_down", _mm_tn(a, dff, "g_w_down").reshape(N_CHIPS, D_FF // N_CHIPS, D_MODEL))
    (dcg, dcv, gwg, gwv, gbg, gbv), got = _conv_gelu_bwd(u, dff, W_down, conv_w, p["conv_b"], "conv_gelu_bwd",
                                                          plans=ex.carry("conv_gelu_bwd"))
    ex.done("conv_gelu_bwd", got)
    g_conv_w = jnp.concatenate([gwg, gwv], axis=1)
    g_conv_b = jnp.concatenate([gbg, gbv], axis=1)
    du, got = _conv_input_bwd(dcg, dcv, conv_w, "conv_input_bwd", plans=ex.carry("conv_input_bwd"))
    ex.done("conv_input_bwd", got)
    dh2 = _mm_nt_blk(du, W_up, "d_ffn_in")
    ex.grad("w_up", _mm_tn_blk(h2, du, N_CHIPS, "g_w_up"))
    dx1, g_pre_ffn = _prenorm_bwd(dh2, x1, p["pre_ffn_norm"], dx2, "pre_ffn_norm_bwd")
    (dmo, dmerged, g_post_mix), got = _postnorm_bwd(dx1, mo, p["post_mix_norm"], W_out, "post_mix_norm_bwd",
                                                    plans=ex.carry("post_mix_norm_bwd"))
    ex.done("post_mix_norm_bwd", got)
    ex.grad("w_out", _mm_tn(merged, dmo, "g_w_out").reshape(N_CHIPS, D_MODEL // N_CHIPS, D_MODEL))
    (dza, dzh, dg0, dg1, dy, dyh), got = _branch_bwd(dmerged, za, zh, proj, W_a, W_h, "branch_bwd",
                                                     plans=ex.carry("branch_bwd"))
    ex.done("branch_bwd", got)
    ex.grad("w_branch_attn", _mm_tn_blk(y, dza, N_CHIPS, "g_w_branch_attn", together=True))
    ex.grad("w_branch_hgrn", _mm_tn_blk(yh, dzh, N_CHIPS, "g_w_branch_hgrn", together=True))
    dqkv, dbs = [], []
    for g in range(N_GROUPS):
        parts, db, got = _attn_bwd(proj, biases[g], lse, y, dy, g, f"attn_bwd{g}", plans=ex.carry(f"attn_bwd{g}"))
        ex.done(f"attn_bwd{g}", got)
        dqkv += parts
        dbs.append(db)
    g_rel_bias = _relbias_bwd(dbs, "rel_bias_bwd")
    dproj, g_lb_raw, g_hgrn_norm = _hgrn_bwd(proj, p["hgrn_lb_raw"], p["hgrn_norm"], o_h, states, dyh, dqkv,
                                             [dg0, dg1], "hgrn_bwd")
    for piece in W_IN_PIECES:
        g, got = _mm_tn_blk(h1, dproj, N_CHIPS, f"g_{piece}", x_cols=W_IN_ROWS[piece],
                            plans=ex.carry(f"g_{piece}"))
        ex.done(f"g_{piece}", got)
        ex.grad(piece, g)
    dh1, got = _mm_nt_blk(dproj, ex.weight("w_in"), "d_proj_in", plans=ex.carry("d_proj_in"))
    ex.done("d_proj_in", got)
    (grad_x, g_pre_mix), got = _prenorm_bwd(dh1, xs, p["pre_mix_norm"], dx1, "pre_mix_norm_bwd",
                                            plans=ex.carry("pre_mix_norm_bwd"))
    ex.done("pre_mix_norm_bwd", got)
    small = dict(pre_mix_norm=g_pre_mix, rel_bias=g_rel_bias, hgrn_lb_raw=g_lb_raw, hgrn_norm=g_hgrn_norm,
                 post_mix_norm=g_post_mix, pre_ffn_norm=g_pre_ffn, conv_w=g_conv_w, conv_b=g_conv_b,
                 post_ffn_norm=g_post_ffn)
    return loss, grad_x, small


SMALL = ("pre_mix_norm", "rel_bias", "hgrn_lb_raw", "hgrn_norm", "post_mix_norm", "pre_ffn_norm", "conv_w", "conv_b",
         "post_ffn_norm")
BIG = ("w_in", "w_up", "w_down", "w_out", "w_branch_attn", "w_branch_hgrn")
WEIGHTS = ("pre_mix_norm", "w_in", "rel_bias", "hgrn_lb_raw", "hgrn_norm", "w_branch_attn", "w_branch_hgrn", "w_out",
           "post_mix_norm", "pre_ffn_norm", "w_up", "conv_w", "conv_b", "w_down", "post_ffn_norm")
MIXER = ("w_out", "w_branch_attn", "w_branch_hgrn")

ICI_PARTS = {"gather_ici_1of3": (0, 176), "gather_ici_2of3": (176, 176), "gather_ici_3of3": (352, 160)}
SCHEDULE = {
    "proj_in": [("gather_ici_cw", MIXER)],
    "attn_fwd0": [("gather_pass", MIXER), ("gather_ici_1of3", ("w_up",))],
    "attn_fwd1": [("gather_ici_2of3", ("w_up",))],
    "attn_fwd2": [("gather_ici_3of3", ("w_up",))],
    "hgrn_fwd": [("gather_pass", ("w_up",))],
    "ffn_up": [("gather_ici", ("w_down",))],
    "conv_gelu_fwd": [("gather_pass", ("w_down",))],
    "conv_gelu_bwd": [("pair", ("w_down",))],
    "conv_input_bwd": [("chip", ("w_down",))],
    "post_mix_norm_bwd": [("pair", ("w_up",))],
    "attn_bwd0": [("chip", ("w_up",)), ("pair", MIXER)],
    "attn_bwd1": [("chip", MIXER)],
    "g_w_in_b": [("pair", ("w_in_a",))],
    "d_proj_in": [("chip", ("w_in_a",)), ("pair", ("w_in_b",))],
    "pre_mix_norm_bwd": [("chip", ("w_in_b",))],
}
W_IN_ROWS = dict(w_in_a=(0, 768), w_in_b=(3, 256))
W_IN_PIECES = tuple(W_IN_ROWS)
REDUCED = W_IN_PIECES + BIG[1:]


class _Exchange:
    def __init__(self, place, slots, conv_w_shard):
        self.place, self.slots, self.conv_w_shard = place, dict(slots), conv_w_shard
        self.conv_w = None
        self.g, self.from_sibling, self.pair_sums, self.arrived = {}, {}, {}, {}
        self.pending = []

    def weight(self, name):
        if name == "conv_w":
            return self.conv_w
        w = self.slots[name]
        return w.reshape(-1, D_MODEL) if name in ("w_out", "w_down") else w

    def project(self, x, w_norm):
        (plan,) = self.carry("proj_in")
        proj, self.slots["w_in"], h, *got = _proj_gathered(x, w_norm, self.slots["w_in"], self.place, "proj_in",
                                                           plan=plan)
        self.done("proj_in", [got])
        return proj, h

    def grad(self, name, g):
        self.g[name] = g

    def carry(self, point):
        plans = []
        self.pending = SCHEDULE.get(point, [])
        for kind, names in self.pending:
            if kind in ("gather_ici", "gather_ici_cw") or kind in ICI_PARTS:
                wholes = [self.conv_w_shard] if kind == "gather_ici_cw" else []
                plans.append(_gather_ici_plan([self.slots[n] for n in names], wholes, ICI_PARTS.get(kind)))
            elif kind == "gather_pass":
                plans.append(_gather_pass_plan([self.slots[n] for n in names]))
            elif kind == "pair":
                plans.append(_pair_plan([self.g[n] for n in names]))
            else:
                for n in names:
                    self.pair_sums[n] = _pair_sum(self.g[n], self.from_sibling[n], self.place[1:2], f"pair_sum_{n}")
                plans.append(_chip_plan([self.pair_sums[n] for n in names]))
        return plans

    def done(self, point, carried):
        for (kind, names), got in zip(self.pending, carried):
            if kind in ("gather_ici", "gather_ici_cw", "gather_pass") or kind in ICI_PARTS:
                self.slots.update(zip(names, got))
                if kind == "gather_ici_cw":
                    self.conv_w = got[len(names)].transpose(1, 0, 2).reshape(3, 2 * D_FF)
            elif kind == "pair":
                self.from_sibling.update(zip(names, got))
            else:
                self.arrived.update(zip(names, got))

    def reduced_halves(self):
        return [_chip_sum(self.arrived[n], self.pair_sums[n], self.place, f"chip_sum_{n}") for n in REDUCED]


def kernel(x, pre_mix_norm, w_in, rel_bias, hgrn_lb_raw, hgrn_norm, w_branch_attn, w_branch_hgrn, w_out, post_mix_norm, pre_ffn_norm, w_up, conv_w, conv_b, w_down, post_ffn_norm, loss_target, m_pre_mix_norm, m_w_in, m_rel_bias, m_hgrn_lb_raw, m_hgrn_norm, m_w_branch_attn, m_w_branch_hgrn, m_w_out, m_post_mix_norm, m_pre_ffn_norm, m_w_up, m_conv_w, m_conv_b, m_w_down, m_post_ffn_norm, v_pre_mix_norm, v_w_in, v_rel_bias, v_hgrn_lb_raw, v_hgrn_norm, v_w_branch_attn, v_w_branch_hgrn, v_w_out, v_post_mix_norm, v_pre_ffn_norm, v_w_up, v_conv_w, v_conv_b, v_w_down, v_post_ffn_norm):
    w = dict(pre_mix_norm=pre_mix_norm, w_in=w_in, rel_bias=rel_bias, hgrn_lb_raw=hgrn_lb_raw, hgrn_norm=hgrn_norm,
             w_branch_attn=w_branch_attn, w_branch_hgrn=w_branch_hgrn, w_out=w_out, post_mix_norm=post_mix_norm,
             pre_ffn_norm=pre_ffn_norm, w_up=w_up, conv_w=conv_w, conv_b=conv_b, w_down=w_down,
             post_ffn_norm=post_ffn_norm)
    m = dict(pre_mix_norm=m_pre_mix_norm, w_in=m_w_in, rel_bias=m_rel_bias, hgrn_lb_raw=m_hgrn_lb_raw,
             hgrn_norm=m_hgrn_norm, w_branch_attn=m_w_branch_attn, w_branch_hgrn=m_w_branch_hgrn, w_out=m_w_out,
             post_mix_norm=m_post_mix_norm, pre_ffn_norm=m_pre_ffn_norm, w_up=m_w_up, conv_w=m_conv_w,
             conv_b=m_conv_b, w_down=m_w_down, post_ffn_norm=m_post_ffn_norm)
    v = dict(pre_mix_norm=v_pre_mix_norm, w_in=v_w_in, rel_bias=v_rel_bias, hgrn_lb_raw=v_hgrn_lb_raw,
             hgrn_norm=v_hgrn_norm, w_branch_attn=v_w_branch_attn, w_branch_hgrn=v_w_branch_hgrn, w_out=v_w_out,
             post_mix_norm=v_post_mix_norm, pre_ffn_norm=v_pre_ffn_norm, w_up=v_w_up, conv_w=v_conv_w,
             conv_b=v_conv_b, w_down=v_w_down, post_ffn_norm=v_post_ffn_norm)
    shard2d = {n: (w[n][0] if w[n].ndim == 3 else w[n]) for n in WEIGHTS}
    chip = 2 * lax.axis_index("x") + lax.axis_index("y")
    core = lax.axis_index("c")

    place = jnp.stack([chip, core]).astype(jnp.int32)
    slots = {n: _cast_into_slot(shard2d[n], place, f"cast_{n}") for n in BIG}
    ex = _Exchange(place, slots, shard2d["conv_w"])
    loss, grad_x, small = _local_step(x[0], loss_target[0], {n: w[n] for n in SMALL if n != "conv_w"}, ex)

    flat = [small[n].reshape(-1) for n in SMALL] + [loss.reshape(-1)]
    sizes = [t.shape[0] for t in flat]
    summed, wholes = _all_sum(jnp.concatenate(flat).reshape(-1, LANES), ex.reduced_halves(), "sum_small")
    summed = summed.reshape(-1)
    offs = [sum(sizes[:i]) for i in range(len(sizes))]
    grads = {}
    for n, o, sz in zip(SMALL, offs, sizes):
        grads[n] = summed[o:o + sz].reshape(small[n].shape)
    loss_total = summed[offs[-1]]
    cw = 2 * D_FF // N_CHIPS
    grads["conv_w"] = lax.dynamic_slice(grads["conv_w"], (0, chip * cw), (3, cw))

    big = dict(zip(REDUCED, wholes))
    big["w_in"] = jnp.concatenate([big.pop(n) for n in W_IN_PIECES], axis=0)
    grads.update(big)

    out_g, out_d, out_m, out_v = [], [], [], []
    for n in WEIGHTS:
        d2, m2, v2 = _adamw(shard2d[n], grads[n], m[n].reshape(shard2d[n].shape), v[n].reshape(shard2d[n].shape),
                            f"adamw_{n}")
        shape = w[n].shape
        out_g.append(grads[n].reshape(shape))
        out_d.append(d2.reshape(shape))
        out_m.append(m2.reshape(shape))
        out_v.append(v2.reshape(shape))
    return (loss_total, grad_x[None], *out_g, *out_d, *out_m, *out_v)
```

```python
import functools
import math

import jax
import jax.numpy as jnp
from jax import lax
from jax.experimental import pallas as pl
from jax.experimental.pallas import tpu as pltpu

F32 = jnp.float32
BF16 = jnp.bfloat16
MESH = pl.DeviceIdType.MESH

D_MODEL = 1024
N_GROUPS = 3
DILATIONS = (1, 4, 16)
HEADS = 8
HEAD_DIM = 64
GROUP_W = HEADS * HEAD_DIM
QKV_W = N_GROUPS * 3 * GROUP_W
BLK = 128
NEG_INF = -1e30
NUM_BUCKETS = 32
MAX_EXACT = 16
MAX_DISTANCE = 2048
HG_HEADS = 4
HG_DK = 128
HG_W = HG_HEADS * HG_DK
HG_CHUNK = 32
HG_TILE = 256
IN_W = QKV_W + 4 * HG_W + 2 * D_MODEL
D_FF = 2816
EPS = 1e-6
N_CHIPS = 4
N_DEV = 8
LANES = 128

ADAM_LR, ADAM_B1, ADAM_B2, ADAM_EPS, ADAM_WD, ADAM_STEP = 0.001, 0.9, 0.999, 1e-08, 0.01, 10

VMEM_LIMIT = 56 * 1024 * 1024


def _cp(n_axes):
    return pltpu.CompilerParams(dimension_semantics=("arbitrary",) * n_axes, vmem_limit_bytes=VMEM_LIMIT)


def _sds(shape, dtype):
    return jax.ShapeDtypeStruct(tuple(shape), dtype)


def _sigmoid(v):
    return 1.0 / (1.0 + jnp.exp(-v))


def _bf(v):
    return v.astype(BF16)


def _dot(a, b, dims):
    return lax.dot_general(a, b, (dims, ((), ())), preferred_element_type=F32)


NN = ((1,), (0,))
NT = ((1,), (1,))
TN = ((0,), (0,))

ANY = pl.BlockSpec(memory_space=pl.ANY)


class _Plan:
    def __init__(self, copies, n_sems, ins=(), inouts=(), outs=()):
        self.copies, self.n_sems = copies, n_sems
        self.ins, self.inouts, self.outs = list(ins), list(inouts), list(outs)


def _call(body, plans=None, *, name, grid, in_specs, out_specs, out_shape, args, scratch_shapes=()):
    plans = list(plans or ())
    in_specs, out_specs, out_shape = list(in_specs), list(out_specs), list(out_shape)
    scratch_shapes = list(scratch_shapes)
    n_in, n_out, n_scr = len(in_specs), len(out_specs), len(scratch_shapes)
    x_in, x_out, aliases, spans = [], [], {}, []
    for p in plans:
        i0, o0 = len(x_in), len(x_out)
        x_in += p.ins
        for a in p.inouts:
            aliases[n_in + len(x_in)] = n_out + len(x_out)
            x_in.append(a)
            x_out.append(_sds(a.shape, a.dtype))
        x_out += p.outs
        spans.append((i0, len(p.ins), o0, len(p.inouts), len(p.outs)))
    sems = [pltpu.SemaphoreType.DMA((p.n_sems,)) for p in plans for _ in range(3)]

    def wrapped(*refs):
        xi = refs[n_in:n_in + len(x_in)]
        base = n_in + len(x_in)
        xo = refs[base + n_out:base + n_out + len(x_out)]
        sbase = base + n_out + len(x_out)
        xs = refs[sbase + n_scr:]
        ids = [pl.program_id(k) for k in range(len(grid))]
        first = functools.reduce(jnp.logical_and, [i == 0 for i in ids])
        last = functools.reduce(jnp.logical_and, [i == g - 1 for i, g in zip(ids, grid)])

        def descriptors(k):
            i0, ni, o0, nio, no = spans[k]
            return plans[k].copies(xi[i0:i0 + ni], xo[o0:o0 + nio], xo[o0 + nio:o0 + nio + no], *xs[3 * k:3 * k + 3])

        @pl.when(first)
        def _():
            for k in range(len(plans)):
                sends, _, local = descriptors(k)
                for cp in (*sends, *local):
                    cp.start()

        body(*refs[:n_in], *refs[base:base + n_out], *refs[sbase:sbase + n_scr])

        @pl.when(last)
        def _():
            for k in range(len(plans)):
                sends, recvs, local = descriptors(k)
                for cp in recvs:
                    cp.wait_recv()
                for cp in sends:
                    cp.wait_send()
                for cp in local:
                    cp.wait()

    res = pl.pallas_call(
        wrapped if plans else body, name=name, grid=grid, in_specs=in_specs + [ANY] * len(x_in),
        out_specs=out_specs + [ANY] * len(x_out), out_shape=out_shape + x_out, input_output_aliases=aliases,
        scratch_shapes=scratch_shapes + sems, compiler_params=_cp(len(grid)))(*args, *x_in)
    res = list(res)
    carried = [res[n_out + o0:n_out + o0 + nio + no] for (_, _, o0, nio, no) in spans]
    return res[:n_out], carried


def _mm_nn_blk(a, wg, name, tm=512, plans=None):
    M, K = a.shape
    nb, _, Nb = wg.shape

    def body(a_ref, w_ref, o_ref):
        o_ref[...] = _dot(_bf(a_ref[...]), w_ref[...], NN)

    (out,), carried = _call(
        body, plans, name=name, grid=(nb, M // tm),
        in_specs=[pl.BlockSpec((tm, K), lambda j, i: (i, 0)), pl.BlockSpec((None, K, Nb), lambda j, i: (j, 0, 0))],
        out_specs=[pl.BlockSpec((tm, Nb), lambda j, i: (i, j))],
        out_shape=[_sds((M, nb * Nb), F32)], args=(a, wg))
    return out if plans is None else (out, carried)


def _mm_nt_blk(dy, wg, name, tm=1024, plans=None):
    M = dy.shape[0]
    nb, K, Nb = wg.shape

    def body(dy_ref, w_ref, o_ref):
        j = pl.program_id(1)
        r = _dot(_bf(dy_ref[...]), w_ref[...], NT)

        @pl.when(j == 0)
        def _():
            o_ref[...] = r

        @pl.when(j > 0)
        def _():
            o_ref[...] += r

    (out,), carried = _call(
        body, plans, name=name, grid=(M // tm, nb),
        in_specs=[pl.BlockSpec((tm, Nb), lambda i, j: (i, j)), pl.BlockSpec((None, K, Nb), lambda i, j: (j, 0, 0))],
        out_specs=[pl.BlockSpec((tm, K), lambda i, j: (i, 0))],
        out_shape=[_sds((M, K), F32)], args=(dy, wg))
    return out if plans is None else (out, carried)


def _mm_nt_prenorm_bwd(dy, wg, xin, w, dres, name, tm=1024):
    M = dy.shape[0]
    nb, K, Nb = wg.shape

    def body(dy_ref, w_ref, x_ref, wn_ref, dres_ref, dx_ref, dw_ref, acc):
        i, j = pl.program_id(0), pl.program_id(1)
        r = _dot(_bf(dy_ref[...]), w_ref[...], NT)

        @pl.when(j == 0)
        def _():
            acc[...] = r

        @pl.when(j > 0)
        def _():
            acc[...] += r

        @pl.when(j == nb - 1)
        def _():
            xv = x_ref[...]
            rinv = _rinv(xv)
            xhat = xv * rinv
            dh = acc[...]
            dx_ref[...] = dres_ref[...] + _norm_bwd(dh, xhat, rinv, wn_ref[...])
            part = jnp.sum(dh * xhat, axis=0, keepdims=True)

            @pl.when(i == 0)
            def _():
                dw_ref[...] = part

            @pl.when(i > 0)
            def _():
                dw_ref[...] += part

    row = pl.BlockSpec((tm, K), lambda i, j: (i, 0))
    vec = pl.BlockSpec((1, K), lambda i, j: (0, 0))
    return pl.pallas_call(
        body, name=name, grid=(M // tm, nb),
        in_specs=[pl.BlockSpec((tm, Nb), lambda i, j: (i, j)), pl.BlockSpec((None, K, Nb), lambda i, j: (j, 0, 0)),
                  row, vec, row],
        out_specs=[row, vec], out_shape=[_sds((M, K), F32), _sds((1, K), F32)],
        scratch_shapes=[pltpu.VMEM((tm, K), F32)], compiler_params=_cp(2))(dy, wg, xin, w, dres)


def _mm_tn_blk(x, dy, nb, name, tk=2048, x_cols=None, plans=None, together=False):
    T, Mx = x.shape
    xk, Mx = (0, Mx) if x_cols is None else x_cols
    Nb = dy.shape[1] // nb
    nj = nb if together else 1

    def body(x_ref, dy_ref, o_ref):
        t = pl.program_id(1)
        r = _dot(_bf(x_ref[...]), _bf(dy_ref[...]), TN)
        for j in range(nj):
            rj = r[:, j * Nb:(j + 1) * Nb]

            @pl.when(t == 0)
            def _():
                o_ref[j] = rj

            @pl.when(t > 0)
            def _():
                o_ref[j] += rj

    (out,), carried = _call(
        body, plans, name=name, grid=(nb // nj, T // tk),
        in_specs=[pl.BlockSpec((tk, Mx), lambda j, t: (t, xk)), pl.BlockSpec((tk, nj * Nb), lambda j, t: (t, j))],
        out_specs=[pl.BlockSpec((nj, Mx, Nb), lambda j, t: (j, 0, 0))],
        out_shape=[_sds((nb, Mx, Nb), F32)], args=(x, dy))
    return out if plans is None else (out, carried)


def _mm_tn(x, dy, name, tk=1024):
    T, Mx = x.shape
    N = dy.shape[1]

    def body(x_ref, dy_ref, o_ref):
        t = pl.program_id(0)
        r = _dot(_bf(x_ref[...]), _bf(dy_ref[...]), TN)

        @pl.when(t == 0)
        def _():
            o_ref[...] = r

        @pl.when(t > 0)
        def _():
            o_ref[...] += r

    return pl.pallas_call(
        body, name=name, grid=(T // tk,),
        in_specs=[pl.BlockSpec((tk, Mx), lambda t: (t, 0)), pl.BlockSpec((tk, N), lambda t: (t, 0))],
        out_specs=pl.BlockSpec((Mx, N), lambda t: (0, 0)),
        out_shape=_sds((Mx, N), F32), compiler_params=_cp(1))(x, dy)


def _tile(arr, bw, col=lambda c: 0):
    return ("tile", arr, bw, col)


def _full(arr):
    return ("full", arr)


def _out_tile(width, dtype, bw, col=lambda c: 0):
    return ("tile", width, dtype, bw, col)


def _out_acc(rows, width, bw, col=lambda c: 0):
    return ("acc", rows, width, bw, col)


def _rows_call(name, body, n_rows, tm, ncol, ins, outs, plans=None):
    in_specs, args = [], []
    for e in ins:
        if e[0] == "tile":
            _, arr, bw, col = e
            in_specs.append(pl.BlockSpec((tm, bw), functools.partial(lambda c, i, col: (i, col(c)), col=col)))
        else:
            arr = e[1]
            in_specs.append(pl.BlockSpec(arr.shape, functools.partial(lambda c, i, nd: (0,) * nd, nd=arr.ndim)))
        args.append(arr)
    out_specs, out_shape = [], []
    for e in outs:
        if e[0] == "tile":
            _, width, dtype, bw, col = e
            out_specs.append(pl.BlockSpec((tm, bw), functools.partial(lambda c, i, col: (i, col(c)), col=col)))
            out_shape.append(_sds((n_rows, width), dtype))
        else:
            _, rows, width, bw, col = e
            out_specs.append(pl.BlockSpec((rows, bw), functools.partial(lambda c, i, col: (0, col(c)), col=col)))
            out_shape.append(_sds((rows, width), F32))
    out, carried = _call(body, plans, name=name, grid=(ncol, n_rows // tm), in_specs=in_specs, out_specs=out_specs,
                         out_shape=out_shape, args=args)
    return out if plans is None else (out, carried)


def _acc(ref, val):
    i = pl.program_id(1)

    @pl.when(i == 0)
    def _():
        ref[...] = val

    @pl.when(i > 0)
    def _():
        ref[...] += val


def _rinv(z):
    return lax.rsqrt(jnp.mean(z * z, axis=-1, keepdims=True) + EPS)


def _norm_bwd(dy, zhat, r, w):
    dyw = dy * w
    return r * (dyw - zhat * jnp.mean(dyw * zhat, axis=-1, keepdims=True))


def _prenorm_bwd(dh, xin, w, dres, name, plans=None):
    def body(dh_ref, x_ref, w_ref, dres_ref, dx_ref, dw_ref):
        xv = x_ref[...]
        r = _rinv(xv)
        xhat = xv * r
        dhv = dh_ref[...]
        dx_ref[...] = dres_ref[...] + _norm_bwd(dhv, xhat, r, w_ref[...])
        _acc(dw_ref, jnp.sum(dhv * xhat, axis=0, keepdims=True))

    return _rows_call(name, body, xin.shape[0], 512, 1,
                      [_tile(dh, D_MODEL), _tile(xin, D_MODEL), _full(w), _tile(dres, D_MODEL)],
                      [_out_tile(D_MODEL, F32, D_MODEL), _out_acc(1, D_MODEL, D_MODEL)], plans)


def _postnorm_bwd(dout, z, w, w_mat, name, plans=None):
    def body(do_ref, z_ref, w_ref, wm_ref, dz_ref, dm_ref, dw_ref):
        zv = z_ref[...]
        r = _rinv(zv)
        zhat = zv * r
        dov = do_ref[...]
        dz = _bf(_norm_bwd(dov, zhat, r, w_ref[...]))
        dz_ref[...] = dz
        dm_ref[...] = _dot(dz, wm_ref[...], NT)
        _acc(dw_ref, jnp.sum(dov * zhat, axis=0, keepdims=True))

    return _rows_call(name, body, z.shape[0], 512, 1,
                      [_tile(dout, D_MODEL), _tile(z, D_MODEL), _full(w), _full(w_mat)],
                      [_out_tile(D_MODEL, BF16, D_MODEL), _out_tile(D_MODEL, F32, D_MODEL),
                       _out_acc(1, D_MODEL, D_MODEL)], plans)


def _t5_bucket(dist):
    n = jnp.maximum(dist, 0)
    nf = jnp.maximum(n, 1).astype(F32)
    large = MAX_EXACT + (jnp.log(nf / MAX_EXACT) / math.log(MAX_DISTANCE / MAX_EXACT)
                         * (NUM_BUCKETS - MAX_EXACT)).astype(jnp.int32)
    large = jnp.minimum(large, NUM_BUCKETS - 1)
    return jnp.where(n < MAX_EXACT, n, large)


def _band_rel():
    return jnp.arange(BLK)[:, None] + BLK - jnp.arange(2 * BLK)[None, :]


def _band_valid():
    rel = _band_rel()
    window = (rel >= 0) & (rel <= BLK)
    first = window & (jnp.arange(2 * BLK)[None, :] >= BLK)
    return jnp.stack([first, window]).astype(F32).reshape(2, 1, BAND)


RES_UNROLL = 8
PAIR = LANES // HEAD_DIM


def _pair_lanes():
    first = lax.broadcasted_iota(jnp.int32, (1, LANES), 1) < HEAD_DIM
    return first, jnp.logical_not(first)


def _heads_per_step(d):
    return HEADS if d == 1 else LANES // HEAD_DIM


def _sub_rows(r, d):
    return pl.ds(r, BLK, stride=d) if d > 1 else pl.ds(0, BLK)


def _for_residues(d, fn):
    if d <= RES_UNROLL:
        for r in range(d):
            fn(r)
    else:
        def group(i, carry):
            for k in range(RES_UNROLL):
                fn(i * RES_UNROLL + k)
            return carry

        lax.fori_loop(0, d // RES_UNROLL, group, 0)


def _attn_specs(d, g, qblock):
    cw = _heads_per_step(d) * HEAD_DIM

    def col(part, hp):
        return (g * 3 + part) * (GROUP_W // cw) + hp

    def cur(part):
        return pl.BlockSpec((d * BLK, cw), lambda hp, n: (qblock(n), col(part, hp)))

    def prev(part):
        return pl.BlockSpec((d * BLK, cw), lambda hp, n: (jnp.maximum(qblock(n) - 1, 0), col(part, hp)))

    return cur, prev


def _attn_fwd(proj, bias, g, name, plans=None):
    S = proj.shape[0]
    d = DILATIONS[g]
    NB = S // (d * BLK)
    hps = _heads_per_step(d)

    def body(q_ref, kp_ref, kc_ref, vp_ref, vc_ref, b_ref, o_ref, lse_ref):
        hp = pl.program_id(0)
        later = jnp.minimum(pl.program_id(1), 1)

        def residue(r):
            rows = _sub_rows(r, d)
            q2 = q_ref[rows, :]
            k2 = jnp.concatenate([kp_ref[rows, :], kc_ref[rows, :]], axis=0)
            v2 = jnp.concatenate([vp_ref[rows, :], vc_ref[rows, :]], axis=0)
            outs, lses = [], []
            for pp in range(hps // PAIR):
                ps = slice(pp * LANES, (pp + 1) * LANES)
                qp, kp, vp = _bf(q2[:, ps]), _bf(k2[:, ps]), _bf(v2[:, ps])
                o_h, lse_h = [], []
                for hh, own in enumerate(_pair_lanes()):
                    s = _dot(qp, jnp.where(own, kp, 0), NT) * (HEAD_DIM ** -0.5) + b_ref[later, hp * hps + pp * PAIR + hh]
                    m = jnp.max(s, axis=-1, keepdims=True)
                    p = jnp.exp(s - m)
                    l = jnp.sum(p, axis=-1, keepdims=True)
                    o_h.append(_dot(_bf(p), vp, NN) / l)
                    lse_h.append(m + jnp.log(l))
                first = _pair_lanes()[0]
                outs.append(jnp.where(first, o_h[0], o_h[1]))
                lses.append(jnp.where(first, lse_h[0], lse_h[1]))
            o_ref[rows, :] = outs[0] if len(outs) == 1 else jnp.concatenate(outs, axis=1)
            lse_ref[rows, :] = lses[0] if len(lses) == 1 else jnp.concatenate(lses, axis=1)

        _for_residues(d, residue)

    cur, prev = _attn_specs(d, g, lambda n: n)
    out = pl.BlockSpec((d * BLK, hps * HEAD_DIM), lambda hp, n: (n, hp))
    res, carried = _call(
        body, plans, name=name, grid=(HEADS // hps, NB),
        in_specs=[cur(0), prev(1), cur(1), prev(2), cur(2),
                  pl.BlockSpec((2, HEADS, BLK, 2 * BLK), lambda hp, n: (0, 0, 0, 0))],
        out_specs=[out, out], out_shape=[_sds((S, GROUP_W), F32)] * 2,
        args=(proj, proj, proj, proj, proj, bias))
    return res if plans is None else (res, carried)


def _attn_bwd(proj, bias, lse, y, dy, g, name, plans=None):
    S = proj.shape[0]
    d = DILATIONS[g]
    NB = S // (d * BLK)
    hps = _heads_per_step(d)

    def body(q_ref, kp_ref, kc_ref, vp_ref, vc_ref, b_ref, l_ref, y_ref, dy_ref,
             dq_ref, dk_ref, dv_ref, db_ref, ck_ref, cv_ref):
        hp, n = pl.program_id(0), pl.program_id(1)

        @pl.when((hp == 0) & (n == 0))
        def _():
            db_ref[...] = jnp.zeros_like(db_ref)

        @pl.when(n == 0)
        def _():
            ck_ref[...] = jnp.zeros_like(ck_ref)
            cv_ref[...] = jnp.zeros_like(cv_ref)

        @pl.when(n < NB)
        def _():
            later = jnp.minimum(n, 1)

            def residue(r):
                rows = _sub_rows(r, d)
                q2 = q_ref[rows, :]
                k2 = jnp.concatenate([kp_ref[rows, :], kc_ref[rows, :]], axis=0)
                v2 = jnp.concatenate([vp_ref[rows, :], vc_ref[rows, :]], axis=0)
                l2, y2, dy2 = l_ref[rows, :], y_ref[rows, :], dy_ref[rows, :]
                dqs, dks, dvs = [], [], []
                for pp in range(hps // PAIR):
                    ps = slice(pp * LANES, (pp + 1) * LANES)
                    qp, kp, vp = _bf(q2[:, ps]), _bf(k2[:, ps]), _bf(v2[:, ps])
                    dyp, yp = dy2[:, ps], y2[:, ps]
                    dq_h, dk_h, dv_h = [], [], []
                    for hh, own in enumerate(_pair_lanes()):
                        head = hp * hps + pp * PAIR + hh
                        s = _dot(qp, jnp.where(own, kp, 0), NT) * (HEAD_DIM ** -0.5) + b_ref[later, head]
                        p = jnp.exp(s - l2[:, pp * LANES + hh * HEAD_DIM:pp * LANES + hh * HEAD_DIM + 1])
                        dyh = jnp.where(own, dyp, 0.0)
                        delta = jnp.sum(dyh * yp, axis=-1, keepdims=True)
                        ds = p * (_dot(_bf(dyh), vp, NT) - delta)
                        db_ref[head] += ds
                        dsb = _bf(ds * (HEAD_DIM ** -0.5))
                        dq_h.append(_dot(dsb, kp, NN))
                        dk_h.append(_dot(dsb, qp, TN))
                        dv_h.append(_dot(_bf(p), _bf(dyp), TN))
                    first = _pair_lanes()[0]
                    dqs.append(jnp.where(first, dq_h[0], dq_h[1]))
                    dks.append(jnp.where(first, dk_h[0], dk_h[1]))
                    dvs.append(jnp.where(first, dv_h[0], dv_h[1]))
                dkb = dks[0] if len(dks) == 1 else jnp.concatenate(dks, axis=1)
                dvb = dvs[0] if len(dvs) == 1 else jnp.concatenate(dvs, axis=1)
                dq_ref[rows, :] = dqs[0] if len(dqs) == 1 else jnp.concatenate(dqs, axis=1)
                dk_ref[rows, :] = ck_ref[rows, :] + dkb[:BLK]
                dv_ref[rows, :] = cv_ref[rows, :] + dvb[:BLK]
                ck_ref[rows, :] = dkb[BLK:]
                cv_ref[rows, :] = dvb[BLK:]

            _for_residues(d, residue)

        @pl.when(n == NB)
        def _():
            dk_ref[...] = ck_ref[...]
            dv_ref[...] = cv_ref[...]

    def qn(n):
        return jnp.minimum(n, NB - 1)

    cur, prev = _attn_specs(d, g, qn)
    cw = hps * HEAD_DIM
    row = pl.BlockSpec((d * BLK, cw), lambda hp, n: (qn(n), hp))
    done = pl.BlockSpec((d * BLK, cw), lambda hp, n: (jnp.maximum(n - 1, 0), hp))
    (dq, dk, dv, db), carried = _call(
        body, plans, name=name, grid=(HEADS // hps, NB + 1),
        in_specs=[cur(0), prev(1), cur(1), prev(2), cur(2),
                  pl.BlockSpec((2, HEADS, BLK, 2 * BLK), lambda hp, n: (0, 0, 0, 0)), row, row, row],
        out_specs=[row, done, done, pl.BlockSpec((HEADS, BLK, 2 * BLK), lambda hp, n: (0, 0, 0))],
        out_shape=[_sds((S, GROUP_W), F32)] * 3 + [_sds((HEADS, BLK, 2 * BLK), F32)],
        scratch_shapes=[pltpu.VMEM((d * BLK, cw), F32)] * 2,
        args=(proj, proj, proj, proj, proj, bias, lse, y, dy))
    return ([dq, dk, dv], db) if plans is None else ([dq, dk, dv], db, carried)


BAND = BLK * 2 * BLK


def _bucket_onehot():
    buckets = jnp.stack([_t5_bucket(_band_rel() * d) for d in DILATIONS]).reshape(N_GROUPS, 1, BAND)
    return (buckets == jnp.arange(NUM_BUCKETS).reshape(1, NUM_BUCKETS, 1)).astype(F32)


def _relbias_fwd(rel_bias, name):
    table = rel_bias.reshape(NUM_BUCKETS, N_GROUPS, HEADS).transpose(1, 0, 2)

    def body(t_ref, oh_ref, valid_ref, o_ref):
        bias = lax.dot_general(t_ref[...], oh_ref[...], (TN, ((), ())), preferred_element_type=F32,
                               precision=lax.Precision.HIGHEST)
        for k in range(2):
            o_ref[k] = jnp.where(valid_ref[k] > 0.5, bias, NEG_INF)

    out = pl.pallas_call(
        body, name=name, grid=(N_GROUPS,),
        in_specs=[pl.BlockSpec((None, NUM_BUCKETS, HEADS), lambda g: (g, 0, 0)),
                  pl.BlockSpec((None, NUM_BUCKETS, BAND), lambda g: (g, 0, 0)),
                  pl.BlockSpec((2, 1, BAND), lambda g: (0, 0, 0))],
        out_specs=pl.BlockSpec((None, 2, HEADS, BAND), lambda g: (g, 0, 0, 0)),
        out_shape=_sds((N_GROUPS, 2, HEADS, BAND), F32), compiler_params=_cp(1))(table, _bucket_onehot(), _band_valid())
    return out.reshape(N_GROUPS, 2, HEADS, BLK, 2 * BLK)


def _relbias_bwd(dbs, name):
    band = BAND
    onehot = _bucket_onehot()
    dbf = jnp.stack([db.reshape(HEADS, band) for db in dbs])

    def body(oh_ref, db_ref, o_ref):
        o_ref[...] = lax.dot_general(oh_ref[...], db_ref[...], (NT, ((), ())), preferred_element_type=F32,
                                     precision=lax.Precision.HIGHEST)

    out = pl.pallas_call(
        body, name=name, grid=(N_GROUPS,),
        in_specs=[pl.BlockSpec((None, NUM_BUCKETS, band), lambda g: (g, 0, 0)),
                  pl.BlockSpec((None, HEADS, band), lambda g: (g, 0, 0))],
        out_specs=pl.BlockSpec((None, NUM_BUCKETS, HEADS), lambda g: (g, 0, 0)),
        out_shape=_sds((N_GROUPS, NUM_BUCKETS, HEADS), F32), compiler_params=_cp(1))(onehot, dbf)
    return out.transpose(1, 0, 2).reshape(NUM_BUCKETS, N_GROUPS * HEADS)


def _chunk_pos(shape):
    return lax.broadcasted_iota(jnp.int32, shape, 0) % HG_CHUNK


def _chunk_cumsum(v):
    pos = _chunk_pos(v.shape)
    s = 1
    while s < HG_CHUNK:
        v = v + jnp.where(pos >= s, pltpu.roll(v, s, 0), 0.0)
        s *= 2
    return v


def _chunk_rev_cumsum(v):
    pos = _chunk_pos(v.shape)
    n = v.shape[0]
    s = 1
    while s < HG_CHUNK:
        v = v + jnp.where(pos < HG_CHUNK - s, pltpu.roll(v, n - s, 0), 0.0)
        s *= 2
    return v


def _lower_bound(raw):
    a0, a1 = raw[0:1], raw[1:2]
    m = jnp.maximum(a0, a1)
    e0, e1 = jnp.exp(a0 - m), jnp.exp(a1 - m)
    return e0 / (e0 + e1)


def _hg_gates(qr, fr, lb):
    sf = _sigmoid(fr)
    f = lb + (1.0 - lb) * sf
    sq = _sigmoid(qr)
    return qr * sq, sq, f, sf


HG_COL0 = QKV_W // HG_W


def _hgrn_fwd(proj, lb_raw, nw, name, plans=None):
    S = proj.shape[0]
    ncs = HG_TILE // HG_CHUNK
    tril = jnp.tril(jnp.ones((HG_CHUNK, HG_CHUNK), dtype=bool))

    def body(q_ref, f_ref, i_ref, og_ref, lb_ref, nw_ref, y_ref, o_ref, st_ref, state):
        @pl.when(pl.program_id(0) == 0)
        def _():
            state[...] = jnp.zeros_like(state)

        lb = _lower_bound(lb_ref[...])
        q, _, f, _ = _hg_gates(q_ref[...], f_ref[...], lb)
        k = 1.0 - f
        G = _chunk_cumsum(jnp.log(f))
        row = lax.broadcasted_iota(jnp.int32, (HG_CHUNK, HG_CHUNK), 0)
        col = lax.broadcasted_iota(jnp.int32, (HG_CHUNK, HG_CHUNK), 1)
        heads = [slice(h * HG_DK, (h + 1) * HG_DK) for h in range(HG_HEADS)]
        sts = [state[h] for h in range(HG_HEADS)]
        for c in range(ncs):
            cs = slice(c * HG_CHUNK, (c + 1) * HG_CHUNK)
            for h, hs in enumerate(heads):
                Gc = G[cs, hs]
                gl = Gc[HG_CHUNK - 1:HG_CHUNK]
                qt = _bf(q[cs, hs] * jnp.exp(Gc))
                kt = _bf(k[cs, hs] * jnp.exp(-Gc))
                kd = _bf(k[cs, hs] * jnp.exp(gl - Gc))
                v = _bf(i_ref[cs, hs])
                A = jnp.where(row >= col, _dot(qt, kt, NT), 0.0)
                o_ref[cs, hs] = _dot(_bf(A), v, NN) + _dot(qt, _bf(sts[h]), NT)
                st_ref[c, h] = sts[h]
                sts[h] = sts[h] * jnp.exp(gl) + _dot(v, kd, TN)
        for h, hs in enumerate(heads):
            state[h] = sts[h]
            oh = o_ref[:, hs]
            og = og_ref[:, hs]
            y_ref[:, hs] = oh * _rinv(oh) * nw_ref[...] * (og * _sigmoid(og))

    def colspec(j):
        return pl.BlockSpec((HG_TILE, HG_W), lambda i: (i, HG_COL0 + j))

    res, carried = _call(
        body, plans, name=name, grid=(S // HG_TILE,),
        in_specs=[colspec(0), colspec(1), colspec(2), colspec(3),
                  pl.BlockSpec((2, HG_W), lambda i: (0, 0)), pl.BlockSpec((1, HG_DK), lambda i: (0, 0))],
        out_specs=[pl.BlockSpec((HG_TILE, HG_W), lambda i: (i, 0))] * 2
        + [pl.BlockSpec((ncs, HG_HEADS, HG_DK, HG_DK), lambda i: (i, 0, 0, 0))],
        out_shape=[_sds((S, HG_W), F32)] * 2 + [_sds((S // HG_CHUNK, HG_HEADS, HG_DK, HG_DK), F32)],
        scratch_shapes=[pltpu.VMEM((HG_HEADS, HG_DK, HG_DK), F32)],
        args=(proj, proj, proj, proj, lb_raw, nw))
    return res if plans is None else (res, carried)


def _hgrn_bwd(proj, lb_raw, nw, o, states, dy, d_attn, d_gates, name):
    S = proj.shape[0]
    ncs = HG_TILE // HG_CHUNK
    nt = S // HG_TILE
    n_a, n_g = len(d_attn), len(d_gates)
    own = [slice(QKV_W + j * HG_W, QKV_W + (j + 1) * HG_W) for j in range(4)]

    def body(q_ref, f_ref, i_ref, og_ref, lb_ref, nw_ref, o_ref, st_ref, dy_ref, *rest):
        attn_refs, gate_refs = rest[:n_a], rest[n_a:n_a + n_g]
        dp_ref, dlb_ref, dnw_ref, dstate, do_s, dG_s, dgl_s, dk_s, dlb_s = rest[n_a + n_g:]
        dq_ref, df_ref, di_ref, dog_ref = (dp_ref.at[:, cols] for cols in own)
        step = pl.program_id(0)
        for k, a_ref in enumerate(attn_refs):
            dp_ref[:, k * GROUP_W:(k + 1) * GROUP_W] = _bf(a_ref[...])
        for k, g_ref in enumerate(gate_refs):
            dp_ref[:, QKV_W + 4 * HG_W + k * D_MODEL:QKV_W + 4 * HG_W + (k + 1) * D_MODEL] = g_ref[...]

        @pl.when(step == 0)
        def _():
            dstate[...] = jnp.zeros_like(dstate)
            dlb_s[...] = jnp.zeros_like(dlb_s)
            dnw_ref[...] = jnp.zeros_like(dnw_ref)

        lb = _lower_bound(lb_ref[...])
        qr = q_ref[...]
        q, sq, f, sf = _hg_gates(qr, f_ref[...], lb)
        k = 1.0 - f
        G = _chunk_cumsum(jnp.log(f))
        nwv = nw_ref[...]
        row = lax.broadcasted_iota(jnp.int32, (HG_CHUNK, HG_CHUNK), 0)
        col = lax.broadcasted_iota(jnp.int32, (HG_CHUNK, HG_CHUNK), 1)
        for h in range(HG_HEADS):
            hs = slice(h * HG_DK, (h + 1) * HG_DK)
            oh = o_ref[:, hs]
            r = _rinv(oh)
            ohat = oh * r
            og = og_ref[:, hs]
            sg = _sigmoid(og)
            dyh = dy_ref[:, hs]
            don = dyh * (og * sg)
            dog_ref[:, hs] = _bf(dyh * (ohat * nwv) * (sg * (1.0 + og * (1.0 - sg))))
            dnw_ref[...] += jnp.sum(don * ohat, axis=0, keepdims=True)
            do_s[:, hs] = _norm_bwd(don, ohat, r, nwv)
        dsts = [dstate[h] for h in range(HG_HEADS)]
        for c in reversed(range(ncs)):
            cs = slice(c * HG_CHUNK, (c + 1) * HG_CHUNK)
            for h in range(HG_HEADS):
                hs = slice(h * HG_DK, (h + 1) * HG_DK)
                dst = dsts[h]
                Gc = G[cs, hs]
                gl = Gc[HG_CHUNK - 1:HG_CHUNK]
                eG, enG, edG, egl = jnp.exp(Gc), jnp.exp(-Gc), jnp.exp(gl - Gc), jnp.exp(gl)
                qt, kt, kd = q[cs, hs] * eG, k[cs, hs] * enG, k[cs, hs] * edG
                qtb, ktb, kdb = _bf(qt), _bf(kt), _bf(kd)
                v = _bf(i_ref[cs, hs])
                do = _bf(do_s[cs, hs])
                st = st_ref[c, h]
                dstb = _bf(dst)
                A = jnp.where(row >= col, _dot(qtb, ktb, NT), 0.0)
                dA = _bf(jnp.where(row >= col, _dot(do, v, NT), 0.0))
                di_ref[cs, hs] = _bf(_dot(_bf(A), do, TN) + _dot(kdb, dstb, NT))
                dqt = _dot(dA, ktb, NN) + _dot(do, _bf(st), NN)
                dkt = _dot(dA, qtb, TN)
                dkd = _dot(v, dstb, NN)
                dgl = egl * jnp.sum(st * dst, axis=0, keepdims=True) + jnp.sum(dkd * kd, axis=0, keepdims=True)
                dsts[h] = dst * egl + _dot(do, qtb, TN)
                dq_ref[cs, hs] = _bf(dqt * eG * (sq[cs, hs] * (1.0 + qr[cs, hs] * (1.0 - sq[cs, hs]))))
                dk_s[cs, hs] = dkt * enG + dkd * edG
                dG_s[cs, hs] = dqt * qt - dkt * kt - dkd * kd
                dgl_s[cs, hs] = jnp.broadcast_to(dgl, (HG_CHUNK, HG_DK))
        for h in range(HG_HEADS):
            dstate[h] = dsts[h]
        dg = _chunk_rev_cumsum(dG_s[...]) + dgl_s[...]
        dfv = dg / f - dk_s[...]
        df_ref[...] = _bf(dfv * (1.0 - lb) * sf * (1.0 - sf))
        dlb_s[...] += jnp.sum(dfv * (1.0 - sf), axis=0, keepdims=True)

        @pl.when(step == nt - 1)
        def _():
            t = dlb_s[...] * lb * (1.0 - lb)
            dlb_ref[...] = jnp.concatenate([t, -t], axis=0)

    def colspec(j):
        return pl.BlockSpec((HG_TILE, HG_W), lambda i: (nt - 1 - i, HG_COL0 + j))

    def rows(width):
        return pl.BlockSpec((HG_TILE, width), lambda i: (nt - 1 - i, 0))

    tile = rows(HG_W)
    return pl.pallas_call(
        body, name=name, grid=(nt,),
        in_specs=[colspec(0), colspec(1), colspec(2), colspec(3),
                  pl.BlockSpec((2, HG_W), lambda i: (0, 0)), pl.BlockSpec((1, HG_DK), lambda i: (0, 0)),
                  tile, pl.BlockSpec((ncs, HG_HEADS, HG_DK, HG_DK), lambda i: (nt - 1 - i, 0, 0, 0)), tile]
        + [rows(GROUP_W)] * n_a + [rows(D_MODEL)] * n_g,
        out_specs=[rows(IN_W), pl.BlockSpec((2, HG_W), lambda i: (0, 0)), pl.BlockSpec((1, HG_DK), lambda i: (0, 0))],
        out_shape=[_sds((S, IN_W), BF16), _sds((2, HG_W), F32), _sds((1, HG_DK), F32)],
        scratch_shapes=[pltpu.VMEM((HG_HEADS, HG_DK, HG_DK), F32)] + [pltpu.VMEM((HG_TILE, HG_W), F32)] * 4
        + [pltpu.VMEM((1, HG_W), F32)],
        compiler_params=_cp(1))(proj, proj, proj, proj, lb_raw, nw, o, states, dy, *d_attn, *d_gates)


GATE_COL0 = (QKV_W + 4 * HG_W) // GROUP_W
HALF_D = D_MODEL // 2


def _gate_tiles(proj):
    return [_tile(proj, HALF_D, functools.partial(lambda c, k: GATE_COL0 + k, k=k)) for k in range(4)]


def _gates(g_refs):
    s0 = _sigmoid(jnp.concatenate([g_refs[0][...], g_refs[1][...]], axis=1))
    s1 = _sigmoid(jnp.concatenate([g_refs[2][...], g_refs[3][...]], axis=1))
    return s0, s1


def _branch_fwd(os_, lses, yh, proj, w_a, w_h, name, plans=None):
    nb = w_a.shape[0]

    def body(o0, o1, o2, l0, l1, l2, yh_ref, g0a, g0b, g1a, g1b, wa_ref, wh_ref,
             y_ref, lse_ref, za_ref, zh_ref, m_ref):
        a, b, c = l0[...], l1[...], l2[...]
        m = jnp.maximum(jnp.maximum(a, b), c)
        ea, eb, ec = jnp.exp(a - m), jnp.exp(b - m), jnp.exp(c - m)
        den = ea + eb + ec
        y = (ea * o0[...] + eb * o1[...] + ec * o2[...]) / den
        y_ref[...] = y
        lse_ref[...] = m + jnp.log(den)
        yb, yhb = _bf(y), _bf(yh_ref[...])
        za = jnp.concatenate([_dot(yb, wa_ref[j], NN) for j in range(nb)], axis=1)
        zh = jnp.concatenate([_dot(yhb, wh_ref[j], NN) for j in range(nb)], axis=1)
        s0, s1 = _gates((g0a, g0b, g1a, g1b))
        za_ref[...] = za
        zh_ref[...] = zh
        m_ref[...] = _bf(s0 * za + s1 * zh)

    return _rows_call(name, body, yh.shape[0], 512, 1,
                      [*[_tile(t, GROUP_W) for t in (*os_, *lses)], _tile(yh, HG_W), *_gate_tiles(proj),
                       _full(w_a), _full(w_h)],
                      [_out_tile(GROUP_W, F32, GROUP_W)] * 2 + [_out_tile(D_MODEL, F32, D_MODEL)] * 2
                      + [_out_tile(D_MODEL, BF16, D_MODEL)], plans)


def _branch_bwd(dm, za, zh, proj, w_a, w_h, name, plans=None):
    nb, _, Nb = w_a.shape

    def body(dm_ref, za_ref, zh_ref, g0a, g0b, g1a, g1b, wa_ref, wh_ref,
             dza_ref, dzh_ref, dg0_ref, dg1_ref, dy_ref, dyh_ref):
        dmv = dm_ref[...]
        s0, s1 = _gates((g0a, g0b, g1a, g1b))
        dza, dzh = _bf(dmv * s0), _bf(dmv * s1)
        dza_ref[...] = dza
        dzh_ref[...] = dzh
        dg0_ref[...] = _bf(dmv * za_ref[...] * s0 * (1.0 - s0))
        dg1_ref[...] = _bf(dmv * zh_ref[...] * s1 * (1.0 - s1))
        dy_ref[...] = sum(_dot(dza[:, j * Nb:(j + 1) * Nb], wa_ref[j], NT) for j in range(nb))
        dyh_ref[...] = sum(_dot(dzh[:, j * Nb:(j + 1) * Nb], wh_ref[j], NT) for j in range(nb))

    return _rows_call(name, body, za.shape[0], 512, 1,
                      [_tile(dm, D_MODEL), _tile(za, D_MODEL), _tile(zh, D_MODEL), *_gate_tiles(proj),
                       _full(w_a), _full(w_h)],
                      [_out_tile(D_MODEL, BF16, D_MODEL)] * 4 + [_out_tile(GROUP_W, F32, GROUP_W),
                                                                 _out_tile(HG_W, F32, HG_W)], plans)


def _mix_out(merged, w_out, x, w_post, w_pre, name):
    def body(m_ref, wo_ref, x_ref, wp_ref, wf_ref, mo_ref, x1_ref, h2_ref):
        z = _dot(m_ref[...], wo_ref[...], NN)
        mo_ref[...] = z
        x1 = x_ref[...] + z * _rinv(z) * wp_ref[...]
        x1_ref[...] = x1
        h2_ref[...] = _bf(x1 * _rinv(x1) * wf_ref[...])

    return _rows_call(name, body, x.shape[0], 512, 1,
                      [_tile(merged, D_MODEL), _full(w_out), _tile(x, D_MODEL), _full(w_post), _full(w_pre)],
                      [_out_tile(D_MODEL, F32, D_MODEL), _out_tile(D_MODEL, F32, D_MODEL),
                       _out_tile(D_MODEL, BF16, D_MODEL)])


def _loss_head(a, w_down, x1, tgt, w, name):
    def body(a_ref, wd_ref, x1_ref, t_ref, w_ref, dx_ref, df_ref, dw_ref, loss_ref):
        z = _dot(a_ref[...], wd_ref[...], NN)
        r = _rinv(z)
        zhat = z * r
        wv = w_ref[...]
        e = x1_ref[...] + zhat * wv - t_ref[...]
        dx = e * (1.0 / D_MODEL)
        dx_ref[...] = dx
        df_ref[...] = _bf(_norm_bwd(dx, zhat, r, wv))
        _acc(dw_ref, jnp.sum(dx * zhat, axis=0, keepdims=True))
        part = 0.5 * jnp.sum(jnp.sum(e * e, axis=1, keepdims=True), axis=0, keepdims=True) * (1.0 / D_MODEL)
        _acc(loss_ref, jnp.broadcast_to(part, (1, LANES)))

    return _rows_call(name, body, x1.shape[0], 512, 1,
                      [_tile(a, D_FF), _full(w_down), _tile(x1, D_MODEL), _tile(tgt, D_MODEL), _full(w)],
                      [_out_tile(D_MODEL, F32, D_MODEL), _out_tile(D_MODEL, BF16, D_MODEL),
                       _out_acc(1, D_MODEL, D_MODEL), _out_acc(1, LANES, LANES)])


CONV_CB = D_FF // 2
CONV_TM = 512
HALO = 8
SQRT_HALF = 0.7071067811865476
INV_SQRT_2PI = 0.3989422804014327


CONV_RS = 32


def _lane_tiles():
    return [slice(k * LANES, (k + 1) * LANES) for k in range(CONV_CB // LANES)]


def _strip_start(i):
    return pl.multiple_of(i * CONV_RS, CONV_RS)


def _strip_taps(u_ref, halo_ref, r0, cs, first_strip, first_tile):
    if first_strip:
        before = jnp.where(first_tile, 0.0, halo_ref[:, cs])
        blk = jnp.concatenate([before, u_ref[0:CONV_RS, cs]], axis=0)
    else:
        blk = u_ref[pl.ds(pl.multiple_of(r0 - HALO, HALO), CONV_RS + HALO), cs]
    return pltpu.roll(blk, 2, 0)[HALO:], pltpu.roll(blk, 1, 0)[HALO:], blk[HALO:]


def _conv(taps, w_ref, b_ref, cs):
    return b_ref[:, cs] + w_ref[0:1, cs] * taps[0] + w_ref[1:2, cs] * taps[1] + w_ref[2:3, cs] * taps[2]


def _conv_specs(tm):
    nh = tm // HALO
    nc = D_FF // CONV_CB

    def tile(off):
        return pl.BlockSpec((tm, CONV_CB), lambda c, i: (i, off + c))

    def halo(off):
        return pl.BlockSpec((HALO, CONV_CB), lambda c, i: (jnp.maximum(i * nh - 1, 0), off + c))

    def small(rows, off):
        return pl.BlockSpec((rows, CONV_CB), lambda c, i: (0, off + c))

    return nc, tile, halo, small


def _conv_gelu_fwd(u, cw, cb, name, plans=None):
    S = u.shape[0]
    tm = CONV_TM
    nc, tile, halo, small = _conv_specs(tm)

    def body(ug, hg, uv, hv, wg, wv, bg, bv, a_ref):
        first_tile = pl.program_id(1) == 0

        def strip(r0, first_strip):
            for cs in _lane_tiles():
                cg = _conv(_strip_taps(ug, hg, r0, cs, first_strip, first_tile), wg, bg, cs)
                cv = _conv(_strip_taps(uv, hv, r0, cs, first_strip, first_tile), wv, bv, cs)
                a_ref[pl.ds(r0, CONV_RS), cs] = _bf(0.5 * cg * (1.0 + lax.erf(cg * SQRT_HALF)) * cv)

        strip(0, True)
        lax.fori_loop(1, tm // CONV_RS, lambda k, c: (strip(_strip_start(k), False), c)[1], 0)

    (a,), carried = _call(
        body, plans, name=name, grid=(nc, S // tm),
        in_specs=[tile(0), halo(0), tile(nc), halo(nc), small(3, 0), small(3, nc), small(1, 0), small(1, nc)],
        out_specs=[tile(0)], out_shape=[_sds((S, D_FF), BF16)], args=(u, u, u, u, cw, cw, cb, cb))
    return a if plans is None else (a, carried)


def _conv_gelu_bwd(u, dff, w_down, cw, cb, name, plans=None):
    S = u.shape[0]
    tm = CONV_TM
    nt = S // tm
    nc, tile, halo, small = _conv_specs(tm)

    def body(ug, hg, uv, hv, wg, wv, bg, bv, dff_ref, wd_ref, dcg_ref, dcv_ref, dwg_ref, dwv_ref, dbg_ref, dbv_ref,
             acc, da_ref):
        i = pl.program_id(1)
        first_tile = i == 0
        da_ref[...] = _dot(dff_ref[...], wd_ref[...], NT)

        @pl.when(first_tile)
        def _():
            acc[...] = jnp.zeros_like(acc)

        def strip(r0, first_strip):
            rows = pl.ds(r0, CONV_RS)
            for cs in _lane_tiles():
                tg = _strip_taps(ug, hg, r0, cs, first_strip, first_tile)
                tv = _strip_taps(uv, hv, r0, cs, first_strip, first_tile)
                cg = _conv(tg, wg, bg, cs)
                cv = _conv(tv, wv, bv, cs)
                phi = 0.5 * (1.0 + lax.erf(cg * SQRT_HALF))
                dav = da_ref[rows, cs]
                dcg = dav * cv * (phi + cg * jnp.exp(-0.5 * cg * cg) * INV_SQRT_2PI)
                dcv = dav * (cg * phi)
                dcg_ref[rows, cs] = dcg
                dcv_ref[rows, cs] = dcv
                for half, (dc, taps) in enumerate(((dcg, tg), (dcv, tv))):
                    for j in range(3):
                        acc[4 * half + j, :, cs] += dc * taps[j]
                    acc[4 * half + 3, :, cs] += dc

        strip(0, True)
        lax.fori_loop(1, tm // CONV_RS, lambda k, c: (strip(_strip_start(k), False), c)[1], 0)

        @pl.when(i == nt - 1)
        def _():
            for half, (dw_ref, db_ref) in enumerate(((dwg_ref, dbg_ref), (dwv_ref, dbv_ref))):
                for j in range(3):
                    dw_ref[j:j + 1, :] = jnp.sum(acc[4 * half + j], axis=0, keepdims=True)
                db_ref[...] = jnp.sum(acc[4 * half + 3], axis=0, keepdims=True)

    res, carried = _call(
        body, plans, name=name, grid=(nc, nt),
        in_specs=[tile(0), halo(0), tile(nc), halo(nc), small(3, 0), small(3, nc), small(1, 0), small(1, nc),
                  pl.BlockSpec((tm, D_MODEL), lambda c, i: (i, 0)), pl.BlockSpec((CONV_CB, D_MODEL), lambda c, i: (c, 0))],
        out_specs=[tile(0), tile(0), small(3, 0), small(3, 0), small(1, 0), small(1, 0)],
        out_shape=[_sds((S, D_FF), F32)] * 2 + [_sds((3, D_FF), F32)] * 2 + [_sds((1, D_FF), F32)] * 2,
        scratch_shapes=[pltpu.VMEM((8, CONV_RS, CONV_CB), F32), pltpu.VMEM((tm, CONV_CB), F32)],
        args=(u, u, u, u, cw, cw, cb, cb, dff, w_down))
    return res if plans is None else (res, carried)


def _conv_input_bwd(dcg, dcv, cw, name, plans=None):
    S = dcg.shape[0]
    tm = CONV_TM // 2
    nh = tm // HALO
    nt = S // tm
    n = CONV_RS + HALO
    tile = pl.BlockSpec((tm, D_FF), lambda i: (i, 0))
    nxt = pl.BlockSpec((HALO, D_FF), lambda i: (jnp.minimum((i + 1) * nh, S // HALO - 1), 0))

    def body(g_ref, ng_ref, v_ref, nv_ref, w_ref, du_ref):
        last_tile = pl.program_id(0) == nt - 1

        def strip(r0, last_strip):
            for half, (dc_ref, n_ref) in enumerate(((g_ref, ng_ref), (v_ref, nv_ref))):
                for k in range(D_FF // LANES):
                    cs = slice(k * LANES, (k + 1) * LANES)
                    ws = slice(half * D_FF + k * LANES, half * D_FF + (k + 1) * LANES)
                    if last_strip:
                        after = jnp.where(last_tile, 0.0, n_ref[:, cs])
                        blk = jnp.concatenate([dc_ref[tm - CONV_RS:tm, cs], after], axis=0)
                    else:
                        blk = dc_ref[pl.ds(r0, n), cs]
                    d1 = pltpu.roll(blk, n - 1, 0)[:CONV_RS]
                    d2 = pltpu.roll(blk, n - 2, 0)[:CONV_RS]
                    du_ref[pl.ds(r0, CONV_RS), ws] = _bf(w_ref[2:3, ws] * blk[:CONV_RS] + w_ref[1:2, ws] * d1
                                                         + w_ref[0:1, ws] * d2)

        lax.fori_loop(0, tm // CONV_RS - 1, lambda k, c: (strip(_strip_start(k), False), c)[1], 0)
        strip(tm - CONV_RS, True)

    (du,), carried = _call(
        body, plans, name=name, grid=(nt,),
        in_specs=[tile, nxt, tile, nxt, pl.BlockSpec((3, 2 * D_FF), lambda i: (0, 0))],
        out_specs=[pl.BlockSpec((tm, 2 * D_FF), lambda i: (i, 0))], out_shape=[_sds((S, 2 * D_FF), BF16)],
        args=(dcg, dcg, dcv, dcv, cw))
    return du if plans is None else (du, carried)


def _row_tile(n, cap):
    best = n
    for t in range(16, cap + 1, 16):
        if n % t == 0:
            best = t
    return best if best <= cap else n


def _rows_for_bytes(nbytes, cols):
    return max(16, nbytes // (4 * cols) // 16 * 16)


def _adamw(w, g, m, v, name):
    R, C = w.shape
    tr = _row_tile(R, _rows_for_bytes(2 << 20, C))

    def body(w_ref, g_ref, m_ref, v_ref, d_ref, nm_ref, nv_ref):
        gv = g_ref[...]
        nm = ADAM_B1 * m_ref[...] + (1.0 - ADAM_B1) * gv
        nv = ADAM_B2 * v_ref[...] + (1.0 - ADAM_B2) * (gv * gv)
        m_hat = nm / (1.0 - ADAM_B1 ** ADAM_STEP)
        v_hat = nv / (1.0 - ADAM_B2 ** ADAM_STEP)
        d_ref[...] = -ADAM_LR * (m_hat / (jnp.sqrt(v_hat) + ADAM_EPS) + ADAM_WD * w_ref[...])
        nm_ref[...] = nm
        nv_ref[...] = nv

    spec = pl.BlockSpec((tr, C), lambda i: (i, 0))
    return pl.pallas_call(body, name=name, grid=(R // tr,), in_specs=[spec] * 4, out_specs=[spec] * 3,
                          out_shape=[_sds((R, C), F32)] * 3, compiler_params=_cp(1))(w, g, m, v)


def _pair_sum(gfull, rcv, c_idx, name):
    nb, R, C = gfull.shape
    half = R // 2
    tr = _row_tile(half, _rows_for_bytes(2 << 20, C))
    nt = half // tr

    def body(c_ref, g_ref, r_ref, o_ref):
        o_ref[...] = _bf(g_ref[...] + r_ref[...])

    return pl.pallas_call(
        body, name=name,
        grid_spec=pltpu.PrefetchScalarGridSpec(
            num_scalar_prefetch=1, grid=(nb, nt),
            in_specs=[pl.BlockSpec((None, tr, C), lambda j, i, c_ref: (j, c_ref[0] * nt + i, 0)),
                      pl.BlockSpec((None, tr, C), lambda j, i, c_ref: (j, i, 0))],
            out_specs=pl.BlockSpec((None, tr, C), lambda j, i, c_ref: (j, i, 0))),
        out_shape=_sds((nb, half, C), BF16), compiler_params=_cp(2))(c_idx, gfull, rcv)


def _chip_sum(arrived, own, place, name):
    nb, H, C = arrived.shape
    tr = _row_tile(H, _rows_for_bytes(2 << 20, C))
    nt = H // tr

    def body(pl_ref, *refs):
        o_ref = refs[nb + 1]
        me = pl_ref[0]
        acc = None
        for k in range(nb):
            term = jnp.where(me == k, refs[nb][...], refs[k][...]).astype(F32)
            acc = term if acc is None else acc + term
        o_ref[...] = acc

    def other(k):
        return pl.BlockSpec((None, tr, C), lambda i, p: (jnp.where(p[0] == k, (k + 1) % nb, k), i, 0))

    return pl.pallas_call(
        body, name=name,
        grid_spec=pltpu.PrefetchScalarGridSpec(
            num_scalar_prefetch=1, grid=(nt,),
            in_specs=[other(k) for k in range(nb)] + [pl.BlockSpec((None, tr, C), lambda i, p: (p[0], i, 0))],
            out_specs=pl.BlockSpec((tr, C), lambda i, p: (p[1] * nt + i, 0))),
        out_shape=_sds((2 * H, C), F32), compiler_params=_cp(1))(place, *([arrived] * nb), own)


def _cast_into_slot(shard, place, name):
    R, C = shard.shape
    tr = _row_tile(R, 256)

    def body(pl_ref, s_ref, o_ref):
        o_ref[...] = _bf(s_ref[...])

    return pl.pallas_call(
        body, name=name,
        grid_spec=pltpu.PrefetchScalarGridSpec(
            num_scalar_prefetch=1, grid=(R // tr,),
            in_specs=[pl.BlockSpec((tr, C), lambda i, p: (i, 0))],
            out_specs=pl.BlockSpec((None, tr, C), lambda i, p: (p[0], i, 0))),
        out_shape=_sds((N_CHIPS, R, C), BF16), compiler_params=_cp(1))(place, shard)


def _place():
    x, y, c = lax.axis_index("x"), lax.axis_index("y"), lax.axis_index("c")
    chips = [(1 - x, y), (x, 1 - y), (1 - x, 1 - y)]
    return x, y, c, chips


def _chip_id(px, py):
    return 2 * px + py


def _remote(src, dst, send_sems, recv_sems, k, to):
    return pltpu.make_async_remote_copy(src_ref=src, dst_ref=dst, send_sem=send_sems.at[k], recv_sem=recv_sems.at[k],
                                        device_id=to, device_id_type=MESH)


def _proj_gathered(x, w_norm, slot, place, name, tm=512, plan=None):
    M, K = x.shape
    nb, _, Nb = slot.shape
    half = K // 2
    nt = M // tm
    cx, cy = place[0] // 2, place[0] % 2
    order = jnp.stack([place[0], _chip_id(1 - cx, cy), _chip_id(cx, 1 - cy), _chip_id(1 - cx, 1 - cy)]).astype(jnp.int32)

    p_in = [] if plan is None else plan.ins + plan.inouts
    p_out = [] if plan is None else [_sds(a.shape, a.dtype) for a in plan.inouts] + plan.outs
    n_pi, n_po = len(p_in), len(p_out)

    def body(order_ref, x_ref, wn_ref, slot_in, *refs):
        o_ref, slot_ref, h_out = refs[n_pi:n_pi + 3]
        s0 = n_pi + 3 + n_po
        w_buf, hs, ici_send, ici_recv, pass_send, pass_recv, load_sem = refs[s0:s0 + 7]

        def carried():
            if plan is None:
                return [], [], []
            ins = refs[:len(plan.ins)]
            outs = refs[n_pi + 3:n_pi + 3 + n_po]
            return plan.copies(ins, outs[:len(plan.inouts)], outs[len(plan.inouts):], *refs[s0 + 7:])

        b, i = pl.program_id(0), pl.program_id(1)
        x, y, c, chips = _place()
        me = _chip_id(x, y)
        sib = (x, y, 1 - c)
        mine, other = pl.ds(c * half, half), pl.ds((1 - c) * half, half)

        def sent(k):
            blk = slot_ref.at[me, mine]
            return _remote(blk, blk, ici_send, ici_recv, k, (*chips[k], c))

        def landed(k):
            blk = slot_ref.at[_chip_id(*chips[k]), mine]
            return _remote(blk, blk, ici_send, ici_recv, k, (*chips[k], c))

        def passed(k, rows):
            blk = slot_ref.at[_chip_id(*chips[k]), rows]
            return _remote(blk, blk, pass_send, pass_recv, k, sib)

        @pl.when((b == 0) & (i == 0))
        def _():
            for k in range(len(chips)):
                sent(k).start()
            sends, _, local = carried()
            for cp in (*sends, *local):
                cp.start()

        for k in range(len(chips)):
            @pl.when((b == k + 1) & (i == 0))
            def _(k=k):
                landed(k).wait_recv()
                passed(k, mine).start()
                passed(k, other).wait_recv()

        @pl.when(i == 0)
        def _():
            load = pltpu.make_async_copy(slot_ref.at[order_ref[b]], w_buf, load_sem.at[0])
            load.start()
            load.wait()

        rows = pl.ds(pl.multiple_of(i * tm, tm), tm)
        keep_h = pltpu.make_async_copy(hs, h_out, load_sem.at[1])

        @pl.when(b == 0)
        def _():
            xv = x_ref[...]
            hs[rows, :] = _bf(xv * _rinv(xv) * wn_ref[...])

        @pl.when((b == 1) & (i == 0))
        def _():
            keep_h.start()

        o_ref[...] = _dot(hs[rows, :], w_buf[...], NN)

        @pl.when((b == nb - 1) & (i == nt - 1))
        def _():
            for k in range(len(chips)):
                sent(k).wait_send()
                passed(k, mine).wait_send()
            sends, recvs, local = carried()
            for cp in recvs:
                cp.wait_recv()
            for cp in sends:
                cp.wait_send()
            for cp in local:
                cp.wait()
            keep_h.wait()

    n_peers = N_CHIPS - 1
    return pl.pallas_call(
        body, name=name,
        grid_spec=pltpu.PrefetchScalarGridSpec(
            num_scalar_prefetch=1, grid=(nb, nt),
            in_specs=[pl.BlockSpec((tm, K), lambda b, i, o: (i, 0)), pl.BlockSpec((1, K), lambda b, i, o: (0, 0)), ANY]
            + [ANY] * n_pi,
            out_specs=[pl.BlockSpec((tm, Nb), lambda b, i, o: (i, o[b])), ANY, ANY] + [ANY] * n_po,
            scratch_shapes=[pltpu.VMEM((K, Nb), BF16), pltpu.VMEM((M, K), BF16)]
            + [pltpu.SemaphoreType.DMA((n_peers,))] * 4 + [pltpu.SemaphoreType.DMA((2,))]
            + ([] if plan is None else [pltpu.SemaphoreType.DMA((plan.n_sems,))] * 3)),
        out_shape=[_sds((M, nb * Nb), F32), _sds(slot.shape, slot.dtype), _sds((M, K), BF16)] + p_out,
        input_output_aliases={3: 1, **({} if plan is None else
                                       {4 + len(plan.ins) + a: 3 + a for a in range(len(plan.inouts))})},
        compiler_params=_cp(2))(order, x, w_norm, slot, *p_in)


def _gather_ici_plan(slots, wholes, part=None):
    ns, nw = len(slots), len(wholes)

    def copies(ins, ios, outs, send_sems, recv_sems, local_sems):
        x, y, c, chips = _place()
        me = _chip_id(x, y)
        sends, recvs = [], []
        for a in range(ns + nw):
            dst = ios[a] if a < ns else outs[a - ns]
            R = dst.shape[1]
            r0, nr = (0, R // 2) if part is None else part
            rows = pl.ds(c * (R // 2) + r0, nr) if a < ns else pl.ds(0, R)
            src = dst.at[me, rows] if a < ns else ins[a - ns]
            for j, chip in enumerate(chips):
                sends.append(_remote(src, dst.at[me, rows], send_sems, recv_sems, 3 * a + j, (*chip, c)))
                landed = dst.at[_chip_id(*chip), rows]
                recvs.append(_remote(landed, landed, send_sems, recv_sems, 3 * a + j, (*chip, c)))
        local = [pltpu.make_async_copy(ins[b], outs[b].at[me], local_sems.at[b]) for b in range(nw)]
        return sends, recvs, local

    return _Plan(copies, 3 * (ns + nw), ins=wholes, inouts=slots,
                 outs=[_sds((N_CHIPS, *s.shape), s.dtype) for s in wholes])


def _gather_pass_plan(slots):
    def copies(ins, ios, outs, send_sems, recv_sems, local_sems):
        x, y, c, chips = _place()
        sib = (x, y, 1 - c)
        sends, recvs = [], []
        for a, buf in enumerate(ios):
            half = buf.shape[1] // 2
            for j, chip in enumerate(chips):
                mine = buf.at[_chip_id(*chip), pl.ds(c * half, half)]
                other = buf.at[_chip_id(*chip), pl.ds((1 - c) * half, half)]
                sends.append(_remote(mine, mine, send_sems, recv_sems, 3 * a + j, sib))
                recvs.append(_remote(other, other, send_sems, recv_sems, 3 * a + j, sib))
        return sends, recvs, []

    return _Plan(copies, 3 * len(slots), inouts=slots)


def _pair_plan(grads):
    def copies(ins, ios, outs, send_sems, recv_sems, local_sems):
        x, y, c, _ = _place()
        sib = (x, y, 1 - c)
        sends, recvs = [], []
        for a, g in enumerate(ins):
            half = g.shape[1] // 2
            sends.append(_remote(g.at[:, pl.ds((1 - c) * half, half), :], outs[a], send_sems, recv_sems, a, sib))
            recvs.append(_remote(outs[a], outs[a], send_sems, recv_sems, a, sib))
        return sends, recvs, []

    return _Plan(copies, len(grads), ins=grads,
                 outs=[_sds((g.shape[0], g.shape[1] // 2, g.shape[2]), g.dtype) for g in grads])


def _chip_plan(parts):
    def copies(ins, ios, outs, send_sems, recv_sems, local_sems):
        x, y, c, chips = _place()
        me = _chip_id(x, y)
        sends, recvs = [], []
        for a, part in enumerate(ins):
            for j, chip in enumerate(chips):
                sends.append(_remote(part.at[_chip_id(*chip)], outs[a].at[me], send_sems, recv_sems, 3 * a + j, (*chip, c)))
                landed = outs[a].at[_chip_id(*chip)]
                recvs.append(_remote(landed, landed, send_sems, recv_sems, 3 * a + j, (*chip, c)))
        return sends, recvs, []

    return _Plan(copies, 3 * len(parts), ins=parts, outs=[_sds(p.shape, p.dtype) for p in parts])


def _all_sum(pack, fulls, name):
    R, C = pack.shape
    n = len(fulls)

    def body(p_ref, *refs):
        o_ref, halves = refs[n], refs[n + 1:2 * n + 1]
        buf, send_sems, recv_sems, pair_send, pair_recv = refs[2 * n + 1:]
        x, y, c, _ = _place()
        sib = (x, y, 1 - c)
        pair = []
        for a, full in enumerate(halves):
            H = full.shape[0] // 2
            mine = full.at[pl.ds(c * H, H)]
            cp = _remote(mine, mine, pair_send, pair_recv, a, sib)
            cp.start()
            pair.append(cp)
        me = 4 * x + 2 * y + c
        buf[me] = p_ref[...]
        cps = []
        for k in range(1, N_DEV):
            to = (x ^ (k >> 2), y ^ ((k >> 1) & 1), c ^ (k & 1))
            cp = _remote(p_ref, buf.at[me], send_sems, recv_sems, k - 1, to)
            cp.start()
            cps.append(cp)
        for k in range(1, N_DEV):
            frm = (x ^ (k >> 2), y ^ ((k >> 1) & 1), c ^ (k & 1))
            slot = buf.at[4 * frm[0] + 2 * frm[1] + frm[2]]
            _remote(slot, slot, send_sems, recv_sems, k - 1, frm).wait_recv()
        acc = buf[0]
        for k in range(1, N_DEV):
            acc = acc + buf[k]
        o_ref[...] = acc
        for cp in cps:
            cp.wait_send()
        for a, (full, cp) in enumerate(zip(halves, pair)):
            H = full.shape[0] // 2
            other = full.at[pl.ds((1 - c) * H, H)]
            _remote(other, other, pair_send, pair_recv, a, sib).wait_recv()
            cp.wait_send()

    vm = pl.BlockSpec(memory_space=pltpu.VMEM)
    res = pl.pallas_call(
        body, name=name, in_specs=[vm] + [ANY] * n, out_specs=[vm] + [ANY] * n,
        out_shape=[_sds((R, C), F32)] + [_sds(f.shape, f.dtype) for f in fulls],
        input_output_aliases={1 + a: 1 + a for a in range(n)},
        scratch_shapes=[pltpu.VMEM((N_DEV, R, C), F32), pltpu.SemaphoreType.DMA((N_DEV - 1,)),
                        pltpu.SemaphoreType.DMA((N_DEV - 1,)), pltpu.SemaphoreType.DMA((n,)),
                        pltpu.SemaphoreType.DMA((n,))])(pack, *fulls)
    return res[0], list(res[1:])


def _local_step(xs, tgt, p, ex):
    proj, h1 = ex.project(xs, p["pre_mix_norm"])
    biases = _relbias_fwd(p["rel_bias"], "rel_bias_fwd")
    fw = []
    for g in range(N_GROUPS):
        res, got = _attn_fwd(proj, biases[g], g, f"attn_fwd{g}", plans=ex.carry(f"attn_fwd{g}"))
        ex.done(f"attn_fwd{g}", got)
        fw.append(res)
    (yh, o_h, states), got = _hgrn_fwd(proj, p["hgrn_lb_raw"], p["hgrn_norm"], "hgrn_fwd", plans=ex.carry("hgrn_fwd"))
    ex.done("hgrn_fwd", got)
    W_a, W_h, W_out = ex.weight("w_branch_attn"), ex.weight("w_branch_hgrn"), ex.weight("w_out")
    (y, lse, za, zh, merged), got = _branch_fwd([t[0] for t in fw], [t[1] for t in fw], yh, proj, W_a, W_h,
                                                "branch_fwd", plans=ex.carry("branch_fwd"))
    ex.done("branch_fwd", got)
    W_up, conv_w = ex.weight("w_up"), ex.weight("conv_w")
    mo, x1, h2 = _mix_out(merged, W_out, xs, p["post_mix_norm"], p["pre_ffn_norm"], "mix_out")
    u, got = _mm_nn_blk(h2, W_up, "ffn_up", tm=1024, plans=ex.carry("ffn_up"))
    ex.done("ffn_up", got)
    a, got = _conv_gelu_fwd(u, conv_w, p["conv_b"], "conv_gelu_fwd", plans=ex.carry("conv_gelu_fwd"))
    ex.done("conv_gelu_fwd", got)
    W_down = ex.weight("w_down")
    dx2, dff, g_post_ffn, loss = _loss_head(a, W_down, x1, tgt, p["post_ffn_norm"], "ffn_down_loss")

    ex.grad("w_down", _mm_tn(a, dff, "g_w_down").reshape(N_CHIPS, D_FF // N_CHIPS, D_MODEL))
    (dcg, dcv, gwg, gwv, gbg, gbv), got = _conv_gelu_bwd(u, dff, W_down, conv_w, p["conv_b"], "conv_gelu_bwd",
                                                          plans=ex.carry("conv_gelu_bwd"))
    ex.done("conv_gelu_bwd", got)
    g_conv_w = jnp.concatenate([gwg, gwv], axis=1)
    g_conv_b = jnp.concatenate([gbg, gbv], axis=1)
    du, got = _conv_input_bwd(dcg, dcv, conv_w, "conv_input_bwd", plans=ex.carry("conv_input_bwd"))
    ex.done("conv_input_bwd", got)
    dx1, g_pre_ffn = _mm_nt_prenorm_bwd(du, W_up, x1, p["pre_ffn_norm"], dx2, "d_ffn_in")
    ex.grad("w_up", _mm_tn_blk(h2, du, N_CHIPS, "g_w_up"))
    (dmo, dmerged, g_post_mix), got = _postnorm_bwd(dx1, mo, p["post_mix_norm"], W_out, "post_mix_norm_bwd",
                                                    plans=ex.carry("post_mix_norm_bwd"))
    ex.done("post_mix_norm_bwd", got)
    ex.grad("w_out", _mm_tn(merged, dmo, "g_w_out").reshape(N_CHIPS, D_MODEL // N_CHIPS, D_MODEL))
    (dza, dzh, dg0, dg1, dy, dyh), got = _branch_bwd(dmerged, za, zh, proj, W_a, W_h, "branch_bwd",
                                                     plans=ex.carry("branch_bwd"))
    ex.done("branch_bwd", got)
    ex.grad("w_branch_attn", _mm_tn_blk(y, dza, N_CHIPS, "g_w_branch_attn", together=True))
    ex.grad("w_branch_hgrn", _mm_tn_blk(yh, dzh, N_CHIPS, "g_w_branch_hgrn", together=True))
    dqkv, dbs = [], []
    for g in range(N_GROUPS):
        parts, db, got = _attn_bwd(proj, biases[g], lse, y, dy, g, f"attn_bwd{g}", plans=ex.carry(f"attn_bwd{g}"))
        ex.done(f"attn_bwd{g}", got)
        dqkv += parts
        dbs.append(db)
    g_rel_bias = _relbias_bwd(dbs, "rel_bias_bwd")
    dproj, g_lb_raw, g_hgrn_norm = _hgrn_bwd(proj, p["hgrn_lb_raw"], p["hgrn_norm"], o_h, states, dyh, dqkv,
                                             [dg0, dg1], "hgrn_bwd")
    for piece in W_IN_PIECES:
        g, got = _mm_tn_blk(h1, dproj, N_CHIPS, f"g_{piece}", x_cols=W_IN_ROWS[piece],
                            plans=ex.carry(f"g_{piece}"))
        ex.done(f"g_{piece}", got)
        ex.grad(piece, g)
    dh1, got = _mm_nt_blk(dproj, ex.weight("w_in"), "d_proj_in", plans=ex.carry("d_proj_in"))
    ex.done("d_proj_in", got)
    (grad_x, g_pre_mix), got = _prenorm_bwd(dh1, xs, p["pre_mix_norm"], dx1, "pre_mix_norm_bwd",
                                            plans=ex.carry("pre_mix_norm_bwd"))
    ex.done("pre_mix_norm_bwd", got)
    small = dict(pre_mix_norm=g_pre_mix, rel_bias=g_rel_bias, hgrn_lb_raw=g_lb_raw, hgrn_norm=g_hgrn_norm,
                 post_mix_norm=g_post_mix, pre_ffn_norm=g_pre_ffn, conv_w=g_conv_w, conv_b=g_conv_b,
                 post_ffn_norm=g_post_ffn)
    return loss, grad_x, small


SMALL = ("pre_mix_norm", "rel_bias", "hgrn_lb_raw", "hgrn_norm", "post_mix_norm", "pre_ffn_norm", "conv_w", "conv_b",
         "post_ffn_norm")
BIG = ("w_in", "w_up", "w_down", "w_out", "w_branch_attn", "w_branch_hgrn")
WEIGHTS = ("pre_mix_norm", "w_in", "rel_bias", "hgrn_lb_raw", "hgrn_norm", "w_branch_attn", "w_branch_hgrn", "w_out",
           "post_mix_norm", "pre_ffn_norm", "w_up", "conv_w", "conv_b", "w_down", "post_ffn_norm")
MIXER = ("w_out", "w_branch_attn", "w_branch_hgrn")

ICI_PARTS = {"gather_ici_1of3": (0, 176), "gather_ici_2of3": (176, 176), "gather_ici_3of3": (352, 160)}
SCHEDULE = {
    "proj_in": [("gather_ici_cw", MIXER)],
    "attn_fwd0": [("gather_pass", MIXER), ("gather_ici_1of3", ("w_up",))],
    "attn_fwd1": [("gather_ici_2of3", ("w_up",))],
    "attn_fwd2": [("gather_ici_3of3", ("w_up",))],
    "hgrn_fwd": [("gather_pass", ("w_up",))],
    "ffn_up": [("gather_ici", ("w_down",))],
    "conv_gelu_fwd": [("gather_pass", ("w_down",))],
    "conv_gelu_bwd": [("pair", ("w_down",))],
    "conv_input_bwd": [("chip", ("w_down",))],
    "post_mix_norm_bwd": [("pair", ("w_up",))],
    "attn_bwd0": [("chip", ("w_up",)), ("pair", MIXER)],
    "attn_bwd1": [("chip", MIXER)],
    "g_w_in_b": [("pair", ("w_in_a",))],
    "d_proj_in": [("chip", ("w_in_a",)), ("pair", ("w_in_b",))],
    "pre_mix_norm_bwd": [("chip", ("w_in_b",))],
}
W_IN_ROWS = dict(w_in_a=(0, 768), w_in_b=(3, 256))
W_IN_PIECES = tuple(W_IN_ROWS)
REDUCED = W_IN_PIECES + BIG[1:]


class _Exchange:
    def __init__(self, place, slots, conv_w_shard):
        self.place, self.slots, self.conv_w_shard = place, dict(slots), conv_w_shard
        self.conv_w = None
        self.g, self.from_sibling, self.pair_sums, self.arrived = {}, {}, {}, {}
        self.pending = []

    def weight(self, name):
        if name == "conv_w":
            return self.conv_w
        w = self.slots[name]
        return w.reshape(-1, D_MODEL) if name in ("w_out", "w_down") else w

    def project(self, x, w_norm):
        (plan,) = self.carry("proj_in")
        proj, self.slots["w_in"], h, *got = _proj_gathered(x, w_norm, self.slots["w_in"], self.place, "proj_in",
                                                           plan=plan)
        self.done("proj_in", [got])
        return proj, h

    def grad(self, name, g):
        self.g[name] = g

    def carry(self, point):
        plans = []
        self.pending = SCHEDULE.get(point, [])
        for kind, names in self.pending:
            if kind in ("gather_ici", "gather_ici_cw") or kind in ICI_PARTS:
                wholes = [self.conv_w_shard] if kind == "gather_ici_cw" else []
                plans.append(_gather_ici_plan([self.slots[n] for n in names], wholes, ICI_PARTS.get(kind)))
            elif kind == "gather_pass":
                plans.append(_gather_pass_plan([self.slots[n] for n in names]))
            elif kind == "pair":
                plans.append(_pair_plan([self.g[n] for n in names]))
            else:
                for n in names:
                    self.pair_sums[n] = _pair_sum(self.g[n], self.from_sibling[n], self.place[1:2], f"pair_sum_{n}")
                plans.append(_chip_plan([self.pair_sums[n] for n in names]))
        return plans

    def done(self, point, carried):
        for (kind, names), got in zip(self.pending, carried):
            if kind in ("gather_ici", "gather_ici_cw", "gather_pass") or kind in ICI_PARTS:
                self.slots.update(zip(names, got))
                if kind == "gather_ici_cw":
                    self.conv_w = got[len(names)].transpose(1, 0, 2).reshape(3, 2 * D_FF)
            elif kind == "pair":
                self.from_sibling.update(zip(names, got))
            else:
                self.arrived.update(zip(names, got))

    def reduced_halves(self):
        return [_chip_sum(self.arrived[n], self.pair_sums[n], self.place, f"chip_sum_{n}") for n in REDUCED]


def kernel(x, pre_mix_norm, w_in, rel_bias, hgrn_lb_raw, hgrn_norm, w_branch_attn, w_branch_hgrn, w_out, post_mix_norm, pre_ffn_norm, w_up, conv_w, conv_b, w_down, post_ffn_norm, loss_target, m_pre_mix_norm, m_w_in, m_rel_bias, m_hgrn_lb_raw, m_hgrn_norm, m_w_branch_attn, m_w_branch_hgrn, m_w_out, m_post_mix_norm, m_pre_ffn_norm, m_w_up, m_conv_w, m_conv_b, m_w_down, m_post_ffn_norm, v_pre_mix_norm, v_w_in, v_rel_bias, v_hgrn_lb_raw, v_hgrn_norm, v_w_branch_attn, v_w_branch_hgrn, v_w_out, v_post_mix_norm, v_pre_ffn_norm, v_w_up, v_conv_w, v_conv_b, v_w_down, v_post_ffn_norm):
    w = dict(pre_mix_norm=pre_mix_norm, w_in=w_in, rel_bias=rel_bias, hgrn_lb_raw=hgrn_lb_raw, hgrn_norm=hgrn_norm,
             w_branch_attn=w_branch_attn, w_branch_hgrn=w_branch_hgrn, w_out=w_out, post_mix_norm=post_mix_norm,
             pre_ffn_norm=pre_ffn_norm, w_up=w_up, conv_w=conv_w, conv_b=conv_b, w_down=w_down,
             post_ffn_norm=post_ffn_norm)
    m = dict(pre_mix_norm=m_pre_mix_norm, w_in=m_w_in, rel_bias=m_rel_bias, hgrn_lb_raw=m_hgrn_lb_raw,
             hgrn_norm=m_hgrn_norm, w_branch_attn=m_w_branch_attn, w_branch_hgrn=m_w_branch_hgrn, w_out=m_w_out,
             post_mix_norm=m_post_mix_norm, pre_ffn_norm=m_pre_ffn_norm, w_up=m_w_up, conv_w=m_conv_w,
             conv_b=m_conv_b, w_down=m_w_down, post_ffn_norm=m_post_ffn_norm)
    v = dict(pre_mix_norm=v_pre_mix_norm, w_in=v_w_in, rel_bias=v_rel_bias, hgrn_lb_raw=v_hgrn_lb_raw,
             hgrn_norm=v_hgrn_norm, w_branch_attn=v_w_branch_attn, w_branch_hgrn=v_w_branch_hgrn, w_out=v_w_out,
             post_mix_norm=v_post_mix_norm, pre_ffn_norm=v_pre_ffn_norm, w_up=v_w_up, conv_w=v_conv_w,
             conv_b=v_conv_b, w_down=v_w_down, post_ffn_norm=v_post_ffn_norm)
    shard2d = {n: (w[n][0] if w[n].ndim == 3 else w[n]) for n in WEIGHTS}
    chip = 2 * lax.axis_index("x") + lax.axis_index("y")
    core = lax.axis_index("c")

    place = jnp.stack([chip, core]).astype(jnp.int32)
    slots = {n: _cast_into_slot(shard2d[n], place, f"cast_{n}") for n in BIG}
    ex = _Exchange(place, slots, shard2d["conv_w"])
    loss, grad_x, small = _local_step(x[0], loss_target[0], {n: w[n] for n in SMALL if n != "conv_w"}, ex)

    flat = [small[n].reshape(-1) for n in SMALL] + [loss.reshape(-1)]
    sizes = [t.shape[0] for t in flat]
    summed, wholes = _all_sum(jnp.concatenate(flat).reshape(-1, LANES), ex.reduced_halves(), "sum_small")
    summed = summed.reshape(-1)
    offs = [sum(sizes[:i]) for i in range(len(sizes))]
    grads = {}
    for n, o, sz in zip(SMALL, offs, sizes):
        grads[n] = summed[o:o + sz].reshape(small[n].shape)
    loss_total = summed[offs[-1]]
    cw = 2 * D_FF // N_CHIPS
    grads["conv_w"] = lax.dynamic_slice(grads["conv_w"], (0, chip * cw), (3, cw))

    big = dict(zip(REDUCED, wholes))
    big["w_in"] = jnp.concatenate([big.pop(n) for n in W_IN_PIECES], axis=0)
    grads.update(big)

    out_g, out_d, out_m, out_v = [], [], [], []
    for n in WEIGHTS:
        d2, m2, v2 = _adamw(shard2d[n], grads[n], m[n].reshape(shard2d[n].shape), v[n].reshape(shard2d[n].shape),
                            f"adamw_{n}")
        shape = w[n].shape
        out_g.append(grads[n].reshape(shape))
        out_d.append(d2.reshape(shape))
        out_m.append(m2.reshape(shape))
        out_v.append(v2.reshape(shape))
    return (loss_total, grad_x[None], *out_g, *out_d, *out_m, *out_v)
```

```python
import functools
import math

import jax
import jax.numpy as jnp
from jax import lax
from jax.experimental import pallas as pl
from jax.experimental.pallas import tpu as pltpu

F32 = jnp.float32
BF16 = jnp.bfloat16
MESH = pl.DeviceIdType.MESH

D_MODEL = 1024
N_GROUPS = 3
DILATIONS = (1, 4, 16)
HEADS = 8
HEAD_DIM = 64
GROUP_W = HEADS * HEAD_DIM
QKV_W = N_GROUPS * 3 * GROUP_W
BLK = 128
NEG_INF = -1e30
NUM_BUCKETS = 32
MAX_EXACT = 16
MAX_DISTANCE = 2048
HG_HEADS = 4
HG_DK = 128
HG_W = HG_HEADS * HG_DK
HG_CHUNK = 32
HG_TILE = 256
IN_W = QKV_W + 4 * HG_W + 2 * D_MODEL
D_FF = 2816
EPS = 1e-6
N_CHIPS = 4
N_DEV = 8
LANES = 128

ADAM_LR, ADAM_B1, ADAM_B2, ADAM_EPS, ADAM_WD, ADAM_STEP = 0.001, 0.9, 0.999, 1e-08, 0.01, 10

VMEM_LIMIT = 56 * 1024 * 1024


def _cp(n_axes):
    return pltpu.CompilerParams(dimension_semantics=("arbitrary",) * n_axes, vmem_limit_bytes=VMEM_LIMIT)


def _sds(shape, dtype):
    return jax.ShapeDtypeStruct(tuple(shape), dtype)


def _sigmoid(v):
    return 1.0 / (1.0 + jnp.exp(-v))


def _bf(v):
    return v.astype(BF16)


def _dot(a, b, dims):
    return lax.dot_general(a, b, (dims, ((), ())), preferred_element_type=F32)


NN = ((1,), (0,))
NT = ((1,), (1,))
TN = ((0,), (0,))

ANY = pl.BlockSpec(memory_space=pl.ANY)


class _Plan:
    def __init__(self, copies, n_sems, ins=(), inouts=(), outs=()):
        self.copies, self.n_sems = copies, n_sems
        self.ins, self.inouts, self.outs = list(ins), list(inouts), list(outs)


def _call(body, plans=None, *, name, grid, in_specs, out_specs, out_shape, args, scratch_shapes=()):
    plans = list(plans or ())
    in_specs, out_specs, out_shape = list(in_specs), list(out_specs), list(out_shape)
    scratch_shapes = list(scratch_shapes)
    n_in, n_out, n_scr = len(in_specs), len(out_specs), len(scratch_shapes)
    x_in, x_out, aliases, spans = [], [], {}, []
    for p in plans:
        i0, o0 = len(x_in), len(x_out)
        x_in += p.ins
        for a in p.inouts:
            aliases[n_in + len(x_in)] = n_out + len(x_out)
            x_in.append(a)
            x_out.append(_sds(a.shape, a.dtype))
        x_out += p.outs
        spans.append((i0, len(p.ins), o0, len(p.inouts), len(p.outs)))
    sems = [pltpu.SemaphoreType.DMA((p.n_sems,)) for p in plans for _ in range(3)]

    def wrapped(*refs):
        xi = refs[n_in:n_in + len(x_in)]
        base = n_in + len(x_in)
        xo = refs[base + n_out:base + n_out + len(x_out)]
        sbase = base + n_out + len(x_out)
        xs = refs[sbase + n_scr:]
        ids = [pl.program_id(k) for k in range(len(grid))]
        first = functools.reduce(jnp.logical_and, [i == 0 for i in ids])
        last = functools.reduce(jnp.logical_and, [i == g - 1 for i, g in zip(ids, grid)])

        def descriptors(k):
            i0, ni, o0, nio, no = spans[k]
            return plans[k].copies(xi[i0:i0 + ni], xo[o0:o0 + nio], xo[o0 + nio:o0 + nio + no], *xs[3 * k:3 * k + 3])

        @pl.when(first)
        def _():
            for k in range(len(plans)):
                sends, _, local = descriptors(k)
                for cp in (*sends, *local):
                    cp.start()

        body(*refs[:n_in], *refs[base:base + n_out], *refs[sbase:sbase + n_scr])

        @pl.when(last)
        def _():
            for k in range(len(plans)):
                sends, recvs, local = descriptors(k)
                for cp in recvs:
                    cp.wait_recv()
                for cp in sends:
                    cp.wait_send()
                for cp in local:
                    cp.wait()

    res = pl.pallas_call(
        wrapped if plans else body, name=name, grid=grid, in_specs=in_specs + [ANY] * len(x_in),
        out_specs=out_specs + [ANY] * len(x_out), out_shape=out_shape + x_out, input_output_aliases=aliases,
        scratch_shapes=scratch_shapes + sems, compiler_params=_cp(len(grid)))(*args, *x_in)
    res = list(res)
    carried = [res[n_out + o0:n_out + o0 + nio + no] for (_, _, o0, nio, no) in spans]
    return res[:n_out], carried


def _mm_nn_blk(a, wg, name, tm=512, plans=None):
    M, K = a.shape
    nb, _, Nb = wg.shape

    def body(a_ref, w_ref, o_ref):
        o_ref[...] = _dot(_bf(a_ref[...]), w_ref[...], NN)

    (out,), carried = _call(
        body, plans, name=name, grid=(nb, M // tm),
        in_specs=[pl.BlockSpec((tm, K), lambda j, i: (i, 0)), pl.BlockSpec((None, K, Nb), lambda j, i: (j, 0, 0))],
        out_specs=[pl.BlockSpec((tm, Nb), lambda j, i: (i, j))],
        out_shape=[_sds((M, nb * Nb), F32)], args=(a, wg))
    return out if plans is None else (out, carried)


def _mm_nt_blk(dy, wg, name, tm=1024, plans=None):
    M = dy.shape[0]
    nb, K, Nb = wg.shape

    def body(dy_ref, w_ref, o_ref):
        j = pl.program_id(1)
        r = _dot(_bf(dy_ref[...]), w_ref[...], NT)

        @pl.when(j == 0)
        def _():
            o_ref[...] = r

        @pl.when(j > 0)
        def _():
            o_ref[...] += r

    (out,), carried = _call(
        body, plans, name=name, grid=(M // tm, nb),
        in_specs=[pl.BlockSpec((tm, Nb), lambda i, j: (i, j)), pl.BlockSpec((None, K, Nb), lambda i, j: (j, 0, 0))],
        out_specs=[pl.BlockSpec((tm, K), lambda i, j: (i, 0))],
        out_shape=[_sds((M, K), F32)], args=(dy, wg))
    return out if plans is None else (out, carried)


def _mm_nt_prenorm_bwd(dy, wg, xin, w, dres, name, tm=1024):
    M = dy.shape[0]
    nb, K, Nb = wg.shape

    def body(dy_ref, w_ref, x_ref, wn_ref, dres_ref, dx_ref, dw_ref, acc):
        i, j = pl.program_id(0), pl.program_id(1)
        r = _dot(_bf(dy_ref[...]), w_ref[...], NT)

        @pl.when(j == 0)
        def _():
            acc[...] = r

        @pl.when(j > 0)
        def _():
            acc[...] += r

        @pl.when(j == nb - 1)
        def _():
            xv = x_ref[...]
            rinv = _rinv(xv)
            xhat = xv * rinv
            dh = acc[...]
            dx_ref[...] = dres_ref[...] + _norm_bwd(dh, xhat, rinv, wn_ref[...])
            part = jnp.sum(dh * xhat, axis=0, keepdims=True)

            @pl.when(i == 0)
            def _():
                dw_ref[...] = part

            @pl.when(i > 0)
            def _():
                dw_ref[...] += part

    row = pl.BlockSpec((tm, K), lambda i, j: (i, 0))
    vec = pl.BlockSpec((1, K), lambda i, j: (0, 0))
    return pl.pallas_call(
        body, name=name, grid=(M // tm, nb),
        in_specs=[pl.BlockSpec((tm, Nb), lambda i, j: (i, j)), pl.BlockSpec((None, K, Nb), lambda i, j: (j, 0, 0)),
                  row, vec, row],
        out_specs=[row, vec], out_shape=[_sds((M, K), F32), _sds((1, K), F32)],
        scratch_shapes=[pltpu.VMEM((tm, K), F32)], compiler_params=_cp(2))(dy, wg, xin, w, dres)


def _mm_tn_blk(x, dy, nb, name, tk=2048, x_cols=None, plans=None, together=False):
    T, Mx = x.shape
    xk, Mx = (0, Mx) if x_cols is None else x_cols
    Nb = dy.shape[1] // nb
    nj = nb if together else 1

    def body(x_ref, dy_ref, o_ref):
        t = pl.program_id(1)
        r = _dot(_bf(x_ref[...]), _bf(dy_ref[...]), TN)
        for j in range(nj):
            rj = r[:, j * Nb:(j + 1) * Nb]

            @pl.when(t == 0)
            def _():
                o_ref[j] = rj

            @pl.when(t > 0)
            def _():
                o_ref[j] += rj

    (out,), carried = _call(
        body, plans, name=name, grid=(nb // nj, T // tk),
        in_specs=[pl.BlockSpec((tk, Mx), lambda j, t: (t, xk)), pl.BlockSpec((tk, nj * Nb), lambda j, t: (t, j))],
        out_specs=[pl.BlockSpec((nj, Mx, Nb), lambda j, t: (j, 0, 0))],
        out_shape=[_sds((nb, Mx, Nb), F32)], args=(x, dy))
    return out if plans is None else (out, carried)


def _mm_tn(x, dy, name, tk=1024):
    T, Mx = x.shape
    N = dy.shape[1]

    def body(x_ref, dy_ref, o_ref):
        t = pl.program_id(0)
        r = _dot(_bf(x_ref[...]), _bf(dy_ref[...]), TN)

        @pl.when(t == 0)
        def _():
            o_ref[...] = r

        @pl.when(t > 0)
        def _():
            o_ref[...] += r

    return pl.pallas_call(
        body, name=name, grid=(T // tk,),
        in_specs=[pl.BlockSpec((tk, Mx), lambda t: (t, 0)), pl.BlockSpec((tk, N), lambda t: (t, 0))],
        out_specs=pl.BlockSpec((Mx, N), lambda t: (0, 0)),
        out_shape=_sds((Mx, N), F32), compiler_params=_cp(1))(x, dy)


def _tile(arr, bw, col=lambda c: 0):
    return ("tile", arr, bw, col)


def _full(arr):
    return ("full", arr)


def _out_tile(width, dtype, bw, col=lambda c: 0):
    return ("tile", width, dtype, bw, col)


def _out_acc(rows, width, bw, col=lambda c: 0):
    return ("acc", rows, width, bw, col)


def _rows_call(name, body, n_rows, tm, ncol, ins, outs, plans=None):
    in_specs, args = [], []
    for e in ins:
        if e[0] == "tile":
            _, arr, bw, col = e
            in_specs.append(pl.BlockSpec((tm, bw), functools.partial(lambda c, i, col: (i, col(c)), col=col)))
        else:
            arr = e[1]
            in_specs.append(pl.BlockSpec(arr.shape, functools.partial(lambda c, i, nd: (0,) * nd, nd=arr.ndim)))
        args.append(arr)
    out_specs, out_shape = [], []
    for e in outs:
        if e[0] == "tile":
            _, width, dtype, bw, col = e
            out_specs.append(pl.BlockSpec((tm, bw), functools.partial(lambda c, i, col: (i, col(c)), col=col)))
            out_shape.append(_sds((n_rows, width), dtype))
        else:
            _, rows, width, bw, col = e
            out_specs.append(pl.BlockSpec((rows, bw), functools.partial(lambda c, i, col: (0, col(c)), col=col)))
            out_shape.append(_sds((rows, width), F32))
    out, carried = _call(body, plans, name=name, grid=(ncol, n_rows // tm), in_specs=in_specs, out_specs=out_specs,
                         out_shape=out_shape, args=args)
    return out if plans is None else (out, carried)


def _acc(ref, val):
    i = pl.program_id(1)

    @pl.when(i == 0)
    def _():
        ref[...] = val

    @pl.when(i > 0)
    def _():
        ref[...] += val


def _rinv(z):
    return lax.rsqrt(jnp.mean(z * z, axis=-1, keepdims=True) + EPS)


def _norm_bwd(dy, zhat, r, w):
    dyw = dy * w
    return r * (dyw - zhat * jnp.mean(dyw * zhat, axis=-1, keepdims=True))


def _prenorm_bwd(dh, xin, w, dres, name, plans=None):
    def body(dh_ref, x_ref, w_ref, dres_ref, dx_ref, dw_ref):
        xv = x_ref[...]
        r = _rinv(xv)
        xhat = xv * r
        dhv = dh_ref[...]
        dx_ref[...] = dres_ref[...] + _norm_bwd(dhv, xhat, r, w_ref[...])
        _acc(dw_ref, jnp.sum(dhv * xhat, axis=0, keepdims=True))

    return _rows_call(name, body, xin.shape[0], 512, 1,
                      [_tile(dh, D_MODEL), _tile(xin, D_MODEL), _full(w), _tile(dres, D_MODEL)],
                      [_out_tile(D_MODEL, F32, D_MODEL), _out_acc(1, D_MODEL, D_MODEL)], plans)


def _postnorm_bwd(dout, z, w, w_mat, name, plans=None):
    def body(do_ref, z_ref, w_ref, wm_ref, dz_ref, dm_ref, dw_ref):
        zv = z_ref[...]
        r = _rinv(zv)
        zhat = zv * r
        dov = do_ref[...]
        dz = _bf(_norm_bwd(dov, zhat, r, w_ref[...]))
        dz_ref[...] = dz
        dm_ref[...] = _dot(dz, wm_ref[...], NT)
        _acc(dw_ref, jnp.sum(dov * zhat, axis=0, keepdims=True))

    return _rows_call(name, body, z.shape[0], 512, 1,
                      [_tile(dout, D_MODEL), _tile(z, D_MODEL), _full(w), _full(w_mat)],
                      [_out_tile(D_MODEL, BF16, D_MODEL), _out_tile(D_MODEL, F32, D_MODEL),
                       _out_acc(1, D_MODEL, D_MODEL)], plans)


def _t5_bucket(dist):
    n = jnp.maximum(dist, 0)
    nf = jnp.maximum(n, 1).astype(F32)
    large = MAX_EXACT + (jnp.log(nf / MAX_EXACT) / math.log(MAX_DISTANCE / MAX_EXACT)
                         * (NUM_BUCKETS - MAX_EXACT)).astype(jnp.int32)
    large = jnp.minimum(large, NUM_BUCKETS - 1)
    return jnp.where(n < MAX_EXACT, n, large)


def _band_rel():
    return jnp.arange(BLK)[:, None] + BLK - jnp.arange(2 * BLK)[None, :]


def _band_valid():
    rel = _band_rel()
    window = (rel >= 0) & (rel <= BLK)
    first = window & (jnp.arange(2 * BLK)[None, :] >= BLK)
    return jnp.stack([first, window]).astype(F32).reshape(2, 1, BAND)


RES_UNROLL = 8
PAIR = LANES // HEAD_DIM


def _pair_lanes():
    first = lax.broadcasted_iota(jnp.int32, (1, LANES), 1) < HEAD_DIM
    return first, jnp.logical_not(first)


def _heads_per_step(d):
    return HEADS if d == 1 else LANES // HEAD_DIM


def _sub_rows(r, d):
    return pl.ds(r, BLK, stride=d) if d > 1 else pl.ds(0, BLK)


def _for_residues(d, fn):
    if d <= RES_UNROLL:
        for r in range(d):
            fn(r)
    else:
        def group(i, carry):
            for k in range(RES_UNROLL):
                fn(i * RES_UNROLL + k)
            return carry

        lax.fori_loop(0, d // RES_UNROLL, group, 0)


def _attn_specs(d, g, qblock):
    cw = _heads_per_step(d) * HEAD_DIM

    def col(part, hp):
        return (g * 3 + part) * (GROUP_W // cw) + hp

    def cur(part):
        return pl.BlockSpec((d * BLK, cw), lambda hp, n: (qblock(n), col(part, hp)))

    def prev(part):
        return pl.BlockSpec((d * BLK, cw), lambda hp, n: (jnp.maximum(qblock(n) - 1, 0), col(part, hp)))

    return cur, prev


def _attn_fwd(proj, bias, g, name, plans=None):
    S = proj.shape[0]
    d = DILATIONS[g]
    NB = S // (d * BLK)
    hps = _heads_per_step(d)

    def body(q_ref, kp_ref, kc_ref, vp_ref, vc_ref, b_ref, o_ref, lse_ref):
        hp = pl.program_id(0)
        later = jnp.minimum(pl.program_id(1), 1)

        def residue(r):
            rows = _sub_rows(r, d)
            q2 = q_ref[rows, :]
            k2 = jnp.concatenate([kp_ref[rows, :], kc_ref[rows, :]], axis=0)
            v2 = jnp.concatenate([vp_ref[rows, :], vc_ref[rows, :]], axis=0)
            outs, lses = [], []
            for pp in range(hps // PAIR):
                ps = slice(pp * LANES, (pp + 1) * LANES)
                qp, kp, vp = _bf(q2[:, ps]), _bf(k2[:, ps]), _bf(v2[:, ps])
                o_h, lse_h = [], []
                for hh, own in enumerate(_pair_lanes()):
                    s = _dot(qp, jnp.where(own, kp, 0), NT) * (HEAD_DIM ** -0.5) + b_ref[later, hp * hps + pp * PAIR + hh]
                    m = jnp.max(s, axis=-1, keepdims=True)
                    p = jnp.exp(s - m)
                    l = jnp.sum(p, axis=-1, keepdims=True)
                    o_h.append(_dot(_bf(p), vp, NN) / l)
                    lse_h.append(m + jnp.log(l))
                first = _pair_lanes()[0]
                outs.append(jnp.where(first, o_h[0], o_h[1]))
                lses.append(jnp.where(first, lse_h[0], lse_h[1]))
            o_ref[rows, :] = outs[0] if len(outs) == 1 else jnp.concatenate(outs, axis=1)
            lse_ref[rows, :] = lses[0] if len(lses) == 1 else jnp.concatenate(lses, axis=1)

        _for_residues(d, residue)

    cur, prev = _attn_specs(d, g, lambda n: n)
    out = pl.BlockSpec((d * BLK, hps * HEAD_DIM), lambda hp, n: (n, hp))
    res, carried = _call(
        body, plans, name=name, grid=(HEADS // hps, NB),
        in_specs=[cur(0), prev(1), cur(1), prev(2), cur(2),
                  pl.BlockSpec((2, HEADS, BLK, 2 * BLK), lambda hp, n: (0, 0, 0, 0))],
        out_specs=[out, out], out_shape=[_sds((S, GROUP_W), F32)] * 2,
        args=(proj, proj, proj, proj, proj, bias))
    return res if plans is None else (res, carried)


def _attn_bwd(proj, bias, lse, y, dy, g, name, plans=None):
    S = proj.shape[0]
    d = DILATIONS[g]
    NB = S // (d * BLK)
    hps = _heads_per_step(d)

    def body(q_ref, kp_ref, kc_ref, vp_ref, vc_ref, b_ref, l_ref, y_ref, dy_ref,
             dq_ref, dk_ref, dv_ref, db_ref, ck_ref, cv_ref):
        hp, n = pl.program_id(0), pl.program_id(1)

        @pl.when((hp == 0) & (n == 0))
        def _():
            db_ref[...] = jnp.zeros_like(db_ref)

        @pl.when(n == 0)
        def _():
            ck_ref[...] = jnp.zeros_like(ck_ref)
            cv_ref[...] = jnp.zeros_like(cv_ref)

        @pl.when(n < NB)
        def _():
            later = jnp.minimum(n, 1)

            def residue(r):
                rows = _sub_rows(r, d)
                q2 = q_ref[rows, :]
                k2 = jnp.concatenate([kp_ref[rows, :], kc_ref[rows, :]], axis=0)
                v2 = jnp.concatenate([vp_ref[rows, :], vc_ref[rows, :]], axis=0)
                l2, y2, dy2 = l_ref[rows, :], y_ref[rows, :], dy_ref[rows, :]
                dqs, dks, dvs = [], [], []
                for pp in range(hps // PAIR):
                    ps = slice(pp * LANES, (pp + 1) * LANES)
                    qp, kp, vp = _bf(q2[:, ps]), _bf(k2[:, ps]), _bf(v2[:, ps])
                    dyp, yp = dy2[:, ps], y2[:, ps]
                    dq_h, dk_h, dv_h = [], [], []
                    for hh, own in enumerate(_pair_lanes()):
                        head = hp * hps + pp * PAIR + hh
                        s = _dot(qp, jnp.where(own, kp, 0), NT) * (HEAD_DIM ** -0.5) + b_ref[later, head]
                        p = jnp.exp(s - l2[:, pp * LANES + hh * HEAD_DIM:pp * LANES + hh * HEAD_DIM + 1])
                        dyh = jnp.where(own, dyp, 0.0)
                        delta = jnp.sum(dyh * yp, axis=-1, keepdims=True)
                        ds = p * (_dot(_bf(dyh), vp, NT) - delta)
                        db_ref[head] += ds
                        dsb = _bf(ds * (HEAD_DIM ** -0.5))
                        dq_h.append(_dot(dsb, kp, NN))
                        dk_h.append(_dot(dsb, qp, TN))
                        dv_h.append(_dot(_bf(p), _bf(dyp), TN))
                    first = _pair_lanes()[0]
                    dqs.append(jnp.where(first, dq_h[0], dq_h[1]))
                    dks.append(jnp.where(first, dk_h[0], dk_h[1]))
                    dvs.append(jnp.where(first, dv_h[0], dv_h[1]))
                dkb = dks[0] if len(dks) == 1 else jnp.concatenate(dks, axis=1)
                dvb = dvs[0] if len(dvs) == 1 else jnp.concatenate(dvs, axis=1)
                dq_ref[rows, :] = dqs[0] if len(dqs) == 1 else jnp.concatenate(dqs, axis=1)
                dk_ref[rows, :] = ck_ref[rows, :] + dkb[:BLK]
                dv_ref[rows, :] = cv_ref[rows, :] + dvb[:BLK]
                ck_ref[rows, :] = dkb[BLK:]
                cv_ref[rows, :] = dvb[BLK:]

            _for_residues(d, residue)

        @pl.when(n == NB)
        def _():
            dk_ref[...] = ck_ref[...]
            dv_ref[...] = cv_ref[...]

    def qn(n):
        return jnp.minimum(n, NB - 1)

    cur, prev = _attn_specs(d, g, qn)
    cw = hps * HEAD_DIM
    row = pl.BlockSpec((d * BLK, cw), lambda hp, n: (qn(n), hp))
    done = pl.BlockSpec((d * BLK, cw), lambda hp, n: (jnp.maximum(n - 1, 0), hp))
    (dq, dk, dv, db), carried = _call(
        body, plans, name=name, grid=(HEADS // hps, NB + 1),
        in_specs=[cur(0), prev(1), cur(1), prev(2), cur(2),
                  pl.BlockSpec((2, HEADS, BLK, 2 * BLK), lambda hp, n: (0, 0, 0, 0)), row, row, row],
        out_specs=[row, done, done, pl.BlockSpec((HEADS, BLK, 2 * BLK), lambda hp, n: (0, 0, 0))],
        out_shape=[_sds((S, GROUP_W), F32)] * 3 + [_sds((HEADS, BLK, 2 * BLK), F32)],
        scratch_shapes=[pltpu.VMEM((d * BLK, cw), F32)] * 2,
        args=(proj, proj, proj, proj, proj, bias, lse, y, dy))
    return ([dq, dk, dv], db) if plans is None else ([dq, dk, dv], db, carried)


BAND = BLK * 2 * BLK


def _bucket_onehot():
    buckets = jnp.stack([_t5_bucket(_band_rel() * d) for d in DILATIONS]).reshape(N_GROUPS, 1, BAND)
    return (buckets == jnp.arange(NUM_BUCKETS).reshape(1, NUM_BUCKETS, 1)).astype(F32)


def _relbias_fwd(rel_bias, name):
    table = rel_bias.reshape(NUM_BUCKETS, N_GROUPS, HEADS).transpose(1, 0, 2)

    def body(t_ref, oh_ref, valid_ref, o_ref):
        bias = lax.dot_general(t_ref[...], oh_ref[...], (TN, ((), ())), preferred_element_type=F32,
                               precision=lax.Precision.HIGHEST)
        for k in range(2):
            o_ref[k] = jnp.where(valid_ref[k] > 0.5, bias, NEG_INF)

    out = pl.pallas_call(
        body, name=name, grid=(N_GROUPS,),
        in_specs=[pl.BlockSpec((None, NUM_BUCKETS, HEADS), lambda g: (g, 0, 0)),
                  pl.BlockSpec((None, NUM_BUCKETS, BAND), lambda g: (g, 0, 0)),
                  pl.BlockSpec((2, 1, BAND), lambda g: (0, 0, 0))],
        out_specs=pl.BlockSpec((None, 2, HEADS, BAND), lambda g: (g, 0, 0, 0)),
        out_shape=_sds((N_GROUPS, 2, HEADS, BAND), F32), compiler_params=_cp(1))(table, _bucket_onehot(), _band_valid())
    return out.reshape(N_GROUPS, 2, HEADS, BLK, 2 * BLK)


def _relbias_bwd(dbs, name):
    band = BAND
    onehot = _bucket_onehot()
    dbf = jnp.stack([db.reshape(HEADS, band) for db in dbs])

    def body(oh_ref, db_ref, o_ref):
        o_ref[...] = lax.dot_general(oh_ref[...], db_ref[...], (NT, ((), ())), preferred_element_type=F32,
                                     precision=lax.Precision.HIGHEST)

    out = pl.pallas_call(
        body, name=name, grid=(N_GROUPS,),
        in_specs=[pl.BlockSpec((None, NUM_BUCKETS, band), lambda g: (g, 0, 0)),
                  pl.BlockSpec((None, HEADS, band), lambda g: (g, 0, 0))],
        out_specs=pl.BlockSpec((None, NUM_BUCKETS, HEADS), lambda g: (g, 0, 0)),
        out_shape=_sds((N_GROUPS, NUM_BUCKETS, HEADS), F32), compiler_params=_cp(1))(onehot, dbf)
    return out.transpose(1, 0, 2).reshape(NUM_BUCKETS, N_GROUPS * HEADS)


def _chunk_pos(shape):
    return lax.broadcasted_iota(jnp.int32, shape, 0) % HG_CHUNK


def _chunk_cumsum(v):
    pos = _chunk_pos(v.shape)
    s = 1
    while s < HG_CHUNK:
        v = v + jnp.where(pos >= s, pltpu.roll(v, s, 0), 0.0)
        s *= 2
    return v


def _chunk_rev_cumsum(v):
    pos = _chunk_pos(v.shape)
    n = v.shape[0]
    s = 1
    while s < HG_CHUNK:
        v = v + jnp.where(pos < HG_CHUNK - s, pltpu.roll(v, n - s, 0), 0.0)
        s *= 2
    return v


def _lower_bound(raw):
    a0, a1 = raw[0:1], raw[1:2]
    m = jnp.maximum(a0, a1)
    e0, e1 = jnp.exp(a0 - m), jnp.exp(a1 - m)
    return e0 / (e0 + e1)


def _hg_gates(qr, fr, lb):
    sf = _sigmoid(fr)
    f = lb + (1.0 - lb) * sf
    sq = _sigmoid(qr)
    return qr * sq, sq, f, sf


HG_COL0 = QKV_W // HG_W


def _hgrn_fwd(proj, lb_raw, nw, name, plans=None):
    S = proj.shape[0]
    ncs = HG_TILE // HG_CHUNK

    def body(q_ref, f_ref, i_ref, og_ref, lb_ref, nw_ref, y_ref, o_ref, st_ref, state):
        @pl.when(pl.program_id(0) == 0)
        def _():
            state[...] = jnp.zeros_like(state)

        lb = _lower_bound(lb_ref[...])
        q, _, f, _ = _hg_gates(q_ref[...], f_ref[...], lb)
        k = 1.0 - f
        G = _chunk_cumsum(jnp.log(f))
        row = lax.broadcasted_iota(jnp.int32, (HG_CHUNK, HG_CHUNK), 0)
        col = lax.broadcasted_iota(jnp.int32, (HG_CHUNK, HG_CHUNK), 1)
        heads = [slice(h * HG_DK, (h + 1) * HG_DK) for h in range(HG_HEADS)]
        sts = [state[h] for h in range(HG_HEADS)]
        for c in range(ncs):
            cs = slice(c * HG_CHUNK, (c + 1) * HG_CHUNK)
            for h, hs in enumerate(heads):
                Gc = G[cs, hs]
                gl = Gc[HG_CHUNK - 1:HG_CHUNK]
                qt = _bf(q[cs, hs] * jnp.exp(Gc))
                kt = _bf(k[cs, hs] * jnp.exp(-Gc))
                kd = _bf(k[cs, hs] * jnp.exp(gl - Gc))
                v = _bf(i_ref[cs, hs])
                A = jnp.where(row >= col, _dot(qt, kt, NT), 0.0)
                o_ref[cs, hs] = _dot(_bf(A), v, NN) + _dot(qt, _bf(sts[h]), NT)
                st_ref[c, h] = sts[h]
                sts[h] = sts[h] * jnp.exp(gl) + _dot(v, kd, TN)
        for h, hs in enumerate(heads):
            state[h] = sts[h]
            oh = o_ref[:, hs]
            og = og_ref[:, hs]
            y_ref[:, hs] = oh * _rinv(oh) * nw_ref[...] * (og * _sigmoid(og))

    def colspec(j):
        return pl.BlockSpec((HG_TILE, HG_W), lambda i: (i, HG_COL0 + j))

    res, carried = _call(
        body, plans, name=name, grid=(S // HG_TILE,),
        in_specs=[colspec(0), colspec(1), colspec(2), colspec(3),
                  pl.BlockSpec((2, HG_W), lambda i: (0, 0)), pl.BlockSpec((1, HG_DK), lambda i: (0, 0))],
        out_specs=[pl.BlockSpec((HG_TILE, HG_W), lambda i: (i, 0))] * 2
        + [pl.BlockSpec((ncs, HG_HEADS, HG_DK, HG_DK), lambda i: (i, 0, 0, 0))],
        out_shape=[_sds((S, HG_W), F32)] * 2 + [_sds((S // HG_CHUNK, HG_HEADS, HG_DK, HG_DK), F32)],
        scratch_shapes=[pltpu.VMEM((HG_HEADS, HG_DK, HG_DK), F32)],
        args=(proj, proj, proj, proj, lb_raw, nw))
    return res if plans is None else (res, carried)


def _hgrn_bwd(proj, lb_raw, nw, o, states, dy, d_attn, d_gates, name):
    S = proj.shape[0]
    ncs = HG_TILE // HG_CHUNK
    nt = S // HG_TILE
    n_a, n_g = len(d_attn), len(d_gates)
    own = [slice(QKV_W + j * HG_W, QKV_W + (j + 1) * HG_W) for j in range(4)]

    def body(q_ref, f_ref, i_ref, og_ref, lb_ref, nw_ref, o_ref, st_ref, dy_ref, *rest):
        attn_refs, gate_refs = rest[:n_a], rest[n_a:n_a + n_g]
        dp_ref, dlb_ref, dnw_ref, dstate, do_s, dG_s, dgl_s, dk_s, dlb_s = rest[n_a + n_g:]
        dq_ref, df_ref, di_ref, dog_ref = (dp_ref.at[:, cols] for cols in own)
        step = pl.program_id(0)
        for k, a_ref in enumerate(attn_refs):
            dp_ref[:, k * GROUP_W:(k + 1) * GROUP_W] = _bf(a_ref[...])
        for k, g_ref in enumerate(gate_refs):
            dp_ref[:, QKV_W + 4 * HG_W + k * D_MODEL:QKV_W + 4 * HG_W + (k + 1) * D_MODEL] = g_ref[...]

        @pl.when(step == 0)
        def _():
            dstate[...] = jnp.zeros_like(dstate)
            dlb_s[...] = jnp.zeros_like(dlb_s)
            dnw_ref[...] = jnp.zeros_like(dnw_ref)

        lb = _lower_bound(lb_ref[...])
        qr = q_ref[...]
        q, sq, f, sf = _hg_gates(qr, f_ref[...], lb)
        k = 1.0 - f
        G = _chunk_cumsum(jnp.log(f))
        nwv = nw_ref[...]
        row = lax.broadcasted_iota(jnp.int32, (HG_CHUNK, HG_CHUNK), 0)
        col = lax.broadcasted_iota(jnp.int32, (HG_CHUNK, HG_CHUNK), 1)
        for h in range(HG_HEADS):
            hs = slice(h * HG_DK, (h + 1) * HG_DK)
            oh = o_ref[:, hs]
            r = _rinv(oh)
            ohat = oh * r
            og = og_ref[:, hs]
            sg = _sigmoid(og)
            dyh = dy_ref[:, hs]
            don = dyh * (og * sg)
            dog_ref[:, hs] = _bf(dyh * (ohat * nwv) * (sg * (1.0 + og * (1.0 - sg))))
            dnw_ref[...] += jnp.sum(don * ohat, axis=0, keepdims=True)
            do_s[:, hs] = _norm_bwd(don, ohat, r, nwv)
        dsts = [dstate[h] for h in range(HG_HEADS)]
        for c in reversed(range(ncs)):
            cs = slice(c * HG_CHUNK, (c + 1) * HG_CHUNK)
            for h in range(HG_HEADS):
                hs = slice(h * HG_DK, (h + 1) * HG_DK)
                dst = dsts[h]
                Gc = G[cs, hs]
                gl = Gc[HG_CHUNK - 1:HG_CHUNK]
                eG, enG, edG, egl = jnp.exp(Gc), jnp.exp(-Gc), jnp.exp(gl - Gc), jnp.exp(gl)
                qt, kt, kd = q[cs, hs] * eG, k[cs, hs] * enG, k[cs, hs] * edG
                qtb, ktb, kdb = _bf(qt), _bf(kt), _bf(kd)
                v = _bf(i_ref[cs, hs])
                do = _bf(do_s[cs, hs])
                st = st_ref[c, h]
                dstb = _bf(dst)
                A = jnp.where(row >= col, _dot(qtb, ktb, NT), 0.0)
                dA = _bf(jnp.where(row >= col, _dot(do, v, NT), 0.0))
                di_ref[cs, hs] = _bf(_dot(_bf(A), do, TN) + _dot(kdb, dstb, NT))
                dqt = _dot(dA, ktb, NN) + _dot(do, _bf(st), NN)
                dkt = _dot(dA, qtb, TN)
                dkd = _dot(v, dstb, NN)
                dgl = egl * jnp.sum(st * dst, axis=0, keepdims=True) + jnp.sum(dkd * kd, axis=0, keepdims=True)
                dsts[h] = dst * egl + _dot(do, qtb, TN)
                dq_ref[cs, hs] = _bf(dqt * eG * (sq[cs, hs] * (1.0 + qr[cs, hs] * (1.0 - sq[cs, hs]))))
                dk_s[cs, hs] = dkt * enG + dkd * edG
                dG_s[cs, hs] = dqt * qt - dkt * kt - dkd * kd
                dgl_s[cs, hs] = jnp.broadcast_to(dgl, (HG_CHUNK, HG_DK))
        for h in range(HG_HEADS):
            dstate[h] = dsts[h]
        dg = _chunk_rev_cumsum(dG_s[...]) + dgl_s[...]
        dfv = dg / f - dk_s[...]
        df_ref[...] = _bf(dfv * (1.0 - lb) * sf * (1.0 - sf))
        dlb_s[...] += jnp.sum(dfv * (1.0 - sf), axis=0, keepdims=True)

        @pl.when(step == nt - 1)
        def _():
            t = dlb_s[...] * lb * (1.0 - lb)
            dlb_ref[...] = jnp.concatenate([t, -t], axis=0)

    def colspec(j):
        return pl.BlockSpec((HG_TILE, HG_W), lambda i: (nt - 1 - i, HG_COL0 + j))

    def rows(width):
        return pl.BlockSpec((HG_TILE, width), lambda i: (nt - 1 - i, 0))

    tile = rows(HG_W)
    return pl.pallas_call(
        body, name=name, grid=(nt,),
        in_specs=[colspec(0), colspec(1), colspec(2), colspec(3),
                  pl.BlockSpec((2, HG_W), lambda i: (0, 0)), pl.BlockSpec((1, HG_DK), lambda i: (0, 0)),
                  tile, pl.BlockSpec((ncs, HG_HEADS, HG_DK, HG_DK), lambda i: (nt - 1 - i, 0, 0, 0)), tile]
        + [rows(GROUP_W)] * n_a + [rows(D_MODEL)] * n_g,
        out_specs=[rows(IN_W), pl.BlockSpec((2, HG_W), lambda i: (0, 0)), pl.BlockSpec((1, HG_DK), lambda i: (0, 0))],
        out_shape=[_sds((S, IN_W), BF16), _sds((2, HG_W), F32), _sds((1, HG_DK), F32)],
        scratch_shapes=[pltpu.VMEM((HG_HEADS, HG_DK, HG_DK), F32)] + [pltpu.VMEM((HG_TILE, HG_W), F32)] * 4
        + [pltpu.VMEM((1, HG_W), F32)],
        compiler_params=_cp(1))(proj, proj, proj, proj, lb_raw, nw, o, states, dy, *d_attn, *d_gates)


GATE_COL0 = (QKV_W + 4 * HG_W) // GROUP_W
HALF_D = D_MODEL // 2


def _gate_tiles(proj):
    return [_tile(proj, HALF_D, functools.partial(lambda c, k: GATE_COL0 + k, k=k)) for k in range(4)]


def _gates(g_refs):
    s0 = _sigmoid(jnp.concatenate([g_refs[0][...], g_refs[1][...]], axis=1))
    s1 = _sigmoid(jnp.concatenate([g_refs[2][...], g_refs[3][...]], axis=1))
    return s0, s1


def _branch_fwd(os_, lses, yh, proj, w_a, w_h, name, plans=None):
    nb = w_a.shape[0]

    def body(o0, o1, o2, l0, l1, l2, yh_ref, g0a, g0b, g1a, g1b, wa_ref, wh_ref,
             y_ref, lse_ref, za_ref, zh_ref, m_ref):
        a, b, c = l0[...], l1[...], l2[...]
        m = jnp.maximum(jnp.maximum(a, b), c)
        ea, eb, ec = jnp.exp(a - m), jnp.exp(b - m), jnp.exp(c - m)
        den = ea + eb + ec
        y = (ea * o0[...] + eb * o1[...] + ec * o2[...]) / den
        y_ref[...] = y
        lse_ref[...] = m + jnp.log(den)
        yb, yhb = _bf(y), _bf(yh_ref[...])
        za = jnp.concatenate([_dot(yb, wa_ref[j], NN) for j in range(nb)], axis=1)
        zh = jnp.concatenate([_dot(yhb, wh_ref[j], NN) for j in range(nb)], axis=1)
        s0, s1 = _gates((g0a, g0b, g1a, g1b))
        za_ref[...] = za
        zh_ref[...] = zh
        m_ref[...] = _bf(s0 * za + s1 * zh)

    return _rows_call(name, body, yh.shape[0], 512, 1,
                      [*[_tile(t, GROUP_W) for t in (*os_, *lses)], _tile(yh, HG_W), *_gate_tiles(proj),
                       _full(w_a), _full(w_h)],
                      [_out_tile(GROUP_W, F32, GROUP_W)] * 2 + [_out_tile(D_MODEL, F32, D_MODEL)] * 2
                      + [_out_tile(D_MODEL, BF16, D_MODEL)], plans)


def _branch_bwd(dm, za, zh, proj, w_a, w_h, name, plans=None):
    nb, _, Nb = w_a.shape

    def body(dm_ref, za_ref, zh_ref, g0a, g0b, g1a, g1b, wa_ref, wh_ref,
             dza_ref, dzh_ref, dg0_ref, dg1_ref, dy_ref, dyh_ref):
        dmv = dm_ref[...]
        s0, s1 = _gates((g0a, g0b, g1a, g1b))
        dza, dzh = _bf(dmv * s0), _bf(dmv * s1)
        dza_ref[...] = dza
        dzh_ref[...] = dzh
        dg0_ref[...] = _bf(dmv * za_ref[...] * s0 * (1.0 - s0))
        dg1_ref[...] = _bf(dmv * zh_ref[...] * s1 * (1.0 - s1))
        dy_ref[...] = sum(_dot(dza[:, j * Nb:(j + 1) * Nb], wa_ref[j], NT) for j in range(nb))
        dyh_ref[...] = sum(_dot(dzh[:, j * Nb:(j + 1) * Nb], wh_ref[j], NT) for j in range(nb))

    return _rows_call(name, body, za.shape[0], 512, 1,
                      [_tile(dm, D_MODEL), _tile(za, D_MODEL), _tile(zh, D_MODEL), *_gate_tiles(proj),
                       _full(w_a), _full(w_h)],
                      [_out_tile(D_MODEL, BF16, D_MODEL)] * 4 + [_out_tile(GROUP_W, F32, GROUP_W),
                                                                 _out_tile(HG_W, F32, HG_W)], plans)


def _mix_out(merged, w_out, x, w_post, w_pre, name):
    def body(m_ref, wo_ref, x_ref, wp_ref, wf_ref, mo_ref, x1_ref, h2_ref):
        z = _dot(m_ref[...], wo_ref[...], NN)
        mo_ref[...] = z
        x1 = x_ref[...] + z * _rinv(z) * wp_ref[...]
        x1_ref[...] = x1
        h2_ref[...] = _bf(x1 * _rinv(x1) * wf_ref[...])

    return _rows_call(name, body, x.shape[0], 512, 1,
                      [_tile(merged, D_MODEL), _full(w_out), _tile(x, D_MODEL), _full(w_post), _full(w_pre)],
                      [_out_tile(D_MODEL, F32, D_MODEL), _out_tile(D_MODEL, F32, D_MODEL),
                       _out_tile(D_MODEL, BF16, D_MODEL)])


def _loss_head(a, w_down, x1, tgt, w, name):
    def body(a_ref, wd_ref, x1_ref, t_ref, w_ref, dx_ref, df_ref, dw_ref, loss_ref):
        z = _dot(a_ref[...], wd_ref[...], NN)
        r = _rinv(z)
        zhat = z * r
        wv = w_ref[...]
        e = x1_ref[...] + zhat * wv - t_ref[...]
        dx = e * (1.0 / D_MODEL)
        dx_ref[...] = dx
        df_ref[...] = _bf(_norm_bwd(dx, zhat, r, wv))
        _acc(dw_ref, jnp.sum(dx * zhat, axis=0, keepdims=True))
        part = 0.5 * jnp.sum(jnp.sum(e * e, axis=1, keepdims=True), axis=0, keepdims=True) * (1.0 / D_MODEL)
        _acc(loss_ref, jnp.broadcast_to(part, (1, LANES)))

    return _rows_call(name, body, x1.shape[0], 512, 1,
                      [_tile(a, D_FF), _full(w_down), _tile(x1, D_MODEL), _tile(tgt, D_MODEL), _full(w)],
                      [_out_tile(D_MODEL, F32, D_MODEL), _out_tile(D_MODEL, BF16, D_MODEL),
                       _out_acc(1, D_MODEL, D_MODEL), _out_acc(1, LANES, LANES)])


CONV_CB = D_FF // 2
CONV_TM = 512
HALO = 8
SQRT_HALF = 0.7071067811865476
INV_SQRT_2PI = 0.3989422804014327


CONV_RS = 32


def _lane_tiles():
    return [slice(k * LANES, (k + 1) * LANES) for k in range(CONV_CB // LANES)]


def _strip_start(i):
    return pl.multiple_of(i * CONV_RS, CONV_RS)


def _strip_taps(u_ref, halo_ref, r0, cs, first_strip, first_tile):
    if first_strip:
        before = jnp.where(first_tile, 0.0, halo_ref[:, cs])
        blk = jnp.concatenate([before, u_ref[0:CONV_RS, cs]], axis=0)
    else:
        blk = u_ref[pl.ds(pl.multiple_of(r0 - HALO, HALO), CONV_RS + HALO), cs]
    return pltpu.roll(blk, 2, 0)[HALO:], pltpu.roll(blk, 1, 0)[HALO:], blk[HALO:]


def _conv(taps, w_ref, b_ref, cs):
    return b_ref[:, cs] + w_ref[0:1, cs] * taps[0] + w_ref[1:2, cs] * taps[1] + w_ref[2:3, cs] * taps[2]


def _conv_specs(tm):
    nh = tm // HALO
    nc = D_FF // CONV_CB

    def tile(off):
        return pl.BlockSpec((tm, CONV_CB), lambda c, i: (i, off + c))

    def halo(off):
        return pl.BlockSpec((HALO, CONV_CB), lambda c, i: (jnp.maximum(i * nh - 1, 0), off + c))

    def small(rows, off):
        return pl.BlockSpec((rows, CONV_CB), lambda c, i: (0, off + c))

    return nc, tile, halo, small


def _conv_gelu_fwd(u, cw, cb, name, plans=None):
    S = u.shape[0]
    tm = CONV_TM
    nc, tile, halo, small = _conv_specs(tm)

    def body(ug, hg, uv, hv, wg, wv, bg, bv, a_ref):
        first_tile = pl.program_id(1) == 0

        def strip(r0, first_strip):
            for cs in _lane_tiles():
                cg = _conv(_strip_taps(ug, hg, r0, cs, first_strip, first_tile), wg, bg, cs)
                cv = _conv(_strip_taps(uv, hv, r0, cs, first_strip, first_tile), wv, bv, cs)
                a_ref[pl.ds(r0, CONV_RS), cs] = _bf(0.5 * cg * (1.0 + lax.erf(cg * SQRT_HALF)) * cv)

        strip(0, True)
        lax.fori_loop(1, tm // CONV_RS, lambda k, c: (strip(_strip_start(k), False), c)[1], 0)

    (a,), carried = _call(
        body, plans, name=name, grid=(nc, S // tm),
        in_specs=[tile(0), halo(0), tile(nc), halo(nc), small(3, 0), small(3, nc), small(1, 0), small(1, nc)],
        out_specs=[tile(0)], out_shape=[_sds((S, D_FF), BF16)], args=(u, u, u, u, cw, cw, cb, cb))
    return a if plans is None else (a, carried)


def _conv_gelu_bwd(u, dff, w_down, cw, cb, name, plans=None):
    S = u.shape[0]
    tm = CONV_TM
    nt = S // tm
    nc, tile, halo, small = _conv_specs(tm)

    def body(ug, hg, uv, hv, wg, wv, bg, bv, dff_ref, wd_ref, dcg_ref, dcv_ref, dwg_ref, dwv_ref, dbg_ref, dbv_ref,
             acc, da_ref):
        i = pl.program_id(1)
        first_tile = i == 0
        da_ref[...] = _dot(dff_ref[...], wd_ref[...], NT)

        @pl.when(first_tile)
        def _():
            acc[...] = jnp.zeros_like(acc)

        def strip(r0, first_strip):
            rows = pl.ds(r0, CONV_RS)
            for cs in _lane_tiles():
                tg = _strip_taps(ug, hg, r0, cs, first_strip, first_tile)
                tv = _strip_taps(uv, hv, r0, cs, first_strip, first_tile)
                cg = _conv(tg, wg, bg, cs)
                cv = _conv(tv, wv, bv, cs)
                phi = 0.5 * (1.0 + lax.erf(cg * SQRT_HALF))
                dav = da_ref[rows, cs]
                dcg = dav * cv * (phi + cg * jnp.exp(-0.5 * cg * cg) * INV_SQRT_2PI)
                dcv = dav * (cg * phi)
                dcg_ref[rows, cs] = dcg
                dcv_ref[rows, cs] = dcv
                for half, (dc, taps) in enumerate(((dcg, tg), (dcv, tv))):
                    for j in range(3):
                        acc[4 * half + j, :, cs] += dc * taps[j]
                    acc[4 * half + 3, :, cs] += dc

        strip(0, True)
        lax.fori_loop(1, tm // CONV_RS, lambda k, c: (strip(_strip_start(k), False), c)[1], 0)

        @pl.when(i == nt - 1)
        def _():
            for half, (dw_ref, db_ref) in enumerate(((dwg_ref, dbg_ref), (dwv_ref, dbv_ref))):
                for j in range(3):
                    dw_ref[j:j + 1, :] = jnp.sum(acc[4 * half + j], axis=0, keepdims=True)
                db_ref[...] = jnp.sum(acc[4 * half + 3], axis=0, keepdims=True)

    res, carried = _call(
        body, plans, name=name, grid=(nc, nt),
        in_specs=[tile(0), halo(0), tile(nc), halo(nc), small(3, 0), small(3, nc), small(1, 0), small(1, nc),
                  pl.BlockSpec((tm, D_MODEL), lambda c, i: (i, 0)), pl.BlockSpec((CONV_CB, D_MODEL), lambda c, i: (c, 0))],
        out_specs=[tile(0), tile(0), small(3, 0), small(3, 0), small(1, 0), small(1, 0)],
        out_shape=[_sds((S, D_FF), F32)] * 2 + [_sds((3, D_FF), F32)] * 2 + [_sds((1, D_FF), F32)] * 2,
        scratch_shapes=[pltpu.VMEM((8, CONV_RS, CONV_CB), F32), pltpu.VMEM((tm, CONV_CB), F32)],
        args=(u, u, u, u, cw, cw, cb, cb, dff, w_down))
    return res if plans is None else (res, carried)


def _conv_input_bwd(dcg, dcv, cw, name, plans=None):
    S = dcg.shape[0]
    tm = CONV_TM // 2
    nh = tm // HALO
    nt = S // tm
    n = CONV_RS + HALO
    tile = pl.BlockSpec((tm, D_FF), lambda i: (i, 0))
    nxt = pl.BlockSpec((HALO, D_FF), lambda i: (jnp.minimum((i + 1) * nh, S // HALO - 1), 0))

    def body(g_ref, ng_ref, v_ref, nv_ref, w_ref, du_ref):
        last_tile = pl.program_id(0) == nt - 1

        def strip(r0, last_strip):
            for half, (dc_ref, n_ref) in enumerate(((g_ref, ng_ref), (v_ref, nv_ref))):
                for k in range(D_FF // LANES):
                    cs = slice(k * LANES, (k + 1) * LANES)
                    ws = slice(half * D_FF + k * LANES, half * D_FF + (k + 1) * LANES)
                    if last_strip:
                        after = jnp.where(last_tile, 0.0, n_ref[:, cs])
                        blk = jnp.concatenate([dc_ref[tm - CONV_RS:tm, cs], after], axis=0)
                    else:
                        blk = dc_ref[pl.ds(r0, n), cs]
                    d1 = pltpu.roll(blk, n - 1, 0)[:CONV_RS]
                    d2 = pltpu.roll(blk, n - 2, 0)[:CONV_RS]
                    du_ref[pl.ds(r0, CONV_RS), ws] = _bf(w_ref[2:3, ws] * blk[:CONV_RS] + w_ref[1:2, ws] * d1
                                                         + w_ref[0:1, ws] * d2)

        lax.fori_loop(0, tm // CONV_RS - 1, lambda k, c: (strip(_strip_start(k), False), c)[1], 0)
        strip(tm - CONV_RS, True)

    (du,), carried = _call(
        body, plans, name=name, grid=(nt,),
        in_specs=[tile, nxt, tile, nxt, pl.BlockSpec((3, 2 * D_FF), lambda i: (0, 0))],
        out_specs=[pl.BlockSpec((tm, 2 * D_FF), lambda i: (i, 0))], out_shape=[_sds((S, 2 * D_FF), BF16)],
        args=(dcg, dcg, dcv, dcv, cw))
    return du if plans is None else (du, carried)


def _row_tile(n, cap):
    best = n
    for t in range(16, cap + 1, 16):
        if n % t == 0:
            best = t
    return best if best <= cap else n


def _rows_for_bytes(nbytes, cols):
    return max(16, nbytes // (4 * cols) // 16 * 16)


def _adamw(w, g, m, v, name):
    R, C = w.shape
    tr = _row_tile(R, _rows_for_bytes(2 << 20, C))

    def body(w_ref, g_ref, m_ref, v_ref, d_ref, nm_ref, nv_ref):
        gv = g_ref[...]
        nm = ADAM_B1 * m_ref[...] + (1.0 - ADAM_B1) * gv
        nv = ADAM_B2 * v_ref[...] + (1.0 - ADAM_B2) * (gv * gv)
        m_hat = nm / (1.0 - ADAM_B1 ** ADAM_STEP)
        v_hat = nv / (1.0 - ADAM_B2 ** ADAM_STEP)
        d_ref[...] = -ADAM_LR * (m_hat / (jnp.sqrt(v_hat) + ADAM_EPS) + ADAM_WD * w_ref[...])
        nm_ref[...] = nm
        nv_ref[...] = nv

    spec = pl.BlockSpec((tr, C), lambda i: (i, 0))
    return pl.pallas_call(body, name=name, grid=(R // tr,), in_specs=[spec] * 4, out_specs=[spec] * 3,
                          out_shape=[_sds((R, C), F32)] * 3, compiler_params=_cp(1))(w, g, m, v)


def _pair_sum(gfull, rcv, c_idx, name):
    nb, R, C = gfull.shape
    half = R // 2
    tr = _row_tile(half, _rows_for_bytes(2 << 20, C))
    nt = half // tr

    def body(c_ref, g_ref, r_ref, o_ref):
        o_ref[...] = _bf(g_ref[...] + r_ref[...])

    return pl.pallas_call(
        body, name=name,
        grid_spec=pltpu.PrefetchScalarGridSpec(
            num_scalar_prefetch=1, grid=(nb, nt),
            in_specs=[pl.BlockSpec((None, tr, C), lambda j, i, c_ref: (j, c_ref[0] * nt + i, 0)),
                      pl.BlockSpec((None, tr, C), lambda j, i, c_ref: (j, i, 0))],
            out_specs=pl.BlockSpec((None, tr, C), lambda j, i, c_ref: (j, i, 0))),
        out_shape=_sds((nb, half, C), BF16), compiler_params=_cp(2))(c_idx, gfull, rcv)


def _chip_sum(arrived, own, place, name):
    nb, H, C = arrived.shape
    tr = _row_tile(H, _rows_for_bytes(2 << 20, C))
    nt = H // tr

    def body(pl_ref, *refs):
        o_ref = refs[nb + 1]
        me = pl_ref[0]
        acc = None
        for k in range(nb):
            term = jnp.where(me == k, refs[nb][...], refs[k][...]).astype(F32)
            acc = term if acc is None else acc + term
        o_ref[...] = acc

    def other(k):
        return pl.BlockSpec((None, tr, C), lambda i, p: (jnp.where(p[0] == k, (k + 1) % nb, k), i, 0))

    return pl.pallas_call(
        body, name=name,
        grid_spec=pltpu.PrefetchScalarGridSpec(
            num_scalar_prefetch=1, grid=(nt,),
            in_specs=[other(k) for k in range(nb)] + [pl.BlockSpec((None, tr, C), lambda i, p: (p[0], i, 0))],
            out_specs=pl.BlockSpec((tr, C), lambda i, p: (p[1] * nt + i, 0))),
        out_shape=_sds((2 * H, C), F32), compiler_params=_cp(1))(place, *([arrived] * nb), own)


def _cast_into_slot(shard, place, name):
    R, C = shard.shape
    tr = _row_tile(R, 256)

    def body(pl_ref, s_ref, o_ref):
        o_ref[...] = _bf(s_ref[...])

    return pl.pallas_call(
        body, name=name,
        grid_spec=pltpu.PrefetchScalarGridSpec(
            num_scalar_prefetch=1, grid=(R // tr,),
            in_specs=[pl.BlockSpec((tr, C), lambda i, p: (i, 0))],
            out_specs=pl.BlockSpec((None, tr, C), lambda i, p: (p[0], i, 0))),
        out_shape=_sds((N_CHIPS, R, C), BF16), compiler_params=_cp(1))(place, shard)


def _place():
    x, y, c = lax.axis_index("x"), lax.axis_index("y"), lax.axis_index("c")
    chips = [(1 - x, y), (x, 1 - y), (1 - x, 1 - y)]
    return x, y, c, chips


def _chip_id(px, py):
    return 2 * px + py


def _remote(src, dst, send_sems, recv_sems, k, to):
    return pltpu.make_async_remote_copy(src_ref=src, dst_ref=dst, send_sem=send_sems.at[k], recv_sem=recv_sems.at[k],
                                        device_id=to, device_id_type=MESH)


def _proj_gathered(x, w_norm, slot, place, name, tm=1024, plan=None):
    M, K = x.shape
    nb, _, Nb = slot.shape
    half = K // 2
    nt = M // tm
    cx, cy = place[0] // 2, place[0] % 2
    order = jnp.stack([place[0], _chip_id(1 - cx, cy), _chip_id(cx, 1 - cy), _chip_id(1 - cx, 1 - cy)]).astype(jnp.int32)

    p_in = [] if plan is None else plan.ins + plan.inouts
    p_out = [] if plan is None else [_sds(a.shape, a.dtype) for a in plan.inouts] + plan.outs
    n_pi, n_po = len(p_in), len(p_out)

    def body(order_ref, x_ref, wn_ref, slot_in, *refs):
        o_ref, slot_ref, h_out = refs[n_pi:n_pi + 3]
        s0 = n_pi + 3 + n_po
        w_buf, hs, ici_send, ici_recv, pass_send, pass_recv, load_sem = refs[s0:s0 + 7]

        def carried():
            if plan is None:
                return [], [], []
            ins = refs[:len(plan.ins)]
            outs = refs[n_pi + 3:n_pi + 3 + n_po]
            return plan.copies(ins, outs[:len(plan.inouts)], outs[len(plan.inouts):], *refs[s0 + 7:])

        b, i = pl.program_id(0), pl.program_id(1)
        x, y, c, chips = _place()
        me = _chip_id(x, y)
        sib = (x, y, 1 - c)
        mine, other = pl.ds(c * half, half), pl.ds((1 - c) * half, half)

        def sent(k):
            blk = slot_ref.at[me, mine]
            return _remote(blk, blk, ici_send, ici_recv, k, (*chips[k], c))

        def landed(k):
            blk = slot_ref.at[_chip_id(*chips[k]), mine]
            return _remote(blk, blk, ici_send, ici_recv, k, (*chips[k], c))

        def passed(k, rows):
            blk = slot_ref.at[_chip_id(*chips[k]), rows]
            return _remote(blk, blk, pass_send, pass_recv, k, sib)

        @pl.when((b == 0) & (i == 0))
        def _():
            for k in range(len(chips)):
                sent(k).start()
            sends, _, local = carried()
            for cp in (*sends, *local):
                cp.start()

        for k in range(len(chips)):
            @pl.when((b == k + 1) & (i == 0))
            def _(k=k):
                landed(k).wait_recv()
                passed(k, mine).start()
                passed(k, other).wait_recv()

        @pl.when(i == 0)
        def _():
            load = pltpu.make_async_copy(slot_ref.at[order_ref[b]], w_buf, load_sem.at[0])
            load.start()
            load.wait()

        rows = pl.ds(pl.multiple_of(i * tm, tm), tm)
        keep_h = pltpu.make_async_copy(hs, h_out, load_sem.at[1])

        @pl.when(b == 0)
        def _():
            xv = x_ref[...]
            hs[rows, :] = _bf(xv * _rinv(xv) * wn_ref[...])

        @pl.when((b == 1) & (i == 0))
        def _():
            keep_h.start()

        o_ref[...] = _dot(hs[rows, :], w_buf[...], NN)

        @pl.when((b == nb - 1) & (i == nt - 1))
        def _():
            for k in range(len(chips)):
                sent(k).wait_send()
                passed(k, mine).wait_send()
            sends, recvs, local = carried()
            for cp in recvs:
                cp.wait_recv()
            for cp in sends:
                cp.wait_send()
            for cp in local:
                cp.wait()
            keep_h.wait()

    n_peers = N_CHIPS - 1
    return pl.pallas_call(
        body, name=name,
        grid_spec=pltpu.PrefetchScalarGridSpec(
            num_scalar_prefetch=1, grid=(nb, nt),
            in_specs=[pl.BlockSpec((tm, K), lambda b, i, o: (i, 0)), pl.BlockSpec((1, K), lambda b, i, o: (0, 0)), ANY]
            + [ANY] * n_pi,
            out_specs=[pl.BlockSpec((tm, Nb), lambda b, i, o: (i, o[b])), ANY, ANY] + [ANY] * n_po,
            scratch_shapes=[pltpu.VMEM((K, Nb), BF16), pltpu.VMEM((M, K), BF16)]
            + [pltpu.SemaphoreType.DMA((n_peers,))] * 4 + [pltpu.SemaphoreType.DMA((2,))]
            + ([] if plan is None else [pltpu.SemaphoreType.DMA((plan.n_sems,))] * 3)),
        out_shape=[_sds((M, nb * Nb), F32), _sds(slot.shape, slot.dtype), _sds((M, K), BF16)] + p_out,
        input_output_aliases={3: 1, **({} if plan is None else
                                       {4 + len(plan.ins) + a: 3 + a for a in range(len(plan.inouts))})},
        compiler_params=_cp(2))(order, x, w_norm, slot, *p_in)


def _gather_ici_plan(slots, wholes, part=None):
    ns, nw = len(slots), len(wholes)

    def copies(ins, ios, outs, send_sems, recv_sems, local_sems):
        x, y, c, chips = _place()
        me = _chip_id(x, y)
        sends, recvs = [], []
        for a in range(ns + nw):
            dst = ios[a] if a < ns else outs[a - ns]
            R = dst.shape[1]
            r0, nr = (0, R // 2) if part is None else part
            rows = pl.ds(c * (R // 2) + r0, nr) if a < ns else pl.ds(0, R)
            src = dst.at[me, rows] if a < ns else ins[a - ns]
            for j, chip in enumerate(chips):
                sends.append(_remote(src, dst.at[me, rows], send_sems, recv_sems, 3 * a + j, (*chip, c)))
                landed = dst.at[_chip_id(*chip), rows]
                recvs.append(_remote(landed, landed, send_sems, recv_sems, 3 * a + j, (*chip, c)))
        local = [pltpu.make_async_copy(ins[b], outs[b].at[me], local_sems.at[b]) for b in range(nw)]
        return sends, recvs, local

    return _Plan(copies, 3 * (ns + nw), ins=wholes, inouts=slots,
                 outs=[_sds((N_CHIPS, *s.shape), s.dtype) for s in wholes])


def _gather_pass_plan(slots):
    def copies(ins, ios, outs, send_sems, recv_sems, local_sems):
        x, y, c, chips = _place()
        sib = (x, y, 1 - c)
        sends, recvs = [], []
        for a, buf in enumerate(ios):
            half = buf.shape[1] // 2
            for j, chip in enumerate(chips):
                mine = buf.at[_chip_id(*chip), pl.ds(c * half, half)]
                other = buf.at[_chip_id(*chip), pl.ds((1 - c) * half, half)]
                sends.append(_remote(mine, mine, send_sems, recv_sems, 3 * a + j, sib))
                recvs.append(_remote(other, other, send_sems, recv_sems, 3 * a + j, sib))
        return sends, recvs, []

    return _Plan(copies, 3 * len(slots), inouts=slots)


def _pair_plan(grads):
    def copies(ins, ios, outs, send_sems, recv_sems, local_sems):
        x, y, c, _ = _place()
        sib = (x, y, 1 - c)
        sends, recvs = [], []
        for a, g in enumerate(ins):
            half = g.shape[1] // 2
            sends.append(_remote(g.at[:, pl.ds((1 - c) * half, half), :], outs[a], send_sems, recv_sems, a, sib))
            recvs.append(_remote(outs[a], outs[a], send_sems, recv_sems, a, sib))
        return sends, recvs, []

    return _Plan(copies, len(grads), ins=grads,
                 outs=[_sds((g.shape[0], g.shape[1] // 2, g.shape[2]), g.dtype) for g in grads])


def _chip_plan(parts):
    def copies(ins, ios, outs, send_sems, recv_sems, local_sems):
        x, y, c, chips = _place()
        me = _chip_id(x, y)
        sends, recvs = [], []
        for a, part in enumerate(ins):
            for j, chip in enumerate(chips):
                sends.append(_remote(part.at[_chip_id(*chip)], outs[a].at[me], send_sems, recv_sems, 3 * a + j, (*chip, c)))
                landed = outs[a].at[_chip_id(*chip)]
                recvs.append(_remote(landed, landed, send_sems, recv_sems, 3 * a + j, (*chip, c)))
        return sends, recvs, []

    return _Plan(copies, 3 * len(parts), ins=parts, outs=[_sds(p.shape, p.dtype) for p in parts])


def _all_sum(pack, fulls, name):
    R, C = pack.shape
    n = len(fulls)

    def body(p_ref, *refs):
        o_ref, halves = refs[n], refs[n + 1:2 * n + 1]
        buf, send_sems, recv_sems, pair_send, pair_recv = refs[2 * n + 1:]
        x, y, c, _ = _place()
        sib = (x, y, 1 - c)
        pair = []
        for a, full in enumerate(halves):
            H = full.shape[0] // 2
            mine = full.at[pl.ds(c * H, H)]
            cp = _remote(mine, mine, pair_send, pair_recv, a, sib)
            cp.start()
            pair.append(cp)
        me = 4 * x + 2 * y + c
        buf[me] = p_ref[...]
        cps = []
        for k in range(1, N_DEV):
            to = (x ^ (k >> 2), y ^ ((k >> 1) & 1), c ^ (k & 1))
            cp = _remote(p_ref, buf.at[me], send_sems, recv_sems, k - 1, to)
            cp.start()
            cps.append(cp)
        for k in range(1, N_DEV):
            frm = (x ^ (k >> 2), y ^ ((k >> 1) & 1), c ^ (k & 1))
            slot = buf.at[4 * frm[0] + 2 * frm[1] + frm[2]]
            _remote(slot, slot, send_sems, recv_sems, k - 1, frm).wait_recv()
        acc = buf[0]
        for k in range(1, N_DEV):
            acc = acc + buf[k]
        o_ref[...] = acc
        for cp in cps:
            cp.wait_send()
        for a, (full, cp) in enumerate(zip(halves, pair)):
            H = full.shape[0] // 2
            other = full.at[pl.ds((1 - c) * H, H)]
            _remote(other, other, pair_send, pair_recv, a, sib).wait_recv()
            cp.wait_send()

    vm = pl.BlockSpec(memory_space=pltpu.VMEM)
    res = pl.pallas_call(
        body, name=name, in_specs=[vm] + [ANY] * n, out_specs=[vm] + [ANY] * n,
        out_shape=[_sds((R, C), F32)] + [_sds(f.shape, f.dtype) for f in fulls],
        input_output_aliases={1 + a: 1 + a for a in range(n)},
        scratch_shapes=[pltpu.VMEM((N_DEV, R, C), F32), pltpu.SemaphoreType.DMA((N_DEV - 1,)),
                        pltpu.SemaphoreType.DMA((N_DEV - 1,)), pltpu.SemaphoreType.DMA((n,)),
                        pltpu.SemaphoreType.DMA((n,))])(pack, *fulls)
    return res[0], list(res[1:])


def _local_step(xs, tgt, p, ex):
    proj, h1 = ex.project(xs, p["pre_mix_norm"])
    biases = _relbias_fwd(p["rel_bias"], "rel_bias_fwd")
    fw = []
    for g in range(N_GROUPS):
        res, got = _attn_fwd(proj, biases[g], g, f"attn_fwd{g}", plans=ex.carry(f"attn_fwd{g}"))
        ex.done(f"attn_fwd{g}", got)
        fw.append(res)
    (yh, o_h, states), got = _hgrn_fwd(proj, p["hgrn_lb_raw"], p["hgrn_norm"], "hgrn_fwd", plans=ex.carry("hgrn_fwd"))
    ex.done("hgrn_fwd", got)
    W_a, W_h, W_out = ex.weight("w_branch_attn"), ex.weight("w_branch_hgrn"), ex.weight("w_out")
    (y, lse, za, zh, merged), got = _branch_fwd([t[0] for t in fw], [t[1] for t in fw], yh, proj, W_a, W_h,
                                                "branch_fwd", plans=ex.carry("branch_fwd"))
    ex.done("branch_fwd", got)
    W_up, conv_w = ex.weight("w_up"), ex.weight("conv_w")
    mo, x1, h2 = _mix_out(merged, W_out, xs, p["post_mix_norm"], p["pre_ffn_norm"], "mix_out")
    u, got = _mm_nn_blk(h2, W_up, "ffn_up", tm=1024, plans=ex.carry("ffn_up"))
    ex.done("ffn_up", got)
    a, got = _conv_gelu_fwd(u, conv_w, p["conv_b"], "conv_gelu_fwd", plans=ex.carry("conv_gelu_fwd"))
    ex.done("conv_gelu_fwd", got)
    W_down = ex.weight("w_down")
    dx2, dff, g_post_ffn, loss = _loss_head(a, W_down, x1, tgt, p["post_ffn_norm"], "ffn_down_loss")

    ex.grad("w_down", _mm_tn(a, dff, "g_w_down").reshape(N_CHIPS, D_FF // N_CHIPS, D_MODEL))
    (dcg, dcv, gwg, gwv, gbg, gbv), got = _conv_gelu_bwd(u, dff, W_down, conv_w, p["conv_b"], "conv_gelu_bwd",
                                                          plans=ex.carry("conv_gelu_bwd"))
    ex.done("conv_gelu_bwd", got)
    g_conv_w = jnp.concatenate([gwg, gwv], axis=1)
    g_conv_b = jnp.concatenate([gbg, gbv], axis=1)
    du, got = _conv_input_bwd(dcg, dcv, conv_w, "conv_input_bwd", plans=ex.carry("conv_input_bwd"))
    ex.done("conv_input_bwd", got)
    dx1, g_pre_ffn = _mm_nt_prenorm_bwd(du, W_up, x1, p["pre_ffn_norm"], dx2, "d_ffn_in")
    ex.grad("w_up", _mm_tn_blk(h2, du, N_CHIPS, "g_w_up"))
    (dmo, dmerged, g_post_mix), got = _postnorm_bwd(dx1, mo, p["post_mix_norm"], W_out, "post_mix_norm_bwd",
                                                    plans=ex.carry("post_mix_norm_bwd"))
    ex.done("post_mix_norm_bwd", got)
    ex.grad("w_out", _mm_tn(merged, dmo, "g_w_out").reshape(N_CHIPS, D_MODEL // N_CHIPS, D_MODEL))
    (dza, dzh, dg0, dg1, dy, dyh), got = _branch_bwd(dmerged, za, zh, proj, W_a, W_h, "branch_bwd",
                                                     plans=ex.carry("branch_bwd"))
    ex.done("branch_bwd", got)
    ex.grad("w_branch_attn", _mm_tn_blk(y, dza, N_CHIPS, "g_w_branch_attn", together=True))
    ex.grad("w_branch_hgrn", _mm_tn_blk(yh, dzh, N_CHIPS, "g_w_branch_hgrn", together=True))
    dqkv, dbs = [], []
    for g in range(N_GROUPS):
        parts, db, got = _attn_bwd(proj, biases[g], lse, y, dy, g, f"attn_bwd{g}", plans=ex.carry(f"attn_bwd{g}"))
        ex.done(f"attn_bwd{g}", got)
        dqkv += parts
        dbs.append(db)
    g_rel_bias = _relbias_bwd(dbs, "rel_bias_bwd")
    dproj, g_lb_raw, g_hgrn_norm = _hgrn_bwd(proj, p["hgrn_lb_raw"], p["hgrn_norm"], o_h, states, dyh, dqkv,
                                             [dg0, dg1], "hgrn_bwd")
    for piece in W_IN_PIECES:
        g, got = _mm_tn_blk(h1, dproj, N_CHIPS, f"g_{piece}", x_cols=W_IN_ROWS[piece],
                            plans=ex.carry(f"g_{piece}"))
        ex.done(f"g_{piece}", got)
        ex.grad(piece, g)
    dh1, got = _mm_nt_blk(dproj, ex.weight("w_in"), "d_proj_in", plans=ex.carry("d_proj_in"))
    ex.done("d_proj_in", got)
    (grad_x, g_pre_mix), got = _prenorm_bwd(dh1, xs, p["pre_mix_norm"], dx1, "pre_mix_norm_bwd",
                                            plans=ex.carry("pre_mix_norm_bwd"))
    ex.done("pre_mix_norm_bwd", got)
    small = dict(pre_mix_norm=g_pre_mix, rel_bias=g_rel_bias, hgrn_lb_raw=g_lb_raw, hgrn_norm=g_hgrn_norm,
                 post_mix_norm=g_post_mix, pre_ffn_norm=g_pre_ffn, conv_w=g_conv_w, conv_b=g_conv_b,
                 post_ffn_norm=g_post_ffn)
    return loss, grad_x, small


SMALL = ("pre_mix_norm", "rel_bias", "hgrn_lb_raw", "hgrn_norm", "post_mix_norm", "pre_ffn_norm", "conv_w", "conv_b",
         "post_ffn_norm")
BIG = ("w_in", "w_up", "w_down", "w_out", "w_branch_attn", "w_branch_hgrn")
WEIGHTS = ("pre_mix_norm", "w_in", "rel_bias", "hgrn_lb_raw", "hgrn_norm", "w_branch_attn", "w_branch_hgrn", "w_out",
           "post_mix_norm", "pre_ffn_norm", "w_up", "conv_w", "conv_b", "w_down", "post_ffn_norm")
MIXER = ("w_out", "w_branch_attn", "w_branch_hgrn")

ICI_PARTS = {"gather_ici_1of3": (0, 176), "gather_ici_2of3": (176, 176), "gather_ici_3of3": (352, 160)}
SCHEDULE = {
    "proj_in": [("gather_ici_cw", MIXER)],
    "attn_fwd0": [("gather_pass", MIXER), ("gather_ici_1of3", ("w_up",))],
    "attn_fwd1": [("gather_ici_2of3", ("w_up",))],
    "attn_fwd2": [("gather_ici_3of3", ("w_up",))],
    "hgrn_fwd": [("gather_pass", ("w_up",))],
    "ffn_up": [("gather_ici", ("w_down",))],
    "conv_gelu_fwd": [("gather_pass", ("w_down",))],
    "conv_gelu_bwd": [("pair", ("w_down",))],
    "conv_input_bwd": [("chip", ("w_down",))],
    "post_mix_norm_bwd": [("pair", ("w_up",))],
    "attn_bwd0": [("chip", ("w_up",)), ("pair", MIXER)],
    "attn_bwd1": [("chip", MIXER)],
    "g_w_in_b": [("pair", ("w_in_a",))],
    "d_proj_in": [("chip", ("w_in_a",)), ("pair", ("w_in_b",))],
    "pre_mix_norm_bwd": [("chip", ("w_in_b",))],
}
W_IN_ROWS = dict(w_in_a=(0, 768), w_in_b=(3, 256))
W_IN_PIECES = tuple(W_IN_ROWS)
REDUCED = W_IN_PIECES + BIG[1:]


class _Exchange:
    def __init__(self, place, slots, conv_w_shard):
        self.place, self.slots, self.conv_w_shard = place, dict(slots), conv_w_shard
        self.conv_w = None
        self.g, self.from_sibling, self.pair_sums, self.arrived = {}, {}, {}, {}
        self.pending = []

    def weight(self, name):
        if name == "conv_w":
            return self.conv_w
        w = self.slots[name]
        return w.reshape(-1, D_MODEL) if name in ("w_out", "w_down") else w

    def project(self, x, w_norm):
        (plan,) = self.carry("proj_in")
        proj, self.slots["w_in"], h, *got = _proj_gathered(x, w_norm, self.slots["w_in"], self.place, "proj_in",
                                                           plan=plan)
        self.done("proj_in", [got])
        return proj, h

    def grad(self, name, g):
        self.g[name] = g

    def carry(self, point):
        plans = []
        self.pending = SCHEDULE.get(point, [])
        for kind, names in self.pending:
            if kind in ("gather_ici", "gather_ici_cw") or kind in ICI_PARTS:
                wholes = [self.conv_w_shard] if kind == "gather_ici_cw" else []
                plans.append(_gather_ici_plan([self.slots[n] for n in names], wholes, ICI_PARTS.get(kind)))
            elif kind == "gather_pass":
                plans.append(_gather_pass_plan([self.slots[n] for n in names]))
            elif kind == "pair":
                plans.append(_pair_plan([self.g[n] for n in names]))
            else:
                for n in names:
                    self.pair_sums[n] = _pair_sum(self.g[n], self.from_sibling[n], self.place[1:2], f"pair_sum_{n}")
                plans.append(_chip_plan([self.pair_sums[n] for n in names]))
        return plans

    def done(self, point, carried):
        for (kind, names), got in zip(self.pending, carried):
            if kind in ("gather_ici", "gather_ici_cw", "gather_pass") or kind in ICI_PARTS:
                self.slots.update(zip(names, got))
                if kind == "gather_ici_cw":
                    self.conv_w = got[len(names)].transpose(1, 0, 2).reshape(3, 2 * D_FF)
            elif kind == "pair":
                self.from_sibling.update(zip(names, got))
            else:
                self.arrived.update(zip(names, got))

    def reduced_halves(self):
        return [_chip_sum(self.arrived[n], self.pair_sums[n], self.place, f"chip_sum_{n}") for n in REDUCED]


def kernel(x, pre_mix_norm, w_in, rel_bias, hgrn_lb_raw, hgrn_norm, w_branch_attn, w_branch_hgrn, w_out, post_mix_norm, pre_ffn_norm, w_up, conv_w, conv_b, w_down, post_ffn_norm, loss_target, m_pre_mix_norm, m_w_in, m_rel_bias, m_hgrn_lb_raw, m_hgrn_norm, m_w_branch_attn, m_w_branch_hgrn, m_w_out, m_post_mix_norm, m_pre_ffn_norm, m_w_up, m_conv_w, m_conv_b, m_w_down, m_post_ffn_norm, v_pre_mix_norm, v_w_in, v_rel_bias, v_hgrn_lb_raw, v_hgrn_norm, v_w_branch_attn, v_w_branch_hgrn, v_w_out, v_post_mix_norm, v_pre_ffn_norm, v_w_up, v_conv_w, v_conv_b, v_w_down, v_post_ffn_norm):
    w = dict(pre_mix_norm=pre_mix_norm, w_in=w_in, rel_bias=rel_bias, hgrn_lb_raw=hgrn_lb_raw, hgrn_norm=hgrn_norm,
             w_branch_attn=w_branch_attn, w_branch_hgrn=w_branch_hgrn, w_out=w_out, post_mix_norm=post_mix_norm,
             pre_ffn_norm=pre_ffn_norm, w_up=w_up, conv_w=conv_w, conv_b=conv_b, w_down=w_down,
             post_ffn_norm=post_ffn_norm)
    m = dict(pre_mix_norm=m_pre_mix_norm, w_in=m_w_in, rel_bias=m_rel_bias, hgrn_lb_raw=m_hgrn_lb_raw,
             hgrn_norm=m_hgrn_norm, w_branch_attn=m_w_branch_attn, w_branch_hgrn=m_w_branch_hgrn, w_out=m_w_out,
             post_mix_norm=m_post_mix_norm, pre_ffn_norm=m_pre_ffn_norm, w_up=m_w_up, conv_w=m_conv_w,
             conv_b=m_conv_b, w_down=m_w_down, post_ffn_norm=m_post_ffn_norm)
    v = dict(pre_mix_norm=v_pre_mix_norm, w_in=v_w_in, rel_bias=v_rel_bias, hgrn_lb_raw=v_hgrn_lb_raw,
             hgrn_norm=v_hgrn_norm, w_branch_attn=v_w_branch_attn, w_branch_hgrn=v_w_branch_hgrn, w_out=v_w_out,
             post_mix_norm=v_post_mix_norm, pre_ffn_norm=v_pre_ffn_norm, w_up=v_w_up, conv_w=v_conv_w,
             conv_b=v_conv_b, w_down=v_w_down, post_ffn_norm=v_post_ffn_norm)
    shard2d = {n: (w[n][0] if w[n].ndim == 3 else w[n]) for n in WEIGHTS}
    chip = 2 * lax.axis_index("x") + lax.axis_index("y")
    core = lax.axis_index("c")

    place = jnp.stack([chip, core]).astype(jnp.int32)
    slots = {n: _cast_into_slot(shard2d[n], place, f"cast_{n}") for n in BIG}
    ex = _Exchange(place, slots, shard2d["conv_w"])
    loss, grad_x, small = _local_step(x[0], loss_target[0], {n: w[n] for n in SMALL if n != "conv_w"}, ex)

    flat = [small[n].reshape(-1) for n in SMALL] + [loss.reshape(-1)]
    sizes = [t.shape[0] for t in flat]
    summed, wholes = _all_sum(jnp.concatenate(flat).reshape(-1, LANES), ex.reduced_halves(), "sum_small")
    summed = summed.reshape(-1)
    offs = [sum(sizes[:i]) for i in range(len(sizes))]
    grads = {}
    for n, o, sz in zip(SMALL, offs, sizes):
        grads[n] = summed[o:o + sz].reshape(small[n].shape)
    loss_total = summed[offs[-1]]
    cw = 2 * D_FF // N_CHIPS
    grads["conv_w"] = lax.dynamic_slice(grads["conv_w"], (0, chip * cw), (3, cw))

    big = dict(zip(REDUCED, wholes))
    big["w_in"] = jnp.concatenate([big.pop(n) for n in W_IN_PIECES], axis=0)
    grads.update(big)

    out_g, out_d, out_m, out_v = [], [], [], []
    for n in WEIGHTS:
        d2, m2, v2 = _adamw(shard2d[n], grads[n], m[n].reshape(shard2d[n].shape), v[n].reshape(shard2d[n].shape),
                            f"adamw_{n}")
        shape = w[n].shape
        out_g.append(grads[n].reshape(shape))
        out_d.append(d2.reshape(shape))
        out_m.append(m2.reshape(shape))
        out_v.append(v2.reshape(shape))
    return (loss_total, grad_x[None], *out_g, *out_d, *out_m, *out_v)
```

```python
import functools
import math

import jax
import jax.numpy as jnp
from jax import lax
from jax.experimental import pallas as pl
from jax.experimental.pallas import tpu as pltpu

F32 = jnp.float32
BF16 = jnp.bfloat16
MESH = pl.DeviceIdType.MESH

D_MODEL = 1024
N_GROUPS = 3
DILATIONS = (1, 4, 16)
HEADS = 8
HEAD_DIM = 64
GROUP_W = HEADS * HEAD_DIM
QKV_W = N_GROUPS * 3 * GROUP_W
BLK = 128
NEG_INF = -1e30
NUM_BUCKETS = 32
MAX_EXACT = 16
MAX_DISTANCE = 2048
HG_HEADS = 4
HG_DK = 128
HG_W = HG_HEADS * HG_DK
HG_CHUNK = 32
HG_TILE = 256
IN_W = QKV_W + 4 * HG_W + 2 * D_MODEL
D_FF = 2816
EPS = 1e-6
N_CHIPS = 4
N_DEV = 8
LANES = 128

ADAM_LR, ADAM_B1, ADAM_B2, ADAM_EPS, ADAM_WD, ADAM_STEP = 0.001, 0.9, 0.999, 1e-08, 0.01, 10

VMEM_LIMIT = 56 * 1024 * 1024


def _cp(n_axes):
    return pltpu.CompilerParams(dimension_semantics=("arbitrary",) * n_axes, vmem_limit_bytes=VMEM_LIMIT)


def _sds(shape, dtype):
    return jax.ShapeDtypeStruct(tuple(shape), dtype)


def _sigmoid(v):
    return 1.0 / (1.0 + jnp.exp(-v))


def _bf(v):
    return v.astype(BF16)


def _dot(a, b, dims):
    return lax.dot_general(a, b, (dims, ((), ())), preferred_element_type=F32)


NN = ((1,), (0,))
NT = ((1,), (1,))
TN = ((0,), (0,))

ANY = pl.BlockSpec(memory_space=pl.ANY)


class _Plan:
    def __init__(self, copies, n_sems, ins=(), inouts=(), outs=()):
        self.copies, self.n_sems = copies, n_sems
        self.ins, self.inouts, self.outs = list(ins), list(inouts), list(outs)


def _call(body, plans=None, *, name, grid, in_specs, out_specs, out_shape, args, scratch_shapes=()):
    plans = list(plans or ())
    in_specs, out_specs, out_shape = list(in_specs), list(out_specs), list(out_shape)
    scratch_shapes = list(scratch_shapes)
    n_in, n_out, n_scr = len(in_specs), len(out_specs), len(scratch_shapes)
    x_in, x_out, aliases, spans = [], [], {}, []
    for p in plans:
        i0, o0 = len(x_in), len(x_out)
        x_in += p.ins
        for a in p.inouts:
            aliases[n_in + len(x_in)] = n_out + len(x_out)
            x_in.append(a)
            x_out.append(_sds(a.shape, a.dtype))
        x_out += p.outs
        spans.append((i0, len(p.ins), o0, len(p.inouts), len(p.outs)))
    sems = [pltpu.SemaphoreType.DMA((p.n_sems,)) for p in plans for _ in range(3)]

    def wrapped(*refs):
        xi = refs[n_in:n_in + len(x_in)]
        base = n_in + len(x_in)
        xo = refs[base + n_out:base + n_out + len(x_out)]
        sbase = base + n_out + len(x_out)
        xs = refs[sbase + n_scr:]
        ids = [pl.program_id(k) for k in range(len(grid))]
        first = functools.reduce(jnp.logical_and, [i == 0 for i in ids])
        last = functools.reduce(jnp.logical_and, [i == g - 1 for i, g in zip(ids, grid)])

        def descriptors(k):
            i0, ni, o0, nio, no = spans[k]
            return plans[k].copies(xi[i0:i0 + ni], xo[o0:o0 + nio], xo[o0 + nio:o0 + nio + no], *xs[3 * k:3 * k + 3])

        @pl.when(first)
        def _():
            for k in range(len(plans)):
                sends, _, local = descriptors(k)
                for cp in (*sends, *local):
                    cp.start()

        body(*refs[:n_in], *refs[base:base + n_out], *refs[sbase:sbase + n_scr])

        @pl.when(last)
        def _():
            for k in range(len(plans)):
                sends, recvs, local = descriptors(k)
                for cp in recvs:
                    cp.wait_recv()
                for cp in sends:
                    cp.wait_send()
                for cp in local:
                    cp.wait()

    res = pl.pallas_call(
        wrapped if plans else body, name=name, grid=grid, in_specs=in_specs + [ANY] * len(x_in),
        out_specs=out_specs + [ANY] * len(x_out), out_shape=out_shape + x_out, input_output_aliases=aliases,
        scratch_shapes=scratch_shapes + sems, compiler_params=_cp(len(grid)))(*args, *x_in)
    res = list(res)
    carried = [res[n_out + o0:n_out + o0 + nio + no] for (_, _, o0, nio, no) in spans]
    return res[:n_out], carried


def _mm_nn_blk(a, wg, name, tm=512, plans=None):
    M, K = a.shape
    nb, _, Nb = wg.shape

    def body(a_ref, w_ref, o_ref):
        o_ref[...] = _dot(_bf(a_ref[...]), w_ref[...], NN)

    (out,), carried = _call(
        body, plans, name=name, grid=(nb, M // tm),
        in_specs=[pl.BlockSpec((tm, K), lambda j, i: (i, 0)), pl.BlockSpec((None, K, Nb), lambda j, i: (j, 0, 0))],
        out_specs=[pl.BlockSpec((tm, Nb), lambda j, i: (i, j))],
        out_shape=[_sds((M, nb * Nb), F32)], args=(a, wg))
    return out if plans is None else (out, carried)


def _mm_nt_blk(dy, wg, name, tm=1024, plans=None):
    M = dy.shape[0]
    nb, K, Nb = wg.shape

    def body(dy_ref, w_ref, o_ref):
        j = pl.program_id(1)
        r = _dot(_bf(dy_ref[...]), w_ref[...], NT)

        @pl.when(j == 0)
        def _():
            o_ref[...] = r

        @pl.when(j > 0)
        def _():
            o_ref[...] += r

    (out,), carried = _call(
        body, plans, name=name, grid=(M // tm, nb),
        in_specs=[pl.BlockSpec((tm, Nb), lambda i, j: (i, j)), pl.BlockSpec((None, K, Nb), lambda i, j: (j, 0, 0))],
        out_specs=[pl.BlockSpec((tm, K), lambda i, j: (i, 0))],
        out_shape=[_sds((M, K), F32)], args=(dy, wg))
    return out if plans is None else (out, carried)


def _mm_nt_prenorm_bwd(dy, wg, xin, w, dres, name, tm=1024):
    M = dy.shape[0]
    nb, K, Nb = wg.shape

    def body(dy_ref, w_ref, x_ref, wn_ref, dres_ref, dx_ref, dw_ref, acc):
        i, j = pl.program_id(0), pl.program_id(1)
        r = _dot(_bf(dy_ref[...]), w_ref[...], NT)

        @pl.when(j == 0)
        def _():
            acc[...] = r

        @pl.when(j > 0)
        def _():
            acc[...] += r

        @pl.when(j == nb - 1)
        def _():
            xv = x_ref[...]
            rinv = _rinv(xv)
            xhat = xv * rinv
            dh = acc[...]
            dx_ref[...] = dres_ref[...] + _norm_bwd(dh, xhat, rinv, wn_ref[...])
            part = jnp.sum(dh * xhat, axis=0, keepdims=True)

            @pl.when(i == 0)
            def _():
                dw_ref[...] = part

            @pl.when(i > 0)
            def _():
                dw_ref[...] += part

    row = pl.BlockSpec((tm, K), lambda i, j: (i, 0))
    vec = pl.BlockSpec((1, K), lambda i, j: (0, 0))
    return pl.pallas_call(
        body, name=name, grid=(M // tm, nb),
        in_specs=[pl.BlockSpec((tm, Nb), lambda i, j: (i, j)), pl.BlockSpec((None, K, Nb), lambda i, j: (j, 0, 0)),
                  row, vec, row],
        out_specs=[row, vec], out_shape=[_sds((M, K), F32), _sds((1, K), F32)],
        scratch_shapes=[pltpu.VMEM((tm, K), F32)], compiler_params=_cp(2))(dy, wg, xin, w, dres)


def _mm_tn_blk(x, dy, nb, name, tk=2048, x_cols=None, plans=None, together=False):
    T, Mx = x.shape
    xk, Mx = (0, Mx) if x_cols is None else x_cols
    Nb = dy.shape[1] // nb
    nj = nb if together else 1

    def body(x_ref, dy_ref, o_ref):
        t = pl.program_id(1)
        r = _dot(_bf(x_ref[...]), _bf(dy_ref[...]), TN)
        for j in range(nj):
            rj = r[:, j * Nb:(j + 1) * Nb]

            @pl.when(t == 0)
            def _():
                o_ref[j] = rj

            @pl.when(t > 0)
            def _():
                o_ref[j] += rj

    (out,), carried = _call(
        body, plans, name=name, grid=(nb // nj, T // tk),
        in_specs=[pl.BlockSpec((tk, Mx), lambda j, t: (t, xk)), pl.BlockSpec((tk, nj * Nb), lambda j, t: (t, j))],
        out_specs=[pl.BlockSpec((nj, Mx, Nb), lambda j, t: (j, 0, 0))],
        out_shape=[_sds((nb, Mx, Nb), F32)], args=(x, dy))
    return out if plans is None else (out, carried)


def _mm_tn(x, dy, name, tk=1024):
    T, Mx = x.shape
    N = dy.shape[1]

    def body(x_ref, dy_ref, o_ref):
        t = pl.program_id(0)
        r = _dot(_bf(x_ref[...]), _bf(dy_ref[...]), TN)

        @pl.when(t == 0)
        def _():
            o_ref[...] = r

        @pl.when(t > 0)
        def _():
            o_ref[...] += r

    return pl.pallas_call(
        body, name=name, grid=(T // tk,),
        in_specs=[pl.BlockSpec((tk, Mx), lambda t: (t, 0)), pl.BlockSpec((tk, N), lambda t: (t, 0))],
        out_specs=pl.BlockSpec((Mx, N), lambda t: (0, 0)),
        out_shape=_sds((Mx, N), F32), compiler_params=_cp(1))(x, dy)


def _tile(arr, bw, col=lambda c: 0):
    return ("tile", arr, bw, col)


def _full(arr):
    return ("full", arr)


def _out_tile(width, dtype, bw, col=lambda c: 0):
    return ("tile", width, dtype, bw, col)


def _out_acc(rows, width, bw, col=lambda c: 0):
    return ("acc", rows, width, bw, col)


def _rows_call(name, body, n_rows, tm, ncol, ins, outs, plans=None):
    in_specs, args = [], []
    for e in ins:
        if e[0] == "tile":
            _, arr, bw, col = e
            in_specs.append(pl.BlockSpec((tm, bw), functools.partial(lambda c, i, col: (i, col(c)), col=col)))
        else:
            arr = e[1]
            in_specs.append(pl.BlockSpec(arr.shape, functools.partial(lambda c, i, nd: (0,) * nd, nd=arr.ndim)))
        args.append(arr)
    out_specs, out_shape = [], []
    for e in outs:
        if e[0] == "tile":
            _, width, dtype, bw, col = e
            out_specs.append(pl.BlockSpec((tm, bw), functools.partial(lambda c, i, col: (i, col(c)), col=col)))
            out_shape.append(_sds((n_rows, width), dtype))
        else:
            _, rows, width, bw, col = e
            out_specs.append(pl.BlockSpec((rows, bw), functools.partial(lambda c, i, col: (0, col(c)), col=col)))
            out_shape.append(_sds((rows, width), F32))
    out, carried = _call(body, plans, name=name, grid=(ncol, n_rows // tm), in_specs=in_specs, out_specs=out_specs,
                         out_shape=out_shape, args=args)
    return out if plans is None else (out, carried)


def _acc(ref, val):
    i = pl.program_id(1)

    @pl.when(i == 0)
    def _():
        ref[...] = val

    @pl.when(i > 0)
    def _():
        ref[...] += val


def _rinv(z):
    return lax.rsqrt(jnp.mean(z * z, axis=-1, keepdims=True) + EPS)


def _norm_bwd(dy, zhat, r, w):
    dyw = dy * w
    return r * (dyw - zhat * jnp.mean(dyw * zhat, axis=-1, keepdims=True))


def _prenorm_bwd(dh, xin, w, dres, name, plans=None):
    def body(dh_ref, x_ref, w_ref, dres_ref, dx_ref, dw_ref):
        xv = x_ref[...]
        r = _rinv(xv)
        xhat = xv * r
        dhv = dh_ref[...]
        dx_ref[...] = dres_ref[...] + _norm_bwd(dhv, xhat, r, w_ref[...])
        _acc(dw_ref, jnp.sum(dhv * xhat, axis=0, keepdims=True))

    return _rows_call(name, body, xin.shape[0], 512, 1,
                      [_tile(dh, D_MODEL), _tile(xin, D_MODEL), _full(w), _tile(dres, D_MODEL)],
                      [_out_tile(D_MODEL, F32, D_MODEL), _out_acc(1, D_MODEL, D_MODEL)], plans)


def _postnorm_bwd(dout, z, w, w_mat, name, plans=None):
    def body(do_ref, z_ref, w_ref, wm_ref, dz_ref, dm_ref, dw_ref):
        zv = z_ref[...]
        r = _rinv(zv)
        zhat = zv * r
        dov = do_ref[...]
        dz = _bf(_norm_bwd(dov, zhat, r, w_ref[...]))
        dz_ref[...] = dz
        dm_ref[...] = _dot(dz, wm_ref[...], NT)
        _acc(dw_ref, jnp.sum(dov * zhat, axis=0, keepdims=True))

    return _rows_call(name, body, z.shape[0], 512, 1,
                      [_tile(dout, D_MODEL), _tile(z, D_MODEL), _full(w), _full(w_mat)],
                      [_out_tile(D_MODEL, BF16, D_MODEL), _out_tile(D_MODEL, F32, D_MODEL),
                       _out_acc(1, D_MODEL, D_MODEL)], plans)


def _t5_bucket(dist):
    n = jnp.maximum(dist, 0)
    nf = jnp.maximum(n, 1).astype(F32)
    large = MAX_EXACT + (jnp.log(nf / MAX_EXACT) / math.log(MAX_DISTANCE / MAX_EXACT)
                         * (NUM_BUCKETS - MAX_EXACT)).astype(jnp.int32)
    large = jnp.minimum(large, NUM_BUCKETS - 1)
    return jnp.where(n < MAX_EXACT, n, large)


def _band_rel():
    return jnp.arange(BLK)[:, None] + BLK - jnp.arange(2 * BLK)[None, :]


def _band_valid():
    rel = _band_rel()
    window = (rel >= 0) & (rel <= BLK)
    first = window & (jnp.arange(2 * BLK)[None, :] >= BLK)
    return jnp.stack([first, window]).astype(F32).reshape(2, 1, BAND)


RES_UNROLL = 8
PAIR = LANES // HEAD_DIM


def _pair_lanes():
    first = lax.broadcasted_iota(jnp.int32, (1, LANES), 1) < HEAD_DIM
    return first, jnp.logical_not(first)


def _heads_per_step(d):
    return HEADS if d == 1 else LANES // HEAD_DIM


def _sub_rows(r, d):
    return pl.ds(r, BLK, stride=d) if d > 1 else pl.ds(0, BLK)


def _for_residues(d, fn):
    if d <= RES_UNROLL:
        for r in range(d):
            fn(r)
    else:
        def group(i, carry):
            for k in range(RES_UNROLL):
                fn(i * RES_UNROLL + k)
            return carry

        lax.fori_loop(0, d // RES_UNROLL, group, 0)


def _attn_specs(d, g, qblock):
    cw = _heads_per_step(d) * HEAD_DIM

    def col(part, hp):
        return (g * 3 + part) * (GROUP_W // cw) + hp

    def cur(part):
        return pl.BlockSpec((d * BLK, cw), lambda hp, n: (qblock(n), col(part, hp)))

    def prev(part):
        return pl.BlockSpec((d * BLK, cw), lambda hp, n: (jnp.maximum(qblock(n) - 1, 0), col(part, hp)))

    return cur, prev


def _attn_fwd(proj, bias, g, name, plans=None):
    S = proj.shape[0]
    d = DILATIONS[g]
    NB = S // (d * BLK)
    hps = _heads_per_step(d)

    def body(q_ref, kp_ref, kc_ref, vp_ref, vc_ref, b_ref, o_ref, lse_ref):
        hp = pl.program_id(0)
        later = jnp.minimum(pl.program_id(1), 1)

        def residue(r):
            rows = _sub_rows(r, d)
            q2 = q_ref[rows, :]
            k2 = jnp.concatenate([kp_ref[rows, :], kc_ref[rows, :]], axis=0)
            v2 = jnp.concatenate([vp_ref[rows, :], vc_ref[rows, :]], axis=0)
            outs, lses = [], []
            for pp in range(hps // PAIR):
                ps = slice(pp * LANES, (pp + 1) * LANES)
                qp, kp, vp = _bf(q2[:, ps]), _bf(k2[:, ps]), _bf(v2[:, ps])
                o_h, lse_h = [], []
                for hh, own in enumerate(_pair_lanes()):
                    s = _dot(qp, jnp.where(own, kp, 0), NT) * (HEAD_DIM ** -0.5) + b_ref[later, hp * hps + pp * PAIR + hh]
                    m = jnp.max(s, axis=-1, keepdims=True)
                    p = jnp.exp(s - m)
                    l = jnp.sum(p, axis=-1, keepdims=True)
                    o_h.append(_dot(_bf(p), vp, NN) / l)
                    lse_h.append(m + jnp.log(l))
                first = _pair_lanes()[0]
                outs.append(jnp.where(first, o_h[0], o_h[1]))
                lses.append(jnp.where(first, lse_h[0], lse_h[1]))
            o_ref[rows, :] = outs[0] if len(outs) == 1 else jnp.concatenate(outs, axis=1)
            lse_ref[rows, :] = lses[0] if len(lses) == 1 else jnp.concatenate(lses, axis=1)

        _for_residues(d, residue)

    cur, prev = _attn_specs(d, g, lambda n: n)
    out = pl.BlockSpec((d * BLK, hps * HEAD_DIM), lambda hp, n: (n, hp))
    res, carried = _call(
        body, plans, name=name, grid=(HEADS // hps, NB),
        in_specs=[cur(0), prev(1), cur(1), prev(2), cur(2),
                  pl.BlockSpec((2, HEADS, BLK, 2 * BLK), lambda hp, n: (0, 0, 0, 0))],
        out_specs=[out, out], out_shape=[_sds((S, GROUP_W), F32)] * 2,
        args=(proj, proj, proj, proj, proj, bias))
    return res if plans is None else (res, carried)


def _attn_bwd(proj, bias, lse, y, dy, g, name, plans=None):
    S = proj.shape[0]
    d = DILATIONS[g]
    NB = S // (d * BLK)
    hps = _heads_per_step(d)

    def body(q_ref, kp_ref, kc_ref, vp_ref, vc_ref, b_ref, l_ref, y_ref, dy_ref,
             dq_ref, dk_ref, dv_ref, db_ref, ck_ref, cv_ref):
        hp, n = pl.program_id(0), pl.program_id(1)

        @pl.when((hp == 0) & (n == 0))
        def _():
            db_ref[...] = jnp.zeros_like(db_ref)

        @pl.when(n == 0)
        def _():
            ck_ref[...] = jnp.zeros_like(ck_ref)
            cv_ref[...] = jnp.zeros_like(cv_ref)

        @pl.when(n < NB)
        def _():
            later = jnp.minimum(n, 1)

            def residue(r):
                rows = _sub_rows(r, d)
                q2 = q_ref[rows, :]
                k2 = jnp.concatenate([kp_ref[rows, :], kc_ref[rows, :]], axis=0)
                v2 = jnp.concatenate([vp_ref[rows, :], vc_ref[rows, :]], axis=0)
                l2, y2, dy2 = l_ref[rows, :], y_ref[rows, :], dy_ref[rows, :]
                dqs, dks, dvs = [], [], []
                for pp in range(hps // PAIR):
                    ps = slice(pp * LANES, (pp + 1) * LANES)
                    qp, kp, vp = _bf(q2[:, ps]), _bf(k2[:, ps]), _bf(v2[:, ps])
                    dyp, yp = dy2[:, ps], y2[:, ps]
                    dq_h, dk_h, dv_h = [], [], []
                    for hh, own in enumerate(_pair_lanes()):
                        head = hp * hps + pp * PAIR + hh
                        s = _dot(qp, jnp.where(own, kp, 0), NT) * (HEAD_DIM ** -0.5) + b_ref[later, head]
                        p = jnp.exp(s - l2[:, pp * LANES + hh * HEAD_DIM:pp * LANES + hh * HEAD_DIM + 1])
                        dyh = jnp.where(own, dyp, 0.0)
                        delta = jnp.sum(dyh * yp, axis=-1, keepdims=True)
                        ds = p * (_dot(_bf(dyh), vp, NT) - delta)
                        db_ref[head] += ds
                        dsb = _bf(ds * (HEAD_DIM ** -0.5))
                        dq_h.append(_dot(dsb, kp, NN))
                        dk_h.append(_dot(dsb, qp, TN))
                        dv_h.append(_dot(_bf(p), _bf(dyp), TN))
                    first = _pair_lanes()[0]
                    dqs.append(jnp.where(first, dq_h[0], dq_h[1]))
                    dks.append(jnp.where(first, dk_h[0], dk_h[1]))
                    dvs.append(jnp.where(first, dv_h[0], dv_h[1]))
                dkb = dks[0] if len(dks) == 1 else jnp.concatenate(dks, axis=1)
                dvb = dvs[0] if len(dvs) == 1 else jnp.concatenate(dvs, axis=1)
                dq_ref[rows, :] = dqs[0] if len(dqs) == 1 else jnp.concatenate(dqs, axis=1)
                dk_ref[rows, :] = ck_ref[rows, :] + dkb[:BLK]
                dv_ref[rows, :] = cv_ref[rows, :] + dvb[:BLK]
                ck_ref[rows, :] = dkb[BLK:]
                cv_ref[rows, :] = dvb[BLK:]

            _for_residues(d, residue)

        @pl.when(n == NB)
        def _():
            dk_ref[...] = ck_ref[...]
            dv_ref[...] = cv_ref[...]

    def qn(n):
        return jnp.minimum(n, NB - 1)

    cur, prev = _attn_specs(d, g, qn)
    cw = hps * HEAD_DIM
    row = pl.BlockSpec((d * BLK, cw), lambda hp, n: (qn(n), hp))
    done = pl.BlockSpec((d * BLK, cw), lambda hp, n: (jnp.maximum(n - 1, 0), hp))
    (dq, dk, dv, db), carried = _call(
        body, plans, name=name, grid=(HEADS // hps, NB + 1),
        in_specs=[cur(0), prev(1), cur(1), prev(2), cur(2),
                  pl.BlockSpec((2, HEADS, BLK, 2 * BLK), lambda hp, n: (0, 0, 0, 0)), row, row, row],
        out_specs=[row, done, done, pl.BlockSpec((HEADS, BLK, 2 * BLK), lambda hp, n: (0, 0, 0))],
        out_shape=[_sds((S, GROUP_W), F32)] * 3 + [_sds((HEADS, BLK, 2 * BLK), F32)],
        scratch_shapes=[pltpu.VMEM((d * BLK, cw), F32)] * 2,
        args=(proj, proj, proj, proj, proj, bias, lse, y, dy))
    return ([dq, dk, dv], db) if plans is None else ([dq, dk, dv], db, carried)


BAND = BLK * 2 * BLK


def _bucket_onehot():
    buckets = jnp.stack([_t5_bucket(_band_rel() * d) for d in DILATIONS]).reshape(N_GROUPS, 1, BAND)
    return (buckets == jnp.arange(NUM_BUCKETS).reshape(1, NUM_BUCKETS, 1)).astype(F32)


def _relbias_fwd(rel_bias, name):
    table = rel_bias.reshape(NUM_BUCKETS, N_GROUPS, HEADS).transpose(1, 0, 2)

    def body(t_ref, oh_ref, valid_ref, o_ref):
        bias = lax.dot_general(t_ref[...], oh_ref[...], (TN, ((), ())), preferred_element_type=F32,
                               precision=lax.Precision.HIGHEST)
        for k in range(2):
            o_ref[k] = jnp.where(valid_ref[k] > 0.5, bias, NEG_INF)

    out = pl.pallas_call(
        body, name=name, grid=(N_GROUPS,),
        in_specs=[pl.BlockSpec((None, NUM_BUCKETS, HEADS), lambda g: (g, 0, 0)),
                  pl.BlockSpec((None, NUM_BUCKETS, BAND), lambda g: (g, 0, 0)),
                  pl.BlockSpec((2, 1, BAND), lambda g: (0, 0, 0))],
        out_specs=pl.BlockSpec((None, 2, HEADS, BAND), lambda g: (g, 0, 0, 0)),
        out_shape=_sds((N_GROUPS, 2, HEADS, BAND), F32), compiler_params=_cp(1))(table, _bucket_onehot(), _band_valid())
    return out.reshape(N_GROUPS, 2, HEADS, BLK, 2 * BLK)


def _relbias_bwd(dbs, name):
    band = BAND
    onehot = _bucket_onehot()
    dbf = jnp.stack([db.reshape(HEADS, band) for db in dbs])

    def body(oh_ref, db_ref, o_ref):
        o_ref[...] = lax.dot_general(oh_ref[...], db_ref[...], (NT, ((), ())), preferred_element_type=F32,
                                     precision=lax.Precision.HIGHEST)

    out = pl.pallas_call(
        body, name=name, grid=(N_GROUPS,),
        in_specs=[pl.BlockSpec((None, NUM_BUCKETS, band), lambda g: (g, 0, 0)),
                  pl.BlockSpec((None, HEADS, band), lambda g: (g, 0, 0))],
        out_specs=pl.BlockSpec((None, NUM_BUCKETS, HEADS), lambda g: (g, 0, 0)),
        out_shape=_sds((N_GROUPS, NUM_BUCKETS, HEADS), F32), compiler_params=_cp(1))(onehot, dbf)
    return out.transpose(1, 0, 2).reshape(NUM_BUCKETS, N_GROUPS * HEADS)


def _chunk_pos(shape):
    return lax.broadcasted_iota(jnp.int32, shape, 0) % HG_CHUNK


def _chunk_cumsum(v):
    pos = _chunk_pos(v.shape)
    s = 1
    while s < HG_CHUNK:
        v = v + jnp.where(pos >= s, pltpu.roll(v, s, 0), 0.0)
        s *= 2
    return v


def _chunk_rev_cumsum(v):
    pos = _chunk_pos(v.shape)
    n = v.shape[0]
    s = 1
    while s < HG_CHUNK:
        v = v + jnp.where(pos < HG_CHUNK - s, pltpu.roll(v, n - s, 0), 0.0)
        s *= 2
    return v


def _lower_bound(raw):
    a0, a1 = raw[0:1], raw[1:2]
    m = jnp.maximum(a0, a1)
    e0, e1 = jnp.exp(a0 - m), jnp.exp(a1 - m)
    return e0 / (e0 + e1)


def _hg_gates(qr, fr, lb):
    sf = _sigmoid(fr)
    f = lb + (1.0 - lb) * sf
    sq = _sigmoid(qr)
    return qr * sq, sq, f, sf


HG_COL0 = QKV_W // HG_W


def _hgrn_fwd(proj, lb_raw, nw, name, plans=None):
    S = proj.shape[0]
    ncs = HG_TILE // HG_CHUNK

    def body(q_ref, f_ref, i_ref, og_ref, lb_ref, nw_ref, y_ref, o_ref, st_ref, state):
        @pl.when(pl.program_id(0) == 0)
        def _():
            state[...] = jnp.zeros_like(state)

        lb = _lower_bound(lb_ref[...])
        q, _, f, _ = _hg_gates(q_ref[...], f_ref[...], lb)
        k = 1.0 - f
        G = _chunk_cumsum(jnp.log(f))
        row = lax.broadcasted_iota(jnp.int32, (HG_CHUNK, HG_CHUNK), 0)
        col = lax.broadcasted_iota(jnp.int32, (HG_CHUNK, HG_CHUNK), 1)
        heads = [slice(h * HG_DK, (h + 1) * HG_DK) for h in range(HG_HEADS)]
        sts = [state[h] for h in range(HG_HEADS)]
        for c in range(ncs):
            cs = slice(c * HG_CHUNK, (c + 1) * HG_CHUNK)
            for h, hs in enumerate(heads):
                Gc = G[cs, hs]
                gl = Gc[HG_CHUNK - 1:HG_CHUNK]
                qt = _bf(q[cs, hs] * jnp.exp(Gc))
                kt = _bf(k[cs, hs] * jnp.exp(-Gc))
                kd = _bf(k[cs, hs] * jnp.exp(gl - Gc))
                v = _bf(i_ref[cs, hs])
                A = jnp.where(row >= col, _dot(qt, kt, NT), 0.0)
                o_ref[cs, hs] = _dot(_bf(A), v, NN) + _dot(qt, _bf(sts[h]), NT)
                st_ref[c, h] = sts[h]
                sts[h] = sts[h] * jnp.exp(gl) + _dot(v, kd, TN)
        for h, hs in enumerate(heads):
            state[h] = sts[h]
            oh = o_ref[:, hs]
            og = og_ref[:, hs]
            y_ref[:, hs] = oh * _rinv(oh) * nw_ref[...] * (og * _sigmoid(og))

    def colspec(j):
        return pl.BlockSpec((HG_TILE, HG_W), lambda i: (i, HG_COL0 + j))

    res, carried = _call(
        body, plans, name=name, grid=(S // HG_TILE,),
        in_specs=[colspec(0), colspec(1), colspec(2), colspec(3),
                  pl.BlockSpec((2, HG_W), lambda i: (0, 0)), pl.BlockSpec((1, HG_DK), lambda i: (0, 0))],
        out_specs=[pl.BlockSpec((HG_TILE, HG_W), lambda i: (i, 0))] * 2
        + [pl.BlockSpec((ncs, HG_HEADS, HG_DK, HG_DK), lambda i: (i, 0, 0, 0))],
        out_shape=[_sds((S, HG_W), F32)] * 2 + [_sds((S // HG_CHUNK, HG_HEADS, HG_DK, HG_DK), F32)],
        scratch_shapes=[pltpu.VMEM((HG_HEADS, HG_DK, HG_DK), F32)],
        args=(proj, proj, proj, proj, lb_raw, nw))
    return res if plans is None else (res, carried)


def _hgrn_bwd(proj, lb_raw, nw, o, states, dy, d_attn, d_gates, name):
    S = proj.shape[0]
    ncs = HG_TILE // HG_CHUNK
    nt = S // HG_TILE
    n_a, n_g = len(d_attn), len(d_gates)
    own = [slice(QKV_W + j * HG_W, QKV_W + (j + 1) * HG_W) for j in range(4)]

    def body(q_ref, f_ref, i_ref, og_ref, lb_ref, nw_ref, o_ref, st_ref, dy_ref, *rest):
        attn_refs, gate_refs = rest[:n_a], rest[n_a:n_a + n_g]
        dp_ref, dlb_ref, dnw_ref, dstate, do_s, dG_s, dgl_s, dk_s, dlb_s = rest[n_a + n_g:]
        dq_ref, df_ref, di_ref, dog_ref = (dp_ref.at[:, cols] for cols in own)
        step = pl.program_id(0)
        for k, a_ref in enumerate(attn_refs):
            dp_ref[:, k * GROUP_W:(k + 1) * GROUP_W] = _bf(a_ref[...])
        for k, g_ref in enumerate(gate_refs):
            dp_ref[:, QKV_W + 4 * HG_W + k * D_MODEL:QKV_W + 4 * HG_W + (k + 1) * D_MODEL] = g_ref[...]

        @pl.when(step == 0)
        def _():
            dstate[...] = jnp.zeros_like(dstate)
            dlb_s[...] = jnp.zeros_like(dlb_s)
            dnw_ref[...] = jnp.zeros_like(dnw_ref)

        lb = _lower_bound(lb_ref[...])
        qr = q_ref[...]
        q, sq, f, sf = _hg_gates(qr, f_ref[...], lb)
        k = 1.0 - f
        G = _chunk_cumsum(jnp.log(f))
        nwv = nw_ref[...]
        row = lax.broadcasted_iota(jnp.int32, (HG_CHUNK, HG_CHUNK), 0)
        col = lax.broadcasted_iota(jnp.int32, (HG_CHUNK, HG_CHUNK), 1)
        for h in range(HG_HEADS):
            hs = slice(h * HG_DK, (h + 1) * HG_DK)
            oh = o_ref[:, hs]
            r = _rinv(oh)
            ohat = oh * r
            og = og_ref[:, hs]
            sg = _sigmoid(og)
            dyh = dy_ref[:, hs]
            don = dyh * (og * sg)
            dog_ref[:, hs] = _bf(dyh * (ohat * nwv) * (sg * (1.0 + og * (1.0 - sg))))
            dnw_ref[...] += jnp.sum(don * ohat, axis=0, keepdims=True)
            do_s[:, hs] = _norm_bwd(don, ohat, r, nwv)
        dsts = [dstate[h] for h in range(HG_HEADS)]
        for c in reversed(range(ncs)):
            cs = slice(c * HG_CHUNK, (c + 1) * HG_CHUNK)
            for h in range(HG_HEADS):
                hs = slice(h * HG_DK, (h + 1) * HG_DK)
                dst = dsts[h]
                Gc = G[cs, hs]
                gl = Gc[HG_CHUNK - 1:HG_CHUNK]
                eG, enG, edG, egl = jnp.exp(Gc), jnp.exp(-Gc), jnp.exp(gl - Gc), jnp.exp(gl)
                qt, kt, kd = q[cs, hs] * eG, k[cs, hs] * enG, k[cs, hs] * edG
                qtb, ktb, kdb = _bf(qt), _bf(kt), _bf(kd)
                v = _bf(i_ref[cs, hs])
                do = _bf(do_s[cs, hs])
                st = st_ref[c, h]
                dstb = _bf(dst)
                A = jnp.where(row >= col, _dot(qtb, ktb, NT), 0.0)
                dA = _bf(jnp.where(row >= col, _dot(do, v, NT), 0.0))
                di_ref[cs, hs] = _bf(_dot(_bf(A), do, TN) + _dot(kdb, dstb, NT))
                dqt = _dot(dA, ktb, NN) + _dot(do, _bf(st), NN)
                dkt = _dot(dA, qtb, TN)
                dkd = _dot(v, dstb, NN)
                dgl = egl * jnp.sum(st * dst, axis=0, keepdims=True) + jnp.sum(dkd * kd, axis=0, keepdims=True)
                dsts[h] = dst * egl + _dot(do, qtb, TN)
                dq_ref[cs, hs] = _bf(dqt * eG * (sq[cs, hs] * (1.0 + qr[cs, hs] * (1.0 - sq[cs, hs]))))
                dk_s[cs, hs] = dkt * enG + dkd * edG
                dG_s[cs, hs] = dqt * qt - dkt * kt - dkd * kd
                dgl_s[cs, hs] = jnp.broadcast_to(dgl, (HG_CHUNK, HG_DK))
        for h in range(HG_HEADS):
            dstate[h] = dsts[h]
        dg = _chunk_rev_cumsum(dG_s[...]) + dgl_s[...]
        dfv = dg / f - dk_s[...]
        df_ref[...] = _bf(dfv * (1.0 - lb) * sf * (1.0 - sf))
        dlb_s[...] += jnp.sum(dfv * (1.0 - sf), axis=0, keepdims=True)

        @pl.when(step == nt - 1)
        def _():
            t = dlb_s[...] * lb * (1.0 - lb)
            dlb_ref[...] = jnp.concatenate([t, -t], axis=0)

    def colspec(j):
        return pl.BlockSpec((HG_TILE, HG_W), lambda i: (nt - 1 - i, HG_COL0 + j))

    def rows(width):
        return pl.BlockSpec((HG_TILE, width), lambda i: (nt - 1 - i, 0))

    tile = rows(HG_W)
    return pl.pallas_call(
        body, name=name, grid=(nt,),
        in_specs=[colspec(0), colspec(1), colspec(2), colspec(3),
                  pl.BlockSpec((2, HG_W), lambda i: (0, 0)), pl.BlockSpec((1, HG_DK), lambda i: (0, 0)),
                  tile, pl.BlockSpec((ncs, HG_HEADS, HG_DK, HG_DK), lambda i: (nt - 1 - i, 0, 0, 0)), tile]
        + [rows(GROUP_W)] * n_a + [rows(D_MODEL)] * n_g,
        out_specs=[rows(IN_W), pl.BlockSpec((2, HG_W), lambda i: (0, 0)), pl.BlockSpec((1, HG_DK), lambda i: (0, 0))],
        out_shape=[_sds((S, IN_W), BF16), _sds((2, HG_W), F32), _sds((1, HG_DK), F32)],
        scratch_shapes=[pltpu.VMEM((HG_HEADS, HG_DK, HG_DK), F32)] + [pltpu.VMEM((HG_TILE, HG_W), F32)] * 4
        + [pltpu.VMEM((1, HG_W), F32)],
        compiler_params=_cp(1))(proj, proj, proj, proj, lb_raw, nw, o, states, dy, *d_attn, *d_gates)


GATE_COL0 = (QKV_W + 4 * HG_W) // GROUP_W
HALF_D = D_MODEL // 2


def _gate_tiles(proj):
    return [_tile(proj, HALF_D, functools.partial(lambda c, k: GATE_COL0 + k, k=k)) for k in range(4)]


def _gates(g_refs):
    s0 = _sigmoid(jnp.concatenate([g_refs[0][...], g_refs[1][...]], axis=1))
    s1 = _sigmoid(jnp.concatenate([g_refs[2][...], g_refs[3][...]], axis=1))
    return s0, s1


def _branch_fwd(os_, lses, yh, proj, w_a, w_h, name, plans=None):
    nb = w_a.shape[0]

    def body(o0, o1, o2, l0, l1, l2, yh_ref, g0a, g0b, g1a, g1b, wa_ref, wh_ref,
             y_ref, lse_ref, za_ref, zh_ref, m_ref):
        a, b, c = l0[...], l1[...], l2[...]
        m = jnp.maximum(jnp.maximum(a, b), c)
        ea, eb, ec = jnp.exp(a - m), jnp.exp(b - m), jnp.exp(c - m)
        den = ea + eb + ec
        y = (ea * o0[...] + eb * o1[...] + ec * o2[...]) / den
        y_ref[...] = y
        lse_ref[...] = m + jnp.log(den)
        yb, yhb = _bf(y), _bf(yh_ref[...])
        za = jnp.concatenate([_dot(yb, wa_ref[j], NN) for j in range(nb)], axis=1)
        zh = jnp.concatenate([_dot(yhb, wh_ref[j], NN) for j in range(nb)], axis=1)
        s0, s1 = _gates((g0a, g0b, g1a, g1b))
        za_ref[...] = za
        zh_ref[...] = zh
        m_ref[...] = _bf(s0 * za + s1 * zh)

    return _rows_call(name, body, yh.shape[0], 512, 1,
                      [*[_tile(t, GROUP_W) for t in (*os_, *lses)], _tile(yh, HG_W), *_gate_tiles(proj),
                       _full(w_a), _full(w_h)],
                      [_out_tile(GROUP_W, F32, GROUP_W)] * 2 + [_out_tile(D_MODEL, F32, D_MODEL)] * 2
                      + [_out_tile(D_MODEL, BF16, D_MODEL)], plans)


def _branch_bwd(dm, za, zh, proj, w_a, w_h, name, plans=None):
    nb, _, Nb = w_a.shape

    def body(dm_ref, za_ref, zh_ref, g0a, g0b, g1a, g1b, wa_ref, wh_ref,
             dza_ref, dzh_ref, dg0_ref, dg1_ref, dy_ref, dyh_ref):
        dmv = dm_ref[...]
        s0, s1 = _gates((g0a, g0b, g1a, g1b))
        dza, dzh = _bf(dmv * s0), _bf(dmv * s1)
        dza_ref[...] = dza
        dzh_ref[...] = dzh
        dg0_ref[...] = _bf(dmv * za_ref[...] * s0 * (1.0 - s0))
        dg1_ref[...] = _bf(dmv * zh_ref[...] * s1 * (1.0 - s1))
        dy_ref[...] = sum(_dot(dza[:, j * Nb:(j + 1) * Nb], wa_ref[j], NT) for j in range(nb))
        dyh_ref[...] = sum(_dot(dzh[:, j * Nb:(j + 1) * Nb], wh_ref[j], NT) for j in range(nb))

    return _rows_call(name, body, za.shape[0], 512, 1,
                      [_tile(dm, D_MODEL), _tile(za, D_MODEL), _tile(zh, D_MODEL), *_gate_tiles(proj),
                       _full(w_a), _full(w_h)],
                      [_out_tile(D_MODEL, BF16, D_MODEL)] * 4 + [_out_tile(GROUP_W, F32, GROUP_W),
                                                                 _out_tile(HG_W, F32, HG_W)], plans)


def _mix_out(merged, w_out, x, w_post, w_pre, name):
    def body(m_ref, wo_ref, x_ref, wp_ref, wf_ref, mo_ref, x1_ref, h2_ref):
        z = _dot(m_ref[...], wo_ref[...], NN)
        mo_ref[...] = z
        x1 = x_ref[...] + z * _rinv(z) * wp_ref[...]
        x1_ref[...] = x1
        h2_ref[...] = _bf(x1 * _rinv(x1) * wf_ref[...])

    return _rows_call(name, body, x.shape[0], 512, 1,
                      [_tile(merged, D_MODEL), _full(w_out), _tile(x, D_MODEL), _full(w_post), _full(w_pre)],
                      [_out_tile(D_MODEL, F32, D_MODEL), _out_tile(D_MODEL, F32, D_MODEL),
                       _out_tile(D_MODEL, BF16, D_MODEL)])


def _loss_head(a, w_down, x1, tgt, w, name):
    def body(a_ref, wd_ref, x1_ref, t_ref, w_ref, dx_ref, df_ref, dw_ref, loss_ref):
        z = _dot(a_ref[...], wd_ref[...], NN)
        r = _rinv(z)
        zhat = z * r
        wv = w_ref[...]
        e = x1_ref[...] + zhat * wv - t_ref[...]
        dx = e * (1.0 / D_MODEL)
        dx_ref[...] = dx
        df_ref[...] = _bf(_norm_bwd(dx, zhat, r, wv))
        _acc(dw_ref, jnp.sum(dx * zhat, axis=0, keepdims=True))
        part = 0.5 * jnp.sum(jnp.sum(e * e, axis=1, keepdims=True), axis=0, keepdims=True) * (1.0 / D_MODEL)
        _acc(loss_ref, jnp.broadcast_to(part, (1, LANES)))

    return _rows_call(name, body, x1.shape[0], 512, 1,
                      [_tile(a, D_FF), _full(w_down), _tile(x1, D_MODEL), _tile(tgt, D_MODEL), _full(w)],
                      [_out_tile(D_MODEL, F32, D_MODEL), _out_tile(D_MODEL, BF16, D_MODEL),
                       _out_acc(1, D_MODEL, D_MODEL), _out_acc(1, LANES, LANES)])


CONV_CB = D_FF // 2
CONV_TM = 512
HALO = 8
SQRT_HALF = 0.7071067811865476
INV_SQRT_2PI = 0.3989422804014327


CONV_RS = 32


def _lane_tiles():
    return [slice(k * LANES, (k + 1) * LANES) for k in range(CONV_CB // LANES)]


def _strip_start(i):
    return pl.multiple_of(i * CONV_RS, CONV_RS)


def _strip_taps(u_ref, halo_ref, r0, cs, first_strip, first_tile):
    if first_strip:
        before = jnp.where(first_tile, 0.0, halo_ref[:, cs])
        blk = jnp.concatenate([before, u_ref[0:CONV_RS, cs]], axis=0)
    else:
        blk = u_ref[pl.ds(pl.multiple_of(r0 - HALO, HALO), CONV_RS + HALO), cs]
    return pltpu.roll(blk, 2, 0)[HALO:], pltpu.roll(blk, 1, 0)[HALO:], blk[HALO:]


def _conv(taps, w_ref, b_ref, cs):
    return b_ref[:, cs] + w_ref[0:1, cs] * taps[0] + w_ref[1:2, cs] * taps[1] + w_ref[2:3, cs] * taps[2]


def _conv_specs(tm):
    nh = tm // HALO
    nc = D_FF // CONV_CB

    def tile(off):
        return pl.BlockSpec((tm, CONV_CB), lambda c, i: (i, off + c))

    def halo(off):
        return pl.BlockSpec((HALO, CONV_CB), lambda c, i: (jnp.maximum(i * nh - 1, 0), off + c))

    def small(rows, off):
        return pl.BlockSpec((rows, CONV_CB), lambda c, i: (0, off + c))

    return nc, tile, halo, small


def _conv_gelu_fwd(u, cw, cb, name, plans=None):
    S = u.shape[0]
    tm = CONV_TM
    nc, tile, halo, small = _conv_specs(tm)

    def body(ug, hg, uv, hv, wg, wv, bg, bv, a_ref):
        first_tile = pl.program_id(1) == 0

        def strip(r0, first_strip):
            for cs in _lane_tiles():
                cg = _conv(_strip_taps(ug, hg, r0, cs, first_strip, first_tile), wg, bg, cs)
                cv = _conv(_strip_taps(uv, hv, r0, cs, first_strip, first_tile), wv, bv, cs)
                a_ref[pl.ds(r0, CONV_RS), cs] = _bf(0.5 * cg * (1.0 + lax.erf(cg * SQRT_HALF)) * cv)

        strip(0, True)
        lax.fori_loop(1, tm // CONV_RS, lambda k, c: (strip(_strip_start(k), False), c)[1], 0)

    (a,), carried = _call(
        body, plans, name=name, grid=(nc, S // tm),
        in_specs=[tile(0), halo(0), tile(nc), halo(nc), small(3, 0), small(3, nc), small(1, 0), small(1, nc)],
        out_specs=[tile(0)], out_shape=[_sds((S, D_FF), BF16)], args=(u, u, u, u, cw, cw, cb, cb))
    return a if plans is None else (a, carried)


def _conv_gelu_bwd(u, dff, w_down, cw, cb, name, plans=None):
    S = u.shape[0]
    tm = CONV_TM
    nt = S // tm
    nc, tile, halo, small = _conv_specs(tm)

    def body(ug, hg, uv, hv, wg, wv, bg, bv, dff_ref, wd_ref, dcg_ref, dcv_ref, dwg_ref, dwv_ref, dbg_ref, dbv_ref,
             acc, da_ref):
        i = pl.program_id(1)
        first_tile = i == 0
        da_ref[...] = _dot(dff_ref[...], wd_ref[...], NT)

        @pl.when(first_tile)
        def _():
            acc[...] = jnp.zeros_like(acc)

        def strip(r0, first_strip):
            rows = pl.ds(r0, CONV_RS)
            for cs in _lane_tiles():
                tg = _strip_taps(ug, hg, r0, cs, first_strip, first_tile)
                tv = _strip_taps(uv, hv, r0, cs, first_strip, first_tile)
                cg = _conv(tg, wg, bg, cs)
                cv = _conv(tv, wv, bv, cs)
                phi = 0.5 * (1.0 + lax.erf(cg * SQRT_HALF))
                dav = da_ref[rows, cs]
                dcg = dav * cv * (phi + cg * jnp.exp(-0.5 * cg * cg) * INV_SQRT_2PI)
                dcv = dav * (cg * phi)
                dcg_ref[rows, cs] = dcg
                dcv_ref[rows, cs] = dcv
                for half, (dc, taps) in enumerate(((dcg, tg), (dcv, tv))):
                    for j in range(3):
                        acc[4 * half + j, :, cs] += dc * taps[j]
                    acc[4 * half + 3, :, cs] += dc

        strip(0, True)
        lax.fori_loop(1, tm // CONV_RS, lambda k, c: (strip(_strip_start(k), False), c)[1], 0)

        @pl.when(i == nt - 1)
        def _():
            for half, (dw_ref, db_ref) in enumerate(((dwg_ref, dbg_ref), (dwv_ref, dbv_ref))):
                for j in range(3):
                    dw_ref[j:j + 1, :] = jnp.sum(acc[4 * half + j], axis=0, keepdims=True)
                db_ref[...] = jnp.sum(acc[4 * half + 3], axis=0, keepdims=True)

    res, carried = _call(
        body, plans, name=name, grid=(nc, nt),
        in_specs=[tile(0), halo(0), tile(nc), halo(nc), small(3, 0), small(3, nc), small(1, 0), small(1, nc),
                  pl.BlockSpec((tm, D_MODEL), lambda c, i: (i, 0)), pl.BlockSpec((CONV_CB, D_MODEL), lambda c, i: (c, 0))],
        out_specs=[tile(0), tile(0), small(3, 0), small(3, 0), small(1, 0), small(1, 0)],
        out_shape=[_sds((S, D_FF), F32)] * 2 + [_sds((3, D_FF), F32)] * 2 + [_sds((1, D_FF), F32)] * 2,
        scratch_shapes=[pltpu.VMEM((8, CONV_RS, CONV_CB), F32), pltpu.VMEM((tm, CONV_CB), F32)],
        args=(u, u, u, u, cw, cw, cb, cb, dff, w_down))
    return res if plans is None else (res, carried)


def _conv_input_bwd(dcg, dcv, cw, name, plans=None):
    S = dcg.shape[0]
    tm = CONV_TM // 2
    nh = tm // HALO
    nt = S // tm
    n = CONV_RS + HALO
    tile = pl.BlockSpec((tm, D_FF), lambda i: (i, 0))
    nxt = pl.BlockSpec((HALO, D_FF), lambda i: (jnp.minimum((i + 1) * nh, S // HALO - 1), 0))

    def body(g_ref, ng_ref, v_ref, nv_ref, w_ref, du_ref):
        last_tile = pl.program_id(0) == nt - 1

        def strip(r0, last_strip):
            for half, (dc_ref, n_ref) in enumerate(((g_ref, ng_ref), (v_ref, nv_ref))):
                for k in range(D_FF // LANES):
                    cs = slice(k * LANES, (k + 1) * LANES)
                    ws = slice(half * D_FF + k * LANES, half * D_FF + (k + 1) * LANES)
                    if last_strip:
                        after = jnp.where(last_tile, 0.0, n_ref[:, cs])
                        blk = jnp.concatenate([dc_ref[tm - CONV_RS:tm, cs], after], axis=0)
                    else:
                        blk = dc_ref[pl.ds(r0, n), cs]
                    d1 = pltpu.roll(blk, n - 1, 0)[:CONV_RS]
                    d2 = pltpu.roll(blk, n - 2, 0)[:CONV_RS]
                    du_ref[pl.ds(r0, CONV_RS), ws] = _bf(w_ref[2:3, ws] * blk[:CONV_RS] + w_ref[1:2, ws] * d1
                                                         + w_ref[0:1, ws] * d2)

        lax.fori_loop(0, tm // CONV_RS - 1, lambda k, c: (strip(_strip_start(k), False), c)[1], 0)
        strip(tm - CONV_RS, True)

    (du,), carried = _call(
        body, plans, name=name, grid=(nt,),
        in_specs=[tile, nxt, tile, nxt, pl.BlockSpec((3, 2 * D_FF), lambda i: (0, 0))],
        out_specs=[pl.BlockSpec((tm, 2 * D_FF), lambda i: (i, 0))], out_shape=[_sds((S, 2 * D_FF), BF16)],
        args=(dcg, dcg, dcv, dcv, cw))
    return du if plans is None else (du, carried)


def _row_tile(n, cap):
    best = n
    for t in range(16, cap + 1, 16):
        if n % t == 0:
            best = t
    return best if best <= cap else n


def _rows_for_bytes(nbytes, cols):
    return max(16, nbytes // (4 * cols) // 16 * 16)


def _adamw_update(w_ref, g_ref, m_ref, v_ref, d_ref, nm_ref, nv_ref):
    gv = g_ref[...]
    nm = ADAM_B1 * m_ref[...] + (1.0 - ADAM_B1) * gv
    nv = ADAM_B2 * v_ref[...] + (1.0 - ADAM_B2) * (gv * gv)
    m_hat = nm / (1.0 - ADAM_B1 ** ADAM_STEP)
    v_hat = nv / (1.0 - ADAM_B2 ** ADAM_STEP)
    d_ref[...] = -ADAM_LR * (m_hat / (jnp.sqrt(v_hat) + ADAM_EPS) + ADAM_WD * w_ref[...])
    nm_ref[...] = nm
    nv_ref[...] = nv


def _adamw(w, g, m, v, name):
    R, C = w.shape
    tr = _row_tile(R, _rows_for_bytes(2 << 20, C))
    spec = pl.BlockSpec((tr, C), lambda i: (i, 0))
    body = functools.partial(_adamw_update)
    return pl.pallas_call(body, name=name, grid=(R // tr,), in_specs=[spec] * 4, out_specs=[spec] * 3,
                          out_shape=[_sds((R, C), F32)] * 3, compiler_params=_cp(1))(w, g, m, v)


def _adamw_small(ws, gs, ms, vs, name):
    n = len(ws)

    def body(*refs):
        ins, outs = refs[:4 * n], refs[4 * n:]
        for k in range(n):
            _adamw_update(ins[k], ins[n + k], ins[2 * n + k], ins[3 * n + k], outs[3 * k], outs[3 * k + 1], outs[3 * k + 2])

    vm = pl.BlockSpec(memory_space=pltpu.VMEM)
    outs = pl.pallas_call(body, name=name, in_specs=[vm] * (4 * n), out_specs=[vm] * (3 * n),
                          out_shape=[_sds(w.shape, F32) for w in ws for _ in range(3)])(*ws, *gs, *ms, *vs)
    return [tuple(outs[3 * k:3 * k + 3]) for k in range(n)]


def _pair_sum(gfull, rcv, c_idx, name):
    nb, R, C = gfull.shape
    half = R // 2
    tr = _row_tile(half, _rows_for_bytes(2 << 20, C))
    nt = half // tr

    def body(c_ref, g_ref, r_ref, o_ref):
        o_ref[...] = _bf(g_ref[...] + r_ref[...])

    return pl.pallas_call(
        body, name=name,
        grid_spec=pltpu.PrefetchScalarGridSpec(
            num_scalar_prefetch=1, grid=(nb, nt),
            in_specs=[pl.BlockSpec((None, tr, C), lambda j, i, c_ref: (j, c_ref[0] * nt + i, 0)),
                      pl.BlockSpec((None, tr, C), lambda j, i, c_ref: (j, i, 0))],
            out_specs=pl.BlockSpec((None, tr, C), lambda j, i, c_ref: (j, i, 0))),
        out_shape=_sds((nb, half, C), BF16), compiler_params=_cp(2))(c_idx, gfull, rcv)


def _chip_sum(arrived, own, place, name):
    nb, H, C = arrived.shape
    tr = _row_tile(H, _rows_for_bytes(2 << 20, C))
    nt = H // tr

    def body(pl_ref, *refs):
        o_ref = refs[nb + 1]
        me = pl_ref[0]
        acc = None
        for k in range(nb):
            term = jnp.where(me == k, refs[nb][...], refs[k][...]).astype(F32)
            acc = term if acc is None else acc + term
        o_ref[...] = acc

    def other(k):
        return pl.BlockSpec((None, tr, C), lambda i, p: (jnp.where(p[0] == k, (k + 1) % nb, k), i, 0))

    return pl.pallas_call(
        body, name=name,
        grid_spec=pltpu.PrefetchScalarGridSpec(
            num_scalar_prefetch=1, grid=(nt,),
            in_specs=[other(k) for k in range(nb)] + [pl.BlockSpec((None, tr, C), lambda i, p: (p[0], i, 0))],
            out_specs=pl.BlockSpec((tr, C), lambda i, p: (p[1] * nt + i, 0))),
        out_shape=_sds((2 * H, C), F32), compiler_params=_cp(1))(place, *([arrived] * nb), own)


def _cast_into_slot(shard, place, name):
    R, C = shard.shape
    tr = _row_tile(R, 256)

    def body(pl_ref, s_ref, o_ref):
        o_ref[...] = _bf(s_ref[...])

    return pl.pallas_call(
        body, name=name,
        grid_spec=pltpu.PrefetchScalarGridSpec(
            num_scalar_prefetch=1, grid=(R // tr,),
            in_specs=[pl.BlockSpec((tr, C), lambda i, p: (i, 0))],
            out_specs=pl.BlockSpec((None, tr, C), lambda i, p: (p[0], i, 0))),
        out_shape=_sds((N_CHIPS, R, C), BF16), compiler_params=_cp(1))(place, shard)


def _place():
    x, y, c = lax.axis_index("x"), lax.axis_index("y"), lax.axis_index("c")
    chips = [(1 - x, y), (x, 1 - y), (1 - x, 1 - y)]
    return x, y, c, chips


def _chip_id(px, py):
    return 2 * px + py


def _remote(src, dst, send_sems, recv_sems, k, to):
    return pltpu.make_async_remote_copy(src_ref=src, dst_ref=dst, send_sem=send_sems.at[k], recv_sem=recv_sems.at[k],
                                        device_id=to, device_id_type=MESH)


def _proj_gathered(x, w_norm, slot, place, name, tm=1024, plan=None):
    M, K = x.shape
    nb, _, Nb = slot.shape
    half = K // 2
    nt = M // tm
    cx, cy = place[0] // 2, place[0] % 2
    order = jnp.stack([place[0], _chip_id(1 - cx, cy), _chip_id(cx, 1 - cy), _chip_id(1 - cx, 1 - cy)]).astype(jnp.int32)

    p_in = [] if plan is None else plan.ins + plan.inouts
    p_out = [] if plan is None else [_sds(a.shape, a.dtype) for a in plan.inouts] + plan.outs
    n_pi, n_po = len(p_in), len(p_out)

    def body(order_ref, x_ref, wn_ref, slot_in, *refs):
        o_ref, slot_ref, h_out = refs[n_pi:n_pi + 3]
        s0 = n_pi + 3 + n_po
        w_buf, hs, ici_send, ici_recv, pass_send, pass_recv, load_sem = refs[s0:s0 + 7]

        def carried():
            if plan is None:
                return [], [], []
            ins = refs[:len(plan.ins)]
            outs = refs[n_pi + 3:n_pi + 3 + n_po]
            return plan.copies(ins, outs[:len(plan.inouts)], outs[len(plan.inouts):], *refs[s0 + 7:])

        b, i = pl.program_id(0), pl.program_id(1)
        x, y, c, chips = _place()
        me = _chip_id(x, y)
        sib = (x, y, 1 - c)
        mine, other = pl.ds(c * half, half), pl.ds((1 - c) * half, half)

        def sent(k):
            blk = slot_ref.at[me, mine]
            return _remote(blk, blk, ici_send, ici_recv, k, (*chips[k], c))

        def landed(k):
            blk = slot_ref.at[_chip_id(*chips[k]), mine]
            return _remote(blk, blk, ici_send, ici_recv, k, (*chips[k], c))

        def passed(k, rows):
            blk = slot_ref.at[_chip_id(*chips[k]), rows]
            return _remote(blk, blk, pass_send, pass_recv, k, sib)

        @pl.when((b == 0) & (i == 0))
        def _():
            for k in range(len(chips)):
                sent(k).start()
            sends, _, local = carried()
            for cp in (*sends, *local):
                cp.start()

        for k in range(len(chips)):
            @pl.when((b == k + 1) & (i == 0))
            def _(k=k):
                landed(k).wait_recv()
                passed(k, mine).start()
                passed(k, other).wait_recv()

        @pl.when(i == 0)
        def _():
            load = pltpu.make_async_copy(slot_ref.at[order_ref[b]], w_buf, load_sem.at[0])
            load.start()
            load.wait()

        rows = pl.ds(pl.multiple_of(i * tm, tm), tm)
        keep_h = pltpu.make_async_copy(hs, h_out, load_sem.at[1])

        @pl.when(b == 0)
        def _():
            xv = x_ref[...]
            hs[rows, :] = _bf(xv * _rinv(xv) * wn_ref[...])

        @pl.when((b == 1) & (i == 0))
        def _():
            keep_h.start()

        o_ref[...] = _dot(hs[rows, :], w_buf[...], NN)

        @pl.when((b == nb - 1) & (i == nt - 1))
        def _():
            for k in range(len(chips)):
                sent(k).wait_send()
                passed(k, mine).wait_send()
            sends, recvs, local = carried()
            for cp in recvs:
                cp.wait_recv()
            for cp in sends:
                cp.wait_send()
            for cp in local:
                cp.wait()
            keep_h.wait()

    n_peers = N_CHIPS - 1
    return pl.pallas_call(
        body, name=name,
        grid_spec=pltpu.PrefetchScalarGridSpec(
            num_scalar_prefetch=1, grid=(nb, nt),
            in_specs=[pl.BlockSpec((tm, K), lambda b, i, o: (i, 0)), pl.BlockSpec((1, K), lambda b, i, o: (0, 0)), ANY]
            + [ANY] * n_pi,
            out_specs=[pl.BlockSpec((tm, Nb), lambda b, i, o: (i, o[b])), ANY, ANY] + [ANY] * n_po,
            scratch_shapes=[pltpu.VMEM((K, Nb), BF16), pltpu.VMEM((M, K), BF16)]
            + [pltpu.SemaphoreType.DMA((n_peers,))] * 4 + [pltpu.SemaphoreType.DMA((2,))]
            + ([] if plan is None else [pltpu.SemaphoreType.DMA((plan.n_sems,))] * 3)),
        out_shape=[_sds((M, nb * Nb), F32), _sds(slot.shape, slot.dtype), _sds((M, K), BF16)] + p_out,
        input_output_aliases={3: 1, **({} if plan is None else
                                       {4 + len(plan.ins) + a: 3 + a for a in range(len(plan.inouts))})},
        compiler_params=_cp(2))(order, x, w_norm, slot, *p_in)


def _gather_ici_plan(slots, wholes, part=None):
    ns, nw = len(slots), len(wholes)

    def copies(ins, ios, outs, send_sems, recv_sems, local_sems):
        x, y, c, chips = _place()
        me = _chip_id(x, y)
        sends, recvs = [], []
        for a in range(ns + nw):
            dst = ios[a] if a < ns else outs[a - ns]
            R = dst.shape[1]
            r0, nr = (0, R // 2) if part is None else part
            rows = pl.ds(c * (R // 2) + r0, nr) if a < ns else pl.ds(0, R)
            src = dst.at[me, rows] if a < ns else ins[a - ns]
            for j, chip in enumerate(chips):
                sends.append(_remote(src, dst.at[me, rows], send_sems, recv_sems, 3 * a + j, (*chip, c)))
                landed = dst.at[_chip_id(*chip), rows]
                recvs.append(_remote(landed, landed, send_sems, recv_sems, 3 * a + j, (*chip, c)))
        local = [pltpu.make_async_copy(ins[b], outs[b].at[me], local_sems.at[b]) for b in range(nw)]
        return sends, recvs, local

    return _Plan(copies, 3 * (ns + nw), ins=wholes, inouts=slots,
                 outs=[_sds((N_CHIPS, *s.shape), s.dtype) for s in wholes])


def _gather_pass_plan(slots):
    def copies(ins, ios, outs, send_sems, recv_sems, local_sems):
        x, y, c, chips = _place()
        sib = (x, y, 1 - c)
        sends, recvs = [], []
        for a, buf in enumerate(ios):
            half = buf.shape[1] // 2
            for j, chip in enumerate(chips):
                mine = buf.at[_chip_id(*chip), pl.ds(c * half, half)]
                other = buf.at[_chip_id(*chip), pl.ds((1 - c) * half, half)]
                sends.append(_remote(mine, mine, send_sems, recv_sems, 3 * a + j, sib))
                recvs.append(_remote(other, other, send_sems, recv_sems, 3 * a + j, sib))
        return sends, recvs, []

    return _Plan(copies, 3 * len(slots), inouts=slots)


def _pair_plan(grads):
    def copies(ins, ios, outs, send_sems, recv_sems, local_sems):
        x, y, c, _ = _place()
        sib = (x, y, 1 - c)
        sends, recvs = [], []
        for a, g in enumerate(ins):
            half = g.shape[1] // 2
            sends.append(_remote(g.at[:, pl.ds((1 - c) * half, half), :], outs[a], send_sems, recv_sems, a, sib))
            recvs.append(_remote(outs[a], outs[a], send_sems, recv_sems, a, sib))
        return sends, recvs, []

    return _Plan(copies, len(grads), ins=grads,
                 outs=[_sds((g.shape[0], g.shape[1] // 2, g.shape[2]), g.dtype) for g in grads])


def _chip_plan(parts):
    def copies(ins, ios, outs, send_sems, recv_sems, local_sems):
        x, y, c, chips = _place()
        me = _chip_id(x, y)
        sends, recvs = [], []
        for a, part in enumerate(ins):
            for j, chip in enumerate(chips):
                sends.append(_remote(part.at[_chip_id(*chip)], outs[a].at[me], send_sems, recv_sems, 3 * a + j, (*chip, c)))
                landed = outs[a].at[_chip_id(*chip)]
                recvs.append(_remote(landed, landed, send_sems, recv_sems, 3 * a + j, (*chip, c)))
        return sends, recvs, []

    return _Plan(copies, 3 * len(parts), ins=parts, outs=[_sds(p.shape, p.dtype) for p in parts])


def _all_sum(pack, fulls, name):
    R, C = pack.shape
    n = len(fulls)

    def body(p_ref, *refs):
        o_ref, halves = refs[n], refs[n + 1:2 * n + 1]
        buf, send_sems, recv_sems, pair_send, pair_recv = refs[2 * n + 1:]
        x, y, c, _ = _place()
        sib = (x, y, 1 - c)
        pair = []
        for a, full in enumerate(halves):
            H = full.shape[0] // 2
            mine = full.at[pl.ds(c * H, H)]
            cp = _remote(mine, mine, pair_send, pair_recv, a, sib)
            cp.start()
            pair.append(cp)
        me = 4 * x + 2 * y + c
        buf[me] = p_ref[...]
        cps = []
        for k in range(1, N_DEV):
            to = (x ^ (k >> 2), y ^ ((k >> 1) & 1), c ^ (k & 1))
            cp = _remote(p_ref, buf.at[me], send_sems, recv_sems, k - 1, to)
            cp.start()
            cps.append(cp)
        for k in range(1, N_DEV):
            frm = (x ^ (k >> 2), y ^ ((k >> 1) & 1), c ^ (k & 1))
            slot = buf.at[4 * frm[0] + 2 * frm[1] + frm[2]]
            _remote(slot, slot, send_sems, recv_sems, k - 1, frm).wait_recv()
        acc = buf[0]
        for k in range(1, N_DEV):
            acc = acc + buf[k]
        o_ref[...] = acc
        for cp in cps:
            cp.wait_send()
        for a, (full, cp) in enumerate(zip(halves, pair)):
            H = full.shape[0] // 2
            other = full.at[pl.ds((1 - c) * H, H)]
            _remote(other, other, pair_send, pair_recv, a, sib).wait_recv()
            cp.wait_send()

    vm = pl.BlockSpec(memory_space=pltpu.VMEM)
    res = pl.pallas_call(
        body, name=name, in_specs=[vm] + [ANY] * n, out_specs=[vm] + [ANY] * n,
        out_shape=[_sds((R, C), F32)] + [_sds(f.shape, f.dtype) for f in fulls],
        input_output_aliases={1 + a: 1 + a for a in range(n)},
        scratch_shapes=[pltpu.VMEM((N_DEV, R, C), F32), pltpu.SemaphoreType.DMA((N_DEV - 1,)),
                        pltpu.SemaphoreType.DMA((N_DEV - 1,)), pltpu.SemaphoreType.DMA((n,)),
                        pltpu.SemaphoreType.DMA((n,))])(pack, *fulls)
    return res[0], list(res[1:])


def _local_step(xs, tgt, p, ex):
    proj, h1 = ex.project(xs, p["pre_mix_norm"])
    biases = _relbias_fwd(p["rel_bias"], "rel_bias_fwd")
    fw = []
    for g in range(N_GROUPS):
        res, got = _attn_fwd(proj, biases[g], g, f"attn_fwd{g}", plans=ex.carry(f"attn_fwd{g}"))
        ex.done(f"attn_fwd{g}", got)
        fw.append(res)
    (yh, o_h, states), got = _hgrn_fwd(proj, p["hgrn_lb_raw"], p["hgrn_norm"], "hgrn_fwd", plans=ex.carry("hgrn_fwd"))
    ex.done("hgrn_fwd", got)
    W_a, W_h, W_out = ex.weight("w_branch_attn"), ex.weight("w_branch_hgrn"), ex.weight("w_out")
    (y, lse, za, zh, merged), got = _branch_fwd([t[0] for t in fw], [t[1] for t in fw], yh, proj, W_a, W_h,
                                                "branch_fwd", plans=ex.carry("branch_fwd"))
    ex.done("branch_fwd", got)
    W_up, conv_w = ex.weight("w_up"), ex.weight("conv_w")
    mo, x1, h2 = _mix_out(merged, W_out, xs, p["post_mix_norm"], p["pre_ffn_norm"], "mix_out")
    u, got = _mm_nn_blk(h2, W_up, "ffn_up", tm=1024, plans=ex.carry("ffn_up"))
    ex.done("ffn_up", got)
    a, got = _conv_gelu_fwd(u, conv_w, p["conv_b"], "conv_gelu_fwd", plans=ex.carry("conv_gelu_fwd"))
    ex.done("conv_gelu_fwd", got)
    W_down = ex.weight("w_down")
    dx2, dff, g_post_ffn, loss = _loss_head(a, W_down, x1, tgt, p["post_ffn_norm"], "ffn_down_loss")

    ex.grad("w_down", _mm_tn(a, dff, "g_w_down").reshape(N_CHIPS, D_FF // N_CHIPS, D_MODEL))
    (dcg, dcv, gwg, gwv, gbg, gbv), got = _conv_gelu_bwd(u, dff, W_down, conv_w, p["conv_b"], "conv_gelu_bwd",
                                                          plans=ex.carry("conv_gelu_bwd"))
    ex.done("conv_gelu_bwd", got)
    g_conv_w = jnp.concatenate([gwg, gwv], axis=1)
    g_conv_b = jnp.concatenate([gbg, gbv], axis=1)
    du, got = _conv_input_bwd(dcg, dcv, conv_w, "conv_input_bwd", plans=ex.carry("conv_input_bwd"))
    ex.done("conv_input_bwd", got)
    dx1, g_pre_ffn = _mm_nt_prenorm_bwd(du, W_up, x1, p["pre_ffn_norm"], dx2, "d_ffn_in")
    ex.grad("w_up", _mm_tn_blk(h2, du, N_CHIPS, "g_w_up"))
    (dmo, dmerged, g_post_mix), got = _postnorm_bwd(dx1, mo, p["post_mix_norm"], W_out, "post_mix_norm_bwd",
                                                    plans=ex.carry("post_mix_norm_bwd"))
    ex.done("post_mix_norm_bwd", got)
    ex.grad("w_out", _mm_tn(merged, dmo, "g_w_out").reshape(N_CHIPS, D_MODEL // N_CHIPS, D_MODEL))
    (dza, dzh, dg0, dg1, dy, dyh), got = _branch_bwd(dmerged, za, zh, proj, W_a, W_h, "branch_bwd",
                                                     plans=ex.carry("branch_bwd"))
    ex.done("branch_bwd", got)
    ex.grad("w_branch_attn", _mm_tn_blk(y, dza, N_CHIPS, "g_w_branch_attn", together=True))
    ex.grad("w_branch_hgrn", _mm_tn_blk(yh, dzh, N_CHIPS, "g_w_branch_hgrn", together=True))
    dqkv, dbs = [], []
    for g in range(N_GROUPS):
        parts, db, got = _attn_bwd(proj, biases[g], lse, y, dy, g, f"attn_bwd{g}", plans=ex.carry(f"attn_bwd{g}"))
        ex.done(f"attn_bwd{g}", got)
        dqkv += parts
        dbs.append(db)
    g_rel_bias = _relbias_bwd(dbs, "rel_bias_bwd")
    dproj, g_lb_raw, g_hgrn_norm = _hgrn_bwd(proj, p["hgrn_lb_raw"], p["hgrn_norm"], o_h, states, dyh, dqkv,
                                             [dg0, dg1], "hgrn_bwd")
    for piece in W_IN_PIECES:
        g, got = _mm_tn_blk(h1, dproj, N_CHIPS, f"g_{piece}", x_cols=W_IN_ROWS[piece],
                            plans=ex.carry(f"g_{piece}"))
        ex.done(f"g_{piece}", got)
        ex.grad(piece, g)
    dh1, got = _mm_nt_blk(dproj, ex.weight("w_in"), "d_proj_in", plans=ex.carry("d_proj_in"))
    ex.done("d_proj_in", got)
    (grad_x, g_pre_mix), got = _prenorm_bwd(dh1, xs, p["pre_mix_norm"], dx1, "pre_mix_norm_bwd",
                                            plans=ex.carry("pre_mix_norm_bwd"))
    ex.done("pre_mix_norm_bwd", got)
    small = dict(pre_mix_norm=g_pre_mix, rel_bias=g_rel_bias, hgrn_lb_raw=g_lb_raw, hgrn_norm=g_hgrn_norm,
                 post_mix_norm=g_post_mix, pre_ffn_norm=g_pre_ffn, conv_w=g_conv_w, conv_b=g_conv_b,
                 post_ffn_norm=g_post_ffn)
    return loss, grad_x, small


SMALL = ("pre_mix_norm", "rel_bias", "hgrn_lb_raw", "hgrn_norm", "post_mix_norm", "pre_ffn_norm", "conv_w", "conv_b",
         "post_ffn_norm")
BIG = ("w_in", "w_up", "w_down", "w_out", "w_branch_attn", "w_branch_hgrn")
WEIGHTS = ("pre_mix_norm", "w_in", "rel_bias", "hgrn_lb_raw", "hgrn_norm", "w_branch_attn", "w_branch_hgrn", "w_out",
           "post_mix_norm", "pre_ffn_norm", "w_up", "conv_w", "conv_b", "w_down", "post_ffn_norm")
MIXER = ("w_out", "w_branch_attn", "w_branch_hgrn")

ICI_PARTS = {"gather_ici_1of3": (0, 176), "gather_ici_2of3": (176, 176), "gather_ici_3of3": (352, 160)}
SCHEDULE = {
    "proj_in": [("gather_ici_cw", MIXER)],
    "attn_fwd0": [("gather_pass", MIXER), ("gather_ici_1of3", ("w_up",))],
    "attn_fwd1": [("gather_ici_2of3", ("w_up",))],
    "attn_fwd2": [("gather_ici_3of3", ("w_up",))],
    "hgrn_fwd": [("gather_pass", ("w_up",))],
    "ffn_up": [("gather_ici", ("w_down",))],
    "conv_gelu_fwd": [("gather_pass", ("w_down",))],
    "conv_gelu_bwd": [("pair", ("w_down",))],
    "conv_input_bwd": [("chip", ("w_down",))],
    "post_mix_norm_bwd": [("pair", ("w_up",))],
    "attn_bwd0": [("chip", ("w_up",)), ("pair", MIXER)],
    "attn_bwd1": [("chip", MIXER)],
    "g_w_in_b": [("pair", ("w_in_a",))],
    "d_proj_in": [("chip", ("w_in_a",)), ("pair", ("w_in_b",))],
    "pre_mix_norm_bwd": [("chip", ("w_in_b",))],
}
W_IN_ROWS = dict(w_in_a=(0, 768), w_in_b=(3, 256))
W_IN_PIECES = tuple(W_IN_ROWS)
REDUCED = W_IN_PIECES + BIG[1:]


class _Exchange:
    def __init__(self, place, slots, conv_w_shard):
        self.place, self.slots, self.conv_w_shard = place, dict(slots), conv_w_shard
        self.conv_w = None
        self.g, self.from_sibling, self.pair_sums, self.arrived = {}, {}, {}, {}
        self.pending = []

    def weight(self, name):
        if name == "conv_w":
            return self.conv_w
        w = self.slots[name]
        return w.reshape(-1, D_MODEL) if name in ("w_out", "w_down") else w

    def project(self, x, w_norm):
        (plan,) = self.carry("proj_in")
        proj, self.slots["w_in"], h, *got = _proj_gathered(x, w_norm, self.slots["w_in"], self.place, "proj_in",
                                                           plan=plan)
        self.done("proj_in", [got])
        return proj, h

    def grad(self, name, g):
        self.g[name] = g

    def carry(self, point):
        plans = []
        self.pending = SCHEDULE.get(point, [])
        for kind, names in self.pending:
            if kind in ("gather_ici", "gather_ici_cw") or kind in ICI_PARTS:
                wholes = [self.conv_w_shard] if kind == "gather_ici_cw" else []
                plans.append(_gather_ici_plan([self.slots[n] for n in names], wholes, ICI_PARTS.get(kind)))
            elif kind == "gather_pass":
                plans.append(_gather_pass_plan([self.slots[n] for n in names]))
            elif kind == "pair":
                plans.append(_pair_plan([self.g[n] for n in names]))
            else:
                for n in names:
                    self.pair_sums[n] = _pair_sum(self.g[n], self.from_sibling[n], self.place[1:2], f"pair_sum_{n}")
                plans.append(_chip_plan([self.pair_sums[n] for n in names]))
        return plans

    def done(self, point, carried):
        for (kind, names), got in zip(self.pending, carried):
            if kind in ("gather_ici", "gather_ici_cw", "gather_pass") or kind in ICI_PARTS:
                self.slots.update(zip(names, got))
                if kind == "gather_ici_cw":
                    self.conv_w = got[len(names)].transpose(1, 0, 2).reshape(3, 2 * D_FF)
            elif kind == "pair":
                self.from_sibling.update(zip(names, got))
            else:
                self.arrived.update(zip(names, got))

    def reduced_halves(self):
        return [_chip_sum(self.arrived[n], self.pair_sums[n], self.place, f"chip_sum_{n}") for n in REDUCED]


def kernel(x, pre_mix_norm, w_in, rel_bias, hgrn_lb_raw, hgrn_norm, w_branch_attn, w_branch_hgrn, w_out, post_mix_norm, pre_ffn_norm, w_up, conv_w, conv_b, w_down, post_ffn_norm, loss_target, m_pre_mix_norm, m_w_in, m_rel_bias, m_hgrn_lb_raw, m_hgrn_norm, m_w_branch_attn, m_w_branch_hgrn, m_w_out, m_post_mix_norm, m_pre_ffn_norm, m_w_up, m_conv_w, m_conv_b, m_w_down, m_post_ffn_norm, v_pre_mix_norm, v_w_in, v_rel_bias, v_hgrn_lb_raw, v_hgrn_norm, v_w_branch_attn, v_w_branch_hgrn, v_w_out, v_post_mix_norm, v_pre_ffn_norm, v_w_up, v_conv_w, v_conv_b, v_w_down, v_post_ffn_norm):
    w = dict(pre_mix_norm=pre_mix_norm, w_in=w_in, rel_bias=rel_bias, hgrn_lb_raw=hgrn_lb_raw, hgrn_norm=hgrn_norm,
             w_branch_attn=w_branch_attn, w_branch_hgrn=w_branch_hgrn, w_out=w_out, post_mix_norm=post_mix_norm,
             pre_ffn_norm=pre_ffn_norm, w_up=w_up, conv_w=conv_w, conv_b=conv_b, w_down=w_down,
             post_ffn_norm=post_ffn_norm)
    m = dict(pre_mix_norm=m_pre_mix_norm, w_in=m_w_in, rel_bias=m_rel_bias, hgrn_lb_raw=m_hgrn_lb_raw,
             hgrn_norm=m_hgrn_norm, w_branch_attn=m_w_branch_attn, w_branch_hgrn=m_w_branch_hgrn, w_out=m_w_out,
             post_mix_norm=m_post_mix_norm, pre_ffn_norm=m_pre_ffn_norm, w_up=m_w_up, conv_w=m_conv_w,
             conv_b=m_conv_b, w_down=m_w_down, post_ffn_norm=m_post_ffn_norm)
    v = dict(pre_mix_norm=v_pre_mix_norm, w_in=v_w_in, rel_bias=v_rel_bias, hgrn_lb_raw=v_hgrn_lb_raw,
             hgrn_norm=v_hgrn_norm, w_branch_attn=v_w_branch_attn, w_branch_hgrn=v_w_branch_hgrn, w_out=v_w_out,
             post_mix_norm=v_post_mix_norm, pre_ffn_norm=v_pre_ffn_norm, w_up=v_w_up, conv_w=v_conv_w,
             conv_b=v_conv_b, w_down=v_w_down, post_ffn_norm=v_post_ffn_norm)
    shard2d = {n: (w[n][0] if w[n].ndim == 3 else w[n]) for n in WEIGHTS}
    chip = 2 * lax.axis_index("x") + lax.axis_index("y")
    core = lax.axis_index("c")

    place = jnp.stack([chip, core]).astype(jnp.int32)
    slots = {n: _cast_into_slot(shard2d[n], place, f"cast_{n}") for n in BIG}
    ex = _Exchange(place, slots, shard2d["conv_w"])
    loss, grad_x, small = _local_step(x[0], loss_target[0], {n: w[n] for n in SMALL if n != "conv_w"}, ex)

    flat = [small[n].reshape(-1) for n in SMALL] + [loss.reshape(-1)]
    sizes = [t.shape[0] for t in flat]
    summed, wholes = _all_sum(jnp.concatenate(flat).reshape(-1, LANES), ex.reduced_halves(), "sum_small")
    summed = summed.reshape(-1)
    offs = [sum(sizes[:i]) for i in range(len(sizes))]
    grads = {}
    for n, o, sz in zip(SMALL, offs, sizes):
        grads[n] = summed[o:o + sz].reshape(small[n].shape)
    loss_total = summed[offs[-1]]
    cw = 2 * D_FF // N_CHIPS
    grads["conv_w"] = lax.dynamic_slice(grads["conv_w"], (0, chip * cw), (3, cw))

    big = dict(zip(REDUCED, wholes))
    big["w_in"] = jnp.concatenate([big.pop(n) for n in W_IN_PIECES], axis=0)
    grads.update(big)

    m2d = {n: m[n].reshape(shard2d[n].shape) for n in WEIGHTS}
    v2d = {n: v[n].reshape(shard2d[n].shape) for n in WEIGHTS}
    updated = dict(zip(SMALL, _adamw_small([shard2d[n] for n in SMALL], [grads[n] for n in SMALL],
                                           [m2d[n] for n in SMALL], [v2d[n] for n in SMALL], "adamw_small")))
    for n in BIG:
        updated[n] = _adamw(shard2d[n], grads[n], m2d[n], v2d[n], f"adamw_{n}")
    out_g, out_d, out_m, out_v = [], [], [], []
    for n in WEIGHTS:
        d2, m2, v2 = updated[n]
        shape = w[n].shape
        out_g.append(grads[n].reshape(shape))
        out_d.append(d2.reshape(shape))
        out_m.append(m2.reshape(shape))
        out_v.append(v2.reshape(shape))
    return (loss_total, grad_x[None], *out_g, *out_d, *out_m, *out_v)
```

```python
import functools
import math

import jax
import jax.numpy as jnp
from jax import lax
from jax.experimental import pallas as pl
from jax.experimental.pallas import tpu as pltpu

F32 = jnp.float32
BF16 = jnp.bfloat16
MESH = pl.DeviceIdType.MESH

D_MODEL = 1024
N_GROUPS = 3
DILATIONS = (1, 4, 16)
HEADS = 8
HEAD_DIM = 64
GROUP_W = HEADS * HEAD_DIM
QKV_W = N_GROUPS * 3 * GROUP_W
BLK = 128
NEG_INF = -1e30
NUM_BUCKETS = 32
MAX_EXACT = 16
MAX_DISTANCE = 2048
HG_HEADS = 4
HG_DK = 128
HG_W = HG_HEADS * HG_DK
HG_CHUNK = 32
HG_TILE = 256
IN_W = QKV_W + 4 * HG_W + 2 * D_MODEL
D_FF = 2816
EPS = 1e-6
N_CHIPS = 4
N_DEV = 8
LANES = 128

ADAM_LR, ADAM_B1, ADAM_B2, ADAM_EPS, ADAM_WD, ADAM_STEP = 0.001, 0.9, 0.999, 1e-08, 0.01, 10

VMEM_LIMIT = 56 * 1024 * 1024


def _cp(n_axes):
    return pltpu.CompilerParams(dimension_semantics=("arbitrary",) * n_axes, vmem_limit_bytes=VMEM_LIMIT)


def _sds(shape, dtype):
    return jax.ShapeDtypeStruct(tuple(shape), dtype)


def _sigmoid(v):
    return 1.0 / (1.0 + jnp.exp(-v))


def _bf(v):
    return v.astype(BF16)


def _dot(a, b, dims):
    return lax.dot_general(a, b, (dims, ((), ())), preferred_element_type=F32)


NN = ((1,), (0,))
NT = ((1,), (1,))
TN = ((0,), (0,))

ANY = pl.BlockSpec(memory_space=pl.ANY)


class _Plan:
    def __init__(self, copies, n_sems, ins=(), inouts=(), outs=()):
        self.copies, self.n_sems = copies, n_sems
        self.ins, self.inouts, self.outs = list(ins), list(inouts), list(outs)


def _call(body, plans=None, *, name, grid, in_specs, out_specs, out_shape, args, scratch_shapes=()):
    plans = list(plans or ())
    in_specs, out_specs, out_shape = list(in_specs), list(out_specs), list(out_shape)
    scratch_shapes = list(scratch_shapes)
    n_in, n_out, n_scr = len(in_specs), len(out_specs), len(scratch_shapes)
    x_in, x_out, aliases, spans = [], [], {}, []
    for p in plans:
        i0, o0 = len(x_in), len(x_out)
        x_in += p.ins
        for a in p.inouts:
            aliases[n_in + len(x_in)] = n_out + len(x_out)
            x_in.append(a)
            x_out.append(_sds(a.shape, a.dtype))
        x_out += p.outs
        spans.append((i0, len(p.ins), o0, len(p.inouts), len(p.outs)))
    sems = [pltpu.SemaphoreType.DMA((p.n_sems,)) for p in plans for _ in range(3)]

    def wrapped(*refs):
        xi = refs[n_in:n_in + len(x_in)]
        base = n_in + len(x_in)
        xo = refs[base + n_out:base + n_out + len(x_out)]
        sbase = base + n_out + len(x_out)
        xs = refs[sbase + n_scr:]
        ids = [pl.program_id(k) for k in range(len(grid))]
        first = functools.reduce(jnp.logical_and, [i == 0 for i in ids])
        last = functools.reduce(jnp.logical_and, [i == g - 1 for i, g in zip(ids, grid)])

        def descriptors(k):
            i0, ni, o0, nio, no = spans[k]
            return plans[k].copies(xi[i0:i0 + ni], xo[o0:o0 + nio], xo[o0 + nio:o0 + nio + no], *xs[3 * k:3 * k + 3])

        @pl.when(first)
        def _():
            for k in range(len(plans)):
                sends, _, local = descriptors(k)
                for cp in (*sends, *local):
                    cp.start()

        body(*refs[:n_in], *refs[base:base + n_out], *refs[sbase:sbase + n_scr])

        @pl.when(last)
        def _():
            for k in range(len(plans)):
                sends, recvs, local = descriptors(k)
                for cp in recvs:
                    cp.wait_recv()
                for cp in sends:
                    cp.wait_send()
                for cp in local:
                    cp.wait()

    res = pl.pallas_call(
        wrapped if plans else body, name=name, grid=grid, in_specs=in_specs + [ANY] * len(x_in),
        out_specs=out_specs + [ANY] * len(x_out), out_shape=out_shape + x_out, input_output_aliases=aliases,
        scratch_shapes=scratch_shapes + sems, compiler_params=_cp(len(grid)))(*args, *x_in)
    res = list(res)
    carried = [res[n_out + o0:n_out + o0 + nio + no] for (_, _, o0, nio, no) in spans]
    return res[:n_out], carried


def _mm_nn_blk(a, wg, name, tm=512, plans=None):
    M, K = a.shape
    nb, _, Nb = wg.shape

    def body(a_ref, w_ref, o_ref):
        o_ref[...] = _dot(_bf(a_ref[...]), w_ref[...], NN)

    (out,), carried = _call(
        body, plans, name=name, grid=(nb, M // tm),
        in_specs=[pl.BlockSpec((tm, K), lambda j, i: (i, 0)), pl.BlockSpec((None, K, Nb), lambda j, i: (j, 0, 0))],
        out_specs=[pl.BlockSpec((tm, Nb), lambda j, i: (i, j))],
        out_shape=[_sds((M, nb * Nb), F32)], args=(a, wg))
    return out if plans is None else (out, carried)


def _mm_nt_blk(dy, wg, name, tm=1024, plans=None):
    M = dy.shape[0]
    nb, K, Nb = wg.shape

    def body(dy_ref, w_ref, o_ref):
        j = pl.program_id(1)
        r = _dot(_bf(dy_ref[...]), w_ref[...], NT)

        @pl.when(j == 0)
        def _():
            o_ref[...] = r

        @pl.when(j > 0)
        def _():
            o_ref[...] += r

    (out,), carried = _call(
        body, plans, name=name, grid=(M // tm, nb),
        in_specs=[pl.BlockSpec((tm, Nb), lambda i, j: (i, j)), pl.BlockSpec((None, K, Nb), lambda i, j: (j, 0, 0))],
        out_specs=[pl.BlockSpec((tm, K), lambda i, j: (i, 0))],
        out_shape=[_sds((M, K), F32)], args=(dy, wg))
    return out if plans is None else (out, carried)


def _mm_nt_prenorm_bwd(dy, wg, xin, w, dres, name, tm=1024):
    M = dy.shape[0]
    nb, K, Nb = wg.shape

    def body(dy_ref, w_ref, x_ref, wn_ref, dres_ref, dx_ref, dw_ref, acc):
        i, j = pl.program_id(0), pl.program_id(1)
        r = _dot(_bf(dy_ref[...]), w_ref[...], NT)

        @pl.when(j == 0)
        def _():
            acc[...] = r

        @pl.when(j > 0)
        def _():
            acc[...] += r

        @pl.when(j == nb - 1)
        def _():
            xv = x_ref[...]
            rinv = _rinv(xv)
            xhat = xv * rinv
            dh = acc[...]
            dx_ref[...] = dres_ref[...] + _norm_bwd(dh, xhat, rinv, wn_ref[...])
            part = jnp.sum(dh * xhat, axis=0, keepdims=True)

            @pl.when(i == 0)
            def _():
                dw_ref[...] = part

            @pl.when(i > 0)
            def _():
                dw_ref[...] += part

    row = pl.BlockSpec((tm, K), lambda i, j: (i, 0))
    vec = pl.BlockSpec((1, K), lambda i, j: (0, 0))
    return pl.pallas_call(
        body, name=name, grid=(M // tm, nb),
        in_specs=[pl.BlockSpec((tm, Nb), lambda i, j: (i, j)), pl.BlockSpec((None, K, Nb), lambda i, j: (j, 0, 0)),
                  row, vec, row],
        out_specs=[row, vec], out_shape=[_sds((M, K), F32), _sds((1, K), F32)],
        scratch_shapes=[pltpu.VMEM((tm, K), F32)], compiler_params=_cp(2))(dy, wg, xin, w, dres)


def _mm_tn_blk(x, dy, nb, name, tk=2048, x_cols=None, plans=None, together=False):
    T, Mx = x.shape
    xk, Mx = (0, Mx) if x_cols is None else x_cols
    Nb = dy.shape[1] // nb
    nj = nb if together else 1

    def body(x_ref, dy_ref, o_ref):
        t = pl.program_id(1)
        r = _dot(_bf(x_ref[...]), _bf(dy_ref[...]), TN)
        for j in range(nj):
            rj = r[:, j * Nb:(j + 1) * Nb]

            @pl.when(t == 0)
            def _():
                o_ref[j] = rj

            @pl.when(t > 0)
            def _():
                o_ref[j] += rj

    (out,), carried = _call(
        body, plans, name=name, grid=(nb // nj, T // tk),
        in_specs=[pl.BlockSpec((tk, Mx), lambda j, t: (t, xk)), pl.BlockSpec((tk, nj * Nb), lambda j, t: (t, j))],
        out_specs=[pl.BlockSpec((nj, Mx, Nb), lambda j, t: (j, 0, 0))],
        out_shape=[_sds((nb, Mx, Nb), F32)], args=(x, dy))
    return out if plans is None else (out, carried)


def _mm_tn(x, dy, name, tk=1024):
    T, Mx = x.shape
    N = dy.shape[1]

    def body(x_ref, dy_ref, o_ref):
        t = pl.program_id(0)
        r = _dot(_bf(x_ref[...]), _bf(dy_ref[...]), TN)

        @pl.when(t == 0)
        def _():
            o_ref[...] = r

        @pl.when(t > 0)
        def _():
            o_ref[...] += r

    return pl.pallas_call(
        body, name=name, grid=(T // tk,),
        in_specs=[pl.BlockSpec((tk, Mx), lambda t: (t, 0)), pl.BlockSpec((tk, N), lambda t: (t, 0))],
        out_specs=pl.BlockSpec((Mx, N), lambda t: (0, 0)),
        out_shape=_sds((Mx, N), F32), compiler_params=_cp(1))(x, dy)


def _tile(arr, bw, col=lambda c: 0):
    return ("tile", arr, bw, col)


def _full(arr):
    return ("full", arr)


def _out_tile(width, dtype, bw, col=lambda c: 0):
    return ("tile", width, dtype, bw, col)


def _out_acc(rows, width, bw, col=lambda c: 0):
    return ("acc", rows, width, bw, col)


def _rows_call(name, body, n_rows, tm, ncol, ins, outs, plans=None):
    in_specs, args = [], []
    for e in ins:
        if e[0] == "tile":
            _, arr, bw, col = e
            in_specs.append(pl.BlockSpec((tm, bw), functools.partial(lambda c, i, col: (i, col(c)), col=col)))
        else:
            arr = e[1]
            in_specs.append(pl.BlockSpec(arr.shape, functools.partial(lambda c, i, nd: (0,) * nd, nd=arr.ndim)))
        args.append(arr)
    out_specs, out_shape = [], []
    for e in outs:
        if e[0] == "tile":
            _, width, dtype, bw, col = e
            out_specs.append(pl.BlockSpec((tm, bw), functools.partial(lambda c, i, col: (i, col(c)), col=col)))
            out_shape.append(_sds((n_rows, width), dtype))
        else:
            _, rows, width, bw, col = e
            out_specs.append(pl.BlockSpec((rows, bw), functools.partial(lambda c, i, col: (0, col(c)), col=col)))
            out_shape.append(_sds((rows, width), F32))
    out, carried = _call(body, plans, name=name, grid=(ncol, n_rows // tm), in_specs=in_specs, out_specs=out_specs,
                         out_shape=out_shape, args=args)
    return out if plans is None else (out, carried)


def _acc(ref, val):
    i = pl.program_id(1)

    @pl.when(i == 0)
    def _():
        ref[...] = val

    @pl.when(i > 0)
    def _():
        ref[...] += val


def _rinv(z):
    return lax.rsqrt(jnp.mean(z * z, axis=-1, keepdims=True) + EPS)


def _norm_bwd(dy, zhat, r, w):
    dyw = dy * w
    return r * (dyw - zhat * jnp.mean(dyw * zhat, axis=-1, keepdims=True))


def _prenorm_bwd(dh, xin, w, dres, name, plans=None):
    def body(dh_ref, x_ref, w_ref, dres_ref, dx_ref, dw_ref):
        xv = x_ref[...]
        r = _rinv(xv)
        xhat = xv * r
        dhv = dh_ref[...]
        dx_ref[...] = dres_ref[...] + _norm_bwd(dhv, xhat, r, w_ref[...])
        _acc(dw_ref, jnp.sum(dhv * xhat, axis=0, keepdims=True))

    return _rows_call(name, body, xin.shape[0], 512, 1,
                      [_tile(dh, D_MODEL), _tile(xin, D_MODEL), _full(w), _tile(dres, D_MODEL)],
                      [_out_tile(D_MODEL, F32, D_MODEL), _out_acc(1, D_MODEL, D_MODEL)], plans)


def _postnorm_bwd(dout, z, w, w_mat, name, plans=None):
    def body(do_ref, z_ref, w_ref, wm_ref, dz_ref, dm_ref, dw_ref):
        zv = z_ref[...]
        r = _rinv(zv)
        zhat = zv * r
        dov = do_ref[...]
        dz = _bf(_norm_bwd(dov, zhat, r, w_ref[...]))
        dz_ref[...] = dz
        dm_ref[...] = _dot(dz, wm_ref[...], NT)
        _acc(dw_ref, jnp.sum(dov * zhat, axis=0, keepdims=True))

    return _rows_call(name, body, z.shape[0], 512, 1,
                      [_tile(dout, D_MODEL), _tile(z, D_MODEL), _full(w), _full(w_mat)],
                      [_out_tile(D_MODEL, BF16, D_MODEL), _out_tile(D_MODEL, F32, D_MODEL),
                       _out_acc(1, D_MODEL, D_MODEL)], plans)


def _t5_bucket(dist):
    n = jnp.maximum(dist, 0)
    nf = jnp.maximum(n, 1).astype(F32)
    large = MAX_EXACT + (jnp.log(nf / MAX_EXACT) / math.log(MAX_DISTANCE / MAX_EXACT)
                         * (NUM_BUCKETS - MAX_EXACT)).astype(jnp.int32)
    large = jnp.minimum(large, NUM_BUCKETS - 1)
    return jnp.where(n < MAX_EXACT, n, large)


def _band_rel():
    return jnp.arange(BLK)[:, None] + BLK - jnp.arange(2 * BLK)[None, :]


def _band_valid():
    rel = _band_rel()
    window = (rel >= 0) & (rel <= BLK)
    first = window & (jnp.arange(2 * BLK)[None, :] >= BLK)
    return jnp.stack([first, window]).astype(F32).reshape(2, 1, BAND)


RES_UNROLL = 8
PAIR = LANES // HEAD_DIM


def _pair_lanes():
    first = lax.broadcasted_iota(jnp.int32, (1, LANES), 1) < HEAD_DIM
    return first, jnp.logical_not(first)


def _heads_per_step(d):
    return HEADS if d == 1 else LANES // HEAD_DIM


def _sub_rows(r, d):
    return pl.ds(r, BLK, stride=d) if d > 1 else pl.ds(0, BLK)


def _for_residues(d, fn):
    if d <= RES_UNROLL:
        for r in range(d):
            fn(r)
    else:
        def group(i, carry):
            for k in range(RES_UNROLL):
                fn(i * RES_UNROLL + k)
            return carry

        lax.fori_loop(0, d // RES_UNROLL, group, 0)


def _attn_specs(d, g, qblock):
    cw = _heads_per_step(d) * HEAD_DIM

    def col(part, hp):
        return (g * 3 + part) * (GROUP_W // cw) + hp

    def cur(part):
        return pl.BlockSpec((d * BLK, cw), lambda hp, n: (qblock(n), col(part, hp)))

    def prev(part):
        return pl.BlockSpec((d * BLK, cw), lambda hp, n: (jnp.maximum(qblock(n) - 1, 0), col(part, hp)))

    return cur, prev


def _attn_fwd(proj, bias, g, name, plans=None):
    S = proj.shape[0]
    d = DILATIONS[g]
    NB = S // (d * BLK)
    hps = _heads_per_step(d)

    def body(q_ref, kp_ref, kc_ref, vp_ref, vc_ref, b_ref, o_ref, lse_ref):
        hp = pl.program_id(0)
        later = jnp.minimum(pl.program_id(1), 1)

        def residue(r):
            rows = _sub_rows(r, d)
            q2 = q_ref[rows, :]
            k2 = jnp.concatenate([kp_ref[rows, :], kc_ref[rows, :]], axis=0)
            v2 = jnp.concatenate([vp_ref[rows, :], vc_ref[rows, :]], axis=0)
            outs, lses = [], []
            for pp in range(hps // PAIR):
                ps = slice(pp * LANES, (pp + 1) * LANES)
                qp, kp, vp = _bf(q2[:, ps]), _bf(k2[:, ps]), _bf(v2[:, ps])
                o_h, lse_h = [], []
                for hh, own in enumerate(_pair_lanes()):
                    s = _dot(qp, jnp.where(own, kp, 0), NT) * (HEAD_DIM ** -0.5) + b_ref[later, hp * hps + pp * PAIR + hh]
                    m = jnp.max(s, axis=-1, keepdims=True)
                    p = jnp.exp(s - m)
                    l = jnp.sum(p, axis=-1, keepdims=True)
                    o_h.append(_dot(_bf(p), vp, NN) / l)
                    lse_h.append(m + jnp.log(l))
                first = _pair_lanes()[0]
                outs.append(jnp.where(first, o_h[0], o_h[1]))
                lses.append(jnp.where(first, lse_h[0], lse_h[1]))
            o_ref[rows, :] = outs[0] if len(outs) == 1 else jnp.concatenate(outs, axis=1)
            lse_ref[rows, :] = lses[0] if len(lses) == 1 else jnp.concatenate(lses, axis=1)

        _for_residues(d, residue)

    cur, prev = _attn_specs(d, g, lambda n: n)
    out = pl.BlockSpec((d * BLK, hps * HEAD_DIM), lambda hp, n: (n, hp))
    res, carried = _call(
        body, plans, name=name, grid=(HEADS // hps, NB),
        in_specs=[cur(0), prev(1), cur(1), prev(2), cur(2),
                  pl.BlockSpec((2, HEADS, BLK, 2 * BLK), lambda hp, n: (0, 0, 0, 0))],
        out_specs=[out, out], out_shape=[_sds((S, GROUP_W), F32)] * 2,
        args=(proj, proj, proj, proj, proj, bias))
    return res if plans is None else (res, carried)


def _attn_bwd(proj, bias, lse, y, dy, g, name, plans=None):
    S = proj.shape[0]
    d = DILATIONS[g]
    NB = S // (d * BLK)
    hps = _heads_per_step(d)

    def body(q_ref, kp_ref, kc_ref, vp_ref, vc_ref, b_ref, l_ref, y_ref, dy_ref,
             dq_ref, dk_ref, dv_ref, db_ref, ck_ref, cv_ref):
        hp, n = pl.program_id(0), pl.program_id(1)

        @pl.when((hp == 0) & (n == 0))
        def _():
            db_ref[...] = jnp.zeros_like(db_ref)

        @pl.when(n == 0)
        def _():
            ck_ref[...] = jnp.zeros_like(ck_ref)
            cv_ref[...] = jnp.zeros_like(cv_ref)

        @pl.when(n < NB)
        def _():
            later = jnp.minimum(n, 1)

            def residue(r):
                rows = _sub_rows(r, d)
                q2 = q_ref[rows, :]
                k2 = jnp.concatenate([kp_ref[rows, :], kc_ref[rows, :]], axis=0)
                v2 = jnp.concatenate([vp_ref[rows, :], vc_ref[rows, :]], axis=0)
                l2, y2, dy2 = l_ref[rows, :], y_ref[rows, :], dy_ref[rows, :]
                dqs, dks, dvs = [], [], []
                for pp in range(hps // PAIR):
                    ps = slice(pp * LANES, (pp + 1) * LANES)
                    qp, kp, vp = _bf(q2[:, ps]), _bf(k2[:, ps]), _bf(v2[:, ps])
                    dyp, yp = dy2[:, ps], y2[:, ps]
                    dq_h, dk_h, dv_h = [], [], []
                    for hh, own in enumerate(_pair_lanes()):
                        head = hp * hps + pp * PAIR + hh
                        s = _dot(qp, jnp.where(own, kp, 0), NT) * (HEAD_DIM ** -0.5) + b_ref[later, head]
                        p = jnp.exp(s - l2[:, pp * LANES + hh * HEAD_DIM:pp * LANES + hh * HEAD_DIM + 1])
                        dyh = jnp.where(own, dyp, 0.0)
                        delta = jnp.sum(dyh * yp, axis=-1, keepdims=True)
                        ds = p * (_dot(_bf(dyh), vp, NT) - delta)
                        db_ref[head] += ds
                        dsb = _bf(ds * (HEAD_DIM ** -0.5))
                        dq_h.append(_dot(dsb, kp, NN))
                        dk_h.append(_dot(dsb, qp, TN))
                        dv_h.append(_dot(_bf(p), _bf(dyp), TN))
                    first = _pair_lanes()[0]
                    dqs.append(jnp.where(first, dq_h[0], dq_h[1]))
                    dks.append(jnp.where(first, dk_h[0], dk_h[1]))
                    dvs.append(jnp.where(first, dv_h[0], dv_h[1]))
                dkb = dks[0] if len(dks) == 1 else jnp.concatenate(dks, axis=1)
                dvb = dvs[0] if len(dvs) == 1 else jnp.concatenate(dvs, axis=1)
                dq_ref[rows, :] = dqs[0] if len(dqs) == 1 else jnp.concatenate(dqs, axis=1)
                dk_ref[rows, :] = ck_ref[rows, :] + dkb[:BLK]
                dv_ref[rows, :] = cv_ref[rows, :] + dvb[:BLK]
                ck_ref[rows, :] = dkb[BLK:]
                cv_ref[rows, :] = dvb[BLK:]

            _for_residues(d, residue)

        @pl.when(n == NB)
        def _():
            dk_ref[...] = ck_ref[...]
            dv_ref[...] = cv_ref[...]

    def qn(n):
        return jnp.minimum(n, NB - 1)

    cur, prev = _attn_specs(d, g, qn)
    cw = hps * HEAD_DIM
    row = pl.BlockSpec((d * BLK, cw), lambda hp, n: (qn(n), hp))
    done = pl.BlockSpec((d * BLK, cw), lambda hp, n: (jnp.maximum(n - 1, 0), hp))
    (dq, dk, dv, db), carried = _call(
        body, plans, name=name, grid=(HEADS // hps, NB + 1),
        in_specs=[cur(0), prev(1), cur(1), prev(2), cur(2),
                  pl.BlockSpec((2, HEADS, BLK, 2 * BLK), lambda hp, n: (0, 0, 0, 0)), row, row, row],
        out_specs=[row, done, done, pl.BlockSpec((HEADS, BLK, 2 * BLK), lambda hp, n: (0, 0, 0))],
        out_shape=[_sds((S, GROUP_W), F32)] * 3 + [_sds((HEADS, BLK, 2 * BLK), F32)],
        scratch_shapes=[pltpu.VMEM((d * BLK, cw), F32)] * 2,
        args=(proj, proj, proj, proj, proj, bias, lse, y, dy))
    return ([dq, dk, dv], db) if plans is None else ([dq, dk, dv], db, carried)


BAND = BLK * 2 * BLK


def _bucket_onehot():
    buckets = jnp.stack([_t5_bucket(_band_rel() * d) for d in DILATIONS]).reshape(N_GROUPS, 1, BAND)
    return (buckets == jnp.arange(NUM_BUCKETS).reshape(1, NUM_BUCKETS, 1)).astype(F32)


def _relbias_fwd(rel_bias, name):
    table = rel_bias.reshape(NUM_BUCKETS, N_GROUPS, HEADS).transpose(1, 0, 2)

    def body(t_ref, oh_ref, valid_ref, o_ref):
        bias = lax.dot_general(t_ref[...], oh_ref[...], (TN, ((), ())), preferred_element_type=F32,
                               precision=lax.Precision.HIGHEST)
        for k in range(2):
            o_ref[k] = jnp.where(valid_ref[k] > 0.5, bias, NEG_INF)

    out = pl.pallas_call(
        body, name=name, grid=(N_GROUPS,),
        in_specs=[pl.BlockSpec((None, NUM_BUCKETS, HEADS), lambda g: (g, 0, 0)),
                  pl.BlockSpec((None, NUM_BUCKETS, BAND), lambda g: (g, 0, 0)),
                  pl.BlockSpec((2, 1, BAND), lambda g: (0, 0, 0))],
        out_specs=pl.BlockSpec((None, 2, HEADS, BAND), lambda g: (g, 0, 0, 0)),
        out_shape=_sds((N_GROUPS, 2, HEADS, BAND), F32), compiler_params=_cp(1))(table, _bucket_onehot(), _band_valid())
    return out.reshape(N_GROUPS, 2, HEADS, BLK, 2 * BLK)


def _relbias_bwd(dbs, name):
    band = BAND
    onehot = _bucket_onehot()
    dbf = jnp.stack([db.reshape(HEADS, band) for db in dbs])

    def body(oh_ref, db_ref, o_ref):
        o_ref[...] = lax.dot_general(oh_ref[...], db_ref[...], (NT, ((), ())), preferred_element_type=F32,
                                     precision=lax.Precision.HIGHEST)

    out = pl.pallas_call(
        body, name=name, grid=(N_GROUPS,),
        in_specs=[pl.BlockSpec((None, NUM_BUCKETS, band), lambda g: (g, 0, 0)),
                  pl.BlockSpec((None, HEADS, band), lambda g: (g, 0, 0))],
        out_specs=pl.BlockSpec((None, NUM_BUCKETS, HEADS), lambda g: (g, 0, 0)),
        out_shape=_sds((N_GROUPS, NUM_BUCKETS, HEADS), F32), compiler_params=_cp(1))(onehot, dbf)
    return out.transpose(1, 0, 2).reshape(NUM_BUCKETS, N_GROUPS * HEADS)


def _chunk_pos(shape):
    return lax.broadcasted_iota(jnp.int32, shape, 0) % HG_CHUNK


def _chunk_cumsum(v):
    pos = _chunk_pos(v.shape)
    s = 1
    while s < HG_CHUNK:
        v = v + jnp.where(pos >= s, pltpu.roll(v, s, 0), 0.0)
        s *= 2
    return v


def _chunk_rev_cumsum(v):
    pos = _chunk_pos(v.shape)
    n = v.shape[0]
    s = 1
    while s < HG_CHUNK:
        v = v + jnp.where(pos < HG_CHUNK - s, pltpu.roll(v, n - s, 0), 0.0)
        s *= 2
    return v


def _lower_bound(raw):
    a0, a1 = raw[0:1], raw[1:2]
    m = jnp.maximum(a0, a1)
    e0, e1 = jnp.exp(a0 - m), jnp.exp(a1 - m)
    return e0 / (e0 + e1)


def _hg_gates(qr, fr, lb):
    sf = _sigmoid(fr)
    f = lb + (1.0 - lb) * sf
    sq = _sigmoid(qr)
    return qr * sq, sq, f, sf


HG_COL0 = QKV_W // HG_W


def _hgrn_fwd(proj, lb_raw, nw, name, plans=None):
    S = proj.shape[0]
    ncs = HG_TILE // HG_CHUNK

    def body(q_ref, f_ref, i_ref, og_ref, lb_ref, nw_ref, y_ref, o_ref, st_ref, state):
        @pl.when(pl.program_id(0) == 0)
        def _():
            state[...] = jnp.zeros_like(state)

        lb = _lower_bound(lb_ref[...])
        q, _, f, _ = _hg_gates(q_ref[...], f_ref[...], lb)
        k = 1.0 - f
        G = _chunk_cumsum(jnp.log(f))
        row = lax.broadcasted_iota(jnp.int32, (HG_CHUNK, HG_CHUNK), 0)
        col = lax.broadcasted_iota(jnp.int32, (HG_CHUNK, HG_CHUNK), 1)
        heads = [slice(h * HG_DK, (h + 1) * HG_DK) for h in range(HG_HEADS)]
        sts = [state[h] for h in range(HG_HEADS)]
        for c in range(ncs):
            cs = slice(c * HG_CHUNK, (c + 1) * HG_CHUNK)
            for h, hs in enumerate(heads):
                Gc = G[cs, hs]
                gl = Gc[HG_CHUNK - 1:HG_CHUNK]
                qt = _bf(q[cs, hs] * jnp.exp(Gc))
                kt = _bf(k[cs, hs] * jnp.exp(-Gc))
                kd = _bf(k[cs, hs] * jnp.exp(gl - Gc))
                v = _bf(i_ref[cs, hs])
                A = jnp.where(row >= col, _dot(qt, kt, NT), 0.0)
                o_ref[cs, hs] = _dot(_bf(A), v, NN) + _dot(qt, _bf(sts[h]), NT)
                st_ref[c, h] = sts[h]
                sts[h] = sts[h] * jnp.exp(gl) + _dot(v, kd, TN)
        for h, hs in enumerate(heads):
            state[h] = sts[h]
            oh = o_ref[:, hs]
            og = og_ref[:, hs]
            y_ref[:, hs] = oh * _rinv(oh) * nw_ref[...] * (og * _sigmoid(og))

    def colspec(j):
        return pl.BlockSpec((HG_TILE, HG_W), lambda i: (i, HG_COL0 + j))

    res, carried = _call(
        body, plans, name=name, grid=(S // HG_TILE,),
        in_specs=[colspec(0), colspec(1), colspec(2), colspec(3),
                  pl.BlockSpec((2, HG_W), lambda i: (0, 0)), pl.BlockSpec((1, HG_DK), lambda i: (0, 0))],
        out_specs=[pl.BlockSpec((HG_TILE, HG_W), lambda i: (i, 0))] * 2
        + [pl.BlockSpec((ncs, HG_HEADS, HG_DK, HG_DK), lambda i: (i, 0, 0, 0))],
        out_shape=[_sds((S, HG_W), F32)] * 2 + [_sds((S // HG_CHUNK, HG_HEADS, HG_DK, HG_DK), F32)],
        scratch_shapes=[pltpu.VMEM((HG_HEADS, HG_DK, HG_DK), F32)],
        args=(proj, proj, proj, proj, lb_raw, nw))
    return res if plans is None else (res, carried)


def _hgrn_bwd(proj, lb_raw, nw, o, states, dy, d_attn, d_gates, name):
    S = proj.shape[0]
    ncs = HG_TILE // HG_CHUNK
    nt = S // HG_TILE
    n_a, n_g = len(d_attn), len(d_gates)
    own = [slice(QKV_W + j * HG_W, QKV_W + (j + 1) * HG_W) for j in range(4)]

    def body(q_ref, f_ref, i_ref, og_ref, lb_ref, nw_ref, o_ref, st_ref, dy_ref, *rest):
        attn_refs, gate_refs = rest[:n_a], rest[n_a:n_a + n_g]
        dp_ref, dlb_ref, dnw_ref, dstate, do_s, dG_s, dgl_s, dk_s, dlb_s = rest[n_a + n_g:]
        dq_ref, df_ref, di_ref, dog_ref = (dp_ref.at[:, cols] for cols in own)
        step = pl.program_id(0)
        for k, a_ref in enumerate(attn_refs):
            dp_ref[:, k * GROUP_W:(k + 1) * GROUP_W] = _bf(a_ref[...])
        for k, g_ref in enumerate(gate_refs):
            dp_ref[:, QKV_W + 4 * HG_W + k * D_MODEL:QKV_W + 4 * HG_W + (k + 1) * D_MODEL] = g_ref[...]

        @pl.when(step == 0)
        def _():
            dstate[...] = jnp.zeros_like(dstate)
            dlb_s[...] = jnp.zeros_like(dlb_s)
            dnw_ref[...] = jnp.zeros_like(dnw_ref)

        lb = _lower_bound(lb_ref[...])
        qr = q_ref[...]
        q, sq, f, sf = _hg_gates(qr, f_ref[...], lb)
        k = 1.0 - f
        G = _chunk_cumsum(jnp.log(f))
        nwv = nw_ref[...]
        row = lax.broadcasted_iota(jnp.int32, (HG_CHUNK, HG_CHUNK), 0)
        col = lax.broadcasted_iota(jnp.int32, (HG_CHUNK, HG_CHUNK), 1)
        for h in range(HG_HEADS):
            hs = slice(h * HG_DK, (h + 1) * HG_DK)
            oh = o_ref[:, hs]
            r = _rinv(oh)
            ohat = oh * r
            og = og_ref[:, hs]
            sg = _sigmoid(og)
            dyh = dy_ref[:, hs]
            don = dyh * (og * sg)
            dog_ref[:, hs] = _bf(dyh * (ohat * nwv) * (sg * (1.0 + og * (1.0 - sg))))
            dnw_ref[...] += jnp.sum(don * ohat, axis=0, keepdims=True)
            do_s[:, hs] = _norm_bwd(don, ohat, r, nwv)
        dsts = [dstate[h] for h in range(HG_HEADS)]
        for c in reversed(range(ncs)):
            cs = slice(c * HG_CHUNK, (c + 1) * HG_CHUNK)
            for h in range(HG_HEADS):
                hs = slice(h * HG_DK, (h + 1) * HG_DK)
                dst = dsts[h]
                Gc = G[cs, hs]
                gl = Gc[HG_CHUNK - 1:HG_CHUNK]
                eG, enG, edG, egl = jnp.exp(Gc), jnp.exp(-Gc), jnp.exp(gl - Gc), jnp.exp(gl)
                qt, kt, kd = q[cs, hs] * eG, k[cs, hs] * enG, k[cs, hs] * edG
                qtb, ktb, kdb = _bf(qt), _bf(kt), _bf(kd)
                v = _bf(i_ref[cs, hs])
                do = _bf(do_s[cs, hs])
                st = st_ref[c, h]
                dstb = _bf(dst)
                A = jnp.where(row >= col, _dot(qtb, ktb, NT), 0.0)
                dA = _bf(jnp.where(row >= col, _dot(do, v, NT), 0.0))
                di_ref[cs, hs] = _bf(_dot(_bf(A), do, TN) + _dot(kdb, dstb, NT))
                dqt = _dot(dA, ktb, NN) + _dot(do, _bf(st), NN)
                dkt = _dot(dA, qtb, TN)
                dkd = _dot(v, dstb, NN)
                dgl = egl * jnp.sum(st * dst, axis=0, keepdims=True) + jnp.sum(dkd * kd, axis=0, keepdims=True)
                dsts[h] = dst * egl + _dot(do, qtb, TN)
                dq_ref[cs, hs] = _bf(dqt * eG * (sq[cs, hs] * (1.0 + qr[cs, hs] * (1.0 - sq[cs, hs]))))
                dk_s[cs, hs] = dkt * enG + dkd * edG
                dG_s[cs, hs] = dqt * qt - dkt * kt - dkd * kd
                dgl_s[cs, hs] = jnp.broadcast_to(dgl, (HG_CHUNK, HG_DK))
        for h in range(HG_HEADS):
            dstate[h] = dsts[h]
        dg = _chunk_rev_cumsum(dG_s[...]) + dgl_s[...]
        dfv = dg / f - dk_s[...]
        df_ref[...] = _bf(dfv * (1.0 - lb) * sf * (1.0 - sf))
        dlb_s[...] += jnp.sum(dfv * (1.0 - sf), axis=0, keepdims=True)

        @pl.when(step == nt - 1)
        def _():
            t = dlb_s[...] * lb * (1.0 - lb)
            dlb_ref[...] = jnp.concatenate([t, -t], axis=0)

    def colspec(j):
        return pl.BlockSpec((HG_TILE, HG_W), lambda i: (nt - 1 - i, HG_COL0 + j))

    def rows(width):
        return pl.BlockSpec((HG_TILE, width), lambda i: (nt - 1 - i, 0))

    tile = rows(HG_W)
    return pl.pallas_call(
        body, name=name, grid=(nt,),
        in_specs=[colspec(0), colspec(1), colspec(2), colspec(3),
                  pl.BlockSpec((2, HG_W), lambda i: (0, 0)), pl.BlockSpec((1, HG_DK), lambda i: (0, 0)),
                  tile, pl.BlockSpec((ncs, HG_HEADS, HG_DK, HG_DK), lambda i: (nt - 1 - i, 0, 0, 0)), tile]
        + [rows(GROUP_W)] * n_a + [rows(D_MODEL)] * n_g,
        out_specs=[rows(IN_W), pl.BlockSpec((2, HG_W), lambda i: (0, 0)), pl.BlockSpec((1, HG_DK), lambda i: (0, 0))],
        out_shape=[_sds((S, IN_W), BF16), _sds((2, HG_W), F32), _sds((1, HG_DK), F32)],
        scratch_shapes=[pltpu.VMEM((HG_HEADS, HG_DK, HG_DK), F32)] + [pltpu.VMEM((HG_TILE, HG_W), F32)] * 4
        + [pltpu.VMEM((1, HG_W), F32)],
        compiler_params=_cp(1))(proj, proj, proj, proj, lb_raw, nw, o, states, dy, *d_attn, *d_gates)


GATE_COL0 = (QKV_W + 4 * HG_W) // GROUP_W
HALF_D = D_MODEL // 2


def _gate_tiles(proj):
    return [_tile(proj, HALF_D, functools.partial(lambda c, k: GATE_COL0 + k, k=k)) for k in range(4)]


def _gates(g_refs):
    s0 = _sigmoid(jnp.concatenate([g_refs[0][...], g_refs[1][...]], axis=1))
    s1 = _sigmoid(jnp.concatenate([g_refs[2][...], g_refs[3][...]], axis=1))
    return s0, s1


def _branch_fwd(os_, lses, yh, proj, w_a, w_h, name, plans=None):
    nb = w_a.shape[0]

    def body(o0, o1, o2, l0, l1, l2, yh_ref, g0a, g0b, g1a, g1b, wa_ref, wh_ref,
             y_ref, lse_ref, za_ref, zh_ref, m_ref):
        a, b, c = l0[...], l1[...], l2[...]
        m = jnp.maximum(jnp.maximum(a, b), c)
        ea, eb, ec = jnp.exp(a - m), jnp.exp(b - m), jnp.exp(c - m)
        den = ea + eb + ec
        y = (ea * o0[...] + eb * o1[...] + ec * o2[...]) / den
        y_ref[...] = y
        lse_ref[...] = m + jnp.log(den)
        yb, yhb = _bf(y), _bf(yh_ref[...])
        za = jnp.concatenate([_dot(yb, wa_ref[j], NN) for j in range(nb)], axis=1)
        zh = jnp.concatenate([_dot(yhb, wh_ref[j], NN) for j in range(nb)], axis=1)
        s0, s1 = _gates((g0a, g0b, g1a, g1b))
        za_ref[...] = za
        zh_ref[...] = zh
        m_ref[...] = _bf(s0 * za + s1 * zh)

    return _rows_call(name, body, yh.shape[0], 512, 1,
                      [*[_tile(t, GROUP_W) for t in (*os_, *lses)], _tile(yh, HG_W), *_gate_tiles(proj),
                       _full(w_a), _full(w_h)],
                      [_out_tile(GROUP_W, F32, GROUP_W)] * 2 + [_out_tile(D_MODEL, F32, D_MODEL)] * 2
                      + [_out_tile(D_MODEL, BF16, D_MODEL)], plans)


def _branch_bwd(dm, za, zh, proj, w_a, w_h, name, plans=None):
    nb, _, Nb = w_a.shape

    def body(dm_ref, za_ref, zh_ref, g0a, g0b, g1a, g1b, wa_ref, wh_ref,
             dza_ref, dzh_ref, dg0_ref, dg1_ref, dy_ref, dyh_ref):
        dmv = dm_ref[...]
        s0, s1 = _gates((g0a, g0b, g1a, g1b))
        dza, dzh = _bf(dmv * s0), _bf(dmv * s1)
        dza_ref[...] = dza
        dzh_ref[...] = dzh
        dg0_ref[...] = _bf(dmv * za_ref[...] * s0 * (1.0 - s0))
        dg1_ref[...] = _bf(dmv * zh_ref[...] * s1 * (1.0 - s1))
        dy_ref[...] = sum(_dot(dza[:, j * Nb:(j + 1) * Nb], wa_ref[j], NT) for j in range(nb))
        dyh_ref[...] = sum(_dot(dzh[:, j * Nb:(j + 1) * Nb], wh_ref[j], NT) for j in range(nb))

    return _rows_call(name, body, za.shape[0], 512, 1,
                      [_tile(dm, D_MODEL), _tile(za, D_MODEL), _tile(zh, D_MODEL), *_gate_tiles(proj),
                       _full(w_a), _full(w_h)],
                      [_out_tile(D_MODEL, BF16, D_MODEL)] * 4 + [_out_tile(GROUP_W, F32, GROUP_W),
                                                                 _out_tile(HG_W, F32, HG_W)], plans)


def _mix_out(merged, w_out, x, w_post, w_pre, name):
    def body(m_ref, wo_ref, x_ref, wp_ref, wf_ref, mo_ref, x1_ref, h2_ref):
        z = _dot(m_ref[...], wo_ref[...], NN)
        mo_ref[...] = z
        x1 = x_ref[...] + z * _rinv(z) * wp_ref[...]
        x1_ref[...] = x1
        h2_ref[...] = _bf(x1 * _rinv(x1) * wf_ref[...])

    return _rows_call(name, body, x.shape[0], 512, 1,
                      [_tile(merged, D_MODEL), _full(w_out), _tile(x, D_MODEL), _full(w_post), _full(w_pre)],
                      [_out_tile(D_MODEL, F32, D_MODEL), _out_tile(D_MODEL, F32, D_MODEL),
                       _out_tile(D_MODEL, BF16, D_MODEL)])


def _loss_head(a, w_down, x1, tgt, w, name):
    def body(a_ref, wd_ref, x1_ref, t_ref, w_ref, dx_ref, df_ref, dw_ref, loss_ref):
        z = _dot(a_ref[...], wd_ref[...], NN)
        r = _rinv(z)
        zhat = z * r
        wv = w_ref[...]
        e = x1_ref[...] + zhat * wv - t_ref[...]
        dx = e * (1.0 / D_MODEL)
        dx_ref[...] = dx
        df_ref[...] = _bf(_norm_bwd(dx, zhat, r, wv))
        _acc(dw_ref, jnp.sum(dx * zhat, axis=0, keepdims=True))
        part = 0.5 * jnp.sum(jnp.sum(e * e, axis=1, keepdims=True), axis=0, keepdims=True) * (1.0 / D_MODEL)
        _acc(loss_ref, jnp.broadcast_to(part, (1, LANES)))

    return _rows_call(name, body, x1.shape[0], 512, 1,
                      [_tile(a, D_FF), _full(w_down), _tile(x1, D_MODEL), _tile(tgt, D_MODEL), _full(w)],
                      [_out_tile(D_MODEL, F32, D_MODEL), _out_tile(D_MODEL, BF16, D_MODEL),
                       _out_acc(1, D_MODEL, D_MODEL), _out_acc(1, LANES, LANES)])


CONV_CB = D_FF // 2
CONV_TM = 512
HALO = 8
SQRT_HALF = 0.7071067811865476
INV_SQRT_2PI = 0.3989422804014327


CONV_RS = 32


def _lane_tiles():
    return [slice(k * LANES, (k + 1) * LANES) for k in range(CONV_CB // LANES)]


def _strip_start(i):
    return pl.multiple_of(i * CONV_RS, CONV_RS)


def _strip_taps(u_ref, halo_ref, r0, cs, first_strip, first_tile):
    if first_strip:
        before = jnp.where(first_tile, 0.0, halo_ref[:, cs])
        blk = jnp.concatenate([before, u_ref[0:CONV_RS, cs]], axis=0)
    else:
        blk = u_ref[pl.ds(pl.multiple_of(r0 - HALO, HALO), CONV_RS + HALO), cs]
    return pltpu.roll(blk, 2, 0)[HALO:], pltpu.roll(blk, 1, 0)[HALO:], blk[HALO:]


def _conv(taps, w_ref, b_ref, cs):
    return b_ref[:, cs] + w_ref[0:1, cs] * taps[0] + w_ref[1:2, cs] * taps[1] + w_ref[2:3, cs] * taps[2]


def _conv_specs(tm):
    nh = tm // HALO
    nc = D_FF // CONV_CB

    def tile(off):
        return pl.BlockSpec((tm, CONV_CB), lambda c, i: (i, off + c))

    def halo(off):
        return pl.BlockSpec((HALO, CONV_CB), lambda c, i: (jnp.maximum(i * nh - 1, 0), off + c))

    def small(rows, off):
        return pl.BlockSpec((rows, CONV_CB), lambda c, i: (0, off + c))

    return nc, tile, halo, small


def _conv_gelu_fwd(u, cw, cb, name, plans=None):
    S = u.shape[0]
    tm = CONV_TM
    nc, tile, halo, small = _conv_specs(tm)

    def body(ug, hg, uv, hv, wg, wv, bg, bv, a_ref):
        first_tile = pl.program_id(1) == 0

        def strip(r0, first_strip):
            for cs in _lane_tiles():
                cg = _conv(_strip_taps(ug, hg, r0, cs, first_strip, first_tile), wg, bg, cs)
                cv = _conv(_strip_taps(uv, hv, r0, cs, first_strip, first_tile), wv, bv, cs)
                a_ref[pl.ds(r0, CONV_RS), cs] = _bf(0.5 * cg * (1.0 + lax.erf(cg * SQRT_HALF)) * cv)

        strip(0, True)
        lax.fori_loop(1, tm // CONV_RS, lambda k, c: (strip(_strip_start(k), False), c)[1], 0)

    (a,), carried = _call(
        body, plans, name=name, grid=(nc, S // tm),
        in_specs=[tile(0), halo(0), tile(nc), halo(nc), small(3, 0), small(3, nc), small(1, 0), small(1, nc)],
        out_specs=[tile(0)], out_shape=[_sds((S, D_FF), BF16)], args=(u, u, u, u, cw, cw, cb, cb))
    return a if plans is None else (a, carried)


def _conv_gelu_bwd(u, dff, w_down, cw, cb, name, plans=None):
    S = u.shape[0]
    tm = CONV_TM
    nt = S // tm
    nc, tile, halo, small = _conv_specs(tm)

    def body(ug, hg, uv, hv, wg, wv, bg, bv, dff_ref, wd_ref, dcg_ref, dcv_ref, dwg_ref, dwv_ref, dbg_ref, dbv_ref,
             acc, da_ref):
        i = pl.program_id(1)
        first_tile = i == 0
        da_ref[...] = _dot(dff_ref[...], wd_ref[...], NT)

        @pl.when(first_tile)
        def _():
            acc[...] = jnp.zeros_like(acc)

        def strip(r0, first_strip):
            rows = pl.ds(r0, CONV_RS)
            for cs in _lane_tiles():
                tg = _strip_taps(ug, hg, r0, cs, first_strip, first_tile)
                tv = _strip_taps(uv, hv, r0, cs, first_strip, first_tile)
                cg = _conv(tg, wg, bg, cs)
                cv = _conv(tv, wv, bv, cs)
                phi = 0.5 * (1.0 + lax.erf(cg * SQRT_HALF))
                dav = da_ref[rows, cs]
                dcg = dav * cv * (phi + cg * jnp.exp(-0.5 * cg * cg) * INV_SQRT_2PI)
                dcv = dav * (cg * phi)
                dcg_ref[rows, cs] = dcg
                dcv_ref[rows, cs] = dcv
                for half, (dc, taps) in enumerate(((dcg, tg), (dcv, tv))):
                    for j in range(3):
                        acc[4 * half + j, :, cs] += dc * taps[j]
                    acc[4 * half + 3, :, cs] += dc

        strip(0, True)
        lax.fori_loop(1, tm // CONV_RS, lambda k, c: (strip(_strip_start(k), False), c)[1], 0)

        @pl.when(i == nt - 1)
        def _():
            for half, (dw_ref, db_ref) in enumerate(((dwg_ref, dbg_ref), (dwv_ref, dbv_ref))):
                for j in range(3):
                    dw_ref[j:j + 1, :] = jnp.sum(acc[4 * half + j], axis=0, keepdims=True)
                db_ref[...] = jnp.sum(acc[4 * half + 3], axis=0, keepdims=True)

    res, carried = _call(
        body, plans, name=name, grid=(nc, nt),
        in_specs=[tile(0), halo(0), tile(nc), halo(nc), small(3, 0), small(3, nc), small(1, 0), small(1, nc),
                  pl.BlockSpec((tm, D_MODEL), lambda c, i: (i, 0)), pl.BlockSpec((CONV_CB, D_MODEL), lambda c, i: (c, 0))],
        out_specs=[tile(0), tile(0), small(3, 0), small(3, 0), small(1, 0), small(1, 0)],
        out_shape=[_sds((S, D_FF), F32)] * 2 + [_sds((3, D_FF), F32)] * 2 + [_sds((1, D_FF), F32)] * 2,
        scratch_shapes=[pltpu.VMEM((8, CONV_RS, CONV_CB), F32), pltpu.VMEM((tm, CONV_CB), F32)],
        args=(u, u, u, u, cw, cw, cb, cb, dff, w_down))
    return res if plans is None else (res, carried)


def _conv_input_bwd(dcg, dcv, cw, name, plans=None):
    S = dcg.shape[0]
    tm = CONV_TM // 2
    nh = tm // HALO
    nt = S // tm
    n = CONV_RS + HALO
    tile = pl.BlockSpec((tm, D_FF), lambda i: (i, 0))
    nxt = pl.BlockSpec((HALO, D_FF), lambda i: (jnp.minimum((i + 1) * nh, S // HALO - 1), 0))

    def body(g_ref, ng_ref, v_ref, nv_ref, w_ref, du_ref):
        last_tile = pl.program_id(0) == nt - 1

        def strip(r0, last_strip):
            for half, (dc_ref, n_ref) in enumerate(((g_ref, ng_ref), (v_ref, nv_ref))):
                for k in range(D_FF // LANES):
                    cs = slice(k * LANES, (k + 1) * LANES)
                    ws = slice(half * D_FF + k * LANES, half * D_FF + (k + 1) * LANES)
                    if last_strip:
                        after = jnp.where(last_tile, 0.0, n_ref[:, cs])
                        blk = jnp.concatenate([dc_ref[tm - CONV_RS:tm, cs], after], axis=0)
                    else:
                        blk = dc_ref[pl.ds(r0, n), cs]
                    d1 = pltpu.roll(blk, n - 1, 0)[:CONV_RS]
                    d2 = pltpu.roll(blk, n - 2, 0)[:CONV_RS]
                    du_ref[pl.ds(r0, CONV_RS), ws] = _bf(w_ref[2:3, ws] * blk[:CONV_RS] + w_ref[1:2, ws] * d1
                                                         + w_ref[0:1, ws] * d2)

        lax.fori_loop(0, tm // CONV_RS - 1, lambda k, c: (strip(_strip_start(k), False), c)[1], 0)
        strip(tm - CONV_RS, True)

    (du,), carried = _call(
        body, plans, name=name, grid=(nt,),
        in_specs=[tile, nxt, tile, nxt, pl.BlockSpec((3, 2 * D_FF), lambda i: (0, 0))],
        out_specs=[pl.BlockSpec((tm, 2 * D_FF), lambda i: (i, 0))], out_shape=[_sds((S, 2 * D_FF), BF16)],
        args=(dcg, dcg, dcv, dcv, cw))
    return du if plans is None else (du, carried)


def _row_tile(n, cap):
    best = n
    for t in range(16, cap + 1, 16):
        if n % t == 0:
            best = t
    return best if best <= cap else n


def _rows_for_bytes(nbytes, cols):
    return max(16, nbytes // (4 * cols) // 16 * 16)


def _adamw_update(w_ref, g_ref, m_ref, v_ref, d_ref, nm_ref, nv_ref):
    gv = g_ref[...]
    nm = ADAM_B1 * m_ref[...] + (1.0 - ADAM_B1) * gv
    nv = ADAM_B2 * v_ref[...] + (1.0 - ADAM_B2) * (gv * gv)
    m_hat = nm / (1.0 - ADAM_B1 ** ADAM_STEP)
    v_hat = nv / (1.0 - ADAM_B2 ** ADAM_STEP)
    d_ref[...] = -ADAM_LR * (m_hat / (jnp.sqrt(v_hat) + ADAM_EPS) + ADAM_WD * w_ref[...])
    nm_ref[...] = nm
    nv_ref[...] = nv


def _adamw(w, g, m, v, name):
    R, C = w.shape
    tr = _row_tile(R, _rows_for_bytes(2 << 20, C))
    spec = pl.BlockSpec((tr, C), lambda i: (i, 0))
    body = functools.partial(_adamw_update)
    return pl.pallas_call(body, name=name, grid=(R // tr,), in_specs=[spec] * 4, out_specs=[spec] * 3,
                          out_shape=[_sds((R, C), F32)] * 3, compiler_params=_cp(1))(w, g, m, v)


def _adamw_small(ws, gs, ms, vs, name):
    n = len(ws)

    def body(*refs):
        ins, outs = refs[:4 * n], refs[4 * n:]
        for k in range(n):
            _adamw_update(ins[k], ins[n + k], ins[2 * n + k], ins[3 * n + k], outs[3 * k], outs[3 * k + 1], outs[3 * k + 2])

    vm = pl.BlockSpec(memory_space=pltpu.VMEM)
    outs = pl.pallas_call(body, name=name, in_specs=[vm] * (4 * n), out_specs=[vm] * (3 * n),
                          out_shape=[_sds(w.shape, F32) for w in ws for _ in range(3)])(*ws, *gs, *ms, *vs)
    return [tuple(outs[3 * k:3 * k + 3]) for k in range(n)]


def _pair_sum(gfull, rcv, c_idx, name):
    nb, R, C = gfull.shape
    half = R // 2
    tr = _row_tile(half, _rows_for_bytes(2 << 20, C))
    nt = half // tr

    def body(c_ref, g_ref, r_ref, o_ref):
        o_ref[...] = _bf(g_ref[...] + r_ref[...])

    return pl.pallas_call(
        body, name=name,
        grid_spec=pltpu.PrefetchScalarGridSpec(
            num_scalar_prefetch=1, grid=(nb, nt),
            in_specs=[pl.BlockSpec((None, tr, C), lambda j, i, c_ref: (j, c_ref[0] * nt + i, 0)),
                      pl.BlockSpec((None, tr, C), lambda j, i, c_ref: (j, i, 0))],
            out_specs=pl.BlockSpec((None, tr, C), lambda j, i, c_ref: (j, i, 0))),
        out_shape=_sds((nb, half, C), BF16), compiler_params=_cp(2))(c_idx, gfull, rcv)


def _chip_sum(arrived, own, place, name):
    nb, H, C = arrived.shape
    tr = _row_tile(H, _rows_for_bytes(2 << 20, C))
    nt = H // tr

    def body(pl_ref, *refs):
        o_ref = refs[nb + 1]
        me = pl_ref[0]
        acc = None
        for k in range(nb):
            term = jnp.where(me == k, refs[nb][...], refs[k][...]).astype(F32)
            acc = term if acc is None else acc + term
        o_ref[...] = acc

    def other(k):
        return pl.BlockSpec((None, tr, C), lambda i, p: (jnp.where(p[0] == k, (k + 1) % nb, k), i, 0))

    return pl.pallas_call(
        body, name=name,
        grid_spec=pltpu.PrefetchScalarGridSpec(
            num_scalar_prefetch=1, grid=(nt,),
            in_specs=[other(k) for k in range(nb)] + [pl.BlockSpec((None, tr, C), lambda i, p: (p[0], i, 0))],
            out_specs=pl.BlockSpec((tr, C), lambda i, p: (p[1] * nt + i, 0))),
        out_shape=_sds((2 * H, C), F32), compiler_params=_cp(1))(place, *([arrived] * nb), own)


def _cast_into_slot(shard, place, name):
    R, C = shard.shape
    tr = _row_tile(R, 256)

    def body(pl_ref, s_ref, o_ref):
        o_ref[...] = _bf(s_ref[...])

    return pl.pallas_call(
        body, name=name,
        grid_spec=pltpu.PrefetchScalarGridSpec(
            num_scalar_prefetch=1, grid=(R // tr,),
            in_specs=[pl.BlockSpec((tr, C), lambda i, p: (i, 0))],
            out_specs=pl.BlockSpec((None, tr, C), lambda i, p: (p[0], i, 0))),
        out_shape=_sds((N_CHIPS, R, C), BF16), compiler_params=_cp(1))(place, shard)


def _place():
    x, y, c = lax.axis_index("x"), lax.axis_index("y"), lax.axis_index("c")
    chips = [(1 - x, y), (x, 1 - y), (1 - x, 1 - y)]
    return x, y, c, chips


def _chip_id(px, py):
    return 2 * px + py


def _remote(src, dst, send_sems, recv_sems, k, to):
    return pltpu.make_async_remote_copy(src_ref=src, dst_ref=dst, send_sem=send_sems.at[k], recv_sem=recv_sems.at[k],
                                        device_id=to, device_id_type=MESH)


def _proj_gathered(x, w_norm, slot, place, name, tm=1024, plan=None):
    M, K = x.shape
    nb, _, Nb = slot.shape
    half = K // 2
    nt = M // tm
    cx, cy = place[0] // 2, place[0] % 2
    order = jnp.stack([place[0], _chip_id(1 - cx, cy), _chip_id(cx, 1 - cy), _chip_id(1 - cx, 1 - cy)]).astype(jnp.int32)

    p_in = [] if plan is None else plan.ins + plan.inouts
    p_out = [] if plan is None else [_sds(a.shape, a.dtype) for a in plan.inouts] + plan.outs
    n_pi, n_po = len(p_in), len(p_out)

    def body(order_ref, x_ref, wn_ref, slot_in, *refs):
        o_ref, slot_ref, h_out = refs[n_pi:n_pi + 3]
        s0 = n_pi + 3 + n_po
        w_buf, hs, ici_send, ici_recv, pass_send, pass_recv, load_sem = refs[s0:s0 + 7]

        def carried():
            if plan is None:
                return [], [], []
            ins = refs[:len(plan.ins)]
            outs = refs[n_pi + 3:n_pi + 3 + n_po]
            return plan.copies(ins, outs[:len(plan.inouts)], outs[len(plan.inouts):], *refs[s0 + 7:])

        b, i = pl.program_id(0), pl.program_id(1)
        x, y, c, chips = _place()
        me = _chip_id(x, y)
        sib = (x, y, 1 - c)
        mine, other = pl.ds(c * half, half), pl.ds((1 - c) * half, half)

        def sent(k):
            blk = slot_ref.at[me, mine]
            return _remote(blk, blk, ici_send, ici_recv, k, (*chips[k], c))

        def landed(k):
            blk = slot_ref.at[_chip_id(*chips[k]), mine]
            return _remote(blk, blk, ici_send, ici_recv, k, (*chips[k], c))

        def passed(k, rows):
            blk = slot_ref.at[_chip_id(*chips[k]), rows]
            return _remote(blk, blk, pass_send, pass_recv, k, sib)

        @pl.when((b == 0) & (i == 0))
        def _():
            for k in range(len(chips)):
                sent(k).start()
            sends, _, local = carried()
            for cp in (*sends, *local):
                cp.start()

        for k in range(len(chips)):
            @pl.when((b == k + 1) & (i == 0))
            def _(k=k):
                landed(k).wait_recv()
                passed(k, mine).start()
                passed(k, other).wait_recv()

        @pl.when(i == 0)
        def _():
            load = pltpu.make_async_copy(slot_ref.at[order_ref[b]], w_buf, load_sem.at[0])
            load.start()
            load.wait()

        rows = pl.ds(pl.multiple_of(i * tm, tm), tm)
        keep_h = pltpu.make_async_copy(hs, h_out, load_sem.at[1])

        @pl.when(b == 0)
        def _():
            xv = x_ref[...]
            hs[rows, :] = _bf(xv * _rinv(xv) * wn_ref[...])

        @pl.when((b == 1) & (i == 0))
        def _():
            keep_h.start()

        o_ref[...] = _dot(hs[rows, :], w_buf[...], NN)

        @pl.when((b == nb - 1) & (i == nt - 1))
        def _():
            for k in range(len(chips)):
                sent(k).wait_send()
                passed(k, mine).wait_send()
            sends, recvs, local = carried()
            for cp in recvs:
                cp.wait_recv()
            for cp in sends:
                cp.wait_send()
            for cp in local:
                cp.wait()
            keep_h.wait()

    n_peers = N_CHIPS - 1
    return pl.pallas_call(
        body, name=name,
        grid_spec=pltpu.PrefetchScalarGridSpec(
            num_scalar_prefetch=1, grid=(nb, nt),
            in_specs=[pl.BlockSpec((tm, K), lambda b, i, o: (i, 0)), pl.BlockSpec((1, K), lambda b, i, o: (0, 0)), ANY]
            + [ANY] * n_pi,
            out_specs=[pl.BlockSpec((tm, Nb), lambda b, i, o: (i, o[b])), ANY, ANY] + [ANY] * n_po,
            scratch_shapes=[pltpu.VMEM((K, Nb), BF16), pltpu.VMEM((M, K), BF16)]
            + [pltpu.SemaphoreType.DMA((n_peers,))] * 4 + [pltpu.SemaphoreType.DMA((2,))]
            + ([] if plan is None else [pltpu.SemaphoreType.DMA((plan.n_sems,))] * 3)),
        out_shape=[_sds((M, nb * Nb), F32), _sds(slot.shape, slot.dtype), _sds((M, K), BF16)] + p_out,
        input_output_aliases={3: 1, **({} if plan is None else
                                       {4 + len(plan.ins) + a: 3 + a for a in range(len(plan.inouts))})},
        compiler_params=_cp(2))(order, x, w_norm, slot, *p_in)


def _gather_ici_plan(slots, wholes, part=None):
    ns, nw = len(slots), len(wholes)

    def copies(ins, ios, outs, send_sems, recv_sems, local_sems):
        x, y, c, chips = _place()
        me = _chip_id(x, y)
        sends, recvs = [], []
        for a in range(ns + nw):
            dst = ios[a] if a < ns else outs[a - ns]
            R = dst.shape[1]
            r0, nr = (0, R // 2) if part is None else part
            rows = pl.ds(c * (R // 2) + r0, nr) if a < ns else pl.ds(0, R)
            src = dst.at[me, rows] if a < ns else ins[a - ns]
            for j, chip in enumerate(chips):
                sends.append(_remote(src, dst.at[me, rows], send_sems, recv_sems, 3 * a + j, (*chip, c)))
                landed = dst.at[_chip_id(*chip), rows]
                recvs.append(_remote(landed, landed, send_sems, recv_sems, 3 * a + j, (*chip, c)))
        local = [pltpu.make_async_copy(ins[b], outs[b].at[me], local_sems.at[b]) for b in range(nw)]
        return sends, recvs, local

    return _Plan(copies, 3 * (ns + nw), ins=wholes, inouts=slots,
                 outs=[_sds((N_CHIPS, *s.shape), s.dtype) for s in wholes])


def _gather_pass_plan(slots):
    def copies(ins, ios, outs, send_sems, recv_sems, local_sems):
        x, y, c, chips = _place()
        sib = (x, y, 1 - c)
        sends, recvs = [], []
        for a, buf in enumerate(ios):
            half = buf.shape[1] // 2
            for j, chip in enumerate(chips):
                mine = buf.at[_chip_id(*chip), pl.ds(c * half, half)]
                other = buf.at[_chip_id(*chip), pl.ds((1 - c) * half, half)]
                sends.append(_remote(mine, mine, send_sems, recv_sems, 3 * a + j, sib))
                recvs.append(_remote(other, other, send_sems, recv_sems, 3 * a + j, sib))
        return sends, recvs, []

    return _Plan(copies, 3 * len(slots), inouts=slots)


def _pair_plan(grads):
    def copies(ins, ios, outs, send_sems, recv_sems, local_sems):
        x, y, c, _ = _place()
        sib = (x, y, 1 - c)
        sends, recvs = [], []
        for a, g in enumerate(ins):
            half = g.shape[1] // 2
            sends.append(_remote(g.at[:, pl.ds((1 - c) * half, half), :], outs[a], send_sems, recv_sems, a, sib))
            recvs.append(_remote(outs[a], outs[a], send_sems, recv_sems, a, sib))
        return sends, recvs, []

    return _Plan(copies, len(grads), ins=grads,
                 outs=[_sds((g.shape[0], g.shape[1] // 2, g.shape[2]), g.dtype) for g in grads])


def _chip_plan(parts):
    def copies(ins, ios, outs, send_sems, recv_sems, local_sems):
        x, y, c, chips = _place()
        me = _chip_id(x, y)
        sends, recvs = [], []
        for a, part in enumerate(ins):
            for j, chip in enumerate(chips):
                sends.append(_remote(part.at[_chip_id(*chip)], outs[a].at[me], send_sems, recv_sems, 3 * a + j, (*chip, c)))
                landed = outs[a].at[_chip_id(*chip)]
                recvs.append(_remote(landed, landed, send_sems, recv_sems, 3 * a + j, (*chip, c)))
        return sends, recvs, []

    return _Plan(copies, 3 * len(parts), ins=parts, outs=[_sds(p.shape, p.dtype) for p in parts])


def _all_sum(pack, fulls, name):
    R, C = pack.shape
    n = len(fulls)

    def body(p_ref, *refs):
        o_ref, halves = refs[n], refs[n + 1:2 * n + 1]
        buf, send_sems, recv_sems, pair_send, pair_recv = refs[2 * n + 1:]
        x, y, c, _ = _place()
        sib = (x, y, 1 - c)
        pair = []
        for a, full in enumerate(halves):
            H = full.shape[0] // 2
            mine = full.at[pl.ds(c * H, H)]
            cp = _remote(mine, mine, pair_send, pair_recv, a, sib)
            cp.start()
            pair.append(cp)
        me = 4 * x + 2 * y + c
        buf[me] = p_ref[...]
        cps = []
        for k in range(1, N_DEV):
            to = (x ^ (k >> 2), y ^ ((k >> 1) & 1), c ^ (k & 1))
            cp = _remote(p_ref, buf.at[me], send_sems, recv_sems, k - 1, to)
            cp.start()
            cps.append(cp)
        for k in range(1, N_DEV):
            frm = (x ^ (k >> 2), y ^ ((k >> 1) & 1), c ^ (k & 1))
            slot = buf.at[4 * frm[0] + 2 * frm[1] + frm[2]]
            _remote(slot, slot, send_sems, recv_sems, k - 1, frm).wait_recv()
        acc = buf[0]
        for k in range(1, N_DEV):
            acc = acc + buf[k]
        o_ref[...] = acc
        for cp in cps:
            cp.wait_send()
        for a, (full, cp) in enumerate(zip(halves, pair)):
            H = full.shape[0] // 2
            other = full.at[pl.ds((1 - c) * H, H)]
            _remote(other, other, pair_send, pair_recv, a, sib).wait_recv()
            cp.wait_send()

    vm = pl.BlockSpec(memory_space=pltpu.VMEM)
    res = pl.pallas_call(
        body, name=name, in_specs=[vm] + [ANY] * n, out_specs=[vm] + [ANY] * n,
        out_shape=[_sds((R, C), F32)] + [_sds(f.shape, f.dtype) for f in fulls],
        input_output_aliases={1 + a: 1 + a for a in range(n)},
        scratch_shapes=[pltpu.VMEM((N_DEV, R, C), F32), pltpu.SemaphoreType.DMA((N_DEV - 1,)),
                        pltpu.SemaphoreType.DMA((N_DEV - 1,)), pltpu.SemaphoreType.DMA((n,)),
                        pltpu.SemaphoreType.DMA((n,))])(pack, *fulls)
    return res[0], list(res[1:])


def _local_step(xs, tgt, p, ex):
    proj, h1 = ex.project(xs, p["pre_mix_norm"])
    biases = _relbias_fwd(p["rel_bias"], "rel_bias_fwd")
    fw = []
    for g in range(N_GROUPS):
        res, got = _attn_fwd(proj, biases[g], g, f"attn_fwd{g}", plans=ex.carry(f"attn_fwd{g}"))
        ex.done(f"attn_fwd{g}", got)
        fw.append(res)
    (yh, o_h, states), got = _hgrn_fwd(proj, p["hgrn_lb_raw"], p["hgrn_norm"], "hgrn_fwd", plans=ex.carry("hgrn_fwd"))
    ex.done("hgrn_fwd", got)
    W_a, W_h, W_out = ex.weight("w_branch_attn"), ex.weight("w_branch_hgrn"), ex.weight("w_out")
    (y, lse, za, zh, merged), got = _branch_fwd([t[0] for t in fw], [t[1] for t in fw], yh, proj, W_a, W_h,
                                                "branch_fwd", plans=ex.carry("branch_fwd"))
    ex.done("branch_fwd", got)
    W_up, conv_w = ex.weight("w_up"), ex.weight("conv_w")
    mo, x1, h2 = _mix_out(merged, W_out, xs, p["post_mix_norm"], p["pre_ffn_norm"], "mix_out")
    u, got = _mm_nn_blk(h2, W_up, "ffn_up", tm=1024, plans=ex.carry("ffn_up"))
    ex.done("ffn_up", got)
    a, got = _conv_gelu_fwd(u, conv_w, p["conv_b"], "conv_gelu_fwd", plans=ex.carry("conv_gelu_fwd"))
    ex.done("conv_gelu_fwd", got)
    W_down = ex.weight("w_down")
    dx2, dff, g_post_ffn, loss = _loss_head(a, W_down, x1, tgt, p["post_ffn_norm"], "ffn_down_loss")

    ex.grad("w_down", _mm_tn(a, dff, "g_w_down").reshape(N_CHIPS, D_FF // N_CHIPS, D_MODEL))
    (dcg, dcv, gwg, gwv, gbg, gbv), got = _conv_gelu_bwd(u, dff, W_down, conv_w, p["conv_b"], "conv_gelu_bwd",
                                                          plans=ex.carry("conv_gelu_bwd"))
    ex.done("conv_gelu_bwd", got)
    g_conv_w = jnp.concatenate([gwg, gwv], axis=1)
    g_conv_b = jnp.concatenate([gbg, gbv], axis=1)
    du, got = _conv_input_bwd(dcg, dcv, conv_w, "conv_input_bwd", plans=ex.carry("conv_input_bwd"))
    ex.done("conv_input_bwd", got)
    dx1, g_pre_ffn = _mm_nt_prenorm_bwd(du, W_up, x1, p["pre_ffn_norm"], dx2, "d_ffn_in")
    ex.grad("w_up", _mm_tn_blk(h2, du, N_CHIPS, "g_w_up"))
    (dmo, dmerged, g_post_mix), got = _postnorm_bwd(dx1, mo, p["post_mix_norm"], W_out, "post_mix_norm_bwd",
                                                    plans=ex.carry("post_mix_norm_bwd"))
    ex.done("post_mix_norm_bwd", got)
    ex.grad("w_out", _mm_tn(merged, dmo, "g_w_out").reshape(N_CHIPS, D_MODEL // N_CHIPS, D_MODEL))
    (dza, dzh, dg0, dg1, dy, dyh), got = _branch_bwd(dmerged, za, zh, proj, W_a, W_h, "branch_bwd",
                                                     plans=ex.carry("branch_bwd"))
    ex.done("branch_bwd", got)
    ex.grad("w_branch_attn", _mm_tn_blk(y, dza, N_CHIPS, "g_w_branch_attn", together=True))
    ex.grad("w_branch_hgrn", _mm_tn_blk(yh, dzh, N_CHIPS, "g_w_branch_hgrn", together=True))
    dqkv, dbs = [], []
    for g in range(N_GROUPS):
        parts, db, got = _attn_bwd(proj, biases[g], lse, y, dy, g, f"attn_bwd{g}", plans=ex.carry(f"attn_bwd{g}"))
        ex.done(f"attn_bwd{g}", got)
        dqkv += parts
        dbs.append(db)
    g_rel_bias = _relbias_bwd(dbs, "rel_bias_bwd")
    dproj, g_lb_raw, g_hgrn_norm = _hgrn_bwd(proj, p["hgrn_lb_raw"], p["hgrn_norm"], o_h, states, dyh, dqkv,
                                             [dg0, dg1], "hgrn_bwd")
    for piece in W_IN_PIECES:
        g, got = _mm_tn_blk(h1, dproj, N_CHIPS, f"g_{piece}", x_cols=W_IN_ROWS[piece],
                            plans=ex.carry(f"g_{piece}"))
        ex.done(f"g_{piece}", got)
        ex.grad(piece, g)
    dh1, got = _mm_nt_blk(dproj, ex.weight("w_in"), "d_proj_in", plans=ex.carry("d_proj_in"))
    ex.done("d_proj_in", got)
    (grad_x, g_pre_mix), got = _prenorm_bwd(dh1, xs, p["pre_mix_norm"], dx1, "pre_mix_norm_bwd",
                                            plans=ex.carry("pre_mix_norm_bwd"))
    ex.done("pre_mix_norm_bwd", got)
    small = dict(pre_mix_norm=g_pre_mix, rel_bias=g_rel_bias, hgrn_lb_raw=g_lb_raw, hgrn_norm=g_hgrn_norm,
                 post_mix_norm=g_post_mix, pre_ffn_norm=g_pre_ffn, conv_w=g_conv_w, conv_b=g_conv_b,
                 post_ffn_norm=g_post_ffn)
    return loss, grad_x, small


SMALL = ("pre_mix_norm", "rel_bias", "hgrn_lb_raw", "hgrn_norm", "post_mix_norm", "pre_ffn_norm", "conv_w", "conv_b",
         "post_ffn_norm")
BIG = ("w_in", "w_up", "w_down", "w_out", "w_branch_attn", "w_branch_hgrn")
WEIGHTS = ("pre_mix_norm", "w_in", "rel_bias", "hgrn_lb_raw", "hgrn_norm", "w_branch_attn", "w_branch_hgrn", "w_out",
           "post_mix_norm", "pre_ffn_norm", "w_up", "conv_w", "conv_b", "w_down", "post_ffn_norm")
MIXER = ("w_out", "w_branch_attn", "w_branch_hgrn")

ICI_PARTS = {"gather_ici_1of3": (0, 176), "gather_ici_2of3": (176, 176), "gather_ici_3of3": (352, 160)}
SCHEDULE = {
    "proj_in": [("gather_ici_cw", MIXER)],
    "attn_fwd0": [("gather_ici_1of3", ("w_up",))],
    "attn_fwd1": [("gather_ici_2of3", ("w_up",))],
    "attn_fwd2": [("gather_pass", MIXER), ("gather_ici_3of3", ("w_up",))],
    "hgrn_fwd": [("gather_pass", ("w_up",))],
    "ffn_up": [("gather_ici", ("w_down",))],
    "conv_gelu_fwd": [("gather_pass", ("w_down",))],
    "conv_gelu_bwd": [("pair", ("w_down",))],
    "conv_input_bwd": [("chip", ("w_down",))],
    "post_mix_norm_bwd": [("pair", ("w_up",))],
    "attn_bwd0": [("chip", ("w_up",)), ("pair", MIXER)],
    "attn_bwd1": [("chip", MIXER)],
    "g_w_in_b": [("pair", ("w_in_a",))],
    "d_proj_in": [("chip", ("w_in_a",)), ("pair", ("w_in_b",))],
    "pre_mix_norm_bwd": [("chip", ("w_in_b",))],
}
W_IN_ROWS = dict(w_in_a=(0, 768), w_in_b=(3, 256))
W_IN_PIECES = tuple(W_IN_ROWS)
REDUCED = W_IN_PIECES + BIG[1:]


class _Exchange:
    def __init__(self, place, slots, conv_w_shard):
        self.place, self.slots, self.conv_w_shard = place, dict(slots), conv_w_shard
        self.conv_w = None
        self.g, self.from_sibling, self.pair_sums, self.arrived = {}, {}, {}, {}
        self.pending = []

    def weight(self, name):
        if name == "conv_w":
            return self.conv_w
        w = self.slots[name]
        return w.reshape(-1, D_MODEL) if name in ("w_out", "w_down") else w

    def project(self, x, w_norm):
        (plan,) = self.carry("proj_in")
        proj, self.slots["w_in"], h, *got = _proj_gathered(x, w_norm, self.slots["w_in"], self.place, "proj_in",
                                                           plan=plan)
        self.done("proj_in", [got])
        return proj, h

    def grad(self, name, g):
        self.g[name] = g

    def carry(self, point):
        plans = []
        self.pending = SCHEDULE.get(point, [])
        for kind, names in self.pending:
            if kind in ("gather_ici", "gather_ici_cw") or kind in ICI_PARTS:
                wholes = [self.conv_w_shard] if kind == "gather_ici_cw" else []
                plans.append(_gather_ici_plan([self.slots[n] for n in names], wholes, ICI_PARTS.get(kind)))
            elif kind == "gather_pass":
                plans.append(_gather_pass_plan([self.slots[n] for n in names]))
            elif kind == "pair":
                plans.append(_pair_plan([self.g[n] for n in names]))
            else:
                for n in names:
                    self.pair_sums[n] = _pair_sum(self.g[n], self.from_sibling[n], self.place[1:2], f"pair_sum_{n}")
                plans.append(_chip_plan([self.pair_sums[n] for n in names]))
        return plans

    def done(self, point, carried):
        for (kind, names), got in zip(self.pending, carried):
            if kind in ("gather_ici", "gather_ici_cw", "gather_pass") or kind in ICI_PARTS:
                self.slots.update(zip(names, got))
                if kind == "gather_ici_cw":
                    self.conv_w = got[len(names)].transpose(1, 0, 2).reshape(3, 2 * D_FF)
            elif kind == "pair":
                self.from_sibling.update(zip(names, got))
            else:
                self.arrived.update(zip(names, got))

    def reduced_halves(self):
        return [_chip_sum(self.arrived[n], self.pair_sums[n], self.place, f"chip_sum_{n}") for n in REDUCED]


def kernel(x, pre_mix_norm, w_in, rel_bias, hgrn_lb_raw, hgrn_norm, w_branch_attn, w_branch_hgrn, w_out, post_mix_norm, pre_ffn_norm, w_up, conv_w, conv_b, w_down, post_ffn_norm, loss_target, m_pre_mix_norm, m_w_in, m_rel_bias, m_hgrn_lb_raw, m_hgrn_norm, m_w_branch_attn, m_w_branch_hgrn, m_w_out, m_post_mix_norm, m_pre_ffn_norm, m_w_up, m_conv_w, m_conv_b, m_w_down, m_post_ffn_norm, v_pre_mix_norm, v_w_in, v_rel_bias, v_hgrn_lb_raw, v_hgrn_norm, v_w_branch_attn, v_w_branch_hgrn, v_w_out, v_post_mix_norm, v_pre_ffn_norm, v_w_up, v_conv_w, v_conv_b, v_w_down, v_post_ffn_norm):
    w = dict(pre_mix_norm=pre_mix_norm, w_in=w_in, rel_bias=rel_bias, hgrn_lb_raw=hgrn_lb_raw, hgrn_norm=hgrn_norm,
             w_branch_attn=w_branch_attn, w_branch_hgrn=w_branch_hgrn, w_out=w_out, post_mix_norm=post_mix_norm,
             pre_ffn_norm=pre_ffn_norm, w_up=w_up, conv_w=conv_w, conv_b=conv_b, w_down=w_down,
             post_ffn_norm=post_ffn_norm)
    m = dict(pre_mix_norm=m_pre_mix_norm, w_in=m_w_in, rel_bias=m_rel_bias, hgrn_lb_raw=m_hgrn_lb_raw,
             hgrn_norm=m_hgrn_norm, w_branch_attn=m_w_branch_attn, w_branch_hgrn=m_w_branch_hgrn, w_out=m_w_out,
             post_mix_norm=m_post_mix_norm, pre_ffn_norm=m_pre_ffn_norm, w_up=m_w_up, conv_w=m_conv_w,
             conv_b=m_conv_b, w_down=m_w_down, post_ffn_norm=m_post_ffn_norm)
    v = dict(pre_mix_norm=v_pre_mix_norm, w_in=v_w_in, rel_bias=v_rel_bias, hgrn_lb_raw=v_hgrn_lb_raw,
             hgrn_norm=v_hgrn_norm, w_branch_attn=v_w_branch_attn, w_branch_hgrn=v_w_branch_hgrn, w_out=v_w_out,
             post_mix_norm=v_post_mix_norm, pre_ffn_norm=v_pre_ffn_norm, w_up=v_w_up, conv_w=v_conv_w,
             conv_b=v_conv_b, w_down=v_w_down, post_ffn_norm=v_post_ffn_norm)
    shard2d = {n: (w[n][0] if w[n].ndim == 3 else w[n]) for n in WEIGHTS}
    chip = 2 * lax.axis_index("x") + lax.axis_index("y")
    core = lax.axis_index("c")

    place = jnp.stack([chip, core]).astype(jnp.int32)
    slots = {n: _cast_into_slot(shard2d[n], place, f"cast_{n}") for n in BIG}
    ex = _Exchange(place, slots, shard2d["conv_w"])
    loss, grad_x, small = _local_step(x[0], loss_target[0], {n: w[n] for n in SMALL if n != "conv_w"}, ex)

    flat = [small[n].reshape(-1) for n in SMALL] + [loss.reshape(-1)]
    sizes = [t.shape[0] for t in flat]
    summed, wholes = _all_sum(jnp.concatenate(flat).reshape(-1, LANES), ex.reduced_halves(), "sum_small")
    summed = summed.reshape(-1)
    offs = [sum(sizes[:i]) for i in range(len(sizes))]
    grads = {}
    for n, o, sz in zip(SMALL, offs, sizes):
        grads[n] = summed[o:o + sz].reshape(small[n].shape)
    loss_total = summed[offs[-1]]
    cw = 2 * D_FF // N_CHIPS
    grads["conv_w"] = lax.dynamic_slice(grads["conv_w"], (0, chip * cw), (3, cw))

    big = dict(zip(REDUCED, wholes))
    big["w_in"] = jnp.concatenate([big.pop(n) for n in W_IN_PIECES], axis=0)
    grads.update(big)

    m2d = {n: m[n].reshape(shard2d[n].shape) for n in WEIGHTS}
    v2d = {n: v[n].reshape(shard2d[n].shape) for n in WEIGHTS}
    updated = dict(zip(SMALL, _adamw_small([shard2d[n] for n in SMALL], [grads[n] for n in SMALL],
                                           [m2d[n] for n in SMALL], [v2d[n] for n in SMALL], "adamw_small")))
    for n in BIG:
        updated[n] = _adamw(shard2d[n], grads[n], m2d[n], v2d[n], f"adamw_{n}")
    out_g, out_d, out_m, out_v = [], [], [], []
    for n in WEIGHTS:
        d2, m2, v2 = updated[n]
        shape = w[n].shape
        out_g.append(grads[n].reshape(shape))
        out_d.append(d2.reshape(shape))
        out_m.append(m2.reshape(shape))
        out_v.append(v2.reshape(shape))
    return (loss_total, grad_x[None], *out_g, *out_d, *out_m, *out_v)
```

```python
import functools
import math

import jax
import jax.numpy as jnp
from jax import lax
from jax.experimental import pallas as pl
from jax.experimental.pallas import tpu as pltpu

F32 = jnp.float32
BF16 = jnp.bfloat16
MESH = pl.DeviceIdType.MESH

D_MODEL = 1024
N_GROUPS = 3
DILATIONS = (1, 4, 16)
HEADS = 8
HEAD_DIM = 64
GROUP_W = HEADS * HEAD_DIM
QKV_W = N_GROUPS * 3 * GROUP_W
BLK = 128
NEG_INF = -1e30
NUM_BUCKETS = 32
MAX_EXACT = 16
MAX_DISTANCE = 2048
HG_HEADS = 4
HG_DK = 128
HG_W = HG_HEADS * HG_DK
HG_CHUNK = 32
HG_TILE = 256
IN_W = QKV_W + 4 * HG_W + 2 * D_MODEL
D_FF = 2816
EPS = 1e-6
N_CHIPS = 4
N_DEV = 8
LANES = 128

ADAM_LR, ADAM_B1, ADAM_B2, ADAM_EPS, ADAM_WD, ADAM_STEP = 0.001, 0.9, 0.999, 1e-08, 0.01, 10

VMEM_LIMIT = 56 * 1024 * 1024


def _cp(n_axes):
    return pltpu.CompilerParams(dimension_semantics=("arbitrary",) * n_axes, vmem_limit_bytes=VMEM_LIMIT)


def _sds(shape, dtype):
    return jax.ShapeDtypeStruct(tuple(shape), dtype)


def _sigmoid(v):
    return 1.0 / (1.0 + jnp.exp(-v))


def _bf(v):
    return v.astype(BF16)


def _dot(a, b, dims):
    return lax.dot_general(a, b, (dims, ((), ())), preferred_element_type=F32)


NN = ((1,), (0,))
NT = ((1,), (1,))
TN = ((0,), (0,))

ANY = pl.BlockSpec(memory_space=pl.ANY)


class _Plan:
    def __init__(self, copies, n_sems, ins=(), inouts=(), outs=()):
        self.copies, self.n_sems = copies, n_sems
        self.ins, self.inouts, self.outs = list(ins), list(inouts), list(outs)


def _call(body, plans=None, *, name, grid, in_specs, out_specs, out_shape, args, scratch_shapes=()):
    plans = list(plans or ())
    in_specs, out_specs, out_shape = list(in_specs), list(out_specs), list(out_shape)
    scratch_shapes = list(scratch_shapes)
    n_in, n_out, n_scr = len(in_specs), len(out_specs), len(scratch_shapes)
    x_in, x_out, aliases, spans = [], [], {}, []
    for p in plans:
        i0, o0 = len(x_in), len(x_out)
        x_in += p.ins
        for a in p.inouts:
            aliases[n_in + len(x_in)] = n_out + len(x_out)
            x_in.append(a)
            x_out.append(_sds(a.shape, a.dtype))
        x_out += p.outs
        spans.append((i0, len(p.ins), o0, len(p.inouts), len(p.outs)))
    sems = [pltpu.SemaphoreType.DMA((p.n_sems,)) for p in plans for _ in range(3)]

    def wrapped(*refs):
        xi = refs[n_in:n_in + len(x_in)]
        base = n_in + len(x_in)
        xo = refs[base + n_out:base + n_out + len(x_out)]
        sbase = base + n_out + len(x_out)
        xs = refs[sbase + n_scr:]
        ids = [pl.program_id(k) for k in range(len(grid))]
        first = functools.reduce(jnp.logical_and, [i == 0 for i in ids])
        last = functools.reduce(jnp.logical_and, [i == g - 1 for i, g in zip(ids, grid)])

        def descriptors(k):
            i0, ni, o0, nio, no = spans[k]
            return plans[k].copies(xi[i0:i0 + ni], xo[o0:o0 + nio], xo[o0 + nio:o0 + nio + no], *xs[3 * k:3 * k + 3])

        @pl.when(first)
        def _():
            for k in range(len(plans)):
                sends, _, local = descriptors(k)
                for cp in (*sends, *local):
                    cp.start()

        body(*refs[:n_in], *refs[base:base + n_out], *refs[sbase:sbase + n_scr])

        @pl.when(last)
        def _():
            for k in range(len(plans)):
                sends, recvs, local = descriptors(k)
                for cp in recvs:
                    cp.wait_recv()
                for cp in sends:
                    cp.wait_send()
                for cp in local:
                    cp.wait()

    res = pl.pallas_call(
        wrapped if plans else body, name=name, grid=grid, in_specs=in_specs + [ANY] * len(x_in),
        out_specs=out_specs + [ANY] * len(x_out), out_shape=out_shape + x_out, input_output_aliases=aliases,
        scratch_shapes=scratch_shapes + sems, compiler_params=_cp(len(grid)))(*args, *x_in)
    res = list(res)
    carried = [res[n_out + o0:n_out + o0 + nio + no] for (_, _, o0, nio, no) in spans]
    return res[:n_out], carried


def _mm_nn_blk(a, wg, name, tm=512, plans=None):
    M, K = a.shape
    nb, _, Nb = wg.shape

    def body(a_ref, w_ref, o_ref):
        o_ref[...] = _dot(_bf(a_ref[...]), w_ref[...], NN)

    (out,), carried = _call(
        body, plans, name=name, grid=(nb, M // tm),
        in_specs=[pl.BlockSpec((tm, K), lambda j, i: (i, 0)), pl.BlockSpec((None, K, Nb), lambda j, i: (j, 0, 0))],
        out_specs=[pl.BlockSpec((tm, Nb), lambda j, i: (i, j))],
        out_shape=[_sds((M, nb * Nb), F32)], args=(a, wg))
    return out if plans is None else (out, carried)


def _mm_nt_blk(dy, wg, name, tm=1024, plans=None):
    M = dy.shape[0]
    nb, K, Nb = wg.shape

    def body(dy_ref, w_ref, o_ref):
        j = pl.program_id(1)
        r = _dot(_bf(dy_ref[...]), w_ref[...], NT)

        @pl.when(j == 0)
        def _():
            o_ref[...] = r

        @pl.when(j > 0)
        def _():
            o_ref[...] += r

    (out,), carried = _call(
        body, plans, name=name, grid=(M // tm, nb),
        in_specs=[pl.BlockSpec((tm, Nb), lambda i, j: (i, j)), pl.BlockSpec((None, K, Nb), lambda i, j: (j, 0, 0))],
        out_specs=[pl.BlockSpec((tm, K), lambda i, j: (i, 0))],
        out_shape=[_sds((M, K), F32)], args=(dy, wg))
    return out if plans is None else (out, carried)


def _mm_nt_prenorm_bwd(dy, wg, xin, w, dres, name, tm=1024):
    M = dy.shape[0]
    nb, K, Nb = wg.shape

    def body(dy_ref, w_ref, x_ref, wn_ref, dres_ref, dx_ref, dw_ref, acc):
        i, j = pl.program_id(0), pl.program_id(1)
        r = _dot(_bf(dy_ref[...]), w_ref[...], NT)

        @pl.when(j == 0)
        def _():
            acc[...] = r

        @pl.when(j > 0)
        def _():
            acc[...] += r

        @pl.when(j == nb - 1)
        def _():
            xv = x_ref[...]
            rinv = _rinv(xv)
            xhat = xv * rinv
            dh = acc[...]
            dx_ref[...] = dres_ref[...] + _norm_bwd(dh, xhat, rinv, wn_ref[...])
            part = jnp.sum(dh * xhat, axis=0, keepdims=True)

            @pl.when(i == 0)
            def _():
                dw_ref[...] = part

            @pl.when(i > 0)
            def _():
                dw_ref[...] += part

    row = pl.BlockSpec((tm, K), lambda i, j: (i, 0))
    vec = pl.BlockSpec((1, K), lambda i, j: (0, 0))
    return pl.pallas_call(
        body, name=name, grid=(M // tm, nb),
        in_specs=[pl.BlockSpec((tm, Nb), lambda i, j: (i, j)), pl.BlockSpec((None, K, Nb), lambda i, j: (j, 0, 0)),
                  row, vec, row],
        out_specs=[row, vec], out_shape=[_sds((M, K), F32), _sds((1, K), F32)],
        scratch_shapes=[pltpu.VMEM((tm, K), F32)], compiler_params=_cp(2))(dy, wg, xin, w, dres)


def _mm_tn_blk(x, dy, nb, name, tk=2048, x_cols=None, plans=None, together=False):
    T, Mx = x.shape
    xk, Mx = (0, Mx) if x_cols is None else x_cols
    Nb = dy.shape[1] // nb
    nj = nb if together else 1

    def body(x_ref, dy_ref, o_ref):
        t = pl.program_id(1)
        r = _dot(_bf(x_ref[...]), _bf(dy_ref[...]), TN)
        for j in range(nj):
            rj = r[:, j * Nb:(j + 1) * Nb]

            @pl.when(t == 0)
            def _():
                o_ref[j] = rj

            @pl.when(t > 0)
            def _():
                o_ref[j] += rj

    (out,), carried = _call(
        body, plans, name=name, grid=(nb // nj, T // tk),
        in_specs=[pl.BlockSpec((tk, Mx), lambda j, t: (t, xk)), pl.BlockSpec((tk, nj * Nb), lambda j, t: (t, j))],
        out_specs=[pl.BlockSpec((nj, Mx, Nb), lambda j, t: (j, 0, 0))],
        out_shape=[_sds((nb, Mx, Nb), F32)], args=(x, dy))
    return out if plans is None else (out, carried)


def _mm_tn(x, dy, name, tk=1024):
    T, Mx = x.shape
    N = dy.shape[1]

    def body(x_ref, dy_ref, o_ref):
        t = pl.program_id(0)
        r = _dot(_bf(x_ref[...]), _bf(dy_ref[...]), TN)

        @pl.when(t == 0)
        def _():
            o_ref[...] = r

        @pl.when(t > 0)
        def _():
            o_ref[...] += r

    return pl.pallas_call(
        body, name=name, grid=(T // tk,),
        in_specs=[pl.BlockSpec((tk, Mx), lambda t: (t, 0)), pl.BlockSpec((tk, N), lambda t: (t, 0))],
        out_specs=pl.BlockSpec((Mx, N), lambda t: (0, 0)),
        out_shape=_sds((Mx, N), F32), compiler_params=_cp(1))(x, dy)


def _tile(arr, bw, col=lambda c: 0):
    return ("tile", arr, bw, col)


def _full(arr):
    return ("full", arr)


def _out_tile(width, dtype, bw, col=lambda c: 0):
    return ("tile", width, dtype, bw, col)


def _out_acc(rows, width, bw, col=lambda c: 0):
    return ("acc", rows, width, bw, col)


def _rows_call(name, body, n_rows, tm, ncol, ins, outs, plans=None):
    in_specs, args = [], []
    for e in ins:
        if e[0] == "tile":
            _, arr, bw, col = e
            in_specs.append(pl.BlockSpec((tm, bw), functools.partial(lambda c, i, col: (i, col(c)), col=col)))
        else:
            arr = e[1]
            in_specs.append(pl.BlockSpec(arr.shape, functools.partial(lambda c, i, nd: (0,) * nd, nd=arr.ndim)))
        args.append(arr)
    out_specs, out_shape = [], []
    for e in outs:
        if e[0] == "tile":
            _, width, dtype, bw, col = e
            out_specs.append(pl.BlockSpec((tm, bw), functools.partial(lambda c, i, col: (i, col(c)), col=col)))
            out_shape.append(_sds((n_rows, width), dtype))
        else:
            _, rows, width, bw, col = e
            out_specs.append(pl.BlockSpec((rows, bw), functools.partial(lambda c, i, col: (0, col(c)), col=col)))
            out_shape.append(_sds((rows, width), F32))
    out, carried = _call(body, plans, name=name, grid=(ncol, n_rows // tm), in_specs=in_specs, out_specs=out_specs,
                         out_shape=out_shape, args=args)
    return out if plans is None else (out, carried)


def _acc(ref, val):
    i = pl.program_id(1)

    @pl.when(i == 0)
    def _():
        ref[...] = val

    @pl.when(i > 0)
    def _():
        ref[...] += val


def _rinv(z):
    return lax.rsqrt(jnp.mean(z * z, axis=-1, keepdims=True) + EPS)


def _norm_bwd(dy, zhat, r, w):
    dyw = dy * w
    return r * (dyw - zhat * jnp.mean(dyw * zhat, axis=-1, keepdims=True))


def _prenorm_bwd(dh, xin, w, dres, name, plans=None):
    def body(dh_ref, x_ref, w_ref, dres_ref, dx_ref, dw_ref):
        xv = x_ref[...]
        r = _rinv(xv)
        xhat = xv * r
        dhv = dh_ref[...]
        dx_ref[...] = dres_ref[...] + _norm_bwd(dhv, xhat, r, w_ref[...])
        _acc(dw_ref, jnp.sum(dhv * xhat, axis=0, keepdims=True))

    return _rows_call(name, body, xin.shape[0], 512, 1,
                      [_tile(dh, D_MODEL), _tile(xin, D_MODEL), _full(w), _tile(dres, D_MODEL)],
                      [_out_tile(D_MODEL, F32, D_MODEL), _out_acc(1, D_MODEL, D_MODEL)], plans)


def _postnorm_bwd(dout, z, w, w_mat, name, plans=None):
    def body(do_ref, z_ref, w_ref, wm_ref, dz_ref, dm_ref, dw_ref):
        zv = z_ref[...]
        r = _rinv(zv)
        zhat = zv * r
        dov = do_ref[...]
        dz = _bf(_norm_bwd(dov, zhat, r, w_ref[...]))
        dz_ref[...] = dz
        dm_ref[...] = _dot(dz, wm_ref[...], NT)
        _acc(dw_ref, jnp.sum(dov * zhat, axis=0, keepdims=True))

    return _rows_call(name, body, z.shape[0], 512, 1,
                      [_tile(dout, D_MODEL), _tile(z, D_MODEL), _full(w), _full(w_mat)],
                      [_out_tile(D_MODEL, BF16, D_MODEL), _out_tile(D_MODEL, F32, D_MODEL),
                       _out_acc(1, D_MODEL, D_MODEL)], plans)


def _t5_bucket(dist):
    n = jnp.maximum(dist, 0)
    nf = jnp.maximum(n, 1).astype(F32)
    large = MAX_EXACT + (jnp.log(nf / MAX_EXACT) / math.log(MAX_DISTANCE / MAX_EXACT)
                         * (NUM_BUCKETS - MAX_EXACT)).astype(jnp.int32)
    large = jnp.minimum(large, NUM_BUCKETS - 1)
    return jnp.where(n < MAX_EXACT, n, large)


def _band_rel():
    return jnp.arange(BLK)[:, None] + BLK - jnp.arange(2 * BLK)[None, :]


def _band_valid():
    rel = _band_rel()
    window = (rel >= 0) & (rel <= BLK)
    first = window & (jnp.arange(2 * BLK)[None, :] >= BLK)
    return jnp.stack([first, window]).astype(F32).reshape(2, 1, BAND)


RES_UNROLL = 8
PAIR = LANES // HEAD_DIM


def _pair_lanes():
    first = lax.broadcasted_iota(jnp.int32, (1, LANES), 1) < HEAD_DIM
    return first, jnp.logical_not(first)


def _heads_per_step(d):
    return HEADS if d == 1 else LANES // HEAD_DIM


def _sub_rows(r, d):
    return pl.ds(r, BLK, stride=d) if d > 1 else pl.ds(0, BLK)


def _for_residues(d, fn):
    if d <= RES_UNROLL:
        for r in range(d):
            fn(r)
    else:
        def group(i, carry):
            for k in range(RES_UNROLL):
                fn(i * RES_UNROLL + k)
            return carry

        lax.fori_loop(0, d // RES_UNROLL, group, 0)


def _attn_specs(d, g, qblock):
    cw = _heads_per_step(d) * HEAD_DIM

    def col(part, hp):
        return (g * 3 + part) * (GROUP_W // cw) + hp

    def cur(part):
        return pl.BlockSpec((d * BLK, cw), lambda hp, n: (qblock(n), col(part, hp)))

    def prev(part):
        return pl.BlockSpec((d * BLK, cw), lambda hp, n: (jnp.maximum(qblock(n) - 1, 0), col(part, hp)))

    return cur, prev


def _attn_fwd(proj, bias, g, name, plans=None):
    S = proj.shape[0]
    d = DILATIONS[g]
    NB = S // (d * BLK)
    hps = _heads_per_step(d)

    def body(q_ref, kp_ref, kc_ref, vp_ref, vc_ref, b_ref, o_ref, lse_ref):
        hp = pl.program_id(0)
        later = jnp.minimum(pl.program_id(1), 1)

        def residue(r):
            rows = _sub_rows(r, d)
            q2 = q_ref[rows, :]
            k2 = jnp.concatenate([kp_ref[rows, :], kc_ref[rows, :]], axis=0)
            v2 = jnp.concatenate([vp_ref[rows, :], vc_ref[rows, :]], axis=0)
            outs, lses = [], []
            for pp in range(hps // PAIR):
                ps = slice(pp * LANES, (pp + 1) * LANES)
                qp, kp, vp = _bf(q2[:, ps]), _bf(k2[:, ps]), _bf(v2[:, ps])
                o_h, lse_h = [], []
                for hh, own in enumerate(_pair_lanes()):
                    s = _dot(qp, jnp.where(own, kp, 0), NT) * (HEAD_DIM ** -0.5) + b_ref[later, hp * hps + pp * PAIR + hh]
                    m = jnp.max(s, axis=-1, keepdims=True)
                    p = jnp.exp(s - m)
                    l = jnp.sum(p, axis=-1, keepdims=True)
                    o_h.append(_dot(_bf(p), vp, NN) / l)
                    lse_h.append(m + jnp.log(l))
                first = _pair_lanes()[0]
                outs.append(jnp.where(first, o_h[0], o_h[1]))
                lses.append(jnp.where(first, lse_h[0], lse_h[1]))
            o_ref[rows, :] = outs[0] if len(outs) == 1 else jnp.concatenate(outs, axis=1)
            lse_ref[rows, :] = lses[0] if len(lses) == 1 else jnp.concatenate(lses, axis=1)

        _for_residues(d, residue)

    cur, prev = _attn_specs(d, g, lambda n: n)
    out = pl.BlockSpec((d * BLK, hps * HEAD_DIM), lambda hp, n: (n, hp))
    res, carried = _call(
        body, plans, name=name, grid=(HEADS // hps, NB),
        in_specs=[cur(0), prev(1), cur(1), prev(2), cur(2),
                  pl.BlockSpec((2, HEADS, BLK, 2 * BLK), lambda hp, n: (0, 0, 0, 0))],
        out_specs=[out, out], out_shape=[_sds((S, GROUP_W), F32)] * 2,
        args=(proj, proj, proj, proj, proj, bias))
    return res if plans is None else (res, carried)


def _attn_bwd(proj, bias, lse, y, dy, g, name, plans=None):
    S = proj.shape[0]
    d = DILATIONS[g]
    NB = S // (d * BLK)
    hps = _heads_per_step(d)

    def body(q_ref, kp_ref, kc_ref, vp_ref, vc_ref, b_ref, l_ref, y_ref, dy_ref,
             dq_ref, dk_ref, dv_ref, db_ref, ck_ref, cv_ref):
        hp, n = pl.program_id(0), pl.program_id(1)

        @pl.when((hp == 0) & (n == 0))
        def _():
            db_ref[...] = jnp.zeros_like(db_ref)

        @pl.when(n == 0)
        def _():
            ck_ref[...] = jnp.zeros_like(ck_ref)
            cv_ref[...] = jnp.zeros_like(cv_ref)

        @pl.when(n < NB)
        def _():
            later = jnp.minimum(n, 1)

            def residue(r):
                rows = _sub_rows(r, d)
                q2 = q_ref[rows, :]
                k2 = jnp.concatenate([kp_ref[rows, :], kc_ref[rows, :]], axis=0)
                v2 = jnp.concatenate([vp_ref[rows, :], vc_ref[rows, :]], axis=0)
                l2, y2, dy2 = l_ref[rows, :], y_ref[rows, :], dy_ref[rows, :]
                dqs, dks, dvs = [], [], []
                for pp in range(hps // PAIR):
                    ps = slice(pp * LANES, (pp + 1) * LANES)
                    qp, kp, vp = _bf(q2[:, ps]), _bf(k2[:, ps]), _bf(v2[:, ps])
                    dyp, yp = dy2[:, ps], y2[:, ps]
                    dq_h, dk_h, dv_h = [], [], []
                    for hh, own in enumerate(_pair_lanes()):
                        head = hp * hps + pp * PAIR + hh
                        s = _dot(qp, jnp.where(own, kp, 0), NT) * (HEAD_DIM ** -0.5) + b_ref[later, head]
                        p = jnp.exp(s - l2[:, pp * LANES + hh * HEAD_DIM:pp * LANES + hh * HEAD_DIM + 1])
                        dyh = jnp.where(own, dyp, 0.0)
                        delta = jnp.sum(dyh * yp, axis=-1, keepdims=True)
                        ds = p * (_dot(_bf(dyh), vp, NT) - delta)
                        db_ref[head] += ds
                        dsb = _bf(ds * (HEAD_DIM ** -0.5))
                        dq_h.append(_dot(dsb, kp, NN))
                        dk_h.append(_dot(dsb, qp, TN))
                        dv_h.append(_dot(_bf(p), _bf(dyp), TN))
                    first = _pair_lanes()[0]
                    dqs.append(jnp.where(first, dq_h[0], dq_h[1]))
                    dks.append(jnp.where(first, dk_h[0], dk_h[1]))
                    dvs.append(jnp.where(first, dv_h[0], dv_h[1]))
                dkb = dks[0] if len(dks) == 1 else jnp.concatenate(dks, axis=1)
                dvb = dvs[0] if len(dvs) == 1 else jnp.concatenate(dvs, axis=1)
                dq_ref[rows, :] = dqs[0] if len(dqs) == 1 else jnp.concatenate(dqs, axis=1)
                dk_ref[rows, :] = ck_ref[rows, :] + dkb[:BLK]
                dv_ref[rows, :] = cv_ref[rows, :] + dvb[:BLK]
                ck_ref[rows, :] = dkb[BLK:]
                cv_ref[rows, :] = dvb[BLK:]

            _for_residues(d, residue)

        @pl.when(n == NB)
        def _():
            dk_ref[...] = ck_ref[...]
            dv_ref[...] = cv_ref[...]

    def qn(n):
        return jnp.minimum(n, NB - 1)

    cur, prev = _attn_specs(d, g, qn)
    cw = hps * HEAD_DIM
    row = pl.BlockSpec((d * BLK, cw), lambda hp, n: (qn(n), hp))
    done = pl.BlockSpec((d * BLK, cw), lambda hp, n: (jnp.maximum(n - 1, 0), hp))
    (dq, dk, dv, db), carried = _call(
        body, plans, name=name, grid=(HEADS // hps, NB + 1),
        in_specs=[cur(0), prev(1), cur(1), prev(2), cur(2),
                  pl.BlockSpec((2, HEADS, BLK, 2 * BLK), lambda hp, n: (0, 0, 0, 0)), row, row, row],
        out_specs=[row, done, done, pl.BlockSpec((HEADS, BLK, 2 * BLK), lambda hp, n: (0, 0, 0))],
        out_shape=[_sds((S, GROUP_W), F32)] * 3 + [_sds((HEADS, BLK, 2 * BLK), F32)],
        scratch_shapes=[pltpu.VMEM((d * BLK, cw), F32)] * 2,
        args=(proj, proj, proj, proj, proj, bias, lse, y, dy))
    return ([dq, dk, dv], db) if plans is None else ([dq, dk, dv], db, carried)


BAND = BLK * 2 * BLK


def _bucket_onehot():
    buckets = jnp.stack([_t5_bucket(_band_rel() * d) for d in DILATIONS]).reshape(N_GROUPS, 1, BAND)
    return (buckets == jnp.arange(NUM_BUCKETS).reshape(1, NUM_BUCKETS, 1)).astype(F32)


def _relbias_fwd(rel_bias, name):
    table = rel_bias.reshape(NUM_BUCKETS, N_GROUPS, HEADS).transpose(1, 0, 2)

    def body(t_ref, oh_ref, valid_ref, o_ref):
        bias = lax.dot_general(t_ref[...], oh_ref[...], (TN, ((), ())), preferred_element_type=F32,
                               precision=lax.Precision.HIGHEST)
        for k in range(2):
            o_ref[k] = jnp.where(valid_ref[k] > 0.5, bias, NEG_INF)

    out = pl.pallas_call(
        body, name=name, grid=(N_GROUPS,),
        in_specs=[pl.BlockSpec((None, NUM_BUCKETS, HEADS), lambda g: (g, 0, 0)),
                  pl.BlockSpec((None, NUM_BUCKETS, BAND), lambda g: (g, 0, 0)),
                  pl.BlockSpec((2, 1, BAND), lambda g: (0, 0, 0))],
        out_specs=pl.BlockSpec((None, 2, HEADS, BAND), lambda g: (g, 0, 0, 0)),
        out_shape=_sds((N_GROUPS, 2, HEADS, BAND), F32), compiler_params=_cp(1))(table, _bucket_onehot(), _band_valid())
    return out.reshape(N_GROUPS, 2, HEADS, BLK, 2 * BLK)


def _relbias_bwd(dbs, name):
    band = BAND
    onehot = _bucket_onehot()
    dbf = jnp.stack([db.reshape(HEADS, band) for db in dbs])

    def body(oh_ref, db_ref, o_ref):
        o_ref[...] = lax.dot_general(oh_ref[...], db_ref[...], (NT, ((), ())), preferred_element_type=F32,
                                     precision=lax.Precision.HIGHEST)

    out = pl.pallas_call(
        body, name=name, grid=(N_GROUPS,),
        in_specs=[pl.BlockSpec((None, NUM_BUCKETS, band), lambda g: (g, 0, 0)),
                  pl.BlockSpec((None, HEADS, band), lambda g: (g, 0, 0))],
        out_specs=pl.BlockSpec((None, NUM_BUCKETS, HEADS), lambda g: (g, 0, 0)),
        out_shape=_sds((N_GROUPS, NUM_BUCKETS, HEADS), F32), compiler_params=_cp(1))(onehot, dbf)
    return out.transpose(1, 0, 2).reshape(NUM_BUCKETS, N_GROUPS * HEADS)


def _chunk_pos(shape):
    return lax.broadcasted_iota(jnp.int32, shape, 0) % HG_CHUNK


def _chunk_cumsum(v):
    pos = _chunk_pos(v.shape)
    s = 1
    while s < HG_CHUNK:
        v = v + jnp.where(pos >= s, pltpu.roll(v, s, 0), 0.0)
        s *= 2
    return v


def _chunk_rev_cumsum(v):
    pos = _chunk_pos(v.shape)
    n = v.shape[0]
    s = 1
    while s < HG_CHUNK:
        v = v + jnp.where(pos < HG_CHUNK - s, pltpu.roll(v, n - s, 0), 0.0)
        s *= 2
    return v


def _lower_bound(raw):
    a0, a1 = raw[0:1], raw[1:2]
    m = jnp.maximum(a0, a1)
    e0, e1 = jnp.exp(a0 - m), jnp.exp(a1 - m)
    return e0 / (e0 + e1)


def _hg_gates(qr, fr, lb):
    sf = _sigmoid(fr)
    f = lb + (1.0 - lb) * sf
    sq = _sigmoid(qr)
    return qr * sq, sq, f, sf


HG_COL0 = QKV_W // HG_W


def _hgrn_fwd(proj, lb_raw, nw, name, plans=None):
    S = proj.shape[0]
    ncs = HG_TILE // HG_CHUNK

    def body(q_ref, f_ref, i_ref, og_ref, lb_ref, nw_ref, y_ref, o_ref, st_ref, state):
        @pl.when(pl.program_id(0) == 0)
        def _():
            state[...] = jnp.zeros_like(state)

        lb = _lower_bound(lb_ref[...])
        q, _, f, _ = _hg_gates(q_ref[...], f_ref[...], lb)
        k = 1.0 - f
        G = _chunk_cumsum(jnp.log(f))
        row = lax.broadcasted_iota(jnp.int32, (HG_CHUNK, HG_CHUNK), 0)
        col = lax.broadcasted_iota(jnp.int32, (HG_CHUNK, HG_CHUNK), 1)
        heads = [slice(h * HG_DK, (h + 1) * HG_DK) for h in range(HG_HEADS)]
        sts = [state[h] for h in range(HG_HEADS)]
        for c in range(ncs):
            cs = slice(c * HG_CHUNK, (c + 1) * HG_CHUNK)
            for h, hs in enumerate(heads):
                Gc = G[cs, hs]
                gl = Gc[HG_CHUNK - 1:HG_CHUNK]
                qt = _bf(q[cs, hs] * jnp.exp(Gc))
                kt = _bf(k[cs, hs] * jnp.exp(-Gc))
                kd = _bf(k[cs, hs] * jnp.exp(gl - Gc))
                v = _bf(i_ref[cs, hs])
                A = jnp.where(row >= col, _dot(qt, kt, NT), 0.0)
                o_ref[cs, hs] = _dot(_bf(A), v, NN) + _dot(qt, _bf(sts[h]), NT)
                st_ref[c, h] = sts[h]
                sts[h] = sts[h] * jnp.exp(gl) + _dot(v, kd, TN)
        for h, hs in enumerate(heads):
            state[h] = sts[h]
            oh = o_ref[:, hs]
            og = og_ref[:, hs]
            y_ref[:, hs] = oh * _rinv(oh) * nw_ref[...] * (og * _sigmoid(og))

    def colspec(j):
        return pl.BlockSpec((HG_TILE, HG_W), lambda i: (i, HG_COL0 + j))

    res, carried = _call(
        body, plans, name=name, grid=(S // HG_TILE,),
        in_specs=[colspec(0), colspec(1), colspec(2), colspec(3),
                  pl.BlockSpec((2, HG_W), lambda i: (0, 0)), pl.BlockSpec((1, HG_DK), lambda i: (0, 0))],
        out_specs=[pl.BlockSpec((HG_TILE, HG_W), lambda i: (i, 0))] * 2
        + [pl.BlockSpec((ncs, HG_HEADS, HG_DK, HG_DK), lambda i: (i, 0, 0, 0))],
        out_shape=[_sds((S, HG_W), F32)] * 2 + [_sds((S // HG_CHUNK, HG_HEADS, HG_DK, HG_DK), F32)],
        scratch_shapes=[pltpu.VMEM((HG_HEADS, HG_DK, HG_DK), F32)],
        args=(proj, proj, proj, proj, lb_raw, nw))
    return res if plans is None else (res, carried)


def _hgrn_bwd(proj, lb_raw, nw, o, states, dy, d_attn, d_gates, name):
    S = proj.shape[0]
    ncs = HG_TILE // HG_CHUNK
    nt = S // HG_TILE
    n_a, n_g = len(d_attn), len(d_gates)
    own = [slice(QKV_W + j * HG_W, QKV_W + (j + 1) * HG_W) for j in range(4)]

    def body(q_ref, f_ref, i_ref, og_ref, lb_ref, nw_ref, o_ref, st_ref, dy_ref, *rest):
        attn_refs, gate_refs = rest[:n_a], rest[n_a:n_a + n_g]
        dp_ref, dlb_ref, dnw_ref, dstate, do_s, dG_s, dgl_s, dk_s, dlb_s = rest[n_a + n_g:]
        dq_ref, df_ref, di_ref, dog_ref = (dp_ref.at[:, cols] for cols in own)
        step = pl.program_id(0)
        for k, a_ref in enumerate(attn_refs):
            dp_ref[:, k * GROUP_W:(k + 1) * GROUP_W] = _bf(a_ref[...])
        for k, g_ref in enumerate(gate_refs):
            dp_ref[:, QKV_W + 4 * HG_W + k * D_MODEL:QKV_W + 4 * HG_W + (k + 1) * D_MODEL] = g_ref[...]

        @pl.when(step == 0)
        def _():
            dstate[...] = jnp.zeros_like(dstate)
            dlb_s[...] = jnp.zeros_like(dlb_s)
            dnw_ref[...] = jnp.zeros_like(dnw_ref)

        lb = _lower_bound(lb_ref[...])
        qr = q_ref[...]
        q, sq, f, sf = _hg_gates(qr, f_ref[...], lb)
        k = 1.0 - f
        G = _chunk_cumsum(jnp.log(f))
        nwv = nw_ref[...]
        row = lax.broadcasted_iota(jnp.int32, (HG_CHUNK, HG_CHUNK), 0)
        col = lax.broadcasted_iota(jnp.int32, (HG_CHUNK, HG_CHUNK), 1)
        for h in range(HG_HEADS):
            hs = slice(h * HG_DK, (h + 1) * HG_DK)
            oh = o_ref[:, hs]
            r = _rinv(oh)
            ohat = oh * r
            og = og_ref[:, hs]
            sg = _sigmoid(og)
            dyh = dy_ref[:, hs]
            don = dyh * (og * sg)
            dog_ref[:, hs] = _bf(dyh * (ohat * nwv) * (sg * (1.0 + og * (1.0 - sg))))
            dnw_ref[...] += jnp.sum(don * ohat, axis=0, keepdims=True)
            do_s[:, hs] = _norm_bwd(don, ohat, r, nwv)
        dsts = [dstate[h] for h in range(HG_HEADS)]
        for c in reversed(range(ncs)):
            cs = slice(c * HG_CHUNK, (c + 1) * HG_CHUNK)
            for h in range(HG_HEADS):
                hs = slice(h * HG_DK, (h + 1) * HG_DK)
                dst = dsts[h]
                Gc = G[cs, hs]
                gl = Gc[HG_CHUNK - 1:HG_CHUNK]
                eG, enG, edG, egl = jnp.exp(Gc), jnp.exp(-Gc), jnp.exp(gl - Gc), jnp.exp(gl)
                qt, kt, kd = q[cs, hs] * eG, k[cs, hs] * enG, k[cs, hs] * edG
                qtb, ktb, kdb = _bf(qt), _bf(kt), _bf(kd)
                v = _bf(i_ref[cs, hs])
                do = _bf(do_s[cs, hs])
                st = st_ref[c, h]
                dstb = _bf(dst)
                A = jnp.where(row >= col, _dot(qtb, ktb, NT), 0.0)
                dA = _bf(jnp.where(row >= col, _dot(do, v, NT), 0.0))
                di_ref[cs, hs] = _bf(_dot(_bf(A), do, TN) + _dot(kdb, dstb, NT))
                dqt = _dot(dA, ktb, NN) + _dot(do, _bf(st), NN)
                dkt = _dot(dA, qtb, TN)
                dkd = _dot(v, dstb, NN)
                dgl = egl * jnp.sum(st * dst, axis=0, keepdims=True) + jnp.sum(dkd * kd, axis=0, keepdims=True)
                dsts[h] = dst * egl + _dot(do, qtb, TN)
                dq_ref[cs, hs] = _bf(dqt * eG * (sq[cs, hs] * (1.0 + qr[cs, hs] * (1.0 - sq[cs, hs]))))
                dk_s[cs, hs] = dkt * enG + dkd * edG
                dG_s[cs, hs] = dqt * qt - dkt * kt - dkd * kd
                dgl_s[cs, hs] = jnp.broadcast_to(dgl, (HG_CHUNK, HG_DK))
        for h in range(HG_HEADS):
            dstate[h] = dsts[h]
        dg = _chunk_rev_cumsum(dG_s[...]) + dgl_s[...]
        dfv = dg / f - dk_s[...]
        df_ref[...] = _bf(dfv * (1.0 - lb) * sf * (1.0 - sf))
        dlb_s[...] += jnp.sum(dfv * (1.0 - sf), axis=0, keepdims=True)

        @pl.when(step == nt - 1)
        def _():
            t = dlb_s[...] * lb * (1.0 - lb)
            dlb_ref[...] = jnp.concatenate([t, -t], axis=0)

    def colspec(j):
        return pl.BlockSpec((HG_TILE, HG_W), lambda i: (nt - 1 - i, HG_COL0 + j))

    def rows(width):
        return pl.BlockSpec((HG_TILE, width), lambda i: (nt - 1 - i, 0))

    tile = rows(HG_W)
    return pl.pallas_call(
        body, name=name, grid=(nt,),
        in_specs=[colspec(0), colspec(1), colspec(2), colspec(3),
                  pl.BlockSpec((2, HG_W), lambda i: (0, 0)), pl.BlockSpec((1, HG_DK), lambda i: (0, 0)),
                  tile, pl.BlockSpec((ncs, HG_HEADS, HG_DK, HG_DK), lambda i: (nt - 1 - i, 0, 0, 0)), tile]
        + [rows(GROUP_W)] * n_a + [rows(D_MODEL)] * n_g,
        out_specs=[rows(IN_W), pl.BlockSpec((2, HG_W), lambda i: (0, 0)), pl.BlockSpec((1, HG_DK), lambda i: (0, 0))],
        out_shape=[_sds((S, IN_W), BF16), _sds((2, HG_W), F32), _sds((1, HG_DK), F32)],
        scratch_shapes=[pltpu.VMEM((HG_HEADS, HG_DK, HG_DK), F32)] + [pltpu.VMEM((HG_TILE, HG_W), F32)] * 4
        + [pltpu.VMEM((1, HG_W), F32)],
        compiler_params=_cp(1))(proj, proj, proj, proj, lb_raw, nw, o, states, dy, *d_attn, *d_gates)


GATE_COL0 = (QKV_W + 4 * HG_W) // GROUP_W
HALF_D = D_MODEL // 2


def _gate_tiles(proj):
    return [_tile(proj, HALF_D, functools.partial(lambda c, k: GATE_COL0 + k, k=k)) for k in range(4)]


def _gates(g_refs):
    s0 = _sigmoid(jnp.concatenate([g_refs[0][...], g_refs[1][...]], axis=1))
    s1 = _sigmoid(jnp.concatenate([g_refs[2][...], g_refs[3][...]], axis=1))
    return s0, s1


def _branch_fwd(os_, lses, yh, proj, w_a, w_h, name, plans=None):
    nb = w_a.shape[0]

    def body(o0, o1, o2, l0, l1, l2, yh_ref, g0a, g0b, g1a, g1b, wa_ref, wh_ref,
             y_ref, lse_ref, za_ref, zh_ref, m_ref):
        a, b, c = l0[...], l1[...], l2[...]
        m = jnp.maximum(jnp.maximum(a, b), c)
        ea, eb, ec = jnp.exp(a - m), jnp.exp(b - m), jnp.exp(c - m)
        den = ea + eb + ec
        y = (ea * o0[...] + eb * o1[...] + ec * o2[...]) / den
        y_ref[...] = y
        lse_ref[...] = m + jnp.log(den)
        yb, yhb = _bf(y), _bf(yh_ref[...])
        za = jnp.concatenate([_dot(yb, wa_ref[j], NN) for j in range(nb)], axis=1)
        zh = jnp.concatenate([_dot(yhb, wh_ref[j], NN) for j in range(nb)], axis=1)
        s0, s1 = _gates((g0a, g0b, g1a, g1b))
        za_ref[...] = za
        zh_ref[...] = zh
        m_ref[...] = _bf(s0 * za + s1 * zh)

    return _rows_call(name, body, yh.shape[0], 512, 1,
                      [*[_tile(t, GROUP_W) for t in (*os_, *lses)], _tile(yh, HG_W), *_gate_tiles(proj),
                       _full(w_a), _full(w_h)],
                      [_out_tile(GROUP_W, F32, GROUP_W)] * 2 + [_out_tile(D_MODEL, F32, D_MODEL)] * 2
                      + [_out_tile(D_MODEL, BF16, D_MODEL)], plans)


def _branch_bwd(dm, za, zh, proj, w_a, w_h, name, plans=None):
    nb, _, Nb = w_a.shape

    def body(dm_ref, za_ref, zh_ref, g0a, g0b, g1a, g1b, wa_ref, wh_ref,
             dza_ref, dzh_ref, dg0_ref, dg1_ref, dy_ref, dyh_ref):
        dmv = dm_ref[...]
        s0, s1 = _gates((g0a, g0b, g1a, g1b))
        dza, dzh = _bf(dmv * s0), _bf(dmv * s1)
        dza_ref[...] = dza
        dzh_ref[...] = dzh
        dg0_ref[...] = _bf(dmv * za_ref[...] * s0 * (1.0 - s0))
        dg1_ref[...] = _bf(dmv * zh_ref[...] * s1 * (1.0 - s1))
        dy_ref[...] = sum(_dot(dza[:, j * Nb:(j + 1) * Nb], wa_ref[j], NT) for j in range(nb))
        dyh_ref[...] = sum(_dot(dzh[:, j * Nb:(j + 1) * Nb], wh_ref[j], NT) for j in range(nb))

    return _rows_call(name, body, za.shape[0], 512, 1,
                      [_tile(dm, D_MODEL), _tile(za, D_MODEL), _tile(zh, D_MODEL), *_gate_tiles(proj),
                       _full(w_a), _full(w_h)],
                      [_out_tile(D_MODEL, BF16, D_MODEL)] * 4 + [_out_tile(GROUP_W, F32, GROUP_W),
                                                                 _out_tile(HG_W, F32, HG_W)], plans)


def _mix_out(merged, w_out, x, w_post, w_pre, name):
    def body(m_ref, wo_ref, x_ref, wp_ref, wf_ref, mo_ref, x1_ref, h2_ref):
        z = _dot(m_ref[...], wo_ref[...], NN)
        mo_ref[...] = z
        x1 = x_ref[...] + z * _rinv(z) * wp_ref[...]
        x1_ref[...] = x1
        h2_ref[...] = _bf(x1 * _rinv(x1) * wf_ref[...])

    return _rows_call(name, body, x.shape[0], 512, 1,
                      [_tile(merged, D_MODEL), _full(w_out), _tile(x, D_MODEL), _full(w_post), _full(w_pre)],
                      [_out_tile(D_MODEL, F32, D_MODEL), _out_tile(D_MODEL, F32, D_MODEL),
                       _out_tile(D_MODEL, BF16, D_MODEL)])


def _loss_head(a, w_down, x1, tgt, w, name):
    def body(a_ref, wd_ref, x1_ref, t_ref, w_ref, dx_ref, df_ref, dw_ref, loss_ref):
        z = _dot(a_ref[...], wd_ref[...], NN)
        r = _rinv(z)
        zhat = z * r
        wv = w_ref[...]
        e = x1_ref[...] + zhat * wv - t_ref[...]
        dx = e * (1.0 / D_MODEL)
        dx_ref[...] = dx
        df_ref[...] = _bf(_norm_bwd(dx, zhat, r, wv))
        _acc(dw_ref, jnp.sum(dx * zhat, axis=0, keepdims=True))
        part = 0.5 * jnp.sum(jnp.sum(e * e, axis=1, keepdims=True), axis=0, keepdims=True) * (1.0 / D_MODEL)
        _acc(loss_ref, jnp.broadcast_to(part, (1, LANES)))

    return _rows_call(name, body, x1.shape[0], 512, 1,
                      [_tile(a, D_FF), _full(w_down), _tile(x1, D_MODEL), _tile(tgt, D_MODEL), _full(w)],
                      [_out_tile(D_MODEL, F32, D_MODEL), _out_tile(D_MODEL, BF16, D_MODEL),
                       _out_acc(1, D_MODEL, D_MODEL), _out_acc(1, LANES, LANES)])


CONV_CB = D_FF // 2
CONV_TM = 512
HALO = 8
SQRT_HALF = 0.7071067811865476
INV_SQRT_2PI = 0.3989422804014327


CONV_RS = 32


def _lane_tiles():
    return [slice(k * LANES, (k + 1) * LANES) for k in range(CONV_CB // LANES)]


def _strip_start(i):
    return pl.multiple_of(i * CONV_RS, CONV_RS)


def _strip_taps(u_ref, halo_ref, r0, cs, first_strip, first_tile):
    if first_strip:
        before = jnp.where(first_tile, 0.0, halo_ref[:, cs])
        blk = jnp.concatenate([before, u_ref[0:CONV_RS, cs]], axis=0)
    else:
        blk = u_ref[pl.ds(pl.multiple_of(r0 - HALO, HALO), CONV_RS + HALO), cs]
    return pltpu.roll(blk, 2, 0)[HALO:], pltpu.roll(blk, 1, 0)[HALO:], blk[HALO:]


def _conv(taps, w_ref, b_ref, cs):
    return b_ref[:, cs] + w_ref[0:1, cs] * taps[0] + w_ref[1:2, cs] * taps[1] + w_ref[2:3, cs] * taps[2]


def _conv_specs(tm):
    nh = tm // HALO
    nc = D_FF // CONV_CB

    def tile(off):
        return pl.BlockSpec((tm, CONV_CB), lambda c, i: (i, off + c))

    def halo(off):
        return pl.BlockSpec((HALO, CONV_CB), lambda c, i: (jnp.maximum(i * nh - 1, 0), off + c))

    def small(rows, off):
        return pl.BlockSpec((rows, CONV_CB), lambda c, i: (0, off + c))

    return nc, tile, halo, small


def _conv_gelu_fwd(u, cw, cb, name, plans=None):
    S = u.shape[0]
    tm = CONV_TM
    nc, tile, halo, small = _conv_specs(tm)

    def body(ug, hg, uv, hv, wg, wv, bg, bv, a_ref):
        first_tile = pl.program_id(1) == 0

        def strip(r0, first_strip):
            for cs in _lane_tiles():
                cg = _conv(_strip_taps(ug, hg, r0, cs, first_strip, first_tile), wg, bg, cs)
                cv = _conv(_strip_taps(uv, hv, r0, cs, first_strip, first_tile), wv, bv, cs)
                a_ref[pl.ds(r0, CONV_RS), cs] = _bf(0.5 * cg * (1.0 + lax.erf(cg * SQRT_HALF)) * cv)

        strip(0, True)
        lax.fori_loop(1, tm // CONV_RS, lambda k, c: (strip(_strip_start(k), False), c)[1], 0)

    (a,), carried = _call(
        body, plans, name=name, grid=(nc, S // tm),
        in_specs=[tile(0), halo(0), tile(nc), halo(nc), small(3, 0), small(3, nc), small(1, 0), small(1, nc)],
        out_specs=[tile(0)], out_shape=[_sds((S, D_FF), BF16)], args=(u, u, u, u, cw, cw, cb, cb))
    return a if plans is None else (a, carried)


def _conv_gelu_bwd(u, dff, w_down, cw, cb, name, plans=None):
    S = u.shape[0]
    tm = CONV_TM
    nt = S // tm
    nc, tile, halo, small = _conv_specs(tm)

    def body(ug, hg, uv, hv, wg, wv, bg, bv, dff_ref, wd_ref, dcg_ref, dcv_ref, dwg_ref, dwv_ref, dbg_ref, dbv_ref,
             acc, da_ref):
        i = pl.program_id(1)
        first_tile = i == 0
        da_ref[...] = _dot(dff_ref[...], wd_ref[...], NT)

        @pl.when(first_tile)
        def _():
            acc[...] = jnp.zeros_like(acc)

        def strip(r0, first_strip):
            rows = pl.ds(r0, CONV_RS)
            for cs in _lane_tiles():
                tg = _strip_taps(ug, hg, r0, cs, first_strip, first_tile)
                tv = _strip_taps(uv, hv, r0, cs, first_strip, first_tile)
                cg = _conv(tg, wg, bg, cs)
                cv = _conv(tv, wv, bv, cs)
                phi = 0.5 * (1.0 + lax.erf(cg * SQRT_HALF))
                dav = da_ref[rows, cs]
                dcg = dav * cv * (phi + cg * jnp.exp(-0.5 * cg * cg) * INV_SQRT_2PI)
                dcv = dav * (cg * phi)
                dcg_ref[rows, cs] = dcg
                dcv_ref[rows, cs] = dcv
                for half, (dc, taps) in enumerate(((dcg, tg), (dcv, tv))):
                    for j in range(3):
                        acc[4 * half + j, :, cs] += dc * taps[j]
                    acc[4 * half + 3, :, cs] += dc

        strip(0, True)
        lax.fori_loop(1, tm // CONV_RS, lambda k, c: (strip(_strip_start(k), False), c)[1], 0)

        @pl.when(i == nt - 1)
        def _():
            for half, (dw_ref, db_ref) in enumerate(((dwg_ref, dbg_ref), (dwv_ref, dbv_ref))):
                for j in range(3):
                    dw_ref[j:j + 1, :] = jnp.sum(acc[4 * half + j], axis=0, keepdims=True)
                db_ref[...] = jnp.sum(acc[4 * half + 3], axis=0, keepdims=True)

    res, carried = _call(
        body, plans, name=name, grid=(nc, nt),
        in_specs=[tile(0), halo(0), tile(nc), halo(nc), small(3, 0), small(3, nc), small(1, 0), small(1, nc),
                  pl.BlockSpec((tm, D_MODEL), lambda c, i: (i, 0)), pl.BlockSpec((CONV_CB, D_MODEL), lambda c, i: (c, 0))],
        out_specs=[tile(0), tile(0), small(3, 0), small(3, 0), small(1, 0), small(1, 0)],
        out_shape=[_sds((S, D_FF), F32)] * 2 + [_sds((3, D_FF), F32)] * 2 + [_sds((1, D_FF), F32)] * 2,
        scratch_shapes=[pltpu.VMEM((8, CONV_RS, CONV_CB), F32), pltpu.VMEM((tm, CONV_CB), F32)],
        args=(u, u, u, u, cw, cw, cb, cb, dff, w_down))
    return res if plans is None else (res, carried)


def _conv_input_bwd(dcg, dcv, cw, name, plans=None):
    S = dcg.shape[0]
    tm = CONV_TM // 2
    nh = tm // HALO
    nt = S // tm
    n = CONV_RS + HALO
    tile = pl.BlockSpec((tm, D_FF), lambda i: (i, 0))
    nxt = pl.BlockSpec((HALO, D_FF), lambda i: (jnp.minimum((i + 1) * nh, S // HALO - 1), 0))

    def body(g_ref, ng_ref, v_ref, nv_ref, w_ref, du_ref):
        last_tile = pl.program_id(0) == nt - 1

        def strip(r0, last_strip):
            for half, (dc_ref, n_ref) in enumerate(((g_ref, ng_ref), (v_ref, nv_ref))):
                for k in range(D_FF // LANES):
                    cs = slice(k * LANES, (k + 1) * LANES)
                    ws = slice(half * D_FF + k * LANES, half * D_FF + (k + 1) * LANES)
                    if last_strip:
                        after = jnp.where(last_tile, 0.0, n_ref[:, cs])
                        blk = jnp.concatenate([dc_ref[tm - CONV_RS:tm, cs], after], axis=0)
                    else:
                        blk = dc_ref[pl.ds(r0, n), cs]
                    d1 = pltpu.roll(blk, n - 1, 0)[:CONV_RS]
                    d2 = pltpu.roll(blk, n - 2, 0)[:CONV_RS]
                    du_ref[pl.ds(r0, CONV_RS), ws] = _bf(w_ref[2:3, ws] * blk[:CONV_RS] + w_ref[1:2, ws] * d1
                                                         + w_ref[0:1, ws] * d2)

        lax.fori_loop(0, tm // CONV_RS - 1, lambda k, c: (strip(_strip_start(k), False), c)[1], 0)
        strip(tm - CONV_RS, True)

    (du,), carried = _call(
        body, plans, name=name, grid=(nt,),
        in_specs=[tile, nxt, tile, nxt, pl.BlockSpec((3, 2 * D_FF), lambda i: (0, 0))],
        out_specs=[pl.BlockSpec((tm, 2 * D_FF), lambda i: (i, 0))], out_shape=[_sds((S, 2 * D_FF), BF16)],
        args=(dcg, dcg, dcv, dcv, cw))
    return du if plans is None else (du, carried)


def _row_tile(n, cap):
    best = n
    for t in range(16, cap + 1, 16):
        if n % t == 0:
            best = t
    return best if best <= cap else n


def _rows_for_bytes(nbytes, cols):
    return max(16, nbytes // (4 * cols) // 16 * 16)


def _adamw_update(w_ref, g_ref, m_ref, v_ref, d_ref, nm_ref, nv_ref):
    gv = g_ref[...]
    nm = ADAM_B1 * m_ref[...] + (1.0 - ADAM_B1) * gv
    nv = ADAM_B2 * v_ref[...] + (1.0 - ADAM_B2) * (gv * gv)
    m_hat = nm / (1.0 - ADAM_B1 ** ADAM_STEP)
    v_hat = nv / (1.0 - ADAM_B2 ** ADAM_STEP)
    d_ref[...] = -ADAM_LR * (m_hat / (jnp.sqrt(v_hat) + ADAM_EPS) + ADAM_WD * w_ref[...])
    nm_ref[...] = nm
    nv_ref[...] = nv


def _adamw(w, g, m, v, name):
    R, C = w.shape
    tr = _row_tile(R, _rows_for_bytes(2 << 20, C))
    spec = pl.BlockSpec((tr, C), lambda i: (i, 0))
    body = functools.partial(_adamw_update)
    return pl.pallas_call(body, name=name, grid=(R // tr,), in_specs=[spec] * 4, out_specs=[spec] * 3,
                          out_shape=[_sds((R, C), F32)] * 3, compiler_params=_cp(1))(w, g, m, v)


def _adamw_small(ws, gs, ms, vs, name):
    n = len(ws)

    def body(*refs):
        ins, outs = refs[:4 * n], refs[4 * n:]
        for k in range(n):
            _adamw_update(ins[k], ins[n + k], ins[2 * n + k], ins[3 * n + k], outs[3 * k], outs[3 * k + 1], outs[3 * k + 2])

    vm = pl.BlockSpec(memory_space=pltpu.VMEM)
    outs = pl.pallas_call(body, name=name, in_specs=[vm] * (4 * n), out_specs=[vm] * (3 * n),
                          out_shape=[_sds(w.shape, F32) for w in ws for _ in range(3)])(*ws, *gs, *ms, *vs)
    return [tuple(outs[3 * k:3 * k + 3]) for k in range(n)]


def _pair_sum(gfull, rcv, c_idx, name):
    nb, R, C = gfull.shape
    half = R // 2
    tr = _row_tile(half, _rows_for_bytes(2 << 20, C))
    nt = half // tr

    def body(c_ref, g_ref, r_ref, o_ref):
        o_ref[...] = _bf(g_ref[...] + r_ref[...])

    return pl.pallas_call(
        body, name=name,
        grid_spec=pltpu.PrefetchScalarGridSpec(
            num_scalar_prefetch=1, grid=(nb, nt),
            in_specs=[pl.BlockSpec((None, tr, C), lambda j, i, c_ref: (j, c_ref[0] * nt + i, 0)),
                      pl.BlockSpec((None, tr, C), lambda j, i, c_ref: (j, i, 0))],
            out_specs=pl.BlockSpec((None, tr, C), lambda j, i, c_ref: (j, i, 0))),
        out_shape=_sds((nb, half, C), BF16), compiler_params=_cp(2))(c_idx, gfull, rcv)


def _chip_sum(arrived, own, place, name):
    nb, H, C = arrived.shape
    tr = _row_tile(H, _rows_for_bytes(2 << 20, C))
    nt = H // tr

    def body(pl_ref, *refs):
        o_ref = refs[nb + 1]
        me = pl_ref[0]
        acc = None
        for k in range(nb):
            term = jnp.where(me == k, refs[nb][...], refs[k][...]).astype(F32)
            acc = term if acc is None else acc + term
        o_ref[...] = acc

    def other(k):
        return pl.BlockSpec((None, tr, C), lambda i, p: (jnp.where(p[0] == k, (k + 1) % nb, k), i, 0))

    return pl.pallas_call(
        body, name=name,
        grid_spec=pltpu.PrefetchScalarGridSpec(
            num_scalar_prefetch=1, grid=(nt,),
            in_specs=[other(k) for k in range(nb)] + [pl.BlockSpec((None, tr, C), lambda i, p: (p[0], i, 0))],
            out_specs=pl.BlockSpec((tr, C), lambda i, p: (p[1] * nt + i, 0))),
        out_shape=_sds((2 * H, C), F32), compiler_params=_cp(1))(place, *([arrived] * nb), own)


def _cast_into_slot(shard, place, name):
    R, C = shard.shape
    tr = _row_tile(R, 256)

    def body(pl_ref, s_ref, o_ref):
        o_ref[...] = _bf(s_ref[...])

    return pl.pallas_call(
        body, name=name,
        grid_spec=pltpu.PrefetchScalarGridSpec(
            num_scalar_prefetch=1, grid=(R // tr,),
            in_specs=[pl.BlockSpec((tr, C), lambda i, p: (i, 0))],
            out_specs=pl.BlockSpec((None, tr, C), lambda i, p: (p[0], i, 0))),
        out_shape=_sds((N_CHIPS, R, C), BF16), compiler_params=_cp(1))(place, shard)


def _place():
    x, y, c = lax.axis_index("x"), lax.axis_index("y"), lax.axis_index("c")
    chips = [(1 - x, y), (x, 1 - y), (1 - x, 1 - y)]
    return x, y, c, chips


def _chip_id(px, py):
    return 2 * px + py


def _remote(src, dst, send_sems, recv_sems, k, to):
    return pltpu.make_async_remote_copy(src_ref=src, dst_ref=dst, send_sem=send_sems.at[k], recv_sem=recv_sems.at[k],
                                        device_id=to, device_id_type=MESH)


def _proj_gathered(x, w_norm, slot, place, name, tm=1024, plan=None):
    M, K = x.shape
    nb, _, Nb = slot.shape
    half = K // 2
    nt = M // tm
    cx, cy = place[0] // 2, place[0] % 2
    order = jnp.stack([place[0], _chip_id(1 - cx, cy), _chip_id(cx, 1 - cy), _chip_id(1 - cx, 1 - cy)]).astype(jnp.int32)

    p_in = [] if plan is None else plan.ins + plan.inouts
    p_out = [] if plan is None else [_sds(a.shape, a.dtype) for a in plan.inouts] + plan.outs
    n_pi, n_po = len(p_in), len(p_out)

    def body(order_ref, x_ref, wn_ref, slot_in, *refs):
        o_ref, slot_ref, h_out = refs[n_pi:n_pi + 3]
        s0 = n_pi + 3 + n_po
        w_buf, hs, ici_send, ici_recv, pass_send, pass_recv, load_sem = refs[s0:s0 + 7]

        def carried():
            if plan is None:
                return [], [], []
            ins = refs[:len(plan.ins)]
            outs = refs[n_pi + 3:n_pi + 3 + n_po]
            return plan.copies(ins, outs[:len(plan.inouts)], outs[len(plan.inouts):], *refs[s0 + 7:])

        b, i = pl.program_id(0), pl.program_id(1)
        x, y, c, chips = _place()
        me = _chip_id(x, y)
        sib = (x, y, 1 - c)
        mine, other = pl.ds(c * half, half), pl.ds((1 - c) * half, half)

        def sent(k):
            blk = slot_ref.at[me, mine]
            return _remote(blk, blk, ici_send, ici_recv, k, (*chips[k], c))

        def landed(k):
            blk = slot_ref.at[_chip_id(*chips[k]), mine]
            return _remote(blk, blk, ici_send, ici_recv, k, (*chips[k], c))

        def passed(k, rows):
            blk = slot_ref.at[_chip_id(*chips[k]), rows]
            return _remote(blk, blk, pass_send, pass_recv, k, sib)

        @pl.when((b == 0) & (i == 0))
        def _():
            for k in range(len(chips)):
                sent(k).start()
            sends, _, local = carried()
            for cp in (*sends, *local):
                cp.start()

        for k in range(len(chips)):
            @pl.when((b == k + 1) & (i == 0))
            def _(k=k):
                landed(k).wait_recv()
                passed(k, mine).start()
                passed(k, other).wait_recv()

        @pl.when(i == 0)
        def _():
            load = pltpu.make_async_copy(slot_ref.at[order_ref[b]], w_buf, load_sem.at[0])
            load.start()
            load.wait()

        rows = pl.ds(pl.multiple_of(i * tm, tm), tm)
        keep_h = pltpu.make_async_copy(hs, h_out, load_sem.at[1])

        @pl.when(b == 0)
        def _():
            xv = x_ref[...]
            hs[rows, :] = _bf(xv * _rinv(xv) * wn_ref[...])

        @pl.when((b == 1) & (i == 0))
        def _():
            keep_h.start()

        o_ref[...] = _dot(hs[rows, :], w_buf[...], NN)

        @pl.when((b == nb - 1) & (i == nt - 1))
        def _():
            for k in range(len(chips)):
                sent(k).wait_send()
                passed(k, mine).wait_send()
            sends, recvs, local = carried()
            for cp in recvs:
                cp.wait_recv()
            for cp in sends:
                cp.wait_send()
            for cp in local:
                cp.wait()
            keep_h.wait()

    n_peers = N_CHIPS - 1
    return pl.pallas_call(
        body, name=name,
        grid_spec=pltpu.PrefetchScalarGridSpec(
            num_scalar_prefetch=1, grid=(nb, nt),
            in_specs=[pl.BlockSpec((tm, K), lambda b, i, o: (i, 0)), pl.BlockSpec((1, K), lambda b, i, o: (0, 0)), ANY]
            + [ANY] * n_pi,
            out_specs=[pl.BlockSpec((tm, Nb), lambda b, i, o: (i, o[b])), ANY, ANY] + [ANY] * n_po,
            scratch_shapes=[pltpu.VMEM((K, Nb), BF16), pltpu.VMEM((M, K), BF16)]
            + [pltpu.SemaphoreType.DMA((n_peers,))] * 4 + [pltpu.SemaphoreType.DMA((2,))]
            + ([] if plan is None else [pltpu.SemaphoreType.DMA((plan.n_sems,))] * 3)),
        out_shape=[_sds((M, nb * Nb), F32), _sds(slot.shape, slot.dtype), _sds((M, K), BF16)] + p_out,
        input_output_aliases={3: 1, **({} if plan is None else
                                       {4 + len(plan.ins) + a: 3 + a for a in range(len(plan.inouts))})},
        compiler_params=_cp(2))(order, x, w_norm, slot, *p_in)


def _gather_ici_plan(slots, wholes, part=None):
    ns, nw = len(slots), len(wholes)

    def copies(ins, ios, outs, send_sems, recv_sems, local_sems):
        x, y, c, chips = _place()
        me = _chip_id(x, y)
        sends, recvs = [], []
        for a in range(ns + nw):
            dst = ios[a] if a < ns else outs[a - ns]
            R = dst.shape[1]
            r0, nr = (0, R // 2) if part is None else part
            rows = pl.ds(c * (R // 2) + r0, nr) if a < ns else pl.ds(0, R)
            src = dst.at[me, rows] if a < ns else ins[a - ns]
            for j, chip in enumerate(chips):
                sends.append(_remote(src, dst.at[me, rows], send_sems, recv_sems, 3 * a + j, (*chip, c)))
                landed = dst.at[_chip_id(*chip), rows]
                recvs.append(_remote(landed, landed, send_sems, recv_sems, 3 * a + j, (*chip, c)))
        local = [pltpu.make_async_copy(ins[b], outs[b].at[me], local_sems.at[b]) for b in range(nw)]
        return sends, recvs, local

    return _Plan(copies, 3 * (ns + nw), ins=wholes, inouts=slots,
                 outs=[_sds((N_CHIPS, *s.shape), s.dtype) for s in wholes])


def _gather_pass_plan(slots):
    def copies(ins, ios, outs, send_sems, recv_sems, local_sems):
        x, y, c, chips = _place()
        sib = (x, y, 1 - c)
        sends, recvs = [], []
        for a, buf in enumerate(ios):
            half = buf.shape[1] // 2
            for j, chip in enumerate(chips):
                mine = buf.at[_chip_id(*chip), pl.ds(c * half, half)]
                other = buf.at[_chip_id(*chip), pl.ds((1 - c) * half, half)]
                sends.append(_remote(mine, mine, send_sems, recv_sems, 3 * a + j, sib))
                recvs.append(_remote(other, other, send_sems, recv_sems, 3 * a + j, sib))
        return sends, recvs, []

    return _Plan(copies, 3 * len(slots), inouts=slots)


def _pair_plan(grads):
    def copies(ins, ios, outs, send_sems, recv_sems, local_sems):
        x, y, c, _ = _place()
        sib = (x, y, 1 - c)
        sends, recvs = [], []
        for a, g in enumerate(ins):
            half = g.shape[1] // 2
            sends.append(_remote(g.at[:, pl.ds((1 - c) * half, half), :], outs[a], send_sems, recv_sems, a, sib))
            recvs.append(_remote(outs[a], outs[a], send_sems, recv_sems, a, sib))
        return sends, recvs, []

    return _Plan(copies, len(grads), ins=grads,
                 outs=[_sds((g.shape[0], g.shape[1] // 2, g.shape[2]), g.dtype) for g in grads])


def _chip_plan(parts):
    def copies(ins, ios, outs, send_sems, recv_sems, local_sems):
        x, y, c, chips = _place()
        me = _chip_id(x, y)
        sends, recvs = [], []
        for a, part in enumerate(ins):
            for j, chip in enumerate(chips):
                sends.append(_remote(part.at[_chip_id(*chip)], outs[a].at[me], send_sems, recv_sems, 3 * a + j, (*chip, c)))
                landed = outs[a].at[_chip_id(*chip)]
                recvs.append(_remote(landed, landed, send_sems, recv_sems, 3 * a + j, (*chip, c)))
        return sends, recvs, []

    return _Plan(copies, 3 * len(parts), ins=parts, outs=[_sds(p.shape, p.dtype) for p in parts])


def _all_sum(pack, fulls, name):
    R, C = pack.shape
    n = len(fulls)

    def body(p_ref, *refs):
        o_ref, halves = refs[n], refs[n + 1:2 * n + 1]
        buf, send_sems, recv_sems, pair_send, pair_recv = refs[2 * n + 1:]
        x, y, c, _ = _place()
        sib = (x, y, 1 - c)
        pair = []
        for a, full in enumerate(halves):
            H = full.shape[0] // 2
            mine = full.at[pl.ds(c * H, H)]
            cp = _remote(mine, mine, pair_send, pair_recv, a, sib)
            cp.start()
            pair.append(cp)
        me = 4 * x + 2 * y + c
        buf[me] = p_ref[...]
        cps = []
        for k in range(1, N_DEV):
            to = (x ^ (k >> 2), y ^ ((k >> 1) & 1), c ^ (k & 1))
            cp = _remote(p_ref, buf.at[me], send_sems, recv_sems, k - 1, to)
            cp.start()
            cps.append(cp)
        for k in range(1, N_DEV):
            frm = (x ^ (k >> 2), y ^ ((k >> 1) & 1), c ^ (k & 1))
            slot = buf.at[4 * frm[0] + 2 * frm[1] + frm[2]]
            _remote(slot, slot, send_sems, recv_sems, k - 1, frm).wait_recv()
        acc = buf[0]
        for k in range(1, N_DEV):
            acc = acc + buf[k]
        o_ref[...] = acc
        for cp in cps:
            cp.wait_send()
        for a, (full, cp) in enumerate(zip(halves, pair)):
            H = full.shape[0] // 2
            other = full.at[pl.ds((1 - c) * H, H)]
            _remote(other, other, pair_send, pair_recv, a, sib).wait_recv()
            cp.wait_send()

    vm = pl.BlockSpec(memory_space=pltpu.VMEM)
    res = pl.pallas_call(
        body, name=name, in_specs=[vm] + [ANY] * n, out_specs=[vm] + [ANY] * n,
        out_shape=[_sds((R, C), F32)] + [_sds(f.shape, f.dtype) for f in fulls],
        input_output_aliases={1 + a: 1 + a for a in range(n)},
        scratch_shapes=[pltpu.VMEM((N_DEV, R, C), F32), pltpu.SemaphoreType.DMA((N_DEV - 1,)),
                        pltpu.SemaphoreType.DMA((N_DEV - 1,)), pltpu.SemaphoreType.DMA((n,)),
                        pltpu.SemaphoreType.DMA((n,))])(pack, *fulls)
    return res[0], list(res[1:])


def _local_step(xs, tgt, p, ex):
    proj, h1 = ex.project(xs, p["pre_mix_norm"])
    biases = _relbias_fwd(p["rel_bias"], "rel_bias_fwd")
    fw = []
    for g in range(N_GROUPS):
        res, got = _attn_fwd(proj, biases[g], g, f"attn_fwd{g}", plans=ex.carry(f"attn_fwd{g}"))
        ex.done(f"attn_fwd{g}", got)
        fw.append(res)
    (yh, o_h, states), got = _hgrn_fwd(proj, p["hgrn_lb_raw"], p["hgrn_norm"], "hgrn_fwd", plans=ex.carry("hgrn_fwd"))
    ex.done("hgrn_fwd", got)
    W_a, W_h, W_out = ex.weight("w_branch_attn"), ex.weight("w_branch_hgrn"), ex.weight("w_out")
    (y, lse, za, zh, merged), got = _branch_fwd([t[0] for t in fw], [t[1] for t in fw], yh, proj, W_a, W_h,
                                                "branch_fwd", plans=ex.carry("branch_fwd"))
    ex.done("branch_fwd", got)
    W_up, conv_w = ex.weight("w_up"), ex.weight("conv_w")
    mo, x1, h2 = _mix_out(merged, W_out, xs, p["post_mix_norm"], p["pre_ffn_norm"], "mix_out")
    u, got = _mm_nn_blk(h2, W_up, "ffn_up", tm=1024, plans=ex.carry("ffn_up"))
    ex.done("ffn_up", got)
    a, got = _conv_gelu_fwd(u, conv_w, p["conv_b"], "conv_gelu_fwd", plans=ex.carry("conv_gelu_fwd"))
    ex.done("conv_gelu_fwd", got)
    W_down = ex.weight("w_down")
    dx2, dff, g_post_ffn, loss = _loss_head(a, W_down, x1, tgt, p["post_ffn_norm"], "ffn_down_loss")

    ex.grad("w_down", _mm_tn(a, dff, "g_w_down").reshape(N_CHIPS, D_FF // N_CHIPS, D_MODEL))
    (dcg, dcv, gwg, gwv, gbg, gbv), got = _conv_gelu_bwd(u, dff, W_down, conv_w, p["conv_b"], "conv_gelu_bwd",
                                                          plans=ex.carry("conv_gelu_bwd"))
    ex.done("conv_gelu_bwd", got)
    g_conv_w = jnp.concatenate([gwg, gwv], axis=1)
    g_conv_b = jnp.concatenate([gbg, gbv], axis=1)
    du, got = _conv_input_bwd(dcg, dcv, conv_w, "conv_input_bwd", plans=ex.carry("conv_input_bwd"))
    ex.done("conv_input_bwd", got)
    dx1, g_pre_ffn = _mm_nt_prenorm_bwd(du, W_up, x1, p["pre_ffn_norm"], dx2, "d_ffn_in")
    ex.grad("w_up", _mm_tn_blk(h2, du, N_CHIPS, "g_w_up"))
    (dmo, dmerged, g_post_mix), got = _postnorm_bwd(dx1, mo, p["post_mix_norm"], W_out, "post_mix_norm_bwd",
                                                    plans=ex.carry("post_mix_norm_bwd"))
    ex.done("post_mix_norm_bwd", got)
    ex.grad("w_out", _mm_tn(merged, dmo, "g_w_out").reshape(N_CHIPS, D_MODEL // N_CHIPS, D_MODEL))
    (dza, dzh, dg0, dg1, dy, dyh), got = _branch_bwd(dmerged, za, zh, proj, W_a, W_h, "branch_bwd",
                                                     plans=ex.carry("branch_bwd"))
    ex.done("branch_bwd", got)
    ex.grad("w_branch_attn", _mm_tn_blk(y, dza, N_CHIPS, "g_w_branch_attn", together=True))
    ex.grad("w_branch_hgrn", _mm_tn_blk(yh, dzh, N_CHIPS, "g_w_branch_hgrn", together=True))
    dqkv, dbs = [], []
    for g in range(N_GROUPS):
        parts, db, got = _attn_bwd(proj, biases[g], lse, y, dy, g, f"attn_bwd{g}", plans=ex.carry(f"attn_bwd{g}"))
        ex.done(f"attn_bwd{g}", got)
        dqkv += parts
        dbs.append(db)
    g_rel_bias = _relbias_bwd(dbs, "rel_bias_bwd")
    dproj, g_lb_raw, g_hgrn_norm = _hgrn_bwd(proj, p["hgrn_lb_raw"], p["hgrn_norm"], o_h, states, dyh, dqkv,
                                             [dg0, dg1], "hgrn_bwd")
    for piece in W_IN_PIECES:
        g, got = _mm_tn_blk(h1, dproj, N_CHIPS, f"g_{piece}", x_cols=W_IN_ROWS[piece],
                            plans=ex.carry(f"g_{piece}"))
        ex.done(f"g_{piece}", got)
        ex.grad(piece, g)
    dh1, got = _mm_nt_blk(dproj, ex.weight("w_in"), "d_proj_in", plans=ex.carry("d_proj_in"))
    ex.done("d_proj_in", got)
    (grad_x, g_pre_mix), got = _prenorm_bwd(dh1, xs, p["pre_mix_norm"], dx1, "pre_mix_norm_bwd",
                                            plans=ex.carry("pre_mix_norm_bwd"))
    ex.done("pre_mix_norm_bwd", got)
    small = dict(pre_mix_norm=g_pre_mix, rel_bias=g_rel_bias, hgrn_lb_raw=g_lb_raw, hgrn_norm=g_hgrn_norm,
                 post_mix_norm=g_post_mix, pre_ffn_norm=g_pre_ffn, conv_w=g_conv_w, conv_b=g_conv_b,
                 post_ffn_norm=g_post_ffn)
    return loss, grad_x, small


SMALL = ("pre_mix_norm", "rel_bias", "hgrn_lb_raw", "hgrn_norm", "post_mix_norm", "pre_ffn_norm", "conv_w", "conv_b",
         "post_ffn_norm")
BIG = ("w_in", "w_up", "w_down", "w_out", "w_branch_attn", "w_branch_hgrn")
WEIGHTS = ("pre_mix_norm", "w_in", "rel_bias", "hgrn_lb_raw", "hgrn_norm", "w_branch_attn", "w_branch_hgrn", "w_out",
           "post_mix_norm", "pre_ffn_norm", "w_up", "conv_w", "conv_b", "w_down", "post_ffn_norm")
MIXER = ("w_out", "w_branch_attn", "w_branch_hgrn")

ICI_PARTS = {"gather_ici_1of3": (0, 128), "gather_ici_2of3": (128, 208), "gather_ici_3of3": (336, 176)}
SCHEDULE = {
    "proj_in": [("gather_ici_cw", MIXER)],
    "attn_fwd0": [("gather_ici_1of3", ("w_up",))],
    "attn_fwd1": [("gather_ici_2of3", ("w_up",))],
    "attn_fwd2": [("gather_pass", MIXER), ("gather_ici_3of3", ("w_up",))],
    "hgrn_fwd": [("gather_pass", ("w_up",))],
    "ffn_up": [("gather_ici", ("w_down",))],
    "conv_gelu_fwd": [("gather_pass", ("w_down",))],
    "conv_gelu_bwd": [("pair", ("w_down",))],
    "conv_input_bwd": [("chip", ("w_down",))],
    "post_mix_norm_bwd": [("pair", ("w_up",))],
    "attn_bwd0": [("chip", ("w_up",)), ("pair", MIXER)],
    "attn_bwd1": [("chip", MIXER)],
    "g_w_in_b": [("pair", ("w_in_a",))],
    "d_proj_in": [("chip", ("w_in_a",)), ("pair", ("w_in_b",))],
    "pre_mix_norm_bwd": [("chip", ("w_in_b",))],
}
W_IN_ROWS = dict(w_in_a=(0, 768), w_in_b=(3, 256))
W_IN_PIECES = tuple(W_IN_ROWS)
REDUCED = W_IN_PIECES + BIG[1:]


class _Exchange:
    def __init__(self, place, slots, conv_w_shard):
        self.place, self.slots, self.conv_w_shard = place, dict(slots), conv_w_shard
        self.conv_w = None
        self.g, self.from_sibling, self.pair_sums, self.arrived = {}, {}, {}, {}
        self.pending = []

    def weight(self, name):
        if name == "conv_w":
            return self.conv_w
        w = self.slots[name]
        return w.reshape(-1, D_MODEL) if name in ("w_out", "w_down") else w

    def project(self, x, w_norm):
        (plan,) = self.carry("proj_in")
        proj, self.slots["w_in"], h, *got = _proj_gathered(x, w_norm, self.slots["w_in"], self.place, "proj_in",
                                                           plan=plan)
        self.done("proj_in", [got])
        return proj, h

    def grad(self, name, g):
        self.g[name] = g

    def carry(self, point):
        plans = []
        self.pending = SCHEDULE.get(point, [])
        for kind, names in self.pending:
            if kind in ("gather_ici", "gather_ici_cw") or kind in ICI_PARTS:
                wholes = [self.conv_w_shard] if kind == "gather_ici_cw" else []
                plans.append(_gather_ici_plan([self.slots[n] for n in names], wholes, ICI_PARTS.get(kind)))
            elif kind == "gather_pass":
                plans.append(_gather_pass_plan([self.slots[n] for n in names]))
            elif kind == "pair":
                plans.append(_pair_plan([self.g[n] for n in names]))
            else:
                for n in names:
                    self.pair_sums[n] = _pair_sum(self.g[n], self.from_sibling[n], self.place[1:2], f"pair_sum_{n}")
                plans.append(_chip_plan([self.pair_sums[n] for n in names]))
        return plans

    def done(self, point, carried):
        for (kind, names), got in zip(self.pending, carried):
            if kind in ("gather_ici", "gather_ici_cw", "gather_pass") or kind in ICI_PARTS:
                self.slots.update(zip(names, got))
                if kind == "gather_ici_cw":
                    self.conv_w = got[len(names)].transpose(1, 0, 2).reshape(3, 2 * D_FF)
            elif kind == "pair":
                self.from_sibling.update(zip(names, got))
            else:
                self.arrived.update(zip(names, got))

    def reduced_halves(self):
        return [_chip_sum(self.arrived[n], self.pair_sums[n], self.place, f"chip_sum_{n}") for n in REDUCED]


def kernel(x, pre_mix_norm, w_in, rel_bias, hgrn_lb_raw, hgrn_norm, w_branch_attn, w_branch_hgrn, w_out, post_mix_norm, pre_ffn_norm, w_up, conv_w, conv_b, w_down, post_ffn_norm, loss_target, m_pre_mix_norm, m_w_in, m_rel_bias, m_hgrn_lb_raw, m_hgrn_norm, m_w_branch_attn, m_w_branch_hgrn, m_w_out, m_post_mix_norm, m_pre_ffn_norm, m_w_up, m_conv_w, m_conv_b, m_w_down, m_post_ffn_norm, v_pre_mix_norm, v_w_in, v_rel_bias, v_hgrn_lb_raw, v_hgrn_norm, v_w_branch_attn, v_w_branch_hgrn, v_w_out, v_post_mix_norm, v_pre_ffn_norm, v_w_up, v_conv_w, v_conv_b, v_w_down, v_post_ffn_norm):
    w = dict(pre_mix_norm=pre_mix_norm, w_in=w_in, rel_bias=rel_bias, hgrn_lb_raw=hgrn_lb_raw, hgrn_norm=hgrn_norm,
             w_branch_attn=w_branch_attn, w_branch_hgrn=w_branch_hgrn, w_out=w_out, post_mix_norm=post_mix_norm,
             pre_ffn_norm=pre_ffn_norm, w_up=w_up, conv_w=conv_w, conv_b=conv_b, w_down=w_down,
             post_ffn_norm=post_ffn_norm)
    m = dict(pre_mix_norm=m_pre_mix_norm, w_in=m_w_in, rel_bias=m_rel_bias, hgrn_lb_raw=m_hgrn_lb_raw,
             hgrn_norm=m_hgrn_norm, w_branch_attn=m_w_branch_attn, w_branch_hgrn=m_w_branch_hgrn, w_out=m_w_out,
             post_mix_norm=m_post_mix_norm, pre_ffn_norm=m_pre_ffn_norm, w_up=m_w_up, conv_w=m_conv_w,
             conv_b=m_conv_b, w_down=m_w_down, post_ffn_norm=m_post_ffn_norm)
    v = dict(pre_mix_norm=v_pre_mix_norm, w_in=v_w_in, rel_bias=v_rel_bias, hgrn_lb_raw=v_hgrn_lb_raw,
             hgrn_norm=v_hgrn_norm, w_branch_attn=v_w_branch_attn, w_branch_hgrn=v_w_branch_hgrn, w_out=v_w_out,
             post_mix_norm=v_post_mix_norm, pre_ffn_norm=v_pre_ffn_norm, w_up=v_w_up, conv_w=v_conv_w,
             conv_b=v_conv_b, w_down=v_w_down, post_ffn_norm=v_post_ffn_norm)
    shard2d = {n: (w[n][0] if w[n].ndim == 3 else w[n]) for n in WEIGHTS}
    chip = 2 * lax.axis_index("x") + lax.axis_index("y")
    core = lax.axis_index("c")

    place = jnp.stack([chip, core]).astype(jnp.int32)
    slots = {n: _cast_into_slot(shard2d[n], place, f"cast_{n}") for n in BIG}
    ex = _Exchange(place, slots, shard2d["conv_w"])
    loss, grad_x, small = _local_step(x[0], loss_target[0], {n: w[n] for n in SMALL if n != "conv_w"}, ex)

    flat = [small[n].reshape(-1) for n in SMALL] + [loss.reshape(-1)]
    sizes = [t.shape[0] for t in flat]
    summed, wholes = _all_sum(jnp.concatenate(flat).reshape(-1, LANES), ex.reduced_halves(), "sum_small")
    summed = summed.reshape(-1)
    offs = [sum(sizes[:i]) for i in range(len(sizes))]
    grads = {}
    for n, o, sz in zip(SMALL, offs, sizes):
        grads[n] = summed[o:o + sz].reshape(small[n].shape)
    loss_total = summed[offs[-1]]
    cw = 2 * D_FF // N_CHIPS
    grads["conv_w"] = lax.dynamic_slice(grads["conv_w"], (0, chip * cw), (3, cw))

    big = dict(zip(REDUCED, wholes))
    big["w_in"] = jnp.concatenate([big.pop(n) for n in W_IN_PIECES], axis=0)
    grads.update(big)

    m2d = {n: m[n].reshape(shard2d[n].shape) for n in WEIGHTS}
    v2d = {n: v[n].reshape(shard2d[n].shape) for n in WEIGHTS}
    updated = dict(zip(SMALL, _adamw_small([shard2d[n] for n in SMALL], [grads[n] for n in SMALL],
                                           [m2d[n] for n in SMALL], [v2d[n] for n in SMALL], "adamw_small")))
    for n in BIG:
        updated[n] = _adamw(shard2d[n], grads[n], m2d[n], v2d[n], f"adamw_{n}")
    out_g, out_d, out_m, out_v = [], [], [], []
    for n in WEIGHTS:
        d2, m2, v2 = updated[n]
        shape = w[n].shape
        out_g.append(grads[n].reshape(shape))
        out_d.append(d2.reshape(shape))
        out_m.append(m2.reshape(shape))
        out_v.append(v2.reshape(shape))
    return (loss_total, grad_x[None], *out_g, *out_d, *out_m, *out_v)
```

```python
import functools
import math

import jax
import jax.numpy as jnp
from jax import lax
from jax.experimental import pallas as pl
from jax.experimental.pallas import tpu as pltpu

F32 = jnp.float32
BF16 = jnp.bfloat16
MESH = pl.DeviceIdType.MESH

D_MODEL = 1024
N_GROUPS = 3
DILATIONS = (1, 4, 16)
HEADS = 8
HEAD_DIM = 64
GROUP_W = HEADS * HEAD_DIM
QKV_W = N_GROUPS * 3 * GROUP_W
BLK = 128
NEG_INF = -1e30
NUM_BUCKETS = 32
MAX_EXACT = 16
MAX_DISTANCE = 2048
HG_HEADS = 4
HG_DK = 128
HG_W = HG_HEADS * HG_DK
HG_CHUNK = 32
HG_TILE = 256
IN_W = QKV_W + 4 * HG_W + 2 * D_MODEL
D_FF = 2816
EPS = 1e-6
N_CHIPS = 4
N_DEV = 8
LANES = 128

ADAM_LR, ADAM_B1, ADAM_B2, ADAM_EPS, ADAM_WD, ADAM_STEP = 0.001, 0.9, 0.999, 1e-08, 0.01, 10

VMEM_LIMIT = 56 * 1024 * 1024


def _cp(n_axes):
    return pltpu.CompilerParams(dimension_semantics=("arbitrary",) * n_axes, vmem_limit_bytes=VMEM_LIMIT)


def _sds(shape, dtype):
    return jax.ShapeDtypeStruct(tuple(shape), dtype)


def _sigmoid(v):
    return 1.0 / (1.0 + jnp.exp(-v))


def _bf(v):
    return v.astype(BF16)


def _dot(a, b, dims):
    return lax.dot_general(a, b, (dims, ((), ())), preferred_element_type=F32)


NN = ((1,), (0,))
NT = ((1,), (1,))
TN = ((0,), (0,))

ANY = pl.BlockSpec(memory_space=pl.ANY)


class _Plan:
    def __init__(self, copies, n_sems, ins=(), inouts=(), outs=()):
        self.copies, self.n_sems = copies, n_sems
        self.ins, self.inouts, self.outs = list(ins), list(inouts), list(outs)


def _call(body, plans=None, *, name, grid, in_specs, out_specs, out_shape, args, scratch_shapes=()):
    plans = list(plans or ())
    in_specs, out_specs, out_shape = list(in_specs), list(out_specs), list(out_shape)
    scratch_shapes = list(scratch_shapes)
    n_in, n_out, n_scr = len(in_specs), len(out_specs), len(scratch_shapes)
    x_in, x_out, aliases, spans = [], [], {}, []
    for p in plans:
        i0, o0 = len(x_in), len(x_out)
        x_in += p.ins
        for a in p.inouts:
            aliases[n_in + len(x_in)] = n_out + len(x_out)
            x_in.append(a)
            x_out.append(_sds(a.shape, a.dtype))
        x_out += p.outs
        spans.append((i0, len(p.ins), o0, len(p.inouts), len(p.outs)))
    sems = [pltpu.SemaphoreType.DMA((p.n_sems,)) for p in plans for _ in range(3)]

    def wrapped(*refs):
        xi = refs[n_in:n_in + len(x_in)]
        base = n_in + len(x_in)
        xo = refs[base + n_out:base + n_out + len(x_out)]
        sbase = base + n_out + len(x_out)
        xs = refs[sbase + n_scr:]
        ids = [pl.program_id(k) for k in range(len(grid))]
        first = functools.reduce(jnp.logical_and, [i == 0 for i in ids])
        last = functools.reduce(jnp.logical_and, [i == g - 1 for i, g in zip(ids, grid)])

        def descriptors(k):
            i0, ni, o0, nio, no = spans[k]
            return plans[k].copies(xi[i0:i0 + ni], xo[o0:o0 + nio], xo[o0 + nio:o0 + nio + no], *xs[3 * k:3 * k + 3])

        @pl.when(first)
        def _():
            for k in range(len(plans)):
                sends, _, local = descriptors(k)
                for cp in (*sends, *local):
                    cp.start()

        body(*refs[:n_in], *refs[base:base + n_out], *refs[sbase:sbase + n_scr])

        @pl.when(last)
        def _():
            for k in range(len(plans)):
                sends, recvs, local = descriptors(k)
                for cp in recvs:
                    cp.wait_recv()
                for cp in sends:
                    cp.wait_send()
                for cp in local:
                    cp.wait()

    res = pl.pallas_call(
        wrapped if plans else body, name=name, grid=grid, in_specs=in_specs + [ANY] * len(x_in),
        out_specs=out_specs + [ANY] * len(x_out), out_shape=out_shape + x_out, input_output_aliases=aliases,
        scratch_shapes=scratch_shapes + sems, compiler_params=_cp(len(grid)))(*args, *x_in)
    res = list(res)
    carried = [res[n_out + o0:n_out + o0 + nio + no] for (_, _, o0, nio, no) in spans]
    return res[:n_out], carried


def _mm_nn_blk(a, wg, name, tm=512, plans=None):
    M, K = a.shape
    nb, _, Nb = wg.shape

    def body(a_ref, w_ref, o_ref):
        o_ref[...] = _dot(_bf(a_ref[...]), w_ref[...], NN)

    (out,), carried = _call(
        body, plans, name=name, grid=(nb, M // tm),
        in_specs=[pl.BlockSpec((tm, K), lambda j, i: (i, 0)), pl.BlockSpec((None, K, Nb), lambda j, i: (j, 0, 0))],
        out_specs=[pl.BlockSpec((tm, Nb), lambda j, i: (i, j))],
        out_shape=[_sds((M, nb * Nb), F32)], args=(a, wg))
    return out if plans is None else (out, carried)


def _mm_nt_blk(dy, wg, name, tm=1024, plans=None):
    M = dy.shape[0]
    nb, K, Nb = wg.shape

    def body(dy_ref, w_ref, o_ref):
        j = pl.program_id(1)
        r = _dot(_bf(dy_ref[...]), w_ref[...], NT)

        @pl.when(j == 0)
        def _():
            o_ref[...] = r

        @pl.when(j > 0)
        def _():
            o_ref[...] += r

    (out,), carried = _call(
        body, plans, name=name, grid=(M // tm, nb),
        in_specs=[pl.BlockSpec((tm, Nb), lambda i, j: (i, j)), pl.BlockSpec((None, K, Nb), lambda i, j: (j, 0, 0))],
        out_specs=[pl.BlockSpec((tm, K), lambda i, j: (i, 0))],
        out_shape=[_sds((M, K), F32)], args=(dy, wg))
    return out if plans is None else (out, carried)


def _mm_nt_prenorm_bwd(dy, wg, xin, w, dres, name, tm=1024):
    M = dy.shape[0]
    nb, K, Nb = wg.shape

    def body(dy_ref, w_ref, x_ref, wn_ref, dres_ref, dx_ref, dw_ref, acc):
        i, j = pl.program_id(0), pl.program_id(1)
        r = _dot(_bf(dy_ref[...]), w_ref[...], NT)

        @pl.when(j == 0)
        def _():
            acc[...] = r

        @pl.when(j > 0)
        def _():
            acc[...] += r

        @pl.when(j == nb - 1)
        def _():
            xv = x_ref[...]
            rinv = _rinv(xv)
            xhat = xv * rinv
            dh = acc[...]
            dx_ref[...] = dres_ref[...] + _norm_bwd(dh, xhat, rinv, wn_ref[...])
            part = jnp.sum(dh * xhat, axis=0, keepdims=True)

            @pl.when(i == 0)
            def _():
                dw_ref[...] = part

            @pl.when(i > 0)
            def _():
                dw_ref[...] += part

    row = pl.BlockSpec((tm, K), lambda i, j: (i, 0))
    vec = pl.BlockSpec((1, K), lambda i, j: (0, 0))
    return pl.pallas_call(
        body, name=name, grid=(M // tm, nb),
        in_specs=[pl.BlockSpec((tm, Nb), lambda i, j: (i, j)), pl.BlockSpec((None, K, Nb), lambda i, j: (j, 0, 0)),
                  row, vec, row],
        out_specs=[row, vec], out_shape=[_sds((M, K), F32), _sds((1, K), F32)],
        scratch_shapes=[pltpu.VMEM((tm, K), F32)], compiler_params=_cp(2))(dy, wg, xin, w, dres)


def _mm_tn_blk(x, dy, nb, name, tk=2048, x_cols=None, plans=None, together=False):
    T, Mx = x.shape
    xk, Mx = (0, Mx) if x_cols is None else x_cols
    Nb = dy.shape[1] // nb
    nj = nb if together else 1

    def body(x_ref, dy_ref, o_ref):
        t = pl.program_id(1)
        r = _dot(_bf(x_ref[...]), _bf(dy_ref[...]), TN)
        for j in range(nj):
            rj = r[:, j * Nb:(j + 1) * Nb]

            @pl.when(t == 0)
            def _():
                o_ref[j] = rj

            @pl.when(t > 0)
            def _():
                o_ref[j] += rj

    (out,), carried = _call(
        body, plans, name=name, grid=(nb // nj, T // tk),
        in_specs=[pl.BlockSpec((tk, Mx), lambda j, t: (t, xk)), pl.BlockSpec((tk, nj * Nb), lambda j, t: (t, j))],
        out_specs=[pl.BlockSpec((nj, Mx, Nb), lambda j, t: (j, 0, 0))],
        out_shape=[_sds((nb, Mx, Nb), F32)], args=(x, dy))
    return out if plans is None else (out, carried)


def _mm_tn(x, dy, name, tk=1024):
    T, Mx = x.shape
    N = dy.shape[1]

    def body(x_ref, dy_ref, o_ref):
        t = pl.program_id(0)
        r = _dot(_bf(x_ref[...]), _bf(dy_ref[...]), TN)

        @pl.when(t == 0)
        def _():
            o_ref[...] = r

        @pl.when(t > 0)
        def _():
            o_ref[...] += r

    return pl.pallas_call(
        body, name=name, grid=(T // tk,),
        in_specs=[pl.BlockSpec((tk, Mx), lambda t: (t, 0)), pl.BlockSpec((tk, N), lambda t: (t, 0))],
        out_specs=pl.BlockSpec((Mx, N), lambda t: (0, 0)),
        out_shape=_sds((Mx, N), F32), compiler_params=_cp(1))(x, dy)


def _tile(arr, bw, col=lambda c: 0):
    return ("tile", arr, bw, col)


def _full(arr):
    return ("full", arr)


def _out_tile(width, dtype, bw, col=lambda c: 0):
    return ("tile", width, dtype, bw, col)


def _out_acc(rows, width, bw, col=lambda c: 0):
    return ("acc", rows, width, bw, col)


def _rows_call(name, body, n_rows, tm, ncol, ins, outs, plans=None):
    in_specs, args = [], []
    for e in ins:
        if e[0] == "tile":
            _, arr, bw, col = e
            in_specs.append(pl.BlockSpec((tm, bw), functools.partial(lambda c, i, col: (i, col(c)), col=col)))
        else:
            arr = e[1]
            in_specs.append(pl.BlockSpec(arr.shape, functools.partial(lambda c, i, nd: (0,) * nd, nd=arr.ndim)))
        args.append(arr)
    out_specs, out_shape = [], []
    for e in outs:
        if e[0] == "tile":
            _, width, dtype, bw, col = e
            out_specs.append(pl.BlockSpec((tm, bw), functools.partial(lambda c, i, col: (i, col(c)), col=col)))
            out_shape.append(_sds((n_rows, width), dtype))
        else:
            _, rows, width, bw, col = e
            out_specs.append(pl.BlockSpec((rows, bw), functools.partial(lambda c, i, col: (0, col(c)), col=col)))
            out_shape.append(_sds((rows, width), F32))
    out, carried = _call(body, plans, name=name, grid=(ncol, n_rows // tm), in_specs=in_specs, out_specs=out_specs,
                         out_shape=out_shape, args=args)
    return out if plans is None else (out, carried)


def _acc(ref, val):
    i = pl.program_id(1)

    @pl.when(i == 0)
    def _():
        ref[...] = val

    @pl.when(i > 0)
    def _():
        ref[...] += val


def _rinv(z):
    return lax.rsqrt(jnp.mean(z * z, axis=-1, keepdims=True) + EPS)


def _norm_bwd(dy, zhat, r, w):
    dyw = dy * w
    return r * (dyw - zhat * jnp.mean(dyw * zhat, axis=-1, keepdims=True))


def _prenorm_bwd(dh, xin, w, dres, name, plans=None):
    def body(dh_ref, x_ref, w_ref, dres_ref, dx_ref, dw_ref):
        xv = x_ref[...]
        r = _rinv(xv)
        xhat = xv * r
        dhv = dh_ref[...]
        dx_ref[...] = dres_ref[...] + _norm_bwd(dhv, xhat, r, w_ref[...])
        _acc(dw_ref, jnp.sum(dhv * xhat, axis=0, keepdims=True))

    return _rows_call(name, body, xin.shape[0], 512, 1,
                      [_tile(dh, D_MODEL), _tile(xin, D_MODEL), _full(w), _tile(dres, D_MODEL)],
                      [_out_tile(D_MODEL, F32, D_MODEL), _out_acc(1, D_MODEL, D_MODEL)], plans)


def _postnorm_bwd(dout, z, w, w_mat, name, plans=None):
    def body(do_ref, z_ref, w_ref, wm_ref, dz_ref, dm_ref, dw_ref):
        zv = z_ref[...]
        r = _rinv(zv)
        zhat = zv * r
        dov = do_ref[...]
        dz = _bf(_norm_bwd(dov, zhat, r, w_ref[...]))
        dz_ref[...] = dz
        dm_ref[...] = _dot(dz, wm_ref[...], NT)
        _acc(dw_ref, jnp.sum(dov * zhat, axis=0, keepdims=True))

    return _rows_call(name, body, z.shape[0], 512, 1,
                      [_tile(dout, D_MODEL), _tile(z, D_MODEL), _full(w), _full(w_mat)],
                      [_out_tile(D_MODEL, BF16, D_MODEL), _out_tile(D_MODEL, F32, D_MODEL),
                       _out_acc(1, D_MODEL, D_MODEL)], plans)


def _t5_bucket(dist):
    n = jnp.maximum(dist, 0)
    nf = jnp.maximum(n, 1).astype(F32)
    large = MAX_EXACT + (jnp.log(nf / MAX_EXACT) / math.log(MAX_DISTANCE / MAX_EXACT)
                         * (NUM_BUCKETS - MAX_EXACT)).astype(jnp.int32)
    large = jnp.minimum(large, NUM_BUCKETS - 1)
    return jnp.where(n < MAX_EXACT, n, large)


def _band_rel():
    return jnp.arange(BLK)[:, None] + BLK - jnp.arange(2 * BLK)[None, :]


def _band_valid():
    rel = _band_rel()
    window = (rel >= 0) & (rel <= BLK)
    first = window & (jnp.arange(2 * BLK)[None, :] >= BLK)
    return jnp.stack([first, window]).astype(F32).reshape(2, 1, BAND)


RES_UNROLL = 8
PAIR = LANES // HEAD_DIM


def _pair_lanes():
    first = lax.broadcasted_iota(jnp.int32, (1, LANES), 1) < HEAD_DIM
    return first, jnp.logical_not(first)


def _heads_per_step(d):
    return HEADS if d == 1 else LANES // HEAD_DIM


def _sub_rows(r, d):
    return pl.ds(r, BLK, stride=d) if d > 1 else pl.ds(0, BLK)


def _for_residues(d, fn):
    if d <= RES_UNROLL:
        for r in range(d):
            fn(r)
    else:
        def group(i, carry):
            for k in range(RES_UNROLL):
                fn(i * RES_UNROLL + k)
            return carry

        lax.fori_loop(0, d // RES_UNROLL, group, 0)


def _attn_specs(d, g, qblock):
    cw = _heads_per_step(d) * HEAD_DIM

    def col(part, hp):
        return (g * 3 + part) * (GROUP_W // cw) + hp

    def cur(part):
        return pl.BlockSpec((d * BLK, cw), lambda hp, n: (qblock(n), col(part, hp)))

    def prev(part):
        return pl.BlockSpec((d * BLK, cw), lambda hp, n: (jnp.maximum(qblock(n) - 1, 0), col(part, hp)))

    return cur, prev


def _attn_fwd(proj, bias, g, name, plans=None):
    S = proj.shape[0]
    d = DILATIONS[g]
    NB = S // (d * BLK)
    hps = _heads_per_step(d)

    def body(q_ref, kp_ref, kc_ref, vp_ref, vc_ref, b_ref, o_ref, lse_ref):
        hp = pl.program_id(0)
        later = jnp.minimum(pl.program_id(1), 1)

        def residue(r):
            rows = _sub_rows(r, d)
            q2 = q_ref[rows, :]
            k2 = jnp.concatenate([kp_ref[rows, :], kc_ref[rows, :]], axis=0)
            v2 = jnp.concatenate([vp_ref[rows, :], vc_ref[rows, :]], axis=0)
            outs, lses = [], []
            for pp in range(hps // PAIR):
                ps = slice(pp * LANES, (pp + 1) * LANES)
                qp, kp, vp = _bf(q2[:, ps]), _bf(k2[:, ps]), _bf(v2[:, ps])
                o_h, lse_h = [], []
                for hh, own in enumerate(_pair_lanes()):
                    s = _dot(qp, jnp.where(own, kp, 0), NT) * (HEAD_DIM ** -0.5) + b_ref[later, hp * hps + pp * PAIR + hh]
                    m = jnp.max(s, axis=-1, keepdims=True)
                    p = jnp.exp(s - m)
                    l = jnp.sum(p, axis=-1, keepdims=True)
                    o_h.append(_dot(_bf(p), vp, NN) / l)
                    lse_h.append(m + jnp.log(l))
                first = _pair_lanes()[0]
                outs.append(jnp.where(first, o_h[0], o_h[1]))
                lses.append(jnp.where(first, lse_h[0], lse_h[1]))
            o_ref[rows, :] = outs[0] if len(outs) == 1 else jnp.concatenate(outs, axis=1)
            lse_ref[rows, :] = lses[0] if len(lses) == 1 else jnp.concatenate(lses, axis=1)

        _for_residues(d, residue)

    cur, prev = _attn_specs(d, g, lambda n: n)
    out = pl.BlockSpec((d * BLK, hps * HEAD_DIM), lambda hp, n: (n, hp))
    res, carried = _call(
        body, plans, name=name, grid=(HEADS // hps, NB),
        in_specs=[cur(0), prev(1), cur(1), prev(2), cur(2),
                  pl.BlockSpec((2, HEADS, BLK, 2 * BLK), lambda hp, n: (0, 0, 0, 0))],
        out_specs=[out, out], out_shape=[_sds((S, GROUP_W), F32)] * 2,
        args=(proj, proj, proj, proj, proj, bias))
    return res if plans is None else (res, carried)


def _attn_bwd(proj, bias, lse, y, dy, g, name, plans=None):
    S = proj.shape[0]
    d = DILATIONS[g]
    NB = S // (d * BLK)
    hps = _heads_per_step(d)

    def body(q_ref, kp_ref, kc_ref, vp_ref, vc_ref, b_ref, l_ref, y_ref, dy_ref,
             dq_ref, dk_ref, dv_ref, db_ref, ck_ref, cv_ref):
        hp, n = pl.program_id(0), pl.program_id(1)

        @pl.when((hp == 0) & (n == 0))
        def _():
            db_ref[...] = jnp.zeros_like(db_ref)

        @pl.when(n == 0)
        def _():
            ck_ref[...] = jnp.zeros_like(ck_ref)
            cv_ref[...] = jnp.zeros_like(cv_ref)

        @pl.when(n < NB)
        def _():
            later = jnp.minimum(n, 1)

            def residue(r):
                rows = _sub_rows(r, d)
                q2 = q_ref[rows, :]
                k2 = jnp.concatenate([kp_ref[rows, :], kc_ref[rows, :]], axis=0)
                v2 = jnp.concatenate([vp_ref[rows, :], vc_ref[rows, :]], axis=0)
                l2, y2, dy2 = l_ref[rows, :], y_ref[rows, :], dy_ref[rows, :]
                dqs, dks, dvs = [], [], []
                for pp in range(hps // PAIR):
                    ps = slice(pp * LANES, (pp + 1) * LANES)
                    qp, kp, vp = _bf(q2[:, ps]), _bf(k2[:, ps]), _bf(v2[:, ps])
                    dyp, yp = dy2[:, ps], y2[:, ps]
                    dq_h, dk_h, dv_h = [], [], []
                    for hh, own in enumerate(_pair_lanes()):
                        head = hp * hps + pp * PAIR + hh
                        s = _dot(qp, jnp.where(own, kp, 0), NT) * (HEAD_DIM ** -0.5) + b_ref[later, head]
                        p = jnp.exp(s - l2[:, pp * LANES + hh * HEAD_DIM:pp * LANES + hh * HEAD_DIM + 1])
                        dyh = jnp.where(own, dyp, 0.0)
                        delta = jnp.sum(dyh * yp, axis=-1, keepdims=True)
                        ds = p * (_dot(_bf(dyh), vp, NT) - delta)
                        db_ref[head] += ds
                        dsb = _bf(ds * (HEAD_DIM ** -0.5))
                        dq_h.append(_dot(dsb, kp, NN))
                        dk_h.append(_dot(dsb, qp, TN))
                        dv_h.append(_dot(_bf(p), _bf(dyp), TN))
                    first = _pair_lanes()[0]
                    dqs.append(jnp.where(first, dq_h[0], dq_h[1]))
                    dks.append(jnp.where(first, dk_h[0], dk_h[1]))
                    dvs.append(jnp.where(first, dv_h[0], dv_h[1]))
                dkb = dks[0] if len(dks) == 1 else jnp.concatenate(dks, axis=1)
                dvb = dvs[0] if len(dvs) == 1 else jnp.concatenate(dvs, axis=1)
                dq_ref[rows, :] = dqs[0] if len(dqs) == 1 else jnp.concatenate(dqs, axis=1)
                dk_ref[rows, :] = ck_ref[rows, :] + dkb[:BLK]
                dv_ref[rows, :] = cv_ref[rows, :] + dvb[:BLK]
                ck_ref[rows, :] = dkb[BLK:]
                cv_ref[rows, :] = dvb[BLK:]

            _for_residues(d, residue)

        @pl.when(n == NB)
        def _():
            dk_ref[...] = ck_ref[...]
            dv_ref[...] = cv_ref[...]

    def qn(n):
        return jnp.minimum(n, NB - 1)

    cur, prev = _attn_specs(d, g, qn)
    cw = hps * HEAD_DIM
    row = pl.BlockSpec((d * BLK, cw), lambda hp, n: (qn(n), hp))
    done = pl.BlockSpec((d * BLK, cw), lambda hp, n: (jnp.maximum(n - 1, 0), hp))
    (dq, dk, dv, db), carried = _call(
        body, plans, name=name, grid=(HEADS // hps, NB + 1),
        in_specs=[cur(0), prev(1), cur(1), prev(2), cur(2),
                  pl.BlockSpec((2, HEADS, BLK, 2 * BLK), lambda hp, n: (0, 0, 0, 0)), row, row, row],
        out_specs=[row, done, done, pl.BlockSpec((HEADS, BLK, 2 * BLK), lambda hp, n: (0, 0, 0))],
        out_shape=[_sds((S, GROUP_W), F32)] * 3 + [_sds((HEADS, BLK, 2 * BLK), F32)],
        scratch_shapes=[pltpu.VMEM((d * BLK, cw), F32)] * 2,
        args=(proj, proj, proj, proj, proj, bias, lse, y, dy))
    return ([dq, dk, dv], db) if plans is None else ([dq, dk, dv], db, carried)


BAND = BLK * 2 * BLK


def _bucket_onehot():
    buckets = jnp.stack([_t5_bucket(_band_rel() * d) for d in DILATIONS]).reshape(N_GROUPS, 1, BAND)
    return (buckets == jnp.arange(NUM_BUCKETS).reshape(1, NUM_BUCKETS, 1)).astype(F32)


def _relbias_fwd(rel_bias, name):
    table = rel_bias.reshape(NUM_BUCKETS, N_GROUPS, HEADS).transpose(1, 0, 2)

    def body(t_ref, oh_ref, valid_ref, o_ref):
        bias = lax.dot_general(t_ref[...], oh_ref[...], (TN, ((), ())), preferred_element_type=F32,
                               precision=lax.Precision.HIGHEST)
        for k in range(2):
            o_ref[k] = jnp.where(valid_ref[k] > 0.5, bias, NEG_INF)

    out = pl.pallas_call(
        body, name=name, grid=(N_GROUPS,),
        in_specs=[pl.BlockSpec((None, NUM_BUCKETS, HEADS), lambda g: (g, 0, 0)),
                  pl.BlockSpec((None, NUM_BUCKETS, BAND), lambda g: (g, 0, 0)),
                  pl.BlockSpec((2, 1, BAND), lambda g: (0, 0, 0))],
        out_specs=pl.BlockSpec((None, 2, HEADS, BAND), lambda g: (g, 0, 0, 0)),
        out_shape=_sds((N_GROUPS, 2, HEADS, BAND), F32), compiler_params=_cp(1))(table, _bucket_onehot(), _band_valid())
    return out.reshape(N_GROUPS, 2, HEADS, BLK, 2 * BLK)


def _relbias_bwd(dbs, name):
    band = BAND
    onehot = _bucket_onehot()
    dbf = jnp.stack([db.reshape(HEADS, band) for db in dbs])

    def body(oh_ref, db_ref, o_ref):
        o_ref[...] = lax.dot_general(oh_ref[...], db_ref[...], (NT, ((), ())), preferred_element_type=F32,
                                     precision=lax.Precision.HIGHEST)

    out = pl.pallas_call(
        body, name=name, grid=(N_GROUPS,),
        in_specs=[pl.BlockSpec((None, NUM_BUCKETS, band), lambda g: (g, 0, 0)),
                  pl.BlockSpec((None, HEADS, band), lambda g: (g, 0, 0))],
        out_specs=pl.BlockSpec((None, NUM_BUCKETS, HEADS), lambda g: (g, 0, 0)),
        out_shape=_sds((N_GROUPS, NUM_BUCKETS, HEADS), F32), compiler_params=_cp(1))(onehot, dbf)
    return out.transpose(1, 0, 2).reshape(NUM_BUCKETS, N_GROUPS * HEADS)


def _chunk_pos(shape):
    return lax.broadcasted_iota(jnp.int32, shape, 0) % HG_CHUNK


def _chunk_cumsum(v):
    pos = _chunk_pos(v.shape)
    s = 1
    while s < HG_CHUNK:
        v = v + jnp.where(pos >= s, pltpu.roll(v, s, 0), 0.0)
        s *= 2
    return v


def _chunk_rev_cumsum(v):
    pos = _chunk_pos(v.shape)
    n = v.shape[0]
    s = 1
    while s < HG_CHUNK:
        v = v + jnp.where(pos < HG_CHUNK - s, pltpu.roll(v, n - s, 0), 0.0)
        s *= 2
    return v


def _lower_bound(raw):
    a0, a1 = raw[0:1], raw[1:2]
    m = jnp.maximum(a0, a1)
    e0, e1 = jnp.exp(a0 - m), jnp.exp(a1 - m)
    return e0 / (e0 + e1)


def _hg_gates(qr, fr, lb):
    sf = _sigmoid(fr)
    f = lb + (1.0 - lb) * sf
    sq = _sigmoid(qr)
    return qr * sq, sq, f, sf


HG_COL0 = QKV_W // HG_W


def _hgrn_fwd(proj, lb_raw, nw, name, plans=None):
    S = proj.shape[0]
    ncs = HG_TILE // HG_CHUNK

    def body(q_ref, f_ref, i_ref, og_ref, lb_ref, nw_ref, y_ref, o_ref, st_ref, state):
        @pl.when(pl.program_id(0) == 0)
        def _():
            state[...] = jnp.zeros_like(state)

        lb = _lower_bound(lb_ref[...])
        q, _, f, _ = _hg_gates(q_ref[...], f_ref[...], lb)
        k = 1.0 - f
        G = _chunk_cumsum(jnp.log(f))
        row = lax.broadcasted_iota(jnp.int32, (HG_CHUNK, HG_CHUNK), 0)
        col = lax.broadcasted_iota(jnp.int32, (HG_CHUNK, HG_CHUNK), 1)
        heads = [slice(h * HG_DK, (h + 1) * HG_DK) for h in range(HG_HEADS)]
        sts = [state[h] for h in range(HG_HEADS)]
        for c in range(ncs):
            cs = slice(c * HG_CHUNK, (c + 1) * HG_CHUNK)
            for h, hs in enumerate(heads):
                Gc = G[cs, hs]
                gl = Gc[HG_CHUNK - 1:HG_CHUNK]
                qt = _bf(q[cs, hs] * jnp.exp(Gc))
                kt = _bf(k[cs, hs] * jnp.exp(-Gc))
                kd = _bf(k[cs, hs] * jnp.exp(gl - Gc))
                v = _bf(i_ref[cs, hs])
                A = jnp.where(row >= col, _dot(qt, kt, NT), 0.0)
                o_ref[cs, hs] = _dot(_bf(A), v, NN) + _dot(qt, _bf(sts[h]), NT)
                st_ref[c, h] = sts[h]
                sts[h] = sts[h] * jnp.exp(gl) + _dot(v, kd, TN)
        for h, hs in enumerate(heads):
            state[h] = sts[h]
            oh = o_ref[:, hs]
            og = og_ref[:, hs]
            y_ref[:, hs] = oh * _rinv(oh) * nw_ref[...] * (og * _sigmoid(og))

    def colspec(j):
        return pl.BlockSpec((HG_TILE, HG_W), lambda i: (i, HG_COL0 + j))

    res, carried = _call(
        body, plans, name=name, grid=(S // HG_TILE,),
        in_specs=[colspec(0), colspec(1), colspec(2), colspec(3),
                  pl.BlockSpec((2, HG_W), lambda i: (0, 0)), pl.BlockSpec((1, HG_DK), lambda i: (0, 0))],
        out_specs=[pl.BlockSpec((HG_TILE, HG_W), lambda i: (i, 0))] * 2
        + [pl.BlockSpec((ncs, HG_HEADS, HG_DK, HG_DK), lambda i: (i, 0, 0, 0))],
        out_shape=[_sds((S, HG_W), F32)] * 2 + [_sds((S // HG_CHUNK, HG_HEADS, HG_DK, HG_DK), F32)],
        scratch_shapes=[pltpu.VMEM((HG_HEADS, HG_DK, HG_DK), F32)],
        args=(proj, proj, proj, proj, lb_raw, nw))
    return res if plans is None else (res, carried)


def _hgrn_bwd(proj, lb_raw, nw, o, states, dy, d_attn, d_gates, name):
    S = proj.shape[0]
    ncs = HG_TILE // HG_CHUNK
    nt = S // HG_TILE
    n_a, n_g = len(d_attn), len(d_gates)
    own = [slice(QKV_W + j * HG_W, QKV_W + (j + 1) * HG_W) for j in range(4)]

    def body(q_ref, f_ref, i_ref, og_ref, lb_ref, nw_ref, o_ref, st_ref, dy_ref, *rest):
        attn_refs, gate_refs = rest[:n_a], rest[n_a:n_a + n_g]
        dp_ref, dlb_ref, dnw_ref, dstate, do_s, dG_s, dgl_s, dk_s, dlb_s = rest[n_a + n_g:]
        dq_ref, df_ref, di_ref, dog_ref = (dp_ref.at[:, cols] for cols in own)
        step = pl.program_id(0)
        for k, a_ref in enumerate(attn_refs):
            dp_ref[:, k * GROUP_W:(k + 1) * GROUP_W] = _bf(a_ref[...])
        for k, g_ref in enumerate(gate_refs):
            dp_ref[:, QKV_W + 4 * HG_W + k * D_MODEL:QKV_W + 4 * HG_W + (k + 1) * D_MODEL] = g_ref[...]

        @pl.when(step == 0)
        def _():
            dstate[...] = jnp.zeros_like(dstate)
            dlb_s[...] = jnp.zeros_like(dlb_s)
            dnw_ref[...] = jnp.zeros_like(dnw_ref)

        lb = _lower_bound(lb_ref[...])
        qr = q_ref[...]
        q, sq, f, sf = _hg_gates(qr, f_ref[...], lb)
        k = 1.0 - f
        G = _chunk_cumsum(jnp.log(f))
        nwv = nw_ref[...]
        row = lax.broadcasted_iota(jnp.int32, (HG_CHUNK, HG_CHUNK), 0)
        col = lax.broadcasted_iota(jnp.int32, (HG_CHUNK, HG_CHUNK), 1)
        for h in range(HG_HEADS):
            hs = slice(h * HG_DK, (h + 1) * HG_DK)
            oh = o_ref[:, hs]
            r = _rinv(oh)
            ohat = oh * r
            og = og_ref[:, hs]
            sg = _sigmoid(og)
            dyh = dy_ref[:, hs]
            don = dyh * (og * sg)
            dog_ref[:, hs] = _bf(dyh * (ohat * nwv) * (sg * (1.0 + og * (1.0 - sg))))
            dnw_ref[...] += jnp.sum(don * ohat, axis=0, keepdims=True)
            do_s[:, hs] = _norm_bwd(don, ohat, r, nwv)
        dsts = [dstate[h] for h in range(HG_HEADS)]
        for c in reversed(range(ncs)):
            cs = slice(c * HG_CHUNK, (c + 1) * HG_CHUNK)
            for h in range(HG_HEADS):
                hs = slice(h * HG_DK, (h + 1) * HG_DK)
                dst = dsts[h]
                Gc = G[cs, hs]
                gl = Gc[HG_CHUNK - 1:HG_CHUNK]
                eG, enG, edG, egl = jnp.exp(Gc), jnp.exp(-Gc), jnp.exp(gl - Gc), jnp.exp(gl)
                qt, kt, kd = q[cs, hs] * eG, k[cs, hs] * enG, k[cs, hs] * edG
                qtb, ktb, kdb = _bf(qt), _bf(kt), _bf(kd)
                v = _bf(i_ref[cs, hs])
                do = _bf(do_s[cs, hs])
                st = st_ref[c, h]
                dstb = _bf(dst)
                A = jnp.where(row >= col, _dot(qtb, ktb, NT), 0.0)
                dA = _bf(jnp.where(row >= col, _dot(do, v, NT), 0.0))
                di_ref[cs, hs] = _bf(_dot(_bf(A), do, TN) + _dot(kdb, dstb, NT))
                dqt = _dot(dA, ktb, NN) + _dot(do, _bf(st), NN)
                dkt = _dot(dA, qtb, TN)
                dkd = _dot(v, dstb, NN)
                dgl = egl * jnp.sum(st * dst, axis=0, keepdims=True) + jnp.sum(dkd * kd, axis=0, keepdims=True)
                dsts[h] = dst * egl + _dot(do, qtb, TN)
                dq_ref[cs, hs] = _bf(dqt * eG * (sq[cs, hs] * (1.0 + qr[cs, hs] * (1.0 - sq[cs, hs]))))
                dk_s[cs, hs] = dkt * enG + dkd * edG
                dG_s[cs, hs] = dqt * qt - dkt * kt - dkd * kd
                dgl_s[cs, hs] = jnp.broadcast_to(dgl, (HG_CHUNK, HG_DK))
        for h in range(HG_HEADS):
            dstate[h] = dsts[h]
        dg = _chunk_rev_cumsum(dG_s[...]) + dgl_s[...]
        dfv = dg / f - dk_s[...]
        df_ref[...] = _bf(dfv * (1.0 - lb) * sf * (1.0 - sf))
        dlb_s[...] += jnp.sum(dfv * (1.0 - sf), axis=0, keepdims=True)

        @pl.when(step == nt - 1)
        def _():
            t = dlb_s[...] * lb * (1.0 - lb)
            dlb_ref[...] = jnp.concatenate([t, -t], axis=0)

    def colspec(j):
        return pl.BlockSpec((HG_TILE, HG_W), lambda i: (nt - 1 - i, HG_COL0 + j))

    def rows(width):
        return pl.BlockSpec((HG_TILE, width), lambda i: (nt - 1 - i, 0))

    tile = rows(HG_W)
    return pl.pallas_call(
        body, name=name, grid=(nt,),
        in_specs=[colspec(0), colspec(1), colspec(2), colspec(3),
                  pl.BlockSpec((2, HG_W), lambda i: (0, 0)), pl.BlockSpec((1, HG_DK), lambda i: (0, 0)),
                  tile, pl.BlockSpec((ncs, HG_HEADS, HG_DK, HG_DK), lambda i: (nt - 1 - i, 0, 0, 0)), tile]
        + [rows(GROUP_W)] * n_a + [rows(D_MODEL)] * n_g,
        out_specs=[rows(IN_W), pl.BlockSpec((2, HG_W), lambda i: (0, 0)), pl.BlockSpec((1, HG_DK), lambda i: (0, 0))],
        out_shape=[_sds((S, IN_W), BF16), _sds((2, HG_W), F32), _sds((1, HG_DK), F32)],
        scratch_shapes=[pltpu.VMEM((HG_HEADS, HG_DK, HG_DK), F32)] + [pltpu.VMEM((HG_TILE, HG_W), F32)] * 4
        + [pltpu.VMEM((1, HG_W), F32)],
        compiler_params=_cp(1))(proj, proj, proj, proj, lb_raw, nw, o, states, dy, *d_attn, *d_gates)


GATE_COL0 = (QKV_W + 4 * HG_W) // GROUP_W
HALF_D = D_MODEL // 2


def _gate_tiles(proj):
    return [_tile(proj, HALF_D, functools.partial(lambda c, k: GATE_COL0 + k, k=k)) for k in range(4)]


def _gates(g_refs):
    s0 = _sigmoid(jnp.concatenate([g_refs[0][...], g_refs[1][...]], axis=1))
    s1 = _sigmoid(jnp.concatenate([g_refs[2][...], g_refs[3][...]], axis=1))
    return s0, s1


def _branch_fwd(os_, lses, yh, proj, w_a, w_h, name, plans=None):
    nb = w_a.shape[0]

    def body(o0, o1, o2, l0, l1, l2, yh_ref, g0a, g0b, g1a, g1b, wa_ref, wh_ref,
             y_ref, lse_ref, za_ref, zh_ref, m_ref):
        a, b, c = l0[...], l1[...], l2[...]
        m = jnp.maximum(jnp.maximum(a, b), c)
        ea, eb, ec = jnp.exp(a - m), jnp.exp(b - m), jnp.exp(c - m)
        den = ea + eb + ec
        y = (ea * o0[...] + eb * o1[...] + ec * o2[...]) / den
        y_ref[...] = y
        lse_ref[...] = m + jnp.log(den)
        yb, yhb = _bf(y), _bf(yh_ref[...])
        za = jnp.concatenate([_dot(yb, wa_ref[j], NN) for j in range(nb)], axis=1)
        zh = jnp.concatenate([_dot(yhb, wh_ref[j], NN) for j in range(nb)], axis=1)
        s0, s1 = _gates((g0a, g0b, g1a, g1b))
        za_ref[...] = za
        zh_ref[...] = zh
        m_ref[...] = _bf(s0 * za + s1 * zh)

    return _rows_call(name, body, yh.shape[0], 512, 1,
                      [*[_tile(t, GROUP_W) for t in (*os_, *lses)], _tile(yh, HG_W), *_gate_tiles(proj),
                       _full(w_a), _full(w_h)],
                      [_out_tile(GROUP_W, F32, GROUP_W)] * 2 + [_out_tile(D_MODEL, F32, D_MODEL)] * 2
                      + [_out_tile(D_MODEL, BF16, D_MODEL)], plans)


def _branch_bwd(dm, za, zh, proj, w_a, w_h, name, plans=None):
    nb, _, Nb = w_a.shape

    def body(dm_ref, za_ref, zh_ref, g0a, g0b, g1a, g1b, wa_ref, wh_ref,
             dza_ref, dzh_ref, dg0_ref, dg1_ref, dy_ref, dyh_ref):
        dmv = dm_ref[...]
        s0, s1 = _gates((g0a, g0b, g1a, g1b))
        dza, dzh = _bf(dmv * s0), _bf(dmv * s1)
        dza_ref[...] = dza
        dzh_ref[...] = dzh
        dg0_ref[...] = _bf(dmv * za_ref[...] * s0 * (1.0 - s0))
        dg1_ref[...] = _bf(dmv * zh_ref[...] * s1 * (1.0 - s1))
        dy_ref[...] = sum(_dot(dza[:, j * Nb:(j + 1) * Nb], wa_ref[j], NT) for j in range(nb))
        dyh_ref[...] = sum(_dot(dzh[:, j * Nb:(j + 1) * Nb], wh_ref[j], NT) for j in range(nb))

    return _rows_call(name, body, za.shape[0], 512, 1,
                      [_tile(dm, D_MODEL), _tile(za, D_MODEL), _tile(zh, D_MODEL), *_gate_tiles(proj),
                       _full(w_a), _full(w_h)],
                      [_out_tile(D_MODEL, BF16, D_MODEL)] * 4 + [_out_tile(GROUP_W, F32, GROUP_W),
                                                                 _out_tile(HG_W, F32, HG_W)], plans)


def _mix_out(merged, w_out, x, w_post, w_pre, name):
    def body(m_ref, wo_ref, x_ref, wp_ref, wf_ref, mo_ref, x1_ref, h2_ref):
        z = _dot(m_ref[...], wo_ref[...], NN)
        mo_ref[...] = z
        x1 = x_ref[...] + z * _rinv(z) * wp_ref[...]
        x1_ref[...] = x1
        h2_ref[...] = _bf(x1 * _rinv(x1) * wf_ref[...])

    return _rows_call(name, body, x.shape[0], 512, 1,
                      [_tile(merged, D_MODEL), _full(w_out), _tile(x, D_MODEL), _full(w_post), _full(w_pre)],
                      [_out_tile(D_MODEL, F32, D_MODEL), _out_tile(D_MODEL, F32, D_MODEL),
                       _out_tile(D_MODEL, BF16, D_MODEL)])


def _loss_head(a, w_down, x1, tgt, w, name):
    def body(a_ref, wd_ref, x1_ref, t_ref, w_ref, dx_ref, df_ref, dw_ref, loss_ref):
        z = _dot(a_ref[...], wd_ref[...], NN)
        r = _rinv(z)
        zhat = z * r
        wv = w_ref[...]
        e = x1_ref[...] + zhat * wv - t_ref[...]
        dx = e * (1.0 / D_MODEL)
        dx_ref[...] = dx
        df_ref[...] = _bf(_norm_bwd(dx, zhat, r, wv))
        _acc(dw_ref, jnp.sum(dx * zhat, axis=0, keepdims=True))
        part = 0.5 * jnp.sum(jnp.sum(e * e, axis=1, keepdims=True), axis=0, keepdims=True) * (1.0 / D_MODEL)
        _acc(loss_ref, jnp.broadcast_to(part, (1, LANES)))

    return _rows_call(name, body, x1.shape[0], 512, 1,
                      [_tile(a, D_FF), _full(w_down), _tile(x1, D_MODEL), _tile(tgt, D_MODEL), _full(w)],
                      [_out_tile(D_MODEL, F32, D_MODEL), _out_tile(D_MODEL, BF16, D_MODEL),
                       _out_acc(1, D_MODEL, D_MODEL), _out_acc(1, LANES, LANES)])


CONV_CB = D_FF // 2
CONV_TM = 512
HALO = 8
SQRT_HALF = 0.7071067811865476
INV_SQRT_2PI = 0.3989422804014327


CONV_RS = 32


def _lane_tiles():
    return [slice(k * LANES, (k + 1) * LANES) for k in range(CONV_CB // LANES)]


def _strip_start(i):
    return pl.multiple_of(i * CONV_RS, CONV_RS)


def _strip_taps(u_ref, halo_ref, r0, cs, first_strip, first_tile):
    if first_strip:
        before = jnp.where(first_tile, 0.0, halo_ref[:, cs])
        blk = jnp.concatenate([before, u_ref[0:CONV_RS, cs]], axis=0)
    else:
        blk = u_ref[pl.ds(pl.multiple_of(r0 - HALO, HALO), CONV_RS + HALO), cs]
    return pltpu.roll(blk, 2, 0)[HALO:], pltpu.roll(blk, 1, 0)[HALO:], blk[HALO:]


def _conv(taps, w_ref, b_ref, cs):
    return b_ref[:, cs] + w_ref[0:1, cs] * taps[0] + w_ref[1:2, cs] * taps[1] + w_ref[2:3, cs] * taps[2]


def _conv_specs(tm):
    nh = tm // HALO
    nc = D_FF // CONV_CB

    def tile(off):
        return pl.BlockSpec((tm, CONV_CB), lambda c, i: (i, off + c))

    def halo(off):
        return pl.BlockSpec((HALO, CONV_CB), lambda c, i: (jnp.maximum(i * nh - 1, 0), off + c))

    def small(rows, off):
        return pl.BlockSpec((rows, CONV_CB), lambda c, i: (0, off + c))

    return nc, tile, halo, small


def _conv_gelu_fwd(u, cw, cb, name, plans=None):
    S = u.shape[0]
    tm = CONV_TM
    nc, tile, halo, small = _conv_specs(tm)

    def body(ug, hg, uv, hv, wg, wv, bg, bv, a_ref):
        first_tile = pl.program_id(1) == 0

        def strip(r0, first_strip):
            for cs in _lane_tiles():
                cg = _conv(_strip_taps(ug, hg, r0, cs, first_strip, first_tile), wg, bg, cs)
                cv = _conv(_strip_taps(uv, hv, r0, cs, first_strip, first_tile), wv, bv, cs)
                a_ref[pl.ds(r0, CONV_RS), cs] = _bf(0.5 * cg * (1.0 + lax.erf(cg * SQRT_HALF)) * cv)

        strip(0, True)
        lax.fori_loop(1, tm // CONV_RS, lambda k, c: (strip(_strip_start(k), False), c)[1], 0)

    (a,), carried = _call(
        body, plans, name=name, grid=(nc, S // tm),
        in_specs=[tile(0), halo(0), tile(nc), halo(nc), small(3, 0), small(3, nc), small(1, 0), small(1, nc)],
        out_specs=[tile(0)], out_shape=[_sds((S, D_FF), BF16)], args=(u, u, u, u, cw, cw, cb, cb))
    return a if plans is None else (a, carried)


def _conv_gelu_bwd(u, dff, w_down, cw, cb, name, plans=None):
    S = u.shape[0]
    tm = CONV_TM
    nt = S // tm
    nc, tile, halo, small = _conv_specs(tm)

    def body(ug, hg, uv, hv, wg, wv, bg, bv, dff_ref, wd_ref, dcg_ref, dcv_ref, dwg_ref, dwv_ref, dbg_ref, dbv_ref,
             acc, da_ref):
        i = pl.program_id(1)
        first_tile = i == 0
        da_ref[...] = _dot(dff_ref[...], wd_ref[...], NT)

        @pl.when(first_tile)
        def _():
            acc[...] = jnp.zeros_like(acc)

        def strip(r0, first_strip):
            rows = pl.ds(r0, CONV_RS)
            for cs in _lane_tiles():
                tg = _strip_taps(ug, hg, r0, cs, first_strip, first_tile)
                tv = _strip_taps(uv, hv, r0, cs, first_strip, first_tile)
                cg = _conv(tg, wg, bg, cs)
                cv = _conv(tv, wv, bv, cs)
                phi = 0.5 * (1.0 + lax.erf(cg * SQRT_HALF))
                dav = da_ref[rows, cs]
                dcg = dav * cv * (phi + cg * jnp.exp(-0.5 * cg * cg) * INV_SQRT_2PI)
                dcv = dav * (cg * phi)
                dcg_ref[rows, cs] = dcg
                dcv_ref[rows, cs] = dcv
                for half, (dc, taps) in enumerate(((dcg, tg), (dcv, tv))):
                    for j in range(3):
                        acc[4 * half + j, :, cs] += dc * taps[j]
                    acc[4 * half + 3, :, cs] += dc

        strip(0, True)
        lax.fori_loop(1, tm // CONV_RS, lambda k, c: (strip(_strip_start(k), False), c)[1], 0)

        @pl.when(i == nt - 1)
        def _():
            for half, (dw_ref, db_ref) in enumerate(((dwg_ref, dbg_ref), (dwv_ref, dbv_ref))):
                for j in range(3):
                    dw_ref[j:j + 1, :] = jnp.sum(acc[4 * half + j], axis=0, keepdims=True)
                db_ref[...] = jnp.sum(acc[4 * half + 3], axis=0, keepdims=True)

    res, carried = _call(
        body, plans, name=name, grid=(nc, nt),
        in_specs=[tile(0), halo(0), tile(nc), halo(nc), small(3, 0), small(3, nc), small(1, 0), small(1, nc),
                  pl.BlockSpec((tm, D_MODEL), lambda c, i: (i, 0)), pl.BlockSpec((CONV_CB, D_MODEL), lambda c, i: (c, 0))],
        out_specs=[tile(0), tile(0), small(3, 0), small(3, 0), small(1, 0), small(1, 0)],
        out_shape=[_sds((S, D_FF), F32)] * 2 + [_sds((3, D_FF), F32)] * 2 + [_sds((1, D_FF), F32)] * 2,
        scratch_shapes=[pltpu.VMEM((8, CONV_RS, CONV_CB), F32), pltpu.VMEM((tm, CONV_CB), F32)],
        args=(u, u, u, u, cw, cw, cb, cb, dff, w_down))
    return res if plans is None else (res, carried)


def _conv_input_bwd(dcg, dcv, cw, name, plans=None):
    S = dcg.shape[0]
    tm = CONV_TM // 2
    nh = tm // HALO
    nt = S // tm
    n = CONV_RS + HALO
    tile = pl.BlockSpec((tm, D_FF), lambda i: (i, 0))
    nxt = pl.BlockSpec((HALO, D_FF), lambda i: (jnp.minimum((i + 1) * nh, S // HALO - 1), 0))

    def body(g_ref, ng_ref, v_ref, nv_ref, w_ref, du_ref):
        last_tile = pl.program_id(0) == nt - 1

        def strip(r0, last_strip):
            for half, (dc_ref, n_ref) in enumerate(((g_ref, ng_ref), (v_ref, nv_ref))):
                for k in range(D_FF // LANES):
                    cs = slice(k * LANES, (k + 1) * LANES)
                    ws = slice(half * D_FF + k * LANES, half * D_FF + (k + 1) * LANES)
                    if last_strip:
                        after = jnp.where(last_tile, 0.0, n_ref[:, cs])
                        blk = jnp.concatenate([dc_ref[tm - CONV_RS:tm, cs], after], axis=0)
                    else:
                        blk = dc_ref[pl.ds(r0, n), cs]
                    d1 = pltpu.roll(blk, n - 1, 0)[:CONV_RS]
                    d2 = pltpu.roll(blk, n - 2, 0)[:CONV_RS]
                    du_ref[pl.ds(r0, CONV_RS), ws] = _bf(w_ref[2:3, ws] * blk[:CONV_RS] + w_ref[1:2, ws] * d1
                                                         + w_ref[0:1, ws] * d2)

        lax.fori_loop(0, tm // CONV_RS - 1, lambda k, c: (strip(_strip_start(k), False), c)[1], 0)
        strip(tm - CONV_RS, True)

    (du,), carried = _call(
        body, plans, name=name, grid=(nt,),
        in_specs=[tile, nxt, tile, nxt, pl.BlockSpec((3, 2 * D_FF), lambda i: (0, 0))],
        out_specs=[pl.BlockSpec((tm, 2 * D_FF), lambda i: (i, 0))], out_shape=[_sds((S, 2 * D_FF), BF16)],
        args=(dcg, dcg, dcv, dcv, cw))
    return du if plans is None else (du, carried)


def _row_tile(n, cap):
    best = n
    for t in range(16, cap + 1, 16):
        if n % t == 0:
            best = t
    return best if best <= cap else n


def _rows_for_bytes(nbytes, cols):
    return max(16, nbytes // (4 * cols) // 16 * 16)


def _adamw_update(w_ref, g_ref, m_ref, v_ref, d_ref, nm_ref, nv_ref):
    gv = g_ref[...]
    nm = ADAM_B1 * m_ref[...] + (1.0 - ADAM_B1) * gv
    nv = ADAM_B2 * v_ref[...] + (1.0 - ADAM_B2) * (gv * gv)
    m_hat = nm / (1.0 - ADAM_B1 ** ADAM_STEP)
    v_hat = nv / (1.0 - ADAM_B2 ** ADAM_STEP)
    d_ref[...] = -ADAM_LR * (m_hat / (jnp.sqrt(v_hat) + ADAM_EPS) + ADAM_WD * w_ref[...])
    nm_ref[...] = nm
    nv_ref[...] = nv


def _adamw(w, g, m, v, name):
    R, C = w.shape
    tr = _row_tile(R, _rows_for_bytes(2 << 20, C))
    spec = pl.BlockSpec((tr, C), lambda i: (i, 0))
    body = functools.partial(_adamw_update)
    return pl.pallas_call(body, name=name, grid=(R // tr,), in_specs=[spec] * 4, out_specs=[spec] * 3,
                          out_shape=[_sds((R, C), F32)] * 3, compiler_params=_cp(1))(w, g, m, v)


def _adamw_small(ws, gs, ms, vs, name):
    n = len(ws)

    def body(*refs):
        ins, outs = refs[:4 * n], refs[4 * n:]
        for k in range(n):
            _adamw_update(ins[k], ins[n + k], ins[2 * n + k], ins[3 * n + k], outs[3 * k], outs[3 * k + 1], outs[3 * k + 2])

    vm = pl.BlockSpec(memory_space=pltpu.VMEM)
    outs = pl.pallas_call(body, name=name, in_specs=[vm] * (4 * n), out_specs=[vm] * (3 * n),
                          out_shape=[_sds(w.shape, F32) for w in ws for _ in range(3)])(*ws, *gs, *ms, *vs)
    return [tuple(outs[3 * k:3 * k + 3]) for k in range(n)]


def _pair_sum(gfull, rcv, c_idx, name):
    nb, R, C = gfull.shape
    half = R // 2
    tr = _row_tile(half, _rows_for_bytes(2 << 20, C))
    nt = half // tr

    def body(c_ref, g_ref, r_ref, o_ref):
        o_ref[...] = _bf(g_ref[...] + r_ref[...])

    return pl.pallas_call(
        body, name=name,
        grid_spec=pltpu.PrefetchScalarGridSpec(
            num_scalar_prefetch=1, grid=(nb, nt),
            in_specs=[pl.BlockSpec((None, tr, C), lambda j, i, c_ref: (j, c_ref[0] * nt + i, 0)),
                      pl.BlockSpec((None, tr, C), lambda j, i, c_ref: (j, i, 0))],
            out_specs=pl.BlockSpec((None, tr, C), lambda j, i, c_ref: (j, i, 0))),
        out_shape=_sds((nb, half, C), BF16), compiler_params=_cp(2))(c_idx, gfull, rcv)


def _chip_sum(arrived, own, place, name):
    nb, H, C = arrived.shape
    tr = _row_tile(H, _rows_for_bytes(2 << 20, C))
    nt = H // tr

    def body(pl_ref, *refs):
        o_ref = refs[nb + 1]
        me = pl_ref[0]
        acc = None
        for k in range(nb):
            term = jnp.where(me == k, refs[nb][...], refs[k][...]).astype(F32)
            acc = term if acc is None else acc + term
        o_ref[...] = acc

    def other(k):
        return pl.BlockSpec((None, tr, C), lambda i, p: (jnp.where(p[0] == k, (k + 1) % nb, k), i, 0))

    return pl.pallas_call(
        body, name=name,
        grid_spec=pltpu.PrefetchScalarGridSpec(
            num_scalar_prefetch=1, grid=(nt,),
            in_specs=[other(k) for k in range(nb)] + [pl.BlockSpec((None, tr, C), lambda i, p: (p[0], i, 0))],
            out_specs=pl.BlockSpec((tr, C), lambda i, p: (p[1] * nt + i, 0))),
        out_shape=_sds((2 * H, C), F32), compiler_params=_cp(1))(place, *([arrived] * nb), own)


def _cast_into_slot(shard, place, name):
    R, C = shard.shape
    tr = _row_tile(R, 256)

    def body(pl_ref, s_ref, o_ref):
        o_ref[...] = _bf(s_ref[...])

    return pl.pallas_call(
        body, name=name,
        grid_spec=pltpu.PrefetchScalarGridSpec(
            num_scalar_prefetch=1, grid=(R // tr,),
            in_specs=[pl.BlockSpec((tr, C), lambda i, p: (i, 0))],
            out_specs=pl.BlockSpec((None, tr, C), lambda i, p: (p[0], i, 0))),
        out_shape=_sds((N_CHIPS, R, C), BF16), compiler_params=_cp(1))(place, shard)


def _place():
    x, y, c = lax.axis_index("x"), lax.axis_index("y"), lax.axis_index("c")
    chips = [(1 - x, y), (x, 1 - y), (1 - x, 1 - y)]
    return x, y, c, chips


def _chip_id(px, py):
    return 2 * px + py


def _remote(src, dst, send_sems, recv_sems, k, to):
    return pltpu.make_async_remote_copy(src_ref=src, dst_ref=dst, send_sem=send_sems.at[k], recv_sem=recv_sems.at[k],
                                        device_id=to, device_id_type=MESH)


def _proj_gathered(x, w_norm, slot, place, name, tm=1024, plan=None):
    M, K = x.shape
    nb, _, Nb = slot.shape
    half = K // 2
    nt = M // tm
    cx, cy = place[0] // 2, place[0] % 2
    order = jnp.stack([place[0], _chip_id(1 - cx, cy), _chip_id(cx, 1 - cy), _chip_id(1 - cx, 1 - cy)]).astype(jnp.int32)

    p_in = [] if plan is None else plan.ins + plan.inouts
    p_out = [] if plan is None else [_sds(a.shape, a.dtype) for a in plan.inouts] + plan.outs
    n_pi, n_po = len(p_in), len(p_out)

    def body(order_ref, x_ref, wn_ref, slot_in, *refs):
        o_ref, slot_ref, h_out = refs[n_pi:n_pi + 3]
        s0 = n_pi + 3 + n_po
        w_buf, hs, ici_send, ici_recv, pass_send, pass_recv, load_sem = refs[s0:s0 + 7]

        def carried():
            if plan is None:
                return [], [], []
            ins = refs[:len(plan.ins)]
            outs = refs[n_pi + 3:n_pi + 3 + n_po]
            return plan.copies(ins, outs[:len(plan.inouts)], outs[len(plan.inouts):], *refs[s0 + 7:])

        b, i = pl.program_id(0), pl.program_id(1)
        x, y, c, chips = _place()
        me = _chip_id(x, y)
        sib = (x, y, 1 - c)
        mine, other = pl.ds(c * half, half), pl.ds((1 - c) * half, half)

        def sent(k):
            blk = slot_ref.at[me, mine]
            return _remote(blk, blk, ici_send, ici_recv, k, (*chips[k], c))

        def landed(k):
            blk = slot_ref.at[_chip_id(*chips[k]), mine]
            return _remote(blk, blk, ici_send, ici_recv, k, (*chips[k], c))

        def passed(k, rows):
            blk = slot_ref.at[_chip_id(*chips[k]), rows]
            return _remote(blk, blk, pass_send, pass_recv, k, sib)

        @pl.when((b == 0) & (i == 0))
        def _():
            for k in range(len(chips)):
                sent(k).start()
            sends, _, local = carried()
            for cp in (*sends, *local):
                cp.start()

        for k in range(len(chips)):
            @pl.when((b == k + 1) & (i == 0))
            def _(k=k):
                landed(k).wait_recv()
                passed(k, mine).start()
                passed(k, other).wait_recv()

        @pl.when(i == 0)
        def _():
            load = pltpu.make_async_copy(slot_ref.at[order_ref[b]], w_buf, load_sem.at[0])
            load.start()
            load.wait()

        rows = pl.ds(pl.multiple_of(i * tm, tm), tm)
        keep_h = pltpu.make_async_copy(hs, h_out, load_sem.at[1])

        @pl.when(b == 0)
        def _():
            xv = x_ref[...]
            hs[rows, :] = _bf(xv * _rinv(xv) * wn_ref[...])

        @pl.when((b == 1) & (i == 0))
        def _():
            keep_h.start()

        o_ref[...] = _dot(hs[rows, :], w_buf[...], NN)

        @pl.when((b == nb - 1) & (i == nt - 1))
        def _():
            for k in range(len(chips)):
                sent(k).wait_send()
                passed(k, mine).wait_send()
            sends, recvs, local = carried()
            for cp in recvs:
                cp.wait_recv()
            for cp in sends:
                cp.wait_send()
            for cp in local:
                cp.wait()
            keep_h.wait()

    n_peers = N_CHIPS - 1
    return pl.pallas_call(
        body, name=name,
        grid_spec=pltpu.PrefetchScalarGridSpec(
            num_scalar_prefetch=1, grid=(nb, nt),
            in_specs=[pl.BlockSpec((tm, K), lambda b, i, o: (i, 0)), pl.BlockSpec((1, K), lambda b, i, o: (0, 0)), ANY]
            + [ANY] * n_pi,
            out_specs=[pl.BlockSpec((tm, Nb), lambda b, i, o: (i, o[b])), ANY, ANY] + [ANY] * n_po,
            scratch_shapes=[pltpu.VMEM((K, Nb), BF16), pltpu.VMEM((M, K), BF16)]
            + [pltpu.SemaphoreType.DMA((n_peers,))] * 4 + [pltpu.SemaphoreType.DMA((2,))]
            + ([] if plan is None else [pltpu.SemaphoreType.DMA((plan.n_sems,))] * 3)),
        out_shape=[_sds((M, nb * Nb), F32), _sds(slot.shape, slot.dtype), _sds((M, K), BF16)] + p_out,
        input_output_aliases={3: 1, **({} if plan is None else
                                       {4 + len(plan.ins) + a: 3 + a for a in range(len(plan.inouts))})},
        compiler_params=_cp(2))(order, x, w_norm, slot, *p_in)


def _gather_ici_plan(slots, wholes, part=None):
    ns, nw = len(slots), len(wholes)

    def copies(ins, ios, outs, send_sems, recv_sems, local_sems):
        x, y, c, chips = _place()
        me = _chip_id(x, y)
        sends, recvs = [], []
        for a in range(ns + nw):
            dst = ios[a] if a < ns else outs[a - ns]
            R = dst.shape[1]
            r0, nr = (0, R // 2) if part is None else part
            rows = pl.ds(c * (R // 2) + r0, nr) if a < ns else pl.ds(0, R)
            src = dst.at[me, rows] if a < ns else ins[a - ns]
            for j, chip in enumerate(chips):
                sends.append(_remote(src, dst.at[me, rows], send_sems, recv_sems, 3 * a + j, (*chip, c)))
                landed = dst.at[_chip_id(*chip), rows]
                recvs.append(_remote(landed, landed, send_sems, recv_sems, 3 * a + j, (*chip, c)))
        local = [pltpu.make_async_copy(ins[b], outs[b].at[me], local_sems.at[b]) for b in range(nw)]
        return sends, recvs, local

    return _Plan(copies, 3 * (ns + nw), ins=wholes, inouts=slots,
                 outs=[_sds((N_CHIPS, *s.shape), s.dtype) for s in wholes])


def _gather_pass_plan(slots):
    def copies(ins, ios, outs, send_sems, recv_sems, local_sems):
        x, y, c, chips = _place()
        sib = (x, y, 1 - c)
        sends, recvs = [], []
        for a, buf in enumerate(ios):
            half = buf.shape[1] // 2
            for j, chip in enumerate(chips):
                mine = buf.at[_chip_id(*chip), pl.ds(c * half, half)]
                other = buf.at[_chip_id(*chip), pl.ds((1 - c) * half, half)]
                sends.append(_remote(mine, mine, send_sems, recv_sems, 3 * a + j, sib))
                recvs.append(_remote(other, other, send_sems, recv_sems, 3 * a + j, sib))
        return sends, recvs, []

    return _Plan(copies, 3 * len(slots), inouts=slots)


def _pair_plan(grads):
    def copies(ins, ios, outs, send_sems, recv_sems, local_sems):
        x, y, c, _ = _place()
        sib = (x, y, 1 - c)
        sends, recvs = [], []
        for a, g in enumerate(ins):
            half = g.shape[1] // 2
            sends.append(_remote(g.at[:, pl.ds((1 - c) * half, half), :], outs[a], send_sems, recv_sems, a, sib))
            recvs.append(_remote(outs[a], outs[a], send_sems, recv_sems, a, sib))
        return sends, recvs, []

    return _Plan(copies, len(grads), ins=grads,
                 outs=[_sds((g.shape[0], g.shape[1] // 2, g.shape[2]), g.dtype) for g in grads])


def _chip_plan(parts):
    def copies(ins, ios, outs, send_sems, recv_sems, local_sems):
        x, y, c, chips = _place()
        me = _chip_id(x, y)
        sends, recvs = [], []
        for a, part in enumerate(ins):
            for j, chip in enumerate(chips):
                sends.append(_remote(part.at[_chip_id(*chip)], outs[a].at[me], send_sems, recv_sems, 3 * a + j, (*chip, c)))
                landed = outs[a].at[_chip_id(*chip)]
                recvs.append(_remote(landed, landed, send_sems, recv_sems, 3 * a + j, (*chip, c)))
        return sends, recvs, []

    return _Plan(copies, 3 * len(parts), ins=parts, outs=[_sds(p.shape, p.dtype) for p in parts])


def _all_sum(pack, fulls, name):
    R, C = pack.shape
    n = len(fulls)

    def body(p_ref, *refs):
        o_ref, halves = refs[n], refs[n + 1:2 * n + 1]
        buf, send_sems, recv_sems, pair_send, pair_recv = refs[2 * n + 1:]
        x, y, c, _ = _place()
        sib = (x, y, 1 - c)
        pair = []
        for a, full in enumerate(halves):
            H = full.shape[0] // 2
            mine = full.at[pl.ds(c * H, H)]
            cp = _remote(mine, mine, pair_send, pair_recv, a, sib)
            cp.start()
            pair.append(cp)
        me = 4 * x + 2 * y + c
        buf[me] = p_ref[...]
        cps = []
        for k in range(1, N_DEV):
            to = (x ^ (k >> 2), y ^ ((k >> 1) & 1), c ^ (k & 1))
            cp = _remote(p_ref, buf.at[me], send_sems, recv_sems, k - 1, to)
            cp.start()
            cps.append(cp)
        for k in range(1, N_DEV):
            frm = (x ^ (k >> 2), y ^ ((k >> 1) & 1), c ^ (k & 1))
            slot = buf.at[4 * frm[0] + 2 * frm[1] + frm[2]]
            _remote(slot, slot, send_sems, recv_sems, k - 1, frm).wait_recv()
        acc = buf[0]
        for k in range(1, N_DEV):
            acc = acc + buf[k]
        o_ref[...] = acc
        for cp in cps:
            cp.wait_send()
        for a, (full, cp) in enumerate(zip(halves, pair)):
            H = full.shape[0] // 2
            other = full.at[pl.ds((1 - c) * H, H)]
            _remote(other, other, pair_send, pair_recv, a, sib).wait_recv()
            cp.wait_send()

    vm = pl.BlockSpec(memory_space=pltpu.VMEM)
    res = pl.pallas_call(
        body, name=name, in_specs=[vm] + [ANY] * n, out_specs=[vm] + [ANY] * n,
        out_shape=[_sds((R, C), F32)] + [_sds(f.shape, f.dtype) for f in fulls],
        input_output_aliases={1 + a: 1 + a for a in range(n)},
        scratch_shapes=[pltpu.VMEM((N_DEV, R, C), F32), pltpu.SemaphoreType.DMA((N_DEV - 1,)),
                        pltpu.SemaphoreType.DMA((N_DEV - 1,)), pltpu.SemaphoreType.DMA((n,)),
                        pltpu.SemaphoreType.DMA((n,))])(pack, *fulls)
    return res[0], list(res[1:])


def _local_step(xs, tgt, p, ex):
    proj, h1 = ex.project(xs, p["pre_mix_norm"])
    biases = _relbias_fwd(p["rel_bias"], "rel_bias_fwd")
    fw = []
    for g in range(N_GROUPS):
        res, got = _attn_fwd(proj, biases[g], g, f"attn_fwd{g}", plans=ex.carry(f"attn_fwd{g}"))
        ex.done(f"attn_fwd{g}", got)
        fw.append(res)
    (yh, o_h, states), got = _hgrn_fwd(proj, p["hgrn_lb_raw"], p["hgrn_norm"], "hgrn_fwd", plans=ex.carry("hgrn_fwd"))
    ex.done("hgrn_fwd", got)
    W_a, W_h, W_out = ex.weight("w_branch_attn"), ex.weight("w_branch_hgrn"), ex.weight("w_out")
    (y, lse, za, zh, merged), got = _branch_fwd([t[0] for t in fw], [t[1] for t in fw], yh, proj, W_a, W_h,
                                                "branch_fwd", plans=ex.carry("branch_fwd"))
    ex.done("branch_fwd", got)
    W_up, conv_w = ex.weight("w_up"), ex.weight("conv_w")
    mo, x1, h2 = _mix_out(merged, W_out, xs, p["post_mix_norm"], p["pre_ffn_norm"], "mix_out")
    u, got = _mm_nn_blk(h2, W_up, "ffn_up", tm=1024, plans=ex.carry("ffn_up"))
    ex.done("ffn_up", got)
    a, got = _conv_gelu_fwd(u, conv_w, p["conv_b"], "conv_gelu_fwd", plans=ex.carry("conv_gelu_fwd"))
    ex.done("conv_gelu_fwd", got)
    W_down = ex.weight("w_down")
    dx2, dff, g_post_ffn, loss = _loss_head(a, W_down, x1, tgt, p["post_ffn_norm"], "ffn_down_loss")

    ex.grad("w_down", _mm_tn(a, dff, "g_w_down").reshape(N_CHIPS, D_FF // N_CHIPS, D_MODEL))
    (dcg, dcv, gwg, gwv, gbg, gbv), got = _conv_gelu_bwd(u, dff, W_down, conv_w, p["conv_b"], "conv_gelu_bwd",
                                                          plans=ex.carry("conv_gelu_bwd"))
    ex.done("conv_gelu_bwd", got)
    g_conv_w = jnp.concatenate([gwg, gwv], axis=1)
    g_conv_b = jnp.concatenate([gbg, gbv], axis=1)
    du, got = _conv_input_bwd(dcg, dcv, conv_w, "conv_input_bwd", plans=ex.carry("conv_input_bwd"))
    ex.done("conv_input_bwd", got)
    dx1, g_pre_ffn = _mm_nt_prenorm_bwd(du, W_up, x1, p["pre_ffn_norm"], dx2, "d_ffn_in")
    ex.grad("w_up", _mm_tn_blk(h2, du, N_CHIPS, "g_w_up"))
    (dmo, dmerged, g_post_mix), got = _postnorm_bwd(dx1, mo, p["post_mix_norm"], W_out, "post_mix_norm_bwd",
                                                    plans=ex.carry("post_mix_norm_bwd"))
    ex.done("post_mix_norm_bwd", got)
    ex.grad("w_out", _mm_tn(merged, dmo, "g_w_out").reshape(N_CHIPS, D_MODEL // N_CHIPS, D_MODEL))
    (dza, dzh, dg0, dg1, dy, dyh), got = _branch_bwd(dmerged, za, zh, proj, W_a, W_h, "branch_bwd",
                                                     plans=ex.carry("branch_bwd"))
    ex.done("branch_bwd", got)
    ex.grad("w_branch_attn", _mm_tn_blk(y, dza, N_CHIPS, "g_w_branch_attn", together=True))
    ex.grad("w_branch_hgrn", _mm_tn_blk(yh, dzh, N_CHIPS, "g_w_branch_hgrn", together=True))
    dqkv, dbs = [], []
    for g in range(N_GROUPS):
        parts, db, got = _attn_bwd(proj, biases[g], lse, y, dy, g, f"attn_bwd{g}", plans=ex.carry(f"attn_bwd{g}"))
        ex.done(f"attn_bwd{g}", got)
        dqkv += parts
        dbs.append(db)
    g_rel_bias = _relbias_bwd(dbs, "rel_bias_bwd")
    dproj, g_lb_raw, g_hgrn_norm = _hgrn_bwd(proj, p["hgrn_lb_raw"], p["hgrn_norm"], o_h, states, dyh, dqkv,
                                             [dg0, dg1], "hgrn_bwd")
    for piece in W_IN_PIECES:
        g, got = _mm_tn_blk(h1, dproj, N_CHIPS, f"g_{piece}", x_cols=W_IN_ROWS[piece],
                            plans=ex.carry(f"g_{piece}"))
        ex.done(f"g_{piece}", got)
        ex.grad(piece, g)
    dh1, got = _mm_nt_blk(dproj, ex.weight("w_in"), "d_proj_in", plans=ex.carry("d_proj_in"))
    ex.done("d_proj_in", got)
    (grad_x, g_pre_mix), got = _prenorm_bwd(dh1, xs, p["pre_mix_norm"], dx1, "pre_mix_norm_bwd",
                                            plans=ex.carry("pre_mix_norm_bwd"))
    ex.done("pre_mix_norm_bwd", got)
    small = dict(pre_mix_norm=g_pre_mix, rel_bias=g_rel_bias, hgrn_lb_raw=g_lb_raw, hgrn_norm=g_hgrn_norm,
                 post_mix_norm=g_post_mix, pre_ffn_norm=g_pre_ffn, conv_w=g_conv_w, conv_b=g_conv_b,
                 post_ffn_norm=g_post_ffn)
    return loss, grad_x, small


SMALL = ("pre_mix_norm", "rel_bias", "hgrn_lb_raw", "hgrn_norm", "post_mix_norm", "pre_ffn_norm", "conv_w", "conv_b",
         "post_ffn_norm")
BIG = ("w_in", "w_up", "w_down", "w_out", "w_branch_attn", "w_branch_hgrn")
WEIGHTS = ("pre_mix_norm", "w_in", "rel_bias", "hgrn_lb_raw", "hgrn_norm", "w_branch_attn", "w_branch_hgrn", "w_out",
           "post_mix_norm", "pre_ffn_norm", "w_up", "conv_w", "conv_b", "w_down", "post_ffn_norm")
MIXER = ("w_out", "w_branch_attn", "w_branch_hgrn")

ICI_PARTS = {"gather_ici_1of3": (0, 176), "gather_ici_2of3": (176, 176), "gather_ici_3of3": (352, 160)}
SCHEDULE = {
    "proj_in": [("gather_ici_cw", MIXER)],
    "attn_fwd1": [("gather_ici_1of3", ("w_up",))],
    "attn_fwd2": [("gather_pass", MIXER), ("gather_ici_2of3", ("w_up",))],
    "hgrn_fwd": [("gather_ici_3of3", ("w_up",))],
    "branch_fwd": [("gather_pass", ("w_up",))],
    "ffn_up": [("gather_ici", ("w_down",))],
    "conv_gelu_fwd": [("gather_pass", ("w_down",))],
    "conv_gelu_bwd": [("pair", ("w_down",))],
    "conv_input_bwd": [("chip", ("w_down",))],
    "post_mix_norm_bwd": [("pair", ("w_up",))],
    "attn_bwd0": [("chip", ("w_up",)), ("pair", MIXER)],
    "attn_bwd1": [("chip", MIXER)],
    "g_w_in_b": [("pair", ("w_in_a",))],
    "d_proj_in": [("chip", ("w_in_a",)), ("pair", ("w_in_b",))],
    "pre_mix_norm_bwd": [("chip", ("w_in_b",))],
}
W_IN_ROWS = dict(w_in_a=(0, 768), w_in_b=(3, 256))
W_IN_PIECES = tuple(W_IN_ROWS)
REDUCED = W_IN_PIECES + BIG[1:]


class _Exchange:
    def __init__(self, place, slots, conv_w_shard):
        self.place, self.slots, self.conv_w_shard = place, dict(slots), conv_w_shard
        self.conv_w = None
        self.g, self.from_sibling, self.pair_sums, self.arrived = {}, {}, {}, {}
        self.pending = []

    def weight(self, name):
        if name == "conv_w":
            return self.conv_w
        w = self.slots[name]
        return w.reshape(-1, D_MODEL) if name in ("w_out", "w_down") else w

    def project(self, x, w_norm):
        (plan,) = self.carry("proj_in")
        proj, self.slots["w_in"], h, *got = _proj_gathered(x, w_norm, self.slots["w_in"], self.place, "proj_in",
                                                           plan=plan)
        self.done("proj_in", [got])
        return proj, h

    def grad(self, name, g):
        self.g[name] = g

    def carry(self, point):
        plans = []
        self.pending = SCHEDULE.get(point, [])
        for kind, names in self.pending:
            if kind in ("gather_ici", "gather_ici_cw") or kind in ICI_PARTS:
                wholes = [self.conv_w_shard] if kind == "gather_ici_cw" else []
                plans.append(_gather_ici_plan([self.slots[n] for n in names], wholes, ICI_PARTS.get(kind)))
            elif kind == "gather_pass":
                plans.append(_gather_pass_plan([self.slots[n] for n in names]))
            elif kind == "pair":
                plans.append(_pair_plan([self.g[n] for n in names]))
            else:
                for n in names:
                    self.pair_sums[n] = _pair_sum(self.g[n], self.from_sibling[n], self.place[1:2], f"pair_sum_{n}")
                plans.append(_chip_plan([self.pair_sums[n] for n in names]))
        return plans

    def done(self, point, carried):
        for (kind, names), got in zip(self.pending, carried):
            if kind in ("gather_ici", "gather_ici_cw", "gather_pass") or kind in ICI_PARTS:
                self.slots.update(zip(names, got))
                if kind == "gather_ici_cw":
                    self.conv_w = got[len(names)].transpose(1, 0, 2).reshape(3, 2 * D_FF)
            elif kind == "pair":
                self.from_sibling.update(zip(names, got))
            else:
                self.arrived.update(zip(names, got))

    def reduced_halves(self):
        return [_chip_sum(self.arrived[n], self.pair_sums[n], self.place, f"chip_sum_{n}") for n in REDUCED]


def kernel(x, pre_mix_norm, w_in, rel_bias, hgrn_lb_raw, hgrn_norm, w_branch_attn, w_branch_hgrn, w_out, post_mix_norm, pre_ffn_norm, w_up, conv_w, conv_b, w_down, post_ffn_norm, loss_target, m_pre_mix_norm, m_w_in, m_rel_bias, m_hgrn_lb_raw, m_hgrn_norm, m_w_branch_attn, m_w_branch_hgrn, m_w_out, m_post_mix_norm, m_pre_ffn_norm, m_w_up, m_conv_w, m_conv_b, m_w_down, m_post_ffn_norm, v_pre_mix_norm, v_w_in, v_rel_bias, v_hgrn_lb_raw, v_hgrn_norm, v_w_branch_attn, v_w_branch_hgrn, v_w_out, v_post_mix_norm, v_pre_ffn_norm, v_w_up, v_conv_w, v_conv_b, v_w_down, v_post_ffn_norm):
    w = dict(pre_mix_norm=pre_mix_norm, w_in=w_in, rel_bias=rel_bias, hgrn_lb_raw=hgrn_lb_raw, hgrn_norm=hgrn_norm,
             w_branch_attn=w_branch_attn, w_branch_hgrn=w_branch_hgrn, w_out=w_out, post_mix_norm=post_mix_norm,
             pre_ffn_norm=pre_ffn_norm, w_up=w_up, conv_w=conv_w, conv_b=conv_b, w_down=w_down,
             post_ffn_norm=post_ffn_norm)
    m = dict(pre_mix_norm=m_pre_mix_norm, w_in=m_w_in, rel_bias=m_rel_bias, hgrn_lb_raw=m_hgrn_lb_raw,
             hgrn_norm=m_hgrn_norm, w_branch_attn=m_w_branch_attn, w_branch_hgrn=m_w_branch_hgrn, w_out=m_w_out,
             post_mix_norm=m_post_mix_norm, pre_ffn_norm=m_pre_ffn_norm, w_up=m_w_up, conv_w=m_conv_w,
             conv_b=m_conv_b, w_down=m_w_down, post_ffn_norm=m_post_ffn_norm)
    v = dict(pre_mix_norm=v_pre_mix_norm, w_in=v_w_in, rel_bias=v_rel_bias, hgrn_lb_raw=v_hgrn_lb_raw,
             hgrn_norm=v_hgrn_norm, w_branch_attn=v_w_branch_attn, w_branch_hgrn=v_w_branch_hgrn, w_out=v_w_out,
             post_mix_norm=v_post_mix_norm, pre_ffn_norm=v_pre_ffn_norm, w_up=v_w_up, conv_w=v_conv_w,
             conv_b=v_conv_b, w_down=v_w_down, post_ffn_norm=v_post_ffn_norm)
    shard2d = {n: (w[n][0] if w[n].ndim == 3 else w[n]) for n in WEIGHTS}
    chip = 2 * lax.axis_index("x") + lax.axis_index("y")
    core = lax.axis_index("c")

    place = jnp.stack([chip, core]).astype(jnp.int32)
    slots = {n: _cast_into_slot(shard2d[n], place, f"cast_{n}") for n in BIG}
    ex = _Exchange(place, slots, shard2d["conv_w"])
    loss, grad_x, small = _local_step(x[0], loss_target[0], {n: w[n] for n in SMALL if n != "conv_w"}, ex)

    flat = [small[n].reshape(-1) for n in SMALL] + [loss.reshape(-1)]
    sizes = [t.shape[0] for t in flat]
    summed, wholes = _all_sum(jnp.concatenate(flat).reshape(-1, LANES), ex.reduced_halves(), "sum_small")
    summed = summed.reshape(-1)
    offs = [sum(sizes[:i]) for i in range(len(sizes))]
    grads = {}
    for n, o, sz in zip(SMALL, offs, sizes):
        grads[n] = summed[o:o + sz].reshape(small[n].shape)
    loss_total = summed[offs[-1]]
    cw = 2 * D_FF // N_CHIPS
    grads["conv_w"] = lax.dynamic_slice(grads["conv_w"], (0, chip * cw), (3, cw))

    big = dict(zip(REDUCED, wholes))
    big["w_in"] = jnp.concatenate([big.pop(n) for n in W_IN_PIECES], axis=0)
    grads.update(big)

    m2d = {n: m[n].reshape(shard2d[n].shape) for n in WEIGHTS}
    v2d = {n: v[n].reshape(shard2d[n].shape) for n in WEIGHTS}
    updated = dict(zip(SMALL, _adamw_small([shard2d[n] for n in SMALL], [grads[n] for n in SMALL],
                                           [m2d[n] for n in SMALL], [v2d[n] for n in SMALL], "adamw_small")))
    for n in BIG:
        updated[n] = _adamw(shard2d[n], grads[n], m2d[n], v2d[n], f"adamw_{n}")
    out_g, out_d, out_m, out_v = [], [], [], []
    for n in WEIGHTS:
        d2, m2, v2 = updated[n]
        shape = w[n].shape
        out_g.append(grads[n].reshape(shape))
        out_d.append(d2.reshape(shape))
        out_m.append(m2.reshape(shape))
        out_v.append(v2.reshape(shape))
    return (loss_total, grad_x[None], *out_g, *out_d, *out_m, *out_v)
```

```python
import functools
import math

import jax
import jax.numpy as jnp
from jax import lax
from jax.experimental import pallas as pl
from jax.experimental.pallas import tpu as pltpu

F32 = jnp.float32
BF16 = jnp.bfloat16
MESH = pl.DeviceIdType.MESH

D_MODEL = 1024
N_GROUPS = 3
DILATIONS = (1, 4, 16)
HEADS = 8
HEAD_DIM = 64
GROUP_W = HEADS * HEAD_DIM
QKV_W = N_GROUPS * 3 * GROUP_W
BLK = 128
NEG_INF = -1e30
NUM_BUCKETS = 32
MAX_EXACT = 16
MAX_DISTANCE = 2048
HG_HEADS = 4
HG_DK = 128
HG_W = HG_HEADS * HG_DK
HG_CHUNK = 32
HG_TILE = 256
IN_W = QKV_W + 4 * HG_W + 2 * D_MODEL
D_FF = 2816
EPS = 1e-6
N_CHIPS = 4
N_DEV = 8
LANES = 128

ADAM_LR, ADAM_B1, ADAM_B2, ADAM_EPS, ADAM_WD, ADAM_STEP = 0.001, 0.9, 0.999, 1e-08, 0.01, 10

VMEM_LIMIT = 56 * 1024 * 1024


def _cp(n_axes):
    return pltpu.CompilerParams(dimension_semantics=("arbitrary",) * n_axes, vmem_limit_bytes=VMEM_LIMIT)


def _sds(shape, dtype):
    return jax.ShapeDtypeStruct(tuple(shape), dtype)


def _sigmoid(v):
    return 1.0 / (1.0 + jnp.exp(-v))


def _bf(v):
    return v.astype(BF16)


def _dot(a, b, dims):
    return lax.dot_general(a, b, (dims, ((), ())), preferred_element_type=F32)


NN = ((1,), (0,))
NT = ((1,), (1,))
TN = ((0,), (0,))

ANY = pl.BlockSpec(memory_space=pl.ANY)


class _Plan:
    def __init__(self, copies, n_sems, ins=(), inouts=(), outs=()):
        self.copies, self.n_sems = copies, n_sems
        self.ins, self.inouts, self.outs = list(ins), list(inouts), list(outs)


def _call(body, plans=None, *, name, grid, in_specs, out_specs, out_shape, args, scratch_shapes=()):
    plans = list(plans or ())
    in_specs, out_specs, out_shape = list(in_specs), list(out_specs), list(out_shape)
    scratch_shapes = list(scratch_shapes)
    n_in, n_out, n_scr = len(in_specs), len(out_specs), len(scratch_shapes)
    x_in, x_out, aliases, spans = [], [], {}, []
    for p in plans:
        i0, o0 = len(x_in), len(x_out)
        x_in += p.ins
        for a in p.inouts:
            aliases[n_in + len(x_in)] = n_out + len(x_out)
            x_in.append(a)
            x_out.append(_sds(a.shape, a.dtype))
        x_out += p.outs
        spans.append((i0, len(p.ins), o0, len(p.inouts), len(p.outs)))
    sems = [pltpu.SemaphoreType.DMA((p.n_sems,)) for p in plans for _ in range(3)]

    def wrapped(*refs):
        xi = refs[n_in:n_in + len(x_in)]
        base = n_in + len(x_in)
        xo = refs[base + n_out:base + n_out + len(x_out)]
        sbase = base + n_out + len(x_out)
        xs = refs[sbase + n_scr:]
        ids = [pl.program_id(k) for k in range(len(grid))]
        first = functools.reduce(jnp.logical_and, [i == 0 for i in ids])
        last = functools.reduce(jnp.logical_and, [i == g - 1 for i, g in zip(ids, grid)])

        def descriptors(k):
            i0, ni, o0, nio, no = spans[k]
            return plans[k].copies(xi[i0:i0 + ni], xo[o0:o0 + nio], xo[o0 + nio:o0 + nio + no], *xs[3 * k:3 * k + 3])

        @pl.when(first)
        def _():
            for k in range(len(plans)):
                sends, _, local = descriptors(k)
                for cp in (*sends, *local):
                    cp.start()

        body(*refs[:n_in], *refs[base:base + n_out], *refs[sbase:sbase + n_scr])

        @pl.when(last)
        def _():
            for k in range(len(plans)):
                sends, recvs, local = descriptors(k)
                for cp in recvs:
                    cp.wait_recv()
                for cp in sends:
                    cp.wait_send()
                for cp in local:
                    cp.wait()

    res = pl.pallas_call(
        wrapped if plans else body, name=name, grid=grid, in_specs=in_specs + [ANY] * len(x_in),
        out_specs=out_specs + [ANY] * len(x_out), out_shape=out_shape + x_out, input_output_aliases=aliases,
        scratch_shapes=scratch_shapes + sems, compiler_params=_cp(len(grid)))(*args, *x_in)
    res = list(res)
    carried = [res[n_out + o0:n_out + o0 + nio + no] for (_, _, o0, nio, no) in spans]
    return res[:n_out], carried


def _mm_nn_blk(a, wg, name, tm=512, plans=None):
    M, K = a.shape
    nb, _, Nb = wg.shape

    def body(a_ref, w_ref, o_ref):
        o_ref[...] = _dot(_bf(a_ref[...]), w_ref[...], NN)

    (out,), carried = _call(
        body, plans, name=name, grid=(nb, M // tm),
        in_specs=[pl.BlockSpec((tm, K), lambda j, i: (i, 0)), pl.BlockSpec((None, K, Nb), lambda j, i: (j, 0, 0))],
        out_specs=[pl.BlockSpec((tm, Nb), lambda j, i: (i, j))],
        out_shape=[_sds((M, nb * Nb), F32)], args=(a, wg))
    return out if plans is None else (out, carried)


def _mm_nt_blk(dy, wg, name, tm=1024, plans=None):
    M = dy.shape[0]
    nb, K, Nb = wg.shape

    def body(dy_ref, w_ref, o_ref):
        j = pl.program_id(1)
        r = _dot(_bf(dy_ref[...]), w_ref[...], NT)

        @pl.when(j == 0)
        def _():
            o_ref[...] = r

        @pl.when(j > 0)
        def _():
            o_ref[...] += r

    (out,), carried = _call(
        body, plans, name=name, grid=(M // tm, nb),
        in_specs=[pl.BlockSpec((tm, Nb), lambda i, j: (i, j)), pl.BlockSpec((None, K, Nb), lambda i, j: (j, 0, 0))],
        out_specs=[pl.BlockSpec((tm, K), lambda i, j: (i, 0))],
        out_shape=[_sds((M, K), F32)], args=(dy, wg))
    return out if plans is None else (out, carried)


def _mm_nt_prenorm_bwd(dy, wg, xin, w, dres, name, tm=1024):
    M = dy.shape[0]
    nb, K, Nb = wg.shape

    def body(dy_ref, w_ref, x_ref, wn_ref, dres_ref, dx_ref, dw_ref, acc):
        i, j = pl.program_id(0), pl.program_id(1)
        r = _dot(_bf(dy_ref[...]), w_ref[...], NT)

        @pl.when(j == 0)
        def _():
            acc[...] = r

        @pl.when(j > 0)
        def _():
            acc[...] += r

        @pl.when(j == nb - 1)
        def _():
            xv = x_ref[...]
            rinv = _rinv(xv)
            xhat = xv * rinv
            dh = acc[...]
            dx_ref[...] = dres_ref[...] + _norm_bwd(dh, xhat, rinv, wn_ref[...])
            part = jnp.sum(dh * xhat, axis=0, keepdims=True)

            @pl.when(i == 0)
            def _():
                dw_ref[...] = part

            @pl.when(i > 0)
            def _():
                dw_ref[...] += part

    row = pl.BlockSpec((tm, K), lambda i, j: (i, 0))
    vec = pl.BlockSpec((1, K), lambda i, j: (0, 0))
    return pl.pallas_call(
        body, name=name, grid=(M // tm, nb),
        in_specs=[pl.BlockSpec((tm, Nb), lambda i, j: (i, j)), pl.BlockSpec((None, K, Nb), lambda i, j: (j, 0, 0)),
                  row, vec, row],
        out_specs=[row, vec], out_shape=[_sds((M, K), F32), _sds((1, K), F32)],
        scratch_shapes=[pltpu.VMEM((tm, K), F32)], compiler_params=_cp(2))(dy, wg, xin, w, dres)


def _mm_tn_blk(x, dy, nb, name, tk=2048, x_cols=None, plans=None, together=False):
    T, Mx = x.shape
    xk, Mx = (0, Mx) if x_cols is None else x_cols
    Nb = dy.shape[1] // nb
    nj = nb if together else 1

    def body(x_ref, dy_ref, o_ref):
        t = pl.program_id(1)
        r = _dot(_bf(x_ref[...]), _bf(dy_ref[...]), TN)
        for j in range(nj):
            rj = r[:, j * Nb:(j + 1) * Nb]

            @pl.when(t == 0)
            def _():
                o_ref[j] = rj

            @pl.when(t > 0)
            def _():
                o_ref[j] += rj

    (out,), carried = _call(
        body, plans, name=name, grid=(nb // nj, T // tk),
        in_specs=[pl.BlockSpec((tk, Mx), lambda j, t: (t, xk)), pl.BlockSpec((tk, nj * Nb), lambda j, t: (t, j))],
        out_specs=[pl.BlockSpec((nj, Mx, Nb), lambda j, t: (j, 0, 0))],
        out_shape=[_sds((nb, Mx, Nb), F32)], args=(x, dy))
    return out if plans is None else (out, carried)


def _mm_tn(x, dy, name, tk=1024):
    T, Mx = x.shape
    N = dy.shape[1]

    def body(x_ref, dy_ref, o_ref):
        t = pl.program_id(0)
        r = _dot(_bf(x_ref[...]), _bf(dy_ref[...]), TN)

        @pl.when(t == 0)
        def _():
            o_ref[...] = r

        @pl.when(t > 0)
        def _():
            o_ref[...] += r

    return pl.pallas_call(
        body, name=name, grid=(T // tk,),
        in_specs=[pl.BlockSpec((tk, Mx), lambda t: (t, 0)), pl.BlockSpec((tk, N), lambda t: (t, 0))],
        out_specs=pl.BlockSpec((Mx, N), lambda t: (0, 0)),
        out_shape=_sds((Mx, N), F32), compiler_params=_cp(1))(x, dy)


def _tile(arr, bw, col=lambda c: 0):
    return ("tile", arr, bw, col)


def _full(arr):
    return ("full", arr)


def _out_tile(width, dtype, bw, col=lambda c: 0):
    return ("tile", width, dtype, bw, col)


def _out_acc(rows, width, bw, col=lambda c: 0):
    return ("acc", rows, width, bw, col)


def _rows_call(name, body, n_rows, tm, ncol, ins, outs, plans=None):
    in_specs, args = [], []
    for e in ins:
        if e[0] == "tile":
            _, arr, bw, col = e
            in_specs.append(pl.BlockSpec((tm, bw), functools.partial(lambda c, i, col: (i, col(c)), col=col)))
        else:
            arr = e[1]
            in_specs.append(pl.BlockSpec(arr.shape, functools.partial(lambda c, i, nd: (0,) * nd, nd=arr.ndim)))
        args.append(arr)
    out_specs, out_shape = [], []
    for e in outs:
        if e[0] == "tile":
            _, width, dtype, bw, col = e
            out_specs.append(pl.BlockSpec((tm, bw), functools.partial(lambda c, i, col: (i, col(c)), col=col)))
            out_shape.append(_sds((n_rows, width), dtype))
        else:
            _, rows, width, bw, col = e
            out_specs.append(pl.BlockSpec((rows, bw), functools.partial(lambda c, i, col: (0, col(c)), col=col)))
            out_shape.append(_sds((rows, width), F32))
    out, carried = _call(body, plans, name=name, grid=(ncol, n_rows // tm), in_specs=in_specs, out_specs=out_specs,
                         out_shape=out_shape, args=args)
    return out if plans is None else (out, carried)


def _acc(ref, val):
    i = pl.program_id(1)

    @pl.when(i == 0)
    def _():
        ref[...] = val

    @pl.when(i > 0)
    def _():
        ref[...] += val


def _rinv(z):
    return lax.rsqrt(jnp.mean(z * z, axis=-1, keepdims=True) + EPS)


def _norm_bwd(dy, zhat, r, w):
    dyw = dy * w
    return r * (dyw - zhat * jnp.mean(dyw * zhat, axis=-1, keepdims=True))


def _prenorm_bwd(dh, xin, w, dres, name, plans=None):
    def body(dh_ref, x_ref, w_ref, dres_ref, dx_ref, dw_ref):
        xv = x_ref[...]
        r = _rinv(xv)
        xhat = xv * r
        dhv = dh_ref[...]
        dx_ref[...] = dres_ref[...] + _norm_bwd(dhv, xhat, r, w_ref[...])
        _acc(dw_ref, jnp.sum(dhv * xhat, axis=0, keepdims=True))

    return _rows_call(name, body, xin.shape[0], 512, 1,
                      [_tile(dh, D_MODEL), _tile(xin, D_MODEL), _full(w), _tile(dres, D_MODEL)],
                      [_out_tile(D_MODEL, F32, D_MODEL), _out_acc(1, D_MODEL, D_MODEL)], plans)


def _postnorm_bwd(dout, z, w, w_mat, name, plans=None):
    def body(do_ref, z_ref, w_ref, wm_ref, dz_ref, dm_ref, dw_ref):
        zv = z_ref[...]
        r = _rinv(zv)
        zhat = zv * r
        dov = do_ref[...]
        dz = _bf(_norm_bwd(dov, zhat, r, w_ref[...]))
        dz_ref[...] = dz
        dm_ref[...] = _dot(dz, wm_ref[...], NT)
        _acc(dw_ref, jnp.sum(dov * zhat, axis=0, keepdims=True))

    return _rows_call(name, body, z.shape[0], 512, 1,
                      [_tile(dout, D_MODEL), _tile(z, D_MODEL), _full(w), _full(w_mat)],
                      [_out_tile(D_MODEL, BF16, D_MODEL), _out_tile(D_MODEL, F32, D_MODEL),
                       _out_acc(1, D_MODEL, D_MODEL)], plans)


def _t5_bucket(dist):
    n = jnp.maximum(dist, 0)
    nf = jnp.maximum(n, 1).astype(F32)
    large = MAX_EXACT + (jnp.log(nf / MAX_EXACT) / math.log(MAX_DISTANCE / MAX_EXACT)
                         * (NUM_BUCKETS - MAX_EXACT)).astype(jnp.int32)
    large = jnp.minimum(large, NUM_BUCKETS - 1)
    return jnp.where(n < MAX_EXACT, n, large)


def _band_rel():
    return jnp.arange(BLK)[:, None] + BLK - jnp.arange(2 * BLK)[None, :]


def _band_valid():
    rel = _band_rel()
    window = (rel >= 0) & (rel <= BLK)
    first = window & (jnp.arange(2 * BLK)[None, :] >= BLK)
    return jnp.stack([first, window]).astype(F32).reshape(2, 1, BAND)


RES_UNROLL = 8
PAIR = LANES // HEAD_DIM


def _pair_lanes():
    first = lax.broadcasted_iota(jnp.int32, (1, LANES), 1) < HEAD_DIM
    return first, jnp.logical_not(first)


def _heads_per_step(d):
    return HEADS if d == 1 else LANES // HEAD_DIM


def _sub_rows(r, d):
    return pl.ds(r, BLK, stride=d) if d > 1 else pl.ds(0, BLK)


def _for_residues(d, fn):
    if d <= RES_UNROLL:
        for r in range(d):
            fn(r)
    else:
        def group(i, carry):
            for k in range(RES_UNROLL):
                fn(i * RES_UNROLL + k)
            return carry

        lax.fori_loop(0, d // RES_UNROLL, group, 0)


def _attn_specs(d, g, qblock):
    cw = _heads_per_step(d) * HEAD_DIM

    def col(part, hp):
        return (g * 3 + part) * (GROUP_W // cw) + hp

    def cur(part):
        return pl.BlockSpec((d * BLK, cw), lambda hp, n: (qblock(n), col(part, hp)))

    def prev(part):
        return pl.BlockSpec((d * BLK, cw), lambda hp, n: (jnp.maximum(qblock(n) - 1, 0), col(part, hp)))

    return cur, prev


def _attn_fwd(proj, bias, g, name, plans=None):
    S = proj.shape[0]
    d = DILATIONS[g]
    NB = S // (d * BLK)
    hps = _heads_per_step(d)

    def body(q_ref, kp_ref, kc_ref, vp_ref, vc_ref, b_ref, o_ref, lse_ref):
        hp = pl.program_id(0)
        later = jnp.minimum(pl.program_id(1), 1)

        def residue(r):
            rows = _sub_rows(r, d)
            q2 = q_ref[rows, :]
            k2 = jnp.concatenate([kp_ref[rows, :], kc_ref[rows, :]], axis=0)
            v2 = jnp.concatenate([vp_ref[rows, :], vc_ref[rows, :]], axis=0)
            outs, lses = [], []
            for pp in range(hps // PAIR):
                ps = slice(pp * LANES, (pp + 1) * LANES)
                qp, kp, vp = _bf(q2[:, ps]), _bf(k2[:, ps]), _bf(v2[:, ps])
                o_h, lse_h = [], []
                for hh, own in enumerate(_pair_lanes()):
                    s = _dot(qp, jnp.where(own, kp, 0), NT) * (HEAD_DIM ** -0.5) + b_ref[later, hp * hps + pp * PAIR + hh]
                    m = jnp.max(s, axis=-1, keepdims=True)
                    p = jnp.exp(s - m)
                    l = jnp.sum(p, axis=-1, keepdims=True)
                    o_h.append(_dot(_bf(p), vp, NN) / l)
                    lse_h.append(m + jnp.log(l))
                first = _pair_lanes()[0]
                outs.append(jnp.where(first, o_h[0], o_h[1]))
                lses.append(jnp.where(first, lse_h[0], lse_h[1]))
            o_ref[rows, :] = outs[0] if len(outs) == 1 else jnp.concatenate(outs, axis=1)
            lse_ref[rows, :] = lses[0] if len(lses) == 1 else jnp.concatenate(lses, axis=1)

        _for_residues(d, residue)

    cur, prev = _attn_specs(d, g, lambda n: n)
    out = pl.BlockSpec((d * BLK, hps * HEAD_DIM), lambda hp, n: (n, hp))
    res, carried = _call(
        body, plans, name=name, grid=(HEADS // hps, NB),
        in_specs=[cur(0), prev(1), cur(1), prev(2), cur(2),
                  pl.BlockSpec((2, HEADS, BLK, 2 * BLK), lambda hp, n: (0, 0, 0, 0))],
        out_specs=[out, out], out_shape=[_sds((S, GROUP_W), F32)] * 2,
        args=(proj, proj, proj, proj, proj, bias))
    return res if plans is None else (res, carried)


def _attn_bwd(proj, bias, lse, y, dy, g, name, plans=None):
    S = proj.shape[0]
    d = DILATIONS[g]
    NB = S // (d * BLK)
    hps = _heads_per_step(d)

    def body(q_ref, kp_ref, kc_ref, vp_ref, vc_ref, b_ref, l_ref, y_ref, dy_ref,
             dq_ref, dk_ref, dv_ref, db_ref, ck_ref, cv_ref):
        hp, n = pl.program_id(0), pl.program_id(1)

        @pl.when((hp == 0) & (n == 0))
        def _():
            db_ref[...] = jnp.zeros_like(db_ref)

        @pl.when(n == 0)
        def _():
            ck_ref[...] = jnp.zeros_like(ck_ref)
            cv_ref[...] = jnp.zeros_like(cv_ref)

        @pl.when(n < NB)
        def _():
            later = jnp.minimum(n, 1)

            def residue(r):
                rows = _sub_rows(r, d)
                q2 = q_ref[rows, :]
                k2 = jnp.concatenate([kp_ref[rows, :], kc_ref[rows, :]], axis=0)
                v2 = jnp.concatenate([vp_ref[rows, :], vc_ref[rows, :]], axis=0)
                l2, y2, dy2 = l_ref[rows, :], y_ref[rows, :], dy_ref[rows, :]
                dqs, dks, dvs = [], [], []
                for pp in range(hps // PAIR):
                    ps = slice(pp * LANES, (pp + 1) * LANES)
                    qp, kp, vp = _bf(q2[:, ps]), _bf(k2[:, ps]), _bf(v2[:, ps])
                    dyp, yp = dy2[:, ps], y2[:, ps]
                    dq_h, dk_h, dv_h = [], [], []
                    for hh, own in enumerate(_pair_lanes()):
                        head = hp * hps + pp * PAIR + hh
                        s = _dot(qp, jnp.where(own, kp, 0), NT) * (HEAD_DIM ** -0.5) + b_ref[later, head]
                        p = jnp.exp(s - l2[:, pp * LANES + hh * HEAD_DIM:pp * LANES + hh * HEAD_DIM + 1])
                        dyh = jnp.where(own, dyp, 0.0)
                        delta = jnp.sum(dyh * yp, axis=-1, keepdims=True)
                        ds = p * (_dot(_bf(dyh), vp, NT) - delta)
                        db_ref[head] += ds
                        dsb = _bf(ds * (HEAD_DIM ** -0.5))
                        dq_h.append(_dot(dsb, kp, NN))
                        dk_h.append(_dot(dsb, qp, TN))
                        dv_h.append(_dot(_bf(p), _bf(dyp), TN))
                    first = _pair_lanes()[0]
                    dqs.append(jnp.where(first, dq_h[0], dq_h[1]))
                    dks.append(jnp.where(first, dk_h[0], dk_h[1]))
                    dvs.append(jnp.where(first, dv_h[0], dv_h[1]))
                dkb = dks[0] if len(dks) == 1 else jnp.concatenate(dks, axis=1)
                dvb = dvs[0] if len(dvs) == 1 else jnp.concatenate(dvs, axis=1)
                dq_ref[rows, :] = dqs[0] if len(dqs) == 1 else jnp.concatenate(dqs, axis=1)
                dk_ref[rows, :] = ck_ref[rows, :] + dkb[:BLK]
                dv_ref[rows, :] = cv_ref[rows, :] + dvb[:BLK]
                ck_ref[rows, :] = dkb[BLK:]
                cv_ref[rows, :] = dvb[BLK:]

            _for_residues(d, residue)

        @pl.when(n == NB)
        def _():
            dk_ref[...] = ck_ref[...]
            dv_ref[...] = cv_ref[...]

    def qn(n):
        return jnp.minimum(n, NB - 1)

    cur, prev = _attn_specs(d, g, qn)
    cw = hps * HEAD_DIM
    row = pl.BlockSpec((d * BLK, cw), lambda hp, n: (qn(n), hp))
    done = pl.BlockSpec((d * BLK, cw), lambda hp, n: (jnp.maximum(n - 1, 0), hp))
    (dq, dk, dv, db), carried = _call(
        body, plans, name=name, grid=(HEADS // hps, NB + 1),
        in_specs=[cur(0), prev(1), cur(1), prev(2), cur(2),
                  pl.BlockSpec((2, HEADS, BLK, 2 * BLK), lambda hp, n: (0, 0, 0, 0)), row, row, row],
        out_specs=[row, done, done, pl.BlockSpec((HEADS, BLK, 2 * BLK), lambda hp, n: (0, 0, 0))],
        out_shape=[_sds((S, GROUP_W), F32)] * 3 + [_sds((HEADS, BLK, 2 * BLK), F32)],
        scratch_shapes=[pltpu.VMEM((d * BLK, cw), F32)] * 2,
        args=(proj, proj, proj, proj, proj, bias, lse, y, dy))
    return ([dq, dk, dv], db) if plans is None else ([dq, dk, dv], db, carried)


BAND = BLK * 2 * BLK


def _bucket_onehot():
    buckets = jnp.stack([_t5_bucket(_band_rel() * d) for d in DILATIONS]).reshape(N_GROUPS, 1, BAND)
    return (buckets == jnp.arange(NUM_BUCKETS).reshape(1, NUM_BUCKETS, 1)).astype(F32)


def _relbias_fwd(rel_bias, name):
    table = rel_bias.reshape(NUM_BUCKETS, N_GROUPS, HEADS).transpose(1, 0, 2)

    def body(t_ref, oh_ref, valid_ref, o_ref):
        bias = lax.dot_general(t_ref[...], oh_ref[...], (TN, ((), ())), preferred_element_type=F32,
                               precision=lax.Precision.HIGHEST)
        for k in range(2):
            o_ref[k] = jnp.where(valid_ref[k] > 0.5, bias, NEG_INF)

    out = pl.pallas_call(
        body, name=name, grid=(N_GROUPS,),
        in_specs=[pl.BlockSpec((None, NUM_BUCKETS, HEADS), lambda g: (g, 0, 0)),
                  pl.BlockSpec((None, NUM_BUCKETS, BAND), lambda g: (g, 0, 0)),
                  pl.BlockSpec((2, 1, BAND), lambda g: (0, 0, 0))],
        out_specs=pl.BlockSpec((None, 2, HEADS, BAND), lambda g: (g, 0, 0, 0)),
        out_shape=_sds((N_GROUPS, 2, HEADS, BAND), F32), compiler_params=_cp(1))(table, _bucket_onehot(), _band_valid())
    return out.reshape(N_GROUPS, 2, HEADS, BLK, 2 * BLK)


def _relbias_bwd(dbs, name):
    band = BAND
    onehot = _bucket_onehot()
    dbf = jnp.stack([db.reshape(HEADS, band) for db in dbs])

    def body(oh_ref, db_ref, o_ref):
        o_ref[...] = lax.dot_general(oh_ref[...], db_ref[...], (NT, ((), ())), preferred_element_type=F32,
                                     precision=lax.Precision.HIGHEST)

    out = pl.pallas_call(
        body, name=name, grid=(N_GROUPS,),
        in_specs=[pl.BlockSpec((None, NUM_BUCKETS, band), lambda g: (g, 0, 0)),
                  pl.BlockSpec((None, HEADS, band), lambda g: (g, 0, 0))],
        out_specs=pl.BlockSpec((None, NUM_BUCKETS, HEADS), lambda g: (g, 0, 0)),
        out_shape=_sds((N_GROUPS, NUM_BUCKETS, HEADS), F32), compiler_params=_cp(1))(onehot, dbf)
    return out.transpose(1, 0, 2).reshape(NUM_BUCKETS, N_GROUPS * HEADS)


def _chunk_pos(shape):
    return lax.broadcasted_iota(jnp.int32, shape, 0) % HG_CHUNK


def _chunk_cumsum(v):
    pos = _chunk_pos(v.shape)
    s = 1
    while s < HG_CHUNK:
        v = v + jnp.where(pos >= s, pltpu.roll(v, s, 0), 0.0)
        s *= 2
    return v


def _chunk_rev_cumsum(v):
    pos = _chunk_pos(v.shape)
    n = v.shape[0]
    s = 1
    while s < HG_CHUNK:
        v = v + jnp.where(pos < HG_CHUNK - s, pltpu.roll(v, n - s, 0), 0.0)
        s *= 2
    return v


def _lower_bound(raw):
    a0, a1 = raw[0:1], raw[1:2]
    m = jnp.maximum(a0, a1)
    e0, e1 = jnp.exp(a0 - m), jnp.exp(a1 - m)
    return e0 / (e0 + e1)


def _hg_gates(qr, fr, lb):
    sf = _sigmoid(fr)
    f = lb + (1.0 - lb) * sf
    sq = _sigmoid(qr)
    return qr * sq, sq, f, sf


HG_COL0 = QKV_W // HG_W


def _hgrn_fwd(proj, lb_raw, nw, name, plans=None):
    S = proj.shape[0]
    ncs = HG_TILE // HG_CHUNK

    def body(q_ref, f_ref, i_ref, og_ref, lb_ref, nw_ref, y_ref, o_ref, st_ref, state):
        @pl.when(pl.program_id(0) == 0)
        def _():
            state[...] = jnp.zeros_like(state)

        lb = _lower_bound(lb_ref[...])
        q, _, f, _ = _hg_gates(q_ref[...], f_ref[...], lb)
        k = 1.0 - f
        G = _chunk_cumsum(jnp.log(f))
        row = lax.broadcasted_iota(jnp.int32, (HG_CHUNK, HG_CHUNK), 0)
        col = lax.broadcasted_iota(jnp.int32, (HG_CHUNK, HG_CHUNK), 1)
        heads = [slice(h * HG_DK, (h + 1) * HG_DK) for h in range(HG_HEADS)]
        sts = [state[h] for h in range(HG_HEADS)]
        for c in range(ncs):
            cs = slice(c * HG_CHUNK, (c + 1) * HG_CHUNK)
            for h, hs in enumerate(heads):
                Gc = G[cs, hs]
                gl = Gc[HG_CHUNK - 1:HG_CHUNK]
                qt = _bf(q[cs, hs] * jnp.exp(Gc))
                kt = _bf(k[cs, hs] * jnp.exp(-Gc))
                kd = _bf(k[cs, hs] * jnp.exp(gl - Gc))
                v = _bf(i_ref[cs, hs])
                A = jnp.where(row >= col, _dot(qt, kt, NT), 0.0)
                o_ref[cs, hs] = _dot(_bf(A), v, NN) + _dot(qt, _bf(sts[h]), NT)
                st_ref[c, h] = sts[h]
                sts[h] = sts[h] * jnp.exp(gl) + _dot(v, kd, TN)
        for h, hs in enumerate(heads):
            state[h] = sts[h]
            oh = o_ref[:, hs]
            og = og_ref[:, hs]
            y_ref[:, hs] = oh * _rinv(oh) * nw_ref[...] * (og * _sigmoid(og))

    def colspec(j):
        return pl.BlockSpec((HG_TILE, HG_W), lambda i: (i, HG_COL0 + j))

    res, carried = _call(
        body, plans, name=name, grid=(S // HG_TILE,),
        in_specs=[colspec(0), colspec(1), colspec(2), colspec(3),
                  pl.BlockSpec((2, HG_W), lambda i: (0, 0)), pl.BlockSpec((1, HG_DK), lambda i: (0, 0))],
        out_specs=[pl.BlockSpec((HG_TILE, HG_W), lambda i: (i, 0))] * 2
        + [pl.BlockSpec((ncs, HG_HEADS, HG_DK, HG_DK), lambda i: (i, 0, 0, 0))],
        out_shape=[_sds((S, HG_W), F32)] * 2 + [_sds((S // HG_CHUNK, HG_HEADS, HG_DK, HG_DK), F32)],
        scratch_shapes=[pltpu.VMEM((HG_HEADS, HG_DK, HG_DK), F32)],
        args=(proj, proj, proj, proj, lb_raw, nw))
    return res if plans is None else (res, carried)


def _hgrn_bwd(proj, lb_raw, nw, o, states, dy, d_attn, d_gates, name):
    S = proj.shape[0]
    ncs = HG_TILE // HG_CHUNK
    nt = S // HG_TILE
    n_a, n_g = len(d_attn), len(d_gates)
    own = [slice(QKV_W + j * HG_W, QKV_W + (j + 1) * HG_W) for j in range(4)]

    def body(q_ref, f_ref, i_ref, og_ref, lb_ref, nw_ref, o_ref, st_ref, dy_ref, *rest):
        attn_refs, gate_refs = rest[:n_a], rest[n_a:n_a + n_g]
        dp_ref, dlb_ref, dnw_ref, dstate, do_s, dG_s, dgl_s, dk_s, dlb_s = rest[n_a + n_g:]
        dq_ref, df_ref, di_ref, dog_ref = (dp_ref.at[:, cols] for cols in own)
        step = pl.program_id(0)
        for k, a_ref in enumerate(attn_refs):
            dp_ref[:, k * GROUP_W:(k + 1) * GROUP_W] = _bf(a_ref[...])
        for k, g_ref in enumerate(gate_refs):
            dp_ref[:, QKV_W + 4 * HG_W + k * D_MODEL:QKV_W + 4 * HG_W + (k + 1) * D_MODEL] = g_ref[...]

        @pl.when(step == 0)
        def _():
            dstate[...] = jnp.zeros_like(dstate)
            dlb_s[...] = jnp.zeros_like(dlb_s)
            dnw_ref[...] = jnp.zeros_like(dnw_ref)

        lb = _lower_bound(lb_ref[...])
        qr = q_ref[...]
        q, sq, f, sf = _hg_gates(qr, f_ref[...], lb)
        k = 1.0 - f
        G = _chunk_cumsum(jnp.log(f))
        nwv = nw_ref[...]
        row = lax.broadcasted_iota(jnp.int32, (HG_CHUNK, HG_CHUNK), 0)
        col = lax.broadcasted_iota(jnp.int32, (HG_CHUNK, HG_CHUNK), 1)
        for h in range(HG_HEADS):
            hs = slice(h * HG_DK, (h + 1) * HG_DK)
            oh = o_ref[:, hs]
            r = _rinv(oh)
            ohat = oh * r
            og = og_ref[:, hs]
            sg = _sigmoid(og)
            dyh = dy_ref[:, hs]
            don = dyh * (og * sg)
            dog_ref[:, hs] = _bf(dyh * (ohat * nwv) * (sg * (1.0 + og * (1.0 - sg))))
            dnw_ref[...] += jnp.sum(don * ohat, axis=0, keepdims=True)
            do_s[:, hs] = _norm_bwd(don, ohat, r, nwv)
        dsts = [dstate[h] for h in range(HG_HEADS)]
        for c in reversed(range(ncs)):
            cs = slice(c * HG_CHUNK, (c + 1) * HG_CHUNK)
            for h in range(HG_HEADS):
                hs = slice(h * HG_DK, (h + 1) * HG_DK)
                dst = dsts[h]
                Gc = G[cs, hs]
                gl = Gc[HG_CHUNK - 1:HG_CHUNK]
                eG, enG, edG, egl = jnp.exp(Gc), jnp.exp(-Gc), jnp.exp(gl - Gc), jnp.exp(gl)
                qt, kt, kd = q[cs, hs] * eG, k[cs, hs] * enG, k[cs, hs] * edG
                qtb, ktb, kdb = _bf(qt), _bf(kt), _bf(kd)
                v = _bf(i_ref[cs, hs])
                do = _bf(do_s[cs, hs])
                st = st_ref[c, h]
                dstb = _bf(dst)
                A = jnp.where(row >= col, _dot(qtb, ktb, NT), 0.0)
                dA = _bf(jnp.where(row >= col, _dot(do, v, NT), 0.0))
                di_ref[cs, hs] = _bf(_dot(_bf(A), do, TN) + _dot(kdb, dstb, NT))
                dqt = _dot(dA, ktb, NN) + _dot(do, _bf(st), NN)
                dkt = _dot(dA, qtb, TN)
                dkd = _dot(v, dstb, NN)
                dgl = egl * jnp.sum(st * dst, axis=0, keepdims=True) + jnp.sum(dkd * kd, axis=0, keepdims=True)
                dsts[h] = dst * egl + _dot(do, qtb, TN)
                dq_ref[cs, hs] = _bf(dqt * eG * (sq[cs, hs] * (1.0 + qr[cs, hs] * (1.0 - sq[cs, hs]))))
                dk_s[cs, hs] = dkt * enG + dkd * edG
                dG_s[cs, hs] = dqt * qt - dkt * kt - dkd * kd
                dgl_s[cs, hs] = jnp.broadcast_to(dgl, (HG_CHUNK, HG_DK))
        for h in range(HG_HEADS):
            dstate[h] = dsts[h]
        dg = _chunk_rev_cumsum(dG_s[...]) + dgl_s[...]
        dfv = dg / f - dk_s[...]
        df_ref[...] = _bf(dfv * (1.0 - lb) * sf * (1.0 - sf))
        dlb_s[...] += jnp.sum(dfv * (1.0 - sf), axis=0, keepdims=True)

        @pl.when(step == nt - 1)
        def _():
            t = dlb_s[...] * lb * (1.0 - lb)
            dlb_ref[...] = jnp.concatenate([t, -t], axis=0)

    def colspec(j):
        return pl.BlockSpec((HG_TILE, HG_W), lambda i: (nt - 1 - i, HG_COL0 + j))

    def rows(width):
        return pl.BlockSpec((HG_TILE, width), lambda i: (nt - 1 - i, 0))

    tile = rows(HG_W)
    return pl.pallas_call(
        body, name=name, grid=(nt,),
        in_specs=[colspec(0), colspec(1), colspec(2), colspec(3),
                  pl.BlockSpec((2, HG_W), lambda i: (0, 0)), pl.BlockSpec((1, HG_DK), lambda i: (0, 0)),
                  tile, pl.BlockSpec((ncs, HG_HEADS, HG_DK, HG_DK), lambda i: (nt - 1 - i, 0, 0, 0)), tile]
        + [rows(GROUP_W)] * n_a + [rows(D_MODEL)] * n_g,
        out_specs=[rows(IN_W), pl.BlockSpec((2, HG_W), lambda i: (0, 0)), pl.BlockSpec((1, HG_DK), lambda i: (0, 0))],
        out_shape=[_sds((S, IN_W), BF16), _sds((2, HG_W), F32), _sds((1, HG_DK), F32)],
        scratch_shapes=[pltpu.VMEM((HG_HEADS, HG_DK, HG_DK), F32)] + [pltpu.VMEM((HG_TILE, HG_W), F32)] * 4
        + [pltpu.VMEM((1, HG_W), F32)],
        compiler_params=_cp(1))(proj, proj, proj, proj, lb_raw, nw, o, states, dy, *d_attn, *d_gates)


GATE_COL0 = (QKV_W + 4 * HG_W) // GROUP_W
HALF_D = D_MODEL // 2


def _gate_tiles(proj):
    return [_tile(proj, HALF_D, functools.partial(lambda c, k: GATE_COL0 + k, k=k)) for k in range(4)]


def _gates(g_refs):
    s0 = _sigmoid(jnp.concatenate([g_refs[0][...], g_refs[1][...]], axis=1))
    s1 = _sigmoid(jnp.concatenate([g_refs[2][...], g_refs[3][...]], axis=1))
    return s0, s1


def _branch_fwd(os_, lses, yh, proj, w_a, w_h, name, plans=None):
    nb = w_a.shape[0]

    def body(o0, o1, o2, l0, l1, l2, yh_ref, g0a, g0b, g1a, g1b, wa_ref, wh_ref,
             y_ref, lse_ref, za_ref, zh_ref, m_ref):
        a, b, c = l0[...], l1[...], l2[...]
        m = jnp.maximum(jnp.maximum(a, b), c)
        ea, eb, ec = jnp.exp(a - m), jnp.exp(b - m), jnp.exp(c - m)
        den = ea + eb + ec
        y = (ea * o0[...] + eb * o1[...] + ec * o2[...]) / den
        y_ref[...] = y
        lse_ref[...] = m + jnp.log(den)
        yb, yhb = _bf(y), _bf(yh_ref[...])
        za = jnp.concatenate([_dot(yb, wa_ref[j], NN) for j in range(nb)], axis=1)
        zh = jnp.concatenate([_dot(yhb, wh_ref[j], NN) for j in range(nb)], axis=1)
        s0, s1 = _gates((g0a, g0b, g1a, g1b))
        za_ref[...] = za
        zh_ref[...] = zh
        m_ref[...] = _bf(s0 * za + s1 * zh)

    return _rows_call(name, body, yh.shape[0], 512, 1,
                      [*[_tile(t, GROUP_W) for t in (*os_, *lses)], _tile(yh, HG_W), *_gate_tiles(proj),
                       _full(w_a), _full(w_h)],
                      [_out_tile(GROUP_W, F32, GROUP_W)] * 2 + [_out_tile(D_MODEL, F32, D_MODEL)] * 2
                      + [_out_tile(D_MODEL, BF16, D_MODEL)], plans)


def _branch_bwd(dm, za, zh, proj, w_a, w_h, name, plans=None):
    nb, _, Nb = w_a.shape

    def body(dm_ref, za_ref, zh_ref, g0a, g0b, g1a, g1b, wa_ref, wh_ref,
             dza_ref, dzh_ref, dg0_ref, dg1_ref, dy_ref, dyh_ref):
        dmv = dm_ref[...]
        s0, s1 = _gates((g0a, g0b, g1a, g1b))
        dza, dzh = _bf(dmv * s0), _bf(dmv * s1)
        dza_ref[...] = dza
        dzh_ref[...] = dzh
        dg0_ref[...] = _bf(dmv * za_ref[...] * s0 * (1.0 - s0))
        dg1_ref[...] = _bf(dmv * zh_ref[...] * s1 * (1.0 - s1))
        dy_ref[...] = sum(_dot(dza[:, j * Nb:(j + 1) * Nb], wa_ref[j], NT) for j in range(nb))
        dyh_ref[...] = sum(_dot(dzh[:, j * Nb:(j + 1) * Nb], wh_ref[j], NT) for j in range(nb))

    return _rows_call(name, body, za.shape[0], 512, 1,
                      [_tile(dm, D_MODEL), _tile(za, D_MODEL), _tile(zh, D_MODEL), *_gate_tiles(proj),
                       _full(w_a), _full(w_h)],
                      [_out_tile(D_MODEL, BF16, D_MODEL)] * 4 + [_out_tile(GROUP_W, F32, GROUP_W),
                                                                 _out_tile(HG_W, F32, HG_W)], plans)


def _mix_out(merged, w_out, x, w_post, w_pre, name):
    def body(m_ref, wo_ref, x_ref, wp_ref, wf_ref, mo_ref, x1_ref, h2_ref):
        z = _dot(m_ref[...], wo_ref[...], NN)
        mo_ref[...] = z
        x1 = x_ref[...] + z * _rinv(z) * wp_ref[...]
        x1_ref[...] = x1
        h2_ref[...] = _bf(x1 * _rinv(x1) * wf_ref[...])

    return _rows_call(name, body, x.shape[0], 512, 1,
                      [_tile(merged, D_MODEL), _full(w_out), _tile(x, D_MODEL), _full(w_post), _full(w_pre)],
                      [_out_tile(D_MODEL, F32, D_MODEL), _out_tile(D_MODEL, F32, D_MODEL),
                       _out_tile(D_MODEL, BF16, D_MODEL)])


def _loss_head(a, w_down, x1, tgt, w, name):
    def body(a_ref, wd_ref, x1_ref, t_ref, w_ref, dx_ref, df_ref, dw_ref, loss_ref):
        z = _dot(a_ref[...], wd_ref[...], NN)
        r = _rinv(z)
        zhat = z * r
        wv = w_ref[...]
        e = x1_ref[...] + zhat * wv - t_ref[...]
        dx = e * (1.0 / D_MODEL)
        dx_ref[...] = dx
        df_ref[...] = _bf(_norm_bwd(dx, zhat, r, wv))
        _acc(dw_ref, jnp.sum(dx * zhat, axis=0, keepdims=True))
        part = 0.5 * jnp.sum(jnp.sum(e * e, axis=1, keepdims=True), axis=0, keepdims=True) * (1.0 / D_MODEL)
        _acc(loss_ref, jnp.broadcast_to(part, (1, LANES)))

    return _rows_call(name, body, x1.shape[0], 512, 1,
                      [_tile(a, D_FF), _full(w_down), _tile(x1, D_MODEL), _tile(tgt, D_MODEL), _full(w)],
                      [_out_tile(D_MODEL, F32, D_MODEL), _out_tile(D_MODEL, BF16, D_MODEL),
                       _out_acc(1, D_MODEL, D_MODEL), _out_acc(1, LANES, LANES)])


CONV_CB = D_FF // 2
CONV_TM = 512
HALO = 8
SQRT_HALF = 0.7071067811865476
INV_SQRT_2PI = 0.3989422804014327


CONV_RS = 32


def _lane_tiles():
    return [slice(k * LANES, (k + 1) * LANES) for k in range(CONV_CB // LANES)]


def _strip_start(i):
    return pl.multiple_of(i * CONV_RS, CONV_RS)


def _strip_taps(u_ref, halo_ref, r0, cs, first_strip, first_tile):
    if first_strip:
        before = jnp.where(first_tile, 0.0, halo_ref[:, cs])
        blk = jnp.concatenate([before, u_ref[0:CONV_RS, cs]], axis=0)
    else:
        blk = u_ref[pl.ds(pl.multiple_of(r0 - HALO, HALO), CONV_RS + HALO), cs]
    return pltpu.roll(blk, 2, 0)[HALO:], pltpu.roll(blk, 1, 0)[HALO:], blk[HALO:]


def _conv(taps, w_ref, b_ref, cs):
    return b_ref[:, cs] + w_ref[0:1, cs] * taps[0] + w_ref[1:2, cs] * taps[1] + w_ref[2:3, cs] * taps[2]


def _conv_specs(tm):
    nh = tm // HALO
    nc = D_FF // CONV_CB

    def tile(off):
        return pl.BlockSpec((tm, CONV_CB), lambda c, i: (i, off + c))

    def halo(off):
        return pl.BlockSpec((HALO, CONV_CB), lambda c, i: (jnp.maximum(i * nh - 1, 0), off + c))

    def small(rows, off):
        return pl.BlockSpec((rows, CONV_CB), lambda c, i: (0, off + c))

    return nc, tile, halo, small


def _conv_gelu_fwd(u, cw, cb, name, plans=None):
    S = u.shape[0]
    tm = CONV_TM
    nc, tile, halo, small = _conv_specs(tm)

    def body(ug, hg, uv, hv, wg, wv, bg, bv, a_ref):
        first_tile = pl.program_id(1) == 0

        def strip(r0, first_strip):
            for cs in _lane_tiles():
                cg = _conv(_strip_taps(ug, hg, r0, cs, first_strip, first_tile), wg, bg, cs)
                cv = _conv(_strip_taps(uv, hv, r0, cs, first_strip, first_tile), wv, bv, cs)
                a_ref[pl.ds(r0, CONV_RS), cs] = _bf(0.5 * cg * (1.0 + lax.erf(cg * SQRT_HALF)) * cv)

        strip(0, True)
        lax.fori_loop(1, tm // CONV_RS, lambda k, c: (strip(_strip_start(k), False), c)[1], 0)

    (a,), carried = _call(
        body, plans, name=name, grid=(nc, S // tm),
        in_specs=[tile(0), halo(0), tile(nc), halo(nc), small(3, 0), small(3, nc), small(1, 0), small(1, nc)],
        out_specs=[tile(0)], out_shape=[_sds((S, D_FF), BF16)], args=(u, u, u, u, cw, cw, cb, cb))
    return a if plans is None else (a, carried)


def _conv_gelu_bwd(u, dff, w_down, cw, cb, name, plans=None):
    S = u.shape[0]
    tm = CONV_TM
    nt = S // tm
    nc, tile, halo, small = _conv_specs(tm)

    def body(ug, hg, uv, hv, wg, wv, bg, bv, dff_ref, wd_ref, dcg_ref, dcv_ref, dwg_ref, dwv_ref, dbg_ref, dbv_ref,
             acc, da_ref):
        i = pl.program_id(1)
        first_tile = i == 0
        da_ref[...] = _dot(dff_ref[...], wd_ref[...], NT)

        @pl.when(first_tile)
        def _():
            acc[...] = jnp.zeros_like(acc)

        def strip(r0, first_strip):
            rows = pl.ds(r0, CONV_RS)
            for cs in _lane_tiles():
                tg = _strip_taps(ug, hg, r0, cs, first_strip, first_tile)
                tv = _strip_taps(uv, hv, r0, cs, first_strip, first_tile)
                cg = _conv(tg, wg, bg, cs)
                cv = _conv(tv, wv, bv, cs)
                phi = 0.5 * (1.0 + lax.erf(cg * SQRT_HALF))
                dav = da_ref[rows, cs]
                dcg = dav * cv * (phi + cg * jnp.exp(-0.5 * cg * cg) * INV_SQRT_2PI)
                dcv = dav * (cg * phi)
                dcg_ref[rows, cs] = dcg
                dcv_ref[rows, cs] = dcv
                for half, (dc, taps) in enumerate(((dcg, tg), (dcv, tv))):
                    for j in range(3):
                        acc[4 * half + j, :, cs] += dc * taps[j]
                    acc[4 * half + 3, :, cs] += dc

        strip(0, True)
        lax.fori_loop(1, tm // CONV_RS, lambda k, c: (strip(_strip_start(k), False), c)[1], 0)

        @pl.when(i == nt - 1)
        def _():
            for half, (dw_ref, db_ref) in enumerate(((dwg_ref, dbg_ref), (dwv_ref, dbv_ref))):
                for j in range(3):
                    dw_ref[j:j + 1, :] = jnp.sum(acc[4 * half + j], axis=0, keepdims=True)
                db_ref[...] = jnp.sum(acc[4 * half + 3], axis=0, keepdims=True)

    res, carried = _call(
        body, plans, name=name, grid=(nc, nt),
        in_specs=[tile(0), halo(0), tile(nc), halo(nc), small(3, 0), small(3, nc), small(1, 0), small(1, nc),
                  pl.BlockSpec((tm, D_MODEL), lambda c, i: (i, 0)), pl.BlockSpec((CONV_CB, D_MODEL), lambda c, i: (c, 0))],
        out_specs=[tile(0), tile(0), small(3, 0), small(3, 0), small(1, 0), small(1, 0)],
        out_shape=[_sds((S, D_FF), F32)] * 2 + [_sds((3, D_FF), F32)] * 2 + [_sds((1, D_FF), F32)] * 2,
        scratch_shapes=[pltpu.VMEM((8, CONV_RS, CONV_CB), F32), pltpu.VMEM((tm, CONV_CB), F32)],
        args=(u, u, u, u, cw, cw, cb, cb, dff, w_down))
    return res if plans is None else (res, carried)


def _conv_input_bwd(dcg, dcv, cw, name, plans=None):
    S = dcg.shape[0]
    tm = CONV_TM // 2
    nh = tm // HALO
    nt = S // tm
    n = CONV_RS + HALO
    tile = pl.BlockSpec((tm, D_FF), lambda i: (i, 0))
    nxt = pl.BlockSpec((HALO, D_FF), lambda i: (jnp.minimum((i + 1) * nh, S // HALO - 1), 0))

    def body(g_ref, ng_ref, v_ref, nv_ref, w_ref, du_ref):
        last_tile = pl.program_id(0) == nt - 1

        def strip(r0, last_strip):
            for half, (dc_ref, n_ref) in enumerate(((g_ref, ng_ref), (v_ref, nv_ref))):
                for k in range(D_FF // LANES):
                    cs = slice(k * LANES, (k + 1) * LANES)
                    ws = slice(half * D_FF + k * LANES, half * D_FF + (k + 1) * LANES)
                    if last_strip:
                        after = jnp.where(last_tile, 0.0, n_ref[:, cs])
                        blk = jnp.concatenate([dc_ref[tm - CONV_RS:tm, cs], after], axis=0)
                    else:
                        blk = dc_ref[pl.ds(r0, n), cs]
                    d1 = pltpu.roll(blk, n - 1, 0)[:CONV_RS]
                    d2 = pltpu.roll(blk, n - 2, 0)[:CONV_RS]
                    du_ref[pl.ds(r0, CONV_RS), ws] = _bf(w_ref[2:3, ws] * blk[:CONV_RS] + w_ref[1:2, ws] * d1
                                                         + w_ref[0:1, ws] * d2)

        lax.fori_loop(0, tm // CONV_RS - 1, lambda k, c: (strip(_strip_start(k), False), c)[1], 0)
        strip(tm - CONV_RS, True)

    (du,), carried = _call(
        body, plans, name=name, grid=(nt,),
        in_specs=[tile, nxt, tile, nxt, pl.BlockSpec((3, 2 * D_FF), lambda i: (0, 0))],
        out_specs=[pl.BlockSpec((tm, 2 * D_FF), lambda i: (i, 0))], out_shape=[_sds((S, 2 * D_FF), BF16)],
        args=(dcg, dcg, dcv, dcv, cw))
    return du if plans is None else (du, carried)


def _row_tile(n, cap):
    best = n
    for t in range(16, cap + 1, 16):
        if n % t == 0:
            best = t
    return best if best <= cap else n


def _rows_for_bytes(nbytes, cols):
    return max(16, nbytes // (4 * cols) // 16 * 16)


def _adamw_update(w_ref, g_ref, m_ref, v_ref, d_ref, nm_ref, nv_ref):
    gv = g_ref[...]
    nm = ADAM_B1 * m_ref[...] + (1.0 - ADAM_B1) * gv
    nv = ADAM_B2 * v_ref[...] + (1.0 - ADAM_B2) * (gv * gv)
    m_hat = nm / (1.0 - ADAM_B1 ** ADAM_STEP)
    v_hat = nv / (1.0 - ADAM_B2 ** ADAM_STEP)
    d_ref[...] = -ADAM_LR * (m_hat / (jnp.sqrt(v_hat) + ADAM_EPS) + ADAM_WD * w_ref[...])
    nm_ref[...] = nm
    nv_ref[...] = nv


def _adamw(w, g, m, v, name):
    R, C = w.shape
    tr = _row_tile(R, _rows_for_bytes(2 << 20, C))
    spec = pl.BlockSpec((tr, C), lambda i: (i, 0))
    body = functools.partial(_adamw_update)
    return pl.pallas_call(body, name=name, grid=(R // tr,), in_specs=[spec] * 4, out_specs=[spec] * 3,
                          out_shape=[_sds((R, C), F32)] * 3, compiler_params=_cp(1))(w, g, m, v)


def _adamw_small(ws, gs, ms, vs, name):
    n = len(ws)

    def body(*refs):
        ins, outs = refs[:4 * n], refs[4 * n:]
        for k in range(n):
            _adamw_update(ins[k], ins[n + k], ins[2 * n + k], ins[3 * n + k], outs[3 * k], outs[3 * k + 1], outs[3 * k + 2])

    vm = pl.BlockSpec(memory_space=pltpu.VMEM)
    outs = pl.pallas_call(body, name=name, in_specs=[vm] * (4 * n), out_specs=[vm] * (3 * n),
                          out_shape=[_sds(w.shape, F32) for w in ws for _ in range(3)])(*ws, *gs, *ms, *vs)
    return [tuple(outs[3 * k:3 * k + 3]) for k in range(n)]


def _pair_sum(gfull, rcv, c_idx, name):
    nb, R, C = gfull.shape
    half = R // 2
    tr = _row_tile(half, _rows_for_bytes(2 << 20, C))
    nt = half // tr

    def body(c_ref, g_ref, r_ref, o_ref):
        o_ref[...] = _bf(g_ref[...] + r_ref[...])

    return pl.pallas_call(
        body, name=name,
        grid_spec=pltpu.PrefetchScalarGridSpec(
            num_scalar_prefetch=1, grid=(nb, nt),
            in_specs=[pl.BlockSpec((None, tr, C), lambda j, i, c_ref: (j, c_ref[0] * nt + i, 0)),
                      pl.BlockSpec((None, tr, C), lambda j, i, c_ref: (j, i, 0))],
            out_specs=pl.BlockSpec((None, tr, C), lambda j, i, c_ref: (j, i, 0))),
        out_shape=_sds((nb, half, C), BF16), compiler_params=_cp(2))(c_idx, gfull, rcv)


def _chip_sum(arrived, own, place, name):
    nb, H, C = arrived.shape
    tr = _row_tile(H, _rows_for_bytes(2 << 20, C))
    nt = H // tr

    def body(pl_ref, *refs):
        o_ref = refs[nb + 1]
        me = pl_ref[0]
        acc = None
        for k in range(nb):
            term = jnp.where(me == k, refs[nb][...], refs[k][...]).astype(F32)
            acc = term if acc is None else acc + term
        o_ref[...] = acc

    def other(k):
        return pl.BlockSpec((None, tr, C), lambda i, p: (jnp.where(p[0] == k, (k + 1) % nb, k), i, 0))

    return pl.pallas_call(
        body, name=name,
        grid_spec=pltpu.PrefetchScalarGridSpec(
            num_scalar_prefetch=1, grid=(nt,),
            in_specs=[other(k) for k in range(nb)] + [pl.BlockSpec((None, tr, C), lambda i, p: (p[0], i, 0))],
            out_specs=pl.BlockSpec((tr, C), lambda i, p: (p[1] * nt + i, 0))),
        out_shape=_sds((2 * H, C), F32), compiler_params=_cp(1))(place, *([arrived] * nb), own)


def _cast_into_slot(shard, place, name):
    R, C = shard.shape
    tr = _row_tile(R, 256)

    def body(pl_ref, s_ref, o_ref):
        o_ref[...] = _bf(s_ref[...])

    return pl.pallas_call(
        body, name=name,
        grid_spec=pltpu.PrefetchScalarGridSpec(
            num_scalar_prefetch=1, grid=(R // tr,),
            in_specs=[pl.BlockSpec((tr, C), lambda i, p: (i, 0))],
            out_specs=pl.BlockSpec((None, tr, C), lambda i, p: (p[0], i, 0))),
        out_shape=_sds((N_CHIPS, R, C), BF16), compiler_params=_cp(1))(place, shard)


def _place():
    x, y, c = lax.axis_index("x"), lax.axis_index("y"), lax.axis_index("c")
    chips = [(1 - x, y), (x, 1 - y), (1 - x, 1 - y)]
    return x, y, c, chips


def _chip_id(px, py):
    return 2 * px + py


def _remote(src, dst, send_sems, recv_sems, k, to):
    return pltpu.make_async_remote_copy(src_ref=src, dst_ref=dst, send_sem=send_sems.at[k], recv_sem=recv_sems.at[k],
                                        device_id=to, device_id_type=MESH)


def _proj_gathered(x, w_norm, slot, place, name, tm=1024, plan=None):
    M, K = x.shape
    nb, _, Nb = slot.shape
    half = K // 2
    nt = M // tm
    cx, cy = place[0] // 2, place[0] % 2
    order = jnp.stack([place[0], _chip_id(1 - cx, cy), _chip_id(cx, 1 - cy), _chip_id(1 - cx, 1 - cy)]).astype(jnp.int32)

    p_in = [] if plan is None else plan.ins + plan.inouts
    p_out = [] if plan is None else [_sds(a.shape, a.dtype) for a in plan.inouts] + plan.outs
    n_pi, n_po = len(p_in), len(p_out)

    def body(order_ref, x_ref, wn_ref, slot_in, *refs):
        o_ref, slot_ref, h_out = refs[n_pi:n_pi + 3]
        s0 = n_pi + 3 + n_po
        w_buf, hs, ici_send, ici_recv, pass_send, pass_recv, load_sem = refs[s0:s0 + 7]

        def carried():
            if plan is None:
                return [], [], []
            ins = refs[:len(plan.ins)]
            outs = refs[n_pi + 3:n_pi + 3 + n_po]
            return plan.copies(ins, outs[:len(plan.inouts)], outs[len(plan.inouts):], *refs[s0 + 7:])

        b, i = pl.program_id(0), pl.program_id(1)
        x, y, c, chips = _place()
        me = _chip_id(x, y)
        sib = (x, y, 1 - c)
        mine, other = pl.ds(c * half, half), pl.ds((1 - c) * half, half)

        def sent(k):
            blk = slot_ref.at[me, mine]
            return _remote(blk, blk, ici_send, ici_recv, k, (*chips[k], c))

        def landed(k):
            blk = slot_ref.at[_chip_id(*chips[k]), mine]
            return _remote(blk, blk, ici_send, ici_recv, k, (*chips[k], c))

        def passed(k, rows):
            blk = slot_ref.at[_chip_id(*chips[k]), rows]
            return _remote(blk, blk, pass_send, pass_recv, k, sib)

        @pl.when((b == 0) & (i == 0))
        def _():
            for k in range(len(chips)):
                sent(k).start()
            sends, _, local = carried()
            for cp in (*sends, *local):
                cp.start()

        for k in range(len(chips)):
            @pl.when((b == k + 1) & (i == 0))
            def _(k=k):
                landed(k).wait_recv()
                passed(k, mine).start()
                passed(k, other).wait_recv()

        @pl.when(i == 0)
        def _():
            load = pltpu.make_async_copy(slot_ref.at[order_ref[b]], w_buf, load_sem.at[0])
            load.start()
            load.wait()

        rows = pl.ds(pl.multiple_of(i * tm, tm), tm)
        keep_h = pltpu.make_async_copy(hs, h_out, load_sem.at[1])

        @pl.when(b == 0)
        def _():
            xv = x_ref[...]
            hs[rows, :] = _bf(xv * _rinv(xv) * wn_ref[...])

        @pl.when((b == 1) & (i == 0))
        def _():
            keep_h.start()

        o_ref[...] = _dot(hs[rows, :], w_buf[...], NN)

        @pl.when((b == nb - 1) & (i == nt - 1))
        def _():
            for k in range(len(chips)):
                sent(k).wait_send()
                passed(k, mine).wait_send()
            sends, recvs, local = carried()
            for cp in recvs:
                cp.wait_recv()
            for cp in sends:
                cp.wait_send()
            for cp in local:
                cp.wait()
            keep_h.wait()

    n_peers = N_CHIPS - 1
    return pl.pallas_call(
        body, name=name,
        grid_spec=pltpu.PrefetchScalarGridSpec(
            num_scalar_prefetch=1, grid=(nb, nt),
            in_specs=[pl.BlockSpec((tm, K), lambda b, i, o: (i, 0)), pl.BlockSpec((1, K), lambda b, i, o: (0, 0)), ANY]
            + [ANY] * n_pi,
            out_specs=[pl.BlockSpec((tm, Nb), lambda b, i, o: (i, o[b])), ANY, ANY] + [ANY] * n_po,
            scratch_shapes=[pltpu.VMEM((K, Nb), BF16), pltpu.VMEM((M, K), BF16)]
            + [pltpu.SemaphoreType.DMA((n_peers,))] * 4 + [pltpu.SemaphoreType.DMA((2,))]
            + ([] if plan is None else [pltpu.SemaphoreType.DMA((plan.n_sems,))] * 3)),
        out_shape=[_sds((M, nb * Nb), F32), _sds(slot.shape, slot.dtype), _sds((M, K), BF16)] + p_out,
        input_output_aliases={3: 1, **({} if plan is None else
                                       {4 + len(plan.ins) + a: 3 + a for a in range(len(plan.inouts))})},
        compiler_params=_cp(2))(order, x, w_norm, slot, *p_in)


def _gather_ici_plan(slots, wholes, part=None):
    ns, nw = len(slots), len(wholes)

    def copies(ins, ios, outs, send_sems, recv_sems, local_sems):
        x, y, c, chips = _place()
        me = _chip_id(x, y)
        sends, recvs = [], []
        for a in range(ns + nw):
            dst = ios[a] if a < ns else outs[a - ns]
            R = dst.shape[1]
            r0, nr = (0, R // 2) if part is None else part
            rows = pl.ds(c * (R // 2) + r0, nr) if a < ns else pl.ds(0, R)
            src = dst.at[me, rows] if a < ns else ins[a - ns]
            for j, chip in enumerate(chips):
                sends.append(_remote(src, dst.at[me, rows], send_sems, recv_sems, 3 * a + j, (*chip, c)))
                landed = dst.at[_chip_id(*chip), rows]
                recvs.append(_remote(landed, landed, send_sems, recv_sems, 3 * a + j, (*chip, c)))
        local = [pltpu.make_async_copy(ins[b], outs[b].at[me], local_sems.at[b]) for b in range(nw)]
        return sends, recvs, local

    return _Plan(copies, 3 * (ns + nw), ins=wholes, inouts=slots,
                 outs=[_sds((N_CHIPS, *s.shape), s.dtype) for s in wholes])


def _gather_pass_plan(slots):
    def copies(ins, ios, outs, send_sems, recv_sems, local_sems):
        x, y, c, chips = _place()
        sib = (x, y, 1 - c)
        sends, recvs = [], []
        for a, buf in enumerate(ios):
            half = buf.shape[1] // 2
            for j, chip in enumerate(chips):
                mine = buf.at[_chip_id(*chip), pl.ds(c * half, half)]
                other = buf.at[_chip_id(*chip), pl.ds((1 - c) * half, half)]
                sends.append(_remote(mine, mine, send_sems, recv_sems, 3 * a + j, sib))
                recvs.append(_remote(other, other, send_sems, recv_sems, 3 * a + j, sib))
        return sends, recvs, []

    return _Plan(copies, 3 * len(slots), inouts=slots)


def _pair_plan(grads):
    def copies(ins, ios, outs, send_sems, recv_sems, local_sems):
        x, y, c, _ = _place()
        sib = (x, y, 1 - c)
        sends, recvs = [], []
        for a, g in enumerate(ins):
            half = g.shape[1] // 2
            sends.append(_remote(g.at[:, pl.ds((1 - c) * half, half), :], outs[a], send_sems, recv_sems, a, sib))
            recvs.append(_remote(outs[a], outs[a], send_sems, recv_sems, a, sib))
        return sends, recvs, []

    return _Plan(copies, len(grads), ins=grads,
                 outs=[_sds((g.shape[0], g.shape[1] // 2, g.shape[2]), g.dtype) for g in grads])


def _chip_plan(parts):
    def copies(ins, ios, outs, send_sems, recv_sems, local_sems):
        x, y, c, chips = _place()
        me = _chip_id(x, y)
        sends, recvs = [], []
        for a, part in enumerate(ins):
            for j, chip in enumerate(chips):
                sends.append(_remote(part.at[_chip_id(*chip)], outs[a].at[me], send_sems, recv_sems, 3 * a + j, (*chip, c)))
                landed = outs[a].at[_chip_id(*chip)]
                recvs.append(_remote(landed, landed, send_sems, recv_sems, 3 * a + j, (*chip, c)))
        return sends, recvs, []

    return _Plan(copies, 3 * len(parts), ins=parts, outs=[_sds(p.shape, p.dtype) for p in parts])


def _all_sum(pack, fulls, name):
    R, C = pack.shape
    n = len(fulls)

    def body(p_ref, *refs):
        o_ref, halves = refs[n], refs[n + 1:2 * n + 1]
        buf, send_sems, recv_sems, pair_send, pair_recv = refs[2 * n + 1:]
        x, y, c, _ = _place()
        sib = (x, y, 1 - c)
        pair = []
        for a, full in enumerate(halves):
            H = full.shape[0] // 2
            mine = full.at[pl.ds(c * H, H)]
            cp = _remote(mine, mine, pair_send, pair_recv, a, sib)
            cp.start()
            pair.append(cp)
        me = 4 * x + 2 * y + c
        buf[me] = p_ref[...]
        cps = []
        for k in range(1, N_DEV):
            to = (x ^ (k >> 2), y ^ ((k >> 1) & 1), c ^ (k & 1))
            cp = _remote(p_ref, buf.at[me], send_sems, recv_sems, k - 1, to)
            cp.start()
            cps.append(cp)
        for k in range(1, N_DEV):
            frm = (x ^ (k >> 2), y ^ ((k >> 1) & 1), c ^ (k & 1))
            slot = buf.at[4 * frm[0] + 2 * frm[1] + frm[2]]
            _remote(slot, slot, send_sems, recv_sems, k - 1, frm).wait_recv()
        acc = buf[0]
        for k in range(1, N_DEV):
            acc = acc + buf[k]
        o_ref[...] = acc
        for cp in cps:
            cp.wait_send()
        for a, (full, cp) in enumerate(zip(halves, pair)):
            H = full.shape[0] // 2
            other = full.at[pl.ds((1 - c) * H, H)]
            _remote(other, other, pair_send, pair_recv, a, sib).wait_recv()
            cp.wait_send()

    vm = pl.BlockSpec(memory_space=pltpu.VMEM)
    res = pl.pallas_call(
        body, name=name, in_specs=[vm] + [ANY] * n, out_specs=[vm] + [ANY] * n,
        out_shape=[_sds((R, C), F32)] + [_sds(f.shape, f.dtype) for f in fulls],
        input_output_aliases={1 + a: 1 + a for a in range(n)},
        scratch_shapes=[pltpu.VMEM((N_DEV, R, C), F32), pltpu.SemaphoreType.DMA((N_DEV - 1,)),
                        pltpu.SemaphoreType.DMA((N_DEV - 1,)), pltpu.SemaphoreType.DMA((n,)),
                        pltpu.SemaphoreType.DMA((n,))])(pack, *fulls)
    return res[0], list(res[1:])


def _local_step(xs, tgt, p, ex):
    proj, h1 = ex.project(xs, p["pre_mix_norm"])
    biases = _relbias_fwd(p["rel_bias"], "rel_bias_fwd")
    fw = []
    for g in range(N_GROUPS):
        res, got = _attn_fwd(proj, biases[g], g, f"attn_fwd{g}", plans=ex.carry(f"attn_fwd{g}"))
        ex.done(f"attn_fwd{g}", got)
        fw.append(res)
    (yh, o_h, states), got = _hgrn_fwd(proj, p["hgrn_lb_raw"], p["hgrn_norm"], "hgrn_fwd", plans=ex.carry("hgrn_fwd"))
    ex.done("hgrn_fwd", got)
    W_a, W_h, W_out = ex.weight("w_branch_attn"), ex.weight("w_branch_hgrn"), ex.weight("w_out")
    (y, lse, za, zh, merged), got = _branch_fwd([t[0] for t in fw], [t[1] for t in fw], yh, proj, W_a, W_h,
                                                "branch_fwd", plans=ex.carry("branch_fwd"))
    ex.done("branch_fwd", got)
    W_up, conv_w = ex.weight("w_up"), ex.weight("conv_w")
    mo, x1, h2 = _mix_out(merged, W_out, xs, p["post_mix_norm"], p["pre_ffn_norm"], "mix_out")
    u, got = _mm_nn_blk(h2, W_up, "ffn_up", tm=1024, plans=ex.carry("ffn_up"))
    ex.done("ffn_up", got)
    a, got = _conv_gelu_fwd(u, conv_w, p["conv_b"], "conv_gelu_fwd", plans=ex.carry("conv_gelu_fwd"))
    ex.done("conv_gelu_fwd", got)
    W_down = ex.weight("w_down")
    dx2, dff, g_post_ffn, loss = _loss_head(a, W_down, x1, tgt, p["post_ffn_norm"], "ffn_down_loss")

    ex.grad("w_down", _mm_tn(a, dff, "g_w_down").reshape(N_CHIPS, D_FF // N_CHIPS, D_MODEL))
    (dcg, dcv, gwg, gwv, gbg, gbv), got = _conv_gelu_bwd(u, dff, W_down, conv_w, p["conv_b"], "conv_gelu_bwd",
                                                          plans=ex.carry("conv_gelu_bwd"))
    ex.done("conv_gelu_bwd", got)
    g_conv_w = jnp.concatenate([gwg, gwv], axis=1)
    g_conv_b = jnp.concatenate([gbg, gbv], axis=1)
    du, got = _conv_input_bwd(dcg, dcv, conv_w, "conv_input_bwd", plans=ex.carry("conv_input_bwd"))
    ex.done("conv_input_bwd", got)
    dx1, g_pre_ffn = _mm_nt_prenorm_bwd(du, W_up, x1, p["pre_ffn_norm"], dx2, "d_ffn_in")
    ex.grad("w_up", _mm_tn_blk(h2, du, N_CHIPS, "g_w_up", tk=h2.shape[0]))
    (dmo, dmerged, g_post_mix), got = _postnorm_bwd(dx1, mo, p["post_mix_norm"], W_out, "post_mix_norm_bwd",
                                                    plans=ex.carry("post_mix_norm_bwd"))
    ex.done("post_mix_norm_bwd", got)
    ex.grad("w_out", _mm_tn(merged, dmo, "g_w_out").reshape(N_CHIPS, D_MODEL // N_CHIPS, D_MODEL))
    (dza, dzh, dg0, dg1, dy, dyh), got = _branch_bwd(dmerged, za, zh, proj, W_a, W_h, "branch_bwd",
                                                     plans=ex.carry("branch_bwd"))
    ex.done("branch_bwd", got)
    ex.grad("w_branch_attn", _mm_tn_blk(y, dza, N_CHIPS, "g_w_branch_attn", together=True))
    ex.grad("w_branch_hgrn", _mm_tn_blk(yh, dzh, N_CHIPS, "g_w_branch_hgrn", together=True))
    dqkv, dbs = [], []
    for g in range(N_GROUPS):
        parts, db, got = _attn_bwd(proj, biases[g], lse, y, dy, g, f"attn_bwd{g}", plans=ex.carry(f"attn_bwd{g}"))
        ex.done(f"attn_bwd{g}", got)
        dqkv += parts
        dbs.append(db)
    g_rel_bias = _relbias_bwd(dbs, "rel_bias_bwd")
    dproj, g_lb_raw, g_hgrn_norm = _hgrn_bwd(proj, p["hgrn_lb_raw"], p["hgrn_norm"], o_h, states, dyh, dqkv,
                                             [dg0, dg1], "hgrn_bwd")
    for piece in W_IN_PIECES:
        g, got = _mm_tn_blk(h1, dproj, N_CHIPS, f"g_{piece}", x_cols=W_IN_ROWS[piece],
                            plans=ex.carry(f"g_{piece}"))
        ex.done(f"g_{piece}", got)
        ex.grad(piece, g)
    dh1, got = _mm_nt_blk(dproj, ex.weight("w_in"), "d_proj_in", plans=ex.carry("d_proj_in"))
    ex.done("d_proj_in", got)
    (grad_x, g_pre_mix), got = _prenorm_bwd(dh1, xs, p["pre_mix_norm"], dx1, "pre_mix_norm_bwd",
                                            plans=ex.carry("pre_mix_norm_bwd"))
    ex.done("pre_mix_norm_bwd", got)
    small = dict(pre_mix_norm=g_pre_mix, rel_bias=g_rel_bias, hgrn_lb_raw=g_lb_raw, hgrn_norm=g_hgrn_norm,
                 post_mix_norm=g_post_mix, pre_ffn_norm=g_pre_ffn, conv_w=g_conv_w, conv_b=g_conv_b,
                 post_ffn_norm=g_post_ffn)
    return loss, grad_x, small


SMALL = ("pre_mix_norm", "rel_bias", "hgrn_lb_raw", "hgrn_norm", "post_mix_norm", "pre_ffn_norm", "conv_w", "conv_b",
         "post_ffn_norm")
BIG = ("w_in", "w_up", "w_down", "w_out", "w_branch_attn", "w_branch_hgrn")
WEIGHTS = ("pre_mix_norm", "w_in", "rel_bias", "hgrn_lb_raw", "hgrn_norm", "w_branch_attn", "w_branch_hgrn", "w_out",
           "post_mix_norm", "pre_ffn_norm", "w_up", "conv_w", "conv_b", "w_down", "post_ffn_norm")
MIXER = ("w_out", "w_branch_attn", "w_branch_hgrn")

ICI_PARTS = {"gather_ici_1of3": (0, 176), "gather_ici_2of3": (176, 176), "gather_ici_3of3": (352, 160)}
SCHEDULE = {
    "proj_in": [("gather_ici_cw", MIXER)],
    "attn_fwd1": [("gather_ici_1of3", ("w_up",))],
    "attn_fwd2": [("gather_pass", MIXER), ("gather_ici_2of3", ("w_up",))],
    "hgrn_fwd": [("gather_ici_3of3", ("w_up",))],
    "branch_fwd": [("gather_pass", ("w_up",))],
    "ffn_up": [("gather_ici", ("w_down",))],
    "conv_gelu_fwd": [("gather_pass", ("w_down",))],
    "conv_gelu_bwd": [("pair", ("w_down",))],
    "conv_input_bwd": [("chip", ("w_down",))],
    "post_mix_norm_bwd": [("pair", ("w_up",))],
    "attn_bwd0": [("chip", ("w_up",)), ("pair", MIXER)],
    "attn_bwd1": [("chip", MIXER)],
    "g_w_in_b": [("pair", ("w_in_a",))],
    "d_proj_in": [("chip", ("w_in_a",)), ("pair", ("w_in_b",))],
    "pre_mix_norm_bwd": [("chip", ("w_in_b",))],
}
W_IN_ROWS = dict(w_in_a=(0, 768), w_in_b=(3, 256))
W_IN_PIECES = tuple(W_IN_ROWS)
REDUCED = W_IN_PIECES + BIG[1:]


class _Exchange:
    def __init__(self, place, slots, conv_w_shard):
        self.place, self.slots, self.conv_w_shard = place, dict(slots), conv_w_shard
        self.conv_w = None
        self.g, self.from_sibling, self.pair_sums, self.arrived = {}, {}, {}, {}
        self.pending = []

    def weight(self, name):
        if name == "conv_w":
            return self.conv_w
        w = self.slots[name]
        return w.reshape(-1, D_MODEL) if name in ("w_out", "w_down") else w

    def project(self, x, w_norm):
        (plan,) = self.carry("proj_in")
        proj, self.slots["w_in"], h, *got = _proj_gathered(x, w_norm, self.slots["w_in"], self.place, "proj_in",
                                                           plan=plan)
        self.done("proj_in", [got])
        return proj, h

    def grad(self, name, g):
        self.g[name] = g

    def carry(self, point):
        plans = []
        self.pending = SCHEDULE.get(point, [])
        for kind, names in self.pending:
            if kind in ("gather_ici", "gather_ici_cw") or kind in ICI_PARTS:
                wholes = [self.conv_w_shard] if kind == "gather_ici_cw" else []
                plans.append(_gather_ici_plan([self.slots[n] for n in names], wholes, ICI_PARTS.get(kind)))
            elif kind == "gather_pass":
                plans.append(_gather_pass_plan([self.slots[n] for n in names]))
            elif kind == "pair":
                plans.append(_pair_plan([self.g[n] for n in names]))
            else:
                for n in names:
                    self.pair_sums[n] = _pair_sum(self.g[n], self.from_sibling[n], self.place[1:2], f"pair_sum_{n}")
                plans.append(_chip_plan([self.pair_sums[n] for n in names]))
        return plans

    def done(self, point, carried):
        for (kind, names), got in zip(self.pending, carried):
            if kind in ("gather_ici", "gather_ici_cw", "gather_pass") or kind in ICI_PARTS:
                self.slots.update(zip(names, got))
                if kind == "gather_ici_cw":
                    self.conv_w = got[len(names)].transpose(1, 0, 2).reshape(3, 2 * D_FF)
            elif kind == "pair":
                self.from_sibling.update(zip(names, got))
            else:
                self.arrived.update(zip(names, got))

    def reduced_halves(self):
        return [_chip_sum(self.arrived[n], self.pair_sums[n], self.place, f"chip_sum_{n}") for n in REDUCED]


def kernel(x, pre_mix_norm, w_in, rel_bias, hgrn_lb_raw, hgrn_norm, w_branch_attn, w_branch_hgrn, w_out, post_mix_norm, pre_ffn_norm, w_up, conv_w, conv_b, w_down, post_ffn_norm, loss_target, m_pre_mix_norm, m_w_in, m_rel_bias, m_hgrn_lb_raw, m_hgrn_norm, m_w_branch_attn, m_w_branch_hgrn, m_w_out, m_post_mix_norm, m_pre_ffn_norm, m_w_up, m_conv_w, m_conv_b, m_w_down, m_post_ffn_norm, v_pre_mix_norm, v_w_in, v_rel_bias, v_hgrn_lb_raw, v_hgrn_norm, v_w_branch_attn, v_w_branch_hgrn, v_w_out, v_post_mix_norm, v_pre_ffn_norm, v_w_up, v_conv_w, v_conv_b, v_w_down, v_post_ffn_norm):
    w = dict(pre_mix_norm=pre_mix_norm, w_in=w_in, rel_bias=rel_bias, hgrn_lb_raw=hgrn_lb_raw, hgrn_norm=hgrn_norm,
             w_branch_attn=w_branch_attn, w_branch_hgrn=w_branch_hgrn, w_out=w_out, post_mix_norm=post_mix_norm,
             pre_ffn_norm=pre_ffn_norm, w_up=w_up, conv_w=conv_w, conv_b=conv_b, w_down=w_down,
             post_ffn_norm=post_ffn_norm)
    m = dict(pre_mix_norm=m_pre_mix_norm, w_in=m_w_in, rel_bias=m_rel_bias, hgrn_lb_raw=m_hgrn_lb_raw,
             hgrn_norm=m_hgrn_norm, w_branch_attn=m_w_branch_attn, w_branch_hgrn=m_w_branch_hgrn, w_out=m_w_out,
             post_mix_norm=m_post_mix_norm, pre_ffn_norm=m_pre_ffn_norm, w_up=m_w_up, conv_w=m_conv_w,
             conv_b=m_conv_b, w_down=m_w_down, post_ffn_norm=m_post_ffn_norm)
    v = dict(pre_mix_norm=v_pre_mix_norm, w_in=v_w_in, rel_bias=v_rel_bias, hgrn_lb_raw=v_hgrn_lb_raw,
             hgrn_norm=v_hgrn_norm, w_branch_attn=v_w_branch_attn, w_branch_hgrn=v_w_branch_hgrn, w_out=v_w_out,
             post_mix_norm=v_post_mix_norm, pre_ffn_norm=v_pre_ffn_norm, w_up=v_w_up, conv_w=v_conv_w,
             conv_b=v_conv_b, w_down=v_w_down, post_ffn_norm=v_post_ffn_norm)
    shard2d = {n: (w[n][0] if w[n].ndim == 3 else w[n]) for n in WEIGHTS}
    chip = 2 * lax.axis_index("x") + lax.axis_index("y")
    core = lax.axis_index("c")

    place = jnp.stack([chip, core]).astype(jnp.int32)
    slots = {n: _cast_into_slot(shard2d[n], place, f"cast_{n}") for n in BIG}
    ex = _Exchange(place, slots, shard2d["conv_w"])
    loss, grad_x, small = _local_step(x[0], loss_target[0], {n: w[n] for n in SMALL if n != "conv_w"}, ex)

    flat = [small[n].reshape(-1) for n in SMALL] + [loss.reshape(-1)]
    sizes = [t.shape[0] for t in flat]
    summed, wholes = _all_sum(jnp.concatenate(flat).reshape(-1, LANES), ex.reduced_halves(), "sum_small")
    summed = summed.reshape(-1)
    offs = [sum(sizes[:i]) for i in range(len(sizes))]
    grads = {}
    for n, o, sz in zip(SMALL, offs, sizes):
        grads[n] = summed[o:o + sz].reshape(small[n].shape)
    loss_total = summed[offs[-1]]
    cw = 2 * D_FF // N_CHIPS
    grads["conv_w"] = lax.dynamic_slice(grads["conv_w"], (0, chip * cw), (3, cw))

    big = dict(zip(REDUCED, wholes))
    big["w_in"] = jnp.concatenate([big.pop(n) for n in W_IN_PIECES], axis=0)
    grads.update(big)

    m2d = {n: m[n].reshape(shard2d[n].shape) for n in WEIGHTS}
    v2d = {n: v[n].reshape(shard2d[n].shape) for n in WEIGHTS}
    updated = dict(zip(SMALL, _adamw_small([shard2d[n] for n in SMALL], [grads[n] for n in SMALL],
                                           [m2d[n] for n in SMALL], [v2d[n] for n in SMALL], "adamw_small")))
    for n in BIG:
        updated[n] = _adamw(shard2d[n], grads[n], m2d[n], v2d[n], f"adamw_{n}")
    out_g, out_d, out_m, out_v = [], [], [], []
    for n in WEIGHTS:
        d2, m2, v2 = updated[n]
        shape = w[n].shape
        out_g.append(grads[n].reshape(shape))
        out_d.append(d2.reshape(shape))
        out_m.append(m2.reshape(shape))
        out_v.append(v2.reshape(shape))
    return (loss_total, grad_x[None], *out_g, *out_d, *out_m, *out_v)
```

```python
import functools
import math

import jax
import jax.numpy as jnp
from jax import lax
from jax.experimental import pallas as pl
from jax.experimental.pallas import tpu as pltpu

F32 = jnp.float32
BF16 = jnp.bfloat16
MESH = pl.DeviceIdType.MESH

D_MODEL = 1024
N_GROUPS = 3
DILATIONS = (1, 4, 16)
HEADS = 8
HEAD_DIM = 64
GROUP_W = HEADS * HEAD_DIM
QKV_W = N_GROUPS * 3 * GROUP_W
BLK = 128
NEG_INF = -1e30
NUM_BUCKETS = 32
MAX_EXACT = 16
MAX_DISTANCE = 2048
HG_HEADS = 4
HG_DK = 128
HG_W = HG_HEADS * HG_DK
HG_CHUNK = 32
HG_TILE = 256
IN_W = QKV_W + 4 * HG_W + 2 * D_MODEL
D_FF = 2816
EPS = 1e-6
N_CHIPS = 4
N_DEV = 8
LANES = 128
SUBLANES = 8

ADAM_LR, ADAM_B1, ADAM_B2, ADAM_EPS, ADAM_WD, ADAM_STEP = 0.001, 0.9, 0.999, 1e-08, 0.01, 10

VMEM_LIMIT = 56 * 1024 * 1024


def _cp(n_axes):
    return pltpu.CompilerParams(dimension_semantics=("arbitrary",) * n_axes, vmem_limit_bytes=VMEM_LIMIT)


def _sds(shape, dtype):
    return jax.ShapeDtypeStruct(tuple(shape), dtype)


def _sigmoid(v):
    return 1.0 / (1.0 + jnp.exp(-v))


def _bf(v):
    return v.astype(BF16)


def _dot(a, b, dims):
    return lax.dot_general(a, b, (dims, ((), ())), preferred_element_type=F32)


NN = ((1,), (0,))
NT = ((1,), (1,))
TN = ((0,), (0,))

ANY = pl.BlockSpec(memory_space=pl.ANY)


class _Plan:
    def __init__(self, copies, n_sems, ins=(), inouts=(), outs=()):
        self.copies, self.n_sems = copies, n_sems
        self.ins, self.inouts, self.outs = list(ins), list(inouts), list(outs)


def _call(body, plans=None, *, name, grid, in_specs, out_specs, out_shape, args, scratch_shapes=()):
    plans = list(plans or ())
    in_specs, out_specs, out_shape = list(in_specs), list(out_specs), list(out_shape)
    scratch_shapes = list(scratch_shapes)
    n_in, n_out, n_scr = len(in_specs), len(out_specs), len(scratch_shapes)
    x_in, x_out, aliases, spans = [], [], {}, []
    for p in plans:
        i0, o0 = len(x_in), len(x_out)
        x_in += p.ins
        for a in p.inouts:
            aliases[n_in + len(x_in)] = n_out + len(x_out)
            x_in.append(a)
            x_out.append(_sds(a.shape, a.dtype))
        x_out += p.outs
        spans.append((i0, len(p.ins), o0, len(p.inouts), len(p.outs)))
    sems = [pltpu.SemaphoreType.DMA((p.n_sems,)) for p in plans for _ in range(3)]

    def wrapped(*refs):
        xi = refs[n_in:n_in + len(x_in)]
        base = n_in + len(x_in)
        xo = refs[base + n_out:base + n_out + len(x_out)]
        sbase = base + n_out + len(x_out)
        xs = refs[sbase + n_scr:]
        ids = [pl.program_id(k) for k in range(len(grid))]
        first = functools.reduce(jnp.logical_and, [i == 0 for i in ids])
        last = functools.reduce(jnp.logical_and, [i == g - 1 for i, g in zip(ids, grid)])

        def descriptors(k):
            i0, ni, o0, nio, no = spans[k]
            return plans[k].copies(xi[i0:i0 + ni], xo[o0:o0 + nio], xo[o0 + nio:o0 + nio + no], *xs[3 * k:3 * k + 3])

        @pl.when(first)
        def _():
            for k in range(len(plans)):
                sends, _, local = descriptors(k)
                for cp in (*sends, *local):
                    cp.start()

        body(*refs[:n_in], *refs[base:base + n_out], *refs[sbase:sbase + n_scr])

        @pl.when(last)
        def _():
            for k in range(len(plans)):
                sends, recvs, local = descriptors(k)
                for cp in recvs:
                    cp.wait_recv()
                for cp in sends:
                    cp.wait_send()
                for cp in local:
                    cp.wait()

    res = pl.pallas_call(
        wrapped if plans else body, name=name, grid=grid, in_specs=in_specs + [ANY] * len(x_in),
        out_specs=out_specs + [ANY] * len(x_out), out_shape=out_shape + x_out, input_output_aliases=aliases,
        scratch_shapes=scratch_shapes + sems, compiler_params=_cp(len(grid)))(*args, *x_in)
    res = list(res)
    carried = [res[n_out + o0:n_out + o0 + nio + no] for (_, _, o0, nio, no) in spans]
    return res[:n_out], carried


def _mm_nn_blk(a, wg, name, tm=512, plans=None):
    M, K = a.shape
    nb, _, Nb = wg.shape

    def body(a_ref, w_ref, o_ref):
        o_ref[...] = _dot(_bf(a_ref[...]), w_ref[...], NN)

    (out,), carried = _call(
        body, plans, name=name, grid=(nb, M // tm),
        in_specs=[pl.BlockSpec((tm, K), lambda j, i: (i, 0)), pl.BlockSpec((None, K, Nb), lambda j, i: (j, 0, 0))],
        out_specs=[pl.BlockSpec((tm, Nb), lambda j, i: (i, j))],
        out_shape=[_sds((M, nb * Nb), F32)], args=(a, wg))
    return out if plans is None else (out, carried)


def _mm_nt_blk(dy, wg, name, tm=1024, plans=None):
    M = dy.shape[0]
    nb, K, Nb = wg.shape

    def body(dy_ref, w_ref, o_ref):
        j = pl.program_id(1)
        r = _dot(_bf(dy_ref[...]), w_ref[...], NT)

        @pl.when(j == 0)
        def _():
            o_ref[...] = r

        @pl.when(j > 0)
        def _():
            o_ref[...] += r

    (out,), carried = _call(
        body, plans, name=name, grid=(M // tm, nb),
        in_specs=[pl.BlockSpec((tm, Nb), lambda i, j: (i, j)), pl.BlockSpec((None, K, Nb), lambda i, j: (j, 0, 0))],
        out_specs=[pl.BlockSpec((tm, K), lambda i, j: (i, 0))],
        out_shape=[_sds((M, K), F32)], args=(dy, wg))
    return out if plans is None else (out, carried)


def _mm_nt_prenorm_bwd(dy, wg, xin, w, dres, name, tm=1024):
    M = dy.shape[0]
    nb, K, Nb = wg.shape

    def body(dy_ref, w_ref, x_ref, wn_ref, dres_ref, dx_ref, dw_ref, acc):
        i, j = pl.program_id(0), pl.program_id(1)
        r = _dot(_bf(dy_ref[...]), w_ref[...], NT)

        @pl.when(j == 0)
        def _():
            acc[...] = r

        @pl.when(j > 0)
        def _():
            acc[...] += r

        @pl.when(j == nb - 1)
        def _():
            wn = wn_ref[...]

            def strip(s, part):
                rows = pl.ds(pl.multiple_of(s * SUBLANES, SUBLANES), SUBLANES)
                xv = x_ref[rows, :]
                rinv = _rinv(xv)
                xhat = xv * rinv
                dh = acc[rows, :]
                dx_ref[rows, :] = dres_ref[rows, :] + _norm_bwd(dh, xhat, rinv, wn)
                return part + dh * xhat

            part = lax.fori_loop(0, tm // SUBLANES, strip, jnp.zeros((SUBLANES, K), F32))
            part = jnp.sum(part, axis=0, keepdims=True)

            @pl.when(i == 0)
            def _():
                dw_ref[...] = part

            @pl.when(i > 0)
            def _():
                dw_ref[...] += part

    row = pl.BlockSpec((tm, K), lambda i, j: (i, 0))
    vec = pl.BlockSpec((1, K), lambda i, j: (0, 0))
    return pl.pallas_call(
        body, name=name, grid=(M // tm, nb),
        in_specs=[pl.BlockSpec((tm, Nb), lambda i, j: (i, j)), pl.BlockSpec((None, K, Nb), lambda i, j: (j, 0, 0)),
                  row, vec, row],
        out_specs=[row, vec], out_shape=[_sds((M, K), F32), _sds((1, K), F32)],
        scratch_shapes=[pltpu.VMEM((tm, K), F32)], compiler_params=_cp(2))(dy, wg, xin, w, dres)


def _mm_tn_blk(x, dy, nb, name, tk=2048, x_cols=None, plans=None, together=False):
    T, Mx = x.shape
    xk, Mx = (0, Mx) if x_cols is None else x_cols
    Nb = dy.shape[1] // nb
    nj = nb if together else 1

    def body(x_ref, dy_ref, o_ref):
        t = pl.program_id(1)
        r = _dot(_bf(x_ref[...]), _bf(dy_ref[...]), TN)
        for j in range(nj):
            rj = r[:, j * Nb:(j + 1) * Nb]

            @pl.when(t == 0)
            def _():
                o_ref[j] = rj

            @pl.when(t > 0)
            def _():
                o_ref[j] += rj

    (out,), carried = _call(
        body, plans, name=name, grid=(nb // nj, T // tk),
        in_specs=[pl.BlockSpec((tk, Mx), lambda j, t: (t, xk)), pl.BlockSpec((tk, nj * Nb), lambda j, t: (t, j))],
        out_specs=[pl.BlockSpec((nj, Mx, Nb), lambda j, t: (j, 0, 0))],
        out_shape=[_sds((nb, Mx, Nb), F32)], args=(x, dy))
    return out if plans is None else (out, carried)


def _mm_tn(x, dy, name, tk=1024):
    T, Mx = x.shape
    N = dy.shape[1]

    def body(x_ref, dy_ref, o_ref):
        t = pl.program_id(0)
        r = _dot(_bf(x_ref[...]), _bf(dy_ref[...]), TN)

        @pl.when(t == 0)
        def _():
            o_ref[...] = r

        @pl.when(t > 0)
        def _():
            o_ref[...] += r

    return pl.pallas_call(
        body, name=name, grid=(T // tk,),
        in_specs=[pl.BlockSpec((tk, Mx), lambda t: (t, 0)), pl.BlockSpec((tk, N), lambda t: (t, 0))],
        out_specs=pl.BlockSpec((Mx, N), lambda t: (0, 0)),
        out_shape=_sds((Mx, N), F32), compiler_params=_cp(1))(x, dy)


def _tile(arr, bw, col=lambda c: 0):
    return ("tile", arr, bw, col)


def _full(arr):
    return ("full", arr)


def _out_tile(width, dtype, bw, col=lambda c: 0):
    return ("tile", width, dtype, bw, col)


def _out_acc(rows, width, bw, col=lambda c: 0):
    return ("acc", rows, width, bw, col)


def _rows_call(name, body, n_rows, tm, ncol, ins, outs, plans=None):
    in_specs, args = [], []
    for e in ins:
        if e[0] == "tile":
            _, arr, bw, col = e
            in_specs.append(pl.BlockSpec((tm, bw), functools.partial(lambda c, i, col: (i, col(c)), col=col)))
        else:
            arr = e[1]
            in_specs.append(pl.BlockSpec(arr.shape, functools.partial(lambda c, i, nd: (0,) * nd, nd=arr.ndim)))
        args.append(arr)
    out_specs, out_shape = [], []
    for e in outs:
        if e[0] == "tile":
            _, width, dtype, bw, col = e
            out_specs.append(pl.BlockSpec((tm, bw), functools.partial(lambda c, i, col: (i, col(c)), col=col)))
            out_shape.append(_sds((n_rows, width), dtype))
        else:
            _, rows, width, bw, col = e
            out_specs.append(pl.BlockSpec((rows, bw), functools.partial(lambda c, i, col: (0, col(c)), col=col)))
            out_shape.append(_sds((rows, width), F32))
    out, carried = _call(body, plans, name=name, grid=(ncol, n_rows // tm), in_specs=in_specs, out_specs=out_specs,
                         out_shape=out_shape, args=args)
    return out if plans is None else (out, carried)


def _acc(ref, val):
    i = pl.program_id(1)

    @pl.when(i == 0)
    def _():
        ref[...] = val

    @pl.when(i > 0)
    def _():
        ref[...] += val


def _rinv(z):
    return lax.rsqrt(jnp.mean(z * z, axis=-1, keepdims=True) + EPS)


def _norm_bwd(dy, zhat, r, w):
    dyw = dy * w
    return r * (dyw - zhat * jnp.mean(dyw * zhat, axis=-1, keepdims=True))


def _prenorm_bwd(dh, xin, w, dres, name, plans=None):
    def body(dh_ref, x_ref, w_ref, dres_ref, dx_ref, dw_ref):
        xv = x_ref[...]
        r = _rinv(xv)
        xhat = xv * r
        dhv = dh_ref[...]
        dx_ref[...] = dres_ref[...] + _norm_bwd(dhv, xhat, r, w_ref[...])
        _acc(dw_ref, jnp.sum(dhv * xhat, axis=0, keepdims=True))

    return _rows_call(name, body, xin.shape[0], 512, 1,
                      [_tile(dh, D_MODEL), _tile(xin, D_MODEL), _full(w), _tile(dres, D_MODEL)],
                      [_out_tile(D_MODEL, F32, D_MODEL), _out_acc(1, D_MODEL, D_MODEL)], plans)


def _postnorm_bwd(dout, z, w, w_mat, name, plans=None):
    def body(do_ref, z_ref, w_ref, wm_ref, dz_ref, dm_ref, dw_ref):
        zv = z_ref[...]
        r = _rinv(zv)
        zhat = zv * r
        dov = do_ref[...]
        dz = _bf(_norm_bwd(dov, zhat, r, w_ref[...]))
        dz_ref[...] = dz
        dm_ref[...] = _dot(dz, wm_ref[...], NT)
        _acc(dw_ref, jnp.sum(dov * zhat, axis=0, keepdims=True))

    return _rows_call(name, body, z.shape[0], 512, 1,
                      [_tile(dout, D_MODEL), _tile(z, D_MODEL), _full(w), _full(w_mat)],
                      [_out_tile(D_MODEL, BF16, D_MODEL), _out_tile(D_MODEL, F32, D_MODEL),
                       _out_acc(1, D_MODEL, D_MODEL)], plans)


def _t5_bucket(dist):
    n = jnp.maximum(dist, 0)
    nf = jnp.maximum(n, 1).astype(F32)
    large = MAX_EXACT + (jnp.log(nf / MAX_EXACT) / math.log(MAX_DISTANCE / MAX_EXACT)
                         * (NUM_BUCKETS - MAX_EXACT)).astype(jnp.int32)
    large = jnp.minimum(large, NUM_BUCKETS - 1)
    return jnp.where(n < MAX_EXACT, n, large)


def _band_rel():
    return jnp.arange(BLK)[:, None] + BLK - jnp.arange(2 * BLK)[None, :]


def _band_valid():
    rel = _band_rel()
    window = (rel >= 0) & (rel <= BLK)
    first = window & (jnp.arange(2 * BLK)[None, :] >= BLK)
    return jnp.stack([first, window]).astype(F32).reshape(2, 1, BAND)


RES_UNROLL = 8
PAIR = LANES // HEAD_DIM


def _pair_lanes():
    first = lax.broadcasted_iota(jnp.int32, (1, LANES), 1) < HEAD_DIM
    return first, jnp.logical_not(first)


def _heads_per_step(d):
    return HEADS if d == 1 else LANES // HEAD_DIM


def _sub_rows(r, d):
    return pl.ds(r, BLK, stride=d) if d > 1 else pl.ds(0, BLK)


def _for_residues(d, fn):
    if d <= RES_UNROLL:
        for r in range(d):
            fn(r)
    else:
        def group(i, carry):
            for k in range(RES_UNROLL):
                fn(i * RES_UNROLL + k)
            return carry

        lax.fori_loop(0, d // RES_UNROLL, group, 0)


def _attn_specs(d, g, qblock):
    cw = _heads_per_step(d) * HEAD_DIM

    def col(part, hp):
        return (g * 3 + part) * (GROUP_W // cw) + hp

    def cur(part):
        return pl.BlockSpec((d * BLK, cw), lambda hp, n: (qblock(n), col(part, hp)))

    def prev(part):
        return pl.BlockSpec((d * BLK, cw), lambda hp, n: (jnp.maximum(qblock(n) - 1, 0), col(part, hp)))

    return cur, prev


def _attn_fwd(proj, bias, g, name, plans=None):
    S = proj.shape[0]
    d = DILATIONS[g]
    NB = S // (d * BLK)
    hps = _heads_per_step(d)

    def body(q_ref, kp_ref, kc_ref, vp_ref, vc_ref, b_ref, o_ref, lse_ref):
        hp = pl.program_id(0)
        later = jnp.minimum(pl.program_id(1), 1)

        def residue(r):
            rows = _sub_rows(r, d)
            q2 = q_ref[rows, :]
            k2 = jnp.concatenate([kp_ref[rows, :], kc_ref[rows, :]], axis=0)
            v2 = jnp.concatenate([vp_ref[rows, :], vc_ref[rows, :]], axis=0)
            outs, lses = [], []
            for pp in range(hps // PAIR):
                ps = slice(pp * LANES, (pp + 1) * LANES)
                qp, kp, vp = _bf(q2[:, ps]), _bf(k2[:, ps]), _bf(v2[:, ps])
                o_h, lse_h = [], []
                for hh, own in enumerate(_pair_lanes()):
                    s = _dot(qp, jnp.where(own, kp, 0), NT) * (HEAD_DIM ** -0.5) + b_ref[later, hp * hps + pp * PAIR + hh]
                    m = jnp.max(s, axis=-1, keepdims=True)
                    p = jnp.exp(s - m)
                    l = jnp.sum(p, axis=-1, keepdims=True)
                    o_h.append(_dot(_bf(p), vp, NN) / l)
                    lse_h.append(m + jnp.log(l))
                first = _pair_lanes()[0]
                outs.append(jnp.where(first, o_h[0], o_h[1]))
                lses.append(jnp.where(first, lse_h[0], lse_h[1]))
            o_ref[rows, :] = outs[0] if len(outs) == 1 else jnp.concatenate(outs, axis=1)
            lse_ref[rows, :] = lses[0] if len(lses) == 1 else jnp.concatenate(lses, axis=1)

        _for_residues(d, residue)

    cur, prev = _attn_specs(d, g, lambda n: n)
    out = pl.BlockSpec((d * BLK, hps * HEAD_DIM), lambda hp, n: (n, hp))
    res, carried = _call(
        body, plans, name=name, grid=(HEADS // hps, NB),
        in_specs=[cur(0), prev(1), cur(1), prev(2), cur(2),
                  pl.BlockSpec((2, HEADS, BLK, 2 * BLK), lambda hp, n: (0, 0, 0, 0))],
        out_specs=[out, out], out_shape=[_sds((S, GROUP_W), F32)] * 2,
        args=(proj, proj, proj, proj, proj, bias))
    return res if plans is None else (res, carried)


def _attn_bwd(proj, bias, lse, y, dy, g, name, plans=None):
    S = proj.shape[0]
    d = DILATIONS[g]
    NB = S // (d * BLK)
    hps = _heads_per_step(d)

    def body(q_ref, kp_ref, kc_ref, vp_ref, vc_ref, b_ref, l_ref, y_ref, dy_ref,
             dq_ref, dk_ref, dv_ref, db_ref, ck_ref, cv_ref):
        hp, n = pl.program_id(0), pl.program_id(1)

        @pl.when((hp == 0) & (n == 0))
        def _():
            db_ref[...] = jnp.zeros_like(db_ref)

        @pl.when(n == 0)
        def _():
            ck_ref[...] = jnp.zeros_like(ck_ref)
            cv_ref[...] = jnp.zeros_like(cv_ref)

        @pl.when(n < NB)
        def _():
            later = jnp.minimum(n, 1)

            def residue(r):
                rows = _sub_rows(r, d)
                q2 = q_ref[rows, :]
                k2 = jnp.concatenate([kp_ref[rows, :], kc_ref[rows, :]], axis=0)
                v2 = jnp.concatenate([vp_ref[rows, :], vc_ref[rows, :]], axis=0)
                l2, y2, dy2 = l_ref[rows, :], y_ref[rows, :], dy_ref[rows, :]
                dqs, dks, dvs = [], [], []
                for pp in range(hps // PAIR):
                    ps = slice(pp * LANES, (pp + 1) * LANES)
                    qp, kp, vp = _bf(q2[:, ps]), _bf(k2[:, ps]), _bf(v2[:, ps])
                    dyp, yp = dy2[:, ps], y2[:, ps]
                    dq_h, dk_h, dv_h = [], [], []
                    for hh, own in enumerate(_pair_lanes()):
                        head = hp * hps + pp * PAIR + hh
                        s = _dot(qp, jnp.where(own, kp, 0), NT) * (HEAD_DIM ** -0.5) + b_ref[later, head]
                        p = jnp.exp(s - l2[:, pp * LANES + hh * HEAD_DIM:pp * LANES + hh * HEAD_DIM + 1])
                        dyh = jnp.where(own, dyp, 0.0)
                        delta = jnp.sum(dyh * yp, axis=-1, keepdims=True)
                        ds = p * (_dot(_bf(dyh), vp, NT) - delta)
                        db_ref[head] += ds
                        dsb = _bf(ds * (HEAD_DIM ** -0.5))
                        dq_h.append(_dot(dsb, kp, NN))
                        dk_h.append(_dot(dsb, qp, TN))
                        dv_h.append(_dot(_bf(p), _bf(dyp), TN))
                    first = _pair_lanes()[0]
                    dqs.append(jnp.where(first, dq_h[0], dq_h[1]))
                    dks.append(jnp.where(first, dk_h[0], dk_h[1]))
                    dvs.append(jnp.where(first, dv_h[0], dv_h[1]))
                dkb = dks[0] if len(dks) == 1 else jnp.concatenate(dks, axis=1)
                dvb = dvs[0] if len(dvs) == 1 else jnp.concatenate(dvs, axis=1)
                dq_ref[rows, :] = dqs[0] if len(dqs) == 1 else jnp.concatenate(dqs, axis=1)
                dk_ref[rows, :] = ck_ref[rows, :] + dkb[:BLK]
                dv_ref[rows, :] = cv_ref[rows, :] + dvb[:BLK]
                ck_ref[rows, :] = dkb[BLK:]
                cv_ref[rows, :] = dvb[BLK:]

            _for_residues(d, residue)

        @pl.when(n == NB)
        def _():
            dk_ref[...] = ck_ref[...]
            dv_ref[...] = cv_ref[...]

    def qn(n):
        return jnp.minimum(n, NB - 1)

    cur, prev = _attn_specs(d, g, qn)
    cw = hps * HEAD_DIM
    row = pl.BlockSpec((d * BLK, cw), lambda hp, n: (qn(n), hp))
    done = pl.BlockSpec((d * BLK, cw), lambda hp, n: (jnp.maximum(n - 1, 0), hp))
    (dq, dk, dv, db), carried = _call(
        body, plans, name=name, grid=(HEADS // hps, NB + 1),
        in_specs=[cur(0), prev(1), cur(1), prev(2), cur(2),
                  pl.BlockSpec((2, HEADS, BLK, 2 * BLK), lambda hp, n: (0, 0, 0, 0)), row, row, row],
        out_specs=[row, done, done, pl.BlockSpec((HEADS, BLK, 2 * BLK), lambda hp, n: (0, 0, 0))],
        out_shape=[_sds((S, GROUP_W), F32)] * 3 + [_sds((HEADS, BLK, 2 * BLK), F32)],
        scratch_shapes=[pltpu.VMEM((d * BLK, cw), F32)] * 2,
        args=(proj, proj, proj, proj, proj, bias, lse, y, dy))
    return ([dq, dk, dv], db) if plans is None else ([dq, dk, dv], db, carried)


BAND = BLK * 2 * BLK


def _bucket_onehot():
    buckets = jnp.stack([_t5_bucket(_band_rel() * d) for d in DILATIONS]).reshape(N_GROUPS, 1, BAND)
    return (buckets == jnp.arange(NUM_BUCKETS).reshape(1, NUM_BUCKETS, 1)).astype(F32)


def _relbias_fwd(rel_bias, name):
    table = rel_bias.reshape(NUM_BUCKETS, N_GROUPS, HEADS).transpose(1, 0, 2)

    def body(t_ref, oh_ref, valid_ref, o_ref):
        bias = lax.dot_general(t_ref[...], oh_ref[...], (TN, ((), ())), preferred_element_type=F32,
                               precision=lax.Precision.HIGHEST)
        for k in range(2):
            o_ref[k] = jnp.where(valid_ref[k] > 0.5, bias, NEG_INF)

    out = pl.pallas_call(
        body, name=name, grid=(N_GROUPS,),
        in_specs=[pl.BlockSpec((None, NUM_BUCKETS, HEADS), lambda g: (g, 0, 0)),
                  pl.BlockSpec((None, NUM_BUCKETS, BAND), lambda g: (g, 0, 0)),
                  pl.BlockSpec((2, 1, BAND), lambda g: (0, 0, 0))],
        out_specs=pl.BlockSpec((None, 2, HEADS, BAND), lambda g: (g, 0, 0, 0)),
        out_shape=_sds((N_GROUPS, 2, HEADS, BAND), F32), compiler_params=_cp(1))(table, _bucket_onehot(), _band_valid())
    return out.reshape(N_GROUPS, 2, HEADS, BLK, 2 * BLK)


def _relbias_bwd(dbs, name):
    band = BAND
    onehot = _bucket_onehot()
    dbf = jnp.stack([db.reshape(HEADS, band) for db in dbs])

    def body(oh_ref, db_ref, o_ref):
        o_ref[...] = lax.dot_general(oh_ref[...], db_ref[...], (NT, ((), ())), preferred_element_type=F32,
                                     precision=lax.Precision.HIGHEST)

    out = pl.pallas_call(
        body, name=name, grid=(N_GROUPS,),
        in_specs=[pl.BlockSpec((None, NUM_BUCKETS, band), lambda g: (g, 0, 0)),
                  pl.BlockSpec((None, HEADS, band), lambda g: (g, 0, 0))],
        out_specs=pl.BlockSpec((None, NUM_BUCKETS, HEADS), lambda g: (g, 0, 0)),
        out_shape=_sds((N_GROUPS, NUM_BUCKETS, HEADS), F32), compiler_params=_cp(1))(onehot, dbf)
    return out.transpose(1, 0, 2).reshape(NUM_BUCKETS, N_GROUPS * HEADS)


def _chunk_pos(shape):
    return lax.broadcasted_iota(jnp.int32, shape, 0) % HG_CHUNK


def _chunk_cumsum(v):
    pos = _chunk_pos(v.shape)
    s = 1
    while s < HG_CHUNK:
        v = v + jnp.where(pos >= s, pltpu.roll(v, s, 0), 0.0)
        s *= 2
    return v


def _chunk_rev_cumsum(v):
    pos = _chunk_pos(v.shape)
    n = v.shape[0]
    s = 1
    while s < HG_CHUNK:
        v = v + jnp.where(pos < HG_CHUNK - s, pltpu.roll(v, n - s, 0), 0.0)
        s *= 2
    return v


def _lower_bound(raw):
    a0, a1 = raw[0:1], raw[1:2]
    m = jnp.maximum(a0, a1)
    e0, e1 = jnp.exp(a0 - m), jnp.exp(a1 - m)
    return e0 / (e0 + e1)


def _hg_gates(qr, fr, lb):
    sf = _sigmoid(fr)
    f = lb + (1.0 - lb) * sf
    sq = _sigmoid(qr)
    return qr * sq, sq, f, sf


HG_COL0 = QKV_W // HG_W


def _hgrn_fwd(proj, lb_raw, nw, name, plans=None):
    S = proj.shape[0]
    ncs = HG_TILE // HG_CHUNK

    def body(q_ref, f_ref, i_ref, og_ref, lb_ref, nw_ref, y_ref, o_ref, st_ref, state):
        @pl.when(pl.program_id(0) == 0)
        def _():
            state[...] = jnp.zeros_like(state)

        lb = _lower_bound(lb_ref[...])
        q, _, f, _ = _hg_gates(q_ref[...], f_ref[...], lb)
        k = 1.0 - f
        G = _chunk_cumsum(jnp.log(f))
        row = lax.broadcasted_iota(jnp.int32, (HG_CHUNK, HG_CHUNK), 0)
        col = lax.broadcasted_iota(jnp.int32, (HG_CHUNK, HG_CHUNK), 1)
        heads = [slice(h * HG_DK, (h + 1) * HG_DK) for h in range(HG_HEADS)]
        sts = [state[h] for h in range(HG_HEADS)]
        for c in range(ncs):
            cs = slice(c * HG_CHUNK, (c + 1) * HG_CHUNK)
            for h, hs in enumerate(heads):
                Gc = G[cs, hs]
                gl = Gc[HG_CHUNK - 1:HG_CHUNK]
                qt = _bf(q[cs, hs] * jnp.exp(Gc))
                kt = _bf(k[cs, hs] * jnp.exp(-Gc))
                kd = _bf(k[cs, hs] * jnp.exp(gl - Gc))
                v = _bf(i_ref[cs, hs])
                A = jnp.where(row >= col, _dot(qt, kt, NT), 0.0)
                o_ref[cs, hs] = _dot(_bf(A), v, NN) + _dot(qt, _bf(sts[h]), NT)
                st_ref[c, h] = sts[h]
                sts[h] = sts[h] * jnp.exp(gl) + _dot(v, kd, TN)
        for h, hs in enumerate(heads):
            state[h] = sts[h]
            oh = o_ref[:, hs]
            og = og_ref[:, hs]
            y_ref[:, hs] = oh * _rinv(oh) * nw_ref[...] * (og * _sigmoid(og))

    def colspec(j):
        return pl.BlockSpec((HG_TILE, HG_W), lambda i: (i, HG_COL0 + j))

    res, carried = _call(
        body, plans, name=name, grid=(S // HG_TILE,),
        in_specs=[colspec(0), colspec(1), colspec(2), colspec(3),
                  pl.BlockSpec((2, HG_W), lambda i: (0, 0)), pl.BlockSpec((1, HG_DK), lambda i: (0, 0))],
        out_specs=[pl.BlockSpec((HG_TILE, HG_W), lambda i: (i, 0))] * 2
        + [pl.BlockSpec((ncs, HG_HEADS, HG_DK, HG_DK), lambda i: (i, 0, 0, 0))],
        out_shape=[_sds((S, HG_W), F32)] * 2 + [_sds((S // HG_CHUNK, HG_HEADS, HG_DK, HG_DK), F32)],
        scratch_shapes=[pltpu.VMEM((HG_HEADS, HG_DK, HG_DK), F32)],
        args=(proj, proj, proj, proj, lb_raw, nw))
    return res if plans is None else (res, carried)


def _hgrn_bwd(proj, lb_raw, nw, o, states, dy, d_attn, d_gates, name):
    S = proj.shape[0]
    ncs = HG_TILE // HG_CHUNK
    nt = S // HG_TILE
    n_a, n_g = len(d_attn), len(d_gates)
    own = [slice(QKV_W + j * HG_W, QKV_W + (j + 1) * HG_W) for j in range(4)]

    def body(q_ref, f_ref, i_ref, og_ref, lb_ref, nw_ref, o_ref, st_ref, dy_ref, *rest):
        attn_refs, gate_refs = rest[:n_a], rest[n_a:n_a + n_g]
        dp_ref, dlb_ref, dnw_ref, dstate, do_s, dG_s, dgl_s, dk_s, dlb_s = rest[n_a + n_g:]
        dq_ref, df_ref, di_ref, dog_ref = (dp_ref.at[:, cols] for cols in own)
        step = pl.program_id(0)
        for k, a_ref in enumerate(attn_refs):
            dp_ref[:, k * GROUP_W:(k + 1) * GROUP_W] = _bf(a_ref[...])
        for k, g_ref in enumerate(gate_refs):
            dp_ref[:, QKV_W + 4 * HG_W + k * D_MODEL:QKV_W + 4 * HG_W + (k + 1) * D_MODEL] = g_ref[...]

        @pl.when(step == 0)
        def _():
            dstate[...] = jnp.zeros_like(dstate)
            dlb_s[...] = jnp.zeros_like(dlb_s)
            dnw_ref[...] = jnp.zeros_like(dnw_ref)

        lb = _lower_bound(lb_ref[...])
        qr = q_ref[...]
        q, sq, f, sf = _hg_gates(qr, f_ref[...], lb)
        k = 1.0 - f
        G = _chunk_cumsum(jnp.log(f))
        nwv = nw_ref[...]
        row = lax.broadcasted_iota(jnp.int32, (HG_CHUNK, HG_CHUNK), 0)
        col = lax.broadcasted_iota(jnp.int32, (HG_CHUNK, HG_CHUNK), 1)
        for h in range(HG_HEADS):
            hs = slice(h * HG_DK, (h + 1) * HG_DK)
            oh = o_ref[:, hs]
            r = _rinv(oh)
            ohat = oh * r
            og = og_ref[:, hs]
            sg = _sigmoid(og)
            dyh = dy_ref[:, hs]
            don = dyh * (og * sg)
            dog_ref[:, hs] = _bf(dyh * (ohat * nwv) * (sg * (1.0 + og * (1.0 - sg))))
            dnw_ref[...] += jnp.sum(don * ohat, axis=0, keepdims=True)
            do_s[:, hs] = _norm_bwd(don, ohat, r, nwv)
        dsts = [dstate[h] for h in range(HG_HEADS)]
        for c in reversed(range(ncs)):
            cs = slice(c * HG_CHUNK, (c + 1) * HG_CHUNK)
            for h in range(HG_HEADS):
                hs = slice(h * HG_DK, (h + 1) * HG_DK)
                dst = dsts[h]
                Gc = G[cs, hs]
                gl = Gc[HG_CHUNK - 1:HG_CHUNK]
                eG, enG, edG, egl = jnp.exp(Gc), jnp.exp(-Gc), jnp.exp(gl - Gc), jnp.exp(gl)
                qt, kt, kd = q[cs, hs] * eG, k[cs, hs] * enG, k[cs, hs] * edG
                qtb, ktb, kdb = _bf(qt), _bf(kt), _bf(kd)
                v = _bf(i_ref[cs, hs])
                do = _bf(do_s[cs, hs])
                st = st_ref[c, h]
                dstb = _bf(dst)
                A = jnp.where(row >= col, _dot(qtb, ktb, NT), 0.0)
                dA = _bf(jnp.where(row >= col, _dot(do, v, NT), 0.0))
                di_ref[cs, hs] = _bf(_dot(_bf(A), do, TN) + _dot(kdb, dstb, NT))
                dqt = _dot(dA, ktb, NN) + _dot(do, _bf(st), NN)
                dkt = _dot(dA, qtb, TN)
                dkd = _dot(v, dstb, NN)
                dgl = egl * jnp.sum(st * dst, axis=0, keepdims=True) + jnp.sum(dkd * kd, axis=0, keepdims=True)
                dsts[h] = dst * egl + _dot(do, qtb, TN)
                dq_ref[cs, hs] = _bf(dqt * eG * (sq[cs, hs] * (1.0 + qr[cs, hs] * (1.0 - sq[cs, hs]))))
                dk_s[cs, hs] = dkt * enG + dkd * edG
                dG_s[cs, hs] = dqt * qt - dkt * kt - dkd * kd
                dgl_s[cs, hs] = jnp.broadcast_to(dgl, (HG_CHUNK, HG_DK))
        for h in range(HG_HEADS):
            dstate[h] = dsts[h]
        dg = _chunk_rev_cumsum(dG_s[...]) + dgl_s[...]
        dfv = dg / f - dk_s[...]
        df_ref[...] = _bf(dfv * (1.0 - lb) * sf * (1.0 - sf))
        dlb_s[...] += jnp.sum(dfv * (1.0 - sf), axis=0, keepdims=True)

        @pl.when(step == nt - 1)
        def _():
            t = dlb_s[...] * lb * (1.0 - lb)
            dlb_ref[...] = jnp.concatenate([t, -t], axis=0)

    def colspec(j):
        return pl.BlockSpec((HG_TILE, HG_W), lambda i: (nt - 1 - i, HG_COL0 + j))

    def rows(width):
        return pl.BlockSpec((HG_TILE, width), lambda i: (nt - 1 - i, 0))

    tile = rows(HG_W)
    return pl.pallas_call(
        body, name=name, grid=(nt,),
        in_specs=[colspec(0), colspec(1), colspec(2), colspec(3),
                  pl.BlockSpec((2, HG_W), lambda i: (0, 0)), pl.BlockSpec((1, HG_DK), lambda i: (0, 0)),
                  tile, pl.BlockSpec((ncs, HG_HEADS, HG_DK, HG_DK), lambda i: (nt - 1 - i, 0, 0, 0)), tile]
        + [rows(GROUP_W)] * n_a + [rows(D_MODEL)] * n_g,
        out_specs=[rows(IN_W), pl.BlockSpec((2, HG_W), lambda i: (0, 0)), pl.BlockSpec((1, HG_DK), lambda i: (0, 0))],
        out_shape=[_sds((S, IN_W), BF16), _sds((2, HG_W), F32), _sds((1, HG_DK), F32)],
        scratch_shapes=[pltpu.VMEM((HG_HEADS, HG_DK, HG_DK), F32)] + [pltpu.VMEM((HG_TILE, HG_W), F32)] * 4
        + [pltpu.VMEM((1, HG_W), F32)],
        compiler_params=_cp(1))(proj, proj, proj, proj, lb_raw, nw, o, states, dy, *d_attn, *d_gates)


GATE_COL0 = (QKV_W + 4 * HG_W) // GROUP_W
HALF_D = D_MODEL // 2


def _gate_tiles(proj):
    return [_tile(proj, HALF_D, functools.partial(lambda c, k: GATE_COL0 + k, k=k)) for k in range(4)]


def _gates(g_refs):
    s0 = _sigmoid(jnp.concatenate([g_refs[0][...], g_refs[1][...]], axis=1))
    s1 = _sigmoid(jnp.concatenate([g_refs[2][...], g_refs[3][...]], axis=1))
    return s0, s1


def _branch_fwd(os_, lses, yh, proj, w_a, w_h, name, plans=None):
    nb = w_a.shape[0]

    def body(o0, o1, o2, l0, l1, l2, yh_ref, g0a, g0b, g1a, g1b, wa_ref, wh_ref,
             y_ref, lse_ref, za_ref, zh_ref, m_ref):
        a, b, c = l0[...], l1[...], l2[...]
        m = jnp.maximum(jnp.maximum(a, b), c)
        ea, eb, ec = jnp.exp(a - m), jnp.exp(b - m), jnp.exp(c - m)
        den = ea + eb + ec
        y = (ea * o0[...] + eb * o1[...] + ec * o2[...]) / den
        y_ref[...] = y
        lse_ref[...] = m + jnp.log(den)
        yb, yhb = _bf(y), _bf(yh_ref[...])
        za = jnp.concatenate([_dot(yb, wa_ref[j], NN) for j in range(nb)], axis=1)
        zh = jnp.concatenate([_dot(yhb, wh_ref[j], NN) for j in range(nb)], axis=1)
        s0, s1 = _gates((g0a, g0b, g1a, g1b))
        za_ref[...] = za
        zh_ref[...] = zh
        m_ref[...] = _bf(s0 * za + s1 * zh)

    return _rows_call(name, body, yh.shape[0], 512, 1,
                      [*[_tile(t, GROUP_W) for t in (*os_, *lses)], _tile(yh, HG_W), *_gate_tiles(proj),
                       _full(w_a), _full(w_h)],
                      [_out_tile(GROUP_W, F32, GROUP_W)] * 2 + [_out_tile(D_MODEL, F32, D_MODEL)] * 2
                      + [_out_tile(D_MODEL, BF16, D_MODEL)], plans)


def _branch_bwd(dm, za, zh, proj, w_a, w_h, name, plans=None):
    nb, _, Nb = w_a.shape

    def body(dm_ref, za_ref, zh_ref, g0a, g0b, g1a, g1b, wa_ref, wh_ref,
             dza_ref, dzh_ref, dg0_ref, dg1_ref, dy_ref, dyh_ref):
        dmv = dm_ref[...]
        s0, s1 = _gates((g0a, g0b, g1a, g1b))
        dza, dzh = _bf(dmv * s0), _bf(dmv * s1)
        dza_ref[...] = dza
        dzh_ref[...] = dzh
        dg0_ref[...] = _bf(dmv * za_ref[...] * s0 * (1.0 - s0))
        dg1_ref[...] = _bf(dmv * zh_ref[...] * s1 * (1.0 - s1))
        dy_ref[...] = sum(_dot(dza[:, j * Nb:(j + 1) * Nb], wa_ref[j], NT) for j in range(nb))
        dyh_ref[...] = sum(_dot(dzh[:, j * Nb:(j + 1) * Nb], wh_ref[j], NT) for j in range(nb))

    return _rows_call(name, body, za.shape[0], 512, 1,
                      [_tile(dm, D_MODEL), _tile(za, D_MODEL), _tile(zh, D_MODEL), *_gate_tiles(proj),
                       _full(w_a), _full(w_h)],
                      [_out_tile(D_MODEL, BF16, D_MODEL)] * 4 + [_out_tile(GROUP_W, F32, GROUP_W),
                                                                 _out_tile(HG_W, F32, HG_W)], plans)


def _mix_out(merged, w_out, x, w_post, w_pre, name):
    def body(m_ref, wo_ref, x_ref, wp_ref, wf_ref, mo_ref, x1_ref, h2_ref):
        z = _dot(m_ref[...], wo_ref[...], NN)
        mo_ref[...] = z
        x1 = x_ref[...] + z * _rinv(z) * wp_ref[...]
        x1_ref[...] = x1
        h2_ref[...] = _bf(x1 * _rinv(x1) * wf_ref[...])

    return _rows_call(name, body, x.shape[0], 512, 1,
                      [_tile(merged, D_MODEL), _full(w_out), _tile(x, D_MODEL), _full(w_post), _full(w_pre)],
                      [_out_tile(D_MODEL, F32, D_MODEL), _out_tile(D_MODEL, F32, D_MODEL),
                       _out_tile(D_MODEL, BF16, D_MODEL)])


def _loss_head(a, w_down, x1, tgt, w, name):
    def body(a_ref, wd_ref, x1_ref, t_ref, w_ref, dx_ref, df_ref, dw_ref, loss_ref):
        z = _dot(a_ref[...], wd_ref[...], NN)
        r = _rinv(z)
        zhat = z * r
        wv = w_ref[...]
        e = x1_ref[...] + zhat * wv - t_ref[...]
        dx = e * (1.0 / D_MODEL)
        dx_ref[...] = dx
        df_ref[...] = _bf(_norm_bwd(dx, zhat, r, wv))
        _acc(dw_ref, jnp.sum(dx * zhat, axis=0, keepdims=True))
        part = 0.5 * jnp.sum(jnp.sum(e * e, axis=1, keepdims=True), axis=0, keepdims=True) * (1.0 / D_MODEL)
        _acc(loss_ref, jnp.broadcast_to(part, (1, LANES)))

    return _rows_call(name, body, x1.shape[0], 512, 1,
                      [_tile(a, D_FF), _full(w_down), _tile(x1, D_MODEL), _tile(tgt, D_MODEL), _full(w)],
                      [_out_tile(D_MODEL, F32, D_MODEL), _out_tile(D_MODEL, BF16, D_MODEL),
                       _out_acc(1, D_MODEL, D_MODEL), _out_acc(1, LANES, LANES)])


CONV_CB = D_FF // 2
CONV_TM = 512
HALO = 8
SQRT_HALF = 0.7071067811865476
INV_SQRT_2PI = 0.3989422804014327


CONV_RS = 32


def _lane_tiles():
    return [slice(k * LANES, (k + 1) * LANES) for k in range(CONV_CB // LANES)]


def _strip_start(i):
    return pl.multiple_of(i * CONV_RS, CONV_RS)


def _strip_taps(u_ref, halo_ref, r0, cs, first_strip, first_tile):
    if first_strip:
        before = jnp.where(first_tile, 0.0, halo_ref[:, cs])
        blk = jnp.concatenate([before, u_ref[0:CONV_RS, cs]], axis=0)
    else:
        blk = u_ref[pl.ds(pl.multiple_of(r0 - HALO, HALO), CONV_RS + HALO), cs]
    return pltpu.roll(blk, 2, 0)[HALO:], pltpu.roll(blk, 1, 0)[HALO:], blk[HALO:]


def _conv(taps, w_ref, b_ref, cs):
    return b_ref[:, cs] + w_ref[0:1, cs] * taps[0] + w_ref[1:2, cs] * taps[1] + w_ref[2:3, cs] * taps[2]


def _conv_specs(tm):
    nh = tm // HALO
    nc = D_FF // CONV_CB

    def tile(off):
        return pl.BlockSpec((tm, CONV_CB), lambda c, i: (i, off + c))

    def halo(off):
        return pl.BlockSpec((HALO, CONV_CB), lambda c, i: (jnp.maximum(i * nh - 1, 0), off + c))

    def small(rows, off):
        return pl.BlockSpec((rows, CONV_CB), lambda c, i: (0, off + c))

    return nc, tile, halo, small


def _conv_gelu_fwd(u, cw, cb, name, plans=None):
    S = u.shape[0]
    tm = CONV_TM
    nc, tile, halo, small = _conv_specs(tm)

    def body(ug, hg, uv, hv, wg, wv, bg, bv, a_ref):
        first_tile = pl.program_id(1) == 0

        def strip(r0, first_strip):
            for cs in _lane_tiles():
                cg = _conv(_strip_taps(ug, hg, r0, cs, first_strip, first_tile), wg, bg, cs)
                cv = _conv(_strip_taps(uv, hv, r0, cs, first_strip, first_tile), wv, bv, cs)
                a_ref[pl.ds(r0, CONV_RS), cs] = _bf(0.5 * cg * (1.0 + lax.erf(cg * SQRT_HALF)) * cv)

        strip(0, True)
        lax.fori_loop(1, tm // CONV_RS, lambda k, c: (strip(_strip_start(k), False), c)[1], 0)

    (a,), carried = _call(
        body, plans, name=name, grid=(nc, S // tm),
        in_specs=[tile(0), halo(0), tile(nc), halo(nc), small(3, 0), small(3, nc), small(1, 0), small(1, nc)],
        out_specs=[tile(0)], out_shape=[_sds((S, D_FF), BF16)], args=(u, u, u, u, cw, cw, cb, cb))
    return a if plans is None else (a, carried)


def _conv_gelu_bwd(u, dff, w_down, cw, cb, name, plans=None):
    S = u.shape[0]
    tm = CONV_TM
    nt = S // tm
    nc, tile, halo, small = _conv_specs(tm)

    def body(ug, hg, uv, hv, wg, wv, bg, bv, dff_ref, wd_ref, dcg_ref, dcv_ref, dwg_ref, dwv_ref, dbg_ref, dbv_ref,
             acc, da_ref):
        i = pl.program_id(1)
        first_tile = i == 0
        da_ref[...] = _dot(dff_ref[...], wd_ref[...], NT)

        @pl.when(first_tile)
        def _():
            acc[...] = jnp.zeros_like(acc)

        def strip(r0, first_strip):
            rows = pl.ds(r0, CONV_RS)
            for cs in _lane_tiles():
                tg = _strip_taps(ug, hg, r0, cs, first_strip, first_tile)
                tv = _strip_taps(uv, hv, r0, cs, first_strip, first_tile)
                cg = _conv(tg, wg, bg, cs)
                cv = _conv(tv, wv, bv, cs)
                phi = 0.5 * (1.0 + lax.erf(cg * SQRT_HALF))
                dav = da_ref[rows, cs]
                dcg = dav * cv * (phi + cg * jnp.exp(-0.5 * cg * cg) * INV_SQRT_2PI)
                dcv = dav * (cg * phi)
                dcg_ref[rows, cs] = dcg
                dcv_ref[rows, cs] = dcv
                for half, (dc, taps) in enumerate(((dcg, tg), (dcv, tv))):
                    for j in range(3):
                        acc[4 * half + j, :, cs] += dc * taps[j]
                    acc[4 * half + 3, :, cs] += dc

        strip(0, True)
        lax.fori_loop(1, tm // CONV_RS, lambda k, c: (strip(_strip_start(k), False), c)[1], 0)

        @pl.when(i == nt - 1)
        def _():
            for half, (dw_ref, db_ref) in enumerate(((dwg_ref, dbg_ref), (dwv_ref, dbv_ref))):
                for j in range(3):
                    dw_ref[j:j + 1, :] = jnp.sum(acc[4 * half + j], axis=0, keepdims=True)
                db_ref[...] = jnp.sum(acc[4 * half + 3], axis=0, keepdims=True)

    res, carried = _call(
        body, plans, name=name, grid=(nc, nt),
        in_specs=[tile(0), halo(0), tile(nc), halo(nc), small(3, 0), small(3, nc), small(1, 0), small(1, nc),
                  pl.BlockSpec((tm, D_MODEL), lambda c, i: (i, 0)), pl.BlockSpec((CONV_CB, D_MODEL), lambda c, i: (c, 0))],
        out_specs=[tile(0), tile(0), small(3, 0), small(3, 0), small(1, 0), small(1, 0)],
        out_shape=[_sds((S, D_FF), F32)] * 2 + [_sds((3, D_FF), F32)] * 2 + [_sds((1, D_FF), F32)] * 2,
        scratch_shapes=[pltpu.VMEM((8, CONV_RS, CONV_CB), F32), pltpu.VMEM((tm, CONV_CB), F32)],
        args=(u, u, u, u, cw, cw, cb, cb, dff, w_down))
    return res if plans is None else (res, carried)


def _conv_input_bwd(dcg, dcv, cw, name, plans=None):
    S = dcg.shape[0]
    tm = CONV_TM // 2
    nh = tm // HALO
    nt = S // tm
    n = CONV_RS + HALO
    tile = pl.BlockSpec((tm, D_FF), lambda i: (i, 0))
    nxt = pl.BlockSpec((HALO, D_FF), lambda i: (jnp.minimum((i + 1) * nh, S // HALO - 1), 0))

    def body(g_ref, ng_ref, v_ref, nv_ref, w_ref, du_ref):
        last_tile = pl.program_id(0) == nt - 1

        def strip(r0, last_strip):
            for half, (dc_ref, n_ref) in enumerate(((g_ref, ng_ref), (v_ref, nv_ref))):
                for k in range(D_FF // LANES):
                    cs = slice(k * LANES, (k + 1) * LANES)
                    ws = slice(half * D_FF + k * LANES, half * D_FF + (k + 1) * LANES)
                    if last_strip:
                        after = jnp.where(last_tile, 0.0, n_ref[:, cs])
                        blk = jnp.concatenate([dc_ref[tm - CONV_RS:tm, cs], after], axis=0)
                    else:
                        blk = dc_ref[pl.ds(r0, n), cs]
                    d1 = pltpu.roll(blk, n - 1, 0)[:CONV_RS]
                    d2 = pltpu.roll(blk, n - 2, 0)[:CONV_RS]
                    du_ref[pl.ds(r0, CONV_RS), ws] = _bf(w_ref[2:3, ws] * blk[:CONV_RS] + w_ref[1:2, ws] * d1
                                                         + w_ref[0:1, ws] * d2)

        lax.fori_loop(0, tm // CONV_RS - 1, lambda k, c: (strip(_strip_start(k), False), c)[1], 0)
        strip(tm - CONV_RS, True)

    (du,), carried = _call(
        body, plans, name=name, grid=(nt,),
        in_specs=[tile, nxt, tile, nxt, pl.BlockSpec((3, 2 * D_FF), lambda i: (0, 0))],
        out_specs=[pl.BlockSpec((tm, 2 * D_FF), lambda i: (i, 0))], out_shape=[_sds((S, 2 * D_FF), BF16)],
        args=(dcg, dcg, dcv, dcv, cw))
    return du if plans is None else (du, carried)


def _row_tile(n, cap):
    best = n
    for t in range(16, cap + 1, 16):
        if n % t == 0:
            best = t
    return best if best <= cap else n


def _rows_for_bytes(nbytes, cols):
    return max(16, nbytes // (4 * cols) // 16 * 16)


def _adamw_update(w_ref, g_ref, m_ref, v_ref, d_ref, nm_ref, nv_ref):
    gv = g_ref[...]
    nm = ADAM_B1 * m_ref[...] + (1.0 - ADAM_B1) * gv
    nv = ADAM_B2 * v_ref[...] + (1.0 - ADAM_B2) * (gv * gv)
    m_hat = nm / (1.0 - ADAM_B1 ** ADAM_STEP)
    v_hat = nv / (1.0 - ADAM_B2 ** ADAM_STEP)
    d_ref[...] = -ADAM_LR * (m_hat / (jnp.sqrt(v_hat) + ADAM_EPS) + ADAM_WD * w_ref[...])
    nm_ref[...] = nm
    nv_ref[...] = nv


def _adamw(w, g, m, v, name):
    R, C = w.shape
    tr = _row_tile(R, _rows_for_bytes(2 << 20, C))
    spec = pl.BlockSpec((tr, C), lambda i: (i, 0))
    body = functools.partial(_adamw_update)
    return pl.pallas_call(body, name=name, grid=(R // tr,), in_specs=[spec] * 4, out_specs=[spec] * 3,
                          out_shape=[_sds((R, C), F32)] * 3, compiler_params=_cp(1))(w, g, m, v)


def _adamw_small(ws, gs, ms, vs, name):
    n = len(ws)

    def body(*refs):
        ins, outs = refs[:4 * n], refs[4 * n:]
        for k in range(n):
            _adamw_update(ins[k], ins[n + k], ins[2 * n + k], ins[3 * n + k], outs[3 * k], outs[3 * k + 1], outs[3 * k + 2])

    vm = pl.BlockSpec(memory_space=pltpu.VMEM)
    outs = pl.pallas_call(body, name=name, in_specs=[vm] * (4 * n), out_specs=[vm] * (3 * n),
                          out_shape=[_sds(w.shape, F32) for w in ws for _ in range(3)])(*ws, *gs, *ms, *vs)
    return [tuple(outs[3 * k:3 * k + 3]) for k in range(n)]


def _pair_sum(gfull, rcv, c_idx, name):
    nb, R, C = gfull.shape
    half = R // 2
    tr = _row_tile(half, _rows_for_bytes(2 << 20, C))
    nt = half // tr

    def body(c_ref, g_ref, r_ref, o_ref):
        o_ref[...] = _bf(g_ref[...] + r_ref[...])

    return pl.pallas_call(
        body, name=name,
        grid_spec=pltpu.PrefetchScalarGridSpec(
            num_scalar_prefetch=1, grid=(nb, nt),
            in_specs=[pl.BlockSpec((None, tr, C), lambda j, i, c_ref: (j, c_ref[0] * nt + i, 0)),
                      pl.BlockSpec((None, tr, C), lambda j, i, c_ref: (j, i, 0))],
            out_specs=pl.BlockSpec((None, tr, C), lambda j, i, c_ref: (j, i, 0))),
        out_shape=_sds((nb, half, C), BF16), compiler_params=_cp(2))(c_idx, gfull, rcv)


def _chip_sum(arrived, own, place, name):
    nb, H, C = arrived.shape
    tr = _row_tile(H, _rows_for_bytes(2 << 20, C))
    nt = H // tr

    def body(pl_ref, *refs):
        o_ref = refs[nb + 1]
        me = pl_ref[0]
        acc = None
        for k in range(nb):
            term = jnp.where(me == k, refs[nb][...], refs[k][...]).astype(F32)
            acc = term if acc is None else acc + term
        o_ref[...] = acc

    def other(k):
        return pl.BlockSpec((None, tr, C), lambda i, p: (jnp.where(p[0] == k, (k + 1) % nb, k), i, 0))

    return pl.pallas_call(
        body, name=name,
        grid_spec=pltpu.PrefetchScalarGridSpec(
            num_scalar_prefetch=1, grid=(nt,),
            in_specs=[other(k) for k in range(nb)] + [pl.BlockSpec((None, tr, C), lambda i, p: (p[0], i, 0))],
            out_specs=pl.BlockSpec((tr, C), lambda i, p: (p[1] * nt + i, 0))),
        out_shape=_sds((2 * H, C), F32), compiler_params=_cp(1))(place, *([arrived] * nb), own)


def _cast_into_slot(shard, place, name):
    R, C = shard.shape
    tr = _row_tile(R, 256)

    def body(pl_ref, s_ref, o_ref):
        o_ref[...] = _bf(s_ref[...])

    return pl.pallas_call(
        body, name=name,
        grid_spec=pltpu.PrefetchScalarGridSpec(
            num_scalar_prefetch=1, grid=(R // tr,),
            in_specs=[pl.BlockSpec((tr, C), lambda i, p: (i, 0))],
            out_specs=pl.BlockSpec((None, tr, C), lambda i, p: (p[0], i, 0))),
        out_shape=_sds((N_CHIPS, R, C), BF16), compiler_params=_cp(1))(place, shard)


def _place():
    x, y, c = lax.axis_index("x"), lax.axis_index("y"), lax.axis_index("c")
    chips = [(1 - x, y), (x, 1 - y), (1 - x, 1 - y)]
    return x, y, c, chips


def _chip_id(px, py):
    return 2 * px + py


def _remote(src, dst, send_sems, recv_sems, k, to):
    return pltpu.make_async_remote_copy(src_ref=src, dst_ref=dst, send_sem=send_sems.at[k], recv_sem=recv_sems.at[k],
                                        device_id=to, device_id_type=MESH)


def _proj_gathered(x, w_norm, slot, place, name, tm=1024, plan=None):
    M, K = x.shape
    nb, _, Nb = slot.shape
    half = K // 2
    nt = M // tm
    cx, cy = place[0] // 2, place[0] % 2
    order = jnp.stack([place[0], _chip_id(1 - cx, cy), _chip_id(cx, 1 - cy), _chip_id(1 - cx, 1 - cy)]).astype(jnp.int32)

    p_in = [] if plan is None else plan.ins + plan.inouts
    p_out = [] if plan is None else [_sds(a.shape, a.dtype) for a in plan.inouts] + plan.outs
    n_pi, n_po = len(p_in), len(p_out)

    def body(order_ref, x_ref, wn_ref, slot_in, *refs):
        o_ref, slot_ref, h_out = refs[n_pi:n_pi + 3]
        s0 = n_pi + 3 + n_po
        w_buf, hs, ici_send, ici_recv, pass_send, pass_recv, load_sem = refs[s0:s0 + 7]

        def carried():
            if plan is None:
                return [], [], []
            ins = refs[:len(plan.ins)]
            outs = refs[n_pi + 3:n_pi + 3 + n_po]
            return plan.copies(ins, outs[:len(plan.inouts)], outs[len(plan.inouts):], *refs[s0 + 7:])

        b, i = pl.program_id(0), pl.program_id(1)
        x, y, c, chips = _place()
        me = _chip_id(x, y)
        sib = (x, y, 1 - c)
        mine, other = pl.ds(c * half, half), pl.ds((1 - c) * half, half)

        def sent(k):
            blk = slot_ref.at[me, mine]
            return _remote(blk, blk, ici_send, ici_recv, k, (*chips[k], c))

        def landed(k):
            blk = slot_ref.at[_chip_id(*chips[k]), mine]
            return _remote(blk, blk, ici_send, ici_recv, k, (*chips[k], c))

        def passed(k, rows):
            blk = slot_ref.at[_chip_id(*chips[k]), rows]
            return _remote(blk, blk, pass_send, pass_recv, k, sib)

        @pl.when((b == 0) & (i == 0))
        def _():
            for k in range(len(chips)):
                sent(k).start()
            sends, _, local = carried()
            for cp in (*sends, *local):
                cp.start()

        for k in range(len(chips)):
            @pl.when((b == k + 1) & (i == 0))
            def _(k=k):
                landed(k).wait_recv()
                passed(k, mine).start()
                passed(k, other).wait_recv()

        @pl.when(i == 0)
        def _():
            load = pltpu.make_async_copy(slot_ref.at[order_ref[b]], w_buf, load_sem.at[0])
            load.start()
            load.wait()

        rows = pl.ds(pl.multiple_of(i * tm, tm), tm)
        keep_h = pltpu.make_async_copy(hs, h_out, load_sem.at[1])

        @pl.when(b == 0)
        def _():
            xv = x_ref[...]
            hs[rows, :] = _bf(xv * _rinv(xv) * wn_ref[...])

        @pl.when((b == 1) & (i == 0))
        def _():
            keep_h.start()

        o_ref[...] = _dot(hs[rows, :], w_buf[...], NN)

        @pl.when((b == nb - 1) & (i == nt - 1))
        def _():
            for k in range(len(chips)):
                sent(k).wait_send()
                passed(k, mine).wait_send()
            sends, recvs, local = carried()
            for cp in recvs:
                cp.wait_recv()
            for cp in sends:
                cp.wait_send()
            for cp in local:
                cp.wait()
            keep_h.wait()

    n_peers = N_CHIPS - 1
    return pl.pallas_call(
        body, name=name,
        grid_spec=pltpu.PrefetchScalarGridSpec(
            num_scalar_prefetch=1, grid=(nb, nt),
            in_specs=[pl.BlockSpec((tm, K), lambda b, i, o: (i, 0)), pl.BlockSpec((1, K), lambda b, i, o: (0, 0)), ANY]
            + [ANY] * n_pi,
            out_specs=[pl.BlockSpec((tm, Nb), lambda b, i, o: (i, o[b])), ANY, ANY] + [ANY] * n_po,
            scratch_shapes=[pltpu.VMEM((K, Nb), BF16), pltpu.VMEM((M, K), BF16)]
            + [pltpu.SemaphoreType.DMA((n_peers,))] * 4 + [pltpu.SemaphoreType.DMA((2,))]
            + ([] if plan is None else [pltpu.SemaphoreType.DMA((plan.n_sems,))] * 3)),
        out_shape=[_sds((M, nb * Nb), F32), _sds(slot.shape, slot.dtype), _sds((M, K), BF16)] + p_out,
        input_output_aliases={3: 1, **({} if plan is None else
                                       {4 + len(plan.ins) + a: 3 + a for a in range(len(plan.inouts))})},
        compiler_params=_cp(2))(order, x, w_norm, slot, *p_in)


def _gather_ici_plan(slots, wholes, part=None):
    ns, nw = len(slots), len(wholes)

    def copies(ins, ios, outs, send_sems, recv_sems, local_sems):
        x, y, c, chips = _place()
        me = _chip_id(x, y)
        sends, recvs = [], []
        for a in range(ns + nw):
            dst = ios[a] if a < ns else outs[a - ns]
            R = dst.shape[1]
            r0, nr = (0, R // 2) if part is None else part
            rows = pl.ds(c * (R // 2) + r0, nr) if a < ns else pl.ds(0, R)
            src = dst.at[me, rows] if a < ns else ins[a - ns]
            for j, chip in enumerate(chips):
                sends.append(_remote(src, dst.at[me, rows], send_sems, recv_sems, 3 * a + j, (*chip, c)))
                landed = dst.at[_chip_id(*chip), rows]
                recvs.append(_remote(landed, landed, send_sems, recv_sems, 3 * a + j, (*chip, c)))
        local = [pltpu.make_async_copy(ins[b], outs[b].at[me], local_sems.at[b]) for b in range(nw)]
        return sends, recvs, local

    return _Plan(copies, 3 * (ns + nw), ins=wholes, inouts=slots,
                 outs=[_sds((N_CHIPS, *s.shape), s.dtype) for s in wholes])


def _gather_pass_plan(slots):
    def copies(ins, ios, outs, send_sems, recv_sems, local_sems):
        x, y, c, chips = _place()
        sib = (x, y, 1 - c)
        sends, recvs = [], []
        for a, buf in enumerate(ios):
            half = buf.shape[1] // 2
            for j, chip in enumerate(chips):
                mine = buf.at[_chip_id(*chip), pl.ds(c * half, half)]
                other = buf.at[_chip_id(*chip), pl.ds((1 - c) * half, half)]
                sends.append(_remote(mine, mine, send_sems, recv_sems, 3 * a + j, sib))
                recvs.append(_remote(other, other, send_sems, recv_sems, 3 * a + j, sib))
        return sends, recvs, []

    return _Plan(copies, 3 * len(slots), inouts=slots)


def _pair_plan(grads):
    def copies(ins, ios, outs, send_sems, recv_sems, local_sems):
        x, y, c, _ = _place()
        sib = (x, y, 1 - c)
        sends, recvs = [], []
        for a, g in enumerate(ins):
            half = g.shape[1] // 2
            sends.append(_remote(g.at[:, pl.ds((1 - c) * half, half), :], outs[a], send_sems, recv_sems, a, sib))
            recvs.append(_remote(outs[a], outs[a], send_sems, recv_sems, a, sib))
        return sends, recvs, []

    return _Plan(copies, len(grads), ins=grads,
                 outs=[_sds((g.shape[0], g.shape[1] // 2, g.shape[2]), g.dtype) for g in grads])


def _chip_plan(parts):
    def copies(ins, ios, outs, send_sems, recv_sems, local_sems):
        x, y, c, chips = _place()
        me = _chip_id(x, y)
        sends, recvs = [], []
        for a, part in enumerate(ins):
            for j, chip in enumerate(chips):
                sends.append(_remote(part.at[_chip_id(*chip)], outs[a].at[me], send_sems, recv_sems, 3 * a + j, (*chip, c)))
                landed = outs[a].at[_chip_id(*chip)]
                recvs.append(_remote(landed, landed, send_sems, recv_sems, 3 * a + j, (*chip, c)))
        return sends, recvs, []

    return _Plan(copies, 3 * len(parts), ins=parts, outs=[_sds(p.shape, p.dtype) for p in parts])


def _all_sum(pack, fulls, name):
    R, C = pack.shape
    n = len(fulls)

    def body(p_ref, *refs):
        o_ref, halves = refs[n], refs[n + 1:2 * n + 1]
        buf, send_sems, recv_sems, pair_send, pair_recv = refs[2 * n + 1:]
        x, y, c, _ = _place()
        sib = (x, y, 1 - c)
        pair = []
        for a, full in enumerate(halves):
            H = full.shape[0] // 2
            mine = full.at[pl.ds(c * H, H)]
            cp = _remote(mine, mine, pair_send, pair_recv, a, sib)
            cp.start()
            pair.append(cp)
        me = 4 * x + 2 * y + c
        buf[me] = p_ref[...]
        cps = []
        for k in range(1, N_DEV):
            to = (x ^ (k >> 2), y ^ ((k >> 1) & 1), c ^ (k & 1))
            cp = _remote(p_ref, buf.at[me], send_sems, recv_sems, k - 1, to)
            cp.start()
            cps.append(cp)
        for k in range(1, N_DEV):
            frm = (x ^ (k >> 2), y ^ ((k >> 1) & 1), c ^ (k & 1))
            slot = buf.at[4 * frm[0] + 2 * frm[1] + frm[2]]
            _remote(slot, slot, send_sems, recv_sems, k - 1, frm).wait_recv()
        acc = buf[0]
        for k in range(1, N_DEV):
            acc = acc + buf[k]
        o_ref[...] = acc
        for cp in cps:
            cp.wait_send()
        for a, (full, cp) in enumerate(zip(halves, pair)):
            H = full.shape[0] // 2
            other = full.at[pl.ds((1 - c) * H, H)]
            _remote(other, other, pair_send, pair_recv, a, sib).wait_recv()
            cp.wait_send()

    vm = pl.BlockSpec(memory_space=pltpu.VMEM)
    res = pl.pallas_call(
        body, name=name, in_specs=[vm] + [ANY] * n, out_specs=[vm] + [ANY] * n,
        out_shape=[_sds((R, C), F32)] + [_sds(f.shape, f.dtype) for f in fulls],
        input_output_aliases={1 + a: 1 + a for a in range(n)},
        scratch_shapes=[pltpu.VMEM((N_DEV, R, C), F32), pltpu.SemaphoreType.DMA((N_DEV - 1,)),
                        pltpu.SemaphoreType.DMA((N_DEV - 1,)), pltpu.SemaphoreType.DMA((n,)),
                        pltpu.SemaphoreType.DMA((n,))])(pack, *fulls)
    return res[0], list(res[1:])


def _local_step(xs, tgt, p, ex):
    proj, h1 = ex.project(xs, p["pre_mix_norm"])
    biases = _relbias_fwd(p["rel_bias"], "rel_bias_fwd")
    fw = []
    for g in range(N_GROUPS):
        res, got = _attn_fwd(proj, biases[g], g, f"attn_fwd{g}", plans=ex.carry(f"attn_fwd{g}"))
        ex.done(f"attn_fwd{g}", got)
        fw.append(res)
    (yh, o_h, states), got = _hgrn_fwd(proj, p["hgrn_lb_raw"], p["hgrn_norm"], "hgrn_fwd", plans=ex.carry("hgrn_fwd"))
    ex.done("hgrn_fwd", got)
    W_a, W_h, W_out = ex.weight("w_branch_attn"), ex.weight("w_branch_hgrn"), ex.weight("w_out")
    (y, lse, za, zh, merged), got = _branch_fwd([t[0] for t in fw], [t[1] for t in fw], yh, proj, W_a, W_h,
                                                "branch_fwd", plans=ex.carry("branch_fwd"))
    ex.done("branch_fwd", got)
    W_up, conv_w = ex.weight("w_up"), ex.weight("conv_w")
    mo, x1, h2 = _mix_out(merged, W_out, xs, p["post_mix_norm"], p["pre_ffn_norm"], "mix_out")
    u, got = _mm_nn_blk(h2, W_up, "ffn_up", tm=1024, plans=ex.carry("ffn_up"))
    ex.done("ffn_up", got)
    a, got = _conv_gelu_fwd(u, conv_w, p["conv_b"], "conv_gelu_fwd", plans=ex.carry("conv_gelu_fwd"))
    ex.done("conv_gelu_fwd", got)
    W_down = ex.weight("w_down")
    dx2, dff, g_post_ffn, loss = _loss_head(a, W_down, x1, tgt, p["post_ffn_norm"], "ffn_down_loss")

    ex.grad("w_down", _mm_tn(a, dff, "g_w_down").reshape(N_CHIPS, D_FF // N_CHIPS, D_MODEL))
    (dcg, dcv, gwg, gwv, gbg, gbv), got = _conv_gelu_bwd(u, dff, W_down, conv_w, p["conv_b"], "conv_gelu_bwd",
                                                          plans=ex.carry("conv_gelu_bwd"))
    ex.done("conv_gelu_bwd", got)
    g_conv_w = jnp.concatenate([gwg, gwv], axis=1)
    g_conv_b = jnp.concatenate([gbg, gbv], axis=1)
    du, got = _conv_input_bwd(dcg, dcv, conv_w, "conv_input_bwd", plans=ex.carry("conv_input_bwd"))
    ex.done("conv_input_bwd", got)
    dx1, g_pre_ffn = _mm_nt_prenorm_bwd(du, W_up, x1, p["pre_ffn_norm"], dx2, "d_ffn_in")
    ex.grad("w_up", _mm_tn_blk(h2, du, N_CHIPS, "g_w_up"))
    (dmo, dmerged, g_post_mix), got = _postnorm_bwd(dx1, mo, p["post_mix_norm"], W_out, "post_mix_norm_bwd",
                                                    plans=ex.carry("post_mix_norm_bwd"))
    ex.done("post_mix_norm_bwd", got)
    ex.grad("w_out", _mm_tn(merged, dmo, "g_w_out").reshape(N_CHIPS, D_MODEL // N_CHIPS, D_MODEL))
    (dza, dzh, dg0, dg1, dy, dyh), got = _branch_bwd(dmerged, za, zh, proj, W_a, W_h, "branch_bwd",
                                                     plans=ex.carry("branch_bwd"))
    ex.done("branch_bwd", got)
    ex.grad("w_branch_attn", _mm_tn_blk(y, dza, N_CHIPS, "g_w_branch_attn", together=True))
    ex.grad("w_branch_hgrn", _mm_tn_blk(yh, dzh, N_CHIPS, "g_w_branch_hgrn", together=True))
    dqkv, dbs = [], []
    for g in range(N_GROUPS):
        parts, db, got = _attn_bwd(proj, biases[g], lse, y, dy, g, f"attn_bwd{g}", plans=ex.carry(f"attn_bwd{g}"))
        ex.done(f"attn_bwd{g}", got)
        dqkv += parts
        dbs.append(db)
    g_rel_bias = _relbias_bwd(dbs, "rel_bias_bwd")
    dproj, g_lb_raw, g_hgrn_norm = _hgrn_bwd(proj, p["hgrn_lb_raw"], p["hgrn_norm"], o_h, states, dyh, dqkv,
                                             [dg0, dg1], "hgrn_bwd")
    for piece in W_IN_PIECES:
        g, got = _mm_tn_blk(h1, dproj, N_CHIPS, f"g_{piece}", x_cols=W_IN_ROWS[piece],
                            plans=ex.carry(f"g_{piece}"))
        ex.done(f"g_{piece}", got)
        ex.grad(piece, g)
    dh1, got = _mm_nt_blk(dproj, ex.weight("w_in"), "d_proj_in", plans=ex.carry("d_proj_in"))
    ex.done("d_proj_in", got)
    (grad_x, g_pre_mix), got = _prenorm_bwd(dh1, xs, p["pre_mix_norm"], dx1, "pre_mix_norm_bwd",
                                            plans=ex.carry("pre_mix_norm_bwd"))
    ex.done("pre_mix_norm_bwd", got)
    small = dict(pre_mix_norm=g_pre_mix, rel_bias=g_rel_bias, hgrn_lb_raw=g_lb_raw, hgrn_norm=g_hgrn_norm,
                 post_mix_norm=g_post_mix, pre_ffn_norm=g_pre_ffn, conv_w=g_conv_w, conv_b=g_conv_b,
                 post_ffn_norm=g_post_ffn)
    return loss, grad_x, small


SMALL = ("pre_mix_norm", "rel_bias", "hgrn_lb_raw", "hgrn_norm", "post_mix_norm", "pre_ffn_norm", "conv_w", "conv_b",
         "post_ffn_norm")
BIG = ("w_in", "w_up", "w_down", "w_out", "w_branch_attn", "w_branch_hgrn")
WEIGHTS = ("pre_mix_norm", "w_in", "rel_bias", "hgrn_lb_raw", "hgrn_norm", "w_branch_attn", "w_branch_hgrn", "w_out",
           "post_mix_norm", "pre_ffn_norm", "w_up", "conv_w", "conv_b", "w_down", "post_ffn_norm")
MIXER = ("w_out", "w_branch_attn", "w_branch_hgrn")

ICI_PARTS = {"gather_ici_1of3": (0, 176), "gather_ici_2of3": (176, 176), "gather_ici_3of3": (352, 160)}
SCHEDULE = {
    "proj_in": [("gather_ici_cw", MIXER)],
    "attn_fwd1": [("gather_ici_1of3", ("w_up",))],
    "attn_fwd2": [("gather_pass", MIXER), ("gather_ici_2of3", ("w_up",))],
    "hgrn_fwd": [("gather_ici_3of3", ("w_up",))],
    "branch_fwd": [("gather_pass", ("w_up",))],
    "ffn_up": [("gather_ici", ("w_down",))],
    "conv_gelu_fwd": [("gather_pass", ("w_down",))],
    "conv_gelu_bwd": [("pair", ("w_down",))],
    "conv_input_bwd": [("chip", ("w_down",))],
    "post_mix_norm_bwd": [("pair", ("w_up",))],
    "attn_bwd0": [("chip", ("w_up",)), ("pair", MIXER)],
    "attn_bwd1": [("chip", MIXER)],
    "g_w_in_b": [("pair", ("w_in_a",))],
    "d_proj_in": [("chip", ("w_in_a",)), ("pair", ("w_in_b",))],
    "pre_mix_norm_bwd": [("chip", ("w_in_b",))],
}
W_IN_ROWS = dict(w_in_a=(0, 768), w_in_b=(3, 256))
W_IN_PIECES = tuple(W_IN_ROWS)
REDUCED = W_IN_PIECES + BIG[1:]


class _Exchange:
    def __init__(self, place, slots, conv_w_shard):
        self.place, self.slots, self.conv_w_shard = place, dict(slots), conv_w_shard
        self.conv_w = None
        self.g, self.from_sibling, self.pair_sums, self.arrived = {}, {}, {}, {}
        self.pending = []

    def weight(self, name):
        if name == "conv_w":
            return self.conv_w
        w = self.slots[name]
        return w.reshape(-1, D_MODEL) if name in ("w_out", "w_down") else w

    def project(self, x, w_norm):
        (plan,) = self.carry("proj_in")
        proj, self.slots["w_in"], h, *got = _proj_gathered(x, w_norm, self.slots["w_in"], self.place, "proj_in",
                                                           plan=plan)
        self.done("proj_in", [got])
        return proj, h

    def grad(self, name, g):
        self.g[name] = g

    def carry(self, point):
        plans = []
        self.pending = SCHEDULE.get(point, [])
        for kind, names in self.pending:
            if kind in ("gather_ici", "gather_ici_cw") or kind in ICI_PARTS:
                wholes = [self.conv_w_shard] if kind == "gather_ici_cw" else []
                plans.append(_gather_ici_plan([self.slots[n] for n in names], wholes, ICI_PARTS.get(kind)))
            elif kind == "gather_pass":
                plans.append(_gather_pass_plan([self.slots[n] for n in names]))
            elif kind == "pair":
                plans.append(_pair_plan([self.g[n] for n in names]))
            else:
                for n in names:
                    self.pair_sums[n] = _pair_sum(self.g[n], self.from_sibling[n], self.place[1:2], f"pair_sum_{n}")
                plans.append(_chip_plan([self.pair_sums[n] for n in names]))
        return plans

    def done(self, point, carried):
        for (kind, names), got in zip(self.pending, carried):
            if kind in ("gather_ici", "gather_ici_cw", "gather_pass") or kind in ICI_PARTS:
                self.slots.update(zip(names, got))
                if kind == "gather_ici_cw":
                    self.conv_w = got[len(names)].transpose(1, 0, 2).reshape(3, 2 * D_FF)
            elif kind == "pair":
                self.from_sibling.update(zip(names, got))
            else:
                self.arrived.update(zip(names, got))

    def reduced_halves(self):
        return [_chip_sum(self.arrived[n], self.pair_sums[n], self.place, f"chip_sum_{n}") for n in REDUCED]


def kernel(x, pre_mix_norm, w_in, rel_bias, hgrn_lb_raw, hgrn_norm, w_branch_attn, w_branch_hgrn, w_out, post_mix_norm, pre_ffn_norm, w_up, conv_w, conv_b, w_down, post_ffn_norm, loss_target, m_pre_mix_norm, m_w_in, m_rel_bias, m_hgrn_lb_raw, m_hgrn_norm, m_w_branch_attn, m_w_branch_hgrn, m_w_out, m_post_mix_norm, m_pre_ffn_norm, m_w_up, m_conv_w, m_conv_b, m_w_down, m_post_ffn_norm, v_pre_mix_norm, v_w_in, v_rel_bias, v_hgrn_lb_raw, v_hgrn_norm, v_w_branch_attn, v_w_branch_hgrn, v_w_out, v_post_mix_norm, v_pre_ffn_norm, v_w_up, v_conv_w, v_conv_b, v_w_down, v_post_ffn_norm):
    w = dict(pre_mix_norm=pre_mix_norm, w_in=w_in, rel_bias=rel_bias, hgrn_lb_raw=hgrn_lb_raw, hgrn_norm=hgrn_norm,
             w_branch_attn=w_branch_attn, w_branch_hgrn=w_branch_hgrn, w_out=w_out, post_mix_norm=post_mix_norm,
             pre_ffn_norm=pre_ffn_norm, w_up=w_up, conv_w=conv_w, conv_b=conv_b, w_down=w_down,
             post_ffn_norm=post_ffn_norm)
    m = dict(pre_mix_norm=m_pre_mix_norm, w_in=m_w_in, rel_bias=m_rel_bias, hgrn_lb_raw=m_hgrn_lb_raw,
             hgrn_norm=m_hgrn_norm, w_branch_attn=m_w_branch_attn, w_branch_hgrn=m_w_branch_hgrn, w_out=m_w_out,
             post_mix_norm=m_post_mix_norm, pre_ffn_norm=m_pre_ffn_norm, w_up=m_w_up, conv_w=m_conv_w,
             conv_b=m_conv_b, w_down=m_w_down, post_ffn_norm=m_post_ffn_norm)
    v = dict(pre_mix_norm=v_pre_mix_norm, w_in=v_w_in, rel_bias=v_rel_bias, hgrn_lb_raw=v_hgrn_lb_raw,
             hgrn_norm=v_hgrn_norm, w_branch_attn=v_w_branch_attn, w_branch_hgrn=v_w_branch_hgrn, w_out=v_w_out,
             post_mix_norm=v_post_mix_norm, pre_ffn_norm=v_pre_ffn_norm, w_up=v_w_up, conv_w=v_conv_w,
             conv_b=v_conv_b, w_down=v_w_down, post_ffn_norm=v_post_ffn_norm)
    shard2d = {n: (w[n][0] if w[n].ndim == 3 else w[n]) for n in WEIGHTS}
    chip = 2 * lax.axis_index("x") + lax.axis_index("y")
    core = lax.axis_index("c")

    place = jnp.stack([chip, core]).astype(jnp.int32)
    slots = {n: _cast_into_slot(shard2d[n], place, f"cast_{n}") for n in BIG}
    ex = _Exchange(place, slots, shard2d["conv_w"])
    loss, grad_x, small = _local_step(x[0], loss_target[0], {n: w[n] for n in SMALL if n != "conv_w"}, ex)

    flat = [small[n].reshape(-1) for n in SMALL] + [loss.reshape(-1)]
    sizes = [t.shape[0] for t in flat]
    summed, wholes = _all_sum(jnp.concatenate(flat).reshape(-1, LANES), ex.reduced_halves(), "sum_small")
    summed = summed.reshape(-1)
    offs = [sum(sizes[:i]) for i in range(len(sizes))]
    grads = {}
    for n, o, sz in zip(SMALL, offs, sizes):
        grads[n] = summed[o:o + sz].reshape(small[n].shape)
    loss_total = summed[offs[-1]]
    cw = 2 * D_FF // N_CHIPS
    grads["conv_w"] = lax.dynamic_slice(grads["conv_w"], (0, chip * cw), (3, cw))

    big = dict(zip(REDUCED, wholes))
    big["w_in"] = jnp.concatenate([big.pop(n) for n in W_IN_PIECES], axis=0)
    grads.update(big)

    m2d = {n: m[n].reshape(shard2d[n].shape) for n in WEIGHTS}
    v2d = {n: v[n].reshape(shard2d[n].shape) for n in WEIGHTS}
    updated = dict(zip(SMALL, _adamw_small([shard2d[n] for n in SMALL], [grads[n] for n in SMALL],
                                           [m2d[n] for n in SMALL], [v2d[n] for n in SMALL], "adamw_small")))
    for n in BIG:
        updated[n] = _adamw(shard2d[n], grads[n], m2d[n], v2d[n], f"adamw_{n}")
    out_g, out_d, out_m, out_v = [], [], [], []
    for n in WEIGHTS:
        d2, m2, v2 = updated[n]
        shape = w[n].shape
        out_g.append(grads[n].reshape(shape))
        out_d.append(d2.reshape(shape))
        out_m.append(m2.reshape(shape))
        out_v.append(v2.reshape(shape))
    return (loss_total, grad_x[None], *out_g, *out_d, *out_m, *out_v)
```

```python
import functools
import math

import jax
import jax.numpy as jnp
from jax import lax
from jax.experimental import pallas as pl
from jax.experimental.pallas import tpu as pltpu

F32 = jnp.float32
BF16 = jnp.bfloat16
MESH = pl.DeviceIdType.MESH

D_MODEL = 1024
N_GROUPS = 3
DILATIONS = (1, 4, 16)
HEADS = 8
HEAD_DIM = 64
GROUP_W = HEADS * HEAD_DIM
QKV_W = N_GROUPS * 3 * GROUP_W
BLK = 128
NEG_INF = -1e30
NUM_BUCKETS = 32
MAX_EXACT = 16
MAX_DISTANCE = 2048
HG_HEADS = 4
HG_DK = 128
HG_W = HG_HEADS * HG_DK
HG_CHUNK = 32
HG_TILE = 256
IN_W = QKV_W + 4 * HG_W + 2 * D_MODEL
D_FF = 2816
EPS = 1e-6
N_CHIPS = 4
N_DEV = 8
LANES = 128
SUBLANES = 8
STRIP_UNROLL = 8

ADAM_LR, ADAM_B1, ADAM_B2, ADAM_EPS, ADAM_WD, ADAM_STEP = 0.001, 0.9, 0.999, 1e-08, 0.01, 10

VMEM_LIMIT = 56 * 1024 * 1024


def _cp(n_axes):
    return pltpu.CompilerParams(dimension_semantics=("arbitrary",) * n_axes, vmem_limit_bytes=VMEM_LIMIT)


def _sds(shape, dtype):
    return jax.ShapeDtypeStruct(tuple(shape), dtype)


def _sigmoid(v):
    return 1.0 / (1.0 + jnp.exp(-v))


def _bf(v):
    return v.astype(BF16)


def _dot(a, b, dims):
    return lax.dot_general(a, b, (dims, ((), ())), preferred_element_type=F32)


NN = ((1,), (0,))
NT = ((1,), (1,))
TN = ((0,), (0,))

ANY = pl.BlockSpec(memory_space=pl.ANY)


class _Plan:
    def __init__(self, copies, n_sems, ins=(), inouts=(), outs=()):
        self.copies, self.n_sems = copies, n_sems
        self.ins, self.inouts, self.outs = list(ins), list(inouts), list(outs)


def _call(body, plans=None, *, name, grid, in_specs, out_specs, out_shape, args, scratch_shapes=()):
    plans = list(plans or ())
    in_specs, out_specs, out_shape = list(in_specs), list(out_specs), list(out_shape)
    scratch_shapes = list(scratch_shapes)
    n_in, n_out, n_scr = len(in_specs), len(out_specs), len(scratch_shapes)
    x_in, x_out, aliases, spans = [], [], {}, []
    for p in plans:
        i0, o0 = len(x_in), len(x_out)
        x_in += p.ins
        for a in p.inouts:
            aliases[n_in + len(x_in)] = n_out + len(x_out)
            x_in.append(a)
            x_out.append(_sds(a.shape, a.dtype))
        x_out += p.outs
        spans.append((i0, len(p.ins), o0, len(p.inouts), len(p.outs)))
    sems = [pltpu.SemaphoreType.DMA((p.n_sems,)) for p in plans for _ in range(3)]

    def wrapped(*refs):
        xi = refs[n_in:n_in + len(x_in)]
        base = n_in + len(x_in)
        xo = refs[base + n_out:base + n_out + len(x_out)]
        sbase = base + n_out + len(x_out)
        xs = refs[sbase + n_scr:]
        ids = [pl.program_id(k) for k in range(len(grid))]
        first = functools.reduce(jnp.logical_and, [i == 0 for i in ids])
        last = functools.reduce(jnp.logical_and, [i == g - 1 for i, g in zip(ids, grid)])

        def descriptors(k):
            i0, ni, o0, nio, no = spans[k]
            return plans[k].copies(xi[i0:i0 + ni], xo[o0:o0 + nio], xo[o0 + nio:o0 + nio + no], *xs[3 * k:3 * k + 3])

        @pl.when(first)
        def _():
            for k in range(len(plans)):
                sends, _, local = descriptors(k)
                for cp in (*sends, *local):
                    cp.start()

        body(*refs[:n_in], *refs[base:base + n_out], *refs[sbase:sbase + n_scr])

        @pl.when(last)
        def _():
            for k in range(len(plans)):
                sends, recvs, local = descriptors(k)
                for cp in recvs:
                    cp.wait_recv()
                for cp in sends:
                    cp.wait_send()
                for cp in local:
                    cp.wait()

    res = pl.pallas_call(
        wrapped if plans else body, name=name, grid=grid, in_specs=in_specs + [ANY] * len(x_in),
        out_specs=out_specs + [ANY] * len(x_out), out_shape=out_shape + x_out, input_output_aliases=aliases,
        scratch_shapes=scratch_shapes + sems, compiler_params=_cp(len(grid)))(*args, *x_in)
    res = list(res)
    carried = [res[n_out + o0:n_out + o0 + nio + no] for (_, _, o0, nio, no) in spans]
    return res[:n_out], carried


def _mm_nn_blk(a, wg, name, tm=512, plans=None):
    M, K = a.shape
    nb, _, Nb = wg.shape

    def body(a_ref, w_ref, o_ref):
        o_ref[...] = _dot(_bf(a_ref[...]), w_ref[...], NN)

    (out,), carried = _call(
        body, plans, name=name, grid=(nb, M // tm),
        in_specs=[pl.BlockSpec((tm, K), lambda j, i: (i, 0)), pl.BlockSpec((None, K, Nb), lambda j, i: (j, 0, 0))],
        out_specs=[pl.BlockSpec((tm, Nb), lambda j, i: (i, j))],
        out_shape=[_sds((M, nb * Nb), F32)], args=(a, wg))
    return out if plans is None else (out, carried)


def _mm_nt_blk(dy, wg, name, tm=1024, plans=None):
    M = dy.shape[0]
    nb, K, Nb = wg.shape

    def body(dy_ref, w_ref, o_ref):
        j = pl.program_id(1)
        r = _dot(_bf(dy_ref[...]), w_ref[...], NT)

        @pl.when(j == 0)
        def _():
            o_ref[...] = r

        @pl.when(j > 0)
        def _():
            o_ref[...] += r

    (out,), carried = _call(
        body, plans, name=name, grid=(M // tm, nb),
        in_specs=[pl.BlockSpec((tm, Nb), lambda i, j: (i, j)), pl.BlockSpec((None, K, Nb), lambda i, j: (j, 0, 0))],
        out_specs=[pl.BlockSpec((tm, K), lambda i, j: (i, 0))],
        out_shape=[_sds((M, K), F32)], args=(dy, wg))
    return out if plans is None else (out, carried)


def _mm_nt_prenorm_bwd(dy, wg, xin, w, dres, name, tm=1024):
    M = dy.shape[0]
    nb, K, Nb = wg.shape

    def body(dy_ref, w_ref, x_ref, wn_ref, dres_ref, dx_ref, dw_ref, acc):
        i, j = pl.program_id(0), pl.program_id(1)
        r = _dot(_bf(dy_ref[...]), w_ref[...], NT)

        @pl.when(j == 0)
        def _():
            acc[...] = r

        @pl.when(j > 0)
        def _():
            acc[...] += r

        @pl.when(j == nb - 1)
        def _():
            wn = wn_ref[...]

            def strip(s, part):
                rows = pl.ds(pl.multiple_of(s * SUBLANES, SUBLANES), SUBLANES)
                xv = x_ref[rows, :]
                rinv = _rinv(xv)
                xhat = xv * rinv
                dh = acc[rows, :]
                dx_ref[rows, :] = dres_ref[rows, :] + _norm_bwd(dh, xhat, rinv, wn)
                return part + dh * xhat

            part = lax.fori_loop(0, tm // SUBLANES, strip, jnp.zeros((SUBLANES, K), F32), unroll=STRIP_UNROLL)
            part = jnp.sum(part, axis=0, keepdims=True)

            @pl.when(i == 0)
            def _():
                dw_ref[...] = part

            @pl.when(i > 0)
            def _():
                dw_ref[...] += part

    row = pl.BlockSpec((tm, K), lambda i, j: (i, 0))
    vec = pl.BlockSpec((1, K), lambda i, j: (0, 0))
    return pl.pallas_call(
        body, name=name, grid=(M // tm, nb),
        in_specs=[pl.BlockSpec((tm, Nb), lambda i, j: (i, j)), pl.BlockSpec((None, K, Nb), lambda i, j: (j, 0, 0)),
                  row, vec, row],
        out_specs=[row, vec], out_shape=[_sds((M, K), F32), _sds((1, K), F32)],
        scratch_shapes=[pltpu.VMEM((tm, K), F32)], compiler_params=_cp(2))(dy, wg, xin, w, dres)


def _mm_tn_blk(x, dy, nb, name, tk=2048, x_cols=None, plans=None, together=False):
    T, Mx = x.shape
    xk, Mx = (0, Mx) if x_cols is None else x_cols
    Nb = dy.shape[1] // nb
    nj = nb if together else 1

    def body(x_ref, dy_ref, o_ref):
        t = pl.program_id(1)
        r = _dot(_bf(x_ref[...]), _bf(dy_ref[...]), TN)
        for j in range(nj):
            rj = r[:, j * Nb:(j + 1) * Nb]

            @pl.when(t == 0)
            def _():
                o_ref[j] = rj

            @pl.when(t > 0)
            def _():
                o_ref[j] += rj

    (out,), carried = _call(
        body, plans, name=name, grid=(nb // nj, T // tk),
        in_specs=[pl.BlockSpec((tk, Mx), lambda j, t: (t, xk)), pl.BlockSpec((tk, nj * Nb), lambda j, t: (t, j))],
        out_specs=[pl.BlockSpec((nj, Mx, Nb), lambda j, t: (j, 0, 0))],
        out_shape=[_sds((nb, Mx, Nb), F32)], args=(x, dy))
    return out if plans is None else (out, carried)


def _mm_tn(x, dy, name, tk=1024):
    T, Mx = x.shape
    N = dy.shape[1]

    def body(x_ref, dy_ref, o_ref):
        t = pl.program_id(0)
        r = _dot(_bf(x_ref[...]), _bf(dy_ref[...]), TN)

        @pl.when(t == 0)
        def _():
            o_ref[...] = r

        @pl.when(t > 0)
        def _():
            o_ref[...] += r

    return pl.pallas_call(
        body, name=name, grid=(T // tk,),
        in_specs=[pl.BlockSpec((tk, Mx), lambda t: (t, 0)), pl.BlockSpec((tk, N), lambda t: (t, 0))],
        out_specs=pl.BlockSpec((Mx, N), lambda t: (0, 0)),
        out_shape=_sds((Mx, N), F32), compiler_params=_cp(1))(x, dy)


def _tile(arr, bw, col=lambda c: 0):
    return ("tile", arr, bw, col)


def _full(arr):
    return ("full", arr)


def _out_tile(width, dtype, bw, col=lambda c: 0):
    return ("tile", width, dtype, bw, col)


def _out_acc(rows, width, bw, col=lambda c: 0):
    return ("acc", rows, width, bw, col)


def _rows_call(name, body, n_rows, tm, ncol, ins, outs, plans=None):
    in_specs, args = [], []
    for e in ins:
        if e[0] == "tile":
            _, arr, bw, col = e
            in_specs.append(pl.BlockSpec((tm, bw), functools.partial(lambda c, i, col: (i, col(c)), col=col)))
        else:
            arr = e[1]
            in_specs.append(pl.BlockSpec(arr.shape, functools.partial(lambda c, i, nd: (0,) * nd, nd=arr.ndim)))
        args.append(arr)
    out_specs, out_shape = [], []
    for e in outs:
        if e[0] == "tile":
            _, width, dtype, bw, col = e
            out_specs.append(pl.BlockSpec((tm, bw), functools.partial(lambda c, i, col: (i, col(c)), col=col)))
            out_shape.append(_sds((n_rows, width), dtype))
        else:
            _, rows, width, bw, col = e
            out_specs.append(pl.BlockSpec((rows, bw), functools.partial(lambda c, i, col: (0, col(c)), col=col)))
            out_shape.append(_sds((rows, width), F32))
    out, carried = _call(body, plans, name=name, grid=(ncol, n_rows // tm), in_specs=in_specs, out_specs=out_specs,
                         out_shape=out_shape, args=args)
    return out if plans is None else (out, carried)


def _acc(ref, val):
    i = pl.program_id(1)

    @pl.when(i == 0)
    def _():
        ref[...] = val

    @pl.when(i > 0)
    def _():
        ref[...] += val


def _rinv(z):
    return lax.rsqrt(jnp.mean(z * z, axis=-1, keepdims=True) + EPS)


def _norm_bwd(dy, zhat, r, w):
    dyw = dy * w
    return r * (dyw - zhat * jnp.mean(dyw * zhat, axis=-1, keepdims=True))


def _prenorm_bwd(dh, xin, w, dres, name, plans=None):
    def body(dh_ref, x_ref, w_ref, dres_ref, dx_ref, dw_ref):
        xv = x_ref[...]
        r = _rinv(xv)
        xhat = xv * r
        dhv = dh_ref[...]
        dx_ref[...] = dres_ref[...] + _norm_bwd(dhv, xhat, r, w_ref[...])
        _acc(dw_ref, jnp.sum(dhv * xhat, axis=0, keepdims=True))

    return _rows_call(name, body, xin.shape[0], 512, 1,
                      [_tile(dh, D_MODEL), _tile(xin, D_MODEL), _full(w), _tile(dres, D_MODEL)],
                      [_out_tile(D_MODEL, F32, D_MODEL), _out_acc(1, D_MODEL, D_MODEL)], plans)


def _postnorm_bwd(dout, z, w, w_mat, name, plans=None):
    def body(do_ref, z_ref, w_ref, wm_ref, dz_ref, dm_ref, dw_ref):
        zv = z_ref[...]
        r = _rinv(zv)
        zhat = zv * r
        dov = do_ref[...]
        dz = _bf(_norm_bwd(dov, zhat, r, w_ref[...]))
        dz_ref[...] = dz
        dm_ref[...] = _dot(dz, wm_ref[...], NT)
        _acc(dw_ref, jnp.sum(dov * zhat, axis=0, keepdims=True))

    return _rows_call(name, body, z.shape[0], 512, 1,
                      [_tile(dout, D_MODEL), _tile(z, D_MODEL), _full(w), _full(w_mat)],
                      [_out_tile(D_MODEL, BF16, D_MODEL), _out_tile(D_MODEL, F32, D_MODEL),
                       _out_acc(1, D_MODEL, D_MODEL)], plans)


def _t5_bucket(dist):
    n = jnp.maximum(dist, 0)
    nf = jnp.maximum(n, 1).astype(F32)
    large = MAX_EXACT + (jnp.log(nf / MAX_EXACT) / math.log(MAX_DISTANCE / MAX_EXACT)
                         * (NUM_BUCKETS - MAX_EXACT)).astype(jnp.int32)
    large = jnp.minimum(large, NUM_BUCKETS - 1)
    return jnp.where(n < MAX_EXACT, n, large)


def _band_rel():
    return jnp.arange(BLK)[:, None] + BLK - jnp.arange(2 * BLK)[None, :]


def _band_valid():
    rel = _band_rel()
    window = (rel >= 0) & (rel <= BLK)
    first = window & (jnp.arange(2 * BLK)[None, :] >= BLK)
    return jnp.stack([first, window]).astype(F32).reshape(2, 1, BAND)


RES_UNROLL = 8
PAIR = LANES // HEAD_DIM


def _pair_lanes():
    first = lax.broadcasted_iota(jnp.int32, (1, LANES), 1) < HEAD_DIM
    return first, jnp.logical_not(first)


def _heads_per_step(d):
    return HEADS if d == 1 else LANES // HEAD_DIM


def _sub_rows(r, d):
    return pl.ds(r, BLK, stride=d) if d > 1 else pl.ds(0, BLK)


def _for_residues(d, fn):
    if d <= RES_UNROLL:
        for r in range(d):
            fn(r)
    else:
        def group(i, carry):
            for k in range(RES_UNROLL):
                fn(i * RES_UNROLL + k)
            return carry

        lax.fori_loop(0, d // RES_UNROLL, group, 0)


def _attn_specs(d, g, qblock):
    cw = _heads_per_step(d) * HEAD_DIM

    def col(part, hp):
        return (g * 3 + part) * (GROUP_W // cw) + hp

    def cur(part):
        return pl.BlockSpec((d * BLK, cw), lambda hp, n: (qblock(n), col(part, hp)))

    def prev(part):
        return pl.BlockSpec((d * BLK, cw), lambda hp, n: (jnp.maximum(qblock(n) - 1, 0), col(part, hp)))

    return cur, prev


def _attn_fwd(proj, bias, g, name, plans=None):
    S = proj.shape[0]
    d = DILATIONS[g]
    NB = S // (d * BLK)
    hps = _heads_per_step(d)

    def body(q_ref, kp_ref, kc_ref, vp_ref, vc_ref, b_ref, o_ref, lse_ref):
        hp = pl.program_id(0)
        later = jnp.minimum(pl.program_id(1), 1)

        def residue(r):
            rows = _sub_rows(r, d)
            q2 = q_ref[rows, :]
            k2 = jnp.concatenate([kp_ref[rows, :], kc_ref[rows, :]], axis=0)
            v2 = jnp.concatenate([vp_ref[rows, :], vc_ref[rows, :]], axis=0)
            outs, lses = [], []
            for pp in range(hps // PAIR):
                ps = slice(pp * LANES, (pp + 1) * LANES)
                qp, kp, vp = _bf(q2[:, ps]), _bf(k2[:, ps]), _bf(v2[:, ps])
                o_h, lse_h = [], []
                for hh, own in enumerate(_pair_lanes()):
                    s = _dot(qp, jnp.where(own, kp, 0), NT) * (HEAD_DIM ** -0.5) + b_ref[later, hp * hps + pp * PAIR + hh]
                    m = jnp.max(s, axis=-1, keepdims=True)
                    p = jnp.exp(s - m)
                    l = jnp.sum(p, axis=-1, keepdims=True)
                    o_h.append(_dot(_bf(p), vp, NN) / l)
                    lse_h.append(m + jnp.log(l))
                first = _pair_lanes()[0]
                outs.append(jnp.where(first, o_h[0], o_h[1]))
                lses.append(jnp.where(first, lse_h[0], lse_h[1]))
            o_ref[rows, :] = outs[0] if len(outs) == 1 else jnp.concatenate(outs, axis=1)
            lse_ref[rows, :] = lses[0] if len(lses) == 1 else jnp.concatenate(lses, axis=1)

        _for_residues(d, residue)

    cur, prev = _attn_specs(d, g, lambda n: n)
    out = pl.BlockSpec((d * BLK, hps * HEAD_DIM), lambda hp, n: (n, hp))
    res, carried = _call(
        body, plans, name=name, grid=(HEADS // hps, NB),
        in_specs=[cur(0), prev(1), cur(1), prev(2), cur(2),
                  pl.BlockSpec((2, HEADS, BLK, 2 * BLK), lambda hp, n: (0, 0, 0, 0))],
        out_specs=[out, out], out_shape=[_sds((S, GROUP_W), F32)] * 2,
        args=(proj, proj, proj, proj, proj, bias))
    return res if plans is None else (res, carried)


def _attn_bwd(proj, bias, lse, y, dy, g, name, plans=None):
    S = proj.shape[0]
    d = DILATIONS[g]
    NB = S // (d * BLK)
    hps = _heads_per_step(d)

    def body(q_ref, kp_ref, kc_ref, vp_ref, vc_ref, b_ref, l_ref, y_ref, dy_ref,
             dq_ref, dk_ref, dv_ref, db_ref, ck_ref, cv_ref):
        hp, n = pl.program_id(0), pl.program_id(1)

        @pl.when((hp == 0) & (n == 0))
        def _():
            db_ref[...] = jnp.zeros_like(db_ref)

        @pl.when(n == 0)
        def _():
            ck_ref[...] = jnp.zeros_like(ck_ref)
            cv_ref[...] = jnp.zeros_like(cv_ref)

        @pl.when(n < NB)
        def _():
            later = jnp.minimum(n, 1)

            def residue(r):
                rows = _sub_rows(r, d)
                q2 = q_ref[rows, :]
                k2 = jnp.concatenate([kp_ref[rows, :], kc_ref[rows, :]], axis=0)
                v2 = jnp.concatenate([vp_ref[rows, :], vc_ref[rows, :]], axis=0)
                l2, y2, dy2 = l_ref[rows, :], y_ref[rows, :], dy_ref[rows, :]
                dqs, dks, dvs = [], [], []
                for pp in range(hps // PAIR):
                    ps = slice(pp * LANES, (pp + 1) * LANES)
                    qp, kp, vp = _bf(q2[:, ps]), _bf(k2[:, ps]), _bf(v2[:, ps])
                    dyp, yp = dy2[:, ps], y2[:, ps]
                    dq_h, dk_h, dv_h = [], [], []
                    for hh, own in enumerate(_pair_lanes()):
                        head = hp * hps + pp * PAIR + hh
                        s = _dot(qp, jnp.where(own, kp, 0), NT) * (HEAD_DIM ** -0.5) + b_ref[later, head]
                        p = jnp.exp(s - l2[:, pp * LANES + hh * HEAD_DIM:pp * LANES + hh * HEAD_DIM + 1])
                        dyh = jnp.where(own, dyp, 0.0)
                        delta = jnp.sum(dyh * yp, axis=-1, keepdims=True)
                        ds = p * (_dot(_bf(dyh), vp, NT) - delta)
                        db_ref[head] += ds
                        dsb = _bf(ds * (HEAD_DIM ** -0.5))
                        dq_h.append(_dot(dsb, kp, NN))
                        dk_h.append(_dot(dsb, qp, TN))
                        dv_h.append(_dot(_bf(p), _bf(dyp), TN))
                    first = _pair_lanes()[0]
                    dqs.append(jnp.where(first, dq_h[0], dq_h[1]))
                    dks.append(jnp.where(first, dk_h[0], dk_h[1]))
                    dvs.append(jnp.where(first, dv_h[0], dv_h[1]))
                dkb = dks[0] if len(dks) == 1 else jnp.concatenate(dks, axis=1)
                dvb = dvs[0] if len(dvs) == 1 else jnp.concatenate(dvs, axis=1)
                dq_ref[rows, :] = dqs[0] if len(dqs) == 1 else jnp.concatenate(dqs, axis=1)
                dk_ref[rows, :] = ck_ref[rows, :] + dkb[:BLK]
                dv_ref[rows, :] = cv_ref[rows, :] + dvb[:BLK]
                ck_ref[rows, :] = dkb[BLK:]
                cv_ref[rows, :] = dvb[BLK:]

            _for_residues(d, residue)

        @pl.when(n == NB)
        def _():
            dk_ref[...] = ck_ref[...]
            dv_ref[...] = cv_ref[...]

    def qn(n):
        return jnp.minimum(n, NB - 1)

    cur, prev = _attn_specs(d, g, qn)
    cw = hps * HEAD_DIM
    row = pl.BlockSpec((d * BLK, cw), lambda hp, n: (qn(n), hp))
    done = pl.BlockSpec((d * BLK, cw), lambda hp, n: (jnp.maximum(n - 1, 0), hp))
    (dq, dk, dv, db), carried = _call(
        body, plans, name=name, grid=(HEADS // hps, NB + 1),
        in_specs=[cur(0), prev(1), cur(1), prev(2), cur(2),
                  pl.BlockSpec((2, HEADS, BLK, 2 * BLK), lambda hp, n: (0, 0, 0, 0)), row, row, row],
        out_specs=[row, done, done, pl.BlockSpec((HEADS, BLK, 2 * BLK), lambda hp, n: (0, 0, 0))],
        out_shape=[_sds((S, GROUP_W), F32)] * 3 + [_sds((HEADS, BLK, 2 * BLK), F32)],
        scratch_shapes=[pltpu.VMEM((d * BLK, cw), F32)] * 2,
        args=(proj, proj, proj, proj, proj, bias, lse, y, dy))
    return ([dq, dk, dv], db) if plans is None else ([dq, dk, dv], db, carried)


BAND = BLK * 2 * BLK


def _bucket_onehot():
    buckets = jnp.stack([_t5_bucket(_band_rel() * d) for d in DILATIONS]).reshape(N_GROUPS, 1, BAND)
    return (buckets == jnp.arange(NUM_BUCKETS).reshape(1, NUM_BUCKETS, 1)).astype(F32)


def _relbias_fwd(rel_bias, name):
    table = rel_bias.reshape(NUM_BUCKETS, N_GROUPS, HEADS).transpose(1, 0, 2)

    def body(t_ref, oh_ref, valid_ref, o_ref):
        bias = lax.dot_general(t_ref[...], oh_ref[...], (TN, ((), ())), preferred_element_type=F32,
                               precision=lax.Precision.HIGHEST)
        for k in range(2):
            o_ref[k] = jnp.where(valid_ref[k] > 0.5, bias, NEG_INF)

    out = pl.pallas_call(
        body, name=name, grid=(N_GROUPS,),
        in_specs=[pl.BlockSpec((None, NUM_BUCKETS, HEADS), lambda g: (g, 0, 0)),
                  pl.BlockSpec((None, NUM_BUCKETS, BAND), lambda g: (g, 0, 0)),
                  pl.BlockSpec((2, 1, BAND), lambda g: (0, 0, 0))],
        out_specs=pl.BlockSpec((None, 2, HEADS, BAND), lambda g: (g, 0, 0, 0)),
        out_shape=_sds((N_GROUPS, 2, HEADS, BAND), F32), compiler_params=_cp(1))(table, _bucket_onehot(), _band_valid())
    return out.reshape(N_GROUPS, 2, HEADS, BLK, 2 * BLK)


def _relbias_bwd(dbs, name):
    band = BAND
    onehot = _bucket_onehot()
    dbf = jnp.stack([db.reshape(HEADS, band) for db in dbs])

    def body(oh_ref, db_ref, o_ref):
        o_ref[...] = lax.dot_general(oh_ref[...], db_ref[...], (NT, ((), ())), preferred_element_type=F32,
                                     precision=lax.Precision.HIGHEST)

    out = pl.pallas_call(
        body, name=name, grid=(N_GROUPS,),
        in_specs=[pl.BlockSpec((None, NUM_BUCKETS, band), lambda g: (g, 0, 0)),
                  pl.BlockSpec((None, HEADS, band), lambda g: (g, 0, 0))],
        out_specs=pl.BlockSpec((None, NUM_BUCKETS, HEADS), lambda g: (g, 0, 0)),
        out_shape=_sds((N_GROUPS, NUM_BUCKETS, HEADS), F32), compiler_params=_cp(1))(onehot, dbf)
    return out.transpose(1, 0, 2).reshape(NUM_BUCKETS, N_GROUPS * HEADS)


def _chunk_pos(shape):
    return lax.broadcasted_iota(jnp.int32, shape, 0) % HG_CHUNK


def _chunk_cumsum(v):
    pos = _chunk_pos(v.shape)
    s = 1
    while s < HG_CHUNK:
        v = v + jnp.where(pos >= s, pltpu.roll(v, s, 0), 0.0)
        s *= 2
    return v


def _chunk_rev_cumsum(v):
    pos = _chunk_pos(v.shape)
    n = v.shape[0]
    s = 1
    while s < HG_CHUNK:
        v = v + jnp.where(pos < HG_CHUNK - s, pltpu.roll(v, n - s, 0), 0.0)
        s *= 2
    return v


def _lower_bound(raw):
    a0, a1 = raw[0:1], raw[1:2]
    m = jnp.maximum(a0, a1)
    e0, e1 = jnp.exp(a0 - m), jnp.exp(a1 - m)
    return e0 / (e0 + e1)


def _hg_gates(qr, fr, lb):
    sf = _sigmoid(fr)
    f = lb + (1.0 - lb) * sf
    sq = _sigmoid(qr)
    return qr * sq, sq, f, sf


HG_COL0 = QKV_W // HG_W


def _hgrn_fwd(proj, lb_raw, nw, name, plans=None):
    S = proj.shape[0]
    ncs = HG_TILE // HG_CHUNK

    def body(q_ref, f_ref, i_ref, og_ref, lb_ref, nw_ref, y_ref, o_ref, st_ref, state):
        @pl.when(pl.program_id(0) == 0)
        def _():
            state[...] = jnp.zeros_like(state)

        lb = _lower_bound(lb_ref[...])
        q, _, f, _ = _hg_gates(q_ref[...], f_ref[...], lb)
        k = 1.0 - f
        G = _chunk_cumsum(jnp.log(f))
        row = lax.broadcasted_iota(jnp.int32, (HG_CHUNK, HG_CHUNK), 0)
        col = lax.broadcasted_iota(jnp.int32, (HG_CHUNK, HG_CHUNK), 1)
        heads = [slice(h * HG_DK, (h + 1) * HG_DK) for h in range(HG_HEADS)]
        sts = [state[h] for h in range(HG_HEADS)]
        for c in range(ncs):
            cs = slice(c * HG_CHUNK, (c + 1) * HG_CHUNK)
            for h, hs in enumerate(heads):
                Gc = G[cs, hs]
                gl = Gc[HG_CHUNK - 1:HG_CHUNK]
                qt = _bf(q[cs, hs] * jnp.exp(Gc))
                kt = _bf(k[cs, hs] * jnp.exp(-Gc))
                kd = _bf(k[cs, hs] * jnp.exp(gl - Gc))
                v = _bf(i_ref[cs, hs])
                A = jnp.where(row >= col, _dot(qt, kt, NT), 0.0)
                o_ref[cs, hs] = _dot(_bf(A), v, NN) + _dot(qt, _bf(sts[h]), NT)
                st_ref[c, h] = sts[h]
                sts[h] = sts[h] * jnp.exp(gl) + _dot(v, kd, TN)
        for h, hs in enumerate(heads):
            state[h] = sts[h]
            oh = o_ref[:, hs]
            og = og_ref[:, hs]
            y_ref[:, hs] = oh * _rinv(oh) * nw_ref[...] * (og * _sigmoid(og))

    def colspec(j):
        return pl.BlockSpec((HG_TILE, HG_W), lambda i: (i, HG_COL0 + j))

    res, carried = _call(
        body, plans, name=name, grid=(S // HG_TILE,),
        in_specs=[colspec(0), colspec(1), colspec(2), colspec(3),
                  pl.BlockSpec((2, HG_W), lambda i: (0, 0)), pl.BlockSpec((1, HG_DK), lambda i: (0, 0))],
        out_specs=[pl.BlockSpec((HG_TILE, HG_W), lambda i: (i, 0))] * 2
        + [pl.BlockSpec((ncs, HG_HEADS, HG_DK, HG_DK), lambda i: (i, 0, 0, 0))],
        out_shape=[_sds((S, HG_W), F32)] * 2 + [_sds((S // HG_CHUNK, HG_HEADS, HG_DK, HG_DK), F32)],
        scratch_shapes=[pltpu.VMEM((HG_HEADS, HG_DK, HG_DK), F32)],
        args=(proj, proj, proj, proj, lb_raw, nw))
    return res if plans is None else (res, carried)


def _hgrn_bwd(proj, lb_raw, nw, o, states, dy, d_attn, d_gates, name):
    S = proj.shape[0]
    ncs = HG_TILE // HG_CHUNK
    nt = S // HG_TILE
    n_a, n_g = len(d_attn), len(d_gates)
    own = [slice(QKV_W + j * HG_W, QKV_W + (j + 1) * HG_W) for j in range(4)]

    def body(q_ref, f_ref, i_ref, og_ref, lb_ref, nw_ref, o_ref, st_ref, dy_ref, *rest):
        attn_refs, gate_refs = rest[:n_a], rest[n_a:n_a + n_g]
        dp_ref, dlb_ref, dnw_ref, dstate, do_s, dG_s, dgl_s, dk_s, dlb_s = rest[n_a + n_g:]
        dq_ref, df_ref, di_ref, dog_ref = (dp_ref.at[:, cols] for cols in own)
        step = pl.program_id(0)
        for k, a_ref in enumerate(attn_refs):
            dp_ref[:, k * GROUP_W:(k + 1) * GROUP_W] = _bf(a_ref[...])
        for k, g_ref in enumerate(gate_refs):
            dp_ref[:, QKV_W + 4 * HG_W + k * D_MODEL:QKV_W + 4 * HG_W + (k + 1) * D_MODEL] = g_ref[...]

        @pl.when(step == 0)
        def _():
            dstate[...] = jnp.zeros_like(dstate)
            dlb_s[...] = jnp.zeros_like(dlb_s)
            dnw_ref[...] = jnp.zeros_like(dnw_ref)

        lb = _lower_bound(lb_ref[...])
        qr = q_ref[...]
        q, sq, f, sf = _hg_gates(qr, f_ref[...], lb)
        k = 1.0 - f
        G = _chunk_cumsum(jnp.log(f))
        nwv = nw_ref[...]
        row = lax.broadcasted_iota(jnp.int32, (HG_CHUNK, HG_CHUNK), 0)
        col = lax.broadcasted_iota(jnp.int32, (HG_CHUNK, HG_CHUNK), 1)
        for h in range(HG_HEADS):
            hs = slice(h * HG_DK, (h + 1) * HG_DK)
            oh = o_ref[:, hs]
            r = _rinv(oh)
            ohat = oh * r
            og = og_ref[:, hs]
            sg = _sigmoid(og)
            dyh = dy_ref[:, hs]
            don = dyh * (og * sg)
            dog_ref[:, hs] = _bf(dyh * (ohat * nwv) * (sg * (1.0 + og * (1.0 - sg))))
            dnw_ref[...] += jnp.sum(don * ohat, axis=0, keepdims=True)
            do_s[:, hs] = _norm_bwd(don, ohat, r, nwv)
        dsts = [dstate[h] for h in range(HG_HEADS)]
        for c in reversed(range(ncs)):
            cs = slice(c * HG_CHUNK, (c + 1) * HG_CHUNK)
            for h in range(HG_HEADS):
                hs = slice(h * HG_DK, (h + 1) * HG_DK)
                dst = dsts[h]
                Gc = G[cs, hs]
                gl = Gc[HG_CHUNK - 1:HG_CHUNK]
                eG, enG, edG, egl = jnp.exp(Gc), jnp.exp(-Gc), jnp.exp(gl - Gc), jnp.exp(gl)
                qt, kt, kd = q[cs, hs] * eG, k[cs, hs] * enG, k[cs, hs] * edG
                qtb, ktb, kdb = _bf(qt), _bf(kt), _bf(kd)
                v = _bf(i_ref[cs, hs])
                do = _bf(do_s[cs, hs])
                st = st_ref[c, h]
                dstb = _bf(dst)
                A = jnp.where(row >= col, _dot(qtb, ktb, NT), 0.0)
                dA = _bf(jnp.where(row >= col, _dot(do, v, NT), 0.0))
                di_ref[cs, hs] = _bf(_dot(_bf(A), do, TN) + _dot(kdb, dstb, NT))
                dqt = _dot(dA, ktb, NN) + _dot(do, _bf(st), NN)
                dkt = _dot(dA, qtb, TN)
                dkd = _dot(v, dstb, NN)
                dgl = egl * jnp.sum(st * dst, axis=0, keepdims=True) + jnp.sum(dkd * kd, axis=0, keepdims=True)
                dsts[h] = dst * egl + _dot(do, qtb, TN)
                dq_ref[cs, hs] = _bf(dqt * eG * (sq[cs, hs] * (1.0 + qr[cs, hs] * (1.0 - sq[cs, hs]))))
                dk_s[cs, hs] = dkt * enG + dkd * edG
                dG_s[cs, hs] = dqt * qt - dkt * kt - dkd * kd
                dgl_s[cs, hs] = jnp.broadcast_to(dgl, (HG_CHUNK, HG_DK))
        for h in range(HG_HEADS):
            dstate[h] = dsts[h]
        dg = _chunk_rev_cumsum(dG_s[...]) + dgl_s[...]
        dfv = dg / f - dk_s[...]
        df_ref[...] = _bf(dfv * (1.0 - lb) * sf * (1.0 - sf))
        dlb_s[...] += jnp.sum(dfv * (1.0 - sf), axis=0, keepdims=True)

        @pl.when(step == nt - 1)
        def _():
            t = dlb_s[...] * lb * (1.0 - lb)
            dlb_ref[...] = jnp.concatenate([t, -t], axis=0)

    def colspec(j):
        return pl.BlockSpec((HG_TILE, HG_W), lambda i: (nt - 1 - i, HG_COL0 + j))

    def rows(width):
        return pl.BlockSpec((HG_TILE, width), lambda i: (nt - 1 - i, 0))

    tile = rows(HG_W)
    return pl.pallas_call(
        body, name=name, grid=(nt,),
        in_specs=[colspec(0), colspec(1), colspec(2), colspec(3),
                  pl.BlockSpec((2, HG_W), lambda i: (0, 0)), pl.BlockSpec((1, HG_DK), lambda i: (0, 0)),
                  tile, pl.BlockSpec((ncs, HG_HEADS, HG_DK, HG_DK), lambda i: (nt - 1 - i, 0, 0, 0)), tile]
        + [rows(GROUP_W)] * n_a + [rows(D_MODEL)] * n_g,
        out_specs=[rows(IN_W), pl.BlockSpec((2, HG_W), lambda i: (0, 0)), pl.BlockSpec((1, HG_DK), lambda i: (0, 0))],
        out_shape=[_sds((S, IN_W), BF16), _sds((2, HG_W), F32), _sds((1, HG_DK), F32)],
        scratch_shapes=[pltpu.VMEM((HG_HEADS, HG_DK, HG_DK), F32)] + [pltpu.VMEM((HG_TILE, HG_W), F32)] * 4
        + [pltpu.VMEM((1, HG_W), F32)],
        compiler_params=_cp(1))(proj, proj, proj, proj, lb_raw, nw, o, states, dy, *d_attn, *d_gates)


GATE_COL0 = (QKV_W + 4 * HG_W) // GROUP_W
HALF_D = D_MODEL // 2


def _gate_tiles(proj):
    return [_tile(proj, HALF_D, functools.partial(lambda c, k: GATE_COL0 + k, k=k)) for k in range(4)]


def _gates(g_refs):
    s0 = _sigmoid(jnp.concatenate([g_refs[0][...], g_refs[1][...]], axis=1))
    s1 = _sigmoid(jnp.concatenate([g_refs[2][...], g_refs[3][...]], axis=1))
    return s0, s1


def _branch_fwd(os_, lses, yh, proj, w_a, w_h, name, plans=None):
    nb = w_a.shape[0]

    def body(o0, o1, o2, l0, l1, l2, yh_ref, g0a, g0b, g1a, g1b, wa_ref, wh_ref,
             y_ref, lse_ref, za_ref, zh_ref, m_ref):
        a, b, c = l0[...], l1[...], l2[...]
        m = jnp.maximum(jnp.maximum(a, b), c)
        ea, eb, ec = jnp.exp(a - m), jnp.exp(b - m), jnp.exp(c - m)
        den = ea + eb + ec
        y = (ea * o0[...] + eb * o1[...] + ec * o2[...]) / den
        y_ref[...] = y
        lse_ref[...] = m + jnp.log(den)
        yb, yhb = _bf(y), _bf(yh_ref[...])
        za = jnp.concatenate([_dot(yb, wa_ref[j], NN) for j in range(nb)], axis=1)
        zh = jnp.concatenate([_dot(yhb, wh_ref[j], NN) for j in range(nb)], axis=1)
        s0, s1 = _gates((g0a, g0b, g1a, g1b))
        za_ref[...] = za
        zh_ref[...] = zh
        m_ref[...] = _bf(s0 * za + s1 * zh)

    return _rows_call(name, body, yh.shape[0], 512, 1,
                      [*[_tile(t, GROUP_W) for t in (*os_, *lses)], _tile(yh, HG_W), *_gate_tiles(proj),
                       _full(w_a), _full(w_h)],
                      [_out_tile(GROUP_W, F32, GROUP_W)] * 2 + [_out_tile(D_MODEL, F32, D_MODEL)] * 2
                      + [_out_tile(D_MODEL, BF16, D_MODEL)], plans)


def _branch_bwd(dm, za, zh, proj, w_a, w_h, name, plans=None):
    nb, _, Nb = w_a.shape

    def body(dm_ref, za_ref, zh_ref, g0a, g0b, g1a, g1b, wa_ref, wh_ref,
             dza_ref, dzh_ref, dg0_ref, dg1_ref, dy_ref, dyh_ref):
        dmv = dm_ref[...]
        s0, s1 = _gates((g0a, g0b, g1a, g1b))
        dza, dzh = _bf(dmv * s0), _bf(dmv * s1)
        dza_ref[...] = dza
        dzh_ref[...] = dzh
        dg0_ref[...] = _bf(dmv * za_ref[...] * s0 * (1.0 - s0))
        dg1_ref[...] = _bf(dmv * zh_ref[...] * s1 * (1.0 - s1))
        dy_ref[...] = sum(_dot(dza[:, j * Nb:(j + 1) * Nb], wa_ref[j], NT) for j in range(nb))
        dyh_ref[...] = sum(_dot(dzh[:, j * Nb:(j + 1) * Nb], wh_ref[j], NT) for j in range(nb))

    return _rows_call(name, body, za.shape[0], 512, 1,
                      [_tile(dm, D_MODEL), _tile(za, D_MODEL), _tile(zh, D_MODEL), *_gate_tiles(proj),
                       _full(w_a), _full(w_h)],
                      [_out_tile(D_MODEL, BF16, D_MODEL)] * 4 + [_out_tile(GROUP_W, F32, GROUP_W),
                                                                 _out_tile(HG_W, F32, HG_W)], plans)


def _mix_out(merged, w_out, x, w_post, w_pre, name):
    def body(m_ref, wo_ref, x_ref, wp_ref, wf_ref, mo_ref, x1_ref, h2_ref):
        z = _dot(m_ref[...], wo_ref[...], NN)
        mo_ref[...] = z
        x1 = x_ref[...] + z * _rinv(z) * wp_ref[...]
        x1_ref[...] = x1
        h2_ref[...] = _bf(x1 * _rinv(x1) * wf_ref[...])

    return _rows_call(name, body, x.shape[0], 512, 1,
                      [_tile(merged, D_MODEL), _full(w_out), _tile(x, D_MODEL), _full(w_post), _full(w_pre)],
                      [_out_tile(D_MODEL, F32, D_MODEL), _out_tile(D_MODEL, F32, D_MODEL),
                       _out_tile(D_MODEL, BF16, D_MODEL)])


def _loss_head(a, w_down, x1, tgt, w, name):
    def body(a_ref, wd_ref, x1_ref, t_ref, w_ref, dx_ref, df_ref, dw_ref, loss_ref):
        z = _dot(a_ref[...], wd_ref[...], NN)
        r = _rinv(z)
        zhat = z * r
        wv = w_ref[...]
        e = x1_ref[...] + zhat * wv - t_ref[...]
        dx = e * (1.0 / D_MODEL)
        dx_ref[...] = dx
        df_ref[...] = _bf(_norm_bwd(dx, zhat, r, wv))
        _acc(dw_ref, jnp.sum(dx * zhat, axis=0, keepdims=True))
        part = 0.5 * jnp.sum(jnp.sum(e * e, axis=1, keepdims=True), axis=0, keepdims=True) * (1.0 / D_MODEL)
        _acc(loss_ref, jnp.broadcast_to(part, (1, LANES)))

    return _rows_call(name, body, x1.shape[0], 512, 1,
                      [_tile(a, D_FF), _full(w_down), _tile(x1, D_MODEL), _tile(tgt, D_MODEL), _full(w)],
                      [_out_tile(D_MODEL, F32, D_MODEL), _out_tile(D_MODEL, BF16, D_MODEL),
                       _out_acc(1, D_MODEL, D_MODEL), _out_acc(1, LANES, LANES)])


CONV_CB = D_FF // 2
CONV_TM = 512
HALO = 8
SQRT_HALF = 0.7071067811865476
INV_SQRT_2PI = 0.3989422804014327


CONV_RS = 32


def _lane_tiles():
    return [slice(k * LANES, (k + 1) * LANES) for k in range(CONV_CB // LANES)]


def _strip_start(i):
    return pl.multiple_of(i * CONV_RS, CONV_RS)


def _strip_taps(u_ref, halo_ref, r0, cs, first_strip, first_tile):
    if first_strip:
        before = jnp.where(first_tile, 0.0, halo_ref[:, cs])
        blk = jnp.concatenate([before, u_ref[0:CONV_RS, cs]], axis=0)
    else:
        blk = u_ref[pl.ds(pl.multiple_of(r0 - HALO, HALO), CONV_RS + HALO), cs]
    return pltpu.roll(blk, 2, 0)[HALO:], pltpu.roll(blk, 1, 0)[HALO:], blk[HALO:]


def _conv(taps, w_ref, b_ref, cs):
    return b_ref[:, cs] + w_ref[0:1, cs] * taps[0] + w_ref[1:2, cs] * taps[1] + w_ref[2:3, cs] * taps[2]


def _conv_specs(tm):
    nh = tm // HALO
    nc = D_FF // CONV_CB

    def tile(off):
        return pl.BlockSpec((tm, CONV_CB), lambda c, i: (i, off + c))

    def halo(off):
        return pl.BlockSpec((HALO, CONV_CB), lambda c, i: (jnp.maximum(i * nh - 1, 0), off + c))

    def small(rows, off):
        return pl.BlockSpec((rows, CONV_CB), lambda c, i: (0, off + c))

    return nc, tile, halo, small


def _conv_gelu_fwd(u, cw, cb, name, plans=None):
    S = u.shape[0]
    tm = CONV_TM
    nc, tile, halo, small = _conv_specs(tm)

    def body(ug, hg, uv, hv, wg, wv, bg, bv, a_ref):
        first_tile = pl.program_id(1) == 0

        def strip(r0, first_strip):
            for cs in _lane_tiles():
                cg = _conv(_strip_taps(ug, hg, r0, cs, first_strip, first_tile), wg, bg, cs)
                cv = _conv(_strip_taps(uv, hv, r0, cs, first_strip, first_tile), wv, bv, cs)
                a_ref[pl.ds(r0, CONV_RS), cs] = _bf(0.5 * cg * (1.0 + lax.erf(cg * SQRT_HALF)) * cv)

        strip(0, True)
        lax.fori_loop(1, tm // CONV_RS, lambda k, c: (strip(_strip_start(k), False), c)[1], 0)

    (a,), carried = _call(
        body, plans, name=name, grid=(nc, S // tm),
        in_specs=[tile(0), halo(0), tile(nc), halo(nc), small(3, 0), small(3, nc), small(1, 0), small(1, nc)],
        out_specs=[tile(0)], out_shape=[_sds((S, D_FF), BF16)], args=(u, u, u, u, cw, cw, cb, cb))
    return a if plans is None else (a, carried)


def _conv_gelu_bwd(u, dff, w_down, cw, cb, name, plans=None):
    S = u.shape[0]
    tm = CONV_TM
    nt = S // tm
    nc, tile, halo, small = _conv_specs(tm)

    def body(ug, hg, uv, hv, wg, wv, bg, bv, dff_ref, wd_ref, dcg_ref, dcv_ref, dwg_ref, dwv_ref, dbg_ref, dbv_ref,
             acc, da_ref):
        i = pl.program_id(1)
        first_tile = i == 0
        da_ref[...] = _dot(dff_ref[...], wd_ref[...], NT)

        @pl.when(first_tile)
        def _():
            acc[...] = jnp.zeros_like(acc)

        def strip(r0, first_strip):
            rows = pl.ds(r0, CONV_RS)
            for cs in _lane_tiles():
                tg = _strip_taps(ug, hg, r0, cs, first_strip, first_tile)
                tv = _strip_taps(uv, hv, r0, cs, first_strip, first_tile)
                cg = _conv(tg, wg, bg, cs)
                cv = _conv(tv, wv, bv, cs)
                phi = 0.5 * (1.0 + lax.erf(cg * SQRT_HALF))
                dav = da_ref[rows, cs]
                dcg = dav * cv * (phi + cg * jnp.exp(-0.5 * cg * cg) * INV_SQRT_2PI)
                dcv = dav * (cg * phi)
                dcg_ref[rows, cs] = dcg
                dcv_ref[rows, cs] = dcv
                for half, (dc, taps) in enumerate(((dcg, tg), (dcv, tv))):
                    for j in range(3):
                        acc[4 * half + j, :, cs] += dc * taps[j]
                    acc[4 * half + 3, :, cs] += dc

        strip(0, True)
        lax.fori_loop(1, tm // CONV_RS, lambda k, c: (strip(_strip_start(k), False), c)[1], 0)

        @pl.when(i == nt - 1)
        def _():
            for half, (dw_ref, db_ref) in enumerate(((dwg_ref, dbg_ref), (dwv_ref, dbv_ref))):
                for j in range(3):
                    dw_ref[j:j + 1, :] = jnp.sum(acc[4 * half + j], axis=0, keepdims=True)
                db_ref[...] = jnp.sum(acc[4 * half + 3], axis=0, keepdims=True)

    res, carried = _call(
        body, plans, name=name, grid=(nc, nt),
        in_specs=[tile(0), halo(0), tile(nc), halo(nc), small(3, 0), small(3, nc), small(1, 0), small(1, nc),
                  pl.BlockSpec((tm, D_MODEL), lambda c, i: (i, 0)), pl.BlockSpec((CONV_CB, D_MODEL), lambda c, i: (c, 0))],
        out_specs=[tile(0), tile(0), small(3, 0), small(3, 0), small(1, 0), small(1, 0)],
        out_shape=[_sds((S, D_FF), F32)] * 2 + [_sds((3, D_FF), F32)] * 2 + [_sds((1, D_FF), F32)] * 2,
        scratch_shapes=[pltpu.VMEM((8, CONV_RS, CONV_CB), F32), pltpu.VMEM((tm, CONV_CB), F32)],
        args=(u, u, u, u, cw, cw, cb, cb, dff, w_down))
    return res if plans is None else (res, carried)


def _conv_input_bwd(dcg, dcv, cw, name, plans=None):
    S = dcg.shape[0]
    tm = CONV_TM // 2
    nh = tm // HALO
    nt = S // tm
    n = CONV_RS + HALO
    tile = pl.BlockSpec((tm, D_FF), lambda i: (i, 0))
    nxt = pl.BlockSpec((HALO, D_FF), lambda i: (jnp.minimum((i + 1) * nh, S // HALO - 1), 0))

    def body(g_ref, ng_ref, v_ref, nv_ref, w_ref, du_ref):
        last_tile = pl.program_id(0) == nt - 1

        def strip(r0, last_strip):
            for half, (dc_ref, n_ref) in enumerate(((g_ref, ng_ref), (v_ref, nv_ref))):
                for k in range(D_FF // LANES):
                    cs = slice(k * LANES, (k + 1) * LANES)
                    ws = slice(half * D_FF + k * LANES, half * D_FF + (k + 1) * LANES)
                    if last_strip:
                        after = jnp.where(last_tile, 0.0, n_ref[:, cs])
                        blk = jnp.concatenate([dc_ref[tm - CONV_RS:tm, cs], after], axis=0)
                    else:
                        blk = dc_ref[pl.ds(r0, n), cs]
                    d1 = pltpu.roll(blk, n - 1, 0)[:CONV_RS]
                    d2 = pltpu.roll(blk, n - 2, 0)[:CONV_RS]
                    du_ref[pl.ds(r0, CONV_RS), ws] = _bf(w_ref[2:3, ws] * blk[:CONV_RS] + w_ref[1:2, ws] * d1
                                                         + w_ref[0:1, ws] * d2)

        lax.fori_loop(0, tm // CONV_RS - 1, lambda k, c: (strip(_strip_start(k), False), c)[1], 0)
        strip(tm - CONV_RS, True)

    (du,), carried = _call(
        body, plans, name=name, grid=(nt,),
        in_specs=[tile, nxt, tile, nxt, pl.BlockSpec((3, 2 * D_FF), lambda i: (0, 0))],
        out_specs=[pl.BlockSpec((tm, 2 * D_FF), lambda i: (i, 0))], out_shape=[_sds((S, 2 * D_FF), BF16)],
        args=(dcg, dcg, dcv, dcv, cw))
    return du if plans is None else (du, carried)


def _row_tile(n, cap):
    best = n
    for t in range(16, cap + 1, 16):
        if n % t == 0:
            best = t
    return best if best <= cap else n


def _rows_for_bytes(nbytes, cols):
    return max(16, nbytes // (4 * cols) // 16 * 16)


def _adamw_update(w_ref, g_ref, m_ref, v_ref, d_ref, nm_ref, nv_ref):
    gv = g_ref[...]
    nm = ADAM_B1 * m_ref[...] + (1.0 - ADAM_B1) * gv
    nv = ADAM_B2 * v_ref[...] + (1.0 - ADAM_B2) * (gv * gv)
    m_hat = nm / (1.0 - ADAM_B1 ** ADAM_STEP)
    v_hat = nv / (1.0 - ADAM_B2 ** ADAM_STEP)
    d_ref[...] = -ADAM_LR * (m_hat / (jnp.sqrt(v_hat) + ADAM_EPS) + ADAM_WD * w_ref[...])
    nm_ref[...] = nm
    nv_ref[...] = nv


def _adamw(w, g, m, v, name):
    R, C = w.shape
    tr = _row_tile(R, _rows_for_bytes(2 << 20, C))
    spec = pl.BlockSpec((tr, C), lambda i: (i, 0))
    body = functools.partial(_adamw_update)
    return pl.pallas_call(body, name=name, grid=(R // tr,), in_specs=[spec] * 4, out_specs=[spec] * 3,
                          out_shape=[_sds((R, C), F32)] * 3, compiler_params=_cp(1))(w, g, m, v)


def _adamw_small(ws, gs, ms, vs, name):
    n = len(ws)

    def body(*refs):
        ins, outs = refs[:4 * n], refs[4 * n:]
        for k in range(n):
            _adamw_update(ins[k], ins[n + k], ins[2 * n + k], ins[3 * n + k], outs[3 * k], outs[3 * k + 1], outs[3 * k + 2])

    vm = pl.BlockSpec(memory_space=pltpu.VMEM)
    outs = pl.pallas_call(body, name=name, in_specs=[vm] * (4 * n), out_specs=[vm] * (3 * n),
                          out_shape=[_sds(w.shape, F32) for w in ws for _ in range(3)])(*ws, *gs, *ms, *vs)
    return [tuple(outs[3 * k:3 * k + 3]) for k in range(n)]


def _pair_sum(gfull, rcv, c_idx, name):
    nb, R, C = gfull.shape
    half = R // 2
    tr = _row_tile(half, _rows_for_bytes(2 << 20, C))
    nt = half // tr

    def body(c_ref, g_ref, r_ref, o_ref):
        o_ref[...] = _bf(g_ref[...] + r_ref[...])

    return pl.pallas_call(
        body, name=name,
        grid_spec=pltpu.PrefetchScalarGridSpec(
            num_scalar_prefetch=1, grid=(nb, nt),
            in_specs=[pl.BlockSpec((None, tr, C), lambda j, i, c_ref: (j, c_ref[0] * nt + i, 0)),
                      pl.BlockSpec((None, tr, C), lambda j, i, c_ref: (j, i, 0))],
            out_specs=pl.BlockSpec((None, tr, C), lambda j, i, c_ref: (j, i, 0))),
        out_shape=_sds((nb, half, C), BF16), compiler_params=_cp(2))(c_idx, gfull, rcv)


def _chip_sum(arrived, own, place, name):
    nb, H, C = arrived.shape
    tr = _row_tile(H, _rows_for_bytes(2 << 20, C))
    nt = H // tr

    def body(pl_ref, *refs):
        o_ref = refs[nb + 1]
        me = pl_ref[0]
        acc = None
        for k in range(nb):
            term = jnp.where(me == k, refs[nb][...], refs[k][...]).astype(F32)
            acc = term if acc is None else acc + term
        o_ref[...] = acc

    def other(k):
        return pl.BlockSpec((None, tr, C), lambda i, p: (jnp.where(p[0] == k, (k + 1) % nb, k), i, 0))

    return pl.pallas_call(
        body, name=name,
        grid_spec=pltpu.PrefetchScalarGridSpec(
            num_scalar_prefetch=1, grid=(nt,),
            in_specs=[other(k) for k in range(nb)] + [pl.BlockSpec((None, tr, C), lambda i, p: (p[0], i, 0))],
            out_specs=pl.BlockSpec((tr, C), lambda i, p: (p[1] * nt + i, 0))),
        out_shape=_sds((2 * H, C), F32), compiler_params=_cp(1))(place, *([arrived] * nb), own)


def _cast_into_slot(shard, place, name):
    R, C = shard.shape
    tr = _row_tile(R, 256)

    def body(pl_ref, s_ref, o_ref):
        o_ref[...] = _bf(s_ref[...])

    return pl.pallas_call(
        body, name=name,
        grid_spec=pltpu.PrefetchScalarGridSpec(
            num_scalar_prefetch=1, grid=(R // tr,),
            in_specs=[pl.BlockSpec((tr, C), lambda i, p: (i, 0))],
            out_specs=pl.BlockSpec((None, tr, C), lambda i, p: (p[0], i, 0))),
        out_shape=_sds((N_CHIPS, R, C), BF16), compiler_params=_cp(1))(place, shard)


def _place():
    x, y, c = lax.axis_index("x"), lax.axis_index("y"), lax.axis_index("c")
    chips = [(1 - x, y), (x, 1 - y), (1 - x, 1 - y)]
    return x, y, c, chips


def _chip_id(px, py):
    return 2 * px + py


def _remote(src, dst, send_sems, recv_sems, k, to):
    return pltpu.make_async_remote_copy(src_ref=src, dst_ref=dst, send_sem=send_sems.at[k], recv_sem=recv_sems.at[k],
                                        device_id=to, device_id_type=MESH)


def _proj_gathered(x, w_norm, slot, place, name, tm=1024, plan=None):
    M, K = x.shape
    nb, _, Nb = slot.shape
    half = K // 2
    nt = M // tm
    cx, cy = place[0] // 2, place[0] % 2
    order = jnp.stack([place[0], _chip_id(1 - cx, cy), _chip_id(cx, 1 - cy), _chip_id(1 - cx, 1 - cy)]).astype(jnp.int32)

    p_in = [] if plan is None else plan.ins + plan.inouts
    p_out = [] if plan is None else [_sds(a.shape, a.dtype) for a in plan.inouts] + plan.outs
    n_pi, n_po = len(p_in), len(p_out)

    def body(order_ref, x_ref, wn_ref, slot_in, *refs):
        o_ref, slot_ref, h_out = refs[n_pi:n_pi + 3]
        s0 = n_pi + 3 + n_po
        w_buf, hs, ici_send, ici_recv, pass_send, pass_recv, load_sem = refs[s0:s0 + 7]

        def carried():
            if plan is None:
                return [], [], []
            ins = refs[:len(plan.ins)]
            outs = refs[n_pi + 3:n_pi + 3 + n_po]
            return plan.copies(ins, outs[:len(plan.inouts)], outs[len(plan.inouts):], *refs[s0 + 7:])

        b, i = pl.program_id(0), pl.program_id(1)
        x, y, c, chips = _place()
        me = _chip_id(x, y)
        sib = (x, y, 1 - c)
        mine, other = pl.ds(c * half, half), pl.ds((1 - c) * half, half)

        def sent(k):
            blk = slot_ref.at[me, mine]
            return _remote(blk, blk, ici_send, ici_recv, k, (*chips[k], c))

        def landed(k):
            blk = slot_ref.at[_chip_id(*chips[k]), mine]
            return _remote(blk, blk, ici_send, ici_recv, k, (*chips[k], c))

        def passed(k, rows):
            blk = slot_ref.at[_chip_id(*chips[k]), rows]
            return _remote(blk, blk, pass_send, pass_recv, k, sib)

        @pl.when((b == 0) & (i == 0))
        def _():
            for k in range(len(chips)):
                sent(k).start()
            sends, _, local = carried()
            for cp in (*sends, *local):
                cp.start()

        for k in range(len(chips)):
            @pl.when((b == k + 1) & (i == 0))
            def _(k=k):
                landed(k).wait_recv()
                passed(k, mine).start()
                passed(k, other).wait_recv()

        @pl.when(i == 0)
        def _():
            load = pltpu.make_async_copy(slot_ref.at[order_ref[b]], w_buf, load_sem.at[0])
            load.start()
            load.wait()

        rows = pl.ds(pl.multiple_of(i * tm, tm), tm)
        keep_h = pltpu.make_async_copy(hs, h_out, load_sem.at[1])

        @pl.when(b == 0)
        def _():
            xv = x_ref[...]
            hs[rows, :] = _bf(xv * _rinv(xv) * wn_ref[...])

        @pl.when((b == 1) & (i == 0))
        def _():
            keep_h.start()

        o_ref[...] = _dot(hs[rows, :], w_buf[...], NN)

        @pl.when((b == nb - 1) & (i == nt - 1))
        def _():
            for k in range(len(chips)):
                sent(k).wait_send()
                passed(k, mine).wait_send()
            sends, recvs, local = carried()
            for cp in recvs:
                cp.wait_recv()
            for cp in sends:
                cp.wait_send()
            for cp in local:
                cp.wait()
            keep_h.wait()

    n_peers = N_CHIPS - 1
    return pl.pallas_call(
        body, name=name,
        grid_spec=pltpu.PrefetchScalarGridSpec(
            num_scalar_prefetch=1, grid=(nb, nt),
            in_specs=[pl.BlockSpec((tm, K), lambda b, i, o: (i, 0)), pl.BlockSpec((1, K), lambda b, i, o: (0, 0)), ANY]
            + [ANY] * n_pi,
            out_specs=[pl.BlockSpec((tm, Nb), lambda b, i, o: (i, o[b])), ANY, ANY] + [ANY] * n_po,
            scratch_shapes=[pltpu.VMEM((K, Nb), BF16), pltpu.VMEM((M, K), BF16)]
            + [pltpu.SemaphoreType.DMA((n_peers,))] * 4 + [pltpu.SemaphoreType.DMA((2,))]
            + ([] if plan is None else [pltpu.SemaphoreType.DMA((plan.n_sems,))] * 3)),
        out_shape=[_sds((M, nb * Nb), F32), _sds(slot.shape, slot.dtype), _sds((M, K), BF16)] + p_out,
        input_output_aliases={3: 1, **({} if plan is None else
                                       {4 + len(plan.ins) + a: 3 + a for a in range(len(plan.inouts))})},
        compiler_params=_cp(2))(order, x, w_norm, slot, *p_in)


def _gather_ici_plan(slots, wholes, part=None):
    ns, nw = len(slots), len(wholes)

    def copies(ins, ios, outs, send_sems, recv_sems, local_sems):
        x, y, c, chips = _place()
        me = _chip_id(x, y)
        sends, recvs = [], []
        for a in range(ns + nw):
            dst = ios[a] if a < ns else outs[a - ns]
            R = dst.shape[1]
            r0, nr = (0, R // 2) if part is None else part
            rows = pl.ds(c * (R // 2) + r0, nr) if a < ns else pl.ds(0, R)
            src = dst.at[me, rows] if a < ns else ins[a - ns]
            for j, chip in enumerate(chips):
                sends.append(_remote(src, dst.at[me, rows], send_sems, recv_sems, 3 * a + j, (*chip, c)))
                landed = dst.at[_chip_id(*chip), rows]
                recvs.append(_remote(landed, landed, send_sems, recv_sems, 3 * a + j, (*chip, c)))
        local = [pltpu.make_async_copy(ins[b], outs[b].at[me], local_sems.at[b]) for b in range(nw)]
        return sends, recvs, local

    return _Plan(copies, 3 * (ns + nw), ins=wholes, inouts=slots,
                 outs=[_sds((N_CHIPS, *s.shape), s.dtype) for s in wholes])


def _gather_pass_plan(slots):
    def copies(ins, ios, outs, send_sems, recv_sems, local_sems):
        x, y, c, chips = _place()
        sib = (x, y, 1 - c)
        sends, recvs = [], []
        for a, buf in enumerate(ios):
            half = buf.shape[1] // 2
            for j, chip in enumerate(chips):
                mine = buf.at[_chip_id(*chip), pl.ds(c * half, half)]
                other = buf.at[_chip_id(*chip), pl.ds((1 - c) * half, half)]
                sends.append(_remote(mine, mine, send_sems, recv_sems, 3 * a + j, sib))
                recvs.append(_remote(other, other, send_sems, recv_sems, 3 * a + j, sib))
        return sends, recvs, []

    return _Plan(copies, 3 * len(slots), inouts=slots)


def _pair_plan(grads):
    def copies(ins, ios, outs, send_sems, recv_sems, local_sems):
        x, y, c, _ = _place()
        sib = (x, y, 1 - c)
        sends, recvs = [], []
        for a, g in enumerate(ins):
            half = g.shape[1] // 2
            sends.append(_remote(g.at[:, pl.ds((1 - c) * half, half), :], outs[a], send_sems, recv_sems, a, sib))
            recvs.append(_remote(outs[a], outs[a], send_sems, recv_sems, a, sib))
        return sends, recvs, []

    return _Plan(copies, len(grads), ins=grads,
                 outs=[_sds((g.shape[0], g.shape[1] // 2, g.shape[2]), g.dtype) for g in grads])


def _chip_plan(parts):
    def copies(ins, ios, outs, send_sems, recv_sems, local_sems):
        x, y, c, chips = _place()
        me = _chip_id(x, y)
        sends, recvs = [], []
        for a, part in enumerate(ins):
            for j, chip in enumerate(chips):
                sends.append(_remote(part.at[_chip_id(*chip)], outs[a].at[me], send_sems, recv_sems, 3 * a + j, (*chip, c)))
                landed = outs[a].at[_chip_id(*chip)]
                recvs.append(_remote(landed, landed, send_sems, recv_sems, 3 * a + j, (*chip, c)))
        return sends, recvs, []

    return _Plan(copies, 3 * len(parts), ins=parts, outs=[_sds(p.shape, p.dtype) for p in parts])


def _all_sum(pack, fulls, name):
    R, C = pack.shape
    n = len(fulls)

    def body(p_ref, *refs):
        o_ref, halves = refs[n], refs[n + 1:2 * n + 1]
        buf, send_sems, recv_sems, pair_send, pair_recv = refs[2 * n + 1:]
        x, y, c, _ = _place()
        sib = (x, y, 1 - c)
        pair = []
        for a, full in enumerate(halves):
            H = full.shape[0] // 2
            mine = full.at[pl.ds(c * H, H)]
            cp = _remote(mine, mine, pair_send, pair_recv, a, sib)
            cp.start()
            pair.append(cp)
        me = 4 * x + 2 * y + c
        buf[me] = p_ref[...]
        cps = []
        for k in range(1, N_DEV):
            to = (x ^ (k >> 2), y ^ ((k >> 1) & 1), c ^ (k & 1))
            cp = _remote(p_ref, buf.at[me], send_sems, recv_sems, k - 1, to)
            cp.start()
            cps.append(cp)
        for k in range(1, N_DEV):
            frm = (x ^ (k >> 2), y ^ ((k >> 1) & 1), c ^ (k & 1))
            slot = buf.at[4 * frm[0] + 2 * frm[1] + frm[2]]
            _remote(slot, slot, send_sems, recv_sems, k - 1, frm).wait_recv()
        acc = buf[0]
        for k in range(1, N_DEV):
            acc = acc + buf[k]
        o_ref[...] = acc
        for cp in cps:
            cp.wait_send()
        for a, (full, cp) in enumerate(zip(halves, pair)):
            H = full.shape[0] // 2
            other = full.at[pl.ds((1 - c) * H, H)]
            _remote(other, other, pair_send, pair_recv, a, sib).wait_recv()
            cp.wait_send()

    vm = pl.BlockSpec(memory_space=pltpu.VMEM)
    res = pl.pallas_call(
        body, name=name, in_specs=[vm] + [ANY] * n, out_specs=[vm] + [ANY] * n,
        out_shape=[_sds((R, C), F32)] + [_sds(f.shape, f.dtype) for f in fulls],
        input_output_aliases={1 + a: 1 + a for a in range(n)},
        scratch_shapes=[pltpu.VMEM((N_DEV, R, C), F32), pltpu.SemaphoreType.DMA((N_DEV - 1,)),
                        pltpu.SemaphoreType.DMA((N_DEV - 1,)), pltpu.SemaphoreType.DMA((n,)),
                        pltpu.SemaphoreType.DMA((n,))])(pack, *fulls)
    return res[0], list(res[1:])


def _local_step(xs, tgt, p, ex):
    proj, h1 = ex.project(xs, p["pre_mix_norm"])
    biases = _relbias_fwd(p["rel_bias"], "rel_bias_fwd")
    fw = []
    for g in range(N_GROUPS):
        res, got = _attn_fwd(proj, biases[g], g, f"attn_fwd{g}", plans=ex.carry(f"attn_fwd{g}"))
        ex.done(f"attn_fwd{g}", got)
        fw.append(res)
    (yh, o_h, states), got = _hgrn_fwd(proj, p["hgrn_lb_raw"], p["hgrn_norm"], "hgrn_fwd", plans=ex.carry("hgrn_fwd"))
    ex.done("hgrn_fwd", got)
    W_a, W_h, W_out = ex.weight("w_branch_attn"), ex.weight("w_branch_hgrn"), ex.weight("w_out")
    (y, lse, za, zh, merged), got = _branch_fwd([t[0] for t in fw], [t[1] for t in fw], yh, proj, W_a, W_h,
                                                "branch_fwd", plans=ex.carry("branch_fwd"))
    ex.done("branch_fwd", got)
    W_up, conv_w = ex.weight("w_up"), ex.weight("conv_w")
    mo, x1, h2 = _mix_out(merged, W_out, xs, p["post_mix_norm"], p["pre_ffn_norm"], "mix_out")
    u, got = _mm_nn_blk(h2, W_up, "ffn_up", tm=1024, plans=ex.carry("ffn_up"))
    ex.done("ffn_up", got)
    a, got = _conv_gelu_fwd(u, conv_w, p["conv_b"], "conv_gelu_fwd", plans=ex.carry("conv_gelu_fwd"))
    ex.done("conv_gelu_fwd", got)
    W_down = ex.weight("w_down")
    dx2, dff, g_post_ffn, loss = _loss_head(a, W_down, x1, tgt, p["post_ffn_norm"], "ffn_down_loss")

    ex.grad("w_down", _mm_tn(a, dff, "g_w_down").reshape(N_CHIPS, D_FF // N_CHIPS, D_MODEL))
    (dcg, dcv, gwg, gwv, gbg, gbv), got = _conv_gelu_bwd(u, dff, W_down, conv_w, p["conv_b"], "conv_gelu_bwd",
                                                          plans=ex.carry("conv_gelu_bwd"))
    ex.done("conv_gelu_bwd", got)
    g_conv_w = jnp.concatenate([gwg, gwv], axis=1)
    g_conv_b = jnp.concatenate([gbg, gbv], axis=1)
    du, got = _conv_input_bwd(dcg, dcv, conv_w, "conv_input_bwd", plans=ex.carry("conv_input_bwd"))
    ex.done("conv_input_bwd", got)
    dx1, g_pre_ffn = _mm_nt_prenorm_bwd(du, W_up, x1, p["pre_ffn_norm"], dx2, "d_ffn_in")
    ex.grad("w_up", _mm_tn_blk(h2, du, N_CHIPS, "g_w_up"))
    (dmo, dmerged, g_post_mix), got = _postnorm_bwd(dx1, mo, p["post_mix_norm"], W_out, "post_mix_norm_bwd",
                                                    plans=ex.carry("post_mix_norm_bwd"))
    ex.done("post_mix_norm_bwd", got)
    ex.grad("w_out", _mm_tn(merged, dmo, "g_w_out").reshape(N_CHIPS, D_MODEL // N_CHIPS, D_MODEL))
    (dza, dzh, dg0, dg1, dy, dyh), got = _branch_bwd(dmerged, za, zh, proj, W_a, W_h, "branch_bwd",
                                                     plans=ex.carry("branch_bwd"))
    ex.done("branch_bwd", got)
    ex.grad("w_branch_attn", _mm_tn_blk(y, dza, N_CHIPS, "g_w_branch_attn", together=True))
    ex.grad("w_branch_hgrn", _mm_tn_blk(yh, dzh, N_CHIPS, "g_w_branch_hgrn", together=True))
    dqkv, dbs = [], []
    for g in range(N_GROUPS):
        parts, db, got = _attn_bwd(proj, biases[g], lse, y, dy, g, f"attn_bwd{g}", plans=ex.carry(f"attn_bwd{g}"))
        ex.done(f"attn_bwd{g}", got)
        dqkv += parts
        dbs.append(db)
    g_rel_bias = _relbias_bwd(dbs, "rel_bias_bwd")
    dproj, g_lb_raw, g_hgrn_norm = _hgrn_bwd(proj, p["hgrn_lb_raw"], p["hgrn_norm"], o_h, states, dyh, dqkv,
                                             [dg0, dg1], "hgrn_bwd")
    for piece in W_IN_PIECES:
        g, got = _mm_tn_blk(h1, dproj, N_CHIPS, f"g_{piece}", x_cols=W_IN_ROWS[piece],
                            plans=ex.carry(f"g_{piece}"))
        ex.done(f"g_{piece}", got)
        ex.grad(piece, g)
    dh1, got = _mm_nt_blk(dproj, ex.weight("w_in"), "d_proj_in", plans=ex.carry("d_proj_in"))
    ex.done("d_proj_in", got)
    (grad_x, g_pre_mix), got = _prenorm_bwd(dh1, xs, p["pre_mix_norm"], dx1, "pre_mix_norm_bwd",
                                            plans=ex.carry("pre_mix_norm_bwd"))
    ex.done("pre_mix_norm_bwd", got)
    small = dict(pre_mix_norm=g_pre_mix, rel_bias=g_rel_bias, hgrn_lb_raw=g_lb_raw, hgrn_norm=g_hgrn_norm,
                 post_mix_norm=g_post_mix, pre_ffn_norm=g_pre_ffn, conv_w=g_conv_w, conv_b=g_conv_b,
                 post_ffn_norm=g_post_ffn)
    return loss, grad_x, small


SMALL = ("pre_mix_norm", "rel_bias", "hgrn_lb_raw", "hgrn_norm", "post_mix_norm", "pre_ffn_norm", "conv_w", "conv_b",
         "post_ffn_norm")
BIG = ("w_in", "w_up", "w_down", "w_out", "w_branch_attn", "w_branch_hgrn")
WEIGHTS = ("pre_mix_norm", "w_in", "rel_bias", "hgrn_lb_raw", "hgrn_norm", "w_branch_attn", "w_branch_hgrn", "w_out",
           "post_mix_norm", "pre_ffn_norm", "w_up", "conv_w", "conv_b", "w_down", "post_ffn_norm")
MIXER = ("w_out", "w_branch_attn", "w_branch_hgrn")

ICI_PARTS = {"gather_ici_1of3": (0, 176), "gather_ici_2of3": (176, 176), "gather_ici_3of3": (352, 160)}
SCHEDULE = {
    "proj_in": [("gather_ici_cw", MIXER)],
    "attn_fwd1": [("gather_ici_1of3", ("w_up",))],
    "attn_fwd2": [("gather_pass", MIXER), ("gather_ici_2of3", ("w_up",))],
    "hgrn_fwd": [("gather_ici_3of3", ("w_up",))],
    "branch_fwd": [("gather_pass", ("w_up",))],
    "ffn_up": [("gather_ici", ("w_down",))],
    "conv_gelu_fwd": [("gather_pass", ("w_down",))],
    "conv_gelu_bwd": [("pair", ("w_down",))],
    "conv_input_bwd": [("chip", ("w_down",))],
    "post_mix_norm_bwd": [("pair", ("w_up",))],
    "attn_bwd0": [("chip", ("w_up",)), ("pair", MIXER)],
    "attn_bwd1": [("chip", MIXER)],
    "g_w_in_b": [("pair", ("w_in_a",))],
    "d_proj_in": [("chip", ("w_in_a",)), ("pair", ("w_in_b",))],
    "pre_mix_norm_bwd": [("chip", ("w_in_b",))],
}
W_IN_ROWS = dict(w_in_a=(0, 768), w_in_b=(3, 256))
W_IN_PIECES = tuple(W_IN_ROWS)
REDUCED = W_IN_PIECES + BIG[1:]


class _Exchange:
    def __init__(self, place, slots, conv_w_shard):
        self.place, self.slots, self.conv_w_shard = place, dict(slots), conv_w_shard
        self.conv_w = None
        self.g, self.from_sibling, self.pair_sums, self.arrived = {}, {}, {}, {}
        self.pending = []

    def weight(self, name):
        if name == "conv_w":
            return self.conv_w
        w = self.slots[name]
        return w.reshape(-1, D_MODEL) if name in ("w_out", "w_down") else w

    def project(self, x, w_norm):
        (plan,) = self.carry("proj_in")
        proj, self.slots["w_in"], h, *got = _proj_gathered(x, w_norm, self.slots["w_in"], self.place, "proj_in",
                                                           plan=plan)
        self.done("proj_in", [got])
        return proj, h

    def grad(self, name, g):
        self.g[name] = g

    def carry(self, point):
        plans = []
        self.pending = SCHEDULE.get(point, [])
        for kind, names in self.pending:
            if kind in ("gather_ici", "gather_ici_cw") or kind in ICI_PARTS:
                wholes = [self.conv_w_shard] if kind == "gather_ici_cw" else []
                plans.append(_gather_ici_plan([self.slots[n] for n in names], wholes, ICI_PARTS.get(kind)))
            elif kind == "gather_pass":
                plans.append(_gather_pass_plan([self.slots[n] for n in names]))
            elif kind == "pair":
                plans.append(_pair_plan([self.g[n] for n in names]))
            else:
                for n in names:
                    self.pair_sums[n] = _pair_sum(self.g[n], self.from_sibling[n], self.place[1:2], f"pair_sum_{n}")
                plans.append(_chip_plan([self.pair_sums[n] for n in names]))
        return plans

    def done(self, point, carried):
        for (kind, names), got in zip(self.pending, carried):
            if kind in ("gather_ici", "gather_ici_cw", "gather_pass") or kind in ICI_PARTS:
                self.slots.update(zip(names, got))
                if kind == "gather_ici_cw":
                    self.conv_w = got[len(names)].transpose(1, 0, 2).reshape(3, 2 * D_FF)
            elif kind == "pair":
                self.from_sibling.update(zip(names, got))
            else:
                self.arrived.update(zip(names, got))

    def reduced_halves(self):
        return [_chip_sum(self.arrived[n], self.pair_sums[n], self.place, f"chip_sum_{n}") for n in REDUCED]


def kernel(x, pre_mix_norm, w_in, rel_bias, hgrn_lb_raw, hgrn_norm, w_branch_attn, w_branch_hgrn, w_out, post_mix_norm, pre_ffn_norm, w_up, conv_w, conv_b, w_down, post_ffn_norm, loss_target, m_pre_mix_norm, m_w_in, m_rel_bias, m_hgrn_lb_raw, m_hgrn_norm, m_w_branch_attn, m_w_branch_hgrn, m_w_out, m_post_mix_norm, m_pre_ffn_norm, m_w_up, m_conv_w, m_conv_b, m_w_down, m_post_ffn_norm, v_pre_mix_norm, v_w_in, v_rel_bias, v_hgrn_lb_raw, v_hgrn_norm, v_w_branch_attn, v_w_branch_hgrn, v_w_out, v_post_mix_norm, v_pre_ffn_norm, v_w_up, v_conv_w, v_conv_b, v_w_down, v_post_ffn_norm):
    w = dict(pre_mix_norm=pre_mix_norm, w_in=w_in, rel_bias=rel_bias, hgrn_lb_raw=hgrn_lb_raw, hgrn_norm=hgrn_norm,
             w_branch_attn=w_branch_attn, w_branch_hgrn=w_branch_hgrn, w_out=w_out, post_mix_norm=post_mix_norm,
             pre_ffn_norm=pre_ffn_norm, w_up=w_up, conv_w=conv_w, conv_b=conv_b, w_down=w_down,
             post_ffn_norm=post_ffn_norm)
    m = dict(pre_mix_norm=m_pre_mix_norm, w_in=m_w_in, rel_bias=m_rel_bias, hgrn_lb_raw=m_hgrn_lb_raw,
             hgrn_norm=m_hgrn_norm, w_branch_attn=m_w_branch_attn, w_branch_hgrn=m_w_branch_hgrn, w_out=m_w_out,
             post_mix_norm=m_post_mix_norm, pre_ffn_norm=m_pre_ffn_norm, w_up=m_w_up, conv_w=m_conv_w,
             conv_b=m_conv_b, w_down=m_w_down, post_ffn_norm=m_post_ffn_norm)
    v = dict(pre_mix_norm=v_pre_mix_norm, w_in=v_w_in, rel_bias=v_rel_bias, hgrn_lb_raw=v_hgrn_lb_raw,
             hgrn_norm=v_hgrn_norm, w_branch_attn=v_w_branch_attn, w_branch_hgrn=v_w_branch_hgrn, w_out=v_w_out,
             post_mix_norm=v_post_mix_norm, pre_ffn_norm=v_pre_ffn_norm, w_up=v_w_up, conv_w=v_conv_w,
             conv_b=v_conv_b, w_down=v_w_down, post_ffn_norm=v_post_ffn_norm)
    shard2d = {n: (w[n][0] if w[n].ndim == 3 else w[n]) for n in WEIGHTS}
    chip = 2 * lax.axis_index("x") + lax.axis_index("y")
    core = lax.axis_index("c")

    place = jnp.stack([chip, core]).astype(jnp.int32)
    slots = {n: _cast_into_slot(shard2d[n], place, f"cast_{n}") for n in BIG}
    ex = _Exchange(place, slots, shard2d["conv_w"])
    loss, grad_x, small = _local_step(x[0], loss_target[0], {n: w[n] for n in SMALL if n != "conv_w"}, ex)

    flat = [small[n].reshape(-1) for n in SMALL] + [loss.reshape(-1)]
    sizes = [t.shape[0] for t in flat]
    summed, wholes = _all_sum(jnp.concatenate(flat).reshape(-1, LANES), ex.reduced_halves(), "sum_small")
    summed = summed.reshape(-1)
    offs = [sum(sizes[:i]) for i in range(len(sizes))]
    grads = {}
    for n, o, sz in zip(SMALL, offs, sizes):
        grads[n] = summed[o:o + sz].reshape(small[n].shape)
    loss_total = summed[offs[-1]]
    cw = 2 * D_FF // N_CHIPS
    grads["conv_w"] = lax.dynamic_slice(grads["conv_w"], (0, chip * cw), (3, cw))

    big = dict(zip(REDUCED, wholes))
    big["w_in"] = jnp.concatenate([big.pop(n) for n in W_IN_PIECES], axis=0)
    grads.update(big)

    m2d = {n: m[n].reshape(shard2d[n].shape) for n in WEIGHTS}
    v2d = {n: v[n].reshape(shard2d[n].shape) for n in WEIGHTS}
    updated = dict(zip(SMALL, _adamw_small([shard2d[n] for n in SMALL], [grads[n] for n in SMALL],
                                           [m2d[n] for n in SMALL], [v2d[n] for n in SMALL], "adamw_small")))
    for n in BIG:
        updated[n] = _adamw(shard2d[n], grads[n], m2d[n], v2d[n], f"adamw_{n}")
    out_g, out_d, out_m, out_v = [], [], [], []
    for n in WEIGHTS:
        d2, m2, v2 = updated[n]
        shape = w[n].shape
        out_g.append(grads[n].reshape(shape))
        out_d.append(d2.reshape(shape))
        out_m.append(m2.reshape(shape))
        out_v.append(v2.reshape(shape))
    return (loss_total, grad_x[None], *out_g, *out_d, *out_m, *out_v)
```
